```python
import math
import jax, jax.numpy as jnp
from jax import lax
import numpy as np

D_MODEL = 1024
BATCH = 8
SEQ = 4096
DEPTH = 1

CHUNK = 64
D_FF = 2816
POOL_WIDTH = D_MODEL // 2
POOL_WINDOWS = (2, 4, 8, 16)
N_POOL_GROUPS = len(POOL_WINDOWS)
POOL_GROUP = POOL_WIDTH // N_POOL_GROUPS
SSM_WIDTH = D_MODEL // 2
SSM_GROUP = 16
N_SSM_GROUPS = SSM_WIDTH // SSM_GROUP
SSM_STATE = 64
N_SUBLAYERS = 3
N_MOD = 3
IN_WIDTH = POOL_WIDTH + SSM_WIDTH + 2 * D_MODEL
EPS = 1e-6
DT_MIN = 1e-3
DT_MAX = 1e-1

kernel_name = "hybrid_pool_s5_macaron_adaln"


def rms_norm(x, g):
    xf = x.astype(jnp.float32)
    y = xf * lax.rsqrt(jnp.mean(xf * xf, axis=-1, keepdims=True) + EPS)
    return (y * g.astype(jnp.float32)).astype(x.dtype)


def modulate(h, shift, scale):
    return h * (1 + scale[:, None, :]) + shift[:, None, :]


def swiglu(h, w_in, w_out):
    a, b = jnp.split(h @ w_in, 2, axis=-1)
    return (jax.nn.silu(a) * b) @ w_out


def multiscale_pool(u, pool_w, pool_b, pool_scale):
    B, T, _ = u.shape
    ug = u.reshape(B, T, N_POOL_GROUPS, POOL_GROUP)
    cs = jnp.cumsum(ug.astype(jnp.float32), axis=1)
    pos = jnp.arange(T)
    means = []
    for k, w in enumerate(POOL_WINDOWS):
        csk = cs[:, :, k]
        prev = jnp.pad(csk, ((0, 0), (w, 0), (0, 0)))[:, :T]
        cnt = jnp.minimum(pos + 1, w).astype(jnp.float32)[None, :, None]
        means.append((csk - prev) / cnt)
    mean = jnp.stack(means, axis=2).astype(u.dtype)
    z = mean - ug
    z = jnp.einsum('btgc,gcd->btgd', z, pool_w) + pool_b.reshape(N_POOL_GROUPS, POOL_GROUP)
    return z.reshape(B, T, POOL_WIDTH) * pool_scale


def _ssm_combine(left, right):
    ar_l, ai_l, br_l, bi_l = left
    ar_r, ai_r, br_r, bi_r = right
    ar = ar_r * ar_l - ai_r * ai_l
    ai = ar_r * ai_l + ai_r * ar_l
    br = ar_r * br_l - ai_r * bi_l + br_r
    bi = ar_r * bi_l + ai_r * br_l + bi_r
    return (ar, ai, br, bi)


def s5_mixer(u, lam_re_log, lam_im, log_dt, b_re, b_im, c_re, c_im, d_skip, w_glu, b_glu):
    B, T, _ = u.shape
    f32 = jnp.float32
    uf = u.astype(f32).reshape(B, T, N_SSM_GROUPS, SSM_GROUP)
    lr = -jnp.exp(lam_re_log.astype(f32))
    li = lam_im.astype(f32)
    dt = jnp.exp(log_dt.astype(f32))[:, None]
    mag = jnp.exp(lr * dt)
    ang = li * dt
    ab_re = mag * jnp.cos(ang)
    ab_im = mag * jnp.sin(ang)
    num_re = ab_re - 1.0
    num_im = ab_im
    den = lr * lr + li * li
    f_re = (num_re * lr + num_im * li) / den
    f_im = (num_im * lr - num_re * li) / den
    br = b_re.astype(f32)
    bi = b_im.astype(f32)
    bb_re = f_re[..., None] * br - f_im[..., None] * bi
    bb_im = f_re[..., None] * bi + f_im[..., None] * br
    bu_re = jnp.einsum('btgh,gnh->btgn', uf, bb_re)
    bu_im = jnp.einsum('btgh,gnh->btgn', uf, bb_im)
    a_re = jnp.broadcast_to(ab_re[None, None], (1, T, N_SSM_GROUPS, SSM_STATE))
    a_im = jnp.broadcast_to(ab_im[None, None], (1, T, N_SSM_GROUPS, SSM_STATE))
    _, _, s_re, s_im = lax.associative_scan(_ssm_combine, (a_re, a_im, bu_re, bu_im), axis=1)
    y = (jnp.einsum('btgn,ghn->btgh', s_re, c_re.astype(f32))
         - jnp.einsum('btgn,ghn->btgh', s_im, c_im.astype(f32)))
    y = y.reshape(B, T, SSM_WIDTH) + d_skip.astype(f32) * uf.reshape(B, T, SSM_WIDTH)
    y = jax.nn.gelu(y.astype(u.dtype), approximate=False)
    val, gate = jnp.split(y @ w_glu + b_glu, 2, axis=-1)
    return val * jax.nn.sigmoid(gate)


def _fwd_setup_inputs(seed: int = 0) -> dict:
    key = jax.random.key(seed)
    ks = jax.random.split(key, 32)
    L, D, F = DEPTH, D_MODEL, D_FF
    G, H, N = N_SSM_GROUPS, SSM_GROUP, SSM_STATE
    nrm = lambda k, shape, s: jax.random.normal(k, shape, jnp.float32) * s
    n_idx = jnp.arange(N, dtype=jnp.float32)[None, :]
    return {
        "x": nrm(ks[0], (BATCH, SEQ, D), 1.0),
        "c": nrm(ks[1], (BATCH, D), 1.0),
        "w_ada": nrm(ks[2], (L, D, N_SUBLAYERS * N_MOD * D), 0.5 * D ** -0.5),
        "b_ada": nrm(ks[3], (L, N_SUBLAYERS * N_MOD * D), 0.01),
        "g_ffn1": 1.0 + nrm(ks[4], (L, D), 0.01),
        "w_ffn1_in": nrm(ks[5], (L, D, 2 * F), D ** -0.5),
        "w_ffn1_out": nrm(ks[6], (L, F, D), F ** -0.5),
        "g_mix": 1.0 + nrm(ks[7], (L, D), 0.01),
        "w_in": nrm(ks[8], (L, D, IN_WIDTH), D ** -0.5),
        "pool_w": nrm(ks[9], (L, N_POOL_GROUPS, POOL_GROUP, POOL_GROUP), POOL_GROUP ** -0.5),
        "pool_b": nrm(ks[10], (L, POOL_WIDTH), 0.01),
        "pool_scale": 1.0 + nrm(ks[11], (L, POOL_WIDTH), 0.05),
        "w_pool_up": nrm(ks[12], (L, POOL_WIDTH, D), POOL_WIDTH ** -0.5),
        "ssm_lam_re_log": jnp.log(0.5) + nrm(ks[13], (L, G, N), 0.01),
        "ssm_lam_im": math.pi * n_idx + nrm(ks[14], (L, G, N), 0.01),
        "ssm_log_dt": jax.random.uniform(ks[15], (L, G), jnp.float32, math.log(DT_MIN), math.log(DT_MAX)),
        "ssm_b_re": nrm(ks[16], (L, G, N, H), (2 * H) ** -0.5),
        "ssm_b_im": nrm(ks[17], (L, G, N, H), (2 * H) ** -0.5),
        "ssm_c_re": nrm(ks[18], (L, G, H, N), N ** -0.5),
        "ssm_c_im": nrm(ks[19], (L, G, H, N), N ** -0.5),
        "ssm_d": nrm(ks[20], (L, SSM_WIDTH), 1.0),
        "w_glu": nrm(ks[21], (L, SSM_WIDTH, 2 * SSM_WIDTH), SSM_WIDTH ** -0.5),
        "b_glu": nrm(ks[22], (L, 2 * SSM_WIDTH), 0.01),
        "w_ssm_up": nrm(ks[23], (L, SSM_WIDTH, D), SSM_WIDTH ** -0.5),
        "w_out": nrm(ks[24], (L, D, D), D ** -0.5),
        "g_ffn2": 1.0 + nrm(ks[25], (L, D), 0.01),
        "w_ffn2_in": nrm(ks[26], (L, D, 2 * F), D ** -0.5),
        "w_ffn2_out": nrm(ks[27], (L, F, D), F ** -0.5),
        "g_final": 1.0 + nrm(ks[28], (D,), 0.01),
    }


def _fwd_reference(x, c, w_ada, b_ada, g_ffn1, w_ffn1_in, w_ffn1_out, g_mix, w_in,
              pool_w, pool_b, pool_scale, w_pool_up,
              ssm_lam_re_log, ssm_lam_im, ssm_log_dt, ssm_b_re, ssm_b_im, ssm_c_re, ssm_c_im, ssm_d,
              w_glu, b_glu, w_ssm_up, w_out, g_ffn2, w_ffn2_in, w_ffn2_out, g_final):
    B = x.shape[0]
    split_pts = (POOL_WIDTH, POOL_WIDTH + SSM_WIDTH, POOL_WIDTH + SSM_WIDTH + D_MODEL)
    for l in range(DEPTH):
        mod = (jax.nn.silu(c) @ w_ada[l] + b_ada[l]).reshape(B, N_SUBLAYERS, N_MOD, D_MODEL)

        h = modulate(rms_norm(x, g_ffn1[l]), mod[:, 0, 0], mod[:, 0, 1])
        x = x + 0.5 * mod[:, 0, 2][:, None, :] * swiglu(h, w_ffn1_in[l], w_ffn1_out[l])

        h = modulate(rms_norm(x, g_mix[l]), mod[:, 1, 0], mod[:, 1, 1])
        u_pool, u_ssm, gl_pool, gl_ssm = jnp.split(h @ w_in[l], split_pts, axis=-1)
        y_pool = multiscale_pool(u_pool, pool_w[l], pool_b[l], pool_scale[l]) @ w_pool_up[l]
        y_ssm = s5_mixer(u_ssm, ssm_lam_re_log[l], ssm_lam_im[l], ssm_log_dt[l],
                         ssm_b_re[l], ssm_b_im[l], ssm_c_re[l], ssm_c_im[l], ssm_d[l],
                         w_glu[l], b_glu[l]) @ w_ssm_up[l]
        merged = jax.nn.sigmoid(gl_pool) * y_pool + jax.nn.sigmoid(gl_ssm) * y_ssm
        x = x + mod[:, 1, 2][:, None, :] * (merged @ w_out[l])

        h = modulate(rms_norm(x, g_ffn2[l]), mod[:, 2, 0], mod[:, 2, 1])
        x = x + 0.5 * mod[:, 2, 2][:, None, :] * swiglu(h, w_ffn2_in[l], w_ffn2_out[l])
    return rms_norm(x, g_final)


import jax as _jax
import jax.numpy as _jnp

TWIN_FORMAT = 'train_step'
FWD_PARAMS = ['x', 'c', 'w_ada', 'b_ada', 'g_ffn1', 'w_ffn1_in', 'w_ffn1_out', 'g_mix', 'w_in', 'pool_w', 'pool_b', 'pool_scale', 'w_pool_up', 'ssm_lam_re_log', 'ssm_lam_im', 'ssm_log_dt', 'ssm_b_re', 'ssm_b_im', 'ssm_c_re', 'ssm_c_im', 'ssm_d', 'w_glu', 'b_glu', 'w_ssm_up', 'w_out', 'g_ffn2', 'w_ffn2_in', 'w_ffn2_out', 'g_final']
TWIN_WEIGHTS = ['w_ada', 'b_ada', 'g_ffn1', 'w_ffn1_in', 'w_ffn1_out', 'g_mix', 'w_in', 'pool_w', 'pool_b', 'pool_scale', 'w_pool_up', 'ssm_lam_re_log', 'ssm_lam_im', 'ssm_log_dt', 'ssm_b_re', 'ssm_b_im', 'ssm_c_re', 'ssm_c_im', 'ssm_d', 'w_glu', 'b_glu', 'w_ssm_up', 'w_out', 'g_ffn2', 'w_ffn2_in', 'w_ffn2_out', 'g_final']
TWIN_DIFF_INPUT = 'x'
TWIN_INPUTS = ['x', 'c', 'w_ada', 'b_ada', 'g_ffn1', 'w_ffn1_in', 'w_ffn1_out', 'g_mix', 'w_in', 'pool_w', 'pool_b', 'pool_scale', 'w_pool_up', 'ssm_lam_re_log', 'ssm_lam_im', 'ssm_log_dt', 'ssm_b_re', 'ssm_b_im', 'ssm_c_re', 'ssm_c_im', 'ssm_d', 'w_glu', 'b_glu', 'w_ssm_up', 'w_out', 'g_ffn2', 'w_ffn2_in', 'w_ffn2_out', 'g_final', 'loss_target', 'm_w_ada', 'm_b_ada', 'm_g_ffn1', 'm_w_ffn1_in', 'm_w_ffn1_out', 'm_g_mix', 'm_w_in', 'm_pool_w', 'm_pool_b', 'm_pool_scale', 'm_w_pool_up', 'm_ssm_lam_re_log', 'm_ssm_lam_im', 'm_ssm_log_dt', 'm_ssm_b_re', 'm_ssm_b_im', 'm_ssm_c_re', 'm_ssm_c_im', 'm_ssm_d', 'm_w_glu', 'm_b_glu', 'm_w_ssm_up', 'm_w_out', 'm_g_ffn2', 'm_w_ffn2_in', 'm_w_ffn2_out', 'm_g_final', 'v_w_ada', 'v_b_ada', 'v_g_ffn1', 'v_w_ffn1_in', 'v_w_ffn1_out', 'v_g_mix', 'v_w_in', 'v_pool_w', 'v_pool_b', 'v_pool_scale', 'v_w_pool_up', 'v_ssm_lam_re_log', 'v_ssm_lam_im', 'v_ssm_log_dt', 'v_ssm_b_re', 'v_ssm_b_im', 'v_ssm_c_re', 'v_ssm_c_im', 'v_ssm_d', 'v_w_glu', 'v_b_glu', 'v_w_ssm_up', 'v_w_out', 'v_g_ffn2', 'v_w_ffn2_in', 'v_w_ffn2_out', 'v_g_final']
TWIN_OUTPUTS = ['loss', 'grad_x', 'grad_w_ada', 'grad_b_ada', 'grad_g_ffn1', 'grad_w_ffn1_in', 'grad_w_ffn1_out', 'grad_g_mix', 'grad_w_in', 'grad_pool_w', 'grad_pool_b', 'grad_pool_scale', 'grad_w_pool_up', 'grad_ssm_lam_re_log', 'grad_ssm_lam_im', 'grad_ssm_log_dt', 'grad_ssm_b_re', 'grad_ssm_b_im', 'grad_ssm_c_re', 'grad_ssm_c_im', 'grad_ssm_d', 'grad_w_glu', 'grad_b_glu', 'grad_w_ssm_up', 'grad_w_out', 'grad_g_ffn2', 'grad_w_ffn2_in', 'grad_w_ffn2_out', 'grad_g_final', 'delta_w_ada', 'delta_b_ada', 'delta_g_ffn1', 'delta_w_ffn1_in', 'delta_w_ffn1_out', 'delta_g_mix', 'delta_w_in', 'delta_pool_w', 'delta_pool_b', 'delta_pool_scale', 'delta_w_pool_up', 'delta_ssm_lam_re_log', 'delta_ssm_lam_im', 'delta_ssm_log_dt', 'delta_ssm_b_re', 'delta_ssm_b_im', 'delta_ssm_c_re', 'delta_ssm_c_im', 'delta_ssm_d', 'delta_w_glu', 'delta_b_glu', 'delta_w_ssm_up', 'delta_w_out', 'delta_g_ffn2', 'delta_w_ffn2_in', 'delta_w_ffn2_out', 'delta_g_final', 'new_m_w_ada', 'new_m_b_ada', 'new_m_g_ffn1', 'new_m_w_ffn1_in', 'new_m_w_ffn1_out', 'new_m_g_mix', 'new_m_w_in', 'new_m_pool_w', 'new_m_pool_b', 'new_m_pool_scale', 'new_m_w_pool_up', 'new_m_ssm_lam_re_log', 'new_m_ssm_lam_im', 'new_m_ssm_log_dt', 'new_m_ssm_b_re', 'new_m_ssm_b_im', 'new_m_ssm_c_re', 'new_m_ssm_c_im', 'new_m_ssm_d', 'new_m_w_glu', 'new_m_b_glu', 'new_m_w_ssm_up', 'new_m_w_out', 'new_m_g_ffn2', 'new_m_w_ffn2_in', 'new_m_w_ffn2_out', 'new_m_g_final', 'new_v_w_ada', 'new_v_b_ada', 'new_v_g_ffn1', 'new_v_w_ffn1_in', 'new_v_w_ffn1_out', 'new_v_g_mix', 'new_v_w_in', 'new_v_pool_w', 'new_v_pool_b', 'new_v_pool_scale', 'new_v_w_pool_up', 'new_v_ssm_lam_re_log', 'new_v_ssm_lam_im', 'new_v_ssm_log_dt', 'new_v_ssm_b_re', 'new_v_ssm_b_im', 'new_v_ssm_c_re', 'new_v_ssm_c_im', 'new_v_ssm_d', 'new_v_w_glu', 'new_v_b_glu', 'new_v_w_ssm_up', 'new_v_w_out', 'new_v_g_ffn2', 'new_v_w_ffn2_in', 'new_v_w_ffn2_out', 'new_v_g_final']
TWIN_LEAF_KINDS = {'loss': 'loss', 'grad_x': 'grad_x', 'grad_w_ada': 'grad_w', 'grad_b_ada': 'grad_w', 'grad_g_ffn1': 'grad_w', 'grad_w_ffn1_in': 'grad_w', 'grad_w_ffn1_out': 'grad_w', 'grad_g_mix': 'grad_w', 'grad_w_in': 'grad_w', 'grad_pool_w': 'grad_w', 'grad_pool_b': 'grad_w', 'grad_pool_scale': 'grad_w', 'grad_w_pool_up': 'grad_w', 'grad_ssm_lam_re_log': 'grad_w', 'grad_ssm_lam_im': 'grad_w', 'grad_ssm_log_dt': 'grad_w', 'grad_ssm_b_re': 'grad_w', 'grad_ssm_b_im': 'grad_w', 'grad_ssm_c_re': 'grad_w', 'grad_ssm_c_im': 'grad_w', 'grad_ssm_d': 'grad_w', 'grad_w_glu': 'grad_w', 'grad_b_glu': 'grad_w', 'grad_w_ssm_up': 'grad_w', 'grad_w_out': 'grad_w', 'grad_g_ffn2': 'grad_w', 'grad_w_ffn2_in': 'grad_w', 'grad_w_ffn2_out': 'grad_w', 'grad_g_final': 'grad_w', 'delta_w_ada': 'delta_w', 'delta_b_ada': 'delta_w', 'delta_g_ffn1': 'delta_w', 'delta_w_ffn1_in': 'delta_w', 'delta_w_ffn1_out': 'delta_w', 'delta_g_mix': 'delta_w', 'delta_w_in': 'delta_w', 'delta_pool_w': 'delta_w', 'delta_pool_b': 'delta_w', 'delta_pool_scale': 'delta_w', 'delta_w_pool_up': 'delta_w', 'delta_ssm_lam_re_log': 'delta_w', 'delta_ssm_lam_im': 'delta_w', 'delta_ssm_log_dt': 'delta_w', 'delta_ssm_b_re': 'delta_w', 'delta_ssm_b_im': 'delta_w', 'delta_ssm_c_re': 'delta_w', 'delta_ssm_c_im': 'delta_w', 'delta_ssm_d': 'delta_w', 'delta_w_glu': 'delta_w', 'delta_b_glu': 'delta_w', 'delta_w_ssm_up': 'delta_w', 'delta_w_out': 'delta_w', 'delta_g_ffn2': 'delta_w', 'delta_w_ffn2_in': 'delta_w', 'delta_w_ffn2_out': 'delta_w', 'delta_g_final': 'delta_w', 'new_m_w_ada': 'new_m', 'new_m_b_ada': 'new_m', 'new_m_g_ffn1': 'new_m', 'new_m_w_ffn1_in': 'new_m', 'new_m_w_ffn1_out': 'new_m', 'new_m_g_mix': 'new_m', 'new_m_w_in': 'new_m', 'new_m_pool_w': 'new_m', 'new_m_pool_b': 'new_m', 'new_m_pool_scale': 'new_m', 'new_m_w_pool_up': 'new_m', 'new_m_ssm_lam_re_log': 'new_m', 'new_m_ssm_lam_im': 'new_m', 'new_m_ssm_log_dt': 'new_m', 'new_m_ssm_b_re': 'new_m', 'new_m_ssm_b_im': 'new_m', 'new_m_ssm_c_re': 'new_m', 'new_m_ssm_c_im': 'new_m', 'new_m_ssm_d': 'new_m', 'new_m_w_glu': 'new_m', 'new_m_b_glu': 'new_m', 'new_m_w_ssm_up': 'new_m', 'new_m_w_out': 'new_m', 'new_m_g_ffn2': 'new_m', 'new_m_w_ffn2_in': 'new_m', 'new_m_w_ffn2_out': 'new_m', 'new_m_g_final': 'new_m', 'new_v_w_ada': 'new_v', 'new_v_b_ada': 'new_v', 'new_v_g_ffn1': 'new_v', 'new_v_w_ffn1_in': 'new_v', 'new_v_w_ffn1_out': 'new_v', 'new_v_g_mix': 'new_v', 'new_v_w_in': 'new_v', 'new_v_pool_w': 'new_v', 'new_v_pool_b': 'new_v', 'new_v_pool_scale': 'new_v', 'new_v_w_pool_up': 'new_v', 'new_v_ssm_lam_re_log': 'new_v', 'new_v_ssm_lam_im': 'new_v', 'new_v_ssm_log_dt': 'new_v', 'new_v_ssm_b_re': 'new_v', 'new_v_ssm_b_im': 'new_v', 'new_v_ssm_c_re': 'new_v', 'new_v_ssm_c_im': 'new_v', 'new_v_ssm_d': 'new_v', 'new_v_w_glu': 'new_v', 'new_v_b_glu': 'new_v', 'new_v_w_ssm_up': 'new_v', 'new_v_w_out': 'new_v', 'new_v_g_ffn2': 'new_v', 'new_v_w_ffn2_in': 'new_v', 'new_v_w_ffn2_out': 'new_v', 'new_v_g_final': 'new_v'}


def _forward(args):
    return _fwd_reference(*[args[k] for k in FWD_PARAMS])


def _output_shape():
    out = _jax.eval_shape(lambda: _forward(_fwd_setup_inputs(0)))
    return out.shape, out.dtype

N_MICROBATCH = 1
ADAM_LR = 0.001
ADAM_B1 = 0.9
ADAM_B2 = 0.999
ADAM_EPS = 1e-08
ADAM_WD = 0.01
ADAM_STEP = 10
PER_EXAMPLE_BATCH_AXIS = {'x': 0, 'c': 0, 'loss_target': 0}
SHARED_INPUTS = []
_WEIGHT_DTYPES = {'w_ada': _jnp.float32, 'b_ada': _jnp.float32, 'g_ffn1': _jnp.float32, 'w_ffn1_in': _jnp.float32, 'w_ffn1_out': _jnp.float32, 'g_mix': _jnp.float32, 'w_in': _jnp.float32, 'pool_w': _jnp.float32, 'pool_b': _jnp.float32, 'pool_scale': _jnp.float32, 'w_pool_up': _jnp.float32, 'ssm_lam_re_log': _jnp.float32, 'ssm_lam_im': _jnp.float32, 'ssm_log_dt': _jnp.float32, 'ssm_b_re': _jnp.float32, 'ssm_b_im': _jnp.float32, 'ssm_c_re': _jnp.float32, 'ssm_c_im': _jnp.float32, 'ssm_d': _jnp.float32, 'w_glu': _jnp.float32, 'b_glu': _jnp.float32, 'w_ssm_up': _jnp.float32, 'w_out': _jnp.float32, 'g_ffn2': _jnp.float32, 'w_ffn2_in': _jnp.float32, 'w_ffn2_out': _jnp.float32, 'g_final': _jnp.float32}
MOMENT_SCALE = {'w_ada': 3.005137e-02, 'b_ada': 4.860967e-02, 'g_ffn1': 2.926728e-02, 'w_ffn1_in': 1.185267e-02, 'w_ffn1_out': 1.930935e-02, 'g_mix': 3.215761e-02, 'w_in': 1.835143e-02, 'pool_w': 3.850672e-02, 'pool_b': 4.133066e-02, 'pool_scale': 3.941419e-02, 'w_pool_up': 2.708810e-02, 'ssm_lam_re_log': 1.759662e-03, 'ssm_lam_im': 2.629393e-03, 'ssm_log_dt': 1.333233e+00, 'ssm_b_re': 1.233994e-03, 'ssm_b_im': 1.163979e-03, 'ssm_c_re': 1.596144e-03, 'ssm_c_im': 1.534382e-03, 'ssm_d': 1.736461e-02, 'w_glu': 1.156145e-02, 'b_glu': 1.643160e-02, 'w_ssm_up': 1.119503e-02, 'w_out': 2.927312e-02, 'g_ffn2': 2.705675e-02, 'w_ffn2_in': 1.172928e-02, 'w_ffn2_out': 1.912990e-02, 'g_final': 3.195853e+01}


def _to_microbatches(a, axis):
    t = _jnp.moveaxis(a, axis, 0)
    t = t.reshape((N_MICROBATCH, t.shape[0] // N_MICROBATCH) + t.shape[1:])
    return _jnp.moveaxis(t, 1, axis + 1)


def setup_inputs(seed: int = 0) -> dict:
    inp = _fwd_setup_inputs(seed)
    key = _jax.random.fold_in(_jax.random.key(seed), 7919)
    shape, _ = _output_shape()
    out = dict(inp)
    out["loss_target"] = _jax.random.normal(_jax.random.fold_in(key, 0), shape, _jnp.float32)
    for i, name in enumerate(TWIN_WEIGHTS):
        w = inp[name].astype(_jnp.float32)
        if MOMENT_SCALE is None:
            s = _jnp.sqrt(_jnp.mean(_jnp.square(w)) + 1e-30)
        else:
            s = MOMENT_SCALE[name]
        km, kv = _jax.random.split(_jax.random.fold_in(key, i + 1))
        out[name] = w
        out["m_" + name] = s * _jax.random.normal(km, w.shape, _jnp.float32)
        out["v_" + name] = (s * s) * _jax.random.uniform(kv, w.shape, _jnp.float32, 0.5, 1.5)
    if N_MICROBATCH > 1:
        for name, axis in PER_EXAMPLE_BATCH_AXIS.items():
            out[name] = _to_microbatches(out[name], axis)
    return {'x': out['x'], 'c': out['c'], 'w_ada': out['w_ada'], 'b_ada': out['b_ada'], 'g_ffn1': out['g_ffn1'], 'w_ffn1_in': out['w_ffn1_in'], 'w_ffn1_out': out['w_ffn1_out'], 'g_mix': out['g_mix'], 'w_in': out['w_in'], 'pool_w': out['pool_w'], 'pool_b': out['pool_b'], 'pool_scale': out['pool_scale'], 'w_pool_up': out['w_pool_up'], 'ssm_lam_re_log': out['ssm_lam_re_log'], 'ssm_lam_im': out['ssm_lam_im'], 'ssm_log_dt': out['ssm_log_dt'], 'ssm_b_re': out['ssm_b_re'], 'ssm_b_im': out['ssm_b_im'], 'ssm_c_re': out['ssm_c_re'], 'ssm_c_im': out['ssm_c_im'], 'ssm_d': out['ssm_d'], 'w_glu': out['w_glu'], 'b_glu': out['b_glu'], 'w_ssm_up': out['w_ssm_up'], 'w_out': out['w_out'], 'g_ffn2': out['g_ffn2'], 'w_ffn2_in': out['w_ffn2_in'], 'w_ffn2_out': out['w_ffn2_out'], 'g_final': out['g_final'], 'loss_target': out['loss_target'], 'm_w_ada': out['m_w_ada'], 'm_b_ada': out['m_b_ada'], 'm_g_ffn1': out['m_g_ffn1'], 'm_w_ffn1_in': out['m_w_ffn1_in'], 'm_w_ffn1_out': out['m_w_ffn1_out'], 'm_g_mix': out['m_g_mix'], 'm_w_in': out['m_w_in'], 'm_pool_w': out['m_pool_w'], 'm_pool_b': out['m_pool_b'], 'm_pool_scale': out['m_pool_scale'], 'm_w_pool_up': out['m_w_pool_up'], 'm_ssm_lam_re_log': out['m_ssm_lam_re_log'], 'm_ssm_lam_im': out['m_ssm_lam_im'], 'm_ssm_log_dt': out['m_ssm_log_dt'], 'm_ssm_b_re': out['m_ssm_b_re'], 'm_ssm_b_im': out['m_ssm_b_im'], 'm_ssm_c_re': out['m_ssm_c_re'], 'm_ssm_c_im': out['m_ssm_c_im'], 'm_ssm_d': out['m_ssm_d'], 'm_w_glu': out['m_w_glu'], 'm_b_glu': out['m_b_glu'], 'm_w_ssm_up': out['m_w_ssm_up'], 'm_w_out': out['m_w_out'], 'm_g_ffn2': out['m_g_ffn2'], 'm_w_ffn2_in': out['m_w_ffn2_in'], 'm_w_ffn2_out': out['m_w_ffn2_out'], 'm_g_final': out['m_g_final'], 'v_w_ada': out['v_w_ada'], 'v_b_ada': out['v_b_ada'], 'v_g_ffn1': out['v_g_ffn1'], 'v_w_ffn1_in': out['v_w_ffn1_in'], 'v_w_ffn1_out': out['v_w_ffn1_out'], 'v_g_mix': out['v_g_mix'], 'v_w_in': out['v_w_in'], 'v_pool_w': out['v_pool_w'], 'v_pool_b': out['v_pool_b'], 'v_pool_scale': out['v_pool_scale'], 'v_w_pool_up': out['v_w_pool_up'], 'v_ssm_lam_re_log': out['v_ssm_lam_re_log'], 'v_ssm_lam_im': out['v_ssm_lam_im'], 'v_ssm_log_dt': out['v_ssm_log_dt'], 'v_ssm_b_re': out['v_ssm_b_re'], 'v_ssm_b_im': out['v_ssm_b_im'], 'v_ssm_c_re': out['v_ssm_c_re'], 'v_ssm_c_im': out['v_ssm_c_im'], 'v_ssm_d': out['v_ssm_d'], 'v_w_glu': out['v_w_glu'], 'v_b_glu': out['v_b_glu'], 'v_w_ssm_up': out['v_w_ssm_up'], 'v_w_out': out['v_w_out'], 'v_g_ffn2': out['v_g_ffn2'], 'v_w_ffn2_in': out['v_w_ffn2_in'], 'v_w_ffn2_out': out['v_w_ffn2_out'], 'v_g_final': out['v_g_final']}


def _loss(weights, diff, rest, loss_target):
    with _jax.named_scope("forward"):
        args = {**rest, TWIN_DIFF_INPUT: diff, **{k: w.astype(_WEIGHT_DTYPES[k]) for k, w in weights.items()}}
        y = _forward(args)
    with _jax.named_scope("loss_head"):
        err = _jnp.square(y.astype(_jnp.float32) - loss_target)
        return 0.5 * _jnp.sum(_jnp.mean(err, axis=-1)) if err.ndim else 0.5 * err


def _adamw(w, g, m, v):
    m = ADAM_B1 * m + (1.0 - ADAM_B1) * g
    v = ADAM_B2 * v + (1.0 - ADAM_B2) * _jnp.square(g)
    m_hat = m / (1.0 - ADAM_B1 ** ADAM_STEP)
    v_hat = v / (1.0 - ADAM_B2 ** ADAM_STEP)
    delta = -ADAM_LR * (m_hat / (_jnp.sqrt(v_hat) + ADAM_EPS) + ADAM_WD * w)
    return delta, m, v


def reference(x, c, w_ada, b_ada, g_ffn1, w_ffn1_in, w_ffn1_out, g_mix, w_in, pool_w, pool_b, pool_scale, w_pool_up, ssm_lam_re_log, ssm_lam_im, ssm_log_dt, ssm_b_re, ssm_b_im, ssm_c_re, ssm_c_im, ssm_d, w_glu, b_glu, w_ssm_up, w_out, g_ffn2, w_ffn2_in, w_ffn2_out, g_final, loss_target, m_w_ada, m_b_ada, m_g_ffn1, m_w_ffn1_in, m_w_ffn1_out, m_g_mix, m_w_in, m_pool_w, m_pool_b, m_pool_scale, m_w_pool_up, m_ssm_lam_re_log, m_ssm_lam_im, m_ssm_log_dt, m_ssm_b_re, m_ssm_b_im, m_ssm_c_re, m_ssm_c_im, m_ssm_d, m_w_glu, m_b_glu, m_w_ssm_up, m_w_out, m_g_ffn2, m_w_ffn2_in, m_w_ffn2_out, m_g_final, v_w_ada, v_b_ada, v_g_ffn1, v_w_ffn1_in, v_w_ffn1_out, v_g_mix, v_w_in, v_pool_w, v_pool_b, v_pool_scale, v_w_pool_up, v_ssm_lam_re_log, v_ssm_lam_im, v_ssm_log_dt, v_ssm_b_re, v_ssm_b_im, v_ssm_c_re, v_ssm_c_im, v_ssm_d, v_w_glu, v_b_glu, v_w_ssm_up, v_w_out, v_g_ffn2, v_w_ffn2_in, v_w_ffn2_out, v_g_final):
    given = dict(x=x, c=c, w_ada=w_ada, b_ada=b_ada, g_ffn1=g_ffn1, w_ffn1_in=w_ffn1_in, w_ffn1_out=w_ffn1_out, g_mix=g_mix, w_in=w_in, pool_w=pool_w, pool_b=pool_b, pool_scale=pool_scale, w_pool_up=w_pool_up, ssm_lam_re_log=ssm_lam_re_log, ssm_lam_im=ssm_lam_im, ssm_log_dt=ssm_log_dt, ssm_b_re=ssm_b_re, ssm_b_im=ssm_b_im, ssm_c_re=ssm_c_re, ssm_c_im=ssm_c_im, ssm_d=ssm_d, w_glu=w_glu, b_glu=b_glu, w_ssm_up=w_ssm_up, w_out=w_out, g_ffn2=g_ffn2, w_ffn2_in=w_ffn2_in, w_ffn2_out=w_ffn2_out, g_final=g_final, loss_target=loss_target, m_w_ada=m_w_ada, m_b_ada=m_b_ada, m_g_ffn1=m_g_ffn1, m_w_ffn1_in=m_w_ffn1_in, m_w_ffn1_out=m_w_ffn1_out, m_g_mix=m_g_mix, m_w_in=m_w_in, m_pool_w=m_pool_w, m_pool_b=m_pool_b, m_pool_scale=m_pool_scale, m_w_pool_up=m_w_pool_up, m_ssm_lam_re_log=m_ssm_lam_re_log, m_ssm_lam_im=m_ssm_lam_im, m_ssm_log_dt=m_ssm_log_dt, m_ssm_b_re=m_ssm_b_re, m_ssm_b_im=m_ssm_b_im, m_ssm_c_re=m_ssm_c_re, m_ssm_c_im=m_ssm_c_im, m_ssm_d=m_ssm_d, m_w_glu=m_w_glu, m_b_glu=m_b_glu, m_w_ssm_up=m_w_ssm_up, m_w_out=m_w_out, m_g_ffn2=m_g_ffn2, m_w_ffn2_in=m_w_ffn2_in, m_w_ffn2_out=m_w_ffn2_out, m_g_final=m_g_final, v_w_ada=v_w_ada, v_b_ada=v_b_ada, v_g_ffn1=v_g_ffn1, v_w_ffn1_in=v_w_ffn1_in, v_w_ffn1_out=v_w_ffn1_out, v_g_mix=v_g_mix, v_w_in=v_w_in, v_pool_w=v_pool_w, v_pool_b=v_pool_b, v_pool_scale=v_pool_scale, v_w_pool_up=v_w_pool_up, v_ssm_lam_re_log=v_ssm_lam_re_log, v_ssm_lam_im=v_ssm_lam_im, v_ssm_log_dt=v_ssm_log_dt, v_ssm_b_re=v_ssm_b_re, v_ssm_b_im=v_ssm_b_im, v_ssm_c_re=v_ssm_c_re, v_ssm_c_im=v_ssm_c_im, v_ssm_d=v_ssm_d, v_w_glu=v_w_glu, v_b_glu=v_b_glu, v_w_ssm_up=v_w_ssm_up, v_w_out=v_w_out, v_g_ffn2=v_g_ffn2, v_w_ffn2_in=v_w_ffn2_in, v_w_ffn2_out=v_w_ffn2_out, v_g_final=v_g_final)
    weights = {n: given[n] for n in TWIN_WEIGHTS}
    shared = {n: given[n] for n in SHARED_INPUTS}
    per_example = {n: given[n] for n in ['x', 'c']}
    grad_fn = _jax.value_and_grad(_loss, argnums=(0, 1))

    def one_microbatch(ex, loss_target):
        ex = dict(ex)
        diff = ex.pop(TWIN_DIFF_INPUT)
        return grad_fn(weights, diff, {**shared, **ex}, loss_target)

    if N_MICROBATCH == 1:
        loss, (grad_w, grad_x) = one_microbatch(per_example, given["loss_target"])
    else:
        def body(carry, xs):
            loss_sum, grad_sum = carry
            l_k, (gw_k, gx_k) = one_microbatch(xs[0], xs[1])
            with _jax.named_scope("update"):
                return (loss_sum + l_k, _jax.tree.map(_jnp.add, grad_sum, gw_k)), gx_k

        init = (_jnp.zeros((), _jnp.float32), _jax.tree.map(_jnp.zeros_like, weights))
        (loss, grad_w), grad_x = _jax.lax.scan(body, init, (per_example, given["loss_target"]))
    with _jax.named_scope("update"):
        delta_w, new_m, new_v = {}, {}, {}
        for n in TWIN_WEIGHTS:
            delta_w[n], new_m[n], new_v[n] = _adamw(weights[n], grad_w[n], given["m_" + n], given["v_" + n])
    return (loss, grad_x, *[grad_w[n] for n in TWIN_WEIGHTS], *[delta_w[n] for n in TWIN_WEIGHTS],
            *[new_m[n] for n in TWIN_WEIGHTS], *[new_v[n] for n in TWIN_WEIGHTS])
```

```python
import functools
import math

import jax
import jax.numpy as jnp
from jax import lax
from jax.experimental import pallas as pl
from jax.experimental.pallas import tpu as pltpu

F32 = jnp.float32
BF16 = jnp.bfloat16
MESH = pl.DeviceIdType.MESH

EPS = 1e-6
POOL_WINDOWS = (2, 4, 8, 16)
POOL_HALO = 16
SSM_GROUP = 16
SSM_STATE = 64
SSM_BLOCKS = 4
N_DEV = 8
N_CHIPS = 4
ADAM_LR = 0.001
ADAM_B1 = 0.9
ADAM_B2 = 0.999
ADAM_EPS = 1e-08
ADAM_WD = 0.01
ADAM_STEP = 10
VMEM_LIMIT = 56 * 1024 * 1024


def _pcall(body, **kw):
    return pl.pallas_call(body, **kw)


def _params(*sem):
    return pltpu.CompilerParams(dimension_semantics=sem, vmem_limit_bytes=VMEM_LIMIT)


def _pick(n, cap, mult=128):
    if n <= cap:
        return n
    best = None
    for d in range(mult, cap + 1, mult):
        if n % d == 0:
            best = d
    assert best is not None, (n, cap, mult)
    return best


def _sigmoid(v):
    return 1.0 / (1.0 + jnp.exp(-v))


def _rowwise(name, fn, ins, params, outs, reds, tm):
    t = ins[0][0].shape[0]
    tm = min(tm, t)
    nb = t // tm
    ni, npar, no = len(ins), len(params), len(outs)

    def body(*refs):
        iv = [r[...] for r in refs[:ni]]
        pv = [r[...] for r in refs[ni:ni + npar]]
        o_refs = refs[ni + npar:ni + npar + no]
        r_refs = refs[ni + npar + no:]
        ovals, rvals = fn(iv, pv)
        for o_ref, val in zip(o_refs, ovals):
            parts = val if isinstance(val, (list, tuple)) else [val]
            off = 0
            for p in parts:
                o_ref[:, off:off + p.shape[1]] = p.astype(o_ref.dtype)
                off += p.shape[1]
        if r_refs:
            @pl.when(pl.program_id(0) == 0)
            def _():
                for r in r_refs:
                    r[...] = jnp.zeros_like(r)
            for r, val in zip(r_refs, rvals):
                r[...] += val

    in_specs = [pl.BlockSpec((tm, w), functools.partial(lambda i, cb: (i, cb), cb=cb)) for (_, w, cb) in ins]
    in_specs += [pl.BlockSpec(p.shape, lambda i: (0, 0)) for p in params]
    out_shape = [jax.ShapeDtypeStruct((t, w), dt) for (w, dt) in outs]
    out_shape += [jax.ShapeDtypeStruct((1, w), F32) for w in reds]
    out_specs = [pl.BlockSpec((tm, w), lambda i: (i, 0)) for (w, _) in outs]
    out_specs += [pl.BlockSpec((1, w), lambda i: (0, 0)) for w in reds]
    res = _pcall(body, name=name, grid=(nb,), in_specs=in_specs, out_specs=out_specs, out_shape=out_shape,
                 compiler_params=_params("arbitrary"))(*[a for a, _, _ in ins], *params)
    return res


def _colsum(v):
    return jnp.sum(v, axis=0, keepdims=True)


def _mm(name, a, b, *, ta=False, tb=False, b_stacked=False, out_stacked=False, tm=512, tn=1024, tk=2816,
        out_dtype=BF16, epi=None, epi_ins=(), epi_outs=None):
    if ta:
        kdim, m = a.shape
    else:
        m, kdim = a.shape
    if b_stacked:
        ns = b.shape[2]
        n = b.shape[1] if tb else N_CHIPS * ns
        assert kdim == (N_CHIPS * ns if tb else b.shape[1]), (name, a.shape, b.shape)
    else:
        n = b.shape[0] if tb else b.shape[1]
        assert kdim == (b.shape[1] if tb else b.shape[0]), (name, a.shape, b.shape)
    tm = _pick(m, tm, 8)
    if b_stacked and not tb:
        tn = _pick(ns, tn)
    elif out_stacked:
        ns_out = n // N_CHIPS
        tn = _pick(ns_out, tn)
    else:
        tn = _pick(n, tn)
    if b_stacked and tb:
        tk = _pick(ns, tk)
    else:
        tk = _pick(kdim, tk, 8 if ta else 128)
    nm, nn, nk = m // tm, n // tn, kdim // tk

    a_spec = pl.BlockSpec((tk, tm), lambda i, j, k: (k, i)) if ta else pl.BlockSpec((tm, tk), lambda i, j, k: (i, k))
    if b_stacked and not tb:
        bps = ns // tn
        b_spec = pl.BlockSpec((None, tk, tn), lambda i, j, k: (j // bps, k, j % bps))
    elif b_stacked and tb:
        bps = ns // tk
        b_spec = pl.BlockSpec((None, tn, tk), lambda i, j, k: (k // bps, j, k % bps))
    elif tb:
        b_spec = pl.BlockSpec((tn, tk), lambda i, j, k: (j, k))
    else:
        b_spec = pl.BlockSpec((tk, tn), lambda i, j, k: (k, j))
    dims = (((0 if ta else 1,), (1 if tb else 0,)), ((), ()))

    if epi_outs is None:
        if out_stacked:
            bps_o = (n // N_CHIPS) // tn
            epi_outs = [((N_CHIPS, m, n // N_CHIPS), out_dtype, (None, tm, tn), lambda i, j: (j // bps_o, i, j % bps_o))]
        else:
            epi_outs = [((m, n), out_dtype, (tm, tn), lambda i, j: (i, j))]
    ne, no = len(epi_ins), len(epi_outs)

    def body(a_ref, b_ref, *rest):
        e_refs = rest[:ne]
        o_refs = rest[ne:ne + no]
        scratch = rest[ne + no:]
        p = lax.dot_general(a_ref[...].astype(BF16), b_ref[...].astype(BF16), dims, preferred_element_type=F32)

        def finish(acc):
            vals = (acc,) if epi is None else epi(acc, *[r[...] for r in e_refs])
            for o_ref, val in zip(o_refs, vals):
                o_ref[...] = val.astype(o_ref.dtype)

        if nk == 1:
            finish(p)
        else:
            acc_ref = scratch[0]
            k = pl.program_id(2)

            @pl.when(k == 0)
            def _():
                acc_ref[...] = p

            @pl.when(k > 0)
            def _():
                acc_ref[...] += p

            @pl.when(k == nk - 1)
            def _():
                finish(acc_ref[...])

    def _ij(f):
        return lambda i, j, k: f(i, j)

    in_specs = [a_spec, b_spec] + [pl.BlockSpec(blk, _ij(f)) for (_, blk, f) in epi_ins]
    out_specs = [pl.BlockSpec(blk, _ij(f)) for (_, _, blk, f) in epi_outs]
    out_shape = [jax.ShapeDtypeStruct(s, dt) for (s, dt, _, _) in epi_outs]
    scratch = [pltpu.VMEM((tm, tn), F32)] if nk > 1 else []
    res = _pcall(body, name=name, grid=(nm, nn, nk), in_specs=in_specs, out_specs=out_specs, out_shape=out_shape,
                 scratch_shapes=scratch, compiler_params=_params("parallel", "parallel", "arbitrary"))(
                     a, b, *[x for x, _, _ in epi_ins])
    return res[0] if len(res) == 1 else res


def _norm_fwd(name, x, g, sc, sh):
    d = x.shape[1]

    def fn(iv, pv):
        (xv,), (gv, scv, shv) = iv, pv
        r = lax.rsqrt(jnp.mean(xv * xv, axis=-1, keepdims=True) + EPS)
        return [xv * r * gv * (1.0 + scv) + shv], []

    return _rowwise(name, fn, [(x, d, 0)], [g, sc, sh], [(d, BF16)], [], 512)[0]


def _norm_bwd(name, dh, x, dres, g, sc):
    d = x.shape[1]

    def fn(iv, pv):
        (dhv, xv, drv), (gv, scv) = iv, pv
        dhv = dhv.astype(F32)
        r = lax.rsqrt(jnp.mean(xv * xv, axis=-1, keepdims=True) + EPS)
        xr = xv * r
        dn = dhv * (1.0 + scv)
        gd = gv * dn
        dx = drv + r * (gd - xr * jnp.mean(gd * xr, axis=-1, keepdims=True))
        return [dx], [_colsum(dhv), _colsum(dhv * xr * gv), _colsum(dn * xr)]

    return _rowwise(name, fn, [(dh, d, 0), (x, d, 0), (dres, d, 0)], [g, sc], [(d, F32)], [d, d, d], 256)


def _loss_bwd(name, x, tgt, g):
    d = x.shape[1]

    def fn(iv, pv):
        (xv, tv), (gv,) = iv, pv
        r = lax.rsqrt(jnp.mean(xv * xv, axis=-1, keepdims=True) + EPS)
        xr = xv * r
        e = xr * gv - tv
        loss_row = 0.5 * jnp.mean(e * e, axis=-1, keepdims=True)
        dout = e * (1.0 / d)
        gd = gv * dout
        dx = r * (gd - xr * jnp.mean(gd * xr, axis=-1, keepdims=True))
        return [dx], [_colsum(dout * xr), _colsum(loss_row * jnp.ones((1, 128), F32))]

    return _rowwise(name, fn, [(x, d, 0), (tgt, d, 0)], [g], [(d, F32)], [d, 128], 256)


def _resgate_bwd(name, dx, y, gt, factor):
    d = dx.shape[1]

    def fn(iv, pv):
        (dxv, yv), (gtv,) = iv, pv
        return [factor * gtv * dxv], [_colsum(factor * dxv * yv.astype(F32))]

    return _rowwise(name, fn, [(dx, d, 0), (y, d, 0)], [gt], [(d, BF16)], [d], 512)


def _swiglu_fwd(name, ab):
    f = ab.shape[1] // 2

    def fn(iv, pv):
        a, b = iv[0].astype(F32), iv[1].astype(F32)
        return [a * _sigmoid(a) * b], []

    return _rowwise(name, fn, [(ab, f, 0), (ab, f, 1)], [], [(f, BF16)], [], 256)[0]


def _swiglu_bwd(name, dact, ab):
    f = ab.shape[1] // 2

    def fn(iv, pv):
        dv, a, b = iv[0].astype(F32), iv[1].astype(F32), iv[2].astype(F32)
        s = _sigmoid(a)
        return [[dv * b * (s * (1.0 + a * (1.0 - s))), dv * (a * s)]], []

    return _rowwise(name, fn, [(dact, f, 0), (ab, f, 0), (ab, f, 1)], [], [(2 * f, BF16)], [], 256)[0]


def _gates_fwd(name, y_pool, y_ssm, uin, d, pw):
    cb = (2 * pw) // d

    def fn(iv, pv):
        yp, ys, glp, gls = [v.astype(F32) for v in iv]
        return [_sigmoid(glp) * yp + _sigmoid(gls) * ys], []

    return _rowwise(name, fn, [(y_pool, d, 0), (y_ssm, d, 0), (uin, d, cb), (uin, d, cb + 1)], [], [(d, BF16)], [], 256)[0]


def _gates_bwd(name, dm, y_pool, y_ssm, uin, d, pw):
    cb = (2 * pw) // d

    def fn(iv, pv):
        dmv, yp, ys, glp, gls = [v.astype(F32) for v in iv]
        sp, ss = _sigmoid(glp), _sigmoid(gls)
        return [dmv * sp, dmv * ss, [dmv * yp * sp * (1.0 - sp), dmv * ys * ss * (1.0 - ss)]], []

    return _rowwise(name, fn, [(dm, d, 0), (y_pool, d, 0), (y_ssm, d, 0), (uin, d, cb), (uin, d, cb + 1)], [],
                    [(d, BF16), (d, BF16), (2 * d, BF16)], [], 256)


def _glu_fwd(name, gv, b_glu):
    w = gv.shape[1] // 2

    def fn(iv, pv):
        gvv, (bv,) = iv[0].astype(F32) + pv[0], pv
        return [gvv[:, :w] * _sigmoid(gvv[:, w:])], []

    return _rowwise(name, fn, [(gv, 2 * w, 0)], [b_glu], [(w, BF16)], [], 512)[0]


def _glu_bwd(name, dsg, gv, b_glu):
    w = gv.shape[1] // 2

    def fn(iv, pv):
        dv = iv[0].astype(F32)
        gvv = iv[1].astype(F32) + pv[0]
        val, s = gvv[:, :w], _sigmoid(gvv[:, w:])
        dval = dv * s
        dgate = dv * val * s * (1.0 - s)
        return [[dval, dgate]], [jnp.concatenate([_colsum(dval), _colsum(dgate)], axis=1)]

    return _rowwise(name, fn, [(dsg, w, 0), (gv, 2 * w, 0)], [b_glu], [(2 * w, BF16)], [2 * w], 512)


def _adamw(name, w, g, m, v, tm=256):
    c = w.shape[1]

    def fn(iv, pv):
        wv, gv, mv, vv = iv
        mn = ADAM_B1 * mv + (1.0 - ADAM_B1) * gv
        vn = ADAM_B2 * vv + (1.0 - ADAM_B2) * (gv * gv)
        m_hat = mn / (1.0 - ADAM_B1 ** ADAM_STEP)
        v_hat = vn / (1.0 - ADAM_B2 ** ADAM_STEP)
        delta = -ADAM_LR * (m_hat / (jnp.sqrt(v_hat) + ADAM_EPS) + ADAM_WD * wv)
        return [delta, mn, vn], []

    return _rowwise(name, fn, [(w, c, 0), (g, c, 0), (m, c, 0), (v, c, 0)], [], [(c, F32)] * 3, [], _pick(w.shape[0], tm, 8))


def _adamw_small(name, w, g8, m, v):
    r = w.shape[0]

    def body(w_ref, g_ref, m_ref, v_ref, go_ref, d_ref, mo_ref, vo_ref):
        gv = g_ref[0:r, :]
        for k in range(1, N_DEV):
            gv = gv + g_ref[k * r:(k + 1) * r, :]
        mn = ADAM_B1 * m_ref[...] + (1.0 - ADAM_B1) * gv
        vn = ADAM_B2 * v_ref[...] + (1.0 - ADAM_B2) * (gv * gv)
        m_hat = mn / (1.0 - ADAM_B1 ** ADAM_STEP)
        v_hat = vn / (1.0 - ADAM_B2 ** ADAM_STEP)
        go_ref[...] = gv
        d_ref[...] = -ADAM_LR * (m_hat / (jnp.sqrt(v_hat) + ADAM_EPS) + ADAM_WD * w_ref[...])
        mo_ref[...] = mn
        vo_ref[...] = vn

    return _pcall(body, name=name, out_shape=[jax.ShapeDtypeStruct((r, 128), F32)] * 4,
                  compiler_params=pltpu.CompilerParams(vmem_limit_bytes=VMEM_LIMIT))(w, g8, m, v)


def _pool_fwd(name, uin, pool_w, pool_b, pool_scale, pw, tm=512):
    t = uin.shape[0]
    tm = min(tm, t)
    ng = len(POOL_WINDOWS)
    gw = pw // ng

    def body(u_ref, w_ref, b_ref, s_ref, p_ref, z_ref, ext):
        i = pl.program_id(0)

        @pl.when(i == 0)
        def _():
            ext[0:POOL_HALO, :] = jnp.zeros((POOL_HALO, pw), F32)

        u = u_ref[...]
        ext[POOL_HALO:POOL_HALO + tm, :] = u
        pos = i * tm + lax.broadcasted_iota(jnp.int32, (tm, 1), 0)
        for k, win in enumerate(POOL_WINDOWS):
            cols = slice(k * gw, (k + 1) * gw)
            acc = u[:, cols]
            for j in range(1, win):
                acc = acc + ext[POOL_HALO - j:POOL_HALO - j + tm, cols]
            cnt = jnp.minimum(pos + 1, win).astype(F32)
            z = acc / cnt - u[:, cols]
            zp = jnp.dot(z.astype(BF16), w_ref[k].astype(BF16), preferred_element_type=F32) + b_ref[:, cols]
            p_ref[:, cols] = (zp * s_ref[:, cols]).astype(BF16)
            z_ref[:, cols] = z.astype(BF16)
        ext[0:POOL_HALO, :] = u[tm - POOL_HALO:tm, :]

    return _pcall(
        body, name=name, grid=(t // tm,),
        in_specs=[pl.BlockSpec((tm, pw), lambda i: (i, 0)), pl.BlockSpec(pool_w.shape, lambda i: (0, 0, 0)),
                  pl.BlockSpec(pool_b.shape, lambda i: (0, 0)), pl.BlockSpec(pool_scale.shape, lambda i: (0, 0))],
        out_specs=[pl.BlockSpec((tm, pw), lambda i: (i, 0))] * 2,
        out_shape=[jax.ShapeDtypeStruct((t, pw), BF16)] * 2,
        scratch_shapes=[pltpu.VMEM((POOL_HALO + tm, pw), F32)],
        compiler_params=_params("arbitrary"))(uin, pool_w, pool_b, pool_scale)


def _pool_bwd(name, dp, z, pool_w, pool_b, pool_scale, tm=512):
    t, pw = z.shape
    tm = min(tm, t)
    nb = t // tm
    ng = len(POOL_WINDOWS)
    gw = pw // ng

    def body(dp_ref, z_ref, w_ref, b_ref, s_ref, du_ref, dw_ref, db_ref, ds_ref, ext):
        i = pl.program_id(0)

        @pl.when(i == 0)
        def _():
            ext[tm:tm + POOL_HALO, :] = jnp.zeros((POOL_HALO, pw), F32)
            dw_ref[...] = jnp.zeros_like(dw_ref)
            db_ref[...] = jnp.zeros_like(db_ref)
            ds_ref[...] = jnp.zeros_like(ds_ref)

        pos = (nb - 1 - i) * tm + lax.broadcasted_iota(jnp.int32, (tm, 1), 0)
        for k, win in enumerate(POOL_WINDOWS):
            cols = slice(k * gw, (k + 1) * gw)
            zk = z_ref[:, cols]
            dpk = dp_ref[:, cols].astype(F32)
            wk = w_ref[k].astype(BF16)
            zp = jnp.dot(zk, wk, preferred_element_type=F32) + b_ref[:, cols]
            ds_ref[:, cols] += _colsum(dpk * zp)
            dzp = dpk * s_ref[:, cols]
            db_ref[:, cols] += _colsum(dzp)
            dzpb = dzp.astype(BF16)
            dz = lax.dot_general(dzpb, wk, (((1,), (1,)), ((), ())), preferred_element_type=F32)
            dw_ref[k] += lax.dot_general(zk, dzpb, (((0,), (0,)), ((), ())), preferred_element_type=F32)
            cnt = jnp.minimum(pos + 1, win).astype(F32)
            r = dz / cnt
            ext[0:tm, cols] = r
            acc = r - dz
            for j in range(1, win):
                acc = acc + ext[j:j + tm, cols]
            du_ref[:, cols] = acc.astype(BF16)
        ext[tm:tm + POOL_HALO, :] = ext[0:POOL_HALO, :]

    rev = lambda i: (nb - 1 - i, 0)
    return _pcall(
        body, name=name, grid=(nb,),
        in_specs=[pl.BlockSpec((tm, pw), rev), pl.BlockSpec((tm, pw), rev), pl.BlockSpec(pool_w.shape, lambda i: (0, 0, 0)),
                  pl.BlockSpec(pool_b.shape, lambda i: (0, 0)), pl.BlockSpec(pool_scale.shape, lambda i: (0, 0))],
        out_specs=[pl.BlockSpec((tm, pw), rev), pl.BlockSpec(pool_w.shape, lambda i: (0, 0, 0)),
                   pl.BlockSpec((1, pw), lambda i: (0, 0)), pl.BlockSpec((1, pw), lambda i: (0, 0))],
        out_shape=[jax.ShapeDtypeStruct((t, pw), BF16), jax.ShapeDtypeStruct(pool_w.shape, F32),
                   jax.ShapeDtypeStruct((1, pw), F32), jax.ShapeDtypeStruct((1, pw), F32)],
        scratch_shapes=[pltpu.VMEM((tm + POOL_HALO, pw), F32)],
        compiler_params=_params("arbitrary"))(dp, z, pool_w, pool_b, pool_scale)


def _ssm_disc(lrl, li, ldt):
    lr = -jnp.exp(lrl)
    dt = jnp.exp(ldt)
    mag = jnp.exp(lr * dt)
    ang = li * dt
    ab_re = mag * jnp.cos(ang)
    ab_im = mag * jnp.sin(ang)
    num_re = ab_re - 1.0
    num_im = ab_im
    den = lr * lr + li * li
    f_re = (num_re * lr + num_im * li) / den
    f_im = (num_im * lr - num_re * li) / den
    return lr, dt, mag, ang, ab_re, ab_im, num_re, num_im, den, f_re, f_im


def _ssm_prep(name, lrl, li, ldt, b_re, b_im):
    gn, h = b_re.shape

    def body(lrl_ref, li_ref, ldt_ref, br_ref, bi_ref, lrdt_ref, ang_ref, bbr_ref, bbi_ref):
        lr, dt, _, ang, _, _, _, _, _, f_re, f_im = _ssm_disc(lrl_ref[...], li_ref[...], ldt_ref[...])
        lrdt_ref[...] = lr * dt
        ang_ref[...] = ang
        br, bi = br_ref[...], bi_ref[...]
        bbr_ref[...] = f_re * br - f_im * bi
        bbi_ref[...] = f_re * bi + f_im * br

    col = jax.ShapeDtypeStruct((gn, 1), F32)
    mat = jax.ShapeDtypeStruct((gn, h), F32)
    return _pcall(body, name=name, out_shape=[col, col, mat, mat])(lrl, li, ldt, b_re, b_im)


def _ssm_param_bwd(name, lrl, li, ldt, b_re, b_im, g_abre, g_abim, g_bbre, g_bbim):
    gn, h = b_re.shape

    def body(lrl_ref, li_ref, ldt_ref, br_ref, bi_ref, gar_ref, gai_ref, gbr_ref, gbi_ref,
             dlrl_ref, dli_ref, dldt_ref, dbr_ref, dbi_ref):
        li_v = li_ref[...]
        lr, dt, mag, ang, ab_re, ab_im, num_re, num_im, den, f_re, f_im = _ssm_disc(lrl_ref[...], li_v, ldt_ref[...])
        br, bi = br_ref[...], bi_ref[...]
        gbr, gbi = gbr_ref[...], gbi_ref[...]
        g_fre = jnp.sum(gbr * br + gbi * bi, axis=1, keepdims=True)
        g_fim = jnp.sum(gbi * br - gbr * bi, axis=1, keepdims=True)
        dbr_ref[...] = gbr * f_re + gbi * f_im
        dbi_ref[...] = gbi * f_re - gbr * f_im
        g_num_re = (g_fre * lr - g_fim * li_v) / den
        g_num_im = (g_fre * li_v + g_fim * lr) / den
        g_den = -(g_fre * f_re + g_fim * f_im) / den
        g_lr = (g_fre * num_re + g_fim * num_im) / den + g_den * 2.0 * lr
        g_li = (g_fre * num_im - g_fim * num_re) / den + g_den * 2.0 * li_v
        g_are = gar_ref[...] + g_num_re
        g_aim = gai_ref[...] + g_num_im
        g_mag = g_are * jnp.cos(ang) + g_aim * jnp.sin(ang)
        g_ang = g_aim * ab_re - g_are * ab_im
        g_lrdt = g_mag * mag
        g_lr = g_lr + g_lrdt * dt
        g_dt = g_lrdt * lr + g_ang * li_v
        g_li = g_li + g_ang * dt
        dlrl_ref[...] = g_lr * lr
        dli_ref[...] = g_li
        dldt_ref[...] = g_dt * dt

    col = jax.ShapeDtypeStruct((gn, 1), F32)
    mat = jax.ShapeDtypeStruct((gn, h), F32)
    return _pcall(body, name=name, out_shape=[col, col, col, mat, mat])(lrl, li, ldt, b_re, b_im, g_abre, g_abim, g_bbre, g_bbim)


def _pow_rows(lrdt, ang, k):
    mag = jnp.exp(k * lrdt)
    return mag * jnp.cos(k * ang), mag * jnp.sin(k * ang)


def _ssm_chunk(t):
    return 256 if t >= 2048 else 128


def _ssm_fwd(name, uin, lrdt, ang, bb_re, bb_im, cc_re, cc_im, d_skip, sw):
    t = uin.shape[0]
    gn = lrdt.shape[1]
    lc = _ssm_chunk(t)
    nsteps = int(math.log2(lc))
    pad = lc // 2
    ub, sb = sw // SSM_BLOCKS, gn // SSM_BLOCKS

    def body(u_ref, lrdt_ref, ang_ref, bbr_ref, bbi_ref, ccr_ref, cci_ref, d_ref, y_ref, ge_ref, sre_ref, sim_ref,
             p_re, p_im, a_re, a_im, b_re, b_im, car_re, car_im):
        i = pl.program_id(0)
        lrdt_v, ang_v = lrdt_ref[...], ang_ref[...]

        @pl.when(i == 0)
        def _():
            k = (lax.broadcasted_iota(jnp.int32, (lc, 1), 0) + 1).astype(F32)
            pr, pi = _pow_rows(lrdt_v, ang_v, k)
            p_re[...] = pr
            p_im[...] = pi
            zeros = jnp.zeros((pad, gn), F32)
            a_re[0:pad, :] = zeros
            a_im[0:pad, :] = zeros
            b_re[0:pad, :] = zeros
            b_im[0:pad, :] = zeros
            car_re[...] = jnp.zeros_like(car_re)
            car_im[...] = jnp.zeros_like(car_im)

        u = u_ref[...]
        ubf = u.astype(BF16)
        for q in range(SSM_BLOCKS):
            uq = ubf[:, q * ub:(q + 1) * ub]
            a_re[pad:pad + lc, q * sb:(q + 1) * sb] = jnp.dot(uq, bbr_ref[q], preferred_element_type=F32)
            a_im[pad:pad + lc, q * sb:(q + 1) * sb] = jnp.dot(uq, bbi_ref[q], preferred_element_type=F32)
        bufs = [(a_re, a_im), (b_re, b_im)]
        for q in range(SSM_BLOCKS):
            cols = slice(q * sb, (q + 1) * sb)
            for j in range(nsteps):
                dd = 1 << j
                (s_re, s_im), (d_re, d_im) = bufs[j % 2], bufs[(j + 1) % 2]
                pr, pi = _pow_rows(lrdt_v[:, cols], ang_v[:, cols], float(dd))
                cr, ci = s_re[pad:pad + lc, cols], s_im[pad:pad + lc, cols]
                hr, hi = s_re[pad - dd:pad - dd + lc, cols], s_im[pad - dd:pad - dd + lc, cols]
                d_re[pad:pad + lc, cols] = cr + (pr * hr - pi * hi)
                d_im[pad:pad + lc, cols] = ci + (pr * hi + pi * hr)
            f_re, f_im = bufs[nsteps % 2]
            cr, ci = car_re[:, cols], car_im[:, cols]
            pr, pi = p_re[:, cols], p_im[:, cols]
            sr = f_re[pad:pad + lc, cols] + (pr * cr - pi * ci)
            si = f_im[pad:pad + lc, cols] + (pr * ci + pi * cr)
            sre_ref[:, cols] = sr
            sim_ref[:, cols] = si
            car_re[:, cols] = sr[lc - 1:lc, :]
            car_im[:, cols] = si[lc - 1:lc, :]
            ycols = slice(q * ub, (q + 1) * ub)
            y = (jnp.dot(sr.astype(BF16), ccr_ref[q], preferred_element_type=F32)
                 - jnp.dot(si.astype(BF16), cci_ref[q], preferred_element_type=F32)
                 + d_ref[:, ycols] * u[:, ycols])
            y_ref[:, ycols] = y
            ge_ref[:, ycols] = (0.5 * y * (1.0 + lax.erf(y * (1.0 / math.sqrt(2.0))))).astype(BF16)

    cblk = sw // sw
    row = lambda i: (0, 0)
    blk3 = lambda i: (0, 0, 0)
    return _pcall(
        body, name=name, grid=(t // lc,),
        in_specs=[pl.BlockSpec((lc, sw), lambda i: (i, cblk)), pl.BlockSpec((1, gn), row), pl.BlockSpec((1, gn), row),
                  pl.BlockSpec(bb_re.shape, blk3), pl.BlockSpec(bb_im.shape, blk3),
                  pl.BlockSpec(cc_re.shape, blk3), pl.BlockSpec(cc_im.shape, blk3), pl.BlockSpec((1, sw), row)],
        out_specs=[pl.BlockSpec((lc, sw), lambda i: (i, 0)), pl.BlockSpec((lc, sw), lambda i: (i, 0)),
                   pl.BlockSpec((lc, gn), lambda i: (i, 0)), pl.BlockSpec((lc, gn), lambda i: (i, 0))],
        out_shape=[jax.ShapeDtypeStruct((t, sw), F32), jax.ShapeDtypeStruct((t, sw), BF16),
                   jax.ShapeDtypeStruct((t, gn), F32), jax.ShapeDtypeStruct((t, gn), F32)],
        scratch_shapes=[pltpu.VMEM((lc, gn), F32), pltpu.VMEM((lc, gn), F32),
                        pltpu.VMEM((pad + lc, gn), F32), pltpu.VMEM((pad + lc, gn), F32),
                        pltpu.VMEM((pad + lc, gn), F32), pltpu.VMEM((pad + lc, gn), F32),
                        pltpu.VMEM((1, gn), F32), pltpu.VMEM((1, gn), F32)],
        compiler_params=_params("arbitrary"))(uin, lrdt, ang, bb_re, bb_im, cc_re, cc_im, d_skip)


def _ssm_bwd(name, dge, y, uin, s_re, s_im, lrdt, ang, bbt_re, bbt_im, cct_re, cct_im, d_skip, sw):
    t = uin.shape[0]
    gn = lrdt.shape[1]
    lc = _ssm_chunk(t)
    nb = t // lc
    nsteps = int(math.log2(lc))
    pad = lc // 2
    ub, sb = sw // SSM_BLOCKS, gn // SSM_BLOCKS
    tail = 8

    def body(dge_ref, y_ref, u_ref, sre_ref, sim_ref, tre_ref, tim_ref, lrdt_ref, ang_ref, btr_ref, bti_ref, ctr_ref, cti_ref,
             d_ref, du_ref, dar_ref, dai_ref, dbr_ref, dbi_ref, dcr_ref, dci_ref, dd_ref,
             q_re, q_im, a_re, a_im, b_re, b_im, sp_re, sp_im, car_re, car_im):
        i = pl.program_id(0)
        lrdt_v, ang_v = lrdt_ref[...], ang_ref[...]

        @pl.when(i == 0)
        def _():
            k = (lc - lax.broadcasted_iota(jnp.int32, (lc, 1), 0)).astype(F32)
            qr, qi = _pow_rows(lrdt_v, ang_v, k)
            q_re[...] = qr
            q_im[...] = qi
            zeros = jnp.zeros((pad, gn), F32)
            a_re[lc:lc + pad, :] = zeros
            a_im[lc:lc + pad, :] = zeros
            b_re[lc:lc + pad, :] = zeros
            b_im[lc:lc + pad, :] = zeros
            car_re[...] = jnp.zeros_like(car_re)
            car_im[...] = jnp.zeros_like(car_im)
            for r in (dar_ref, dai_ref, dbr_ref, dbi_ref, dcr_ref, dci_ref, dd_ref):
                r[...] = jnp.zeros_like(r)

        first = (i == nb - 1).astype(F32)
        sp_re[0:tail, :] = tre_ref[...] * (1.0 - first)
        sp_im[0:tail, :] = tim_ref[...] * (1.0 - first)
        sp_re[tail:tail + lc, :] = sre_ref[...]
        sp_im[tail:tail + lc, :] = sim_ref[...]

        yv = y_ref[...]
        u = u_ref[...]
        cdf = 0.5 * (1.0 + lax.erf(yv * (1.0 / math.sqrt(2.0))))
        pdf = jnp.exp(-0.5 * yv * yv) * (1.0 / math.sqrt(2.0 * math.pi))
        dy = dge_ref[...].astype(F32) * (cdf + yv * pdf)
        dd_ref[...] += _colsum(dy * u)
        dyb = dy.astype(BF16)
        ubf = u.astype(BF16)
        for q in range(SSM_BLOCKS):
            dq = dyb[:, q * ub:(q + 1) * ub]
            a_re[0:lc, q * sb:(q + 1) * sb] = jnp.dot(dq, ctr_ref[q], preferred_element_type=F32)
            a_im[0:lc, q * sb:(q + 1) * sb] = -jnp.dot(dq, cti_ref[q], preferred_element_type=F32)
        bufs = [(a_re, a_im), (b_re, b_im)]
        tn_dims = (((0,), (0,)), ((), ()))
        for q in range(SSM_BLOCKS):
            cols = slice(q * sb, (q + 1) * sb)
            ycols = slice(q * ub, (q + 1) * ub)
            for j in range(nsteps):
                ds = 1 << j
                (s_r, s_i), (d_r, d_i) = bufs[j % 2], bufs[(j + 1) % 2]
                pr, pi = _pow_rows(lrdt_v[:, cols], ang_v[:, cols], float(ds))
                cr, ci = s_r[0:lc, cols], s_i[0:lc, cols]
                hr, hi = s_r[ds:ds + lc, cols], s_i[ds:ds + lc, cols]
                d_r[0:lc, cols] = cr + (pr * hr + pi * hi)
                d_i[0:lc, cols] = ci + (pr * hi - pi * hr)
            f_r, f_i = bufs[nsteps % 2]
            cr, ci = car_re[:, cols], car_im[:, cols]
            qr, qi = q_re[:, cols], q_im[:, cols]
            lam_r = f_r[0:lc, cols] + (qr * cr + qi * ci)
            lam_i = f_i[0:lc, cols] + (qr * ci - qi * cr)
            car_re[:, cols] = lam_r[0:1, :]
            car_im[:, cols] = lam_i[0:1, :]
            pvr, pvi = sp_re[tail - 1:tail - 1 + lc, cols], sp_im[tail - 1:tail - 1 + lc, cols]
            dar_ref[:, cols] += _colsum(lam_r * pvr + lam_i * pvi)
            dai_ref[:, cols] += _colsum(lam_i * pvr - lam_r * pvi)
            lrb, lib = lam_r.astype(BF16), lam_i.astype(BF16)
            du_ref[:, ycols] = (jnp.dot(lrb, btr_ref[q], preferred_element_type=F32)
                                + jnp.dot(lib, bti_ref[q], preferred_element_type=F32)
                                + d_ref[:, ycols] * dy[:, ycols]).astype(BF16)
            uq = ubf[:, ycols]
            dbr_ref[q] += lax.dot_general(uq, lrb, tn_dims, preferred_element_type=F32)
            dbi_ref[q] += lax.dot_general(uq, lib, tn_dims, preferred_element_type=F32)
            dq = dyb[:, ycols]
            dcr_ref[q] += lax.dot_general(sre_ref[:, cols].astype(BF16), dq, tn_dims, preferred_element_type=F32)
            dci_ref[q] -= lax.dot_general(sim_ref[:, cols].astype(BF16), dq, tn_dims, preferred_element_type=F32)

    cblk = 1
    rev = lambda i: (nb - 1 - i, 0)
    revu = lambda i: (nb - 1 - i, cblk)
    tailmap = lambda i: (jnp.maximum((nb - 1 - i) * (lc // tail) - 1, 0), 0)
    row = lambda i: (0, 0)
    blk3 = lambda i: (0, 0, 0)
    return _pcall(
        body, name=name, grid=(nb,),
        in_specs=[pl.BlockSpec((lc, sw), rev), pl.BlockSpec((lc, sw), rev), pl.BlockSpec((lc, sw), revu),
                  pl.BlockSpec((lc, gn), rev), pl.BlockSpec((lc, gn), rev),
                  pl.BlockSpec((tail, gn), tailmap), pl.BlockSpec((tail, gn), tailmap),
                  pl.BlockSpec((1, gn), row), pl.BlockSpec((1, gn), row),
                  pl.BlockSpec(bbt_re.shape, blk3), pl.BlockSpec(bbt_im.shape, blk3),
                  pl.BlockSpec(cct_re.shape, blk3), pl.BlockSpec(cct_im.shape, blk3), pl.BlockSpec((1, sw), row)],
        out_specs=[pl.BlockSpec((lc, sw), rev), pl.BlockSpec((1, gn), row), pl.BlockSpec((1, gn), row),
                   pl.BlockSpec((SSM_BLOCKS, ub, sb), blk3), pl.BlockSpec((SSM_BLOCKS, ub, sb), blk3),
                   pl.BlockSpec((SSM_BLOCKS, sb, ub), blk3), pl.BlockSpec((SSM_BLOCKS, sb, ub), blk3),
                   pl.BlockSpec((1, sw), row)],
        out_shape=[jax.ShapeDtypeStruct((t, sw), BF16), jax.ShapeDtypeStruct((1, gn), F32), jax.ShapeDtypeStruct((1, gn), F32),
                   jax.ShapeDtypeStruct((SSM_BLOCKS, ub, sb), F32), jax.ShapeDtypeStruct((SSM_BLOCKS, ub, sb), F32),
                   jax.ShapeDtypeStruct((SSM_BLOCKS, sb, ub), F32), jax.ShapeDtypeStruct((SSM_BLOCKS, sb, ub), F32),
                   jax.ShapeDtypeStruct((1, sw), F32)],
        scratch_shapes=[pltpu.VMEM((lc, gn), F32), pltpu.VMEM((lc, gn), F32),
                        pltpu.VMEM((lc + pad, gn), F32), pltpu.VMEM((lc + pad, gn), F32),
                        pltpu.VMEM((lc + pad, gn), F32), pltpu.VMEM((lc + pad, gn), F32),
                        pltpu.VMEM((tail + lc, gn), F32), pltpu.VMEM((tail + lc, gn), F32),
                        pltpu.VMEM((1, gn), F32), pltpu.VMEM((1, gn), F32)],
        compiler_params=_params("arbitrary"))(dge, y, uin, s_re, s_im, s_re, s_im, lrdt, ang, bbt_re, bbt_im, cct_re, cct_im, d_skip)


def _blockdiag_b(bb, sw):
    gpb = (sw // SSM_GROUP) // SSM_BLOCKS
    b4 = bb.reshape(SSM_BLOCKS, gpb, SSM_STATE, SSM_GROUP)
    eye = jnp.eye(gpb, dtype=bb.dtype)
    out = jnp.einsum('qgnh,gk->qghkn', b4, eye)
    return out.reshape(SSM_BLOCKS, gpb * SSM_GROUP, gpb * SSM_STATE)


def _blockdiag_c(cc, sw):
    gpb = (sw // SSM_GROUP) // SSM_BLOCKS
    c4 = cc.reshape(SSM_BLOCKS, gpb, SSM_GROUP, SSM_STATE)
    eye = jnp.eye(gpb, dtype=cc.dtype)
    out = jnp.einsum('qghn,gk->qgnkh', c4, eye)
    return out.reshape(SSM_BLOCKS, gpb * SSM_STATE, gpb * SSM_GROUP)


def _diag_of_b(dbb, sw):
    gpb = (sw // SSM_GROUP) // SSM_BLOCKS
    d5 = dbb.reshape(SSM_BLOCKS, gpb, SSM_GROUP, gpb, SSM_STATE)
    return jnp.einsum('qghgn->qgnh', d5).reshape(SSM_BLOCKS * gpb * SSM_STATE, SSM_GROUP)


def _diag_of_c(dcc, sw):
    gpb = (sw // SSM_GROUP) // SSM_BLOCKS
    d5 = dcc.reshape(SSM_BLOCKS, gpb, SSM_STATE, gpb, SSM_GROUP)
    return jnp.einsum('qgngh->qghn', d5).reshape(SSM_BLOCKS * gpb, SSM_GROUP, SSM_STATE)


def _ada_fwd(name, c_all, w_sh, b_sh):
    nb, d = c_all.shape
    ncol = w_sh.shape[1]
    tn = _pick(ncol, 768)

    def body(c_ref, w_ref, b_ref, o_ref):
        cv = c_ref[...]
        sil = cv * _sigmoid(cv)
        o_ref[...] = jnp.dot(sil, w_ref[...], preferred_element_type=F32, precision=lax.Precision.HIGHEST) + b_ref[...]

    return _pcall(body, name=name, grid=(ncol // tn,),
                  in_specs=[pl.BlockSpec((nb, d), lambda j: (0, 0)), pl.BlockSpec((d, tn), lambda j: (0, j)),
                            pl.BlockSpec((1, tn), lambda j: (0, j))],
                  out_specs=pl.BlockSpec((nb, tn), lambda j: (0, j)),
                  out_shape=jax.ShapeDtypeStruct((nb, ncol), F32), compiler_params=_params("parallel"))(c_all, w_sh, b_sh)


def _ada_bwd(name, c_all, dmod_sh):
    nb, d = c_all.shape
    ncol = dmod_sh.shape[1]
    tn = _pick(ncol, 768)

    def body(c_ref, g_ref, o_ref):
        cv = c_ref[...]
        sil = cv * _sigmoid(cv)
        o_ref[...] = lax.dot_general(sil, g_ref[...], (((0,), (0,)), ((), ())), preferred_element_type=F32,
                                     precision=lax.Precision.HIGHEST)

    return _pcall(body, name=name, grid=(ncol // tn,),
                  in_specs=[pl.BlockSpec((nb, d), lambda j: (0, 0)), pl.BlockSpec((nb, tn), lambda j: (0, j))],
                  out_specs=pl.BlockSpec((d, tn), lambda j: (0, j)),
                  out_shape=jax.ShapeDtypeStruct((d, ncol), F32), compiler_params=_params("parallel"))(c_all, dmod_sh)


def _place():
    return lax.axis_index("x"), lax.axis_index("y"), lax.axis_index("c")


def _allgather_small(name, blk):
    m_per, n = blk.shape

    def body(x_ref, out_ref, send_sems, recv_sems, local_sem):
        x, y, c = _place()
        me, sibling = (x, y, c), (x, y, 1 - c)
        chips = [(1 - x, y), (x, 1 - y), (1 - x, 1 - y)]

        def rows(px, py, pc):
            return out_ref.at[pl.ds((4 * px + 2 * py + pc) * m_per, m_per), :]

        def copy(k, block, to, src=None):
            return pltpu.make_async_remote_copy(
                src_ref=rows(*block) if src is None else src, dst_ref=rows(*block),
                send_sem=send_sems.at[k], recv_sem=recv_sems.at[k], device_id=to, device_id_type=MESH)

        mine = pltpu.make_async_copy(x_ref, rows(*me), local_sem)
        mine.start()
        first = [copy(0, me, sibling, src=x_ref)]
        first += [copy(1 + j, me, (*chip, c), src=x_ref) for j, chip in enumerate(chips)]
        for cp in first:
            cp.start()
        passed = [copy(4 + j, (*chip, c), sibling) for j, chip in enumerate(chips)]
        for j, chip in enumerate(chips):
            copy(1 + j, (*chip, c), me).wait_recv()
            passed[j].start()
        copy(0, sibling, me).wait_recv()
        for j, chip in enumerate(chips):
            copy(4 + j, (*chip, 1 - c), me).wait_recv()
        for cp in first + passed:
            cp.wait_send()
        mine.wait()

    return _pcall(body, name=name, out_shape=jax.ShapeDtypeStruct((N_DEV * m_per, n), blk.dtype),
                  in_specs=[pl.BlockSpec(memory_space=pltpu.VMEM)], out_specs=pl.BlockSpec(memory_space=pltpu.VMEM),
                  scratch_shapes=[pltpu.SemaphoreType.DMA((7,)), pltpu.SemaphoreType.DMA((7,)), pltpu.SemaphoreType.DMA],
                  compiler_params=pltpu.CompilerParams(vmem_limit_bytes=VMEM_LIMIT))(blk)


def _allgather_weights(name, shards):
    nw = len(shards)

    def body(*refs):
        src = refs[:nw]
        out = refs[nw:2 * nw]
        send_ici, recv_ici, send_sib, recv_sib, local_sem = refs[2 * nw:]
        x, y, c = _place()
        s_me = 2 * x + y
        chips = [(1 - x, y), (x, 1 - y), (1 - x, 1 - y)]
        locals_, sends, fwds = [], [], []
        for w in range(nw):
            hr = src[w].shape[0] // 2
            cp = pltpu.make_async_copy(src[w], out[w].at[s_me], local_sem.at[w])
            cp.start()
            locals_.append(cp)
            for j, (cx, cy) in enumerate(chips):
                half = pl.ds(pl.multiple_of(c * hr, 16), hr)
                cp = pltpu.make_async_remote_copy(
                    src_ref=src[w].at[half, :], dst_ref=out[w].at[s_me, half, :],
                    send_sem=send_ici.at[w, j], recv_sem=recv_ici.at[w, j], device_id=(cx, cy, c), device_id_type=MESH)
                cp.start()
                sends.append(cp)
        for w in range(nw):
            hr = src[w].shape[0] // 2
            for j, (cx, cy) in enumerate(chips):
                s_j = 2 * cx + cy
                half = pl.ds(pl.multiple_of(c * hr, 16), hr)
                got = out[w].at[s_j, half, :]
                pltpu.make_async_remote_copy(
                    src_ref=got, dst_ref=got, send_sem=send_ici.at[w, j], recv_sem=recv_ici.at[w, j],
                    device_id=(cx, cy, c), device_id_type=MESH).wait_recv()
                cp = pltpu.make_async_remote_copy(
                    src_ref=got, dst_ref=got, send_sem=send_sib.at[w, j], recv_sem=recv_sib.at[w, j],
                    device_id=(x, y, 1 - c), device_id_type=MESH)
                cp.start()
                fwds.append(cp)
        for w in range(nw):
            hr = src[w].shape[0] // 2
            for j, (cx, cy) in enumerate(chips):
                s_j = 2 * cx + cy
                other = pl.ds(pl.multiple_of((1 - c) * hr, 16), hr)
                got = out[w].at[s_j, other, :]
                pltpu.make_async_remote_copy(
                    src_ref=got, dst_ref=got, send_sem=send_sib.at[w, j], recv_sem=recv_sib.at[w, j],
                    device_id=(x, y, 1 - c), device_id_type=MESH).wait_recv()
        for cp in sends + fwds:
            cp.wait_send()
        for cp in locals_:
            cp.wait()

    any_spec = pl.BlockSpec(memory_space=pl.ANY)
    return _pcall(
        body, name=name, out_shape=[jax.ShapeDtypeStruct((N_CHIPS,) + s.shape, s.dtype) for s in shards],
        in_specs=[any_spec] * nw, out_specs=[any_spec] * nw,
        scratch_shapes=[pltpu.SemaphoreType.DMA((nw, 3)), pltpu.SemaphoreType.DMA((nw, 3)),
                        pltpu.SemaphoreType.DMA((nw, 3)), pltpu.SemaphoreType.DMA((nw, 3)), pltpu.SemaphoreType.DMA((nw,))],
    )(*shards)


def _reduce_scatter(name, g):
    _, rs, cs = g.shape
    hr = rs // 2
    ch = _pick(hr, 64, 16)

    def body(g_ref, out_ref, recv, send_sems, recv_sems, sib_sems, local_sem):
        x, y, c = _place()
        me = 4 * x + 2 * y + c
        mine = pltpu.make_async_copy(g_ref.at[2 * x + y, pl.ds(pl.multiple_of(c * hr, 16), hr), :], recv.at[me], local_sem)
        mine.start()
        offs = [(dx, dy, dc) for dx in (0, 1) for dy in (0, 1) for dc in (0, 1)][1:]
        peers = []
        for dx, dy, dc in offs:
            peers.append((1 - x if dx else x, 1 - y if dy else y, 1 - c if dc else c))
        sends = []
        for k, (tx, ty, tc) in enumerate(peers):
            cp = pltpu.make_async_remote_copy(
                src_ref=g_ref.at[2 * tx + ty, pl.ds(pl.multiple_of(tc * hr, 16), hr), :], dst_ref=recv.at[me],
                send_sem=send_sems.at[k], recv_sem=recv_sems.at[k], device_id=(tx, ty, tc), device_id_type=MESH)
            cp.start()
            sends.append(cp)
        for k, (tx, ty, tc) in enumerate(peers):
            slot = recv.at[4 * tx + 2 * ty + tc]
            pltpu.make_async_remote_copy(src_ref=slot, dst_ref=slot, send_sem=send_sems.at[k], recv_sem=recv_sems.at[k],
                                         device_id=(tx, ty, tc), device_id_type=MESH).wait_recv()
        mine.wait()
        base = pl.multiple_of(c * hr, 16)
        for r0 in range(0, hr, ch):
            acc = recv[0, r0:r0 + ch, :].astype(F32)
            for k in range(1, N_DEV):
                acc = acc + recv[k, r0:r0 + ch, :].astype(F32)
            out_ref[pl.ds(base + r0, ch), :] = acc
        half = out_ref.at[pl.ds(base, hr), :]
        sib = pltpu.make_async_remote_copy(src_ref=half, dst_ref=half, send_sem=sib_sems.at[0], recv_sem=sib_sems.at[1],
                                           device_id=(x, y, 1 - c), device_id_type=MESH)
        sib.start()
        other = out_ref.at[pl.ds(pl.multiple_of((1 - c) * hr, 16), hr), :]
        pltpu.make_async_remote_copy(src_ref=other, dst_ref=other, send_sem=sib_sems.at[0], recv_sem=sib_sems.at[1],
                                     device_id=(x, y, 1 - c), device_id_type=MESH).wait_recv()
        sib.wait_send()
        for cp in sends:
            cp.wait_send()

    return _pcall(
        body, name=name, out_shape=jax.ShapeDtypeStruct((rs, cs), F32),
        in_specs=[pl.BlockSpec(memory_space=pl.ANY)], out_specs=pl.BlockSpec(memory_space=pltpu.VMEM),
        scratch_shapes=[pltpu.VMEM((N_DEV, hr, cs), BF16), pltpu.SemaphoreType.DMA((7,)), pltpu.SemaphoreType.DMA((7,)),
                        pltpu.SemaphoreType.DMA((2,)), pltpu.SemaphoreType.DMA],
        compiler_params=pltpu.CompilerParams(vmem_limit_bytes=VMEM_LIMIT))(g)


def _ffn_fwd(tag, x, g, sh, sc, gt, w_in, w_out):
    t, d = x.shape
    h = _norm_fwd(tag + "_norm", x, g, sc, sh)
    ab = _mm(tag + "_in", h, w_in, b_stacked=True, tn=1408)
    act = _swiglu_fwd(tag + "_act", ab)
    blk = (512, d)

    def epi(acc, res, scale):
        return res + (0.5 * scale) * acc, acc

    x_new, y = _mm(tag + "_out", act, w_out, tm=512, tn=d, epi=epi,
                   epi_ins=[(x, blk, lambda i, j: (i, j)), (gt, (1, d), lambda i, j: (0, j))],
                   epi_outs=[((t, d), F32, blk, lambda i, j: (i, j)), ((t, d), BF16, blk, lambda i, j: (i, j))])
    return x_new, (x, h, ab, act, y)


def _ffn_bwd(tag, dx_new, saved, g, sc, gt, w_in, w_out):
    x, h, ab, act, y = saved
    d = x.shape[1]
    f = act.shape[1]
    dy, dgt = _resgate_bwd(tag + "_dgate", dx_new, y, gt, 0.5)
    dact = _mm(tag + "_dact", dy, w_out, tb=True, tn=1408)
    dw_out = _mm(tag + "_dwout", act, dy, ta=True, tm=1408, tn=d, tk=512)
    dab = _swiglu_bwd(tag + "_dswiglu", dact, ab)
    dh = _mm(tag + "_dh", dab, w_in, tb=True, b_stacked=True, tn=d, tk=1408)
    dw_in = _mm(tag + "_dwin", h, dab, ta=True, out_stacked=True, tm=d, tn=1408, tk=512)
    dx, dsh, dsc, dg = _norm_bwd(tag + "_dnorm", dh, x, dx_new, g, sc)
    return dx, (dw_in, dw_out.reshape(N_CHIPS, f // N_CHIPS, d)), dsh, dsc, dgt, dg


def _row(v):
    return v.reshape(1, -1)


def _pack(parts):
    cols = []
    for p in parts:
        flat = p.reshape(-1).astype(F32)
        padn = (-flat.shape[0]) % 128
        cols.append(jnp.pad(flat, (0, padn)) if padn else flat)
    flat = jnp.concatenate(cols)
    padn = (-flat.shape[0]) % 1024
    if padn:
        flat = jnp.pad(flat, (0, padn))
    return flat.reshape(-1, 128)


def _unpack(packed, shapes):
    flat = packed.reshape(-1)
    out, off = [], 0
    for s in shapes:
        n = math.prod(s)
        out.append(flat[off:off + n].reshape(s))
        off += n + ((-n) % 128)
    return out


SMALL = ['b_ada', 'g_ffn1', 'g_mix', 'pool_w', 'pool_b', 'pool_scale', 'ssm_lam_re_log', 'ssm_lam_im', 'ssm_log_dt',
         'ssm_b_re', 'ssm_b_im', 'ssm_c_re', 'ssm_c_im', 'ssm_d', 'b_glu', 'g_ffn2', 'g_final']
BIG = ['w_ffn1_in', 'w_ffn1_out', 'w_in', 'w_pool_up', 'w_glu', 'w_ssm_up', 'w_out', 'w_ffn2_in', 'w_ffn2_out']
WEIGHTS = ['w_ada', 'b_ada', 'g_ffn1', 'w_ffn1_in', 'w_ffn1_out', 'g_mix', 'w_in', 'pool_w', 'pool_b', 'pool_scale', 'w_pool_up',
           'ssm_lam_re_log', 'ssm_lam_im', 'ssm_log_dt', 'ssm_b_re', 'ssm_b_im', 'ssm_c_re', 'ssm_c_im', 'ssm_d', 'w_glu', 'b_glu',
           'w_ssm_up', 'w_out', 'g_ffn2', 'w_ffn2_in', 'w_ffn2_out', 'g_final']


def kernel(x, c, w_ada, b_ada, g_ffn1, w_ffn1_in, w_ffn1_out, g_mix, w_in, pool_w, pool_b, pool_scale, w_pool_up, ssm_lam_re_log, ssm_lam_im, ssm_log_dt, ssm_b_re, ssm_b_im, ssm_c_re, ssm_c_im, ssm_d, w_glu, b_glu, w_ssm_up, w_out, g_ffn2, w_ffn2_in, w_ffn2_out, g_final, loss_target, m_w_ada, m_b_ada, m_g_ffn1, m_w_ffn1_in, m_w_ffn1_out, m_g_mix, m_w_in, m_pool_w, m_pool_b, m_pool_scale, m_w_pool_up, m_ssm_lam_re_log, m_ssm_lam_im, m_ssm_log_dt, m_ssm_b_re, m_ssm_b_im, m_ssm_c_re, m_ssm_c_im, m_ssm_d, m_w_glu, m_b_glu, m_w_ssm_up, m_w_out, m_g_ffn2, m_w_ffn2_in, m_w_ffn2_out, m_g_final, v_w_ada, v_b_ada, v_g_ffn1, v_w_ffn1_in, v_w_ffn1_out, v_g_mix, v_w_in, v_pool_w, v_pool_b, v_pool_scale, v_w_pool_up, v_ssm_lam_re_log, v_ssm_lam_im, v_ssm_log_dt, v_ssm_b_re, v_ssm_b_im, v_ssm_c_re, v_ssm_c_im, v_ssm_d, v_w_glu, v_b_glu, v_w_ssm_up, v_w_out, v_g_ffn2, v_w_ffn2_in, v_w_ffn2_out, v_g_final):
    args = dict(locals())
    wt = {n: args[n] for n in WEIGHTS}
    mom = {n: args["m_" + n] for n in WEIGHTS}
    var = {n: args["v_" + n] for n in WEIGHTS}

    t, d = x.shape[1], x.shape[2]
    pw = pool_b.shape[1]
    sw = ssm_d.shape[1]
    ngrp = sw // SSM_GROUP
    gn = ngrp * SSM_STATE
    xi, yi, ci = _place()
    b_me = 4 * xi + 2 * yi + ci
    s_me = 2 * xi + yi
    x2d = x[0]
    tgt = loss_target[0]

    c_all = _allgather_small("ag_c", c.reshape(8, d // 8)).reshape(N_DEV, d)
    ncol = w_ada.shape[2]
    b_sh = lax.dynamic_slice(b_ada, (0, s_me * ncol), (1, ncol))
    mod_sh = _ada_fwd("ada_fwd", c_all, w_ada[0], b_sh)
    mod_all = _allgather_small("ag_mod", mod_sh)
    mod_me = jnp.concatenate([lax.dynamic_slice(mod_all, (16 * s + b_me, 0), (1, ncol)) for s in range(N_CHIPS)], axis=1)
    mod = mod_me.reshape(9, d)
    sh1, sc1, gt1, sh2, sc2, gt2, sh3, sc3, gt3 = [mod[k:k + 1] for k in range(9)]

    shards = [wt[n][0].astype(BF16) for n in BIG]
    full = dict(zip(BIG, _allgather_weights("ag_weights", shards)))
    f = w_ffn1_out.shape[1] * N_CHIPS
    w1_out = full['w_ffn1_out'].reshape(f, d)
    w2_out = full['w_ffn2_out'].reshape(f, d)
    wo = full['w_out'].reshape(d, d)

    x1, sav1 = _ffn_fwd("ffn1", x2d, g_ffn1, sh1, sc1, gt1, full['w_ffn1_in'], w1_out)

    h2 = _norm_fwd("mix_norm", x1, g_mix, sc2, sh2)
    uin = _mm("mix_in", h2, full['w_in'], b_stacked=True, tn=768, out_dtype=F32)
    p_pool, z_pool = _pool_fwd("pool_fwd", uin, pool_w[0], pool_b, pool_scale, pw)
    y_pool = _mm("pool_up", p_pool, full['w_pool_up'], b_stacked=True, tn=256)

    col = lambda a: a.reshape(gn, 1)
    lrl_c, li_c = col(ssm_lam_re_log), col(ssm_lam_im)
    ldt_c = col(jnp.broadcast_to(ssm_log_dt.reshape(ngrp, 1), (ngrp, SSM_STATE)))
    b_re2, b_im2 = ssm_b_re.reshape(gn, SSM_GROUP), ssm_b_im.reshape(gn, SSM_GROUP)
    lrdt_c, ang_c, bb_re, bb_im = _ssm_prep("ssm_prep", lrl_c, li_c, ldt_c, b_re2, b_im2)
    lrdt, ang = lrdt_c.reshape(1, gn), ang_c.reshape(1, gn)
    bbd_re, bbd_im = _blockdiag_b(bb_re, sw), _blockdiag_b(bb_im, sw)
    ccd_re, ccd_im = _blockdiag_c(ssm_c_re[0], sw), _blockdiag_c(ssm_c_im[0], sw)
    y_s, ge, s_re, s_im = _ssm_fwd("ssm_fwd", uin, lrdt, ang, bbd_re.astype(BF16), bbd_im.astype(BF16),
                                   ccd_re.astype(BF16), ccd_im.astype(BF16), ssm_d, sw)
    gv = _mm("glu_in", ge, full['w_glu'], b_stacked=True, tn=256)
    sg = _glu_fwd("glu_act", gv, b_glu)
    y_ssm = _mm("ssm_up", sg, full['w_ssm_up'], b_stacked=True, tn=256)
    merged = _gates_fwd("gates", y_pool, y_ssm, uin, d, pw)
    blk = (512, d)

    def epi2(acc, res, scale):
        return res + scale * acc, acc

    x2, y2 = _mm("mix_out", merged, wo, tm=512, tn=d, epi=epi2,
                 epi_ins=[(x1, blk, lambda i, j: (i, j)), (gt2, (1, d), lambda i, j: (0, j))],
                 epi_outs=[((t, d), F32, blk, lambda i, j: (i, j)), ((t, d), BF16, blk, lambda i, j: (i, j))])

    x3, sav3 = _ffn_fwd("ffn2", x2, g_ffn2, sh3, sc3, gt3, full['w_ffn2_in'], w2_out)

    dx3, dg_final, loss_v = _loss_bwd("loss", x3, tgt, _row(g_final))
    loss = lax.psum(loss_v[0, 0], ("x", "y", "c"))

    gb = {}
    gs = {}
    dx2, (gb['w_ffn2_in'], gb['w_ffn2_out']), dsh3, dsc3, dgt3, gs['g_ffn2'] = _ffn_bwd(
        "ffn2b", dx3, sav3, g_ffn2, sc3, gt3, full['w_ffn2_in'], w2_out)

    dy2, dgt2 = _resgate_bwd("mix_dgate", dx2, y2, gt2, 1.0)
    dmerged = _mm("mix_dmerged", dy2, wo, tb=True, tn=d)
    gb['w_out'] = _mm("mix_dwout", merged, dy2, ta=True, tm=d, tn=d, tk=512).reshape(N_CHIPS, d // N_CHIPS, d)
    dy_pool, dy_ssm, dgl = _gates_bwd("gates_bwd", dmerged, y_pool, y_ssm, uin, d, pw)

    dsg = _mm("ssm_dup", dy_ssm, full['w_ssm_up'], tb=True, b_stacked=True, tn=sw, tk=256)
    gb['w_ssm_up'] = _mm("ssm_dwup", sg, dy_ssm, ta=True, out_stacked=True, tm=sw, tn=256, tk=512)
    dgv, gs['b_glu'] = _glu_bwd("glu_bwd", dsg, gv, b_glu)
    dge = _mm("glu_dge", dgv, full['w_glu'], tb=True, b_stacked=True, tn=sw, tk=256)
    gb['w_glu'] = _mm("glu_dw", ge, dgv, ta=True, out_stacked=True, tm=sw, tn=256, tk=512)
    tr = lambda a: jnp.swapaxes(a, 1, 2).astype(BF16)
    (du_ssm, g_abre, g_abim, g_bbd_re, g_bbd_im, g_ccd_re, g_ccd_im, gs['ssm_d']) = _ssm_bwd(
        "ssm_bwd", dge, y_s, uin, s_re, s_im, lrdt, ang, tr(bbd_re), tr(bbd_im), tr(ccd_re), tr(ccd_im), ssm_d, sw)
    gs['ssm_c_re'], gs['ssm_c_im'] = _diag_of_c(g_ccd_re, sw), _diag_of_c(g_ccd_im, sw)
    d_lrl, d_li, d_ldt, d_bre, d_bim = _ssm_param_bwd(
        "ssm_param_bwd", lrl_c, li_c, ldt_c, b_re2, b_im2, g_abre.reshape(gn, 1), g_abim.reshape(gn, 1),
        _diag_of_b(g_bbd_re, sw), _diag_of_b(g_bbd_im, sw))
    gs['ssm_lam_re_log'], gs['ssm_lam_im'] = d_lrl, d_li
    gs['ssm_log_dt'] = jnp.sum(d_ldt.reshape(ngrp, SSM_STATE), axis=1)
    gs['ssm_b_re'], gs['ssm_b_im'] = d_bre, d_bim

    dp = _mm("pool_dup", dy_pool, full['w_pool_up'], tb=True, b_stacked=True, tn=pw, tk=256)
    gb['w_pool_up'] = _mm("pool_dwup", p_pool, dy_pool, ta=True, out_stacked=True, tm=pw, tn=256, tk=512)
    du_pool, gs['pool_w'], gs['pool_b'], gs['pool_scale'] = _pool_bwd("pool_bwd", dp, z_pool, pool_w[0], pool_b, pool_scale)

    duin = jnp.concatenate([du_pool, du_ssm, dgl], axis=1)
    dh2 = _mm("mix_dh", duin, full['w_in'], tb=True, b_stacked=True, tn=d, tk=768)
    gb['w_in'] = _mm("mix_dwin", h2, duin, ta=True, out_stacked=True, tm=d, tn=768, tk=512)
    dx1, dsh2, dsc2, gs['g_mix'] = _norm_bwd("mix_dnorm", dh2, x1, dx2, g_mix, sc2)

    dx0, (gb['w_ffn1_in'], gb['w_ffn1_out']), dsh1, dsc1, dgt1, gs['g_ffn1'] = _ffn_bwd(
        "ffn1b", dx1, sav1, g_ffn1, sc1, gt1, full['w_ffn1_in'], w1_out)
    gs['g_final'] = dg_final

    dmod = jnp.concatenate([dsh1, dsc1, dgt1, dsh2, dsc2, dgt2, dsh3, dsc3, dgt3], axis=1)
    gs['b_ada'] = dmod
    dmod_all = _allgather_small("ag_dmod", dmod.reshape(-1, 128)).reshape(N_DEV, 9 * d)
    dmod_sh = lax.dynamic_slice(dmod_all, (0, s_me * ncol), (N_DEV, ncol))
    g_w_ada = _ada_bwd("ada_bwd", c_all, dmod_sh)

    grads, delta, new_m, new_v = {}, {}, {}, {}
    small_shapes = [wt[n].shape for n in SMALL]
    g8 = _allgather_small("ag_small_grads", _pack([gs[n] for n in SMALL]))
    res = _adamw_small("adamw_small", _pack([wt[n] for n in SMALL]), g8, _pack([mom[n] for n in SMALL]), _pack([var[n] for n in SMALL]))
    for dst, packed in zip((grads, delta, new_m, new_v), res):
        for n, val in zip(SMALL, _unpack(packed, small_shapes)):
            dst[n] = val

    for n in BIG:
        g_sum = _reduce_scatter("rs_" + n, gb[n])
        dl, mn, vn = _adamw("adamw_" + n, wt[n][0], g_sum, mom[n][0], var[n][0])
        grads[n], delta[n], new_m[n], new_v[n] = g_sum[None], dl[None], mn[None], vn[None]
    dl, mn, vn = _adamw("adamw_w_ada", w_ada[0], g_w_ada, m_w_ada[0], v_w_ada[0])
    grads['w_ada'], delta['w_ada'], new_m['w_ada'], new_v['w_ada'] = g_w_ada[None], dl[None], mn[None], vn[None]

    return (loss, dx0[None], *[grads[n] for n in WEIGHTS], *[delta[n] for n in WEIGHTS],
            *[new_m[n] for n in WEIGHTS], *[new_v[n] for n in WEIGHTS])
```

```python
import functools
import math

import jax
import jax.numpy as jnp
from jax import lax
from jax.experimental import pallas as pl
from jax.experimental.pallas import tpu as pltpu

F32 = jnp.float32
BF16 = jnp.bfloat16
MESH = pl.DeviceIdType.MESH

EPS = 1e-6
POOL_WINDOWS = (2, 4, 8, 16)
POOL_HALO = 16
SSM_GROUP = 16
SSM_STATE = 64
SSM_BLOCKS = 4
N_DEV = 8
N_CHIPS = 4
ADAM_LR = 0.001
ADAM_B1 = 0.9
ADAM_B2 = 0.999
ADAM_EPS = 1e-08
ADAM_WD = 0.01
ADAM_STEP = 10
VMEM_LIMIT = 56 * 1024 * 1024


ANY_SPEC = pl.BlockSpec(memory_space=pl.ANY)
HBM_SPEC = pl.BlockSpec(memory_space=pltpu.HBM)
SEM_SPEC = pl.BlockSpec(memory_space=pltpu.SEMAPHORE)
EFFECT = pltpu.SideEffectType.DATAFLOW_SIDE_EFFECTING


def _hbm(a):
    return pltpu.with_memory_space_constraint(a, pltpu.HBM)


def _pcall(body, **kw):
    return pl.pallas_call(body, **kw)


def _params(*sem):
    return pltpu.CompilerParams(dimension_semantics=sem, vmem_limit_bytes=VMEM_LIMIT)


def _pick(n, cap, mult=128):
    if n <= cap:
        return n
    best = None
    for d in range(mult, cap + 1, mult):
        if n % d == 0:
            best = d
    assert best is not None, (n, cap, mult)
    return best


def _sigmoid(v):
    return 1.0 / (1.0 + jnp.exp(-v))


def _rowwise(name, fn, ins, params, outs, reds, tm, deps=()):
    t = ins[0][0].shape[0]
    tm = min(tm, t)
    nb = t // tm
    ni, npar, no, nd = len(ins), len(params), len(outs), len(deps)

    def body(*refs):
        iv = [r[...] for r in refs[:ni]]
        pv = [r[...] for r in refs[ni:ni + npar]]
        o_refs = refs[ni + npar + nd:ni + npar + nd + no]
        r_refs = refs[ni + npar + nd + no:]
        ovals, rvals = fn(iv, pv)
        for o_ref, val in zip(o_refs, ovals):
            parts = val if isinstance(val, (list, tuple)) else [val]
            off = 0
            for p in parts:
                o_ref[:, off:off + p.shape[1]] = p.astype(o_ref.dtype)
                off += p.shape[1]
        if r_refs:
            @pl.when(pl.program_id(0) == 0)
            def _():
                for r in r_refs:
                    r[...] = jnp.zeros_like(r)
            for r, val in zip(r_refs, rvals):
                r[...] += val

    in_specs = [pl.BlockSpec((tm, w), functools.partial(lambda i, cb: (i, cb), cb=cb)) for (_, w, cb) in ins]
    in_specs += [pl.BlockSpec(p.shape, lambda i: (0, 0)) for p in params]
    in_specs += [ANY_SPEC] * nd
    out_shape = [jax.ShapeDtypeStruct((t, w), dt) for (w, dt) in outs]
    out_shape += [jax.ShapeDtypeStruct((1, w), F32) for w in reds]
    out_specs = [pl.BlockSpec((tm, w), lambda i: (i, 0)) for (w, _) in outs]
    out_specs += [pl.BlockSpec((1, w), lambda i: (0, 0)) for w in reds]
    res = _pcall(body, name=name, grid=(nb,), in_specs=in_specs, out_specs=out_specs, out_shape=out_shape,
                 compiler_params=_params("arbitrary"))(*[a for a, _, _ in ins], *params, *deps)
    return res


def _colsum(v):
    return jnp.sum(v, axis=0, keepdims=True)


def _mm(name, a, b, *, ta=False, tb=False, b_stacked=False, out_stacked=False, tm=512, tn=1024, tk=2816,
        out_dtype=BF16, epi=None, epi_ins=(), epi_outs=None, deps=()):
    if ta:
        kdim, m = a.shape
    else:
        m, kdim = a.shape
    if b_stacked:
        ns = b.shape[2]
        n = b.shape[1] if tb else N_CHIPS * ns
        assert kdim == (N_CHIPS * ns if tb else b.shape[1]), (name, a.shape, b.shape)
    else:
        n = b.shape[0] if tb else b.shape[1]
        assert kdim == (b.shape[1] if tb else b.shape[0]), (name, a.shape, b.shape)
    tm = _pick(m, tm, 8)
    if b_stacked and not tb:
        tn = _pick(ns, tn)
    elif out_stacked:
        ns_out = n // N_CHIPS
        tn = _pick(ns_out, tn)
    else:
        tn = _pick(n, tn)
    if b_stacked and tb:
        tk = _pick(ns, tk)
    else:
        tk = _pick(kdim, tk, 8 if ta else 128)
    nm, nn, nk = m // tm, n // tn, kdim // tk

    a_spec = pl.BlockSpec((tk, tm), lambda i, j, k: (k, i)) if ta else pl.BlockSpec((tm, tk), lambda i, j, k: (i, k))
    if b_stacked and not tb:
        bps = ns // tn
        b_spec = pl.BlockSpec((None, tk, tn), lambda i, j, k: (j // bps, k, j % bps))
    elif b_stacked and tb:
        bps = ns // tk
        b_spec = pl.BlockSpec((None, tn, tk), lambda i, j, k: (k // bps, j, k % bps))
    elif tb:
        b_spec = pl.BlockSpec((tn, tk), lambda i, j, k: (j, k))
    else:
        b_spec = pl.BlockSpec((tk, tn), lambda i, j, k: (k, j))
    dims = (((0 if ta else 1,), (1 if tb else 0,)), ((), ()))

    if epi_outs is None:
        if out_stacked:
            bps_o = (n // N_CHIPS) // tn
            epi_outs = [((N_CHIPS, m, n // N_CHIPS), out_dtype, (None, tm, tn), lambda i, j: (j // bps_o, i, j % bps_o))]
        else:
            epi_outs = [((m, n), out_dtype, (tm, tn), lambda i, j: (i, j))]
    ne, no, nd = len(epi_ins), len(epi_outs), len(deps)

    def body(a_ref, b_ref, *rest):
        e_refs = rest[:ne]
        o_refs = rest[ne + nd:ne + nd + no]
        scratch = rest[ne + nd + no:]
        p = lax.dot_general(a_ref[...].astype(BF16), b_ref[...].astype(BF16), dims, preferred_element_type=F32)

        def finish(acc):
            vals = (acc,) if epi is None else epi(acc, *[r[...] for r in e_refs])
            for o_ref, val in zip(o_refs, vals):
                o_ref[...] = val.astype(o_ref.dtype)

        if nk == 1:
            finish(p)
        else:
            acc_ref = scratch[0]
            k = pl.program_id(2)

            @pl.when(k == 0)
            def _():
                acc_ref[...] = p

            @pl.when(k > 0)
            def _():
                acc_ref[...] += p

            @pl.when(k == nk - 1)
            def _():
                finish(acc_ref[...])

    def _ij(f):
        return lambda i, j, k: f(i, j)

    in_specs = [a_spec, b_spec] + [pl.BlockSpec(blk, _ij(f)) for (_, blk, f) in epi_ins] + [ANY_SPEC] * nd
    out_specs = [pl.BlockSpec(blk, _ij(f)) for (_, _, blk, f) in epi_outs]
    out_shape = [jax.ShapeDtypeStruct(s, dt) for (s, dt, _, _) in epi_outs]
    scratch = [pltpu.VMEM((tm, tn), F32)] if nk > 1 else []
    res = _pcall(body, name=name, grid=(nm, nn, nk), in_specs=in_specs, out_specs=out_specs, out_shape=out_shape,
                 scratch_shapes=scratch, compiler_params=_params("parallel", "parallel", "arbitrary"))(
                     a, b, *[x for x, _, _ in epi_ins], *deps)
    return res[0] if len(res) == 1 else res


def _norm_fwd(name, x, g, sc, sh):
    d = x.shape[1]

    def fn(iv, pv):
        (xv,), (gv, scv, shv) = iv, pv
        r = lax.rsqrt(jnp.mean(xv * xv, axis=-1, keepdims=True) + EPS)
        return [xv * r * gv * (1.0 + scv) + shv], []

    return _rowwise(name, fn, [(x, d, 0)], [g, sc, sh], [(d, BF16)], [], 512)[0]


def _norm_bwd(name, dh, x, dres, g, sc):
    d = x.shape[1]

    def fn(iv, pv):
        (dhv, xv, drv), (gv, scv) = iv, pv
        dhv = dhv.astype(F32)
        r = lax.rsqrt(jnp.mean(xv * xv, axis=-1, keepdims=True) + EPS)
        xr = xv * r
        dn = dhv * (1.0 + scv)
        gd = gv * dn
        dx = drv + r * (gd - xr * jnp.mean(gd * xr, axis=-1, keepdims=True))
        return [dx], [_colsum(dhv), _colsum(dhv * xr * gv), _colsum(dn * xr)]

    return _rowwise(name, fn, [(dh, d, 0), (x, d, 0), (dres, d, 0)], [g, sc], [(d, F32)], [d, d, d], 256)


def _loss_bwd(name, x, tgt, g):
    d = x.shape[1]

    def fn(iv, pv):
        (xv, tv), (gv,) = iv, pv
        r = lax.rsqrt(jnp.mean(xv * xv, axis=-1, keepdims=True) + EPS)
        xr = xv * r
        e = xr * gv - tv
        loss_row = 0.5 * jnp.mean(e * e, axis=-1, keepdims=True)
        dout = e * (1.0 / d)
        gd = gv * dout
        dx = r * (gd - xr * jnp.mean(gd * xr, axis=-1, keepdims=True))
        return [dx], [_colsum(dout * xr), _colsum(loss_row * jnp.ones((1, 128), F32))]

    return _rowwise(name, fn, [(x, d, 0), (tgt, d, 0)], [g], [(d, F32)], [d, 128], 256)


def _resgate_bwd(name, dx, y, gt, factor):
    d = dx.shape[1]

    def fn(iv, pv):
        (dxv, yv), (gtv,) = iv, pv
        return [factor * gtv * dxv], [_colsum(factor * dxv * yv.astype(F32))]

    return _rowwise(name, fn, [(dx, d, 0), (y, d, 0)], [gt], [(d, BF16)], [d], 512)


def _swiglu_fwd(name, ab):
    f = ab.shape[1] // 2

    def fn(iv, pv):
        a, b = iv[0].astype(F32), iv[1].astype(F32)
        return [a * _sigmoid(a) * b], []

    return _rowwise(name, fn, [(ab, f, 0), (ab, f, 1)], [], [(f, BF16)], [], 256)[0]


def _swiglu_bwd(name, dact, ab, deps=()):
    f = ab.shape[1] // 2

    def fn(iv, pv):
        dv, a, b = iv[0].astype(F32), iv[1].astype(F32), iv[2].astype(F32)
        s = _sigmoid(a)
        return [[dv * b * (s * (1.0 + a * (1.0 - s))), dv * (a * s)]], []

    return _rowwise(name, fn, [(dact, f, 0), (ab, f, 0), (ab, f, 1)], [], [(2 * f, BF16)], [], 256, deps=deps)[0]


def _gates_fwd(name, y_pool, y_ssm, uin, d, pw):
    cb = (2 * pw) // d

    def fn(iv, pv):
        yp, ys, glp, gls = [v.astype(F32) for v in iv]
        return [_sigmoid(glp) * yp + _sigmoid(gls) * ys], []

    return _rowwise(name, fn, [(y_pool, d, 0), (y_ssm, d, 0), (uin, d, cb), (uin, d, cb + 1)], [], [(d, BF16)], [], 256)[0]


def _gates_bwd(name, dm, y_pool, y_ssm, uin, d, pw):
    cb = (2 * pw) // d

    def fn(iv, pv):
        dmv, yp, ys, glp, gls = [v.astype(F32) for v in iv]
        sp, ss = _sigmoid(glp), _sigmoid(gls)
        return [dmv * sp, dmv * ss, [dmv * yp * sp * (1.0 - sp), dmv * ys * ss * (1.0 - ss)]], []

    return _rowwise(name, fn, [(dm, d, 0), (y_pool, d, 0), (y_ssm, d, 0), (uin, d, cb), (uin, d, cb + 1)], [],
                    [(d, BF16), (d, BF16), (2 * d, BF16)], [], 256)


def _glu_fwd(name, gv, b_glu):
    w = gv.shape[1] // 2

    def fn(iv, pv):
        gvv, (bv,) = iv[0].astype(F32) + pv[0], pv
        return [gvv[:, :w] * _sigmoid(gvv[:, w:])], []

    return _rowwise(name, fn, [(gv, 2 * w, 0)], [b_glu], [(w, BF16)], [], 512)[0]


def _glu_bwd(name, dsg, gv, b_glu, deps=()):
    w = gv.shape[1] // 2

    def fn(iv, pv):
        dv = iv[0].astype(F32)
        gvv = iv[1].astype(F32) + pv[0]
        val, s = gvv[:, :w], _sigmoid(gvv[:, w:])
        dval = dv * s
        dgate = dv * val * s * (1.0 - s)
        return [[dval, dgate]], [jnp.concatenate([_colsum(dval), _colsum(dgate)], axis=1)]

    return _rowwise(name, fn, [(dsg, w, 0), (gv, 2 * w, 0)], [b_glu], [(2 * w, BF16)], [2 * w], 512, deps=deps)


def _adamw(name, w, g, m, v, tm=256):
    c = w.shape[1]

    def fn(iv, pv):
        wv, gv, mv, vv = iv
        mn = ADAM_B1 * mv + (1.0 - ADAM_B1) * gv
        vn = ADAM_B2 * vv + (1.0 - ADAM_B2) * (gv * gv)
        m_hat = mn / (1.0 - ADAM_B1 ** ADAM_STEP)
        v_hat = vn / (1.0 - ADAM_B2 ** ADAM_STEP)
        delta = -ADAM_LR * (m_hat / (jnp.sqrt(v_hat) + ADAM_EPS) + ADAM_WD * wv)
        return [delta, mn, vn], []

    return _rowwise(name, fn, [(w, c, 0), (g, c, 0), (m, c, 0), (v, c, 0)], [], [(c, F32)] * 3, [], _pick(w.shape[0], tm, 8))


def _adamw_small(name, w, g8, m, v):
    r = w.shape[0]

    def body(w_ref, g_ref, m_ref, v_ref, go_ref, d_ref, mo_ref, vo_ref):
        gv = g_ref[0:r, :]
        for k in range(1, N_DEV):
            gv = gv + g_ref[k * r:(k + 1) * r, :]
        mn = ADAM_B1 * m_ref[...] + (1.0 - ADAM_B1) * gv
        vn = ADAM_B2 * v_ref[...] + (1.0 - ADAM_B2) * (gv * gv)
        m_hat = mn / (1.0 - ADAM_B1 ** ADAM_STEP)
        v_hat = vn / (1.0 - ADAM_B2 ** ADAM_STEP)
        go_ref[...] = gv
        d_ref[...] = -ADAM_LR * (m_hat / (jnp.sqrt(v_hat) + ADAM_EPS) + ADAM_WD * w_ref[...])
        mo_ref[...] = mn
        vo_ref[...] = vn

    return _pcall(body, name=name, out_shape=[jax.ShapeDtypeStruct((r, 128), F32)] * 4,
                  compiler_params=pltpu.CompilerParams(vmem_limit_bytes=VMEM_LIMIT))(w, g8, m, v)


def _pool_fwd(name, uin, pool_w, pool_b, pool_scale, pw, tm=512):
    t = uin.shape[0]
    tm = min(tm, t)
    ng = len(POOL_WINDOWS)
    gw = pw // ng

    def body(u_ref, w_ref, b_ref, s_ref, p_ref, z_ref, ext):
        i = pl.program_id(0)

        @pl.when(i == 0)
        def _():
            ext[0:POOL_HALO, :] = jnp.zeros((POOL_HALO, pw), F32)

        u = u_ref[...]
        ext[POOL_HALO:POOL_HALO + tm, :] = u
        pos = i * tm + lax.broadcasted_iota(jnp.int32, (tm, 1), 0)
        for k, win in enumerate(POOL_WINDOWS):
            cols = slice(k * gw, (k + 1) * gw)
            acc = u[:, cols]
            for j in range(1, win):
                acc = acc + ext[POOL_HALO - j:POOL_HALO - j + tm, cols]
            cnt = jnp.minimum(pos + 1, win).astype(F32)
            z = acc / cnt - u[:, cols]
            zp = jnp.dot(z.astype(BF16), w_ref[k].astype(BF16), preferred_element_type=F32) + b_ref[:, cols]
            p_ref[:, cols] = (zp * s_ref[:, cols]).astype(BF16)
            z_ref[:, cols] = z.astype(BF16)
        ext[0:POOL_HALO, :] = u[tm - POOL_HALO:tm, :]

    return _pcall(
        body, name=name, grid=(t // tm,),
        in_specs=[pl.BlockSpec((tm, pw), lambda i: (i, 0)), pl.BlockSpec(pool_w.shape, lambda i: (0, 0, 0)),
                  pl.BlockSpec(pool_b.shape, lambda i: (0, 0)), pl.BlockSpec(pool_scale.shape, lambda i: (0, 0))],
        out_specs=[pl.BlockSpec((tm, pw), lambda i: (i, 0))] * 2,
        out_shape=[jax.ShapeDtypeStruct((t, pw), BF16)] * 2,
        scratch_shapes=[pltpu.VMEM((POOL_HALO + tm, pw), F32)],
        compiler_params=_params("arbitrary"))(uin, pool_w, pool_b, pool_scale)


def _pool_bwd(name, dp, z, pool_w, pool_b, pool_scale, tm=512):
    t, pw = z.shape
    tm = min(tm, t)
    nb = t // tm
    ng = len(POOL_WINDOWS)
    gw = pw // ng

    def body(dp_ref, z_ref, w_ref, b_ref, s_ref, du_ref, dw_ref, db_ref, ds_ref, ext):
        i = pl.program_id(0)

        @pl.when(i == 0)
        def _():
            ext[tm:tm + POOL_HALO, :] = jnp.zeros((POOL_HALO, pw), F32)
            dw_ref[...] = jnp.zeros_like(dw_ref)
            db_ref[...] = jnp.zeros_like(db_ref)
            ds_ref[...] = jnp.zeros_like(ds_ref)

        pos = (nb - 1 - i) * tm + lax.broadcasted_iota(jnp.int32, (tm, 1), 0)
        for k, win in enumerate(POOL_WINDOWS):
            cols = slice(k * gw, (k + 1) * gw)
            zk = z_ref[:, cols]
            dpk = dp_ref[:, cols].astype(F32)
            wk = w_ref[k].astype(BF16)
            zp = jnp.dot(zk, wk, preferred_element_type=F32) + b_ref[:, cols]
            ds_ref[:, cols] += _colsum(dpk * zp)
            dzp = dpk * s_ref[:, cols]
            db_ref[:, cols] += _colsum(dzp)
            dzpb = dzp.astype(BF16)
            dz = lax.dot_general(dzpb, wk, (((1,), (1,)), ((), ())), preferred_element_type=F32)
            dw_ref[k] += lax.dot_general(zk, dzpb, (((0,), (0,)), ((), ())), preferred_element_type=F32)
            cnt = jnp.minimum(pos + 1, win).astype(F32)
            r = dz / cnt
            ext[0:tm, cols] = r
            acc = r - dz
            for j in range(1, win):
                acc = acc + ext[j:j + tm, cols]
            du_ref[:, cols] = acc.astype(BF16)
        ext[tm:tm + POOL_HALO, :] = ext[0:POOL_HALO, :]

    rev = lambda i: (nb - 1 - i, 0)
    return _pcall(
        body, name=name, grid=(nb,),
        in_specs=[pl.BlockSpec((tm, pw), rev), pl.BlockSpec((tm, pw), rev), pl.BlockSpec(pool_w.shape, lambda i: (0, 0, 0)),
                  pl.BlockSpec(pool_b.shape, lambda i: (0, 0)), pl.BlockSpec(pool_scale.shape, lambda i: (0, 0))],
        out_specs=[pl.BlockSpec((tm, pw), rev), pl.BlockSpec(pool_w.shape, lambda i: (0, 0, 0)),
                   pl.BlockSpec((1, pw), lambda i: (0, 0)), pl.BlockSpec((1, pw), lambda i: (0, 0))],
        out_shape=[jax.ShapeDtypeStruct((t, pw), BF16), jax.ShapeDtypeStruct(pool_w.shape, F32),
                   jax.ShapeDtypeStruct((1, pw), F32), jax.ShapeDtypeStruct((1, pw), F32)],
        scratch_shapes=[pltpu.VMEM((tm + POOL_HALO, pw), F32)],
        compiler_params=_params("arbitrary"))(dp, z, pool_w, pool_b, pool_scale)


def _ssm_disc(lrl, li, ldt):
    lr = -jnp.exp(lrl)
    dt = jnp.exp(ldt)
    mag = jnp.exp(lr * dt)
    ang = li * dt
    ab_re = mag * jnp.cos(ang)
    ab_im = mag * jnp.sin(ang)
    num_re = ab_re - 1.0
    num_im = ab_im
    den = lr * lr + li * li
    f_re = (num_re * lr + num_im * li) / den
    f_im = (num_im * lr - num_re * li) / den
    return lr, dt, mag, ang, ab_re, ab_im, num_re, num_im, den, f_re, f_im


def _ssm_prep(name, lrl, li, ldt, b_re, b_im):
    gn, h = b_re.shape

    def body(lrl_ref, li_ref, ldt_ref, br_ref, bi_ref, lrdt_ref, ang_ref, bbr_ref, bbi_ref):
        lr, dt, _, ang, _, _, _, _, _, f_re, f_im = _ssm_disc(lrl_ref[...], li_ref[...], ldt_ref[...])
        lrdt_ref[...] = lr * dt
        ang_ref[...] = ang
        br, bi = br_ref[...], bi_ref[...]
        bbr_ref[...] = f_re * br - f_im * bi
        bbi_ref[...] = f_re * bi + f_im * br

    col = jax.ShapeDtypeStruct((gn, 1), F32)
    mat = jax.ShapeDtypeStruct((gn, h), F32)
    return _pcall(body, name=name, out_shape=[col, col, mat, mat])(lrl, li, ldt, b_re, b_im)


def _ssm_param_bwd(name, lrl, li, ldt, b_re, b_im, g_abre, g_abim, g_bbre, g_bbim):
    gn, h = b_re.shape

    def body(lrl_ref, li_ref, ldt_ref, br_ref, bi_ref, gar_ref, gai_ref, gbr_ref, gbi_ref,
             dlrl_ref, dli_ref, dldt_ref, dbr_ref, dbi_ref):
        li_v = li_ref[...]
        lr, dt, mag, ang, ab_re, ab_im, num_re, num_im, den, f_re, f_im = _ssm_disc(lrl_ref[...], li_v, ldt_ref[...])
        br, bi = br_ref[...], bi_ref[...]
        gbr, gbi = gbr_ref[...], gbi_ref[...]
        g_fre = jnp.sum(gbr * br + gbi * bi, axis=1, keepdims=True)
        g_fim = jnp.sum(gbi * br - gbr * bi, axis=1, keepdims=True)
        dbr_ref[...] = gbr * f_re + gbi * f_im
        dbi_ref[...] = gbi * f_re - gbr * f_im
        g_num_re = (g_fre * lr - g_fim * li_v) / den
        g_num_im = (g_fre * li_v + g_fim * lr) / den
        g_den = -(g_fre * f_re + g_fim * f_im) / den
        g_lr = (g_fre * num_re + g_fim * num_im) / den + g_den * 2.0 * lr
        g_li = (g_fre * num_im - g_fim * num_re) / den + g_den * 2.0 * li_v
        g_are = gar_ref[...] + g_num_re
        g_aim = gai_ref[...] + g_num_im
        g_mag = g_are * jnp.cos(ang) + g_aim * jnp.sin(ang)
        g_ang = g_aim * ab_re - g_are * ab_im
        g_lrdt = g_mag * mag
        g_lr = g_lr + g_lrdt * dt
        g_dt = g_lrdt * lr + g_ang * li_v
        g_li = g_li + g_ang * dt
        dlrl_ref[...] = g_lr * lr
        dli_ref[...] = g_li
        dldt_ref[...] = g_dt * dt

    col = jax.ShapeDtypeStruct((gn, 1), F32)
    mat = jax.ShapeDtypeStruct((gn, h), F32)
    return _pcall(body, name=name, out_shape=[col, col, col, mat, mat])(lrl, li, ldt, b_re, b_im, g_abre, g_abim, g_bbre, g_bbim)


def _pow_rows(lrdt, ang, k):
    mag = jnp.exp(k * lrdt)
    return mag * jnp.cos(k * ang), mag * jnp.sin(k * ang)


def _ssm_chunk(t):
    return 256 if t >= 2048 else 128


def _ssm_fwd(name, uin, lrdt, ang, bb_re, bb_im, cc_re, cc_im, d_skip, sw):
    t = uin.shape[0]
    gn = lrdt.shape[1]
    lc = _ssm_chunk(t)
    nsteps = int(math.log2(lc))
    pad = lc // 2
    ub, sb = sw // SSM_BLOCKS, gn // SSM_BLOCKS

    def body(u_ref, lrdt_ref, ang_ref, bbr_ref, bbi_ref, ccr_ref, cci_ref, d_ref, y_ref, ge_ref, sre_ref, sim_ref,
             p_re, p_im, a_re, a_im, b_re, b_im, car_re, car_im):
        i = pl.program_id(0)
        lrdt_v, ang_v = lrdt_ref[...], ang_ref[...]

        @pl.when(i == 0)
        def _():
            k = (lax.broadcasted_iota(jnp.int32, (lc, 1), 0) + 1).astype(F32)
            pr, pi = _pow_rows(lrdt_v, ang_v, k)
            p_re[...] = pr
            p_im[...] = pi
            zeros = jnp.zeros((pad, gn), F32)
            a_re[0:pad, :] = zeros
            a_im[0:pad, :] = zeros
            b_re[0:pad, :] = zeros
            b_im[0:pad, :] = zeros
            car_re[...] = jnp.zeros_like(car_re)
            car_im[...] = jnp.zeros_like(car_im)

        u = u_ref[...]
        ubf = u.astype(BF16)
        for q in range(SSM_BLOCKS):
            uq = ubf[:, q * ub:(q + 1) * ub]
            a_re[pad:pad + lc, q * sb:(q + 1) * sb] = jnp.dot(uq, bbr_ref[q], preferred_element_type=F32)
            a_im[pad:pad + lc, q * sb:(q + 1) * sb] = jnp.dot(uq, bbi_ref[q], preferred_element_type=F32)
        bufs = [(a_re, a_im), (b_re, b_im)]
        for q in range(SSM_BLOCKS):
            cols = slice(q * sb, (q + 1) * sb)
            for j in range(nsteps):
                dd = 1 << j
                (s_re, s_im), (d_re, d_im) = bufs[j % 2], bufs[(j + 1) % 2]
                pr, pi = _pow_rows(lrdt_v[:, cols], ang_v[:, cols], float(dd))
                cr, ci = s_re[pad:pad + lc, cols], s_im[pad:pad + lc, cols]
                hr, hi = s_re[pad - dd:pad - dd + lc, cols], s_im[pad - dd:pad - dd + lc, cols]
                d_re[pad:pad + lc, cols] = cr + (pr * hr - pi * hi)
                d_im[pad:pad + lc, cols] = ci + (pr * hi + pi * hr)
            f_re, f_im = bufs[nsteps % 2]
            cr, ci = car_re[:, cols], car_im[:, cols]
            pr, pi = p_re[:, cols], p_im[:, cols]
            sr = f_re[pad:pad + lc, cols] + (pr * cr - pi * ci)
            si = f_im[pad:pad + lc, cols] + (pr * ci + pi * cr)
            sre_ref[:, cols] = sr
            sim_ref[:, cols] = si
            car_re[:, cols] = sr[lc - 1:lc, :]
            car_im[:, cols] = si[lc - 1:lc, :]
            ycols = slice(q * ub, (q + 1) * ub)
            y = (jnp.dot(sr.astype(BF16), ccr_ref[q], preferred_element_type=F32)
                 - jnp.dot(si.astype(BF16), cci_ref[q], preferred_element_type=F32)
                 + d_ref[:, ycols] * u[:, ycols])
            y_ref[:, ycols] = y
            ge_ref[:, ycols] = (0.5 * y * (1.0 + lax.erf(y * (1.0 / math.sqrt(2.0))))).astype(BF16)

    cblk = sw // sw
    row = lambda i: (0, 0)
    blk3 = lambda i: (0, 0, 0)
    return _pcall(
        body, name=name, grid=(t // lc,),
        in_specs=[pl.BlockSpec((lc, sw), lambda i: (i, cblk)), pl.BlockSpec((1, gn), row), pl.BlockSpec((1, gn), row),
                  pl.BlockSpec(bb_re.shape, blk3), pl.BlockSpec(bb_im.shape, blk3),
                  pl.BlockSpec(cc_re.shape, blk3), pl.BlockSpec(cc_im.shape, blk3), pl.BlockSpec((1, sw), row)],
        out_specs=[pl.BlockSpec((lc, sw), lambda i: (i, 0)), pl.BlockSpec((lc, sw), lambda i: (i, 0)),
                   pl.BlockSpec((lc, gn), lambda i: (i, 0)), pl.BlockSpec((lc, gn), lambda i: (i, 0))],
        out_shape=[jax.ShapeDtypeStruct((t, sw), F32), jax.ShapeDtypeStruct((t, sw), BF16),
                   jax.ShapeDtypeStruct((t, gn), F32), jax.ShapeDtypeStruct((t, gn), F32)],
        scratch_shapes=[pltpu.VMEM((lc, gn), F32), pltpu.VMEM((lc, gn), F32),
                        pltpu.VMEM((pad + lc, gn), F32), pltpu.VMEM((pad + lc, gn), F32),
                        pltpu.VMEM((pad + lc, gn), F32), pltpu.VMEM((pad + lc, gn), F32),
                        pltpu.VMEM((1, gn), F32), pltpu.VMEM((1, gn), F32)],
        compiler_params=_params("arbitrary"))(uin, lrdt, ang, bb_re, bb_im, cc_re, cc_im, d_skip)


def _ssm_bwd(name, dge, y, uin, s_re, s_im, lrdt, ang, bbt_re, bbt_im, cct_re, cct_im, d_skip, sw):
    t = uin.shape[0]
    gn = lrdt.shape[1]
    lc = _ssm_chunk(t)
    nb = t // lc
    nsteps = int(math.log2(lc))
    pad = lc // 2
    ub, sb = sw // SSM_BLOCKS, gn // SSM_BLOCKS
    tail = 8

    def body(dge_ref, y_ref, u_ref, sre_ref, sim_ref, tre_ref, tim_ref, lrdt_ref, ang_ref, btr_ref, bti_ref, ctr_ref, cti_ref,
             d_ref, du_ref, dar_ref, dai_ref, dbr_ref, dbi_ref, dcr_ref, dci_ref, dd_ref,
             q_re, q_im, a_re, a_im, b_re, b_im, sp_re, sp_im, car_re, car_im):
        i = pl.program_id(0)
        lrdt_v, ang_v = lrdt_ref[...], ang_ref[...]

        @pl.when(i == 0)
        def _():
            k = (lc - lax.broadcasted_iota(jnp.int32, (lc, 1), 0)).astype(F32)
            qr, qi = _pow_rows(lrdt_v, ang_v, k)
            q_re[...] = qr
            q_im[...] = qi
            zeros = jnp.zeros((pad, gn), F32)
            a_re[lc:lc + pad, :] = zeros
            a_im[lc:lc + pad, :] = zeros
            b_re[lc:lc + pad, :] = zeros
            b_im[lc:lc + pad, :] = zeros
            car_re[...] = jnp.zeros_like(car_re)
            car_im[...] = jnp.zeros_like(car_im)
            for r in (dar_ref, dai_ref, dbr_ref, dbi_ref, dcr_ref, dci_ref, dd_ref):
                r[...] = jnp.zeros_like(r)

        first = (i == nb - 1).astype(F32)
        sp_re[0:tail, :] = tre_ref[...] * (1.0 - first)
        sp_im[0:tail, :] = tim_ref[...] * (1.0 - first)
        sp_re[tail:tail + lc, :] = sre_ref[...]
        sp_im[tail:tail + lc, :] = sim_ref[...]

        yv = y_ref[...]
        u = u_ref[...]
        cdf = 0.5 * (1.0 + lax.erf(yv * (1.0 / math.sqrt(2.0))))
        pdf = jnp.exp(-0.5 * yv * yv) * (1.0 / math.sqrt(2.0 * math.pi))
        dy = dge_ref[...].astype(F32) * (cdf + yv * pdf)
        dd_ref[...] += _colsum(dy * u)
        dyb = dy.astype(BF16)
        ubf = u.astype(BF16)
        for q in range(SSM_BLOCKS):
            dq = dyb[:, q * ub:(q + 1) * ub]
            a_re[0:lc, q * sb:(q + 1) * sb] = jnp.dot(dq, ctr_ref[q], preferred_element_type=F32)
            a_im[0:lc, q * sb:(q + 1) * sb] = -jnp.dot(dq, cti_ref[q], preferred_element_type=F32)
        bufs = [(a_re, a_im), (b_re, b_im)]
        tn_dims = (((0,), (0,)), ((), ()))
        for q in range(SSM_BLOCKS):
            cols = slice(q * sb, (q + 1) * sb)
            ycols = slice(q * ub, (q + 1) * ub)
            for j in range(nsteps):
                ds = 1 << j
                (s_r, s_i), (d_r, d_i) = bufs[j % 2], bufs[(j + 1) % 2]
                pr, pi = _pow_rows(lrdt_v[:, cols], ang_v[:, cols], float(ds))
                cr, ci = s_r[0:lc, cols], s_i[0:lc, cols]
                hr, hi = s_r[ds:ds + lc, cols], s_i[ds:ds + lc, cols]
                d_r[0:lc, cols] = cr + (pr * hr + pi * hi)
                d_i[0:lc, cols] = ci + (pr * hi - pi * hr)
            f_r, f_i = bufs[nsteps % 2]
            cr, ci = car_re[:, cols], car_im[:, cols]
            qr, qi = q_re[:, cols], q_im[:, cols]
            lam_r = f_r[0:lc, cols] + (qr * cr + qi * ci)
            lam_i = f_i[0:lc, cols] + (qr * ci - qi * cr)
            car_re[:, cols] = lam_r[0:1, :]
            car_im[:, cols] = lam_i[0:1, :]
            pvr, pvi = sp_re[tail - 1:tail - 1 + lc, cols], sp_im[tail - 1:tail - 1 + lc, cols]
            dar_ref[:, cols] += _colsum(lam_r * pvr + lam_i * pvi)
            dai_ref[:, cols] += _colsum(lam_i * pvr - lam_r * pvi)
            lrb, lib = lam_r.astype(BF16), lam_i.astype(BF16)
            du_ref[:, ycols] = (jnp.dot(lrb, btr_ref[q], preferred_element_type=F32)
                                + jnp.dot(lib, bti_ref[q], preferred_element_type=F32)
                                + d_ref[:, ycols] * dy[:, ycols]).astype(BF16)
            uq = ubf[:, ycols]
            dbr_ref[q] += lax.dot_general(uq, lrb, tn_dims, preferred_element_type=F32)
            dbi_ref[q] += lax.dot_general(uq, lib, tn_dims, preferred_element_type=F32)
            dq = dyb[:, ycols]
            dcr_ref[q] += lax.dot_general(sre_ref[:, cols].astype(BF16), dq, tn_dims, preferred_element_type=F32)
            dci_ref[q] -= lax.dot_general(sim_ref[:, cols].astype(BF16), dq, tn_dims, preferred_element_type=F32)

    cblk = 1
    rev = lambda i: (nb - 1 - i, 0)
    revu = lambda i: (nb - 1 - i, cblk)
    tailmap = lambda i: (jnp.maximum((nb - 1 - i) * (lc // tail) - 1, 0), 0)
    row = lambda i: (0, 0)
    blk3 = lambda i: (0, 0, 0)
    return _pcall(
        body, name=name, grid=(nb,),
        in_specs=[pl.BlockSpec((lc, sw), rev), pl.BlockSpec((lc, sw), rev), pl.BlockSpec((lc, sw), revu),
                  pl.BlockSpec((lc, gn), rev), pl.BlockSpec((lc, gn), rev),
                  pl.BlockSpec((tail, gn), tailmap), pl.BlockSpec((tail, gn), tailmap),
                  pl.BlockSpec((1, gn), row), pl.BlockSpec((1, gn), row),
                  pl.BlockSpec(bbt_re.shape, blk3), pl.BlockSpec(bbt_im.shape, blk3),
                  pl.BlockSpec(cct_re.shape, blk3), pl.BlockSpec(cct_im.shape, blk3), pl.BlockSpec((1, sw), row)],
        out_specs=[pl.BlockSpec((lc, sw), rev), pl.BlockSpec((1, gn), row), pl.BlockSpec((1, gn), row),
                   pl.BlockSpec((SSM_BLOCKS, ub, sb), blk3), pl.BlockSpec((SSM_BLOCKS, ub, sb), blk3),
                   pl.BlockSpec((SSM_BLOCKS, sb, ub), blk3), pl.BlockSpec((SSM_BLOCKS, sb, ub), blk3),
                   pl.BlockSpec((1, sw), row)],
        out_shape=[jax.ShapeDtypeStruct((t, sw), BF16), jax.ShapeDtypeStruct((1, gn), F32), jax.ShapeDtypeStruct((1, gn), F32),
                   jax.ShapeDtypeStruct((SSM_BLOCKS, ub, sb), F32), jax.ShapeDtypeStruct((SSM_BLOCKS, ub, sb), F32),
                   jax.ShapeDtypeStruct((SSM_BLOCKS, sb, ub), F32), jax.ShapeDtypeStruct((SSM_BLOCKS, sb, ub), F32),
                   jax.ShapeDtypeStruct((1, sw), F32)],
        scratch_shapes=[pltpu.VMEM((lc, gn), F32), pltpu.VMEM((lc, gn), F32),
                        pltpu.VMEM((lc + pad, gn), F32), pltpu.VMEM((lc + pad, gn), F32),
                        pltpu.VMEM((lc + pad, gn), F32), pltpu.VMEM((lc + pad, gn), F32),
                        pltpu.VMEM((tail + lc, gn), F32), pltpu.VMEM((tail + lc, gn), F32),
                        pltpu.VMEM((1, gn), F32), pltpu.VMEM((1, gn), F32)],
        compiler_params=_params("arbitrary"))(dge, y, uin, s_re, s_im, s_re, s_im, lrdt, ang, bbt_re, bbt_im, cct_re, cct_im, d_skip)


def _blockdiag_b(bb, sw):
    gpb = (sw // SSM_GROUP) // SSM_BLOCKS
    b4 = bb.reshape(SSM_BLOCKS, gpb, SSM_STATE, SSM_GROUP)
    eye = jnp.eye(gpb, dtype=bb.dtype)
    out = jnp.einsum('qgnh,gk->qghkn', b4, eye)
    return out.reshape(SSM_BLOCKS, gpb * SSM_GROUP, gpb * SSM_STATE)


def _blockdiag_c(cc, sw):
    gpb = (sw // SSM_GROUP) // SSM_BLOCKS
    c4 = cc.reshape(SSM_BLOCKS, gpb, SSM_GROUP, SSM_STATE)
    eye = jnp.eye(gpb, dtype=cc.dtype)
    out = jnp.einsum('qghn,gk->qgnkh', c4, eye)
    return out.reshape(SSM_BLOCKS, gpb * SSM_STATE, gpb * SSM_GROUP)


def _diag_of_b(dbb, sw):
    gpb = (sw // SSM_GROUP) // SSM_BLOCKS
    d5 = dbb.reshape(SSM_BLOCKS, gpb, SSM_GROUP, gpb, SSM_STATE)
    return jnp.einsum('qghgn->qgnh', d5).reshape(SSM_BLOCKS * gpb * SSM_STATE, SSM_GROUP)


def _diag_of_c(dcc, sw):
    gpb = (sw // SSM_GROUP) // SSM_BLOCKS
    d5 = dcc.reshape(SSM_BLOCKS, gpb, SSM_STATE, gpb, SSM_GROUP)
    return jnp.einsum('qgngh->qghn', d5).reshape(SSM_BLOCKS * gpb, SSM_GROUP, SSM_STATE)


def _ada_fwd(name, c_all, w_sh, b_sh):
    nb, d = c_all.shape
    ncol = w_sh.shape[1]
    tn = _pick(ncol, 768)

    def body(c_ref, w_ref, b_ref, o_ref):
        cv = c_ref[...]
        sil = cv * _sigmoid(cv)
        o_ref[...] = jnp.dot(sil, w_ref[...], preferred_element_type=F32, precision=lax.Precision.HIGHEST) + b_ref[...]

    return _pcall(body, name=name, grid=(ncol // tn,),
                  in_specs=[pl.BlockSpec((nb, d), lambda j: (0, 0)), pl.BlockSpec((d, tn), lambda j: (0, j)),
                            pl.BlockSpec((1, tn), lambda j: (0, j))],
                  out_specs=pl.BlockSpec((nb, tn), lambda j: (0, j)),
                  out_shape=jax.ShapeDtypeStruct((nb, ncol), F32), compiler_params=_params("parallel"))(c_all, w_sh, b_sh)


def _ada_bwd(name, c_all, dmod_sh):
    nb, d = c_all.shape
    ncol = dmod_sh.shape[1]
    tn = _pick(ncol, 768)

    def body(c_ref, g_ref, o_ref):
        cv = c_ref[...]
        sil = cv * _sigmoid(cv)
        o_ref[...] = lax.dot_general(sil, g_ref[...], (((0,), (0,)), ((), ())), preferred_element_type=F32,
                                     precision=lax.Precision.HIGHEST)

    return _pcall(body, name=name, grid=(ncol // tn,),
                  in_specs=[pl.BlockSpec((nb, d), lambda j: (0, 0)), pl.BlockSpec((nb, tn), lambda j: (0, j))],
                  out_specs=pl.BlockSpec((d, tn), lambda j: (0, j)),
                  out_shape=jax.ShapeDtypeStruct((d, ncol), F32), compiler_params=_params("parallel"))(c_all, dmod_sh)


def _place():
    return lax.axis_index("x"), lax.axis_index("y"), lax.axis_index("c")


def _allgather_small(name, blk, deps=()):
    m_per, n = blk.shape

    def body(x_ref, *rest):
        out_ref, send_sems, recv_sems, local_sem = rest[len(deps):]
        x, y, c = _place()
        me, sibling = (x, y, c), (x, y, 1 - c)
        chips = [(1 - x, y), (x, 1 - y), (1 - x, 1 - y)]

        def rows(px, py, pc):
            return out_ref.at[pl.ds((4 * px + 2 * py + pc) * m_per, m_per), :]

        def copy(k, block, to, src=None):
            return pltpu.make_async_remote_copy(
                src_ref=rows(*block) if src is None else src, dst_ref=rows(*block),
                send_sem=send_sems.at[k], recv_sem=recv_sems.at[k], device_id=to, device_id_type=MESH)

        mine = pltpu.make_async_copy(x_ref, rows(*me), local_sem)
        mine.start()
        first = [copy(0, me, sibling, src=x_ref)]
        first += [copy(1 + j, me, (*chip, c), src=x_ref) for j, chip in enumerate(chips)]
        for cp in first:
            cp.start()
        passed = [copy(4 + j, (*chip, c), sibling) for j, chip in enumerate(chips)]
        for j, chip in enumerate(chips):
            copy(1 + j, (*chip, c), me).wait_recv()
            passed[j].start()
        copy(0, sibling, me).wait_recv()
        for j, chip in enumerate(chips):
            copy(4 + j, (*chip, 1 - c), me).wait_recv()
        for cp in first + passed:
            cp.wait_send()
        mine.wait()

    return _pcall(body, name=name, out_shape=jax.ShapeDtypeStruct((N_DEV * m_per, n), blk.dtype),
                  in_specs=[pl.BlockSpec(memory_space=pltpu.VMEM)] + [ANY_SPEC] * len(deps),
                  out_specs=pl.BlockSpec(memory_space=pltpu.VMEM),
                  scratch_shapes=[pltpu.SemaphoreType.DMA((7,)), pltpu.SemaphoreType.DMA((7,)), pltpu.SemaphoreType.DMA],
                  compiler_params=pltpu.CompilerParams(vmem_limit_bytes=VMEM_LIMIT))(blk, *deps)


def _other_chips(x, y):
    return [(1 - x, y), (x, 1 - y), (1 - x, 1 - y)]


def _place_shards(name, shards):
    nw = len(shards)

    def body(*refs):
        src, out, sem = refs[:nw], refs[nw:2 * nw], refs[2 * nw]
        x, y, _ = _place()
        cps = [pltpu.make_async_copy(src[w], out[w].at[2 * x + y], sem.at[w]) for w in range(nw)]
        for cp in cps:
            cp.start()
        for cp in cps:
            cp.wait()

    return _pcall(body, name=name, out_shape=[jax.ShapeDtypeStruct((N_CHIPS,) + s.shape, s.dtype) for s in shards],
                  in_specs=[ANY_SPEC] * nw, out_specs=[ANY_SPEC] * nw, scratch_shapes=[pltpu.SemaphoreType.DMA((nw,))])(*shards)


def _ag_copy(src, land, send, recv, wi, j, chip, x, y, c):
    hr = src.shape[0] // 2
    half = pl.ds(pl.multiple_of(c * hr, 16), hr)
    return pltpu.make_async_remote_copy(
        src_ref=src.at[half, :], dst_ref=land.at[2 * x + y, half, :], send_sem=send.at[3 * wi + j], recv_sem=recv.at[3 * wi + j],
        device_id=(chip[0], chip[1], c), device_id_type=MESH)


def _ag_start(name, shards, lands, groups):
    nw, ng = len(shards), len(groups)

    def body(*refs):
        src, land = refs[:nw], refs[nw:2 * nw]
        sems = refs[2 * nw:2 * nw + 2 * ng]
        token = refs[-1]
        x, y, c = _place()
        for gi, grp in enumerate(groups):
            for wi, w in enumerate(grp):
                for j, chip in enumerate(_other_chips(x, y)):
                    _ag_copy(src[w], land[w], sems[2 * gi], sems[2 * gi + 1], wi, j, chip, x, y, c).start()
        token[...] = jnp.zeros_like(token)

    sem_shapes = []
    for grp in groups:
        sem_shapes += [pltpu.SemaphoreType.DMA((3 * len(grp),))] * 2
    out_shape = sem_shapes + [pltpu.HBM(s.shape, s.dtype) for s in shards] + [pltpu.HBM(l.shape, l.dtype) for l in lands]
    out_shape += [jax.ShapeDtypeStruct((8, 128), F32)]
    res = _pcall(body, name=name, out_shape=out_shape, in_specs=[HBM_SPEC] * (2 * nw),
                 out_specs=[SEM_SPEC] * (2 * ng) + [HBM_SPEC] * (2 * nw) + [pl.BlockSpec(memory_space=pltpu.VMEM)],
                 input_output_aliases={i: 2 * ng + i for i in range(2 * nw)},
                 compiler_params=pltpu.CompilerParams(has_side_effects=EFFECT))(
                     *[_hbm(s) for s in shards], *[_hbm(l) for l in lands])
    sems = [(res[2 * gi], res[2 * gi + 1]) for gi in range(ng)]
    return sems, list(res[2 * ng:2 * ng + nw]), list(res[2 * ng + nw:2 * ng + 2 * nw]), res[-1]


def _ag_wait(name, shards, lands, send, recv, after):
    n = len(shards)

    def body(*refs):
        src, land = refs[:n], refs[n:2 * n]
        send_sem, recv_sem = refs[2 * n], refs[2 * n + 1]
        x, y, c = _place()
        for wi in range(n):
            for j, chip in enumerate(_other_chips(x, y)):
                _ag_copy(src[wi], land[wi], send_sem, recv_sem, wi, j, chip, x, y, c).wait_send()
                _ag_copy(src[wi], land[wi], send_sem, recv_sem, wi, j, chip, chip[0], chip[1], c).wait_recv()

    res = _pcall(body, name=name, out_shape=[pltpu.HBM(a.shape, a.dtype) for a in list(shards) + list(lands)],
                 in_specs=[HBM_SPEC] * (2 * n) + [SEM_SPEC, SEM_SPEC, ANY_SPEC], out_specs=[HBM_SPEC] * (2 * n),
                 input_output_aliases={i: i for i in range(2 * n)},
                 compiler_params=pltpu.CompilerParams(has_side_effects=EFFECT))(*shards, *lands, send, recv, after)
    return list(res[n:])


def _ag_forward(name, lands):
    n = len(lands)

    def body(*refs):
        out = refs[n:2 * n]
        send, recv = refs[2 * n], refs[2 * n + 1]
        x, y, c = _place()
        sib = (x, y, 1 - c)
        cps = []
        for wi in range(n):
            hr = out[wi].shape[1] // 2
            for j, (cx, cy) in enumerate(_other_chips(x, y)):
                got = out[wi].at[2 * cx + cy, pl.ds(pl.multiple_of(c * hr, 16), hr), :]
                cp = pltpu.make_async_remote_copy(src_ref=got, dst_ref=got, send_sem=send.at[3 * wi + j], recv_sem=recv.at[3 * wi + j],
                                                  device_id=sib, device_id_type=MESH)
                cp.start()
                cps.append(cp)
        for wi in range(n):
            hr = out[wi].shape[1] // 2
            for j, (cx, cy) in enumerate(_other_chips(x, y)):
                got = out[wi].at[2 * cx + cy, pl.ds(pl.multiple_of((1 - c) * hr, 16), hr), :]
                pltpu.make_async_remote_copy(src_ref=got, dst_ref=got, send_sem=send.at[3 * wi + j], recv_sem=recv.at[3 * wi + j],
                                             device_id=sib, device_id_type=MESH).wait_recv()
        for cp in cps:
            cp.wait_send()

    res = _pcall(body, name=name, out_shape=[jax.ShapeDtypeStruct(l.shape, l.dtype) for l in lands],
                 in_specs=[ANY_SPEC] * n, out_specs=[ANY_SPEC] * n, input_output_aliases={i: i for i in range(n)},
                 scratch_shapes=[pltpu.SemaphoreType.DMA((3 * n,)), pltpu.SemaphoreType.DMA((3 * n,))])(*lands)
    return list(res)


def _peers(x, y, c):
    offs = [(dx, dy, dc) for dx in (0, 1) for dy in (0, 1) for dc in (0, 1)][1:]
    return [(1 - x if dx else x, 1 - y if dy else y, 1 - c if dc else c) for dx, dy, dc in offs]


def _rs_copy(g_ref, land_ref, send, recv, wi, k, to, sender):
    hr = g_ref.shape[1] // 2
    return pltpu.make_async_remote_copy(
        src_ref=g_ref.at[2 * to[0] + to[1], pl.ds(pl.multiple_of(to[2] * hr, 16), hr), :], dst_ref=land_ref.at[sender],
        send_sem=send.at[7 * wi + k], recv_sem=recv.at[7 * wi + k], device_id=to, device_id_type=MESH)


def _rs_start(name, gs):
    n = len(gs)
    lands = [lax.empty((N_DEV, g.shape[1] // 2, g.shape[2]), BF16) for g in gs]

    def body(*refs):
        g, land = refs[:n], refs[n:2 * n]
        send, recv = refs[2 * n], refs[2 * n + 1]
        token = refs[-1]
        x, y, c = _place()
        me = 4 * x + 2 * y + c
        for wi in range(n):
            for k, to in enumerate(_peers(x, y, c)):
                _rs_copy(g[wi], land[wi], send, recv, wi, k, to, me).start()
        token[...] = jnp.zeros_like(token)

    out_shape = [pltpu.SemaphoreType.DMA((7 * n,))] * 2 + [pltpu.HBM(a.shape, a.dtype) for a in list(gs) + lands]
    out_shape += [jax.ShapeDtypeStruct((8, 128), F32)]
    res = _pcall(body, name=name, out_shape=out_shape, in_specs=[HBM_SPEC] * (2 * n),
                 out_specs=[SEM_SPEC] * 2 + [HBM_SPEC] * (2 * n) + [pl.BlockSpec(memory_space=pltpu.VMEM)],
                 input_output_aliases={i: 2 + i for i in range(2 * n)},
                 compiler_params=pltpu.CompilerParams(has_side_effects=EFFECT))(
                     *[_hbm(a) for a in gs], *[_hbm(a) for a in lands])
    return res[0], res[1], list(res[2:2 + n]), list(res[2 + n:2 + 2 * n]), res[-1]


def _rs_wait(name, gs, lands, send, recv, after):
    n = len(gs)

    def body(*refs):
        g, land = refs[:n], refs[n:2 * n]
        send_sem, recv_sem = refs[2 * n], refs[2 * n + 1]
        x, y, c = _place()
        me = 4 * x + 2 * y + c
        for wi in range(n):
            for k, to in enumerate(_peers(x, y, c)):
                _rs_copy(g[wi], land[wi], send_sem, recv_sem, wi, k, to, me).wait_send()
                _rs_copy(g[wi], land[wi], send_sem, recv_sem, wi, k, (x, y, c), 4 * to[0] + 2 * to[1] + to[2]).wait_recv()

    res = _pcall(body, name=name, out_shape=[pltpu.HBM(a.shape, a.dtype) for a in list(gs) + list(lands)],
                 in_specs=[HBM_SPEC] * (2 * n) + [SEM_SPEC, SEM_SPEC, ANY_SPEC], out_specs=[HBM_SPEC] * (2 * n),
                 input_output_aliases={i: i for i in range(2 * n)},
                 compiler_params=pltpu.CompilerParams(has_side_effects=EFFECT))(*gs, *lands, send, recv, after)
    return list(res[:n]), list(res[n:])


def _rs_sum(name, g, land):
    _, rs, cs = g.shape
    hr = rs // 2
    ch = _pick(hr, 64, 16)

    def body(g_ref, land_ref, out_ref, recv, local_sems, sib_sems):
        x, y, c = _place()
        me = 4 * x + 2 * y + c
        cps = [pltpu.make_async_copy(g_ref.at[2 * x + y, pl.ds(pl.multiple_of(c * hr, 16), hr), :], recv.at[me], local_sems.at[7])]
        for k, (tx, ty, tc) in enumerate(_peers(x, y, c)):
            slot = 4 * tx + 2 * ty + tc
            cps.append(pltpu.make_async_copy(land_ref.at[slot], recv.at[slot], local_sems.at[k]))
        for cp in cps:
            cp.start()
        for cp in cps:
            cp.wait()
        base = pl.multiple_of(c * hr, 16)
        for r0 in range(0, hr, ch):
            acc = recv[0, r0:r0 + ch, :].astype(F32)
            for k in range(1, N_DEV):
                acc = acc + recv[k, r0:r0 + ch, :].astype(F32)
            out_ref[pl.ds(base + r0, ch), :] = acc
        half = out_ref.at[pl.ds(base, hr), :]
        sib = pltpu.make_async_remote_copy(src_ref=half, dst_ref=half, send_sem=sib_sems.at[0], recv_sem=sib_sems.at[1],
                                           device_id=(x, y, 1 - c), device_id_type=MESH)
        sib.start()
        other = out_ref.at[pl.ds(pl.multiple_of((1 - c) * hr, 16), hr), :]
        pltpu.make_async_remote_copy(src_ref=other, dst_ref=other, send_sem=sib_sems.at[0], recv_sem=sib_sems.at[1],
                                     device_id=(x, y, 1 - c), device_id_type=MESH).wait_recv()
        sib.wait_send()

    return _pcall(
        body, name=name, out_shape=jax.ShapeDtypeStruct((rs, cs), F32),
        in_specs=[ANY_SPEC, ANY_SPEC], out_specs=pl.BlockSpec(memory_space=pltpu.VMEM),
        scratch_shapes=[pltpu.VMEM((N_DEV, hr, cs), BF16), pltpu.SemaphoreType.DMA((8,)), pltpu.SemaphoreType.DMA((2,))],
        compiler_params=pltpu.CompilerParams(vmem_limit_bytes=VMEM_LIMIT))(g, land)


def _ffn_fwd(tag, x, g, sh, sc, gt, get_w_in, get_w_out):
    t, d = x.shape
    h = _norm_fwd(tag + "_norm", x, g, sc, sh)
    ab = _mm(tag + "_in", h, get_w_in(h), b_stacked=True, tn=1408)
    act = _swiglu_fwd(tag + "_act", ab)
    w_out = get_w_out(act)
    blk = (512, d)

    def epi(acc, res, scale):
        return res + (0.5 * scale) * acc, acc

    x_new, y = _mm(tag + "_out", act, w_out, tm=512, tn=d, epi=epi,
                   epi_ins=[(x, blk, lambda i, j: (i, j)), (gt, (1, d), lambda i, j: (0, j))],
                   epi_outs=[((t, d), F32, blk, lambda i, j: (i, j)), ((t, d), BF16, blk, lambda i, j: (i, j))])
    return x_new, (x, h, ab, act, y)


def _ffn_bwd(tag, dx_new, saved, g, sc, gt, w_in, w_out, start_rs):
    x, h, ab, act, y = saved
    d = x.shape[1]
    f = act.shape[1]
    dy, dgt = _resgate_bwd(tag + "_dgate", dx_new, y, gt, 0.5)
    dact = _mm(tag + "_dact", dy, w_out, tb=True, tn=1408)
    dw_out = _mm(tag + "_dwout", act, dy, ta=True, tm=1408, tn=d, tk=512)
    tok = start_rs("out", dw_out.reshape(N_CHIPS, f // N_CHIPS, d))
    dab = _swiglu_bwd(tag + "_dswiglu", dact, ab, deps=(tok,))
    dw_in = _mm(tag + "_dwin", h, dab, ta=True, out_stacked=True, tm=d, tn=1408, tk=512)
    tok = start_rs("in", dw_in)
    dh = _mm(tag + "_dh", dab, w_in, tb=True, b_stacked=True, tn=d, tk=1408, deps=(tok,))
    dx, dsh, dsc, dg = _norm_bwd(tag + "_dnorm", dh, x, dx_new, g, sc)
    return dx, dsh, dsc, dgt, dg


def _row(v):
    return v.reshape(1, -1)


def _pack(parts):
    cols = []
    for p in parts:
        flat = p.reshape(-1).astype(F32)
        padn = (-flat.shape[0]) % 128
        cols.append(jnp.pad(flat, (0, padn)) if padn else flat)
    flat = jnp.concatenate(cols)
    padn = (-flat.shape[0]) % 1024
    if padn:
        flat = jnp.pad(flat, (0, padn))
    return flat.reshape(-1, 128)


def _unpack(packed, shapes):
    flat = packed.reshape(-1)
    out, off = [], 0
    for s in shapes:
        n = math.prod(s)
        out.append(flat[off:off + n].reshape(s))
        off += n + ((-n) % 128)
    return out


SMALL = ['b_ada', 'g_ffn1', 'g_mix', 'pool_w', 'pool_b', 'pool_scale', 'ssm_lam_re_log', 'ssm_lam_im', 'ssm_log_dt',
         'ssm_b_re', 'ssm_b_im', 'ssm_c_re', 'ssm_c_im', 'ssm_d', 'b_glu', 'g_ffn2', 'g_final']
BIG = ['w_ffn1_in', 'w_ffn1_out', 'w_in', 'w_pool_up', 'w_glu', 'w_ssm_up', 'w_out', 'w_ffn2_in', 'w_ffn2_out']
AG_GROUPS = [[0], [1], [2, 3, 4, 5, 6], [7, 8]]
WEIGHTS = ['w_ada', 'b_ada', 'g_ffn1', 'w_ffn1_in', 'w_ffn1_out', 'g_mix', 'w_in', 'pool_w', 'pool_b', 'pool_scale', 'w_pool_up',
           'ssm_lam_re_log', 'ssm_lam_im', 'ssm_log_dt', 'ssm_b_re', 'ssm_b_im', 'ssm_c_re', 'ssm_c_im', 'ssm_d', 'w_glu', 'b_glu',
           'w_ssm_up', 'w_out', 'g_ffn2', 'w_ffn2_in', 'w_ffn2_out', 'g_final']


def kernel(x, c, w_ada, b_ada, g_ffn1, w_ffn1_in, w_ffn1_out, g_mix, w_in, pool_w, pool_b, pool_scale, w_pool_up, ssm_lam_re_log, ssm_lam_im, ssm_log_dt, ssm_b_re, ssm_b_im, ssm_c_re, ssm_c_im, ssm_d, w_glu, b_glu, w_ssm_up, w_out, g_ffn2, w_ffn2_in, w_ffn2_out, g_final, loss_target, m_w_ada, m_b_ada, m_g_ffn1, m_w_ffn1_in, m_w_ffn1_out, m_g_mix, m_w_in, m_pool_w, m_pool_b, m_pool_scale, m_w_pool_up, m_ssm_lam_re_log, m_ssm_lam_im, m_ssm_log_dt, m_ssm_b_re, m_ssm_b_im, m_ssm_c_re, m_ssm_c_im, m_ssm_d, m_w_glu, m_b_glu, m_w_ssm_up, m_w_out, m_g_ffn2, m_w_ffn2_in, m_w_ffn2_out, m_g_final, v_w_ada, v_b_ada, v_g_ffn1, v_w_ffn1_in, v_w_ffn1_out, v_g_mix, v_w_in, v_pool_w, v_pool_b, v_pool_scale, v_w_pool_up, v_ssm_lam_re_log, v_ssm_lam_im, v_ssm_log_dt, v_ssm_b_re, v_ssm_b_im, v_ssm_c_re, v_ssm_c_im, v_ssm_d, v_w_glu, v_b_glu, v_w_ssm_up, v_w_out, v_g_ffn2, v_w_ffn2_in, v_w_ffn2_out, v_g_final):
    args = dict(locals())
    wt = {n: args[n] for n in WEIGHTS}
    mom = {n: args["m_" + n] for n in WEIGHTS}
    var = {n: args["v_" + n] for n in WEIGHTS}

    t, d = x.shape[1], x.shape[2]
    pw = pool_b.shape[1]
    sw = ssm_d.shape[1]
    ngrp = sw // SSM_GROUP
    gn = ngrp * SSM_STATE
    xi, yi, ci = _place()
    b_me = 4 * xi + 2 * yi + ci
    s_me = 2 * xi + yi
    x2d = x[0]
    tgt = loss_target[0]

    shards = [wt[n][0].astype(BF16) for n in BIG]
    ag_sems, shards_t, lands_t, tok = _ag_start("ag_start", shards, _place_shards("ag_place", shards), AG_GROUPS)
    full = {}

    def weights(gi, after):
        grp = AG_GROUPS[gi]
        if BIG[grp[0]] not in full:
            ls = _ag_wait("ag_wait%d" % gi, [shards_t[w] for w in grp], [lands_t[w] for w in grp], *ag_sems[gi], after)
            for w, l in zip(grp, _ag_forward("ag_fwd%d" % gi, ls)):
                full[BIG[w]] = l
        return full

    c_all = _allgather_small("ag_c", c.reshape(8, d // 8), deps=(tok,)).reshape(N_DEV, d)
    ncol = w_ada.shape[2]
    b_sh = lax.dynamic_slice(b_ada, (0, s_me * ncol), (1, ncol))
    mod_sh = _ada_fwd("ada_fwd", c_all, w_ada[0], b_sh)
    mod_all = _allgather_small("ag_mod", mod_sh)
    mod_me = jnp.concatenate([lax.dynamic_slice(mod_all, (16 * s + b_me, 0), (1, ncol)) for s in range(N_CHIPS)], axis=1)
    mod = mod_me.reshape(9, d)
    sh1, sc1, gt1, sh2, sc2, gt2, sh3, sc3, gt3 = [mod[k:k + 1] for k in range(9)]

    f = w_ffn1_out.shape[1] * N_CHIPS

    x1, sav1 = _ffn_fwd("ffn1", x2d, g_ffn1, sh1, sc1, gt1, lambda after: weights(0, after)['w_ffn1_in'],
                        lambda after: weights(1, after)['w_ffn1_out'].reshape(f, d))

    h2 = _norm_fwd("mix_norm", x1, g_mix, sc2, sh2)
    weights(2, h2)
    wo = full['w_out'].reshape(d, d)
    uin = _mm("mix_in", h2, full['w_in'], b_stacked=True, tn=768, out_dtype=F32)
    p_pool, z_pool = _pool_fwd("pool_fwd", uin, pool_w[0], pool_b, pool_scale, pw)
    y_pool = _mm("pool_up", p_pool, full['w_pool_up'], b_stacked=True, tn=256)

    col = lambda a: a.reshape(gn, 1)
    lrl_c, li_c = col(ssm_lam_re_log), col(ssm_lam_im)
    ldt_c = col(jnp.broadcast_to(ssm_log_dt.reshape(ngrp, 1), (ngrp, SSM_STATE)))
    b_re2, b_im2 = ssm_b_re.reshape(gn, SSM_GROUP), ssm_b_im.reshape(gn, SSM_GROUP)
    lrdt_c, ang_c, bb_re, bb_im = _ssm_prep("ssm_prep", lrl_c, li_c, ldt_c, b_re2, b_im2)
    lrdt, ang = lrdt_c.reshape(1, gn), ang_c.reshape(1, gn)
    bbd_re, bbd_im = _blockdiag_b(bb_re, sw), _blockdiag_b(bb_im, sw)
    ccd_re, ccd_im = _blockdiag_c(ssm_c_re[0], sw), _blockdiag_c(ssm_c_im[0], sw)
    y_s, ge, s_re, s_im = _ssm_fwd("ssm_fwd", uin, lrdt, ang, bbd_re.astype(BF16), bbd_im.astype(BF16),
                                   ccd_re.astype(BF16), ccd_im.astype(BF16), ssm_d, sw)
    gv = _mm("glu_in", ge, full['w_glu'], b_stacked=True, tn=256)
    sg = _glu_fwd("glu_act", gv, b_glu)
    y_ssm = _mm("ssm_up", sg, full['w_ssm_up'], b_stacked=True, tn=256)
    merged = _gates_fwd("gates", y_pool, y_ssm, uin, d, pw)
    blk = (512, d)

    def epi2(acc, res, scale):
        return res + scale * acc, acc

    x2, y2 = _mm("mix_out", merged, wo, tm=512, tn=d, epi=epi2,
                 epi_ins=[(x1, blk, lambda i, j: (i, j)), (gt2, (1, d), lambda i, j: (0, j))],
                 epi_outs=[((t, d), F32, blk, lambda i, j: (i, j)), ((t, d), BF16, blk, lambda i, j: (i, j))])

    x3, sav3 = _ffn_fwd("ffn2", x2, g_ffn2, sh3, sc3, gt3, lambda after: weights(3, after)['w_ffn2_in'],
                        lambda after: weights(3, after)['w_ffn2_out'].reshape(f, d))

    dx3, dg_final, loss_v = _loss_bwd("loss", x3, tgt, _row(g_final))
    loss = lax.psum(loss_v[0, 0], ("x", "y", "c"))

    gs = {}
    rs_open = []

    def start_rs(names, gbs):
        send, recv, g_thru, land_thru, token = _rs_start("rs_start_" + names[0], gbs)
        rs_open.append((names, send, recv, g_thru, land_thru))
        return token

    dx2, dsh3, dsc3, dgt3, gs['g_ffn2'] = _ffn_bwd(
        "ffn2b", dx3, sav3, g_ffn2, sc3, gt3, full['w_ffn2_in'], full['w_ffn2_out'].reshape(f, d),
        lambda key, g: start_rs(['w_ffn2_' + key], [g]))

    dy2, dgt2 = _resgate_bwd("mix_dgate", dx2, y2, gt2, 1.0)
    dmerged = _mm("mix_dmerged", dy2, wo, tb=True, tn=d)
    g_wo = _mm("mix_dwout", merged, dy2, ta=True, tm=d, tn=d, tk=512).reshape(N_CHIPS, d // N_CHIPS, d)
    dy_pool, dy_ssm, dgl = _gates_bwd("gates_bwd", dmerged, y_pool, y_ssm, uin, d, pw)

    dp = _mm("pool_dup", dy_pool, full['w_pool_up'], tb=True, b_stacked=True, tn=pw, tk=256)
    g_wpu = _mm("pool_dwup", p_pool, dy_pool, ta=True, out_stacked=True, tm=pw, tn=256, tk=512)
    du_pool, gs['pool_w'], gs['pool_b'], gs['pool_scale'] = _pool_bwd("pool_bwd", dp, z_pool, pool_w[0], pool_b, pool_scale)

    dsg = _mm("ssm_dup", dy_ssm, full['w_ssm_up'], tb=True, b_stacked=True, tn=sw, tk=256)
    g_wsu = _mm("ssm_dwup", sg, dy_ssm, ta=True, out_stacked=True, tm=sw, tn=256, tk=512)
    tok = start_rs(['w_out', 'w_pool_up', 'w_ssm_up'], [g_wo, g_wpu, g_wsu])
    dgv, gs['b_glu'] = _glu_bwd("glu_bwd", dsg, gv, b_glu, deps=(tok,))
    dge = _mm("glu_dge", dgv, full['w_glu'], tb=True, b_stacked=True, tn=sw, tk=256)
    g_wglu = _mm("glu_dw", ge, dgv, ta=True, out_stacked=True, tm=sw, tn=256, tk=512)
    tr = lambda a: jnp.swapaxes(a, 1, 2).astype(BF16)
    (du_ssm, g_abre, g_abim, g_bbd_re, g_bbd_im, g_ccd_re, g_ccd_im, gs['ssm_d']) = _ssm_bwd(
        "ssm_bwd", dge, y_s, uin, s_re, s_im, lrdt, ang, tr(bbd_re), tr(bbd_im), tr(ccd_re), tr(ccd_im), ssm_d, sw)
    gs['ssm_c_re'], gs['ssm_c_im'] = _diag_of_c(g_ccd_re, sw), _diag_of_c(g_ccd_im, sw)
    d_lrl, d_li, d_ldt, d_bre, d_bim = _ssm_param_bwd(
        "ssm_param_bwd", lrl_c, li_c, ldt_c, b_re2, b_im2, g_abre.reshape(gn, 1), g_abim.reshape(gn, 1),
        _diag_of_b(g_bbd_re, sw), _diag_of_b(g_bbd_im, sw))
    gs['ssm_lam_re_log'], gs['ssm_lam_im'] = d_lrl, d_li
    gs['ssm_log_dt'] = jnp.sum(d_ldt.reshape(ngrp, SSM_STATE), axis=1)
    gs['ssm_b_re'], gs['ssm_b_im'] = d_bre, d_bim

    duin = jnp.concatenate([du_pool, du_ssm, dgl], axis=1)
    g_win = _mm("mix_dwin", h2, duin, ta=True, out_stacked=True, tm=d, tn=768, tk=512)
    tok = start_rs(['w_glu', 'w_in'], [g_wglu, g_win])
    dh2 = _mm("mix_dh", duin, full['w_in'], tb=True, b_stacked=True, tn=d, tk=768, deps=(tok,))
    dx1, dsh2, dsc2, gs['g_mix'] = _norm_bwd("mix_dnorm", dh2, x1, dx2, g_mix, sc2)

    dx0, dsh1, dsc1, dgt1, gs['g_ffn1'] = _ffn_bwd(
        "ffn1b", dx1, sav1, g_ffn1, sc1, gt1, full['w_ffn1_in'], full['w_ffn1_out'].reshape(f, d),
        lambda key, g: start_rs(['w_ffn1_' + key], [g]))
    gs['g_final'] = dg_final

    dmod = jnp.concatenate([dsh1, dsc1, dgt1, dsh2, dsc2, dgt2, dsh3, dsc3, dgt3], axis=1)
    gs['b_ada'] = dmod
    dmod_all = _allgather_small("ag_dmod", dmod.reshape(-1, 128)).reshape(N_DEV, 9 * d)
    dmod_sh = lax.dynamic_slice(dmod_all, (0, s_me * ncol), (N_DEV, ncol))
    g_w_ada = _ada_bwd("ada_bwd", c_all, dmod_sh)

    grads, delta, new_m, new_v = {}, {}, {}, {}
    small_shapes = [wt[n].shape for n in SMALL]
    g8 = _allgather_small("ag_small_grads", _pack([gs[n] for n in SMALL]))
    res = _adamw_small("adamw_small", _pack([wt[n] for n in SMALL]), g8, _pack([mom[n] for n in SMALL]), _pack([var[n] for n in SMALL]))
    for dst, packed in zip((grads, delta, new_m, new_v), res):
        for n, val in zip(SMALL, _unpack(packed, small_shapes)):
            dst[n] = val

    dl, mn, vn = _adamw("adamw_w_ada", w_ada[0], g_w_ada, m_w_ada[0], v_w_ada[0])
    grads['w_ada'], delta['w_ada'], new_m['w_ada'], new_v['w_ada'] = g_w_ada[None], dl[None], mn[None], vn[None]
    after = dl
    for names, send, recv, g_thru, land_thru in rs_open:
        g_done, land_done = _rs_wait("rs_wait_" + names[0], g_thru, land_thru, send, recv, after)
        for n, g_own, land in zip(names, g_done, land_done):
            g_sum = _rs_sum("rs_sum_" + n, g_own, land)
            dl, mn, vn = _adamw("adamw_" + n, wt[n][0], g_sum, mom[n][0], var[n][0])
            grads[n], delta[n], new_m[n], new_v[n] = g_sum[None], dl[None], mn[None], vn[None]
            after = dl

    return (loss, dx0[None], *[grads[n] for n in WEIGHTS], *[delta[n] for n in WEIGHTS],
            *[new_m[n] for n in WEIGHTS], *[new_v[n] for n in WEIGHTS])
```

```python
import functools
import math

import jax
import jax.numpy as jnp
from jax import lax
from jax.experimental import pallas as pl
from jax.experimental.pallas import tpu as pltpu

F32 = jnp.float32
BF16 = jnp.bfloat16
MESH = pl.DeviceIdType.MESH

EPS = 1e-6
POOL_WINDOWS = (2, 4, 8, 16)
POOL_HALO = 16
SSM_GROUP = 16
SSM_STATE = 64
SSM_BLOCKS = 4
N_DEV = 8
N_CHIPS = 4
ADAM_LR = 0.001
ADAM_B1 = 0.9
ADAM_B2 = 0.999
ADAM_EPS = 1e-08
ADAM_WD = 0.01
ADAM_STEP = 10
VMEM_LIMIT = 56 * 1024 * 1024


ANY_SPEC = pl.BlockSpec(memory_space=pl.ANY)
HBM_SPEC = pl.BlockSpec(memory_space=pltpu.HBM)
SEM_SPEC = pl.BlockSpec(memory_space=pltpu.SEMAPHORE)
EFFECT = pltpu.SideEffectType.DATAFLOW_SIDE_EFFECTING


def _hbm(a):
    return pltpu.with_memory_space_constraint(a, pltpu.HBM)


def _pcall(body, **kw):
    return pl.pallas_call(body, **kw)


def _params(*sem):
    return pltpu.CompilerParams(dimension_semantics=sem, vmem_limit_bytes=VMEM_LIMIT)


def _pick(n, cap, mult=128):
    if n <= cap:
        return n
    best = None
    for d in range(mult, cap + 1, mult):
        if n % d == 0:
            best = d
    assert best is not None, (n, cap, mult)
    return best


def _sigmoid(v):
    return 1.0 / (1.0 + jnp.exp(-v))


def _rowwise(name, fn, ins, params, outs, reds, tm, deps=()):
    t = ins[0][0].shape[0]
    tm = min(tm, t)
    nb = t // tm
    ni, npar, no, nd = len(ins), len(params), len(outs), len(deps)

    def body(*refs):
        iv = [r[...] for r in refs[:ni]]
        pv = [r[...] for r in refs[ni:ni + npar]]
        o_refs = refs[ni + npar + nd:ni + npar + nd + no]
        r_refs = refs[ni + npar + nd + no:]
        ovals, rvals = fn(iv, pv)
        for o_ref, val in zip(o_refs, ovals):
            parts = val if isinstance(val, (list, tuple)) else [val]
            off = 0
            for p in parts:
                o_ref[:, off:off + p.shape[1]] = p.astype(o_ref.dtype)
                off += p.shape[1]
        if r_refs:
            @pl.when(pl.program_id(0) == 0)
            def _():
                for r in r_refs:
                    r[...] = jnp.zeros_like(r)
            for r, val in zip(r_refs, rvals):
                r[...] += val

    in_specs = [pl.BlockSpec((tm, w), functools.partial(lambda i, cb: (i, cb), cb=cb)) for (_, w, cb) in ins]
    in_specs += [pl.BlockSpec(p.shape, lambda i: (0, 0)) for p in params]
    in_specs += [ANY_SPEC] * nd
    out_shape = [jax.ShapeDtypeStruct((t, w), dt) for (w, dt) in outs]
    out_shape += [jax.ShapeDtypeStruct((1, w), F32) for w in reds]
    out_specs = [pl.BlockSpec((tm, w), lambda i: (i, 0)) for (w, _) in outs]
    out_specs += [pl.BlockSpec((1, w), lambda i: (0, 0)) for w in reds]
    res = _pcall(body, name=name, grid=(nb,), in_specs=in_specs, out_specs=out_specs, out_shape=out_shape,
                 compiler_params=_params("arbitrary"))(*[a for a, _, _ in ins], *params, *deps)
    return res


def _colsum(v):
    return jnp.sum(v, axis=0, keepdims=True)


def _mm(name, a, b, *, ta=False, tb=False, b_stacked=False, out_stacked=False, tm=512, tn=1024, tk=2816,
        out_dtype=BF16, epi=None, epi_ins=(), epi_outs=None, deps=()):
    if ta:
        kdim, m = a.shape
    else:
        m, kdim = a.shape
    if b_stacked:
        ns = b.shape[2]
        n = b.shape[1] if tb else N_CHIPS * ns
        assert kdim == (N_CHIPS * ns if tb else b.shape[1]), (name, a.shape, b.shape)
    else:
        n = b.shape[0] if tb else b.shape[1]
        assert kdim == (b.shape[1] if tb else b.shape[0]), (name, a.shape, b.shape)
    tm = _pick(m, tm, 8)
    if b_stacked and not tb:
        tn = _pick(ns, tn)
    elif out_stacked:
        ns_out = n // N_CHIPS
        tn = _pick(ns_out, tn)
    else:
        tn = _pick(n, tn)
    if b_stacked and tb:
        tk = _pick(ns, tk)
    else:
        tk = _pick(kdim, tk, 8 if ta else 128)
    nm, nn, nk = m // tm, n // tn, kdim // tk

    a_spec = pl.BlockSpec((tk, tm), lambda i, j, k: (k, i)) if ta else pl.BlockSpec((tm, tk), lambda i, j, k: (i, k))
    if b_stacked and not tb:
        bps = ns // tn
        b_spec = pl.BlockSpec((None, tk, tn), lambda i, j, k: (j // bps, k, j % bps))
    elif b_stacked and tb:
        bps = ns // tk
        b_spec = pl.BlockSpec((None, tn, tk), lambda i, j, k: (k // bps, j, k % bps))
    elif tb:
        b_spec = pl.BlockSpec((tn, tk), lambda i, j, k: (j, k))
    else:
        b_spec = pl.BlockSpec((tk, tn), lambda i, j, k: (k, j))
    dims = (((0 if ta else 1,), (1 if tb else 0,)), ((), ()))

    if epi_outs is None:
        if out_stacked:
            bps_o = (n // N_CHIPS) // tn
            epi_outs = [((N_CHIPS, m, n // N_CHIPS), out_dtype, (None, tm, tn), lambda i, j: (j // bps_o, i, j % bps_o))]
        else:
            epi_outs = [((m, n), out_dtype, (tm, tn), lambda i, j: (i, j))]
    ne, no, nd = len(epi_ins), len(epi_outs), len(deps)

    def body(a_ref, b_ref, *rest):
        e_refs = rest[:ne]
        o_refs = rest[ne + nd:ne + nd + no]
        scratch = rest[ne + nd + no:]
        p = lax.dot_general(a_ref[...].astype(BF16), b_ref[...].astype(BF16), dims, preferred_element_type=F32)

        def finish(acc):
            vals = (acc,) if epi is None else epi(acc, *[r[...] for r in e_refs])
            for o_ref, val in zip(o_refs, vals):
                o_ref[...] = val.astype(o_ref.dtype)

        if nk == 1:
            finish(p)
        else:
            acc_ref = scratch[0]
            k = pl.program_id(2)

            @pl.when(k == 0)
            def _():
                acc_ref[...] = p

            @pl.when(k > 0)
            def _():
                acc_ref[...] += p

            @pl.when(k == nk - 1)
            def _():
                finish(acc_ref[...])

    def _ij(f):
        return lambda i, j, k: f(i, j)

    in_specs = [a_spec, b_spec] + [pl.BlockSpec(blk, _ij(f)) for (_, blk, f) in epi_ins] + [ANY_SPEC] * nd
    out_specs = [pl.BlockSpec(blk, _ij(f)) for (_, _, blk, f) in epi_outs]
    out_shape = [jax.ShapeDtypeStruct(s, dt) for (s, dt, _, _) in epi_outs]
    scratch = [pltpu.VMEM((tm, tn), F32)] if nk > 1 else []
    res = _pcall(body, name=name, grid=(nm, nn, nk), in_specs=in_specs, out_specs=out_specs, out_shape=out_shape,
                 scratch_shapes=scratch, compiler_params=_params("parallel", "parallel", "arbitrary"))(
                     a, b, *[x for x, _, _ in epi_ins], *deps)
    return res[0] if len(res) == 1 else res


def _norm_fwd(name, x, g, sc, sh, deps=()):
    d = x.shape[1]

    def fn(iv, pv):
        (xv,), (gv, scv, shv) = iv, pv
        r = lax.rsqrt(jnp.mean(xv * xv, axis=-1, keepdims=True) + EPS)
        return [xv * r * gv * (1.0 + scv) + shv], []

    return _rowwise(name, fn, [(x, d, 0)], [g, sc, sh], [(d, BF16)], [], 512, deps=deps)[0]


def _norm_bwd(name, dh, x, dres, g, sc):
    d = x.shape[1]

    def fn(iv, pv):
        (dhv, xv, drv), (gv, scv) = iv, pv
        dhv = dhv.astype(F32)
        r = lax.rsqrt(jnp.mean(xv * xv, axis=-1, keepdims=True) + EPS)
        xr = xv * r
        dn = dhv * (1.0 + scv)
        gd = gv * dn
        dx = drv + r * (gd - xr * jnp.mean(gd * xr, axis=-1, keepdims=True))
        return [dx], [_colsum(dhv), _colsum(dhv * xr * gv), _colsum(dn * xr)]

    return _rowwise(name, fn, [(dh, d, 0), (x, d, 0), (dres, d, 0)], [g, sc], [(d, F32)], [d, d, d], 256)


def _loss_bwd(name, x, tgt, g):
    d = x.shape[1]

    def fn(iv, pv):
        (xv, tv), (gv,) = iv, pv
        r = lax.rsqrt(jnp.mean(xv * xv, axis=-1, keepdims=True) + EPS)
        xr = xv * r
        e = xr * gv - tv
        loss_row = 0.5 * jnp.mean(e * e, axis=-1, keepdims=True)
        dout = e * (1.0 / d)
        gd = gv * dout
        dx = r * (gd - xr * jnp.mean(gd * xr, axis=-1, keepdims=True))
        return [dx], [_colsum(dout * xr), _colsum(loss_row * jnp.ones((1, 128), F32))]

    return _rowwise(name, fn, [(x, d, 0), (tgt, d, 0)], [g], [(d, F32)], [d, 128], 256)


def _resgate_bwd(name, dx, y, gt, factor, deps=()):
    d = dx.shape[1]

    def fn(iv, pv):
        (dxv, yv), (gtv,) = iv, pv
        return [factor * gtv * dxv], [_colsum(factor * dxv * yv.astype(F32))]

    return _rowwise(name, fn, [(dx, d, 0), (y, d, 0)], [gt], [(d, BF16)], [d], 512, deps=deps)


def _swiglu_fwd(name, ab):
    f = ab.shape[1] // 2

    def fn(iv, pv):
        a, b = iv[0].astype(F32), iv[1].astype(F32)
        return [a * _sigmoid(a) * b], []

    return _rowwise(name, fn, [(ab, f, 0), (ab, f, 1)], [], [(f, BF16)], [], 256)[0]


def _swiglu_bwd(name, dact, ab, deps=()):
    f = ab.shape[1] // 2

    def fn(iv, pv):
        dv, a, b = iv[0].astype(F32), iv[1].astype(F32), iv[2].astype(F32)
        s = _sigmoid(a)
        return [[dv * b * (s * (1.0 + a * (1.0 - s))), dv * (a * s)]], []

    return _rowwise(name, fn, [(dact, f, 0), (ab, f, 0), (ab, f, 1)], [], [(2 * f, BF16)], [], 256, deps=deps)[0]


def _gates_fwd(name, y_pool, y_ssm, uin, d, pw):
    cb = (2 * pw) // d

    def fn(iv, pv):
        yp, ys, glp, gls = [v.astype(F32) for v in iv]
        return [_sigmoid(glp) * yp + _sigmoid(gls) * ys], []

    return _rowwise(name, fn, [(y_pool, d, 0), (y_ssm, d, 0), (uin, d, cb), (uin, d, cb + 1)], [], [(d, BF16)], [], 256)[0]


def _gates_bwd(name, dm, y_pool, y_ssm, uin, d, pw):
    cb = (2 * pw) // d

    def fn(iv, pv):
        dmv, yp, ys, glp, gls = [v.astype(F32) for v in iv]
        sp, ss = _sigmoid(glp), _sigmoid(gls)
        return [dmv * sp, dmv * ss, [dmv * yp * sp * (1.0 - sp), dmv * ys * ss * (1.0 - ss)]], []

    return _rowwise(name, fn, [(dm, d, 0), (y_pool, d, 0), (y_ssm, d, 0), (uin, d, cb), (uin, d, cb + 1)], [],
                    [(d, BF16), (d, BF16), (2 * d, BF16)], [], 256)


def _glu_fwd(name, gv, b_glu):
    w = gv.shape[1] // 2

    def fn(iv, pv):
        gvv, (bv,) = iv[0].astype(F32) + pv[0], pv
        return [gvv[:, :w] * _sigmoid(gvv[:, w:])], []

    return _rowwise(name, fn, [(gv, 2 * w, 0)], [b_glu], [(w, BF16)], [], 512)[0]


def _glu_bwd(name, dsg, gv, b_glu, deps=()):
    w = gv.shape[1] // 2

    def fn(iv, pv):
        dv = iv[0].astype(F32)
        gvv = iv[1].astype(F32) + pv[0]
        val, s = gvv[:, :w], _sigmoid(gvv[:, w:])
        dval = dv * s
        dgate = dv * val * s * (1.0 - s)
        return [[dval, dgate]], [jnp.concatenate([_colsum(dval), _colsum(dgate)], axis=1)]

    return _rowwise(name, fn, [(dsg, w, 0), (gv, 2 * w, 0)], [b_glu], [(2 * w, BF16)], [2 * w], 512, deps=deps)


def _adamw(name, w, g, m, v, tm=256):
    c = w.shape[1]

    def fn(iv, pv):
        wv, gv, mv, vv = iv
        mn = ADAM_B1 * mv + (1.0 - ADAM_B1) * gv
        vn = ADAM_B2 * vv + (1.0 - ADAM_B2) * (gv * gv)
        m_hat = mn / (1.0 - ADAM_B1 ** ADAM_STEP)
        v_hat = vn / (1.0 - ADAM_B2 ** ADAM_STEP)
        delta = -ADAM_LR * (m_hat / (jnp.sqrt(v_hat) + ADAM_EPS) + ADAM_WD * wv)
        return [delta, mn, vn], []

    return _rowwise(name, fn, [(w, c, 0), (g, c, 0), (m, c, 0), (v, c, 0)], [], [(c, F32)] * 3, [], _pick(w.shape[0], tm, 8))


def _adamw_small(name, w, g8, m, v):
    r = w.shape[0]

    def body(w_ref, g_ref, m_ref, v_ref, go_ref, d_ref, mo_ref, vo_ref):
        gv = g_ref[0:r, :]
        for k in range(1, N_DEV):
            gv = gv + g_ref[k * r:(k + 1) * r, :]
        mn = ADAM_B1 * m_ref[...] + (1.0 - ADAM_B1) * gv
        vn = ADAM_B2 * v_ref[...] + (1.0 - ADAM_B2) * (gv * gv)
        m_hat = mn / (1.0 - ADAM_B1 ** ADAM_STEP)
        v_hat = vn / (1.0 - ADAM_B2 ** ADAM_STEP)
        go_ref[...] = gv
        d_ref[...] = -ADAM_LR * (m_hat / (jnp.sqrt(v_hat) + ADAM_EPS) + ADAM_WD * w_ref[...])
        mo_ref[...] = mn
        vo_ref[...] = vn

    return _pcall(body, name=name, out_shape=[jax.ShapeDtypeStruct((r, 128), F32)] * 4,
                  compiler_params=pltpu.CompilerParams(vmem_limit_bytes=VMEM_LIMIT))(w, g8, m, v)


def _pool_fwd(name, uin, pool_w, pool_b, pool_scale, pw, tm=512):
    t = uin.shape[0]
    tm = min(tm, t)
    ng = len(POOL_WINDOWS)
    gw = pw // ng

    def body(u_ref, w_ref, b_ref, s_ref, p_ref, z_ref, ext):
        i = pl.program_id(0)

        @pl.when(i == 0)
        def _():
            ext[0:POOL_HALO, :] = jnp.zeros((POOL_HALO, pw), F32)

        u = u_ref[...]
        ext[POOL_HALO:POOL_HALO + tm, :] = u
        pos = i * tm + lax.broadcasted_iota(jnp.int32, (tm, 1), 0)
        for k, win in enumerate(POOL_WINDOWS):
            cols = slice(k * gw, (k + 1) * gw)
            acc = u[:, cols]
            for j in range(1, win):
                acc = acc + ext[POOL_HALO - j:POOL_HALO - j + tm, cols]
            cnt = jnp.minimum(pos + 1, win).astype(F32)
            z = acc / cnt - u[:, cols]
            zp = jnp.dot(z.astype(BF16), w_ref[k].astype(BF16), preferred_element_type=F32) + b_ref[:, cols]
            p_ref[:, cols] = (zp * s_ref[:, cols]).astype(BF16)
            z_ref[:, cols] = z.astype(BF16)
        ext[0:POOL_HALO, :] = u[tm - POOL_HALO:tm, :]

    return _pcall(
        body, name=name, grid=(t // tm,),
        in_specs=[pl.BlockSpec((tm, pw), lambda i: (i, 0)), pl.BlockSpec(pool_w.shape, lambda i: (0, 0, 0)),
                  pl.BlockSpec(pool_b.shape, lambda i: (0, 0)), pl.BlockSpec(pool_scale.shape, lambda i: (0, 0))],
        out_specs=[pl.BlockSpec((tm, pw), lambda i: (i, 0))] * 2,
        out_shape=[jax.ShapeDtypeStruct((t, pw), BF16)] * 2,
        scratch_shapes=[pltpu.VMEM((POOL_HALO + tm, pw), F32)],
        compiler_params=_params("arbitrary"))(uin, pool_w, pool_b, pool_scale)


def _pool_bwd(name, dp, z, pool_w, pool_b, pool_scale, tm=512):
    t, pw = z.shape
    tm = min(tm, t)
    nb = t // tm
    ng = len(POOL_WINDOWS)
    gw = pw // ng

    def body(dp_ref, z_ref, w_ref, b_ref, s_ref, du_ref, dw_ref, db_ref, ds_ref, ext):
        i = pl.program_id(0)

        @pl.when(i == 0)
        def _():
            ext[tm:tm + POOL_HALO, :] = jnp.zeros((POOL_HALO, pw), F32)
            dw_ref[...] = jnp.zeros_like(dw_ref)
            db_ref[...] = jnp.zeros_like(db_ref)
            ds_ref[...] = jnp.zeros_like(ds_ref)

        pos = (nb - 1 - i) * tm + lax.broadcasted_iota(jnp.int32, (tm, 1), 0)
        for k, win in enumerate(POOL_WINDOWS):
            cols = slice(k * gw, (k + 1) * gw)
            zk = z_ref[:, cols]
            dpk = dp_ref[:, cols].astype(F32)
            wk = w_ref[k].astype(BF16)
            zp = jnp.dot(zk, wk, preferred_element_type=F32) + b_ref[:, cols]
            ds_ref[:, cols] += _colsum(dpk * zp)
            dzp = dpk * s_ref[:, cols]
            db_ref[:, cols] += _colsum(dzp)
            dzpb = dzp.astype(BF16)
            dz = lax.dot_general(dzpb, wk, (((1,), (1,)), ((), ())), preferred_element_type=F32)
            dw_ref[k] += lax.dot_general(zk, dzpb, (((0,), (0,)), ((), ())), preferred_element_type=F32)
            cnt = jnp.minimum(pos + 1, win).astype(F32)
            r = dz / cnt
            ext[0:tm, cols] = r
            acc = r - dz
            for j in range(1, win):
                acc = acc + ext[j:j + tm, cols]
            du_ref[:, cols] = acc.astype(BF16)
        ext[tm:tm + POOL_HALO, :] = ext[0:POOL_HALO, :]

    rev = lambda i: (nb - 1 - i, 0)
    return _pcall(
        body, name=name, grid=(nb,),
        in_specs=[pl.BlockSpec((tm, pw), rev), pl.BlockSpec((tm, pw), rev), pl.BlockSpec(pool_w.shape, lambda i: (0, 0, 0)),
                  pl.BlockSpec(pool_b.shape, lambda i: (0, 0)), pl.BlockSpec(pool_scale.shape, lambda i: (0, 0))],
        out_specs=[pl.BlockSpec((tm, pw), rev), pl.BlockSpec(pool_w.shape, lambda i: (0, 0, 0)),
                   pl.BlockSpec((1, pw), lambda i: (0, 0)), pl.BlockSpec((1, pw), lambda i: (0, 0))],
        out_shape=[jax.ShapeDtypeStruct((t, pw), BF16), jax.ShapeDtypeStruct(pool_w.shape, F32),
                   jax.ShapeDtypeStruct((1, pw), F32), jax.ShapeDtypeStruct((1, pw), F32)],
        scratch_shapes=[pltpu.VMEM((tm + POOL_HALO, pw), F32)],
        compiler_params=_params("arbitrary"))(dp, z, pool_w, pool_b, pool_scale)


def _ssm_disc(lrl, li, ldt):
    lr = -jnp.exp(lrl)
    dt = jnp.exp(ldt)
    mag = jnp.exp(lr * dt)
    ang = li * dt
    ab_re = mag * jnp.cos(ang)
    ab_im = mag * jnp.sin(ang)
    num_re = ab_re - 1.0
    num_im = ab_im
    den = lr * lr + li * li
    f_re = (num_re * lr + num_im * li) / den
    f_im = (num_im * lr - num_re * li) / den
    return lr, dt, mag, ang, ab_re, ab_im, num_re, num_im, den, f_re, f_im


def _ssm_prep(name, lrl, li, ldt, b_re, b_im):
    gn, h = b_re.shape

    def body(lrl_ref, li_ref, ldt_ref, br_ref, bi_ref, lrdt_ref, ang_ref, bbr_ref, bbi_ref):
        lr, dt, _, ang, _, _, _, _, _, f_re, f_im = _ssm_disc(lrl_ref[...], li_ref[...], ldt_ref[...])
        lrdt_ref[...] = lr * dt
        ang_ref[...] = ang
        br, bi = br_ref[...], bi_ref[...]
        bbr_ref[...] = f_re * br - f_im * bi
        bbi_ref[...] = f_re * bi + f_im * br

    col = jax.ShapeDtypeStruct((gn, 1), F32)
    mat = jax.ShapeDtypeStruct((gn, h), F32)
    return _pcall(body, name=name, out_shape=[col, col, mat, mat])(lrl, li, ldt, b_re, b_im)


def _ssm_param_bwd(name, lrl, li, ldt, b_re, b_im, g_abre, g_abim, g_bbre, g_bbim):
    gn, h = b_re.shape

    def body(lrl_ref, li_ref, ldt_ref, br_ref, bi_ref, gar_ref, gai_ref, gbr_ref, gbi_ref,
             dlrl_ref, dli_ref, dldt_ref, dbr_ref, dbi_ref):
        li_v = li_ref[...]
        lr, dt, mag, ang, ab_re, ab_im, num_re, num_im, den, f_re, f_im = _ssm_disc(lrl_ref[...], li_v, ldt_ref[...])
        br, bi = br_ref[...], bi_ref[...]
        gbr, gbi = gbr_ref[...], gbi_ref[...]
        g_fre = jnp.sum(gbr * br + gbi * bi, axis=1, keepdims=True)
        g_fim = jnp.sum(gbi * br - gbr * bi, axis=1, keepdims=True)
        dbr_ref[...] = gbr * f_re + gbi * f_im
        dbi_ref[...] = gbi * f_re - gbr * f_im
        g_num_re = (g_fre * lr - g_fim * li_v) / den
        g_num_im = (g_fre * li_v + g_fim * lr) / den
        g_den = -(g_fre * f_re + g_fim * f_im) / den
        g_lr = (g_fre * num_re + g_fim * num_im) / den + g_den * 2.0 * lr
        g_li = (g_fre * num_im - g_fim * num_re) / den + g_den * 2.0 * li_v
        g_are = gar_ref[...] + g_num_re
        g_aim = gai_ref[...] + g_num_im
        g_mag = g_are * jnp.cos(ang) + g_aim * jnp.sin(ang)
        g_ang = g_aim * ab_re - g_are * ab_im
        g_lrdt = g_mag * mag
        g_lr = g_lr + g_lrdt * dt
        g_dt = g_lrdt * lr + g_ang * li_v
        g_li = g_li + g_ang * dt
        dlrl_ref[...] = g_lr * lr
        dli_ref[...] = g_li
        dldt_ref[...] = g_dt * dt

    col = jax.ShapeDtypeStruct((gn, 1), F32)
    mat = jax.ShapeDtypeStruct((gn, h), F32)
    return _pcall(body, name=name, out_shape=[col, col, col, mat, mat])(lrl, li, ldt, b_re, b_im, g_abre, g_abim, g_bbre, g_bbim)


def _pow_rows(lrdt, ang, k):
    mag = jnp.exp(k * lrdt)
    return mag * jnp.cos(k * ang), mag * jnp.sin(k * ang)


def _ssm_chunk(t):
    return 256 if t >= 2048 else 128


def _ssm_fwd(name, uin, lrdt, ang, bb_re, bb_im, cc_re, cc_im, d_skip, sw):
    t = uin.shape[0]
    gn = lrdt.shape[1]
    lc = _ssm_chunk(t)
    nsteps = int(math.log2(lc))
    pad = lc // 2
    ub, sb = sw // SSM_BLOCKS, gn // SSM_BLOCKS

    def body(u_ref, lrdt_ref, ang_ref, bbr_ref, bbi_ref, ccr_ref, cci_ref, d_ref, y_ref, ge_ref, sre_ref, sim_ref,
             p_re, p_im, a_re, a_im, b_re, b_im, car_re, car_im):
        i = pl.program_id(0)
        lrdt_v, ang_v = lrdt_ref[...], ang_ref[...]

        @pl.when(i == 0)
        def _():
            k = (lax.broadcasted_iota(jnp.int32, (lc, 1), 0) + 1).astype(F32)
            pr, pi = _pow_rows(lrdt_v, ang_v, k)
            p_re[...] = pr
            p_im[...] = pi
            zeros = jnp.zeros((pad, gn), F32)
            a_re[0:pad, :] = zeros
            a_im[0:pad, :] = zeros
            b_re[0:pad, :] = zeros
            b_im[0:pad, :] = zeros
            car_re[...] = jnp.zeros_like(car_re)
            car_im[...] = jnp.zeros_like(car_im)

        u = u_ref[...]
        ubf = u.astype(BF16)
        for q in range(SSM_BLOCKS):
            uq = ubf[:, q * ub:(q + 1) * ub]
            a_re[pad:pad + lc, q * sb:(q + 1) * sb] = jnp.dot(uq, bbr_ref[q], preferred_element_type=F32)
            a_im[pad:pad + lc, q * sb:(q + 1) * sb] = jnp.dot(uq, bbi_ref[q], preferred_element_type=F32)
        bufs = [(a_re, a_im), (b_re, b_im)]
        for q in range(SSM_BLOCKS):
            cols = slice(q * sb, (q + 1) * sb)
            for j in range(nsteps):
                dd = 1 << j
                (s_re, s_im), (d_re, d_im) = bufs[j % 2], bufs[(j + 1) % 2]
                pr, pi = _pow_rows(lrdt_v[:, cols], ang_v[:, cols], float(dd))
                cr, ci = s_re[pad:pad + lc, cols], s_im[pad:pad + lc, cols]
                hr, hi = s_re[pad - dd:pad - dd + lc, cols], s_im[pad - dd:pad - dd + lc, cols]
                d_re[pad:pad + lc, cols] = cr + (pr * hr - pi * hi)
                d_im[pad:pad + lc, cols] = ci + (pr * hi + pi * hr)
            f_re, f_im = bufs[nsteps % 2]
            cr, ci = car_re[:, cols], car_im[:, cols]
            pr, pi = p_re[:, cols], p_im[:, cols]
            sr = f_re[pad:pad + lc, cols] + (pr * cr - pi * ci)
            si = f_im[pad:pad + lc, cols] + (pr * ci + pi * cr)
            sre_ref[:, cols] = sr
            sim_ref[:, cols] = si
            car_re[:, cols] = sr[lc - 1:lc, :]
            car_im[:, cols] = si[lc - 1:lc, :]
            ycols = slice(q * ub, (q + 1) * ub)
            y = (jnp.dot(sr.astype(BF16), ccr_ref[q], preferred_element_type=F32)
                 - jnp.dot(si.astype(BF16), cci_ref[q], preferred_element_type=F32)
                 + d_ref[:, ycols] * u[:, ycols])
            y_ref[:, ycols] = y
            ge_ref[:, ycols] = (0.5 * y * (1.0 + lax.erf(y * (1.0 / math.sqrt(2.0))))).astype(BF16)

    cblk = sw // sw
    row = lambda i: (0, 0)
    blk3 = lambda i: (0, 0, 0)
    return _pcall(
        body, name=name, grid=(t // lc,),
        in_specs=[pl.BlockSpec((lc, sw), lambda i: (i, cblk)), pl.BlockSpec((1, gn), row), pl.BlockSpec((1, gn), row),
                  pl.BlockSpec(bb_re.shape, blk3), pl.BlockSpec(bb_im.shape, blk3),
                  pl.BlockSpec(cc_re.shape, blk3), pl.BlockSpec(cc_im.shape, blk3), pl.BlockSpec((1, sw), row)],
        out_specs=[pl.BlockSpec((lc, sw), lambda i: (i, 0)), pl.BlockSpec((lc, sw), lambda i: (i, 0)),
                   pl.BlockSpec((lc, gn), lambda i: (i, 0)), pl.BlockSpec((lc, gn), lambda i: (i, 0))],
        out_shape=[jax.ShapeDtypeStruct((t, sw), F32), jax.ShapeDtypeStruct((t, sw), BF16),
                   jax.ShapeDtypeStruct((t, gn), F32), jax.ShapeDtypeStruct((t, gn), F32)],
        scratch_shapes=[pltpu.VMEM((lc, gn), F32), pltpu.VMEM((lc, gn), F32),
                        pltpu.VMEM((pad + lc, gn), F32), pltpu.VMEM((pad + lc, gn), F32),
                        pltpu.VMEM((pad + lc, gn), F32), pltpu.VMEM((pad + lc, gn), F32),
                        pltpu.VMEM((1, gn), F32), pltpu.VMEM((1, gn), F32)],
        compiler_params=_params("arbitrary"))(uin, lrdt, ang, bb_re, bb_im, cc_re, cc_im, d_skip)


def _ssm_bwd(name, dge, y, uin, s_re, s_im, lrdt, ang, bbt_re, bbt_im, cct_re, cct_im, d_skip, sw):
    t = uin.shape[0]
    gn = lrdt.shape[1]
    lc = _ssm_chunk(t)
    nb = t // lc
    nsteps = int(math.log2(lc))
    pad = lc // 2
    ub, sb = sw // SSM_BLOCKS, gn // SSM_BLOCKS
    tail = 8

    def body(dge_ref, y_ref, u_ref, sre_ref, sim_ref, tre_ref, tim_ref, lrdt_ref, ang_ref, btr_ref, bti_ref, ctr_ref, cti_ref,
             d_ref, du_ref, dar_ref, dai_ref, dbr_ref, dbi_ref, dcr_ref, dci_ref, dd_ref,
             q_re, q_im, a_re, a_im, b_re, b_im, sp_re, sp_im, car_re, car_im):
        i = pl.program_id(0)
        lrdt_v, ang_v = lrdt_ref[...], ang_ref[...]

        @pl.when(i == 0)
        def _():
            k = (lc - lax.broadcasted_iota(jnp.int32, (lc, 1), 0)).astype(F32)
            qr, qi = _pow_rows(lrdt_v, ang_v, k)
            q_re[...] = qr
            q_im[...] = qi
            zeros = jnp.zeros((pad, gn), F32)
            a_re[lc:lc + pad, :] = zeros
            a_im[lc:lc + pad, :] = zeros
            b_re[lc:lc + pad, :] = zeros
            b_im[lc:lc + pad, :] = zeros
            car_re[...] = jnp.zeros_like(car_re)
            car_im[...] = jnp.zeros_like(car_im)
            for r in (dar_ref, dai_ref, dbr_ref, dbi_ref, dcr_ref, dci_ref, dd_ref):
                r[...] = jnp.zeros_like(r)

        first = (i == nb - 1).astype(F32)
        sp_re[0:tail, :] = tre_ref[...] * (1.0 - first)
        sp_im[0:tail, :] = tim_ref[...] * (1.0 - first)
        sp_re[tail:tail + lc, :] = sre_ref[...]
        sp_im[tail:tail + lc, :] = sim_ref[...]

        yv = y_ref[...]
        u = u_ref[...]
        cdf = 0.5 * (1.0 + lax.erf(yv * (1.0 / math.sqrt(2.0))))
        pdf = jnp.exp(-0.5 * yv * yv) * (1.0 / math.sqrt(2.0 * math.pi))
        dy = dge_ref[...].astype(F32) * (cdf + yv * pdf)
        dd_ref[...] += _colsum(dy * u)
        dyb = dy.astype(BF16)
        ubf = u.astype(BF16)
        for q in range(SSM_BLOCKS):
            dq = dyb[:, q * ub:(q + 1) * ub]
            a_re[0:lc, q * sb:(q + 1) * sb] = jnp.dot(dq, ctr_ref[q], preferred_element_type=F32)
            a_im[0:lc, q * sb:(q + 1) * sb] = -jnp.dot(dq, cti_ref[q], preferred_element_type=F32)
        bufs = [(a_re, a_im), (b_re, b_im)]
        tn_dims = (((0,), (0,)), ((), ()))
        for q in range(SSM_BLOCKS):
            cols = slice(q * sb, (q + 1) * sb)
            ycols = slice(q * ub, (q + 1) * ub)
            for j in range(nsteps):
                ds = 1 << j
                (s_r, s_i), (d_r, d_i) = bufs[j % 2], bufs[(j + 1) % 2]
                pr, pi = _pow_rows(lrdt_v[:, cols], ang_v[:, cols], float(ds))
                cr, ci = s_r[0:lc, cols], s_i[0:lc, cols]
                hr, hi = s_r[ds:ds + lc, cols], s_i[ds:ds + lc, cols]
                d_r[0:lc, cols] = cr + (pr * hr + pi * hi)
                d_i[0:lc, cols] = ci + (pr * hi - pi * hr)
            f_r, f_i = bufs[nsteps % 2]
            cr, ci = car_re[:, cols], car_im[:, cols]
            qr, qi = q_re[:, cols], q_im[:, cols]
            lam_r = f_r[0:lc, cols] + (qr * cr + qi * ci)
            lam_i = f_i[0:lc, cols] + (qr * ci - qi * cr)
            car_re[:, cols] = lam_r[0:1, :]
            car_im[:, cols] = lam_i[0:1, :]
            pvr, pvi = sp_re[tail - 1:tail - 1 + lc, cols], sp_im[tail - 1:tail - 1 + lc, cols]
            dar_ref[:, cols] += _colsum(lam_r * pvr + lam_i * pvi)
            dai_ref[:, cols] += _colsum(lam_i * pvr - lam_r * pvi)
            lrb, lib = lam_r.astype(BF16), lam_i.astype(BF16)
            du_ref[:, ycols] = (jnp.dot(lrb, btr_ref[q], preferred_element_type=F32)
                                + jnp.dot(lib, bti_ref[q], preferred_element_type=F32)
                                + d_ref[:, ycols] * dy[:, ycols]).astype(BF16)
            uq = ubf[:, ycols]
            dbr_ref[q] += lax.dot_general(uq, lrb, tn_dims, preferred_element_type=F32)
            dbi_ref[q] += lax.dot_general(uq, lib, tn_dims, preferred_element_type=F32)
            dq = dyb[:, ycols]
            dcr_ref[q] += lax.dot_general(sre_ref[:, cols].astype(BF16), dq, tn_dims, preferred_element_type=F32)
            dci_ref[q] -= lax.dot_general(sim_ref[:, cols].astype(BF16), dq, tn_dims, preferred_element_type=F32)

    cblk = 1
    rev = lambda i: (nb - 1 - i, 0)
    revu = lambda i: (nb - 1 - i, cblk)
    tailmap = lambda i: (jnp.maximum((nb - 1 - i) * (lc // tail) - 1, 0), 0)
    row = lambda i: (0, 0)
    blk3 = lambda i: (0, 0, 0)
    return _pcall(
        body, name=name, grid=(nb,),
        in_specs=[pl.BlockSpec((lc, sw), rev), pl.BlockSpec((lc, sw), rev), pl.BlockSpec((lc, sw), revu),
                  pl.BlockSpec((lc, gn), rev), pl.BlockSpec((lc, gn), rev),
                  pl.BlockSpec((tail, gn), tailmap), pl.BlockSpec((tail, gn), tailmap),
                  pl.BlockSpec((1, gn), row), pl.BlockSpec((1, gn), row),
                  pl.BlockSpec(bbt_re.shape, blk3), pl.BlockSpec(bbt_im.shape, blk3),
                  pl.BlockSpec(cct_re.shape, blk3), pl.BlockSpec(cct_im.shape, blk3), pl.BlockSpec((1, sw), row)],
        out_specs=[pl.BlockSpec((lc, sw), rev), pl.BlockSpec((1, gn), row), pl.BlockSpec((1, gn), row),
                   pl.BlockSpec((SSM_BLOCKS, ub, sb), blk3), pl.BlockSpec((SSM_BLOCKS, ub, sb), blk3),
                   pl.BlockSpec((SSM_BLOCKS, sb, ub), blk3), pl.BlockSpec((SSM_BLOCKS, sb, ub), blk3),
                   pl.BlockSpec((1, sw), row)],
        out_shape=[jax.ShapeDtypeStruct((t, sw), BF16), jax.ShapeDtypeStruct((1, gn), F32), jax.ShapeDtypeStruct((1, gn), F32),
                   jax.ShapeDtypeStruct((SSM_BLOCKS, ub, sb), F32), jax.ShapeDtypeStruct((SSM_BLOCKS, ub, sb), F32),
                   jax.ShapeDtypeStruct((SSM_BLOCKS, sb, ub), F32), jax.ShapeDtypeStruct((SSM_BLOCKS, sb, ub), F32),
                   jax.ShapeDtypeStruct((1, sw), F32)],
        scratch_shapes=[pltpu.VMEM((lc, gn), F32), pltpu.VMEM((lc, gn), F32),
                        pltpu.VMEM((lc + pad, gn), F32), pltpu.VMEM((lc + pad, gn), F32),
                        pltpu.VMEM((lc + pad, gn), F32), pltpu.VMEM((lc + pad, gn), F32),
                        pltpu.VMEM((tail + lc, gn), F32), pltpu.VMEM((tail + lc, gn), F32),
                        pltpu.VMEM((1, gn), F32), pltpu.VMEM((1, gn), F32)],
        compiler_params=_params("arbitrary"))(dge, y, uin, s_re, s_im, s_re, s_im, lrdt, ang, bbt_re, bbt_im, cct_re, cct_im, d_skip)


def _blockdiag_b(bb, sw):
    gpb = (sw // SSM_GROUP) // SSM_BLOCKS
    b4 = bb.reshape(SSM_BLOCKS, gpb, SSM_STATE, SSM_GROUP)
    eye = jnp.eye(gpb, dtype=bb.dtype)
    out = jnp.einsum('qgnh,gk->qghkn', b4, eye)
    return out.reshape(SSM_BLOCKS, gpb * SSM_GROUP, gpb * SSM_STATE)


def _blockdiag_c(cc, sw):
    gpb = (sw // SSM_GROUP) // SSM_BLOCKS
    c4 = cc.reshape(SSM_BLOCKS, gpb, SSM_GROUP, SSM_STATE)
    eye = jnp.eye(gpb, dtype=cc.dtype)
    out = jnp.einsum('qghn,gk->qgnkh', c4, eye)
    return out.reshape(SSM_BLOCKS, gpb * SSM_STATE, gpb * SSM_GROUP)


def _diag_of_b(dbb, sw):
    gpb = (sw // SSM_GROUP) // SSM_BLOCKS
    d5 = dbb.reshape(SSM_BLOCKS, gpb, SSM_GROUP, gpb, SSM_STATE)
    return jnp.einsum('qghgn->qgnh', d5).reshape(SSM_BLOCKS * gpb * SSM_STATE, SSM_GROUP)


def _diag_of_c(dcc, sw):
    gpb = (sw // SSM_GROUP) // SSM_BLOCKS
    d5 = dcc.reshape(SSM_BLOCKS, gpb, SSM_STATE, gpb, SSM_GROUP)
    return jnp.einsum('qgngh->qghn', d5).reshape(SSM_BLOCKS * gpb, SSM_GROUP, SSM_STATE)


def _ada_fwd(name, c_all, w_sh, b_sh):
    nb, d = c_all.shape
    ncol = w_sh.shape[1]
    tn = _pick(ncol, 768)

    def body(c_ref, w_ref, b_ref, o_ref):
        cv = c_ref[...]
        sil = cv * _sigmoid(cv)
        o_ref[...] = jnp.dot(sil, w_ref[...], preferred_element_type=F32, precision=lax.Precision.HIGHEST) + b_ref[...]

    return _pcall(body, name=name, grid=(ncol // tn,),
                  in_specs=[pl.BlockSpec((nb, d), lambda j: (0, 0)), pl.BlockSpec((d, tn), lambda j: (0, j)),
                            pl.BlockSpec((1, tn), lambda j: (0, j))],
                  out_specs=pl.BlockSpec((nb, tn), lambda j: (0, j)),
                  out_shape=jax.ShapeDtypeStruct((nb, ncol), F32), compiler_params=_params("parallel"))(c_all, w_sh, b_sh)


def _ada_bwd(name, c_all, dmod_sh):
    nb, d = c_all.shape
    ncol = dmod_sh.shape[1]
    tn = _pick(ncol, 768)

    def body(c_ref, g_ref, o_ref):
        cv = c_ref[...]
        sil = cv * _sigmoid(cv)
        o_ref[...] = lax.dot_general(sil, g_ref[...], (((0,), (0,)), ((), ())), preferred_element_type=F32,
                                     precision=lax.Precision.HIGHEST)

    return _pcall(body, name=name, grid=(ncol // tn,),
                  in_specs=[pl.BlockSpec((nb, d), lambda j: (0, 0)), pl.BlockSpec((nb, tn), lambda j: (0, j))],
                  out_specs=pl.BlockSpec((d, tn), lambda j: (0, j)),
                  out_shape=jax.ShapeDtypeStruct((d, ncol), F32), compiler_params=_params("parallel"))(c_all, dmod_sh)


def _place():
    return lax.axis_index("x"), lax.axis_index("y"), lax.axis_index("c")


def _allgather_small(name, blk, deps=()):
    m_per, n = blk.shape

    def body(x_ref, *rest):
        out_ref, send_sems, recv_sems, local_sem = rest[len(deps):]
        x, y, c = _place()
        me, sibling = (x, y, c), (x, y, 1 - c)
        chips = [(1 - x, y), (x, 1 - y), (1 - x, 1 - y)]

        def rows(px, py, pc):
            return out_ref.at[pl.ds((4 * px + 2 * py + pc) * m_per, m_per), :]

        def copy(k, block, to, src=None):
            return pltpu.make_async_remote_copy(
                src_ref=rows(*block) if src is None else src, dst_ref=rows(*block),
                send_sem=send_sems.at[k], recv_sem=recv_sems.at[k], device_id=to, device_id_type=MESH)

        mine = pltpu.make_async_copy(x_ref, rows(*me), local_sem)
        mine.start()
        first = [copy(0, me, sibling, src=x_ref)]
        first += [copy(1 + j, me, (*chip, c), src=x_ref) for j, chip in enumerate(chips)]
        for cp in first:
            cp.start()
        passed = [copy(4 + j, (*chip, c), sibling) for j, chip in enumerate(chips)]
        for j, chip in enumerate(chips):
            copy(1 + j, (*chip, c), me).wait_recv()
            passed[j].start()
        copy(0, sibling, me).wait_recv()
        for j, chip in enumerate(chips):
            copy(4 + j, (*chip, 1 - c), me).wait_recv()
        for cp in first + passed:
            cp.wait_send()
        mine.wait()

    return _pcall(body, name=name, out_shape=jax.ShapeDtypeStruct((N_DEV * m_per, n), blk.dtype),
                  in_specs=[pl.BlockSpec(memory_space=pltpu.VMEM)] + [ANY_SPEC] * len(deps),
                  out_specs=pl.BlockSpec(memory_space=pltpu.VMEM),
                  scratch_shapes=[pltpu.SemaphoreType.DMA((7,)), pltpu.SemaphoreType.DMA((7,)), pltpu.SemaphoreType.DMA],
                  compiler_params=pltpu.CompilerParams(vmem_limit_bytes=VMEM_LIMIT))(blk, *deps)


def _other_chips(x, y):
    return [(1 - x, y), (x, 1 - y), (1 - x, 1 - y)]


def _place_shards(shards, s_me):
    return [lax.dynamic_update_slice(lax.empty((N_CHIPS,) + s.shape, s.dtype), s[None], (s_me, 0, 0)) for s in shards]


def _ag_copy(src, land, send, recv, wi, j, chip, x, y, c):
    hr = src.shape[0] // 2
    half = pl.ds(pl.multiple_of(c * hr, 16), hr)
    return pltpu.make_async_remote_copy(
        src_ref=src.at[half, :], dst_ref=land.at[2 * x + y, half, :], send_sem=send.at[3 * wi + j], recv_sem=recv.at[3 * wi + j],
        device_id=(chip[0], chip[1], c), device_id_type=MESH)


def _ag_start(name, shards, lands, groups, deps=()):
    nw, ng, nd = len(shards), len(groups), len(deps)

    def body(*refs):
        src, land = refs[:nw], refs[nw:2 * nw]
        sems = refs[2 * nw + nd:2 * nw + nd + 2 * ng]
        token = refs[-1]
        x, y, c = _place()
        for gi, grp in enumerate(groups):
            for wi, w in enumerate(grp):
                for j, chip in enumerate(_other_chips(x, y)):
                    _ag_copy(src[w], land[w], sems[2 * gi], sems[2 * gi + 1], wi, j, chip, x, y, c).start()
        token[...] = jnp.zeros_like(token)

    sem_shapes = []
    for grp in groups:
        sem_shapes += [pltpu.SemaphoreType.DMA((3 * len(grp),))] * 2
    out_shape = sem_shapes + [pltpu.HBM(s.shape, s.dtype) for s in shards] + [pltpu.HBM(l.shape, l.dtype) for l in lands]
    out_shape += [jax.ShapeDtypeStruct((8, 128), F32)]
    res = _pcall(body, name=name, out_shape=out_shape, in_specs=[HBM_SPEC] * (2 * nw) + [ANY_SPEC] * nd,
                 out_specs=[SEM_SPEC] * (2 * ng) + [HBM_SPEC] * (2 * nw) + [pl.BlockSpec(memory_space=pltpu.VMEM)],
                 input_output_aliases={i: 2 * ng + i for i in range(2 * nw)},
                 compiler_params=pltpu.CompilerParams(has_side_effects=EFFECT))(
                     *[_hbm(s) for s in shards], *[_hbm(l) for l in lands], *deps)
    sems = [(res[2 * gi], res[2 * gi + 1]) for gi in range(ng)]
    return sems, list(res[2 * ng:2 * ng + nw]), list(res[2 * ng + nw:2 * ng + 2 * nw]), res[-1]


def _ag_wait(name, shards, lands, send, recv, after):
    n = len(shards)

    def body(*refs):
        src, land = refs[:n], refs[n:2 * n]
        send_sem, recv_sem = refs[2 * n], refs[2 * n + 1]
        x, y, c = _place()
        for wi in range(n):
            for j, chip in enumerate(_other_chips(x, y)):
                _ag_copy(src[wi], land[wi], send_sem, recv_sem, wi, j, chip, x, y, c).wait_send()
                _ag_copy(src[wi], land[wi], send_sem, recv_sem, wi, j, chip, chip[0], chip[1], c).wait_recv()

    res = _pcall(body, name=name, out_shape=[pltpu.HBM(a.shape, a.dtype) for a in list(shards) + list(lands)],
                 in_specs=[HBM_SPEC] * (2 * n) + [SEM_SPEC, SEM_SPEC, ANY_SPEC], out_specs=[HBM_SPEC] * (2 * n),
                 input_output_aliases={i: i for i in range(2 * n)},
                 compiler_params=pltpu.CompilerParams(has_side_effects=EFFECT))(*shards, *lands, send, recv, after)
    return list(res[n:])


def _ag_forward(name, lands):
    n = len(lands)

    def body(*refs):
        out = refs[n:2 * n]
        send, recv = refs[2 * n], refs[2 * n + 1]
        x, y, c = _place()
        sib = (x, y, 1 - c)
        cps = []
        for wi in range(n):
            hr = out[wi].shape[1] // 2
            for j, (cx, cy) in enumerate(_other_chips(x, y)):
                got = out[wi].at[2 * cx + cy, pl.ds(pl.multiple_of(c * hr, 16), hr), :]
                cp = pltpu.make_async_remote_copy(src_ref=got, dst_ref=got, send_sem=send.at[3 * wi + j], recv_sem=recv.at[3 * wi + j],
                                                  device_id=sib, device_id_type=MESH)
                cp.start()
                cps.append(cp)
        for wi in range(n):
            hr = out[wi].shape[1] // 2
            for j, (cx, cy) in enumerate(_other_chips(x, y)):
                got = out[wi].at[2 * cx + cy, pl.ds(pl.multiple_of((1 - c) * hr, 16), hr), :]
                pltpu.make_async_remote_copy(src_ref=got, dst_ref=got, send_sem=send.at[3 * wi + j], recv_sem=recv.at[3 * wi + j],
                                             device_id=sib, device_id_type=MESH).wait_recv()
        for cp in cps:
            cp.wait_send()

    res = _pcall(body, name=name, out_shape=[jax.ShapeDtypeStruct(l.shape, l.dtype) for l in lands],
                 in_specs=[ANY_SPEC] * n, out_specs=[ANY_SPEC] * n, input_output_aliases={i: i for i in range(n)},
                 scratch_shapes=[pltpu.SemaphoreType.DMA((3 * n,)), pltpu.SemaphoreType.DMA((3 * n,))])(*lands)
    return list(res)


def _peers(x, y, c):
    offs = [(dx, dy, dc) for dx in (0, 1) for dy in (0, 1) for dc in (0, 1)][1:]
    return [(1 - x if dx else x, 1 - y if dy else y, 1 - c if dc else c) for dx, dy, dc in offs]


def _rs_copy(g_ref, land_ref, send, recv, wi, k, to, sender):
    hr = g_ref.shape[1] // 2
    return pltpu.make_async_remote_copy(
        src_ref=g_ref.at[2 * to[0] + to[1], pl.ds(pl.multiple_of(to[2] * hr, 16), hr), :], dst_ref=land_ref.at[sender],
        send_sem=send.at[7 * wi + k], recv_sem=recv.at[7 * wi + k], device_id=to, device_id_type=MESH)


def _rs_start(name, gs):
    n = len(gs)
    lands = [lax.empty((N_DEV, g.shape[1] // 2, g.shape[2]), BF16) for g in gs]

    def body(*refs):
        g, land = refs[:n], refs[n:2 * n]
        send, recv = refs[2 * n], refs[2 * n + 1]
        token = refs[-1]
        x, y, c = _place()
        me = 4 * x + 2 * y + c
        for wi in range(n):
            for k, to in enumerate(_peers(x, y, c)):
                _rs_copy(g[wi], land[wi], send, recv, wi, k, to, me).start()
        token[...] = jnp.zeros_like(token)

    out_shape = [pltpu.SemaphoreType.DMA((7 * n,))] * 2 + [pltpu.HBM(a.shape, a.dtype) for a in list(gs) + lands]
    out_shape += [jax.ShapeDtypeStruct((8, 128), F32)]
    res = _pcall(body, name=name, out_shape=out_shape, in_specs=[HBM_SPEC] * (2 * n),
                 out_specs=[SEM_SPEC] * 2 + [HBM_SPEC] * (2 * n) + [pl.BlockSpec(memory_space=pltpu.VMEM)],
                 input_output_aliases={i: 2 + i for i in range(2 * n)},
                 compiler_params=pltpu.CompilerParams(has_side_effects=EFFECT))(
                     *[_hbm(a) for a in gs], *[_hbm(a) for a in lands])
    return res[0], res[1], list(res[2:2 + n]), list(res[2 + n:2 + 2 * n]), res[-1]


def _rs_wait(name, gs, lands, send, recv, after):
    n = len(gs)

    def body(*refs):
        g, land = refs[:n], refs[n:2 * n]
        send_sem, recv_sem = refs[2 * n], refs[2 * n + 1]
        x, y, c = _place()
        me = 4 * x + 2 * y + c
        for wi in range(n):
            for k, to in enumerate(_peers(x, y, c)):
                _rs_copy(g[wi], land[wi], send_sem, recv_sem, wi, k, to, me).wait_send()
                _rs_copy(g[wi], land[wi], send_sem, recv_sem, wi, k, (x, y, c), 4 * to[0] + 2 * to[1] + to[2]).wait_recv()

    res = _pcall(body, name=name, out_shape=[pltpu.HBM(a.shape, a.dtype) for a in list(gs) + list(lands)],
                 in_specs=[HBM_SPEC] * (2 * n) + [SEM_SPEC, SEM_SPEC, ANY_SPEC], out_specs=[HBM_SPEC] * (2 * n),
                 input_output_aliases={i: i for i in range(2 * n)},
                 compiler_params=pltpu.CompilerParams(has_side_effects=EFFECT))(*gs, *lands, send, recv, after)
    return list(res[:n]), list(res[n:])


def _bc_copy(blk_ref, land_ref, send, recv, k, to, slot):
    return pltpu.make_async_remote_copy(src_ref=blk_ref, dst_ref=land_ref.at[slot], send_sem=send.at[k], recv_sem=recv.at[k],
                                        device_id=to, device_id_type=MESH)


def _bcast_start(name, blk):
    land = lax.empty((N_DEV,) + blk.shape, blk.dtype)

    def body(blk_ref, land_ref, send, recv, blk_thru, land_thru, token):
        x, y, c = _place()
        for k, to in enumerate(_peers(x, y, c)):
            _bc_copy(blk_ref, land_ref, send, recv, k, to, 4 * x + 2 * y + c).start()
        token[...] = jnp.zeros_like(token)

    return _pcall(body, name=name,
                  out_shape=[pltpu.SemaphoreType.DMA((7,)), pltpu.SemaphoreType.DMA((7,)), pltpu.HBM(blk.shape, blk.dtype),
                             pltpu.HBM(land.shape, land.dtype), jax.ShapeDtypeStruct((8, 128), F32)],
                  in_specs=[HBM_SPEC, HBM_SPEC], out_specs=[SEM_SPEC, SEM_SPEC, HBM_SPEC, HBM_SPEC, pl.BlockSpec(memory_space=pltpu.VMEM)],
                  input_output_aliases={0: 2, 1: 3}, compiler_params=pltpu.CompilerParams(has_side_effects=EFFECT))(_hbm(blk), _hbm(land))


def _bcast_wait(name, blk, land, send, recv, after):
    def body(blk_ref, land_ref, send_sem, recv_sem, after_ref, blk_thru, land_thru):
        x, y, c = _place()
        for k, to in enumerate(_peers(x, y, c)):
            _bc_copy(blk_ref, land_ref, send_sem, recv_sem, k, to, 4 * x + 2 * y + c).wait_send()
            _bc_copy(blk_ref, land_ref, send_sem, recv_sem, k, to, 4 * to[0] + 2 * to[1] + to[2]).wait_recv()

    return _pcall(body, name=name, out_shape=[pltpu.HBM(blk.shape, blk.dtype), pltpu.HBM(land.shape, land.dtype)],
                  in_specs=[HBM_SPEC, HBM_SPEC, SEM_SPEC, SEM_SPEC, ANY_SPEC], out_specs=[HBM_SPEC, HBM_SPEC],
                  input_output_aliases={0: 0, 1: 1}, compiler_params=pltpu.CompilerParams(has_side_effects=EFFECT))(
                      blk, land, send, recv, after)[1]


def _rs_sum(name, g, land):
    _, rs, cs = g.shape
    hr = rs // 2
    ch = _pick(hr, 64, 16)

    def body(g_ref, land_ref, out_ref, recv, local_sems, sib_sems):
        x, y, c = _place()
        me = 4 * x + 2 * y + c
        cps = [pltpu.make_async_copy(g_ref.at[2 * x + y, pl.ds(pl.multiple_of(c * hr, 16), hr), :], recv.at[me], local_sems.at[7])]
        for k, (tx, ty, tc) in enumerate(_peers(x, y, c)):
            slot = 4 * tx + 2 * ty + tc
            cps.append(pltpu.make_async_copy(land_ref.at[slot], recv.at[slot], local_sems.at[k]))
        for cp in cps:
            cp.start()
        for cp in cps:
            cp.wait()
        base = pl.multiple_of(c * hr, 16)
        for r0 in range(0, hr, ch):
            acc = recv[0, r0:r0 + ch, :].astype(F32)
            for k in range(1, N_DEV):
                acc = acc + recv[k, r0:r0 + ch, :].astype(F32)
            out_ref[pl.ds(base + r0, ch), :] = acc
        half = out_ref.at[pl.ds(base, hr), :]
        sib = pltpu.make_async_remote_copy(src_ref=half, dst_ref=half, send_sem=sib_sems.at[0], recv_sem=sib_sems.at[1],
                                           device_id=(x, y, 1 - c), device_id_type=MESH)
        sib.start()
        other = out_ref.at[pl.ds(pl.multiple_of((1 - c) * hr, 16), hr), :]
        pltpu.make_async_remote_copy(src_ref=other, dst_ref=other, send_sem=sib_sems.at[0], recv_sem=sib_sems.at[1],
                                     device_id=(x, y, 1 - c), device_id_type=MESH).wait_recv()
        sib.wait_send()

    return _pcall(
        body, name=name, out_shape=jax.ShapeDtypeStruct((rs, cs), F32),
        in_specs=[ANY_SPEC, ANY_SPEC], out_specs=pl.BlockSpec(memory_space=pltpu.VMEM),
        scratch_shapes=[pltpu.VMEM((N_DEV, hr, cs), BF16), pltpu.SemaphoreType.DMA((8,)), pltpu.SemaphoreType.DMA((2,))],
        compiler_params=pltpu.CompilerParams(vmem_limit_bytes=VMEM_LIMIT))(g, land)


def _ffn_fwd(tag, x, g, sh, sc, gt, get_w_in, get_w_out, deps=()):
    t, d = x.shape
    h = _norm_fwd(tag + "_norm", x, g, sc, sh, deps=deps)
    ab = _mm(tag + "_in", h, get_w_in(h), b_stacked=True, tn=1408)
    act = _swiglu_fwd(tag + "_act", ab)
    w_out = get_w_out(act)
    blk = (512, d)

    def epi(acc, res, scale):
        return res + (0.5 * scale) * acc, acc

    x_new, y = _mm(tag + "_out", act, w_out, tm=512, tn=d, epi=epi,
                   epi_ins=[(x, blk, lambda i, j: (i, j)), (gt, (1, d), lambda i, j: (0, j))],
                   epi_outs=[((t, d), F32, blk, lambda i, j: (i, j)), ((t, d), BF16, blk, lambda i, j: (i, j))])
    return x_new, (x, h, ab, act, y)


def _ffn_bwd(tag, dx_new, saved, g, sc, gt, w_in, w_out, start_rs, deps=()):
    x, h, ab, act, y = saved
    d = x.shape[1]
    f = act.shape[1]
    dy, dgt = _resgate_bwd(tag + "_dgate", dx_new, y, gt, 0.5, deps=deps)
    dact = _mm(tag + "_dact", dy, w_out, tb=True, tn=1408)
    dw_out = _mm(tag + "_dwout", act, dy, ta=True, tm=1408, tn=d, tk=512)
    tok = start_rs("out", dw_out.reshape(N_CHIPS, f // N_CHIPS, d))
    dab = _swiglu_bwd(tag + "_dswiglu", dact, ab, deps=(tok,))
    dw_in = _mm(tag + "_dwin", h, dab, ta=True, out_stacked=True, tm=d, tn=1408, tk=512)
    tok = start_rs("in", dw_in)
    dh = _mm(tag + "_dh", dab, w_in, tb=True, b_stacked=True, tn=d, tk=1408, deps=(tok,))
    dx, dsh, dsc, dg = _norm_bwd(tag + "_dnorm", dh, x, dx_new, g, sc)
    return dx, dsh, dsc, dgt, dg


def _row(v):
    return v.reshape(1, -1)


def _pack(parts):
    cols = []
    for p in parts:
        flat = p.reshape(-1).astype(F32)
        padn = (-flat.shape[0]) % 128
        cols.append(jnp.pad(flat, (0, padn)) if padn else flat)
    flat = jnp.concatenate(cols)
    padn = (-flat.shape[0]) % 1024
    if padn:
        flat = jnp.pad(flat, (0, padn))
    return flat.reshape(-1, 128)


def _unpack(packed, shapes):
    flat = packed.reshape(-1)
    out, off = [], 0
    for s in shapes:
        n = math.prod(s)
        out.append(flat[off:off + n].reshape(s))
        off += n + ((-n) % 128)
    return out


SMALL = ['b_ada', 'g_ffn1', 'g_mix', 'pool_w', 'pool_b', 'pool_scale', 'ssm_lam_re_log', 'ssm_lam_im', 'ssm_log_dt',
         'ssm_b_re', 'ssm_b_im', 'ssm_c_re', 'ssm_c_im', 'ssm_d', 'b_glu', 'g_ffn2', 'g_final']
BIG = ['w_ffn1_in', 'w_ffn1_out', 'w_in', 'w_pool_up', 'w_glu', 'w_ssm_up', 'w_out', 'w_ffn2_in', 'w_ffn2_out']
AG_GROUPS = [[0], [1], [2, 3, 4, 5, 6], [7, 8]]
SMALL_LATE = ['b_ada', 'g_ffn1']
WEIGHTS = ['w_ada', 'b_ada', 'g_ffn1', 'w_ffn1_in', 'w_ffn1_out', 'g_mix', 'w_in', 'pool_w', 'pool_b', 'pool_scale', 'w_pool_up',
           'ssm_lam_re_log', 'ssm_lam_im', 'ssm_log_dt', 'ssm_b_re', 'ssm_b_im', 'ssm_c_re', 'ssm_c_im', 'ssm_d', 'w_glu', 'b_glu',
           'w_ssm_up', 'w_out', 'g_ffn2', 'w_ffn2_in', 'w_ffn2_out', 'g_final']


def kernel(x, c, w_ada, b_ada, g_ffn1, w_ffn1_in, w_ffn1_out, g_mix, w_in, pool_w, pool_b, pool_scale, w_pool_up, ssm_lam_re_log, ssm_lam_im, ssm_log_dt, ssm_b_re, ssm_b_im, ssm_c_re, ssm_c_im, ssm_d, w_glu, b_glu, w_ssm_up, w_out, g_ffn2, w_ffn2_in, w_ffn2_out, g_final, loss_target, m_w_ada, m_b_ada, m_g_ffn1, m_w_ffn1_in, m_w_ffn1_out, m_g_mix, m_w_in, m_pool_w, m_pool_b, m_pool_scale, m_w_pool_up, m_ssm_lam_re_log, m_ssm_lam_im, m_ssm_log_dt, m_ssm_b_re, m_ssm_b_im, m_ssm_c_re, m_ssm_c_im, m_ssm_d, m_w_glu, m_b_glu, m_w_ssm_up, m_w_out, m_g_ffn2, m_w_ffn2_in, m_w_ffn2_out, m_g_final, v_w_ada, v_b_ada, v_g_ffn1, v_w_ffn1_in, v_w_ffn1_out, v_g_mix, v_w_in, v_pool_w, v_pool_b, v_pool_scale, v_w_pool_up, v_ssm_lam_re_log, v_ssm_lam_im, v_ssm_log_dt, v_ssm_b_re, v_ssm_b_im, v_ssm_c_re, v_ssm_c_im, v_ssm_d, v_w_glu, v_b_glu, v_w_ssm_up, v_w_out, v_g_ffn2, v_w_ffn2_in, v_w_ffn2_out, v_g_final):
    args = dict(locals())
    wt = {n: args[n] for n in WEIGHTS}
    mom = {n: args["m_" + n] for n in WEIGHTS}
    var = {n: args["v_" + n] for n in WEIGHTS}

    t, d = x.shape[1], x.shape[2]
    pw = pool_b.shape[1]
    sw = ssm_d.shape[1]
    ngrp = sw // SSM_GROUP
    gn = ngrp * SSM_STATE
    xi, yi, ci = _place()
    b_me = 4 * xi + 2 * yi + ci
    s_me = 2 * xi + yi
    x2d = x[0]
    tgt = loss_target[0]

    c_all = _allgather_small("ag_c", c.reshape(8, d // 8)).reshape(N_DEV, d)
    ncol = w_ada.shape[2]
    b_sh = lax.dynamic_slice(b_ada, (0, s_me * ncol), (1, ncol))
    mod_sh = _ada_fwd("ada_fwd", c_all, w_ada[0], b_sh)
    mod_all = _allgather_small("ag_mod", mod_sh)

    shards = [wt[n][0].astype(BF16) for n in BIG]
    ag_sems, shards_t, lands_t, tok = _ag_start("ag_start", shards, _place_shards(shards, s_me), AG_GROUPS, deps=(mod_all,))
    full = {}

    def weights(gi, after):
        grp = AG_GROUPS[gi]
        if BIG[grp[0]] not in full:
            ls = _ag_wait("ag_wait%d" % gi, [shards_t[w] for w in grp], [lands_t[w] for w in grp], *ag_sems[gi], after)
            for w, l in zip(grp, _ag_forward("ag_fwd%d" % gi, ls)):
                full[BIG[w]] = l
        return full

    mod_me = jnp.concatenate([lax.dynamic_slice(mod_all, (16 * s + b_me, 0), (1, ncol)) for s in range(N_CHIPS)], axis=1)
    mod = mod_me.reshape(9, d)
    sh1, sc1, gt1, sh2, sc2, gt2, sh3, sc3, gt3 = [mod[k:k + 1] for k in range(9)]

    f = w_ffn1_out.shape[1] * N_CHIPS

    x1, sav1 = _ffn_fwd("ffn1", x2d, g_ffn1, sh1, sc1, gt1, lambda after: weights(0, after)['w_ffn1_in'],
                        lambda after: weights(1, after)['w_ffn1_out'].reshape(f, d), deps=(tok,))

    h2 =_norm_fwd("mix_norm", x1, g_mix, sc2, sh2)
    weights(2, h2)
    wo = full['w_out'].reshape(d, d)
    uin = _mm("mix_in", h2, full['w_in'], b_stacked=True, tn=768, out_dtype=F32)
    p_pool, z_pool = _pool_fwd("pool_fwd", uin, pool_w[0], pool_b, pool_scale, pw)
    y_pool = _mm("pool_up", p_pool, full['w_pool_up'], b_stacked=True, tn=256)

    col = lambda a: a.reshape(gn, 1)
    lrl_c, li_c = col(ssm_lam_re_log), col(ssm_lam_im)
    ldt_c = col(jnp.broadcast_to(ssm_log_dt.reshape(ngrp, 1), (ngrp, SSM_STATE)))
    b_re2, b_im2 = ssm_b_re.reshape(gn, SSM_GROUP), ssm_b_im.reshape(gn, SSM_GROUP)
    lrdt_c, ang_c, bb_re, bb_im = _ssm_prep("ssm_prep", lrl_c, li_c, ldt_c, b_re2, b_im2)
    lrdt, ang = lrdt_c.reshape(1, gn), ang_c.reshape(1, gn)
    bbd_re, bbd_im = _blockdiag_b(bb_re, sw), _blockdiag_b(bb_im, sw)
    ccd_re, ccd_im = _blockdiag_c(ssm_c_re[0], sw), _blockdiag_c(ssm_c_im[0], sw)
    y_s, ge, s_re, s_im = _ssm_fwd("ssm_fwd", uin, lrdt, ang, bbd_re.astype(BF16), bbd_im.astype(BF16),
                                   ccd_re.astype(BF16), ccd_im.astype(BF16), ssm_d, sw)
    gv = _mm("glu_in", ge, full['w_glu'], b_stacked=True, tn=256)
    sg = _glu_fwd("glu_act", gv, b_glu)
    y_ssm = _mm("ssm_up", sg, full['w_ssm_up'], b_stacked=True, tn=256)
    merged = _gates_fwd("gates", y_pool, y_ssm, uin, d, pw)
    blk = (512, d)

    def epi2(acc, res, scale):
        return res + scale * acc, acc

    x2, y2 = _mm("mix_out", merged, wo, tm=512, tn=d, epi=epi2,
                 epi_ins=[(x1, blk, lambda i, j: (i, j)), (gt2, (1, d), lambda i, j: (0, j))],
                 epi_outs=[((t, d), F32, blk, lambda i, j: (i, j)), ((t, d), BF16, blk, lambda i, j: (i, j))])

    x3, sav3 = _ffn_fwd("ffn2", x2, g_ffn2, sh3, sc3, gt3, lambda after: weights(3, after)['w_ffn2_in'],
                        lambda after: weights(3, after)['w_ffn2_out'].reshape(f, d))

    dx3, dg_final, loss_v = _loss_bwd("loss", x3, tgt, _row(g_final))
    loss = lax.psum(loss_v[0, 0], ("x", "y", "c"))

    gs = {}
    rs_open = []

    def start_rs(names, gbs):
        send, recv, g_thru, land_thru, token = _rs_start("rs_start_" + names[0], gbs)
        rs_open.append((names, send, recv, g_thru, land_thru))
        return token

    dx2, dsh3, dsc3, dgt3, gs['g_ffn2'] = _ffn_bwd(
        "ffn2b", dx3, sav3, g_ffn2, sc3, gt3, full['w_ffn2_in'], full['w_ffn2_out'].reshape(f, d),
        lambda key, g: start_rs(['w_ffn2_' + key], [g]))

    dy2, dgt2 = _resgate_bwd("mix_dgate", dx2, y2, gt2, 1.0)
    dmerged = _mm("mix_dmerged", dy2, wo, tb=True, tn=d)
    g_wo = _mm("mix_dwout", merged, dy2, ta=True, tm=d, tn=d, tk=512).reshape(N_CHIPS, d // N_CHIPS, d)
    dy_pool, dy_ssm, dgl = _gates_bwd("gates_bwd", dmerged, y_pool, y_ssm, uin, d, pw)

    dp = _mm("pool_dup", dy_pool, full['w_pool_up'], tb=True, b_stacked=True, tn=pw, tk=256)
    g_wpu = _mm("pool_dwup", p_pool, dy_pool, ta=True, out_stacked=True, tm=pw, tn=256, tk=512)
    du_pool, gs['pool_w'], gs['pool_b'], gs['pool_scale'] = _pool_bwd("pool_bwd", dp, z_pool, pool_w[0], pool_b, pool_scale)

    dsg = _mm("ssm_dup", dy_ssm, full['w_ssm_up'], tb=True, b_stacked=True, tn=sw, tk=256)
    g_wsu = _mm("ssm_dwup", sg, dy_ssm, ta=True, out_stacked=True, tm=sw, tn=256, tk=512)
    tok = start_rs(['w_out', 'w_pool_up', 'w_ssm_up'], [g_wo, g_wpu, g_wsu])
    dgv, gs['b_glu'] = _glu_bwd("glu_bwd", dsg, gv, b_glu, deps=(tok,))
    dge = _mm("glu_dge", dgv, full['w_glu'], tb=True, b_stacked=True, tn=sw, tk=256)
    g_wglu = _mm("glu_dw", ge, dgv, ta=True, out_stacked=True, tm=sw, tn=256, tk=512)
    tr = lambda a: jnp.swapaxes(a, 1, 2).astype(BF16)
    (du_ssm, g_abre, g_abim, g_bbd_re, g_bbd_im, g_ccd_re, g_ccd_im, gs['ssm_d']) = _ssm_bwd(
        "ssm_bwd", dge, y_s, uin, s_re, s_im, lrdt, ang, tr(bbd_re), tr(bbd_im), tr(ccd_re), tr(ccd_im), ssm_d, sw)
    gs['ssm_c_re'], gs['ssm_c_im'] = _diag_of_c(g_ccd_re, sw), _diag_of_c(g_ccd_im, sw)
    d_lrl, d_li, d_ldt, d_bre, d_bim = _ssm_param_bwd(
        "ssm_param_bwd", lrl_c, li_c, ldt_c, b_re2, b_im2, g_abre.reshape(gn, 1), g_abim.reshape(gn, 1),
        _diag_of_b(g_bbd_re, sw), _diag_of_b(g_bbd_im, sw))
    gs['ssm_lam_re_log'], gs['ssm_lam_im'] = d_lrl, d_li
    gs['ssm_log_dt'] = jnp.sum(d_ldt.reshape(ngrp, SSM_STATE), axis=1)
    gs['ssm_b_re'], gs['ssm_b_im'] = d_bre, d_bim

    duin = jnp.concatenate([du_pool, du_ssm, dgl], axis=1)
    g_win = _mm("mix_dwin", h2, duin, ta=True, out_stacked=True, tm=d, tn=768, tk=512)
    tok = start_rs(['w_glu', 'w_in'], [g_wglu, g_win])
    dh2 = _mm("mix_dh", duin, full['w_in'], tb=True, b_stacked=True, tn=d, tk=768, deps=(tok,))
    dx1, dsh2, dsc2, gs['g_mix'] = _norm_bwd("mix_dnorm", dh2, x1, dx2, g_mix, sc2)

    gs['g_final'] = dg_final
    early = [n for n in SMALL if n not in SMALL_LATE]
    sg_blk = _pack([gs[n] for n in early])
    sg_send, sg_recv, sg_blk, sg_land, tok = _bcast_start("sg_start", sg_blk)

    dx0, dsh1, dsc1, dgt1, gs['g_ffn1'] = _ffn_bwd(
        "ffn1b", dx1, sav1, g_ffn1, sc1, gt1, full['w_ffn1_in'], full['w_ffn1_out'].reshape(f, d),
        lambda key, g: start_rs(['w_ffn1_' + key], [g]), deps=(tok,))

    gs['b_ada'] = jnp.concatenate([dsh1, dsc1, dgt1, dsh2, dsc2, dgt2, dsh3, dsc3, dgt3], axis=1)
    late8 = _allgather_small("ag_late", _pack([gs[n] for n in SMALL_LATE]))
    dmod_all = late8.reshape(N_DEV, -1)[:, :9 * d]
    dmod_sh = lax.dynamic_slice(dmod_all, (0, s_me * ncol), (N_DEV, ncol))
    g_w_ada = _ada_bwd("ada_bwd", c_all, dmod_sh)

    grads, delta, new_m, new_v = {}, {}, {}, {}
    sg_land = _bcast_wait("sg_wait", sg_blk, sg_land, sg_send, sg_recv, g_w_ada)
    early8 = lax.dynamic_update_slice(sg_land, sg_blk[None], (b_me, 0, 0)).reshape(-1, 128)
    for tag, names, g8 in (("early", early, early8), ("late", SMALL_LATE, late8)):
        res = _adamw_small("adamw_small_" + tag, _pack([wt[n] for n in names]), g8,
                           _pack([mom[n] for n in names]), _pack([var[n] for n in names]))
        for dst, packed in zip((grads, delta, new_m, new_v), res):
            for n, val in zip(names, _unpack(packed, [wt[n].shape for n in names])):
                dst[n] = val

    dl, mn, vn = _adamw("adamw_w_ada", w_ada[0], g_w_ada, m_w_ada[0], v_w_ada[0])
    grads['w_ada'], delta['w_ada'], new_m['w_ada'], new_v['w_ada'] = g_w_ada[None], dl[None], mn[None], vn[None]
    after = dl
    for names, send, recv, g_thru, land_thru in rs_open:
        g_done, land_done = _rs_wait("rs_wait_" + names[0], g_thru, land_thru, send, recv, after)
        for n, g_own, land in zip(names, g_done, land_done):
            g_sum = _rs_sum("rs_sum_" + n, g_own, land)
            dl, mn, vn = _adamw("adamw_" + n, wt[n][0], g_sum, mom[n][0], var[n][0])
            grads[n], delta[n], new_m[n], new_v[n] = g_sum[None], dl[None], mn[None], vn[None]
            after = dl

    return (loss, dx0[None], *[grads[n] for n in WEIGHTS], *[delta[n] for n in WEIGHTS],
            *[new_m[n] for n in WEIGHTS], *[new_v[n] for n in WEIGHTS])
```

```python
import functools
import math

import jax
import jax.numpy as jnp
from jax import lax
from jax.experimental import pallas as pl
from jax.experimental.pallas import tpu as pltpu

F32 = jnp.float32
BF16 = jnp.bfloat16
MESH = pl.DeviceIdType.MESH

EPS = 1e-6
POOL_WINDOWS = (2, 4, 8, 16)
POOL_HALO = 16
SSM_GROUP = 16
SSM_STATE = 64
SSM_BLOCKS = 4
N_DEV = 8
N_CHIPS = 4
ADAM_LR = 0.001
ADAM_B1 = 0.9
ADAM_B2 = 0.999
ADAM_EPS = 1e-08
ADAM_WD = 0.01
ADAM_STEP = 10
VMEM_LIMIT = 56 * 1024 * 1024


ANY_SPEC = pl.BlockSpec(memory_space=pl.ANY)
HBM_SPEC = pl.BlockSpec(memory_space=pltpu.HBM)
SEM_SPEC = pl.BlockSpec(memory_space=pltpu.SEMAPHORE)
EFFECT = pltpu.SideEffectType.DATAFLOW_SIDE_EFFECTING


def _hbm(a):
    return pltpu.with_memory_space_constraint(a, pltpu.HBM)


def _pcall(body, **kw):
    return pl.pallas_call(body, **kw)


def _params(*sem):
    return pltpu.CompilerParams(dimension_semantics=sem, vmem_limit_bytes=VMEM_LIMIT)


def _pick(n, cap, mult=128):
    if n <= cap:
        return n
    best = None
    for d in range(mult, cap + 1, mult):
        if n % d == 0:
            best = d
    assert best is not None, (n, cap, mult)
    return best


def _sigmoid(v):
    return 1.0 / (1.0 + jnp.exp(-v))


def _rowwise(name, fn, ins, params, outs, reds, tm, deps=()):
    t = ins[0][0].shape[0]
    tm = min(tm, t)
    nb = t // tm
    ni, npar, no, nd = len(ins), len(params), len(outs), len(deps)

    def body(*refs):
        iv = [r[...] for r in refs[:ni]]
        pv = [r[...] for r in refs[ni:ni + npar]]
        o_refs = refs[ni + npar + nd:ni + npar + nd + no]
        r_refs = refs[ni + npar + nd + no:]
        ovals, rvals = fn(iv, pv)
        for o_ref, val in zip(o_refs, ovals):
            parts = val if isinstance(val, (list, tuple)) else [val]
            off = 0
            for p in parts:
                o_ref[:, off:off + p.shape[1]] = p.astype(o_ref.dtype)
                off += p.shape[1]
        if r_refs:
            @pl.when(pl.program_id(0) == 0)
            def _():
                for r in r_refs:
                    r[...] = jnp.zeros_like(r)
            for r, val in zip(r_refs, rvals):
                r[...] += val

    in_specs = [pl.BlockSpec((tm, w), functools.partial(lambda i, cb: (i, cb), cb=cb)) for (_, w, cb) in ins]
    in_specs += [pl.BlockSpec(p.shape, lambda i: (0, 0)) for p in params]
    in_specs += [ANY_SPEC] * nd
    out_shape = [jax.ShapeDtypeStruct((t, w), dt) for (w, dt) in outs]
    out_shape += [jax.ShapeDtypeStruct((1, w), F32) for w in reds]
    out_specs = [pl.BlockSpec((tm, w), lambda i: (i, 0)) for (w, _) in outs]
    out_specs += [pl.BlockSpec((1, w), lambda i: (0, 0)) for w in reds]
    res = _pcall(body, name=name, grid=(nb,), in_specs=in_specs, out_specs=out_specs, out_shape=out_shape,
                 compiler_params=_params("arbitrary"))(*[a for a, _, _ in ins], *params, *deps)
    return res


def _colsum(v):
    return jnp.sum(v, axis=0, keepdims=True)


def _mm(name, a, b, *, ta=False, tb=False, b_stacked=False, out_stacked=False, tm=512, tn=1024, tk=2816,
        out_dtype=BF16, epi=None, epi_ins=(), epi_outs=None, deps=(), j_outer=False):
    if ta:
        kdim, m = a.shape
    else:
        m, kdim = a.shape
    ns = None
    if b_stacked:
        ns = b.shape[2]
        n = b.shape[1] if tb else N_CHIPS * ns
        assert kdim == (N_CHIPS * ns if tb else b.shape[1]), (name, a.shape, b.shape)
    else:
        n = b.shape[0] if tb else b.shape[1]
        assert kdim == (b.shape[1] if tb else b.shape[0]), (name, a.shape, b.shape)
        if out_stacked:
            ns = n // N_CHIPS

    def shards(want):
        return max(g for g in (1, 2, 4) if g * ns <= max(want, ns))

    tm = _pick(m, tm, 128 if ta else 8)
    gn = gk = 1
    if (b_stacked and not tb) or out_stacked:
        gn = shards(tn)
        tn = gn * ns
    else:
        tn = _pick(n, tn)
    if b_stacked and tb:
        gk = shards(tk)
        tk = gk * ns
    else:
        tk = _pick(kdim, tk, 8 if ta else 128)
    nm, nn, nk = m // tm, n // tn, kdim // tk

    def ij(f):
        return (lambda g0, g1, k: f(g1, g0, k)) if j_outer else f

    a_spec = pl.BlockSpec((tk, tm), ij(lambda i, j, k: (k, i))) if ta else pl.BlockSpec((tm, tk), ij(lambda i, j, k: (i, k)))
    if b_stacked and not tb:
        b_spec = pl.BlockSpec((gn, tk, ns), ij(lambda i, j, k: (j, k, 0)))
    elif b_stacked and tb:
        b_spec = pl.BlockSpec((gk, tn, ns), ij(lambda i, j, k: (k, j, 0)))
    elif tb:
        b_spec = pl.BlockSpec((tn, tk), ij(lambda i, j, k: (j, k)))
    else:
        b_spec = pl.BlockSpec((tk, tn), ij(lambda i, j, k: (k, j)))
    dims = (((0 if ta else 1,), (1 if tb else 0,)), ((), ()))

    if epi_outs is None:
        if out_stacked:
            epi_outs = [((N_CHIPS, m, ns), out_dtype, (gn, tm, ns), lambda i, j: (j, i, 0))]
        else:
            epi_outs = [((m, n), out_dtype, (tm, tn), lambda i, j: (i, j))]
    ne, no, nd = len(epi_ins), len(epi_outs), len(deps)

    def body(a_ref, b_ref, *rest):
        e_refs = rest[:ne]
        o_refs = rest[ne + nd:ne + nd + no]
        scratch = rest[ne + nd + no:]
        av = a_ref[...].astype(BF16)
        if b_stacked and not tb:
            parts = [lax.dot_general(av, b_ref[s].astype(BF16), dims, preferred_element_type=F32) for s in range(gn)]
        elif b_stacked and tb:
            p = None
            for s in range(gk):
                q = lax.dot_general(av[:, s * ns:(s + 1) * ns], b_ref[s].astype(BF16), dims, preferred_element_type=F32)
                p = q if p is None else p + q
            parts = [p]
        else:
            parts = [lax.dot_general(av, b_ref[...].astype(BF16), dims, preferred_element_type=F32)]

        def finish(acc_parts):
            if epi is None and out_stacked:
                acc = acc_parts[0]
                for s in range(gn):
                    o_refs[0][s] = acc[:, s * ns:(s + 1) * ns].astype(out_dtype)
            elif epi is None:
                w = acc_parts[0].shape[1]
                for s, part in enumerate(acc_parts):
                    o_refs[0][:, s * w:(s + 1) * w] = part.astype(out_dtype)
            else:
                acc = acc_parts[0] if len(acc_parts) == 1 else jnp.concatenate(acc_parts, axis=1)
                for o_ref, val in zip(o_refs, epi(acc, *[r[...] for r in e_refs])):
                    o_ref[...] = val.astype(o_ref.dtype)

        if nk == 1:
            finish(parts)
        else:
            acc_ref = scratch[0]
            k = pl.program_id(2)
            w = parts[0].shape[1]

            @pl.when(k == 0)
            def _():
                for s, part in enumerate(parts):
                    acc_ref[:, s * w:(s + 1) * w] = part

            @pl.when(k > 0)
            def _():
                for s, part in enumerate(parts):
                    acc_ref[:, s * w:(s + 1) * w] += part

            @pl.when(k == nk - 1)
            def _():
                finish([acc_ref[...]])

    def _ij(f):
        return ij(lambda i, j, k: f(i, j))

    in_specs = [a_spec, b_spec] + [pl.BlockSpec(blk, _ij(f)) for (_, blk, f) in epi_ins] + [ANY_SPEC] * nd
    out_specs = [pl.BlockSpec(blk, _ij(f)) for (_, _, blk, f) in epi_outs]
    out_shape = [jax.ShapeDtypeStruct(s, dt) for (s, dt, _, _) in epi_outs]
    scratch = [pltpu.VMEM((tm, tn), F32)] if nk > 1 else []
    grid = (nn, nm, nk) if j_outer else (nm, nn, nk)
    res = _pcall(body, name=name, grid=grid, in_specs=in_specs, out_specs=out_specs, out_shape=out_shape,
                 scratch_shapes=scratch, compiler_params=_params("parallel", "parallel", "arbitrary"))(
                     a, b, *[x for x, _, _ in epi_ins], *deps)
    return res[0] if len(res) == 1 else res


def _norm_fwd(name, x, g, sc, sh, deps=()):
    d = x.shape[1]

    def fn(iv, pv):
        (xv,), (gv, scv, shv) = iv, pv
        r = lax.rsqrt(jnp.mean(xv * xv, axis=-1, keepdims=True) + EPS)
        return [xv * r * gv * (1.0 + scv) + shv], []

    return _rowwise(name, fn, [(x, d, 0)], [g, sc, sh], [(d, BF16)], [], 512, deps=deps)[0]


def _norm_bwd(name, dh, x, dres, g, sc):
    d = x.shape[1]

    def fn(iv, pv):
        (dhv, xv, drv), (gv, scv) = iv, pv
        dhv = dhv.astype(F32)
        r = lax.rsqrt(jnp.mean(xv * xv, axis=-1, keepdims=True) + EPS)
        xr = xv * r
        dn = dhv * (1.0 + scv)
        gd = gv * dn
        dx = drv + r * (gd - xr * jnp.mean(gd * xr, axis=-1, keepdims=True))
        return [dx], [_colsum(dhv), _colsum(dhv * xr * gv), _colsum(dn * xr)]

    return _rowwise(name, fn, [(dh, d, 0), (x, d, 0), (dres, d, 0)], [g, sc], [(d, F32)], [d, d, d], 256)


def _loss_bwd(name, x, tgt, g):
    d = x.shape[1]

    def fn(iv, pv):
        (xv, tv), (gv,) = iv, pv
        r = lax.rsqrt(jnp.mean(xv * xv, axis=-1, keepdims=True) + EPS)
        xr = xv * r
        e = xr * gv - tv
        loss_row = 0.5 * jnp.mean(e * e, axis=-1, keepdims=True)
        dout = e * (1.0 / d)
        gd = gv * dout
        dx = r * (gd - xr * jnp.mean(gd * xr, axis=-1, keepdims=True))
        return [dx], [_colsum(dout * xr), _colsum(loss_row * jnp.ones((1, 128), F32))]

    return _rowwise(name, fn, [(x, d, 0), (tgt, d, 0)], [g], [(d, F32)], [d, 128], 256)


def _resgate_bwd(name, dx, y, gt, factor, deps=()):
    d = dx.shape[1]

    def fn(iv, pv):
        (dxv, yv), (gtv,) = iv, pv
        return [factor * gtv * dxv], [_colsum(factor * dxv * yv.astype(F32))]

    return _rowwise(name, fn, [(dx, d, 0), (y, d, 0)], [gt], [(d, BF16)], [d], 512, deps=deps)


def _swiglu_fwd(name, ab):
    f = ab.shape[1] // 2

    def fn(iv, pv):
        a, b = iv[0].astype(F32), iv[1].astype(F32)
        return [a * _sigmoid(a) * b], []

    return _rowwise(name, fn, [(ab, f, 0), (ab, f, 1)], [], [(f, BF16)], [], 256)[0]


def _swiglu_bwd(name, dact, ab, deps=()):
    f = ab.shape[1] // 2

    def fn(iv, pv):
        dv, a, b = iv[0].astype(F32), iv[1].astype(F32), iv[2].astype(F32)
        s = _sigmoid(a)
        return [[dv * b * (s * (1.0 + a * (1.0 - s))), dv * (a * s)]], []

    return _rowwise(name, fn, [(dact, f, 0), (ab, f, 0), (ab, f, 1)], [], [(2 * f, BF16)], [], 256, deps=deps)[0]


def _gates_fwd(name, y_pool, y_ssm, uin, d, pw):
    cb = (2 * pw) // d

    def fn(iv, pv):
        yp, ys, glp, gls = [v.astype(F32) for v in iv]
        return [_sigmoid(glp) * yp + _sigmoid(gls) * ys], []

    return _rowwise(name, fn, [(y_pool, d, 0), (y_ssm, d, 0), (uin, d, cb), (uin, d, cb + 1)], [], [(d, BF16)], [], 256)[0]


def _gates_bwd(name, dm, y_pool, y_ssm, uin, d, pw):
    cb = (2 * pw) // d

    def fn(iv, pv):
        dmv, yp, ys, glp, gls = [v.astype(F32) for v in iv]
        sp, ss = _sigmoid(glp), _sigmoid(gls)
        return [dmv * sp, dmv * ss, [dmv * yp * sp * (1.0 - sp), dmv * ys * ss * (1.0 - ss)]], []

    return _rowwise(name, fn, [(dm, d, 0), (y_pool, d, 0), (y_ssm, d, 0), (uin, d, cb), (uin, d, cb + 1)], [],
                    [(d, BF16), (d, BF16), (2 * d, BF16)], [], 256)


def _glu_fwd(name, gv, b_glu):
    w = gv.shape[1] // 2

    def fn(iv, pv):
        gvv, (bv,) = iv[0].astype(F32) + pv[0], pv
        return [gvv[:, :w] * _sigmoid(gvv[:, w:])], []

    return _rowwise(name, fn, [(gv, 2 * w, 0)], [b_glu], [(w, BF16)], [], 512)[0]


def _glu_bwd(name, dsg, gv, b_glu, deps=()):
    w = gv.shape[1] // 2

    def fn(iv, pv):
        dv = iv[0].astype(F32)
        gvv = iv[1].astype(F32) + pv[0]
        val, s = gvv[:, :w], _sigmoid(gvv[:, w:])
        dval = dv * s
        dgate = dv * val * s * (1.0 - s)
        return [[dval, dgate]], [jnp.concatenate([_colsum(dval), _colsum(dgate)], axis=1)]

    return _rowwise(name, fn, [(dsg, w, 0), (gv, 2 * w, 0)], [b_glu], [(2 * w, BF16)], [2 * w], 512, deps=deps)


def _adamw(name, w, g, m, v, tm=256):
    c = w.shape[1]

    def fn(iv, pv):
        wv, gv, mv, vv = iv
        mn = ADAM_B1 * mv + (1.0 - ADAM_B1) * gv
        vn = ADAM_B2 * vv + (1.0 - ADAM_B2) * (gv * gv)
        m_hat = mn / (1.0 - ADAM_B1 ** ADAM_STEP)
        v_hat = vn / (1.0 - ADAM_B2 ** ADAM_STEP)
        delta = -ADAM_LR * (m_hat / (jnp.sqrt(v_hat) + ADAM_EPS) + ADAM_WD * wv)
        return [delta, mn, vn], []

    return _rowwise(name, fn, [(w, c, 0), (g, c, 0), (m, c, 0), (v, c, 0)], [], [(c, F32)] * 3, [], _pick(w.shape[0], tm, 8))


def _adamw_small(name, w, g8, m, v):
    r = w.shape[0]

    def body(w_ref, g_ref, m_ref, v_ref, go_ref, d_ref, mo_ref, vo_ref):
        gv = g_ref[0:r, :]
        for k in range(1, N_DEV):
            gv = gv + g_ref[k * r:(k + 1) * r, :]
        mn = ADAM_B1 * m_ref[...] + (1.0 - ADAM_B1) * gv
        vn = ADAM_B2 * v_ref[...] + (1.0 - ADAM_B2) * (gv * gv)
        m_hat = mn / (1.0 - ADAM_B1 ** ADAM_STEP)
        v_hat = vn / (1.0 - ADAM_B2 ** ADAM_STEP)
        go_ref[...] = gv
        d_ref[...] = -ADAM_LR * (m_hat / (jnp.sqrt(v_hat) + ADAM_EPS) + ADAM_WD * w_ref[...])
        mo_ref[...] = mn
        vo_ref[...] = vn

    return _pcall(body, name=name, out_shape=[jax.ShapeDtypeStruct((r, 128), F32)] * 4,
                  compiler_params=pltpu.CompilerParams(vmem_limit_bytes=VMEM_LIMIT))(w, g8, m, v)


def _pool_fwd(name, uin, pool_w, pool_b, pool_scale, pw, tm=512):
    t = uin.shape[0]
    tm = min(tm, t)
    ng = len(POOL_WINDOWS)
    gw = pw // ng

    def body(u_ref, w_ref, b_ref, s_ref, p_ref, z_ref, ext):
        i = pl.program_id(0)

        @pl.when(i == 0)
        def _():
            ext[0:POOL_HALO, :] = jnp.zeros((POOL_HALO, pw), F32)

        u = u_ref[...]
        ext[POOL_HALO:POOL_HALO + tm, :] = u
        pos = i * tm + lax.broadcasted_iota(jnp.int32, (tm, 1), 0)
        for k, win in enumerate(POOL_WINDOWS):
            cols = slice(k * gw, (k + 1) * gw)
            acc = u[:, cols]
            for j in range(1, win):
                acc = acc + ext[POOL_HALO - j:POOL_HALO - j + tm, cols]
            cnt = jnp.minimum(pos + 1, win).astype(F32)
            z = acc / cnt - u[:, cols]
            zp = jnp.dot(z.astype(BF16), w_ref[k].astype(BF16), preferred_element_type=F32) + b_ref[:, cols]
            p_ref[:, cols] = (zp * s_ref[:, cols]).astype(BF16)
            z_ref[:, cols] = z.astype(BF16)
        ext[0:POOL_HALO, :] = u[tm - POOL_HALO:tm, :]

    return _pcall(
        body, name=name, grid=(t // tm,),
        in_specs=[pl.BlockSpec((tm, pw), lambda i: (i, 0)), pl.BlockSpec(pool_w.shape, lambda i: (0, 0, 0)),
                  pl.BlockSpec(pool_b.shape, lambda i: (0, 0)), pl.BlockSpec(pool_scale.shape, lambda i: (0, 0))],
        out_specs=[pl.BlockSpec((tm, pw), lambda i: (i, 0))] * 2,
        out_shape=[jax.ShapeDtypeStruct((t, pw), BF16)] * 2,
        scratch_shapes=[pltpu.VMEM((POOL_HALO + tm, pw), F32)],
        compiler_params=_params("arbitrary"))(uin, pool_w, pool_b, pool_scale)


def _pool_bwd(name, dp, z, pool_w, pool_b, pool_scale, tm=512):
    t, pw = z.shape
    tm = min(tm, t)
    nb = t // tm
    ng = len(POOL_WINDOWS)
    gw = pw // ng

    def body(dp_ref, z_ref, w_ref, b_ref, s_ref, du_ref, dw_ref, db_ref, ds_ref, ext):
        i = pl.program_id(0)

        @pl.when(i == 0)
        def _():
            ext[tm:tm + POOL_HALO, :] = jnp.zeros((POOL_HALO, pw), F32)
            dw_ref[...] = jnp.zeros_like(dw_ref)
            db_ref[...] = jnp.zeros_like(db_ref)
            ds_ref[...] = jnp.zeros_like(ds_ref)

        pos = (nb - 1 - i) * tm + lax.broadcasted_iota(jnp.int32, (tm, 1), 0)
        for k, win in enumerate(POOL_WINDOWS):
            cols = slice(k * gw, (k + 1) * gw)
            zk = z_ref[:, cols]
            dpk = dp_ref[:, cols].astype(F32)
            wk = w_ref[k].astype(BF16)
            zp = jnp.dot(zk, wk, preferred_element_type=F32) + b_ref[:, cols]
            ds_ref[:, cols] += _colsum(dpk * zp)
            dzp = dpk * s_ref[:, cols]
            db_ref[:, cols] += _colsum(dzp)
            dzpb = dzp.astype(BF16)
            dz = lax.dot_general(dzpb, wk, (((1,), (1,)), ((), ())), preferred_element_type=F32)
            dw_ref[k] += lax.dot_general(zk, dzpb, (((0,), (0,)), ((), ())), preferred_element_type=F32)
            cnt = jnp.minimum(pos + 1, win).astype(F32)
            r = dz / cnt
            ext[0:tm, cols] = r
            acc = r - dz
            for j in range(1, win):
                acc = acc + ext[j:j + tm, cols]
            du_ref[:, cols] = acc.astype(BF16)
        ext[tm:tm + POOL_HALO, :] = ext[0:POOL_HALO, :]

    rev = lambda i: (nb - 1 - i, 0)
    return _pcall(
        body, name=name, grid=(nb,),
        in_specs=[pl.BlockSpec((tm, pw), rev), pl.BlockSpec((tm, pw), rev), pl.BlockSpec(pool_w.shape, lambda i: (0, 0, 0)),
                  pl.BlockSpec(pool_b.shape, lambda i: (0, 0)), pl.BlockSpec(pool_scale.shape, lambda i: (0, 0))],
        out_specs=[pl.BlockSpec((tm, pw), rev), pl.BlockSpec(pool_w.shape, lambda i: (0, 0, 0)),
                   pl.BlockSpec((1, pw), lambda i: (0, 0)), pl.BlockSpec((1, pw), lambda i: (0, 0))],
        out_shape=[jax.ShapeDtypeStruct((t, pw), BF16), jax.ShapeDtypeStruct(pool_w.shape, F32),
                   jax.ShapeDtypeStruct((1, pw), F32), jax.ShapeDtypeStruct((1, pw), F32)],
        scratch_shapes=[pltpu.VMEM((tm + POOL_HALO, pw), F32)],
        compiler_params=_params("arbitrary"))(dp, z, pool_w, pool_b, pool_scale)


def _ssm_disc(lrl, li, ldt):
    lr = -jnp.exp(lrl)
    dt = jnp.exp(ldt)
    mag = jnp.exp(lr * dt)
    ang = li * dt
    ab_re = mag * jnp.cos(ang)
    ab_im = mag * jnp.sin(ang)
    num_re = ab_re - 1.0
    num_im = ab_im
    den = lr * lr + li * li
    f_re = (num_re * lr + num_im * li) / den
    f_im = (num_im * lr - num_re * li) / den
    return lr, dt, mag, ang, ab_re, ab_im, num_re, num_im, den, f_re, f_im


def _ssm_prep(name, lrl, li, ldt, b_re, b_im):
    gn, h = b_re.shape

    def body(lrl_ref, li_ref, ldt_ref, br_ref, bi_ref, lrdt_ref, ang_ref, bbr_ref, bbi_ref):
        lr, dt, _, ang, _, _, _, _, _, f_re, f_im = _ssm_disc(lrl_ref[...], li_ref[...], ldt_ref[...])
        lrdt_ref[...] = lr * dt
        ang_ref[...] = ang
        br, bi = br_ref[...], bi_ref[...]
        bbr_ref[...] = f_re * br - f_im * bi
        bbi_ref[...] = f_re * bi + f_im * br

    col = jax.ShapeDtypeStruct((gn, 1), F32)
    mat = jax.ShapeDtypeStruct((gn, h), F32)
    return _pcall(body, name=name, out_shape=[col, col, mat, mat])(lrl, li, ldt, b_re, b_im)


def _ssm_param_bwd(name, lrl, li, ldt, b_re, b_im, g_abre, g_abim, g_bbre, g_bbim):
    gn, h = b_re.shape

    def body(lrl_ref, li_ref, ldt_ref, br_ref, bi_ref, gar_ref, gai_ref, gbr_ref, gbi_ref,
             dlrl_ref, dli_ref, dldt_ref, dbr_ref, dbi_ref):
        li_v = li_ref[...]
        lr, dt, mag, ang, ab_re, ab_im, num_re, num_im, den, f_re, f_im = _ssm_disc(lrl_ref[...], li_v, ldt_ref[...])
        br, bi = br_ref[...], bi_ref[...]
        gbr, gbi = gbr_ref[...], gbi_ref[...]
        g_fre = jnp.sum(gbr * br + gbi * bi, axis=1, keepdims=True)
        g_fim = jnp.sum(gbi * br - gbr * bi, axis=1, keepdims=True)
        dbr_ref[...] = gbr * f_re + gbi * f_im
        dbi_ref[...] = gbi * f_re - gbr * f_im
        g_num_re = (g_fre * lr - g_fim * li_v) / den
        g_num_im = (g_fre * li_v + g_fim * lr) / den
        g_den = -(g_fre * f_re + g_fim * f_im) / den
        g_lr = (g_fre * num_re + g_fim * num_im) / den + g_den * 2.0 * lr
        g_li = (g_fre * num_im - g_fim * num_re) / den + g_den * 2.0 * li_v
        g_are = gar_ref[...] + g_num_re
        g_aim = gai_ref[...] + g_num_im
        g_mag = g_are * jnp.cos(ang) + g_aim * jnp.sin(ang)
        g_ang = g_aim * ab_re - g_are * ab_im
        g_lrdt = g_mag * mag
        g_lr = g_lr + g_lrdt * dt
        g_dt = g_lrdt * lr + g_ang * li_v
        g_li = g_li + g_ang * dt
        dlrl_ref[...] = g_lr * lr
        dli_ref[...] = g_li
        dldt_ref[...] = g_dt * dt

    col = jax.ShapeDtypeStruct((gn, 1), F32)
    mat = jax.ShapeDtypeStruct((gn, h), F32)
    return _pcall(body, name=name, out_shape=[col, col, col, mat, mat])(lrl, li, ldt, b_re, b_im, g_abre, g_abim, g_bbre, g_bbim)


def _pow_rows(lrdt, ang, k):
    mag = jnp.exp(k * lrdt)
    return mag * jnp.cos(k * ang), mag * jnp.sin(k * ang)


def _ssm_chunk(t):
    return 256 if t >= 2048 else 128


def _ssm_fwd(name, uin, lrdt, ang, bb_re, bb_im, cc_re, cc_im, d_skip, sw):
    t = uin.shape[0]
    gn = lrdt.shape[1]
    lc = _ssm_chunk(t)
    nsteps = int(math.log2(lc))
    pad = lc // 2
    ub, sb = sw // SSM_BLOCKS, gn // SSM_BLOCKS

    def body(u_ref, lrdt_ref, ang_ref, bbr_ref, bbi_ref, ccr_ref, cci_ref, d_ref, y_ref, ge_ref, sre_ref, sim_ref,
             p_re, p_im, a_re, a_im, b_re, b_im, car_re, car_im):
        i = pl.program_id(0)
        lrdt_v, ang_v = lrdt_ref[...], ang_ref[...]

        @pl.when(i == 0)
        def _():
            k = (lax.broadcasted_iota(jnp.int32, (lc, 1), 0) + 1).astype(F32)
            pr, pi = _pow_rows(lrdt_v, ang_v, k)
            p_re[...] = pr
            p_im[...] = pi
            zeros = jnp.zeros((pad, gn), F32)
            a_re[0:pad, :] = zeros
            a_im[0:pad, :] = zeros
            b_re[0:pad, :] = zeros
            b_im[0:pad, :] = zeros
            car_re[...] = jnp.zeros_like(car_re)
            car_im[...] = jnp.zeros_like(car_im)

        u = u_ref[...]
        ubf = u.astype(BF16)
        for q in range(SSM_BLOCKS):
            uq = ubf[:, q * ub:(q + 1) * ub]
            a_re[pad:pad + lc, q * sb:(q + 1) * sb] = jnp.dot(uq, bbr_ref[q], preferred_element_type=F32)
            a_im[pad:pad + lc, q * sb:(q + 1) * sb] = jnp.dot(uq, bbi_ref[q], preferred_element_type=F32)
        bufs = [(a_re, a_im), (b_re, b_im)]
        for q in range(SSM_BLOCKS):
            cols = slice(q * sb, (q + 1) * sb)
            for j in range(nsteps):
                dd = 1 << j
                (s_re, s_im), (d_re, d_im) = bufs[j % 2], bufs[(j + 1) % 2]
                pr, pi = _pow_rows(lrdt_v[:, cols], ang_v[:, cols], float(dd))
                cr, ci = s_re[pad:pad + lc, cols], s_im[pad:pad + lc, cols]
                hr, hi = s_re[pad - dd:pad - dd + lc, cols], s_im[pad - dd:pad - dd + lc, cols]
                d_re[pad:pad + lc, cols] = cr + (pr * hr - pi * hi)
                d_im[pad:pad + lc, cols] = ci + (pr * hi + pi * hr)
            f_re, f_im = bufs[nsteps % 2]
            cr, ci = car_re[:, cols], car_im[:, cols]
            pr, pi = p_re[:, cols], p_im[:, cols]
            sr = f_re[pad:pad + lc, cols] + (pr * cr - pi * ci)
            si = f_im[pad:pad + lc, cols] + (pr * ci + pi * cr)
            sre_ref[:, cols] = sr
            sim_ref[:, cols] = si
            car_re[:, cols] = sr[lc - 1:lc, :]
            car_im[:, cols] = si[lc - 1:lc, :]
            ycols = slice(q * ub, (q + 1) * ub)
            y = (jnp.dot(sr.astype(BF16), ccr_ref[q], preferred_element_type=F32)
                 - jnp.dot(si.astype(BF16), cci_ref[q], preferred_element_type=F32)
                 + d_ref[:, ycols] * u[:, ycols])
            y_ref[:, ycols] = y
            ge_ref[:, ycols] = (0.5 * y * (1.0 + lax.erf(y * (1.0 / math.sqrt(2.0))))).astype(BF16)

    cblk = sw // sw
    row = lambda i: (0, 0)
    blk3 = lambda i: (0, 0, 0)
    return _pcall(
        body, name=name, grid=(t // lc,),
        in_specs=[pl.BlockSpec((lc, sw), lambda i: (i, cblk)), pl.BlockSpec((1, gn), row), pl.BlockSpec((1, gn), row),
                  pl.BlockSpec(bb_re.shape, blk3), pl.BlockSpec(bb_im.shape, blk3),
                  pl.BlockSpec(cc_re.shape, blk3), pl.BlockSpec(cc_im.shape, blk3), pl.BlockSpec((1, sw), row)],
        out_specs=[pl.BlockSpec((lc, sw), lambda i: (i, 0)), pl.BlockSpec((lc, sw), lambda i: (i, 0)),
                   pl.BlockSpec((lc, gn), lambda i: (i, 0)), pl.BlockSpec((lc, gn), lambda i: (i, 0))],
        out_shape=[jax.ShapeDtypeStruct((t, sw), F32), jax.ShapeDtypeStruct((t, sw), BF16),
                   jax.ShapeDtypeStruct((t, gn), F32), jax.ShapeDtypeStruct((t, gn), F32)],
        scratch_shapes=[pltpu.VMEM((lc, gn), F32), pltpu.VMEM((lc, gn), F32),
                        pltpu.VMEM((pad + lc, gn), F32), pltpu.VMEM((pad + lc, gn), F32),
                        pltpu.VMEM((pad + lc, gn), F32), pltpu.VMEM((pad + lc, gn), F32),
                        pltpu.VMEM((1, gn), F32), pltpu.VMEM((1, gn), F32)],
        compiler_params=_params("arbitrary"))(uin, lrdt, ang, bb_re, bb_im, cc_re, cc_im, d_skip)


def _ssm_bwd(name, dge, y, uin, s_re, s_im, lrdt, ang, bbt_re, bbt_im, cct_re, cct_im, d_skip, sw):
    t = uin.shape[0]
    gn = lrdt.shape[1]
    lc = _ssm_chunk(t)
    nb = t // lc
    nsteps = int(math.log2(lc))
    pad = lc // 2
    ub, sb = sw // SSM_BLOCKS, gn // SSM_BLOCKS
    tail = 8

    def body(dge_ref, y_ref, u_ref, sre_ref, sim_ref, tre_ref, tim_ref, lrdt_ref, ang_ref, btr_ref, bti_ref, ctr_ref, cti_ref,
             d_ref, du_ref, dar_ref, dai_ref, dbr_ref, dbi_ref, dcr_ref, dci_ref, dd_ref,
             q_re, q_im, a_re, a_im, b_re, b_im, sp_re, sp_im, car_re, car_im):
        i = pl.program_id(0)
        lrdt_v, ang_v = lrdt_ref[...], ang_ref[...]

        @pl.when(i == 0)
        def _():
            k = (lc - lax.broadcasted_iota(jnp.int32, (lc, 1), 0)).astype(F32)
            qr, qi = _pow_rows(lrdt_v, ang_v, k)
            q_re[...] = qr
            q_im[...] = qi
            zeros = jnp.zeros((pad, gn), F32)
            a_re[lc:lc + pad, :] = zeros
            a_im[lc:lc + pad, :] = zeros
            b_re[lc:lc + pad, :] = zeros
            b_im[lc:lc + pad, :] = zeros
            car_re[...] = jnp.zeros_like(car_re)
            car_im[...] = jnp.zeros_like(car_im)
            for r in (dar_ref, dai_ref, dbr_ref, dbi_ref, dcr_ref, dci_ref, dd_ref):
                r[...] = jnp.zeros_like(r)

        first = (i == nb - 1).astype(F32)
        sp_re[0:tail, :] = tre_ref[...] * (1.0 - first)
        sp_im[0:tail, :] = tim_ref[...] * (1.0 - first)
        sp_re[tail:tail + lc, :] = sre_ref[...]
        sp_im[tail:tail + lc, :] = sim_ref[...]

        yv = y_ref[...]
        u = u_ref[...]
        cdf = 0.5 * (1.0 + lax.erf(yv * (1.0 / math.sqrt(2.0))))
        pdf = jnp.exp(-0.5 * yv * yv) * (1.0 / math.sqrt(2.0 * math.pi))
        dy = dge_ref[...].astype(F32) * (cdf + yv * pdf)
        dd_ref[...] += _colsum(dy * u)
        dyb = dy.astype(BF16)
        ubf = u.astype(BF16)
        for q in range(SSM_BLOCKS):
            dq = dyb[:, q * ub:(q + 1) * ub]
            a_re[0:lc, q * sb:(q + 1) * sb] = jnp.dot(dq, ctr_ref[q], preferred_element_type=F32)
            a_im[0:lc, q * sb:(q + 1) * sb] = -jnp.dot(dq, cti_ref[q], preferred_element_type=F32)
        bufs = [(a_re, a_im), (b_re, b_im)]
        tn_dims = (((0,), (0,)), ((), ()))
        for q in range(SSM_BLOCKS):
            cols = slice(q * sb, (q + 1) * sb)
            ycols = slice(q * ub, (q + 1) * ub)
            for j in range(nsteps):
                ds = 1 << j
                (s_r, s_i), (d_r, d_i) = bufs[j % 2], bufs[(j + 1) % 2]
                pr, pi = _pow_rows(lrdt_v[:, cols], ang_v[:, cols], float(ds))
                cr, ci = s_r[0:lc, cols], s_i[0:lc, cols]
                hr, hi = s_r[ds:ds + lc, cols], s_i[ds:ds + lc, cols]
                d_r[0:lc, cols] = cr + (pr * hr + pi * hi)
                d_i[0:lc, cols] = ci + (pr * hi - pi * hr)
            f_r, f_i = bufs[nsteps % 2]
            cr, ci = car_re[:, cols], car_im[:, cols]
            qr, qi = q_re[:, cols], q_im[:, cols]
            lam_r = f_r[0:lc, cols] + (qr * cr + qi * ci)
            lam_i = f_i[0:lc, cols] + (qr * ci - qi * cr)
            car_re[:, cols] = lam_r[0:1, :]
            car_im[:, cols] = lam_i[0:1, :]
            pvr, pvi = sp_re[tail - 1:tail - 1 + lc, cols], sp_im[tail - 1:tail - 1 + lc, cols]
            dar_ref[:, cols] += _colsum(lam_r * pvr + lam_i * pvi)
            dai_ref[:, cols] += _colsum(lam_i * pvr - lam_r * pvi)
            lrb, lib = lam_r.astype(BF16), lam_i.astype(BF16)
            du_ref[:, ycols] = (jnp.dot(lrb, btr_ref[q], preferred_element_type=F32)
                                + jnp.dot(lib, bti_ref[q], preferred_element_type=F32)
                                + d_ref[:, ycols] * dy[:, ycols]).astype(BF16)
            uq = ubf[:, ycols]
            dbr_ref[q] += lax.dot_general(uq, lrb, tn_dims, preferred_element_type=F32)
            dbi_ref[q] += lax.dot_general(uq, lib, tn_dims, preferred_element_type=F32)
            dq = dyb[:, ycols]
            dcr_ref[q] += lax.dot_general(sre_ref[:, cols].astype(BF16), dq, tn_dims, preferred_element_type=F32)
            dci_ref[q] -= lax.dot_general(sim_ref[:, cols].astype(BF16), dq, tn_dims, preferred_element_type=F32)

    cblk = 1
    rev = lambda i: (nb - 1 - i, 0)
    revu = lambda i: (nb - 1 - i, cblk)
    tailmap = lambda i: (jnp.maximum((nb - 1 - i) * (lc // tail) - 1, 0), 0)
    row = lambda i: (0, 0)
    blk3 = lambda i: (0, 0, 0)
    return _pcall(
        body, name=name, grid=(nb,),
        in_specs=[pl.BlockSpec((lc, sw), rev), pl.BlockSpec((lc, sw), rev), pl.BlockSpec((lc, sw), revu),
                  pl.BlockSpec((lc, gn), rev), pl.BlockSpec((lc, gn), rev),
                  pl.BlockSpec((tail, gn), tailmap), pl.BlockSpec((tail, gn), tailmap),
                  pl.BlockSpec((1, gn), row), pl.BlockSpec((1, gn), row),
                  pl.BlockSpec(bbt_re.shape, blk3), pl.BlockSpec(bbt_im.shape, blk3),
                  pl.BlockSpec(cct_re.shape, blk3), pl.BlockSpec(cct_im.shape, blk3), pl.BlockSpec((1, sw), row)],
        out_specs=[pl.BlockSpec((lc, sw), rev), pl.BlockSpec((1, gn), row), pl.BlockSpec((1, gn), row),
                   pl.BlockSpec((SSM_BLOCKS, ub, sb), blk3), pl.BlockSpec((SSM_BLOCKS, ub, sb), blk3),
                   pl.BlockSpec((SSM_BLOCKS, sb, ub), blk3), pl.BlockSpec((SSM_BLOCKS, sb, ub), blk3),
                   pl.BlockSpec((1, sw), row)],
        out_shape=[jax.ShapeDtypeStruct((t, sw), BF16), jax.ShapeDtypeStruct((1, gn), F32), jax.ShapeDtypeStruct((1, gn), F32),
                   jax.ShapeDtypeStruct((SSM_BLOCKS, ub, sb), F32), jax.ShapeDtypeStruct((SSM_BLOCKS, ub, sb), F32),
                   jax.ShapeDtypeStruct((SSM_BLOCKS, sb, ub), F32), jax.ShapeDtypeStruct((SSM_BLOCKS, sb, ub), F32),
                   jax.ShapeDtypeStruct((1, sw), F32)],
        scratch_shapes=[pltpu.VMEM((lc, gn), F32), pltpu.VMEM((lc, gn), F32),
                        pltpu.VMEM((lc + pad, gn), F32), pltpu.VMEM((lc + pad, gn), F32),
                        pltpu.VMEM((lc + pad, gn), F32), pltpu.VMEM((lc + pad, gn), F32),
                        pltpu.VMEM((tail + lc, gn), F32), pltpu.VMEM((tail + lc, gn), F32),
                        pltpu.VMEM((1, gn), F32), pltpu.VMEM((1, gn), F32)],
        compiler_params=_params("arbitrary"))(dge, y, uin, s_re, s_im, s_re, s_im, lrdt, ang, bbt_re, bbt_im, cct_re, cct_im, d_skip)


def _blockdiag_b(bb, sw):
    gpb = (sw // SSM_GROUP) // SSM_BLOCKS
    b4 = bb.reshape(SSM_BLOCKS, gpb, SSM_STATE, SSM_GROUP)
    eye = jnp.eye(gpb, dtype=bb.dtype)
    out = jnp.einsum('qgnh,gk->qghkn', b4, eye)
    return out.reshape(SSM_BLOCKS, gpb * SSM_GROUP, gpb * SSM_STATE)


def _blockdiag_c(cc, sw):
    gpb = (sw // SSM_GROUP) // SSM_BLOCKS
    c4 = cc.reshape(SSM_BLOCKS, gpb, SSM_GROUP, SSM_STATE)
    eye = jnp.eye(gpb, dtype=cc.dtype)
    out = jnp.einsum('qghn,gk->qgnkh', c4, eye)
    return out.reshape(SSM_BLOCKS, gpb * SSM_STATE, gpb * SSM_GROUP)


def _diag_of_b(dbb, sw):
    gpb = (sw // SSM_GROUP) // SSM_BLOCKS
    d5 = dbb.reshape(SSM_BLOCKS, gpb, SSM_GROUP, gpb, SSM_STATE)
    return jnp.einsum('qghgn->qgnh', d5).reshape(SSM_BLOCKS * gpb * SSM_STATE, SSM_GROUP)


def _diag_of_c(dcc, sw):
    gpb = (sw // SSM_GROUP) // SSM_BLOCKS
    d5 = dcc.reshape(SSM_BLOCKS, gpb, SSM_STATE, gpb, SSM_GROUP)
    return jnp.einsum('qgngh->qghn', d5).reshape(SSM_BLOCKS * gpb, SSM_GROUP, SSM_STATE)


def _ada_fwd(name, c_all, w_sh, b_sh):
    nb, d = c_all.shape
    ncol = w_sh.shape[1]
    tn = _pick(ncol, 768)

    def body(c_ref, w_ref, b_ref, o_ref):
        cv = c_ref[...]
        sil = cv * _sigmoid(cv)
        o_ref[...] = jnp.dot(sil, w_ref[...], preferred_element_type=F32, precision=lax.Precision.HIGHEST) + b_ref[...]

    return _pcall(body, name=name, grid=(ncol // tn,),
                  in_specs=[pl.BlockSpec((nb, d), lambda j: (0, 0)), pl.BlockSpec((d, tn), lambda j: (0, j)),
                            pl.BlockSpec((1, tn), lambda j: (0, j))],
                  out_specs=pl.BlockSpec((nb, tn), lambda j: (0, j)),
                  out_shape=jax.ShapeDtypeStruct((nb, ncol), F32), compiler_params=_params("parallel"))(c_all, w_sh, b_sh)


def _ada_bwd(name, c_all, dmod_sh):
    nb, d = c_all.shape
    ncol = dmod_sh.shape[1]
    tn = _pick(ncol, 768)

    def body(c_ref, g_ref, o_ref):
        cv = c_ref[...]
        sil = cv * _sigmoid(cv)
        o_ref[...] = lax.dot_general(sil, g_ref[...], (((0,), (0,)), ((), ())), preferred_element_type=F32,
                                     precision=lax.Precision.HIGHEST)

    return _pcall(body, name=name, grid=(ncol // tn,),
                  in_specs=[pl.BlockSpec((nb, d), lambda j: (0, 0)), pl.BlockSpec((nb, tn), lambda j: (0, j))],
                  out_specs=pl.BlockSpec((d, tn), lambda j: (0, j)),
                  out_shape=jax.ShapeDtypeStruct((d, ncol), F32), compiler_params=_params("parallel"))(c_all, dmod_sh)


def _place():
    return lax.axis_index("x"), lax.axis_index("y"), lax.axis_index("c")


def _allgather_small(name, blk, deps=()):
    m_per, n = blk.shape

    def body(x_ref, *rest):
        out_ref, send_sems, recv_sems, local_sem = rest[len(deps):]
        x, y, c = _place()
        me, sibling = (x, y, c), (x, y, 1 - c)
        chips = [(1 - x, y), (x, 1 - y), (1 - x, 1 - y)]

        def rows(px, py, pc):
            return out_ref.at[pl.ds((4 * px + 2 * py + pc) * m_per, m_per), :]

        def copy(k, block, to, src=None):
            return pltpu.make_async_remote_copy(
                src_ref=rows(*block) if src is None else src, dst_ref=rows(*block),
                send_sem=send_sems.at[k], recv_sem=recv_sems.at[k], device_id=to, device_id_type=MESH)

        mine = pltpu.make_async_copy(x_ref, rows(*me), local_sem)
        mine.start()
        first = [copy(0, me, sibling, src=x_ref)]
        first += [copy(1 + j, me, (*chip, c), src=x_ref) for j, chip in enumerate(chips)]
        for cp in first:
            cp.start()
        passed = [copy(4 + j, (*chip, c), sibling) for j, chip in enumerate(chips)]
        for j, chip in enumerate(chips):
            copy(1 + j, (*chip, c), me).wait_recv()
            passed[j].start()
        copy(0, sibling, me).wait_recv()
        for j, chip in enumerate(chips):
            copy(4 + j, (*chip, 1 - c), me).wait_recv()
        for cp in first + passed:
            cp.wait_send()
        mine.wait()

    return _pcall(body, name=name, out_shape=jax.ShapeDtypeStruct((N_DEV * m_per, n), blk.dtype),
                  in_specs=[pl.BlockSpec(memory_space=pltpu.VMEM)] + [ANY_SPEC] * len(deps),
                  out_specs=pl.BlockSpec(memory_space=pltpu.VMEM),
                  scratch_shapes=[pltpu.SemaphoreType.DMA((7,)), pltpu.SemaphoreType.DMA((7,)), pltpu.SemaphoreType.DMA],
                  compiler_params=pltpu.CompilerParams(vmem_limit_bytes=VMEM_LIMIT))(blk, *deps)


def _other_chips(x, y):
    return [(1 - x, y), (x, 1 - y), (1 - x, 1 - y)]


def _place_shards(shards, s_me):
    return [lax.dynamic_update_slice(lax.empty((N_CHIPS,) + s.shape, s.dtype), s[None], (s_me, 0, 0)) for s in shards]


def _ag_copy(src, land, send, recv, wi, j, chip, x, y, c):
    hr = src.shape[0] // 2
    half = pl.ds(pl.multiple_of(c * hr, 16), hr)
    return pltpu.make_async_remote_copy(
        src_ref=src.at[half, :], dst_ref=land.at[2 * x + y, half, :], send_sem=send.at[3 * wi + j], recv_sem=recv.at[3 * wi + j],
        device_id=(chip[0], chip[1], c), device_id_type=MESH)


def _ag_start(name, shards, lands, groups, deps=()):
    nw, ng, nd = len(shards), len(groups), len(deps)

    def body(*refs):
        src, land = refs[:nw], refs[nw:2 * nw]
        sems = refs[2 * nw + nd:2 * nw + nd + 2 * ng]
        token = refs[-1]
        x, y, c = _place()
        for gi, grp in enumerate(groups):
            for wi, w in enumerate(grp):
                for j, chip in enumerate(_other_chips(x, y)):
                    _ag_copy(src[w], land[w], sems[2 * gi], sems[2 * gi + 1], wi, j, chip, x, y, c).start()
        token[...] = jnp.zeros_like(token)

    sem_shapes = []
    for grp in groups:
        sem_shapes += [pltpu.SemaphoreType.DMA((3 * len(grp),))] * 2
    out_shape = sem_shapes + [pltpu.HBM(s.shape, s.dtype) for s in shards] + [pltpu.HBM(l.shape, l.dtype) for l in lands]
    out_shape += [jax.ShapeDtypeStruct((8, 128), F32)]
    res = _pcall(body, name=name, out_shape=out_shape, in_specs=[HBM_SPEC] * (2 * nw) + [ANY_SPEC] * nd,
                 out_specs=[SEM_SPEC] * (2 * ng) + [HBM_SPEC] * (2 * nw) + [pl.BlockSpec(memory_space=pltpu.VMEM)],
                 input_output_aliases={i: 2 * ng + i for i in range(2 * nw)},
                 compiler_params=pltpu.CompilerParams(has_side_effects=EFFECT))(
                     *[_hbm(s) for s in shards], *[_hbm(l) for l in lands], *deps)
    sems = [(res[2 * gi], res[2 * gi + 1]) for gi in range(ng)]
    return sems, list(res[2 * ng:2 * ng + nw]), list(res[2 * ng + nw:2 * ng + 2 * nw]), res[-1]


def _ag_wait(name, shards, lands, send, recv, after):
    n = len(shards)

    def body(*refs):
        src, land = refs[:n], refs[n:2 * n]
        send_sem, recv_sem = refs[2 * n], refs[2 * n + 1]
        x, y, c = _place()
        for wi in range(n):
            for j, chip in enumerate(_other_chips(x, y)):
                _ag_copy(src[wi], land[wi], send_sem, recv_sem, wi, j, chip, x, y, c).wait_send()
                _ag_copy(src[wi], land[wi], send_sem, recv_sem, wi, j, chip, chip[0], chip[1], c).wait_recv()

    res = _pcall(body, name=name, out_shape=[pltpu.HBM(a.shape, a.dtype) for a in list(shards) + list(lands)],
                 in_specs=[HBM_SPEC] * (2 * n) + [SEM_SPEC, SEM_SPEC, ANY_SPEC], out_specs=[HBM_SPEC] * (2 * n),
                 input_output_aliases={i: i for i in range(2 * n)},
                 compiler_params=pltpu.CompilerParams(has_side_effects=EFFECT))(*shards, *lands, send, recv, after)
    return list(res[n:])


def _ag_forward(name, lands):
    n = len(lands)

    def body(*refs):
        out = refs[n:2 * n]
        send, recv = refs[2 * n], refs[2 * n + 1]
        x, y, c = _place()
        sib = (x, y, 1 - c)
        cps = []
        for wi in range(n):
            hr = out[wi].shape[1] // 2
            for j, (cx, cy) in enumerate(_other_chips(x, y)):
                got = out[wi].at[2 * cx + cy, pl.ds(pl.multiple_of(c * hr, 16), hr), :]
                cp = pltpu.make_async_remote_copy(src_ref=got, dst_ref=got, send_sem=send.at[3 * wi + j], recv_sem=recv.at[3 * wi + j],
                                                  device_id=sib, device_id_type=MESH)
                cp.start()
                cps.append(cp)
        for wi in range(n):
            hr = out[wi].shape[1] // 2
            for j, (cx, cy) in enumerate(_other_chips(x, y)):
                got = out[wi].at[2 * cx + cy, pl.ds(pl.multiple_of((1 - c) * hr, 16), hr), :]
                pltpu.make_async_remote_copy(src_ref=got, dst_ref=got, send_sem=send.at[3 * wi + j], recv_sem=recv.at[3 * wi + j],
                                             device_id=sib, device_id_type=MESH).wait_recv()
        for cp in cps:
            cp.wait_send()

    res = _pcall(body, name=name, out_shape=[jax.ShapeDtypeStruct(l.shape, l.dtype) for l in lands],
                 in_specs=[ANY_SPEC] * n, out_specs=[ANY_SPEC] * n, input_output_aliases={i: i for i in range(n)},
                 scratch_shapes=[pltpu.SemaphoreType.DMA((3 * n,)), pltpu.SemaphoreType.DMA((3 * n,))])(*lands)
    return list(res)


def _peers(x, y, c):
    offs = [(dx, dy, dc) for dx in (0, 1) for dy in (0, 1) for dc in (0, 1)][1:]
    return [(1 - x if dx else x, 1 - y if dy else y, 1 - c if dc else c) for dx, dy, dc in offs]


def _rs_copy(g_ref, land_ref, send, recv, wi, k, to, sender):
    hr = g_ref.shape[1] // 2
    return pltpu.make_async_remote_copy(
        src_ref=g_ref.at[2 * to[0] + to[1], pl.ds(pl.multiple_of(to[2] * hr, 16), hr), :], dst_ref=land_ref.at[sender],
        send_sem=send.at[7 * wi + k], recv_sem=recv.at[7 * wi + k], device_id=to, device_id_type=MESH)


def _rs_start(name, gs):
    n = len(gs)
    lands = [lax.empty((N_DEV, g.shape[1] // 2, g.shape[2]), BF16) for g in gs]

    def body(*refs):
        g, land = refs[:n], refs[n:2 * n]
        send, recv = refs[2 * n], refs[2 * n + 1]
        token = refs[-1]
        x, y, c = _place()
        me = 4 * x + 2 * y + c
        for wi in range(n):
            for k, to in enumerate(_peers(x, y, c)):
                _rs_copy(g[wi], land[wi], send, recv, wi, k, to, me).start()
        token[...] = jnp.zeros_like(token)

    out_shape = [pltpu.SemaphoreType.DMA((7 * n,))] * 2 + [pltpu.HBM(a.shape, a.dtype) for a in list(gs) + lands]
    out_shape += [jax.ShapeDtypeStruct((8, 128), F32)]
    res = _pcall(body, name=name, out_shape=out_shape, in_specs=[HBM_SPEC] * (2 * n),
                 out_specs=[SEM_SPEC] * 2 + [HBM_SPEC] * (2 * n) + [pl.BlockSpec(memory_space=pltpu.VMEM)],
                 input_output_aliases={i: 2 + i for i in range(2 * n)},
                 compiler_params=pltpu.CompilerParams(has_side_effects=EFFECT))(
                     *[_hbm(a) for a in gs], *[_hbm(a) for a in lands])
    return res[0], res[1], list(res[2:2 + n]), list(res[2 + n:2 + 2 * n]), res[-1]


def _rs_wait(name, gs, lands, send, recv, after):
    n = len(gs)

    def body(*refs):
        g, land = refs[:n], refs[n:2 * n]
        send_sem, recv_sem = refs[2 * n], refs[2 * n + 1]
        x, y, c = _place()
        me = 4 * x + 2 * y + c
        for wi in range(n):
            for k, to in enumerate(_peers(x, y, c)):
                _rs_copy(g[wi], land[wi], send_sem, recv_sem, wi, k, to, me).wait_send()
                _rs_copy(g[wi], land[wi], send_sem, recv_sem, wi, k, (x, y, c), 4 * to[0] + 2 * to[1] + to[2]).wait_recv()

    res = _pcall(body, name=name, out_shape=[pltpu.HBM(a.shape, a.dtype) for a in list(gs) + list(lands)],
                 in_specs=[HBM_SPEC] * (2 * n) + [SEM_SPEC, SEM_SPEC, ANY_SPEC], out_specs=[HBM_SPEC] * (2 * n),
                 input_output_aliases={i: i for i in range(2 * n)},
                 compiler_params=pltpu.CompilerParams(has_side_effects=EFFECT))(*gs, *lands, send, recv, after)
    return list(res[:n]), list(res[n:])


def _bc_copy(blk_ref, land_ref, send, recv, k, to, slot):
    return pltpu.make_async_remote_copy(src_ref=blk_ref, dst_ref=land_ref.at[slot], send_sem=send.at[k], recv_sem=recv.at[k],
                                        device_id=to, device_id_type=MESH)


def _bcast_start(name, blk):
    land = lax.empty((N_DEV,) + blk.shape, blk.dtype)

    def body(blk_ref, land_ref, send, recv, blk_thru, land_thru, token):
        x, y, c = _place()
        for k, to in enumerate(_peers(x, y, c)):
            _bc_copy(blk_ref, land_ref, send, recv, k, to, 4 * x + 2 * y + c).start()
        token[...] = jnp.zeros_like(token)

    return _pcall(body, name=name,
                  out_shape=[pltpu.SemaphoreType.DMA((7,)), pltpu.SemaphoreType.DMA((7,)), pltpu.HBM(blk.shape, blk.dtype),
                             pltpu.HBM(land.shape, land.dtype), jax.ShapeDtypeStruct((8, 128), F32)],
                  in_specs=[HBM_SPEC, HBM_SPEC], out_specs=[SEM_SPEC, SEM_SPEC, HBM_SPEC, HBM_SPEC, pl.BlockSpec(memory_space=pltpu.VMEM)],
                  input_output_aliases={0: 2, 1: 3}, compiler_params=pltpu.CompilerParams(has_side_effects=EFFECT))(_hbm(blk), _hbm(land))


def _bcast_wait(name, blk, land, send, recv, after):
    def body(blk_ref, land_ref, send_sem, recv_sem, after_ref, blk_thru, land_thru):
        x, y, c = _place()
        for k, to in enumerate(_peers(x, y, c)):
            _bc_copy(blk_ref, land_ref, send_sem, recv_sem, k, to, 4 * x + 2 * y + c).wait_send()
            _bc_copy(blk_ref, land_ref, send_sem, recv_sem, k, to, 4 * to[0] + 2 * to[1] + to[2]).wait_recv()

    return _pcall(body, name=name, out_shape=[pltpu.HBM(blk.shape, blk.dtype), pltpu.HBM(land.shape, land.dtype)],
                  in_specs=[HBM_SPEC, HBM_SPEC, SEM_SPEC, SEM_SPEC, ANY_SPEC], out_specs=[HBM_SPEC, HBM_SPEC],
                  input_output_aliases={0: 0, 1: 1}, compiler_params=pltpu.CompilerParams(has_side_effects=EFFECT))(
                      blk, land, send, recv, after)[1]


def _rs_sum(name, g, land):
    _, rs, cs = g.shape
    hr = rs // 2
    ch = _pick(hr, 64, 16)

    def body(g_ref, land_ref, out_ref, recv, local_sems, sib_sems):
        x, y, c = _place()
        me = 4 * x + 2 * y + c
        cps = [pltpu.make_async_copy(g_ref.at[2 * x + y, pl.ds(pl.multiple_of(c * hr, 16), hr), :], recv.at[me], local_sems.at[7])]
        for k, (tx, ty, tc) in enumerate(_peers(x, y, c)):
            slot = 4 * tx + 2 * ty + tc
            cps.append(pltpu.make_async_copy(land_ref.at[slot], recv.at[slot], local_sems.at[k]))
        for cp in cps:
            cp.start()
        for cp in cps:
            cp.wait()
        base = pl.multiple_of(c * hr, 16)
        for r0 in range(0, hr, ch):
            acc = recv[0, r0:r0 + ch, :].astype(F32)
            for k in range(1, N_DEV):
                acc = acc + recv[k, r0:r0 + ch, :].astype(F32)
            out_ref[pl.ds(base + r0, ch), :] = acc
        half = out_ref.at[pl.ds(base, hr), :]
        sib = pltpu.make_async_remote_copy(src_ref=half, dst_ref=half, send_sem=sib_sems.at[0], recv_sem=sib_sems.at[1],
                                           device_id=(x, y, 1 - c), device_id_type=MESH)
        sib.start()
        other = out_ref.at[pl.ds(pl.multiple_of((1 - c) * hr, 16), hr), :]
        pltpu.make_async_remote_copy(src_ref=other, dst_ref=other, send_sem=sib_sems.at[0], recv_sem=sib_sems.at[1],
                                     device_id=(x, y, 1 - c), device_id_type=MESH).wait_recv()
        sib.wait_send()

    return _pcall(
        body, name=name, out_shape=jax.ShapeDtypeStruct((rs, cs), F32),
        in_specs=[ANY_SPEC, ANY_SPEC], out_specs=pl.BlockSpec(memory_space=pltpu.VMEM),
        scratch_shapes=[pltpu.VMEM((N_DEV, hr, cs), BF16), pltpu.SemaphoreType.DMA((8,)), pltpu.SemaphoreType.DMA((2,))],
        compiler_params=pltpu.CompilerParams(vmem_limit_bytes=VMEM_LIMIT))(g, land)


def _ffn_fwd(tag, x, g, sh, sc, gt, get_w_in, get_w_out, deps=()):
    t, d = x.shape
    h = _norm_fwd(tag + "_norm", x, g, sc, sh, deps=deps)
    ab = _mm(tag + "_in", h, get_w_in(h), b_stacked=True, tm=1024, tn=1408, j_outer=True)
    act = _swiglu_fwd(tag + "_act", ab)
    w_out = get_w_out(act)
    blk = (512, d)

    def epi(acc, res, scale):
        return res + (0.5 * scale) * acc, acc

    x_new, y = _mm(tag + "_out", act, w_out, tm=512, tn=d, epi=epi,
                   epi_ins=[(x, blk, lambda i, j: (i, j)), (gt, (1, d), lambda i, j: (0, j))],
                   epi_outs=[((t, d), F32, blk, lambda i, j: (i, j)), ((t, d), BF16, blk, lambda i, j: (i, j))])
    return x_new, (x, h, ab, act, y)


def _ffn_bwd(tag, dx_new, saved, g, sc, gt, w_in, w_out, start_rs, deps=()):
    x, h, ab, act, y = saved
    d = x.shape[1]
    f = act.shape[1]
    dy, dgt = _resgate_bwd(tag + "_dgate", dx_new, y, gt, 0.5, deps=deps)
    dact = _mm(tag + "_dact", dy, w_out, tb=True, tn=2816)
    dw_out = _mm(tag + "_dwout", act, dy, ta=True, tm=1408, tn=d, tk=2048)
    tok = start_rs("out", dw_out.reshape(N_CHIPS, f // N_CHIPS, d))
    dab = _swiglu_bwd(tag + "_dswiglu", dact, ab, deps=(tok,))
    dw_in = _mm(tag + "_dwin", h, dab, ta=True, out_stacked=True, tm=d, tn=1408, tk=2048)
    tok = start_rs("in", dw_in)
    dh = _mm(tag + "_dh", dab, w_in, tb=True, b_stacked=True, tn=d, tk=5632, deps=(tok,))
    dx, dsh, dsc, dg = _norm_bwd(tag + "_dnorm", dh, x, dx_new, g, sc)
    return dx, dsh, dsc, dgt, dg


def _row(v):
    return v.reshape(1, -1)


def _pack(parts):
    cols = []
    for p in parts:
        flat = p.reshape(-1).astype(F32)
        padn = (-flat.shape[0]) % 128
        cols.append(jnp.pad(flat, (0, padn)) if padn else flat)
    flat = jnp.concatenate(cols)
    padn = (-flat.shape[0]) % 1024
    if padn:
        flat = jnp.pad(flat, (0, padn))
    return flat.reshape(-1, 128)


def _unpack(packed, shapes):
    flat = packed.reshape(-1)
    out, off = [], 0
    for s in shapes:
        n = math.prod(s)
        out.append(flat[off:off + n].reshape(s))
        off += n + ((-n) % 128)
    return out


SMALL = ['b_ada', 'g_ffn1', 'g_mix', 'pool_w', 'pool_b', 'pool_scale', 'ssm_lam_re_log', 'ssm_lam_im', 'ssm_log_dt',
         'ssm_b_re', 'ssm_b_im', 'ssm_c_re', 'ssm_c_im', 'ssm_d', 'b_glu', 'g_ffn2', 'g_final']
BIG = ['w_ffn1_in', 'w_ffn1_out', 'w_in', 'w_pool_up', 'w_glu', 'w_ssm_up', 'w_out', 'w_ffn2_in', 'w_ffn2_out']
AG_GROUPS = [[0], [1], [2, 3, 4, 5, 6], [7, 8]]
SMALL_LATE = ['b_ada', 'g_ffn1']
WEIGHTS = ['w_ada', 'b_ada', 'g_ffn1', 'w_ffn1_in', 'w_ffn1_out', 'g_mix', 'w_in', 'pool_w', 'pool_b', 'pool_scale', 'w_pool_up',
           'ssm_lam_re_log', 'ssm_lam_im', 'ssm_log_dt', 'ssm_b_re', 'ssm_b_im', 'ssm_c_re', 'ssm_c_im', 'ssm_d', 'w_glu', 'b_glu',
           'w_ssm_up', 'w_out', 'g_ffn2', 'w_ffn2_in', 'w_ffn2_out', 'g_final']


def kernel(x, c, w_ada, b_ada, g_ffn1, w_ffn1_in, w_ffn1_out, g_mix, w_in, pool_w, pool_b, pool_scale, w_pool_up, ssm_lam_re_log, ssm_lam_im, ssm_log_dt, ssm_b_re, ssm_b_im, ssm_c_re, ssm_c_im, ssm_d, w_glu, b_glu, w_ssm_up, w_out, g_ffn2, w_ffn2_in, w_ffn2_out, g_final, loss_target, m_w_ada, m_b_ada, m_g_ffn1, m_w_ffn1_in, m_w_ffn1_out, m_g_mix, m_w_in, m_pool_w, m_pool_b, m_pool_scale, m_w_pool_up, m_ssm_lam_re_log, m_ssm_lam_im, m_ssm_log_dt, m_ssm_b_re, m_ssm_b_im, m_ssm_c_re, m_ssm_c_im, m_ssm_d, m_w_glu, m_b_glu, m_w_ssm_up, m_w_out, m_g_ffn2, m_w_ffn2_in, m_w_ffn2_out, m_g_final, v_w_ada, v_b_ada, v_g_ffn1, v_w_ffn1_in, v_w_ffn1_out, v_g_mix, v_w_in, v_pool_w, v_pool_b, v_pool_scale, v_w_pool_up, v_ssm_lam_re_log, v_ssm_lam_im, v_ssm_log_dt, v_ssm_b_re, v_ssm_b_im, v_ssm_c_re, v_ssm_c_im, v_ssm_d, v_w_glu, v_b_glu, v_w_ssm_up, v_w_out, v_g_ffn2, v_w_ffn2_in, v_w_ffn2_out, v_g_final):
    args = dict(locals())
    wt = {n: args[n] for n in WEIGHTS}
    mom = {n: args["m_" + n] for n in WEIGHTS}
    var = {n: args["v_" + n] for n in WEIGHTS}

    t, d = x.shape[1], x.shape[2]
    pw = pool_b.shape[1]
    sw = ssm_d.shape[1]
    ngrp = sw // SSM_GROUP
    gn = ngrp * SSM_STATE
    xi, yi, ci = _place()
    b_me = 4 * xi + 2 * yi + ci
    s_me = 2 * xi + yi
    x2d = x[0]
    tgt = loss_target[0]

    c_all = _allgather_small("ag_c", c.reshape(8, d // 8)).reshape(N_DEV, d)
    ncol = w_ada.shape[2]
    b_sh = lax.dynamic_slice(b_ada, (0, s_me * ncol), (1, ncol))
    mod_sh = _ada_fwd("ada_fwd", c_all, w_ada[0], b_sh)
    mod_all = _allgather_small("ag_mod", mod_sh)

    shards = [wt[n][0].astype(BF16) for n in BIG]
    ag_sems, shards_t, lands_t, tok = _ag_start("ag_start", shards, _place_shards(shards, s_me), AG_GROUPS, deps=(mod_all,))
    full = {}

    def weights(gi, after):
        grp = AG_GROUPS[gi]
        if BIG[grp[0]] not in full:
            ls = _ag_wait("ag_wait%d" % gi, [shards_t[w] for w in grp], [lands_t[w] for w in grp], *ag_sems[gi], after)
            for w, l in zip(grp, _ag_forward("ag_fwd%d" % gi, ls)):
                full[BIG[w]] = l
        return full

    mod_me = jnp.concatenate([lax.dynamic_slice(mod_all, (16 * s + b_me, 0), (1, ncol)) for s in range(N_CHIPS)], axis=1)
    mod = mod_me.reshape(9, d)
    sh1, sc1, gt1, sh2, sc2, gt2, sh3, sc3, gt3 = [mod[k:k + 1] for k in range(9)]

    f = w_ffn1_out.shape[1] * N_CHIPS

    x1, sav1 = _ffn_fwd("ffn1", x2d, g_ffn1, sh1, sc1, gt1, lambda after: weights(0, after)['w_ffn1_in'],
                        lambda after: weights(1, after)['w_ffn1_out'].reshape(f, d), deps=(tok,))

    h2 =_norm_fwd("mix_norm", x1, g_mix, sc2, sh2)
    weights(2, h2)
    wo = full['w_out'].reshape(d, d)
    uin = _mm("mix_in", h2, full['w_in'], b_stacked=True, tm=1024, tn=768, out_dtype=F32, j_outer=True)
    p_pool, z_pool = _pool_fwd("pool_fwd", uin, pool_w[0], pool_b, pool_scale, pw)
    y_pool = _mm("pool_up", p_pool, full['w_pool_up'], b_stacked=True, tm=1024, tn=d)

    col = lambda a: a.reshape(gn, 1)
    lrl_c, li_c = col(ssm_lam_re_log), col(ssm_lam_im)
    ldt_c = col(jnp.broadcast_to(ssm_log_dt.reshape(ngrp, 1), (ngrp, SSM_STATE)))
    b_re2, b_im2 = ssm_b_re.reshape(gn, SSM_GROUP), ssm_b_im.reshape(gn, SSM_GROUP)
    lrdt_c, ang_c, bb_re, bb_im = _ssm_prep("ssm_prep", lrl_c, li_c, ldt_c, b_re2, b_im2)
    lrdt, ang = lrdt_c.reshape(1, gn), ang_c.reshape(1, gn)
    bbd_re, bbd_im = _blockdiag_b(bb_re, sw), _blockdiag_b(bb_im, sw)
    ccd_re, ccd_im = _blockdiag_c(ssm_c_re[0], sw), _blockdiag_c(ssm_c_im[0], sw)
    y_s, ge, s_re, s_im = _ssm_fwd("ssm_fwd", uin, lrdt, ang, bbd_re.astype(BF16), bbd_im.astype(BF16),
                                   ccd_re.astype(BF16), ccd_im.astype(BF16), ssm_d, sw)
    gv = _mm("glu_in", ge, full['w_glu'], b_stacked=True, tm=1024, tn=d)
    sg = _glu_fwd("glu_act", gv, b_glu)
    y_ssm = _mm("ssm_up", sg, full['w_ssm_up'], b_stacked=True, tm=1024, tn=d)
    merged = _gates_fwd("gates", y_pool, y_ssm, uin, d, pw)
    blk = (512, d)

    def epi2(acc, res, scale):
        return res + scale * acc, acc

    x2, y2 = _mm("mix_out", merged, wo, tm=512, tn=d, epi=epi2,
                 epi_ins=[(x1, blk, lambda i, j: (i, j)), (gt2, (1, d), lambda i, j: (0, j))],
                 epi_outs=[((t, d), F32, blk, lambda i, j: (i, j)), ((t, d), BF16, blk, lambda i, j: (i, j))])

    x3, sav3 = _ffn_fwd("ffn2", x2, g_ffn2, sh3, sc3, gt3, lambda after: weights(3, after)['w_ffn2_in'],
                        lambda after: weights(3, after)['w_ffn2_out'].reshape(f, d))

    dx3, dg_final, loss_v = _loss_bwd("loss", x3, tgt, _row(g_final))
    loss = lax.psum(loss_v[0, 0], ("x", "y", "c"))

    gs = {}
    rs_open = []

    def start_rs(names, gbs):
        send, recv, g_thru, land_thru, token = _rs_start("rs_start_" + names[0], gbs)
        rs_open.append((names, send, recv, g_thru, land_thru))
        return token

    dx2, dsh3, dsc3, dgt3, gs['g_ffn2'] = _ffn_bwd(
        "ffn2b", dx3, sav3, g_ffn2, sc3, gt3, full['w_ffn2_in'], full['w_ffn2_out'].reshape(f, d),
        lambda key, g: start_rs(['w_ffn2_' + key], [g]))

    dy2, dgt2 = _resgate_bwd("mix_dgate", dx2, y2, gt2, 1.0)
    dmerged = _mm("mix_dmerged", dy2, wo, tb=True, tn=d)
    g_wo = _mm("mix_dwout", merged, dy2, ta=True, tm=d, tn=d, tk=2048).reshape(N_CHIPS, d // N_CHIPS, d)
    dy_pool, dy_ssm, dgl = _gates_bwd("gates_bwd", dmerged, y_pool, y_ssm, uin, d, pw)

    dp = _mm("pool_dup", dy_pool, full['w_pool_up'], tb=True, b_stacked=True, tm=1024, tn=pw, tk=d)
    g_wpu = _mm("pool_dwup", p_pool, dy_pool, ta=True, out_stacked=True, tm=pw, tn=d, tk=4096)
    du_pool, gs['pool_w'], gs['pool_b'], gs['pool_scale'] = _pool_bwd("pool_bwd", dp, z_pool, pool_w[0], pool_b, pool_scale)

    dsg = _mm("ssm_dup", dy_ssm, full['w_ssm_up'], tb=True, b_stacked=True, tm=1024, tn=sw, tk=d)
    g_wsu = _mm("ssm_dwup", sg, dy_ssm, ta=True, out_stacked=True, tm=sw, tn=d, tk=4096)
    tok = start_rs(['w_out', 'w_pool_up', 'w_ssm_up'], [g_wo, g_wpu, g_wsu])
    dgv, gs['b_glu'] = _glu_bwd("glu_bwd", dsg, gv, b_glu, deps=(tok,))
    dge = _mm("glu_dge", dgv, full['w_glu'], tb=True, b_stacked=True, tm=1024, tn=sw, tk=d)
    g_wglu = _mm("glu_dw", ge, dgv, ta=True, out_stacked=True, tm=sw, tn=d, tk=4096)
    tr = lambda a: jnp.swapaxes(a, 1, 2).astype(BF16)
    (du_ssm, g_abre, g_abim, g_bbd_re, g_bbd_im, g_ccd_re, g_ccd_im, gs['ssm_d']) = _ssm_bwd(
        "ssm_bwd", dge, y_s, uin, s_re, s_im, lrdt, ang, tr(bbd_re), tr(bbd_im), tr(ccd_re), tr(ccd_im), ssm_d, sw)
    gs['ssm_c_re'], gs['ssm_c_im'] = _diag_of_c(g_ccd_re, sw), _diag_of_c(g_ccd_im, sw)
    d_lrl, d_li, d_ldt, d_bre, d_bim = _ssm_param_bwd(
        "ssm_param_bwd", lrl_c, li_c, ldt_c, b_re2, b_im2, g_abre.reshape(gn, 1), g_abim.reshape(gn, 1),
        _diag_of_b(g_bbd_re, sw), _diag_of_b(g_bbd_im, sw))
    gs['ssm_lam_re_log'], gs['ssm_lam_im'] = d_lrl, d_li
    gs['ssm_log_dt'] = jnp.sum(d_ldt.reshape(ngrp, SSM_STATE), axis=1)
    gs['ssm_b_re'], gs['ssm_b_im'] = d_bre, d_bim

    duin = jnp.concatenate([du_pool, du_ssm, dgl], axis=1)
    g_win = _mm("mix_dwin", h2, duin, ta=True, out_stacked=True, tm=d, tn=768, tk=4096)
    tok = start_rs(['w_glu', 'w_in'], [g_wglu, g_win])
    dh2 = _mm("mix_dh", duin, full['w_in'], tb=True, b_stacked=True, tn=d, tk=3072, deps=(tok,))
    dx1, dsh2, dsc2, gs['g_mix'] = _norm_bwd("mix_dnorm", dh2, x1, dx2, g_mix, sc2)

    gs['g_final'] = dg_final
    early = [n for n in SMALL if n not in SMALL_LATE]
    sg_blk = _pack([gs[n] for n in early])
    sg_send, sg_recv, sg_blk, sg_land, tok = _bcast_start("sg_start", sg_blk)

    dx0, dsh1, dsc1, dgt1, gs['g_ffn1'] = _ffn_bwd(
        "ffn1b", dx1, sav1, g_ffn1, sc1, gt1, full['w_ffn1_in'], full['w_ffn1_out'].reshape(f, d),
        lambda key, g: start_rs(['w_ffn1_' + key], [g]), deps=(tok,))

    gs['b_ada'] = jnp.concatenate([dsh1, dsc1, dgt1, dsh2, dsc2, dgt2, dsh3, dsc3, dgt3], axis=1)
    late8 = _allgather_small("ag_late", _pack([gs[n] for n in SMALL_LATE]))
    dmod_all = late8.reshape(N_DEV, -1)[:, :9 * d]
    dmod_sh = lax.dynamic_slice(dmod_all, (0, s_me * ncol), (N_DEV, ncol))
    g_w_ada = _ada_bwd("ada_bwd", c_all, dmod_sh)

    grads, delta, new_m, new_v = {}, {}, {}, {}
    sg_land = _bcast_wait("sg_wait", sg_blk, sg_land, sg_send, sg_recv, g_w_ada)
    early8 = lax.dynamic_update_slice(sg_land, sg_blk[None], (b_me, 0, 0)).reshape(-1, 128)
    for tag, names, g8 in (("early", early, early8), ("late", SMALL_LATE, late8)):
        res = _adamw_small("adamw_small_" + tag, _pack([wt[n] for n in names]), g8,
                           _pack([mom[n] for n in names]), _pack([var[n] for n in names]))
        for dst, packed in zip((grads, delta, new_m, new_v), res):
            for n, val in zip(names, _unpack(packed, [wt[n].shape for n in names])):
                dst[n] = val

    dl, mn, vn = _adamw("adamw_w_ada", w_ada[0], g_w_ada, m_w_ada[0], v_w_ada[0])
    grads['w_ada'], delta['w_ada'], new_m['w_ada'], new_v['w_ada'] = g_w_ada[None], dl[None], mn[None], vn[None]
    after = dl
    for names, send, recv, g_thru, land_thru in rs_open:
        g_done, land_done = _rs_wait("rs_wait_" + names[0], g_thru, land_thru, send, recv, after)
        for n, g_own, land in zip(names, g_done, land_done):
            g_sum = _rs_sum("rs_sum_" + n, g_own, land)
            dl, mn, vn = _adamw("adamw_" + n, wt[n][0], g_sum, mom[n][0], var[n][0])
            grads[n], delta[n], new_m[n], new_v[n] = g_sum[None], dl[None], mn[None], vn[None]
            after = dl

    return (loss, dx0[None], *[grads[n] for n in WEIGHTS], *[delta[n] for n in WEIGHTS],
            *[new_m[n] for n in WEIGHTS], *[new_v[n] for n in WEIGHTS])
```

```python
import functools
import math

import jax
import jax.numpy as jnp
from jax import lax
from jax.experimental import pallas as pl
from jax.experimental.pallas import tpu as pltpu

F32 = jnp.float32
BF16 = jnp.bfloat16
MESH = pl.DeviceIdType.MESH

EPS = 1e-6
POOL_WINDOWS = (2, 4, 8, 16)
POOL_HALO = 16
SSM_GROUP = 16
SSM_STATE = 64
SSM_BLOCKS = 4
N_DEV = 8
N_CHIPS = 4
ADAM_LR = 0.001
ADAM_B1 = 0.9
ADAM_B2 = 0.999
ADAM_EPS = 1e-08
ADAM_WD = 0.01
ADAM_STEP = 10
VMEM_LIMIT = 56 * 1024 * 1024


ANY_SPEC = pl.BlockSpec(memory_space=pl.ANY)
HBM_SPEC = pl.BlockSpec(memory_space=pltpu.HBM)
SEM_SPEC = pl.BlockSpec(memory_space=pltpu.SEMAPHORE)
EFFECT = pltpu.SideEffectType.DATAFLOW_SIDE_EFFECTING


def _hbm(a):
    return pltpu.with_memory_space_constraint(a, pltpu.HBM)


def _pcall(body, **kw):
    return pl.pallas_call(body, **kw)


def _params(*sem):
    return pltpu.CompilerParams(dimension_semantics=sem, vmem_limit_bytes=VMEM_LIMIT)


def _pick(n, cap, mult=128):
    if n <= cap:
        return n
    best = None
    for d in range(mult, cap + 1, mult):
        if n % d == 0:
            best = d
    assert best is not None, (n, cap, mult)
    return best


def _sigmoid(v):
    return 1.0 / (1.0 + jnp.exp(-v))


def _rowwise(name, fn, ins, params, outs, reds, tm, deps=()):
    t = ins[0][0].shape[0]
    tm = min(tm, t)
    nb = t // tm
    ni, npar, no, nd = len(ins), len(params), len(outs), len(deps)

    def body(*refs):
        iv = [r[...] for r in refs[:ni]]
        pv = [r[...] for r in refs[ni:ni + npar]]
        o_refs = refs[ni + npar + nd:ni + npar + nd + no]
        r_refs = refs[ni + npar + nd + no:]
        ovals, rvals = fn(iv, pv)
        for o_ref, val in zip(o_refs, ovals):
            parts = val if isinstance(val, (list, tuple)) else [val]
            off = 0
            for p in parts:
                o_ref[:, off:off + p.shape[1]] = p.astype(o_ref.dtype)
                off += p.shape[1]
        if r_refs:
            @pl.when(pl.program_id(0) == 0)
            def _():
                for r in r_refs:
                    r[...] = jnp.zeros_like(r)
            for r, val in zip(r_refs, rvals):
                r[...] += val

    in_specs = [pl.BlockSpec((tm, w), functools.partial(lambda i, cb: (i, cb), cb=cb)) for (_, w, cb) in ins]
    in_specs += [pl.BlockSpec(p.shape, lambda i: (0, 0)) for p in params]
    in_specs += [ANY_SPEC] * nd
    out_shape = [jax.ShapeDtypeStruct((t, w), dt) for (w, dt) in outs]
    out_shape += [jax.ShapeDtypeStruct((1, w), F32) for w in reds]
    out_specs = [pl.BlockSpec((tm, w), lambda i: (i, 0)) for (w, _) in outs]
    out_specs += [pl.BlockSpec((1, w), lambda i: (0, 0)) for w in reds]
    res = _pcall(body, name=name, grid=(nb,), in_specs=in_specs, out_specs=out_specs, out_shape=out_shape,
                 compiler_params=_params("arbitrary"))(*[a for a, _, _ in ins], *params, *deps)
    return res


def _colsum(v):
    return jnp.sum(v, axis=0, keepdims=True)


def _mm(name, a, b, *, ta=False, tb=False, b_stacked=False, out_stacked=False, tm=512, tn=1024, tk=2816,
        out_dtype=BF16, epi=None, epi_ins=(), epi_outs=None, deps=(), j_outer=False):
    if ta:
        kdim, m = a.shape
    else:
        m, kdim = a.shape
    ns = None
    if b_stacked:
        ns = b.shape[2]
        n = b.shape[1] if tb else N_CHIPS * ns
        assert kdim == (N_CHIPS * ns if tb else b.shape[1]), (name, a.shape, b.shape)
    else:
        n = b.shape[0] if tb else b.shape[1]
        assert kdim == (b.shape[1] if tb else b.shape[0]), (name, a.shape, b.shape)
        if out_stacked:
            ns = n // N_CHIPS

    def shards(want):
        return max(g for g in (1, 2, 4) if g * ns <= max(want, ns))

    tm = _pick(m, tm, 128 if ta else 8)
    gn = gk = 1
    if (b_stacked and not tb) or out_stacked:
        gn = shards(tn)
        tn = gn * ns
    else:
        tn = _pick(n, tn)
    if b_stacked and tb:
        gk = shards(tk)
        tk = gk * ns
    else:
        tk = _pick(kdim, tk, 8 if ta else 128)
    nm, nn, nk = m // tm, n // tn, kdim // tk

    def ij(f):
        return (lambda g0, g1, k: f(g1, g0, k)) if j_outer else f

    a_spec = pl.BlockSpec((tk, tm), ij(lambda i, j, k: (k, i))) if ta else pl.BlockSpec((tm, tk), ij(lambda i, j, k: (i, k)))
    if b_stacked and not tb:
        b_spec = pl.BlockSpec((gn, tk, ns), ij(lambda i, j, k: (j, k, 0)))
    elif b_stacked and tb:
        b_spec = pl.BlockSpec((gk, tn, ns), ij(lambda i, j, k: (k, j, 0)))
    elif tb:
        b_spec = pl.BlockSpec((tn, tk), ij(lambda i, j, k: (j, k)))
    else:
        b_spec = pl.BlockSpec((tk, tn), ij(lambda i, j, k: (k, j)))
    dims = (((0 if ta else 1,), (1 if tb else 0,)), ((), ()))

    if epi_outs is None:
        if out_stacked:
            epi_outs = [((N_CHIPS, m, ns), out_dtype, (gn, tm, ns), lambda i, j: (j, i, 0))]
        else:
            epi_outs = [((m, n), out_dtype, (tm, tn), lambda i, j: (i, j))]
    ne, no, nd = len(epi_ins), len(epi_outs), len(deps)

    def body(a_ref, b_ref, *rest):
        e_refs = rest[:ne]
        o_refs = rest[ne + nd:ne + nd + no]
        scratch = rest[ne + nd + no:]
        av = a_ref[...].astype(BF16)
        if b_stacked and not tb:
            parts = [lax.dot_general(av, b_ref[s].astype(BF16), dims, preferred_element_type=F32) for s in range(gn)]
        elif b_stacked and tb:
            p = None
            for s in range(gk):
                q = lax.dot_general(av[:, s * ns:(s + 1) * ns], b_ref[s].astype(BF16), dims, preferred_element_type=F32)
                p = q if p is None else p + q
            parts = [p]
        else:
            parts = [lax.dot_general(av, b_ref[...].astype(BF16), dims, preferred_element_type=F32)]

        def finish(acc_parts):
            if epi is None and out_stacked:
                acc = acc_parts[0]
                for s in range(gn):
                    o_refs[0][s] = acc[:, s * ns:(s + 1) * ns].astype(out_dtype)
            elif epi is None:
                w = acc_parts[0].shape[1]
                for s, part in enumerate(acc_parts):
                    o_refs[0][:, s * w:(s + 1) * w] = part.astype(out_dtype)
            else:
                acc = acc_parts[0] if len(acc_parts) == 1 else jnp.concatenate(acc_parts, axis=1)
                for o_ref, val in zip(o_refs, epi(acc, *[r[...] for r in e_refs])):
                    o_ref[...] = val.astype(o_ref.dtype)

        if nk == 1:
            finish(parts)
        else:
            acc_ref = scratch[0]
            k = pl.program_id(2)
            w = parts[0].shape[1]

            @pl.when(k == 0)
            def _():
                for s, part in enumerate(parts):
                    acc_ref[:, s * w:(s + 1) * w] = part

            @pl.when(k > 0)
            def _():
                for s, part in enumerate(parts):
                    acc_ref[:, s * w:(s + 1) * w] += part

            @pl.when(k == nk - 1)
            def _():
                finish([acc_ref[...]])

    def _ij(f):
        return ij(lambda i, j, k: f(i, j))

    in_specs = [a_spec, b_spec] + [pl.BlockSpec(blk, _ij(f)) for (_, blk, f) in epi_ins] + [ANY_SPEC] * nd
    out_specs = [pl.BlockSpec(blk, _ij(f)) for (_, _, blk, f) in epi_outs]
    out_shape = [jax.ShapeDtypeStruct(s, dt) for (s, dt, _, _) in epi_outs]
    scratch = [pltpu.VMEM((tm, tn), F32)] if nk > 1 else []
    grid = (nn, nm, nk) if j_outer else (nm, nn, nk)
    res = _pcall(body, name=name, grid=grid, in_specs=in_specs, out_specs=out_specs, out_shape=out_shape,
                 scratch_shapes=scratch, compiler_params=_params("parallel", "parallel", "arbitrary"))(
                     a, b, *[x for x, _, _ in epi_ins], *deps)
    return res[0] if len(res) == 1 else res


def _norm_fwd(name, x, g, sc, sh, deps=()):
    d = x.shape[1]

    def fn(iv, pv):
        (xv,), (gv, scv, shv) = iv, pv
        r = lax.rsqrt(jnp.mean(xv * xv, axis=-1, keepdims=True) + EPS)
        return [xv * r * gv * (1.0 + scv) + shv], []

    return _rowwise(name, fn, [(x, d, 0)], [g, sc, sh], [(d, BF16)], [], 512, deps=deps)[0]


def _norm_bwd(name, dh, x, dres, g, sc):
    d = x.shape[1]

    def fn(iv, pv):
        (dhv, xv, drv), (gv, scv) = iv, pv
        dhv = dhv.astype(F32)
        r = lax.rsqrt(jnp.mean(xv * xv, axis=-1, keepdims=True) + EPS)
        xr = xv * r
        dn = dhv * (1.0 + scv)
        gd = gv * dn
        dx = drv + r * (gd - xr * jnp.mean(gd * xr, axis=-1, keepdims=True))
        return [dx], [_colsum(dhv), _colsum(dhv * xr * gv), _colsum(dn * xr)]

    return _rowwise(name, fn, [(dh, d, 0), (x, d, 0), (dres, d, 0)], [g, sc], [(d, F32)], [d, d, d], 256)


def _loss_bwd(name, x, tgt, g):
    d = x.shape[1]

    def fn(iv, pv):
        (xv, tv), (gv,) = iv, pv
        r = lax.rsqrt(jnp.mean(xv * xv, axis=-1, keepdims=True) + EPS)
        xr = xv * r
        e = xr * gv - tv
        loss_row = 0.5 * jnp.mean(e * e, axis=-1, keepdims=True)
        dout = e * (1.0 / d)
        gd = gv * dout
        dx = r * (gd - xr * jnp.mean(gd * xr, axis=-1, keepdims=True))
        return [dx], [_colsum(dout * xr), _colsum(loss_row * jnp.ones((1, 128), F32))]

    return _rowwise(name, fn, [(x, d, 0), (tgt, d, 0)], [g], [(d, F32)], [d, 128], 256)


def _resgate_bwd(name, dx, y, gt, factor, deps=()):
    d = dx.shape[1]

    def fn(iv, pv):
        (dxv, yv), (gtv,) = iv, pv
        return [factor * gtv * dxv], [_colsum(factor * dxv * yv.astype(F32))]

    return _rowwise(name, fn, [(dx, d, 0), (y, d, 0)], [gt], [(d, BF16)], [d], 512, deps=deps)


def _swiglu_fwd(name, ab):
    f = ab.shape[1] // 2

    def fn(iv, pv):
        a, b = iv[0].astype(F32), iv[1].astype(F32)
        return [a * _sigmoid(a) * b], []

    return _rowwise(name, fn, [(ab, f, 0), (ab, f, 1)], [], [(f, BF16)], [], 256)[0]


def _swiglu_bwd(name, dact, ab, deps=()):
    f = ab.shape[1] // 2

    def fn(iv, pv):
        dv, a, b = iv[0].astype(F32), iv[1].astype(F32), iv[2].astype(F32)
        s = _sigmoid(a)
        return [[dv * b * (s * (1.0 + a * (1.0 - s))), dv * (a * s)]], []

    return _rowwise(name, fn, [(dact, f, 0), (ab, f, 0), (ab, f, 1)], [], [(2 * f, BF16)], [], 256, deps=deps)[0]


def _gates_fwd(name, y_pool, y_ssm, uin, d, pw):
    cb = (2 * pw) // d

    def fn(iv, pv):
        yp, ys, glp, gls = [v.astype(F32) for v in iv]
        return [_sigmoid(glp) * yp + _sigmoid(gls) * ys], []

    return _rowwise(name, fn, [(y_pool, d, 0), (y_ssm, d, 0), (uin, d, cb), (uin, d, cb + 1)], [], [(d, BF16)], [], 256)[0]


def _gates_bwd(name, dm, y_pool, y_ssm, uin, d, pw):
    cb = (2 * pw) // d

    def fn(iv, pv):
        dmv, yp, ys, glp, gls = [v.astype(F32) for v in iv]
        sp, ss = _sigmoid(glp), _sigmoid(gls)
        return [dmv * sp, dmv * ss, [dmv * yp * sp * (1.0 - sp), dmv * ys * ss * (1.0 - ss)]], []

    return _rowwise(name, fn, [(dm, d, 0), (y_pool, d, 0), (y_ssm, d, 0), (uin, d, cb), (uin, d, cb + 1)], [],
                    [(d, BF16), (d, BF16), (2 * d, BF16)], [], 256)


def _glu_fwd(name, gv, b_glu):
    w = gv.shape[1] // 2

    def fn(iv, pv):
        gvv, (bv,) = iv[0].astype(F32) + pv[0], pv
        return [gvv[:, :w] * _sigmoid(gvv[:, w:])], []

    return _rowwise(name, fn, [(gv, 2 * w, 0)], [b_glu], [(w, BF16)], [], 512)[0]


def _glu_bwd(name, dsg, gv, b_glu, deps=()):
    w = gv.shape[1] // 2

    def fn(iv, pv):
        dv = iv[0].astype(F32)
        gvv = iv[1].astype(F32) + pv[0]
        val, s = gvv[:, :w], _sigmoid(gvv[:, w:])
        dval = dv * s
        dgate = dv * val * s * (1.0 - s)
        return [[dval, dgate]], [jnp.concatenate([_colsum(dval), _colsum(dgate)], axis=1)]

    return _rowwise(name, fn, [(dsg, w, 0), (gv, 2 * w, 0)], [b_glu], [(2 * w, BF16)], [2 * w], 512, deps=deps)


def _adamw(name, w, g, m, v, tm=256):
    c = w.shape[1]

    def fn(iv, pv):
        wv, gv, mv, vv = iv
        mn = ADAM_B1 * mv + (1.0 - ADAM_B1) * gv
        vn = ADAM_B2 * vv + (1.0 - ADAM_B2) * (gv * gv)
        m_hat = mn / (1.0 - ADAM_B1 ** ADAM_STEP)
        v_hat = vn / (1.0 - ADAM_B2 ** ADAM_STEP)
        delta = -ADAM_LR * (m_hat / (jnp.sqrt(v_hat) + ADAM_EPS) + ADAM_WD * wv)
        return [delta, mn, vn], []

    return _rowwise(name, fn, [(w, c, 0), (g, c, 0), (m, c, 0), (v, c, 0)], [], [(c, F32)] * 3, [], _pick(w.shape[0], tm, 8))


def _adamw_small(name, w, g8, m, v):
    r = w.shape[0]

    def body(w_ref, g_ref, m_ref, v_ref, go_ref, d_ref, mo_ref, vo_ref):
        gv = g_ref[0:r, :]
        for k in range(1, N_DEV):
            gv = gv + g_ref[k * r:(k + 1) * r, :]
        mn = ADAM_B1 * m_ref[...] + (1.0 - ADAM_B1) * gv
        vn = ADAM_B2 * v_ref[...] + (1.0 - ADAM_B2) * (gv * gv)
        m_hat = mn / (1.0 - ADAM_B1 ** ADAM_STEP)
        v_hat = vn / (1.0 - ADAM_B2 ** ADAM_STEP)
        go_ref[...] = gv
        d_ref[...] = -ADAM_LR * (m_hat / (jnp.sqrt(v_hat) + ADAM_EPS) + ADAM_WD * w_ref[...])
        mo_ref[...] = mn
        vo_ref[...] = vn

    return _pcall(body, name=name, out_shape=[jax.ShapeDtypeStruct((r, 128), F32)] * 4,
                  compiler_params=pltpu.CompilerParams(vmem_limit_bytes=VMEM_LIMIT))(w, g8, m, v)


def _pool_fwd(name, uin, pool_w, pool_b, pool_scale, pw, tm=512):
    t = uin.shape[0]
    tm = min(tm, t)
    ng = len(POOL_WINDOWS)
    gw = pw // ng

    def body(u_ref, w_ref, b_ref, s_ref, p_ref, z_ref, ext):
        i = pl.program_id(0)

        @pl.when(i == 0)
        def _():
            ext[0:POOL_HALO, :] = jnp.zeros((POOL_HALO, pw), F32)

        u = u_ref[...]
        ext[POOL_HALO:POOL_HALO + tm, :] = u
        pos = i * tm + lax.broadcasted_iota(jnp.int32, (tm, 1), 0)
        for k, win in enumerate(POOL_WINDOWS):
            cols = slice(k * gw, (k + 1) * gw)
            acc = u[:, cols]
            for j in range(1, win):
                acc = acc + ext[POOL_HALO - j:POOL_HALO - j + tm, cols]
            cnt = jnp.minimum(pos + 1, win).astype(F32)
            z = acc / cnt - u[:, cols]
            zp = jnp.dot(z.astype(BF16), w_ref[k].astype(BF16), preferred_element_type=F32) + b_ref[:, cols]
            p_ref[:, cols] = (zp * s_ref[:, cols]).astype(BF16)
            z_ref[:, cols] = z.astype(BF16)
        ext[0:POOL_HALO, :] = u[tm - POOL_HALO:tm, :]

    return _pcall(
        body, name=name, grid=(t // tm,),
        in_specs=[pl.BlockSpec((tm, pw), lambda i: (i, 0)), pl.BlockSpec(pool_w.shape, lambda i: (0, 0, 0)),
                  pl.BlockSpec(pool_b.shape, lambda i: (0, 0)), pl.BlockSpec(pool_scale.shape, lambda i: (0, 0))],
        out_specs=[pl.BlockSpec((tm, pw), lambda i: (i, 0))] * 2,
        out_shape=[jax.ShapeDtypeStruct((t, pw), BF16)] * 2,
        scratch_shapes=[pltpu.VMEM((POOL_HALO + tm, pw), F32)],
        compiler_params=_params("arbitrary"))(uin, pool_w, pool_b, pool_scale)


def _pool_bwd(name, dp, z, pool_w, pool_b, pool_scale, tm=512):
    t, pw = z.shape
    tm = min(tm, t)
    nb = t // tm
    ng = len(POOL_WINDOWS)
    gw = pw // ng

    def body(dp_ref, z_ref, w_ref, b_ref, s_ref, du_ref, dw_ref, db_ref, ds_ref, ext):
        i = pl.program_id(0)

        @pl.when(i == 0)
        def _():
            ext[tm:tm + POOL_HALO, :] = jnp.zeros((POOL_HALO, pw), F32)
            dw_ref[...] = jnp.zeros_like(dw_ref)
            db_ref[...] = jnp.zeros_like(db_ref)
            ds_ref[...] = jnp.zeros_like(ds_ref)

        pos = (nb - 1 - i) * tm + lax.broadcasted_iota(jnp.int32, (tm, 1), 0)
        for k, win in enumerate(POOL_WINDOWS):
            cols = slice(k * gw, (k + 1) * gw)
            zk = z_ref[:, cols]
            dpk = dp_ref[:, cols].astype(F32)
            wk = w_ref[k].astype(BF16)
            zp = jnp.dot(zk, wk, preferred_element_type=F32) + b_ref[:, cols]
            ds_ref[:, cols] += _colsum(dpk * zp)
            dzp = dpk * s_ref[:, cols]
            db_ref[:, cols] += _colsum(dzp)
            dzpb = dzp.astype(BF16)
            dz = lax.dot_general(dzpb, wk, (((1,), (1,)), ((), ())), preferred_element_type=F32)
            dw_ref[k] += lax.dot_general(zk, dzpb, (((0,), (0,)), ((), ())), preferred_element_type=F32)
            cnt = jnp.minimum(pos + 1, win).astype(F32)
            r = dz / cnt
            ext[0:tm, cols] = r
            acc = r - dz
            for j in range(1, win):
                acc = acc + ext[j:j + tm, cols]
            du_ref[:, cols] = acc.astype(BF16)
        ext[tm:tm + POOL_HALO, :] = ext[0:POOL_HALO, :]

    rev = lambda i: (nb - 1 - i, 0)
    return _pcall(
        body, name=name, grid=(nb,),
        in_specs=[pl.BlockSpec((tm, pw), rev), pl.BlockSpec((tm, pw), rev), pl.BlockSpec(pool_w.shape, lambda i: (0, 0, 0)),
                  pl.BlockSpec(pool_b.shape, lambda i: (0, 0)), pl.BlockSpec(pool_scale.shape, lambda i: (0, 0))],
        out_specs=[pl.BlockSpec((tm, pw), rev), pl.BlockSpec(pool_w.shape, lambda i: (0, 0, 0)),
                   pl.BlockSpec((1, pw), lambda i: (0, 0)), pl.BlockSpec((1, pw), lambda i: (0, 0))],
        out_shape=[jax.ShapeDtypeStruct((t, pw), BF16), jax.ShapeDtypeStruct(pool_w.shape, F32),
                   jax.ShapeDtypeStruct((1, pw), F32), jax.ShapeDtypeStruct((1, pw), F32)],
        scratch_shapes=[pltpu.VMEM((tm + POOL_HALO, pw), F32)],
        compiler_params=_params("arbitrary"))(dp, z, pool_w, pool_b, pool_scale)


def _ssm_disc(lrl, li, ldt):
    lr = -jnp.exp(lrl)
    dt = jnp.exp(ldt)
    mag = jnp.exp(lr * dt)
    ang = li * dt
    ab_re = mag * jnp.cos(ang)
    ab_im = mag * jnp.sin(ang)
    num_re = ab_re - 1.0
    num_im = ab_im
    den = lr * lr + li * li
    f_re = (num_re * lr + num_im * li) / den
    f_im = (num_im * lr - num_re * li) / den
    return lr, dt, mag, ang, ab_re, ab_im, num_re, num_im, den, f_re, f_im


def _ssm_prep(name, lrl, li, ldt, b_re, b_im):
    gn, h = b_re.shape

    def body(lrl_ref, li_ref, ldt_ref, br_ref, bi_ref, lrdt_ref, ang_ref, bbr_ref, bbi_ref):
        lr, dt, _, ang, _, _, _, _, _, f_re, f_im = _ssm_disc(lrl_ref[...], li_ref[...], ldt_ref[...])
        lrdt_ref[...] = lr * dt
        ang_ref[...] = ang
        br, bi = br_ref[...], bi_ref[...]
        bbr_ref[...] = f_re * br - f_im * bi
        bbi_ref[...] = f_re * bi + f_im * br

    col = jax.ShapeDtypeStruct((gn, 1), F32)
    mat = jax.ShapeDtypeStruct((gn, h), F32)
    return _pcall(body, name=name, out_shape=[col, col, mat, mat])(lrl, li, ldt, b_re, b_im)


def _ssm_param_bwd(name, lrl, li, ldt, b_re, b_im, g_abre, g_abim, g_bbre, g_bbim):
    gn, h = b_re.shape

    def body(lrl_ref, li_ref, ldt_ref, br_ref, bi_ref, gar_ref, gai_ref, gbr_ref, gbi_ref,
             dlrl_ref, dli_ref, dldt_ref, dbr_ref, dbi_ref):
        li_v = li_ref[...]
        lr, dt, mag, ang, ab_re, ab_im, num_re, num_im, den, f_re, f_im = _ssm_disc(lrl_ref[...], li_v, ldt_ref[...])
        br, bi = br_ref[...], bi_ref[...]
        gbr, gbi = gbr_ref[...], gbi_ref[...]
        g_fre = jnp.sum(gbr * br + gbi * bi, axis=1, keepdims=True)
        g_fim = jnp.sum(gbi * br - gbr * bi, axis=1, keepdims=True)
        dbr_ref[...] = gbr * f_re + gbi * f_im
        dbi_ref[...] = gbi * f_re - gbr * f_im
        g_num_re = (g_fre * lr - g_fim * li_v) / den
        g_num_im = (g_fre * li_v + g_fim * lr) / den
        g_den = -(g_fre * f_re + g_fim * f_im) / den
        g_lr = (g_fre * num_re + g_fim * num_im) / den + g_den * 2.0 * lr
        g_li = (g_fre * num_im - g_fim * num_re) / den + g_den * 2.0 * li_v
        g_are = gar_ref[...] + g_num_re
        g_aim = gai_ref[...] + g_num_im
        g_mag = g_are * jnp.cos(ang) + g_aim * jnp.sin(ang)
        g_ang = g_aim * ab_re - g_are * ab_im
        g_lrdt = g_mag * mag
        g_lr = g_lr + g_lrdt * dt
        g_dt = g_lrdt * lr + g_ang * li_v
        g_li = g_li + g_ang * dt
        dlrl_ref[...] = g_lr * lr
        dli_ref[...] = g_li
        dldt_ref[...] = g_dt * dt

    col = jax.ShapeDtypeStruct((gn, 1), F32)
    mat = jax.ShapeDtypeStruct((gn, h), F32)
    return _pcall(body, name=name, out_shape=[col, col, col, mat, mat])(lrl, li, ldt, b_re, b_im, g_abre, g_abim, g_bbre, g_bbim)


def _pow_rows(lrdt, ang, k):
    mag = jnp.exp(k * lrdt)
    return mag * jnp.cos(k * ang), mag * jnp.sin(k * ang)


def _ssm_chunk(t):
    return 256 if t >= 2048 else 128


def _to_segments(dst, srcs, jn):
    for q, src in enumerate(srcs):
        for j in range(jn):
            dst[8 * j:8 * j + 8, 128 * q:128 * (q + 1)] = src[pl.ds(j, 8, stride=jn), :]


def _from_segments(dst, src, q, jn, dtype):
    for s in range(8):
        dst[s * jn:(s + 1) * jn, 128 * q:128 * (q + 1)] = src[q, pl.ds(s, jn, stride=8), :].astype(dtype)


def _fill_rows8(dst_re, dst_im, v_re, v_im):
    for j in range(v_re.shape[0]):
        dst_re[8 * j:8 * j + 8, :] = jnp.broadcast_to(v_re[j:j + 1, :], (8, v_re.shape[1]))
        dst_im[8 * j:8 * j + 8, :] = jnp.broadcast_to(v_im[j:j + 1, :], (8, v_im.shape[1]))


def _cmul(ar, ai, br, bi):
    return ar * br - ai * bi, ar * bi + ai * br


def _cmul_conj(ar, ai, br, bi):
    return ar * br + ai * bi, ar * bi - ai * br


def _ssm_fwd(name, uin, lrdt, ang, bb_re, bb_im, cc_re, cc_im, d_skip, sw):
    t = uin.shape[0]
    gn = lrdt.shape[1]
    lc = _ssm_chunk(t)
    jn = lc // 8
    ub, sb = sw // SSM_BLOCKS, gn // SSM_BLOCKS
    nq = sw // 128
    assert ub == 128 and nq == SSM_BLOCKS

    def body(*refs):
        u_refs = refs[:nq]
        (lrdt_ref, ang_ref, bbr_ref, bbi_ref, ccr_ref, cci_ref, d_ref, y_ref, ge_ref, sre_ref, sim_ref,
         p_re, p_im, a_re, a_im, up, yp, gp, cst_re, cst_im, car_re, car_im) = refs[nq:]
        i = pl.program_id(0)
        lrdt_v, ang_v = lrdt_ref[...], ang_ref[...]

        @pl.when(i == 0)
        def _():
            k = (lax.broadcasted_iota(jnp.int32, (jn, 1), 0) + 1).astype(F32)
            _fill_rows8(p_re, p_im, *_pow_rows(lrdt_v, ang_v, k))
            car_re[...] = jnp.zeros_like(car_re)
            car_im[...] = jnp.zeros_like(car_im)

        _to_segments(up, u_refs, jn)
        u = up[...]
        ubf = u.astype(BF16)
        for q in range(SSM_BLOCKS):
            uq = ubf[:, q * ub:(q + 1) * ub]
            a_re[:, q * sb:(q + 1) * sb] = jnp.dot(uq, bbr_ref[q], preferred_element_type=F32)
            a_im[:, q * sb:(q + 1) * sb] = jnp.dot(uq, bbi_ref[q], preferred_element_type=F32)
        a1r, a1i = _pow_rows(lrdt_v, ang_v, 1.0)
        ajr, aji = _pow_rows(lrdt_v, ang_v, float(jn))
        for q in range(SSM_BLOCKS):
            cols = slice(q * sb, (q + 1) * sb)
            ar8 = jnp.broadcast_to(a1r[:, cols], (8, sb))
            ai8 = jnp.broadcast_to(a1i[:, cols], (8, sb))

            def step(j, carry, cols=cols, ar8=ar8, ai8=ai8):
                sr, si = carry
                rows = pl.ds(pl.multiple_of(j * 8, 8), 8)
                mr, mi = _cmul(ar8, ai8, sr, si)
                nr, ni = mr + a_re[rows, cols], mi + a_im[rows, cols]
                a_re[rows, cols] = nr
                a_im[rows, cols] = ni
                return nr, ni

            lax.fori_loop(1, jn, step, (a_re[0:8, cols], a_im[0:8, cols]), unroll=4)
        er, ei = a_re[lc - 8:lc, :], a_im[lc - 8:lc, :]
        gr, gi = car_re[...], car_im[...]
        for s in range(8):
            cst_re[s:s + 1, :] = gr
            cst_im[s:s + 1, :] = gi
            mr, mi = _cmul(ajr, aji, gr, gi)
            gr, gi = mr + er[s:s + 1, :], mi + ei[s:s + 1, :]
        car_re[...] = gr
        car_im[...] = gi
        for q in range(SSM_BLOCKS):
            cols = slice(q * sb, (q + 1) * sb)
            cr = jnp.tile(cst_re[:, cols], (jn, 1))
            ci = jnp.tile(cst_im[:, cols], (jn, 1))
            mr, mi = _cmul(p_re[:, cols], p_im[:, cols], cr, ci)
            sr, si = a_re[:, cols] + mr, a_im[:, cols] + mi
            sre_ref[:, cols] = sr
            sim_ref[:, cols] = si
            ycols = slice(q * ub, (q + 1) * ub)
            y = (jnp.dot(sr.astype(BF16), ccr_ref[q], preferred_element_type=F32)
                 - jnp.dot(si.astype(BF16), cci_ref[q], preferred_element_type=F32)
                 + d_ref[:, ycols] * u[:, ycols])
            yp[q] = y
            gp[q] = 0.5 * y * (1.0 + lax.erf(y * (1.0 / math.sqrt(2.0))))
            _from_segments(y_ref, yp, q, jn, F32)
            _from_segments(ge_ref, gp, q, jn, BF16)

    row = lambda i: (0, 0)
    blk3 = lambda i: (0, 0, 0)
    return _pcall(
        body, name=name, grid=(t // lc,),
        in_specs=[pl.BlockSpec((lc, 128), functools.partial(lambda i, q: (i, nq + q), q=q)) for q in range(nq)]
        + [pl.BlockSpec((1, gn), row), pl.BlockSpec((1, gn), row),
           pl.BlockSpec(bb_re.shape, blk3), pl.BlockSpec(bb_im.shape, blk3),
           pl.BlockSpec(cc_re.shape, blk3), pl.BlockSpec(cc_im.shape, blk3), pl.BlockSpec((1, sw), row)],
        out_specs=[pl.BlockSpec((lc, sw), lambda i: (i, 0)), pl.BlockSpec((lc, sw), lambda i: (i, 0)),
                   pl.BlockSpec((lc, gn), lambda i: (i, 0)), pl.BlockSpec((lc, gn), lambda i: (i, 0))],
        out_shape=[jax.ShapeDtypeStruct((t, sw), F32), jax.ShapeDtypeStruct((t, sw), BF16),
                   jax.ShapeDtypeStruct((t, gn), F32), jax.ShapeDtypeStruct((t, gn), F32)],
        scratch_shapes=[pltpu.VMEM((lc, gn), F32), pltpu.VMEM((lc, gn), F32), pltpu.VMEM((lc, gn), F32), pltpu.VMEM((lc, gn), F32),
                        pltpu.VMEM((lc, sw), F32), pltpu.VMEM((nq, lc, 128), F32), pltpu.VMEM((nq, lc, 128), F32),
                        pltpu.VMEM((8, gn), F32), pltpu.VMEM((8, gn), F32), pltpu.VMEM((1, gn), F32), pltpu.VMEM((1, gn), F32)],
        compiler_params=_params("arbitrary"))(*([uin] * nq), lrdt, ang, bb_re, bb_im, cc_re, cc_im, d_skip)


def _ssm_bwd(name, dge, y, uin, s_re, s_im, lrdt, ang, bbt_re, bbt_im, cct_re, cct_im, d_skip, sw):
    t = uin.shape[0]
    gn = lrdt.shape[1]
    lc = _ssm_chunk(t)
    nb = t // lc
    jn = lc // 8
    ub, sb = sw // SSM_BLOCKS, gn // SSM_BLOCKS
    nq = sw // 128
    tail = 8

    def body(*refs):
        dge_refs, y_refs, u_refs = refs[:nq], refs[nq:2 * nq], refs[2 * nq:3 * nq]
        (sre_ref, sim_ref, tre_ref, tim_ref, lrdt_ref, ang_ref, btr_ref, bti_ref, ctr_ref, cti_ref, d_ref,
         du_ref, dar_ref, dai_ref, dbr_ref, dbi_ref, dcr_ref, dci_ref, dd_ref,
         q_re, q_im, a_re, a_im, dgp, yp, up, dup, cst_re, cst_im, sp_re, sp_im, car_re, car_im) = refs[3 * nq:]
        i = pl.program_id(0)
        lrdt_v, ang_v = lrdt_ref[...], ang_ref[...]

        @pl.when(i == 0)
        def _():
            k = (jn - lax.broadcasted_iota(jnp.int32, (jn, 1), 0)).astype(F32)
            _fill_rows8(q_re, q_im, *_pow_rows(lrdt_v, ang_v, k))
            car_re[...] = jnp.zeros_like(car_re)
            car_im[...] = jnp.zeros_like(car_im)
            for r in (dar_ref, dai_ref, dbr_ref, dbi_ref, dcr_ref, dci_ref, dd_ref):
                r[...] = jnp.zeros_like(r)

        _to_segments(dgp, dge_refs, jn)
        _to_segments(yp, y_refs, jn)
        _to_segments(up, u_refs, jn)
        yv = yp[...]
        u = up[...]
        cdf = 0.5 * (1.0 + lax.erf(yv * (1.0 / math.sqrt(2.0))))
        pdf = jnp.exp(-0.5 * yv * yv) * (1.0 / math.sqrt(2.0 * math.pi))
        dy = dgp[...] * (cdf + yv * pdf)
        dd_ref[...] += _colsum(dy * u)
        dyb = dy.astype(BF16)
        ubf = u.astype(BF16)
        for q in range(SSM_BLOCKS):
            dq = dyb[:, q * ub:(q + 1) * ub]
            a_re[:, q * sb:(q + 1) * sb] = jnp.dot(dq, ctr_ref[q], preferred_element_type=F32)
            a_im[:, q * sb:(q + 1) * sb] = -jnp.dot(dq, cti_ref[q], preferred_element_type=F32)
        a1r, a1i = _pow_rows(lrdt_v, ang_v, 1.0)
        ajr, aji = _pow_rows(lrdt_v, ang_v, float(jn))
        for q in range(SSM_BLOCKS):
            cols = slice(q * sb, (q + 1) * sb)
            ar8 = jnp.broadcast_to(a1r[:, cols], (8, sb))
            ai8 = jnp.broadcast_to(a1i[:, cols], (8, sb))

            def step(jj, carry, cols=cols, ar8=ar8, ai8=ai8):
                sr, si = carry
                rows = pl.ds(pl.multiple_of((jn - 2 - jj) * 8, 8), 8)
                mr, mi = _cmul_conj(ar8, ai8, sr, si)
                nr, ni = mr + a_re[rows, cols], mi + a_im[rows, cols]
                a_re[rows, cols] = nr
                a_im[rows, cols] = ni
                return nr, ni

            lax.fori_loop(0, jn - 1, step, (a_re[lc - 8:lc, cols], a_im[lc - 8:lc, cols]), unroll=4)
        er, ei = a_re[0:8, :], a_im[0:8, :]
        hr, hi = car_re[...], car_im[...]
        for s in range(7, -1, -1):
            cst_re[s:s + 1, :] = hr
            cst_im[s:s + 1, :] = hi
            mr, mi = _cmul_conj(ajr, aji, hr, hi)
            hr, hi = mr + er[s:s + 1, :], mi + ei[s:s + 1, :]
        car_re[...] = hr
        car_im[...] = hi
        first = (i == nb - 1).astype(F32)
        sp_re[0:tail, :] = tre_ref[...] * (1.0 - first)
        sp_im[0:tail, :] = tim_ref[...] * (1.0 - first)
        sp_re[tail:2 * tail, :] = sre_ref[lc - 8:lc, :]
        sp_im[tail:2 * tail, :] = sim_ref[lc - 8:lc, :]
        tn_dims = (((0,), (0,)), ((), ()))
        for q in range(SSM_BLOCKS):
            cols = slice(q * sb, (q + 1) * sb)
            ycols = slice(q * ub, (q + 1) * ub)
            cr = jnp.tile(cst_re[:, cols], (jn, 1))
            ci = jnp.tile(cst_im[:, cols], (jn, 1))
            mr, mi = _cmul_conj(q_re[:, cols], q_im[:, cols], cr, ci)
            lam_r, lam_i = a_re[:, cols] + mr, a_im[:, cols] + mi
            s_r, s_i = sre_ref[:, cols], sim_ref[:, cols]
            p0r, p0i = sp_re[tail - 1:2 * tail - 1, cols], sp_im[tail - 1:2 * tail - 1, cols]
            l0r, l0i, l1r, l1i = lam_r[0:8], lam_i[0:8], lam_r[8:lc], lam_i[8:lc]
            pvr, pvi = s_r[0:lc - 8], s_i[0:lc - 8]
            dar_ref[:, cols] += _colsum(l1r * pvr + l1i * pvi) + _colsum(l0r * p0r + l0i * p0i)
            dai_ref[:, cols] += _colsum(l1i * pvr - l1r * pvi) + _colsum(l0i * p0r - l0r * p0i)
            lrb, lib = lam_r.astype(BF16), lam_i.astype(BF16)
            dup[q] = (jnp.dot(lrb, btr_ref[q], preferred_element_type=F32)
                      + jnp.dot(lib, bti_ref[q], preferred_element_type=F32) + d_ref[:, ycols] * dy[:, ycols])
            _from_segments(du_ref, dup, q, jn, BF16)
            uq = ubf[:, ycols]
            dbr_ref[q] += lax.dot_general(uq, lrb, tn_dims, preferred_element_type=F32)
            dbi_ref[q] += lax.dot_general(uq, lib, tn_dims, preferred_element_type=F32)
            dq = dyb[:, ycols]
            dcr_ref[q] += lax.dot_general(s_r.astype(BF16), dq, tn_dims, preferred_element_type=F32)
            dci_ref[q] -= lax.dot_general(s_i.astype(BF16), dq, tn_dims, preferred_element_type=F32)

    rev = lambda i: (nb - 1 - i, 0)
    tailmap = lambda i: (jnp.maximum((nb - 1 - i) * (lc // tail) - 1, 0), 0)
    row = lambda i: (0, 0)
    blk3 = lambda i: (0, 0, 0)
    colblk = lambda base: [pl.BlockSpec((lc, 128), functools.partial(lambda i, q: (nb - 1 - i, base + q), q=q)) for q in range(nq)]
    return _pcall(
        body, name=name, grid=(nb,),
        in_specs=colblk(0) + colblk(0) + colblk(nq)
        + [pl.BlockSpec((lc, gn), rev), pl.BlockSpec((lc, gn), rev),
           pl.BlockSpec((tail, gn), tailmap), pl.BlockSpec((tail, gn), tailmap),
           pl.BlockSpec((1, gn), row), pl.BlockSpec((1, gn), row),
           pl.BlockSpec(bbt_re.shape, blk3), pl.BlockSpec(bbt_im.shape, blk3),
           pl.BlockSpec(cct_re.shape, blk3), pl.BlockSpec(cct_im.shape, blk3), pl.BlockSpec((1, sw), row)],
        out_specs=[pl.BlockSpec((lc, sw), rev), pl.BlockSpec((1, gn), row), pl.BlockSpec((1, gn), row),
                   pl.BlockSpec((SSM_BLOCKS, ub, sb), blk3), pl.BlockSpec((SSM_BLOCKS, ub, sb), blk3),
                   pl.BlockSpec((SSM_BLOCKS, sb, ub), blk3), pl.BlockSpec((SSM_BLOCKS, sb, ub), blk3),
                   pl.BlockSpec((1, sw), row)],
        out_shape=[jax.ShapeDtypeStruct((t, sw), BF16), jax.ShapeDtypeStruct((1, gn), F32), jax.ShapeDtypeStruct((1, gn), F32),
                   jax.ShapeDtypeStruct((SSM_BLOCKS, ub, sb), F32), jax.ShapeDtypeStruct((SSM_BLOCKS, ub, sb), F32),
                   jax.ShapeDtypeStruct((SSM_BLOCKS, sb, ub), F32), jax.ShapeDtypeStruct((SSM_BLOCKS, sb, ub), F32),
                   jax.ShapeDtypeStruct((1, sw), F32)],
        scratch_shapes=[pltpu.VMEM((lc, gn), F32), pltpu.VMEM((lc, gn), F32), pltpu.VMEM((lc, gn), F32), pltpu.VMEM((lc, gn), F32),
                        pltpu.VMEM((lc, sw), F32), pltpu.VMEM((lc, sw), F32), pltpu.VMEM((lc, sw), F32),
                        pltpu.VMEM((nq, lc, 128), F32), pltpu.VMEM((8, gn), F32), pltpu.VMEM((8, gn), F32),
                        pltpu.VMEM((2 * tail, gn), F32), pltpu.VMEM((2 * tail, gn), F32),
                        pltpu.VMEM((1, gn), F32), pltpu.VMEM((1, gn), F32)],
        compiler_params=_params("arbitrary"))(*([dge] * nq), *([y] * nq), *([uin] * nq), s_re, s_im, s_re, s_im, lrdt, ang,
                                               bbt_re, bbt_im, cct_re, cct_im, d_skip)


def _ssm_fwd_doubling(name, uin, lrdt, ang, bb_re, bb_im, cc_re, cc_im, d_skip, sw):
    t = uin.shape[0]
    gn = lrdt.shape[1]
    lc = _ssm_chunk(t)
    nsteps = int(math.log2(lc))
    pad = lc // 2
    ub, sb = sw // SSM_BLOCKS, gn // SSM_BLOCKS

    def body(u_ref, lrdt_ref, ang_ref, bbr_ref, bbi_ref, ccr_ref, cci_ref, d_ref, y_ref, ge_ref, sre_ref, sim_ref,
             p_re, p_im, a_re, a_im, b_re, b_im, car_re, car_im):
        i = pl.program_id(0)
        lrdt_v, ang_v = lrdt_ref[...], ang_ref[...]

        @pl.when(i == 0)
        def _():
            k = (lax.broadcasted_iota(jnp.int32, (lc, 1), 0) + 1).astype(F32)
            pr, pi = _pow_rows(lrdt_v, ang_v, k)
            p_re[...] = pr
            p_im[...] = pi
            zeros = jnp.zeros((pad, gn), F32)
            a_re[0:pad, :] = zeros
            a_im[0:pad, :] = zeros
            b_re[0:pad, :] = zeros
            b_im[0:pad, :] = zeros
            car_re[...] = jnp.zeros_like(car_re)
            car_im[...] = jnp.zeros_like(car_im)

        u = u_ref[...]
        ubf = u.astype(BF16)
        for q in range(SSM_BLOCKS):
            uq = ubf[:, q * ub:(q + 1) * ub]
            a_re[pad:pad + lc, q * sb:(q + 1) * sb] = jnp.dot(uq, bbr_ref[q], preferred_element_type=F32)
            a_im[pad:pad + lc, q * sb:(q + 1) * sb] = jnp.dot(uq, bbi_ref[q], preferred_element_type=F32)
        bufs = [(a_re, a_im), (b_re, b_im)]
        for q in range(SSM_BLOCKS):
            cols = slice(q * sb, (q + 1) * sb)
            for j in range(nsteps):
                dd = 1 << j
                (s_re, s_im), (d_re, d_im) = bufs[j % 2], bufs[(j + 1) % 2]
                pr, pi = _pow_rows(lrdt_v[:, cols], ang_v[:, cols], float(dd))
                cr, ci = s_re[pad:pad + lc, cols], s_im[pad:pad + lc, cols]
                hr, hi = s_re[pad - dd:pad - dd + lc, cols], s_im[pad - dd:pad - dd + lc, cols]
                d_re[pad:pad + lc, cols] = cr + (pr * hr - pi * hi)
                d_im[pad:pad + lc, cols] = ci + (pr * hi + pi * hr)
            f_re, f_im = bufs[nsteps % 2]
            cr, ci = car_re[:, cols], car_im[:, cols]
            pr, pi = p_re[:, cols], p_im[:, cols]
            sr = f_re[pad:pad + lc, cols] + (pr * cr - pi * ci)
            si = f_im[pad:pad + lc, cols] + (pr * ci + pi * cr)
            sre_ref[:, cols] = sr
            sim_ref[:, cols] = si
            car_re[:, cols] = sr[lc - 1:lc, :]
            car_im[:, cols] = si[lc - 1:lc, :]
            ycols = slice(q * ub, (q + 1) * ub)
            y = (jnp.dot(sr.astype(BF16), ccr_ref[q], preferred_element_type=F32)
                 - jnp.dot(si.astype(BF16), cci_ref[q], preferred_element_type=F32)
                 + d_ref[:, ycols] * u[:, ycols])
            y_ref[:, ycols] = y
            ge_ref[:, ycols] = (0.5 * y * (1.0 + lax.erf(y * (1.0 / math.sqrt(2.0))))).astype(BF16)

    cblk = sw // sw
    row = lambda i: (0, 0)
    blk3 = lambda i: (0, 0, 0)
    return _pcall(
        body, name=name, grid=(t // lc,),
        in_specs=[pl.BlockSpec((lc, sw), lambda i: (i, cblk)), pl.BlockSpec((1, gn), row), pl.BlockSpec((1, gn), row),
                  pl.BlockSpec(bb_re.shape, blk3), pl.BlockSpec(bb_im.shape, blk3),
                  pl.BlockSpec(cc_re.shape, blk3), pl.BlockSpec(cc_im.shape, blk3), pl.BlockSpec((1, sw), row)],
        out_specs=[pl.BlockSpec((lc, sw), lambda i: (i, 0)), pl.BlockSpec((lc, sw), lambda i: (i, 0)),
                   pl.BlockSpec((lc, gn), lambda i: (i, 0)), pl.BlockSpec((lc, gn), lambda i: (i, 0))],
        out_shape=[jax.ShapeDtypeStruct((t, sw), F32), jax.ShapeDtypeStruct((t, sw), BF16),
                   jax.ShapeDtypeStruct((t, gn), F32), jax.ShapeDtypeStruct((t, gn), F32)],
        scratch_shapes=[pltpu.VMEM((lc, gn), F32), pltpu.VMEM((lc, gn), F32),
                        pltpu.VMEM((pad + lc, gn), F32), pltpu.VMEM((pad + lc, gn), F32),
                        pltpu.VMEM((pad + lc, gn), F32), pltpu.VMEM((pad + lc, gn), F32),
                        pltpu.VMEM((1, gn), F32), pltpu.VMEM((1, gn), F32)],
        compiler_params=_params("arbitrary"))(uin, lrdt, ang, bb_re, bb_im, cc_re, cc_im, d_skip)


def _ssm_bwd_doubling(name, dge, y, uin, s_re, s_im, lrdt, ang, bbt_re, bbt_im, cct_re, cct_im, d_skip, sw):
    t = uin.shape[0]
    gn = lrdt.shape[1]
    lc = _ssm_chunk(t)
    nb = t // lc
    nsteps = int(math.log2(lc))
    pad = lc // 2
    ub, sb = sw // SSM_BLOCKS, gn // SSM_BLOCKS
    tail = 8

    def body(dge_ref, y_ref, u_ref, sre_ref, sim_ref, tre_ref, tim_ref, lrdt_ref, ang_ref, btr_ref, bti_ref, ctr_ref, cti_ref,
             d_ref, du_ref, dar_ref, dai_ref, dbr_ref, dbi_ref, dcr_ref, dci_ref, dd_ref,
             q_re, q_im, a_re, a_im, b_re, b_im, sp_re, sp_im, car_re, car_im):
        i = pl.program_id(0)
        lrdt_v, ang_v = lrdt_ref[...], ang_ref[...]

        @pl.when(i == 0)
        def _():
            k = (lc - lax.broadcasted_iota(jnp.int32, (lc, 1), 0)).astype(F32)
            qr, qi = _pow_rows(lrdt_v, ang_v, k)
            q_re[...] = qr
            q_im[...] = qi
            zeros = jnp.zeros((pad, gn), F32)
            a_re[lc:lc + pad, :] = zeros
            a_im[lc:lc + pad, :] = zeros
            b_re[lc:lc + pad, :] = zeros
            b_im[lc:lc + pad, :] = zeros
            car_re[...] = jnp.zeros_like(car_re)
            car_im[...] = jnp.zeros_like(car_im)
            for r in (dar_ref, dai_ref, dbr_ref, dbi_ref, dcr_ref, dci_ref, dd_ref):
                r[...] = jnp.zeros_like(r)

        first = (i == nb - 1).astype(F32)
        sp_re[0:tail, :] = tre_ref[...] * (1.0 - first)
        sp_im[0:tail, :] = tim_ref[...] * (1.0 - first)
        sp_re[tail:tail + lc, :] = sre_ref[...]
        sp_im[tail:tail + lc, :] = sim_ref[...]

        yv = y_ref[...]
        u = u_ref[...]
        cdf = 0.5 * (1.0 + lax.erf(yv * (1.0 / math.sqrt(2.0))))
        pdf = jnp.exp(-0.5 * yv * yv) * (1.0 / math.sqrt(2.0 * math.pi))
        dy = dge_ref[...].astype(F32) * (cdf + yv * pdf)
        dd_ref[...] += _colsum(dy * u)
        dyb = dy.astype(BF16)
        ubf = u.astype(BF16)
        for q in range(SSM_BLOCKS):
            dq = dyb[:, q * ub:(q + 1) * ub]
            a_re[0:lc, q * sb:(q + 1) * sb] = jnp.dot(dq, ctr_ref[q], preferred_element_type=F32)
            a_im[0:lc, q * sb:(q + 1) * sb] = -jnp.dot(dq, cti_ref[q], preferred_element_type=F32)
        bufs = [(a_re, a_im), (b_re, b_im)]
        tn_dims = (((0,), (0,)), ((), ()))
        for q in range(SSM_BLOCKS):
            cols = slice(q * sb, (q + 1) * sb)
            ycols = slice(q * ub, (q + 1) * ub)
            for j in range(nsteps):
                ds = 1 << j
                (s_r, s_i), (d_r, d_i) = bufs[j % 2], bufs[(j + 1) % 2]
                pr, pi = _pow_rows(lrdt_v[:, cols], ang_v[:, cols], float(ds))
                cr, ci = s_r[0:lc, cols], s_i[0:lc, cols]
                hr, hi = s_r[ds:ds + lc, cols], s_i[ds:ds + lc, cols]
                d_r[0:lc, cols] = cr + (pr * hr + pi * hi)
                d_i[0:lc, cols] = ci + (pr * hi - pi * hr)
            f_r, f_i = bufs[nsteps % 2]
            cr, ci = car_re[:, cols], car_im[:, cols]
            qr, qi = q_re[:, cols], q_im[:, cols]
            lam_r = f_r[0:lc, cols] + (qr * cr + qi * ci)
            lam_i = f_i[0:lc, cols] + (qr * ci - qi * cr)
            car_re[:, cols] = lam_r[0:1, :]
            car_im[:, cols] = lam_i[0:1, :]
            pvr, pvi = sp_re[tail - 1:tail - 1 + lc, cols], sp_im[tail - 1:tail - 1 + lc, cols]
            dar_ref[:, cols] += _colsum(lam_r * pvr + lam_i * pvi)
            dai_ref[:, cols] += _colsum(lam_i * pvr - lam_r * pvi)
            lrb, lib = lam_r.astype(BF16), lam_i.astype(BF16)
            du_ref[:, ycols] = (jnp.dot(lrb, btr_ref[q], preferred_element_type=F32)
                                + jnp.dot(lib, bti_ref[q], preferred_element_type=F32)
                                + d_ref[:, ycols] * dy[:, ycols]).astype(BF16)
            uq = ubf[:, ycols]
            dbr_ref[q] += lax.dot_general(uq, lrb, tn_dims, preferred_element_type=F32)
            dbi_ref[q] += lax.dot_general(uq, lib, tn_dims, preferred_element_type=F32)
            dq = dyb[:, ycols]
            dcr_ref[q] += lax.dot_general(sre_ref[:, cols].astype(BF16), dq, tn_dims, preferred_element_type=F32)
            dci_ref[q] -= lax.dot_general(sim_ref[:, cols].astype(BF16), dq, tn_dims, preferred_element_type=F32)

    cblk = 1
    rev = lambda i: (nb - 1 - i, 0)
    revu = lambda i: (nb - 1 - i, cblk)
    tailmap = lambda i: (jnp.maximum((nb - 1 - i) * (lc // tail) - 1, 0), 0)
    row = lambda i: (0, 0)
    blk3 = lambda i: (0, 0, 0)
    return _pcall(
        body, name=name, grid=(nb,),
        in_specs=[pl.BlockSpec((lc, sw), rev), pl.BlockSpec((lc, sw), rev), pl.BlockSpec((lc, sw), revu),
                  pl.BlockSpec((lc, gn), rev), pl.BlockSpec((lc, gn), rev),
                  pl.BlockSpec((tail, gn), tailmap), pl.BlockSpec((tail, gn), tailmap),
                  pl.BlockSpec((1, gn), row), pl.BlockSpec((1, gn), row),
                  pl.BlockSpec(bbt_re.shape, blk3), pl.BlockSpec(bbt_im.shape, blk3),
                  pl.BlockSpec(cct_re.shape, blk3), pl.BlockSpec(cct_im.shape, blk3), pl.BlockSpec((1, sw), row)],
        out_specs=[pl.BlockSpec((lc, sw), rev), pl.BlockSpec((1, gn), row), pl.BlockSpec((1, gn), row),
                   pl.BlockSpec((SSM_BLOCKS, ub, sb), blk3), pl.BlockSpec((SSM_BLOCKS, ub, sb), blk3),
                   pl.BlockSpec((SSM_BLOCKS, sb, ub), blk3), pl.BlockSpec((SSM_BLOCKS, sb, ub), blk3),
                   pl.BlockSpec((1, sw), row)],
        out_shape=[jax.ShapeDtypeStruct((t, sw), BF16), jax.ShapeDtypeStruct((1, gn), F32), jax.ShapeDtypeStruct((1, gn), F32),
                   jax.ShapeDtypeStruct((SSM_BLOCKS, ub, sb), F32), jax.ShapeDtypeStruct((SSM_BLOCKS, ub, sb), F32),
                   jax.ShapeDtypeStruct((SSM_BLOCKS, sb, ub), F32), jax.ShapeDtypeStruct((SSM_BLOCKS, sb, ub), F32),
                   jax.ShapeDtypeStruct((1, sw), F32)],
        scratch_shapes=[pltpu.VMEM((lc, gn), F32), pltpu.VMEM((lc, gn), F32),
                        pltpu.VMEM((lc + pad, gn), F32), pltpu.VMEM((lc + pad, gn), F32),
                        pltpu.VMEM((lc + pad, gn), F32), pltpu.VMEM((lc + pad, gn), F32),
                        pltpu.VMEM((tail + lc, gn), F32), pltpu.VMEM((tail + lc, gn), F32),
                        pltpu.VMEM((1, gn), F32), pltpu.VMEM((1, gn), F32)],
        compiler_params=_params("arbitrary"))(dge, y, uin, s_re, s_im, s_re, s_im, lrdt, ang, bbt_re, bbt_im, cct_re, cct_im, d_skip)


def _blockdiag_b(bb, sw):
    gpb = (sw // SSM_GROUP) // SSM_BLOCKS
    b4 = bb.reshape(SSM_BLOCKS, gpb, SSM_STATE, SSM_GROUP)
    eye = jnp.eye(gpb, dtype=bb.dtype)
    out = jnp.einsum('qgnh,gk->qghkn', b4, eye)
    return out.reshape(SSM_BLOCKS, gpb * SSM_GROUP, gpb * SSM_STATE)


def _blockdiag_c(cc, sw):
    gpb = (sw // SSM_GROUP) // SSM_BLOCKS
    c4 = cc.reshape(SSM_BLOCKS, gpb, SSM_GROUP, SSM_STATE)
    eye = jnp.eye(gpb, dtype=cc.dtype)
    out = jnp.einsum('qghn,gk->qgnkh', c4, eye)
    return out.reshape(SSM_BLOCKS, gpb * SSM_STATE, gpb * SSM_GROUP)


def _diag_of_b(dbb, sw):
    gpb = (sw // SSM_GROUP) // SSM_BLOCKS
    d5 = dbb.reshape(SSM_BLOCKS, gpb, SSM_GROUP, gpb, SSM_STATE)
    return jnp.einsum('qghgn->qgnh', d5).reshape(SSM_BLOCKS * gpb * SSM_STATE, SSM_GROUP)


def _diag_of_c(dcc, sw):
    gpb = (sw // SSM_GROUP) // SSM_BLOCKS
    d5 = dcc.reshape(SSM_BLOCKS, gpb, SSM_STATE, gpb, SSM_GROUP)
    return jnp.einsum('qgngh->qghn', d5).reshape(SSM_BLOCKS * gpb, SSM_GROUP, SSM_STATE)


def _ada_fwd(name, c_all, w_sh, b_sh):
    nb, d = c_all.shape
    ncol = w_sh.shape[1]
    tn = _pick(ncol, 768)

    def body(c_ref, w_ref, b_ref, o_ref):
        cv = c_ref[...]
        sil = cv * _sigmoid(cv)
        o_ref[...] = jnp.dot(sil, w_ref[...], preferred_element_type=F32, precision=lax.Precision.HIGHEST) + b_ref[...]

    return _pcall(body, name=name, grid=(ncol // tn,),
                  in_specs=[pl.BlockSpec((nb, d), lambda j: (0, 0)), pl.BlockSpec((d, tn), lambda j: (0, j)),
                            pl.BlockSpec((1, tn), lambda j: (0, j))],
                  out_specs=pl.BlockSpec((nb, tn), lambda j: (0, j)),
                  out_shape=jax.ShapeDtypeStruct((nb, ncol), F32), compiler_params=_params("parallel"))(c_all, w_sh, b_sh)


def _ada_bwd(name, c_all, dmod_sh):
    nb, d = c_all.shape
    ncol = dmod_sh.shape[1]
    tn = _pick(ncol, 768)

    def body(c_ref, g_ref, o_ref):
        cv = c_ref[...]
        sil = cv * _sigmoid(cv)
        o_ref[...] = lax.dot_general(sil, g_ref[...], (((0,), (0,)), ((), ())), preferred_element_type=F32,
                                     precision=lax.Precision.HIGHEST)

    return _pcall(body, name=name, grid=(ncol // tn,),
                  in_specs=[pl.BlockSpec((nb, d), lambda j: (0, 0)), pl.BlockSpec((nb, tn), lambda j: (0, j))],
                  out_specs=pl.BlockSpec((d, tn), lambda j: (0, j)),
                  out_shape=jax.ShapeDtypeStruct((d, ncol), F32), compiler_params=_params("parallel"))(c_all, dmod_sh)


def _place():
    return lax.axis_index("x"), lax.axis_index("y"), lax.axis_index("c")


def _allgather_small(name, blk, deps=()):
    m_per, n = blk.shape

    def body(x_ref, *rest):
        out_ref, send_sems, recv_sems, local_sem = rest[len(deps):]
        x, y, c = _place()
        me, sibling = (x, y, c), (x, y, 1 - c)
        chips = [(1 - x, y), (x, 1 - y), (1 - x, 1 - y)]

        def rows(px, py, pc):
            return out_ref.at[pl.ds((4 * px + 2 * py + pc) * m_per, m_per), :]

        def copy(k, block, to, src=None):
            return pltpu.make_async_remote_copy(
                src_ref=rows(*block) if src is None else src, dst_ref=rows(*block),
                send_sem=send_sems.at[k], recv_sem=recv_sems.at[k], device_id=to, device_id_type=MESH)

        mine = pltpu.make_async_copy(x_ref, rows(*me), local_sem)
        mine.start()
        first = [copy(0, me, sibling, src=x_ref)]
        first += [copy(1 + j, me, (*chip, c), src=x_ref) for j, chip in enumerate(chips)]
        for cp in first:
            cp.start()
        passed = [copy(4 + j, (*chip, c), sibling) for j, chip in enumerate(chips)]
        for j, chip in enumerate(chips):
            copy(1 + j, (*chip, c), me).wait_recv()
            passed[j].start()
        copy(0, sibling, me).wait_recv()
        for j, chip in enumerate(chips):
            copy(4 + j, (*chip, 1 - c), me).wait_recv()
        for cp in first + passed:
            cp.wait_send()
        mine.wait()

    return _pcall(body, name=name, out_shape=jax.ShapeDtypeStruct((N_DEV * m_per, n), blk.dtype),
                  in_specs=[pl.BlockSpec(memory_space=pltpu.VMEM)] + [ANY_SPEC] * len(deps),
                  out_specs=pl.BlockSpec(memory_space=pltpu.VMEM),
                  scratch_shapes=[pltpu.SemaphoreType.DMA((7,)), pltpu.SemaphoreType.DMA((7,)), pltpu.SemaphoreType.DMA],
                  compiler_params=pltpu.CompilerParams(vmem_limit_bytes=VMEM_LIMIT))(blk, *deps)


def _other_chips(x, y):
    return [(1 - x, y), (x, 1 - y), (1 - x, 1 - y)]


def _place_shards(shards, s_me):
    return [lax.dynamic_update_slice(lax.empty((N_CHIPS,) + s.shape, s.dtype), s[None], (s_me, 0, 0)) for s in shards]


def _ag_copy(src, land, send, recv, wi, j, chip, x, y, c):
    hr = src.shape[0] // 2
    half = pl.ds(pl.multiple_of(c * hr, 16), hr)
    return pltpu.make_async_remote_copy(
        src_ref=src.at[half, :], dst_ref=land.at[2 * x + y, half, :], send_sem=send.at[3 * wi + j], recv_sem=recv.at[3 * wi + j],
        device_id=(chip[0], chip[1], c), device_id_type=MESH)


def _ag_start(name, shards, lands, groups, deps=()):
    nw, ng, nd = len(shards), len(groups), len(deps)

    def body(*refs):
        src, land = refs[:nw], refs[nw:2 * nw]
        sems = refs[2 * nw + nd:2 * nw + nd + 2 * ng]
        token = refs[-1]
        x, y, c = _place()
        for gi, grp in enumerate(groups):
            for wi, w in enumerate(grp):
                for j, chip in enumerate(_other_chips(x, y)):
                    _ag_copy(src[w], land[w], sems[2 * gi], sems[2 * gi + 1], wi, j, chip, x, y, c).start()
        token[...] = jnp.zeros_like(token)

    sem_shapes = []
    for grp in groups:
        sem_shapes += [pltpu.SemaphoreType.DMA((3 * len(grp),))] * 2
    out_shape = sem_shapes + [pltpu.HBM(s.shape, s.dtype) for s in shards] + [pltpu.HBM(l.shape, l.dtype) for l in lands]
    out_shape += [jax.ShapeDtypeStruct((8, 128), F32)]
    res = _pcall(body, name=name, out_shape=out_shape, in_specs=[HBM_SPEC] * (2 * nw) + [ANY_SPEC] * nd,
                 out_specs=[SEM_SPEC] * (2 * ng) + [HBM_SPEC] * (2 * nw) + [pl.BlockSpec(memory_space=pltpu.VMEM)],
                 input_output_aliases={i: 2 * ng + i for i in range(2 * nw)},
                 compiler_params=pltpu.CompilerParams(has_side_effects=EFFECT))(
                     *[_hbm(s) for s in shards], *[_hbm(l) for l in lands], *deps)
    sems = [(res[2 * gi], res[2 * gi + 1]) for gi in range(ng)]
    return sems, list(res[2 * ng:2 * ng + nw]), list(res[2 * ng + nw:2 * ng + 2 * nw]), res[-1]


def _ag_wait(name, shards, lands, send, recv, after):
    n = len(shards)

    def body(*refs):
        src, land = refs[:n], refs[n:2 * n]
        send_sem, recv_sem = refs[2 * n], refs[2 * n + 1]
        x, y, c = _place()
        for wi in range(n):
            for j, chip in enumerate(_other_chips(x, y)):
                _ag_copy(src[wi], land[wi], send_sem, recv_sem, wi, j, chip, x, y, c).wait_send()
                _ag_copy(src[wi], land[wi], send_sem, recv_sem, wi, j, chip, chip[0], chip[1], c).wait_recv()

    res = _pcall(body, name=name, out_shape=[pltpu.HBM(a.shape, a.dtype) for a in list(shards) + list(lands)],
                 in_specs=[HBM_SPEC] * (2 * n) + [SEM_SPEC, SEM_SPEC, ANY_SPEC], out_specs=[HBM_SPEC] * (2 * n),
                 input_output_aliases={i: i for i in range(2 * n)},
                 compiler_params=pltpu.CompilerParams(has_side_effects=EFFECT))(*shards, *lands, send, recv, after)
    return list(res[n:])


def _ag_forward(name, lands):
    n = len(lands)

    def body(*refs):
        out = refs[n:2 * n]
        send, recv = refs[2 * n], refs[2 * n + 1]
        x, y, c = _place()
        sib = (x, y, 1 - c)
        cps = []
        for wi in range(n):
            hr = out[wi].shape[1] // 2
            for j, (cx, cy) in enumerate(_other_chips(x, y)):
                got = out[wi].at[2 * cx + cy, pl.ds(pl.multiple_of(c * hr, 16), hr), :]
                cp = pltpu.make_async_remote_copy(src_ref=got, dst_ref=got, send_sem=send.at[3 * wi + j], recv_sem=recv.at[3 * wi + j],
                                                  device_id=sib, device_id_type=MESH)
                cp.start()
                cps.append(cp)
        for wi in range(n):
            hr = out[wi].shape[1] // 2
            for j, (cx, cy) in enumerate(_other_chips(x, y)):
                got = out[wi].at[2 * cx + cy, pl.ds(pl.multiple_of((1 - c) * hr, 16), hr), :]
                pltpu.make_async_remote_copy(src_ref=got, dst_ref=got, send_sem=send.at[3 * wi + j], recv_sem=recv.at[3 * wi + j],
                                             device_id=sib, device_id_type=MESH).wait_recv()
        for cp in cps:
            cp.wait_send()

    res = _pcall(body, name=name, out_shape=[jax.ShapeDtypeStruct(l.shape, l.dtype) for l in lands],
                 in_specs=[ANY_SPEC] * n, out_specs=[ANY_SPEC] * n, input_output_aliases={i: i for i in range(n)},
                 scratch_shapes=[pltpu.SemaphoreType.DMA((3 * n,)), pltpu.SemaphoreType.DMA((3 * n,))])(*lands)
    return list(res)


def _peers(x, y, c):
    offs = [(dx, dy, dc) for dx in (0, 1) for dy in (0, 1) for dc in (0, 1)][1:]
    return [(1 - x if dx else x, 1 - y if dy else y, 1 - c if dc else c) for dx, dy, dc in offs]


def _rs_copy(g_ref, land_ref, send, recv, wi, k, to, sender):
    hr = g_ref.shape[1] // 2
    return pltpu.make_async_remote_copy(
        src_ref=g_ref.at[2 * to[0] + to[1], pl.ds(pl.multiple_of(to[2] * hr, 16), hr), :], dst_ref=land_ref.at[sender],
        send_sem=send.at[7 * wi + k], recv_sem=recv.at[7 * wi + k], device_id=to, device_id_type=MESH)


def _rs_start(name, gs):
    n = len(gs)
    lands = [lax.empty((N_DEV, g.shape[1] // 2, g.shape[2]), BF16) for g in gs]

    def body(*refs):
        g, land = refs[:n], refs[n:2 * n]
        send, recv = refs[2 * n], refs[2 * n + 1]
        token = refs[-1]
        x, y, c = _place()
        me = 4 * x + 2 * y + c
        for wi in range(n):
            for k, to in enumerate(_peers(x, y, c)):
                _rs_copy(g[wi], land[wi], send, recv, wi, k, to, me).start()
        token[...] = jnp.zeros_like(token)

    out_shape = [pltpu.SemaphoreType.DMA((7 * n,))] * 2 + [pltpu.HBM(a.shape, a.dtype) for a in list(gs) + lands]
    out_shape += [jax.ShapeDtypeStruct((8, 128), F32)]
    res = _pcall(body, name=name, out_shape=out_shape, in_specs=[HBM_SPEC] * (2 * n),
                 out_specs=[SEM_SPEC] * 2 + [HBM_SPEC] * (2 * n) + [pl.BlockSpec(memory_space=pltpu.VMEM)],
                 input_output_aliases={i: 2 + i for i in range(2 * n)},
                 compiler_params=pltpu.CompilerParams(has_side_effects=EFFECT))(
                     *[_hbm(a) for a in gs], *[_hbm(a) for a in lands])
    return res[0], res[1], list(res[2:2 + n]), list(res[2 + n:2 + 2 * n]), res[-1]


def _rs_wait(name, gs, lands, send, recv, after):
    n = len(gs)

    def body(*refs):
        g, land = refs[:n], refs[n:2 * n]
        send_sem, recv_sem = refs[2 * n], refs[2 * n + 1]
        x, y, c = _place()
        me = 4 * x + 2 * y + c
        for wi in range(n):
            for k, to in enumerate(_peers(x, y, c)):
                _rs_copy(g[wi], land[wi], send_sem, recv_sem, wi, k, to, me).wait_send()
                _rs_copy(g[wi], land[wi], send_sem, recv_sem, wi, k, (x, y, c), 4 * to[0] + 2 * to[1] + to[2]).wait_recv()

    res = _pcall(body, name=name, out_shape=[pltpu.HBM(a.shape, a.dtype) for a in list(gs) + list(lands)],
                 in_specs=[HBM_SPEC] * (2 * n) + [SEM_SPEC, SEM_SPEC, ANY_SPEC], out_specs=[HBM_SPEC] * (2 * n),
                 input_output_aliases={i: i for i in range(2 * n)},
                 compiler_params=pltpu.CompilerParams(has_side_effects=EFFECT))(*gs, *lands, send, recv, after)
    return list(res[:n]), list(res[n:])


def _bc_copy(blk_ref, land_ref, send, recv, k, to, slot):
    return pltpu.make_async_remote_copy(src_ref=blk_ref, dst_ref=land_ref.at[slot], send_sem=send.at[k], recv_sem=recv.at[k],
                                        device_id=to, device_id_type=MESH)


def _bcast_start(name, blk):
    land = lax.empty((N_DEV,) + blk.shape, blk.dtype)

    def body(blk_ref, land_ref, send, recv, blk_thru, land_thru, token):
        x, y, c = _place()
        for k, to in enumerate(_peers(x, y, c)):
            _bc_copy(blk_ref, land_ref, send, recv, k, to, 4 * x + 2 * y + c).start()
        token[...] = jnp.zeros_like(token)

    return _pcall(body, name=name,
                  out_shape=[pltpu.SemaphoreType.DMA((7,)), pltpu.SemaphoreType.DMA((7,)), pltpu.HBM(blk.shape, blk.dtype),
                             pltpu.HBM(land.shape, land.dtype), jax.ShapeDtypeStruct((8, 128), F32)],
                  in_specs=[HBM_SPEC, HBM_SPEC], out_specs=[SEM_SPEC, SEM_SPEC, HBM_SPEC, HBM_SPEC, pl.BlockSpec(memory_space=pltpu.VMEM)],
                  input_output_aliases={0: 2, 1: 3}, compiler_params=pltpu.CompilerParams(has_side_effects=EFFECT))(_hbm(blk), _hbm(land))


def _bcast_wait(name, blk, land, send, recv, after):
    def body(blk_ref, land_ref, send_sem, recv_sem, after_ref, blk_thru, land_thru):
        x, y, c = _place()
        for k, to in enumerate(_peers(x, y, c)):
            _bc_copy(blk_ref, land_ref, send_sem, recv_sem, k, to, 4 * x + 2 * y + c).wait_send()
            _bc_copy(blk_ref, land_ref, send_sem, recv_sem, k, to, 4 * to[0] + 2 * to[1] + to[2]).wait_recv()

    return _pcall(body, name=name, out_shape=[pltpu.HBM(blk.shape, blk.dtype), pltpu.HBM(land.shape, land.dtype)],
                  in_specs=[HBM_SPEC, HBM_SPEC, SEM_SPEC, SEM_SPEC, ANY_SPEC], out_specs=[HBM_SPEC, HBM_SPEC],
                  input_output_aliases={0: 0, 1: 1}, compiler_params=pltpu.CompilerParams(has_side_effects=EFFECT))(
                      blk, land, send, recv, after)[1]


def _rs_sum(name, g, land):
    _, rs, cs = g.shape
    hr = rs // 2
    ch = _pick(hr, 64, 16)

    def body(g_ref, land_ref, out_ref, recv, local_sems, sib_sems):
        x, y, c = _place()
        me = 4 * x + 2 * y + c
        cps = [pltpu.make_async_copy(g_ref.at[2 * x + y, pl.ds(pl.multiple_of(c * hr, 16), hr), :], recv.at[me], local_sems.at[7])]
        for k, (tx, ty, tc) in enumerate(_peers(x, y, c)):
            slot = 4 * tx + 2 * ty + tc
            cps.append(pltpu.make_async_copy(land_ref.at[slot], recv.at[slot], local_sems.at[k]))
        for cp in cps:
            cp.start()
        for cp in cps:
            cp.wait()
        base = pl.multiple_of(c * hr, 16)
        for r0 in range(0, hr, ch):
            acc = recv[0, r0:r0 + ch, :].astype(F32)
            for k in range(1, N_DEV):
                acc = acc + recv[k, r0:r0 + ch, :].astype(F32)
            out_ref[pl.ds(base + r0, ch), :] = acc
        half = out_ref.at[pl.ds(base, hr), :]
        sib = pltpu.make_async_remote_copy(src_ref=half, dst_ref=half, send_sem=sib_sems.at[0], recv_sem=sib_sems.at[1],
                                           device_id=(x, y, 1 - c), device_id_type=MESH)
        sib.start()
        other = out_ref.at[pl.ds(pl.multiple_of((1 - c) * hr, 16), hr), :]
        pltpu.make_async_remote_copy(src_ref=other, dst_ref=other, send_sem=sib_sems.at[0], recv_sem=sib_sems.at[1],
                                     device_id=(x, y, 1 - c), device_id_type=MESH).wait_recv()
        sib.wait_send()

    return _pcall(
        body, name=name, out_shape=jax.ShapeDtypeStruct((rs, cs), F32),
        in_specs=[ANY_SPEC, ANY_SPEC], out_specs=pl.BlockSpec(memory_space=pltpu.VMEM),
        scratch_shapes=[pltpu.VMEM((N_DEV, hr, cs), BF16), pltpu.SemaphoreType.DMA((8,)), pltpu.SemaphoreType.DMA((2,))],
        compiler_params=pltpu.CompilerParams(vmem_limit_bytes=VMEM_LIMIT))(g, land)


def _ffn_fwd(tag, x, g, sh, sc, gt, get_w_in, get_w_out, deps=()):
    t, d = x.shape
    h = _norm_fwd(tag + "_norm", x, g, sc, sh, deps=deps)
    ab = _mm(tag + "_in", h, get_w_in(h), b_stacked=True, tm=1024, tn=1408, j_outer=True)
    act = _swiglu_fwd(tag + "_act", ab)
    w_out = get_w_out(act)
    blk = (512, d)

    def epi(acc, res, scale):
        return res + (0.5 * scale) * acc, acc

    x_new, y = _mm(tag + "_out", act, w_out, tm=512, tn=d, epi=epi,
                   epi_ins=[(x, blk, lambda i, j: (i, j)), (gt, (1, d), lambda i, j: (0, j))],
                   epi_outs=[((t, d), F32, blk, lambda i, j: (i, j)), ((t, d), BF16, blk, lambda i, j: (i, j))])
    return x_new, (x, h, ab, act, y)


def _ffn_bwd(tag, dx_new, saved, g, sc, gt, w_in, w_out, start_rs, deps=()):
    x, h, ab, act, y = saved
    d = x.shape[1]
    f = act.shape[1]
    dy, dgt = _resgate_bwd(tag + "_dgate", dx_new, y, gt, 0.5, deps=deps)
    dact = _mm(tag + "_dact", dy, w_out, tb=True, tn=2816)
    dw_out = _mm(tag + "_dwout", act, dy, ta=True, tm=1408, tn=d, tk=2048)
    tok = start_rs("out", dw_out.reshape(N_CHIPS, f // N_CHIPS, d))
    dab = _swiglu_bwd(tag + "_dswiglu", dact, ab, deps=(tok,))
    dw_in = _mm(tag + "_dwin", h, dab, ta=True, out_stacked=True, tm=d, tn=1408, tk=2048)
    tok = start_rs("in", dw_in)
    dh = _mm(tag + "_dh", dab, w_in, tb=True, b_stacked=True, tn=d, tk=5632, deps=(tok,))
    dx, dsh, dsc, dg = _norm_bwd(tag + "_dnorm", dh, x, dx_new, g, sc)
    return dx, dsh, dsc, dgt, dg


def _row(v):
    return v.reshape(1, -1)


def _pack(parts):
    cols = []
    for p in parts:
        flat = p.reshape(-1).astype(F32)
        padn = (-flat.shape[0]) % 128
        cols.append(jnp.pad(flat, (0, padn)) if padn else flat)
    flat = jnp.concatenate(cols)
    padn = (-flat.shape[0]) % 1024
    if padn:
        flat = jnp.pad(flat, (0, padn))
    return flat.reshape(-1, 128)


def _unpack(packed, shapes):
    flat = packed.reshape(-1)
    out, off = [], 0
    for s in shapes:
        n = math.prod(s)
        out.append(flat[off:off + n].reshape(s))
        off += n + ((-n) % 128)
    return out


SMALL = ['b_ada', 'g_ffn1', 'g_mix', 'pool_w', 'pool_b', 'pool_scale', 'ssm_lam_re_log', 'ssm_lam_im', 'ssm_log_dt',
         'ssm_b_re', 'ssm_b_im', 'ssm_c_re', 'ssm_c_im', 'ssm_d', 'b_glu', 'g_ffn2', 'g_final']
BIG = ['w_ffn1_in', 'w_ffn1_out', 'w_in', 'w_pool_up', 'w_glu', 'w_ssm_up', 'w_out', 'w_ffn2_in', 'w_ffn2_out']
AG_GROUPS = [[0], [1], [2, 3, 4, 5, 6], [7, 8]]
SMALL_LATE = ['b_ada', 'g_ffn1']
WEIGHTS = ['w_ada', 'b_ada', 'g_ffn1', 'w_ffn1_in', 'w_ffn1_out', 'g_mix', 'w_in', 'pool_w', 'pool_b', 'pool_scale', 'w_pool_up',
           'ssm_lam_re_log', 'ssm_lam_im', 'ssm_log_dt', 'ssm_b_re', 'ssm_b_im', 'ssm_c_re', 'ssm_c_im', 'ssm_d', 'w_glu', 'b_glu',
           'w_ssm_up', 'w_out', 'g_ffn2', 'w_ffn2_in', 'w_ffn2_out', 'g_final']


def kernel(x, c, w_ada, b_ada, g_ffn1, w_ffn1_in, w_ffn1_out, g_mix, w_in, pool_w, pool_b, pool_scale, w_pool_up, ssm_lam_re_log, ssm_lam_im, ssm_log_dt, ssm_b_re, ssm_b_im, ssm_c_re, ssm_c_im, ssm_d, w_glu, b_glu, w_ssm_up, w_out, g_ffn2, w_ffn2_in, w_ffn2_out, g_final, loss_target, m_w_ada, m_b_ada, m_g_ffn1, m_w_ffn1_in, m_w_ffn1_out, m_g_mix, m_w_in, m_pool_w, m_pool_b, m_pool_scale, m_w_pool_up, m_ssm_lam_re_log, m_ssm_lam_im, m_ssm_log_dt, m_ssm_b_re, m_ssm_b_im, m_ssm_c_re, m_ssm_c_im, m_ssm_d, m_w_glu, m_b_glu, m_w_ssm_up, m_w_out, m_g_ffn2, m_w_ffn2_in, m_w_ffn2_out, m_g_final, v_w_ada, v_b_ada, v_g_ffn1, v_w_ffn1_in, v_w_ffn1_out, v_g_mix, v_w_in, v_pool_w, v_pool_b, v_pool_scale, v_w_pool_up, v_ssm_lam_re_log, v_ssm_lam_im, v_ssm_log_dt, v_ssm_b_re, v_ssm_b_im, v_ssm_c_re, v_ssm_c_im, v_ssm_d, v_w_glu, v_b_glu, v_w_ssm_up, v_w_out, v_g_ffn2, v_w_ffn2_in, v_w_ffn2_out, v_g_final):
    args = dict(locals())
    wt = {n: args[n] for n in WEIGHTS}
    mom = {n: args["m_" + n] for n in WEIGHTS}
    var = {n: args["v_" + n] for n in WEIGHTS}

    t, d = x.shape[1], x.shape[2]
    pw = pool_b.shape[1]
    sw = ssm_d.shape[1]
    ngrp = sw // SSM_GROUP
    gn = ngrp * SSM_STATE
    xi, yi, ci = _place()
    b_me = 4 * xi + 2 * yi + ci
    s_me = 2 * xi + yi
    x2d = x[0]
    tgt = loss_target[0]

    c_all = _allgather_small("ag_c", c.reshape(8, d // 8)).reshape(N_DEV, d)
    ncol = w_ada.shape[2]
    b_sh = lax.dynamic_slice(b_ada, (0, s_me * ncol), (1, ncol))
    mod_sh = _ada_fwd("ada_fwd", c_all, w_ada[0], b_sh)
    mod_all = _allgather_small("ag_mod", mod_sh)

    shards = [wt[n][0].astype(BF16) for n in BIG]
    ag_sems, shards_t, lands_t, tok = _ag_start("ag_start", shards, _place_shards(shards, s_me), AG_GROUPS, deps=(mod_all,))
    full = {}

    def weights(gi, after):
        grp = AG_GROUPS[gi]
        if BIG[grp[0]] not in full:
            ls = _ag_wait("ag_wait%d" % gi, [shards_t[w] for w in grp], [lands_t[w] for w in grp], *ag_sems[gi], after)
            for w, l in zip(grp, _ag_forward("ag_fwd%d" % gi, ls)):
                full[BIG[w]] = l
        return full

    mod_me = jnp.concatenate([lax.dynamic_slice(mod_all, (16 * s + b_me, 0), (1, ncol)) for s in range(N_CHIPS)], axis=1)
    mod = mod_me.reshape(9, d)
    sh1, sc1, gt1, sh2, sc2, gt2, sh3, sc3, gt3 = [mod[k:k + 1] for k in range(9)]

    f = w_ffn1_out.shape[1] * N_CHIPS

    x1, sav1 = _ffn_fwd("ffn1", x2d, g_ffn1, sh1, sc1, gt1, lambda after: weights(0, after)['w_ffn1_in'],
                        lambda after: weights(1, after)['w_ffn1_out'].reshape(f, d), deps=(tok,))

    h2 =_norm_fwd("mix_norm", x1, g_mix, sc2, sh2)
    weights(2, h2)
    wo = full['w_out'].reshape(d, d)
    uin = _mm("mix_in", h2, full['w_in'], b_stacked=True, tm=1024, tn=768, out_dtype=F32, j_outer=True)
    p_pool, z_pool = _pool_fwd("pool_fwd", uin, pool_w[0], pool_b, pool_scale, pw)
    y_pool = _mm("pool_up", p_pool, full['w_pool_up'], b_stacked=True, tm=1024, tn=d)

    col = lambda a: a.reshape(gn, 1)
    lrl_c, li_c = col(ssm_lam_re_log), col(ssm_lam_im)
    ldt_c = col(jnp.broadcast_to(ssm_log_dt.reshape(ngrp, 1), (ngrp, SSM_STATE)))
    b_re2, b_im2 = ssm_b_re.reshape(gn, SSM_GROUP), ssm_b_im.reshape(gn, SSM_GROUP)
    lrdt_c, ang_c, bb_re, bb_im = _ssm_prep("ssm_prep", lrl_c, li_c, ldt_c, b_re2, b_im2)
    lrdt, ang = lrdt_c.reshape(1, gn), ang_c.reshape(1, gn)
    bbd_re, bbd_im = _blockdiag_b(bb_re, sw), _blockdiag_b(bb_im, sw)
    ccd_re, ccd_im = _blockdiag_c(ssm_c_re[0], sw), _blockdiag_c(ssm_c_im[0], sw)
    y_s, ge, s_re, s_im = _ssm_fwd("ssm_fwd", uin, lrdt, ang, bbd_re.astype(BF16), bbd_im.astype(BF16),
                                   ccd_re.astype(BF16), ccd_im.astype(BF16), ssm_d, sw)
    gv = _mm("glu_in", ge, full['w_glu'], b_stacked=True, tm=1024, tn=d)
    sg = _glu_fwd("glu_act", gv, b_glu)
    y_ssm = _mm("ssm_up", sg, full['w_ssm_up'], b_stacked=True, tm=1024, tn=d)
    merged = _gates_fwd("gates", y_pool, y_ssm, uin, d, pw)
    blk = (512, d)

    def epi2(acc, res, scale):
        return res + scale * acc, acc

    x2, y2 = _mm("mix_out", merged, wo, tm=512, tn=d, epi=epi2,
                 epi_ins=[(x1, blk, lambda i, j: (i, j)), (gt2, (1, d), lambda i, j: (0, j))],
                 epi_outs=[((t, d), F32, blk, lambda i, j: (i, j)), ((t, d), BF16, blk, lambda i, j: (i, j))])

    x3, sav3 = _ffn_fwd("ffn2", x2, g_ffn2, sh3, sc3, gt3, lambda after: weights(3, after)['w_ffn2_in'],
                        lambda after: weights(3, after)['w_ffn2_out'].reshape(f, d))

    dx3, dg_final, loss_v = _loss_bwd("loss", x3, tgt, _row(g_final))
    loss = lax.psum(loss_v[0, 0], ("x", "y", "c"))

    gs = {}
    rs_open = []

    def start_rs(names, gbs):
        send, recv, g_thru, land_thru, token = _rs_start("rs_start_" + names[0], gbs)
        rs_open.append((names, send, recv, g_thru, land_thru))
        return token

    dx2, dsh3, dsc3, dgt3, gs['g_ffn2'] = _ffn_bwd(
        "ffn2b", dx3, sav3, g_ffn2, sc3, gt3, full['w_ffn2_in'], full['w_ffn2_out'].reshape(f, d),
        lambda key, g: start_rs(['w_ffn2_' + key], [g]))

    dy2, dgt2 = _resgate_bwd("mix_dgate", dx2, y2, gt2, 1.0)
    dmerged = _mm("mix_dmerged", dy2, wo, tb=True, tn=d)
    g_wo = _mm("mix_dwout", merged, dy2, ta=True, tm=d, tn=d, tk=2048).reshape(N_CHIPS, d // N_CHIPS, d)
    dy_pool, dy_ssm, dgl = _gates_bwd("gates_bwd", dmerged, y_pool, y_ssm, uin, d, pw)

    dp = _mm("pool_dup", dy_pool, full['w_pool_up'], tb=True, b_stacked=True, tm=1024, tn=pw, tk=d)
    g_wpu = _mm("pool_dwup", p_pool, dy_pool, ta=True, out_stacked=True, tm=pw, tn=d, tk=4096)
    du_pool, gs['pool_w'], gs['pool_b'], gs['pool_scale'] = _pool_bwd("pool_bwd", dp, z_pool, pool_w[0], pool_b, pool_scale)

    dsg = _mm("ssm_dup", dy_ssm, full['w_ssm_up'], tb=True, b_stacked=True, tm=1024, tn=sw, tk=d)
    g_wsu = _mm("ssm_dwup", sg, dy_ssm, ta=True, out_stacked=True, tm=sw, tn=d, tk=4096)
    tok = start_rs(['w_out', 'w_pool_up', 'w_ssm_up'], [g_wo, g_wpu, g_wsu])
    dgv, gs['b_glu'] = _glu_bwd("glu_bwd", dsg, gv, b_glu, deps=(tok,))
    dge = _mm("glu_dge", dgv, full['w_glu'], tb=True, b_stacked=True, tm=1024, tn=sw, tk=d, out_dtype=F32)
    g_wglu = _mm("glu_dw", ge, dgv, ta=True, out_stacked=True, tm=sw, tn=d, tk=4096)
    tr = lambda a: jnp.swapaxes(a, 1, 2).astype(BF16)
    (du_ssm, g_abre, g_abim, g_bbd_re, g_bbd_im, g_ccd_re, g_ccd_im, gs['ssm_d']) = _ssm_bwd(
        "ssm_bwd", dge, y_s, uin, s_re, s_im, lrdt, ang, tr(bbd_re), tr(bbd_im), tr(ccd_re), tr(ccd_im), ssm_d, sw)
    gs['ssm_c_re'], gs['ssm_c_im'] = _diag_of_c(g_ccd_re, sw), _diag_of_c(g_ccd_im, sw)
    d_lrl, d_li, d_ldt, d_bre, d_bim = _ssm_param_bwd(
        "ssm_param_bwd", lrl_c, li_c, ldt_c, b_re2, b_im2, g_abre.reshape(gn, 1), g_abim.reshape(gn, 1),
        _diag_of_b(g_bbd_re, sw), _diag_of_b(g_bbd_im, sw))
    gs['ssm_lam_re_log'], gs['ssm_lam_im'] = d_lrl, d_li
    gs['ssm_log_dt'] = jnp.sum(d_ldt.reshape(ngrp, SSM_STATE), axis=1)
    gs['ssm_b_re'], gs['ssm_b_im'] = d_bre, d_bim

    duin = jnp.concatenate([du_pool, du_ssm, dgl], axis=1)
    g_win = _mm("mix_dwin", h2, duin, ta=True, out_stacked=True, tm=d, tn=768, tk=4096)
    tok = start_rs(['w_glu', 'w_in'], [g_wglu, g_win])
    dh2 = _mm("mix_dh", duin, full['w_in'], tb=True, b_stacked=True, tn=d, tk=3072, deps=(tok,))
    dx1, dsh2, dsc2, gs['g_mix'] = _norm_bwd("mix_dnorm", dh2, x1, dx2, g_mix, sc2)

    gs['g_final'] = dg_final
    early = [n for n in SMALL if n not in SMALL_LATE]
    sg_blk = _pack([gs[n] for n in early])
    sg_send, sg_recv, sg_blk, sg_land, tok = _bcast_start("sg_start", sg_blk)

    dx0, dsh1, dsc1, dgt1, gs['g_ffn1'] = _ffn_bwd(
        "ffn1b", dx1, sav1, g_ffn1, sc1, gt1, full['w_ffn1_in'], full['w_ffn1_out'].reshape(f, d),
        lambda key, g: start_rs(['w_ffn1_' + key], [g]), deps=(tok,))

    gs['b_ada'] = jnp.concatenate([dsh1, dsc1, dgt1, dsh2, dsc2, dgt2, dsh3, dsc3, dgt3], axis=1)
    late8 = _allgather_small("ag_late", _pack([gs[n] for n in SMALL_LATE]))
    dmod_all = late8.reshape(N_DEV, -1)[:, :9 * d]
    dmod_sh = lax.dynamic_slice(dmod_all, (0, s_me * ncol), (N_DEV, ncol))
    g_w_ada = _ada_bwd("ada_bwd", c_all, dmod_sh)

    grads, delta, new_m, new_v = {}, {}, {}, {}
    sg_land = _bcast_wait("sg_wait", sg_blk, sg_land, sg_send, sg_recv, g_w_ada)
    early8 = lax.dynamic_update_slice(sg_land, sg_blk[None], (b_me, 0, 0)).reshape(-1, 128)
    for tag, names, g8 in (("early", early, early8), ("late", SMALL_LATE, late8)):
        res = _adamw_small("adamw_small_" + tag, _pack([wt[n] for n in names]), g8,
                           _pack([mom[n] for n in names]), _pack([var[n] for n in names]))
        for dst, packed in zip((grads, delta, new_m, new_v), res):
            for n, val in zip(names, _unpack(packed, [wt[n].shape for n in names])):
                dst[n] = val

    dl, mn, vn = _adamw("adamw_w_ada", w_ada[0], g_w_ada, m_w_ada[0], v_w_ada[0])
    grads['w_ada'], delta['w_ada'], new_m['w_ada'], new_v['w_ada'] = g_w_ada[None], dl[None], mn[None], vn[None]
    after = dl
    for names, send, recv, g_thru, land_thru in rs_open:
        g_done, land_done = _rs_wait("rs_wait_" + names[0], g_thru, land_thru, send, recv, after)
        for n, g_own, land in zip(names, g_done, land_done):
            g_sum = _rs_sum("rs_sum_" + n, g_own, land)
            dl, mn, vn = _adamw("adamw_" + n, wt[n][0], g_sum, mom[n][0], var[n][0])
            grads[n], delta[n], new_m[n], new_v[n] = g_sum[None], dl[None], mn[None], vn[None]
            after = dl

    return (loss, dx0[None], *[grads[n] for n in WEIGHTS], *[delta[n] for n in WEIGHTS],
            *[new_m[n] for n in WEIGHTS], *[new_v[n] for n in WEIGHTS])
```

```python
import functools
import math

import jax
import jax.numpy as jnp
from jax import lax
from jax.experimental import pallas as pl
from jax.experimental.pallas import tpu as pltpu

F32 = jnp.float32
BF16 = jnp.bfloat16
MESH = pl.DeviceIdType.MESH

EPS = 1e-6
POOL_WINDOWS = (2, 4, 8, 16)
POOL_HALO = 16
SSM_GROUP = 16
SSM_STATE = 64
SSM_BLOCKS = 4
N_DEV = 8
N_CHIPS = 4
ADAM_LR = 0.001
ADAM_B1 = 0.9
ADAM_B2 = 0.999
ADAM_EPS = 1e-08
ADAM_WD = 0.01
ADAM_STEP = 10
VMEM_LIMIT = 56 * 1024 * 1024


ANY_SPEC = pl.BlockSpec(memory_space=pl.ANY)
HBM_SPEC = pl.BlockSpec(memory_space=pltpu.HBM)
SEM_SPEC = pl.BlockSpec(memory_space=pltpu.SEMAPHORE)
EFFECT = pltpu.SideEffectType.DATAFLOW_SIDE_EFFECTING


def _hbm(a):
    return pltpu.with_memory_space_constraint(a, pltpu.HBM)


def _pcall(body, **kw):
    return pl.pallas_call(body, **kw)


def _params(*sem):
    return pltpu.CompilerParams(dimension_semantics=sem, vmem_limit_bytes=VMEM_LIMIT)


def _pick(n, cap, mult=128):
    if n <= cap:
        return n
    best = None
    for d in range(mult, cap + 1, mult):
        if n % d == 0:
            best = d
    assert best is not None, (n, cap, mult)
    return best


def _sigmoid(v):
    return 1.0 / (1.0 + jnp.exp(-v))


def _rowwise(name, fn, ins, params, outs, reds, tm, deps=()):
    t = ins[0][0].shape[0]
    tm = min(tm, t)
    nb = t // tm
    ni, npar, no, nd = len(ins), len(params), len(outs), len(deps)

    def body(*refs):
        iv = [r[...] for r in refs[:ni]]
        pv = [r[...] for r in refs[ni:ni + npar]]
        o_refs = refs[ni + npar + nd:ni + npar + nd + no]
        r_refs = refs[ni + npar + nd + no:]
        ovals, rvals = fn(iv, pv)
        for o_ref, val in zip(o_refs, ovals):
            parts = val if isinstance(val, (list, tuple)) else [val]
            off = 0
            for p in parts:
                o_ref[:, off:off + p.shape[1]] = p.astype(o_ref.dtype)
                off += p.shape[1]
        if r_refs:
            @pl.when(pl.program_id(0) == 0)
            def _():
                for r in r_refs:
                    r[...] = jnp.zeros_like(r)
            for r, val in zip(r_refs, rvals):
                r[...] += val

    in_specs = [pl.BlockSpec((tm, w), functools.partial(lambda i, cb: (i, cb), cb=cb)) for (_, w, cb) in ins]
    in_specs += [pl.BlockSpec(p.shape, lambda i: (0, 0)) for p in params]
    in_specs += [ANY_SPEC] * nd
    out_shape = [jax.ShapeDtypeStruct((t, w), dt) for (w, dt) in outs]
    out_shape += [jax.ShapeDtypeStruct((1, w), F32) for w in reds]
    out_specs = [pl.BlockSpec((tm, w), lambda i: (i, 0)) for (w, _) in outs]
    out_specs += [pl.BlockSpec((1, w), lambda i: (0, 0)) for w in reds]
    res = _pcall(body, name=name, grid=(nb,), in_specs=in_specs, out_specs=out_specs, out_shape=out_shape,
                 compiler_params=_params("arbitrary"))(*[a for a, _, _ in ins], *params, *deps)
    return res


def _colsum(v):
    return jnp.sum(v, axis=0, keepdims=True)


def _mm(name, a, b, *, ta=False, tb=False, b_stacked=False, out_stacked=False, tm=512, tn=1024, tk=2816,
        out_dtype=BF16, epi=None, epi_ins=(), epi_outs=None, deps=(), j_outer=False):
    if ta:
        kdim, m = a.shape
    else:
        m, kdim = a.shape
    ns = None
    if b_stacked:
        ns = b.shape[2]
        n = b.shape[1] if tb else N_CHIPS * ns
        assert kdim == (N_CHIPS * ns if tb else b.shape[1]), (name, a.shape, b.shape)
    else:
        n = b.shape[0] if tb else b.shape[1]
        assert kdim == (b.shape[1] if tb else b.shape[0]), (name, a.shape, b.shape)
        if out_stacked:
            ns = n // N_CHIPS

    def shards(want):
        return max(g for g in (1, 2, 4) if g * ns <= max(want, ns))

    tm = _pick(m, tm, 128 if ta else 8)
    gn = gk = 1
    if (b_stacked and not tb) or out_stacked:
        gn = shards(tn)
        tn = gn * ns
    else:
        tn = _pick(n, tn)
    if b_stacked and tb:
        gk = shards(tk)
        tk = gk * ns
    else:
        tk = _pick(kdim, tk, 8 if ta else 128)
    nm, nn, nk = m // tm, n // tn, kdim // tk

    def ij(f):
        return (lambda g0, g1, k: f(g1, g0, k)) if j_outer else f

    a_spec = pl.BlockSpec((tk, tm), ij(lambda i, j, k: (k, i))) if ta else pl.BlockSpec((tm, tk), ij(lambda i, j, k: (i, k)))
    if b_stacked and not tb:
        b_spec = pl.BlockSpec((gn, tk, ns), ij(lambda i, j, k: (j, k, 0)))
    elif b_stacked and tb:
        b_spec = pl.BlockSpec((gk, tn, ns), ij(lambda i, j, k: (k, j, 0)))
    elif tb:
        b_spec = pl.BlockSpec((tn, tk), ij(lambda i, j, k: (j, k)))
    else:
        b_spec = pl.BlockSpec((tk, tn), ij(lambda i, j, k: (k, j)))
    dims = (((0 if ta else 1,), (1 if tb else 0,)), ((), ()))

    if epi_outs is None:
        if out_stacked:
            epi_outs = [((N_CHIPS, m, ns), out_dtype, (gn, tm, ns), lambda i, j: (j, i, 0))]
        else:
            epi_outs = [((m, n), out_dtype, (tm, tn), lambda i, j: (i, j))]
    ne, no, nd = len(epi_ins), len(epi_outs), len(deps)

    def body(a_ref, b_ref, *rest):
        e_refs = rest[:ne]
        o_refs = rest[ne + nd:ne + nd + no]
        scratch = rest[ne + nd + no:]
        av = a_ref[...].astype(BF16)
        if b_stacked and not tb:
            parts = [lax.dot_general(av, b_ref[s].astype(BF16), dims, preferred_element_type=F32) for s in range(gn)]
        elif b_stacked and tb:
            p = None
            for s in range(gk):
                q = lax.dot_general(av[:, s * ns:(s + 1) * ns], b_ref[s].astype(BF16), dims, preferred_element_type=F32)
                p = q if p is None else p + q
            parts = [p]
        else:
            parts = [lax.dot_general(av, b_ref[...].astype(BF16), dims, preferred_element_type=F32)]

        def finish(acc_parts):
            if epi is None and out_stacked:
                acc = acc_parts[0]
                for s in range(gn):
                    o_refs[0][s] = acc[:, s * ns:(s + 1) * ns].astype(out_dtype)
            elif epi is None:
                w = acc_parts[0].shape[1]
                for s, part in enumerate(acc_parts):
                    o_refs[0][:, s * w:(s + 1) * w] = part.astype(out_dtype)
            else:
                acc = acc_parts[0] if len(acc_parts) == 1 else jnp.concatenate(acc_parts, axis=1)
                for o_ref, val in zip(o_refs, epi(acc, *[r[...] for r in e_refs])):
                    o_ref[...] = val.astype(o_ref.dtype)

        if nk == 1:
            finish(parts)
        else:
            acc_ref = scratch[0]
            k = pl.program_id(2)
            w = parts[0].shape[1]

            @pl.when(k == 0)
            def _():
                for s, part in enumerate(parts):
                    acc_ref[:, s * w:(s + 1) * w] = part

            @pl.when(k > 0)
            def _():
                for s, part in enumerate(parts):
                    acc_ref[:, s * w:(s + 1) * w] += part

            @pl.when(k == nk - 1)
            def _():
                finish([acc_ref[...]])

    def _ij(f):
        return ij(lambda i, j, k: f(i, j))

    in_specs = [a_spec, b_spec] + [pl.BlockSpec(blk, _ij(f)) for (_, blk, f) in epi_ins] + [ANY_SPEC] * nd
    out_specs = [pl.BlockSpec(blk, _ij(f)) for (_, _, blk, f) in epi_outs]
    out_shape = [jax.ShapeDtypeStruct(s, dt) for (s, dt, _, _) in epi_outs]
    scratch = [pltpu.VMEM((tm, tn), F32)] if nk > 1 else []
    grid = (nn, nm, nk) if j_outer else (nm, nn, nk)
    res = _pcall(body, name=name, grid=grid, in_specs=in_specs, out_specs=out_specs, out_shape=out_shape,
                 scratch_shapes=scratch, compiler_params=_params("parallel", "parallel", "arbitrary"))(
                     a, b, *[x for x, _, _ in epi_ins], *deps)
    return res[0] if len(res) == 1 else res


def _norm_fwd(name, x, g, sc, sh, deps=()):
    d = x.shape[1]

    def fn(iv, pv):
        (xv,), (gv, scv, shv) = iv, pv
        r = lax.rsqrt(jnp.mean(xv * xv, axis=-1, keepdims=True) + EPS)
        return [xv * r * gv * (1.0 + scv) + shv], []

    return _rowwise(name, fn, [(x, d, 0)], [g, sc, sh], [(d, BF16)], [], 512, deps=deps)[0]


def _norm_bwd(name, dh, x, dres, g, sc, y_up, gt_up, factor_up):
    d = x.shape[1]
    up = y_up is not None

    def fn(iv, pv):
        dhv, xv, drv = iv[:3]
        gv, scv = pv[:2]
        dhv = dhv.astype(F32)
        r = lax.rsqrt(jnp.mean(xv * xv, axis=-1, keepdims=True) + EPS)
        xr = xv * r
        dn = dhv * (1.0 + scv)
        gd = gv * dn
        dx = drv + r * (gd - xr * jnp.mean(gd * xr, axis=-1, keepdims=True))
        outs, reds = [dx], [_colsum(dhv), _colsum(dhv * xr * gv), _colsum(dn * xr)]
        if up:
            fdx = factor_up * dx
            outs.append(pv[2] * fdx)
            reds.append(_colsum(fdx * iv[3].astype(F32)))
        return outs, reds

    ins = [(dh, d, 0), (x, d, 0), (dres, d, 0)] + ([(y_up, d, 0)] if up else [])
    res = _rowwise(name, fn, ins, [g, sc] + ([gt_up] if up else []), [(d, F32)] + ([(d, BF16)] if up else []),
                   [d] * (4 if up else 3), 256)
    if up:
        dx, dy_up, dsh, dsc, dg, dgt_up = res
        return dx, dy_up, dsh, dsc, dg, dgt_up
    dx, dsh, dsc, dg = res
    return dx, None, dsh, dsc, dg, None


def _loss_bwd(name, x, tgt, g, y_up, gt_up, factor_up):
    d = x.shape[1]

    def fn(iv, pv):
        (xv, tv, yv), (gv, gtv) = iv, pv
        r = lax.rsqrt(jnp.mean(xv * xv, axis=-1, keepdims=True) + EPS)
        xr = xv * r
        e = xr * gv - tv
        loss_row = 0.5 * jnp.mean(e * e, axis=-1, keepdims=True)
        dout = e * (1.0 / d)
        gd = gv * dout
        dx = r * (gd - xr * jnp.mean(gd * xr, axis=-1, keepdims=True))
        fdx = factor_up * dx
        return [dx, gtv * fdx], [_colsum(dout * xr), _colsum(loss_row * jnp.ones((1, 128), F32)), _colsum(fdx * yv.astype(F32))]

    return _rowwise(name, fn, [(x, d, 0), (tgt, d, 0), (y_up, d, 0)], [g, gt_up], [(d, F32), (d, BF16)], [d, 128, d], 256)


def _resgate_bwd(name, dx, y, gt, factor, deps=()):
    d = dx.shape[1]

    def fn(iv, pv):
        (dxv, yv), (gtv,) = iv, pv
        return [factor * gtv * dxv], [_colsum(factor * dxv * yv.astype(F32))]

    return _rowwise(name, fn, [(dx, d, 0), (y, d, 0)], [gt], [(d, BF16)], [d], 512, deps=deps)


def _swiglu_fwd(name, ab):
    f = ab.shape[1] // 2

    def fn(iv, pv):
        a, b = iv[0].astype(F32), iv[1].astype(F32)
        return [a * _sigmoid(a) * b], []

    return _rowwise(name, fn, [(ab, f, 0), (ab, f, 1)], [], [(f, BF16)], [], 256)[0]


def _swiglu_bwd(name, dact, ab, deps=()):
    f = ab.shape[1] // 2

    def fn(iv, pv):
        dv, a, b = iv[0].astype(F32), iv[1].astype(F32), iv[2].astype(F32)
        s = _sigmoid(a)
        return [[dv * b * (s * (1.0 + a * (1.0 - s))), dv * (a * s)]], []

    return _rowwise(name, fn, [(dact, f, 0), (ab, f, 0), (ab, f, 1)], [], [(2 * f, BF16)], [], 256, deps=deps)[0]


def _gates_fwd(name, y_pool, y_ssm, uin, d, pw):
    cb = (2 * pw) // d

    def fn(iv, pv):
        yp, ys, glp, gls = [v.astype(F32) for v in iv]
        return [_sigmoid(glp) * yp + _sigmoid(gls) * ys], []

    return _rowwise(name, fn, [(y_pool, d, 0), (y_ssm, d, 0), (uin, d, cb), (uin, d, cb + 1)], [], [(d, BF16)], [], 256)[0]


def _gates_bwd(name, dm, y_pool, y_ssm, uin, d, pw):
    cb = (2 * pw) // d

    def fn(iv, pv):
        dmv, yp, ys, glp, gls = [v.astype(F32) for v in iv]
        sp, ss = _sigmoid(glp), _sigmoid(gls)
        return [dmv * sp, dmv * ss, [dmv * yp * sp * (1.0 - sp), dmv * ys * ss * (1.0 - ss)]], []

    return _rowwise(name, fn, [(dm, d, 0), (y_pool, d, 0), (y_ssm, d, 0), (uin, d, cb), (uin, d, cb + 1)], [],
                    [(d, BF16), (d, BF16), (2 * d, BF16)], [], 256)


def _glu_fwd(name, gv, b_glu):
    w = gv.shape[1] // 2

    def fn(iv, pv):
        gvv, (bv,) = iv[0].astype(F32) + pv[0], pv
        return [gvv[:, :w] * _sigmoid(gvv[:, w:])], []

    return _rowwise(name, fn, [(gv, 2 * w, 0)], [b_glu], [(w, BF16)], [], 512)[0]


def _glu_bwd(name, dsg, gv, b_glu, deps=()):
    w = gv.shape[1] // 2

    def fn(iv, pv):
        dv = iv[0].astype(F32)
        gvv = iv[1].astype(F32) + pv[0]
        val, s = gvv[:, :w], _sigmoid(gvv[:, w:])
        dval = dv * s
        dgate = dv * val * s * (1.0 - s)
        return [[dval, dgate]], [jnp.concatenate([_colsum(dval), _colsum(dgate)], axis=1)]

    return _rowwise(name, fn, [(dsg, w, 0), (gv, 2 * w, 0)], [b_glu], [(2 * w, BF16)], [2 * w], 512, deps=deps)


def _adamw(name, w, g, m, v, tm=256):
    c = w.shape[1]

    def fn(iv, pv):
        wv, gv, mv, vv = iv
        mn = ADAM_B1 * mv + (1.0 - ADAM_B1) * gv
        vn = ADAM_B2 * vv + (1.0 - ADAM_B2) * (gv * gv)
        m_hat = mn / (1.0 - ADAM_B1 ** ADAM_STEP)
        v_hat = vn / (1.0 - ADAM_B2 ** ADAM_STEP)
        delta = -ADAM_LR * (m_hat / (jnp.sqrt(v_hat) + ADAM_EPS) + ADAM_WD * wv)
        return [delta, mn, vn], []

    return _rowwise(name, fn, [(w, c, 0), (g, c, 0), (m, c, 0), (v, c, 0)], [], [(c, F32)] * 3, [], _pick(w.shape[0], tm, 8))


def _adamw_small(name, w, g8, m, v):
    r = w.shape[0]

    def body(w_ref, g_ref, m_ref, v_ref, go_ref, d_ref, mo_ref, vo_ref):
        gv = g_ref[0:r, :]
        for k in range(1, N_DEV):
            gv = gv + g_ref[k * r:(k + 1) * r, :]
        mn = ADAM_B1 * m_ref[...] + (1.0 - ADAM_B1) * gv
        vn = ADAM_B2 * v_ref[...] + (1.0 - ADAM_B2) * (gv * gv)
        m_hat = mn / (1.0 - ADAM_B1 ** ADAM_STEP)
        v_hat = vn / (1.0 - ADAM_B2 ** ADAM_STEP)
        go_ref[...] = gv
        d_ref[...] = -ADAM_LR * (m_hat / (jnp.sqrt(v_hat) + ADAM_EPS) + ADAM_WD * w_ref[...])
        mo_ref[...] = mn
        vo_ref[...] = vn

    return _pcall(body, name=name, out_shape=[jax.ShapeDtypeStruct((r, 128), F32)] * 4,
                  compiler_params=pltpu.CompilerParams(vmem_limit_bytes=VMEM_LIMIT))(w, g8, m, v)


def _pool_fwd(name, uin, pool_w, pool_b, pool_scale, pw, tm=512):
    t = uin.shape[0]
    tm = min(tm, t)
    ng = len(POOL_WINDOWS)
    gw = pw // ng

    def body(u_ref, w_ref, b_ref, s_ref, p_ref, z_ref, ext):
        i = pl.program_id(0)

        @pl.when(i == 0)
        def _():
            ext[0:POOL_HALO, :] = jnp.zeros((POOL_HALO, pw), F32)

        u = u_ref[...]
        ext[POOL_HALO:POOL_HALO + tm, :] = u
        pos = i * tm + lax.broadcasted_iota(jnp.int32, (tm, 1), 0)
        for k, win in enumerate(POOL_WINDOWS):
            cols = slice(k * gw, (k + 1) * gw)
            acc = u[:, cols]
            for j in range(1, win):
                acc = acc + ext[POOL_HALO - j:POOL_HALO - j + tm, cols]
            cnt = jnp.minimum(pos + 1, win).astype(F32)
            z = acc / cnt - u[:, cols]
            zp = jnp.dot(z.astype(BF16), w_ref[k].astype(BF16), preferred_element_type=F32) + b_ref[:, cols]
            p_ref[:, cols] = (zp * s_ref[:, cols]).astype(BF16)
            z_ref[:, cols] = z.astype(BF16)
        ext[0:POOL_HALO, :] = u[tm - POOL_HALO:tm, :]

    return _pcall(
        body, name=name, grid=(t // tm,),
        in_specs=[pl.BlockSpec((tm, pw), lambda i: (i, 0)), pl.BlockSpec(pool_w.shape, lambda i: (0, 0, 0)),
                  pl.BlockSpec(pool_b.shape, lambda i: (0, 0)), pl.BlockSpec(pool_scale.shape, lambda i: (0, 0))],
        out_specs=[pl.BlockSpec((tm, pw), lambda i: (i, 0))] * 2,
        out_shape=[jax.ShapeDtypeStruct((t, pw), BF16)] * 2,
        scratch_shapes=[pltpu.VMEM((POOL_HALO + tm, pw), F32)],
        compiler_params=_params("arbitrary"))(uin, pool_w, pool_b, pool_scale)


def _pool_bwd(name, dp, z, pool_w, pool_b, pool_scale, tm=512):
    t, pw = z.shape
    tm = min(tm, t)
    nb = t // tm
    ng = len(POOL_WINDOWS)
    gw = pw // ng

    def body(dp_ref, z_ref, w_ref, b_ref, s_ref, du_ref, dw_ref, db_ref, ds_ref, ext):
        i = pl.program_id(0)

        @pl.when(i == 0)
        def _():
            ext[tm:tm + POOL_HALO, :] = jnp.zeros((POOL_HALO, pw), F32)
            dw_ref[...] = jnp.zeros_like(dw_ref)
            db_ref[...] = jnp.zeros_like(db_ref)
            ds_ref[...] = jnp.zeros_like(ds_ref)

        pos = (nb - 1 - i) * tm + lax.broadcasted_iota(jnp.int32, (tm, 1), 0)
        for k, win in enumerate(POOL_WINDOWS):
            cols = slice(k * gw, (k + 1) * gw)
            zk = z_ref[:, cols]
            dpk = dp_ref[:, cols].astype(F32)
            wk = w_ref[k].astype(BF16)
            zp = jnp.dot(zk, wk, preferred_element_type=F32) + b_ref[:, cols]
            ds_ref[:, cols] += _colsum(dpk * zp)
            dzp = dpk * s_ref[:, cols]
            db_ref[:, cols] += _colsum(dzp)
            dzpb = dzp.astype(BF16)
            dz = lax.dot_general(dzpb, wk, (((1,), (1,)), ((), ())), preferred_element_type=F32)
            dw_ref[k] += lax.dot_general(zk, dzpb, (((0,), (0,)), ((), ())), preferred_element_type=F32)
            cnt = jnp.minimum(pos + 1, win).astype(F32)
            r = dz / cnt
            ext[0:tm, cols] = r
            acc = r - dz
            for j in range(1, win):
                acc = acc + ext[j:j + tm, cols]
            du_ref[:, cols] = acc.astype(BF16)
        ext[tm:tm + POOL_HALO, :] = ext[0:POOL_HALO, :]

    rev = lambda i: (nb - 1 - i, 0)
    return _pcall(
        body, name=name, grid=(nb,),
        in_specs=[pl.BlockSpec((tm, pw), rev), pl.BlockSpec((tm, pw), rev), pl.BlockSpec(pool_w.shape, lambda i: (0, 0, 0)),
                  pl.BlockSpec(pool_b.shape, lambda i: (0, 0)), pl.BlockSpec(pool_scale.shape, lambda i: (0, 0))],
        out_specs=[pl.BlockSpec((tm, pw), rev), pl.BlockSpec(pool_w.shape, lambda i: (0, 0, 0)),
                   pl.BlockSpec((1, pw), lambda i: (0, 0)), pl.BlockSpec((1, pw), lambda i: (0, 0))],
        out_shape=[jax.ShapeDtypeStruct((t, pw), BF16), jax.ShapeDtypeStruct(pool_w.shape, F32),
                   jax.ShapeDtypeStruct((1, pw), F32), jax.ShapeDtypeStruct((1, pw), F32)],
        scratch_shapes=[pltpu.VMEM((tm + POOL_HALO, pw), F32)],
        compiler_params=_params("arbitrary"))(dp, z, pool_w, pool_b, pool_scale)


def _ssm_disc(lrl, li, ldt):
    lr = -jnp.exp(lrl)
    dt = jnp.exp(ldt)
    mag = jnp.exp(lr * dt)
    ang = li * dt
    ab_re = mag * jnp.cos(ang)
    ab_im = mag * jnp.sin(ang)
    num_re = ab_re - 1.0
    num_im = ab_im
    den = lr * lr + li * li
    f_re = (num_re * lr + num_im * li) / den
    f_im = (num_im * lr - num_re * li) / den
    return lr, dt, mag, ang, ab_re, ab_im, num_re, num_im, den, f_re, f_im


def _ssm_prep(name, lrl, li, ldt, b_re, b_im):
    gn, h = b_re.shape

    def body(lrl_ref, li_ref, ldt_ref, br_ref, bi_ref, lrdt_ref, ang_ref, bbr_ref, bbi_ref):
        lr, dt, _, ang, _, _, _, _, _, f_re, f_im = _ssm_disc(lrl_ref[...], li_ref[...], ldt_ref[...])
        lrdt_ref[...] = lr * dt
        ang_ref[...] = ang
        br, bi = br_ref[...], bi_ref[...]
        bbr_ref[...] = f_re * br - f_im * bi
        bbi_ref[...] = f_re * bi + f_im * br

    col = jax.ShapeDtypeStruct((gn, 1), F32)
    mat = jax.ShapeDtypeStruct((gn, h), F32)
    return _pcall(body, name=name, out_shape=[col, col, mat, mat])(lrl, li, ldt, b_re, b_im)


def _ssm_param_bwd(name, lrl, li, ldt, b_re, b_im, g_abre, g_abim, g_bbre, g_bbim):
    gn, h = b_re.shape

    def body(lrl_ref, li_ref, ldt_ref, br_ref, bi_ref, gar_ref, gai_ref, gbr_ref, gbi_ref,
             dlrl_ref, dli_ref, dldt_ref, dbr_ref, dbi_ref):
        li_v = li_ref[...]
        lr, dt, mag, ang, ab_re, ab_im, num_re, num_im, den, f_re, f_im = _ssm_disc(lrl_ref[...], li_v, ldt_ref[...])
        br, bi = br_ref[...], bi_ref[...]
        gbr, gbi = gbr_ref[...], gbi_ref[...]
        g_fre = jnp.sum(gbr * br + gbi * bi, axis=1, keepdims=True)
        g_fim = jnp.sum(gbi * br - gbr * bi, axis=1, keepdims=True)
        dbr_ref[...] = gbr * f_re + gbi * f_im
        dbi_ref[...] = gbi * f_re - gbr * f_im
        g_num_re = (g_fre * lr - g_fim * li_v) / den
        g_num_im = (g_fre * li_v + g_fim * lr) / den
        g_den = -(g_fre * f_re + g_fim * f_im) / den
        g_lr = (g_fre * num_re + g_fim * num_im) / den + g_den * 2.0 * lr
        g_li = (g_fre * num_im - g_fim * num_re) / den + g_den * 2.0 * li_v
        g_are = gar_ref[...] + g_num_re
        g_aim = gai_ref[...] + g_num_im
        g_mag = g_are * jnp.cos(ang) + g_aim * jnp.sin(ang)
        g_ang = g_aim * ab_re - g_are * ab_im
        g_lrdt = g_mag * mag
        g_lr = g_lr + g_lrdt * dt
        g_dt = g_lrdt * lr + g_ang * li_v
        g_li = g_li + g_ang * dt
        dlrl_ref[...] = g_lr * lr
        dli_ref[...] = g_li
        dldt_ref[...] = g_dt * dt

    col = jax.ShapeDtypeStruct((gn, 1), F32)
    mat = jax.ShapeDtypeStruct((gn, h), F32)
    return _pcall(body, name=name, out_shape=[col, col, col, mat, mat])(lrl, li, ldt, b_re, b_im, g_abre, g_abim, g_bbre, g_bbim)


def _pow_rows(lrdt, ang, k):
    mag = jnp.exp(k * lrdt)
    return mag * jnp.cos(k * ang), mag * jnp.sin(k * ang)


def _ssm_chunk(t):
    return 256 if t >= 2048 else 128


def _to_segments(dst, srcs, jn):
    for q, src in enumerate(srcs):
        for j in range(jn):
            dst[8 * j:8 * j + 8, 128 * q:128 * (q + 1)] = src[pl.ds(j, 8, stride=jn), :]


def _from_segments(dst, src, q, jn, dtype):
    for s in range(8):
        dst[s * jn:(s + 1) * jn, 128 * q:128 * (q + 1)] = src[q, pl.ds(s, jn, stride=8), :].astype(dtype)


def _fill_rows8(dst_re, dst_im, v_re, v_im):
    for j in range(v_re.shape[0]):
        dst_re[8 * j:8 * j + 8, :] = jnp.broadcast_to(v_re[j:j + 1, :], (8, v_re.shape[1]))
        dst_im[8 * j:8 * j + 8, :] = jnp.broadcast_to(v_im[j:j + 1, :], (8, v_im.shape[1]))


def _cmul(ar, ai, br, bi):
    return ar * br - ai * bi, ar * bi + ai * br


def _cmul_conj(ar, ai, br, bi):
    return ar * br + ai * bi, ar * bi - ai * br


def _ssm_fwd(name, uin, lrdt, ang, bb_re, bb_im, cc_re, cc_im, d_skip, sw):
    t = uin.shape[0]
    gn = lrdt.shape[1]
    lc = _ssm_chunk(t)
    jn = lc // 8
    ub, sb = sw // SSM_BLOCKS, gn // SSM_BLOCKS
    nq = sw // 128
    assert ub == 128 and nq == SSM_BLOCKS

    def body(*refs):
        u_refs = refs[:nq]
        (lrdt_ref, ang_ref, bbr_ref, bbi_ref, ccr_ref, cci_ref, d_ref, y_ref, ge_ref, sre_ref, sim_ref,
         p_re, p_im, a_re, a_im, up, yp, gp, cst_re, cst_im, car_re, car_im) = refs[nq:]
        i = pl.program_id(0)
        lrdt_v, ang_v = lrdt_ref[...], ang_ref[...]

        @pl.when(i == 0)
        def _():
            k = (lax.broadcasted_iota(jnp.int32, (jn, 1), 0) + 1).astype(F32)
            _fill_rows8(p_re, p_im, *_pow_rows(lrdt_v, ang_v, k))
            car_re[...] = jnp.zeros_like(car_re)
            car_im[...] = jnp.zeros_like(car_im)

        _to_segments(up, u_refs, jn)
        u = up[...]
        ubf = u.astype(BF16)
        for q in range(SSM_BLOCKS):
            uq = ubf[:, q * ub:(q + 1) * ub]
            a_re[:, q * sb:(q + 1) * sb] = jnp.dot(uq, bbr_ref[q], preferred_element_type=F32)
            a_im[:, q * sb:(q + 1) * sb] = jnp.dot(uq, bbi_ref[q], preferred_element_type=F32)
        a1r, a1i = _pow_rows(lrdt_v, ang_v, 1.0)
        ajr, aji = _pow_rows(lrdt_v, ang_v, float(jn))
        for q in range(SSM_BLOCKS):
            cols = slice(q * sb, (q + 1) * sb)
            ar8 = jnp.broadcast_to(a1r[:, cols], (8, sb))
            ai8 = jnp.broadcast_to(a1i[:, cols], (8, sb))

            def step(j, carry, cols=cols, ar8=ar8, ai8=ai8):
                sr, si = carry
                rows = pl.ds(pl.multiple_of(j * 8, 8), 8)
                mr, mi = _cmul(ar8, ai8, sr, si)
                nr, ni = mr + a_re[rows, cols], mi + a_im[rows, cols]
                a_re[rows, cols] = nr
                a_im[rows, cols] = ni
                return nr, ni

            lax.fori_loop(1, jn, step, (a_re[0:8, cols], a_im[0:8, cols]), unroll=4)
        er, ei = a_re[lc - 8:lc, :], a_im[lc - 8:lc, :]
        gr, gi = car_re[...], car_im[...]
        for s in range(8):
            cst_re[s:s + 1, :] = gr
            cst_im[s:s + 1, :] = gi
            mr, mi = _cmul(ajr, aji, gr, gi)
            gr, gi = mr + er[s:s + 1, :], mi + ei[s:s + 1, :]
        car_re[...] = gr
        car_im[...] = gi
        for q in range(SSM_BLOCKS):
            cols = slice(q * sb, (q + 1) * sb)
            cr = jnp.tile(cst_re[:, cols], (jn, 1))
            ci = jnp.tile(cst_im[:, cols], (jn, 1))
            mr, mi = _cmul(p_re[:, cols], p_im[:, cols], cr, ci)
            sr, si = a_re[:, cols] + mr, a_im[:, cols] + mi
            sre_ref[:, cols] = sr
            sim_ref[:, cols] = si
            ycols = slice(q * ub, (q + 1) * ub)
            y = (jnp.dot(sr.astype(BF16), ccr_ref[q], preferred_element_type=F32)
                 - jnp.dot(si.astype(BF16), cci_ref[q], preferred_element_type=F32)
                 + d_ref[:, ycols] * u[:, ycols])
            yp[q] = y
            gp[q] = 0.5 * y * (1.0 + lax.erf(y * (1.0 / math.sqrt(2.0))))
            _from_segments(y_ref, yp, q, jn, F32)
            _from_segments(ge_ref, gp, q, jn, BF16)

    row = lambda i: (0, 0)
    blk3 = lambda i: (0, 0, 0)
    return _pcall(
        body, name=name, grid=(t // lc,),
        in_specs=[pl.BlockSpec((lc, 128), functools.partial(lambda i, q: (i, nq + q), q=q)) for q in range(nq)]
        + [pl.BlockSpec((1, gn), row), pl.BlockSpec((1, gn), row),
           pl.BlockSpec(bb_re.shape, blk3), pl.BlockSpec(bb_im.shape, blk3),
           pl.BlockSpec(cc_re.shape, blk3), pl.BlockSpec(cc_im.shape, blk3), pl.BlockSpec((1, sw), row)],
        out_specs=[pl.BlockSpec((lc, sw), lambda i: (i, 0)), pl.BlockSpec((lc, sw), lambda i: (i, 0)),
                   pl.BlockSpec((lc, gn), lambda i: (i, 0)), pl.BlockSpec((lc, gn), lambda i: (i, 0))],
        out_shape=[jax.ShapeDtypeStruct((t, sw), F32), jax.ShapeDtypeStruct((t, sw), BF16),
                   jax.ShapeDtypeStruct((t, gn), F32), jax.ShapeDtypeStruct((t, gn), F32)],
        scratch_shapes=[pltpu.VMEM((lc, gn), F32), pltpu.VMEM((lc, gn), F32), pltpu.VMEM((lc, gn), F32), pltpu.VMEM((lc, gn), F32),
                        pltpu.VMEM((lc, sw), F32), pltpu.VMEM((nq, lc, 128), F32), pltpu.VMEM((nq, lc, 128), F32),
                        pltpu.VMEM((8, gn), F32), pltpu.VMEM((8, gn), F32), pltpu.VMEM((1, gn), F32), pltpu.VMEM((1, gn), F32)],
        compiler_params=_params("arbitrary"))(*([uin] * nq), lrdt, ang, bb_re, bb_im, cc_re, cc_im, d_skip)


def _ssm_bwd(name, dge, y, uin, s_re, s_im, lrdt, ang, bbt_re, bbt_im, cct_re, cct_im, d_skip, sw):
    t = uin.shape[0]
    gn = lrdt.shape[1]
    lc = _ssm_chunk(t)
    nb = t // lc
    jn = lc // 8
    ub, sb = sw // SSM_BLOCKS, gn // SSM_BLOCKS
    nq = sw // 128
    tail = 8

    def body(*refs):
        dge_refs, y_refs, u_refs = refs[:nq], refs[nq:2 * nq], refs[2 * nq:3 * nq]
        (sre_ref, sim_ref, tre_ref, tim_ref, lrdt_ref, ang_ref, btr_ref, bti_ref, ctr_ref, cti_ref, d_ref,
         du_ref, dar_ref, dai_ref, dbr_ref, dbi_ref, dcr_ref, dci_ref, dd_ref,
         q_re, q_im, a_re, a_im, dgp, yp, up, dup, cst_re, cst_im, sp_re, sp_im, car_re, car_im) = refs[3 * nq:]
        i = pl.program_id(0)
        lrdt_v, ang_v = lrdt_ref[...], ang_ref[...]

        @pl.when(i == 0)
        def _():
            k = (jn - lax.broadcasted_iota(jnp.int32, (jn, 1), 0)).astype(F32)
            _fill_rows8(q_re, q_im, *_pow_rows(lrdt_v, ang_v, k))
            car_re[...] = jnp.zeros_like(car_re)
            car_im[...] = jnp.zeros_like(car_im)
            for r in (dar_ref, dai_ref, dbr_ref, dbi_ref, dcr_ref, dci_ref, dd_ref):
                r[...] = jnp.zeros_like(r)

        _to_segments(dgp, dge_refs, jn)
        _to_segments(yp, y_refs, jn)
        _to_segments(up, u_refs, jn)
        yv = yp[...]
        u = up[...]
        cdf = 0.5 * (1.0 + lax.erf(yv * (1.0 / math.sqrt(2.0))))
        pdf = jnp.exp(-0.5 * yv * yv) * (1.0 / math.sqrt(2.0 * math.pi))
        dy = dgp[...] * (cdf + yv * pdf)
        dd_ref[...] += _colsum(dy * u)
        dyb = dy.astype(BF16)
        ubf = u.astype(BF16)
        for q in range(SSM_BLOCKS):
            dq = dyb[:, q * ub:(q + 1) * ub]
            a_re[:, q * sb:(q + 1) * sb] = jnp.dot(dq, ctr_ref[q], preferred_element_type=F32)
            a_im[:, q * sb:(q + 1) * sb] = -jnp.dot(dq, cti_ref[q], preferred_element_type=F32)
        a1r, a1i = _pow_rows(lrdt_v, ang_v, 1.0)
        ajr, aji = _pow_rows(lrdt_v, ang_v, float(jn))
        for q in range(SSM_BLOCKS):
            cols = slice(q * sb, (q + 1) * sb)
            ar8 = jnp.broadcast_to(a1r[:, cols], (8, sb))
            ai8 = jnp.broadcast_to(a1i[:, cols], (8, sb))

            def step(jj, carry, cols=cols, ar8=ar8, ai8=ai8):
                sr, si = carry
                rows = pl.ds(pl.multiple_of((jn - 2 - jj) * 8, 8), 8)
                mr, mi = _cmul_conj(ar8, ai8, sr, si)
                nr, ni = mr + a_re[rows, cols], mi + a_im[rows, cols]
                a_re[rows, cols] = nr
                a_im[rows, cols] = ni
                return nr, ni

            lax.fori_loop(0, jn - 1, step, (a_re[lc - 8:lc, cols], a_im[lc - 8:lc, cols]), unroll=4)
        er, ei = a_re[0:8, :], a_im[0:8, :]
        hr, hi = car_re[...], car_im[...]
        for s in range(7, -1, -1):
            cst_re[s:s + 1, :] = hr
            cst_im[s:s + 1, :] = hi
            mr, mi = _cmul_conj(ajr, aji, hr, hi)
            hr, hi = mr + er[s:s + 1, :], mi + ei[s:s + 1, :]
        car_re[...] = hr
        car_im[...] = hi
        first = (i == nb - 1).astype(F32)
        sp_re[0:tail, :] = tre_ref[...] * (1.0 - first)
        sp_im[0:tail, :] = tim_ref[...] * (1.0 - first)
        sp_re[tail:2 * tail, :] = sre_ref[lc - 8:lc, :]
        sp_im[tail:2 * tail, :] = sim_ref[lc - 8:lc, :]
        tn_dims = (((0,), (0,)), ((), ()))
        for q in range(SSM_BLOCKS):
            cols = slice(q * sb, (q + 1) * sb)
            ycols = slice(q * ub, (q + 1) * ub)
            cr = jnp.tile(cst_re[:, cols], (jn, 1))
            ci = jnp.tile(cst_im[:, cols], (jn, 1))
            mr, mi = _cmul_conj(q_re[:, cols], q_im[:, cols], cr, ci)
            lam_r, lam_i = a_re[:, cols] + mr, a_im[:, cols] + mi
            s_r, s_i = sre_ref[:, cols], sim_ref[:, cols]
            p0r, p0i = sp_re[tail - 1:2 * tail - 1, cols], sp_im[tail - 1:2 * tail - 1, cols]
            l0r, l0i, l1r, l1i = lam_r[0:8], lam_i[0:8], lam_r[8:lc], lam_i[8:lc]
            pvr, pvi = s_r[0:lc - 8], s_i[0:lc - 8]
            dar_ref[:, cols] += _colsum(l1r * pvr + l1i * pvi) + _colsum(l0r * p0r + l0i * p0i)
            dai_ref[:, cols] += _colsum(l1i * pvr - l1r * pvi) + _colsum(l0i * p0r - l0r * p0i)
            lrb, lib = lam_r.astype(BF16), lam_i.astype(BF16)
            dup[q] = (jnp.dot(lrb, btr_ref[q], preferred_element_type=F32)
                      + jnp.dot(lib, bti_ref[q], preferred_element_type=F32) + d_ref[:, ycols] * dy[:, ycols])
            _from_segments(du_ref, dup, q, jn, BF16)
            uq = ubf[:, ycols]
            dbr_ref[q] += lax.dot_general(uq, lrb, tn_dims, preferred_element_type=F32)
            dbi_ref[q] += lax.dot_general(uq, lib, tn_dims, preferred_element_type=F32)
            dq = dyb[:, ycols]
            dcr_ref[q] += lax.dot_general(s_r.astype(BF16), dq, tn_dims, preferred_element_type=F32)
            dci_ref[q] -= lax.dot_general(s_i.astype(BF16), dq, tn_dims, preferred_element_type=F32)

    rev = lambda i: (nb - 1 - i, 0)
    tailmap = lambda i: (jnp.maximum((nb - 1 - i) * (lc // tail) - 1, 0), 0)
    row = lambda i: (0, 0)
    blk3 = lambda i: (0, 0, 0)
    colblk = lambda base: [pl.BlockSpec((lc, 128), functools.partial(lambda i, q: (nb - 1 - i, base + q), q=q)) for q in range(nq)]
    return _pcall(
        body, name=name, grid=(nb,),
        in_specs=colblk(0) + colblk(0) + colblk(nq)
        + [pl.BlockSpec((lc, gn), rev), pl.BlockSpec((lc, gn), rev),
           pl.BlockSpec((tail, gn), tailmap), pl.BlockSpec((tail, gn), tailmap),
           pl.BlockSpec((1, gn), row), pl.BlockSpec((1, gn), row),
           pl.BlockSpec(bbt_re.shape, blk3), pl.BlockSpec(bbt_im.shape, blk3),
           pl.BlockSpec(cct_re.shape, blk3), pl.BlockSpec(cct_im.shape, blk3), pl.BlockSpec((1, sw), row)],
        out_specs=[pl.BlockSpec((lc, sw), rev), pl.BlockSpec((1, gn), row), pl.BlockSpec((1, gn), row),
                   pl.BlockSpec((SSM_BLOCKS, ub, sb), blk3), pl.BlockSpec((SSM_BLOCKS, ub, sb), blk3),
                   pl.BlockSpec((SSM_BLOCKS, sb, ub), blk3), pl.BlockSpec((SSM_BLOCKS, sb, ub), blk3),
                   pl.BlockSpec((1, sw), row)],
        out_shape=[jax.ShapeDtypeStruct((t, sw), BF16), jax.ShapeDtypeStruct((1, gn), F32), jax.ShapeDtypeStruct((1, gn), F32),
                   jax.ShapeDtypeStruct((SSM_BLOCKS, ub, sb), F32), jax.ShapeDtypeStruct((SSM_BLOCKS, ub, sb), F32),
                   jax.ShapeDtypeStruct((SSM_BLOCKS, sb, ub), F32), jax.ShapeDtypeStruct((SSM_BLOCKS, sb, ub), F32),
                   jax.ShapeDtypeStruct((1, sw), F32)],
        scratch_shapes=[pltpu.VMEM((lc, gn), F32), pltpu.VMEM((lc, gn), F32), pltpu.VMEM((lc, gn), F32), pltpu.VMEM((lc, gn), F32),
                        pltpu.VMEM((lc, sw), F32), pltpu.VMEM((lc, sw), F32), pltpu.VMEM((lc, sw), F32),
                        pltpu.VMEM((nq, lc, 128), F32), pltpu.VMEM((8, gn), F32), pltpu.VMEM((8, gn), F32),
                        pltpu.VMEM((2 * tail, gn), F32), pltpu.VMEM((2 * tail, gn), F32),
                        pltpu.VMEM((1, gn), F32), pltpu.VMEM((1, gn), F32)],
        compiler_params=_params("arbitrary"))(*([dge] * nq), *([y] * nq), *([uin] * nq), s_re, s_im, s_re, s_im, lrdt, ang,
                                               bbt_re, bbt_im, cct_re, cct_im, d_skip)


def _ssm_fwd_doubling(name, uin, lrdt, ang, bb_re, bb_im, cc_re, cc_im, d_skip, sw):
    t = uin.shape[0]
    gn = lrdt.shape[1]
    lc = _ssm_chunk(t)
    nsteps = int(math.log2(lc))
    pad = lc // 2
    ub, sb = sw // SSM_BLOCKS, gn // SSM_BLOCKS

    def body(u_ref, lrdt_ref, ang_ref, bbr_ref, bbi_ref, ccr_ref, cci_ref, d_ref, y_ref, ge_ref, sre_ref, sim_ref,
             p_re, p_im, a_re, a_im, b_re, b_im, car_re, car_im):
        i = pl.program_id(0)
        lrdt_v, ang_v = lrdt_ref[...], ang_ref[...]

        @pl.when(i == 0)
        def _():
            k = (lax.broadcasted_iota(jnp.int32, (lc, 1), 0) + 1).astype(F32)
            pr, pi = _pow_rows(lrdt_v, ang_v, k)
            p_re[...] = pr
            p_im[...] = pi
            zeros = jnp.zeros((pad, gn), F32)
            a_re[0:pad, :] = zeros
            a_im[0:pad, :] = zeros
            b_re[0:pad, :] = zeros
            b_im[0:pad, :] = zeros
            car_re[...] = jnp.zeros_like(car_re)
            car_im[...] = jnp.zeros_like(car_im)

        u = u_ref[...]
        ubf = u.astype(BF16)
        for q in range(SSM_BLOCKS):
            uq = ubf[:, q * ub:(q + 1) * ub]
            a_re[pad:pad + lc, q * sb:(q + 1) * sb] = jnp.dot(uq, bbr_ref[q], preferred_element_type=F32)
            a_im[pad:pad + lc, q * sb:(q + 1) * sb] = jnp.dot(uq, bbi_ref[q], preferred_element_type=F32)
        bufs = [(a_re, a_im), (b_re, b_im)]
        for q in range(SSM_BLOCKS):
            cols = slice(q * sb, (q + 1) * sb)
            for j in range(nsteps):
                dd = 1 << j
                (s_re, s_im), (d_re, d_im) = bufs[j % 2], bufs[(j + 1) % 2]
                pr, pi = _pow_rows(lrdt_v[:, cols], ang_v[:, cols], float(dd))
                cr, ci = s_re[pad:pad + lc, cols], s_im[pad:pad + lc, cols]
                hr, hi = s_re[pad - dd:pad - dd + lc, cols], s_im[pad - dd:pad - dd + lc, cols]
                d_re[pad:pad + lc, cols] = cr + (pr * hr - pi * hi)
                d_im[pad:pad + lc, cols] = ci + (pr * hi + pi * hr)
            f_re, f_im = bufs[nsteps % 2]
            cr, ci = car_re[:, cols], car_im[:, cols]
            pr, pi = p_re[:, cols], p_im[:, cols]
            sr = f_re[pad:pad + lc, cols] + (pr * cr - pi * ci)
            si = f_im[pad:pad + lc, cols] + (pr * ci + pi * cr)
            sre_ref[:, cols] = sr
            sim_ref[:, cols] = si
            car_re[:, cols] = sr[lc - 1:lc, :]
            car_im[:, cols] = si[lc - 1:lc, :]
            ycols = slice(q * ub, (q + 1) * ub)
            y = (jnp.dot(sr.astype(BF16), ccr_ref[q], preferred_element_type=F32)
                 - jnp.dot(si.astype(BF16), cci_ref[q], preferred_element_type=F32)
                 + d_ref[:, ycols] * u[:, ycols])
            y_ref[:, ycols] = y
            ge_ref[:, ycols] = (0.5 * y * (1.0 + lax.erf(y * (1.0 / math.sqrt(2.0))))).astype(BF16)

    cblk = sw // sw
    row = lambda i: (0, 0)
    blk3 = lambda i: (0, 0, 0)
    return _pcall(
        body, name=name, grid=(t // lc,),
        in_specs=[pl.BlockSpec((lc, sw), lambda i: (i, cblk)), pl.BlockSpec((1, gn), row), pl.BlockSpec((1, gn), row),
                  pl.BlockSpec(bb_re.shape, blk3), pl.BlockSpec(bb_im.shape, blk3),
                  pl.BlockSpec(cc_re.shape, blk3), pl.BlockSpec(cc_im.shape, blk3), pl.BlockSpec((1, sw), row)],
        out_specs=[pl.BlockSpec((lc, sw), lambda i: (i, 0)), pl.BlockSpec((lc, sw), lambda i: (i, 0)),
                   pl.BlockSpec((lc, gn), lambda i: (i, 0)), pl.BlockSpec((lc, gn), lambda i: (i, 0))],
        out_shape=[jax.ShapeDtypeStruct((t, sw), F32), jax.ShapeDtypeStruct((t, sw), BF16),
                   jax.ShapeDtypeStruct((t, gn), F32), jax.ShapeDtypeStruct((t, gn), F32)],
        scratch_shapes=[pltpu.VMEM((lc, gn), F32), pltpu.VMEM((lc, gn), F32),
                        pltpu.VMEM((pad + lc, gn), F32), pltpu.VMEM((pad + lc, gn), F32),
                        pltpu.VMEM((pad + lc, gn), F32), pltpu.VMEM((pad + lc, gn), F32),
                        pltpu.VMEM((1, gn), F32), pltpu.VMEM((1, gn), F32)],
        compiler_params=_params("arbitrary"))(uin, lrdt, ang, bb_re, bb_im, cc_re, cc_im, d_skip)


def _ssm_bwd_doubling(name, dge, y, uin, s_re, s_im, lrdt, ang, bbt_re, bbt_im, cct_re, cct_im, d_skip, sw):
    t = uin.shape[0]
    gn = lrdt.shape[1]
    lc = _ssm_chunk(t)
    nb = t // lc
    nsteps = int(math.log2(lc))
    pad = lc // 2
    ub, sb = sw // SSM_BLOCKS, gn // SSM_BLOCKS
    tail = 8

    def body(dge_ref, y_ref, u_ref, sre_ref, sim_ref, tre_ref, tim_ref, lrdt_ref, ang_ref, btr_ref, bti_ref, ctr_ref, cti_ref,
             d_ref, du_ref, dar_ref, dai_ref, dbr_ref, dbi_ref, dcr_ref, dci_ref, dd_ref,
             q_re, q_im, a_re, a_im, b_re, b_im, sp_re, sp_im, car_re, car_im):
        i = pl.program_id(0)
        lrdt_v, ang_v = lrdt_ref[...], ang_ref[...]

        @pl.when(i == 0)
        def _():
            k = (lc - lax.broadcasted_iota(jnp.int32, (lc, 1), 0)).astype(F32)
            qr, qi = _pow_rows(lrdt_v, ang_v, k)
            q_re[...] = qr
            q_im[...] = qi
            zeros = jnp.zeros((pad, gn), F32)
            a_re[lc:lc + pad, :] = zeros
            a_im[lc:lc + pad, :] = zeros
            b_re[lc:lc + pad, :] = zeros
            b_im[lc:lc + pad, :] = zeros
            car_re[...] = jnp.zeros_like(car_re)
            car_im[...] = jnp.zeros_like(car_im)
            for r in (dar_ref, dai_ref, dbr_ref, dbi_ref, dcr_ref, dci_ref, dd_ref):
                r[...] = jnp.zeros_like(r)

        first = (i == nb - 1).astype(F32)
        sp_re[0:tail, :] = tre_ref[...] * (1.0 - first)
        sp_im[0:tail, :] = tim_ref[...] * (1.0 - first)
        sp_re[tail:tail + lc, :] = sre_ref[...]
        sp_im[tail:tail + lc, :] = sim_ref[...]

        yv = y_ref[...]
        u = u_ref[...]
        cdf = 0.5 * (1.0 + lax.erf(yv * (1.0 / math.sqrt(2.0))))
        pdf = jnp.exp(-0.5 * yv * yv) * (1.0 / math.sqrt(2.0 * math.pi))
        dy = dge_ref[...].astype(F32) * (cdf + yv * pdf)
        dd_ref[...] += _colsum(dy * u)
        dyb = dy.astype(BF16)
        ubf = u.astype(BF16)
        for q in range(SSM_BLOCKS):
            dq = dyb[:, q * ub:(q + 1) * ub]
            a_re[0:lc, q * sb:(q + 1) * sb] = jnp.dot(dq, ctr_ref[q], preferred_element_type=F32)
            a_im[0:lc, q * sb:(q + 1) * sb] = -jnp.dot(dq, cti_ref[q], preferred_element_type=F32)
        bufs = [(a_re, a_im), (b_re, b_im)]
        tn_dims = (((0,), (0,)), ((), ()))
        for q in range(SSM_BLOCKS):
            cols = slice(q * sb, (q + 1) * sb)
            ycols = slice(q * ub, (q + 1) * ub)
            for j in range(nsteps):
                ds = 1 << j
                (s_r, s_i), (d_r, d_i) = bufs[j % 2], bufs[(j + 1) % 2]
                pr, pi = _pow_rows(lrdt_v[:, cols], ang_v[:, cols], float(ds))
                cr, ci = s_r[0:lc, cols], s_i[0:lc, cols]
                hr, hi = s_r[ds:ds + lc, cols], s_i[ds:ds + lc, cols]
                d_r[0:lc, cols] = cr + (pr * hr + pi * hi)
                d_i[0:lc, cols] = ci + (pr * hi - pi * hr)
            f_r, f_i = bufs[nsteps % 2]
            cr, ci = car_re[:, cols], car_im[:, cols]
            qr, qi = q_re[:, cols], q_im[:, cols]
            lam_r = f_r[0:lc, cols] + (qr * cr + qi * ci)
            lam_i = f_i[0:lc, cols] + (qr * ci - qi * cr)
            car_re[:, cols] = lam_r[0:1, :]
            car_im[:, cols] = lam_i[0:1, :]
            pvr, pvi = sp_re[tail - 1:tail - 1 + lc, cols], sp_im[tail - 1:tail - 1 + lc, cols]
            dar_ref[:, cols] += _colsum(lam_r * pvr + lam_i * pvi)
            dai_ref[:, cols] += _colsum(lam_i * pvr - lam_r * pvi)
            lrb, lib = lam_r.astype(BF16), lam_i.astype(BF16)
            du_ref[:, ycols] = (jnp.dot(lrb, btr_ref[q], preferred_element_type=F32)
                                + jnp.dot(lib, bti_ref[q], preferred_element_type=F32)
                                + d_ref[:, ycols] * dy[:, ycols]).astype(BF16)
            uq = ubf[:, ycols]
            dbr_ref[q] += lax.dot_general(uq, lrb, tn_dims, preferred_element_type=F32)
            dbi_ref[q] += lax.dot_general(uq, lib, tn_dims, preferred_element_type=F32)
            dq = dyb[:, ycols]
            dcr_ref[q] += lax.dot_general(sre_ref[:, cols].astype(BF16), dq, tn_dims, preferred_element_type=F32)
            dci_ref[q] -= lax.dot_general(sim_ref[:, cols].astype(BF16), dq, tn_dims, preferred_element_type=F32)

    cblk = 1
    rev = lambda i: (nb - 1 - i, 0)
    revu = lambda i: (nb - 1 - i, cblk)
    tailmap = lambda i: (jnp.maximum((nb - 1 - i) * (lc // tail) - 1, 0), 0)
    row = lambda i: (0, 0)
    blk3 = lambda i: (0, 0, 0)
    return _pcall(
        body, name=name, grid=(nb,),
        in_specs=[pl.BlockSpec((lc, sw), rev), pl.BlockSpec((lc, sw), rev), pl.BlockSpec((lc, sw), revu),
                  pl.BlockSpec((lc, gn), rev), pl.BlockSpec((lc, gn), rev),
                  pl.BlockSpec((tail, gn), tailmap), pl.BlockSpec((tail, gn), tailmap),
                  pl.BlockSpec((1, gn), row), pl.BlockSpec((1, gn), row),
                  pl.BlockSpec(bbt_re.shape, blk3), pl.BlockSpec(bbt_im.shape, blk3),
                  pl.BlockSpec(cct_re.shape, blk3), pl.BlockSpec(cct_im.shape, blk3), pl.BlockSpec((1, sw), row)],
        out_specs=[pl.BlockSpec((lc, sw), rev), pl.BlockSpec((1, gn), row), pl.BlockSpec((1, gn), row),
                   pl.BlockSpec((SSM_BLOCKS, ub, sb), blk3), pl.BlockSpec((SSM_BLOCKS, ub, sb), blk3),
                   pl.BlockSpec((SSM_BLOCKS, sb, ub), blk3), pl.BlockSpec((SSM_BLOCKS, sb, ub), blk3),
                   pl.BlockSpec((1, sw), row)],
        out_shape=[jax.ShapeDtypeStruct((t, sw), BF16), jax.ShapeDtypeStruct((1, gn), F32), jax.ShapeDtypeStruct((1, gn), F32),
                   jax.ShapeDtypeStruct((SSM_BLOCKS, ub, sb), F32), jax.ShapeDtypeStruct((SSM_BLOCKS, ub, sb), F32),
                   jax.ShapeDtypeStruct((SSM_BLOCKS, sb, ub), F32), jax.ShapeDtypeStruct((SSM_BLOCKS, sb, ub), F32),
                   jax.ShapeDtypeStruct((1, sw), F32)],
        scratch_shapes=[pltpu.VMEM((lc, gn), F32), pltpu.VMEM((lc, gn), F32),
                        pltpu.VMEM((lc + pad, gn), F32), pltpu.VMEM((lc + pad, gn), F32),
                        pltpu.VMEM((lc + pad, gn), F32), pltpu.VMEM((lc + pad, gn), F32),
                        pltpu.VMEM((tail + lc, gn), F32), pltpu.VMEM((tail + lc, gn), F32),
                        pltpu.VMEM((1, gn), F32), pltpu.VMEM((1, gn), F32)],
        compiler_params=_params("arbitrary"))(dge, y, uin, s_re, s_im, s_re, s_im, lrdt, ang, bbt_re, bbt_im, cct_re, cct_im, d_skip)


def _blockdiag_b(bb, sw):
    gpb = (sw // SSM_GROUP) // SSM_BLOCKS
    b4 = bb.reshape(SSM_BLOCKS, gpb, SSM_STATE, SSM_GROUP)
    eye = jnp.eye(gpb, dtype=bb.dtype)
    out = jnp.einsum('qgnh,gk->qghkn', b4, eye)
    return out.reshape(SSM_BLOCKS, gpb * SSM_GROUP, gpb * SSM_STATE)


def _blockdiag_c(cc, sw):
    gpb = (sw // SSM_GROUP) // SSM_BLOCKS
    c4 = cc.reshape(SSM_BLOCKS, gpb, SSM_GROUP, SSM_STATE)
    eye = jnp.eye(gpb, dtype=cc.dtype)
    out = jnp.einsum('qghn,gk->qgnkh', c4, eye)
    return out.reshape(SSM_BLOCKS, gpb * SSM_STATE, gpb * SSM_GROUP)


def _diag_of_b(dbb, sw):
    gpb = (sw // SSM_GROUP) // SSM_BLOCKS
    d5 = dbb.reshape(SSM_BLOCKS, gpb, SSM_GROUP, gpb, SSM_STATE)
    return jnp.einsum('qghgn->qgnh', d5).reshape(SSM_BLOCKS * gpb * SSM_STATE, SSM_GROUP)


def _diag_of_c(dcc, sw):
    gpb = (sw // SSM_GROUP) // SSM_BLOCKS
    d5 = dcc.reshape(SSM_BLOCKS, gpb, SSM_STATE, gpb, SSM_GROUP)
    return jnp.einsum('qgngh->qghn', d5).reshape(SSM_BLOCKS * gpb, SSM_GROUP, SSM_STATE)


def _ada_fwd(name, c_all, w_sh, b_sh):
    nb, d = c_all.shape
    ncol = w_sh.shape[1]
    tn = _pick(ncol, 768)

    def body(c_ref, w_ref, b_ref, o_ref):
        cv = c_ref[...]
        sil = cv * _sigmoid(cv)
        o_ref[...] = jnp.dot(sil, w_ref[...], preferred_element_type=F32, precision=lax.Precision.HIGHEST) + b_ref[...]

    return _pcall(body, name=name, grid=(ncol // tn,),
                  in_specs=[pl.BlockSpec((nb, d), lambda j: (0, 0)), pl.BlockSpec((d, tn), lambda j: (0, j)),
                            pl.BlockSpec((1, tn), lambda j: (0, j))],
                  out_specs=pl.BlockSpec((nb, tn), lambda j: (0, j)),
                  out_shape=jax.ShapeDtypeStruct((nb, ncol), F32), compiler_params=_params("parallel"))(c_all, w_sh, b_sh)


def _ada_bwd(name, c_all, dmod_sh):
    nb, d = c_all.shape
    ncol = dmod_sh.shape[1]
    tn = _pick(ncol, 768)

    def body(c_ref, g_ref, o_ref):
        cv = c_ref[...]
        sil = cv * _sigmoid(cv)
        o_ref[...] = lax.dot_general(sil, g_ref[...], (((0,), (0,)), ((), ())), preferred_element_type=F32,
                                     precision=lax.Precision.HIGHEST)

    return _pcall(body, name=name, grid=(ncol // tn,),
                  in_specs=[pl.BlockSpec((nb, d), lambda j: (0, 0)), pl.BlockSpec((nb, tn), lambda j: (0, j))],
                  out_specs=pl.BlockSpec((d, tn), lambda j: (0, j)),
                  out_shape=jax.ShapeDtypeStruct((d, ncol), F32), compiler_params=_params("parallel"))(c_all, dmod_sh)


def _place():
    return lax.axis_index("x"), lax.axis_index("y"), lax.axis_index("c")


def _allgather_small(name, blk, deps=()):
    m_per, n = blk.shape

    def body(x_ref, *rest):
        out_ref, send_sems, recv_sems, local_sem = rest[len(deps):]
        x, y, c = _place()
        me, sibling = (x, y, c), (x, y, 1 - c)
        chips = [(1 - x, y), (x, 1 - y), (1 - x, 1 - y)]

        def rows(px, py, pc):
            return out_ref.at[pl.ds((4 * px + 2 * py + pc) * m_per, m_per), :]

        def copy(k, block, to, src=None):
            return pltpu.make_async_remote_copy(
                src_ref=rows(*block) if src is None else src, dst_ref=rows(*block),
                send_sem=send_sems.at[k], recv_sem=recv_sems.at[k], device_id=to, device_id_type=MESH)

        mine = pltpu.make_async_copy(x_ref, rows(*me), local_sem)
        mine.start()
        first = [copy(0, me, sibling, src=x_ref)]
        first += [copy(1 + j, me, (*chip, c), src=x_ref) for j, chip in enumerate(chips)]
        for cp in first:
            cp.start()
        passed = [copy(4 + j, (*chip, c), sibling) for j, chip in enumerate(chips)]
        for j, chip in enumerate(chips):
            copy(1 + j, (*chip, c), me).wait_recv()
            passed[j].start()
        copy(0, sibling, me).wait_recv()
        for j, chip in enumerate(chips):
            copy(4 + j, (*chip, 1 - c), me).wait_recv()
        for cp in first + passed:
            cp.wait_send()
        mine.wait()

    return _pcall(body, name=name, out_shape=jax.ShapeDtypeStruct((N_DEV * m_per, n), blk.dtype),
                  in_specs=[pl.BlockSpec(memory_space=pltpu.VMEM)] + [ANY_SPEC] * len(deps),
                  out_specs=pl.BlockSpec(memory_space=pltpu.VMEM),
                  scratch_shapes=[pltpu.SemaphoreType.DMA((7,)), pltpu.SemaphoreType.DMA((7,)), pltpu.SemaphoreType.DMA],
                  compiler_params=pltpu.CompilerParams(vmem_limit_bytes=VMEM_LIMIT))(blk, *deps)


def _other_chips(x, y):
    return [(1 - x, y), (x, 1 - y), (1 - x, 1 - y)]


def _place_shards(shards, s_me):
    return [lax.dynamic_update_slice(lax.empty((N_CHIPS,) + s.shape, s.dtype), s[None], (s_me, 0, 0)) for s in shards]


def _ag_copy(src, land, send, recv, wi, j, chip, x, y, c):
    hr = src.shape[0] // 2
    half = pl.ds(pl.multiple_of(c * hr, 16), hr)
    return pltpu.make_async_remote_copy(
        src_ref=src.at[half, :], dst_ref=land.at[2 * x + y, half, :], send_sem=send.at[3 * wi + j], recv_sem=recv.at[3 * wi + j],
        device_id=(chip[0], chip[1], c), device_id_type=MESH)


def _ag_start(name, shards, lands, groups, deps=()):
    nw, ng, nd = len(shards), len(groups), len(deps)

    def body(*refs):
        src, land = refs[:nw], refs[nw:2 * nw]
        sems = refs[2 * nw + nd:2 * nw + nd + 2 * ng]
        token = refs[-1]
        x, y, c = _place()
        for gi, grp in enumerate(groups):
            for wi, w in enumerate(grp):
                for j, chip in enumerate(_other_chips(x, y)):
                    _ag_copy(src[w], land[w], sems[2 * gi], sems[2 * gi + 1], wi, j, chip, x, y, c).start()
        token[...] = jnp.zeros_like(token)

    sem_shapes = []
    for grp in groups:
        sem_shapes += [pltpu.SemaphoreType.DMA((3 * len(grp),))] * 2
    out_shape = sem_shapes + [pltpu.HBM(s.shape, s.dtype) for s in shards] + [pltpu.HBM(l.shape, l.dtype) for l in lands]
    out_shape += [jax.ShapeDtypeStruct((8, 128), F32)]
    res = _pcall(body, name=name, out_shape=out_shape, in_specs=[HBM_SPEC] * (2 * nw) + [ANY_SPEC] * nd,
                 out_specs=[SEM_SPEC] * (2 * ng) + [HBM_SPEC] * (2 * nw) + [pl.BlockSpec(memory_space=pltpu.VMEM)],
                 input_output_aliases={i: 2 * ng + i for i in range(2 * nw)},
                 compiler_params=pltpu.CompilerParams(has_side_effects=EFFECT))(
                     *[_hbm(s) for s in shards], *[_hbm(l) for l in lands], *deps)
    sems = [(res[2 * gi], res[2 * gi + 1]) for gi in range(ng)]
    return sems, list(res[2 * ng:2 * ng + nw]), list(res[2 * ng + nw:2 * ng + 2 * nw]), res[-1]


def _ag_wait(name, shards, lands, send, recv, after):
    n = len(shards)

    def body(*refs):
        src, land = refs[:n], refs[n:2 * n]
        send_sem, recv_sem = refs[2 * n], refs[2 * n + 1]
        x, y, c = _place()
        for wi in range(n):
            for j, chip in enumerate(_other_chips(x, y)):
                _ag_copy(src[wi], land[wi], send_sem, recv_sem, wi, j, chip, x, y, c).wait_send()
                _ag_copy(src[wi], land[wi], send_sem, recv_sem, wi, j, chip, chip[0], chip[1], c).wait_recv()

    res = _pcall(body, name=name, out_shape=[pltpu.HBM(a.shape, a.dtype) for a in list(shards) + list(lands)],
                 in_specs=[HBM_SPEC] * (2 * n) + [SEM_SPEC, SEM_SPEC, ANY_SPEC], out_specs=[HBM_SPEC] * (2 * n),
                 input_output_aliases={i: i for i in range(2 * n)},
                 compiler_params=pltpu.CompilerParams(has_side_effects=EFFECT))(*shards, *lands, send, recv, after)
    return list(res[n:])


def _ag_forward(name, lands):
    n = len(lands)

    def body(*refs):
        out = refs[n:2 * n]
        send, recv = refs[2 * n], refs[2 * n + 1]
        x, y, c = _place()
        sib = (x, y, 1 - c)
        cps = []
        for wi in range(n):
            hr = out[wi].shape[1] // 2
            for j, (cx, cy) in enumerate(_other_chips(x, y)):
                got = out[wi].at[2 * cx + cy, pl.ds(pl.multiple_of(c * hr, 16), hr), :]
                cp = pltpu.make_async_remote_copy(src_ref=got, dst_ref=got, send_sem=send.at[3 * wi + j], recv_sem=recv.at[3 * wi + j],
                                                  device_id=sib, device_id_type=MESH)
                cp.start()
                cps.append(cp)
        for wi in range(n):
            hr = out[wi].shape[1] // 2
            for j, (cx, cy) in enumerate(_other_chips(x, y)):
                got = out[wi].at[2 * cx + cy, pl.ds(pl.multiple_of((1 - c) * hr, 16), hr), :]
                pltpu.make_async_remote_copy(src_ref=got, dst_ref=got, send_sem=send.at[3 * wi + j], recv_sem=recv.at[3 * wi + j],
                                             device_id=sib, device_id_type=MESH).wait_recv()
        for cp in cps:
            cp.wait_send()

    res = _pcall(body, name=name, out_shape=[jax.ShapeDtypeStruct(l.shape, l.dtype) for l in lands],
                 in_specs=[ANY_SPEC] * n, out_specs=[ANY_SPEC] * n, input_output_aliases={i: i for i in range(n)},
                 scratch_shapes=[pltpu.SemaphoreType.DMA((3 * n,)), pltpu.SemaphoreType.DMA((3 * n,))])(*lands)
    return list(res)


def _peers(x, y, c):
    offs = [(dx, dy, dc) for dx in (0, 1) for dy in (0, 1) for dc in (0, 1)][1:]
    return [(1 - x if dx else x, 1 - y if dy else y, 1 - c if dc else c) for dx, dy, dc in offs]


def _rs_copy(g_ref, land_ref, send, recv, wi, k, to, sender):
    hr = g_ref.shape[1] // 2
    return pltpu.make_async_remote_copy(
        src_ref=g_ref.at[2 * to[0] + to[1], pl.ds(pl.multiple_of(to[2] * hr, 16), hr), :], dst_ref=land_ref.at[sender],
        send_sem=send.at[7 * wi + k], recv_sem=recv.at[7 * wi + k], device_id=to, device_id_type=MESH)


def _rs_start(name, gs):
    n = len(gs)
    lands = [lax.empty((N_DEV, g.shape[1] // 2, g.shape[2]), BF16) for g in gs]

    def body(*refs):
        g, land = refs[:n], refs[n:2 * n]
        send, recv = refs[2 * n], refs[2 * n + 1]
        token = refs[-1]
        x, y, c = _place()
        me = 4 * x + 2 * y + c
        for wi in range(n):
            for k, to in enumerate(_peers(x, y, c)):
                _rs_copy(g[wi], land[wi], send, recv, wi, k, to, me).start()
        token[...] = jnp.zeros_like(token)

    out_shape = [pltpu.SemaphoreType.DMA((7 * n,))] * 2 + [pltpu.HBM(a.shape, a.dtype) for a in list(gs) + lands]
    out_shape += [jax.ShapeDtypeStruct((8, 128), F32)]
    res = _pcall(body, name=name, out_shape=out_shape, in_specs=[HBM_SPEC] * (2 * n),
                 out_specs=[SEM_SPEC] * 2 + [HBM_SPEC] * (2 * n) + [pl.BlockSpec(memory_space=pltpu.VMEM)],
                 input_output_aliases={i: 2 + i for i in range(2 * n)},
                 compiler_params=pltpu.CompilerParams(has_side_effects=EFFECT))(
                     *[_hbm(a) for a in gs], *[_hbm(a) for a in lands])
    return res[0], res[1], list(res[2:2 + n]), list(res[2 + n:2 + 2 * n]), res[-1]


def _rs_wait(name, gs, lands, send, recv, after):
    n = len(gs)

    def body(*refs):
        g, land = refs[:n], refs[n:2 * n]
        send_sem, recv_sem = refs[2 * n], refs[2 * n + 1]
        x, y, c = _place()
        me = 4 * x + 2 * y + c
        for wi in range(n):
            for k, to in enumerate(_peers(x, y, c)):
                _rs_copy(g[wi], land[wi], send_sem, recv_sem, wi, k, to, me).wait_send()
                _rs_copy(g[wi], land[wi], send_sem, recv_sem, wi, k, (x, y, c), 4 * to[0] + 2 * to[1] + to[2]).wait_recv()

    res = _pcall(body, name=name, out_shape=[pltpu.HBM(a.shape, a.dtype) for a in list(gs) + list(lands)],
                 in_specs=[HBM_SPEC] * (2 * n) + [SEM_SPEC, SEM_SPEC, ANY_SPEC], out_specs=[HBM_SPEC] * (2 * n),
                 input_output_aliases={i: i for i in range(2 * n)},
                 compiler_params=pltpu.CompilerParams(has_side_effects=EFFECT))(*gs, *lands, send, recv, after)
    return list(res[:n]), list(res[n:])


def _bc_copy(blk_ref, land_ref, send, recv, k, to, slot):
    return pltpu.make_async_remote_copy(src_ref=blk_ref, dst_ref=land_ref.at[slot], send_sem=send.at[k], recv_sem=recv.at[k],
                                        device_id=to, device_id_type=MESH)


def _bcast_start(name, blk):
    land = lax.empty((N_DEV,) + blk.shape, blk.dtype)

    def body(blk_ref, land_ref, send, recv, blk_thru, land_thru, token):
        x, y, c = _place()
        for k, to in enumerate(_peers(x, y, c)):
            _bc_copy(blk_ref, land_ref, send, recv, k, to, 4 * x + 2 * y + c).start()
        token[...] = jnp.zeros_like(token)

    return _pcall(body, name=name,
                  out_shape=[pltpu.SemaphoreType.DMA((7,)), pltpu.SemaphoreType.DMA((7,)), pltpu.HBM(blk.shape, blk.dtype),
                             pltpu.HBM(land.shape, land.dtype), jax.ShapeDtypeStruct((8, 128), F32)],
                  in_specs=[HBM_SPEC, HBM_SPEC], out_specs=[SEM_SPEC, SEM_SPEC, HBM_SPEC, HBM_SPEC, pl.BlockSpec(memory_space=pltpu.VMEM)],
                  input_output_aliases={0: 2, 1: 3}, compiler_params=pltpu.CompilerParams(has_side_effects=EFFECT))(_hbm(blk), _hbm(land))


def _bcast_wait(name, blk, land, send, recv, after):
    def body(blk_ref, land_ref, send_sem, recv_sem, after_ref, blk_thru, land_thru):
        x, y, c = _place()
        for k, to in enumerate(_peers(x, y, c)):
            _bc_copy(blk_ref, land_ref, send_sem, recv_sem, k, to, 4 * x + 2 * y + c).wait_send()
            _bc_copy(blk_ref, land_ref, send_sem, recv_sem, k, to, 4 * to[0] + 2 * to[1] + to[2]).wait_recv()

    return _pcall(body, name=name, out_shape=[pltpu.HBM(blk.shape, blk.dtype), pltpu.HBM(land.shape, land.dtype)],
                  in_specs=[HBM_SPEC, HBM_SPEC, SEM_SPEC, SEM_SPEC, ANY_SPEC], out_specs=[HBM_SPEC, HBM_SPEC],
                  input_output_aliases={0: 0, 1: 1}, compiler_params=pltpu.CompilerParams(has_side_effects=EFFECT))(
                      blk, land, send, recv, after)[1]


def _rs_sum(name, g, land):
    _, rs, cs = g.shape
    hr = rs // 2
    ch = _pick(hr, 64, 16)

    def body(g_ref, land_ref, out_ref, recv, local_sems, sib_sems):
        x, y, c = _place()
        me = 4 * x + 2 * y + c
        cps = [pltpu.make_async_copy(g_ref.at[2 * x + y, pl.ds(pl.multiple_of(c * hr, 16), hr), :], recv.at[me], local_sems.at[7])]
        for k, (tx, ty, tc) in enumerate(_peers(x, y, c)):
            slot = 4 * tx + 2 * ty + tc
            cps.append(pltpu.make_async_copy(land_ref.at[slot], recv.at[slot], local_sems.at[k]))
        for cp in cps:
            cp.start()
        for cp in cps:
            cp.wait()
        base = pl.multiple_of(c * hr, 16)
        for r0 in range(0, hr, ch):
            acc = recv[0, r0:r0 + ch, :].astype(F32)
            for k in range(1, N_DEV):
                acc = acc + recv[k, r0:r0 + ch, :].astype(F32)
            out_ref[pl.ds(base + r0, ch), :] = acc
        half = out_ref.at[pl.ds(base, hr), :]
        sib = pltpu.make_async_remote_copy(src_ref=half, dst_ref=half, send_sem=sib_sems.at[0], recv_sem=sib_sems.at[1],
                                           device_id=(x, y, 1 - c), device_id_type=MESH)
        sib.start()
        other = out_ref.at[pl.ds(pl.multiple_of((1 - c) * hr, 16), hr), :]
        pltpu.make_async_remote_copy(src_ref=other, dst_ref=other, send_sem=sib_sems.at[0], recv_sem=sib_sems.at[1],
                                     device_id=(x, y, 1 - c), device_id_type=MESH).wait_recv()
        sib.wait_send()

    return _pcall(
        body, name=name, out_shape=jax.ShapeDtypeStruct((rs, cs), F32),
        in_specs=[ANY_SPEC, ANY_SPEC], out_specs=pl.BlockSpec(memory_space=pltpu.VMEM),
        scratch_shapes=[pltpu.VMEM((N_DEV, hr, cs), BF16), pltpu.SemaphoreType.DMA((8,)), pltpu.SemaphoreType.DMA((2,))],
        compiler_params=pltpu.CompilerParams(vmem_limit_bytes=VMEM_LIMIT))(g, land)


def _gate_norm_epilogue(factor, x_in, gt, nxt, t, d):
    blk = (512, d)
    full = lambda i, j: (i, j)
    rowv = lambda i, j: (0, j)

    def epi(acc, res, scale, *norm):
        x_new = res + (factor * scale) * acc
        if not norm:
            return x_new, acc
        gv, scv, shv = norm
        r = lax.rsqrt(jnp.mean(x_new * x_new, axis=-1, keepdims=True) + EPS)
        return x_new, acc, x_new * r * gv * (1.0 + scv) + shv

    ins = [(x_in, blk, full), (gt, (1, d), rowv)] + [(v, (1, d), rowv) for v in (nxt or ())]
    outs = [((t, d), F32, blk, full), ((t, d), BF16, blk, full)] + ([((t, d), BF16, blk, full)] if nxt else [])
    return epi, ins, outs


def _ffn_fwd(tag, x, h, gt, get_w_in, get_w_out, nxt):
    t, d = x.shape
    ab = _mm(tag + "_in", h, get_w_in(h), b_stacked=True, tm=1024, tn=1408, j_outer=True)
    act = _swiglu_fwd(tag + "_act", ab)
    epi, epi_ins, epi_outs = _gate_norm_epilogue(0.5, x, gt, nxt, t, d)
    res = _mm(tag + "_out", act, get_w_out(act), tm=512, tn=d, epi=epi, epi_ins=epi_ins, epi_outs=epi_outs)
    return res[0], (res[2] if nxt else None), (x, h, ab, act, res[1])


def _ffn_bwd(tag, dx_new, dy, saved, g, sc, w_in, w_out, start_rs, up, deps=()):
    x, h, ab, act, _ = saved
    d = x.shape[1]
    f = act.shape[1]
    dact = _mm(tag + "_dact", dy, w_out, tb=True, tn=2816, deps=deps)
    dw_out = _mm(tag + "_dwout", act, dy, ta=True, tm=1408, tn=d, tk=2048)
    tok = start_rs("out", dw_out.reshape(N_CHIPS, f // N_CHIPS, d))
    dab = _swiglu_bwd(tag + "_dswiglu", dact, ab, deps=(tok,))
    dw_in = _mm(tag + "_dwin", h, dab, ta=True, out_stacked=True, tm=d, tn=1408, tk=2048)
    tok = start_rs("in", dw_in)
    dh = _mm(tag + "_dh", dab, w_in, tb=True, b_stacked=True, tn=d, tk=5632, deps=(tok,))
    return _norm_bwd(tag + "_dnorm", dh, x, dx_new, g, sc, *(up or (None, None, None)))


def _row(v):
    return v.reshape(1, -1)


def _pack(parts):
    cols = []
    for p in parts:
        flat = p.reshape(-1).astype(F32)
        padn = (-flat.shape[0]) % 128
        cols.append(jnp.pad(flat, (0, padn)) if padn else flat)
    flat = jnp.concatenate(cols)
    padn = (-flat.shape[0]) % 1024
    if padn:
        flat = jnp.pad(flat, (0, padn))
    return flat.reshape(-1, 128)


def _unpack(packed, shapes):
    flat = packed.reshape(-1)
    out, off = [], 0
    for s in shapes:
        n = math.prod(s)
        out.append(flat[off:off + n].reshape(s))
        off += n + ((-n) % 128)
    return out


SMALL = ['b_ada', 'g_ffn1', 'g_mix', 'pool_w', 'pool_b', 'pool_scale', 'ssm_lam_re_log', 'ssm_lam_im', 'ssm_log_dt',
         'ssm_b_re', 'ssm_b_im', 'ssm_c_re', 'ssm_c_im', 'ssm_d', 'b_glu', 'g_ffn2', 'g_final']
BIG = ['w_ffn1_in', 'w_ffn1_out', 'w_in', 'w_pool_up', 'w_glu', 'w_ssm_up', 'w_out', 'w_ffn2_in', 'w_ffn2_out']
AG_GROUPS = [[0], [1], [2, 3, 4, 5, 6], [7, 8]]
SMALL_LATE = ['b_ada', 'g_ffn1']
WEIGHTS = ['w_ada', 'b_ada', 'g_ffn1', 'w_ffn1_in', 'w_ffn1_out', 'g_mix', 'w_in', 'pool_w', 'pool_b', 'pool_scale', 'w_pool_up',
           'ssm_lam_re_log', 'ssm_lam_im', 'ssm_log_dt', 'ssm_b_re', 'ssm_b_im', 'ssm_c_re', 'ssm_c_im', 'ssm_d', 'w_glu', 'b_glu',
           'w_ssm_up', 'w_out', 'g_ffn2', 'w_ffn2_in', 'w_ffn2_out', 'g_final']


def kernel(x, c, w_ada, b_ada, g_ffn1, w_ffn1_in, w_ffn1_out, g_mix, w_in, pool_w, pool_b, pool_scale, w_pool_up, ssm_lam_re_log, ssm_lam_im, ssm_log_dt, ssm_b_re, ssm_b_im, ssm_c_re, ssm_c_im, ssm_d, w_glu, b_glu, w_ssm_up, w_out, g_ffn2, w_ffn2_in, w_ffn2_out, g_final, loss_target, m_w_ada, m_b_ada, m_g_ffn1, m_w_ffn1_in, m_w_ffn1_out, m_g_mix, m_w_in, m_pool_w, m_pool_b, m_pool_scale, m_w_pool_up, m_ssm_lam_re_log, m_ssm_lam_im, m_ssm_log_dt, m_ssm_b_re, m_ssm_b_im, m_ssm_c_re, m_ssm_c_im, m_ssm_d, m_w_glu, m_b_glu, m_w_ssm_up, m_w_out, m_g_ffn2, m_w_ffn2_in, m_w_ffn2_out, m_g_final, v_w_ada, v_b_ada, v_g_ffn1, v_w_ffn1_in, v_w_ffn1_out, v_g_mix, v_w_in, v_pool_w, v_pool_b, v_pool_scale, v_w_pool_up, v_ssm_lam_re_log, v_ssm_lam_im, v_ssm_log_dt, v_ssm_b_re, v_ssm_b_im, v_ssm_c_re, v_ssm_c_im, v_ssm_d, v_w_glu, v_b_glu, v_w_ssm_up, v_w_out, v_g_ffn2, v_w_ffn2_in, v_w_ffn2_out, v_g_final):
    args = dict(locals())
    wt = {n: args[n] for n in WEIGHTS}
    mom = {n: args["m_" + n] for n in WEIGHTS}
    var = {n: args["v_" + n] for n in WEIGHTS}

    t, d = x.shape[1], x.shape[2]
    pw = pool_b.shape[1]
    sw = ssm_d.shape[1]
    ngrp = sw // SSM_GROUP
    gn = ngrp * SSM_STATE
    xi, yi, ci = _place()
    b_me = 4 * xi + 2 * yi + ci
    s_me = 2 * xi + yi
    x2d = x[0]
    tgt = loss_target[0]

    c_all = _allgather_small("ag_c", c.reshape(8, d // 8)).reshape(N_DEV, d)
    ncol = w_ada.shape[2]
    b_sh = lax.dynamic_slice(b_ada, (0, s_me * ncol), (1, ncol))
    mod_sh = _ada_fwd("ada_fwd", c_all, w_ada[0], b_sh)
    mod_all = _allgather_small("ag_mod", mod_sh)

    shards = [wt[n][0].astype(BF16) for n in BIG]
    ag_sems, shards_t, lands_t, tok = _ag_start("ag_start", shards, _place_shards(shards, s_me), AG_GROUPS, deps=(mod_all,))
    full = {}

    def weights(gi, after):
        grp = AG_GROUPS[gi]
        if BIG[grp[0]] not in full:
            ls = _ag_wait("ag_wait%d" % gi, [shards_t[w] for w in grp], [lands_t[w] for w in grp], *ag_sems[gi], after)
            for w, l in zip(grp, _ag_forward("ag_fwd%d" % gi, ls)):
                full[BIG[w]] = l
        return full

    mod_me = jnp.concatenate([lax.dynamic_slice(mod_all, (16 * s + b_me, 0), (1, ncol)) for s in range(N_CHIPS)], axis=1)
    mod = mod_me.reshape(9, d)
    sh1, sc1, gt1, sh2, sc2, gt2, sh3, sc3, gt3 = [mod[k:k + 1] for k in range(9)]

    f = w_ffn1_out.shape[1] * N_CHIPS

    h1 = _norm_fwd("ffn1_norm", x2d, g_ffn1, sc1, sh1, deps=(tok,))
    x1, h2, sav1 = _ffn_fwd("ffn1", x2d, h1, gt1, lambda after: weights(0, after)['w_ffn1_in'],
                            lambda after: weights(1, after)['w_ffn1_out'].reshape(f, d), (g_mix, sc2, sh2))
    weights(2, h2)
    wo = full['w_out'].reshape(d, d)
    uin = _mm("mix_in", h2, full['w_in'], b_stacked=True, tm=1024, tn=768, out_dtype=F32, j_outer=True)
    p_pool, z_pool = _pool_fwd("pool_fwd", uin, pool_w[0], pool_b, pool_scale, pw)
    y_pool = _mm("pool_up", p_pool, full['w_pool_up'], b_stacked=True, tm=1024, tn=d)

    col = lambda a: a.reshape(gn, 1)
    lrl_c, li_c = col(ssm_lam_re_log), col(ssm_lam_im)
    ldt_c = col(jnp.broadcast_to(ssm_log_dt.reshape(ngrp, 1), (ngrp, SSM_STATE)))
    b_re2, b_im2 = ssm_b_re.reshape(gn, SSM_GROUP), ssm_b_im.reshape(gn, SSM_GROUP)
    lrdt_c, ang_c, bb_re, bb_im = _ssm_prep("ssm_prep", lrl_c, li_c, ldt_c, b_re2, b_im2)
    lrdt, ang = lrdt_c.reshape(1, gn), ang_c.reshape(1, gn)
    bbd_re, bbd_im = _blockdiag_b(bb_re, sw), _blockdiag_b(bb_im, sw)
    ccd_re, ccd_im = _blockdiag_c(ssm_c_re[0], sw), _blockdiag_c(ssm_c_im[0], sw)
    y_s, ge, s_re, s_im = _ssm_fwd("ssm_fwd", uin, lrdt, ang, bbd_re.astype(BF16), bbd_im.astype(BF16),
                                   ccd_re.astype(BF16), ccd_im.astype(BF16), ssm_d, sw)
    gv = _mm("glu_in", ge, full['w_glu'], b_stacked=True, tm=1024, tn=d)
    sg = _glu_fwd("glu_act", gv, b_glu)
    y_ssm = _mm("ssm_up", sg, full['w_ssm_up'], b_stacked=True, tm=1024, tn=d)
    merged = _gates_fwd("gates", y_pool, y_ssm, uin, d, pw)
    epi, epi_ins, epi_outs = _gate_norm_epilogue(1.0, x1, gt2, (g_ffn2, sc3, sh3), t, d)
    x2, y2, h3 = _mm("mix_out", merged, wo, tm=512, tn=d, epi=epi, epi_ins=epi_ins, epi_outs=epi_outs)

    x3, _, sav3 = _ffn_fwd("ffn2", x2, h3, gt3, lambda after: weights(3, after)['w_ffn2_in'],
                           lambda after: weights(3, after)['w_ffn2_out'].reshape(f, d), None)

    dx3, dy3, dg_final, loss_v, dgt3 = _loss_bwd("loss", x3, tgt, _row(g_final), sav3[4], gt3, 0.5)

    gs = {}
    rs_open = []

    def start_rs(names, gbs):
        send, recv, g_thru, land_thru, token = _rs_start("rs_start_" + names[0], gbs)
        rs_open.append((names, send, recv, g_thru, land_thru))
        return token

    dx2, dy2, dsh3, dsc3, gs['g_ffn2'], dgt2 = _ffn_bwd(
        "ffn2b", dx3, dy3, sav3, g_ffn2, sc3, full['w_ffn2_in'], full['w_ffn2_out'].reshape(f, d),
        lambda key, g: start_rs(['w_ffn2_' + key], [g]), (y2, gt2, 1.0))

    dmerged = _mm("mix_dmerged", dy2, wo, tb=True, tn=d)
    g_wo = _mm("mix_dwout", merged, dy2, ta=True, tm=d, tn=d, tk=2048).reshape(N_CHIPS, d // N_CHIPS, d)
    dy_pool, dy_ssm, dgl = _gates_bwd("gates_bwd", dmerged, y_pool, y_ssm, uin, d, pw)

    dp = _mm("pool_dup", dy_pool, full['w_pool_up'], tb=True, b_stacked=True, tm=1024, tn=pw, tk=d)
    g_wpu = _mm("pool_dwup", p_pool, dy_pool, ta=True, out_stacked=True, tm=pw, tn=d, tk=4096)
    du_pool, gs['pool_w'], gs['pool_b'], gs['pool_scale'] = _pool_bwd("pool_bwd", dp, z_pool, pool_w[0], pool_b, pool_scale)

    dsg = _mm("ssm_dup", dy_ssm, full['w_ssm_up'], tb=True, b_stacked=True, tm=1024, tn=sw, tk=d)
    g_wsu = _mm("ssm_dwup", sg, dy_ssm, ta=True, out_stacked=True, tm=sw, tn=d, tk=4096)
    tok = start_rs(['w_out', 'w_pool_up', 'w_ssm_up'], [g_wo, g_wpu, g_wsu])
    dgv, gs['b_glu'] = _glu_bwd("glu_bwd", dsg, gv, b_glu, deps=(tok,))
    dge = _mm("glu_dge", dgv, full['w_glu'], tb=True, b_stacked=True, tm=1024, tn=sw, tk=d, out_dtype=F32)
    g_wglu = _mm("glu_dw", ge, dgv, ta=True, out_stacked=True, tm=sw, tn=d, tk=4096)
    tr = lambda a: jnp.swapaxes(a, 1, 2).astype(BF16)
    (du_ssm, g_abre, g_abim, g_bbd_re, g_bbd_im, g_ccd_re, g_ccd_im, gs['ssm_d']) = _ssm_bwd(
        "ssm_bwd", dge, y_s, uin, s_re, s_im, lrdt, ang, tr(bbd_re), tr(bbd_im), tr(ccd_re), tr(ccd_im), ssm_d, sw)
    gs['ssm_c_re'], gs['ssm_c_im'] = _diag_of_c(g_ccd_re, sw), _diag_of_c(g_ccd_im, sw)
    d_lrl, d_li, d_ldt, d_bre, d_bim = _ssm_param_bwd(
        "ssm_param_bwd", lrl_c, li_c, ldt_c, b_re2, b_im2, g_abre.reshape(gn, 1), g_abim.reshape(gn, 1),
        _diag_of_b(g_bbd_re, sw), _diag_of_b(g_bbd_im, sw))
    gs['ssm_lam_re_log'], gs['ssm_lam_im'] = d_lrl, d_li
    gs['ssm_log_dt'] = jnp.sum(d_ldt.reshape(ngrp, SSM_STATE), axis=1)
    gs['ssm_b_re'], gs['ssm_b_im'] = d_bre, d_bim

    duin = jnp.concatenate([du_pool, du_ssm, dgl], axis=1)
    g_win = _mm("mix_dwin", h2, duin, ta=True, out_stacked=True, tm=d, tn=768, tk=4096)
    tok = start_rs(['w_glu', 'w_in'], [g_wglu, g_win])
    dh2 = _mm("mix_dh", duin, full['w_in'], tb=True, b_stacked=True, tn=d, tk=3072, deps=(tok,))
    dx1, dy1, dsh2, dsc2, gs['g_mix'], dgt1 = _norm_bwd("mix_dnorm", dh2, x1, dx2, g_mix, sc2, sav1[4], gt1, 0.5)

    gs['g_final'] = dg_final
    early = [n for n in SMALL if n not in SMALL_LATE]
    sg_blk = _pack([gs[n] for n in early])
    sg_send, sg_recv, sg_blk, sg_land, tok = _bcast_start("sg_start", sg_blk)

    dx0, _, dsh1, dsc1, gs['g_ffn1'], _ = _ffn_bwd(
        "ffn1b", dx1, dy1, sav1, g_ffn1, sc1, full['w_ffn1_in'], full['w_ffn1_out'].reshape(f, d),
        lambda key, g: start_rs(['w_ffn1_' + key], [g]), None, deps=(tok,))

    gs['b_ada'] = jnp.concatenate([dsh1, dsc1, dgt1, dsh2, dsc2, dgt2, dsh3, dsc3, dgt3], axis=1)
    lt_blk = _pack([gs[n] for n in SMALL_LATE] + [loss_v])
    lt_send, lt_recv, lt_blk, lt_land, tok = _bcast_start("late_start", lt_blk)

    grads, delta, new_m, new_v = {}, {}, {}, {}

    def small_update(tag, names, blk, land, extra):
        g8 = lax.dynamic_update_slice(land, blk[None], (b_me, 0, 0)).reshape(-1, 128)
        res = _adamw_small("adamw_small_" + tag, _pack([wt[n] for n in names] + extra), g8,
                           _pack([mom[n] for n in names] + extra), _pack([var[n] for n in names] + extra))
        for dst, packed in zip((grads, delta, new_m, new_v), res):
            for n, val in zip(names, _unpack(packed, [wt[n].shape for n in names])):
                dst[n] = val
        return g8

    after = tok
    for k, (names, send, recv, g_thru, land_thru) in enumerate(rs_open):
        g_done, land_done = _rs_wait("rs_wait_" + names[0], g_thru, land_thru, send, recv, after)
        for n, g_own, land in zip(names, g_done, land_done):
            g_sum = _rs_sum("rs_sum_" + n, g_own, land)
            dl, mn, vn = _adamw("adamw_" + n, wt[n][0], g_sum, mom[n][0], var[n][0])
            grads[n], delta[n], new_m[n], new_v[n] = g_sum[None], dl[None], mn[None], vn[None]
            after = dl
        if k == len(rs_open) - 3:
            small_update("early", early, sg_blk, _bcast_wait("sg_wait", sg_blk, sg_land, sg_send, sg_recv, after), [])

    lt_land = _bcast_wait("late_wait", lt_blk, lt_land, lt_send, lt_recv, after)
    late8 = small_update("late", SMALL_LATE, lt_blk, lt_land, [jnp.zeros((128,), F32)]).reshape(N_DEV, -1)
    loss = jnp.sum(late8[:, 10 * d])
    dmod_sh = lax.dynamic_slice(late8, (0, s_me * ncol), (N_DEV, ncol))
    g_w_ada = _ada_bwd("ada_bwd", c_all, dmod_sh)
    dl, mn, vn = _adamw("adamw_w_ada", w_ada[0], g_w_ada, m_w_ada[0], v_w_ada[0])
    grads['w_ada'], delta['w_ada'], new_m['w_ada'], new_v['w_ada'] = g_w_ada[None], dl[None], mn[None], vn[None]

    return (loss, dx0[None], *[grads[n] for n in WEIGHTS], *[delta[n] for n in WEIGHTS],
            *[new_m[n] for n in WEIGHTS], *[new_v[n] for n in WEIGHTS])
```

```python
import functools
import math

import jax
import jax.numpy as jnp
from jax import lax
from jax.experimental import pallas as pl
from jax.experimental.pallas import tpu as pltpu

F32 = jnp.float32
BF16 = jnp.bfloat16
MESH = pl.DeviceIdType.MESH

EPS = 1e-6
POOL_WINDOWS = (2, 4, 8, 16)
POOL_HALO = 16
SSM_GROUP = 16
SSM_STATE = 64
SSM_BLOCKS = 4
N_DEV = 8
N_CHIPS = 4
ADAM_LR = 0.001
ADAM_B1 = 0.9
ADAM_B2 = 0.999
ADAM_EPS = 1e-08
ADAM_WD = 0.01
ADAM_STEP = 10
VMEM_LIMIT = 56 * 1024 * 1024


ANY_SPEC = pl.BlockSpec(memory_space=pl.ANY)
HBM_SPEC = pl.BlockSpec(memory_space=pltpu.HBM)
SEM_SPEC = pl.BlockSpec(memory_space=pltpu.SEMAPHORE)
EFFECT = pltpu.SideEffectType.DATAFLOW_SIDE_EFFECTING


def _hbm(a):
    return pltpu.with_memory_space_constraint(a, pltpu.HBM)


def _pcall(body, **kw):
    return pl.pallas_call(body, **kw)


def _params(*sem):
    return pltpu.CompilerParams(dimension_semantics=sem, vmem_limit_bytes=VMEM_LIMIT)


def _pick(n, cap, mult=128):
    if n <= cap:
        return n
    best = None
    for d in range(mult, cap + 1, mult):
        if n % d == 0:
            best = d
    assert best is not None, (n, cap, mult)
    return best


def _sigmoid(v):
    return 1.0 / (1.0 + jnp.exp(-v))


def _rowwise(name, fn, ins, params, outs, reds, tm, deps=()):
    t = ins[0][0].shape[0]
    tm = min(tm, t)
    nb = t // tm
    ni, npar, no, nd = len(ins), len(params), len(outs), len(deps)

    def body(*refs):
        iv = [r[...] for r in refs[:ni]]
        pv = [r[...] for r in refs[ni:ni + npar]]
        o_refs = refs[ni + npar + nd:ni + npar + nd + no]
        r_refs = refs[ni + npar + nd + no:]
        ovals, rvals = fn(iv, pv)
        for o_ref, val in zip(o_refs, ovals):
            off = 0
            if isinstance(val, tuple) and val[0] == "at":
                _, off, val = val
            parts = val if isinstance(val, (list, tuple)) else [val]
            for p in parts:
                o_ref[:, off:off + p.shape[1]] = p.astype(o_ref.dtype)
                off += p.shape[1]
        if r_refs:
            @pl.when(pl.program_id(0) == 0)
            def _():
                for r in r_refs:
                    r[...] = jnp.zeros_like(r)
            for r, val in zip(r_refs, rvals):
                r[...] += val

    in_specs = [pl.BlockSpec((tm, w), functools.partial(lambda i, cb: (i, cb), cb=cb)) for (_, w, cb) in ins]
    in_specs += [pl.BlockSpec(p.shape, lambda i: (0, 0)) for p in params]
    in_specs += [ANY_SPEC] * nd
    out_shape = [jax.ShapeDtypeStruct((t, w), dt) for (w, dt) in outs]
    out_shape += [jax.ShapeDtypeStruct((1, w), F32) for w in reds]
    out_specs = [pl.BlockSpec((tm, w), lambda i: (i, 0)) for (w, _) in outs]
    out_specs += [pl.BlockSpec((1, w), lambda i: (0, 0)) for w in reds]
    res = _pcall(body, name=name, grid=(nb,), in_specs=in_specs, out_specs=out_specs, out_shape=out_shape,
                 compiler_params=_params("arbitrary"))(*[a for a, _, _ in ins], *params, *deps)
    return res


def _colsum(v):
    return jnp.sum(v, axis=0, keepdims=True)


def _mm(name, a, b, *, ta=False, tb=False, b_stacked=False, out_stacked=False, tm=512, tn=1024, tk=2816,
        out_dtype=BF16, epi=None, epi_ins=(), epi_outs=None, deps=(), j_outer=False, a_halves=False, b_halves=False):
    if a_halves:
        _, m, kdim = a.shape
        kdim *= 2
    elif ta:
        kdim, m = a.shape
    else:
        m, kdim = a.shape
    ns = None
    if b_stacked:
        ns = b.shape[2]
        n = b.shape[1] if tb else N_CHIPS * ns
        assert kdim == (N_CHIPS * ns if tb else b.shape[1]), (name, a.shape, b.shape)
    else:
        if b_halves:
            n = 2 * b.shape[2]
            assert kdim == b.shape[1] and not tb, (name, a.shape, b.shape)
        else:
            n = b.shape[0] if tb else b.shape[1]
            assert kdim == (b.shape[1] if tb else b.shape[0]), (name, a.shape, b.shape)
        if out_stacked:
            ns = n // N_CHIPS

    def shards(want):
        return max(g for g in (1, 2, 4) if g * ns <= max(want, ns))

    tm = _pick(m, tm, 128 if ta else 8)
    gn = gk = 1
    if (b_stacked and not tb) or out_stacked:
        gn = shards(min(tn, n // 2) if b_halves else tn)
        tn = gn * ns
    else:
        tn = _pick(n, tn)
    if b_stacked and tb:
        gk = shards(tk)
        tk = gk * ns
    else:
        tk = _pick(kdim, tk, 8 if ta else 128)
    nm, nn, nk = m // tm, n // tn, kdim // tk

    def ij(f):
        return (lambda g0, g1, k: f(g1, g0, k)) if j_outer else f

    if a_halves:
        assert b_stacked and tb and gk == N_CHIPS and nk == 1, name
        a_spec = pl.BlockSpec((2, tm, kdim // 2), ij(lambda i, j, k: (0, i, 0)))
    elif ta:
        a_spec = pl.BlockSpec((tk, tm), ij(lambda i, j, k: (k, i)))
    else:
        a_spec = pl.BlockSpec((tm, tk), ij(lambda i, j, k: (i, k)))
    if b_stacked and not tb:
        b_spec = pl.BlockSpec((gn, tk, ns), ij(lambda i, j, k: (j, k, 0)))
    elif b_stacked and tb:
        b_spec = pl.BlockSpec((gk, tn, ns), ij(lambda i, j, k: (k, j, 0)))
    elif b_halves:
        bph = (n // 2) // tn
        b_spec = pl.BlockSpec((None, tk, tn), ij(lambda i, j, k: (j // bph, k, j % bph)))
    elif tb:
        b_spec = pl.BlockSpec((tn, tk), ij(lambda i, j, k: (j, k)))
    else:
        b_spec = pl.BlockSpec((tk, tn), ij(lambda i, j, k: (k, j)))
    dims = (((0 if ta else 1,), (1 if tb else 0,)), ((), ()))

    if epi_outs is None:
        if out_stacked:
            epi_outs = [((N_CHIPS, m, ns), out_dtype, (gn, tm, ns), lambda i, j: (j, i, 0))]
        else:
            epi_outs = [((m, n), out_dtype, (tm, tn), lambda i, j: (i, j))]
    ne, no, nd = len(epi_ins), len(epi_outs), len(deps)

    def body(a_ref, b_ref, *rest):
        e_refs = rest[:ne]
        o_refs = rest[ne + nd:ne + nd + no]
        scratch = rest[ne + nd + no:]
        av = None if a_halves else a_ref[...].astype(BF16)
        if b_stacked and not tb:
            parts = [lax.dot_general(av, b_ref[s].astype(BF16), dims, preferred_element_type=F32) for s in range(gn)]
        elif b_stacked and tb:
            p = None
            for s in range(gk):
                if a_halves:
                    a_s = a_ref[s // 2, :, (s % 2) * ns:(s % 2 + 1) * ns].astype(BF16)
                else:
                    a_s = av[:, s * ns:(s + 1) * ns]
                q = lax.dot_general(a_s, b_ref[s].astype(BF16), dims, preferred_element_type=F32)
                p = q if p is None else p + q
            parts = [p]
        else:
            parts = [lax.dot_general(av, b_ref[...].astype(BF16), dims, preferred_element_type=F32)]

        def finish(acc_parts):
            if epi is None and out_stacked:
                acc = acc_parts[0]
                for s in range(gn):
                    o_refs[0][s] = acc[:, s * ns:(s + 1) * ns].astype(out_dtype)
            elif epi is None:
                w = acc_parts[0].shape[1]
                for s, part in enumerate(acc_parts):
                    o_refs[0][:, s * w:(s + 1) * w] = part.astype(out_dtype)
            else:
                acc = acc_parts[0] if len(acc_parts) == 1 else jnp.concatenate(acc_parts, axis=1)
                for o_ref, val in zip(o_refs, epi(acc, *[r[...] for r in e_refs])):
                    o_ref[...] = val.astype(o_ref.dtype)

        if nk == 1:
            finish(parts)
        else:
            acc_ref = scratch[0]
            k = pl.program_id(2)
            w = parts[0].shape[1]

            @pl.when(k == 0)
            def _():
                for s, part in enumerate(parts):
                    acc_ref[:, s * w:(s + 1) * w] = part

            @pl.when(k > 0)
            def _():
                for s, part in enumerate(parts):
                    acc_ref[:, s * w:(s + 1) * w] += part

            @pl.when(k == nk - 1)
            def _():
                finish([acc_ref[...]])

    def _ij(f):
        return ij(lambda i, j, k: f(i, j))

    in_specs = [a_spec, b_spec] + [pl.BlockSpec(blk, _ij(f)) for (_, blk, f) in epi_ins] + [ANY_SPEC] * nd
    out_specs = [pl.BlockSpec(blk, _ij(f)) for (_, _, blk, f) in epi_outs]
    out_shape = [jax.ShapeDtypeStruct(s, dt) for (s, dt, _, _) in epi_outs]
    scratch = [pltpu.VMEM((tm, tn), F32)] if nk > 1 else []
    grid = (nn, nm, nk) if j_outer else (nm, nn, nk)
    res = _pcall(body, name=name, grid=grid, in_specs=in_specs, out_specs=out_specs, out_shape=out_shape,
                 scratch_shapes=scratch, compiler_params=_params("parallel", "parallel", "arbitrary"))(
                     a, b, *[x for x, _, _ in epi_ins], *deps)
    return res[0] if len(res) == 1 else res


def _norm_fwd(name, x, g, sc, sh, deps=()):
    d = x.shape[1]

    def fn(iv, pv):
        (xv,), (gv, scv, shv) = iv, pv
        r = lax.rsqrt(jnp.mean(xv * xv, axis=-1, keepdims=True) + EPS)
        return [xv * r * gv * (1.0 + scv) + shv], []

    return _rowwise(name, fn, [(x, d, 0)], [g, sc, sh], [(d, BF16)], [], 512, deps=deps)[0]


def _norm_bwd(name, dh, x, dres, g, sc, y_up, gt_up, factor_up):
    d = x.shape[1]
    up = y_up is not None

    def fn(iv, pv):
        dhv, xv, drv = iv[:3]
        gv, scv = pv[:2]
        dhv = dhv.astype(F32)
        r = lax.rsqrt(jnp.mean(xv * xv, axis=-1, keepdims=True) + EPS)
        xr = xv * r
        dn = dhv * (1.0 + scv)
        gd = gv * dn
        dx = drv + r * (gd - xr * jnp.mean(gd * xr, axis=-1, keepdims=True))
        outs, reds = [dx], [_colsum(dhv), _colsum(dhv * xr * gv), _colsum(dn * xr)]
        if up:
            fdx = factor_up * dx
            outs.append(pv[2] * fdx)
            reds.append(_colsum(fdx * iv[3].astype(F32)))
        return outs, reds

    ins = [(dh, d, 0), (x, d, 0), (dres, d, 0)] + ([(y_up, d, 0)] if up else [])
    res = _rowwise(name, fn, ins, [g, sc] + ([gt_up] if up else []), [(d, F32)] + ([(d, BF16)] if up else []),
                   [d] * (4 if up else 3), 256)
    if up:
        dx, dy_up, dsh, dsc, dg, dgt_up = res
        return dx, dy_up, dsh, dsc, dg, dgt_up
    dx, dsh, dsc, dg = res
    return dx, None, dsh, dsc, dg, None


def _loss_bwd(name, x, tgt, g, y_up, gt_up, factor_up):
    d = x.shape[1]

    def fn(iv, pv):
        (xv, tv, yv), (gv, gtv) = iv, pv
        r = lax.rsqrt(jnp.mean(xv * xv, axis=-1, keepdims=True) + EPS)
        xr = xv * r
        e = xr * gv - tv
        loss_row = 0.5 * jnp.mean(e * e, axis=-1, keepdims=True)
        dout = e * (1.0 / d)
        gd = gv * dout
        dx = r * (gd - xr * jnp.mean(gd * xr, axis=-1, keepdims=True))
        fdx = factor_up * dx
        return [dx, gtv * fdx], [_colsum(dout * xr), _colsum(loss_row * jnp.ones((1, 128), F32)), _colsum(fdx * yv.astype(F32))]

    return _rowwise(name, fn, [(x, d, 0), (tgt, d, 0), (y_up, d, 0)], [g, gt_up], [(d, F32), (d, BF16)], [d, 128, d], 256)


def _resgate_bwd(name, dx, y, gt, factor, deps=()):
    d = dx.shape[1]

    def fn(iv, pv):
        (dxv, yv), (gtv,) = iv, pv
        return [factor * gtv * dxv], [_colsum(factor * dxv * yv.astype(F32))]

    return _rowwise(name, fn, [(dx, d, 0), (y, d, 0)], [gt], [(d, BF16)], [d], 512, deps=deps)


def _swiglu_fwd(name, ab):
    f = ab.shape[1] // 2

    def fn(iv, pv):
        a, b = iv[0].astype(F32), iv[1].astype(F32)
        return [a * _sigmoid(a) * b], []

    return _rowwise(name, fn, [(ab, f, 0), (ab, f, 1)], [], [(f, BF16)], [], 256)[0]


def _swiglu_bwd(name, dact, ab, deps=()):
    f = ab.shape[1] // 2

    def fn(iv, pv):
        dv, a, b = iv[0].astype(F32), iv[1].astype(F32), iv[2].astype(F32)
        s = _sigmoid(a)
        return [[dv * b * (s * (1.0 + a * (1.0 - s))), dv * (a * s)]], []

    return _rowwise(name, fn, [(dact, f, 0), (ab, f, 0), (ab, f, 1)], [], [(2 * f, BF16)], [], 256, deps=deps)[0]


def _gates_fwd(name, y_pool, y_ssm, uin, d, pw):
    cb = (2 * pw) // d

    def fn(iv, pv):
        yp, ys, glp, gls = [v.astype(F32) for v in iv]
        return [_sigmoid(glp) * yp + _sigmoid(gls) * ys], []

    return _rowwise(name, fn, [(y_pool, d, 0), (y_ssm, d, 0), (uin, d, cb), (uin, d, cb + 1)], [], [(d, BF16)], [], 256)[0]


def _gates_bwd(name, dm, y_pool, y_ssm, uin, d, pw):
    cb = (2 * pw) // d

    def fn(iv, pv):
        dmv, yp, ys, glp, gls = [v.astype(F32) for v in iv]
        sp, ss = _sigmoid(glp), _sigmoid(gls)
        return [dmv * sp, dmv * ss, ("at", 2 * pw, [dmv * yp * sp * (1.0 - sp), dmv * ys * ss * (1.0 - ss)])], []

    return _rowwise(name, fn, [(dm, d, 0), (y_pool, d, 0), (y_ssm, d, 0), (uin, d, cb), (uin, d, cb + 1)], [],
                    [(d, BF16), (d, BF16), (2 * pw + 2 * d, BF16)], [], 256)


def _glu_fwd(name, gv, b_glu):
    w = gv.shape[1] // 2

    def fn(iv, pv):
        gvv, (bv,) = iv[0].astype(F32) + pv[0], pv
        return [gvv[:, :w] * _sigmoid(gvv[:, w:])], []

    return _rowwise(name, fn, [(gv, 2 * w, 0)], [b_glu], [(w, BF16)], [], 512)[0]


def _glu_bwd(name, dsg, gv, b_glu, deps=()):
    w = gv.shape[1] // 2

    def fn(iv, pv):
        dv = iv[0].astype(F32)
        gvv = iv[1].astype(F32) + pv[0]
        val, s = gvv[:, :w], _sigmoid(gvv[:, w:])
        dval = dv * s
        dgate = dv * val * s * (1.0 - s)
        return [[dval, dgate]], [jnp.concatenate([_colsum(dval), _colsum(dgate)], axis=1)]

    return _rowwise(name, fn, [(dsg, w, 0), (gv, 2 * w, 0)], [b_glu], [(2 * w, BF16)], [2 * w], 512, deps=deps)


def _adamw(name, w, g, m, v, tm=256):
    c = w.shape[1]

    def fn(iv, pv):
        wv, gv, mv, vv = iv
        mn = ADAM_B1 * mv + (1.0 - ADAM_B1) * gv
        vn = ADAM_B2 * vv + (1.0 - ADAM_B2) * (gv * gv)
        m_hat = mn / (1.0 - ADAM_B1 ** ADAM_STEP)
        v_hat = vn / (1.0 - ADAM_B2 ** ADAM_STEP)
        delta = -ADAM_LR * (m_hat / (jnp.sqrt(v_hat) + ADAM_EPS) + ADAM_WD * wv)
        return [delta, mn, vn], []

    return _rowwise(name, fn, [(w, c, 0), (g, c, 0), (m, c, 0), (v, c, 0)], [], [(c, F32)] * 3, [], _pick(w.shape[0], tm, 8))


def _adamw_small(name, w, g8, m, v):
    r = w.shape[0]

    def body(w_ref, g_ref, m_ref, v_ref, go_ref, d_ref, mo_ref, vo_ref):
        gv = g_ref[0:r, :]
        for k in range(1, N_DEV):
            gv = gv + g_ref[k * r:(k + 1) * r, :]
        mn = ADAM_B1 * m_ref[...] + (1.0 - ADAM_B1) * gv
        vn = ADAM_B2 * v_ref[...] + (1.0 - ADAM_B2) * (gv * gv)
        m_hat = mn / (1.0 - ADAM_B1 ** ADAM_STEP)
        v_hat = vn / (1.0 - ADAM_B2 ** ADAM_STEP)
        go_ref[...] = gv
        d_ref[...] = -ADAM_LR * (m_hat / (jnp.sqrt(v_hat) + ADAM_EPS) + ADAM_WD * w_ref[...])
        mo_ref[...] = mn
        vo_ref[...] = vn

    return _pcall(body, name=name, out_shape=[jax.ShapeDtypeStruct((r, 128), F32)] * 4,
                  compiler_params=pltpu.CompilerParams(vmem_limit_bytes=VMEM_LIMIT))(w, g8, m, v)


def _pool_fwd(name, uin, pool_w, pool_b, pool_scale, pw, tm=512):
    t = uin.shape[0]
    tm = min(tm, t)
    ng = len(POOL_WINDOWS)
    gw = pw // ng

    def body(u_ref, w_ref, b_ref, s_ref, p_ref, z_ref, ext):
        i = pl.program_id(0)

        @pl.when(i == 0)
        def _():
            ext[0:POOL_HALO, :] = jnp.zeros((POOL_HALO, pw), F32)

        u = u_ref[...]
        ext[POOL_HALO:POOL_HALO + tm, :] = u
        pos = i * tm + lax.broadcasted_iota(jnp.int32, (tm, 1), 0)
        for k, win in enumerate(POOL_WINDOWS):
            cols = slice(k * gw, (k + 1) * gw)
            acc = u[:, cols]
            for j in range(1, win):
                acc = acc + ext[POOL_HALO - j:POOL_HALO - j + tm, cols]
            cnt = jnp.minimum(pos + 1, win).astype(F32)
            z = acc / cnt - u[:, cols]
            zp = jnp.dot(z.astype(BF16), w_ref[k].astype(BF16), preferred_element_type=F32) + b_ref[:, cols]
            p_ref[:, cols] = (zp * s_ref[:, cols]).astype(BF16)
            z_ref[:, cols] = z.astype(BF16)
        ext[0:POOL_HALO, :] = u[tm - POOL_HALO:tm, :]

    return _pcall(
        body, name=name, grid=(t // tm,),
        in_specs=[pl.BlockSpec((tm, pw), lambda i: (i, 0)), pl.BlockSpec(pool_w.shape, lambda i: (0, 0, 0)),
                  pl.BlockSpec(pool_b.shape, lambda i: (0, 0)), pl.BlockSpec(pool_scale.shape, lambda i: (0, 0))],
        out_specs=[pl.BlockSpec((tm, pw), lambda i: (i, 0))] * 2,
        out_shape=[jax.ShapeDtypeStruct((t, pw), BF16)] * 2,
        scratch_shapes=[pltpu.VMEM((POOL_HALO + tm, pw), F32)],
        compiler_params=_params("arbitrary"))(uin, pool_w, pool_b, pool_scale)


def _pool_bwd(name, dp, z, pool_w, pool_b, pool_scale, duin, tm=512):
    t, pw = z.shape
    tm = min(tm, t)
    nb = t // tm
    ng = len(POOL_WINDOWS)
    gw = pw // ng

    def body(dp_ref, z_ref, w_ref, b_ref, s_ref, duin_ref, du_ref, dw_ref, db_ref, ds_ref, ext):
        i = pl.program_id(0)

        @pl.when(i == 0)
        def _():
            ext[tm:tm + POOL_HALO, :] = jnp.zeros((POOL_HALO, pw), F32)
            dw_ref[...] = jnp.zeros_like(dw_ref)
            db_ref[...] = jnp.zeros_like(db_ref)
            ds_ref[...] = jnp.zeros_like(ds_ref)

        pos = (nb - 1 - i) * tm + lax.broadcasted_iota(jnp.int32, (tm, 1), 0)
        for k, win in enumerate(POOL_WINDOWS):
            cols = slice(k * gw, (k + 1) * gw)
            zk = z_ref[:, cols]
            dpk = dp_ref[:, cols].astype(F32)
            wk = w_ref[k].astype(BF16)
            zp = jnp.dot(zk, wk, preferred_element_type=F32) + b_ref[:, cols]
            ds_ref[:, cols] += _colsum(dpk * zp)
            dzp = dpk * s_ref[:, cols]
            db_ref[:, cols] += _colsum(dzp)
            dzpb = dzp.astype(BF16)
            dz = lax.dot_general(dzpb, wk, (((1,), (1,)), ((), ())), preferred_element_type=F32)
            dw_ref[k] += lax.dot_general(zk, dzpb, (((0,), (0,)), ((), ())), preferred_element_type=F32)
            cnt = jnp.minimum(pos + 1, win).astype(F32)
            r = dz / cnt
            ext[0:tm, cols] = r
            acc = r - dz
            for j in range(1, win):
                acc = acc + ext[j:j + tm, cols]
            du_ref[:, cols] = acc.astype(BF16)
        ext[tm:tm + POOL_HALO, :] = ext[0:POOL_HALO, :]

    rev = lambda i: (nb - 1 - i, 0)
    return _pcall(
        body, name=name, grid=(nb,),
        in_specs=[pl.BlockSpec((tm, pw), rev), pl.BlockSpec((tm, pw), rev), pl.BlockSpec(pool_w.shape, lambda i: (0, 0, 0)),
                  pl.BlockSpec(pool_b.shape, lambda i: (0, 0)), pl.BlockSpec(pool_scale.shape, lambda i: (0, 0)), ANY_SPEC],
        out_specs=[pl.BlockSpec((tm, pw), rev), pl.BlockSpec(pool_w.shape, lambda i: (0, 0, 0)),
                   pl.BlockSpec((1, pw), lambda i: (0, 0)), pl.BlockSpec((1, pw), lambda i: (0, 0))],
        out_shape=[jax.ShapeDtypeStruct(duin.shape, BF16), jax.ShapeDtypeStruct(pool_w.shape, F32),
                   jax.ShapeDtypeStruct((1, pw), F32), jax.ShapeDtypeStruct((1, pw), F32)],
        scratch_shapes=[pltpu.VMEM((tm + POOL_HALO, pw), F32)], input_output_aliases={5: 0},
        compiler_params=_params("arbitrary"))(dp, z, pool_w, pool_b, pool_scale, duin)


def _ssm_disc(lrl, li, ldt):
    lr = -jnp.exp(lrl)
    dt = jnp.exp(ldt)
    mag = jnp.exp(lr * dt)
    ang = li * dt
    ab_re = mag * jnp.cos(ang)
    ab_im = mag * jnp.sin(ang)
    num_re = ab_re - 1.0
    num_im = ab_im
    den = lr * lr + li * li
    f_re = (num_re * lr + num_im * li) / den
    f_im = (num_im * lr - num_re * li) / den
    return lr, dt, mag, ang, ab_re, ab_im, num_re, num_im, den, f_re, f_im


def _ssm_prep(name, lrl, li, ldt, b_re, b_im):
    gn, h = b_re.shape

    def body(lrl_ref, li_ref, ldt_ref, br_ref, bi_ref, lrdt_ref, ang_ref, bbr_ref, bbi_ref):
        lr, dt, _, ang, _, _, _, _, _, f_re, f_im = _ssm_disc(lrl_ref[...], li_ref[...], ldt_ref[...])
        lrdt_ref[...] = lr * dt
        ang_ref[...] = ang
        br, bi = br_ref[...], bi_ref[...]
        bbr_ref[...] = f_re * br - f_im * bi
        bbi_ref[...] = f_re * bi + f_im * br

    col = jax.ShapeDtypeStruct((gn, 1), F32)
    mat = jax.ShapeDtypeStruct((gn, h), F32)
    return _pcall(body, name=name, out_shape=[col, col, mat, mat])(lrl, li, ldt, b_re, b_im)


def _ssm_param_bwd(name, lrl, li, ldt, b_re, b_im, g_abre, g_abim, g_bbre, g_bbim):
    gn, h = b_re.shape

    def body(lrl_ref, li_ref, ldt_ref, br_ref, bi_ref, gar_ref, gai_ref, gbr_ref, gbi_ref,
             dlrl_ref, dli_ref, dldt_ref, dbr_ref, dbi_ref):
        li_v = li_ref[...]
        lr, dt, mag, ang, ab_re, ab_im, num_re, num_im, den, f_re, f_im = _ssm_disc(lrl_ref[...], li_v, ldt_ref[...])
        br, bi = br_ref[...], bi_ref[...]
        gbr, gbi = gbr_ref[...], gbi_ref[...]
        g_fre = jnp.sum(gbr * br + gbi * bi, axis=1, keepdims=True)
        g_fim = jnp.sum(gbi * br - gbr * bi, axis=1, keepdims=True)
        dbr_ref[...] = gbr * f_re + gbi * f_im
        dbi_ref[...] = gbi * f_re - gbr * f_im
        g_num_re = (g_fre * lr - g_fim * li_v) / den
        g_num_im = (g_fre * li_v + g_fim * lr) / den
        g_den = -(g_fre * f_re + g_fim * f_im) / den
        g_lr = (g_fre * num_re + g_fim * num_im) / den + g_den * 2.0 * lr
        g_li = (g_fre * num_im - g_fim * num_re) / den + g_den * 2.0 * li_v
        g_are = gar_ref[...] + g_num_re
        g_aim = gai_ref[...] + g_num_im
        g_mag = g_are * jnp.cos(ang) + g_aim * jnp.sin(ang)
        g_ang = g_aim * ab_re - g_are * ab_im
        g_lrdt = g_mag * mag
        g_lr = g_lr + g_lrdt * dt
        g_dt = g_lrdt * lr + g_ang * li_v
        g_li = g_li + g_ang * dt
        dlrl_ref[...] = g_lr * lr
        dli_ref[...] = g_li
        dldt_ref[...] = g_dt * dt

    col = jax.ShapeDtypeStruct((gn, 1), F32)
    mat = jax.ShapeDtypeStruct((gn, h), F32)
    return _pcall(body, name=name, out_shape=[col, col, col, mat, mat])(lrl, li, ldt, b_re, b_im, g_abre, g_abim, g_bbre, g_bbim)


def _pow_rows(lrdt, ang, k):
    mag = jnp.exp(k * lrdt)
    return mag * jnp.cos(k * ang), mag * jnp.sin(k * ang)


def _ssm_chunk(t):
    return 256 if t >= 2048 else 128


def _to_segments(dst, srcs, jn):
    for q, src in enumerate(srcs):
        for j in range(jn):
            dst[8 * j:8 * j + 8, 128 * q:128 * (q + 1)] = src[pl.ds(j, 8, stride=jn), :]


def _from_segments(dst, src, q, jn, dtype):
    for s in range(8):
        dst[s * jn:(s + 1) * jn, 128 * q:128 * (q + 1)] = src[q, pl.ds(s, jn, stride=8), :].astype(dtype)


def _fill_rows8(dst_re, dst_im, v_re, v_im):
    for j in range(v_re.shape[0]):
        dst_re[8 * j:8 * j + 8, :] = jnp.broadcast_to(v_re[j:j + 1, :], (8, v_re.shape[1]))
        dst_im[8 * j:8 * j + 8, :] = jnp.broadcast_to(v_im[j:j + 1, :], (8, v_im.shape[1]))


def _cmul(ar, ai, br, bi):
    return ar * br - ai * bi, ar * bi + ai * br


def _cmul_conj(ar, ai, br, bi):
    return ar * br + ai * bi, ar * bi - ai * br


def _ssm_fwd(name, uin, lrdt, ang, bb_re, bb_im, cc_re, cc_im, d_skip, sw):
    t = uin.shape[0]
    gn = lrdt.shape[1]
    lc = _ssm_chunk(t)
    jn = lc // 8
    ub, sb = sw // SSM_BLOCKS, gn // SSM_BLOCKS
    nq = sw // 128
    assert ub == 128 and nq == SSM_BLOCKS

    def body(*refs):
        u_refs = refs[:nq]
        (lrdt_ref, ang_ref, bbr_ref, bbi_ref, ccr_ref, cci_ref, d_ref, y_ref, ge_ref, sre_ref, sim_ref,
         p_re, p_im, a_re, a_im, up, yp, gp, cst_re, cst_im, car_re, car_im) = refs[nq:]
        i = pl.program_id(0)
        lrdt_v, ang_v = lrdt_ref[...], ang_ref[...]

        @pl.when(i == 0)
        def _():
            k = (lax.broadcasted_iota(jnp.int32, (jn, 1), 0) + 1).astype(F32)
            _fill_rows8(p_re, p_im, *_pow_rows(lrdt_v, ang_v, k))
            car_re[...] = jnp.zeros_like(car_re)
            car_im[...] = jnp.zeros_like(car_im)

        _to_segments(up, u_refs, jn)
        u = up[...]
        ubf = u.astype(BF16)
        for q in range(SSM_BLOCKS):
            uq = ubf[:, q * ub:(q + 1) * ub]
            a_re[:, q * sb:(q + 1) * sb] = jnp.dot(uq, bbr_ref[q], preferred_element_type=F32)
            a_im[:, q * sb:(q + 1) * sb] = jnp.dot(uq, bbi_ref[q], preferred_element_type=F32)
        a1r, a1i = _pow_rows(lrdt_v, ang_v, 1.0)
        ajr, aji = _pow_rows(lrdt_v, ang_v, float(jn))
        for q in range(SSM_BLOCKS):
            cols = slice(q * sb, (q + 1) * sb)
            ar8 = jnp.broadcast_to(a1r[:, cols], (8, sb))
            ai8 = jnp.broadcast_to(a1i[:, cols], (8, sb))

            def step(j, carry, cols=cols, ar8=ar8, ai8=ai8):
                sr, si = carry
                rows = pl.ds(pl.multiple_of(j * 8, 8), 8)
                mr, mi = _cmul(ar8, ai8, sr, si)
                nr, ni = mr + a_re[rows, cols], mi + a_im[rows, cols]
                a_re[rows, cols] = nr
                a_im[rows, cols] = ni
                return nr, ni

            lax.fori_loop(1, jn, step, (a_re[0:8, cols], a_im[0:8, cols]), unroll=4)
        er, ei = a_re[lc - 8:lc, :], a_im[lc - 8:lc, :]
        gr, gi = car_re[...], car_im[...]
        for s in range(8):
            cst_re[s:s + 1, :] = gr
            cst_im[s:s + 1, :] = gi
            mr, mi = _cmul(ajr, aji, gr, gi)
            gr, gi = mr + er[s:s + 1, :], mi + ei[s:s + 1, :]
        car_re[...] = gr
        car_im[...] = gi
        for q in range(SSM_BLOCKS):
            cols = slice(q * sb, (q + 1) * sb)
            cr = jnp.tile(cst_re[:, cols], (jn, 1))
            ci = jnp.tile(cst_im[:, cols], (jn, 1))
            mr, mi = _cmul(p_re[:, cols], p_im[:, cols], cr, ci)
            sr, si = a_re[:, cols] + mr, a_im[:, cols] + mi
            sre_ref[:, cols] = sr
            sim_ref[:, cols] = si
            ycols = slice(q * ub, (q + 1) * ub)
            y = (jnp.dot(sr.astype(BF16), ccr_ref[q], preferred_element_type=F32)
                 - jnp.dot(si.astype(BF16), cci_ref[q], preferred_element_type=F32)
                 + d_ref[:, ycols] * u[:, ycols])
            yp[q] = y
            gp[q] = 0.5 * y * (1.0 + lax.erf(y * (1.0 / math.sqrt(2.0))))
            _from_segments(y_ref, yp, q, jn, F32)
            _from_segments(ge_ref, gp, q, jn, BF16)

    row = lambda i: (0, 0)
    blk3 = lambda i: (0, 0, 0)
    return _pcall(
        body, name=name, grid=(t // lc,),
        in_specs=[pl.BlockSpec((lc, 128), functools.partial(lambda i, q: (i, nq + q), q=q)) for q in range(nq)]
        + [pl.BlockSpec((1, gn), row), pl.BlockSpec((1, gn), row),
           pl.BlockSpec(bb_re.shape, blk3), pl.BlockSpec(bb_im.shape, blk3),
           pl.BlockSpec(cc_re.shape, blk3), pl.BlockSpec(cc_im.shape, blk3), pl.BlockSpec((1, sw), row)],
        out_specs=[pl.BlockSpec((lc, sw), lambda i: (i, 0)), pl.BlockSpec((lc, sw), lambda i: (i, 0)),
                   pl.BlockSpec((lc, gn), lambda i: (i, 0)), pl.BlockSpec((lc, gn), lambda i: (i, 0))],
        out_shape=[jax.ShapeDtypeStruct((t, sw), F32), jax.ShapeDtypeStruct((t, sw), BF16),
                   jax.ShapeDtypeStruct((t, gn), F32), jax.ShapeDtypeStruct((t, gn), F32)],
        scratch_shapes=[pltpu.VMEM((lc, gn), F32), pltpu.VMEM((lc, gn), F32), pltpu.VMEM((lc, gn), F32), pltpu.VMEM((lc, gn), F32),
                        pltpu.VMEM((lc, sw), F32), pltpu.VMEM((nq, lc, 128), F32), pltpu.VMEM((nq, lc, 128), F32),
                        pltpu.VMEM((8, gn), F32), pltpu.VMEM((8, gn), F32), pltpu.VMEM((1, gn), F32), pltpu.VMEM((1, gn), F32)],
        compiler_params=_params("arbitrary"))(*([uin] * nq), lrdt, ang, bb_re, bb_im, cc_re, cc_im, d_skip)


def _ssm_bwd(name, dge, y, uin, s_re, s_im, lrdt, ang, bbt_re, bbt_im, cct_re, cct_im, d_skip, sw, duin):
    t = uin.shape[0]
    gn = lrdt.shape[1]
    lc = _ssm_chunk(t)
    nb = t // lc
    jn = lc // 8
    ub, sb = sw // SSM_BLOCKS, gn // SSM_BLOCKS
    nq = sw // 128
    tail = 8

    def body(*refs):
        dge_refs, y_refs, u_refs = refs[:nq], refs[nq:2 * nq], refs[2 * nq:3 * nq]
        (sre_ref, sim_ref, tre_ref, tim_ref, lrdt_ref, ang_ref, btr_ref, bti_ref, ctr_ref, cti_ref, d_ref, duin_ref,
         du_ref, dar_ref, dai_ref, dbr_ref, dbi_ref, dcr_ref, dci_ref, dd_ref,
         q_re, q_im, a_re, a_im, dgp, yp, up, dup, cst_re, cst_im, sp_re, sp_im, car_re, car_im) = refs[3 * nq:]
        i = pl.program_id(0)
        lrdt_v, ang_v = lrdt_ref[...], ang_ref[...]

        @pl.when(i == 0)
        def _():
            k = (jn - lax.broadcasted_iota(jnp.int32, (jn, 1), 0)).astype(F32)
            _fill_rows8(q_re, q_im, *_pow_rows(lrdt_v, ang_v, k))
            car_re[...] = jnp.zeros_like(car_re)
            car_im[...] = jnp.zeros_like(car_im)
            for r in (dar_ref, dai_ref, dbr_ref, dbi_ref, dcr_ref, dci_ref, dd_ref):
                r[...] = jnp.zeros_like(r)

        _to_segments(dgp, dge_refs, jn)
        _to_segments(yp, y_refs, jn)
        _to_segments(up, u_refs, jn)
        yv = yp[...]
        u = up[...]
        cdf = 0.5 * (1.0 + lax.erf(yv * (1.0 / math.sqrt(2.0))))
        pdf = jnp.exp(-0.5 * yv * yv) * (1.0 / math.sqrt(2.0 * math.pi))
        dy = dgp[...] * (cdf + yv * pdf)
        dd_ref[...] += _colsum(dy * u)
        dyb = dy.astype(BF16)
        ubf = u.astype(BF16)
        for q in range(SSM_BLOCKS):
            dq = dyb[:, q * ub:(q + 1) * ub]
            a_re[:, q * sb:(q + 1) * sb] = jnp.dot(dq, ctr_ref[q], preferred_element_type=F32)
            a_im[:, q * sb:(q + 1) * sb] = -jnp.dot(dq, cti_ref[q], preferred_element_type=F32)
        a1r, a1i = _pow_rows(lrdt_v, ang_v, 1.0)
        ajr, aji = _pow_rows(lrdt_v, ang_v, float(jn))
        for q in range(SSM_BLOCKS):
            cols = slice(q * sb, (q + 1) * sb)
            ar8 = jnp.broadcast_to(a1r[:, cols], (8, sb))
            ai8 = jnp.broadcast_to(a1i[:, cols], (8, sb))

            def step(jj, carry, cols=cols, ar8=ar8, ai8=ai8):
                sr, si = carry
                rows = pl.ds(pl.multiple_of((jn - 2 - jj) * 8, 8), 8)
                mr, mi = _cmul_conj(ar8, ai8, sr, si)
                nr, ni = mr + a_re[rows, cols], mi + a_im[rows, cols]
                a_re[rows, cols] = nr
                a_im[rows, cols] = ni
                return nr, ni

            lax.fori_loop(0, jn - 1, step, (a_re[lc - 8:lc, cols], a_im[lc - 8:lc, cols]), unroll=4)
        er, ei = a_re[0:8, :], a_im[0:8, :]
        hr, hi = car_re[...], car_im[...]
        for s in range(7, -1, -1):
            cst_re[s:s + 1, :] = hr
            cst_im[s:s + 1, :] = hi
            mr, mi = _cmul_conj(ajr, aji, hr, hi)
            hr, hi = mr + er[s:s + 1, :], mi + ei[s:s + 1, :]
        car_re[...] = hr
        car_im[...] = hi
        first = (i == nb - 1).astype(F32)
        sp_re[0:tail, :] = tre_ref[...] * (1.0 - first)
        sp_im[0:tail, :] = tim_ref[...] * (1.0 - first)
        sp_re[tail:2 * tail, :] = sre_ref[lc - 8:lc, :]
        sp_im[tail:2 * tail, :] = sim_ref[lc - 8:lc, :]
        tn_dims = (((0,), (0,)), ((), ()))
        for q in range(SSM_BLOCKS):
            cols = slice(q * sb, (q + 1) * sb)
            ycols = slice(q * ub, (q + 1) * ub)
            cr = jnp.tile(cst_re[:, cols], (jn, 1))
            ci = jnp.tile(cst_im[:, cols], (jn, 1))
            mr, mi = _cmul_conj(q_re[:, cols], q_im[:, cols], cr, ci)
            lam_r, lam_i = a_re[:, cols] + mr, a_im[:, cols] + mi
            s_r, s_i = sre_ref[:, cols], sim_ref[:, cols]
            p0r, p0i = sp_re[tail - 1:2 * tail - 1, cols], sp_im[tail - 1:2 * tail - 1, cols]
            l0r, l0i, l1r, l1i = lam_r[0:8], lam_i[0:8], lam_r[8:lc], lam_i[8:lc]
            pvr, pvi = s_r[0:lc - 8], s_i[0:lc - 8]
            dar_ref[:, cols] += _colsum(l1r * pvr + l1i * pvi) + _colsum(l0r * p0r + l0i * p0i)
            dai_ref[:, cols] += _colsum(l1i * pvr - l1r * pvi) + _colsum(l0i * p0r - l0r * p0i)
            lrb, lib = lam_r.astype(BF16), lam_i.astype(BF16)
            dup[q] = (jnp.dot(lrb, btr_ref[q], preferred_element_type=F32)
                      + jnp.dot(lib, bti_ref[q], preferred_element_type=F32) + d_ref[:, ycols] * dy[:, ycols])
            _from_segments(du_ref, dup, q, jn, BF16)
            uq = ubf[:, ycols]
            dbr_ref[q] += lax.dot_general(uq, lrb, tn_dims, preferred_element_type=F32)
            dbi_ref[q] += lax.dot_general(uq, lib, tn_dims, preferred_element_type=F32)
            dq = dyb[:, ycols]
            dcr_ref[q] += lax.dot_general(s_r.astype(BF16), dq, tn_dims, preferred_element_type=F32)
            dci_ref[q] -= lax.dot_general(s_i.astype(BF16), dq, tn_dims, preferred_element_type=F32)

    rev = lambda i: (nb - 1 - i, 0)
    tailmap = lambda i: (jnp.maximum((nb - 1 - i) * (lc // tail) - 1, 0), 0)
    row = lambda i: (0, 0)
    blk3 = lambda i: (0, 0, 0)
    colblk = lambda base: [pl.BlockSpec((lc, 128), functools.partial(lambda i, q: (nb - 1 - i, base + q), q=q)) for q in range(nq)]
    return _pcall(
        body, name=name, grid=(nb,),
        in_specs=colblk(0) + colblk(0) + colblk(nq)
        + [pl.BlockSpec((lc, gn), rev), pl.BlockSpec((lc, gn), rev),
           pl.BlockSpec((tail, gn), tailmap), pl.BlockSpec((tail, gn), tailmap),
           pl.BlockSpec((1, gn), row), pl.BlockSpec((1, gn), row),
           pl.BlockSpec(bbt_re.shape, blk3), pl.BlockSpec(bbt_im.shape, blk3),
           pl.BlockSpec(cct_re.shape, blk3), pl.BlockSpec(cct_im.shape, blk3), pl.BlockSpec((1, sw), row), ANY_SPEC],
        out_specs=[pl.BlockSpec((lc, sw), lambda i: (nb - 1 - i, 1)), pl.BlockSpec((1, gn), row), pl.BlockSpec((1, gn), row),
                   pl.BlockSpec((SSM_BLOCKS, ub, sb), blk3), pl.BlockSpec((SSM_BLOCKS, ub, sb), blk3),
                   pl.BlockSpec((SSM_BLOCKS, sb, ub), blk3), pl.BlockSpec((SSM_BLOCKS, sb, ub), blk3),
                   pl.BlockSpec((1, sw), row)],
        out_shape=[jax.ShapeDtypeStruct(duin.shape, BF16), jax.ShapeDtypeStruct((1, gn), F32), jax.ShapeDtypeStruct((1, gn), F32),
                   jax.ShapeDtypeStruct((SSM_BLOCKS, ub, sb), F32), jax.ShapeDtypeStruct((SSM_BLOCKS, ub, sb), F32),
                   jax.ShapeDtypeStruct((SSM_BLOCKS, sb, ub), F32), jax.ShapeDtypeStruct((SSM_BLOCKS, sb, ub), F32),
                   jax.ShapeDtypeStruct((1, sw), F32)],
        scratch_shapes=[pltpu.VMEM((lc, gn), F32), pltpu.VMEM((lc, gn), F32), pltpu.VMEM((lc, gn), F32), pltpu.VMEM((lc, gn), F32),
                        pltpu.VMEM((lc, sw), F32), pltpu.VMEM((lc, sw), F32), pltpu.VMEM((lc, sw), F32),
                        pltpu.VMEM((nq, lc, 128), F32), pltpu.VMEM((8, gn), F32), pltpu.VMEM((8, gn), F32),
                        pltpu.VMEM((2 * tail, gn), F32), pltpu.VMEM((2 * tail, gn), F32),
                        pltpu.VMEM((1, gn), F32), pltpu.VMEM((1, gn), F32)],
        input_output_aliases={3 * nq + 11: 0},
        compiler_params=_params("arbitrary"))(*([dge] * nq), *([y] * nq), *([uin] * nq), s_re, s_im, s_re, s_im, lrdt, ang,
                                               bbt_re, bbt_im, cct_re, cct_im, d_skip, duin)


def _ssm_fwd_doubling(name, uin, lrdt, ang, bb_re, bb_im, cc_re, cc_im, d_skip, sw):
    t = uin.shape[0]
    gn = lrdt.shape[1]
    lc = _ssm_chunk(t)
    nsteps = int(math.log2(lc))
    pad = lc // 2
    ub, sb = sw // SSM_BLOCKS, gn // SSM_BLOCKS

    def body(u_ref, lrdt_ref, ang_ref, bbr_ref, bbi_ref, ccr_ref, cci_ref, d_ref, y_ref, ge_ref, sre_ref, sim_ref,
             p_re, p_im, a_re, a_im, b_re, b_im, car_re, car_im):
        i = pl.program_id(0)
        lrdt_v, ang_v = lrdt_ref[...], ang_ref[...]

        @pl.when(i == 0)
        def _():
            k = (lax.broadcasted_iota(jnp.int32, (lc, 1), 0) + 1).astype(F32)
            pr, pi = _pow_rows(lrdt_v, ang_v, k)
            p_re[...] = pr
            p_im[...] = pi
            zeros = jnp.zeros((pad, gn), F32)
            a_re[0:pad, :] = zeros
            a_im[0:pad, :] = zeros
            b_re[0:pad, :] = zeros
            b_im[0:pad, :] = zeros
            car_re[...] = jnp.zeros_like(car_re)
            car_im[...] = jnp.zeros_like(car_im)

        u = u_ref[...]
        ubf = u.astype(BF16)
        for q in range(SSM_BLOCKS):
            uq = ubf[:, q * ub:(q + 1) * ub]
            a_re[pad:pad + lc, q * sb:(q + 1) * sb] = jnp.dot(uq, bbr_ref[q], preferred_element_type=F32)
            a_im[pad:pad + lc, q * sb:(q + 1) * sb] = jnp.dot(uq, bbi_ref[q], preferred_element_type=F32)
        bufs = [(a_re, a_im), (b_re, b_im)]
        for q in range(SSM_BLOCKS):
            cols = slice(q * sb, (q + 1) * sb)
            for j in range(nsteps):
                dd = 1 << j
                (s_re, s_im), (d_re, d_im) = bufs[j % 2], bufs[(j + 1) % 2]
                pr, pi = _pow_rows(lrdt_v[:, cols], ang_v[:, cols], float(dd))
                cr, ci = s_re[pad:pad + lc, cols], s_im[pad:pad + lc, cols]
                hr, hi = s_re[pad - dd:pad - dd + lc, cols], s_im[pad - dd:pad - dd + lc, cols]
                d_re[pad:pad + lc, cols] = cr + (pr * hr - pi * hi)
                d_im[pad:pad + lc, cols] = ci + (pr * hi + pi * hr)
            f_re, f_im = bufs[nsteps % 2]
            cr, ci = car_re[:, cols], car_im[:, cols]
            pr, pi = p_re[:, cols], p_im[:, cols]
            sr = f_re[pad:pad + lc, cols] + (pr * cr - pi * ci)
            si = f_im[pad:pad + lc, cols] + (pr * ci + pi * cr)
            sre_ref[:, cols] = sr
            sim_ref[:, cols] = si
            car_re[:, cols] = sr[lc - 1:lc, :]
            car_im[:, cols] = si[lc - 1:lc, :]
            ycols = slice(q * ub, (q + 1) * ub)
            y = (jnp.dot(sr.astype(BF16), ccr_ref[q], preferred_element_type=F32)
                 - jnp.dot(si.astype(BF16), cci_ref[q], preferred_element_type=F32)
                 + d_ref[:, ycols] * u[:, ycols])
            y_ref[:, ycols] = y
            ge_ref[:, ycols] = (0.5 * y * (1.0 + lax.erf(y * (1.0 / math.sqrt(2.0))))).astype(BF16)

    cblk = sw // sw
    row = lambda i: (0, 0)
    blk3 = lambda i: (0, 0, 0)
    return _pcall(
        body, name=name, grid=(t // lc,),
        in_specs=[pl.BlockSpec((lc, sw), lambda i: (i, cblk)), pl.BlockSpec((1, gn), row), pl.BlockSpec((1, gn), row),
                  pl.BlockSpec(bb_re.shape, blk3), pl.BlockSpec(bb_im.shape, blk3),
                  pl.BlockSpec(cc_re.shape, blk3), pl.BlockSpec(cc_im.shape, blk3), pl.BlockSpec((1, sw), row)],
        out_specs=[pl.BlockSpec((lc, sw), lambda i: (i, 0)), pl.BlockSpec((lc, sw), lambda i: (i, 0)),
                   pl.BlockSpec((lc, gn), lambda i: (i, 0)), pl.BlockSpec((lc, gn), lambda i: (i, 0))],
        out_shape=[jax.ShapeDtypeStruct((t, sw), F32), jax.ShapeDtypeStruct((t, sw), BF16),
                   jax.ShapeDtypeStruct((t, gn), F32), jax.ShapeDtypeStruct((t, gn), F32)],
        scratch_shapes=[pltpu.VMEM((lc, gn), F32), pltpu.VMEM((lc, gn), F32),
                        pltpu.VMEM((pad + lc, gn), F32), pltpu.VMEM((pad + lc, gn), F32),
                        pltpu.VMEM((pad + lc, gn), F32), pltpu.VMEM((pad + lc, gn), F32),
                        pltpu.VMEM((1, gn), F32), pltpu.VMEM((1, gn), F32)],
        compiler_params=_params("arbitrary"))(uin, lrdt, ang, bb_re, bb_im, cc_re, cc_im, d_skip)


def _ssm_bwd_doubling(name, dge, y, uin, s_re, s_im, lrdt, ang, bbt_re, bbt_im, cct_re, cct_im, d_skip, sw):
    t = uin.shape[0]
    gn = lrdt.shape[1]
    lc = _ssm_chunk(t)
    nb = t // lc
    nsteps = int(math.log2(lc))
    pad = lc // 2
    ub, sb = sw // SSM_BLOCKS, gn // SSM_BLOCKS
    tail = 8

    def body(dge_ref, y_ref, u_ref, sre_ref, sim_ref, tre_ref, tim_ref, lrdt_ref, ang_ref, btr_ref, bti_ref, ctr_ref, cti_ref,
             d_ref, du_ref, dar_ref, dai_ref, dbr_ref, dbi_ref, dcr_ref, dci_ref, dd_ref,
             q_re, q_im, a_re, a_im, b_re, b_im, sp_re, sp_im, car_re, car_im):
        i = pl.program_id(0)
        lrdt_v, ang_v = lrdt_ref[...], ang_ref[...]

        @pl.when(i == 0)
        def _():
            k = (lc - lax.broadcasted_iota(jnp.int32, (lc, 1), 0)).astype(F32)
            qr, qi = _pow_rows(lrdt_v, ang_v, k)
            q_re[...] = qr
            q_im[...] = qi
            zeros = jnp.zeros((pad, gn), F32)
            a_re[lc:lc + pad, :] = zeros
            a_im[lc:lc + pad, :] = zeros
            b_re[lc:lc + pad, :] = zeros
            b_im[lc:lc + pad, :] = zeros
            car_re[...] = jnp.zeros_like(car_re)
            car_im[...] = jnp.zeros_like(car_im)
            for r in (dar_ref, dai_ref, dbr_ref, dbi_ref, dcr_ref, dci_ref, dd_ref):
                r[...] = jnp.zeros_like(r)

        first = (i == nb - 1).astype(F32)
        sp_re[0:tail, :] = tre_ref[...] * (1.0 - first)
        sp_im[0:tail, :] = tim_ref[...] * (1.0 - first)
        sp_re[tail:tail + lc, :] = sre_ref[...]
        sp_im[tail:tail + lc, :] = sim_ref[...]

        yv = y_ref[...]
        u = u_ref[...]
        cdf = 0.5 * (1.0 + lax.erf(yv * (1.0 / math.sqrt(2.0))))
        pdf = jnp.exp(-0.5 * yv * yv) * (1.0 / math.sqrt(2.0 * math.pi))
        dy = dge_ref[...].astype(F32) * (cdf + yv * pdf)
        dd_ref[...] += _colsum(dy * u)
        dyb = dy.astype(BF16)
        ubf = u.astype(BF16)
        for q in range(SSM_BLOCKS):
            dq = dyb[:, q * ub:(q + 1) * ub]
            a_re[0:lc, q * sb:(q + 1) * sb] = jnp.dot(dq, ctr_ref[q], preferred_element_type=F32)
            a_im[0:lc, q * sb:(q + 1) * sb] = -jnp.dot(dq, cti_ref[q], preferred_element_type=F32)
        bufs = [(a_re, a_im), (b_re, b_im)]
        tn_dims = (((0,), (0,)), ((), ()))
        for q in range(SSM_BLOCKS):
            cols = slice(q * sb, (q + 1) * sb)
            ycols = slice(q * ub, (q + 1) * ub)
            for j in range(nsteps):
                ds = 1 << j
                (s_r, s_i), (d_r, d_i) = bufs[j % 2], bufs[(j + 1) % 2]
                pr, pi = _pow_rows(lrdt_v[:, cols], ang_v[:, cols], float(ds))
                cr, ci = s_r[0:lc, cols], s_i[0:lc, cols]
                hr, hi = s_r[ds:ds + lc, cols], s_i[ds:ds + lc, cols]
                d_r[0:lc, cols] = cr + (pr * hr + pi * hi)
                d_i[0:lc, cols] = ci + (pr * hi - pi * hr)
            f_r, f_i = bufs[nsteps % 2]
            cr, ci = car_re[:, cols], car_im[:, cols]
            qr, qi = q_re[:, cols], q_im[:, cols]
            lam_r = f_r[0:lc, cols] + (qr * cr + qi * ci)
            lam_i = f_i[0:lc, cols] + (qr * ci - qi * cr)
            car_re[:, cols] = lam_r[0:1, :]
            car_im[:, cols] = lam_i[0:1, :]
            pvr, pvi = sp_re[tail - 1:tail - 1 + lc, cols], sp_im[tail - 1:tail - 1 + lc, cols]
            dar_ref[:, cols] += _colsum(lam_r * pvr + lam_i * pvi)
            dai_ref[:, cols] += _colsum(lam_i * pvr - lam_r * pvi)
            lrb, lib = lam_r.astype(BF16), lam_i.astype(BF16)
            du_ref[:, ycols] = (jnp.dot(lrb, btr_ref[q], preferred_element_type=F32)
                                + jnp.dot(lib, bti_ref[q], preferred_element_type=F32)
                                + d_ref[:, ycols] * dy[:, ycols]).astype(BF16)
            uq = ubf[:, ycols]
            dbr_ref[q] += lax.dot_general(uq, lrb, tn_dims, preferred_element_type=F32)
            dbi_ref[q] += lax.dot_general(uq, lib, tn_dims, preferred_element_type=F32)
            dq = dyb[:, ycols]
            dcr_ref[q] += lax.dot_general(sre_ref[:, cols].astype(BF16), dq, tn_dims, preferred_element_type=F32)
            dci_ref[q] -= lax.dot_general(sim_ref[:, cols].astype(BF16), dq, tn_dims, preferred_element_type=F32)

    cblk = 1
    rev = lambda i: (nb - 1 - i, 0)
    revu = lambda i: (nb - 1 - i, cblk)
    tailmap = lambda i: (jnp.maximum((nb - 1 - i) * (lc // tail) - 1, 0), 0)
    row = lambda i: (0, 0)
    blk3 = lambda i: (0, 0, 0)
    return _pcall(
        body, name=name, grid=(nb,),
        in_specs=[pl.BlockSpec((lc, sw), rev), pl.BlockSpec((lc, sw), rev), pl.BlockSpec((lc, sw), revu),
                  pl.BlockSpec((lc, gn), rev), pl.BlockSpec((lc, gn), rev),
                  pl.BlockSpec((tail, gn), tailmap), pl.BlockSpec((tail, gn), tailmap),
                  pl.BlockSpec((1, gn), row), pl.BlockSpec((1, gn), row),
                  pl.BlockSpec(bbt_re.shape, blk3), pl.BlockSpec(bbt_im.shape, blk3),
                  pl.BlockSpec(cct_re.shape, blk3), pl.BlockSpec(cct_im.shape, blk3), pl.BlockSpec((1, sw), row)],
        out_specs=[pl.BlockSpec((lc, sw), rev), pl.BlockSpec((1, gn), row), pl.BlockSpec((1, gn), row),
                   pl.BlockSpec((SSM_BLOCKS, ub, sb), blk3), pl.BlockSpec((SSM_BLOCKS, ub, sb), blk3),
                   pl.BlockSpec((SSM_BLOCKS, sb, ub), blk3), pl.BlockSpec((SSM_BLOCKS, sb, ub), blk3),
                   pl.BlockSpec((1, sw), row)],
        out_shape=[jax.ShapeDtypeStruct((t, sw), BF16), jax.ShapeDtypeStruct((1, gn), F32), jax.ShapeDtypeStruct((1, gn), F32),
                   jax.ShapeDtypeStruct((SSM_BLOCKS, ub, sb), F32), jax.ShapeDtypeStruct((SSM_BLOCKS, ub, sb), F32),
                   jax.ShapeDtypeStruct((SSM_BLOCKS, sb, ub), F32), jax.ShapeDtypeStruct((SSM_BLOCKS, sb, ub), F32),
                   jax.ShapeDtypeStruct((1, sw), F32)],
        scratch_shapes=[pltpu.VMEM((lc, gn), F32), pltpu.VMEM((lc, gn), F32),
                        pltpu.VMEM((lc + pad, gn), F32), pltpu.VMEM((lc + pad, gn), F32),
                        pltpu.VMEM((lc + pad, gn), F32), pltpu.VMEM((lc + pad, gn), F32),
                        pltpu.VMEM((tail + lc, gn), F32), pltpu.VMEM((tail + lc, gn), F32),
                        pltpu.VMEM((1, gn), F32), pltpu.VMEM((1, gn), F32)],
        compiler_params=_params("arbitrary"))(dge, y, uin, s_re, s_im, s_re, s_im, lrdt, ang, bbt_re, bbt_im, cct_re, cct_im, d_skip)


def _blockdiag_b(bb, sw):
    gpb = (sw // SSM_GROUP) // SSM_BLOCKS
    b4 = bb.reshape(SSM_BLOCKS, gpb, SSM_STATE, SSM_GROUP)
    eye = jnp.eye(gpb, dtype=bb.dtype)
    out = jnp.einsum('qgnh,gk->qghkn', b4, eye)
    return out.reshape(SSM_BLOCKS, gpb * SSM_GROUP, gpb * SSM_STATE)


def _blockdiag_c(cc, sw):
    gpb = (sw // SSM_GROUP) // SSM_BLOCKS
    c4 = cc.reshape(SSM_BLOCKS, gpb, SSM_GROUP, SSM_STATE)
    eye = jnp.eye(gpb, dtype=cc.dtype)
    out = jnp.einsum('qghn,gk->qgnkh', c4, eye)
    return out.reshape(SSM_BLOCKS, gpb * SSM_STATE, gpb * SSM_GROUP)


def _diag_of_b(dbb, sw):
    gpb = (sw // SSM_GROUP) // SSM_BLOCKS
    d5 = dbb.reshape(SSM_BLOCKS, gpb, SSM_GROUP, gpb, SSM_STATE)
    return jnp.einsum('qghgn->qgnh', d5).reshape(SSM_BLOCKS * gpb * SSM_STATE, SSM_GROUP)


def _diag_of_c(dcc, sw):
    gpb = (sw // SSM_GROUP) // SSM_BLOCKS
    d5 = dcc.reshape(SSM_BLOCKS, gpb, SSM_STATE, gpb, SSM_GROUP)
    return jnp.einsum('qgngh->qghn', d5).reshape(SSM_BLOCKS * gpb, SSM_GROUP, SSM_STATE)


def _ada_fwd(name, c_all, w_sh, b_sh):
    nb, d = c_all.shape
    ncol = w_sh.shape[1]
    tn = _pick(ncol, 768)

    def body(c_ref, w_ref, b_ref, o_ref):
        cv = c_ref[...]
        sil = cv * _sigmoid(cv)
        o_ref[...] = jnp.dot(sil, w_ref[...], preferred_element_type=F32, precision=lax.Precision.HIGHEST) + b_ref[...]

    return _pcall(body, name=name, grid=(ncol // tn,),
                  in_specs=[pl.BlockSpec((nb, d), lambda j: (0, 0)), pl.BlockSpec((d, tn), lambda j: (0, j)),
                            pl.BlockSpec((1, tn), lambda j: (0, j))],
                  out_specs=pl.BlockSpec((nb, tn), lambda j: (0, j)),
                  out_shape=jax.ShapeDtypeStruct((nb, ncol), F32), compiler_params=_params("parallel"))(c_all, w_sh, b_sh)


def _ada_bwd(name, c_all, dmod_sh):
    nb, d = c_all.shape
    ncol = dmod_sh.shape[1]
    tn = _pick(ncol, 768)

    def body(c_ref, g_ref, o_ref):
        cv = c_ref[...]
        sil = cv * _sigmoid(cv)
        o_ref[...] = lax.dot_general(sil, g_ref[...], (((0,), (0,)), ((), ())), preferred_element_type=F32,
                                     precision=lax.Precision.HIGHEST)

    return _pcall(body, name=name, grid=(ncol // tn,),
                  in_specs=[pl.BlockSpec((nb, d), lambda j: (0, 0)), pl.BlockSpec((nb, tn), lambda j: (0, j))],
                  out_specs=pl.BlockSpec((d, tn), lambda j: (0, j)),
                  out_shape=jax.ShapeDtypeStruct((d, ncol), F32), compiler_params=_params("parallel"))(c_all, dmod_sh)


def _place():
    return lax.axis_index("x"), lax.axis_index("y"), lax.axis_index("c")


def _allgather_small(name, blk, deps=()):
    m_per, n = blk.shape

    def body(x_ref, *rest):
        out_ref, send_sems, recv_sems, local_sem = rest[len(deps):]
        x, y, c = _place()
        me, sibling = (x, y, c), (x, y, 1 - c)
        chips = [(1 - x, y), (x, 1 - y), (1 - x, 1 - y)]

        def rows(px, py, pc):
            return out_ref.at[pl.ds((4 * px + 2 * py + pc) * m_per, m_per), :]

        def copy(k, block, to, src=None):
            return pltpu.make_async_remote_copy(
                src_ref=rows(*block) if src is None else src, dst_ref=rows(*block),
                send_sem=send_sems.at[k], recv_sem=recv_sems.at[k], device_id=to, device_id_type=MESH)

        mine = pltpu.make_async_copy(x_ref, rows(*me), local_sem)
        mine.start()
        first = [copy(0, me, sibling, src=x_ref)]
        first += [copy(1 + j, me, (*chip, c), src=x_ref) for j, chip in enumerate(chips)]
        for cp in first:
            cp.start()
        passed = [copy(4 + j, (*chip, c), sibling) for j, chip in enumerate(chips)]
        for j, chip in enumerate(chips):
            copy(1 + j, (*chip, c), me).wait_recv()
            passed[j].start()
        copy(0, sibling, me).wait_recv()
        for j, chip in enumerate(chips):
            copy(4 + j, (*chip, 1 - c), me).wait_recv()
        for cp in first + passed:
            cp.wait_send()
        mine.wait()

    return _pcall(body, name=name, out_shape=jax.ShapeDtypeStruct((N_DEV * m_per, n), blk.dtype),
                  in_specs=[pl.BlockSpec(memory_space=pltpu.VMEM)] + [ANY_SPEC] * len(deps),
                  out_specs=pl.BlockSpec(memory_space=pltpu.VMEM),
                  scratch_shapes=[pltpu.SemaphoreType.DMA((7,)), pltpu.SemaphoreType.DMA((7,)), pltpu.SemaphoreType.DMA],
                  compiler_params=pltpu.CompilerParams(vmem_limit_bytes=VMEM_LIMIT))(blk, *deps)


def _other_chips(x, y):
    return [(1 - x, y), (x, 1 - y), (1 - x, 1 - y)]


def _place_shards(shards, s_me):
    return [lax.dynamic_update_slice(lax.empty((N_CHIPS,) + s.shape, s.dtype), s[None], (s_me, 0, 0)) for s in shards]


def _ag_copy(src, land, send, recv, wi, j, chip, x, y, c):
    hr = src.shape[0] // 2
    half = pl.ds(pl.multiple_of(c * hr, 16), hr)
    return pltpu.make_async_remote_copy(
        src_ref=src.at[half, :], dst_ref=land.at[2 * x + y, half, :], send_sem=send.at[3 * wi + j], recv_sem=recv.at[3 * wi + j],
        device_id=(chip[0], chip[1], c), device_id_type=MESH)


def _ag_start(name, shards, lands, groups, deps=()):
    nw, ng, nd = len(shards), len(groups), len(deps)

    def body(*refs):
        src, land = refs[:nw], refs[nw:2 * nw]
        sems = refs[2 * nw + nd:2 * nw + nd + 2 * ng]
        token = refs[-1]
        x, y, c = _place()
        for gi, grp in enumerate(groups):
            for wi, w in enumerate(grp):
                for j, chip in enumerate(_other_chips(x, y)):
                    _ag_copy(src[w], land[w], sems[2 * gi], sems[2 * gi + 1], wi, j, chip, x, y, c).start()
        token[...] = jnp.zeros_like(token)

    sem_shapes = []
    for grp in groups:
        sem_shapes += [pltpu.SemaphoreType.DMA((3 * len(grp),))] * 2
    out_shape = sem_shapes + [pltpu.HBM(s.shape, s.dtype) for s in shards] + [pltpu.HBM(l.shape, l.dtype) for l in lands]
    out_shape += [jax.ShapeDtypeStruct((8, 128), F32)]
    res = _pcall(body, name=name, out_shape=out_shape, in_specs=[HBM_SPEC] * (2 * nw) + [ANY_SPEC] * nd,
                 out_specs=[SEM_SPEC] * (2 * ng) + [HBM_SPEC] * (2 * nw) + [pl.BlockSpec(memory_space=pltpu.VMEM)],
                 input_output_aliases={i: 2 * ng + i for i in range(2 * nw)},
                 compiler_params=pltpu.CompilerParams(has_side_effects=EFFECT))(
                     *[_hbm(s) for s in shards], *[_hbm(l) for l in lands], *deps)
    sems = [(res[2 * gi], res[2 * gi + 1]) for gi in range(ng)]
    return sems, list(res[2 * ng:2 * ng + nw]), list(res[2 * ng + nw:2 * ng + 2 * nw]), res[-1]


def _ag_wait(name, shards, lands, send, recv, after):
    n = len(shards)

    def body(*refs):
        src, land = refs[:n], refs[n:2 * n]
        send_sem, recv_sem = refs[2 * n], refs[2 * n + 1]
        x, y, c = _place()
        for wi in range(n):
            for j, chip in enumerate(_other_chips(x, y)):
                _ag_copy(src[wi], land[wi], send_sem, recv_sem, wi, j, chip, x, y, c).wait_send()
                _ag_copy(src[wi], land[wi], send_sem, recv_sem, wi, j, chip, chip[0], chip[1], c).wait_recv()

    res = _pcall(body, name=name, out_shape=[pltpu.HBM(a.shape, a.dtype) for a in list(shards) + list(lands)],
                 in_specs=[HBM_SPEC] * (2 * n) + [SEM_SPEC, SEM_SPEC, ANY_SPEC], out_specs=[HBM_SPEC] * (2 * n),
                 input_output_aliases={i: i for i in range(2 * n)},
                 compiler_params=pltpu.CompilerParams(has_side_effects=EFFECT))(*shards, *lands, send, recv, after)
    return list(res[n:])


def _ag_forward(name, lands):
    n = len(lands)

    def body(*refs):
        out = refs[n:2 * n]
        send, recv = refs[2 * n], refs[2 * n + 1]
        x, y, c = _place()
        sib = (x, y, 1 - c)
        cps = []
        for wi in range(n):
            hr = out[wi].shape[1] // 2
            for j, (cx, cy) in enumerate(_other_chips(x, y)):
                got = out[wi].at[2 * cx + cy, pl.ds(pl.multiple_of(c * hr, 16), hr), :]
                cp = pltpu.make_async_remote_copy(src_ref=got, dst_ref=got, send_sem=send.at[3 * wi + j], recv_sem=recv.at[3 * wi + j],
                                                  device_id=sib, device_id_type=MESH)
                cp.start()
                cps.append(cp)
        for wi in range(n):
            hr = out[wi].shape[1] // 2
            for j, (cx, cy) in enumerate(_other_chips(x, y)):
                got = out[wi].at[2 * cx + cy, pl.ds(pl.multiple_of((1 - c) * hr, 16), hr), :]
                pltpu.make_async_remote_copy(src_ref=got, dst_ref=got, send_sem=send.at[3 * wi + j], recv_sem=recv.at[3 * wi + j],
                                             device_id=sib, device_id_type=MESH).wait_recv()
        for cp in cps:
            cp.wait_send()

    res = _pcall(body, name=name, out_shape=[jax.ShapeDtypeStruct(l.shape, l.dtype) for l in lands],
                 in_specs=[ANY_SPEC] * n, out_specs=[ANY_SPEC] * n, input_output_aliases={i: i for i in range(n)},
                 scratch_shapes=[pltpu.SemaphoreType.DMA((3 * n,)), pltpu.SemaphoreType.DMA((3 * n,))])(*lands)
    return list(res)


def _peers(x, y, c):
    offs = [(dx, dy, dc) for dx in (0, 1) for dy in (0, 1) for dc in (0, 1)][1:]
    return [(1 - x if dx else x, 1 - y if dy else y, 1 - c if dc else c) for dx, dy, dc in offs]


def _rs_copy(g_ref, land_ref, send, recv, wi, k, to, sender):
    hr = g_ref.shape[1] // 2
    return pltpu.make_async_remote_copy(
        src_ref=g_ref.at[2 * to[0] + to[1], pl.ds(pl.multiple_of(to[2] * hr, 16), hr), :], dst_ref=land_ref.at[sender],
        send_sem=send.at[7 * wi + k], recv_sem=recv.at[7 * wi + k], device_id=to, device_id_type=MESH)


def _rs_start(name, gs):
    n = len(gs)
    lands = [lax.empty((N_DEV, g.shape[1] // 2, g.shape[2]), BF16) for g in gs]

    def body(*refs):
        g, land = refs[:n], refs[n:2 * n]
        send, recv = refs[2 * n], refs[2 * n + 1]
        token = refs[-1]
        x, y, c = _place()
        me = 4 * x + 2 * y + c
        for wi in range(n):
            for k, to in enumerate(_peers(x, y, c)):
                _rs_copy(g[wi], land[wi], send, recv, wi, k, to, me).start()
        token[...] = jnp.zeros_like(token)

    out_shape = [pltpu.SemaphoreType.DMA((7 * n,))] * 2 + [pltpu.HBM(a.shape, a.dtype) for a in list(gs) + lands]
    out_shape += [jax.ShapeDtypeStruct((8, 128), F32)]
    res = _pcall(body, name=name, out_shape=out_shape, in_specs=[HBM_SPEC] * (2 * n),
                 out_specs=[SEM_SPEC] * 2 + [HBM_SPEC] * (2 * n) + [pl.BlockSpec(memory_space=pltpu.VMEM)],
                 input_output_aliases={i: 2 + i for i in range(2 * n)},
                 compiler_params=pltpu.CompilerParams(has_side_effects=EFFECT))(
                     *[_hbm(a) for a in gs], *[_hbm(a) for a in lands])
    return res[0], res[1], list(res[2:2 + n]), list(res[2 + n:2 + 2 * n]), res[-1]


def _rs_wait(name, gs, lands, send, recv, after):
    n = len(gs)

    def body(*refs):
        g, land = refs[:n], refs[n:2 * n]
        send_sem, recv_sem = refs[2 * n], refs[2 * n + 1]
        x, y, c = _place()
        me = 4 * x + 2 * y + c
        for wi in range(n):
            for k, to in enumerate(_peers(x, y, c)):
                _rs_copy(g[wi], land[wi], send_sem, recv_sem, wi, k, to, me).wait_send()
                _rs_copy(g[wi], land[wi], send_sem, recv_sem, wi, k, (x, y, c), 4 * to[0] + 2 * to[1] + to[2]).wait_recv()

    res = _pcall(body, name=name, out_shape=[pltpu.HBM(a.shape, a.dtype) for a in list(gs) + list(lands)],
                 in_specs=[HBM_SPEC] * (2 * n) + [SEM_SPEC, SEM_SPEC, ANY_SPEC], out_specs=[HBM_SPEC] * (2 * n),
                 input_output_aliases={i: i for i in range(2 * n)},
                 compiler_params=pltpu.CompilerParams(has_side_effects=EFFECT))(*gs, *lands, send, recv, after)
    return list(res[:n]), list(res[n:])


def _bc_copy(blk_ref, land_ref, send, recv, k, to, slot):
    return pltpu.make_async_remote_copy(src_ref=blk_ref, dst_ref=land_ref.at[slot], send_sem=send.at[k], recv_sem=recv.at[k],
                                        device_id=to, device_id_type=MESH)


def _bcast_start(name, blk):
    land = lax.empty((N_DEV,) + blk.shape, blk.dtype)

    def body(blk_ref, land_ref, send, recv, blk_thru, land_thru, token):
        x, y, c = _place()
        for k, to in enumerate(_peers(x, y, c)):
            _bc_copy(blk_ref, land_ref, send, recv, k, to, 4 * x + 2 * y + c).start()
        token[...] = jnp.zeros_like(token)

    return _pcall(body, name=name,
                  out_shape=[pltpu.SemaphoreType.DMA((7,)), pltpu.SemaphoreType.DMA((7,)), pltpu.HBM(blk.shape, blk.dtype),
                             pltpu.HBM(land.shape, land.dtype), jax.ShapeDtypeStruct((8, 128), F32)],
                  in_specs=[HBM_SPEC, HBM_SPEC], out_specs=[SEM_SPEC, SEM_SPEC, HBM_SPEC, HBM_SPEC, pl.BlockSpec(memory_space=pltpu.VMEM)],
                  input_output_aliases={0: 2, 1: 3}, compiler_params=pltpu.CompilerParams(has_side_effects=EFFECT))(_hbm(blk), _hbm(land))


def _bcast_wait(name, blk, land, send, recv, after):
    def body(blk_ref, land_ref, send_sem, recv_sem, after_ref, blk_thru, land_thru):
        x, y, c = _place()
        for k, to in enumerate(_peers(x, y, c)):
            _bc_copy(blk_ref, land_ref, send_sem, recv_sem, k, to, 4 * x + 2 * y + c).wait_send()
            _bc_copy(blk_ref, land_ref, send_sem, recv_sem, k, to, 4 * to[0] + 2 * to[1] + to[2]).wait_recv()

    return _pcall(body, name=name, out_shape=[pltpu.HBM(blk.shape, blk.dtype), pltpu.HBM(land.shape, land.dtype)],
                  in_specs=[HBM_SPEC, HBM_SPEC, SEM_SPEC, SEM_SPEC, ANY_SPEC], out_specs=[HBM_SPEC, HBM_SPEC],
                  input_output_aliases={0: 0, 1: 1}, compiler_params=pltpu.CompilerParams(has_side_effects=EFFECT))(
                      blk, land, send, recv, after)[1]


def _rs_sum(name, g, land):
    _, rs, cs = g.shape
    hr = rs // 2
    ch = _pick(hr, 64, 16)

    def body(g_ref, land_ref, out_ref, recv, local_sems, sib_sems):
        x, y, c = _place()
        me = 4 * x + 2 * y + c
        cps = [pltpu.make_async_copy(g_ref.at[2 * x + y, pl.ds(pl.multiple_of(c * hr, 16), hr), :], recv.at[me], local_sems.at[7])]
        for k, (tx, ty, tc) in enumerate(_peers(x, y, c)):
            slot = 4 * tx + 2 * ty + tc
            cps.append(pltpu.make_async_copy(land_ref.at[slot], recv.at[slot], local_sems.at[k]))
        for cp in cps:
            cp.start()
        for cp in cps:
            cp.wait()
        base = pl.multiple_of(c * hr, 16)
        for r0 in range(0, hr, ch):
            acc = recv[0, r0:r0 + ch, :].astype(F32)
            for k in range(1, N_DEV):
                acc = acc + recv[k, r0:r0 + ch, :].astype(F32)
            out_ref[pl.ds(base + r0, ch), :] = acc
        half = out_ref.at[pl.ds(base, hr), :]
        sib = pltpu.make_async_remote_copy(src_ref=half, dst_ref=half, send_sem=sib_sems.at[0], recv_sem=sib_sems.at[1],
                                           device_id=(x, y, 1 - c), device_id_type=MESH)
        sib.start()
        other = out_ref.at[pl.ds(pl.multiple_of((1 - c) * hr, 16), hr), :]
        pltpu.make_async_remote_copy(src_ref=other, dst_ref=other, send_sem=sib_sems.at[0], recv_sem=sib_sems.at[1],
                                     device_id=(x, y, 1 - c), device_id_type=MESH).wait_recv()
        sib.wait_send()

    return _pcall(
        body, name=name, out_shape=jax.ShapeDtypeStruct((rs, cs), F32),
        in_specs=[ANY_SPEC, ANY_SPEC], out_specs=pl.BlockSpec(memory_space=pltpu.VMEM),
        scratch_shapes=[pltpu.VMEM((N_DEV, hr, cs), BF16), pltpu.SemaphoreType.DMA((8,)), pltpu.SemaphoreType.DMA((2,))],
        compiler_params=pltpu.CompilerParams(vmem_limit_bytes=VMEM_LIMIT))(g, land)


def _gate_norm_epilogue(factor, x_in, gt, nxt, t, d):
    blk = (512, d)
    full = lambda i, j: (i, j)
    rowv = lambda i, j: (0, j)

    def epi(acc, res, scale, *norm):
        x_new = res + (factor * scale) * acc
        if not norm:
            return x_new, acc
        gv, scv, shv = norm
        r = lax.rsqrt(jnp.mean(x_new * x_new, axis=-1, keepdims=True) + EPS)
        return x_new, acc, x_new * r * gv * (1.0 + scv) + shv

    ins = [(x_in, blk, full), (gt, (1, d), rowv)] + [(v, (1, d), rowv) for v in (nxt or ())]
    outs = [((t, d), F32, blk, full), ((t, d), BF16, blk, full)] + ([((t, d), BF16, blk, full)] if nxt else [])
    return epi, ins, outs


def _ffn_in_act(name, h, w_in, tm=512):
    t, d = h.shape
    ns = w_in.shape[2]
    tm = _pick(t, tm, 8)
    w4 = w_in.reshape(2, 2, d, ns)

    def body(h_ref, w_ref, ab_ref, act_ref):
        hv = h_ref[...]
        a = jnp.dot(hv, w_ref[0], preferred_element_type=F32)
        b = jnp.dot(hv, w_ref[1], preferred_element_type=F32)
        ab_ref[0] = a.astype(BF16)
        ab_ref[1] = b.astype(BF16)
        act_ref[...] = (a * _sigmoid(a) * b).astype(BF16)

    return _pcall(body, name=name, grid=(2, t // tm),
                  in_specs=[pl.BlockSpec((tm, d), lambda c, i: (i, 0)), pl.BlockSpec((2, None, d, ns), lambda c, i: (0, c, 0, 0))],
                  out_specs=[pl.BlockSpec((2, tm, ns), lambda c, i: (0, i, c)), pl.BlockSpec((tm, ns), lambda c, i: (i, c))],
                  out_shape=[jax.ShapeDtypeStruct((2, t, 2 * ns), BF16), jax.ShapeDtypeStruct((t, 2 * ns), BF16)],
                  compiler_params=_params("parallel", "parallel"))(h, w4)


def _dswiglu_epilogue(ab, t, f, tn):
    blk = (2, 512, tn)
    idx = lambda i, j: (0, i, j)

    def epi(dact, abv):
        a, b = abv[0].astype(F32), abv[1].astype(F32)
        s = _sigmoid(a)
        return (jnp.stack([dact * b * (s * (1.0 + a * (1.0 - s))), dact * (a * s)]),)

    return epi, [(ab, blk, idx)], [((2, t, f), BF16, blk, idx)]


def _ffn_fwd(tag, x, h, gt, get_w_in, get_w_out, nxt):
    t, d = x.shape
    ab, act = _ffn_in_act(tag + "_in", h, get_w_in(h))
    epi, epi_ins, epi_outs = _gate_norm_epilogue(0.5, x, gt, nxt, t, d)
    res = _mm(tag + "_out", act, get_w_out(act), tm=512, tn=d, epi=epi, epi_ins=epi_ins, epi_outs=epi_outs)
    return res[0], (res[2] if nxt else None), (x, h, ab, act, res[1])


def _ffn_bwd(tag, dx_new, dy, saved, g, sc, w_in, w_out, start_rs, up, deps=()):
    x, h, ab, act, _ = saved
    d = x.shape[1]
    f = act.shape[1]
    dw_out = _mm(tag + "_dwout", act, dy, ta=True, tm=1408, tn=d, tk=2048, deps=deps)
    tok = start_rs("out", dw_out.reshape(N_CHIPS, f // N_CHIPS, d))
    epi, epi_ins, epi_outs = _dswiglu_epilogue(ab, x.shape[0], f, w_in.shape[2])
    dab = _mm(tag + "_dact", dy, w_out, tb=True, tm=512, tn=w_in.shape[2], epi=epi, epi_ins=epi_ins, epi_outs=epi_outs,
              deps=(tok,), j_outer=True)
    dw_in = _mm(tag + "_dwin", h, dab, ta=True, out_stacked=True, b_halves=True, tm=d, tn=1408, tk=2048)
    tok = start_rs("in", dw_in)
    dh = _mm(tag + "_dh", dab, w_in, tb=True, b_stacked=True, a_halves=True, tn=d, tk=5632, deps=(tok,))
    return _norm_bwd(tag + "_dnorm", dh, x, dx_new, g, sc, *(up or (None, None, None)))


def _row(v):
    return v.reshape(1, -1)


def _pack(parts):
    cols = []
    for p in parts:
        flat = p.reshape(-1).astype(F32)
        padn = (-flat.shape[0]) % 128
        cols.append(jnp.pad(flat, (0, padn)) if padn else flat)
    flat = jnp.concatenate(cols)
    padn = (-flat.shape[0]) % 1024
    if padn:
        flat = jnp.pad(flat, (0, padn))
    return flat.reshape(-1, 128)


def _unpack(packed, shapes):
    flat = packed.reshape(-1)
    out, off = [], 0
    for s in shapes:
        n = math.prod(s)
        out.append(flat[off:off + n].reshape(s))
        off += n + ((-n) % 128)
    return out


SMALL = ['b_ada', 'g_ffn1', 'g_mix', 'pool_w', 'pool_b', 'pool_scale', 'ssm_lam_re_log', 'ssm_lam_im', 'ssm_log_dt',
         'ssm_b_re', 'ssm_b_im', 'ssm_c_re', 'ssm_c_im', 'ssm_d', 'b_glu', 'g_ffn2', 'g_final']
BIG = ['w_ffn1_in', 'w_ffn1_out', 'w_in', 'w_pool_up', 'w_glu', 'w_ssm_up', 'w_out', 'w_ffn2_in', 'w_ffn2_out']
AG_GROUPS = [[0], [1], [2, 3, 4, 5, 6], [7, 8]]
SMALL_LATE = ['b_ada', 'g_ffn1']
WEIGHTS = ['w_ada', 'b_ada', 'g_ffn1', 'w_ffn1_in', 'w_ffn1_out', 'g_mix', 'w_in', 'pool_w', 'pool_b', 'pool_scale', 'w_pool_up',
           'ssm_lam_re_log', 'ssm_lam_im', 'ssm_log_dt', 'ssm_b_re', 'ssm_b_im', 'ssm_c_re', 'ssm_c_im', 'ssm_d', 'w_glu', 'b_glu',
           'w_ssm_up', 'w_out', 'g_ffn2', 'w_ffn2_in', 'w_ffn2_out', 'g_final']


def kernel(x, c, w_ada, b_ada, g_ffn1, w_ffn1_in, w_ffn1_out, g_mix, w_in, pool_w, pool_b, pool_scale, w_pool_up, ssm_lam_re_log, ssm_lam_im, ssm_log_dt, ssm_b_re, ssm_b_im, ssm_c_re, ssm_c_im, ssm_d, w_glu, b_glu, w_ssm_up, w_out, g_ffn2, w_ffn2_in, w_ffn2_out, g_final, loss_target, m_w_ada, m_b_ada, m_g_ffn1, m_w_ffn1_in, m_w_ffn1_out, m_g_mix, m_w_in, m_pool_w, m_pool_b, m_pool_scale, m_w_pool_up, m_ssm_lam_re_log, m_ssm_lam_im, m_ssm_log_dt, m_ssm_b_re, m_ssm_b_im, m_ssm_c_re, m_ssm_c_im, m_ssm_d, m_w_glu, m_b_glu, m_w_ssm_up, m_w_out, m_g_ffn2, m_w_ffn2_in, m_w_ffn2_out, m_g_final, v_w_ada, v_b_ada, v_g_ffn1, v_w_ffn1_in, v_w_ffn1_out, v_g_mix, v_w_in, v_pool_w, v_pool_b, v_pool_scale, v_w_pool_up, v_ssm_lam_re_log, v_ssm_lam_im, v_ssm_log_dt, v_ssm_b_re, v_ssm_b_im, v_ssm_c_re, v_ssm_c_im, v_ssm_d, v_w_glu, v_b_glu, v_w_ssm_up, v_w_out, v_g_ffn2, v_w_ffn2_in, v_w_ffn2_out, v_g_final):
    args = dict(locals())
    wt = {n: args[n] for n in WEIGHTS}
    mom = {n: args["m_" + n] for n in WEIGHTS}
    var = {n: args["v_" + n] for n in WEIGHTS}

    t, d = x.shape[1], x.shape[2]
    pw = pool_b.shape[1]
    sw = ssm_d.shape[1]
    ngrp = sw // SSM_GROUP
    gn = ngrp * SSM_STATE
    xi, yi, ci = _place()
    b_me = 4 * xi + 2 * yi + ci
    s_me = 2 * xi + yi
    x2d = x[0]
    tgt = loss_target[0]

    c_all = _allgather_small("ag_c", c.reshape(8, d // 8)).reshape(N_DEV, d)
    ncol = w_ada.shape[2]
    b_sh = lax.dynamic_slice(b_ada, (0, s_me * ncol), (1, ncol))
    mod_sh = _ada_fwd("ada_fwd", c_all, w_ada[0], b_sh)
    md_send, md_recv, mod_sh, md_land, tok = _bcast_start("mod_start", mod_sh)

    shards = [wt[n][0].astype(BF16) for n in BIG]
    ag_sems, shards_t, lands_t, tok = _ag_start("ag_start", shards, _place_shards(shards, s_me), AG_GROUPS, deps=(tok,))
    md_land = _bcast_wait("mod_wait", mod_sh, md_land, md_send, md_recv, tok)
    mod_all = lax.dynamic_update_slice(md_land, mod_sh[None], (b_me, 0, 0))
    full = {}

    def weights(gi, after):
        grp = AG_GROUPS[gi]
        if BIG[grp[0]] not in full:
            ls = _ag_wait("ag_wait%d" % gi, [shards_t[w] for w in grp], [lands_t[w] for w in grp], *ag_sems[gi], after)
            for w, l in zip(grp, _ag_forward("ag_fwd%d" % gi, ls)):
                full[BIG[w]] = l
        return full

    mod_me = jnp.concatenate([lax.dynamic_slice(mod_all, (2 * s, b_me, 0), (1, 1, ncol))[0] for s in range(N_CHIPS)], axis=1)
    mod = mod_me.reshape(9, d)
    sh1, sc1, gt1, sh2, sc2, gt2, sh3, sc3, gt3 = [mod[k:k + 1] for k in range(9)]

    f = w_ffn1_out.shape[1] * N_CHIPS

    h1 = _norm_fwd("ffn1_norm", x2d, g_ffn1, sc1, sh1, deps=(tok,))
    x1, h2, sav1 = _ffn_fwd("ffn1", x2d, h1, gt1, lambda after: weights(0, after)['w_ffn1_in'],
                            lambda after: weights(1, after)['w_ffn1_out'].reshape(f, d), (g_mix, sc2, sh2))
    weights(2, h2)
    wo = full['w_out'].reshape(d, d)
    uin = _mm("mix_in", h2, full['w_in'], b_stacked=True, tm=1024, tn=768, out_dtype=F32, j_outer=True)
    p_pool, z_pool = _pool_fwd("pool_fwd", uin, pool_w[0], pool_b, pool_scale, pw)
    y_pool = _mm("pool_up", p_pool, full['w_pool_up'], b_stacked=True, tm=1024, tn=d)

    col = lambda a: a.reshape(gn, 1)
    lrl_c, li_c = col(ssm_lam_re_log), col(ssm_lam_im)
    ldt_c = col(jnp.broadcast_to(ssm_log_dt.reshape(ngrp, 1), (ngrp, SSM_STATE)))
    b_re2, b_im2 = ssm_b_re.reshape(gn, SSM_GROUP), ssm_b_im.reshape(gn, SSM_GROUP)
    lrdt_c, ang_c, bb_re, bb_im = _ssm_prep("ssm_prep", lrl_c, li_c, ldt_c, b_re2, b_im2)
    lrdt, ang = lrdt_c.reshape(1, gn), ang_c.reshape(1, gn)
    bbd_re, bbd_im = _blockdiag_b(bb_re, sw), _blockdiag_b(bb_im, sw)
    ccd_re, ccd_im = _blockdiag_c(ssm_c_re[0], sw), _blockdiag_c(ssm_c_im[0], sw)
    y_s, ge, s_re, s_im = _ssm_fwd("ssm_fwd", uin, lrdt, ang, bbd_re.astype(BF16), bbd_im.astype(BF16),
                                   ccd_re.astype(BF16), ccd_im.astype(BF16), ssm_d, sw)
    gv = _mm("glu_in", ge, full['w_glu'], b_stacked=True, tm=1024, tn=d)
    sg = _glu_fwd("glu_act", gv, b_glu)
    y_ssm = _mm("ssm_up", sg, full['w_ssm_up'], b_stacked=True, tm=1024, tn=d)
    merged = _gates_fwd("gates", y_pool, y_ssm, uin, d, pw)
    epi, epi_ins, epi_outs = _gate_norm_epilogue(1.0, x1, gt2, (g_ffn2, sc3, sh3), t, d)
    x2, y2, h3 = _mm("mix_out", merged, wo, tm=512, tn=d, epi=epi, epi_ins=epi_ins, epi_outs=epi_outs)

    x3, _, sav3 = _ffn_fwd("ffn2", x2, h3, gt3, lambda after: weights(3, after)['w_ffn2_in'],
                           lambda after: weights(3, after)['w_ffn2_out'].reshape(f, d), None)

    dx3, dy3, dg_final, loss_v, dgt3 = _loss_bwd("loss", x3, tgt, _row(g_final), sav3[4], gt3, 0.5)

    gs = {}
    rs_open = []

    def start_rs(names, gbs):
        send, recv, g_thru, land_thru, token = _rs_start("rs_start_" + names[0], gbs)
        rs_open.append((names, send, recv, g_thru, land_thru))
        return token

    dx2, dy2, dsh3, dsc3, gs['g_ffn2'], dgt2 = _ffn_bwd(
        "ffn2b", dx3, dy3, sav3, g_ffn2, sc3, full['w_ffn2_in'], full['w_ffn2_out'].reshape(f, d),
        lambda key, g: start_rs(['w_ffn2_' + key], [g]), (y2, gt2, 1.0))

    dmerged = _mm("mix_dmerged", dy2, wo, tb=True, tn=d)
    g_wo = _mm("mix_dwout", merged, dy2, ta=True, tm=d, tn=d, tk=2048).reshape(N_CHIPS, d // N_CHIPS, d)
    dy_pool, dy_ssm, duin = _gates_bwd("gates_bwd", dmerged, y_pool, y_ssm, uin, d, pw)

    dp = _mm("pool_dup", dy_pool, full['w_pool_up'], tb=True, b_stacked=True, tm=1024, tn=pw, tk=d)
    g_wpu = _mm("pool_dwup", p_pool, dy_pool, ta=True, out_stacked=True, tm=pw, tn=d, tk=4096)
    duin, gs['pool_w'], gs['pool_b'], gs['pool_scale'] = _pool_bwd("pool_bwd", dp, z_pool, pool_w[0], pool_b, pool_scale, duin)

    dsg = _mm("ssm_dup", dy_ssm, full['w_ssm_up'], tb=True, b_stacked=True, tm=1024, tn=sw, tk=d)
    g_wsu = _mm("ssm_dwup", sg, dy_ssm, ta=True, out_stacked=True, tm=sw, tn=d, tk=4096)
    tok = start_rs(['w_out', 'w_pool_up', 'w_ssm_up'], [g_wo, g_wpu, g_wsu])
    dgv, gs['b_glu'] = _glu_bwd("glu_bwd", dsg, gv, b_glu, deps=(tok,))
    dge = _mm("glu_dge", dgv, full['w_glu'], tb=True, b_stacked=True, tm=1024, tn=sw, tk=d, out_dtype=F32)
    g_wglu = _mm("glu_dw", ge, dgv, ta=True, out_stacked=True, tm=sw, tn=d, tk=4096)
    tr = lambda a: jnp.swapaxes(a, 1, 2).astype(BF16)
    (duin, g_abre, g_abim, g_bbd_re, g_bbd_im, g_ccd_re, g_ccd_im, gs['ssm_d']) = _ssm_bwd(
        "ssm_bwd", dge, y_s, uin, s_re, s_im, lrdt, ang, tr(bbd_re), tr(bbd_im), tr(ccd_re), tr(ccd_im), ssm_d, sw, duin)
    gs['ssm_c_re'], gs['ssm_c_im'] = _diag_of_c(g_ccd_re, sw), _diag_of_c(g_ccd_im, sw)
    d_lrl, d_li, d_ldt, d_bre, d_bim = _ssm_param_bwd(
        "ssm_param_bwd", lrl_c, li_c, ldt_c, b_re2, b_im2, g_abre.reshape(gn, 1), g_abim.reshape(gn, 1),
        _diag_of_b(g_bbd_re, sw), _diag_of_b(g_bbd_im, sw))
    gs['ssm_lam_re_log'], gs['ssm_lam_im'] = d_lrl, d_li
    gs['ssm_log_dt'] = jnp.sum(d_ldt.reshape(ngrp, SSM_STATE), axis=1)
    gs['ssm_b_re'], gs['ssm_b_im'] = d_bre, d_bim

    g_win =_mm("mix_dwin", h2, duin, ta=True, out_stacked=True, tm=d, tn=768, tk=4096)
    tok = start_rs(['w_glu', 'w_in'], [g_wglu, g_win])
    dh2 = _mm("mix_dh", duin, full['w_in'], tb=True, b_stacked=True, tn=d, tk=3072, deps=(tok,))
    dx1, dy1, dsh2, dsc2, gs['g_mix'], dgt1 = _norm_bwd("mix_dnorm", dh2, x1, dx2, g_mix, sc2, sav1[4], gt1, 0.5)

    gs['g_final'] = dg_final
    early = [n for n in SMALL if n not in SMALL_LATE]
    sg_blk = _pack([gs[n] for n in early])
    sg_send, sg_recv, sg_blk, sg_land, tok = _bcast_start("sg_start", sg_blk)

    dx0, _, dsh1, dsc1, gs['g_ffn1'], _ = _ffn_bwd(
        "ffn1b", dx1, dy1, sav1, g_ffn1, sc1, full['w_ffn1_in'], full['w_ffn1_out'].reshape(f, d),
        lambda key, g: start_rs(['w_ffn1_' + key], [g]), None, deps=(tok,))

    gs['b_ada'] = jnp.concatenate([dsh1, dsc1, dgt1, dsh2, dsc2, dgt2, dsh3, dsc3, dgt3], axis=1)
    lt_blk = _pack([gs[n] for n in SMALL_LATE] + [loss_v])
    lt_send, lt_recv, lt_blk, lt_land, tok = _bcast_start("late_start", lt_blk)

    grads, delta, new_m, new_v = {}, {}, {}, {}

    def small_update(tag, names, blk, land, extra):
        g8 = lax.dynamic_update_slice(land, blk[None], (b_me, 0, 0)).reshape(-1, 128)
        res = _adamw_small("adamw_small_" + tag, _pack([wt[n] for n in names] + extra), g8,
                           _pack([mom[n] for n in names] + extra), _pack([var[n] for n in names] + extra))
        for dst, packed in zip((grads, delta, new_m, new_v), res):
            for n, val in zip(names, _unpack(packed, [wt[n].shape for n in names])):
                dst[n] = val
        return g8

    after = tok
    for k, (names, send, recv, g_thru, land_thru) in enumerate(rs_open):
        g_done, land_done = _rs_wait("rs_wait_" + names[0], g_thru, land_thru, send, recv, after)
        for n, g_own, land in zip(names, g_done, land_done):
            g_sum = _rs_sum("rs_sum_" + n, g_own, land)
            dl, mn, vn = _adamw("adamw_" + n, wt[n][0], g_sum, mom[n][0], var[n][0])
            grads[n], delta[n], new_m[n], new_v[n] = g_sum[None], dl[None], mn[None], vn[None]
            after = dl
        if k == len(rs_open) - 3:
            small_update("early", early, sg_blk, _bcast_wait("sg_wait", sg_blk, sg_land, sg_send, sg_recv, after), [])

    lt_land = _bcast_wait("late_wait", lt_blk, lt_land, lt_send, lt_recv, after)
    late8 = small_update("late", SMALL_LATE, lt_blk, lt_land, [jnp.zeros((128,), F32)]).reshape(N_DEV, -1)
    loss = jnp.sum(late8[:, 10 * d])
    dmod_sh = lax.dynamic_slice(late8, (0, s_me * ncol), (N_DEV, ncol))
    g_w_ada = _ada_bwd("ada_bwd", c_all, dmod_sh)
    dl, mn, vn = _adamw("adamw_w_ada", w_ada[0], g_w_ada, m_w_ada[0], v_w_ada[0])
    grads['w_ada'], delta['w_ada'], new_m['w_ada'], new_v['w_ada'] = g_w_ada[None], dl[None], mn[None], vn[None]

    return (loss, dx0[None], *[grads[n] for n in WEIGHTS], *[delta[n] for n in WEIGHTS],
            *[new_m[n] for n in WEIGHTS], *[new_v[n] for n in WEIGHTS])
```

```python
import functools
import math

import jax
import jax.numpy as jnp
from jax import lax
from jax.experimental import pallas as pl
from jax.experimental.pallas import tpu as pltpu

F32 = jnp.float32
BF16 = jnp.bfloat16
MESH = pl.DeviceIdType.MESH

EPS = 1e-6
POOL_WINDOWS = (2, 4, 8, 16)
POOL_HALO = 16
SSM_GROUP = 16
SSM_STATE = 64
SSM_BLOCKS = 4
N_DEV = 8
N_CHIPS = 4
ADAM_LR = 0.001
ADAM_B1 = 0.9
ADAM_B2 = 0.999
ADAM_EPS = 1e-08
ADAM_WD = 0.01
ADAM_STEP = 10
VMEM_LIMIT = 56 * 1024 * 1024


ANY_SPEC = pl.BlockSpec(memory_space=pl.ANY)
HBM_SPEC = pl.BlockSpec(memory_space=pltpu.HBM)
SEM_SPEC = pl.BlockSpec(memory_space=pltpu.SEMAPHORE)
EFFECT = pltpu.SideEffectType.DATAFLOW_SIDE_EFFECTING


def _hbm(a):
    return pltpu.with_memory_space_constraint(a, pltpu.HBM)


def _pcall(body, **kw):
    return pl.pallas_call(body, **kw)


def _params(*sem):
    return pltpu.CompilerParams(dimension_semantics=sem, vmem_limit_bytes=VMEM_LIMIT)


def _pick(n, cap, mult=128):
    if n <= cap:
        return n
    best = None
    for d in range(mult, cap + 1, mult):
        if n % d == 0:
            best = d
    assert best is not None, (n, cap, mult)
    return best


def _sigmoid(v):
    return 1.0 / (1.0 + jnp.exp(-v))


def _rowwise(name, fn, ins, params, outs, reds, tm, deps=()):
    t = ins[0][0].shape[0]
    tm = min(tm, t)
    nb = t // tm
    ni, npar, no, nd = len(ins), len(params), len(outs), len(deps)

    def body(*refs):
        iv = [r[...] for r in refs[:ni]]
        pv = [r[...] for r in refs[ni:ni + npar]]
        o_refs = refs[ni + npar + nd:ni + npar + nd + no]
        r_refs = refs[ni + npar + nd + no:]
        ovals, rvals = fn(iv, pv)
        for o_ref, val in zip(o_refs, ovals):
            off = 0
            if isinstance(val, tuple) and val[0] == "at":
                _, off, val = val
            parts = val if isinstance(val, (list, tuple)) else [val]
            for p in parts:
                o_ref[:, off:off + p.shape[1]] = p.astype(o_ref.dtype)
                off += p.shape[1]
        if r_refs:
            @pl.when(pl.program_id(0) == 0)
            def _():
                for r in r_refs:
                    r[...] = jnp.zeros_like(r)
            for r, val in zip(r_refs, rvals):
                r[...] += val

    in_specs = [pl.BlockSpec((tm, w), functools.partial(lambda i, cb: (i, cb), cb=cb)) for (_, w, cb) in ins]
    in_specs += [pl.BlockSpec(p.shape, lambda i: (0, 0)) for p in params]
    in_specs += [ANY_SPEC] * nd
    out_shape = [jax.ShapeDtypeStruct((t, w), dt) for (w, dt) in outs]
    out_shape += [jax.ShapeDtypeStruct((1, w), F32) for w in reds]
    out_specs = [pl.BlockSpec((tm, w), lambda i: (i, 0)) for (w, _) in outs]
    out_specs += [pl.BlockSpec((1, w), lambda i: (0, 0)) for w in reds]
    res = _pcall(body, name=name, grid=(nb,), in_specs=in_specs, out_specs=out_specs, out_shape=out_shape,
                 compiler_params=_params("arbitrary"))(*[a for a, _, _ in ins], *params, *deps)
    return res


def _colsum(v):
    return jnp.sum(v, axis=0, keepdims=True)


def _mm(name, a, b, *, ta=False, tb=False, b_stacked=False, out_stacked=False, tm=512, tn=1024, tk=2816,
        out_dtype=BF16, epi=None, epi_ins=(), epi_outs=None, deps=(), j_outer=False, a_halves=False, b_halves=False):
    if a_halves:
        _, m, kdim = a.shape
        kdim *= 2
    elif ta:
        kdim, m = a.shape
    else:
        m, kdim = a.shape
    ns = None
    if b_stacked:
        ns = b.shape[2]
        n = b.shape[1] if tb else N_CHIPS * ns
        assert kdim == (N_CHIPS * ns if tb else b.shape[1]), (name, a.shape, b.shape)
    else:
        if b_halves:
            n = 2 * b.shape[2]
            assert kdim == b.shape[1] and not tb, (name, a.shape, b.shape)
        else:
            n = b.shape[0] if tb else b.shape[1]
            assert kdim == (b.shape[1] if tb else b.shape[0]), (name, a.shape, b.shape)
        if out_stacked:
            ns = n // N_CHIPS

    def shards(want):
        return max(g for g in (1, 2, 4) if g * ns <= max(want, ns))

    tm = _pick(m, tm, 128 if ta else 8)
    gn = gk = 1
    if (b_stacked and not tb) or out_stacked:
        gn = shards(min(tn, n // 2) if b_halves else tn)
        tn = gn * ns
    else:
        tn = _pick(n, tn)
    if b_stacked and tb:
        gk = shards(tk)
        tk = gk * ns
    else:
        tk = _pick(kdim, tk, 8 if ta else 128)
    nm, nn, nk = m // tm, n // tn, kdim // tk

    def ij(f):
        return (lambda g0, g1, k: f(g1, g0, k)) if j_outer else f

    if a_halves:
        assert b_stacked and tb and gk == N_CHIPS and nk == 1, name
        a_spec = pl.BlockSpec((2, tm, kdim // 2), ij(lambda i, j, k: (0, i, 0)))
    elif ta:
        a_spec = pl.BlockSpec((tk, tm), ij(lambda i, j, k: (k, i)))
    else:
        a_spec = pl.BlockSpec((tm, tk), ij(lambda i, j, k: (i, k)))
    if b_stacked and not tb:
        b_spec = pl.BlockSpec((gn, tk, ns), ij(lambda i, j, k: (j, k, 0)))
    elif b_stacked and tb:
        b_spec = pl.BlockSpec((gk, tn, ns), ij(lambda i, j, k: (k, j, 0)))
    elif b_halves:
        bph = (n // 2) // tn
        b_spec = pl.BlockSpec((None, tk, tn), ij(lambda i, j, k: (j // bph, k, j % bph)))
    elif tb:
        b_spec = pl.BlockSpec((tn, tk), ij(lambda i, j, k: (j, k)))
    else:
        b_spec = pl.BlockSpec((tk, tn), ij(lambda i, j, k: (k, j)))
    dims = (((0 if ta else 1,), (1 if tb else 0,)), ((), ()))

    if epi_outs is None:
        if out_stacked:
            epi_outs = [((N_CHIPS, m, ns), out_dtype, (gn, tm, ns), lambda i, j: (j, i, 0))]
        else:
            epi_outs = [((m, n), out_dtype, (tm, tn), lambda i, j: (i, j))]
    ne, no, nd = len(epi_ins), len(epi_outs), len(deps)

    def body(a_ref, b_ref, *rest):
        e_refs = rest[:ne]
        o_refs = rest[ne + nd:ne + nd + no]
        scratch = rest[ne + nd + no:]
        av = None if a_halves else a_ref[...].astype(BF16)
        if b_stacked and not tb:
            parts = [lax.dot_general(av, b_ref[s].astype(BF16), dims, preferred_element_type=F32) for s in range(gn)]
        elif b_stacked and tb:
            p = None
            for s in range(gk):
                if a_halves:
                    a_s = a_ref[s // 2, :, (s % 2) * ns:(s % 2 + 1) * ns].astype(BF16)
                else:
                    a_s = av[:, s * ns:(s + 1) * ns]
                q = lax.dot_general(a_s, b_ref[s].astype(BF16), dims, preferred_element_type=F32)
                p = q if p is None else p + q
            parts = [p]
        else:
            parts = [lax.dot_general(av, b_ref[...].astype(BF16), dims, preferred_element_type=F32)]

        def finish(acc_parts):
            if epi is None and out_stacked:
                acc = acc_parts[0]
                for s in range(gn):
                    o_refs[0][s] = acc[:, s * ns:(s + 1) * ns].astype(out_dtype)
            elif epi is None:
                w = acc_parts[0].shape[1]
                for s, part in enumerate(acc_parts):
                    o_refs[0][:, s * w:(s + 1) * w] = part.astype(out_dtype)
            else:
                acc = acc_parts[0] if len(acc_parts) == 1 else jnp.concatenate(acc_parts, axis=1)
                for o_ref, val in zip(o_refs, epi(acc, *[r[...] for r in e_refs])):
                    o_ref[...] = val.astype(o_ref.dtype)

        if nk == 1:
            finish(parts)
        else:
            acc_ref = scratch[0]
            k = pl.program_id(2)
            w = parts[0].shape[1]

            @pl.when(k == 0)
            def _():
                for s, part in enumerate(parts):
                    acc_ref[:, s * w:(s + 1) * w] = part

            @pl.when(k > 0)
            def _():
                for s, part in enumerate(parts):
                    acc_ref[:, s * w:(s + 1) * w] += part

            @pl.when(k == nk - 1)
            def _():
                finish([acc_ref[...]])

    def _ij(f):
        return ij(lambda i, j, k: f(i, j))

    in_specs = [a_spec, b_spec] + [pl.BlockSpec(blk, _ij(f)) for (_, blk, f) in epi_ins] + [ANY_SPEC] * nd
    out_specs = [pl.BlockSpec(blk, _ij(f)) for (_, _, blk, f) in epi_outs]
    out_shape = [jax.ShapeDtypeStruct(s, dt) for (s, dt, _, _) in epi_outs]
    scratch = [pltpu.VMEM((tm, tn), F32)] if nk > 1 else []
    grid = (nn, nm, nk) if j_outer else (nm, nn, nk)
    res = _pcall(body, name=name, grid=grid, in_specs=in_specs, out_specs=out_specs, out_shape=out_shape,
                 scratch_shapes=scratch, compiler_params=_params("parallel", "parallel", "arbitrary"))(
                     a, b, *[x for x, _, _ in epi_ins], *deps)
    return res[0] if len(res) == 1 else res


def _norm_fwd(name, x, g, sc, sh, deps=()):
    d = x.shape[1]

    def fn(iv, pv):
        (xv,), (gv, scv, shv) = iv, pv
        r = lax.rsqrt(jnp.mean(xv * xv, axis=-1, keepdims=True) + EPS)
        return [xv * r * gv * (1.0 + scv) + shv], []

    return _rowwise(name, fn, [(x, d, 0)], [g, sc, sh], [(d, BF16)], [], 512, deps=deps)[0]


def _norm_bwd(name, dh, x, dres, g, sc, y_up, gt_up, factor_up):
    d = x.shape[1]
    up = y_up is not None

    def fn(iv, pv):
        dhv, xv, drv = iv[:3]
        gv, scv = pv[:2]
        dhv = dhv.astype(F32)
        r = lax.rsqrt(jnp.mean(xv * xv, axis=-1, keepdims=True) + EPS)
        xr = xv * r
        dn = dhv * (1.0 + scv)
        gd = gv * dn
        dx = drv + r * (gd - xr * jnp.mean(gd * xr, axis=-1, keepdims=True))
        outs, reds = [dx], [_colsum(dhv), _colsum(dhv * xr * gv), _colsum(dn * xr)]
        if up:
            fdx = factor_up * dx
            outs.append(pv[2] * fdx)
            reds.append(_colsum(fdx * iv[3].astype(F32)))
        return outs, reds

    ins = [(dh, d, 0), (x, d, 0), (dres, d, 0)] + ([(y_up, d, 0)] if up else [])
    res = _rowwise(name, fn, ins, [g, sc] + ([gt_up] if up else []), [(d, F32)] + ([(d, BF16)] if up else []),
                   [d] * (4 if up else 3), 256)
    if up:
        dx, dy_up, dsh, dsc, dg, dgt_up = res
        return dx, dy_up, dsh, dsc, dg, dgt_up
    dx, dsh, dsc, dg = res
    return dx, None, dsh, dsc, dg, None


def _loss_bwd(name, x, tgt, g, y_up, gt_up, factor_up):
    d = x.shape[1]

    def fn(iv, pv):
        (xv, tv, yv), (gv, gtv) = iv, pv
        r = lax.rsqrt(jnp.mean(xv * xv, axis=-1, keepdims=True) + EPS)
        xr = xv * r
        e = xr * gv - tv
        loss_row = 0.5 * jnp.mean(e * e, axis=-1, keepdims=True)
        dout = e * (1.0 / d)
        gd = gv * dout
        dx = r * (gd - xr * jnp.mean(gd * xr, axis=-1, keepdims=True))
        fdx = factor_up * dx
        return [dx, gtv * fdx], [_colsum(dout * xr), _colsum(loss_row * jnp.ones((1, 128), F32)), _colsum(fdx * yv.astype(F32))]

    return _rowwise(name, fn, [(x, d, 0), (tgt, d, 0), (y_up, d, 0)], [g, gt_up], [(d, F32), (d, BF16)], [d, 128, d], 256)


def _resgate_bwd(name, dx, y, gt, factor, deps=()):
    d = dx.shape[1]

    def fn(iv, pv):
        (dxv, yv), (gtv,) = iv, pv
        return [factor * gtv * dxv], [_colsum(factor * dxv * yv.astype(F32))]

    return _rowwise(name, fn, [(dx, d, 0), (y, d, 0)], [gt], [(d, BF16)], [d], 512, deps=deps)


def _swiglu_fwd(name, ab):
    f = ab.shape[1] // 2

    def fn(iv, pv):
        a, b = iv[0].astype(F32), iv[1].astype(F32)
        return [a * _sigmoid(a) * b], []

    return _rowwise(name, fn, [(ab, f, 0), (ab, f, 1)], [], [(f, BF16)], [], 256)[0]


def _swiglu_bwd(name, dact, ab, deps=()):
    f = ab.shape[1] // 2

    def fn(iv, pv):
        dv, a, b = iv[0].astype(F32), iv[1].astype(F32), iv[2].astype(F32)
        s = _sigmoid(a)
        return [[dv * b * (s * (1.0 + a * (1.0 - s))), dv * (a * s)]], []

    return _rowwise(name, fn, [(dact, f, 0), (ab, f, 0), (ab, f, 1)], [], [(2 * f, BF16)], [], 256, deps=deps)[0]


def _gates_fwd(name, y_pool, y_ssm, uin, d, pw):
    cb = (2 * pw) // d

    def fn(iv, pv):
        yp, ys, glp, gls = [v.astype(F32) for v in iv]
        return [_sigmoid(glp) * yp + _sigmoid(gls) * ys], []

    return _rowwise(name, fn, [(y_pool, d, 0), (y_ssm, d, 0), (uin, d, cb), (uin, d, cb + 1)], [], [(d, BF16)], [], 256)[0]


def _gates_bwd(name, dm, y_pool, y_ssm, uin, d, pw):
    cb = (2 * pw) // d

    def fn(iv, pv):
        dmv, yp, ys, glp, gls = [v.astype(F32) for v in iv]
        sp, ss = _sigmoid(glp), _sigmoid(gls)
        return [dmv * sp, dmv * ss, ("at", 2 * pw, [dmv * yp * sp * (1.0 - sp), dmv * ys * ss * (1.0 - ss)])], []

    return _rowwise(name, fn, [(dm, d, 0), (y_pool, d, 0), (y_ssm, d, 0), (uin, d, cb), (uin, d, cb + 1)], [],
                    [(d, BF16), (d, BF16), (2 * pw + 2 * d, BF16)], [], 256)


def _glu_fwd(name, gv, b_glu):
    w = gv.shape[1] // 2

    def fn(iv, pv):
        gvv, (bv,) = iv[0].astype(F32) + pv[0], pv
        return [gvv[:, :w] * _sigmoid(gvv[:, w:])], []

    return _rowwise(name, fn, [(gv, 2 * w, 0)], [b_glu], [(w, BF16)], [], 512)[0]


def _glu_bwd(name, dsg, gv, b_glu, deps=()):
    w = gv.shape[1] // 2

    def fn(iv, pv):
        dv = iv[0].astype(F32)
        gvv = iv[1].astype(F32) + pv[0]
        val, s = gvv[:, :w], _sigmoid(gvv[:, w:])
        dval = dv * s
        dgate = dv * val * s * (1.0 - s)
        return [[dval, dgate]], [jnp.concatenate([_colsum(dval), _colsum(dgate)], axis=1)]

    return _rowwise(name, fn, [(dsg, w, 0), (gv, 2 * w, 0)], [b_glu], [(2 * w, BF16)], [2 * w], 512, deps=deps)


def _adamw(name, w, g, m, v, tm=256):
    c = w.shape[1]

    def fn(iv, pv):
        wv, gv, mv, vv = iv
        mn = ADAM_B1 * mv + (1.0 - ADAM_B1) * gv
        vn = ADAM_B2 * vv + (1.0 - ADAM_B2) * (gv * gv)
        m_hat = mn / (1.0 - ADAM_B1 ** ADAM_STEP)
        v_hat = vn / (1.0 - ADAM_B2 ** ADAM_STEP)
        delta = -ADAM_LR * (m_hat / (jnp.sqrt(v_hat) + ADAM_EPS) + ADAM_WD * wv)
        return [delta, mn, vn], []

    return _rowwise(name, fn, [(w, c, 0), (g, c, 0), (m, c, 0), (v, c, 0)], [], [(c, F32)] * 3, [], _pick(w.shape[0], tm, 8))


def _adamw_small(name, w, g8, m, v):
    r = w.shape[0]

    def body(w_ref, g_ref, m_ref, v_ref, go_ref, d_ref, mo_ref, vo_ref):
        gv = g_ref[0:r, :]
        for k in range(1, N_DEV):
            gv = gv + g_ref[k * r:(k + 1) * r, :]
        mn = ADAM_B1 * m_ref[...] + (1.0 - ADAM_B1) * gv
        vn = ADAM_B2 * v_ref[...] + (1.0 - ADAM_B2) * (gv * gv)
        m_hat = mn / (1.0 - ADAM_B1 ** ADAM_STEP)
        v_hat = vn / (1.0 - ADAM_B2 ** ADAM_STEP)
        go_ref[...] = gv
        d_ref[...] = -ADAM_LR * (m_hat / (jnp.sqrt(v_hat) + ADAM_EPS) + ADAM_WD * w_ref[...])
        mo_ref[...] = mn
        vo_ref[...] = vn

    return _pcall(body, name=name, out_shape=[jax.ShapeDtypeStruct((r, 128), F32)] * 4,
                  compiler_params=pltpu.CompilerParams(vmem_limit_bytes=VMEM_LIMIT))(w, g8, m, v)


def _pool_fwd(name, uin, pool_w, pool_b, pool_scale, pw, tm=512):
    t = uin.shape[0]
    tm = min(tm, t)
    ng = len(POOL_WINDOWS)
    gw = pw // ng

    def body(u_ref, w_ref, b_ref, s_ref, p_ref, z_ref, ext):
        i = pl.program_id(0)

        @pl.when(i == 0)
        def _():
            ext[0:POOL_HALO, :] = jnp.zeros((POOL_HALO, pw), F32)

        u = u_ref[...]
        ext[POOL_HALO:POOL_HALO + tm, :] = u
        pos = i * tm + lax.broadcasted_iota(jnp.int32, (tm, 1), 0)
        for k, win in enumerate(POOL_WINDOWS):
            cols = slice(k * gw, (k + 1) * gw)
            acc = u[:, cols]
            for j in range(1, win):
                acc = acc + ext[POOL_HALO - j:POOL_HALO - j + tm, cols]
            cnt = jnp.minimum(pos + 1, win).astype(F32)
            z = acc / cnt - u[:, cols]
            zp = jnp.dot(z.astype(BF16), w_ref[k].astype(BF16), preferred_element_type=F32) + b_ref[:, cols]
            p_ref[:, cols] = (zp * s_ref[:, cols]).astype(BF16)
            z_ref[:, cols] = z.astype(BF16)
        ext[0:POOL_HALO, :] = u[tm - POOL_HALO:tm, :]

    return _pcall(
        body, name=name, grid=(t // tm,),
        in_specs=[pl.BlockSpec((tm, pw), lambda i: (i, 0)), pl.BlockSpec(pool_w.shape, lambda i: (0, 0, 0)),
                  pl.BlockSpec(pool_b.shape, lambda i: (0, 0)), pl.BlockSpec(pool_scale.shape, lambda i: (0, 0))],
        out_specs=[pl.BlockSpec((tm, pw), lambda i: (i, 0))] * 2,
        out_shape=[jax.ShapeDtypeStruct((t, pw), BF16)] * 2,
        scratch_shapes=[pltpu.VMEM((POOL_HALO + tm, pw), F32)],
        compiler_params=_params("arbitrary"))(uin, pool_w, pool_b, pool_scale)


def _pool_bwd(name, dp, z, pool_w, pool_b, pool_scale, duin, tm=512):
    t, pw = z.shape
    tm = min(tm, t)
    nb = t // tm
    ng = len(POOL_WINDOWS)
    gw = pw // ng

    def body(dp_ref, z_ref, w_ref, b_ref, s_ref, duin_ref, du_ref, dw_ref, db_ref, ds_ref, ext):
        i = pl.program_id(0)

        @pl.when(i == 0)
        def _():
            ext[tm:tm + POOL_HALO, :] = jnp.zeros((POOL_HALO, pw), F32)
            dw_ref[...] = jnp.zeros_like(dw_ref)
            db_ref[...] = jnp.zeros_like(db_ref)
            ds_ref[...] = jnp.zeros_like(ds_ref)

        pos = (nb - 1 - i) * tm + lax.broadcasted_iota(jnp.int32, (tm, 1), 0)
        for k, win in enumerate(POOL_WINDOWS):
            cols = slice(k * gw, (k + 1) * gw)
            zk = z_ref[:, cols]
            dpk = dp_ref[:, cols].astype(F32)
            wk = w_ref[k].astype(BF16)
            zp = jnp.dot(zk, wk, preferred_element_type=F32) + b_ref[:, cols]
            ds_ref[:, cols] += _colsum(dpk * zp)
            dzp = dpk * s_ref[:, cols]
            db_ref[:, cols] += _colsum(dzp)
            dzpb = dzp.astype(BF16)
            dz = lax.dot_general(dzpb, wk, (((1,), (1,)), ((), ())), preferred_element_type=F32)
            dw_ref[k] += lax.dot_general(zk, dzpb, (((0,), (0,)), ((), ())), preferred_element_type=F32)
            cnt = jnp.minimum(pos + 1, win).astype(F32)
            r = dz / cnt
            ext[0:tm, cols] = r
            acc = r - dz
            for j in range(1, win):
                acc = acc + ext[j:j + tm, cols]
            du_ref[:, cols] = acc.astype(BF16)
        ext[tm:tm + POOL_HALO, :] = ext[0:POOL_HALO, :]

    rev = lambda i: (nb - 1 - i, 0)
    return _pcall(
        body, name=name, grid=(nb,),
        in_specs=[pl.BlockSpec((tm, pw), rev), pl.BlockSpec((tm, pw), rev), pl.BlockSpec(pool_w.shape, lambda i: (0, 0, 0)),
                  pl.BlockSpec(pool_b.shape, lambda i: (0, 0)), pl.BlockSpec(pool_scale.shape, lambda i: (0, 0)), ANY_SPEC],
        out_specs=[pl.BlockSpec((tm, pw), rev), pl.BlockSpec(pool_w.shape, lambda i: (0, 0, 0)),
                   pl.BlockSpec((1, pw), lambda i: (0, 0)), pl.BlockSpec((1, pw), lambda i: (0, 0))],
        out_shape=[jax.ShapeDtypeStruct(duin.shape, BF16), jax.ShapeDtypeStruct(pool_w.shape, F32),
                   jax.ShapeDtypeStruct((1, pw), F32), jax.ShapeDtypeStruct((1, pw), F32)],
        scratch_shapes=[pltpu.VMEM((tm + POOL_HALO, pw), F32)], input_output_aliases={5: 0},
        compiler_params=_params("arbitrary"))(dp, z, pool_w, pool_b, pool_scale, duin)


def _ssm_disc(lrl, li, ldt):
    lr = -jnp.exp(lrl)
    dt = jnp.exp(ldt)
    mag = jnp.exp(lr * dt)
    ang = li * dt
    ab_re = mag * jnp.cos(ang)
    ab_im = mag * jnp.sin(ang)
    num_re = ab_re - 1.0
    num_im = ab_im
    den = lr * lr + li * li
    f_re = (num_re * lr + num_im * li) / den
    f_im = (num_im * lr - num_re * li) / den
    return lr, dt, mag, ang, ab_re, ab_im, num_re, num_im, den, f_re, f_im


def _ssm_prep(name, lrl, li, ldt, b_re, b_im):
    gn, h = b_re.shape

    def body(lrl_ref, li_ref, ldt_ref, br_ref, bi_ref, lrdt_ref, ang_ref, bbr_ref, bbi_ref):
        lr, dt, _, ang, _, _, _, _, _, f_re, f_im = _ssm_disc(lrl_ref[...], li_ref[...], ldt_ref[...])
        lrdt_ref[...] = lr * dt
        ang_ref[...] = ang
        br, bi = br_ref[...], bi_ref[...]
        bbr_ref[...] = f_re * br - f_im * bi
        bbi_ref[...] = f_re * bi + f_im * br

    col = jax.ShapeDtypeStruct((gn, 1), F32)
    mat = jax.ShapeDtypeStruct((gn, h), F32)
    return _pcall(body, name=name, out_shape=[col, col, mat, mat])(lrl, li, ldt, b_re, b_im)


def _ssm_param_bwd(name, lrl, li, ldt, b_re, b_im, g_abre, g_abim, g_bbre, g_bbim):
    gn, h = b_re.shape

    def body(lrl_ref, li_ref, ldt_ref, br_ref, bi_ref, gar_ref, gai_ref, gbr_ref, gbi_ref,
             dlrl_ref, dli_ref, dldt_ref, dbr_ref, dbi_ref):
        li_v = li_ref[...]
        lr, dt, mag, ang, ab_re, ab_im, num_re, num_im, den, f_re, f_im = _ssm_disc(lrl_ref[...], li_v, ldt_ref[...])
        br, bi = br_ref[...], bi_ref[...]
        gbr, gbi = gbr_ref[...], gbi_ref[...]
        g_fre = jnp.sum(gbr * br + gbi * bi, axis=1, keepdims=True)
        g_fim = jnp.sum(gbi * br - gbr * bi, axis=1, keepdims=True)
        dbr_ref[...] = gbr * f_re + gbi * f_im
        dbi_ref[...] = gbi * f_re - gbr * f_im
        g_num_re = (g_fre * lr - g_fim * li_v) / den
        g_num_im = (g_fre * li_v + g_fim * lr) / den
        g_den = -(g_fre * f_re + g_fim * f_im) / den
        g_lr = (g_fre * num_re + g_fim * num_im) / den + g_den * 2.0 * lr
        g_li = (g_fre * num_im - g_fim * num_re) / den + g_den * 2.0 * li_v
        g_are = gar_ref[...] + g_num_re
        g_aim = gai_ref[...] + g_num_im
        g_mag = g_are * jnp.cos(ang) + g_aim * jnp.sin(ang)
        g_ang = g_aim * ab_re - g_are * ab_im
        g_lrdt = g_mag * mag
        g_lr = g_lr + g_lrdt * dt
        g_dt = g_lrdt * lr + g_ang * li_v
        g_li = g_li + g_ang * dt
        dlrl_ref[...] = g_lr * lr
        dli_ref[...] = g_li
        dldt_ref[...] = g_dt * dt

    col = jax.ShapeDtypeStruct((gn, 1), F32)
    mat = jax.ShapeDtypeStruct((gn, h), F32)
    return _pcall(body, name=name, out_shape=[col, col, col, mat, mat])(lrl, li, ldt, b_re, b_im, g_abre, g_abim, g_bbre, g_bbim)


def _pow_rows(lrdt, ang, k):
    mag = jnp.exp(k * lrdt)
    return mag * jnp.cos(k * ang), mag * jnp.sin(k * ang)


def _ssm_chunk(t):
    return 256 if t >= 2048 else 128


def _to_segments(dst, srcs, jn):
    for q, src in enumerate(srcs):
        for j in range(jn):
            dst[8 * j:8 * j + 8, 128 * q:128 * (q + 1)] = src[pl.ds(j, 8, stride=jn), :]


def _from_segments(dst, src, q, jn, dtype):
    for s in range(8):
        dst[s * jn:(s + 1) * jn, 128 * q:128 * (q + 1)] = src[q, pl.ds(s, jn, stride=8), :].astype(dtype)


def _fill_rows8(dst_re, dst_im, v_re, v_im):
    for j in range(v_re.shape[0]):
        dst_re[8 * j:8 * j + 8, :] = jnp.broadcast_to(v_re[j:j + 1, :], (8, v_re.shape[1]))
        dst_im[8 * j:8 * j + 8, :] = jnp.broadcast_to(v_im[j:j + 1, :], (8, v_im.shape[1]))


def _cmul(ar, ai, br, bi):
    return ar * br - ai * bi, ar * bi + ai * br


def _cmul_conj(ar, ai, br, bi):
    return ar * br + ai * bi, ar * bi - ai * br


def _ssm_fwd(name, uin, lrdt, ang, bb_re, bb_im, cc_re, cc_im, d_skip, sw):
    t = uin.shape[0]
    gn = lrdt.shape[1]
    lc = _ssm_chunk(t)
    jn = lc // 8
    ub, sb = sw // SSM_BLOCKS, gn // SSM_BLOCKS
    nq = sw // 128
    assert ub == 128 and nq == SSM_BLOCKS

    def body(*refs):
        u_refs = refs[:nq]
        (lrdt_ref, ang_ref, bbr_ref, bbi_ref, ccr_ref, cci_ref, d_ref, y_ref, ge_ref, sre_ref, sim_ref,
         p_re, p_im, a_re, a_im, up, yp, gp, cst_re, cst_im, car_re, car_im) = refs[nq:]
        i = pl.program_id(0)
        lrdt_v, ang_v = lrdt_ref[...], ang_ref[...]

        @pl.when(i == 0)
        def _():
            k = (lax.broadcasted_iota(jnp.int32, (jn, 1), 0) + 1).astype(F32)
            _fill_rows8(p_re, p_im, *_pow_rows(lrdt_v, ang_v, k))
            car_re[...] = jnp.zeros_like(car_re)
            car_im[...] = jnp.zeros_like(car_im)

        _to_segments(up, u_refs, jn)
        u = up[...]
        ubf = u.astype(BF16)
        for q in range(SSM_BLOCKS):
            uq = ubf[:, q * ub:(q + 1) * ub]
            a_re[:, q * sb:(q + 1) * sb] = jnp.dot(uq, bbr_ref[q], preferred_element_type=F32)
            a_im[:, q * sb:(q + 1) * sb] = jnp.dot(uq, bbi_ref[q], preferred_element_type=F32)
        a1r, a1i = _pow_rows(lrdt_v, ang_v, 1.0)
        ajr, aji = _pow_rows(lrdt_v, ang_v, float(jn))
        for q in range(SSM_BLOCKS):
            cols = slice(q * sb, (q + 1) * sb)
            ar8 = jnp.broadcast_to(a1r[:, cols], (8, sb))
            ai8 = jnp.broadcast_to(a1i[:, cols], (8, sb))

            def step(j, carry, cols=cols, ar8=ar8, ai8=ai8):
                sr, si = carry
                rows = pl.ds(pl.multiple_of(j * 8, 8), 8)
                mr, mi = _cmul(ar8, ai8, sr, si)
                nr, ni = mr + a_re[rows, cols], mi + a_im[rows, cols]
                a_re[rows, cols] = nr
                a_im[rows, cols] = ni
                return nr, ni

            lax.fori_loop(1, jn, step, (a_re[0:8, cols], a_im[0:8, cols]), unroll=4)
        er, ei = a_re[lc - 8:lc, :], a_im[lc - 8:lc, :]
        gr, gi = car_re[...], car_im[...]
        for s in range(8):
            cst_re[s:s + 1, :] = gr
            cst_im[s:s + 1, :] = gi
            mr, mi = _cmul(ajr, aji, gr, gi)
            gr, gi = mr + er[s:s + 1, :], mi + ei[s:s + 1, :]
        car_re[...] = gr
        car_im[...] = gi
        for q in range(SSM_BLOCKS):
            cols = slice(q * sb, (q + 1) * sb)
            cr = jnp.tile(cst_re[:, cols], (jn, 1))
            ci = jnp.tile(cst_im[:, cols], (jn, 1))
            mr, mi = _cmul(p_re[:, cols], p_im[:, cols], cr, ci)
            srb, sib = (a_re[:, cols] + mr).astype(BF16), (a_im[:, cols] + mi).astype(BF16)
            sre_ref[:, cols] = srb
            sim_ref[:, cols] = sib
            ycols = slice(q * ub, (q + 1) * ub)
            y = (jnp.dot(srb, ccr_ref[q], preferred_element_type=F32) - jnp.dot(sib, cci_ref[q], preferred_element_type=F32)
                 + d_ref[:, ycols] * u[:, ycols])
            yp[q] = y
            gp[q] = 0.5 * y * (1.0 + lax.erf(y * (1.0 / math.sqrt(2.0))))
            _from_segments(y_ref, yp, q, jn, F32)
            _from_segments(ge_ref, gp, q, jn, BF16)

    row = lambda i: (0, 0)
    blk3 = lambda i: (0, 0, 0)
    return _pcall(
        body, name=name, grid=(t // lc,),
        in_specs=[pl.BlockSpec((lc, 128), functools.partial(lambda i, q: (i, nq + q), q=q)) for q in range(nq)]
        + [pl.BlockSpec((1, gn), row), pl.BlockSpec((1, gn), row),
           pl.BlockSpec(bb_re.shape, blk3), pl.BlockSpec(bb_im.shape, blk3),
           pl.BlockSpec(cc_re.shape, blk3), pl.BlockSpec(cc_im.shape, blk3), pl.BlockSpec((1, sw), row)],
        out_specs=[pl.BlockSpec((lc, sw), lambda i: (i, 0)), pl.BlockSpec((lc, sw), lambda i: (i, 0)),
                   pl.BlockSpec((lc, gn), lambda i: (i, 0)), pl.BlockSpec((lc, gn), lambda i: (i, 0))],
        out_shape=[jax.ShapeDtypeStruct((t, sw), F32), jax.ShapeDtypeStruct((t, sw), BF16),
                   jax.ShapeDtypeStruct((t, gn), BF16), jax.ShapeDtypeStruct((t, gn), BF16)],
        scratch_shapes=[pltpu.VMEM((lc, gn), F32), pltpu.VMEM((lc, gn), F32), pltpu.VMEM((lc, gn), F32), pltpu.VMEM((lc, gn), F32),
                        pltpu.VMEM((lc, sw), F32), pltpu.VMEM((nq, lc, 128), F32), pltpu.VMEM((nq, lc, 128), F32),
                        pltpu.VMEM((8, gn), F32), pltpu.VMEM((8, gn), F32), pltpu.VMEM((1, gn), F32), pltpu.VMEM((1, gn), F32)],
        compiler_params=_params("arbitrary"))(*([uin] * nq), lrdt, ang, bb_re, bb_im, cc_re, cc_im, d_skip)


def _ssm_bwd(name, dge, y, uin, s_re, s_im, lrdt, ang, bbt_re, bbt_im, cct_re, cct_im, d_skip, sw, duin):
    t = uin.shape[0]
    gn = lrdt.shape[1]
    lc = _ssm_chunk(t)
    nb = t // lc
    jn = lc // 8
    ub, sb = sw // SSM_BLOCKS, gn // SSM_BLOCKS
    nq = sw // 128
    tail = 16

    def body(*refs):
        dge_refs, y_refs, u_refs = refs[:nq], refs[nq:2 * nq], refs[2 * nq:3 * nq]
        (sre_ref, sim_ref, tre_ref, tim_ref, lrdt_ref, ang_ref, btr_ref, bti_ref, ctr_ref, cti_ref, d_ref, duin_ref,
         du_ref, dar_ref, dai_ref, dbr_ref, dbi_ref, dcr_ref, dci_ref, dd_ref,
         q_re, q_im, a_re, a_im, dgp, yp, up, dup, cst_re, cst_im, sp_re, sp_im, car_re, car_im) = refs[3 * nq:]
        i = pl.program_id(0)
        lrdt_v, ang_v = lrdt_ref[...], ang_ref[...]

        @pl.when(i == 0)
        def _():
            k = (jn - lax.broadcasted_iota(jnp.int32, (jn, 1), 0)).astype(F32)
            _fill_rows8(q_re, q_im, *_pow_rows(lrdt_v, ang_v, k))
            car_re[...] = jnp.zeros_like(car_re)
            car_im[...] = jnp.zeros_like(car_im)
            for r in (dar_ref, dai_ref, dbr_ref, dbi_ref, dcr_ref, dci_ref, dd_ref):
                r[...] = jnp.zeros_like(r)

        _to_segments(dgp, dge_refs, jn)
        _to_segments(yp, y_refs, jn)
        _to_segments(up, u_refs, jn)
        yv = yp[...]
        u = up[...]
        cdf = 0.5 * (1.0 + lax.erf(yv * (1.0 / math.sqrt(2.0))))
        pdf = jnp.exp(-0.5 * yv * yv) * (1.0 / math.sqrt(2.0 * math.pi))
        dy = dgp[...] * (cdf + yv * pdf)
        dd_ref[...] += _colsum(dy * u)
        dyb = dy.astype(BF16)
        ubf = u.astype(BF16)
        for q in range(SSM_BLOCKS):
            dq = dyb[:, q * ub:(q + 1) * ub]
            a_re[:, q * sb:(q + 1) * sb] = jnp.dot(dq, ctr_ref[q], preferred_element_type=F32)
            a_im[:, q * sb:(q + 1) * sb] = -jnp.dot(dq, cti_ref[q], preferred_element_type=F32)
        a1r, a1i = _pow_rows(lrdt_v, ang_v, 1.0)
        ajr, aji = _pow_rows(lrdt_v, ang_v, float(jn))
        for q in range(SSM_BLOCKS):
            cols = slice(q * sb, (q + 1) * sb)
            ar8 = jnp.broadcast_to(a1r[:, cols], (8, sb))
            ai8 = jnp.broadcast_to(a1i[:, cols], (8, sb))

            def step(jj, carry, cols=cols, ar8=ar8, ai8=ai8):
                sr, si = carry
                rows = pl.ds(pl.multiple_of((jn - 2 - jj) * 8, 8), 8)
                mr, mi = _cmul_conj(ar8, ai8, sr, si)
                nr, ni = mr + a_re[rows, cols], mi + a_im[rows, cols]
                a_re[rows, cols] = nr
                a_im[rows, cols] = ni
                return nr, ni

            lax.fori_loop(0, jn - 1, step, (a_re[lc - 8:lc, cols], a_im[lc - 8:lc, cols]), unroll=4)
        er, ei = a_re[0:8, :], a_im[0:8, :]
        hr, hi = car_re[...], car_im[...]
        for s in range(7, -1, -1):
            cst_re[s:s + 1, :] = hr
            cst_im[s:s + 1, :] = hi
            mr, mi = _cmul_conj(ajr, aji, hr, hi)
            hr, hi = mr + er[s:s + 1, :], mi + ei[s:s + 1, :]
        car_re[...] = hr
        car_im[...] = hi
        first = (i == nb - 1).astype(F32)
        sp_re[0:tail, :] = tre_ref[...].astype(F32) * (1.0 - first)
        sp_im[0:tail, :] = tim_ref[...].astype(F32) * (1.0 - first)
        sp_re[tail:tail + 8, :] = sre_ref[lc - tail:lc, :].astype(F32)[tail - 8:tail]
        sp_im[tail:tail + 8, :] = sim_ref[lc - tail:lc, :].astype(F32)[tail - 8:tail]
        tn_dims = (((0,), (0,)), ((), ()))
        for q in range(SSM_BLOCKS):
            cols = slice(q * sb, (q + 1) * sb)
            ycols = slice(q * ub, (q + 1) * ub)
            cr = jnp.tile(cst_re[:, cols], (jn, 1))
            ci = jnp.tile(cst_im[:, cols], (jn, 1))
            mr, mi = _cmul_conj(q_re[:, cols], q_im[:, cols], cr, ci)
            lam_r, lam_i = a_re[:, cols] + mr, a_im[:, cols] + mi
            s_r, s_i = sre_ref[:, cols], sim_ref[:, cols]
            p0r, p0i = sp_re[tail - 1:tail + 7, cols], sp_im[tail - 1:tail + 7, cols]
            l0r, l0i, l1r, l1i = lam_r[0:8], lam_i[0:8], lam_r[8:lc], lam_i[8:lc]
            pvr, pvi = s_r.astype(F32)[0:lc - 8], s_i.astype(F32)[0:lc - 8]
            dar_ref[:, cols] += _colsum(l1r * pvr + l1i * pvi) + _colsum(l0r * p0r + l0i * p0i)
            dai_ref[:, cols] += _colsum(l1i * pvr - l1r * pvi) + _colsum(l0i * p0r - l0r * p0i)
            lrb, lib = lam_r.astype(BF16), lam_i.astype(BF16)
            dup[q] = (jnp.dot(lrb, btr_ref[q], preferred_element_type=F32)
                      + jnp.dot(lib, bti_ref[q], preferred_element_type=F32) + d_ref[:, ycols] * dy[:, ycols])
            _from_segments(du_ref, dup, q, jn, BF16)
            uq = ubf[:, ycols]
            dbr_ref[q] += lax.dot_general(uq, lrb, tn_dims, preferred_element_type=F32)
            dbi_ref[q] += lax.dot_general(uq, lib, tn_dims, preferred_element_type=F32)
            dq = dyb[:, ycols]
            dcr_ref[q] += lax.dot_general(s_r.astype(BF16), dq, tn_dims, preferred_element_type=F32)
            dci_ref[q] -= lax.dot_general(s_i.astype(BF16), dq, tn_dims, preferred_element_type=F32)

    rev = lambda i: (nb - 1 - i, 0)
    tailmap = lambda i: (jnp.maximum((nb - 1 - i) * (lc // tail) - 1, 0), 0)
    row = lambda i: (0, 0)
    blk3 = lambda i: (0, 0, 0)
    colblk = lambda base: [pl.BlockSpec((lc, 128), functools.partial(lambda i, q: (nb - 1 - i, base + q), q=q)) for q in range(nq)]
    return _pcall(
        body, name=name, grid=(nb,),
        in_specs=colblk(0) + colblk(0) + colblk(nq)
        + [pl.BlockSpec((lc, gn), rev), pl.BlockSpec((lc, gn), rev),
           pl.BlockSpec((tail, gn), tailmap), pl.BlockSpec((tail, gn), tailmap),
           pl.BlockSpec((1, gn), row), pl.BlockSpec((1, gn), row),
           pl.BlockSpec(bbt_re.shape, blk3), pl.BlockSpec(bbt_im.shape, blk3),
           pl.BlockSpec(cct_re.shape, blk3), pl.BlockSpec(cct_im.shape, blk3), pl.BlockSpec((1, sw), row), ANY_SPEC],
        out_specs=[pl.BlockSpec((lc, sw), lambda i: (nb - 1 - i, 1)), pl.BlockSpec((1, gn), row), pl.BlockSpec((1, gn), row),
                   pl.BlockSpec((SSM_BLOCKS, ub, sb), blk3), pl.BlockSpec((SSM_BLOCKS, ub, sb), blk3),
                   pl.BlockSpec((SSM_BLOCKS, sb, ub), blk3), pl.BlockSpec((SSM_BLOCKS, sb, ub), blk3),
                   pl.BlockSpec((1, sw), row)],
        out_shape=[jax.ShapeDtypeStruct(duin.shape, BF16), jax.ShapeDtypeStruct((1, gn), F32), jax.ShapeDtypeStruct((1, gn), F32),
                   jax.ShapeDtypeStruct((SSM_BLOCKS, ub, sb), F32), jax.ShapeDtypeStruct((SSM_BLOCKS, ub, sb), F32),
                   jax.ShapeDtypeStruct((SSM_BLOCKS, sb, ub), F32), jax.ShapeDtypeStruct((SSM_BLOCKS, sb, ub), F32),
                   jax.ShapeDtypeStruct((1, sw), F32)],
        scratch_shapes=[pltpu.VMEM((lc, gn), F32), pltpu.VMEM((lc, gn), F32), pltpu.VMEM((lc, gn), F32), pltpu.VMEM((lc, gn), F32),
                        pltpu.VMEM((lc, sw), F32), pltpu.VMEM((lc, sw), F32), pltpu.VMEM((lc, sw), F32),
                        pltpu.VMEM((nq, lc, 128), F32), pltpu.VMEM((8, gn), F32), pltpu.VMEM((8, gn), F32),
                        pltpu.VMEM((tail + 8, gn), F32), pltpu.VMEM((tail + 8, gn), F32),
                        pltpu.VMEM((1, gn), F32), pltpu.VMEM((1, gn), F32)],
        input_output_aliases={3 * nq + 11: 0},
        compiler_params=_params("arbitrary"))(*([dge] * nq), *([y] * nq), *([uin] * nq), s_re, s_im, s_re, s_im, lrdt, ang,
                                               bbt_re, bbt_im, cct_re, cct_im, d_skip, duin)


def _ssm_fwd_doubling(name, uin, lrdt, ang, bb_re, bb_im, cc_re, cc_im, d_skip, sw):
    t = uin.shape[0]
    gn = lrdt.shape[1]
    lc = _ssm_chunk(t)
    nsteps = int(math.log2(lc))
    pad = lc // 2
    ub, sb = sw // SSM_BLOCKS, gn // SSM_BLOCKS

    def body(u_ref, lrdt_ref, ang_ref, bbr_ref, bbi_ref, ccr_ref, cci_ref, d_ref, y_ref, ge_ref, sre_ref, sim_ref,
             p_re, p_im, a_re, a_im, b_re, b_im, car_re, car_im):
        i = pl.program_id(0)
        lrdt_v, ang_v = lrdt_ref[...], ang_ref[...]

        @pl.when(i == 0)
        def _():
            k = (lax.broadcasted_iota(jnp.int32, (lc, 1), 0) + 1).astype(F32)
            pr, pi = _pow_rows(lrdt_v, ang_v, k)
            p_re[...] = pr
            p_im[...] = pi
            zeros = jnp.zeros((pad, gn), F32)
            a_re[0:pad, :] = zeros
            a_im[0:pad, :] = zeros
            b_re[0:pad, :] = zeros
            b_im[0:pad, :] = zeros
            car_re[...] = jnp.zeros_like(car_re)
            car_im[...] = jnp.zeros_like(car_im)

        u = u_ref[...]
        ubf = u.astype(BF16)
        for q in range(SSM_BLOCKS):
            uq = ubf[:, q * ub:(q + 1) * ub]
            a_re[pad:pad + lc, q * sb:(q + 1) * sb] = jnp.dot(uq, bbr_ref[q], preferred_element_type=F32)
            a_im[pad:pad + lc, q * sb:(q + 1) * sb] = jnp.dot(uq, bbi_ref[q], preferred_element_type=F32)
        bufs = [(a_re, a_im), (b_re, b_im)]
        for q in range(SSM_BLOCKS):
            cols = slice(q * sb, (q + 1) * sb)
            for j in range(nsteps):
                dd = 1 << j
                (s_re, s_im), (d_re, d_im) = bufs[j % 2], bufs[(j + 1) % 2]
                pr, pi = _pow_rows(lrdt_v[:, cols], ang_v[:, cols], float(dd))
                cr, ci = s_re[pad:pad + lc, cols], s_im[pad:pad + lc, cols]
                hr, hi = s_re[pad - dd:pad - dd + lc, cols], s_im[pad - dd:pad - dd + lc, cols]
                d_re[pad:pad + lc, cols] = cr + (pr * hr - pi * hi)
                d_im[pad:pad + lc, cols] = ci + (pr * hi + pi * hr)
            f_re, f_im = bufs[nsteps % 2]
            cr, ci = car_re[:, cols], car_im[:, cols]
            pr, pi = p_re[:, cols], p_im[:, cols]
            sr = f_re[pad:pad + lc, cols] + (pr * cr - pi * ci)
            si = f_im[pad:pad + lc, cols] + (pr * ci + pi * cr)
            sre_ref[:, cols] = sr
            sim_ref[:, cols] = si
            car_re[:, cols] = sr[lc - 1:lc, :]
            car_im[:, cols] = si[lc - 1:lc, :]
            ycols = slice(q * ub, (q + 1) * ub)
            y = (jnp.dot(sr.astype(BF16), ccr_ref[q], preferred_element_type=F32)
                 - jnp.dot(si.astype(BF16), cci_ref[q], preferred_element_type=F32)
                 + d_ref[:, ycols] * u[:, ycols])
            y_ref[:, ycols] = y
            ge_ref[:, ycols] = (0.5 * y * (1.0 + lax.erf(y * (1.0 / math.sqrt(2.0))))).astype(BF16)

    cblk = sw // sw
    row = lambda i: (0, 0)
    blk3 = lambda i: (0, 0, 0)
    return _pcall(
        body, name=name, grid=(t // lc,),
        in_specs=[pl.BlockSpec((lc, sw), lambda i: (i, cblk)), pl.BlockSpec((1, gn), row), pl.BlockSpec((1, gn), row),
                  pl.BlockSpec(bb_re.shape, blk3), pl.BlockSpec(bb_im.shape, blk3),
                  pl.BlockSpec(cc_re.shape, blk3), pl.BlockSpec(cc_im.shape, blk3), pl.BlockSpec((1, sw), row)],
        out_specs=[pl.BlockSpec((lc, sw), lambda i: (i, 0)), pl.BlockSpec((lc, sw), lambda i: (i, 0)),
                   pl.BlockSpec((lc, gn), lambda i: (i, 0)), pl.BlockSpec((lc, gn), lambda i: (i, 0))],
        out_shape=[jax.ShapeDtypeStruct((t, sw), F32), jax.ShapeDtypeStruct((t, sw), BF16),
                   jax.ShapeDtypeStruct((t, gn), F32), jax.ShapeDtypeStruct((t, gn), F32)],
        scratch_shapes=[pltpu.VMEM((lc, gn), F32), pltpu.VMEM((lc, gn), F32),
                        pltpu.VMEM((pad + lc, gn), F32), pltpu.VMEM((pad + lc, gn), F32),
                        pltpu.VMEM((pad + lc, gn), F32), pltpu.VMEM((pad + lc, gn), F32),
                        pltpu.VMEM((1, gn), F32), pltpu.VMEM((1, gn), F32)],
        compiler_params=_params("arbitrary"))(uin, lrdt, ang, bb_re, bb_im, cc_re, cc_im, d_skip)


def _ssm_bwd_doubling(name, dge, y, uin, s_re, s_im, lrdt, ang, bbt_re, bbt_im, cct_re, cct_im, d_skip, sw):
    t = uin.shape[0]
    gn = lrdt.shape[1]
    lc = _ssm_chunk(t)
    nb = t // lc
    nsteps = int(math.log2(lc))
    pad = lc // 2
    ub, sb = sw // SSM_BLOCKS, gn // SSM_BLOCKS
    tail = 8

    def body(dge_ref, y_ref, u_ref, sre_ref, sim_ref, tre_ref, tim_ref, lrdt_ref, ang_ref, btr_ref, bti_ref, ctr_ref, cti_ref,
             d_ref, du_ref, dar_ref, dai_ref, dbr_ref, dbi_ref, dcr_ref, dci_ref, dd_ref,
             q_re, q_im, a_re, a_im, b_re, b_im, sp_re, sp_im, car_re, car_im):
        i = pl.program_id(0)
        lrdt_v, ang_v = lrdt_ref[...], ang_ref[...]

        @pl.when(i == 0)
        def _():
            k = (lc - lax.broadcasted_iota(jnp.int32, (lc, 1), 0)).astype(F32)
            qr, qi = _pow_rows(lrdt_v, ang_v, k)
            q_re[...] = qr
            q_im[...] = qi
            zeros = jnp.zeros((pad, gn), F32)
            a_re[lc:lc + pad, :] = zeros
            a_im[lc:lc + pad, :] = zeros
            b_re[lc:lc + pad, :] = zeros
            b_im[lc:lc + pad, :] = zeros
            car_re[...] = jnp.zeros_like(car_re)
            car_im[...] = jnp.zeros_like(car_im)
            for r in (dar_ref, dai_ref, dbr_ref, dbi_ref, dcr_ref, dci_ref, dd_ref):
                r[...] = jnp.zeros_like(r)

        first = (i == nb - 1).astype(F32)
        sp_re[0:tail, :] = tre_ref[...] * (1.0 - first)
        sp_im[0:tail, :] = tim_ref[...] * (1.0 - first)
        sp_re[tail:tail + lc, :] = sre_ref[...]
        sp_im[tail:tail + lc, :] = sim_ref[...]

        yv = y_ref[...]
        u = u_ref[...]
        cdf = 0.5 * (1.0 + lax.erf(yv * (1.0 / math.sqrt(2.0))))
        pdf = jnp.exp(-0.5 * yv * yv) * (1.0 / math.sqrt(2.0 * math.pi))
        dy = dge_ref[...].astype(F32) * (cdf + yv * pdf)
        dd_ref[...] += _colsum(dy * u)
        dyb = dy.astype(BF16)
        ubf = u.astype(BF16)
        for q in range(SSM_BLOCKS):
            dq = dyb[:, q * ub:(q + 1) * ub]
            a_re[0:lc, q * sb:(q + 1) * sb] = jnp.dot(dq, ctr_ref[q], preferred_element_type=F32)
            a_im[0:lc, q * sb:(q + 1) * sb] = -jnp.dot(dq, cti_ref[q], preferred_element_type=F32)
        bufs = [(a_re, a_im), (b_re, b_im)]
        tn_dims = (((0,), (0,)), ((), ()))
        for q in range(SSM_BLOCKS):
            cols = slice(q * sb, (q + 1) * sb)
            ycols = slice(q * ub, (q + 1) * ub)
            for j in range(nsteps):
                ds = 1 << j
                (s_r, s_i), (d_r, d_i) = bufs[j % 2], bufs[(j + 1) % 2]
                pr, pi = _pow_rows(lrdt_v[:, cols], ang_v[:, cols], float(ds))
                cr, ci = s_r[0:lc, cols], s_i[0:lc, cols]
                hr, hi = s_r[ds:ds + lc, cols], s_i[ds:ds + lc, cols]
                d_r[0:lc, cols] = cr + (pr * hr + pi * hi)
                d_i[0:lc, cols] = ci + (pr * hi - pi * hr)
            f_r, f_i = bufs[nsteps % 2]
            cr, ci = car_re[:, cols], car_im[:, cols]
            qr, qi = q_re[:, cols], q_im[:, cols]
            lam_r = f_r[0:lc, cols] + (qr * cr + qi * ci)
            lam_i = f_i[0:lc, cols] + (qr * ci - qi * cr)
            car_re[:, cols] = lam_r[0:1, :]
            car_im[:, cols] = lam_i[0:1, :]
            pvr, pvi = sp_re[tail - 1:tail - 1 + lc, cols], sp_im[tail - 1:tail - 1 + lc, cols]
            dar_ref[:, cols] += _colsum(lam_r * pvr + lam_i * pvi)
            dai_ref[:, cols] += _colsum(lam_i * pvr - lam_r * pvi)
            lrb, lib = lam_r.astype(BF16), lam_i.astype(BF16)
            du_ref[:, ycols] = (jnp.dot(lrb, btr_ref[q], preferred_element_type=F32)
                                + jnp.dot(lib, bti_ref[q], preferred_element_type=F32)
                                + d_ref[:, ycols] * dy[:, ycols]).astype(BF16)
            uq = ubf[:, ycols]
            dbr_ref[q] += lax.dot_general(uq, lrb, tn_dims, preferred_element_type=F32)
            dbi_ref[q] += lax.dot_general(uq, lib, tn_dims, preferred_element_type=F32)
            dq = dyb[:, ycols]
            dcr_ref[q] += lax.dot_general(sre_ref[:, cols].astype(BF16), dq, tn_dims, preferred_element_type=F32)
            dci_ref[q] -= lax.dot_general(sim_ref[:, cols].astype(BF16), dq, tn_dims, preferred_element_type=F32)

    cblk = 1
    rev = lambda i: (nb - 1 - i, 0)
    revu = lambda i: (nb - 1 - i, cblk)
    tailmap = lambda i: (jnp.maximum((nb - 1 - i) * (lc // tail) - 1, 0), 0)
    row = lambda i: (0, 0)
    blk3 = lambda i: (0, 0, 0)
    return _pcall(
        body, name=name, grid=(nb,),
        in_specs=[pl.BlockSpec((lc, sw), rev), pl.BlockSpec((lc, sw), rev), pl.BlockSpec((lc, sw), revu),
                  pl.BlockSpec((lc, gn), rev), pl.BlockSpec((lc, gn), rev),
                  pl.BlockSpec((tail, gn), tailmap), pl.BlockSpec((tail, gn), tailmap),
                  pl.BlockSpec((1, gn), row), pl.BlockSpec((1, gn), row),
                  pl.BlockSpec(bbt_re.shape, blk3), pl.BlockSpec(bbt_im.shape, blk3),
                  pl.BlockSpec(cct_re.shape, blk3), pl.BlockSpec(cct_im.shape, blk3), pl.BlockSpec((1, sw), row)],
        out_specs=[pl.BlockSpec((lc, sw), rev), pl.BlockSpec((1, gn), row), pl.BlockSpec((1, gn), row),
                   pl.BlockSpec((SSM_BLOCKS, ub, sb), blk3), pl.BlockSpec((SSM_BLOCKS, ub, sb), blk3),
                   pl.BlockSpec((SSM_BLOCKS, sb, ub), blk3), pl.BlockSpec((SSM_BLOCKS, sb, ub), blk3),
                   pl.BlockSpec((1, sw), row)],
        out_shape=[jax.ShapeDtypeStruct((t, sw), BF16), jax.ShapeDtypeStruct((1, gn), F32), jax.ShapeDtypeStruct((1, gn), F32),
                   jax.ShapeDtypeStruct((SSM_BLOCKS, ub, sb), F32), jax.ShapeDtypeStruct((SSM_BLOCKS, ub, sb), F32),
                   jax.ShapeDtypeStruct((SSM_BLOCKS, sb, ub), F32), jax.ShapeDtypeStruct((SSM_BLOCKS, sb, ub), F32),
                   jax.ShapeDtypeStruct((1, sw), F32)],
        scratch_shapes=[pltpu.VMEM((lc, gn), F32), pltpu.VMEM((lc, gn), F32),
                        pltpu.VMEM((lc + pad, gn), F32), pltpu.VMEM((lc + pad, gn), F32),
                        pltpu.VMEM((lc + pad, gn), F32), pltpu.VMEM((lc + pad, gn), F32),
                        pltpu.VMEM((tail + lc, gn), F32), pltpu.VMEM((tail + lc, gn), F32),
                        pltpu.VMEM((1, gn), F32), pltpu.VMEM((1, gn), F32)],
        compiler_params=_params("arbitrary"))(dge, y, uin, s_re, s_im, s_re, s_im, lrdt, ang, bbt_re, bbt_im, cct_re, cct_im, d_skip)


def _blockdiag_b(bb, sw):
    gpb = (sw // SSM_GROUP) // SSM_BLOCKS
    b4 = bb.reshape(SSM_BLOCKS, gpb, SSM_STATE, SSM_GROUP)
    eye = jnp.eye(gpb, dtype=bb.dtype)
    out = jnp.einsum('qgnh,gk->qghkn', b4, eye)
    return out.reshape(SSM_BLOCKS, gpb * SSM_GROUP, gpb * SSM_STATE)


def _blockdiag_c(cc, sw):
    gpb = (sw // SSM_GROUP) // SSM_BLOCKS
    c4 = cc.reshape(SSM_BLOCKS, gpb, SSM_GROUP, SSM_STATE)
    eye = jnp.eye(gpb, dtype=cc.dtype)
    out = jnp.einsum('qghn,gk->qgnkh', c4, eye)
    return out.reshape(SSM_BLOCKS, gpb * SSM_STATE, gpb * SSM_GROUP)


def _diag_of_b(dbb, sw):
    gpb = (sw // SSM_GROUP) // SSM_BLOCKS
    d5 = dbb.reshape(SSM_BLOCKS, gpb, SSM_GROUP, gpb, SSM_STATE)
    return jnp.einsum('qghgn->qgnh', d5).reshape(SSM_BLOCKS * gpb * SSM_STATE, SSM_GROUP)


def _diag_of_c(dcc, sw):
    gpb = (sw // SSM_GROUP) // SSM_BLOCKS
    d5 = dcc.reshape(SSM_BLOCKS, gpb, SSM_STATE, gpb, SSM_GROUP)
    return jnp.einsum('qgngh->qghn', d5).reshape(SSM_BLOCKS * gpb, SSM_GROUP, SSM_STATE)


def _ada_fwd(name, c_all, w_sh, b_sh):
    nb, d = c_all.shape
    ncol = w_sh.shape[1]
    tn = _pick(ncol, 768)

    def body(c_ref, w_ref, b_ref, o_ref):
        cv = c_ref[...]
        sil = cv * _sigmoid(cv)
        o_ref[...] = jnp.dot(sil, w_ref[...], preferred_element_type=F32, precision=lax.Precision.HIGHEST) + b_ref[...]

    return _pcall(body, name=name, grid=(ncol // tn,),
                  in_specs=[pl.BlockSpec((nb, d), lambda j: (0, 0)), pl.BlockSpec((d, tn), lambda j: (0, j)),
                            pl.BlockSpec((1, tn), lambda j: (0, j))],
                  out_specs=pl.BlockSpec((nb, tn), lambda j: (0, j)),
                  out_shape=jax.ShapeDtypeStruct((nb, ncol), F32), compiler_params=_params("parallel"))(c_all, w_sh, b_sh)


def _ada_bwd(name, c_all, dmod_sh):
    nb, d = c_all.shape
    ncol = dmod_sh.shape[1]
    tn = _pick(ncol, 768)

    def body(c_ref, g_ref, o_ref):
        cv = c_ref[...]
        sil = cv * _sigmoid(cv)
        o_ref[...] = lax.dot_general(sil, g_ref[...], (((0,), (0,)), ((), ())), preferred_element_type=F32,
                                     precision=lax.Precision.HIGHEST)

    return _pcall(body, name=name, grid=(ncol // tn,),
                  in_specs=[pl.BlockSpec((nb, d), lambda j: (0, 0)), pl.BlockSpec((nb, tn), lambda j: (0, j))],
                  out_specs=pl.BlockSpec((d, tn), lambda j: (0, j)),
                  out_shape=jax.ShapeDtypeStruct((d, ncol), F32), compiler_params=_params("parallel"))(c_all, dmod_sh)


def _place():
    return lax.axis_index("x"), lax.axis_index("y"), lax.axis_index("c")


def _allgather_small(name, blk, deps=()):
    m_per, n = blk.shape

    def body(x_ref, *rest):
        out_ref, send_sems, recv_sems, local_sem = rest[len(deps):]
        x, y, c = _place()
        me, sibling = (x, y, c), (x, y, 1 - c)
        chips = [(1 - x, y), (x, 1 - y), (1 - x, 1 - y)]

        def rows(px, py, pc):
            return out_ref.at[pl.ds((4 * px + 2 * py + pc) * m_per, m_per), :]

        def copy(k, block, to, src=None):
            return pltpu.make_async_remote_copy(
                src_ref=rows(*block) if src is None else src, dst_ref=rows(*block),
                send_sem=send_sems.at[k], recv_sem=recv_sems.at[k], device_id=to, device_id_type=MESH)

        mine = pltpu.make_async_copy(x_ref, rows(*me), local_sem)
        mine.start()
        first = [copy(0, me, sibling, src=x_ref)]
        first += [copy(1 + j, me, (*chip, c), src=x_ref) for j, chip in enumerate(chips)]
        for cp in first:
            cp.start()
        passed = [copy(4 + j, (*chip, c), sibling) for j, chip in enumerate(chips)]
        for j, chip in enumerate(chips):
            copy(1 + j, (*chip, c), me).wait_recv()
            passed[j].start()
        copy(0, sibling, me).wait_recv()
        for j, chip in enumerate(chips):
            copy(4 + j, (*chip, 1 - c), me).wait_recv()
        for cp in first + passed:
            cp.wait_send()
        mine.wait()

    return _pcall(body, name=name, out_shape=jax.ShapeDtypeStruct((N_DEV * m_per, n), blk.dtype),
                  in_specs=[pl.BlockSpec(memory_space=pltpu.VMEM)] + [ANY_SPEC] * len(deps),
                  out_specs=pl.BlockSpec(memory_space=pltpu.VMEM),
                  scratch_shapes=[pltpu.SemaphoreType.DMA((7,)), pltpu.SemaphoreType.DMA((7,)), pltpu.SemaphoreType.DMA],
                  compiler_params=pltpu.CompilerParams(vmem_limit_bytes=VMEM_LIMIT))(blk, *deps)


def _other_chips(x, y):
    return [(1 - x, y), (x, 1 - y), (1 - x, 1 - y)]


def _place_shards(shards, s_me):
    return [lax.dynamic_update_slice(lax.empty((N_CHIPS,) + s.shape, s.dtype), s[None], (s_me, 0, 0)) for s in shards]


def _ag_copy(src, land, send, recv, wi, j, chip, x, y, c, tc, both):
    hr = src.shape[0] // 2
    half = pl.ds(pl.multiple_of(c * hr, 16), hr)
    k = 3 * wi + j
    return pltpu.make_async_remote_copy(
        src_ref=src.at[half, :], dst_ref=land.at[2 * x + y, half, :],
        send_sem=send.at[2 * k + tc if both else k], recv_sem=recv.at[2 * k + c if both else k],
        device_id=(chip[0], chip[1], tc), device_id_type=MESH)


def _ag_targets(c, both):
    return (0, 1) if both else (c,)


def _ag_start(name, shards, lands, groups, direct, deps=()):
    nw, ng, nd = len(shards), len(groups), len(deps)

    def body(*refs):
        src, land = refs[:nw], refs[nw:2 * nw]
        sems = refs[2 * nw + nd:2 * nw + nd + 2 * ng]
        token = refs[-1]
        x, y, c = _place()
        for gi, grp in enumerate(groups):
            for wi, w in enumerate(grp):
                for j, chip in enumerate(_other_chips(x, y)):
                    for tc in _ag_targets(c, direct[gi]):
                        _ag_copy(src[w], land[w], sems[2 * gi], sems[2 * gi + 1], wi, j, chip, x, y, c, tc, direct[gi]).start()
        token[...] = jnp.zeros_like(token)

    sem_shapes = []
    for gi, grp in enumerate(groups):
        sem_shapes += [pltpu.SemaphoreType.DMA(((6 if direct[gi] else 3) * len(grp),))] * 2
    out_shape = sem_shapes + [pltpu.HBM(s.shape, s.dtype) for s in shards] + [pltpu.HBM(l.shape, l.dtype) for l in lands]
    out_shape += [jax.ShapeDtypeStruct((8, 128), F32)]
    res = _pcall(body, name=name, out_shape=out_shape, in_specs=[HBM_SPEC] * (2 * nw) + [ANY_SPEC] * nd,
                 out_specs=[SEM_SPEC] * (2 * ng) + [HBM_SPEC] * (2 * nw) + [pl.BlockSpec(memory_space=pltpu.VMEM)],
                 input_output_aliases={i: 2 * ng + i for i in range(2 * nw)},
                 compiler_params=pltpu.CompilerParams(has_side_effects=EFFECT))(
                     *[_hbm(s) for s in shards], *[_hbm(l) for l in lands], *deps)
    sems = [(res[2 * gi], res[2 * gi + 1]) for gi in range(ng)]
    return sems, list(res[2 * ng:2 * ng + nw]), list(res[2 * ng + nw:2 * ng + 2 * nw]), res[-1]


def _ag_wait(name, shards, lands, send, recv, both, after):
    n = len(shards)

    def body(*refs):
        src, land = refs[:n], refs[n:2 * n]
        send_sem, recv_sem = refs[2 * n], refs[2 * n + 1]
        x, y, c = _place()
        for wi in range(n):
            for j, chip in enumerate(_other_chips(x, y)):
                for tc in _ag_targets(c, both):
                    _ag_copy(src[wi], land[wi], send_sem, recv_sem, wi, j, chip, x, y, c, tc, both).wait_send()
                    _ag_copy(src[wi], land[wi], send_sem, recv_sem, wi, j, chip, chip[0], chip[1], tc, c, both).wait_recv()

    res = _pcall(body, name=name, out_shape=[pltpu.HBM(a.shape, a.dtype) for a in list(shards) + list(lands)],
                 in_specs=[HBM_SPEC] * (2 * n) + [SEM_SPEC, SEM_SPEC] + [ANY_SPEC] * len(after), out_specs=[HBM_SPEC] * (2 * n),
                 input_output_aliases={i: i for i in range(2 * n)},
                 compiler_params=pltpu.CompilerParams(has_side_effects=EFFECT))(*shards, *lands, send, recv, *after)
    return list(res[n:])


def _ag_forward(name, lands):
    n = len(lands)

    def body(*refs):
        out = refs[n:2 * n]
        send, recv = refs[2 * n], refs[2 * n + 1]
        x, y, c = _place()
        sib = (x, y, 1 - c)
        cps = []
        for wi in range(n):
            hr = out[wi].shape[1] // 2
            for j, (cx, cy) in enumerate(_other_chips(x, y)):
                got = out[wi].at[2 * cx + cy, pl.ds(pl.multiple_of(c * hr, 16), hr), :]
                cp = pltpu.make_async_remote_copy(src_ref=got, dst_ref=got, send_sem=send.at[3 * wi + j], recv_sem=recv.at[3 * wi + j],
                                                  device_id=sib, device_id_type=MESH)
                cp.start()
                cps.append(cp)
        for wi in range(n):
            hr = out[wi].shape[1] // 2
            for j, (cx, cy) in enumerate(_other_chips(x, y)):
                got = out[wi].at[2 * cx + cy, pl.ds(pl.multiple_of((1 - c) * hr, 16), hr), :]
                pltpu.make_async_remote_copy(src_ref=got, dst_ref=got, send_sem=send.at[3 * wi + j], recv_sem=recv.at[3 * wi + j],
                                             device_id=sib, device_id_type=MESH).wait_recv()
        for cp in cps:
            cp.wait_send()

    res = _pcall(body, name=name, out_shape=[jax.ShapeDtypeStruct(l.shape, l.dtype) for l in lands],
                 in_specs=[ANY_SPEC] * n, out_specs=[ANY_SPEC] * n, input_output_aliases={i: i for i in range(n)},
                 scratch_shapes=[pltpu.SemaphoreType.DMA((3 * n,)), pltpu.SemaphoreType.DMA((3 * n,))])(*lands)
    return list(res)


def _peers(x, y, c):
    offs = [(dx, dy, dc) for dx in (0, 1) for dy in (0, 1) for dc in (0, 1)][1:]
    return [(1 - x if dx else x, 1 - y if dy else y, 1 - c if dc else c) for dx, dy, dc in offs]


def _rs_copy(g_ref, land_ref, send, recv, wi, k, to, sender):
    hr = g_ref.shape[1] // 2
    return pltpu.make_async_remote_copy(
        src_ref=g_ref.at[2 * to[0] + to[1], pl.ds(pl.multiple_of(to[2] * hr, 16), hr), :], dst_ref=land_ref.at[sender],
        send_sem=send.at[7 * wi + k], recv_sem=recv.at[7 * wi + k], device_id=to, device_id_type=MESH)


def _rs_start(name, gs):
    n = len(gs)
    lands = [lax.empty((N_DEV, g.shape[1] // 2, g.shape[2]), BF16) for g in gs]

    def body(*refs):
        g, land = refs[:n], refs[n:2 * n]
        send, recv = refs[2 * n], refs[2 * n + 1]
        token = refs[-1]
        x, y, c = _place()
        me = 4 * x + 2 * y + c
        for wi in range(n):
            for k, to in enumerate(_peers(x, y, c)):
                _rs_copy(g[wi], land[wi], send, recv, wi, k, to, me).start()
        token[...] = jnp.zeros_like(token)

    out_shape = [pltpu.SemaphoreType.DMA((7 * n,))] * 2 + [pltpu.HBM(a.shape, a.dtype) for a in list(gs) + lands]
    out_shape += [jax.ShapeDtypeStruct((8, 128), F32)]
    res = _pcall(body, name=name, out_shape=out_shape, in_specs=[HBM_SPEC] * (2 * n),
                 out_specs=[SEM_SPEC] * 2 + [HBM_SPEC] * (2 * n) + [pl.BlockSpec(memory_space=pltpu.VMEM)],
                 input_output_aliases={i: 2 + i for i in range(2 * n)},
                 compiler_params=pltpu.CompilerParams(has_side_effects=EFFECT))(
                     *[_hbm(a) for a in gs], *[_hbm(a) for a in lands])
    return res[0], res[1], list(res[2:2 + n]), list(res[2 + n:2 + 2 * n]), res[-1]


def _rs_wait(name, gs, lands, send, recv, after):
    n = len(gs)

    def body(*refs):
        g, land = refs[:n], refs[n:2 * n]
        send_sem, recv_sem = refs[2 * n], refs[2 * n + 1]
        x, y, c = _place()
        me = 4 * x + 2 * y + c
        for wi in range(n):
            for k, to in enumerate(_peers(x, y, c)):
                _rs_copy(g[wi], land[wi], send_sem, recv_sem, wi, k, to, me).wait_send()
                _rs_copy(g[wi], land[wi], send_sem, recv_sem, wi, k, (x, y, c), 4 * to[0] + 2 * to[1] + to[2]).wait_recv()

    res = _pcall(body, name=name, out_shape=[pltpu.HBM(a.shape, a.dtype) for a in list(gs) + list(lands)],
                 in_specs=[HBM_SPEC] * (2 * n) + [SEM_SPEC, SEM_SPEC, ANY_SPEC], out_specs=[HBM_SPEC] * (2 * n),
                 input_output_aliases={i: i for i in range(2 * n)},
                 compiler_params=pltpu.CompilerParams(has_side_effects=EFFECT))(*gs, *lands, send, recv, after)
    return list(res[:n]), list(res[n:])


def _bc_copy(blk_ref, land_ref, send, recv, k, to, slot):
    return pltpu.make_async_remote_copy(src_ref=blk_ref, dst_ref=land_ref.at[slot], send_sem=send.at[k], recv_sem=recv.at[k],
                                        device_id=to, device_id_type=MESH)


def _bcast_start(name, blk):
    land = lax.empty((N_DEV,) + blk.shape, blk.dtype)

    def body(blk_ref, land_ref, send, recv, blk_thru, land_thru, token):
        x, y, c = _place()
        for k, to in enumerate(_peers(x, y, c)):
            _bc_copy(blk_ref, land_ref, send, recv, k, to, 4 * x + 2 * y + c).start()
        token[...] = jnp.zeros_like(token)

    return _pcall(body, name=name,
                  out_shape=[pltpu.SemaphoreType.DMA((7,)), pltpu.SemaphoreType.DMA((7,)), pltpu.HBM(blk.shape, blk.dtype),
                             pltpu.HBM(land.shape, land.dtype), jax.ShapeDtypeStruct((8, 128), F32)],
                  in_specs=[HBM_SPEC, HBM_SPEC], out_specs=[SEM_SPEC, SEM_SPEC, HBM_SPEC, HBM_SPEC, pl.BlockSpec(memory_space=pltpu.VMEM)],
                  input_output_aliases={0: 2, 1: 3}, compiler_params=pltpu.CompilerParams(has_side_effects=EFFECT))(_hbm(blk), _hbm(land))


def _bcast_wait(name, blk, land, send, recv, after):
    def body(blk_ref, land_ref, send_sem, recv_sem, after_ref, blk_thru, land_thru):
        x, y, c = _place()
        for k, to in enumerate(_peers(x, y, c)):
            _bc_copy(blk_ref, land_ref, send_sem, recv_sem, k, to, 4 * x + 2 * y + c).wait_send()
            _bc_copy(blk_ref, land_ref, send_sem, recv_sem, k, to, 4 * to[0] + 2 * to[1] + to[2]).wait_recv()

    return _pcall(body, name=name, out_shape=[pltpu.HBM(blk.shape, blk.dtype), pltpu.HBM(land.shape, land.dtype)],
                  in_specs=[HBM_SPEC, HBM_SPEC, SEM_SPEC, SEM_SPEC, ANY_SPEC], out_specs=[HBM_SPEC, HBM_SPEC],
                  input_output_aliases={0: 0, 1: 1}, compiler_params=pltpu.CompilerParams(has_side_effects=EFFECT))(
                      blk, land, send, recv, after)[1]


def _rs_sum(name, g, land):
    _, rs, cs = g.shape
    hr = rs // 2
    ch = _pick(hr, 64, 16)

    def body(g_ref, land_ref, out_ref, recv, local_sems, sib_sems):
        x, y, c = _place()
        me = 4 * x + 2 * y + c
        cps = [pltpu.make_async_copy(g_ref.at[2 * x + y, pl.ds(pl.multiple_of(c * hr, 16), hr), :], recv.at[me], local_sems.at[7])]
        for k, (tx, ty, tc) in enumerate(_peers(x, y, c)):
            slot = 4 * tx + 2 * ty + tc
            cps.append(pltpu.make_async_copy(land_ref.at[slot], recv.at[slot], local_sems.at[k]))
        for cp in cps:
            cp.start()
        for cp in cps:
            cp.wait()
        base = pl.multiple_of(c * hr, 16)
        for r0 in range(0, hr, ch):
            acc = recv[0, r0:r0 + ch, :].astype(F32)
            for k in range(1, N_DEV):
                acc = acc + recv[k, r0:r0 + ch, :].astype(F32)
            out_ref[pl.ds(base + r0, ch), :] = acc
        half = out_ref.at[pl.ds(base, hr), :]
        sib = pltpu.make_async_remote_copy(src_ref=half, dst_ref=half, send_sem=sib_sems.at[0], recv_sem=sib_sems.at[1],
                                           device_id=(x, y, 1 - c), device_id_type=MESH)
        sib.start()
        other = out_ref.at[pl.ds(pl.multiple_of((1 - c) * hr, 16), hr), :]
        pltpu.make_async_remote_copy(src_ref=other, dst_ref=other, send_sem=sib_sems.at[0], recv_sem=sib_sems.at[1],
                                     device_id=(x, y, 1 - c), device_id_type=MESH).wait_recv()
        sib.wait_send()

    return _pcall(
        body, name=name, out_shape=jax.ShapeDtypeStruct((rs, cs), F32),
        in_specs=[ANY_SPEC, ANY_SPEC], out_specs=pl.BlockSpec(memory_space=pltpu.VMEM),
        scratch_shapes=[pltpu.VMEM((N_DEV, hr, cs), BF16), pltpu.SemaphoreType.DMA((8,)), pltpu.SemaphoreType.DMA((2,))],
        compiler_params=pltpu.CompilerParams(vmem_limit_bytes=VMEM_LIMIT))(g, land)


def _gate_norm_epilogue(factor, x_in, gt, nxt, t, d):
    blk = (512, d)
    full = lambda i, j: (i, j)
    rowv = lambda i, j: (0, j)

    def epi(acc, res, scale, *norm):
        x_new = res + (factor * scale) * acc
        if not norm:
            return x_new, acc
        gv, scv, shv = norm
        r = lax.rsqrt(jnp.mean(x_new * x_new, axis=-1, keepdims=True) + EPS)
        return x_new, acc, x_new * r * gv * (1.0 + scv) + shv

    ins = [(x_in, blk, full), (gt, (1, d), rowv)] + [(v, (1, d), rowv) for v in (nxt or ())]
    outs = [((t, d), F32, blk, full), ((t, d), BF16, blk, full)] + ([((t, d), BF16, blk, full)] if nxt else [])
    return epi, ins, outs


def _ffn_in_act(name, h, w_in, tm=512):
    t, d = h.shape
    ns = w_in.shape[2]
    tm = _pick(t, tm, 8)
    w4 = w_in.reshape(2, 2, d, ns)

    def body(h_ref, w_ref, ab_ref, act_ref):
        hv = h_ref[...]
        a = jnp.dot(hv, w_ref[0], preferred_element_type=F32)
        b = jnp.dot(hv, w_ref[1], preferred_element_type=F32)
        ab_ref[0] = a.astype(BF16)
        ab_ref[1] = b.astype(BF16)
        act_ref[...] = (a * _sigmoid(a) * b).astype(BF16)

    return _pcall(body, name=name, grid=(2, t // tm),
                  in_specs=[pl.BlockSpec((tm, d), lambda c, i: (i, 0)), pl.BlockSpec((2, None, d, ns), lambda c, i: (0, c, 0, 0))],
                  out_specs=[pl.BlockSpec((2, tm, ns), lambda c, i: (0, i, c)), pl.BlockSpec((tm, ns), lambda c, i: (i, c))],
                  out_shape=[jax.ShapeDtypeStruct((2, t, 2 * ns), BF16), jax.ShapeDtypeStruct((t, 2 * ns), BF16)],
                  compiler_params=_params("parallel", "parallel"))(h, w4)


def _dswiglu_epilogue(ab, t, f, tn):
    blk = (2, 512, tn)
    idx = lambda i, j: (0, i, j)

    def epi(dact, abv):
        a, b = abv[0].astype(F32), abv[1].astype(F32)
        s = _sigmoid(a)
        return (jnp.stack([dact * b * (s * (1.0 + a * (1.0 - s))), dact * (a * s)]),)

    return epi, [(ab, blk, idx)], [((2, t, f), BF16, blk, idx)]


def _ffn_fwd(tag, x, h, gt, get_w_in, get_w_out, nxt):
    t, d = x.shape
    ab, act = _ffn_in_act(tag + "_in", h, get_w_in(h))
    epi, epi_ins, epi_outs = _gate_norm_epilogue(0.5, x, gt, nxt, t, d)
    res = _mm(tag + "_out", act, get_w_out(act), tm=512, tn=d, epi=epi, epi_ins=epi_ins, epi_outs=epi_outs)
    return res[0], (res[2] if nxt else None), (x, h, ab, act, res[1])


def _ffn_bwd(tag, dx_new, dy, saved, g, sc, w_in, w_out, start_rs, up, deps=()):
    x, h, ab, act, _ = saved
    d = x.shape[1]
    f = act.shape[1]
    dw_out = _mm(tag + "_dwout", act, dy, ta=True, tm=1408, tn=d, tk=2048, deps=deps)
    tok = start_rs("out", dw_out.reshape(N_CHIPS, f // N_CHIPS, d))
    epi, epi_ins, epi_outs = _dswiglu_epilogue(ab, x.shape[0], f, w_in.shape[2])
    dab = _mm(tag + "_dact", dy, w_out, tb=True, tm=512, tn=w_in.shape[2], epi=epi, epi_ins=epi_ins, epi_outs=epi_outs,
              deps=(tok,), j_outer=True)
    dw_in = _mm(tag + "_dwin", h, dab, ta=True, out_stacked=True, b_halves=True, tm=d, tn=1408, tk=2048)
    tok = start_rs("in", dw_in)
    dh = _mm(tag + "_dh", dab, w_in, tb=True, b_stacked=True, a_halves=True, tn=d, tk=5632, deps=(tok,))
    return _norm_bwd(tag + "_dnorm", dh, x, dx_new, g, sc, *(up or (None, None, None)))


def _row(v):
    return v.reshape(1, -1)


def _pack(parts):
    cols = []
    for p in parts:
        flat = p.reshape(-1).astype(F32)
        padn = (-flat.shape[0]) % 128
        cols.append(jnp.pad(flat, (0, padn)) if padn else flat)
    flat = jnp.concatenate(cols)
    padn = (-flat.shape[0]) % 1024
    if padn:
        flat = jnp.pad(flat, (0, padn))
    return flat.reshape(-1, 128)


def _unpack(packed, shapes):
    flat = packed.reshape(-1)
    out, off = [], 0
    for s in shapes:
        n = math.prod(s)
        out.append(flat[off:off + n].reshape(s))
        off += n + ((-n) % 128)
    return out


SMALL = ['b_ada', 'g_ffn1', 'g_mix', 'pool_w', 'pool_b', 'pool_scale', 'ssm_lam_re_log', 'ssm_lam_im', 'ssm_log_dt',
         'ssm_b_re', 'ssm_b_im', 'ssm_c_re', 'ssm_c_im', 'ssm_d', 'b_glu', 'g_ffn2', 'g_final']
BIG = ['w_ffn1_in', 'w_ffn1_out', 'w_in', 'w_pool_up', 'w_glu', 'w_ssm_up', 'w_out', 'w_ffn2_in', 'w_ffn2_out']
AG_GROUPS = [[0], [1], [2, 3, 4, 5, 6], [7, 8]]
AG_DIRECT = [False, True, True, True]
SMALL_LATE = ['b_ada', 'g_ffn1']
WEIGHTS = ['w_ada', 'b_ada', 'g_ffn1', 'w_ffn1_in', 'w_ffn1_out', 'g_mix', 'w_in', 'pool_w', 'pool_b', 'pool_scale', 'w_pool_up',
           'ssm_lam_re_log', 'ssm_lam_im', 'ssm_log_dt', 'ssm_b_re', 'ssm_b_im', 'ssm_c_re', 'ssm_c_im', 'ssm_d', 'w_glu', 'b_glu',
           'w_ssm_up', 'w_out', 'g_ffn2', 'w_ffn2_in', 'w_ffn2_out', 'g_final']


def kernel(x, c, w_ada, b_ada, g_ffn1, w_ffn1_in, w_ffn1_out, g_mix, w_in, pool_w, pool_b, pool_scale, w_pool_up, ssm_lam_re_log, ssm_lam_im, ssm_log_dt, ssm_b_re, ssm_b_im, ssm_c_re, ssm_c_im, ssm_d, w_glu, b_glu, w_ssm_up, w_out, g_ffn2, w_ffn2_in, w_ffn2_out, g_final, loss_target, m_w_ada, m_b_ada, m_g_ffn1, m_w_ffn1_in, m_w_ffn1_out, m_g_mix, m_w_in, m_pool_w, m_pool_b, m_pool_scale, m_w_pool_up, m_ssm_lam_re_log, m_ssm_lam_im, m_ssm_log_dt, m_ssm_b_re, m_ssm_b_im, m_ssm_c_re, m_ssm_c_im, m_ssm_d, m_w_glu, m_b_glu, m_w_ssm_up, m_w_out, m_g_ffn2, m_w_ffn2_in, m_w_ffn2_out, m_g_final, v_w_ada, v_b_ada, v_g_ffn1, v_w_ffn1_in, v_w_ffn1_out, v_g_mix, v_w_in, v_pool_w, v_pool_b, v_pool_scale, v_w_pool_up, v_ssm_lam_re_log, v_ssm_lam_im, v_ssm_log_dt, v_ssm_b_re, v_ssm_b_im, v_ssm_c_re, v_ssm_c_im, v_ssm_d, v_w_glu, v_b_glu, v_w_ssm_up, v_w_out, v_g_ffn2, v_w_ffn2_in, v_w_ffn2_out, v_g_final):
    args = dict(locals())
    wt = {n: args[n] for n in WEIGHTS}
    mom = {n: args["m_" + n] for n in WEIGHTS}
    var = {n: args["v_" + n] for n in WEIGHTS}

    t, d = x.shape[1], x.shape[2]
    pw = pool_b.shape[1]
    sw = ssm_d.shape[1]
    ngrp = sw // SSM_GROUP
    gn = ngrp * SSM_STATE
    xi, yi, ci = _place()
    b_me = 4 * xi + 2 * yi + ci
    s_me = 2 * xi + yi
    x2d = x[0]
    tgt = loss_target[0]

    c_all = _allgather_small("ag_c", c.reshape(8, d // 8)).reshape(N_DEV, d)
    ncol = w_ada.shape[2]
    b_sh = lax.dynamic_slice(b_ada, (0, s_me * ncol), (1, ncol))
    mod_sh = _ada_fwd("ada_fwd", c_all, w_ada[0], b_sh)
    md_send, md_recv, mod_sh, md_land, tok = _bcast_start("mod_start", mod_sh)

    shards = [wt[n][0].astype(BF16) for n in BIG]
    ag_sems, shards_t, lands_t, tok = _ag_start("ag_start", shards, _place_shards(shards, s_me), AG_GROUPS, AG_DIRECT, deps=(tok,))
    md_land = _bcast_wait("mod_wait", mod_sh, md_land, md_send, md_recv, tok)
    mod_all = lax.dynamic_update_slice(md_land, mod_sh[None], (b_me, 0, 0))
    full = {}

    def weights(gi, *after):
        grp = AG_GROUPS[gi]
        if BIG[grp[0]] not in full:
            ls = _ag_wait("ag_wait%d" % gi, [shards_t[w] for w in grp], [lands_t[w] for w in grp], *ag_sems[gi],
                          AG_DIRECT[gi], after)
            for w, l in zip(grp, ls if AG_DIRECT[gi] else _ag_forward("ag_fwd%d" % gi, ls)):
                full[BIG[w]] = l
        return full

    mod_me = jnp.concatenate([lax.dynamic_slice(mod_all, (2 * s, b_me, 0), (1, 1, ncol))[0] for s in range(N_CHIPS)], axis=1)
    mod = mod_me.reshape(9, d)
    sh1, sc1, gt1, sh2, sc2, gt2, sh3, sc3, gt3 = [mod[k:k + 1] for k in range(9)]

    f = w_ffn1_out.shape[1] * N_CHIPS

    col = lambda a: a.reshape(gn, 1)
    lrl_c, li_c = col(ssm_lam_re_log), col(ssm_lam_im)
    ldt_c = col(jnp.broadcast_to(ssm_log_dt.reshape(ngrp, 1), (ngrp, SSM_STATE)))
    b_re2, b_im2 = ssm_b_re.reshape(gn, SSM_GROUP), ssm_b_im.reshape(gn, SSM_GROUP)
    lrdt_c, ang_c, bb_re, bb_im = _ssm_prep("ssm_prep", lrl_c, li_c, ldt_c, b_re2, b_im2)
    lrdt, ang = lrdt_c.reshape(1, gn), ang_c.reshape(1, gn)
    tr = lambda a: jnp.swapaxes(a, 1, 2)
    bbd = [_blockdiag_b(v, sw).astype(BF16) for v in (bb_re, bb_im)]
    ccd = [_blockdiag_c(v[0], sw).astype(BF16) for v in (ssm_c_re, ssm_c_im)]
    bbd_t, ccd_t = [tr(v) for v in bbd], [tr(v) for v in ccd]
    early = [n for n in SMALL if n not in SMALL_LATE]
    packs = {}
    for tag, names, extra in (("early", early, []), ("late", SMALL_LATE, [jnp.zeros((128,), F32)])):
        packs[tag] = [_pack([src[n] for n in names] + extra) for src in (wt, mom, var)]
    shadow = [lrdt, ang, ldt_c, *bbd, *ccd, *bbd_t, *ccd_t, *packs["early"], *packs["late"]]

    h1 = _norm_fwd("ffn1_norm", x2d, g_ffn1, sc1, sh1, deps=(tok,))
    x1, h2, sav1 = _ffn_fwd("ffn1", x2d, h1, gt1, lambda after: weights(0, after, *shadow)['w_ffn1_in'],
                            lambda after: weights(1, after)['w_ffn1_out'].reshape(f, d), (g_mix, sc2, sh2))
    weights(2, h2)
    wo = full['w_out'].reshape(d, d)
    uin = _mm("mix_in", h2, full['w_in'], b_stacked=True, tm=1024, tn=768, out_dtype=F32, j_outer=True)
    p_pool, z_pool = _pool_fwd("pool_fwd", uin, pool_w[0], pool_b, pool_scale, pw)
    y_pool = _mm("pool_up", p_pool, full['w_pool_up'], b_stacked=True, tm=1024, tn=d)
    y_s, ge, s_re, s_im = _ssm_fwd("ssm_fwd", uin, lrdt, ang, *bbd, *ccd, ssm_d, sw)
    gv = _mm("glu_in", ge, full['w_glu'], b_stacked=True, tm=1024, tn=d)
    sg = _glu_fwd("glu_act", gv, b_glu)
    y_ssm = _mm("ssm_up", sg, full['w_ssm_up'], b_stacked=True, tm=1024, tn=d)
    merged = _gates_fwd("gates", y_pool, y_ssm, uin, d, pw)
    epi, epi_ins, epi_outs = _gate_norm_epilogue(1.0, x1, gt2, (g_ffn2, sc3, sh3), t, d)
    x2, y2, h3 = _mm("mix_out", merged, wo, tm=512, tn=d, epi=epi, epi_ins=epi_ins, epi_outs=epi_outs)

    x3, _, sav3 = _ffn_fwd("ffn2", x2, h3, gt3, lambda after: weights(3, after)['w_ffn2_in'],
                           lambda after: weights(3, after)['w_ffn2_out'].reshape(f, d), None)

    dx3, dy3, dg_final, loss_v, dgt3 = _loss_bwd("loss", x3, tgt, _row(g_final), sav3[4], gt3, 0.5)

    gs = {}
    rs_open = []

    def start_rs(names, gbs):
        send, recv, g_thru, land_thru, token = _rs_start("rs_start_" + names[0], gbs)
        rs_open.append((names, send, recv, g_thru, land_thru))
        return token

    dx2, dy2, dsh3, dsc3, gs['g_ffn2'], dgt2 = _ffn_bwd(
        "ffn2b", dx3, dy3, sav3, g_ffn2, sc3, full['w_ffn2_in'], full['w_ffn2_out'].reshape(f, d),
        lambda key, g: start_rs(['w_ffn2_' + key], [g]), (y2, gt2, 1.0))

    dmerged = _mm("mix_dmerged", dy2, wo, tb=True, tn=d)
    g_wo = _mm("mix_dwout", merged, dy2, ta=True, tm=d, tn=d, tk=2048).reshape(N_CHIPS, d // N_CHIPS, d)
    dy_pool, dy_ssm, duin = _gates_bwd("gates_bwd", dmerged, y_pool, y_ssm, uin, d, pw)

    dp = _mm("pool_dup", dy_pool, full['w_pool_up'], tb=True, b_stacked=True, tm=1024, tn=pw, tk=d)
    g_wpu = _mm("pool_dwup", p_pool, dy_pool, ta=True, out_stacked=True, tm=pw, tn=d, tk=4096)
    duin, gs['pool_w'], gs['pool_b'], gs['pool_scale'] = _pool_bwd("pool_bwd", dp, z_pool, pool_w[0], pool_b, pool_scale, duin)

    dsg = _mm("ssm_dup", dy_ssm, full['w_ssm_up'], tb=True, b_stacked=True, tm=1024, tn=sw, tk=d)
    g_wsu = _mm("ssm_dwup", sg, dy_ssm, ta=True, out_stacked=True, tm=sw, tn=d, tk=4096)
    tok = start_rs(['w_out', 'w_pool_up', 'w_ssm_up'], [g_wo, g_wpu, g_wsu])
    dgv, gs['b_glu'] = _glu_bwd("glu_bwd", dsg, gv, b_glu, deps=(tok,))
    dge = _mm("glu_dge", dgv, full['w_glu'], tb=True, b_stacked=True, tm=1024, tn=sw, tk=d, out_dtype=F32)
    g_wglu = _mm("glu_dw", ge, dgv, ta=True, out_stacked=True, tm=sw, tn=d, tk=4096)
    (duin, g_abre, g_abim, g_bbd_re, g_bbd_im, g_ccd_re, g_ccd_im, gs['ssm_d']) = _ssm_bwd(
        "ssm_bwd", dge, y_s, uin, s_re, s_im, lrdt, ang, *bbd_t, *ccd_t, ssm_d, sw, duin)
    gs['ssm_c_re'], gs['ssm_c_im'] = _diag_of_c(g_ccd_re, sw), _diag_of_c(g_ccd_im, sw)
    d_lrl, d_li, d_ldt, d_bre, d_bim = _ssm_param_bwd(
        "ssm_param_bwd", lrl_c, li_c, ldt_c, b_re2, b_im2, g_abre.reshape(gn, 1), g_abim.reshape(gn, 1),
        _diag_of_b(g_bbd_re, sw), _diag_of_b(g_bbd_im, sw))
    gs['ssm_lam_re_log'], gs['ssm_lam_im'] = d_lrl, d_li
    gs['ssm_log_dt'] = jnp.sum(d_ldt.reshape(ngrp, SSM_STATE), axis=1)
    gs['ssm_b_re'], gs['ssm_b_im'] = d_bre, d_bim

    g_win =_mm("mix_dwin", h2, duin, ta=True, out_stacked=True, tm=d, tn=768, tk=4096)
    tok = start_rs(['w_glu', 'w_in'], [g_wglu, g_win])
    dh2 = _mm("mix_dh", duin, full['w_in'], tb=True, b_stacked=True, tn=d, tk=3072, deps=(tok,))
    dx1, dy1, dsh2, dsc2, gs['g_mix'], dgt1 = _norm_bwd("mix_dnorm", dh2, x1, dx2, g_mix, sc2, sav1[4], gt1, 0.5)

    gs['g_final'] = dg_final
    sg_blk = _pack([gs[n] for n in early])
    sg_send, sg_recv, sg_blk, sg_land, tok = _bcast_start("sg_start", sg_blk)

    dx0, _, dsh1, dsc1, gs['g_ffn1'], _ = _ffn_bwd(
        "ffn1b", dx1, dy1, sav1, g_ffn1, sc1, full['w_ffn1_in'], full['w_ffn1_out'].reshape(f, d),
        lambda key, g: start_rs(['w_ffn1_' + key], [g]), None, deps=(tok,))

    gs['b_ada'] = jnp.concatenate([dsh1, dsc1, dgt1, dsh2, dsc2, dgt2, dsh3, dsc3, dgt3], axis=1)
    lt_blk = _pack([gs[n] for n in SMALL_LATE] + [loss_v])
    lt_send, lt_recv, lt_blk, lt_land, tok = _bcast_start("late_start", lt_blk)

    grads, delta, new_m, new_v = {}, {}, {}, {}

    def small_update(tag, names, blk, land):
        g8 = lax.dynamic_update_slice(land, blk[None], (b_me, 0, 0)).reshape(-1, 128)
        w_pack, m_pack, v_pack = packs[tag]
        res = _adamw_small("adamw_small_" + tag, w_pack, g8, m_pack, v_pack)
        for dst, packed in zip((grads, delta, new_m, new_v), res):
            for n, val in zip(names, _unpack(packed, [wt[n].shape for n in names])):
                dst[n] = val
        return g8

    after = tok
    for k, (names, send, recv, g_thru, land_thru) in enumerate(rs_open):
        g_done, land_done = _rs_wait("rs_wait_" + names[0], g_thru, land_thru, send, recv, after)
        for n, g_own, land in zip(names, g_done, land_done):
            g_sum = _rs_sum("rs_sum_" + n, g_own, land)
            dl, mn, vn = _adamw("adamw_" + n, wt[n][0], g_sum, mom[n][0], var[n][0])
            grads[n], delta[n], new_m[n], new_v[n] = g_sum[None], dl[None], mn[None], vn[None]
            after = dl
        if k == len(rs_open) - 3:
            small_update("early", early, sg_blk, _bcast_wait("sg_wait", sg_blk, sg_land, sg_send, sg_recv, after))

    lt_land = _bcast_wait("late_wait", lt_blk, lt_land, lt_send, lt_recv, after)
    late8 = small_update("late", SMALL_LATE, lt_blk, lt_land).reshape(N_DEV, -1)
    loss = jnp.sum(late8[:, 10 * d])
    dmod_sh = lax.dynamic_slice(late8, (0, s_me * ncol), (N_DEV, ncol))
    g_w_ada = _ada_bwd("ada_bwd", c_all, dmod_sh)
    dl, mn, vn = _adamw("adamw_w_ada", w_ada[0], g_w_ada, m_w_ada[0], v_w_ada[0])
    grads['w_ada'], delta['w_ada'], new_m['w_ada'], new_v['w_ada'] = g_w_ada[None], dl[None], mn[None], vn[None]

    return (loss, dx0[None], *[grads[n] for n in WEIGHTS], *[delta[n] for n in WEIGHTS],
            *[new_m[n] for n in WEIGHTS], *[new_v[n] for n in WEIGHTS])
```

```python
import functools
import math

import jax
import jax.numpy as jnp
from jax import lax
from jax.experimental import pallas as pl
from jax.experimental.pallas import tpu as pltpu

F32 = jnp.float32
BF16 = jnp.bfloat16
MESH = pl.DeviceIdType.MESH

EPS = 1e-6
POOL_WINDOWS = (2, 4, 8, 16)
POOL_HALO = 16
SSM_GROUP = 16
SSM_STATE = 64
SSM_BLOCKS = 4
N_DEV = 8
N_CHIPS = 4
ADAM_LR = 0.001
ADAM_B1 = 0.9
ADAM_B2 = 0.999
ADAM_EPS = 1e-08
ADAM_WD = 0.01
ADAM_STEP = 10
VMEM_LIMIT = 56 * 1024 * 1024


ANY_SPEC = pl.BlockSpec(memory_space=pl.ANY)
HBM_SPEC = pl.BlockSpec(memory_space=pltpu.HBM)
SEM_SPEC = pl.BlockSpec(memory_space=pltpu.SEMAPHORE)
EFFECT = pltpu.SideEffectType.DATAFLOW_SIDE_EFFECTING


def _hbm(a):
    return pltpu.with_memory_space_constraint(a, pltpu.HBM)


def _pcall(body, **kw):
    return pl.pallas_call(body, **kw)


def _params(*sem):
    return pltpu.CompilerParams(dimension_semantics=sem, vmem_limit_bytes=VMEM_LIMIT)


def _pick(n, cap, mult=128):
    if n <= cap:
        return n
    best = None
    for d in range(mult, cap + 1, mult):
        if n % d == 0:
            best = d
    assert best is not None, (n, cap, mult)
    return best


def _sigmoid(v):
    return 1.0 / (1.0 + jnp.exp(-v))


def _rowwise(name, fn, ins, params, outs, reds, tm, deps=()):
    t = ins[0][0].shape[0]
    tm = min(tm, t)
    nb = t // tm
    ni, npar, no, nd = len(ins), len(params), len(outs), len(deps)

    def body(*refs):
        iv = [r[...] for r in refs[:ni]]
        pv = [r[...] for r in refs[ni:ni + npar]]
        o_refs = refs[ni + npar + nd:ni + npar + nd + no]
        r_refs = refs[ni + npar + nd + no:]
        ovals, rvals = fn(iv, pv)
        for o_ref, val in zip(o_refs, ovals):
            off = 0
            if isinstance(val, tuple) and val[0] == "at":
                _, off, val = val
            parts = val if isinstance(val, (list, tuple)) else [val]
            for p in parts:
                o_ref[:, off:off + p.shape[1]] = p.astype(o_ref.dtype)
                off += p.shape[1]
        if r_refs:
            @pl.when(pl.program_id(0) == 0)
            def _():
                for r in r_refs:
                    r[...] = jnp.zeros_like(r)
            for r, val in zip(r_refs, rvals):
                r[...] += val

    in_specs = [pl.BlockSpec((tm, w), functools.partial(lambda i, cb: (i, cb), cb=cb)) for (_, w, cb) in ins]
    in_specs += [pl.BlockSpec(p.shape, lambda i: (0, 0)) for p in params]
    in_specs += [ANY_SPEC] * nd
    out_shape = [jax.ShapeDtypeStruct((t, w), dt) for (w, dt) in outs]
    out_shape += [jax.ShapeDtypeStruct((1, w), F32) for w in reds]
    out_specs = [pl.BlockSpec((tm, w), lambda i: (i, 0)) for (w, _) in outs]
    out_specs += [pl.BlockSpec((1, w), lambda i: (0, 0)) for w in reds]
    res = _pcall(body, name=name, grid=(nb,), in_specs=in_specs, out_specs=out_specs, out_shape=out_shape,
                 compiler_params=_params("arbitrary"))(*[a for a, _, _ in ins], *params, *deps)
    return res


def _colsum(v):
    return jnp.sum(v, axis=0, keepdims=True)


def _mm(name, a, b, *, ta=False, tb=False, b_stacked=False, out_stacked=False, tm=512, tn=1024, tk=2816,
        out_dtype=BF16, epi=None, epi_ins=(), epi_outs=None, epi_reds=(), deps=(), j_outer=False, a_halves=False,
        b_halves=False, b_resident=False):
    if a_halves:
        _, m, kdim = a.shape
        kdim *= 2
    elif ta:
        kdim, m = a.shape
    else:
        m, kdim = a.shape
    ns = None
    if b_stacked:
        ns = b.shape[2]
        n = b.shape[1] if tb else N_CHIPS * ns
        assert kdim == (N_CHIPS * ns if tb else b.shape[1]), (name, a.shape, b.shape)
    else:
        if b_halves:
            n = 2 * b.shape[2]
            assert kdim == b.shape[1] and not tb, (name, a.shape, b.shape)
        else:
            n = b.shape[0] if tb else b.shape[1]
            assert kdim == (b.shape[1] if tb else b.shape[0]), (name, a.shape, b.shape)
        if out_stacked:
            ns = n // N_CHIPS

    def shards(want):
        return max(g for g in (1, 2, 4) if g * ns <= max(want, ns))

    tm = _pick(m, tm, 128 if ta else 8)
    gn = gk = 1
    if (b_stacked and not tb) or out_stacked:
        gn = shards(min(tn, n // 2) if b_halves else tn)
        tn = gn * ns
    else:
        tn = _pick(n, tn)
    if b_stacked and tb:
        gk = shards(tk)
        tk = gk * ns
    else:
        tk = _pick(kdim, tk, 8 if ta else 128)
    nm, nn, nk = m // tm, n // tn, kdim // tk

    def ij(f):
        return (lambda g0, g1, k: f(g1, g0, k)) if j_outer else f

    if a_halves:
        assert b_stacked and tb and gk == N_CHIPS and nk == 1, name
        a_spec = pl.BlockSpec((2, tm, kdim // 2), ij(lambda i, j, k: (0, i, 0)))
    elif ta:
        a_spec = pl.BlockSpec((tk, tm), ij(lambda i, j, k: (k, i)))
    else:
        a_spec = pl.BlockSpec((tm, tk), ij(lambda i, j, k: (i, k)))
    if b_stacked and not tb:
        b_spec = pl.BlockSpec((gn, tk, ns), ij(lambda i, j, k: (j, k, 0)))
    elif b_stacked and tb:
        b_spec = pl.BlockSpec((gk, tn, ns), ij(lambda i, j, k: (k, j, 0)))
    elif b_halves:
        bph = (n // 2) // tn
        b_spec = pl.BlockSpec((None, tk, tn), ij(lambda i, j, k: (j // bph, k, j % bph)))
    elif tb:
        b_spec = pl.BlockSpec((tn, tk), ij(lambda i, j, k: (j, k)))
    else:
        b_spec = pl.BlockSpec((tk, tn), ij(lambda i, j, k: (k, j)))
    dims = (((0 if ta else 1,), (1 if tb else 0,)), ((), ()))

    if epi_outs is None:
        if out_stacked:
            epi_outs = [((N_CHIPS, m, ns), out_dtype, (gn, tm, ns), lambda i, j: (j, i, 0))]
        else:
            epi_outs = [((m, n), out_dtype, (tm, tn), lambda i, j: (i, j))]
    ne, no, nd, nr = len(epi_ins), len(epi_outs), len(deps), len(epi_reds)
    assert not nr or (nn == 1 and not j_outer), name

    def body(a_ref, b_ref, *rest):
        e_refs = rest[:ne]
        o_refs = rest[ne + nd:ne + nd + no]
        r_refs = rest[ne + nd + no:ne + nd + no + nr]
        scratch = rest[ne + nd + no + nr:]
        av = None if a_halves else a_ref[...].astype(BF16)
        if b_stacked and not tb:
            parts = [lax.dot_general(av, b_ref[s].astype(BF16), dims, preferred_element_type=F32) for s in range(gn)]
        elif b_stacked and tb:
            p = None
            for s in range(gk):
                if a_halves:
                    a_s = a_ref[s // 2, :, (s % 2) * ns:(s % 2 + 1) * ns].astype(BF16)
                else:
                    a_s = av[:, s * ns:(s + 1) * ns]
                q = lax.dot_general(a_s, b_ref[s].astype(BF16), dims, preferred_element_type=F32)
                p = q if p is None else p + q
            parts = [p]
        else:
            parts = [lax.dot_general(av, b_ref[...].astype(BF16), dims, preferred_element_type=F32)]

        def finish(acc_parts):
            if epi is None and out_stacked:
                acc = acc_parts[0]
                for s in range(gn):
                    o_refs[0][s] = acc[:, s * ns:(s + 1) * ns].astype(out_dtype)
            elif epi is None:
                w = acc_parts[0].shape[1]
                for s, part in enumerate(acc_parts):
                    o_refs[0][:, s * w:(s + 1) * w] = part.astype(out_dtype)
            else:
                acc = acc_parts[0] if len(acc_parts) == 1 else jnp.concatenate(acc_parts, axis=1)
                vals = epi(acc, *[r[...] for r in e_refs])
                if nr:
                    vals, reds = vals

                    @pl.when(pl.program_id(0) == 0)
                    def _():
                        for r in r_refs:
                            r[...] = jnp.zeros_like(r)
                    for r, val in zip(r_refs, reds):
                        r[...] += val
                for o_ref, val in zip(o_refs, vals):
                    o_ref[...] = val.astype(o_ref.dtype)

        if nk == 1:
            finish(parts)
        else:
            acc_ref = scratch[0]
            k = pl.program_id(2)
            w = parts[0].shape[1]

            @pl.when(k == 0)
            def _():
                for s, part in enumerate(parts):
                    acc_ref[:, s * w:(s + 1) * w] = part

            @pl.when(k > 0)
            def _():
                for s, part in enumerate(parts):
                    acc_ref[:, s * w:(s + 1) * w] += part

            @pl.when(k == nk - 1)
            def _():
                finish([acc_ref[...]])

    def _ij(f):
        return ij(lambda i, j, k: f(i, j))

    if b_resident:
        assert nk == 1 and (nn == 1 or j_outer), name
        b_spec = pl.BlockSpec(b_spec.block_shape, b_spec.index_map, pipeline_mode=pl.Buffered(1))
    in_specs = [a_spec, b_spec] + [pl.BlockSpec(blk, _ij(f)) for (_, blk, f) in epi_ins] + [ANY_SPEC] * nd
    out_specs = [pl.BlockSpec(blk, _ij(f)) for (_, _, blk, f) in epi_outs]
    out_specs += [pl.BlockSpec((1, w), lambda *_: (0, 0)) for w in epi_reds]
    out_shape = [jax.ShapeDtypeStruct(s, dt) for (s, dt, _, _) in epi_outs] + [jax.ShapeDtypeStruct((1, w), F32) for w in epi_reds]
    scratch = [pltpu.VMEM((tm, tn), F32)] if nk > 1 else []
    grid = (nn, nm, nk) if j_outer else (nm, nn, nk)
    res = _pcall(body, name=name, grid=grid, in_specs=in_specs, out_specs=out_specs, out_shape=out_shape, scratch_shapes=scratch,
                 compiler_params=_params(*(("arbitrary",) * 3 if nr else ("parallel", "parallel", "arbitrary"))))(
                     a, b, *[x for x, _, _ in epi_ins], *deps)
    return res[0] if len(res) == 1 else res


def _norm_fwd(name, x, g, sc, sh, deps=()):
    d = x.shape[1]

    def fn(iv, pv):
        (xv,), (gv, scv, shv) = iv, pv
        r = lax.rsqrt(jnp.mean(xv * xv, axis=-1, keepdims=True) + EPS)
        return [xv * r * gv * (1.0 + scv) + shv], []

    return _rowwise(name, fn, [(x, d, 0)], [g, sc, sh], [(d, BF16)], [], 512, deps=deps)[0]


def _norm_bwd(name, dh, x, dres, g, sc, y_up, gt_up, factor_up):
    d = x.shape[1]
    up = y_up is not None

    def fn(iv, pv):
        dhv, xv, drv = iv[:3]
        gv, scv = pv[:2]
        dhv = dhv.astype(F32)
        r = lax.rsqrt(jnp.mean(xv * xv, axis=-1, keepdims=True) + EPS)
        xr = xv * r
        dn = dhv * (1.0 + scv)
        gd = gv * dn
        dx = drv + r * (gd - xr * jnp.mean(gd * xr, axis=-1, keepdims=True))
        outs, reds = [dx], [_colsum(dhv), _colsum(dhv * xr * gv), _colsum(dn * xr)]
        if up:
            fdx = factor_up * dx
            outs.append(pv[2] * fdx)
            reds.append(_colsum(fdx * iv[3].astype(F32)))
        return outs, reds

    ins = [(dh, d, 0), (x, d, 0), (dres, d, 0)] + ([(y_up, d, 0)] if up else [])
    res = _rowwise(name, fn, ins, [g, sc] + ([gt_up] if up else []), [(d, F32)] + ([(d, BF16)] if up else []),
                   [d] * (4 if up else 3), 256)
    if up:
        dx, dy_up, dsh, dsc, dg, dgt_up = res
        return dx, dy_up, dsh, dsc, dg, dgt_up
    dx, dsh, dsc, dg = res
    return dx, None, dsh, dsc, dg, None


def _loss_bwd(name, x, tgt, g, y_up, gt_up, factor_up):
    d = x.shape[1]

    def fn(iv, pv):
        (xv, tv, yv), (gv, gtv) = iv, pv
        r = lax.rsqrt(jnp.mean(xv * xv, axis=-1, keepdims=True) + EPS)
        xr = xv * r
        e = xr * gv - tv
        loss_row = 0.5 * jnp.mean(e * e, axis=-1, keepdims=True)
        dout = e * (1.0 / d)
        gd = gv * dout
        dx = r * (gd - xr * jnp.mean(gd * xr, axis=-1, keepdims=True))
        fdx = factor_up * dx
        return [dx, gtv * fdx], [_colsum(dout * xr), _colsum(loss_row * jnp.ones((1, 128), F32)), _colsum(fdx * yv.astype(F32))]

    return _rowwise(name, fn, [(x, d, 0), (tgt, d, 0), (y_up, d, 0)], [g, gt_up], [(d, F32), (d, BF16)], [d, 128, d], 256)


def _resgate_bwd(name, dx, y, gt, factor, deps=()):
    d = dx.shape[1]

    def fn(iv, pv):
        (dxv, yv), (gtv,) = iv, pv
        return [factor * gtv * dxv], [_colsum(factor * dxv * yv.astype(F32))]

    return _rowwise(name, fn, [(dx, d, 0), (y, d, 0)], [gt], [(d, BF16)], [d], 512, deps=deps)


def _swiglu_fwd(name, ab):
    f = ab.shape[1] // 2

    def fn(iv, pv):
        a, b = iv[0].astype(F32), iv[1].astype(F32)
        return [a * _sigmoid(a) * b], []

    return _rowwise(name, fn, [(ab, f, 0), (ab, f, 1)], [], [(f, BF16)], [], 256)[0]


def _swiglu_bwd(name, dact, ab, deps=()):
    f = ab.shape[1] // 2

    def fn(iv, pv):
        dv, a, b = iv[0].astype(F32), iv[1].astype(F32), iv[2].astype(F32)
        s = _sigmoid(a)
        return [[dv * b * (s * (1.0 + a * (1.0 - s))), dv * (a * s)]], []

    return _rowwise(name, fn, [(dact, f, 0), (ab, f, 0), (ab, f, 1)], [], [(2 * f, BF16)], [], 256, deps=deps)[0]


def _gates_fwd(name, y_pool, y_ssm, uin, d, pw):
    cb = (2 * pw) // d

    def fn(iv, pv):
        yp, ys, glp, gls = [v.astype(F32) for v in iv]
        return [_sigmoid(glp) * yp + _sigmoid(gls) * ys], []

    return _rowwise(name, fn, [(y_pool, d, 0), (y_ssm, d, 0), (uin, d, cb), (uin, d, cb + 1)], [], [(d, BF16)], [], 256)[0]


def _gates_bwd(name, dm, y_pool, y_ssm, uin, d, pw):
    cb = (2 * pw) // d

    def fn(iv, pv):
        dmv, yp, ys, glp, gls = [v.astype(F32) for v in iv]
        sp, ss = _sigmoid(glp), _sigmoid(gls)
        return [dmv * sp, dmv * ss, ("at", 2 * pw, [dmv * yp * sp * (1.0 - sp), dmv * ys * ss * (1.0 - ss)])], []

    return _rowwise(name, fn, [(dm, d, 0), (y_pool, d, 0), (y_ssm, d, 0), (uin, d, cb), (uin, d, cb + 1)], [],
                    [(d, BF16), (d, BF16), (2 * pw + 2 * d, BF16)], [], 256)


def _glu_fwd(name, gv, b_glu):
    w = gv.shape[1] // 2

    def fn(iv, pv):
        gvv, (bv,) = iv[0].astype(F32) + pv[0], pv
        return [gvv[:, :w] * _sigmoid(gvv[:, w:])], []

    return _rowwise(name, fn, [(gv, 2 * w, 0)], [b_glu], [(w, BF16)], [], 512)[0]


def _glu_bwd(name, dsg, gv, b_glu, deps=()):
    w = gv.shape[1] // 2

    def fn(iv, pv):
        dv = iv[0].astype(F32)
        gvv = iv[1].astype(F32) + pv[0]
        val, s = gvv[:, :w], _sigmoid(gvv[:, w:])
        dval = dv * s
        dgate = dv * val * s * (1.0 - s)
        return [[dval, dgate]], [jnp.concatenate([_colsum(dval), _colsum(dgate)], axis=1)]

    return _rowwise(name, fn, [(dsg, w, 0), (gv, 2 * w, 0)], [b_glu], [(2 * w, BF16)], [2 * w], 512, deps=deps)


def _adamw(name, w, g, m, v, tm=256):
    c = w.shape[1]

    def fn(iv, pv):
        wv, gv, mv, vv = iv
        mn = ADAM_B1 * mv + (1.0 - ADAM_B1) * gv
        vn = ADAM_B2 * vv + (1.0 - ADAM_B2) * (gv * gv)
        m_hat = mn / (1.0 - ADAM_B1 ** ADAM_STEP)
        v_hat = vn / (1.0 - ADAM_B2 ** ADAM_STEP)
        delta = -ADAM_LR * (m_hat / (jnp.sqrt(v_hat) + ADAM_EPS) + ADAM_WD * wv)
        return [delta, mn, vn], []

    return _rowwise(name, fn, [(w, c, 0), (g, c, 0), (m, c, 0), (v, c, 0)], [], [(c, F32)] * 3, [], _pick(w.shape[0], tm, 8))


def _adamw_small(name, w, g8, m, v):
    r = w.shape[0]

    def body(w_ref, g_ref, m_ref, v_ref, go_ref, d_ref, mo_ref, vo_ref):
        gv = g_ref[0:r, :]
        for k in range(1, N_DEV):
            gv = gv + g_ref[k * r:(k + 1) * r, :]
        mn = ADAM_B1 * m_ref[...] + (1.0 - ADAM_B1) * gv
        vn = ADAM_B2 * v_ref[...] + (1.0 - ADAM_B2) * (gv * gv)
        m_hat = mn / (1.0 - ADAM_B1 ** ADAM_STEP)
        v_hat = vn / (1.0 - ADAM_B2 ** ADAM_STEP)
        go_ref[...] = gv
        d_ref[...] = -ADAM_LR * (m_hat / (jnp.sqrt(v_hat) + ADAM_EPS) + ADAM_WD * w_ref[...])
        mo_ref[...] = mn
        vo_ref[...] = vn

    return _pcall(body, name=name, out_shape=[jax.ShapeDtypeStruct((r, 128), F32)] * 4,
                  compiler_params=pltpu.CompilerParams(vmem_limit_bytes=VMEM_LIMIT))(w, g8, m, v)


def _pool_fwd(name, uin, pool_w, pool_b, pool_scale, pw, tm=512):
    t = uin.shape[0]
    tm = min(tm, t)
    ng = len(POOL_WINDOWS)
    gw = pw // ng

    def body(u_ref, w_ref, b_ref, s_ref, p_ref, z_ref, ext):
        i = pl.program_id(0)

        @pl.when(i == 0)
        def _():
            ext[0:POOL_HALO, :] = jnp.zeros((POOL_HALO, pw), F32)

        u = u_ref[...]
        ext[POOL_HALO:POOL_HALO + tm, :] = u
        pos = i * tm + lax.broadcasted_iota(jnp.int32, (tm, 1), 0)
        for k, win in enumerate(POOL_WINDOWS):
            cols = slice(k * gw, (k + 1) * gw)
            acc = u[:, cols]
            for j in range(1, win):
                acc = acc + ext[POOL_HALO - j:POOL_HALO - j + tm, cols]
            cnt = jnp.minimum(pos + 1, win).astype(F32)
            z = acc / cnt - u[:, cols]
            zp = jnp.dot(z.astype(BF16), w_ref[k].astype(BF16), preferred_element_type=F32) + b_ref[:, cols]
            p_ref[:, cols] = (zp * s_ref[:, cols]).astype(BF16)
            z_ref[:, cols] = z.astype(BF16)
        ext[0:POOL_HALO, :] = u[tm - POOL_HALO:tm, :]

    return _pcall(
        body, name=name, grid=(t // tm,),
        in_specs=[pl.BlockSpec((tm, pw), lambda i: (i, 0)), pl.BlockSpec(pool_w.shape, lambda i: (0, 0, 0)),
                  pl.BlockSpec(pool_b.shape, lambda i: (0, 0)), pl.BlockSpec(pool_scale.shape, lambda i: (0, 0))],
        out_specs=[pl.BlockSpec((tm, pw), lambda i: (i, 0))] * 2,
        out_shape=[jax.ShapeDtypeStruct((t, pw), BF16)] * 2,
        scratch_shapes=[pltpu.VMEM((POOL_HALO + tm, pw), F32)],
        compiler_params=_params("arbitrary"))(uin, pool_w, pool_b, pool_scale)


def _pool_bwd(name, dp, z, pool_w, pool_b, pool_scale, duin, tm=512):
    t, pw = z.shape
    tm = min(tm, t)
    nb = t // tm
    ng = len(POOL_WINDOWS)
    gw = pw // ng

    def body(dp_ref, z_ref, w_ref, b_ref, s_ref, duin_ref, du_ref, dw_ref, db_ref, ds_ref, ext):
        i = pl.program_id(0)

        @pl.when(i == 0)
        def _():
            ext[tm:tm + POOL_HALO, :] = jnp.zeros((POOL_HALO, pw), F32)
            dw_ref[...] = jnp.zeros_like(dw_ref)
            db_ref[...] = jnp.zeros_like(db_ref)
            ds_ref[...] = jnp.zeros_like(ds_ref)

        pos = (nb - 1 - i) * tm + lax.broadcasted_iota(jnp.int32, (tm, 1), 0)
        for k, win in enumerate(POOL_WINDOWS):
            cols = slice(k * gw, (k + 1) * gw)
            zk = z_ref[:, cols]
            dpk = dp_ref[:, cols].astype(F32)
            wk = w_ref[k].astype(BF16)
            zp = jnp.dot(zk, wk, preferred_element_type=F32) + b_ref[:, cols]
            ds_ref[:, cols] += _colsum(dpk * zp)
            dzp = dpk * s_ref[:, cols]
            db_ref[:, cols] += _colsum(dzp)
            dzpb = dzp.astype(BF16)
            dz = lax.dot_general(dzpb, wk, (((1,), (1,)), ((), ())), preferred_element_type=F32)
            dw_ref[k] += lax.dot_general(zk, dzpb, (((0,), (0,)), ((), ())), preferred_element_type=F32)
            cnt = jnp.minimum(pos + 1, win).astype(F32)
            r = dz / cnt
            ext[0:tm, cols] = r
            acc = r - dz
            for j in range(1, win):
                acc = acc + ext[j:j + tm, cols]
            du_ref[:, cols] = acc.astype(BF16)
        ext[tm:tm + POOL_HALO, :] = ext[0:POOL_HALO, :]

    rev = lambda i: (nb - 1 - i, 0)
    return _pcall(
        body, name=name, grid=(nb,),
        in_specs=[pl.BlockSpec((tm, pw), rev), pl.BlockSpec((tm, pw), rev), pl.BlockSpec(pool_w.shape, lambda i: (0, 0, 0)),
                  pl.BlockSpec(pool_b.shape, lambda i: (0, 0)), pl.BlockSpec(pool_scale.shape, lambda i: (0, 0)), ANY_SPEC],
        out_specs=[pl.BlockSpec((tm, pw), rev), pl.BlockSpec(pool_w.shape, lambda i: (0, 0, 0)),
                   pl.BlockSpec((1, pw), lambda i: (0, 0)), pl.BlockSpec((1, pw), lambda i: (0, 0))],
        out_shape=[jax.ShapeDtypeStruct(duin.shape, BF16), jax.ShapeDtypeStruct(pool_w.shape, F32),
                   jax.ShapeDtypeStruct((1, pw), F32), jax.ShapeDtypeStruct((1, pw), F32)],
        scratch_shapes=[pltpu.VMEM((tm + POOL_HALO, pw), F32)], input_output_aliases={5: 0},
        compiler_params=_params("arbitrary"))(dp, z, pool_w, pool_b, pool_scale, duin)


def _ssm_disc(lrl, li, ldt):
    lr = -jnp.exp(lrl)
    dt = jnp.exp(ldt)
    mag = jnp.exp(lr * dt)
    ang = li * dt
    ab_re = mag * jnp.cos(ang)
    ab_im = mag * jnp.sin(ang)
    num_re = ab_re - 1.0
    num_im = ab_im
    den = lr * lr + li * li
    f_re = (num_re * lr + num_im * li) / den
    f_im = (num_im * lr - num_re * li) / den
    return lr, dt, mag, ang, ab_re, ab_im, num_re, num_im, den, f_re, f_im


def _ssm_prep(name, lrl, li, ldt, b_re, b_im):
    gn, h = b_re.shape

    def body(lrl_ref, li_ref, ldt_ref, br_ref, bi_ref, lrdt_ref, ang_ref, bbr_ref, bbi_ref):
        lr, dt, _, ang, _, _, _, _, _, f_re, f_im = _ssm_disc(lrl_ref[...], li_ref[...], ldt_ref[...])
        lrdt_ref[...] = lr * dt
        ang_ref[...] = ang
        br, bi = br_ref[...], bi_ref[...]
        bbr_ref[...] = f_re * br - f_im * bi
        bbi_ref[...] = f_re * bi + f_im * br

    col = jax.ShapeDtypeStruct((gn, 1), F32)
    mat = jax.ShapeDtypeStruct((gn, h), F32)
    return _pcall(body, name=name, out_shape=[col, col, mat, mat])(lrl, li, ldt, b_re, b_im)


def _ssm_param_bwd(name, lrl, li, ldt, b_re, b_im, g_abre, g_abim, g_bbre, g_bbim):
    gn, h = b_re.shape

    def body(lrl_ref, li_ref, ldt_ref, br_ref, bi_ref, gar_ref, gai_ref, gbr_ref, gbi_ref,
             dlrl_ref, dli_ref, dldt_ref, dbr_ref, dbi_ref):
        li_v = li_ref[...]
        lr, dt, mag, ang, ab_re, ab_im, num_re, num_im, den, f_re, f_im = _ssm_disc(lrl_ref[...], li_v, ldt_ref[...])
        br, bi = br_ref[...], bi_ref[...]
        gbr, gbi = gbr_ref[...], gbi_ref[...]
        g_fre = jnp.sum(gbr * br + gbi * bi, axis=1, keepdims=True)
        g_fim = jnp.sum(gbi * br - gbr * bi, axis=1, keepdims=True)
        dbr_ref[...] = gbr * f_re + gbi * f_im
        dbi_ref[...] = gbi * f_re - gbr * f_im
        g_num_re = (g_fre * lr - g_fim * li_v) / den
        g_num_im = (g_fre * li_v + g_fim * lr) / den
        g_den = -(g_fre * f_re + g_fim * f_im) / den
        g_lr = (g_fre * num_re + g_fim * num_im) / den + g_den * 2.0 * lr
        g_li = (g_fre * num_im - g_fim * num_re) / den + g_den * 2.0 * li_v
        g_are = gar_ref[...] + g_num_re
        g_aim = gai_ref[...] + g_num_im
        g_mag = g_are * jnp.cos(ang) + g_aim * jnp.sin(ang)
        g_ang = g_aim * ab_re - g_are * ab_im
        g_lrdt = g_mag * mag
        g_lr = g_lr + g_lrdt * dt
        g_dt = g_lrdt * lr + g_ang * li_v
        g_li = g_li + g_ang * dt
        dlrl_ref[...] = g_lr * lr
        dli_ref[...] = g_li
        dldt_ref[...] = g_dt * dt

    col = jax.ShapeDtypeStruct((gn, 1), F32)
    mat = jax.ShapeDtypeStruct((gn, h), F32)
    return _pcall(body, name=name, out_shape=[col, col, col, mat, mat])(lrl, li, ldt, b_re, b_im, g_abre, g_abim, g_bbre, g_bbim)


def _pow_rows(lrdt, ang, k):
    mag = jnp.exp(k * lrdt)
    return mag * jnp.cos(k * ang), mag * jnp.sin(k * ang)


def _ssm_chunk(t):
    return 256 if t >= 2048 else 128


def _to_segments(dst, srcs, jn):
    for q, src in enumerate(srcs):
        for j in range(jn):
            dst[8 * j:8 * j + 8, 128 * q:128 * (q + 1)] = src[pl.ds(j, 8, stride=jn), :]


def _from_segments(dst, src, q, jn, dtype):
    for s in range(8):
        dst[s * jn:(s + 1) * jn, 128 * q:128 * (q + 1)] = src[q, pl.ds(s, jn, stride=8), :].astype(dtype)


def _fill_rows8(dst_re, dst_im, v_re, v_im):
    for j in range(v_re.shape[0]):
        dst_re[8 * j:8 * j + 8, :] = jnp.broadcast_to(v_re[j:j + 1, :], (8, v_re.shape[1]))
        dst_im[8 * j:8 * j + 8, :] = jnp.broadcast_to(v_im[j:j + 1, :], (8, v_im.shape[1]))


def _cmul(ar, ai, br, bi):
    return ar * br - ai * bi, ar * bi + ai * br


def _cmul_conj(ar, ai, br, bi):
    return ar * br + ai * bi, ar * bi - ai * br


def _ssm_fwd(name, uin, lrdt, ang, bb_re, bb_im, cc_re, cc_im, d_skip, sw):
    t = uin.shape[0]
    gn = lrdt.shape[1]
    lc = _ssm_chunk(t)
    jn = lc // 8
    ub, sb = sw // SSM_BLOCKS, gn // SSM_BLOCKS
    nq = sw // 128
    assert ub == 128 and nq == SSM_BLOCKS

    def body(*refs):
        u_refs = refs[:nq]
        (lrdt_ref, ang_ref, bbr_ref, bbi_ref, ccr_ref, cci_ref, d_ref, y_ref, ge_ref, sre_ref, sim_ref,
         p_re, p_im, a_re, a_im, up, yp, gp, cst_re, cst_im, car_re, car_im) = refs[nq:]
        i = pl.program_id(0)
        lrdt_v, ang_v = lrdt_ref[...], ang_ref[...]

        @pl.when(i == 0)
        def _():
            k = (lax.broadcasted_iota(jnp.int32, (jn, 1), 0) + 1).astype(F32)
            _fill_rows8(p_re, p_im, *_pow_rows(lrdt_v, ang_v, k))
            car_re[...] = jnp.zeros_like(car_re)
            car_im[...] = jnp.zeros_like(car_im)

        _to_segments(up, u_refs, jn)
        u = up[...]
        ubf = u.astype(BF16)
        for q in range(SSM_BLOCKS):
            uq = ubf[:, q * ub:(q + 1) * ub]
            a_re[:, q * sb:(q + 1) * sb] = jnp.dot(uq, bbr_ref[q], preferred_element_type=F32)
            a_im[:, q * sb:(q + 1) * sb] = jnp.dot(uq, bbi_ref[q], preferred_element_type=F32)
        a1r, a1i = _pow_rows(lrdt_v, ang_v, 1.0)
        ajr, aji = _pow_rows(lrdt_v, ang_v, float(jn))
        for q in range(SSM_BLOCKS):
            cols = slice(q * sb, (q + 1) * sb)
            ar8 = jnp.broadcast_to(a1r[:, cols], (8, sb))
            ai8 = jnp.broadcast_to(a1i[:, cols], (8, sb))

            def step(j, carry, cols=cols, ar8=ar8, ai8=ai8):
                sr, si = carry
                rows = pl.ds(pl.multiple_of(j * 8, 8), 8)
                mr, mi = _cmul(ar8, ai8, sr, si)
                nr, ni = mr + a_re[rows, cols], mi + a_im[rows, cols]
                a_re[rows, cols] = nr
                a_im[rows, cols] = ni
                return nr, ni

            lax.fori_loop(1, jn, step, (a_re[0:8, cols], a_im[0:8, cols]), unroll=4)
        er, ei = a_re[lc - 8:lc, :], a_im[lc - 8:lc, :]
        gr, gi = car_re[...], car_im[...]
        for s in range(8):
            cst_re[s:s + 1, :] = gr
            cst_im[s:s + 1, :] = gi
            mr, mi = _cmul(ajr, aji, gr, gi)
            gr, gi = mr + er[s:s + 1, :], mi + ei[s:s + 1, :]
        car_re[...] = gr
        car_im[...] = gi
        for q in range(SSM_BLOCKS):
            cols = slice(q * sb, (q + 1) * sb)
            cr = jnp.tile(cst_re[:, cols], (jn, 1))
            ci = jnp.tile(cst_im[:, cols], (jn, 1))
            mr, mi = _cmul(p_re[:, cols], p_im[:, cols], cr, ci)
            srb, sib = (a_re[:, cols] + mr).astype(BF16), (a_im[:, cols] + mi).astype(BF16)
            sre_ref[:, cols] = srb
            sim_ref[:, cols] = sib
            ycols = slice(q * ub, (q + 1) * ub)
            y = (jnp.dot(srb, ccr_ref[q], preferred_element_type=F32) - jnp.dot(sib, cci_ref[q], preferred_element_type=F32)
                 + d_ref[:, ycols] * u[:, ycols])
            yp[q] = y
            gp[q] = 0.5 * y * (1.0 + lax.erf(y * (1.0 / math.sqrt(2.0))))
            _from_segments(y_ref, yp, q, jn, F32)
            _from_segments(ge_ref, gp, q, jn, BF16)

    row = lambda i: (0, 0)
    blk3 = lambda i: (0, 0, 0)
    return _pcall(
        body, name=name, grid=(t // lc,),
        in_specs=[pl.BlockSpec((lc, 128), functools.partial(lambda i, q: (i, nq + q), q=q)) for q in range(nq)]
        + [pl.BlockSpec((1, gn), row), pl.BlockSpec((1, gn), row),
           pl.BlockSpec(bb_re.shape, blk3), pl.BlockSpec(bb_im.shape, blk3),
           pl.BlockSpec(cc_re.shape, blk3), pl.BlockSpec(cc_im.shape, blk3), pl.BlockSpec((1, sw), row)],
        out_specs=[pl.BlockSpec((lc, sw), lambda i: (i, 0)), pl.BlockSpec((lc, sw), lambda i: (i, 0)),
                   pl.BlockSpec((lc, gn), lambda i: (i, 0)), pl.BlockSpec((lc, gn), lambda i: (i, 0))],
        out_shape=[jax.ShapeDtypeStruct((t, sw), F32), jax.ShapeDtypeStruct((t, sw), BF16),
                   jax.ShapeDtypeStruct((t, gn), BF16), jax.ShapeDtypeStruct((t, gn), BF16)],
        scratch_shapes=[pltpu.VMEM((lc, gn), F32), pltpu.VMEM((lc, gn), F32), pltpu.VMEM((lc, gn), F32), pltpu.VMEM((lc, gn), F32),
                        pltpu.VMEM((lc, sw), F32), pltpu.VMEM((nq, lc, 128), F32), pltpu.VMEM((nq, lc, 128), F32),
                        pltpu.VMEM((8, gn), F32), pltpu.VMEM((8, gn), F32), pltpu.VMEM((1, gn), F32), pltpu.VMEM((1, gn), F32)],
        compiler_params=_params("arbitrary"))(*([uin] * nq), lrdt, ang, bb_re, bb_im, cc_re, cc_im, d_skip)


def _ssm_bwd(name, dge, y, uin, s_re, s_im, lrdt, ang, bbt_re, bbt_im, cct_re, cct_im, d_skip, sw, duin):
    t = uin.shape[0]
    gn = lrdt.shape[1]
    lc = _ssm_chunk(t)
    nb = t // lc
    jn = lc // 8
    ub, sb = sw // SSM_BLOCKS, gn // SSM_BLOCKS
    nq = sw // 128
    tail = 16

    def body(*refs):
        dge_refs, y_refs, u_refs = refs[:nq], refs[nq:2 * nq], refs[2 * nq:3 * nq]
        (sre_ref, sim_ref, tre_ref, tim_ref, lrdt_ref, ang_ref, btr_ref, bti_ref, ctr_ref, cti_ref, d_ref, duin_ref,
         du_ref, dar_ref, dai_ref, dbr_ref, dbi_ref, dcr_ref, dci_ref, dd_ref,
         q_re, q_im, a_re, a_im, dgp, yp, up, dup, cst_re, cst_im, sp_re, sp_im, car_re, car_im) = refs[3 * nq:]
        i = pl.program_id(0)
        lrdt_v, ang_v = lrdt_ref[...], ang_ref[...]

        @pl.when(i == 0)
        def _():
            k = (jn - lax.broadcasted_iota(jnp.int32, (jn, 1), 0)).astype(F32)
            _fill_rows8(q_re, q_im, *_pow_rows(lrdt_v, ang_v, k))
            car_re[...] = jnp.zeros_like(car_re)
            car_im[...] = jnp.zeros_like(car_im)
            for r in (dar_ref, dai_ref, dbr_ref, dbi_ref, dcr_ref, dci_ref, dd_ref):
                r[...] = jnp.zeros_like(r)

        _to_segments(dgp, dge_refs, jn)
        _to_segments(yp, y_refs, jn)
        _to_segments(up, u_refs, jn)
        yv = yp[...]
        u = up[...]
        cdf = 0.5 * (1.0 + lax.erf(yv * (1.0 / math.sqrt(2.0))))
        pdf = jnp.exp(-0.5 * yv * yv) * (1.0 / math.sqrt(2.0 * math.pi))
        dy = dgp[...] * (cdf + yv * pdf)
        dd_ref[...] += _colsum(dy * u)
        dyb = dy.astype(BF16)
        ubf = u.astype(BF16)
        for q in range(SSM_BLOCKS):
            dq = dyb[:, q * ub:(q + 1) * ub]
            a_re[:, q * sb:(q + 1) * sb] = jnp.dot(dq, ctr_ref[q], preferred_element_type=F32)
            a_im[:, q * sb:(q + 1) * sb] = -jnp.dot(dq, cti_ref[q], preferred_element_type=F32)
        a1r, a1i = _pow_rows(lrdt_v, ang_v, 1.0)
        ajr, aji = _pow_rows(lrdt_v, ang_v, float(jn))
        for q in range(SSM_BLOCKS):
            cols = slice(q * sb, (q + 1) * sb)
            ar8 = jnp.broadcast_to(a1r[:, cols], (8, sb))
            ai8 = jnp.broadcast_to(a1i[:, cols], (8, sb))

            def step(jj, carry, cols=cols, ar8=ar8, ai8=ai8):
                sr, si = carry
                rows = pl.ds(pl.multiple_of((jn - 2 - jj) * 8, 8), 8)
                mr, mi = _cmul_conj(ar8, ai8, sr, si)
                nr, ni = mr + a_re[rows, cols], mi + a_im[rows, cols]
                a_re[rows, cols] = nr
                a_im[rows, cols] = ni
                return nr, ni

            lax.fori_loop(0, jn - 1, step, (a_re[lc - 8:lc, cols], a_im[lc - 8:lc, cols]), unroll=4)
        er, ei = a_re[0:8, :], a_im[0:8, :]
        hr, hi = car_re[...], car_im[...]
        for s in range(7, -1, -1):
            cst_re[s:s + 1, :] = hr
            cst_im[s:s + 1, :] = hi
            mr, mi = _cmul_conj(ajr, aji, hr, hi)
            hr, hi = mr + er[s:s + 1, :], mi + ei[s:s + 1, :]
        car_re[...] = hr
        car_im[...] = hi
        first = (i == nb - 1).astype(F32)
        sp_re[0:tail, :] = tre_ref[...].astype(F32) * (1.0 - first)
        sp_im[0:tail, :] = tim_ref[...].astype(F32) * (1.0 - first)
        sp_re[tail:tail + 8, :] = sre_ref[lc - tail:lc, :].astype(F32)[tail - 8:tail]
        sp_im[tail:tail + 8, :] = sim_ref[lc - tail:lc, :].astype(F32)[tail - 8:tail]
        tn_dims = (((0,), (0,)), ((), ()))
        for q in range(SSM_BLOCKS):
            cols = slice(q * sb, (q + 1) * sb)
            ycols = slice(q * ub, (q + 1) * ub)
            cr = jnp.tile(cst_re[:, cols], (jn, 1))
            ci = jnp.tile(cst_im[:, cols], (jn, 1))
            mr, mi = _cmul_conj(q_re[:, cols], q_im[:, cols], cr, ci)
            lam_r, lam_i = a_re[:, cols] + mr, a_im[:, cols] + mi
            s_r, s_i = sre_ref[:, cols], sim_ref[:, cols]
            p0r, p0i = sp_re[tail - 1:tail + 7, cols], sp_im[tail - 1:tail + 7, cols]
            l0r, l0i, l1r, l1i = lam_r[0:8], lam_i[0:8], lam_r[8:lc], lam_i[8:lc]
            pvr, pvi = s_r.astype(F32)[0:lc - 8], s_i.astype(F32)[0:lc - 8]
            dar_ref[:, cols] += _colsum(l1r * pvr + l1i * pvi) + _colsum(l0r * p0r + l0i * p0i)
            dai_ref[:, cols] += _colsum(l1i * pvr - l1r * pvi) + _colsum(l0i * p0r - l0r * p0i)
            lrb, lib = lam_r.astype(BF16), lam_i.astype(BF16)
            dup[q] = (jnp.dot(lrb, btr_ref[q], preferred_element_type=F32)
                      + jnp.dot(lib, bti_ref[q], preferred_element_type=F32) + d_ref[:, ycols] * dy[:, ycols])
            _from_segments(du_ref, dup, q, jn, BF16)
            uq = ubf[:, ycols]
            dbr_ref[q] += lax.dot_general(uq, lrb, tn_dims, preferred_element_type=F32)
            dbi_ref[q] += lax.dot_general(uq, lib, tn_dims, preferred_element_type=F32)
            dq = dyb[:, ycols]
            dcr_ref[q] += lax.dot_general(s_r.astype(BF16), dq, tn_dims, preferred_element_type=F32)
            dci_ref[q] -= lax.dot_general(s_i.astype(BF16), dq, tn_dims, preferred_element_type=F32)

    rev = lambda i: (nb - 1 - i, 0)
    tailmap = lambda i: (jnp.maximum((nb - 1 - i) * (lc // tail) - 1, 0), 0)
    row = lambda i: (0, 0)
    blk3 = lambda i: (0, 0, 0)
    colblk = lambda base: [pl.BlockSpec((lc, 128), functools.partial(lambda i, q: (nb - 1 - i, base + q), q=q)) for q in range(nq)]
    return _pcall(
        body, name=name, grid=(nb,),
        in_specs=colblk(0) + colblk(0) + colblk(nq)
        + [pl.BlockSpec((lc, gn), rev), pl.BlockSpec((lc, gn), rev),
           pl.BlockSpec((tail, gn), tailmap), pl.BlockSpec((tail, gn), tailmap),
           pl.BlockSpec((1, gn), row), pl.BlockSpec((1, gn), row),
           pl.BlockSpec(bbt_re.shape, blk3), pl.BlockSpec(bbt_im.shape, blk3),
           pl.BlockSpec(cct_re.shape, blk3), pl.BlockSpec(cct_im.shape, blk3), pl.BlockSpec((1, sw), row), ANY_SPEC],
        out_specs=[pl.BlockSpec((lc, sw), lambda i: (nb - 1 - i, 1)), pl.BlockSpec((1, gn), row), pl.BlockSpec((1, gn), row),
                   pl.BlockSpec((SSM_BLOCKS, ub, sb), blk3), pl.BlockSpec((SSM_BLOCKS, ub, sb), blk3),
                   pl.BlockSpec((SSM_BLOCKS, sb, ub), blk3), pl.BlockSpec((SSM_BLOCKS, sb, ub), blk3),
                   pl.BlockSpec((1, sw), row)],
        out_shape=[jax.ShapeDtypeStruct(duin.shape, BF16), jax.ShapeDtypeStruct((1, gn), F32), jax.ShapeDtypeStruct((1, gn), F32),
                   jax.ShapeDtypeStruct((SSM_BLOCKS, ub, sb), F32), jax.ShapeDtypeStruct((SSM_BLOCKS, ub, sb), F32),
                   jax.ShapeDtypeStruct((SSM_BLOCKS, sb, ub), F32), jax.ShapeDtypeStruct((SSM_BLOCKS, sb, ub), F32),
                   jax.ShapeDtypeStruct((1, sw), F32)],
        scratch_shapes=[pltpu.VMEM((lc, gn), F32), pltpu.VMEM((lc, gn), F32), pltpu.VMEM((lc, gn), F32), pltpu.VMEM((lc, gn), F32),
                        pltpu.VMEM((lc, sw), F32), pltpu.VMEM((lc, sw), F32), pltpu.VMEM((lc, sw), F32),
                        pltpu.VMEM((nq, lc, 128), F32), pltpu.VMEM((8, gn), F32), pltpu.VMEM((8, gn), F32),
                        pltpu.VMEM((tail + 8, gn), F32), pltpu.VMEM((tail + 8, gn), F32),
                        pltpu.VMEM((1, gn), F32), pltpu.VMEM((1, gn), F32)],
        input_output_aliases={3 * nq + 11: 0},
        compiler_params=_params("arbitrary"))(*([dge] * nq), *([y] * nq), *([uin] * nq), s_re, s_im, s_re, s_im, lrdt, ang,
                                               bbt_re, bbt_im, cct_re, cct_im, d_skip, duin)


def _ssm_fwd_doubling(name, uin, lrdt, ang, bb_re, bb_im, cc_re, cc_im, d_skip, sw):
    t = uin.shape[0]
    gn = lrdt.shape[1]
    lc = _ssm_chunk(t)
    nsteps = int(math.log2(lc))
    pad = lc // 2
    ub, sb = sw // SSM_BLOCKS, gn // SSM_BLOCKS

    def body(u_ref, lrdt_ref, ang_ref, bbr_ref, bbi_ref, ccr_ref, cci_ref, d_ref, y_ref, ge_ref, sre_ref, sim_ref,
             p_re, p_im, a_re, a_im, b_re, b_im, car_re, car_im):
        i = pl.program_id(0)
        lrdt_v, ang_v = lrdt_ref[...], ang_ref[...]

        @pl.when(i == 0)
        def _():
            k = (lax.broadcasted_iota(jnp.int32, (lc, 1), 0) + 1).astype(F32)
            pr, pi = _pow_rows(lrdt_v, ang_v, k)
            p_re[...] = pr
            p_im[...] = pi
            zeros = jnp.zeros((pad, gn), F32)
            a_re[0:pad, :] = zeros
            a_im[0:pad, :] = zeros
            b_re[0:pad, :] = zeros
            b_im[0:pad, :] = zeros
            car_re[...] = jnp.zeros_like(car_re)
            car_im[...] = jnp.zeros_like(car_im)

        u = u_ref[...]
        ubf = u.astype(BF16)
        for q in range(SSM_BLOCKS):
            uq = ubf[:, q * ub:(q + 1) * ub]
            a_re[pad:pad + lc, q * sb:(q + 1) * sb] = jnp.dot(uq, bbr_ref[q], preferred_element_type=F32)
            a_im[pad:pad + lc, q * sb:(q + 1) * sb] = jnp.dot(uq, bbi_ref[q], preferred_element_type=F32)
        bufs = [(a_re, a_im), (b_re, b_im)]
        for q in range(SSM_BLOCKS):
            cols = slice(q * sb, (q + 1) * sb)
            for j in range(nsteps):
                dd = 1 << j
                (s_re, s_im), (d_re, d_im) = bufs[j % 2], bufs[(j + 1) % 2]
                pr, pi = _pow_rows(lrdt_v[:, cols], ang_v[:, cols], float(dd))
                cr, ci = s_re[pad:pad + lc, cols], s_im[pad:pad + lc, cols]
                hr, hi = s_re[pad - dd:pad - dd + lc, cols], s_im[pad - dd:pad - dd + lc, cols]
                d_re[pad:pad + lc, cols] = cr + (pr * hr - pi * hi)
                d_im[pad:pad + lc, cols] = ci + (pr * hi + pi * hr)
            f_re, f_im = bufs[nsteps % 2]
            cr, ci = car_re[:, cols], car_im[:, cols]
            pr, pi = p_re[:, cols], p_im[:, cols]
            sr = f_re[pad:pad + lc, cols] + (pr * cr - pi * ci)
            si = f_im[pad:pad + lc, cols] + (pr * ci + pi * cr)
            sre_ref[:, cols] = sr
            sim_ref[:, cols] = si
            car_re[:, cols] = sr[lc - 1:lc, :]
            car_im[:, cols] = si[lc - 1:lc, :]
            ycols = slice(q * ub, (q + 1) * ub)
            y = (jnp.dot(sr.astype(BF16), ccr_ref[q], preferred_element_type=F32)
                 - jnp.dot(si.astype(BF16), cci_ref[q], preferred_element_type=F32)
                 + d_ref[:, ycols] * u[:, ycols])
            y_ref[:, ycols] = y
            ge_ref[:, ycols] = (0.5 * y * (1.0 + lax.erf(y * (1.0 / math.sqrt(2.0))))).astype(BF16)

    cblk = sw // sw
    row = lambda i: (0, 0)
    blk3 = lambda i: (0, 0, 0)
    return _pcall(
        body, name=name, grid=(t // lc,),
        in_specs=[pl.BlockSpec((lc, sw), lambda i: (i, cblk)), pl.BlockSpec((1, gn), row), pl.BlockSpec((1, gn), row),
                  pl.BlockSpec(bb_re.shape, blk3), pl.BlockSpec(bb_im.shape, blk3),
                  pl.BlockSpec(cc_re.shape, blk3), pl.BlockSpec(cc_im.shape, blk3), pl.BlockSpec((1, sw), row)],
        out_specs=[pl.BlockSpec((lc, sw), lambda i: (i, 0)), pl.BlockSpec((lc, sw), lambda i: (i, 0)),
                   pl.BlockSpec((lc, gn), lambda i: (i, 0)), pl.BlockSpec((lc, gn), lambda i: (i, 0))],
        out_shape=[jax.ShapeDtypeStruct((t, sw), F32), jax.ShapeDtypeStruct((t, sw), BF16),
                   jax.ShapeDtypeStruct((t, gn), F32), jax.ShapeDtypeStruct((t, gn), F32)],
        scratch_shapes=[pltpu.VMEM((lc, gn), F32), pltpu.VMEM((lc, gn), F32),
                        pltpu.VMEM((pad + lc, gn), F32), pltpu.VMEM((pad + lc, gn), F32),
                        pltpu.VMEM((pad + lc, gn), F32), pltpu.VMEM((pad + lc, gn), F32),
                        pltpu.VMEM((1, gn), F32), pltpu.VMEM((1, gn), F32)],
        compiler_params=_params("arbitrary"))(uin, lrdt, ang, bb_re, bb_im, cc_re, cc_im, d_skip)


def _ssm_bwd_doubling(name, dge, y, uin, s_re, s_im, lrdt, ang, bbt_re, bbt_im, cct_re, cct_im, d_skip, sw):
    t = uin.shape[0]
    gn = lrdt.shape[1]
    lc = _ssm_chunk(t)
    nb = t // lc
    nsteps = int(math.log2(lc))
    pad = lc // 2
    ub, sb = sw // SSM_BLOCKS, gn // SSM_BLOCKS
    tail = 8

    def body(dge_ref, y_ref, u_ref, sre_ref, sim_ref, tre_ref, tim_ref, lrdt_ref, ang_ref, btr_ref, bti_ref, ctr_ref, cti_ref,
             d_ref, du_ref, dar_ref, dai_ref, dbr_ref, dbi_ref, dcr_ref, dci_ref, dd_ref,
             q_re, q_im, a_re, a_im, b_re, b_im, sp_re, sp_im, car_re, car_im):
        i = pl.program_id(0)
        lrdt_v, ang_v = lrdt_ref[...], ang_ref[...]

        @pl.when(i == 0)
        def _():
            k = (lc - lax.broadcasted_iota(jnp.int32, (lc, 1), 0)).astype(F32)
            qr, qi = _pow_rows(lrdt_v, ang_v, k)
            q_re[...] = qr
            q_im[...] = qi
            zeros = jnp.zeros((pad, gn), F32)
            a_re[lc:lc + pad, :] = zeros
            a_im[lc:lc + pad, :] = zeros
            b_re[lc:lc + pad, :] = zeros
            b_im[lc:lc + pad, :] = zeros
            car_re[...] = jnp.zeros_like(car_re)
            car_im[...] = jnp.zeros_like(car_im)
            for r in (dar_ref, dai_ref, dbr_ref, dbi_ref, dcr_ref, dci_ref, dd_ref):
                r[...] = jnp.zeros_like(r)

        first = (i == nb - 1).astype(F32)
        sp_re[0:tail, :] = tre_ref[...] * (1.0 - first)
        sp_im[0:tail, :] = tim_ref[...] * (1.0 - first)
        sp_re[tail:tail + lc, :] = sre_ref[...]
        sp_im[tail:tail + lc, :] = sim_ref[...]

        yv = y_ref[...]
        u = u_ref[...]
        cdf = 0.5 * (1.0 + lax.erf(yv * (1.0 / math.sqrt(2.0))))
        pdf = jnp.exp(-0.5 * yv * yv) * (1.0 / math.sqrt(2.0 * math.pi))
        dy = dge_ref[...].astype(F32) * (cdf + yv * pdf)
        dd_ref[...] += _colsum(dy * u)
        dyb = dy.astype(BF16)
        ubf = u.astype(BF16)
        for q in range(SSM_BLOCKS):
            dq = dyb[:, q * ub:(q + 1) * ub]
            a_re[0:lc, q * sb:(q + 1) * sb] = jnp.dot(dq, ctr_ref[q], preferred_element_type=F32)
            a_im[0:lc, q * sb:(q + 1) * sb] = -jnp.dot(dq, cti_ref[q], preferred_element_type=F32)
        bufs = [(a_re, a_im), (b_re, b_im)]
        tn_dims = (((0,), (0,)), ((), ()))
        for q in range(SSM_BLOCKS):
            cols = slice(q * sb, (q + 1) * sb)
            ycols = slice(q * ub, (q + 1) * ub)
            for j in range(nsteps):
                ds = 1 << j
                (s_r, s_i), (d_r, d_i) = bufs[j % 2], bufs[(j + 1) % 2]
                pr, pi = _pow_rows(lrdt_v[:, cols], ang_v[:, cols], float(ds))
                cr, ci = s_r[0:lc, cols], s_i[0:lc, cols]
                hr, hi = s_r[ds:ds + lc, cols], s_i[ds:ds + lc, cols]
                d_r[0:lc, cols] = cr + (pr * hr + pi * hi)
                d_i[0:lc, cols] = ci + (pr * hi - pi * hr)
            f_r, f_i = bufs[nsteps % 2]
            cr, ci = car_re[:, cols], car_im[:, cols]
            qr, qi = q_re[:, cols], q_im[:, cols]
            lam_r = f_r[0:lc, cols] + (qr * cr + qi * ci)
            lam_i = f_i[0:lc, cols] + (qr * ci - qi * cr)
            car_re[:, cols] = lam_r[0:1, :]
            car_im[:, cols] = lam_i[0:1, :]
            pvr, pvi = sp_re[tail - 1:tail - 1 + lc, cols], sp_im[tail - 1:tail - 1 + lc, cols]
            dar_ref[:, cols] += _colsum(lam_r * pvr + lam_i * pvi)
            dai_ref[:, cols] += _colsum(lam_i * pvr - lam_r * pvi)
            lrb, lib = lam_r.astype(BF16), lam_i.astype(BF16)
            du_ref[:, ycols] = (jnp.dot(lrb, btr_ref[q], preferred_element_type=F32)
                                + jnp.dot(lib, bti_ref[q], preferred_element_type=F32)
                                + d_ref[:, ycols] * dy[:, ycols]).astype(BF16)
            uq = ubf[:, ycols]
            dbr_ref[q] += lax.dot_general(uq, lrb, tn_dims, preferred_element_type=F32)
            dbi_ref[q] += lax.dot_general(uq, lib, tn_dims, preferred_element_type=F32)
            dq = dyb[:, ycols]
            dcr_ref[q] += lax.dot_general(sre_ref[:, cols].astype(BF16), dq, tn_dims, preferred_element_type=F32)
            dci_ref[q] -= lax.dot_general(sim_ref[:, cols].astype(BF16), dq, tn_dims, preferred_element_type=F32)

    cblk = 1
    rev = lambda i: (nb - 1 - i, 0)
    revu = lambda i: (nb - 1 - i, cblk)
    tailmap = lambda i: (jnp.maximum((nb - 1 - i) * (lc // tail) - 1, 0), 0)
    row = lambda i: (0, 0)
    blk3 = lambda i: (0, 0, 0)
    return _pcall(
        body, name=name, grid=(nb,),
        in_specs=[pl.BlockSpec((lc, sw), rev), pl.BlockSpec((lc, sw), rev), pl.BlockSpec((lc, sw), revu),
                  pl.BlockSpec((lc, gn), rev), pl.BlockSpec((lc, gn), rev),
                  pl.BlockSpec((tail, gn), tailmap), pl.BlockSpec((tail, gn), tailmap),
                  pl.BlockSpec((1, gn), row), pl.BlockSpec((1, gn), row),
                  pl.BlockSpec(bbt_re.shape, blk3), pl.BlockSpec(bbt_im.shape, blk3),
                  pl.BlockSpec(cct_re.shape, blk3), pl.BlockSpec(cct_im.shape, blk3), pl.BlockSpec((1, sw), row)],
        out_specs=[pl.BlockSpec((lc, sw), rev), pl.BlockSpec((1, gn), row), pl.BlockSpec((1, gn), row),
                   pl.BlockSpec((SSM_BLOCKS, ub, sb), blk3), pl.BlockSpec((SSM_BLOCKS, ub, sb), blk3),
                   pl.BlockSpec((SSM_BLOCKS, sb, ub), blk3), pl.BlockSpec((SSM_BLOCKS, sb, ub), blk3),
                   pl.BlockSpec((1, sw), row)],
        out_shape=[jax.ShapeDtypeStruct((t, sw), BF16), jax.ShapeDtypeStruct((1, gn), F32), jax.ShapeDtypeStruct((1, gn), F32),
                   jax.ShapeDtypeStruct((SSM_BLOCKS, ub, sb), F32), jax.ShapeDtypeStruct((SSM_BLOCKS, ub, sb), F32),
                   jax.ShapeDtypeStruct((SSM_BLOCKS, sb, ub), F32), jax.ShapeDtypeStruct((SSM_BLOCKS, sb, ub), F32),
                   jax.ShapeDtypeStruct((1, sw), F32)],
        scratch_shapes=[pltpu.VMEM((lc, gn), F32), pltpu.VMEM((lc, gn), F32),
                        pltpu.VMEM((lc + pad, gn), F32), pltpu.VMEM((lc + pad, gn), F32),
                        pltpu.VMEM((lc + pad, gn), F32), pltpu.VMEM((lc + pad, gn), F32),
                        pltpu.VMEM((tail + lc, gn), F32), pltpu.VMEM((tail + lc, gn), F32),
                        pltpu.VMEM((1, gn), F32), pltpu.VMEM((1, gn), F32)],
        compiler_params=_params("arbitrary"))(dge, y, uin, s_re, s_im, s_re, s_im, lrdt, ang, bbt_re, bbt_im, cct_re, cct_im, d_skip)


def _blockdiag_b(bb, sw):
    gpb = (sw // SSM_GROUP) // SSM_BLOCKS
    b4 = bb.reshape(SSM_BLOCKS, gpb, SSM_STATE, SSM_GROUP)
    eye = jnp.eye(gpb, dtype=bb.dtype)
    out = jnp.einsum('qgnh,gk->qghkn', b4, eye)
    return out.reshape(SSM_BLOCKS, gpb * SSM_GROUP, gpb * SSM_STATE)


def _blockdiag_c(cc, sw):
    gpb = (sw // SSM_GROUP) // SSM_BLOCKS
    c4 = cc.reshape(SSM_BLOCKS, gpb, SSM_GROUP, SSM_STATE)
    eye = jnp.eye(gpb, dtype=cc.dtype)
    out = jnp.einsum('qghn,gk->qgnkh', c4, eye)
    return out.reshape(SSM_BLOCKS, gpb * SSM_STATE, gpb * SSM_GROUP)


def _diag_of_b(dbb, sw):
    gpb = (sw // SSM_GROUP) // SSM_BLOCKS
    d5 = dbb.reshape(SSM_BLOCKS, gpb, SSM_GROUP, gpb, SSM_STATE)
    return jnp.einsum('qghgn->qgnh', d5).reshape(SSM_BLOCKS * gpb * SSM_STATE, SSM_GROUP)


def _diag_of_c(dcc, sw):
    gpb = (sw // SSM_GROUP) // SSM_BLOCKS
    d5 = dcc.reshape(SSM_BLOCKS, gpb, SSM_STATE, gpb, SSM_GROUP)
    return jnp.einsum('qgngh->qghn', d5).reshape(SSM_BLOCKS * gpb, SSM_GROUP, SSM_STATE)


def _ada_fwd(name, c_all, w_sh, b_sh):
    nb, d = c_all.shape
    ncol = w_sh.shape[1]
    tn = _pick(ncol, 768)

    def body(c_ref, w_ref, b_ref, o_ref):
        cv = c_ref[...]
        sil = cv * _sigmoid(cv)
        o_ref[...] = jnp.dot(sil, w_ref[...], preferred_element_type=F32, precision=lax.Precision.HIGHEST) + b_ref[...]

    return _pcall(body, name=name, grid=(ncol // tn,),
                  in_specs=[pl.BlockSpec((nb, d), lambda j: (0, 0)), pl.BlockSpec((d, tn), lambda j: (0, j)),
                            pl.BlockSpec((1, tn), lambda j: (0, j))],
                  out_specs=pl.BlockSpec((nb, tn), lambda j: (0, j)),
                  out_shape=jax.ShapeDtypeStruct((nb, ncol), F32), compiler_params=_params("parallel"))(c_all, w_sh, b_sh)


def _ada_bwd(name, c_all, dmod_sh):
    nb, d = c_all.shape
    ncol = dmod_sh.shape[1]
    tn = _pick(ncol, 768)

    def body(c_ref, g_ref, o_ref):
        cv = c_ref[...]
        sil = cv * _sigmoid(cv)
        o_ref[...] = lax.dot_general(sil, g_ref[...], (((0,), (0,)), ((), ())), preferred_element_type=F32,
                                     precision=lax.Precision.HIGHEST)

    return _pcall(body, name=name, grid=(ncol // tn,),
                  in_specs=[pl.BlockSpec((nb, d), lambda j: (0, 0)), pl.BlockSpec((nb, tn), lambda j: (0, j))],
                  out_specs=pl.BlockSpec((d, tn), lambda j: (0, j)),
                  out_shape=jax.ShapeDtypeStruct((d, ncol), F32), compiler_params=_params("parallel"))(c_all, dmod_sh)


def _place():
    return lax.axis_index("x"), lax.axis_index("y"), lax.axis_index("c")


def _allgather_small(name, blk, deps=()):
    m_per, n = blk.shape

    def body(x_ref, *rest):
        out_ref, send_sems, recv_sems, local_sem = rest[len(deps):]
        x, y, c = _place()
        me, sibling = (x, y, c), (x, y, 1 - c)
        chips = [(1 - x, y), (x, 1 - y), (1 - x, 1 - y)]

        def rows(px, py, pc):
            return out_ref.at[pl.ds((4 * px + 2 * py + pc) * m_per, m_per), :]

        def copy(k, block, to, src=None):
            return pltpu.make_async_remote_copy(
                src_ref=rows(*block) if src is None else src, dst_ref=rows(*block),
                send_sem=send_sems.at[k], recv_sem=recv_sems.at[k], device_id=to, device_id_type=MESH)

        mine = pltpu.make_async_copy(x_ref, rows(*me), local_sem)
        mine.start()
        first = [copy(0, me, sibling, src=x_ref)]
        first += [copy(1 + j, me, (*chip, c), src=x_ref) for j, chip in enumerate(chips)]
        for cp in first:
            cp.start()
        passed = [copy(4 + j, (*chip, c), sibling) for j, chip in enumerate(chips)]
        for j, chip in enumerate(chips):
            copy(1 + j, (*chip, c), me).wait_recv()
            passed[j].start()
        copy(0, sibling, me).wait_recv()
        for j, chip in enumerate(chips):
            copy(4 + j, (*chip, 1 - c), me).wait_recv()
        for cp in first + passed:
            cp.wait_send()
        mine.wait()

    return _pcall(body, name=name, out_shape=jax.ShapeDtypeStruct((N_DEV * m_per, n), blk.dtype),
                  in_specs=[pl.BlockSpec(memory_space=pltpu.VMEM)] + [ANY_SPEC] * len(deps),
                  out_specs=pl.BlockSpec(memory_space=pltpu.VMEM),
                  scratch_shapes=[pltpu.SemaphoreType.DMA((7,)), pltpu.SemaphoreType.DMA((7,)), pltpu.SemaphoreType.DMA],
                  compiler_params=pltpu.CompilerParams(vmem_limit_bytes=VMEM_LIMIT))(blk, *deps)


def _other_chips(x, y):
    return [(1 - x, y), (x, 1 - y), (1 - x, 1 - y)]


def _place_shards(shards, s_me):
    return [lax.dynamic_update_slice(lax.empty((N_CHIPS,) + s.shape, s.dtype), s[None], (s_me, 0, 0)) for s in shards]


def _ag_copy(src, land, send, recv, wi, j, chip, x, y, c, tc, both):
    hr = src.shape[0] // 2
    half = pl.ds(pl.multiple_of(c * hr, 16), hr)
    k = 3 * wi + j
    return pltpu.make_async_remote_copy(
        src_ref=src.at[half, :], dst_ref=land.at[2 * x + y, half, :],
        send_sem=send.at[2 * k + tc if both else k], recv_sem=recv.at[2 * k + c if both else k],
        device_id=(chip[0], chip[1], tc), device_id_type=MESH)


def _ag_targets(c, both):
    return (0, 1) if both else (c,)


def _ag_start(name, shards, lands, groups, direct, deps=()):
    nw, ng, nd = len(shards), len(groups), len(deps)

    def body(*refs):
        src, land = refs[:nw], refs[nw:2 * nw]
        sems = refs[2 * nw + nd:2 * nw + nd + 2 * ng]
        token = refs[-1]
        x, y, c = _place()
        for gi, grp in enumerate(groups):
            for wi, w in enumerate(grp):
                for j, chip in enumerate(_other_chips(x, y)):
                    for tc in _ag_targets(c, direct[gi]):
                        _ag_copy(src[w], land[w], sems[2 * gi], sems[2 * gi + 1], wi, j, chip, x, y, c, tc, direct[gi]).start()
        token[...] = jnp.zeros_like(token)

    sem_shapes = []
    for gi, grp in enumerate(groups):
        sem_shapes += [pltpu.SemaphoreType.DMA(((6 if direct[gi] else 3) * len(grp),))] * 2
    out_shape = sem_shapes + [pltpu.HBM(s.shape, s.dtype) for s in shards] + [pltpu.HBM(l.shape, l.dtype) for l in lands]
    out_shape += [jax.ShapeDtypeStruct((8, 128), F32)]
    res = _pcall(body, name=name, out_shape=out_shape, in_specs=[HBM_SPEC] * (2 * nw) + [ANY_SPEC] * nd,
                 out_specs=[SEM_SPEC] * (2 * ng) + [HBM_SPEC] * (2 * nw) + [pl.BlockSpec(memory_space=pltpu.VMEM)],
                 input_output_aliases={i: 2 * ng + i for i in range(2 * nw)},
                 compiler_params=pltpu.CompilerParams(has_side_effects=EFFECT))(
                     *[_hbm(s) for s in shards], *[_hbm(l) for l in lands], *deps)
    sems = [(res[2 * gi], res[2 * gi + 1]) for gi in range(ng)]
    return sems, list(res[2 * ng:2 * ng + nw]), list(res[2 * ng + nw:2 * ng + 2 * nw]), res[-1]


def _ag_wait(name, shards, lands, send, recv, both, after):
    n = len(shards)

    def body(*refs):
        src, land = refs[:n], refs[n:2 * n]
        send_sem, recv_sem = refs[2 * n], refs[2 * n + 1]
        x, y, c = _place()
        for wi in range(n):
            for j, chip in enumerate(_other_chips(x, y)):
                for tc in _ag_targets(c, both):
                    _ag_copy(src[wi], land[wi], send_sem, recv_sem, wi, j, chip, x, y, c, tc, both).wait_send()
                    _ag_copy(src[wi], land[wi], send_sem, recv_sem, wi, j, chip, chip[0], chip[1], tc, c, both).wait_recv()

    res = _pcall(body, name=name, out_shape=[pltpu.HBM(a.shape, a.dtype) for a in list(shards) + list(lands)],
                 in_specs=[HBM_SPEC] * (2 * n) + [SEM_SPEC, SEM_SPEC] + [ANY_SPEC] * len(after), out_specs=[HBM_SPEC] * (2 * n),
                 input_output_aliases={i: i for i in range(2 * n)},
                 compiler_params=pltpu.CompilerParams(has_side_effects=EFFECT))(*shards, *lands, send, recv, *after)
    return list(res[n:])


def _ag_forward(name, lands):
    n = len(lands)

    def body(*refs):
        out = refs[n:2 * n]
        send, recv = refs[2 * n], refs[2 * n + 1]
        x, y, c = _place()
        sib = (x, y, 1 - c)
        cps = []
        for wi in range(n):
            hr = out[wi].shape[1] // 2
            for j, (cx, cy) in enumerate(_other_chips(x, y)):
                got = out[wi].at[2 * cx + cy, pl.ds(pl.multiple_of(c * hr, 16), hr), :]
                cp = pltpu.make_async_remote_copy(src_ref=got, dst_ref=got, send_sem=send.at[3 * wi + j], recv_sem=recv.at[3 * wi + j],
                                                  device_id=sib, device_id_type=MESH)
                cp.start()
                cps.append(cp)
        for wi in range(n):
            hr = out[wi].shape[1] // 2
            for j, (cx, cy) in enumerate(_other_chips(x, y)):
                got = out[wi].at[2 * cx + cy, pl.ds(pl.multiple_of((1 - c) * hr, 16), hr), :]
                pltpu.make_async_remote_copy(src_ref=got, dst_ref=got, send_sem=send.at[3 * wi + j], recv_sem=recv.at[3 * wi + j],
                                             device_id=sib, device_id_type=MESH).wait_recv()
        for cp in cps:
            cp.wait_send()

    res = _pcall(body, name=name, out_shape=[jax.ShapeDtypeStruct(l.shape, l.dtype) for l in lands],
                 in_specs=[ANY_SPEC] * n, out_specs=[ANY_SPEC] * n, input_output_aliases={i: i for i in range(n)},
                 scratch_shapes=[pltpu.SemaphoreType.DMA((3 * n,)), pltpu.SemaphoreType.DMA((3 * n,))])(*lands)
    return list(res)


def _peers(x, y, c):
    offs = [(dx, dy, dc) for dx in (0, 1) for dy in (0, 1) for dc in (0, 1)][1:]
    return [(1 - x if dx else x, 1 - y if dy else y, 1 - c if dc else c) for dx, dy, dc in offs]


def _rs_copy(g_ref, land_ref, send, recv, wi, k, to, sender):
    hr = g_ref.shape[1] // 2
    return pltpu.make_async_remote_copy(
        src_ref=g_ref.at[2 * to[0] + to[1], pl.ds(pl.multiple_of(to[2] * hr, 16), hr), :], dst_ref=land_ref.at[sender],
        send_sem=send.at[7 * wi + k], recv_sem=recv.at[7 * wi + k], device_id=to, device_id_type=MESH)


def _rs_start(name, gs):
    n = len(gs)
    lands = [lax.empty((N_DEV, g.shape[1] // 2, g.shape[2]), BF16) for g in gs]

    def body(*refs):
        g, land = refs[:n], refs[n:2 * n]
        send, recv = refs[2 * n], refs[2 * n + 1]
        token = refs[-1]
        x, y, c = _place()
        me = 4 * x + 2 * y + c
        for wi in range(n):
            for k, to in enumerate(_peers(x, y, c)):
                _rs_copy(g[wi], land[wi], send, recv, wi, k, to, me).start()
        token[...] = jnp.zeros_like(token)

    out_shape = [pltpu.SemaphoreType.DMA((7 * n,))] * 2 + [pltpu.HBM(a.shape, a.dtype) for a in list(gs) + lands]
    out_shape += [jax.ShapeDtypeStruct((8, 128), F32)]
    res = _pcall(body, name=name, out_shape=out_shape, in_specs=[HBM_SPEC] * (2 * n),
                 out_specs=[SEM_SPEC] * 2 + [HBM_SPEC] * (2 * n) + [pl.BlockSpec(memory_space=pltpu.VMEM)],
                 input_output_aliases={i: 2 + i for i in range(2 * n)},
                 compiler_params=pltpu.CompilerParams(has_side_effects=EFFECT))(
                     *[_hbm(a) for a in gs], *[_hbm(a) for a in lands])
    return res[0], res[1], list(res[2:2 + n]), list(res[2 + n:2 + 2 * n]), res[-1]


def _rs_wait(name, gs, lands, send, recv, after):
    n = len(gs)

    def body(*refs):
        g, land = refs[:n], refs[n:2 * n]
        send_sem, recv_sem = refs[2 * n], refs[2 * n + 1]
        x, y, c = _place()
        me = 4 * x + 2 * y + c
        for wi in range(n):
            for k, to in enumerate(_peers(x, y, c)):
                _rs_copy(g[wi], land[wi], send_sem, recv_sem, wi, k, to, me).wait_send()
                _rs_copy(g[wi], land[wi], send_sem, recv_sem, wi, k, (x, y, c), 4 * to[0] + 2 * to[1] + to[2]).wait_recv()

    res = _pcall(body, name=name, out_shape=[pltpu.HBM(a.shape, a.dtype) for a in list(gs) + list(lands)],
                 in_specs=[HBM_SPEC] * (2 * n) + [SEM_SPEC, SEM_SPEC, ANY_SPEC], out_specs=[HBM_SPEC] * (2 * n),
                 input_output_aliases={i: i for i in range(2 * n)},
                 compiler_params=pltpu.CompilerParams(has_side_effects=EFFECT))(*gs, *lands, send, recv, after)
    return list(res[:n]), list(res[n:])


def _bc_copy(blk_ref, land_ref, send, recv, k, to, slot):
    return pltpu.make_async_remote_copy(src_ref=blk_ref, dst_ref=land_ref.at[slot], send_sem=send.at[k], recv_sem=recv.at[k],
                                        device_id=to, device_id_type=MESH)


def _bcast_start(name, blk):
    land = lax.empty((N_DEV,) + blk.shape, blk.dtype)

    def body(blk_ref, land_ref, send, recv, blk_thru, land_thru, token):
        x, y, c = _place()
        for k, to in enumerate(_peers(x, y, c)):
            _bc_copy(blk_ref, land_ref, send, recv, k, to, 4 * x + 2 * y + c).start()
        token[...] = jnp.zeros_like(token)

    return _pcall(body, name=name,
                  out_shape=[pltpu.SemaphoreType.DMA((7,)), pltpu.SemaphoreType.DMA((7,)), pltpu.HBM(blk.shape, blk.dtype),
                             pltpu.HBM(land.shape, land.dtype), jax.ShapeDtypeStruct((8, 128), F32)],
                  in_specs=[HBM_SPEC, HBM_SPEC], out_specs=[SEM_SPEC, SEM_SPEC, HBM_SPEC, HBM_SPEC, pl.BlockSpec(memory_space=pltpu.VMEM)],
                  input_output_aliases={0: 2, 1: 3}, compiler_params=pltpu.CompilerParams(has_side_effects=EFFECT))(_hbm(blk), _hbm(land))


def _bcast_wait(name, blk, land, send, recv, after):
    def body(blk_ref, land_ref, send_sem, recv_sem, after_ref, blk_thru, land_thru):
        x, y, c = _place()
        for k, to in enumerate(_peers(x, y, c)):
            _bc_copy(blk_ref, land_ref, send_sem, recv_sem, k, to, 4 * x + 2 * y + c).wait_send()
            _bc_copy(blk_ref, land_ref, send_sem, recv_sem, k, to, 4 * to[0] + 2 * to[1] + to[2]).wait_recv()

    return _pcall(body, name=name, out_shape=[pltpu.HBM(blk.shape, blk.dtype), pltpu.HBM(land.shape, land.dtype)],
                  in_specs=[HBM_SPEC, HBM_SPEC, SEM_SPEC, SEM_SPEC, ANY_SPEC], out_specs=[HBM_SPEC, HBM_SPEC],
                  input_output_aliases={0: 0, 1: 1}, compiler_params=pltpu.CompilerParams(has_side_effects=EFFECT))(
                      blk, land, send, recv, after)[1]


def _rs_sum(name, g, land):
    _, rs, cs = g.shape
    hr = rs // 2
    ch = _pick(hr, 64, 16)

    def body(g_ref, land_ref, out_ref, recv, local_sems, sib_sems):
        x, y, c = _place()
        me = 4 * x + 2 * y + c
        cps = [pltpu.make_async_copy(g_ref.at[2 * x + y, pl.ds(pl.multiple_of(c * hr, 16), hr), :], recv.at[me], local_sems.at[7])]
        for k, (tx, ty, tc) in enumerate(_peers(x, y, c)):
            slot = 4 * tx + 2 * ty + tc
            cps.append(pltpu.make_async_copy(land_ref.at[slot], recv.at[slot], local_sems.at[k]))
        for cp in cps:
            cp.start()
        for cp in cps:
            cp.wait()
        base = pl.multiple_of(c * hr, 16)
        for r0 in range(0, hr, ch):
            acc = recv[0, r0:r0 + ch, :].astype(F32)
            for k in range(1, N_DEV):
                acc = acc + recv[k, r0:r0 + ch, :].astype(F32)
            out_ref[pl.ds(base + r0, ch), :] = acc
        half = out_ref.at[pl.ds(base, hr), :]
        sib = pltpu.make_async_remote_copy(src_ref=half, dst_ref=half, send_sem=sib_sems.at[0], recv_sem=sib_sems.at[1],
                                           device_id=(x, y, 1 - c), device_id_type=MESH)
        sib.start()
        other = out_ref.at[pl.ds(pl.multiple_of((1 - c) * hr, 16), hr), :]
        pltpu.make_async_remote_copy(src_ref=other, dst_ref=other, send_sem=sib_sems.at[0], recv_sem=sib_sems.at[1],
                                     device_id=(x, y, 1 - c), device_id_type=MESH).wait_recv()
        sib.wait_send()

    return _pcall(
        body, name=name, out_shape=jax.ShapeDtypeStruct((rs, cs), F32),
        in_specs=[ANY_SPEC, ANY_SPEC], out_specs=pl.BlockSpec(memory_space=pltpu.VMEM),
        scratch_shapes=[pltpu.VMEM((N_DEV, hr, cs), BF16), pltpu.SemaphoreType.DMA((8,)), pltpu.SemaphoreType.DMA((2,))],
        compiler_params=pltpu.CompilerParams(vmem_limit_bytes=VMEM_LIMIT))(g, land)


def _gate_norm_epilogue(factor, x_in, gt, nxt, t, d):
    blk = (512, d)
    full = lambda i, j: (i, j)
    rowv = lambda i, j: (0, j)

    def epi(acc, res, scale, *norm):
        x_new = res + (factor * scale) * acc
        if not norm:
            return x_new, acc
        gv, scv, shv = norm
        r = lax.rsqrt(jnp.mean(x_new * x_new, axis=-1, keepdims=True) + EPS)
        return x_new, acc, x_new * r * gv * (1.0 + scv) + shv

    ins = [(x_in, blk, full), (gt, (1, d), rowv)] + [(v, (1, d), rowv) for v in (nxt or ())]
    outs = [((t, d), F32, blk, full), ((t, d), BF16, blk, full)] + ([((t, d), BF16, blk, full)] if nxt else [])
    return epi, ins, outs


def _loss_epilogue(x_in, gt, tgt, g_final, t, d):
    blk = (512, d)
    full = lambda i, j: (i, j)
    rowv = lambda i, j: (0, j)

    def epi(acc, res, scale, tv, gv):
        xv = res + (0.5 * scale) * acc
        r = lax.rsqrt(jnp.mean(xv * xv, axis=-1, keepdims=True) + EPS)
        xr = xv * r
        e = xr * gv - tv
        loss_row = 0.5 * jnp.mean(e * e, axis=-1, keepdims=True)
        dout = e * (1.0 / d)
        gd = gv * dout
        dx = r * (gd - xr * jnp.mean(gd * xr, axis=-1, keepdims=True))
        fdx = 0.5 * dx
        return [dx, scale * fdx], [_colsum(dout * xr), _colsum(loss_row * jnp.ones((1, 128), F32)), _colsum(fdx * acc)]

    ins = [(x_in, blk, full), (gt, (1, d), rowv), (tgt, blk, full), (g_final, (1, d), rowv)]
    return epi, ins, [((t, d), F32, blk, full), ((t, d), BF16, blk, full)], [d, 128, d]


def _norm_bwd_epilogue(x, dres, g, sc, up, t, d):
    blk = (512, d)
    full = lambda i, j: (i, j)
    rowv = lambda i, j: (0, j)

    def epi(dhv, xv, drv, gv, scv, *rest):
        r = lax.rsqrt(jnp.mean(xv * xv, axis=-1, keepdims=True) + EPS)
        xr = xv * r
        dn = dhv * (1.0 + scv)
        gd = gv * dn
        dx = drv + r * (gd - xr * jnp.mean(gd * xr, axis=-1, keepdims=True))
        outs, reds = [dx], [_colsum(dhv), _colsum(dhv * xr * gv), _colsum(dn * xr)]
        if up:
            yv, gtv = rest
            fdx = up[2] * dx
            outs.append(gtv * fdx)
            reds.append(_colsum(fdx * yv.astype(F32)))
        return outs, reds

    ins = [(x, blk, full), (dres, blk, full), (g, (1, d), rowv), (sc, (1, d), rowv)]
    ins += [(up[0], blk, full), (up[1], (1, d), rowv)] if up else []
    outs = [((t, d), F32, blk, full)] + ([((t, d), BF16, blk, full)] if up else [])
    return epi, ins, outs, [d] * (4 if up else 3)


def _ffn_in_act(name, h, w_in, tm=512):
    t, d = h.shape
    ns = w_in.shape[2]
    tm = _pick(t, tm, 8)
    w4 = w_in.reshape(2, 2, d, ns)

    def body(h_ref, w_ref, ab_ref, act_ref):
        hv = h_ref[...]
        a = jnp.dot(hv, w_ref[0], preferred_element_type=F32)
        b = jnp.dot(hv, w_ref[1], preferred_element_type=F32)
        ab_ref[0] = a.astype(BF16)
        ab_ref[1] = b.astype(BF16)
        act_ref[...] = (a * _sigmoid(a) * b).astype(BF16)

    return _pcall(body, name=name, grid=(2, t // tm),
                  in_specs=[pl.BlockSpec((tm, d), lambda c, i: (i, 0)), pl.BlockSpec((2, None, d, ns), lambda c, i: (0, c, 0, 0))],
                  out_specs=[pl.BlockSpec((2, tm, ns), lambda c, i: (0, i, c)), pl.BlockSpec((tm, ns), lambda c, i: (i, c))],
                  out_shape=[jax.ShapeDtypeStruct((2, t, 2 * ns), BF16), jax.ShapeDtypeStruct((t, 2 * ns), BF16)],
                  compiler_params=_params("parallel", "parallel"))(h, w4)


def _dswiglu_epilogue(ab, t, f, tn):
    blk = (2, 512, tn)
    idx = lambda i, j: (0, i, j)

    def epi(dact, abv):
        a, b = abv[0].astype(F32), abv[1].astype(F32)
        s = _sigmoid(a)
        return (jnp.stack([dact * b * (s * (1.0 + a * (1.0 - s))), dact * (a * s)]),)

    return epi, [(ab, blk, idx)], [((2, t, f), BF16, blk, idx)]


def _ffn_fwd(tag, x, h, gt, get_w_in, get_w_out, nxt=None, loss=None):
    t, d = x.shape
    ab, act = _ffn_in_act(tag + "_in", h, get_w_in(h))
    if loss:
        epi, epi_ins, epi_outs, epi_reds = _loss_epilogue(x, gt, *loss, t, d)
        res = _mm(tag + "_out", act, get_w_out(act), tm=512, tn=d, epi=epi, epi_ins=epi_ins, epi_outs=epi_outs,
                  epi_reds=epi_reds, b_resident=True)
        return res, (x, h, ab, act, None)
    epi, epi_ins, epi_outs = _gate_norm_epilogue(0.5, x, gt, nxt, t, d)
    res = _mm(tag + "_out", act, get_w_out(act), tm=512, tn=d, epi=epi, epi_ins=epi_ins, epi_outs=epi_outs, b_resident=True)
    return res[0], res[2], (x, h, ab, act, res[1])


def _ffn_bwd(tag, dx_new, dy, saved, g, sc, w_in, w_out, start_rs, up, deps=()):
    x, h, ab, act, _ = saved
    d = x.shape[1]
    f = act.shape[1]
    dw_out = _mm(tag + "_dwout", act, dy, ta=True, tm=1408, tn=d, tk=2048, deps=deps)
    tok = start_rs("out", dw_out.reshape(N_CHIPS, f // N_CHIPS, d))
    epi, epi_ins, epi_outs = _dswiglu_epilogue(ab, x.shape[0], f, w_in.shape[2])
    dab = _mm(tag + "_dact", dy, w_out, tb=True, tm=512, tn=w_in.shape[2], epi=epi, epi_ins=epi_ins, epi_outs=epi_outs,
              deps=(tok,), j_outer=True)
    dw_in = _mm(tag + "_dwin", h, dab, ta=True, out_stacked=True, b_halves=True, tm=d, tn=1408, tk=2048)
    tok = start_rs("in", dw_in)
    epi, epi_ins, epi_outs, epi_reds = _norm_bwd_epilogue(x, dx_new, g, sc, up, x.shape[0], d)
    res = _mm(tag + "_dh", dab, w_in, tb=True, b_stacked=True, a_halves=True, tn=d, tk=5632, deps=(tok,), epi=epi,
              epi_ins=epi_ins, epi_outs=epi_outs, epi_reds=epi_reds, b_resident=True)
    if up:
        return res
    return res[0], None, res[1], res[2], res[3], None


def _row(v):
    return v.reshape(1, -1)


def _pack(parts):
    cols = []
    for p in parts:
        flat = p.reshape(-1).astype(F32)
        padn = (-flat.shape[0]) % 128
        cols.append(jnp.pad(flat, (0, padn)) if padn else flat)
    flat = jnp.concatenate(cols)
    padn = (-flat.shape[0]) % 1024
    if padn:
        flat = jnp.pad(flat, (0, padn))
    return flat.reshape(-1, 128)


def _unpack(packed, shapes):
    flat = packed.reshape(-1)
    out, off = [], 0
    for s in shapes:
        n = math.prod(s)
        out.append(flat[off:off + n].reshape(s))
        off += n + ((-n) % 128)
    return out


SMALL = ['b_ada', 'g_ffn1', 'g_mix', 'pool_w', 'pool_b', 'pool_scale', 'ssm_lam_re_log', 'ssm_lam_im', 'ssm_log_dt',
         'ssm_b_re', 'ssm_b_im', 'ssm_c_re', 'ssm_c_im', 'ssm_d', 'b_glu', 'g_ffn2', 'g_final']
BIG = ['w_ffn1_in', 'w_ffn1_out', 'w_in', 'w_pool_up', 'w_glu', 'w_ssm_up', 'w_out', 'w_ffn2_in', 'w_ffn2_out']
AG_GROUPS = [[0], [1], [2, 3, 4, 5, 6], [7, 8]]
AG_DIRECT = [False, False, False, True]
SMALL_LATE = ['b_ada', 'g_ffn1']
WEIGHTS = ['w_ada', 'b_ada', 'g_ffn1', 'w_ffn1_in', 'w_ffn1_out', 'g_mix', 'w_in', 'pool_w', 'pool_b', 'pool_scale', 'w_pool_up',
           'ssm_lam_re_log', 'ssm_lam_im', 'ssm_log_dt', 'ssm_b_re', 'ssm_b_im', 'ssm_c_re', 'ssm_c_im', 'ssm_d', 'w_glu', 'b_glu',
           'w_ssm_up', 'w_out', 'g_ffn2', 'w_ffn2_in', 'w_ffn2_out', 'g_final']


def kernel(x, c, w_ada, b_ada, g_ffn1, w_ffn1_in, w_ffn1_out, g_mix, w_in, pool_w, pool_b, pool_scale, w_pool_up, ssm_lam_re_log, ssm_lam_im, ssm_log_dt, ssm_b_re, ssm_b_im, ssm_c_re, ssm_c_im, ssm_d, w_glu, b_glu, w_ssm_up, w_out, g_ffn2, w_ffn2_in, w_ffn2_out, g_final, loss_target, m_w_ada, m_b_ada, m_g_ffn1, m_w_ffn1_in, m_w_ffn1_out, m_g_mix, m_w_in, m_pool_w, m_pool_b, m_pool_scale, m_w_pool_up, m_ssm_lam_re_log, m_ssm_lam_im, m_ssm_log_dt, m_ssm_b_re, m_ssm_b_im, m_ssm_c_re, m_ssm_c_im, m_ssm_d, m_w_glu, m_b_glu, m_w_ssm_up, m_w_out, m_g_ffn2, m_w_ffn2_in, m_w_ffn2_out, m_g_final, v_w_ada, v_b_ada, v_g_ffn1, v_w_ffn1_in, v_w_ffn1_out, v_g_mix, v_w_in, v_pool_w, v_pool_b, v_pool_scale, v_w_pool_up, v_ssm_lam_re_log, v_ssm_lam_im, v_ssm_log_dt, v_ssm_b_re, v_ssm_b_im, v_ssm_c_re, v_ssm_c_im, v_ssm_d, v_w_glu, v_b_glu, v_w_ssm_up, v_w_out, v_g_ffn2, v_w_ffn2_in, v_w_ffn2_out, v_g_final):
    args = dict(locals())
    wt = {n: args[n] for n in WEIGHTS}
    mom = {n: args["m_" + n] for n in WEIGHTS}
    var = {n: args["v_" + n] for n in WEIGHTS}

    t, d = x.shape[1], x.shape[2]
    pw = pool_b.shape[1]
    sw = ssm_d.shape[1]
    ngrp = sw // SSM_GROUP
    gn = ngrp * SSM_STATE
    xi, yi, ci = _place()
    b_me = 4 * xi + 2 * yi + ci
    s_me = 2 * xi + yi
    x2d = x[0]
    tgt = loss_target[0]

    c_all = _allgather_small("ag_c", c.reshape(8, d // 8)).reshape(N_DEV, d)
    ncol = w_ada.shape[2]
    b_sh = lax.dynamic_slice(b_ada, (0, s_me * ncol), (1, ncol))
    mod_sh = _ada_fwd("ada_fwd", c_all, w_ada[0], b_sh)
    md_send, md_recv, mod_sh, md_land, tok = _bcast_start("mod_start", mod_sh)

    shards = [wt[n][0].astype(BF16) for n in BIG]
    ag_sems, shards_t, lands_t, tok = _ag_start("ag_start", shards, _place_shards(shards, s_me), AG_GROUPS, AG_DIRECT, deps=(tok,))
    md_land = _bcast_wait("mod_wait", mod_sh, md_land, md_send, md_recv, tok)
    mod_all = lax.dynamic_update_slice(md_land, mod_sh[None], (b_me, 0, 0))
    full = {}

    def weights(gi, *after):
        grp = AG_GROUPS[gi]
        if BIG[grp[0]] not in full:
            ls = _ag_wait("ag_wait%d" % gi, [shards_t[w] for w in grp], [lands_t[w] for w in grp], *ag_sems[gi],
                          AG_DIRECT[gi], after)
            for w, l in zip(grp, ls if AG_DIRECT[gi] else _ag_forward("ag_fwd%d" % gi, ls)):
                full[BIG[w]] = l
        return full

    mod_me = jnp.concatenate([lax.dynamic_slice(mod_all, (2 * s, b_me, 0), (1, 1, ncol))[0] for s in range(N_CHIPS)], axis=1)
    mod = mod_me.reshape(9, d)
    sh1, sc1, gt1, sh2, sc2, gt2, sh3, sc3, gt3 = [mod[k:k + 1] for k in range(9)]

    f = w_ffn1_out.shape[1] * N_CHIPS

    col = lambda a: a.reshape(gn, 1)
    lrl_c, li_c = col(ssm_lam_re_log), col(ssm_lam_im)
    ldt_c = col(jnp.broadcast_to(ssm_log_dt.reshape(ngrp, 1), (ngrp, SSM_STATE)))
    b_re2, b_im2 = ssm_b_re.reshape(gn, SSM_GROUP), ssm_b_im.reshape(gn, SSM_GROUP)
    lrdt_c, ang_c, bb_re, bb_im = _ssm_prep("ssm_prep", lrl_c, li_c, ldt_c, b_re2, b_im2)
    lrdt, ang = lrdt_c.reshape(1, gn), ang_c.reshape(1, gn)
    tr = lambda a: jnp.swapaxes(a, 1, 2)
    bbd = [_blockdiag_b(v, sw).astype(BF16) for v in (bb_re, bb_im)]
    ccd = [_blockdiag_c(v[0], sw).astype(BF16) for v in (ssm_c_re, ssm_c_im)]
    bbd_t, ccd_t = [tr(v) for v in bbd], [tr(v) for v in ccd]
    early = [n for n in SMALL if n not in SMALL_LATE]
    packs = {}
    for tag, names, extra in (("early", early, []), ("late", SMALL_LATE, [jnp.zeros((128,), F32)])):
        packs[tag] = [_pack([src[n] for n in names] + extra) for src in (wt, mom, var)]
    shadow = [lrdt, ang, ldt_c, *bbd, *ccd, *bbd_t, *ccd_t, *packs["early"], *packs["late"]]

    h1 = _norm_fwd("ffn1_norm", x2d, g_ffn1, sc1, sh1, deps=(tok,))
    x1, h2, sav1 = _ffn_fwd("ffn1", x2d, h1, gt1, lambda after: weights(0, after, *shadow)['w_ffn1_in'],
                            lambda after: weights(1, after)['w_ffn1_out'].reshape(f, d), (g_mix, sc2, sh2))
    weights(2, h2)
    wo = full['w_out'].reshape(d, d)
    uin = _mm("mix_in", h2, full['w_in'], b_stacked=True, tm=1024, tn=768, out_dtype=F32, j_outer=True)
    p_pool, z_pool = _pool_fwd("pool_fwd", uin, pool_w[0], pool_b, pool_scale, pw)
    y_pool = _mm("pool_up", p_pool, full['w_pool_up'], b_stacked=True, tm=1024, tn=d)
    y_s, ge, s_re, s_im = _ssm_fwd("ssm_fwd", uin, lrdt, ang, *bbd, *ccd, ssm_d, sw)
    gv = _mm("glu_in", ge, full['w_glu'], b_stacked=True, tm=1024, tn=d)
    sg = _glu_fwd("glu_act", gv, b_glu)
    y_ssm = _mm("ssm_up", sg, full['w_ssm_up'], b_stacked=True, tm=1024, tn=d)
    merged = _gates_fwd("gates", y_pool, y_ssm, uin, d, pw)
    epi, epi_ins, epi_outs = _gate_norm_epilogue(1.0, x1, gt2, (g_ffn2, sc3, sh3), t, d)
    x2, y2, h3 = _mm("mix_out", merged, wo, tm=512, tn=d, epi=epi, epi_ins=epi_ins, epi_outs=epi_outs)

    (dx3, dy3, dg_final, loss_v, dgt3), sav3 = _ffn_fwd(
        "ffn2", x2, h3, gt3, lambda after: weights(3, after)['w_ffn2_in'],
        lambda after: weights(3, after)['w_ffn2_out'].reshape(f, d), loss=(tgt, _row(g_final)))

    gs = {}
    rs_open = []

    def start_rs(names, gbs):
        send, recv, g_thru, land_thru, token = _rs_start("rs_start_" + names[0], gbs)
        rs_open.append((names, send, recv, g_thru, land_thru))
        return token

    dx2, dy2, dsh3, dsc3, gs['g_ffn2'], dgt2 = _ffn_bwd(
        "ffn2b", dx3, dy3, sav3, g_ffn2, sc3, full['w_ffn2_in'], full['w_ffn2_out'].reshape(f, d),
        lambda key, g: start_rs(['w_ffn2_' + key], [g]), (y2, gt2, 1.0))

    dmerged = _mm("mix_dmerged", dy2, wo, tb=True, tn=d)
    g_wo = _mm("mix_dwout", merged, dy2, ta=True, tm=d, tn=d, tk=2048).reshape(N_CHIPS, d // N_CHIPS, d)
    dy_pool, dy_ssm, duin = _gates_bwd("gates_bwd", dmerged, y_pool, y_ssm, uin, d, pw)

    dp = _mm("pool_dup", dy_pool, full['w_pool_up'], tb=True, b_stacked=True, tm=1024, tn=pw, tk=d)
    g_wpu = _mm("pool_dwup", p_pool, dy_pool, ta=True, out_stacked=True, tm=pw, tn=d, tk=4096)
    duin, gs['pool_w'], gs['pool_b'], gs['pool_scale'] = _pool_bwd("pool_bwd", dp, z_pool, pool_w[0], pool_b, pool_scale, duin)

    dsg = _mm("ssm_dup", dy_ssm, full['w_ssm_up'], tb=True, b_stacked=True, tm=1024, tn=sw, tk=d)
    g_wsu = _mm("ssm_dwup", sg, dy_ssm, ta=True, out_stacked=True, tm=sw, tn=d, tk=4096)
    tok = start_rs(['w_out', 'w_pool_up', 'w_ssm_up'], [g_wo, g_wpu, g_wsu])
    dgv, gs['b_glu'] = _glu_bwd("glu_bwd", dsg, gv, b_glu, deps=(tok,))
    dge = _mm("glu_dge", dgv, full['w_glu'], tb=True, b_stacked=True, tm=1024, tn=sw, tk=d, out_dtype=F32)
    g_wglu = _mm("glu_dw", ge, dgv, ta=True, out_stacked=True, tm=sw, tn=d, tk=4096)
    (duin, g_abre, g_abim, g_bbd_re, g_bbd_im, g_ccd_re, g_ccd_im, gs['ssm_d']) = _ssm_bwd(
        "ssm_bwd", dge, y_s, uin, s_re, s_im, lrdt, ang, *bbd_t, *ccd_t, ssm_d, sw, duin)
    gs['ssm_c_re'], gs['ssm_c_im'] = _diag_of_c(g_ccd_re, sw), _diag_of_c(g_ccd_im, sw)
    d_lrl, d_li, d_ldt, d_bre, d_bim = _ssm_param_bwd(
        "ssm_param_bwd", lrl_c, li_c, ldt_c, b_re2, b_im2, g_abre.reshape(gn, 1), g_abim.reshape(gn, 1),
        _diag_of_b(g_bbd_re, sw), _diag_of_b(g_bbd_im, sw))
    gs['ssm_lam_re_log'], gs['ssm_lam_im'] = d_lrl, d_li
    gs['ssm_log_dt'] = jnp.sum(d_ldt.reshape(ngrp, SSM_STATE), axis=1)
    gs['ssm_b_re'], gs['ssm_b_im'] = d_bre, d_bim

    g_win =_mm("mix_dwin", h2, duin, ta=True, out_stacked=True, tm=d, tn=768, tk=4096)
    tok = start_rs(['w_glu', 'w_in'], [g_wglu, g_win])
    epi, epi_ins, epi_outs, epi_reds = _norm_bwd_epilogue(x1, dx2, g_mix, sc2, (sav1[4], gt1, 0.5), t, d)
    dx1, dy1, dsh2, dsc2, gs['g_mix'], dgt1 = _mm(
        "mix_dh", duin, full['w_in'], tb=True, b_stacked=True, tn=d, tk=3072, deps=(tok,), epi=epi, epi_ins=epi_ins,
        epi_outs=epi_outs, epi_reds=epi_reds, b_resident=True)

    gs['g_final'] = dg_final
    sg_blk = _pack([gs[n] for n in early])
    sg_send, sg_recv, sg_blk, sg_land, tok = _bcast_start("sg_start", sg_blk)

    dx0, _, dsh1, dsc1, gs['g_ffn1'], _ = _ffn_bwd(
        "ffn1b", dx1, dy1, sav1, g_ffn1, sc1, full['w_ffn1_in'], full['w_ffn1_out'].reshape(f, d),
        lambda key, g: start_rs(['w_ffn1_' + key], [g]), None, deps=(tok,))

    gs['b_ada'] = jnp.concatenate([dsh1, dsc1, dgt1, dsh2, dsc2, dgt2, dsh3, dsc3, dgt3], axis=1)
    lt_blk = _pack([gs[n] for n in SMALL_LATE] + [loss_v])
    lt_send, lt_recv, lt_blk, lt_land, tok = _bcast_start("late_start", lt_blk)

    grads, delta, new_m, new_v = {}, {}, {}, {}

    def small_update(tag, names, blk, land):
        g8 = lax.dynamic_update_slice(land, blk[None], (b_me, 0, 0)).reshape(-1, 128)
        w_pack, m_pack, v_pack = packs[tag]
        res = _adamw_small("adamw_small_" + tag, w_pack, g8, m_pack, v_pack)
        for dst, packed in zip((grads, delta, new_m, new_v), res):
            for n, val in zip(names, _unpack(packed, [wt[n].shape for n in names])):
                dst[n] = val
        return g8

    after = tok
    for k, (names, send, recv, g_thru, land_thru) in enumerate(rs_open):
        g_done, land_done = _rs_wait("rs_wait_" + names[0], g_thru, land_thru, send, recv, after)
        for n, g_own, land in zip(names, g_done, land_done):
            g_sum = _rs_sum("rs_sum_" + n, g_own, land)
            dl, mn, vn = _adamw("adamw_" + n, wt[n][0], g_sum, mom[n][0], var[n][0])
            grads[n], delta[n], new_m[n], new_v[n] = g_sum[None], dl[None], mn[None], vn[None]
            after = dl
        if k == len(rs_open) - 3:
            small_update("early", early, sg_blk, _bcast_wait("sg_wait", sg_blk, sg_land, sg_send, sg_recv, after))

    lt_land = _bcast_wait("late_wait", lt_blk, lt_land, lt_send, lt_recv, after)
    late8 = small_update("late", SMALL_LATE, lt_blk, lt_land).reshape(N_DEV, -1)
    loss = jnp.sum(late8[:, 10 * d])
    dmod_sh = lax.dynamic_slice(late8, (0, s_me * ncol), (N_DEV, ncol))
    g_w_ada = _ada_bwd("ada_bwd", c_all, dmod_sh)
    dl, mn, vn = _adamw("adamw_w_ada", w_ada[0], g_w_ada, m_w_ada[0], v_w_ada[0])
    grads['w_ada'], delta['w_ada'], new_m['w_ada'], new_v['w_ada'] = g_w_ada[None], dl[None], mn[None], vn[None]

    return (loss, dx0[None], *[grads[n] for n in WEIGHTS], *[delta[n] for n in WEIGHTS],
            *[new_m[n] for n in WEIGHTS], *[new_v[n] for n in WEIGHTS])
```

```python
import functools
import math

import jax
import jax.numpy as jnp
from jax import lax
from jax.experimental import pallas as pl
from jax.experimental.pallas import tpu as pltpu

F32 = jnp.float32
BF16 = jnp.bfloat16
MESH = pl.DeviceIdType.MESH

EPS = 1e-6
POOL_WINDOWS = (2, 4, 8, 16)
POOL_HALO = 16
SSM_GROUP = 16
SSM_STATE = 64
SSM_BLOCKS = 4
N_DEV = 8
N_CHIPS = 4
ADAM_LR = 0.001
ADAM_B1 = 0.9
ADAM_B2 = 0.999
ADAM_EPS = 1e-08
ADAM_WD = 0.01
ADAM_STEP = 10
VMEM_LIMIT = 56 * 1024 * 1024


ANY_SPEC = pl.BlockSpec(memory_space=pl.ANY)
HBM_SPEC = pl.BlockSpec(memory_space=pltpu.HBM)
SEM_SPEC = pl.BlockSpec(memory_space=pltpu.SEMAPHORE)
EFFECT = pltpu.SideEffectType.DATAFLOW_SIDE_EFFECTING


def _hbm(a):
    return pltpu.with_memory_space_constraint(a, pltpu.HBM)


def _pcall(body, **kw):
    return pl.pallas_call(body, **kw)


def _params(*sem):
    return pltpu.CompilerParams(dimension_semantics=sem, vmem_limit_bytes=VMEM_LIMIT)


def _pick(n, cap, mult=128):
    if n <= cap:
        return n
    best = None
    for d in range(mult, cap + 1, mult):
        if n % d == 0:
            best = d
    assert best is not None, (n, cap, mult)
    return best


def _sigmoid(v):
    return 1.0 / (1.0 + jnp.exp(-v))


def _rowwise(name, fn, ins, params, outs, reds, tm, deps=()):
    t = ins[0][0].shape[0]
    tm = min(tm, t)
    nb = t // tm
    ni, npar, no, nd = len(ins), len(params), len(outs), len(deps)

    def body(*refs):
        iv = [r[...] for r in refs[:ni]]
        pv = [r[...] for r in refs[ni:ni + npar]]
        o_refs = refs[ni + npar + nd:ni + npar + nd + no]
        r_refs = refs[ni + npar + nd + no:]
        ovals, rvals = fn(iv, pv)
        for o_ref, val in zip(o_refs, ovals):
            off = 0
            if isinstance(val, tuple) and val[0] == "at":
                _, off, val = val
            parts = val if isinstance(val, (list, tuple)) else [val]
            for p in parts:
                o_ref[:, off:off + p.shape[1]] = p.astype(o_ref.dtype)
                off += p.shape[1]
        if r_refs:
            @pl.when(pl.program_id(0) == 0)
            def _():
                for r in r_refs:
                    r[...] = jnp.zeros_like(r)
            for r, val in zip(r_refs, rvals):
                r[...] += val

    in_specs = [pl.BlockSpec((tm, w), functools.partial(lambda i, cb: (i, cb), cb=cb)) for (_, w, cb) in ins]
    in_specs += [pl.BlockSpec(p.shape, lambda i: (0, 0)) for p in params]
    in_specs += [ANY_SPEC] * nd
    out_shape = [jax.ShapeDtypeStruct((t, w), dt) for (w, dt) in outs]
    out_shape += [jax.ShapeDtypeStruct((1, w), F32) for w in reds]
    out_specs = [pl.BlockSpec((tm, w), lambda i: (i, 0)) for (w, _) in outs]
    out_specs += [pl.BlockSpec((1, w), lambda i: (0, 0)) for w in reds]
    res = _pcall(body, name=name, grid=(nb,), in_specs=in_specs, out_specs=out_specs, out_shape=out_shape,
                 compiler_params=_params("arbitrary"))(*[a for a, _, _ in ins], *params, *deps)
    return res


def _colsum(v):
    return jnp.sum(v, axis=0, keepdims=True)


def _mm(name, a, b, *, ta=False, tb=False, b_stacked=False, out_stacked=False, tm=512, tn=1024, tk=2816,
        out_dtype=BF16, epi=None, epi_ins=(), epi_outs=None, epi_reds=(), deps=(), j_outer=False, a_halves=False,
        b_halves=False, b_resident=False):
    if a_halves:
        _, m, kdim = a.shape
        kdim *= 2
    elif ta:
        kdim, m = a.shape
    else:
        m, kdim = a.shape
    ns = None
    if b_stacked:
        ns = b.shape[2]
        n = b.shape[1] if tb else N_CHIPS * ns
        assert kdim == (N_CHIPS * ns if tb else b.shape[1]), (name, a.shape, b.shape)
    else:
        if b_halves:
            n = 2 * b.shape[2]
            assert kdim == b.shape[1] and not tb, (name, a.shape, b.shape)
        else:
            n = b.shape[0] if tb else b.shape[1]
            assert kdim == (b.shape[1] if tb else b.shape[0]), (name, a.shape, b.shape)
        if out_stacked:
            ns = n // N_CHIPS

    def shards(want):
        return max(g for g in (1, 2, 4) if g * ns <= max(want, ns))

    tm = _pick(m, tm, 128 if ta else 8)
    gn = gk = 1
    if (b_stacked and not tb) or out_stacked:
        gn = shards(min(tn, n // 2) if b_halves else tn)
        tn = gn * ns
    else:
        tn = _pick(n, tn)
    if b_stacked and tb:
        gk = shards(tk)
        tk = gk * ns
    else:
        tk = _pick(kdim, tk, 8 if ta else 128)
    nm, nn, nk = m // tm, n // tn, kdim // tk

    def ij(f):
        return (lambda g0, g1, k: f(g1, g0, k)) if j_outer else f

    if a_halves:
        assert b_stacked and tb and gk == N_CHIPS and nk == 1, name
        a_spec = pl.BlockSpec((2, tm, kdim // 2), ij(lambda i, j, k: (0, i, 0)))
    elif ta:
        a_spec = pl.BlockSpec((tk, tm), ij(lambda i, j, k: (k, i)))
    else:
        a_spec = pl.BlockSpec((tm, tk), ij(lambda i, j, k: (i, k)))
    if b_stacked and not tb:
        b_spec = pl.BlockSpec((gn, tk, ns), ij(lambda i, j, k: (j, k, 0)))
    elif b_stacked and tb:
        b_spec = pl.BlockSpec((gk, tn, ns), ij(lambda i, j, k: (k, j, 0)))
    elif b_halves:
        bph = (n // 2) // tn
        b_spec = pl.BlockSpec((None, tk, tn), ij(lambda i, j, k: (j // bph, k, j % bph)))
    elif tb:
        b_spec = pl.BlockSpec((tn, tk), ij(lambda i, j, k: (j, k)))
    else:
        b_spec = pl.BlockSpec((tk, tn), ij(lambda i, j, k: (k, j)))
    dims = (((0 if ta else 1,), (1 if tb else 0,)), ((), ()))

    if epi_outs is None:
        if out_stacked:
            epi_outs = [((N_CHIPS, m, ns), out_dtype, (gn, tm, ns), lambda i, j: (j, i, 0))]
        else:
            epi_outs = [((m, n), out_dtype, (tm, tn), lambda i, j: (i, j))]
    ne, no, nd, nr = len(epi_ins), len(epi_outs), len(deps), len(epi_reds)
    assert not nr or (nn == 1 and not j_outer), name

    def body(a_ref, b_ref, *rest):
        e_refs = rest[:ne]
        o_refs = rest[ne + nd:ne + nd + no]
        r_refs = rest[ne + nd + no:ne + nd + no + nr]
        scratch = rest[ne + nd + no + nr:]
        av = None if a_halves else a_ref[...].astype(BF16)
        if b_stacked and not tb:
            parts = [lax.dot_general(av, b_ref[s].astype(BF16), dims, preferred_element_type=F32) for s in range(gn)]
        elif b_stacked and tb:
            p = None
            for s in range(gk):
                if a_halves:
                    a_s = a_ref[s // 2, :, (s % 2) * ns:(s % 2 + 1) * ns].astype(BF16)
                else:
                    a_s = av[:, s * ns:(s + 1) * ns]
                q = lax.dot_general(a_s, b_ref[s].astype(BF16), dims, preferred_element_type=F32)
                p = q if p is None else p + q
            parts = [p]
        else:
            parts = [lax.dot_general(av, b_ref[...].astype(BF16), dims, preferred_element_type=F32)]

        def finish(acc_parts):
            if epi is None and out_stacked:
                acc = acc_parts[0]
                for s in range(gn):
                    o_refs[0][s] = acc[:, s * ns:(s + 1) * ns].astype(out_dtype)
            elif epi is None:
                w = acc_parts[0].shape[1]
                for s, part in enumerate(acc_parts):
                    o_refs[0][:, s * w:(s + 1) * w] = part.astype(out_dtype)
            else:
                acc = acc_parts[0] if len(acc_parts) == 1 else jnp.concatenate(acc_parts, axis=1)
                vals = epi(acc, *[r[...] for r in e_refs])
                if nr:
                    vals, reds = vals

                    @pl.when(pl.program_id(0) == 0)
                    def _():
                        for r in r_refs:
                            r[...] = jnp.zeros_like(r)
                    for r, val in zip(r_refs, reds):
                        r[...] += val
                for o_ref, val in zip(o_refs, vals):
                    if isinstance(val, tuple) and val[0] == "at":
                        o_ref[:, val[1]:val[1] + val[2].shape[1]] = val[2].astype(o_ref.dtype)
                    else:
                        o_ref[...] = val.astype(o_ref.dtype)

        if nk == 1:
            finish(parts)
        else:
            acc_ref = scratch[0]
            k = pl.program_id(2)
            w = parts[0].shape[1]

            @pl.when(k == 0)
            def _():
                for s, part in enumerate(parts):
                    acc_ref[:, s * w:(s + 1) * w] = part

            @pl.when(k > 0)
            def _():
                for s, part in enumerate(parts):
                    acc_ref[:, s * w:(s + 1) * w] += part

            @pl.when(k == nk - 1)
            def _():
                finish([acc_ref[...]])

    def _ij(f):
        return ij(lambda i, j, k: f(i, j))

    if b_resident:
        assert nk == 1 and (nn == 1 or j_outer), name
        b_spec = pl.BlockSpec(b_spec.block_shape, b_spec.index_map, pipeline_mode=pl.Buffered(1))
    in_specs = [a_spec, b_spec] + [pl.BlockSpec(blk, _ij(f)) for (_, blk, f) in epi_ins] + [ANY_SPEC] * nd
    out_specs = [pl.BlockSpec(blk, _ij(f)) for (_, _, blk, f) in epi_outs]
    out_specs += [pl.BlockSpec((1, w), lambda *_: (0, 0)) for w in epi_reds]
    out_shape = [jax.ShapeDtypeStruct(s, dt) for (s, dt, _, _) in epi_outs] + [jax.ShapeDtypeStruct((1, w), F32) for w in epi_reds]
    scratch = [pltpu.VMEM((tm, tn), F32)] if nk > 1 else []
    grid = (nn, nm, nk) if j_outer else (nm, nn, nk)
    res = _pcall(body, name=name, grid=grid, in_specs=in_specs, out_specs=out_specs, out_shape=out_shape, scratch_shapes=scratch,
                 compiler_params=_params(*(("arbitrary",) * 3 if nr else ("parallel", "parallel", "arbitrary"))))(
                     a, b, *[x for x, _, _ in epi_ins], *deps)
    return res[0] if len(res) == 1 else res


def _norm_fwd(name, x, g, sc, sh, deps=()):
    d = x.shape[1]

    def fn(iv, pv):
        (xv,), (gv, scv, shv) = iv, pv
        r = lax.rsqrt(jnp.mean(xv * xv, axis=-1, keepdims=True) + EPS)
        return [xv * r * gv * (1.0 + scv) + shv], []

    return _rowwise(name, fn, [(x, d, 0)], [g, sc, sh], [(d, BF16)], [], 512, deps=deps)[0]


def _norm_bwd(name, dh, x, dres, g, sc, y_up, gt_up, factor_up):
    d = x.shape[1]
    up = y_up is not None

    def fn(iv, pv):
        dhv, xv, drv = iv[:3]
        gv, scv = pv[:2]
        dhv = dhv.astype(F32)
        r = lax.rsqrt(jnp.mean(xv * xv, axis=-1, keepdims=True) + EPS)
        xr = xv * r
        dn = dhv * (1.0 + scv)
        gd = gv * dn
        dx = drv + r * (gd - xr * jnp.mean(gd * xr, axis=-1, keepdims=True))
        outs, reds = [dx], [_colsum(dhv), _colsum(dhv * xr * gv), _colsum(dn * xr)]
        if up:
            fdx = factor_up * dx
            outs.append(pv[2] * fdx)
            reds.append(_colsum(fdx * iv[3].astype(F32)))
        return outs, reds

    ins = [(dh, d, 0), (x, d, 0), (dres, d, 0)] + ([(y_up, d, 0)] if up else [])
    res = _rowwise(name, fn, ins, [g, sc] + ([gt_up] if up else []), [(d, F32)] + ([(d, BF16)] if up else []),
                   [d] * (4 if up else 3), 256)
    if up:
        dx, dy_up, dsh, dsc, dg, dgt_up = res
        return dx, dy_up, dsh, dsc, dg, dgt_up
    dx, dsh, dsc, dg = res
    return dx, None, dsh, dsc, dg, None


def _loss_bwd(name, x, tgt, g, y_up, gt_up, factor_up):
    d = x.shape[1]

    def fn(iv, pv):
        (xv, tv, yv), (gv, gtv) = iv, pv
        r = lax.rsqrt(jnp.mean(xv * xv, axis=-1, keepdims=True) + EPS)
        xr = xv * r
        e = xr * gv - tv
        loss_row = 0.5 * jnp.mean(e * e, axis=-1, keepdims=True)
        dout = e * (1.0 / d)
        gd = gv * dout
        dx = r * (gd - xr * jnp.mean(gd * xr, axis=-1, keepdims=True))
        fdx = factor_up * dx
        return [dx, gtv * fdx], [_colsum(dout * xr), _colsum(loss_row * jnp.ones((1, 128), F32)), _colsum(fdx * yv.astype(F32))]

    return _rowwise(name, fn, [(x, d, 0), (tgt, d, 0), (y_up, d, 0)], [g, gt_up], [(d, F32), (d, BF16)], [d, 128, d], 256)


def _resgate_bwd(name, dx, y, gt, factor, deps=()):
    d = dx.shape[1]

    def fn(iv, pv):
        (dxv, yv), (gtv,) = iv, pv
        return [factor * gtv * dxv], [_colsum(factor * dxv * yv.astype(F32))]

    return _rowwise(name, fn, [(dx, d, 0), (y, d, 0)], [gt], [(d, BF16)], [d], 512, deps=deps)


def _swiglu_fwd(name, ab):
    f = ab.shape[1] // 2

    def fn(iv, pv):
        a, b = iv[0].astype(F32), iv[1].astype(F32)
        return [a * _sigmoid(a) * b], []

    return _rowwise(name, fn, [(ab, f, 0), (ab, f, 1)], [], [(f, BF16)], [], 256)[0]


def _swiglu_bwd(name, dact, ab, deps=()):
    f = ab.shape[1] // 2

    def fn(iv, pv):
        dv, a, b = iv[0].astype(F32), iv[1].astype(F32), iv[2].astype(F32)
        s = _sigmoid(a)
        return [[dv * b * (s * (1.0 + a * (1.0 - s))), dv * (a * s)]], []

    return _rowwise(name, fn, [(dact, f, 0), (ab, f, 0), (ab, f, 1)], [], [(2 * f, BF16)], [], 256, deps=deps)[0]


def _gates_fwd(name, y_pool, y_ssm, uin, d, pw):
    cb = (2 * pw) // d

    def fn(iv, pv):
        yp, ys, glp, gls = [v.astype(F32) for v in iv]
        return [_sigmoid(glp) * yp + _sigmoid(gls) * ys], []

    return _rowwise(name, fn, [(y_pool, d, 0), (y_ssm, d, 0), (uin, d, cb), (uin, d, cb + 1)], [], [(d, BF16)], [], 256)[0]


def _gates_bwd(name, dm, y_pool, y_ssm, uin, d, pw):
    cb = (2 * pw) // d

    def fn(iv, pv):
        dmv, yp, ys, glp, gls = [v.astype(F32) for v in iv]
        sp, ss = _sigmoid(glp), _sigmoid(gls)
        return [dmv * sp, dmv * ss, ("at", 2 * pw, [dmv * yp * sp * (1.0 - sp), dmv * ys * ss * (1.0 - ss)])], []

    return _rowwise(name, fn, [(dm, d, 0), (y_pool, d, 0), (y_ssm, d, 0), (uin, d, cb), (uin, d, cb + 1)], [],
                    [(d, BF16), (d, BF16), (2 * pw + 2 * d, BF16)], [], 256)


def _glu_fwd(name, gv, b_glu):
    w = gv.shape[1] // 2

    def fn(iv, pv):
        gvv, (bv,) = iv[0].astype(F32) + pv[0], pv
        return [gvv[:, :w] * _sigmoid(gvv[:, w:])], []

    return _rowwise(name, fn, [(gv, 2 * w, 0)], [b_glu], [(w, BF16)], [], 512)[0]


def _glu_bwd(name, dsg, gv, b_glu, deps=()):
    w = gv.shape[1] // 2

    def fn(iv, pv):
        dv = iv[0].astype(F32)
        gvv = iv[1].astype(F32) + pv[0]
        val, s = gvv[:, :w], _sigmoid(gvv[:, w:])
        dval = dv * s
        dgate = dv * val * s * (1.0 - s)
        return [[dval, dgate]], [jnp.concatenate([_colsum(dval), _colsum(dgate)], axis=1)]

    return _rowwise(name, fn, [(dsg, w, 0), (gv, 2 * w, 0)], [b_glu], [(2 * w, BF16)], [2 * w], 512, deps=deps)


def _adamw(name, w, g, m, v, tm=256, with_g=True):
    c = w.shape[1]

    def fn(iv, pv):
        wv, gv, mv, vv = iv
        mn = ADAM_B1 * mv + (1.0 - ADAM_B1) * gv
        vn = ADAM_B2 * vv + (1.0 - ADAM_B2) * (gv * gv)
        m_hat = mn / (1.0 - ADAM_B1 ** ADAM_STEP)
        v_hat = vn / (1.0 - ADAM_B2 ** ADAM_STEP)
        delta = -ADAM_LR * (m_hat / (jnp.sqrt(v_hat) + ADAM_EPS) + ADAM_WD * wv)
        return ([gv] if with_g else []) + [delta, mn, vn], []

    return _rowwise(name, fn, [(w, c, 0), (g, c, 0), (m, c, 0), (v, c, 0)], [], [(c, F32)] * (4 if with_g else 3), [],
                    _pick(w.shape[0], tm, 8))


def _adamw_small(name, w, g8, m, v):
    r = w.shape[0]

    def body(w_ref, g_ref, m_ref, v_ref, go_ref, d_ref, mo_ref, vo_ref):
        gv = g_ref[0:r, :]
        for k in range(1, N_DEV):
            gv = gv + g_ref[k * r:(k + 1) * r, :]
        mn = ADAM_B1 * m_ref[...] + (1.0 - ADAM_B1) * gv
        vn = ADAM_B2 * v_ref[...] + (1.0 - ADAM_B2) * (gv * gv)
        m_hat = mn / (1.0 - ADAM_B1 ** ADAM_STEP)
        v_hat = vn / (1.0 - ADAM_B2 ** ADAM_STEP)
        go_ref[...] = gv
        d_ref[...] = -ADAM_LR * (m_hat / (jnp.sqrt(v_hat) + ADAM_EPS) + ADAM_WD * w_ref[...])
        mo_ref[...] = mn
        vo_ref[...] = vn

    return _pcall(body, name=name, out_shape=[jax.ShapeDtypeStruct((r, 128), F32)] * 4,
                  compiler_params=pltpu.CompilerParams(vmem_limit_bytes=VMEM_LIMIT))(w, g8, m, v)


def _pool_fwd(name, uin, pool_w, pool_b, pool_scale, pw, tm=512):
    t = uin.shape[0]
    tm = min(tm, t)
    ng = len(POOL_WINDOWS)
    gw = pw // ng

    def body(u_ref, w_ref, b_ref, s_ref, p_ref, z_ref, ext):
        i = pl.program_id(0)

        @pl.when(i == 0)
        def _():
            ext[0:POOL_HALO, :] = jnp.zeros((POOL_HALO, pw), F32)

        u = u_ref[...]
        ext[POOL_HALO:POOL_HALO + tm, :] = u
        pos = i * tm + lax.broadcasted_iota(jnp.int32, (tm, 1), 0)
        for k, win in enumerate(POOL_WINDOWS):
            cols = slice(k * gw, (k + 1) * gw)
            acc = u[:, cols]
            for j in range(1, win):
                acc = acc + ext[POOL_HALO - j:POOL_HALO - j + tm, cols]
            cnt = jnp.minimum(pos + 1, win).astype(F32)
            z = acc / cnt - u[:, cols]
            zp = jnp.dot(z.astype(BF16), w_ref[k].astype(BF16), preferred_element_type=F32) + b_ref[:, cols]
            p_ref[:, cols] = (zp * s_ref[:, cols]).astype(BF16)
            z_ref[:, cols] = z.astype(BF16)
        ext[0:POOL_HALO, :] = u[tm - POOL_HALO:tm, :]

    return _pcall(
        body, name=name, grid=(t // tm,),
        in_specs=[pl.BlockSpec((tm, pw), lambda i: (i, 0)), pl.BlockSpec(pool_w.shape, lambda i: (0, 0, 0)),
                  pl.BlockSpec(pool_b.shape, lambda i: (0, 0)), pl.BlockSpec(pool_scale.shape, lambda i: (0, 0))],
        out_specs=[pl.BlockSpec((tm, pw), lambda i: (i, 0))] * 2,
        out_shape=[jax.ShapeDtypeStruct((t, pw), BF16)] * 2,
        scratch_shapes=[pltpu.VMEM((POOL_HALO + tm, pw), F32)],
        compiler_params=_params("arbitrary"))(uin, pool_w, pool_b, pool_scale)


def _pool_bwd(name, dp, z, pool_w, pool_b, pool_scale, duin, tm=512):
    t, pw = z.shape
    tm = min(tm, t)
    nb = t // tm
    ng = len(POOL_WINDOWS)
    gw = pw // ng

    def body(dp_ref, z_ref, w_ref, b_ref, s_ref, duin_ref, du_ref, dw_ref, db_ref, ds_ref, ext):
        i = pl.program_id(0)

        @pl.when(i == 0)
        def _():
            ext[tm:tm + POOL_HALO, :] = jnp.zeros((POOL_HALO, pw), F32)
            dw_ref[...] = jnp.zeros_like(dw_ref)
            db_ref[...] = jnp.zeros_like(db_ref)
            ds_ref[...] = jnp.zeros_like(ds_ref)

        pos = (nb - 1 - i) * tm + lax.broadcasted_iota(jnp.int32, (tm, 1), 0)
        for k, win in enumerate(POOL_WINDOWS):
            cols = slice(k * gw, (k + 1) * gw)
            zk = z_ref[:, cols]
            dpk = dp_ref[:, cols].astype(F32)
            wk = w_ref[k].astype(BF16)
            zp = jnp.dot(zk, wk, preferred_element_type=F32) + b_ref[:, cols]
            ds_ref[:, cols] += _colsum(dpk * zp)
            dzp = dpk * s_ref[:, cols]
            db_ref[:, cols] += _colsum(dzp)
            dzpb = dzp.astype(BF16)
            dz = lax.dot_general(dzpb, wk, (((1,), (1,)), ((), ())), preferred_element_type=F32)
            dw_ref[k] += lax.dot_general(zk, dzpb, (((0,), (0,)), ((), ())), preferred_element_type=F32)
            cnt = jnp.minimum(pos + 1, win).astype(F32)
            r = dz / cnt
            ext[0:tm, cols] = r
            acc = r - dz
            for j in range(1, win):
                acc = acc + ext[j:j + tm, cols]
            du_ref[:, cols] = acc.astype(BF16)
        ext[tm:tm + POOL_HALO, :] = ext[0:POOL_HALO, :]

    rev = lambda i: (nb - 1 - i, 0)
    return _pcall(
        body, name=name, grid=(nb,),
        in_specs=[pl.BlockSpec((tm, pw), rev), pl.BlockSpec((tm, pw), rev), pl.BlockSpec(pool_w.shape, lambda i: (0, 0, 0)),
                  pl.BlockSpec(pool_b.shape, lambda i: (0, 0)), pl.BlockSpec(pool_scale.shape, lambda i: (0, 0)), ANY_SPEC],
        out_specs=[pl.BlockSpec((tm, pw), rev), pl.BlockSpec(pool_w.shape, lambda i: (0, 0, 0)),
                   pl.BlockSpec((1, pw), lambda i: (0, 0)), pl.BlockSpec((1, pw), lambda i: (0, 0))],
        out_shape=[jax.ShapeDtypeStruct(duin.shape, BF16), jax.ShapeDtypeStruct(pool_w.shape, F32),
                   jax.ShapeDtypeStruct((1, pw), F32), jax.ShapeDtypeStruct((1, pw), F32)],
        scratch_shapes=[pltpu.VMEM((tm + POOL_HALO, pw), F32)], input_output_aliases={5: 0},
        compiler_params=_params("arbitrary"))(dp, z, pool_w, pool_b, pool_scale, duin)


def _ssm_disc(lrl, li, ldt):
    lr = -jnp.exp(lrl)
    dt = jnp.exp(ldt)
    mag = jnp.exp(lr * dt)
    ang = li * dt
    ab_re = mag * jnp.cos(ang)
    ab_im = mag * jnp.sin(ang)
    num_re = ab_re - 1.0
    num_im = ab_im
    den = lr * lr + li * li
    f_re = (num_re * lr + num_im * li) / den
    f_im = (num_im * lr - num_re * li) / den
    return lr, dt, mag, ang, ab_re, ab_im, num_re, num_im, den, f_re, f_im


def _ssm_prep(name, lrl, li, ldt, b_re, b_im):
    gn, h = b_re.shape

    def body(lrl_ref, li_ref, ldt_ref, br_ref, bi_ref, lrdt_ref, ang_ref, bbr_ref, bbi_ref):
        lr, dt, _, ang, _, _, _, _, _, f_re, f_im = _ssm_disc(lrl_ref[...], li_ref[...], ldt_ref[...])
        lrdt_ref[...] = lr * dt
        ang_ref[...] = ang
        br, bi = br_ref[...], bi_ref[...]
        bbr_ref[...] = f_re * br - f_im * bi
        bbi_ref[...] = f_re * bi + f_im * br

    col = jax.ShapeDtypeStruct((gn, 1), F32)
    mat = jax.ShapeDtypeStruct((gn, h), F32)
    return _pcall(body, name=name, out_shape=[col, col, mat, mat])(lrl, li, ldt, b_re, b_im)


def _ssm_param_bwd(name, lrl, li, ldt, b_re, b_im, g_abre, g_abim, g_bbre, g_bbim):
    gn, h = b_re.shape

    def body(lrl_ref, li_ref, ldt_ref, br_ref, bi_ref, gar_ref, gai_ref, gbr_ref, gbi_ref,
             dlrl_ref, dli_ref, dldt_ref, dbr_ref, dbi_ref):
        li_v = li_ref[...]
        lr, dt, mag, ang, ab_re, ab_im, num_re, num_im, den, f_re, f_im = _ssm_disc(lrl_ref[...], li_v, ldt_ref[...])
        br, bi = br_ref[...], bi_ref[...]
        gbr, gbi = gbr_ref[...], gbi_ref[...]
        g_fre = jnp.sum(gbr * br + gbi * bi, axis=1, keepdims=True)
        g_fim = jnp.sum(gbi * br - gbr * bi, axis=1, keepdims=True)
        dbr_ref[...] = gbr * f_re + gbi * f_im
        dbi_ref[...] = gbi * f_re - gbr * f_im
        g_num_re = (g_fre * lr - g_fim * li_v) / den
        g_num_im = (g_fre * li_v + g_fim * lr) / den
        g_den = -(g_fre * f_re + g_fim * f_im) / den
        g_lr = (g_fre * num_re + g_fim * num_im) / den + g_den * 2.0 * lr
        g_li = (g_fre * num_im - g_fim * num_re) / den + g_den * 2.0 * li_v
        g_are = gar_ref[...] + g_num_re
        g_aim = gai_ref[...] + g_num_im
        g_mag = g_are * jnp.cos(ang) + g_aim * jnp.sin(ang)
        g_ang = g_aim * ab_re - g_are * ab_im
        g_lrdt = g_mag * mag
        g_lr = g_lr + g_lrdt * dt
        g_dt = g_lrdt * lr + g_ang * li_v
        g_li = g_li + g_ang * dt
        dlrl_ref[...] = g_lr * lr
        dli_ref[...] = g_li
        dldt_ref[...] = g_dt * dt

    col = jax.ShapeDtypeStruct((gn, 1), F32)
    mat = jax.ShapeDtypeStruct((gn, h), F32)
    return _pcall(body, name=name, out_shape=[col, col, col, mat, mat])(lrl, li, ldt, b_re, b_im, g_abre, g_abim, g_bbre, g_bbim)


def _pow_rows(lrdt, ang, k):
    mag = jnp.exp(k * lrdt)
    return mag * jnp.cos(k * ang), mag * jnp.sin(k * ang)


def _ssm_chunk(t):
    return 256 if t >= 2048 else 128


def _to_segments(dst, srcs, jn):
    for q, src in enumerate(srcs):
        for j in range(jn):
            dst[8 * j:8 * j + 8, 128 * q:128 * (q + 1)] = src[pl.ds(j, 8, stride=jn), :]


def _from_segments(dst, src, q, jn, dtype):
    for s in range(8):
        dst[s * jn:(s + 1) * jn, 128 * q:128 * (q + 1)] = src[q, pl.ds(s, jn, stride=8), :].astype(dtype)


def _fill_rows8(dst_re, dst_im, v_re, v_im):
    for j in range(v_re.shape[0]):
        dst_re[8 * j:8 * j + 8, :] = jnp.broadcast_to(v_re[j:j + 1, :], (8, v_re.shape[1]))
        dst_im[8 * j:8 * j + 8, :] = jnp.broadcast_to(v_im[j:j + 1, :], (8, v_im.shape[1]))


def _cmul(ar, ai, br, bi):
    return ar * br - ai * bi, ar * bi + ai * br


def _cmul_conj(ar, ai, br, bi):
    return ar * br + ai * bi, ar * bi - ai * br


def _ssm_fwd(name, uin, lrdt, ang, bb_re, bb_im, cc_re, cc_im, d_skip, sw):
    t = uin.shape[0]
    gn = lrdt.shape[1]
    lc = _ssm_chunk(t)
    jn = lc // 8
    ub, sb = sw // SSM_BLOCKS, gn // SSM_BLOCKS
    nq = sw // 128
    assert ub == 128 and nq == SSM_BLOCKS

    def body(*refs):
        u_refs = refs[:nq]
        (lrdt_ref, ang_ref, bbr_ref, bbi_ref, ccr_ref, cci_ref, d_ref, y_ref, ge_ref, sre_ref, sim_ref,
         p_re, p_im, a_re, a_im, up, yp, cst_re, cst_im, car_re, car_im) = refs[nq:]
        i = pl.program_id(0)
        lrdt_v, ang_v = lrdt_ref[...], ang_ref[...]

        @pl.when(i == 0)
        def _():
            k = (lax.broadcasted_iota(jnp.int32, (jn, 1), 0) + 1).astype(F32)
            _fill_rows8(p_re, p_im, *_pow_rows(lrdt_v, ang_v, k))
            car_re[...] = jnp.zeros_like(car_re)
            car_im[...] = jnp.zeros_like(car_im)

        _to_segments(up, u_refs, jn)
        u = up[...]
        ubf = u.astype(BF16)
        for q in range(SSM_BLOCKS):
            uq = ubf[:, q * ub:(q + 1) * ub]
            a_re[:, q * sb:(q + 1) * sb] = jnp.dot(uq, bbr_ref[q], preferred_element_type=F32)
            a_im[:, q * sb:(q + 1) * sb] = jnp.dot(uq, bbi_ref[q], preferred_element_type=F32)
        a1r, a1i = _pow_rows(lrdt_v, ang_v, 1.0)
        ajr, aji = _pow_rows(lrdt_v, ang_v, float(jn))
        for q in range(SSM_BLOCKS):
            cols = slice(q * sb, (q + 1) * sb)
            ar8 = jnp.broadcast_to(a1r[:, cols], (8, sb))
            ai8 = jnp.broadcast_to(a1i[:, cols], (8, sb))

            def step(j, carry, cols=cols, ar8=ar8, ai8=ai8):
                sr, si = carry
                rows = pl.ds(pl.multiple_of(j * 8, 8), 8)
                mr, mi = _cmul(ar8, ai8, sr, si)
                nr, ni = mr + a_re[rows, cols], mi + a_im[rows, cols]
                a_re[rows, cols] = nr
                a_im[rows, cols] = ni
                return nr, ni

            lax.fori_loop(1, jn, step, (a_re[0:8, cols], a_im[0:8, cols]), unroll=4)
        er, ei = a_re[lc - 8:lc, :], a_im[lc - 8:lc, :]
        gr, gi = car_re[...], car_im[...]
        for s in range(8):
            cst_re[s:s + 1, :] = gr
            cst_im[s:s + 1, :] = gi
            mr, mi = _cmul(ajr, aji, gr, gi)
            gr, gi = mr + er[s:s + 1, :], mi + ei[s:s + 1, :]
        car_re[...] = gr
        car_im[...] = gi
        for q in range(SSM_BLOCKS):
            cols = slice(q * sb, (q + 1) * sb)
            cr = jnp.tile(cst_re[:, cols], (jn, 1))
            ci = jnp.tile(cst_im[:, cols], (jn, 1))
            mr, mi = _cmul(p_re[:, cols], p_im[:, cols], cr, ci)
            srb, sib = (a_re[:, cols] + mr).astype(BF16), (a_im[:, cols] + mi).astype(BF16)
            sre_ref[:, cols] = srb
            sim_ref[:, cols] = sib
            ycols = slice(q * ub, (q + 1) * ub)
            y = (jnp.dot(srb, ccr_ref[q], preferred_element_type=F32) - jnp.dot(sib, cci_ref[q], preferred_element_type=F32)
                 + d_ref[:, ycols] * u[:, ycols])
            yp[q] = y
            _from_segments(y_ref, yp, q, jn, F32)
            yt = y_ref[:, ycols]
            ge_ref[:, ycols] = (0.5 * yt * (1.0 + lax.erf(yt * (1.0 / math.sqrt(2.0))))).astype(BF16)

    row = lambda i: (0, 0)
    blk3 = lambda i: (0, 0, 0)
    return _pcall(
        body, name=name, grid=(t // lc,),
        in_specs=[pl.BlockSpec((lc, 128), functools.partial(lambda i, q: (i, nq + q), q=q)) for q in range(nq)]
        + [pl.BlockSpec((1, gn), row), pl.BlockSpec((1, gn), row),
           pl.BlockSpec(bb_re.shape, blk3), pl.BlockSpec(bb_im.shape, blk3),
           pl.BlockSpec(cc_re.shape, blk3), pl.BlockSpec(cc_im.shape, blk3), pl.BlockSpec((1, sw), row)],
        out_specs=[pl.BlockSpec((lc, sw), lambda i: (i, 0)), pl.BlockSpec((lc, sw), lambda i: (i, 0)),
                   pl.BlockSpec((lc, gn), lambda i: (i, 0)), pl.BlockSpec((lc, gn), lambda i: (i, 0))],
        out_shape=[jax.ShapeDtypeStruct((t, sw), F32), jax.ShapeDtypeStruct((t, sw), BF16),
                   jax.ShapeDtypeStruct((t, gn), BF16), jax.ShapeDtypeStruct((t, gn), BF16)],
        scratch_shapes=[pltpu.VMEM((lc, gn), F32), pltpu.VMEM((lc, gn), F32), pltpu.VMEM((lc, gn), F32), pltpu.VMEM((lc, gn), F32),
                        pltpu.VMEM((lc, sw), F32), pltpu.VMEM((nq, lc, 128), F32),
                        pltpu.VMEM((8, gn), F32), pltpu.VMEM((8, gn), F32), pltpu.VMEM((1, gn), F32), pltpu.VMEM((1, gn), F32)],
        compiler_params=_params("arbitrary"))(*([uin] * nq), lrdt, ang, bb_re, bb_im, cc_re, cc_im, d_skip)


def _ssm_bwd(name, dge, y, uin, s_re, s_im, lrdt, ang, bbt_re, bbt_im, cct_re, cct_im, d_skip, sw, duin):
    t = uin.shape[0]
    gn = lrdt.shape[1]
    lc = _ssm_chunk(t)
    nb = t // lc
    jn = lc // 8
    ub, sb = sw // SSM_BLOCKS, gn // SSM_BLOCKS
    nq = sw // 128
    tail = 16

    def body(dge_ref, y_ref, u_ref, sre_ref, sim_ref, tre_ref, tim_ref, lrdt_ref, ang_ref, btr_ref, bti_ref, ctr_ref, cti_ref,
             d_ref, duin_ref, du_ref, dar_ref, dai_ref, dbr_ref, dbi_ref, dcr_ref, dci_ref, dd_ref,
             q_re, q_im, a_re, a_im, dyp, up, dys, us, dup, cst_re, cst_im, sp_re, sp_im, car_re, car_im):
        i = pl.program_id(0)
        lrdt_v, ang_v = lrdt_ref[...], ang_ref[...]

        @pl.when(i == 0)
        def _():
            k = (jn - lax.broadcasted_iota(jnp.int32, (jn, 1), 0)).astype(F32)
            _fill_rows8(q_re, q_im, *_pow_rows(lrdt_v, ang_v, k))
            car_re[...] = jnp.zeros_like(car_re)
            car_im[...] = jnp.zeros_like(car_im)
            for r in (dar_ref, dai_ref, dbr_ref, dbi_ref, dcr_ref, dci_ref, dd_ref):
                r[...] = jnp.zeros_like(r)

        yv = y_ref[...]
        ut = u_ref[...]
        cdf = 0.5 * (1.0 + lax.erf(yv * (1.0 / math.sqrt(2.0))))
        pdf = jnp.exp(-0.5 * yv * yv) * (1.0 / math.sqrt(2.0 * math.pi))
        dyt = dge_ref[...].astype(F32) * (cdf + yv * pdf)
        dd_ref[...] += _colsum(dyt * ut)
        for q in range(nq):
            dys[q] = dyt[:, 128 * q:128 * (q + 1)]
            us[q] = ut[:, 128 * q:128 * (q + 1)]
        _to_segments(dyp, [dys.at[q] for q in range(nq)], jn)
        _to_segments(up, [us.at[q] for q in range(nq)], jn)
        dy = dyp[...]
        u = up[...]
        dyb = dy.astype(BF16)
        ubf = u.astype(BF16)
        for q in range(SSM_BLOCKS):
            dq = dyb[:, q * ub:(q + 1) * ub]
            a_re[:, q * sb:(q + 1) * sb] = jnp.dot(dq, ctr_ref[q], preferred_element_type=F32)
            a_im[:, q * sb:(q + 1) * sb] = -jnp.dot(dq, cti_ref[q], preferred_element_type=F32)
        a1r, a1i = _pow_rows(lrdt_v, ang_v, 1.0)
        ajr, aji = _pow_rows(lrdt_v, ang_v, float(jn))
        for q in range(SSM_BLOCKS):
            cols = slice(q * sb, (q + 1) * sb)
            ar8 = jnp.broadcast_to(a1r[:, cols], (8, sb))
            ai8 = jnp.broadcast_to(a1i[:, cols], (8, sb))

            def step(jj, carry, cols=cols, ar8=ar8, ai8=ai8):
                sr, si = carry
                rows = pl.ds(pl.multiple_of((jn - 2 - jj) * 8, 8), 8)
                mr, mi = _cmul_conj(ar8, ai8, sr, si)
                nr, ni = mr + a_re[rows, cols], mi + a_im[rows, cols]
                a_re[rows, cols] = nr
                a_im[rows, cols] = ni
                return nr, ni

            lax.fori_loop(0, jn - 1, step, (a_re[lc - 8:lc, cols], a_im[lc - 8:lc, cols]), unroll=4)
        er, ei = a_re[0:8, :], a_im[0:8, :]
        hr, hi = car_re[...], car_im[...]
        for s in range(7, -1, -1):
            cst_re[s:s + 1, :] = hr
            cst_im[s:s + 1, :] = hi
            mr, mi = _cmul_conj(ajr, aji, hr, hi)
            hr, hi = mr + er[s:s + 1, :], mi + ei[s:s + 1, :]
        car_re[...] = hr
        car_im[...] = hi
        first = (i == nb - 1).astype(F32)
        sp_re[0:tail, :] = tre_ref[...].astype(F32) * (1.0 - first)
        sp_im[0:tail, :] = tim_ref[...].astype(F32) * (1.0 - first)
        sp_re[tail:tail + 8, :] = sre_ref[lc - tail:lc, :].astype(F32)[tail - 8:tail]
        sp_im[tail:tail + 8, :] = sim_ref[lc - tail:lc, :].astype(F32)[tail - 8:tail]
        tn_dims = (((0,), (0,)), ((), ()))
        for q in range(SSM_BLOCKS):
            cols = slice(q * sb, (q + 1) * sb)
            ycols = slice(q * ub, (q + 1) * ub)
            cr = jnp.tile(cst_re[:, cols], (jn, 1))
            ci = jnp.tile(cst_im[:, cols], (jn, 1))
            mr, mi = _cmul_conj(q_re[:, cols], q_im[:, cols], cr, ci)
            lam_r, lam_i = a_re[:, cols] + mr, a_im[:, cols] + mi
            s_r, s_i = sre_ref[:, cols], sim_ref[:, cols]
            p0r, p0i = sp_re[tail - 1:tail + 7, cols], sp_im[tail - 1:tail + 7, cols]
            l0r, l0i, l1r, l1i = lam_r[0:8], lam_i[0:8], lam_r[8:lc], lam_i[8:lc]
            pvr, pvi = s_r.astype(F32)[0:lc - 8], s_i.astype(F32)[0:lc - 8]
            dar_ref[:, cols] += _colsum(l1r * pvr + l1i * pvi) + _colsum(l0r * p0r + l0i * p0i)
            dai_ref[:, cols] += _colsum(l1i * pvr - l1r * pvi) + _colsum(l0i * p0r - l0r * p0i)
            lrb, lib = lam_r.astype(BF16), lam_i.astype(BF16)
            dup[q] = (jnp.dot(lrb, btr_ref[q], preferred_element_type=F32)
                      + jnp.dot(lib, bti_ref[q], preferred_element_type=F32) + d_ref[:, ycols] * dy[:, ycols])
            _from_segments(du_ref, dup, q, jn, BF16)
            uq = ubf[:, ycols]
            dbr_ref[q] += lax.dot_general(uq, lrb, tn_dims, preferred_element_type=F32)
            dbi_ref[q] += lax.dot_general(uq, lib, tn_dims, preferred_element_type=F32)
            dq = dyb[:, ycols]
            dcr_ref[q] += lax.dot_general(s_r.astype(BF16), dq, tn_dims, preferred_element_type=F32)
            dci_ref[q] -= lax.dot_general(s_i.astype(BF16), dq, tn_dims, preferred_element_type=F32)

    rev = lambda i: (nb - 1 - i, 0)
    tailmap = lambda i: (jnp.maximum((nb - 1 - i) * (lc // tail) - 1, 0), 0)
    row = lambda i: (0, 0)
    blk3 = lambda i: (0, 0, 0)
    return _pcall(
        body, name=name, grid=(nb,),
        in_specs=[pl.BlockSpec((lc, sw), rev), pl.BlockSpec((lc, sw), rev), pl.BlockSpec((lc, sw), lambda i: (nb - 1 - i, 1)),
                  pl.BlockSpec((lc, gn), rev), pl.BlockSpec((lc, gn), rev),
                  pl.BlockSpec((tail, gn), tailmap), pl.BlockSpec((tail, gn), tailmap),
                  pl.BlockSpec((1, gn), row), pl.BlockSpec((1, gn), row),
                  pl.BlockSpec(bbt_re.shape, blk3), pl.BlockSpec(bbt_im.shape, blk3),
                  pl.BlockSpec(cct_re.shape, blk3), pl.BlockSpec(cct_im.shape, blk3), pl.BlockSpec((1, sw), row), ANY_SPEC],
        out_specs=[pl.BlockSpec((lc, sw), lambda i: (nb - 1 - i, 1)), pl.BlockSpec((1, gn), row), pl.BlockSpec((1, gn), row),
                   pl.BlockSpec((SSM_BLOCKS, ub, sb), blk3), pl.BlockSpec((SSM_BLOCKS, ub, sb), blk3),
                   pl.BlockSpec((SSM_BLOCKS, sb, ub), blk3), pl.BlockSpec((SSM_BLOCKS, sb, ub), blk3),
                   pl.BlockSpec((1, sw), row)],
        out_shape=[jax.ShapeDtypeStruct(duin.shape, BF16), jax.ShapeDtypeStruct((1, gn), F32), jax.ShapeDtypeStruct((1, gn), F32),
                   jax.ShapeDtypeStruct((SSM_BLOCKS, ub, sb), F32), jax.ShapeDtypeStruct((SSM_BLOCKS, ub, sb), F32),
                   jax.ShapeDtypeStruct((SSM_BLOCKS, sb, ub), F32), jax.ShapeDtypeStruct((SSM_BLOCKS, sb, ub), F32),
                   jax.ShapeDtypeStruct((1, sw), F32)],
        scratch_shapes=[pltpu.VMEM((lc, gn), F32), pltpu.VMEM((lc, gn), F32), pltpu.VMEM((lc, gn), F32), pltpu.VMEM((lc, gn), F32),
                        pltpu.VMEM((lc, sw), F32), pltpu.VMEM((lc, sw), F32),
                        pltpu.VMEM((nq, lc, 128), F32), pltpu.VMEM((nq, lc, 128), F32), pltpu.VMEM((nq, lc, 128), F32),
                        pltpu.VMEM((8, gn), F32), pltpu.VMEM((8, gn), F32),
                        pltpu.VMEM((tail + 8, gn), F32), pltpu.VMEM((tail + 8, gn), F32),
                        pltpu.VMEM((1, gn), F32), pltpu.VMEM((1, gn), F32)],
        input_output_aliases={14: 0},
        compiler_params=_params("arbitrary"))(dge, y, uin, s_re, s_im, s_re, s_im, lrdt, ang,
                                               bbt_re, bbt_im, cct_re, cct_im, d_skip, duin)


def _ssm_fwd_doubling(name, uin, lrdt, ang, bb_re, bb_im, cc_re, cc_im, d_skip, sw):
    t = uin.shape[0]
    gn = lrdt.shape[1]
    lc = _ssm_chunk(t)
    nsteps = int(math.log2(lc))
    pad = lc // 2
    ub, sb = sw // SSM_BLOCKS, gn // SSM_BLOCKS

    def body(u_ref, lrdt_ref, ang_ref, bbr_ref, bbi_ref, ccr_ref, cci_ref, d_ref, y_ref, ge_ref, sre_ref, sim_ref,
             p_re, p_im, a_re, a_im, b_re, b_im, car_re, car_im):
        i = pl.program_id(0)
        lrdt_v, ang_v = lrdt_ref[...], ang_ref[...]

        @pl.when(i == 0)
        def _():
            k = (lax.broadcasted_iota(jnp.int32, (lc, 1), 0) + 1).astype(F32)
            pr, pi = _pow_rows(lrdt_v, ang_v, k)
            p_re[...] = pr
            p_im[...] = pi
            zeros = jnp.zeros((pad, gn), F32)
            a_re[0:pad, :] = zeros
            a_im[0:pad, :] = zeros
            b_re[0:pad, :] = zeros
            b_im[0:pad, :] = zeros
            car_re[...] = jnp.zeros_like(car_re)
            car_im[...] = jnp.zeros_like(car_im)

        u = u_ref[...]
        ubf = u.astype(BF16)
        for q in range(SSM_BLOCKS):
            uq = ubf[:, q * ub:(q + 1) * ub]
            a_re[pad:pad + lc, q * sb:(q + 1) * sb] = jnp.dot(uq, bbr_ref[q], preferred_element_type=F32)
            a_im[pad:pad + lc, q * sb:(q + 1) * sb] = jnp.dot(uq, bbi_ref[q], preferred_element_type=F32)
        bufs = [(a_re, a_im), (b_re, b_im)]
        for q in range(SSM_BLOCKS):
            cols = slice(q * sb, (q + 1) * sb)
            for j in range(nsteps):
                dd = 1 << j
                (s_re, s_im), (d_re, d_im) = bufs[j % 2], bufs[(j + 1) % 2]
                pr, pi = _pow_rows(lrdt_v[:, cols], ang_v[:, cols], float(dd))
                cr, ci = s_re[pad:pad + lc, cols], s_im[pad:pad + lc, cols]
                hr, hi = s_re[pad - dd:pad - dd + lc, cols], s_im[pad - dd:pad - dd + lc, cols]
                d_re[pad:pad + lc, cols] = cr + (pr * hr - pi * hi)
                d_im[pad:pad + lc, cols] = ci + (pr * hi + pi * hr)
            f_re, f_im = bufs[nsteps % 2]
            cr, ci = car_re[:, cols], car_im[:, cols]
            pr, pi = p_re[:, cols], p_im[:, cols]
            sr = f_re[pad:pad + lc, cols] + (pr * cr - pi * ci)
            si = f_im[pad:pad + lc, cols] + (pr * ci + pi * cr)
            sre_ref[:, cols] = sr
            sim_ref[:, cols] = si
            car_re[:, cols] = sr[lc - 1:lc, :]
            car_im[:, cols] = si[lc - 1:lc, :]
            ycols = slice(q * ub, (q + 1) * ub)
            y = (jnp.dot(sr.astype(BF16), ccr_ref[q], preferred_element_type=F32)
                 - jnp.dot(si.astype(BF16), cci_ref[q], preferred_element_type=F32)
                 + d_ref[:, ycols] * u[:, ycols])
            y_ref[:, ycols] = y
            ge_ref[:, ycols] = (0.5 * y * (1.0 + lax.erf(y * (1.0 / math.sqrt(2.0))))).astype(BF16)

    cblk = sw // sw
    row = lambda i: (0, 0)
    blk3 = lambda i: (0, 0, 0)
    return _pcall(
        body, name=name, grid=(t // lc,),
        in_specs=[pl.BlockSpec((lc, sw), lambda i: (i, cblk)), pl.BlockSpec((1, gn), row), pl.BlockSpec((1, gn), row),
                  pl.BlockSpec(bb_re.shape, blk3), pl.BlockSpec(bb_im.shape, blk3),
                  pl.BlockSpec(cc_re.shape, blk3), pl.BlockSpec(cc_im.shape, blk3), pl.BlockSpec((1, sw), row)],
        out_specs=[pl.BlockSpec((lc, sw), lambda i: (i, 0)), pl.BlockSpec((lc, sw), lambda i: (i, 0)),
                   pl.BlockSpec((lc, gn), lambda i: (i, 0)), pl.BlockSpec((lc, gn), lambda i: (i, 0))],
        out_shape=[jax.ShapeDtypeStruct((t, sw), F32), jax.ShapeDtypeStruct((t, sw), BF16),
                   jax.ShapeDtypeStruct((t, gn), F32), jax.ShapeDtypeStruct((t, gn), F32)],
        scratch_shapes=[pltpu.VMEM((lc, gn), F32), pltpu.VMEM((lc, gn), F32),
                        pltpu.VMEM((pad + lc, gn), F32), pltpu.VMEM((pad + lc, gn), F32),
                        pltpu.VMEM((pad + lc, gn), F32), pltpu.VMEM((pad + lc, gn), F32),
                        pltpu.VMEM((1, gn), F32), pltpu.VMEM((1, gn), F32)],
        compiler_params=_params("arbitrary"))(uin, lrdt, ang, bb_re, bb_im, cc_re, cc_im, d_skip)


def _ssm_bwd_doubling(name, dge, y, uin, s_re, s_im, lrdt, ang, bbt_re, bbt_im, cct_re, cct_im, d_skip, sw):
    t = uin.shape[0]
    gn = lrdt.shape[1]
    lc = _ssm_chunk(t)
    nb = t // lc
    nsteps = int(math.log2(lc))
    pad = lc // 2
    ub, sb = sw // SSM_BLOCKS, gn // SSM_BLOCKS
    tail = 8

    def body(dge_ref, y_ref, u_ref, sre_ref, sim_ref, tre_ref, tim_ref, lrdt_ref, ang_ref, btr_ref, bti_ref, ctr_ref, cti_ref,
             d_ref, du_ref, dar_ref, dai_ref, dbr_ref, dbi_ref, dcr_ref, dci_ref, dd_ref,
             q_re, q_im, a_re, a_im, b_re, b_im, sp_re, sp_im, car_re, car_im):
        i = pl.program_id(0)
        lrdt_v, ang_v = lrdt_ref[...], ang_ref[...]

        @pl.when(i == 0)
        def _():
            k = (lc - lax.broadcasted_iota(jnp.int32, (lc, 1), 0)).astype(F32)
            qr, qi = _pow_rows(lrdt_v, ang_v, k)
            q_re[...] = qr
            q_im[...] = qi
            zeros = jnp.zeros((pad, gn), F32)
            a_re[lc:lc + pad, :] = zeros
            a_im[lc:lc + pad, :] = zeros
            b_re[lc:lc + pad, :] = zeros
            b_im[lc:lc + pad, :] = zeros
            car_re[...] = jnp.zeros_like(car_re)
            car_im[...] = jnp.zeros_like(car_im)
            for r in (dar_ref, dai_ref, dbr_ref, dbi_ref, dcr_ref, dci_ref, dd_ref):
                r[...] = jnp.zeros_like(r)

        first = (i == nb - 1).astype(F32)
        sp_re[0:tail, :] = tre_ref[...] * (1.0 - first)
        sp_im[0:tail, :] = tim_ref[...] * (1.0 - first)
        sp_re[tail:tail + lc, :] = sre_ref[...]
        sp_im[tail:tail + lc, :] = sim_ref[...]

        yv = y_ref[...]
        u = u_ref[...]
        cdf = 0.5 * (1.0 + lax.erf(yv * (1.0 / math.sqrt(2.0))))
        pdf = jnp.exp(-0.5 * yv * yv) * (1.0 / math.sqrt(2.0 * math.pi))
        dy = dge_ref[...].astype(F32) * (cdf + yv * pdf)
        dd_ref[...] += _colsum(dy * u)
        dyb = dy.astype(BF16)
        ubf = u.astype(BF16)
        for q in range(SSM_BLOCKS):
            dq = dyb[:, q * ub:(q + 1) * ub]
            a_re[0:lc, q * sb:(q + 1) * sb] = jnp.dot(dq, ctr_ref[q], preferred_element_type=F32)
            a_im[0:lc, q * sb:(q + 1) * sb] = -jnp.dot(dq, cti_ref[q], preferred_element_type=F32)
        bufs = [(a_re, a_im), (b_re, b_im)]
        tn_dims = (((0,), (0,)), ((), ()))
        for q in range(SSM_BLOCKS):
            cols = slice(q * sb, (q + 1) * sb)
            ycols = slice(q * ub, (q + 1) * ub)
            for j in range(nsteps):
                ds = 1 << j
                (s_r, s_i), (d_r, d_i) = bufs[j % 2], bufs[(j + 1) % 2]
                pr, pi = _pow_rows(lrdt_v[:, cols], ang_v[:, cols], float(ds))
                cr, ci = s_r[0:lc, cols], s_i[0:lc, cols]
                hr, hi = s_r[ds:ds + lc, cols], s_i[ds:ds + lc, cols]
                d_r[0:lc, cols] = cr + (pr * hr + pi * hi)
                d_i[0:lc, cols] = ci + (pr * hi - pi * hr)
            f_r, f_i = bufs[nsteps % 2]
            cr, ci = car_re[:, cols], car_im[:, cols]
            qr, qi = q_re[:, cols], q_im[:, cols]
            lam_r = f_r[0:lc, cols] + (qr * cr + qi * ci)
            lam_i = f_i[0:lc, cols] + (qr * ci - qi * cr)
            car_re[:, cols] = lam_r[0:1, :]
            car_im[:, cols] = lam_i[0:1, :]
            pvr, pvi = sp_re[tail - 1:tail - 1 + lc, cols], sp_im[tail - 1:tail - 1 + lc, cols]
            dar_ref[:, cols] += _colsum(lam_r * pvr + lam_i * pvi)
            dai_ref[:, cols] += _colsum(lam_i * pvr - lam_r * pvi)
            lrb, lib = lam_r.astype(BF16), lam_i.astype(BF16)
            du_ref[:, ycols] = (jnp.dot(lrb, btr_ref[q], preferred_element_type=F32)
                                + jnp.dot(lib, bti_ref[q], preferred_element_type=F32)
                                + d_ref[:, ycols] * dy[:, ycols]).astype(BF16)
            uq = ubf[:, ycols]
            dbr_ref[q] += lax.dot_general(uq, lrb, tn_dims, preferred_element_type=F32)
            dbi_ref[q] += lax.dot_general(uq, lib, tn_dims, preferred_element_type=F32)
            dq = dyb[:, ycols]
            dcr_ref[q] += lax.dot_general(sre_ref[:, cols].astype(BF16), dq, tn_dims, preferred_element_type=F32)
            dci_ref[q] -= lax.dot_general(sim_ref[:, cols].astype(BF16), dq, tn_dims, preferred_element_type=F32)

    cblk = 1
    rev = lambda i: (nb - 1 - i, 0)
    revu = lambda i: (nb - 1 - i, cblk)
    tailmap = lambda i: (jnp.maximum((nb - 1 - i) * (lc // tail) - 1, 0), 0)
    row = lambda i: (0, 0)
    blk3 = lambda i: (0, 0, 0)
    return _pcall(
        body, name=name, grid=(nb,),
        in_specs=[pl.BlockSpec((lc, sw), rev), pl.BlockSpec((lc, sw), rev), pl.BlockSpec((lc, sw), revu),
                  pl.BlockSpec((lc, gn), rev), pl.BlockSpec((lc, gn), rev),
                  pl.BlockSpec((tail, gn), tailmap), pl.BlockSpec((tail, gn), tailmap),
                  pl.BlockSpec((1, gn), row), pl.BlockSpec((1, gn), row),
                  pl.BlockSpec(bbt_re.shape, blk3), pl.BlockSpec(bbt_im.shape, blk3),
                  pl.BlockSpec(cct_re.shape, blk3), pl.BlockSpec(cct_im.shape, blk3), pl.BlockSpec((1, sw), row)],
        out_specs=[pl.BlockSpec((lc, sw), rev), pl.BlockSpec((1, gn), row), pl.BlockSpec((1, gn), row),
                   pl.BlockSpec((SSM_BLOCKS, ub, sb), blk3), pl.BlockSpec((SSM_BLOCKS, ub, sb), blk3),
                   pl.BlockSpec((SSM_BLOCKS, sb, ub), blk3), pl.BlockSpec((SSM_BLOCKS, sb, ub), blk3),
                   pl.BlockSpec((1, sw), row)],
        out_shape=[jax.ShapeDtypeStruct((t, sw), BF16), jax.ShapeDtypeStruct((1, gn), F32), jax.ShapeDtypeStruct((1, gn), F32),
                   jax.ShapeDtypeStruct((SSM_BLOCKS, ub, sb), F32), jax.ShapeDtypeStruct((SSM_BLOCKS, ub, sb), F32),
                   jax.ShapeDtypeStruct((SSM_BLOCKS, sb, ub), F32), jax.ShapeDtypeStruct((SSM_BLOCKS, sb, ub), F32),
                   jax.ShapeDtypeStruct((1, sw), F32)],
        scratch_shapes=[pltpu.VMEM((lc, gn), F32), pltpu.VMEM((lc, gn), F32),
                        pltpu.VMEM((lc + pad, gn), F32), pltpu.VMEM((lc + pad, gn), F32),
                        pltpu.VMEM((lc + pad, gn), F32), pltpu.VMEM((lc + pad, gn), F32),
                        pltpu.VMEM((tail + lc, gn), F32), pltpu.VMEM((tail + lc, gn), F32),
                        pltpu.VMEM((1, gn), F32), pltpu.VMEM((1, gn), F32)],
        compiler_params=_params("arbitrary"))(dge, y, uin, s_re, s_im, s_re, s_im, lrdt, ang, bbt_re, bbt_im, cct_re, cct_im, d_skip)


def _blockdiag_b(bb, sw):
    gpb = (sw // SSM_GROUP) // SSM_BLOCKS
    b4 = bb.reshape(SSM_BLOCKS, gpb, SSM_STATE, SSM_GROUP)
    eye = jnp.eye(gpb, dtype=bb.dtype)
    out = jnp.einsum('qgnh,gk->qghkn', b4, eye)
    return out.reshape(SSM_BLOCKS, gpb * SSM_GROUP, gpb * SSM_STATE)


def _blockdiag_c(cc, sw):
    gpb = (sw // SSM_GROUP) // SSM_BLOCKS
    c4 = cc.reshape(SSM_BLOCKS, gpb, SSM_GROUP, SSM_STATE)
    eye = jnp.eye(gpb, dtype=cc.dtype)
    out = jnp.einsum('qghn,gk->qgnkh', c4, eye)
    return out.reshape(SSM_BLOCKS, gpb * SSM_STATE, gpb * SSM_GROUP)


def _diag_of_b(dbb, sw):
    gpb = (sw // SSM_GROUP) // SSM_BLOCKS
    d5 = dbb.reshape(SSM_BLOCKS, gpb, SSM_GROUP, gpb, SSM_STATE)
    return jnp.einsum('qghgn->qgnh', d5).reshape(SSM_BLOCKS * gpb * SSM_STATE, SSM_GROUP)


def _diag_of_c(dcc, sw):
    gpb = (sw // SSM_GROUP) // SSM_BLOCKS
    d5 = dcc.reshape(SSM_BLOCKS, gpb, SSM_STATE, gpb, SSM_GROUP)
    return jnp.einsum('qgngh->qghn', d5).reshape(SSM_BLOCKS * gpb, SSM_GROUP, SSM_STATE)


def _ada_fwd(name, c_all, w_sh, b_sh):
    nb, d = c_all.shape
    ncol = w_sh.shape[1]
    tn = _pick(ncol, 768)

    def body(c_ref, w_ref, b_ref, o_ref):
        cv = c_ref[...]
        sil = cv * _sigmoid(cv)
        o_ref[...] = jnp.dot(sil, w_ref[...], preferred_element_type=F32, precision=lax.Precision.HIGHEST) + b_ref[...]

    return _pcall(body, name=name, grid=(ncol // tn,),
                  in_specs=[pl.BlockSpec((nb, d), lambda j: (0, 0)), pl.BlockSpec((d, tn), lambda j: (0, j)),
                            pl.BlockSpec((1, tn), lambda j: (0, j))],
                  out_specs=pl.BlockSpec((nb, tn), lambda j: (0, j)),
                  out_shape=jax.ShapeDtypeStruct((nb, ncol), F32), compiler_params=_params("parallel"))(c_all, w_sh, b_sh)


def _ada_bwd(name, c_all, dmod_sh):
    nb, d = c_all.shape
    ncol = dmod_sh.shape[1]
    tn = _pick(ncol, 768)

    def body(c_ref, g_ref, o_ref):
        cv = c_ref[...]
        sil = cv * _sigmoid(cv)
        o_ref[...] = lax.dot_general(sil, g_ref[...], (((0,), (0,)), ((), ())), preferred_element_type=F32,
                                     precision=lax.Precision.HIGHEST)

    return _pcall(body, name=name, grid=(ncol // tn,),
                  in_specs=[pl.BlockSpec((nb, d), lambda j: (0, 0)), pl.BlockSpec((nb, tn), lambda j: (0, j))],
                  out_specs=pl.BlockSpec((d, tn), lambda j: (0, j)),
                  out_shape=jax.ShapeDtypeStruct((d, ncol), F32), compiler_params=_params("parallel"))(c_all, dmod_sh)


def _place():
    return lax.axis_index("x"), lax.axis_index("y"), lax.axis_index("c")


def _allgather_small(name, blk, deps=()):
    m_per, n = blk.shape

    def body(x_ref, *rest):
        out_ref, send_sems, recv_sems, local_sem = rest[len(deps):]
        x, y, c = _place()
        me, sibling = (x, y, c), (x, y, 1 - c)
        chips = [(1 - x, y), (x, 1 - y), (1 - x, 1 - y)]

        def rows(px, py, pc):
            return out_ref.at[pl.ds((4 * px + 2 * py + pc) * m_per, m_per), :]

        def copy(k, block, to, src=None):
            return pltpu.make_async_remote_copy(
                src_ref=rows(*block) if src is None else src, dst_ref=rows(*block),
                send_sem=send_sems.at[k], recv_sem=recv_sems.at[k], device_id=to, device_id_type=MESH)

        mine = pltpu.make_async_copy(x_ref, rows(*me), local_sem)
        mine.start()
        first = [copy(0, me, sibling, src=x_ref)]
        first += [copy(1 + j, me, (*chip, c), src=x_ref) for j, chip in enumerate(chips)]
        for cp in first:
            cp.start()
        passed = [copy(4 + j, (*chip, c), sibling) for j, chip in enumerate(chips)]
        for j, chip in enumerate(chips):
            copy(1 + j, (*chip, c), me).wait_recv()
            passed[j].start()
        copy(0, sibling, me).wait_recv()
        for j, chip in enumerate(chips):
            copy(4 + j, (*chip, 1 - c), me).wait_recv()
        for cp in first + passed:
            cp.wait_send()
        mine.wait()

    return _pcall(body, name=name, out_shape=jax.ShapeDtypeStruct((N_DEV * m_per, n), blk.dtype),
                  in_specs=[pl.BlockSpec(memory_space=pltpu.VMEM)] + [ANY_SPEC] * len(deps),
                  out_specs=pl.BlockSpec(memory_space=pltpu.VMEM),
                  scratch_shapes=[pltpu.SemaphoreType.DMA((7,)), pltpu.SemaphoreType.DMA((7,)), pltpu.SemaphoreType.DMA],
                  compiler_params=pltpu.CompilerParams(vmem_limit_bytes=VMEM_LIMIT))(blk, *deps)


def _other_chips(x, y):
    return [(1 - x, y), (x, 1 - y), (1 - x, 1 - y)]


def _place_shards(shards, s_me):
    return [lax.dynamic_update_slice(lax.empty((N_CHIPS,) + s.shape, s.dtype), s[None], (s_me, 0, 0)) for s in shards]


def _ag_copy(src, land, send, recv, wi, j, chip, x, y, c, tc, both):
    hr = src.shape[0] // 2
    half = pl.ds(pl.multiple_of(c * hr, 16), hr)
    k = 3 * wi + j
    return pltpu.make_async_remote_copy(
        src_ref=src.at[half, :], dst_ref=land.at[2 * x + y, half, :],
        send_sem=send.at[2 * k + tc if both else k], recv_sem=recv.at[2 * k + c if both else k],
        device_id=(chip[0], chip[1], tc), device_id_type=MESH)


def _ag_targets(c, both):
    return (0, 1) if both else (c,)


def _ag_start(name, shards, lands, groups, direct, deps=()):
    nw, ng, nd = len(shards), len(groups), len(deps)

    def body(*refs):
        src, land = refs[:nw], refs[nw:2 * nw]
        sems = refs[2 * nw + nd:2 * nw + nd + 2 * ng]
        token = refs[-1]
        x, y, c = _place()
        for gi, grp in enumerate(groups):
            for wi, w in enumerate(grp):
                for j, chip in enumerate(_other_chips(x, y)):
                    for tc in _ag_targets(c, direct[gi]):
                        _ag_copy(src[w], land[w], sems[2 * gi], sems[2 * gi + 1], wi, j, chip, x, y, c, tc, direct[gi]).start()
        token[...] = jnp.zeros_like(token)

    sem_shapes = []
    for gi, grp in enumerate(groups):
        sem_shapes += [pltpu.SemaphoreType.DMA(((6 if direct[gi] else 3) * len(grp),))] * 2
    out_shape = sem_shapes + [pltpu.HBM(s.shape, s.dtype) for s in shards] + [pltpu.HBM(l.shape, l.dtype) for l in lands]
    out_shape += [jax.ShapeDtypeStruct((8, 128), F32)]
    res = _pcall(body, name=name, out_shape=out_shape, in_specs=[HBM_SPEC] * (2 * nw) + [ANY_SPEC] * nd,
                 out_specs=[SEM_SPEC] * (2 * ng) + [HBM_SPEC] * (2 * nw) + [pl.BlockSpec(memory_space=pltpu.VMEM)],
                 input_output_aliases={i: 2 * ng + i for i in range(2 * nw)},
                 compiler_params=pltpu.CompilerParams(has_side_effects=EFFECT))(
                     *[_hbm(s) for s in shards], *[_hbm(l) for l in lands], *deps)
    sems = [(res[2 * gi], res[2 * gi + 1]) for gi in range(ng)]
    return sems, list(res[2 * ng:2 * ng + nw]), list(res[2 * ng + nw:2 * ng + 2 * nw]), res[-1]


def _ag_wait(name, shards, lands, send, recv, both, after):
    n = len(shards)

    def body(*refs):
        src, land = refs[:n], refs[n:2 * n]
        send_sem, recv_sem = refs[2 * n], refs[2 * n + 1]
        x, y, c = _place()
        for wi in range(n):
            for j, chip in enumerate(_other_chips(x, y)):
                for tc in _ag_targets(c, both):
                    _ag_copy(src[wi], land[wi], send_sem, recv_sem, wi, j, chip, x, y, c, tc, both).wait_send()
                    _ag_copy(src[wi], land[wi], send_sem, recv_sem, wi, j, chip, chip[0], chip[1], tc, c, both).wait_recv()

    res = _pcall(body, name=name, out_shape=[pltpu.HBM(a.shape, a.dtype) for a in list(shards) + list(lands)],
                 in_specs=[HBM_SPEC] * (2 * n) + [SEM_SPEC, SEM_SPEC] + [ANY_SPEC] * len(after), out_specs=[HBM_SPEC] * (2 * n),
                 input_output_aliases={i: i for i in range(2 * n)},
                 compiler_params=pltpu.CompilerParams(has_side_effects=EFFECT))(*shards, *lands, send, recv, *after)
    return list(res[n:])


def _ag_forward(name, lands):
    n = len(lands)

    def body(*refs):
        out = refs[n:2 * n]
        send, recv = refs[2 * n], refs[2 * n + 1]
        x, y, c = _place()
        sib = (x, y, 1 - c)
        cps = []
        for wi in range(n):
            hr = out[wi].shape[1] // 2
            for j, (cx, cy) in enumerate(_other_chips(x, y)):
                got = out[wi].at[2 * cx + cy, pl.ds(pl.multiple_of(c * hr, 16), hr), :]
                cp = pltpu.make_async_remote_copy(src_ref=got, dst_ref=got, send_sem=send.at[3 * wi + j], recv_sem=recv.at[3 * wi + j],
                                                  device_id=sib, device_id_type=MESH)
                cp.start()
                cps.append(cp)
        for wi in range(n):
            hr = out[wi].shape[1] // 2
            for j, (cx, cy) in enumerate(_other_chips(x, y)):
                got = out[wi].at[2 * cx + cy, pl.ds(pl.multiple_of((1 - c) * hr, 16), hr), :]
                pltpu.make_async_remote_copy(src_ref=got, dst_ref=got, send_sem=send.at[3 * wi + j], recv_sem=recv.at[3 * wi + j],
                                             device_id=sib, device_id_type=MESH).wait_recv()
        for cp in cps:
            cp.wait_send()

    res = _pcall(body, name=name, out_shape=[jax.ShapeDtypeStruct(l.shape, l.dtype) for l in lands],
                 in_specs=[ANY_SPEC] * n, out_specs=[ANY_SPEC] * n, input_output_aliases={i: i for i in range(n)},
                 scratch_shapes=[pltpu.SemaphoreType.DMA((3 * n,)), pltpu.SemaphoreType.DMA((3 * n,))])(*lands)
    return list(res)


def _peers(x, y, c):
    offs = [(dx, dy, dc) for dx in (0, 1) for dy in (0, 1) for dc in (0, 1)][1:]
    return [(1 - x if dx else x, 1 - y if dy else y, 1 - c if dc else c) for dx, dy, dc in offs]


def _rs_copy(g_ref, land_ref, send, recv, wi, k, to, sender):
    hr = g_ref.shape[1] // 2
    return pltpu.make_async_remote_copy(
        src_ref=g_ref.at[2 * to[0] + to[1], pl.ds(pl.multiple_of(to[2] * hr, 16), hr), :], dst_ref=land_ref.at[sender],
        send_sem=send.at[7 * wi + k], recv_sem=recv.at[7 * wi + k], device_id=to, device_id_type=MESH)


def _rs_start(name, gs):
    n = len(gs)
    lands = [lax.empty((N_DEV, g.shape[1] // 2, g.shape[2]), BF16) for g in gs]

    def body(*refs):
        g, land = refs[:n], refs[n:2 * n]
        send, recv = refs[2 * n], refs[2 * n + 1]
        token = refs[-1]
        x, y, c = _place()
        me = 4 * x + 2 * y + c
        for wi in range(n):
            for k, to in enumerate(_peers(x, y, c)):
                _rs_copy(g[wi], land[wi], send, recv, wi, k, to, me).start()
        token[...] = jnp.zeros_like(token)

    out_shape = [pltpu.SemaphoreType.DMA((7 * n,))] * 2 + [pltpu.HBM(a.shape, a.dtype) for a in list(gs) + lands]
    out_shape += [jax.ShapeDtypeStruct((8, 128), F32)]
    res = _pcall(body, name=name, out_shape=out_shape, in_specs=[HBM_SPEC] * (2 * n),
                 out_specs=[SEM_SPEC] * 2 + [HBM_SPEC] * (2 * n) + [pl.BlockSpec(memory_space=pltpu.VMEM)],
                 input_output_aliases={i: 2 + i for i in range(2 * n)},
                 compiler_params=pltpu.CompilerParams(has_side_effects=EFFECT))(
                     *[_hbm(a) for a in gs], *[_hbm(a) for a in lands])
    return res[0], res[1], list(res[2:2 + n]), list(res[2 + n:2 + 2 * n]), res[-1]


def _rs_wait(name, gs, lands, send, recv, after):
    n = len(gs)

    def body(*refs):
        g, land = refs[:n], refs[n:2 * n]
        send_sem, recv_sem = refs[2 * n], refs[2 * n + 1]
        x, y, c = _place()
        me = 4 * x + 2 * y + c
        for wi in range(n):
            for k, to in enumerate(_peers(x, y, c)):
                _rs_copy(g[wi], land[wi], send_sem, recv_sem, wi, k, to, me).wait_send()
                _rs_copy(g[wi], land[wi], send_sem, recv_sem, wi, k, (x, y, c), 4 * to[0] + 2 * to[1] + to[2]).wait_recv()

    res = _pcall(body, name=name, out_shape=[pltpu.HBM(a.shape, a.dtype) for a in list(gs) + list(lands)],
                 in_specs=[HBM_SPEC] * (2 * n) + [SEM_SPEC, SEM_SPEC, ANY_SPEC], out_specs=[HBM_SPEC] * (2 * n),
                 input_output_aliases={i: i for i in range(2 * n)},
                 compiler_params=pltpu.CompilerParams(has_side_effects=EFFECT))(*gs, *lands, send, recv, after)
    return list(res[:n]), list(res[n:])


def _bc_copy(blk_ref, land_ref, send, recv, k, to, slot):
    return pltpu.make_async_remote_copy(src_ref=blk_ref, dst_ref=land_ref.at[slot], send_sem=send.at[k], recv_sem=recv.at[k],
                                        device_id=to, device_id_type=MESH)


def _bcast_start(name, blk):
    land = lax.empty((N_DEV,) + blk.shape, blk.dtype)

    def body(blk_ref, land_ref, send, recv, blk_thru, land_thru, token):
        x, y, c = _place()
        for k, to in enumerate(_peers(x, y, c)):
            _bc_copy(blk_ref, land_ref, send, recv, k, to, 4 * x + 2 * y + c).start()
        token[...] = jnp.zeros_like(token)

    return _pcall(body, name=name,
                  out_shape=[pltpu.SemaphoreType.DMA((7,)), pltpu.SemaphoreType.DMA((7,)), pltpu.HBM(blk.shape, blk.dtype),
                             pltpu.HBM(land.shape, land.dtype), jax.ShapeDtypeStruct((8, 128), F32)],
                  in_specs=[HBM_SPEC, HBM_SPEC], out_specs=[SEM_SPEC, SEM_SPEC, HBM_SPEC, HBM_SPEC, pl.BlockSpec(memory_space=pltpu.VMEM)],
                  input_output_aliases={0: 2, 1: 3}, compiler_params=pltpu.CompilerParams(has_side_effects=EFFECT))(_hbm(blk), _hbm(land))


def _bcast_wait(name, blk, land, send, recv, after):
    def body(blk_ref, land_ref, send_sem, recv_sem, after_ref, blk_thru, land_thru):
        x, y, c = _place()
        for k, to in enumerate(_peers(x, y, c)):
            _bc_copy(blk_ref, land_ref, send_sem, recv_sem, k, to, 4 * x + 2 * y + c).wait_send()
            _bc_copy(blk_ref, land_ref, send_sem, recv_sem, k, to, 4 * to[0] + 2 * to[1] + to[2]).wait_recv()

    return _pcall(body, name=name, out_shape=[pltpu.HBM(blk.shape, blk.dtype), pltpu.HBM(land.shape, land.dtype)],
                  in_specs=[HBM_SPEC, HBM_SPEC, SEM_SPEC, SEM_SPEC, ANY_SPEC], out_specs=[HBM_SPEC, HBM_SPEC],
                  input_output_aliases={0: 0, 1: 1}, compiler_params=pltpu.CompilerParams(has_side_effects=EFFECT))(
                      blk, land, send, recv, after)[1]


def _rs_sum(name, g, land):
    _, rs, cs = g.shape
    hr = rs // 2
    ch = _pick(hr, 64, 16)

    def body(g_ref, land_ref, out_ref, recv, local_sems, sib_sems):
        x, y, c = _place()
        me = 4 * x + 2 * y + c
        cps = [pltpu.make_async_copy(g_ref.at[2 * x + y, pl.ds(pl.multiple_of(c * hr, 16), hr), :], recv.at[me], local_sems.at[7])]
        for k, (tx, ty, tc) in enumerate(_peers(x, y, c)):
            slot = 4 * tx + 2 * ty + tc
            cps.append(pltpu.make_async_copy(land_ref.at[slot], recv.at[slot], local_sems.at[k]))
        for cp in cps:
            cp.start()
        for cp in cps:
            cp.wait()
        base = pl.multiple_of(c * hr, 16)
        for r0 in range(0, hr, ch):
            acc = recv[0, r0:r0 + ch, :].astype(F32)
            for k in range(1, N_DEV):
                acc = acc + recv[k, r0:r0 + ch, :].astype(F32)
            out_ref[pl.ds(base + r0, ch), :] = acc
        half = out_ref.at[pl.ds(base, hr), :]
        sib = pltpu.make_async_remote_copy(src_ref=half, dst_ref=half, send_sem=sib_sems.at[0], recv_sem=sib_sems.at[1],
                                           device_id=(x, y, 1 - c), device_id_type=MESH)
        sib.start()
        other = out_ref.at[pl.ds(pl.multiple_of((1 - c) * hr, 16), hr), :]
        pltpu.make_async_remote_copy(src_ref=other, dst_ref=other, send_sem=sib_sems.at[0], recv_sem=sib_sems.at[1],
                                     device_id=(x, y, 1 - c), device_id_type=MESH).wait_recv()
        sib.wait_send()

    return _pcall(
        body, name=name, out_shape=jax.ShapeDtypeStruct((rs, cs), F32),
        in_specs=[ANY_SPEC, ANY_SPEC], out_specs=pl.BlockSpec(memory_space=pltpu.VMEM),
        scratch_shapes=[pltpu.VMEM((N_DEV, hr, cs), BF16), pltpu.SemaphoreType.DMA((8,)), pltpu.SemaphoreType.DMA((2,))],
        compiler_params=pltpu.CompilerParams(vmem_limit_bytes=VMEM_LIMIT))(g, land)


def _gate_norm_epilogue(factor, x_in, gt, nxt, t, d):
    blk = (512, d)
    full = lambda i, j: (i, j)
    rowv = lambda i, j: (0, j)

    def epi(acc, res, scale, *norm):
        x_new = res + (factor * scale) * acc
        if not norm:
            return x_new, acc
        gv, scv, shv = norm
        r = lax.rsqrt(jnp.mean(x_new * x_new, axis=-1, keepdims=True) + EPS)
        return x_new, acc, x_new * r * gv * (1.0 + scv) + shv

    ins = [(x_in, blk, full), (gt, (1, d), rowv)] + [(v, (1, d), rowv) for v in (nxt or ())]
    outs = [((t, d), F32, blk, full), ((t, d), BF16, blk, full)] + ([((t, d), BF16, blk, full)] if nxt else [])
    return epi, ins, outs


_FULL = lambda i, j: (i, j)
_ROWV = lambda i, j: (0, j)


def _glu_epilogue(b_glu, t, w):
    def epi(acc, bv):
        g = acc + bv
        return acc, g[:, :w] * _sigmoid(g[:, w:])

    return epi, [(b_glu, (1, 2 * w), _ROWV)], [((t, 2 * w), BF16, (512, 2 * w), _FULL), ((t, w), BF16, (512, w), _FULL)]


def _glu_bwd_epilogue(gv, b_glu, t, w):
    def epi(dsg, gvv, bv):
        g = gvv.astype(F32) + bv
        val, s = g[:, :w], _sigmoid(g[:, w:])
        dval, dgate = dsg * s, dsg * val * s * (1.0 - s)
        return [jnp.concatenate([dval, dgate], axis=1)], [jnp.concatenate([_colsum(dval), _colsum(dgate)], axis=1)]

    return epi, [(gv, (512, 2 * w), _FULL), (b_glu, (1, 2 * w), _ROWV)], [((t, 2 * w), BF16, (512, 2 * w), _FULL)], [2 * w]


def _gates_epilogue(y_pool, uin, t, d, pw):
    cb = (2 * pw) // d

    def epi(acc, yp, glp, gls):
        return acc, _sigmoid(glp) * yp.astype(F32) + _sigmoid(gls) * acc

    ins = [(y_pool, (512, d), _FULL), (uin, (512, d), lambda i, j: (i, cb)), (uin, (512, d), lambda i, j: (i, cb + 1))]
    return epi, ins, [((t, d), BF16, (512, d), _FULL)] * 2


def _gates_bwd_epilogue(y_pool, y_ssm, uin, t, d, pw):
    cb = (2 * pw) // d

    def epi(dm, yp, ys, glp, gls):
        sp, ss = _sigmoid(glp), _sigmoid(gls)
        dgl = jnp.concatenate([dm * yp.astype(F32) * sp * (1.0 - sp), dm * ys.astype(F32) * ss * (1.0 - ss)], axis=1)
        return dm * sp, dm * ss, ("at", 2 * pw, dgl)

    ins = [(y_pool, (512, d), _FULL), (y_ssm, (512, d), _FULL),
           (uin, (512, d), lambda i, j: (i, cb)), (uin, (512, d), lambda i, j: (i, cb + 1))]
    wide = 2 * pw + 2 * d
    return epi, ins, [((t, d), BF16, (512, d), _FULL)] * 2 + [((t, wide), BF16, (512, wide), _FULL)]


def _loss_epilogue(x_in, gt, tgt, g_final, t, d):
    blk = (512, d)
    full = lambda i, j: (i, j)
    rowv = lambda i, j: (0, j)

    def epi(acc, res, scale, tv, gv):
        xv = res + (0.5 * scale) * acc
        r = lax.rsqrt(jnp.mean(xv * xv, axis=-1, keepdims=True) + EPS)
        xr = xv * r
        e = xr * gv - tv
        loss_row = 0.5 * jnp.mean(e * e, axis=-1, keepdims=True)
        dout = e * (1.0 / d)
        gd = gv * dout
        dx = r * (gd - xr * jnp.mean(gd * xr, axis=-1, keepdims=True))
        fdx = 0.5 * dx
        return [dx, scale * fdx], [_colsum(dout * xr), _colsum(loss_row * jnp.ones((1, 128), F32)), _colsum(fdx * acc)]

    ins = [(x_in, blk, full), (gt, (1, d), rowv), (tgt, blk, full), (g_final, (1, d), rowv)]
    return epi, ins, [((t, d), F32, blk, full), ((t, d), BF16, blk, full)], [d, 128, d]


def _norm_bwd_epilogue(x, dres, g, sc, up, t, d):
    blk = (512, d)
    full = lambda i, j: (i, j)
    rowv = lambda i, j: (0, j)

    def epi(dhv, xv, drv, gv, scv, *rest):
        r = lax.rsqrt(jnp.mean(xv * xv, axis=-1, keepdims=True) + EPS)
        xr = xv * r
        dn = dhv * (1.0 + scv)
        gd = gv * dn
        dx = drv + r * (gd - xr * jnp.mean(gd * xr, axis=-1, keepdims=True))
        outs, reds = [dx], [_colsum(dhv), _colsum(dhv * xr * gv), _colsum(dn * xr)]
        if up:
            yv, gtv = rest
            fdx = up[2] * dx
            outs.append(gtv * fdx)
            reds.append(_colsum(fdx * yv.astype(F32)))
        return outs, reds

    ins = [(x, blk, full), (dres, blk, full), (g, (1, d), rowv), (sc, (1, d), rowv)]
    ins += [(up[0], blk, full), (up[1], (1, d), rowv)] if up else []
    outs = [((t, d), F32, blk, full)] + ([((t, d), BF16, blk, full)] if up else [])
    return epi, ins, outs, [d] * (4 if up else 3)


def _ffn_in_act(name, h, w_in, tm=512):
    t, d = h.shape
    ns = w_in.shape[2]
    tm = _pick(t, tm, 8)
    w4 = w_in.reshape(2, 2, d, ns)

    def body(h_ref, w_ref, ab_ref, act_ref):
        hv = h_ref[...]
        a = jnp.dot(hv, w_ref[0], preferred_element_type=F32)
        b = jnp.dot(hv, w_ref[1], preferred_element_type=F32)
        ab_ref[0] = a.astype(BF16)
        ab_ref[1] = b.astype(BF16)
        act_ref[...] = (a * _sigmoid(a) * b).astype(BF16)

    return _pcall(body, name=name, grid=(2, t // tm),
                  in_specs=[pl.BlockSpec((tm, d), lambda c, i: (i, 0)), pl.BlockSpec((2, None, d, ns), lambda c, i: (0, c, 0, 0))],
                  out_specs=[pl.BlockSpec((2, tm, ns), lambda c, i: (0, i, c)), pl.BlockSpec((tm, ns), lambda c, i: (i, c))],
                  out_shape=[jax.ShapeDtypeStruct((2, t, 2 * ns), BF16), jax.ShapeDtypeStruct((t, 2 * ns), BF16)],
                  compiler_params=_params("parallel", "parallel"))(h, w4)


def _dswiglu_epilogue(ab, t, f, tn):
    blk = (2, 512, tn)
    idx = lambda i, j: (0, i, j)

    def epi(dact, abv):
        a, b = abv[0].astype(F32), abv[1].astype(F32)
        s = _sigmoid(a)
        return (jnp.stack([dact * b * (s * (1.0 + a * (1.0 - s))), dact * (a * s)]),)

    return epi, [(ab, blk, idx)], [((2, t, f), BF16, blk, idx)]


def _ffn_fwd(tag, x, h, gt, get_w_in, get_w_out, nxt=None, loss=None):
    t, d = x.shape
    ab, act = _ffn_in_act(tag + "_in", h, get_w_in(h))
    if loss:
        epi, epi_ins, epi_outs, epi_reds = _loss_epilogue(x, gt, *loss, t, d)
        res = _mm(tag + "_out", act, get_w_out(act), tm=512, tn=d, epi=epi, epi_ins=epi_ins, epi_outs=epi_outs,
                  epi_reds=epi_reds, b_resident=True)
        return res, (x, h, ab, act, None)
    epi, epi_ins, epi_outs = _gate_norm_epilogue(0.5, x, gt, nxt, t, d)
    res = _mm(tag + "_out", act, get_w_out(act), tm=512, tn=d, epi=epi, epi_ins=epi_ins, epi_outs=epi_outs, b_resident=True)
    return res[0], res[2], (x, h, ab, act, res[1])


def _ffn_bwd(tag, dx_new, dy, saved, g, sc, w_in, w_out, start_rs, up, deps=()):
    x, h, ab, act, _ = saved
    d = x.shape[1]
    f = act.shape[1]
    dw_out = _mm(tag + "_dwout", act, dy, ta=True, tm=1408, tn=d, tk=2048, deps=deps)
    tok = start_rs("out", dw_out.reshape(N_CHIPS, f // N_CHIPS, d))
    epi, epi_ins, epi_outs = _dswiglu_epilogue(ab, x.shape[0], f, w_in.shape[2])
    dab = _mm(tag + "_dact", dy, w_out, tb=True, tm=512, tn=w_in.shape[2], epi=epi, epi_ins=epi_ins, epi_outs=epi_outs,
              deps=(tok,), j_outer=True)
    dw_in = _mm(tag + "_dwin", h, dab, ta=True, out_stacked=True, b_halves=True, tm=d, tn=1408, tk=2048)
    tok = start_rs("in", dw_in)
    epi, epi_ins, epi_outs, epi_reds = _norm_bwd_epilogue(x, dx_new, g, sc, up, x.shape[0], d)
    res = _mm(tag + "_dh", dab, w_in, tb=True, b_stacked=True, a_halves=True, tn=d, tk=5632, deps=(tok,), epi=epi,
              epi_ins=epi_ins, epi_outs=epi_outs, epi_reds=epi_reds, b_resident=True)
    if up:
        return res
    return res[0], None, res[1], res[2], res[3], None


def _row(v):
    return v.reshape(1, -1)


def _pack(parts):
    cols = []
    for p in parts:
        flat = p.reshape(-1).astype(F32)
        padn = (-flat.shape[0]) % 128
        cols.append(jnp.pad(flat, (0, padn)) if padn else flat)
    flat = jnp.concatenate(cols)
    padn = (-flat.shape[0]) % 1024
    if padn:
        flat = jnp.pad(flat, (0, padn))
    return flat.reshape(-1, 128)


def _unpack(packed, shapes):
    flat = packed.reshape(-1)
    out, off = [], 0
    for s in shapes:
        n = math.prod(s)
        out.append(flat[off:off + n].reshape(s))
        off += n + ((-n) % 128)
    return out


SMALL = ['b_ada', 'g_ffn1', 'g_mix', 'pool_w', 'pool_b', 'pool_scale', 'ssm_lam_re_log', 'ssm_lam_im', 'ssm_log_dt',
         'ssm_b_re', 'ssm_b_im', 'ssm_c_re', 'ssm_c_im', 'ssm_d', 'b_glu', 'g_ffn2', 'g_final']
BIG = ['w_ffn1_in', 'w_ffn1_out', 'w_in', 'w_pool_up', 'w_glu', 'w_ssm_up', 'w_out', 'w_ffn2_in', 'w_ffn2_out']
AG_GROUPS = [[0], [1], [2, 3, 4, 5, 6], [7, 8]]
AG_DIRECT = [False, False, False, True]
SMALL_LATE = ['b_ada', 'g_ffn1']
WEIGHTS = ['w_ada', 'b_ada', 'g_ffn1', 'w_ffn1_in', 'w_ffn1_out', 'g_mix', 'w_in', 'pool_w', 'pool_b', 'pool_scale', 'w_pool_up',
           'ssm_lam_re_log', 'ssm_lam_im', 'ssm_log_dt', 'ssm_b_re', 'ssm_b_im', 'ssm_c_re', 'ssm_c_im', 'ssm_d', 'w_glu', 'b_glu',
           'w_ssm_up', 'w_out', 'g_ffn2', 'w_ffn2_in', 'w_ffn2_out', 'g_final']


def kernel(x, c, w_ada, b_ada, g_ffn1, w_ffn1_in, w_ffn1_out, g_mix, w_in, pool_w, pool_b, pool_scale, w_pool_up, ssm_lam_re_log, ssm_lam_im, ssm_log_dt, ssm_b_re, ssm_b_im, ssm_c_re, ssm_c_im, ssm_d, w_glu, b_glu, w_ssm_up, w_out, g_ffn2, w_ffn2_in, w_ffn2_out, g_final, loss_target, m_w_ada, m_b_ada, m_g_ffn1, m_w_ffn1_in, m_w_ffn1_out, m_g_mix, m_w_in, m_pool_w, m_pool_b, m_pool_scale, m_w_pool_up, m_ssm_lam_re_log, m_ssm_lam_im, m_ssm_log_dt, m_ssm_b_re, m_ssm_b_im, m_ssm_c_re, m_ssm_c_im, m_ssm_d, m_w_glu, m_b_glu, m_w_ssm_up, m_w_out, m_g_ffn2, m_w_ffn2_in, m_w_ffn2_out, m_g_final, v_w_ada, v_b_ada, v_g_ffn1, v_w_ffn1_in, v_w_ffn1_out, v_g_mix, v_w_in, v_pool_w, v_pool_b, v_pool_scale, v_w_pool_up, v_ssm_lam_re_log, v_ssm_lam_im, v_ssm_log_dt, v_ssm_b_re, v_ssm_b_im, v_ssm_c_re, v_ssm_c_im, v_ssm_d, v_w_glu, v_b_glu, v_w_ssm_up, v_w_out, v_g_ffn2, v_w_ffn2_in, v_w_ffn2_out, v_g_final):
    args = dict(locals())
    wt = {n: args[n] for n in WEIGHTS}
    mom = {n: args["m_" + n] for n in WEIGHTS}
    var = {n: args["v_" + n] for n in WEIGHTS}

    t, d = x.shape[1], x.shape[2]
    pw = pool_b.shape[1]
    sw = ssm_d.shape[1]
    ngrp = sw // SSM_GROUP
    gn = ngrp * SSM_STATE
    xi, yi, ci = _place()
    b_me = 4 * xi + 2 * yi + ci
    s_me = 2 * xi + yi
    x2d = x[0]
    tgt = loss_target[0]

    c_all = _allgather_small("ag_c", c.reshape(8, d // 8)).reshape(N_DEV, d)
    ncol = w_ada.shape[2]
    b_sh = lax.dynamic_slice(b_ada, (0, s_me * ncol), (1, ncol))
    mod_sh = _ada_fwd("ada_fwd", c_all, w_ada[0], b_sh)
    md_send, md_recv, mod_sh, md_land, tok = _bcast_start("mod_start", mod_sh)

    shards = [wt[n][0].astype(BF16) for n in BIG]
    ag_sems, shards_t, lands_t, tok = _ag_start("ag_start", shards, _place_shards(shards, s_me), AG_GROUPS, AG_DIRECT, deps=(tok,))
    md_land = _bcast_wait("mod_wait", mod_sh, md_land, md_send, md_recv, tok)
    mod_all = lax.dynamic_update_slice(md_land, mod_sh[None], (b_me, 0, 0))
    full = {}

    def weights(gi, *after):
        grp = AG_GROUPS[gi]
        if BIG[grp[0]] not in full:
            ls = _ag_wait("ag_wait%d" % gi, [shards_t[w] for w in grp], [lands_t[w] for w in grp], *ag_sems[gi],
                          AG_DIRECT[gi], after)
            for w, l in zip(grp, ls if AG_DIRECT[gi] else _ag_forward("ag_fwd%d" % gi, ls)):
                full[BIG[w]] = l
        return full

    mod_me = jnp.concatenate([lax.dynamic_slice(mod_all, (2 * s, b_me, 0), (1, 1, ncol))[0] for s in range(N_CHIPS)], axis=1)
    mod = mod_me.reshape(9, d)
    sh1, sc1, gt1, sh2, sc2, gt2, sh3, sc3, gt3 = [mod[k:k + 1] for k in range(9)]

    f = w_ffn1_out.shape[1] * N_CHIPS

    col = lambda a: a.reshape(gn, 1)
    lrl_c, li_c = col(ssm_lam_re_log), col(ssm_lam_im)
    ldt_c = col(jnp.broadcast_to(ssm_log_dt.reshape(ngrp, 1), (ngrp, SSM_STATE)))
    b_re2, b_im2 = ssm_b_re.reshape(gn, SSM_GROUP), ssm_b_im.reshape(gn, SSM_GROUP)
    lrdt_c, ang_c, bb_re, bb_im = _ssm_prep("ssm_prep", lrl_c, li_c, ldt_c, b_re2, b_im2)
    lrdt, ang = lrdt_c.reshape(1, gn), ang_c.reshape(1, gn)
    tr = lambda a: jnp.swapaxes(a, 1, 2)
    bbd = [_blockdiag_b(v, sw).astype(BF16) for v in (bb_re, bb_im)]
    ccd = [_blockdiag_c(v[0], sw).astype(BF16) for v in (ssm_c_re, ssm_c_im)]
    bbd_t, ccd_t = [tr(v) for v in bbd], [tr(v) for v in ccd]
    early = [n for n in SMALL if n not in SMALL_LATE]
    packs = {}
    for tag, names, extra in (("early", early, []), ("late", SMALL_LATE, [jnp.zeros((128,), F32)])):
        packs[tag] = [_pack([src[n] for n in names] + extra) for src in (wt, mom, var)]
    shadow = [lrdt, ang, ldt_c, *bbd, *ccd, *bbd_t, *ccd_t, *packs["early"], *packs["late"]]

    h1 = _norm_fwd("ffn1_norm", x2d, g_ffn1, sc1, sh1, deps=(tok,))
    x1, h2, sav1 = _ffn_fwd("ffn1", x2d, h1, gt1, lambda after: weights(0, after, *shadow)['w_ffn1_in'],
                            lambda after: weights(1, after)['w_ffn1_out'].reshape(f, d), (g_mix, sc2, sh2))
    weights(2, h2)
    wo = full['w_out'].reshape(d, d)
    uin = _mm("mix_in", h2, full['w_in'], b_stacked=True, tm=1024, tn=768, out_dtype=F32, j_outer=True)
    p_pool, z_pool = _pool_fwd("pool_fwd", uin, pool_w[0], pool_b, pool_scale, pw)
    y_pool = _mm("pool_up", p_pool, full['w_pool_up'], b_stacked=True, tm=1024, tn=d)
    y_s, ge, s_re, s_im = _ssm_fwd("ssm_fwd", uin, lrdt, ang, *bbd, *ccd, ssm_d, sw)
    epi, epi_ins, epi_outs = _glu_epilogue(b_glu, t, sw)
    gv, sg = _mm("glu_in", ge, full['w_glu'], b_stacked=True, tm=512, tn=d, epi=epi, epi_ins=epi_ins, epi_outs=epi_outs,
                 b_resident=True)
    epi, epi_ins, epi_outs = _gates_epilogue(y_pool, uin, t, d, pw)
    y_ssm, merged = _mm("ssm_up", sg, full['w_ssm_up'], b_stacked=True, tm=512, tn=d, epi=epi, epi_ins=epi_ins,
                        epi_outs=epi_outs, b_resident=True)
    epi, epi_ins, epi_outs = _gate_norm_epilogue(1.0, x1, gt2, (g_ffn2, sc3, sh3), t, d)
    x2, y2, h3 = _mm("mix_out", merged, wo, tm=512, tn=d, epi=epi, epi_ins=epi_ins, epi_outs=epi_outs, b_resident=True)

    (dx3, dy3, dg_final, loss_v, dgt3), sav3 = _ffn_fwd(
        "ffn2", x2, h3, gt3, lambda after: weights(3, after)['w_ffn2_in'],
        lambda after: weights(3, after)['w_ffn2_out'].reshape(f, d), loss=(tgt, _row(g_final)))

    gs = {}
    rs_open = []

    def start_rs(names, gbs):
        send, recv, g_thru, land_thru, token = _rs_start("rs_start_" + names[0], gbs)
        rs_open.append((names, send, recv, g_thru, land_thru))
        return token

    dx2, dy2, dsh3, dsc3, gs['g_ffn2'], dgt2 = _ffn_bwd(
        "ffn2b", dx3, dy3, sav3, g_ffn2, sc3, full['w_ffn2_in'], full['w_ffn2_out'].reshape(f, d),
        lambda key, g: start_rs(['w_ffn2_' + key], [g]), (y2, gt2, 1.0))

    epi, epi_ins, epi_outs = _gates_bwd_epilogue(y_pool, y_ssm, uin, t, d, pw)
    dy_pool, dy_ssm, duin = _mm("mix_dmerged", dy2, wo, tb=True, tm=512, tn=d, epi=epi, epi_ins=epi_ins, epi_outs=epi_outs,
                                b_resident=True)
    g_wo = _mm("mix_dwout", merged, dy2, ta=True, tm=d, tn=d, tk=2048).reshape(N_CHIPS, d // N_CHIPS, d)

    dp = _mm("pool_dup", dy_pool, full['w_pool_up'], tb=True, b_stacked=True, tm=1024, tn=pw, tk=d)
    g_wpu = _mm("pool_dwup", p_pool, dy_pool, ta=True, out_stacked=True, tm=pw, tn=d, tk=4096)
    duin, gs['pool_w'], gs['pool_b'], gs['pool_scale'] = _pool_bwd("pool_bwd", dp, z_pool, pool_w[0], pool_b, pool_scale, duin)

    g_wsu = _mm("ssm_dwup", sg, dy_ssm, ta=True, out_stacked=True, tm=sw, tn=d, tk=4096)
    tok = start_rs(['w_out', 'w_pool_up', 'w_ssm_up'], [g_wo, g_wpu, g_wsu])
    epi, epi_ins, epi_outs, epi_reds = _glu_bwd_epilogue(gv, b_glu, t, sw)
    dgv, gs['b_glu'] = _mm("ssm_dup", dy_ssm, full['w_ssm_up'], tb=True, b_stacked=True, tm=512, tn=sw, tk=d, deps=(tok,),
                           epi=epi, epi_ins=epi_ins, epi_outs=epi_outs, epi_reds=epi_reds, b_resident=True)
    dge = _mm("glu_dge", dgv, full['w_glu'], tb=True, b_stacked=True, tm=1024, tn=sw, tk=d)
    g_wglu = _mm("glu_dw", ge, dgv, ta=True, out_stacked=True, tm=sw, tn=d, tk=4096)
    (duin, g_abre, g_abim, g_bbd_re, g_bbd_im, g_ccd_re, g_ccd_im, gs['ssm_d']) = _ssm_bwd(
        "ssm_bwd", dge, y_s, uin, s_re, s_im, lrdt, ang, *bbd_t, *ccd_t, ssm_d, sw, duin)
    gs['ssm_c_re'], gs['ssm_c_im'] = _diag_of_c(g_ccd_re, sw), _diag_of_c(g_ccd_im, sw)
    d_lrl, d_li, d_ldt, d_bre, d_bim = _ssm_param_bwd(
        "ssm_param_bwd", lrl_c, li_c, ldt_c, b_re2, b_im2, g_abre.reshape(gn, 1), g_abim.reshape(gn, 1),
        _diag_of_b(g_bbd_re, sw), _diag_of_b(g_bbd_im, sw))
    gs['ssm_lam_re_log'], gs['ssm_lam_im'] = d_lrl, d_li
    gs['ssm_log_dt'] = jnp.sum(d_ldt.reshape(ngrp, SSM_STATE), axis=1)
    gs['ssm_b_re'], gs['ssm_b_im'] = d_bre, d_bim

    g_win =_mm("mix_dwin", h2, duin, ta=True, out_stacked=True, tm=d, tn=768, tk=4096)
    tok = start_rs(['w_glu', 'w_in'], [g_wglu, g_win])
    epi, epi_ins, epi_outs, epi_reds = _norm_bwd_epilogue(x1, dx2, g_mix, sc2, (sav1[4], gt1, 0.5), t, d)
    dx1, dy1, dsh2, dsc2, gs['g_mix'], dgt1 = _mm(
        "mix_dh", duin, full['w_in'], tb=True, b_stacked=True, tn=d, tk=3072, deps=(tok,), epi=epi, epi_ins=epi_ins,
        epi_outs=epi_outs, epi_reds=epi_reds, b_resident=True)

    gs['g_final'] = dg_final
    sg_blk = _pack([gs[n] for n in early])
    sg_send, sg_recv, sg_blk, sg_land, tok = _bcast_start("sg_start", sg_blk)

    dx0, _, dsh1, dsc1, gs['g_ffn1'], _ = _ffn_bwd(
        "ffn1b", dx1, dy1, sav1, g_ffn1, sc1, full['w_ffn1_in'], full['w_ffn1_out'].reshape(f, d),
        lambda key, g: start_rs(['w_ffn1_' + key], [g]), None, deps=(tok,))

    gs['b_ada'] = jnp.concatenate([dsh1, dsc1, dgt1, dsh2, dsc2, dgt2, dsh3, dsc3, dgt3], axis=1)
    lt_blk = _pack([gs[n] for n in SMALL_LATE] + [loss_v])
    lt_send, lt_recv, lt_blk, lt_land, tok = _bcast_start("late_start", lt_blk)

    grads, delta, new_m, new_v = {}, {}, {}, {}

    def small_update(tag, names, blk, land):
        g8 = lax.dynamic_update_slice(land, blk[None], (b_me, 0, 0)).reshape(-1, 128)
        w_pack, m_pack, v_pack = packs[tag]
        res = _adamw_small("adamw_small_" + tag, w_pack, g8, m_pack, v_pack)
        for dst, packed in zip((grads, delta, new_m, new_v), res):
            for n, val in zip(names, _unpack(packed, [wt[n].shape for n in names])):
                dst[n] = val
        return g8

    after = tok
    for k, (names, send, recv, g_thru, land_thru) in enumerate(rs_open):
        g_done, land_done = _rs_wait("rs_wait_" + names[0], g_thru, land_thru, send, recv, after)
        for n, g_own, land in zip(names, g_done, land_done):
            g_sum = _rs_sum("rs_sum_" + n, g_own, land)
            g_out, dl, mn, vn = _adamw("adamw_" + n, wt[n][0], g_sum, mom[n][0], var[n][0])
            grads[n], delta[n], new_m[n], new_v[n] = g_out[None], dl[None], mn[None], vn[None]
            after = dl
        if k == len(rs_open) - 3:
            small_update("early", early, sg_blk, _bcast_wait("sg_wait", sg_blk, sg_land, sg_send, sg_recv, after))

    lt_land = _bcast_wait("late_wait", lt_blk, lt_land, lt_send, lt_recv, after)
    late8 = small_update("late", SMALL_LATE, lt_blk, lt_land).reshape(N_DEV, -1)
    loss = jnp.sum(late8[:, 10 * d])
    dmod_sh = lax.dynamic_slice(late8, (0, s_me * ncol), (N_DEV, ncol))
    g_w_ada = _ada_bwd("ada_bwd", c_all, dmod_sh)
    dl, mn, vn = _adamw("adamw_w_ada", w_ada[0], g_w_ada, m_w_ada[0], v_w_ada[0], with_g=False)
    grads['w_ada'], delta['w_ada'], new_m['w_ada'], new_v['w_ada'] = g_w_ada[None], dl[None], mn[None], vn[None]

    return (loss, dx0[None], *[grads[n] for n in WEIGHTS], *[delta[n] for n in WEIGHTS],
            *[new_m[n] for n in WEIGHTS], *[new_v[n] for n in WEIGHTS])
```

```python
import functools
import math

import jax
import jax.numpy as jnp
from jax import lax
from jax.experimental import pallas as pl
from jax.experimental.pallas import tpu as pltpu

F32 = jnp.float32
BF16 = jnp.bfloat16
MESH = pl.DeviceIdType.MESH

EPS = 1e-6
POOL_WINDOWS = (2, 4, 8, 16)
POOL_HALO = 16
SSM_GROUP = 16
SSM_STATE = 64
SSM_BLOCKS = 4
N_DEV = 8
N_CHIPS = 4
ADAM_LR = 0.001
ADAM_B1 = 0.9
ADAM_B2 = 0.999
ADAM_EPS = 1e-08
ADAM_WD = 0.01
ADAM_STEP = 10
VMEM_LIMIT = 56 * 1024 * 1024


ANY_SPEC = pl.BlockSpec(memory_space=pl.ANY)
HBM_SPEC = pl.BlockSpec(memory_space=pltpu.HBM)
SEM_SPEC = pl.BlockSpec(memory_space=pltpu.SEMAPHORE)
EFFECT = pltpu.SideEffectType.DATAFLOW_SIDE_EFFECTING


def _hbm(a):
    return pltpu.with_memory_space_constraint(a, pltpu.HBM)


def _pcall(body, **kw):
    return pl.pallas_call(body, **kw)


def _params(*sem):
    return pltpu.CompilerParams(dimension_semantics=sem, vmem_limit_bytes=VMEM_LIMIT)


def _pick(n, cap, mult=128):
    if n <= cap:
        return n
    best = None
    for d in range(mult, cap + 1, mult):
        if n % d == 0:
            best = d
    assert best is not None, (n, cap, mult)
    return best


def _sigmoid(v):
    return 1.0 / (1.0 + jnp.exp(-v))


def _rowwise(name, fn, ins, params, outs, reds, tm, deps=()):
    t = ins[0][0].shape[0]
    tm = min(tm, t)
    nb = t // tm
    ni, npar, no, nd = len(ins), len(params), len(outs), len(deps)

    def body(*refs):
        iv = [r[...] for r in refs[:ni]]
        pv = [r[...] for r in refs[ni:ni + npar]]
        o_refs = refs[ni + npar + nd:ni + npar + nd + no]
        r_refs = refs[ni + npar + nd + no:]
        ovals, rvals = fn(iv, pv)
        for o_ref, val in zip(o_refs, ovals):
            off = 0
            if isinstance(val, tuple) and val[0] == "at":
                _, off, val = val
            parts = val if isinstance(val, (list, tuple)) else [val]
            for p in parts:
                o_ref[:, off:off + p.shape[1]] = p.astype(o_ref.dtype)
                off += p.shape[1]
        if r_refs:
            @pl.when(pl.program_id(0) == 0)
            def _():
                for r in r_refs:
                    r[...] = jnp.zeros_like(r)
            for r, val in zip(r_refs, rvals):
                r[...] += val

    in_specs = [pl.BlockSpec((tm, w), functools.partial(lambda i, cb: (i, cb), cb=cb)) for (_, w, cb) in ins]
    in_specs += [pl.BlockSpec(p.shape, lambda i: (0, 0)) for p in params]
    in_specs += [ANY_SPEC] * nd
    out_shape = [jax.ShapeDtypeStruct((t, w), dt) for (w, dt) in outs]
    out_shape += [jax.ShapeDtypeStruct((1, w), F32) for w in reds]
    out_specs = [pl.BlockSpec((tm, w), lambda i: (i, 0)) for (w, _) in outs]
    out_specs += [pl.BlockSpec((1, w), lambda i: (0, 0)) for w in reds]
    res = _pcall(body, name=name, grid=(nb,), in_specs=in_specs, out_specs=out_specs, out_shape=out_shape,
                 compiler_params=_params("arbitrary"))(*[a for a, _, _ in ins], *params, *deps)
    return res


def _colsum(v):
    return jnp.sum(v, axis=0, keepdims=True)


def _mm(name, a, b, *, ta=False, tb=False, b_stacked=False, out_stacked=False, tm=512, tn=1024, tk=2816,
        out_dtype=BF16, epi=None, epi_ins=(), epi_outs=None, epi_reds=(), deps=(), j_outer=False, a_halves=False,
        b_halves=False, b_resident=False):
    if a_halves:
        _, m, kdim = a.shape
        kdim *= 2
    elif ta:
        kdim, m = a.shape
    else:
        m, kdim = a.shape
    ns = None
    if b_stacked:
        ns = b.shape[2]
        n = b.shape[1] if tb else N_CHIPS * ns
        assert kdim == (N_CHIPS * ns if tb else b.shape[1]), (name, a.shape, b.shape)
    else:
        if b_halves:
            n = 2 * b.shape[2]
            assert kdim == b.shape[1] and not tb, (name, a.shape, b.shape)
        else:
            n = b.shape[0] if tb else b.shape[1]
            assert kdim == (b.shape[1] if tb else b.shape[0]), (name, a.shape, b.shape)
        if out_stacked:
            ns = n // N_CHIPS

    def shards(want):
        return max(g for g in (1, 2, 4) if g * ns <= max(want, ns))

    tm = _pick(m, tm, 128 if ta else 8)
    gn = gk = 1
    if (b_stacked and not tb) or out_stacked:
        gn = shards(min(tn, n // 2) if b_halves else tn)
        tn = gn * ns
    else:
        tn = _pick(n, tn)
    if b_stacked and tb:
        gk = shards(tk)
        tk = gk * ns
    else:
        tk = _pick(kdim, tk, 8 if ta else 128)
    nm, nn, nk = m // tm, n // tn, kdim // tk

    def ij(f):
        return (lambda g0, g1, k: f(g1, g0, k)) if j_outer else f

    if a_halves:
        assert b_stacked and tb and gk == N_CHIPS and nk == 1, name
        a_spec = pl.BlockSpec((2, tm, kdim // 2), ij(lambda i, j, k: (0, i, 0)))
    elif ta:
        a_spec = pl.BlockSpec((tk, tm), ij(lambda i, j, k: (k, i)))
    else:
        a_spec = pl.BlockSpec((tm, tk), ij(lambda i, j, k: (i, k)))
    if b_stacked and not tb:
        b_spec = pl.BlockSpec((gn, tk, ns), ij(lambda i, j, k: (j, k, 0)))
    elif b_stacked and tb:
        b_spec = pl.BlockSpec((gk, tn, ns), ij(lambda i, j, k: (k, j, 0)))
    elif b_halves:
        bph = (n // 2) // tn
        b_spec = pl.BlockSpec((None, tk, tn), ij(lambda i, j, k: (j // bph, k, j % bph)))
    elif tb:
        b_spec = pl.BlockSpec((tn, tk), ij(lambda i, j, k: (j, k)))
    else:
        b_spec = pl.BlockSpec((tk, tn), ij(lambda i, j, k: (k, j)))
    dims = (((0 if ta else 1,), (1 if tb else 0,)), ((), ()))

    if epi_outs is None:
        if out_stacked:
            epi_outs = [((N_CHIPS, m, ns), out_dtype, (gn, tm, ns), lambda i, j: (j, i, 0))]
        else:
            epi_outs = [((m, n), out_dtype, (tm, tn), lambda i, j: (i, j))]
    ne, no, nd, nr = len(epi_ins), len(epi_outs), len(deps), len(epi_reds)
    assert not nr or (nn == 1 and not j_outer), name

    def body(a_ref, b_ref, *rest):
        e_refs = rest[:ne]
        o_refs = rest[ne + nd:ne + nd + no]
        r_refs = rest[ne + nd + no:ne + nd + no + nr]
        scratch = rest[ne + nd + no + nr:]
        av = None if a_halves else a_ref[...].astype(BF16)
        if b_stacked and not tb:
            parts = [lax.dot_general(av, b_ref[s].astype(BF16), dims, preferred_element_type=F32) for s in range(gn)]
        elif b_stacked and tb:
            p = None
            for s in range(gk):
                if a_halves:
                    a_s = a_ref[s // 2, :, (s % 2) * ns:(s % 2 + 1) * ns].astype(BF16)
                else:
                    a_s = av[:, s * ns:(s + 1) * ns]
                q = lax.dot_general(a_s, b_ref[s].astype(BF16), dims, preferred_element_type=F32)
                p = q if p is None else p + q
            parts = [p]
        else:
            parts = [lax.dot_general(av, b_ref[...].astype(BF16), dims, preferred_element_type=F32)]

        def finish(acc_parts):
            if epi is None and out_stacked:
                acc = acc_parts[0]
                for s in range(gn):
                    o_refs[0][s] = acc[:, s * ns:(s + 1) * ns].astype(out_dtype)
            elif epi is None:
                w = acc_parts[0].shape[1]
                for s, part in enumerate(acc_parts):
                    o_refs[0][:, s * w:(s + 1) * w] = part.astype(out_dtype)
            else:
                acc = acc_parts[0] if len(acc_parts) == 1 else jnp.concatenate(acc_parts, axis=1)
                vals = epi(acc, *[r[...] for r in e_refs])
                if nr:
                    vals, reds = vals

                    @pl.when(pl.program_id(0) == 0)
                    def _():
                        for r in r_refs:
                            r[...] = jnp.zeros_like(r)
                    for r, val in zip(r_refs, reds):
                        r[...] += val
                for o_ref, val in zip(o_refs, vals):
                    if isinstance(val, tuple) and val[0] == "at":
                        o_ref[:, val[1]:val[1] + val[2].shape[1]] = val[2].astype(o_ref.dtype)
                    else:
                        o_ref[...] = val.astype(o_ref.dtype)

        if nk == 1:
            finish(parts)
        else:
            acc_ref = scratch[0]
            k = pl.program_id(2)
            w = parts[0].shape[1]

            @pl.when(k == 0)
            def _():
                for s, part in enumerate(parts):
                    acc_ref[:, s * w:(s + 1) * w] = part

            @pl.when(k > 0)
            def _():
                for s, part in enumerate(parts):
                    acc_ref[:, s * w:(s + 1) * w] += part

            @pl.when(k == nk - 1)
            def _():
                finish([acc_ref[...]])

    def _ij(f):
        return ij(lambda i, j, k: f(i, j))

    if b_resident:
        assert nk == 1 and (nn == 1 or j_outer), name
        b_spec = pl.BlockSpec(b_spec.block_shape, b_spec.index_map, pipeline_mode=pl.Buffered(1))
    in_specs = [a_spec, b_spec] + [pl.BlockSpec(blk, _ij(f)) for (_, blk, f) in epi_ins] + [ANY_SPEC] * nd
    out_specs = [pl.BlockSpec(blk, _ij(f)) for (_, _, blk, f) in epi_outs]
    out_specs += [pl.BlockSpec((1, w), lambda *_: (0, 0)) for w in epi_reds]
    out_shape = [jax.ShapeDtypeStruct(s, dt) for (s, dt, _, _) in epi_outs] + [jax.ShapeDtypeStruct((1, w), F32) for w in epi_reds]
    scratch = [pltpu.VMEM((tm, tn), F32)] if nk > 1 else []
    grid = (nn, nm, nk) if j_outer else (nm, nn, nk)
    res = _pcall(body, name=name, grid=grid, in_specs=in_specs, out_specs=out_specs, out_shape=out_shape, scratch_shapes=scratch,
                 compiler_params=_params(*(("arbitrary",) * 3 if nr else ("parallel", "parallel", "arbitrary"))))(
                     a, b, *[x for x, _, _ in epi_ins], *deps)
    return res[0] if len(res) == 1 else res


def _norm_fwd(name, x, g, sc, sh, deps=()):
    d = x.shape[1]

    def fn(iv, pv):
        (xv,), (gv, scv, shv) = iv, pv
        r = lax.rsqrt(jnp.mean(xv * xv, axis=-1, keepdims=True) + EPS)
        return [xv * r * gv * (1.0 + scv) + shv], []

    return _rowwise(name, fn, [(x, d, 0)], [g, sc, sh], [(d, BF16)], [], 512, deps=deps)[0]


def _norm_bwd(name, dh, x, dres, g, sc, y_up, gt_up, factor_up):
    d = x.shape[1]
    up = y_up is not None

    def fn(iv, pv):
        dhv, xv, drv = iv[:3]
        gv, scv = pv[:2]
        dhv = dhv.astype(F32)
        r = lax.rsqrt(jnp.mean(xv * xv, axis=-1, keepdims=True) + EPS)
        xr = xv * r
        dn = dhv * (1.0 + scv)
        gd = gv * dn
        dx = drv + r * (gd - xr * jnp.mean(gd * xr, axis=-1, keepdims=True))
        outs, reds = [dx], [_colsum(dhv), _colsum(dhv * xr * gv), _colsum(dn * xr)]
        if up:
            fdx = factor_up * dx
            outs.append(pv[2] * fdx)
            reds.append(_colsum(fdx * iv[3].astype(F32)))
        return outs, reds

    ins = [(dh, d, 0), (x, d, 0), (dres, d, 0)] + ([(y_up, d, 0)] if up else [])
    res = _rowwise(name, fn, ins, [g, sc] + ([gt_up] if up else []), [(d, F32)] + ([(d, BF16)] if up else []),
                   [d] * (4 if up else 3), 256)
    if up:
        dx, dy_up, dsh, dsc, dg, dgt_up = res
        return dx, dy_up, dsh, dsc, dg, dgt_up
    dx, dsh, dsc, dg = res
    return dx, None, dsh, dsc, dg, None


def _loss_bwd(name, x, tgt, g, y_up, gt_up, factor_up):
    d = x.shape[1]

    def fn(iv, pv):
        (xv, tv, yv), (gv, gtv) = iv, pv
        r = lax.rsqrt(jnp.mean(xv * xv, axis=-1, keepdims=True) + EPS)
        xr = xv * r
        e = xr * gv - tv
        loss_row = 0.5 * jnp.mean(e * e, axis=-1, keepdims=True)
        dout = e * (1.0 / d)
        gd = gv * dout
        dx = r * (gd - xr * jnp.mean(gd * xr, axis=-1, keepdims=True))
        fdx = factor_up * dx
        return [dx, gtv * fdx], [_colsum(dout * xr), _colsum(loss_row * jnp.ones((1, 128), F32)), _colsum(fdx * yv.astype(F32))]

    return _rowwise(name, fn, [(x, d, 0), (tgt, d, 0), (y_up, d, 0)], [g, gt_up], [(d, F32), (d, BF16)], [d, 128, d], 256)


def _resgate_bwd(name, dx, y, gt, factor, deps=()):
    d = dx.shape[1]

    def fn(iv, pv):
        (dxv, yv), (gtv,) = iv, pv
        return [factor * gtv * dxv], [_colsum(factor * dxv * yv.astype(F32))]

    return _rowwise(name, fn, [(dx, d, 0), (y, d, 0)], [gt], [(d, BF16)], [d], 512, deps=deps)


def _swiglu_fwd(name, ab):
    f = ab.shape[1] // 2

    def fn(iv, pv):
        a, b = iv[0].astype(F32), iv[1].astype(F32)
        return [a * _sigmoid(a) * b], []

    return _rowwise(name, fn, [(ab, f, 0), (ab, f, 1)], [], [(f, BF16)], [], 256)[0]


def _swiglu_bwd(name, dact, ab, deps=()):
    f = ab.shape[1] // 2

    def fn(iv, pv):
        dv, a, b = iv[0].astype(F32), iv[1].astype(F32), iv[2].astype(F32)
        s = _sigmoid(a)
        return [[dv * b * (s * (1.0 + a * (1.0 - s))), dv * (a * s)]], []

    return _rowwise(name, fn, [(dact, f, 0), (ab, f, 0), (ab, f, 1)], [], [(2 * f, BF16)], [], 256, deps=deps)[0]


def _gates_fwd(name, y_pool, y_ssm, uin, d, pw):
    cb = (2 * pw) // d

    def fn(iv, pv):
        yp, ys, glp, gls = [v.astype(F32) for v in iv]
        return [_sigmoid(glp) * yp + _sigmoid(gls) * ys], []

    return _rowwise(name, fn, [(y_pool, d, 0), (y_ssm, d, 0), (uin, d, cb), (uin, d, cb + 1)], [], [(d, BF16)], [], 256)[0]


def _gates_bwd(name, dm, y_pool, y_ssm, uin, d, pw):
    cb = (2 * pw) // d

    def fn(iv, pv):
        dmv, yp, ys, glp, gls = [v.astype(F32) for v in iv]
        sp, ss = _sigmoid(glp), _sigmoid(gls)
        return [dmv * sp, dmv * ss, ("at", 2 * pw, [dmv * yp * sp * (1.0 - sp), dmv * ys * ss * (1.0 - ss)])], []

    return _rowwise(name, fn, [(dm, d, 0), (y_pool, d, 0), (y_ssm, d, 0), (uin, d, cb), (uin, d, cb + 1)], [],
                    [(d, BF16), (d, BF16), (2 * pw + 2 * d, BF16)], [], 256)


def _glu_fwd(name, gv, b_glu):
    w = gv.shape[1] // 2

    def fn(iv, pv):
        gvv, (bv,) = iv[0].astype(F32) + pv[0], pv
        return [gvv[:, :w] * _sigmoid(gvv[:, w:])], []

    return _rowwise(name, fn, [(gv, 2 * w, 0)], [b_glu], [(w, BF16)], [], 512)[0]


def _glu_bwd(name, dsg, gv, b_glu, deps=()):
    w = gv.shape[1] // 2

    def fn(iv, pv):
        dv = iv[0].astype(F32)
        gvv = iv[1].astype(F32) + pv[0]
        val, s = gvv[:, :w], _sigmoid(gvv[:, w:])
        dval = dv * s
        dgate = dv * val * s * (1.0 - s)
        return [[dval, dgate]], [jnp.concatenate([_colsum(dval), _colsum(dgate)], axis=1)]

    return _rowwise(name, fn, [(dsg, w, 0), (gv, 2 * w, 0)], [b_glu], [(2 * w, BF16)], [2 * w], 512, deps=deps)


def _adamw(name, w, g, m, v, tm=256, with_g=True):
    c = w.shape[1]

    def fn(iv, pv):
        wv, gv, mv, vv = iv
        mn = ADAM_B1 * mv + (1.0 - ADAM_B1) * gv
        vn = ADAM_B2 * vv + (1.0 - ADAM_B2) * (gv * gv)
        m_hat = mn / (1.0 - ADAM_B1 ** ADAM_STEP)
        v_hat = vn / (1.0 - ADAM_B2 ** ADAM_STEP)
        delta = -ADAM_LR * (m_hat / (jnp.sqrt(v_hat) + ADAM_EPS) + ADAM_WD * wv)
        return ([gv] if with_g else []) + [delta, mn, vn], []

    return _rowwise(name, fn, [(w, c, 0), (g, c, 0), (m, c, 0), (v, c, 0)], [], [(c, F32)] * (4 if with_g else 3), [],
                    _pick(w.shape[0], tm, 8))


def _unpack_plan(shape):
    n = math.prod(shape)
    if len(shape) == 2 and shape[0] == 1 and n % 128 == 0:
        return [((slice(None), slice(128 * r, 128 * (r + 1))), slice(r, r + 1), slice(None)) for r in range(n // 128)]
    if len(shape) == 2 and shape[0] == 1 and n < 128:
        return [((slice(None), slice(None)), slice(0, 1), slice(0, n))]
    if len(shape) == 1 and n % 128 == 0:
        return [((slice(128 * r, 128 * (r + 1)),), r, slice(None)) for r in range(n // 128)]
    if len(shape) == 3 and shape[0] == 1 and shape[2] == 64:
        return [((0, slice(2 * r + h, 2 * r + h + 1), slice(None)), slice(r, r + 1), slice(64 * h, 64 * (h + 1)))
                for r in range(n // 128) for h in range(2)]
    if len(shape) == 4 and shape[0] == 1 and shape[2:] == (128, 128):
        return [((0, k), slice(128 * k, 128 * (k + 1)), slice(None)) for k in range(shape[1])]
    return None


def _adamw_small(name, w, g8, m, v, shapes):
    r = w.shape[0]
    plans, rows0, off = [], [], 0
    for s in shapes:
        plans.append(_unpack_plan(s))
        rows0.append(off // 128)
        off += math.prod(s) + (-math.prod(s)) % 128
    direct = [i for i, p in enumerate(plans) if p is not None]

    def body(w_ref, g_ref, m_ref, v_ref, *out):
        packed = out[4 * len(direct):]
        gv = g_ref[0:r, :]
        for k in range(1, N_DEV):
            gv = gv + g_ref[k * r:(k + 1) * r, :]
        mn = ADAM_B1 * m_ref[...] + (1.0 - ADAM_B1) * gv
        vn = ADAM_B2 * v_ref[...] + (1.0 - ADAM_B2) * (gv * gv)
        m_hat = mn / (1.0 - ADAM_B1 ** ADAM_STEP)
        v_hat = vn / (1.0 - ADAM_B2 ** ADAM_STEP)
        packed[0][...] = gv
        packed[1][...] = -ADAM_LR * (m_hat / (jnp.sqrt(v_hat) + ADAM_EPS) + ADAM_WD * w_ref[...])
        packed[2][...] = mn
        packed[3][...] = vn
        for di, i in enumerate(direct):
            for kind in range(4):
                o_ref, src = out[4 * di + kind], packed[kind]
                for o_idx, row, lanes in plans[i]:
                    row = (rows0[i] + row) if isinstance(row, int) else slice(rows0[i] + row.start, rows0[i] + row.stop)
                    o_ref[o_idx] = src[row, lanes]

    out_shape = [jax.ShapeDtypeStruct(shapes[i], F32) for i in direct for _ in range(4)]
    out_shape += [jax.ShapeDtypeStruct((r, 128), F32)] * 4
    res = _pcall(body, name=name, out_shape=out_shape,
                 compiler_params=pltpu.CompilerParams(vmem_limit_bytes=VMEM_LIMIT))(w, g8, m, v)
    per_param = [None] * len(shapes)
    for di, i in enumerate(direct):
        per_param[i] = res[4 * di:4 * di + 4]
    return per_param, res[4 * len(direct):]


def _pool_fwd(name, uin, pool_w, pool_b, pool_scale, pw, tm=512):
    t = uin.shape[0]
    tm = min(tm, t)
    ng = len(POOL_WINDOWS)
    gw = pw // ng

    def body(u_ref, w_ref, b_ref, s_ref, p_ref, z_ref, ext):
        i = pl.program_id(0)

        @pl.when(i == 0)
        def _():
            ext[0:POOL_HALO, :] = jnp.zeros((POOL_HALO, pw), F32)

        u = u_ref[...]
        ext[POOL_HALO:POOL_HALO + tm, :] = u
        pos = i * tm + lax.broadcasted_iota(jnp.int32, (tm, 1), 0)
        for k, win in enumerate(POOL_WINDOWS):
            cols = slice(k * gw, (k + 1) * gw)
            acc = u[:, cols]
            for j in range(1, win):
                acc = acc + ext[POOL_HALO - j:POOL_HALO - j + tm, cols]
            cnt = jnp.minimum(pos + 1, win).astype(F32)
            z = acc / cnt - u[:, cols]
            zp = jnp.dot(z.astype(BF16), w_ref[k].astype(BF16), preferred_element_type=F32) + b_ref[:, cols]
            p_ref[:, cols] = (zp * s_ref[:, cols]).astype(BF16)
            z_ref[:, cols] = z.astype(BF16)
        ext[0:POOL_HALO, :] = u[tm - POOL_HALO:tm, :]

    return _pcall(
        body, name=name, grid=(t // tm,),
        in_specs=[pl.BlockSpec((tm, pw), lambda i: (i, 0)), pl.BlockSpec(pool_w.shape, lambda i: (0, 0, 0)),
                  pl.BlockSpec(pool_b.shape, lambda i: (0, 0)), pl.BlockSpec(pool_scale.shape, lambda i: (0, 0))],
        out_specs=[pl.BlockSpec((tm, pw), lambda i: (i, 0))] * 2,
        out_shape=[jax.ShapeDtypeStruct((t, pw), BF16)] * 2,
        scratch_shapes=[pltpu.VMEM((POOL_HALO + tm, pw), F32)],
        compiler_params=_params("arbitrary"))(uin, pool_w, pool_b, pool_scale)


def _pool_bwd(name, dp, z, pool_w, pool_b, pool_scale, duin, tm=512):
    t, pw = z.shape
    tm = min(tm, t)
    nb = t // tm
    ng = len(POOL_WINDOWS)
    gw = pw // ng

    def body(dp_ref, z_ref, w_ref, b_ref, s_ref, duin_ref, du_ref, dw_ref, db_ref, ds_ref, ext):
        i = pl.program_id(0)

        @pl.when(i == 0)
        def _():
            ext[tm:tm + POOL_HALO, :] = jnp.zeros((POOL_HALO, pw), F32)
            dw_ref[...] = jnp.zeros_like(dw_ref)
            db_ref[...] = jnp.zeros_like(db_ref)
            ds_ref[...] = jnp.zeros_like(ds_ref)

        pos = (nb - 1 - i) * tm + lax.broadcasted_iota(jnp.int32, (tm, 1), 0)
        for k, win in enumerate(POOL_WINDOWS):
            cols = slice(k * gw, (k + 1) * gw)
            zk = z_ref[:, cols]
            dpk = dp_ref[:, cols].astype(F32)
            wk = w_ref[k].astype(BF16)
            zp = jnp.dot(zk, wk, preferred_element_type=F32) + b_ref[:, cols]
            ds_ref[:, cols] += _colsum(dpk * zp)
            dzp = dpk * s_ref[:, cols]
            db_ref[:, cols] += _colsum(dzp)
            dzpb = dzp.astype(BF16)
            dz = lax.dot_general(dzpb, wk, (((1,), (1,)), ((), ())), preferred_element_type=F32)
            dw_ref[k] += lax.dot_general(zk, dzpb, (((0,), (0,)), ((), ())), preferred_element_type=F32)
            cnt = jnp.minimum(pos + 1, win).astype(F32)
            r = dz / cnt
            ext[0:tm, cols] = r
            acc = r - dz
            for j in range(1, win):
                acc = acc + ext[j:j + tm, cols]
            du_ref[:, cols] = acc.astype(BF16)
        ext[tm:tm + POOL_HALO, :] = ext[0:POOL_HALO, :]

    rev = lambda i: (nb - 1 - i, 0)
    return _pcall(
        body, name=name, grid=(nb,),
        in_specs=[pl.BlockSpec((tm, pw), rev), pl.BlockSpec((tm, pw), rev), pl.BlockSpec(pool_w.shape, lambda i: (0, 0, 0)),
                  pl.BlockSpec(pool_b.shape, lambda i: (0, 0)), pl.BlockSpec(pool_scale.shape, lambda i: (0, 0)), ANY_SPEC],
        out_specs=[pl.BlockSpec((tm, pw), rev), pl.BlockSpec(pool_w.shape, lambda i: (0, 0, 0)),
                   pl.BlockSpec((1, pw), lambda i: (0, 0)), pl.BlockSpec((1, pw), lambda i: (0, 0))],
        out_shape=[jax.ShapeDtypeStruct(duin.shape, BF16), jax.ShapeDtypeStruct(pool_w.shape, F32),
                   jax.ShapeDtypeStruct((1, pw), F32), jax.ShapeDtypeStruct((1, pw), F32)],
        scratch_shapes=[pltpu.VMEM((tm + POOL_HALO, pw), F32)], input_output_aliases={5: 0},
        compiler_params=_params("arbitrary"))(dp, z, pool_w, pool_b, pool_scale, duin)


def _ssm_disc(lrl, li, ldt):
    lr = -jnp.exp(lrl)
    dt = jnp.exp(ldt)
    mag = jnp.exp(lr * dt)
    ang = li * dt
    ab_re = mag * jnp.cos(ang)
    ab_im = mag * jnp.sin(ang)
    num_re = ab_re - 1.0
    num_im = ab_im
    den = lr * lr + li * li
    f_re = (num_re * lr + num_im * li) / den
    f_im = (num_im * lr - num_re * li) / den
    return lr, dt, mag, ang, ab_re, ab_im, num_re, num_im, den, f_re, f_im


def _ssm_prep(name, lrl, li, ldt, b_re, b_im):
    gn, h = b_re.shape

    def body(lrl_ref, li_ref, ldt_ref, br_ref, bi_ref, lrdt_ref, ang_ref, bbr_ref, bbi_ref):
        lr, dt, _, ang, _, _, _, _, _, f_re, f_im = _ssm_disc(lrl_ref[...], li_ref[...], ldt_ref[...])
        lrdt_ref[...] = lr * dt
        ang_ref[...] = ang
        br, bi = br_ref[...], bi_ref[...]
        bbr_ref[...] = f_re * br - f_im * bi
        bbi_ref[...] = f_re * bi + f_im * br

    col = jax.ShapeDtypeStruct((gn, 1), F32)
    mat = jax.ShapeDtypeStruct((gn, h), F32)
    return _pcall(body, name=name, out_shape=[col, col, mat, mat])(lrl, li, ldt, b_re, b_im)


def _ssm_param_bwd(name, lrl, li, ldt, b_re, b_im, g_abre, g_abim, g_bbre, g_bbim):
    gn, h = b_re.shape

    def body(lrl_ref, li_ref, ldt_ref, br_ref, bi_ref, gar_ref, gai_ref, gbr_ref, gbi_ref,
             dlrl_ref, dli_ref, dldt_ref, dbr_ref, dbi_ref):
        li_v = li_ref[...]
        lr, dt, mag, ang, ab_re, ab_im, num_re, num_im, den, f_re, f_im = _ssm_disc(lrl_ref[...], li_v, ldt_ref[...])
        br, bi = br_ref[...], bi_ref[...]
        gbr, gbi = gbr_ref[...], gbi_ref[...]
        g_fre = jnp.sum(gbr * br + gbi * bi, axis=1, keepdims=True)
        g_fim = jnp.sum(gbi * br - gbr * bi, axis=1, keepdims=True)
        dbr_ref[...] = gbr * f_re + gbi * f_im
        dbi_ref[...] = gbi * f_re - gbr * f_im
        g_num_re = (g_fre * lr - g_fim * li_v) / den
        g_num_im = (g_fre * li_v + g_fim * lr) / den
        g_den = -(g_fre * f_re + g_fim * f_im) / den
        g_lr = (g_fre * num_re + g_fim * num_im) / den + g_den * 2.0 * lr
        g_li = (g_fre * num_im - g_fim * num_re) / den + g_den * 2.0 * li_v
        g_are = gar_ref[...] + g_num_re
        g_aim = gai_ref[...] + g_num_im
        g_mag = g_are * jnp.cos(ang) + g_aim * jnp.sin(ang)
        g_ang = g_aim * ab_re - g_are * ab_im
        g_lrdt = g_mag * mag
        g_lr = g_lr + g_lrdt * dt
        g_dt = g_lrdt * lr + g_ang * li_v
        g_li = g_li + g_ang * dt
        dlrl_ref[...] = g_lr * lr
        dli_ref[...] = g_li
        dldt_ref[...] = g_dt * dt

    col = jax.ShapeDtypeStruct((gn, 1), F32)
    mat = jax.ShapeDtypeStruct((gn, h), F32)
    return _pcall(body, name=name, out_shape=[col, col, col, mat, mat])(lrl, li, ldt, b_re, b_im, g_abre, g_abim, g_bbre, g_bbim)


def _pow_rows(lrdt, ang, k):
    mag = jnp.exp(k * lrdt)
    return mag * jnp.cos(k * ang), mag * jnp.sin(k * ang)


def _ssm_chunk(t):
    return 256 if t >= 2048 else 128


def _to_segments(dst, srcs, jn):
    for q, src in enumerate(srcs):
        for j in range(jn):
            dst[8 * j:8 * j + 8, 128 * q:128 * (q + 1)] = src[pl.ds(j, 8, stride=jn), :]


def _from_segments(dst, src, q, jn, dtype):
    for s in range(8):
        dst[s * jn:(s + 1) * jn, 128 * q:128 * (q + 1)] = src[q, pl.ds(s, jn, stride=8), :].astype(dtype)


def _fill_rows8(dst_re, dst_im, v_re, v_im):
    for j in range(v_re.shape[0]):
        dst_re[8 * j:8 * j + 8, :] = jnp.broadcast_to(v_re[j:j + 1, :], (8, v_re.shape[1]))
        dst_im[8 * j:8 * j + 8, :] = jnp.broadcast_to(v_im[j:j + 1, :], (8, v_im.shape[1]))


def _cmul(ar, ai, br, bi):
    return ar * br - ai * bi, ar * bi + ai * br


def _cmul_conj(ar, ai, br, bi):
    return ar * br + ai * bi, ar * bi - ai * br


def _ssm_fwd(name, uin, lrdt, ang, bb_re, bb_im, cc_re, cc_im, d_skip, sw):
    t = uin.shape[0]
    gn = lrdt.shape[1]
    lc = _ssm_chunk(t)
    jn = lc // 8
    ub, sb = sw // SSM_BLOCKS, gn // SSM_BLOCKS
    nq = sw // 128
    assert ub == 128 and nq == SSM_BLOCKS

    def body(*refs):
        u_refs = refs[:nq]
        (lrdt_ref, ang_ref, bbr_ref, bbi_ref, ccr_ref, cci_ref, d_ref, y_ref, ge_ref, sre_ref, sim_ref,
         p_re, p_im, a_re, a_im, up, yp, cst_re, cst_im, car_re, car_im) = refs[nq:]
        i = pl.program_id(0)
        lrdt_v, ang_v = lrdt_ref[...], ang_ref[...]

        @pl.when(i == 0)
        def _():
            k = (lax.broadcasted_iota(jnp.int32, (jn, 1), 0) + 1).astype(F32)
            _fill_rows8(p_re, p_im, *_pow_rows(lrdt_v, ang_v, k))
            car_re[...] = jnp.zeros_like(car_re)
            car_im[...] = jnp.zeros_like(car_im)

        _to_segments(up, u_refs, jn)
        u = up[...]
        ubf = u.astype(BF16)
        for q in range(SSM_BLOCKS):
            uq = ubf[:, q * ub:(q + 1) * ub]
            a_re[:, q * sb:(q + 1) * sb] = jnp.dot(uq, bbr_ref[q], preferred_element_type=F32)
            a_im[:, q * sb:(q + 1) * sb] = jnp.dot(uq, bbi_ref[q], preferred_element_type=F32)
        a1r, a1i = _pow_rows(lrdt_v, ang_v, 1.0)
        ajr, aji = _pow_rows(lrdt_v, ang_v, float(jn))
        for q in range(SSM_BLOCKS):
            cols = slice(q * sb, (q + 1) * sb)
            ar8 = jnp.broadcast_to(a1r[:, cols], (8, sb))
            ai8 = jnp.broadcast_to(a1i[:, cols], (8, sb))

            def step(j, carry, cols=cols, ar8=ar8, ai8=ai8):
                sr, si = carry
                rows = pl.ds(pl.multiple_of(j * 8, 8), 8)
                mr, mi = _cmul(ar8, ai8, sr, si)
                nr, ni = mr + a_re[rows, cols], mi + a_im[rows, cols]
                a_re[rows, cols] = nr
                a_im[rows, cols] = ni
                return nr, ni

            lax.fori_loop(1, jn, step, (a_re[0:8, cols], a_im[0:8, cols]), unroll=4)
        er, ei = a_re[lc - 8:lc, :], a_im[lc - 8:lc, :]
        gr, gi = car_re[...], car_im[...]
        for s in range(8):
            cst_re[s:s + 1, :] = gr
            cst_im[s:s + 1, :] = gi
            mr, mi = _cmul(ajr, aji, gr, gi)
            gr, gi = mr + er[s:s + 1, :], mi + ei[s:s + 1, :]
        car_re[...] = gr
        car_im[...] = gi
        for q in range(SSM_BLOCKS):
            cols = slice(q * sb, (q + 1) * sb)
            cr = jnp.tile(cst_re[:, cols], (jn, 1))
            ci = jnp.tile(cst_im[:, cols], (jn, 1))
            mr, mi = _cmul(p_re[:, cols], p_im[:, cols], cr, ci)
            srb, sib = (a_re[:, cols] + mr).astype(BF16), (a_im[:, cols] + mi).astype(BF16)
            sre_ref[:, cols] = srb
            sim_ref[:, cols] = sib
            ycols = slice(q * ub, (q + 1) * ub)
            y = (jnp.dot(srb, ccr_ref[q], preferred_element_type=F32) - jnp.dot(sib, cci_ref[q], preferred_element_type=F32)
                 + d_ref[:, ycols] * u[:, ycols])
            yp[q] = y
            _from_segments(y_ref, yp, q, jn, F32)
            yt = y_ref[:, ycols]
            ge_ref[:, ycols] = (0.5 * yt * (1.0 + lax.erf(yt * (1.0 / math.sqrt(2.0))))).astype(BF16)

    row = lambda i: (0, 0)
    blk3 = lambda i: (0, 0, 0)
    return _pcall(
        body, name=name, grid=(t // lc,),
        in_specs=[pl.BlockSpec((lc, 128), functools.partial(lambda i, q: (i, nq + q), q=q)) for q in range(nq)]
        + [pl.BlockSpec((1, gn), row), pl.BlockSpec((1, gn), row),
           pl.BlockSpec(bb_re.shape, blk3), pl.BlockSpec(bb_im.shape, blk3),
           pl.BlockSpec(cc_re.shape, blk3), pl.BlockSpec(cc_im.shape, blk3), pl.BlockSpec((1, sw), row)],
        out_specs=[pl.BlockSpec((lc, sw), lambda i: (i, 0)), pl.BlockSpec((lc, sw), lambda i: (i, 0)),
                   pl.BlockSpec((lc, gn), lambda i: (i, 0)), pl.BlockSpec((lc, gn), lambda i: (i, 0))],
        out_shape=[jax.ShapeDtypeStruct((t, sw), F32), jax.ShapeDtypeStruct((t, sw), BF16),
                   jax.ShapeDtypeStruct((t, gn), BF16), jax.ShapeDtypeStruct((t, gn), BF16)],
        scratch_shapes=[pltpu.VMEM((lc, gn), F32), pltpu.VMEM((lc, gn), F32), pltpu.VMEM((lc, gn), F32), pltpu.VMEM((lc, gn), F32),
                        pltpu.VMEM((lc, sw), F32), pltpu.VMEM((nq, lc, 128), F32),
                        pltpu.VMEM((8, gn), F32), pltpu.VMEM((8, gn), F32), pltpu.VMEM((1, gn), F32), pltpu.VMEM((1, gn), F32)],
        compiler_params=_params("arbitrary"))(*([uin] * nq), lrdt, ang, bb_re, bb_im, cc_re, cc_im, d_skip)


def _ssm_bwd(name, dge, y, uin, s_re, s_im, lrdt, ang, bbt_re, bbt_im, cct_re, cct_im, d_skip, sw, duin):
    t = uin.shape[0]
    gn = lrdt.shape[1]
    lc = _ssm_chunk(t)
    nb = t // lc
    jn = lc // 8
    ub, sb = sw // SSM_BLOCKS, gn // SSM_BLOCKS
    nq = sw // 128
    tail = 16

    def body(dge_ref, y_ref, u_ref, sre_ref, sim_ref, tre_ref, tim_ref, lrdt_ref, ang_ref, btr_ref, bti_ref, ctr_ref, cti_ref,
             d_ref, duin_ref, du_ref, dar_ref, dai_ref, dbr_ref, dbi_ref, dcr_ref, dci_ref, dd_ref,
             q_re, q_im, a_re, a_im, dyp, up, dys, us, dup, cst_re, cst_im, sp_re, sp_im, car_re, car_im):
        i = pl.program_id(0)
        lrdt_v, ang_v = lrdt_ref[...], ang_ref[...]

        @pl.when(i == 0)
        def _():
            k = (jn - lax.broadcasted_iota(jnp.int32, (jn, 1), 0)).astype(F32)
            _fill_rows8(q_re, q_im, *_pow_rows(lrdt_v, ang_v, k))
            car_re[...] = jnp.zeros_like(car_re)
            car_im[...] = jnp.zeros_like(car_im)
            for r in (dar_ref, dai_ref, dbr_ref, dbi_ref, dcr_ref, dci_ref, dd_ref):
                r[...] = jnp.zeros_like(r)

        yv = y_ref[...]
        ut = u_ref[...]
        cdf = 0.5 * (1.0 + lax.erf(yv * (1.0 / math.sqrt(2.0))))
        pdf = jnp.exp(-0.5 * yv * yv) * (1.0 / math.sqrt(2.0 * math.pi))
        dyt = dge_ref[...].astype(F32) * (cdf + yv * pdf)
        dd_ref[...] += _colsum(dyt * ut)
        for q in range(nq):
            dys[q] = dyt[:, 128 * q:128 * (q + 1)]
            us[q] = ut[:, 128 * q:128 * (q + 1)]
        _to_segments(dyp, [dys.at[q] for q in range(nq)], jn)
        _to_segments(up, [us.at[q] for q in range(nq)], jn)
        dy = dyp[...]
        u = up[...]
        dyb = dy.astype(BF16)
        ubf = u.astype(BF16)
        for q in range(SSM_BLOCKS):
            dq = dyb[:, q * ub:(q + 1) * ub]
            a_re[:, q * sb:(q + 1) * sb] = jnp.dot(dq, ctr_ref[q], preferred_element_type=F32)
            a_im[:, q * sb:(q + 1) * sb] = -jnp.dot(dq, cti_ref[q], preferred_element_type=F32)
        a1r, a1i = _pow_rows(lrdt_v, ang_v, 1.0)
        ajr, aji = _pow_rows(lrdt_v, ang_v, float(jn))
        for q in range(SSM_BLOCKS):
            cols = slice(q * sb, (q + 1) * sb)
            ar8 = jnp.broadcast_to(a1r[:, cols], (8, sb))
            ai8 = jnp.broadcast_to(a1i[:, cols], (8, sb))

            def step(jj, carry, cols=cols, ar8=ar8, ai8=ai8):
                sr, si = carry
                rows = pl.ds(pl.multiple_of((jn - 2 - jj) * 8, 8), 8)
                mr, mi = _cmul_conj(ar8, ai8, sr, si)
                nr, ni = mr + a_re[rows, cols], mi + a_im[rows, cols]
                a_re[rows, cols] = nr
                a_im[rows, cols] = ni
                return nr, ni

            lax.fori_loop(0, jn - 1, step, (a_re[lc - 8:lc, cols], a_im[lc - 8:lc, cols]), unroll=4)
        er, ei = a_re[0:8, :], a_im[0:8, :]
        hr, hi = car_re[...], car_im[...]
        for s in range(7, -1, -1):
            cst_re[s:s + 1, :] = hr
            cst_im[s:s + 1, :] = hi
            mr, mi = _cmul_conj(ajr, aji, hr, hi)
            hr, hi = mr + er[s:s + 1, :], mi + ei[s:s + 1, :]
        car_re[...] = hr
        car_im[...] = hi
        first = (i == nb - 1).astype(F32)
        sp_re[0:tail, :] = tre_ref[...].astype(F32) * (1.0 - first)
        sp_im[0:tail, :] = tim_ref[...].astype(F32) * (1.0 - first)
        sp_re[tail:tail + 8, :] = sre_ref[lc - tail:lc, :].astype(F32)[tail - 8:tail]
        sp_im[tail:tail + 8, :] = sim_ref[lc - tail:lc, :].astype(F32)[tail - 8:tail]
        tn_dims = (((0,), (0,)), ((), ()))
        for q in range(SSM_BLOCKS):
            cols = slice(q * sb, (q + 1) * sb)
            ycols = slice(q * ub, (q + 1) * ub)
            cr = jnp.tile(cst_re[:, cols], (jn, 1))
            ci = jnp.tile(cst_im[:, cols], (jn, 1))
            mr, mi = _cmul_conj(q_re[:, cols], q_im[:, cols], cr, ci)
            lam_r, lam_i = a_re[:, cols] + mr, a_im[:, cols] + mi
            s_r, s_i = sre_ref[:, cols], sim_ref[:, cols]
            p0r, p0i = sp_re[tail - 1:tail + 7, cols], sp_im[tail - 1:tail + 7, cols]
            l0r, l0i, l1r, l1i = lam_r[0:8], lam_i[0:8], lam_r[8:lc], lam_i[8:lc]
            pvr, pvi = s_r.astype(F32)[0:lc - 8], s_i.astype(F32)[0:lc - 8]
            dar_ref[:, cols] += _colsum(l1r * pvr + l1i * pvi) + _colsum(l0r * p0r + l0i * p0i)
            dai_ref[:, cols] += _colsum(l1i * pvr - l1r * pvi) + _colsum(l0i * p0r - l0r * p0i)
            lrb, lib = lam_r.astype(BF16), lam_i.astype(BF16)
            dup[q] = (jnp.dot(lrb, btr_ref[q], preferred_element_type=F32)
                      + jnp.dot(lib, bti_ref[q], preferred_element_type=F32) + d_ref[:, ycols] * dy[:, ycols])
            _from_segments(du_ref, dup, q, jn, BF16)
            uq = ubf[:, ycols]
            dbr_ref[q] += lax.dot_general(uq, lrb, tn_dims, preferred_element_type=F32)
            dbi_ref[q] += lax.dot_general(uq, lib, tn_dims, preferred_element_type=F32)
            dq = dyb[:, ycols]
            dcr_ref[q] += lax.dot_general(s_r.astype(BF16), dq, tn_dims, preferred_element_type=F32)
            dci_ref[q] -= lax.dot_general(s_i.astype(BF16), dq, tn_dims, preferred_element_type=F32)

    rev = lambda i: (nb - 1 - i, 0)
    tailmap = lambda i: (jnp.maximum((nb - 1 - i) * (lc // tail) - 1, 0), 0)
    row = lambda i: (0, 0)
    blk3 = lambda i: (0, 0, 0)
    return _pcall(
        body, name=name, grid=(nb,),
        in_specs=[pl.BlockSpec((lc, sw), rev), pl.BlockSpec((lc, sw), rev), pl.BlockSpec((lc, sw), lambda i: (nb - 1 - i, 1)),
                  pl.BlockSpec((lc, gn), rev), pl.BlockSpec((lc, gn), rev),
                  pl.BlockSpec((tail, gn), tailmap), pl.BlockSpec((tail, gn), tailmap),
                  pl.BlockSpec((1, gn), row), pl.BlockSpec((1, gn), row),
                  pl.BlockSpec(bbt_re.shape, blk3), pl.BlockSpec(bbt_im.shape, blk3),
                  pl.BlockSpec(cct_re.shape, blk3), pl.BlockSpec(cct_im.shape, blk3), pl.BlockSpec((1, sw), row), ANY_SPEC],
        out_specs=[pl.BlockSpec((lc, sw), lambda i: (nb - 1 - i, 1)), pl.BlockSpec((1, gn), row), pl.BlockSpec((1, gn), row),
                   pl.BlockSpec((SSM_BLOCKS, ub, sb), blk3), pl.BlockSpec((SSM_BLOCKS, ub, sb), blk3),
                   pl.BlockSpec((SSM_BLOCKS, sb, ub), blk3), pl.BlockSpec((SSM_BLOCKS, sb, ub), blk3),
                   pl.BlockSpec((1, sw), row)],
        out_shape=[jax.ShapeDtypeStruct(duin.shape, BF16), jax.ShapeDtypeStruct((1, gn), F32), jax.ShapeDtypeStruct((1, gn), F32),
                   jax.ShapeDtypeStruct((SSM_BLOCKS, ub, sb), F32), jax.ShapeDtypeStruct((SSM_BLOCKS, ub, sb), F32),
                   jax.ShapeDtypeStruct((SSM_BLOCKS, sb, ub), F32), jax.ShapeDtypeStruct((SSM_BLOCKS, sb, ub), F32),
                   jax.ShapeDtypeStruct((1, sw), F32)],
        scratch_shapes=[pltpu.VMEM((lc, gn), F32), pltpu.VMEM((lc, gn), F32), pltpu.VMEM((lc, gn), F32), pltpu.VMEM((lc, gn), F32),
                        pltpu.VMEM((lc, sw), F32), pltpu.VMEM((lc, sw), F32),
                        pltpu.VMEM((nq, lc, 128), F32), pltpu.VMEM((nq, lc, 128), F32), pltpu.VMEM((nq, lc, 128), F32),
                        pltpu.VMEM((8, gn), F32), pltpu.VMEM((8, gn), F32),
                        pltpu.VMEM((tail + 8, gn), F32), pltpu.VMEM((tail + 8, gn), F32),
                        pltpu.VMEM((1, gn), F32), pltpu.VMEM((1, gn), F32)],
        input_output_aliases={14: 0},
        compiler_params=_params("arbitrary"))(dge, y, uin, s_re, s_im, s_re, s_im, lrdt, ang,
                                               bbt_re, bbt_im, cct_re, cct_im, d_skip, duin)


def _ssm_fwd_doubling(name, uin, lrdt, ang, bb_re, bb_im, cc_re, cc_im, d_skip, sw):
    t = uin.shape[0]
    gn = lrdt.shape[1]
    lc = _ssm_chunk(t)
    nsteps = int(math.log2(lc))
    pad = lc // 2
    ub, sb = sw // SSM_BLOCKS, gn // SSM_BLOCKS

    def body(u_ref, lrdt_ref, ang_ref, bbr_ref, bbi_ref, ccr_ref, cci_ref, d_ref, y_ref, ge_ref, sre_ref, sim_ref,
             p_re, p_im, a_re, a_im, b_re, b_im, car_re, car_im):
        i = pl.program_id(0)
        lrdt_v, ang_v = lrdt_ref[...], ang_ref[...]

        @pl.when(i == 0)
        def _():
            k = (lax.broadcasted_iota(jnp.int32, (lc, 1), 0) + 1).astype(F32)
            pr, pi = _pow_rows(lrdt_v, ang_v, k)
            p_re[...] = pr
            p_im[...] = pi
            zeros = jnp.zeros((pad, gn), F32)
            a_re[0:pad, :] = zeros
            a_im[0:pad, :] = zeros
            b_re[0:pad, :] = zeros
            b_im[0:pad, :] = zeros
            car_re[...] = jnp.zeros_like(car_re)
            car_im[...] = jnp.zeros_like(car_im)

        u = u_ref[...]
        ubf = u.astype(BF16)
        for q in range(SSM_BLOCKS):
            uq = ubf[:, q * ub:(q + 1) * ub]
            a_re[pad:pad + lc, q * sb:(q + 1) * sb] = jnp.dot(uq, bbr_ref[q], preferred_element_type=F32)
            a_im[pad:pad + lc, q * sb:(q + 1) * sb] = jnp.dot(uq, bbi_ref[q], preferred_element_type=F32)
        bufs = [(a_re, a_im), (b_re, b_im)]
        for q in range(SSM_BLOCKS):
            cols = slice(q * sb, (q + 1) * sb)
            for j in range(nsteps):
                dd = 1 << j
                (s_re, s_im), (d_re, d_im) = bufs[j % 2], bufs[(j + 1) % 2]
                pr, pi = _pow_rows(lrdt_v[:, cols], ang_v[:, cols], float(dd))
                cr, ci = s_re[pad:pad + lc, cols], s_im[pad:pad + lc, cols]
                hr, hi = s_re[pad - dd:pad - dd + lc, cols], s_im[pad - dd:pad - dd + lc, cols]
                d_re[pad:pad + lc, cols] = cr + (pr * hr - pi * hi)
                d_im[pad:pad + lc, cols] = ci + (pr * hi + pi * hr)
            f_re, f_im = bufs[nsteps % 2]
            cr, ci = car_re[:, cols], car_im[:, cols]
            pr, pi = p_re[:, cols], p_im[:, cols]
            sr = f_re[pad:pad + lc, cols] + (pr * cr - pi * ci)
            si = f_im[pad:pad + lc, cols] + (pr * ci + pi * cr)
            sre_ref[:, cols] = sr
            sim_ref[:, cols] = si
            car_re[:, cols] = sr[lc - 1:lc, :]
            car_im[:, cols] = si[lc - 1:lc, :]
            ycols = slice(q * ub, (q + 1) * ub)
            y = (jnp.dot(sr.astype(BF16), ccr_ref[q], preferred_element_type=F32)
                 - jnp.dot(si.astype(BF16), cci_ref[q], preferred_element_type=F32)
                 + d_ref[:, ycols] * u[:, ycols])
            y_ref[:, ycols] = y
            ge_ref[:, ycols] = (0.5 * y * (1.0 + lax.erf(y * (1.0 / math.sqrt(2.0))))).astype(BF16)

    cblk = sw // sw
    row = lambda i: (0, 0)
    blk3 = lambda i: (0, 0, 0)
    return _pcall(
        body, name=name, grid=(t // lc,),
        in_specs=[pl.BlockSpec((lc, sw), lambda i: (i, cblk)), pl.BlockSpec((1, gn), row), pl.BlockSpec((1, gn), row),
                  pl.BlockSpec(bb_re.shape, blk3), pl.BlockSpec(bb_im.shape, blk3),
                  pl.BlockSpec(cc_re.shape, blk3), pl.BlockSpec(cc_im.shape, blk3), pl.BlockSpec((1, sw), row)],
        out_specs=[pl.BlockSpec((lc, sw), lambda i: (i, 0)), pl.BlockSpec((lc, sw), lambda i: (i, 0)),
                   pl.BlockSpec((lc, gn), lambda i: (i, 0)), pl.BlockSpec((lc, gn), lambda i: (i, 0))],
        out_shape=[jax.ShapeDtypeStruct((t, sw), F32), jax.ShapeDtypeStruct((t, sw), BF16),
                   jax.ShapeDtypeStruct((t, gn), F32), jax.ShapeDtypeStruct((t, gn), F32)],
        scratch_shapes=[pltpu.VMEM((lc, gn), F32), pltpu.VMEM((lc, gn), F32),
                        pltpu.VMEM((pad + lc, gn), F32), pltpu.VMEM((pad + lc, gn), F32),
                        pltpu.VMEM((pad + lc, gn), F32), pltpu.VMEM((pad + lc, gn), F32),
                        pltpu.VMEM((1, gn), F32), pltpu.VMEM((1, gn), F32)],
        compiler_params=_params("arbitrary"))(uin, lrdt, ang, bb_re, bb_im, cc_re, cc_im, d_skip)


def _ssm_bwd_doubling(name, dge, y, uin, s_re, s_im, lrdt, ang, bbt_re, bbt_im, cct_re, cct_im, d_skip, sw):
    t = uin.shape[0]
    gn = lrdt.shape[1]
    lc = _ssm_chunk(t)
    nb = t // lc
    nsteps = int(math.log2(lc))
    pad = lc // 2
    ub, sb = sw // SSM_BLOCKS, gn // SSM_BLOCKS
    tail = 8

    def body(dge_ref, y_ref, u_ref, sre_ref, sim_ref, tre_ref, tim_ref, lrdt_ref, ang_ref, btr_ref, bti_ref, ctr_ref, cti_ref,
             d_ref, du_ref, dar_ref, dai_ref, dbr_ref, dbi_ref, dcr_ref, dci_ref, dd_ref,
             q_re, q_im, a_re, a_im, b_re, b_im, sp_re, sp_im, car_re, car_im):
        i = pl.program_id(0)
        lrdt_v, ang_v = lrdt_ref[...], ang_ref[...]

        @pl.when(i == 0)
        def _():
            k = (lc - lax.broadcasted_iota(jnp.int32, (lc, 1), 0)).astype(F32)
            qr, qi = _pow_rows(lrdt_v, ang_v, k)
            q_re[...] = qr
            q_im[...] = qi
            zeros = jnp.zeros((pad, gn), F32)
            a_re[lc:lc + pad, :] = zeros
            a_im[lc:lc + pad, :] = zeros
            b_re[lc:lc + pad, :] = zeros
            b_im[lc:lc + pad, :] = zeros
            car_re[...] = jnp.zeros_like(car_re)
            car_im[...] = jnp.zeros_like(car_im)
            for r in (dar_ref, dai_ref, dbr_ref, dbi_ref, dcr_ref, dci_ref, dd_ref):
                r[...] = jnp.zeros_like(r)

        first = (i == nb - 1).astype(F32)
        sp_re[0:tail, :] = tre_ref[...] * (1.0 - first)
        sp_im[0:tail, :] = tim_ref[...] * (1.0 - first)
        sp_re[tail:tail + lc, :] = sre_ref[...]
        sp_im[tail:tail + lc, :] = sim_ref[...]

        yv = y_ref[...]
        u = u_ref[...]
        cdf = 0.5 * (1.0 + lax.erf(yv * (1.0 / math.sqrt(2.0))))
        pdf = jnp.exp(-0.5 * yv * yv) * (1.0 / math.sqrt(2.0 * math.pi))
        dy = dge_ref[...].astype(F32) * (cdf + yv * pdf)
        dd_ref[...] += _colsum(dy * u)
        dyb = dy.astype(BF16)
        ubf = u.astype(BF16)
        for q in range(SSM_BLOCKS):
            dq = dyb[:, q * ub:(q + 1) * ub]
            a_re[0:lc, q * sb:(q + 1) * sb] = jnp.dot(dq, ctr_ref[q], preferred_element_type=F32)
            a_im[0:lc, q * sb:(q + 1) * sb] = -jnp.dot(dq, cti_ref[q], preferred_element_type=F32)
        bufs = [(a_re, a_im), (b_re, b_im)]
        tn_dims = (((0,), (0,)), ((), ()))
        for q in range(SSM_BLOCKS):
            cols = slice(q * sb, (q + 1) * sb)
            ycols = slice(q * ub, (q + 1) * ub)
            for j in range(nsteps):
                ds = 1 << j
                (s_r, s_i), (d_r, d_i) = bufs[j % 2], bufs[(j + 1) % 2]
                pr, pi = _pow_rows(lrdt_v[:, cols], ang_v[:, cols], float(ds))
                cr, ci = s_r[0:lc, cols], s_i[0:lc, cols]
                hr, hi = s_r[ds:ds + lc, cols], s_i[ds:ds + lc, cols]
                d_r[0:lc, cols] = cr + (pr * hr + pi * hi)
                d_i[0:lc, cols] = ci + (pr * hi - pi * hr)
            f_r, f_i = bufs[nsteps % 2]
            cr, ci = car_re[:, cols], car_im[:, cols]
            qr, qi = q_re[:, cols], q_im[:, cols]
            lam_r = f_r[0:lc, cols] + (qr * cr + qi * ci)
            lam_i = f_i[0:lc, cols] + (qr * ci - qi * cr)
            car_re[:, cols] = lam_r[0:1, :]
            car_im[:, cols] = lam_i[0:1, :]
            pvr, pvi = sp_re[tail - 1:tail - 1 + lc, cols], sp_im[tail - 1:tail - 1 + lc, cols]
            dar_ref[:, cols] += _colsum(lam_r * pvr + lam_i * pvi)
            dai_ref[:, cols] += _colsum(lam_i * pvr - lam_r * pvi)
            lrb, lib = lam_r.astype(BF16), lam_i.astype(BF16)
            du_ref[:, ycols] = (jnp.dot(lrb, btr_ref[q], preferred_element_type=F32)
                                + jnp.dot(lib, bti_ref[q], preferred_element_type=F32)
                                + d_ref[:, ycols] * dy[:, ycols]).astype(BF16)
            uq = ubf[:, ycols]
            dbr_ref[q] += lax.dot_general(uq, lrb, tn_dims, preferred_element_type=F32)
            dbi_ref[q] += lax.dot_general(uq, lib, tn_dims, preferred_element_type=F32)
            dq = dyb[:, ycols]
            dcr_ref[q] += lax.dot_general(sre_ref[:, cols].astype(BF16), dq, tn_dims, preferred_element_type=F32)
            dci_ref[q] -= lax.dot_general(sim_ref[:, cols].astype(BF16), dq, tn_dims, preferred_element_type=F32)

    cblk = 1
    rev = lambda i: (nb - 1 - i, 0)
    revu = lambda i: (nb - 1 - i, cblk)
    tailmap = lambda i: (jnp.maximum((nb - 1 - i) * (lc // tail) - 1, 0), 0)
    row = lambda i: (0, 0)
    blk3 = lambda i: (0, 0, 0)
    return _pcall(
        body, name=name, grid=(nb,),
        in_specs=[pl.BlockSpec((lc, sw), rev), pl.BlockSpec((lc, sw), rev), pl.BlockSpec((lc, sw), revu),
                  pl.BlockSpec((lc, gn), rev), pl.BlockSpec((lc, gn), rev),
                  pl.BlockSpec((tail, gn), tailmap), pl.BlockSpec((tail, gn), tailmap),
                  pl.BlockSpec((1, gn), row), pl.BlockSpec((1, gn), row),
                  pl.BlockSpec(bbt_re.shape, blk3), pl.BlockSpec(bbt_im.shape, blk3),
                  pl.BlockSpec(cct_re.shape, blk3), pl.BlockSpec(cct_im.shape, blk3), pl.BlockSpec((1, sw), row)],
        out_specs=[pl.BlockSpec((lc, sw), rev), pl.BlockSpec((1, gn), row), pl.BlockSpec((1, gn), row),
                   pl.BlockSpec((SSM_BLOCKS, ub, sb), blk3), pl.BlockSpec((SSM_BLOCKS, ub, sb), blk3),
                   pl.BlockSpec((SSM_BLOCKS, sb, ub), blk3), pl.BlockSpec((SSM_BLOCKS, sb, ub), blk3),
                   pl.BlockSpec((1, sw), row)],
        out_shape=[jax.ShapeDtypeStruct((t, sw), BF16), jax.ShapeDtypeStruct((1, gn), F32), jax.ShapeDtypeStruct((1, gn), F32),
                   jax.ShapeDtypeStruct((SSM_BLOCKS, ub, sb), F32), jax.ShapeDtypeStruct((SSM_BLOCKS, ub, sb), F32),
                   jax.ShapeDtypeStruct((SSM_BLOCKS, sb, ub), F32), jax.ShapeDtypeStruct((SSM_BLOCKS, sb, ub), F32),
                   jax.ShapeDtypeStruct((1, sw), F32)],
        scratch_shapes=[pltpu.VMEM((lc, gn), F32), pltpu.VMEM((lc, gn), F32),
                        pltpu.VMEM((lc + pad, gn), F32), pltpu.VMEM((lc + pad, gn), F32),
                        pltpu.VMEM((lc + pad, gn), F32), pltpu.VMEM((lc + pad, gn), F32),
                        pltpu.VMEM((tail + lc, gn), F32), pltpu.VMEM((tail + lc, gn), F32),
                        pltpu.VMEM((1, gn), F32), pltpu.VMEM((1, gn), F32)],
        compiler_params=_params("arbitrary"))(dge, y, uin, s_re, s_im, s_re, s_im, lrdt, ang, bbt_re, bbt_im, cct_re, cct_im, d_skip)


def _blockdiag_b(bb, sw):
    gpb = (sw // SSM_GROUP) // SSM_BLOCKS
    b4 = bb.reshape(SSM_BLOCKS, gpb, SSM_STATE, SSM_GROUP)
    eye = jnp.eye(gpb, dtype=bb.dtype)
    out = jnp.einsum('qgnh,gk->qghkn', b4, eye)
    return out.reshape(SSM_BLOCKS, gpb * SSM_GROUP, gpb * SSM_STATE)


def _blockdiag_c(cc, sw):
    gpb = (sw // SSM_GROUP) // SSM_BLOCKS
    c4 = cc.reshape(SSM_BLOCKS, gpb, SSM_GROUP, SSM_STATE)
    eye = jnp.eye(gpb, dtype=cc.dtype)
    out = jnp.einsum('qghn,gk->qgnkh', c4, eye)
    return out.reshape(SSM_BLOCKS, gpb * SSM_STATE, gpb * SSM_GROUP)


def _diag_of_b(dbb, sw):
    gpb = (sw // SSM_GROUP) // SSM_BLOCKS
    d5 = dbb.reshape(SSM_BLOCKS, gpb, SSM_GROUP, gpb, SSM_STATE)
    return jnp.einsum('qghgn->qgnh', d5).reshape(SSM_BLOCKS * gpb * SSM_STATE, SSM_GROUP)


def _diag_of_c(dcc, sw):
    gpb = (sw // SSM_GROUP) // SSM_BLOCKS
    d5 = dcc.reshape(SSM_BLOCKS, gpb, SSM_STATE, gpb, SSM_GROUP)
    return jnp.einsum('qgngh->qghn', d5).reshape(SSM_BLOCKS * gpb, SSM_GROUP, SSM_STATE)


def _ada_fwd(name, c_all, w_sh, b_sh):
    nb, d = c_all.shape
    ncol = w_sh.shape[1]
    tn = _pick(ncol, 768)

    def body(c_ref, w_ref, b_ref, o_ref):
        cv = c_ref[...]
        sil = cv * _sigmoid(cv)
        o_ref[...] = jnp.dot(sil, w_ref[...], preferred_element_type=F32, precision=lax.Precision.HIGHEST) + b_ref[...]

    return _pcall(body, name=name, grid=(ncol // tn,),
                  in_specs=[pl.BlockSpec((nb, d), lambda j: (0, 0)), pl.BlockSpec((d, tn), lambda j: (0, j)),
                            pl.BlockSpec((1, tn), lambda j: (0, j))],
                  out_specs=pl.BlockSpec((nb, tn), lambda j: (0, j)),
                  out_shape=jax.ShapeDtypeStruct((nb, ncol), F32), compiler_params=_params("parallel"))(c_all, w_sh, b_sh)


def _ada_bwd(name, c_all, dmod_sh):
    nb, d = c_all.shape
    ncol = dmod_sh.shape[1]
    tn = _pick(ncol, 768)

    def body(c_ref, g_ref, o_ref):
        cv = c_ref[...]
        sil = cv * _sigmoid(cv)
        o_ref[...] = lax.dot_general(sil, g_ref[...], (((0,), (0,)), ((), ())), preferred_element_type=F32,
                                     precision=lax.Precision.HIGHEST)

    return _pcall(body, name=name, grid=(ncol // tn,),
                  in_specs=[pl.BlockSpec((nb, d), lambda j: (0, 0)), pl.BlockSpec((nb, tn), lambda j: (0, j))],
                  out_specs=pl.BlockSpec((d, tn), lambda j: (0, j)),
                  out_shape=jax.ShapeDtypeStruct((d, ncol), F32), compiler_params=_params("parallel"))(c_all, dmod_sh)


def _place():
    return lax.axis_index("x"), lax.axis_index("y"), lax.axis_index("c")


def _allgather_small(name, blk, deps=()):
    m_per, n = blk.shape

    def body(x_ref, *rest):
        out_ref, send_sems, recv_sems, local_sem = rest[len(deps):]
        x, y, c = _place()
        me, sibling = (x, y, c), (x, y, 1 - c)
        chips = [(1 - x, y), (x, 1 - y), (1 - x, 1 - y)]

        def rows(px, py, pc):
            return out_ref.at[pl.ds((4 * px + 2 * py + pc) * m_per, m_per), :]

        def copy(k, block, to, src=None):
            return pltpu.make_async_remote_copy(
                src_ref=rows(*block) if src is None else src, dst_ref=rows(*block),
                send_sem=send_sems.at[k], recv_sem=recv_sems.at[k], device_id=to, device_id_type=MESH)

        mine = pltpu.make_async_copy(x_ref, rows(*me), local_sem)
        mine.start()
        first = [copy(0, me, sibling, src=x_ref)]
        first += [copy(1 + j, me, (*chip, c), src=x_ref) for j, chip in enumerate(chips)]
        for cp in first:
            cp.start()
        passed = [copy(4 + j, (*chip, c), sibling) for j, chip in enumerate(chips)]
        for j, chip in enumerate(chips):
            copy(1 + j, (*chip, c), me).wait_recv()
            passed[j].start()
        copy(0, sibling, me).wait_recv()
        for j, chip in enumerate(chips):
            copy(4 + j, (*chip, 1 - c), me).wait_recv()
        for cp in first + passed:
            cp.wait_send()
        mine.wait()

    return _pcall(body, name=name, out_shape=jax.ShapeDtypeStruct((N_DEV * m_per, n), blk.dtype),
                  in_specs=[pl.BlockSpec(memory_space=pltpu.VMEM)] + [ANY_SPEC] * len(deps),
                  out_specs=pl.BlockSpec(memory_space=pltpu.VMEM),
                  scratch_shapes=[pltpu.SemaphoreType.DMA((7,)), pltpu.SemaphoreType.DMA((7,)), pltpu.SemaphoreType.DMA],
                  compiler_params=pltpu.CompilerParams(vmem_limit_bytes=VMEM_LIMIT))(blk, *deps)


def _other_chips(x, y):
    return [(1 - x, y), (x, 1 - y), (1 - x, 1 - y)]


def _place_shards(shards, s_me):
    return [lax.dynamic_update_slice(lax.empty((N_CHIPS,) + s.shape, s.dtype), s[None], (s_me, 0, 0)) for s in shards]


def _ag_copy(src, land, send, recv, wi, j, chip, x, y, c, tc, both):
    hr = src.shape[0] // 2
    half = pl.ds(pl.multiple_of(c * hr, 16), hr)
    k = 3 * wi + j
    return pltpu.make_async_remote_copy(
        src_ref=src.at[half, :], dst_ref=land.at[2 * x + y, half, :],
        send_sem=send.at[2 * k + tc if both else k], recv_sem=recv.at[2 * k + c if both else k],
        device_id=(chip[0], chip[1], tc), device_id_type=MESH)


def _ag_targets(c, both):
    return (0, 1) if both else (c,)


def _ag_start(name, shards, lands, groups, direct, deps=()):
    nw, ng, nd = len(shards), len(groups), len(deps)

    def body(*refs):
        src, land = refs[:nw], refs[nw:2 * nw]
        sems = refs[2 * nw + nd:2 * nw + nd + 2 * ng]
        token = refs[-1]
        x, y, c = _place()
        for gi, grp in enumerate(groups):
            for wi, w in enumerate(grp):
                for j, chip in enumerate(_other_chips(x, y)):
                    for tc in _ag_targets(c, direct[gi]):
                        _ag_copy(src[w], land[w], sems[2 * gi], sems[2 * gi + 1], wi, j, chip, x, y, c, tc, direct[gi]).start()
        token[...] = jnp.zeros_like(token)

    sem_shapes = []
    for gi, grp in enumerate(groups):
        sem_shapes += [pltpu.SemaphoreType.DMA(((6 if direct[gi] else 3) * len(grp),))] * 2
    out_shape = sem_shapes + [pltpu.HBM(s.shape, s.dtype) for s in shards] + [pltpu.HBM(l.shape, l.dtype) for l in lands]
    out_shape += [jax.ShapeDtypeStruct((8, 128), F32)]
    res = _pcall(body, name=name, out_shape=out_shape, in_specs=[HBM_SPEC] * (2 * nw) + [ANY_SPEC] * nd,
                 out_specs=[SEM_SPEC] * (2 * ng) + [HBM_SPEC] * (2 * nw) + [pl.BlockSpec(memory_space=pltpu.VMEM)],
                 input_output_aliases={i: 2 * ng + i for i in range(2 * nw)},
                 compiler_params=pltpu.CompilerParams(has_side_effects=EFFECT))(
                     *[_hbm(s) for s in shards], *[_hbm(l) for l in lands], *deps)
    sems = [(res[2 * gi], res[2 * gi + 1]) for gi in range(ng)]
    return sems, list(res[2 * ng:2 * ng + nw]), list(res[2 * ng + nw:2 * ng + 2 * nw]), res[-1]


def _ag_wait(name, shards, lands, send, recv, both, after):
    n = len(shards)

    def body(*refs):
        src, land = refs[:n], refs[n:2 * n]
        send_sem, recv_sem = refs[2 * n], refs[2 * n + 1]
        x, y, c = _place()
        for wi in range(n):
            for j, chip in enumerate(_other_chips(x, y)):
                for tc in _ag_targets(c, both):
                    _ag_copy(src[wi], land[wi], send_sem, recv_sem, wi, j, chip, x, y, c, tc, both).wait_send()
                    _ag_copy(src[wi], land[wi], send_sem, recv_sem, wi, j, chip, chip[0], chip[1], tc, c, both).wait_recv()

    res = _pcall(body, name=name, out_shape=[pltpu.HBM(a.shape, a.dtype) for a in list(shards) + list(lands)],
                 in_specs=[HBM_SPEC] * (2 * n) + [SEM_SPEC, SEM_SPEC] + [ANY_SPEC] * len(after), out_specs=[HBM_SPEC] * (2 * n),
                 input_output_aliases={i: i for i in range(2 * n)},
                 compiler_params=pltpu.CompilerParams(has_side_effects=EFFECT))(*shards, *lands, send, recv, *after)
    return list(res[n:])


def _ag_forward(name, lands):
    n = len(lands)

    def body(*refs):
        out = refs[n:2 * n]
        send, recv = refs[2 * n], refs[2 * n + 1]
        x, y, c = _place()
        sib = (x, y, 1 - c)
        cps = []
        for wi in range(n):
            hr = out[wi].shape[1] // 2
            for j, (cx, cy) in enumerate(_other_chips(x, y)):
                got = out[wi].at[2 * cx + cy, pl.ds(pl.multiple_of(c * hr, 16), hr), :]
                cp = pltpu.make_async_remote_copy(src_ref=got, dst_ref=got, send_sem=send.at[3 * wi + j], recv_sem=recv.at[3 * wi + j],
                                                  device_id=sib, device_id_type=MESH)
                cp.start()
                cps.append(cp)
        for wi in range(n):
            hr = out[wi].shape[1] // 2
            for j, (cx, cy) in enumerate(_other_chips(x, y)):
                got = out[wi].at[2 * cx + cy, pl.ds(pl.multiple_of((1 - c) * hr, 16), hr), :]
                pltpu.make_async_remote_copy(src_ref=got, dst_ref=got, send_sem=send.at[3 * wi + j], recv_sem=recv.at[3 * wi + j],
                                             device_id=sib, device_id_type=MESH).wait_recv()
        for cp in cps:
            cp.wait_send()

    res = _pcall(body, name=name, out_shape=[jax.ShapeDtypeStruct(l.shape, l.dtype) for l in lands],
                 in_specs=[ANY_SPEC] * n, out_specs=[ANY_SPEC] * n, input_output_aliases={i: i for i in range(n)},
                 scratch_shapes=[pltpu.SemaphoreType.DMA((3 * n,)), pltpu.SemaphoreType.DMA((3 * n,))])(*lands)
    return list(res)


def _peers(x, y, c):
    offs = [(dx, dy, dc) for dx in (0, 1) for dy in (0, 1) for dc in (0, 1)][1:]
    return [(1 - x if dx else x, 1 - y if dy else y, 1 - c if dc else c) for dx, dy, dc in offs]


def _rs_copy(g_ref, land_ref, send, recv, wi, k, to, sender):
    hr = g_ref.shape[1] // 2
    return pltpu.make_async_remote_copy(
        src_ref=g_ref.at[2 * to[0] + to[1], pl.ds(pl.multiple_of(to[2] * hr, 16), hr), :], dst_ref=land_ref.at[sender],
        send_sem=send.at[7 * wi + k], recv_sem=recv.at[7 * wi + k], device_id=to, device_id_type=MESH)


def _rs_start(name, gs):
    n = len(gs)
    lands = [lax.empty((N_DEV, g.shape[1] // 2, g.shape[2]), BF16) for g in gs]

    def body(*refs):
        g, land = refs[:n], refs[n:2 * n]
        send, recv = refs[2 * n], refs[2 * n + 1]
        token = refs[-1]
        x, y, c = _place()
        me = 4 * x + 2 * y + c
        for wi in range(n):
            for k, to in enumerate(_peers(x, y, c)):
                _rs_copy(g[wi], land[wi], send, recv, wi, k, to, me).start()
        token[...] = jnp.zeros_like(token)

    out_shape = [pltpu.SemaphoreType.DMA((7 * n,))] * 2 + [pltpu.HBM(a.shape, a.dtype) for a in list(gs) + lands]
    out_shape += [jax.ShapeDtypeStruct((8, 128), F32)]
    res = _pcall(body, name=name, out_shape=out_shape, in_specs=[HBM_SPEC] * (2 * n),
                 out_specs=[SEM_SPEC] * 2 + [HBM_SPEC] * (2 * n) + [pl.BlockSpec(memory_space=pltpu.VMEM)],
                 input_output_aliases={i: 2 + i for i in range(2 * n)},
                 compiler_params=pltpu.CompilerParams(has_side_effects=EFFECT))(
                     *[_hbm(a) for a in gs], *[_hbm(a) for a in lands])
    return res[0], res[1], list(res[2:2 + n]), list(res[2 + n:2 + 2 * n]), res[-1]


def _rs_wait(name, gs, lands, send, recv, after):
    n = len(gs)

    def body(*refs):
        g, land = refs[:n], refs[n:2 * n]
        send_sem, recv_sem = refs[2 * n], refs[2 * n + 1]
        x, y, c = _place()
        me = 4 * x + 2 * y + c
        for wi in range(n):
            for k, to in enumerate(_peers(x, y, c)):
                _rs_copy(g[wi], land[wi], send_sem, recv_sem, wi, k, to, me).wait_send()
                _rs_copy(g[wi], land[wi], send_sem, recv_sem, wi, k, (x, y, c), 4 * to[0] + 2 * to[1] + to[2]).wait_recv()

    res = _pcall(body, name=name, out_shape=[pltpu.HBM(a.shape, a.dtype) for a in list(gs) + list(lands)],
                 in_specs=[HBM_SPEC] * (2 * n) + [SEM_SPEC, SEM_SPEC, ANY_SPEC], out_specs=[HBM_SPEC] * (2 * n),
                 input_output_aliases={i: i for i in range(2 * n)},
                 compiler_params=pltpu.CompilerParams(has_side_effects=EFFECT))(*gs, *lands, send, recv, after)
    return list(res[:n]), list(res[n:])


def _bc_copy(blk_ref, land_ref, send, recv, k, to, slot):
    return pltpu.make_async_remote_copy(src_ref=blk_ref, dst_ref=land_ref.at[slot], send_sem=send.at[k], recv_sem=recv.at[k],
                                        device_id=to, device_id_type=MESH)


def _bcast_start(name, blk):
    land = lax.empty((N_DEV,) + blk.shape, blk.dtype)

    def body(blk_ref, land_ref, send, recv, blk_thru, land_thru, token):
        x, y, c = _place()
        for k, to in enumerate(_peers(x, y, c)):
            _bc_copy(blk_ref, land_ref, send, recv, k, to, 4 * x + 2 * y + c).start()
        token[...] = jnp.zeros_like(token)

    return _pcall(body, name=name,
                  out_shape=[pltpu.SemaphoreType.DMA((7,)), pltpu.SemaphoreType.DMA((7,)), pltpu.HBM(blk.shape, blk.dtype),
                             pltpu.HBM(land.shape, land.dtype), jax.ShapeDtypeStruct((8, 128), F32)],
                  in_specs=[HBM_SPEC, HBM_SPEC], out_specs=[SEM_SPEC, SEM_SPEC, HBM_SPEC, HBM_SPEC, pl.BlockSpec(memory_space=pltpu.VMEM)],
                  input_output_aliases={0: 2, 1: 3}, compiler_params=pltpu.CompilerParams(has_side_effects=EFFECT))(_hbm(blk), _hbm(land))


def _bcast_wait(name, blk, land, send, recv, after):
    def body(blk_ref, land_ref, send_sem, recv_sem, after_ref, blk_thru, land_thru):
        x, y, c = _place()
        for k, to in enumerate(_peers(x, y, c)):
            _bc_copy(blk_ref, land_ref, send_sem, recv_sem, k, to, 4 * x + 2 * y + c).wait_send()
            _bc_copy(blk_ref, land_ref, send_sem, recv_sem, k, to, 4 * to[0] + 2 * to[1] + to[2]).wait_recv()

    return _pcall(body, name=name, out_shape=[pltpu.HBM(blk.shape, blk.dtype), pltpu.HBM(land.shape, land.dtype)],
                  in_specs=[HBM_SPEC, HBM_SPEC, SEM_SPEC, SEM_SPEC, ANY_SPEC], out_specs=[HBM_SPEC, HBM_SPEC],
                  input_output_aliases={0: 0, 1: 1}, compiler_params=pltpu.CompilerParams(has_side_effects=EFFECT))(
                      blk, land, send, recv, after)[1]


def _rs_sum(name, gs, lands):
    n = len(gs)

    def body(*refs):
        g_refs, land_refs, out_refs = refs[:n], refs[n:2 * n], refs[2 * n:3 * n]
        recvs = refs[3 * n:4 * n]
        local_sems, sib_send, sib_recv = refs[4 * n:]
        x, y, c = _place()
        me = 4 * x + 2 * y + c
        cps = []
        for wi in range(n):
            hr = g_refs[wi].shape[1] // 2
            own = g_refs[wi].at[2 * x + y, pl.ds(pl.multiple_of(c * hr, 16), hr), :]
            cps.append(pltpu.make_async_copy(own, recvs[wi].at[me], local_sems.at[8 * wi + 7]))
            for k, (tx, ty, tc) in enumerate(_peers(x, y, c)):
                slot = 4 * tx + 2 * ty + tc
                cps.append(pltpu.make_async_copy(land_refs[wi].at[slot], recvs[wi].at[slot], local_sems.at[8 * wi + k]))
        for cp in cps:
            cp.start()
        sibs = []
        for wi in range(n):
            hr = g_refs[wi].shape[1] // 2
            ch = _pick(hr, 64, 16)
            for cp in cps[8 * wi:8 * wi + 8]:
                cp.wait()
            base = pl.multiple_of(c * hr, 16)
            for r0 in range(0, hr, ch):
                acc = recvs[wi][0, r0:r0 + ch, :].astype(F32)
                for k in range(1, N_DEV):
                    acc = acc + recvs[wi][k, r0:r0 + ch, :].astype(F32)
                out_refs[wi][pl.ds(base + r0, ch), :] = acc
            half = out_refs[wi].at[pl.ds(base, hr), :]
            sib = pltpu.make_async_remote_copy(src_ref=half, dst_ref=half, send_sem=sib_send.at[wi], recv_sem=sib_recv.at[wi],
                                               device_id=(x, y, 1 - c), device_id_type=MESH)
            sib.start()
            sibs.append(sib)
        for wi in range(n):
            hr = g_refs[wi].shape[1] // 2
            other = out_refs[wi].at[pl.ds(pl.multiple_of((1 - c) * hr, 16), hr), :]
            pltpu.make_async_remote_copy(src_ref=other, dst_ref=other, send_sem=sib_send.at[wi], recv_sem=sib_recv.at[wi],
                                         device_id=(x, y, 1 - c), device_id_type=MESH).wait_recv()
            sibs[wi].wait_send()

    res = _pcall(
        body, name=name, out_shape=[jax.ShapeDtypeStruct(g.shape[1:], F32) for g in gs],
        in_specs=[ANY_SPEC] * (2 * n), out_specs=[pl.BlockSpec(memory_space=pltpu.VMEM)] * n,
        scratch_shapes=[pltpu.VMEM((N_DEV, g.shape[1] // 2, g.shape[2]), BF16) for g in gs]
        + [pltpu.SemaphoreType.DMA((8 * n,)), pltpu.SemaphoreType.DMA((n,)), pltpu.SemaphoreType.DMA((n,))],
        compiler_params=pltpu.CompilerParams(vmem_limit_bytes=VMEM_LIMIT))(*gs, *lands)
    return list(res)


def _gate_norm_epilogue(factor, x_in, gt, nxt, t, d):
    blk = (512, d)
    full = lambda i, j: (i, j)
    rowv = lambda i, j: (0, j)

    def epi(acc, res, scale, *norm):
        x_new = res + (factor * scale) * acc
        if not norm:
            return x_new, acc
        gv, scv, shv = norm
        r = lax.rsqrt(jnp.mean(x_new * x_new, axis=-1, keepdims=True) + EPS)
        return x_new, acc, x_new * r * gv * (1.0 + scv) + shv

    ins = [(x_in, blk, full), (gt, (1, d), rowv)] + [(v, (1, d), rowv) for v in (nxt or ())]
    outs = [((t, d), F32, blk, full), ((t, d), BF16, blk, full)] + ([((t, d), BF16, blk, full)] if nxt else [])
    return epi, ins, outs


_FULL = lambda i, j: (i, j)
_ROWV = lambda i, j: (0, j)


def _glu_epilogue(b_glu, t, w):
    def epi(acc, bv):
        g = acc + bv
        return acc, g[:, :w] * _sigmoid(g[:, w:])

    return epi, [(b_glu, (1, 2 * w), _ROWV)], [((t, 2 * w), BF16, (512, 2 * w), _FULL), ((t, w), BF16, (512, w), _FULL)]


def _glu_bwd_epilogue(gv, b_glu, t, w):
    def epi(dsg, gvv, bv):
        g = gvv.astype(F32) + bv
        val, s = g[:, :w], _sigmoid(g[:, w:])
        dval, dgate = dsg * s, dsg * val * s * (1.0 - s)
        return [jnp.concatenate([dval, dgate], axis=1)], [jnp.concatenate([_colsum(dval), _colsum(dgate)], axis=1)]

    return epi, [(gv, (512, 2 * w), _FULL), (b_glu, (1, 2 * w), _ROWV)], [((t, 2 * w), BF16, (512, 2 * w), _FULL)], [2 * w]


def _gates_epilogue(y_pool, uin, t, d, pw):
    cb = (2 * pw) // d

    def epi(acc, yp, glp, gls):
        return acc, _sigmoid(glp) * yp.astype(F32) + _sigmoid(gls) * acc

    ins = [(y_pool, (512, d), _FULL), (uin, (512, d), lambda i, j: (i, cb)), (uin, (512, d), lambda i, j: (i, cb + 1))]
    return epi, ins, [((t, d), BF16, (512, d), _FULL)] * 2


def _gates_bwd_epilogue(y_pool, y_ssm, uin, t, d, pw):
    cb = (2 * pw) // d

    def epi(dm, yp, ys, glp, gls):
        sp, ss = _sigmoid(glp), _sigmoid(gls)
        dgl = jnp.concatenate([dm * yp.astype(F32) * sp * (1.0 - sp), dm * ys.astype(F32) * ss * (1.0 - ss)], axis=1)
        return dm * sp, dm * ss, ("at", 2 * pw, dgl)

    ins = [(y_pool, (512, d), _FULL), (y_ssm, (512, d), _FULL),
           (uin, (512, d), lambda i, j: (i, cb)), (uin, (512, d), lambda i, j: (i, cb + 1))]
    wide = 2 * pw + 2 * d
    return epi, ins, [((t, d), BF16, (512, d), _FULL)] * 2 + [((t, wide), BF16, (512, wide), _FULL)]


def _loss_epilogue(x_in, gt, tgt, g_final, t, d):
    blk = (512, d)
    full = lambda i, j: (i, j)
    rowv = lambda i, j: (0, j)

    def epi(acc, res, scale, tv, gv):
        xv = res + (0.5 * scale) * acc
        r = lax.rsqrt(jnp.mean(xv * xv, axis=-1, keepdims=True) + EPS)
        xr = xv * r
        e = xr * gv - tv
        loss_row = 0.5 * jnp.mean(e * e, axis=-1, keepdims=True)
        dout = e * (1.0 / d)
        gd = gv * dout
        dx = r * (gd - xr * jnp.mean(gd * xr, axis=-1, keepdims=True))
        fdx = 0.5 * dx
        return [dx, scale * fdx], [_colsum(dout * xr), _colsum(loss_row * jnp.ones((1, 128), F32)), _colsum(fdx * acc)]

    ins = [(x_in, blk, full), (gt, (1, d), rowv), (tgt, blk, full), (g_final, (1, d), rowv)]
    return epi, ins, [((t, d), F32, blk, full), ((t, d), BF16, blk, full)], [d, 128, d]


def _norm_bwd_epilogue(x, dres, g, sc, up, t, d):
    blk = (512, d)
    full = lambda i, j: (i, j)
    rowv = lambda i, j: (0, j)

    def epi(dhv, xv, drv, gv, scv, *rest):
        r = lax.rsqrt(jnp.mean(xv * xv, axis=-1, keepdims=True) + EPS)
        xr = xv * r
        dn = dhv * (1.0 + scv)
        gd = gv * dn
        dx = drv + r * (gd - xr * jnp.mean(gd * xr, axis=-1, keepdims=True))
        outs, reds = [dx], [_colsum(dhv), _colsum(dhv * xr * gv), _colsum(dn * xr)]
        if up:
            yv, gtv = rest
            fdx = up[2] * dx
            outs.append(gtv * fdx)
            reds.append(_colsum(fdx * yv.astype(F32)))
        return outs, reds

    ins = [(x, blk, full), (dres, blk, full), (g, (1, d), rowv), (sc, (1, d), rowv)]
    ins += [(up[0], blk, full), (up[1], (1, d), rowv)] if up else []
    outs = [((t, d), F32, blk, full)] + ([((t, d), BF16, blk, full)] if up else [])
    return epi, ins, outs, [d] * (4 if up else 3)


def _ffn_in_act(name, h, w_in, tm=512):
    t, d = h.shape
    ns = w_in.shape[2]
    tm = _pick(t, tm, 8)
    w4 = w_in.reshape(2, 2, d, ns)

    def body(h_ref, w_ref, ab_ref, act_ref):
        hv = h_ref[...]
        a = jnp.dot(hv, w_ref[0], preferred_element_type=F32)
        b = jnp.dot(hv, w_ref[1], preferred_element_type=F32)
        ab_ref[0] = a.astype(BF16)
        ab_ref[1] = b.astype(BF16)
        act_ref[...] = (a * _sigmoid(a) * b).astype(BF16)

    return _pcall(body, name=name, grid=(2, t // tm),
                  in_specs=[pl.BlockSpec((tm, d), lambda c, i: (i, 0)), pl.BlockSpec((2, None, d, ns), lambda c, i: (0, c, 0, 0))],
                  out_specs=[pl.BlockSpec((2, tm, ns), lambda c, i: (0, i, c)), pl.BlockSpec((tm, ns), lambda c, i: (i, c))],
                  out_shape=[jax.ShapeDtypeStruct((2, t, 2 * ns), BF16), jax.ShapeDtypeStruct((t, 2 * ns), BF16)],
                  compiler_params=_params("parallel", "parallel"))(h, w4)


def _dswiglu_epilogue(ab, t, f, tn):
    blk = (2, 512, tn)
    idx = lambda i, j: (0, i, j)

    def epi(dact, abv):
        a, b = abv[0].astype(F32), abv[1].astype(F32)
        s = _sigmoid(a)
        return (jnp.stack([dact * b * (s * (1.0 + a * (1.0 - s))), dact * (a * s)]),)

    return epi, [(ab, blk, idx)], [((2, t, f), BF16, blk, idx)]


def _ffn_fwd(tag, x, h, gt, get_w_in, get_w_out, nxt=None, loss=None):
    t, d = x.shape
    ab, act = _ffn_in_act(tag + "_in", h, get_w_in(h))
    if loss:
        epi, epi_ins, epi_outs, epi_reds = _loss_epilogue(x, gt, *loss, t, d)
        res = _mm(tag + "_out", act, get_w_out(act), tm=512, tn=d, epi=epi, epi_ins=epi_ins, epi_outs=epi_outs,
                  epi_reds=epi_reds, b_resident=True)
        return res, (x, h, ab, act, None)
    epi, epi_ins, epi_outs = _gate_norm_epilogue(0.5, x, gt, nxt, t, d)
    res = _mm(tag + "_out", act, get_w_out(act), tm=512, tn=d, epi=epi, epi_ins=epi_ins, epi_outs=epi_outs, b_resident=True)
    return res[0], res[2], (x, h, ab, act, res[1])


def _ffn_bwd(tag, dx_new, dy, saved, g, sc, w_in, w_out, start_rs, up, deps=()):
    x, h, ab, act, _ = saved
    d = x.shape[1]
    f = act.shape[1]
    dw_out = _mm(tag + "_dwout", act, dy, ta=True, tm=1408, tn=d, tk=2048, deps=deps)
    tok = start_rs("out", dw_out.reshape(N_CHIPS, f // N_CHIPS, d))
    epi, epi_ins, epi_outs = _dswiglu_epilogue(ab, x.shape[0], f, w_in.shape[2])
    dab = _mm(tag + "_dact", dy, w_out, tb=True, tm=512, tn=w_in.shape[2], epi=epi, epi_ins=epi_ins, epi_outs=epi_outs,
              deps=(tok,), j_outer=True)
    dw_in = _mm(tag + "_dwin", h, dab, ta=True, out_stacked=True, b_halves=True, tm=d, tn=1408, tk=2048)
    tok = start_rs("in", dw_in)
    epi, epi_ins, epi_outs, epi_reds = _norm_bwd_epilogue(x, dx_new, g, sc, up, x.shape[0], d)
    res = _mm(tag + "_dh", dab, w_in, tb=True, b_stacked=True, a_halves=True, tn=d, tk=5632, deps=(tok,), epi=epi,
              epi_ins=epi_ins, epi_outs=epi_outs, epi_reds=epi_reds, b_resident=True)
    if up:
        return res
    return res[0], None, res[1], res[2], res[3], None


def _row(v):
    return v.reshape(1, -1)


def _pack(parts):
    cols = []
    for p in parts:
        flat = p.reshape(-1).astype(F32)
        padn = (-flat.shape[0]) % 128
        cols.append(jnp.pad(flat, (0, padn)) if padn else flat)
    flat = jnp.concatenate(cols)
    padn = (-flat.shape[0]) % 1024
    if padn:
        flat = jnp.pad(flat, (0, padn))
    return flat.reshape(-1, 128)


def _unpack(packed, shapes):
    flat = packed.reshape(-1)
    out, off = [], 0
    for s in shapes:
        n = math.prod(s)
        out.append(flat[off:off + n].reshape(s))
        off += n + ((-n) % 128)
    return out


SMALL = ['b_ada', 'g_ffn1', 'g_mix', 'pool_w', 'pool_b', 'pool_scale', 'ssm_lam_re_log', 'ssm_lam_im', 'ssm_log_dt',
         'ssm_b_re', 'ssm_b_im', 'ssm_c_re', 'ssm_c_im', 'ssm_d', 'b_glu', 'g_ffn2', 'g_final']
BIG = ['w_ffn1_in', 'w_ffn1_out', 'w_in', 'w_pool_up', 'w_glu', 'w_ssm_up', 'w_out', 'w_ffn2_in', 'w_ffn2_out']
AG_GROUPS = [[0], [1], [2, 3, 4, 5, 6], [7, 8]]
AG_DIRECT = [False, False, False, True]
SMALL_LATE = ['b_ada', 'g_ffn1']
RS_SUM_GROUPS = [[0, 1], [2, 3], [4, 5]]
WEIGHTS = ['w_ada', 'b_ada', 'g_ffn1', 'w_ffn1_in', 'w_ffn1_out', 'g_mix', 'w_in', 'pool_w', 'pool_b', 'pool_scale', 'w_pool_up',
           'ssm_lam_re_log', 'ssm_lam_im', 'ssm_log_dt', 'ssm_b_re', 'ssm_b_im', 'ssm_c_re', 'ssm_c_im', 'ssm_d', 'w_glu', 'b_glu',
           'w_ssm_up', 'w_out', 'g_ffn2', 'w_ffn2_in', 'w_ffn2_out', 'g_final']


def kernel(x, c, w_ada, b_ada, g_ffn1, w_ffn1_in, w_ffn1_out, g_mix, w_in, pool_w, pool_b, pool_scale, w_pool_up, ssm_lam_re_log, ssm_lam_im, ssm_log_dt, ssm_b_re, ssm_b_im, ssm_c_re, ssm_c_im, ssm_d, w_glu, b_glu, w_ssm_up, w_out, g_ffn2, w_ffn2_in, w_ffn2_out, g_final, loss_target, m_w_ada, m_b_ada, m_g_ffn1, m_w_ffn1_in, m_w_ffn1_out, m_g_mix, m_w_in, m_pool_w, m_pool_b, m_pool_scale, m_w_pool_up, m_ssm_lam_re_log, m_ssm_lam_im, m_ssm_log_dt, m_ssm_b_re, m_ssm_b_im, m_ssm_c_re, m_ssm_c_im, m_ssm_d, m_w_glu, m_b_glu, m_w_ssm_up, m_w_out, m_g_ffn2, m_w_ffn2_in, m_w_ffn2_out, m_g_final, v_w_ada, v_b_ada, v_g_ffn1, v_w_ffn1_in, v_w_ffn1_out, v_g_mix, v_w_in, v_pool_w, v_pool_b, v_pool_scale, v_w_pool_up, v_ssm_lam_re_log, v_ssm_lam_im, v_ssm_log_dt, v_ssm_b_re, v_ssm_b_im, v_ssm_c_re, v_ssm_c_im, v_ssm_d, v_w_glu, v_b_glu, v_w_ssm_up, v_w_out, v_g_ffn2, v_w_ffn2_in, v_w_ffn2_out, v_g_final):
    args = dict(locals())
    wt = {n: args[n] for n in WEIGHTS}
    mom = {n: args["m_" + n] for n in WEIGHTS}
    var = {n: args["v_" + n] for n in WEIGHTS}

    t, d = x.shape[1], x.shape[2]
    pw = pool_b.shape[1]
    sw = ssm_d.shape[1]
    ngrp = sw // SSM_GROUP
    gn = ngrp * SSM_STATE
    xi, yi, ci = _place()
    b_me = 4 * xi + 2 * yi + ci
    s_me = 2 * xi + yi
    x2d = x[0]
    tgt = loss_target[0]

    c_all = _allgather_small("ag_c", c.reshape(8, d // 8)).reshape(N_DEV, d)
    ncol = w_ada.shape[2]
    b_sh = lax.dynamic_slice(b_ada, (0, s_me * ncol), (1, ncol))
    mod_sh = _ada_fwd("ada_fwd", c_all, w_ada[0], b_sh)
    md_send, md_recv, mod_sh, md_land, tok = _bcast_start("mod_start", mod_sh)

    shards = [wt[n][0].astype(BF16) for n in BIG]
    ag_sems, shards_t, lands_t, tok = _ag_start("ag_start", shards, _place_shards(shards, s_me), AG_GROUPS, AG_DIRECT, deps=(tok,))
    md_land = _bcast_wait("mod_wait", mod_sh, md_land, md_send, md_recv, tok)
    mod_all = lax.dynamic_update_slice(md_land, mod_sh[None], (b_me, 0, 0))
    full = {}

    def weights(gi, *after):
        grp = AG_GROUPS[gi]
        if BIG[grp[0]] not in full:
            ls = _ag_wait("ag_wait%d" % gi, [shards_t[w] for w in grp], [lands_t[w] for w in grp], *ag_sems[gi],
                          AG_DIRECT[gi], after)
            for w, l in zip(grp, ls if AG_DIRECT[gi] else _ag_forward("ag_fwd%d" % gi, ls)):
                full[BIG[w]] = l
        return full

    mod_me = jnp.concatenate([lax.dynamic_slice(mod_all, (2 * s, b_me, 0), (1, 1, ncol))[0] for s in range(N_CHIPS)], axis=1)
    mod = mod_me.reshape(9, d)
    sh1, sc1, gt1, sh2, sc2, gt2, sh3, sc3, gt3 = [mod[k:k + 1] for k in range(9)]

    f = w_ffn1_out.shape[1] * N_CHIPS

    col = lambda a: a.reshape(gn, 1)
    lrl_c, li_c = col(ssm_lam_re_log), col(ssm_lam_im)
    ldt_c = col(jnp.broadcast_to(ssm_log_dt.reshape(ngrp, 1), (ngrp, SSM_STATE)))
    b_re2, b_im2 = ssm_b_re.reshape(gn, SSM_GROUP), ssm_b_im.reshape(gn, SSM_GROUP)
    lrdt_c, ang_c, bb_re, bb_im = _ssm_prep("ssm_prep", lrl_c, li_c, ldt_c, b_re2, b_im2)
    lrdt, ang = lrdt_c.reshape(1, gn), ang_c.reshape(1, gn)
    tr = lambda a: jnp.swapaxes(a, 1, 2)
    bbd = [_blockdiag_b(v, sw).astype(BF16) for v in (bb_re, bb_im)]
    ccd = [_blockdiag_c(v[0], sw).astype(BF16) for v in (ssm_c_re, ssm_c_im)]
    bbd_t, ccd_t = [tr(v) for v in bbd], [tr(v) for v in ccd]
    early = [n for n in SMALL if n not in SMALL_LATE]
    packs = {}
    for tag, names, extra in (("early", early, []), ("late", SMALL_LATE, [jnp.zeros((128,), F32)])):
        packs[tag] = [_pack([src[n] for n in names] + extra) for src in (wt, mom, var)]
    shadow = [lrdt, ang, ldt_c, *bbd, *ccd, *bbd_t, *ccd_t, *packs["early"], *packs["late"]]

    h1 = _norm_fwd("ffn1_norm", x2d, g_ffn1, sc1, sh1, deps=(tok,))
    x1, h2, sav1 = _ffn_fwd("ffn1", x2d, h1, gt1, lambda after: weights(0, after, *shadow)['w_ffn1_in'],
                            lambda after: weights(1, after)['w_ffn1_out'].reshape(f, d), (g_mix, sc2, sh2))
    weights(2, h2)
    wo = full['w_out'].reshape(d, d)
    uin = _mm("mix_in", h2, full['w_in'], b_stacked=True, tm=1024, tn=768, out_dtype=F32, j_outer=True)
    p_pool, z_pool = _pool_fwd("pool_fwd", uin, pool_w[0], pool_b, pool_scale, pw)
    y_pool = _mm("pool_up", p_pool, full['w_pool_up'], b_stacked=True, tm=1024, tn=d)
    y_s, ge, s_re, s_im = _ssm_fwd("ssm_fwd", uin, lrdt, ang, *bbd, *ccd, ssm_d, sw)
    epi, epi_ins, epi_outs = _glu_epilogue(b_glu, t, sw)
    gv, sg = _mm("glu_in", ge, full['w_glu'], b_stacked=True, tm=512, tn=d, epi=epi, epi_ins=epi_ins, epi_outs=epi_outs,
                 b_resident=True)
    epi, epi_ins, epi_outs = _gates_epilogue(y_pool, uin, t, d, pw)
    y_ssm, merged = _mm("ssm_up", sg, full['w_ssm_up'], b_stacked=True, tm=512, tn=d, epi=epi, epi_ins=epi_ins,
                        epi_outs=epi_outs, b_resident=True)
    epi, epi_ins, epi_outs = _gate_norm_epilogue(1.0, x1, gt2, (g_ffn2, sc3, sh3), t, d)
    x2, y2, h3 = _mm("mix_out", merged, wo, tm=512, tn=d, epi=epi, epi_ins=epi_ins, epi_outs=epi_outs, b_resident=True)

    (dx3, dy3, dg_final, loss_v, dgt3), sav3 = _ffn_fwd(
        "ffn2", x2, h3, gt3, lambda after: weights(3, after)['w_ffn2_in'],
        lambda after: weights(3, after)['w_ffn2_out'].reshape(f, d), loss=(tgt, _row(g_final)))

    gs = {}
    rs_open = []

    def start_rs(names, gbs):
        send, recv, g_thru, land_thru, token = _rs_start("rs_start_" + names[0], gbs)
        rs_open.append((names, send, recv, g_thru, land_thru))
        return token

    dx2, dy2, dsh3, dsc3, gs['g_ffn2'], dgt2 = _ffn_bwd(
        "ffn2b", dx3, dy3, sav3, g_ffn2, sc3, full['w_ffn2_in'], full['w_ffn2_out'].reshape(f, d),
        lambda key, g: start_rs(['w_ffn2_' + key], [g]), (y2, gt2, 1.0))

    epi, epi_ins, epi_outs = _gates_bwd_epilogue(y_pool, y_ssm, uin, t, d, pw)
    dy_pool, dy_ssm, duin = _mm("mix_dmerged", dy2, wo, tb=True, tm=512, tn=d, epi=epi, epi_ins=epi_ins, epi_outs=epi_outs,
                                b_resident=True)
    g_wo = _mm("mix_dwout", merged, dy2, ta=True, tm=d, tn=d, tk=2048).reshape(N_CHIPS, d // N_CHIPS, d)

    dp = _mm("pool_dup", dy_pool, full['w_pool_up'], tb=True, b_stacked=True, tm=1024, tn=pw, tk=d)
    g_wpu = _mm("pool_dwup", p_pool, dy_pool, ta=True, out_stacked=True, tm=pw, tn=d, tk=4096)
    duin, gs['pool_w'], gs['pool_b'], gs['pool_scale'] = _pool_bwd("pool_bwd", dp, z_pool, pool_w[0], pool_b, pool_scale, duin)

    g_wsu = _mm("ssm_dwup", sg, dy_ssm, ta=True, out_stacked=True, tm=sw, tn=d, tk=4096)
    tok = start_rs(['w_out', 'w_pool_up', 'w_ssm_up'], [g_wo, g_wpu, g_wsu])
    epi, epi_ins, epi_outs, epi_reds = _glu_bwd_epilogue(gv, b_glu, t, sw)
    dgv, gs['b_glu'] = _mm("ssm_dup", dy_ssm, full['w_ssm_up'], tb=True, b_stacked=True, tm=512, tn=sw, tk=d, deps=(tok,),
                           epi=epi, epi_ins=epi_ins, epi_outs=epi_outs, epi_reds=epi_reds, b_resident=True)
    dge = _mm("glu_dge", dgv, full['w_glu'], tb=True, b_stacked=True, tm=1024, tn=sw, tk=d)
    g_wglu = _mm("glu_dw", ge, dgv, ta=True, out_stacked=True, tm=sw, tn=d, tk=4096)
    (duin, g_abre, g_abim, g_bbd_re, g_bbd_im, g_ccd_re, g_ccd_im, gs['ssm_d']) = _ssm_bwd(
        "ssm_bwd", dge, y_s, uin, s_re, s_im, lrdt, ang, *bbd_t, *ccd_t, ssm_d, sw, duin)
    gs['ssm_c_re'], gs['ssm_c_im'] = _diag_of_c(g_ccd_re, sw), _diag_of_c(g_ccd_im, sw)
    d_lrl, d_li, d_ldt, d_bre, d_bim = _ssm_param_bwd(
        "ssm_param_bwd", lrl_c, li_c, ldt_c, b_re2, b_im2, g_abre.reshape(gn, 1), g_abim.reshape(gn, 1),
        _diag_of_b(g_bbd_re, sw), _diag_of_b(g_bbd_im, sw))
    gs['ssm_lam_re_log'], gs['ssm_lam_im'] = d_lrl, d_li
    gs['ssm_log_dt'] = jnp.sum(d_ldt.reshape(ngrp, SSM_STATE), axis=1)
    gs['ssm_b_re'], gs['ssm_b_im'] = d_bre, d_bim

    g_win =_mm("mix_dwin", h2, duin, ta=True, out_stacked=True, tm=d, tn=768, tk=4096)
    tok = start_rs(['w_glu', 'w_in'], [g_wglu, g_win])
    epi, epi_ins, epi_outs, epi_reds = _norm_bwd_epilogue(x1, dx2, g_mix, sc2, (sav1[4], gt1, 0.5), t, d)
    dx1, dy1, dsh2, dsc2, gs['g_mix'], dgt1 = _mm(
        "mix_dh", duin, full['w_in'], tb=True, b_stacked=True, tn=d, tk=3072, deps=(tok,), epi=epi, epi_ins=epi_ins,
        epi_outs=epi_outs, epi_reds=epi_reds, b_resident=True)

    gs['g_final'] = dg_final
    sg_blk = _pack([gs[n] for n in early])
    sg_send, sg_recv, sg_blk, sg_land, tok = _bcast_start("sg_start", sg_blk)

    dx0, _, dsh1, dsc1, gs['g_ffn1'], _ = _ffn_bwd(
        "ffn1b", dx1, dy1, sav1, g_ffn1, sc1, full['w_ffn1_in'], full['w_ffn1_out'].reshape(f, d),
        lambda key, g: start_rs(['w_ffn1_' + key], [g]), None, deps=(tok,))

    gs['b_ada'] = jnp.concatenate([dsh1, dsc1, dgt1, dsh2, dsc2, dgt2, dsh3, dsc3, dgt3], axis=1)
    lt_blk = _pack([gs[n] for n in SMALL_LATE] + [loss_v])
    lt_send, lt_recv, lt_blk, lt_land, tok = _bcast_start("late_start", lt_blk)

    grads, delta, new_m, new_v = {}, {}, {}, {}

    def small_update(tag, names, blk, land):
        g8 = lax.dynamic_update_slice(land, blk[None], (b_me, 0, 0)).reshape(-1, 128)
        w_pack, m_pack, v_pack = packs[tag]
        shapes = [wt[n].shape for n in names]
        per_param, packed = _adamw_small("adamw_small_" + tag, w_pack, g8, m_pack, v_pack, shapes)
        for k, dst in enumerate((grads, delta, new_m, new_v)):
            rest = _unpack(packed[k], shapes)
            for i, n in enumerate(names):
                dst[n] = per_param[i][k] if per_param[i] is not None else rest[i]
        return g8

    after = tok
    for group in RS_SUM_GROUPS:
        names, g_done, land_done = [], [], []
        for k in group:
            nk, send, recv, g_thru, land_thru = rs_open[k]
            gd, ld = _rs_wait("rs_wait_" + nk[0], g_thru, land_thru, send, recv, after)
            names, g_done, land_done = names + nk, g_done + gd, land_done + ld
        for n, g_sum in zip(names, _rs_sum("rs_sum_" + names[0], g_done, land_done)):
            g_out, dl, mn, vn = _adamw("adamw_" + n, wt[n][0], g_sum, mom[n][0], var[n][0])
            grads[n], delta[n], new_m[n], new_v[n] = g_out[None], dl[None], mn[None], vn[None]
            after = dl
        if group is RS_SUM_GROUPS[-2]:
            small_update("early", early, sg_blk, _bcast_wait("sg_wait", sg_blk, sg_land, sg_send, sg_recv, after))

    lt_land = _bcast_wait("late_wait", lt_blk, lt_land, lt_send, lt_recv, after)
    late8 = small_update("late", SMALL_LATE, lt_blk, lt_land).reshape(N_DEV, -1)
    loss = jnp.sum(late8[:, 10 * d])
    dmod_sh = lax.dynamic_slice(late8, (0, s_me * ncol), (N_DEV, ncol))
    g_w_ada = _ada_bwd("ada_bwd", c_all, dmod_sh)
    dl, mn, vn = _adamw("adamw_w_ada", w_ada[0], g_w_ada, m_w_ada[0], v_w_ada[0], with_g=False)
    grads['w_ada'], delta['w_ada'], new_m['w_ada'], new_v['w_ada'] = g_w_ada[None], dl[None], mn[None], vn[None]

    return (loss, dx0[None], *[grads[n] for n in WEIGHTS], *[delta[n] for n in WEIGHTS],
            *[new_m[n] for n in WEIGHTS], *[new_v[n] for n in WEIGHTS])
```

```python
import functools
import math

import jax
import jax.numpy as jnp
from jax import lax
from jax.experimental import pallas as pl
from jax.experimental.pallas import tpu as pltpu

F32 = jnp.float32
BF16 = jnp.bfloat16
MESH = pl.DeviceIdType.MESH

EPS = 1e-6
POOL_WINDOWS = (2, 4, 8, 16)
POOL_HALO = 16
SSM_GROUP = 16
SSM_STATE = 64
SSM_BLOCKS = 4
N_DEV = 8
N_CHIPS = 4
ADAM_LR = 0.001
ADAM_B1 = 0.9
ADAM_B2 = 0.999
ADAM_EPS = 1e-08
ADAM_WD = 0.01
ADAM_STEP = 10
VMEM_LIMIT = 56 * 1024 * 1024


ANY_SPEC = pl.BlockSpec(memory_space=pl.ANY)
HBM_SPEC = pl.BlockSpec(memory_space=pltpu.HBM)
SEM_SPEC = pl.BlockSpec(memory_space=pltpu.SEMAPHORE)
EFFECT = pltpu.SideEffectType.DATAFLOW_SIDE_EFFECTING


def _hbm(a):
    return pltpu.with_memory_space_constraint(a, pltpu.HBM)


def _pcall(body, **kw):
    return pl.pallas_call(body, **kw)


def _params(*sem):
    return pltpu.CompilerParams(dimension_semantics=sem, vmem_limit_bytes=VMEM_LIMIT)


def _pick(n, cap, mult=128):
    if n <= cap:
        return n
    best = None
    for d in range(mult, cap + 1, mult):
        if n % d == 0:
            best = d
    assert best is not None, (n, cap, mult)
    return best


def _sigmoid(v):
    return 1.0 / (1.0 + jnp.exp(-v))


def _sigmoid_fast(v):
    return pl.reciprocal(1.0 + jnp.exp(-v), approx=True)


def _rowwise(name, fn, ins, params, outs, reds, tm, deps=()):
    t = ins[0][0].shape[0]
    tm = min(tm, t)
    nb = t // tm
    ni, npar, no, nd = len(ins), len(params), len(outs), len(deps)

    def body(*refs):
        iv = [r[...] for r in refs[:ni]]
        pv = [r[...] for r in refs[ni:ni + npar]]
        o_refs = refs[ni + npar + nd:ni + npar + nd + no]
        r_refs = refs[ni + npar + nd + no:]
        ovals, rvals = fn(iv, pv)
        for o_ref, val in zip(o_refs, ovals):
            off = 0
            if isinstance(val, tuple) and val[0] == "at":
                _, off, val = val
            parts = val if isinstance(val, (list, tuple)) else [val]
            for p in parts:
                o_ref[:, off:off + p.shape[1]] = p.astype(o_ref.dtype)
                off += p.shape[1]
        if r_refs:
            @pl.when(pl.program_id(0) == 0)
            def _():
                for r in r_refs:
                    r[...] = jnp.zeros_like(r)
            for r, val in zip(r_refs, rvals):
                r[...] += val

    in_specs = [pl.BlockSpec((tm, w), functools.partial(lambda i, cb: (i, cb), cb=cb)) for (_, w, cb) in ins]
    in_specs += [pl.BlockSpec(p.shape, lambda i: (0, 0)) for p in params]
    in_specs += [ANY_SPEC] * nd
    out_shape = [jax.ShapeDtypeStruct((t, w), dt) for (w, dt) in outs]
    out_shape += [jax.ShapeDtypeStruct((1, w), F32) for w in reds]
    out_specs = [pl.BlockSpec((tm, w), lambda i: (i, 0)) for (w, _) in outs]
    out_specs += [pl.BlockSpec((1, w), lambda i: (0, 0)) for w in reds]
    res = _pcall(body, name=name, grid=(nb,), in_specs=in_specs, out_specs=out_specs, out_shape=out_shape,
                 compiler_params=_params("arbitrary"))(*[a for a, _, _ in ins], *params, *deps)
    return res


def _colsum(v):
    return jnp.sum(v, axis=0, keepdims=True)


def _mm(name, a, b, *, ta=False, tb=False, b_stacked=False, out_stacked=False, tm=512, tn=1024, tk=2816,
        out_dtype=BF16, epi=None, epi_ins=(), epi_outs=None, epi_reds=(), deps=(), j_outer=False, a_halves=False,
        b_halves=False, b_resident=False):
    if a_halves:
        _, m, kdim = a.shape
        kdim *= 2
    elif ta:
        kdim, m = a.shape
    else:
        m, kdim = a.shape
    ns = None
    if b_stacked:
        ns = b.shape[2]
        n = b.shape[1] if tb else N_CHIPS * ns
        assert kdim == (N_CHIPS * ns if tb else b.shape[1]), (name, a.shape, b.shape)
    else:
        if b_halves:
            n = 2 * b.shape[2]
            assert kdim == b.shape[1] and not tb, (name, a.shape, b.shape)
        else:
            n = b.shape[0] if tb else b.shape[1]
            assert kdim == (b.shape[1] if tb else b.shape[0]), (name, a.shape, b.shape)
        if out_stacked:
            ns = n // N_CHIPS

    def shards(want):
        return max(g for g in (1, 2, 4) if g * ns <= max(want, ns))

    tm = _pick(m, tm, 128 if ta else 8)
    gn = gk = 1
    if (b_stacked and not tb) or out_stacked:
        gn = shards(min(tn, n // 2) if b_halves else tn)
        tn = gn * ns
    else:
        tn = _pick(n, tn)
    if b_stacked and tb:
        gk = shards(tk)
        tk = gk * ns
    else:
        tk = _pick(kdim, tk, 8 if ta else 128)
    nm, nn, nk = m // tm, n // tn, kdim // tk

    def ij(f):
        return (lambda g0, g1, k: f(g1, g0, k)) if j_outer else f

    if a_halves:
        assert b_stacked and tb and gk == N_CHIPS and nk == 1, name
        a_spec = pl.BlockSpec((2, tm, kdim // 2), ij(lambda i, j, k: (0, i, 0)))
    elif ta:
        a_spec = pl.BlockSpec((tk, tm), ij(lambda i, j, k: (k, i)))
    else:
        a_spec = pl.BlockSpec((tm, tk), ij(lambda i, j, k: (i, k)))
    if b_stacked and not tb:
        b_spec = pl.BlockSpec((gn, tk, ns), ij(lambda i, j, k: (j, k, 0)))
    elif b_stacked and tb:
        b_spec = pl.BlockSpec((gk, tn, ns), ij(lambda i, j, k: (k, j, 0)))
    elif b_halves:
        bph = (n // 2) // tn
        b_spec = pl.BlockSpec((None, tk, tn), ij(lambda i, j, k: (j // bph, k, j % bph)))
    elif tb:
        b_spec = pl.BlockSpec((tn, tk), ij(lambda i, j, k: (j, k)))
    else:
        b_spec = pl.BlockSpec((tk, tn), ij(lambda i, j, k: (k, j)))
    dims = (((0 if ta else 1,), (1 if tb else 0,)), ((), ()))

    if epi_outs is None:
        if out_stacked:
            epi_outs = [((N_CHIPS, m, ns), out_dtype, (gn, tm, ns), lambda i, j: (j, i, 0))]
        else:
            epi_outs = [((m, n), out_dtype, (tm, tn), lambda i, j: (i, j))]
    ne, no, nd, nr = len(epi_ins), len(epi_outs), len(deps), len(epi_reds)
    assert not nr or (nn == 1 and not j_outer), name

    def body(a_ref, b_ref, *rest):
        e_refs = rest[:ne]
        o_refs = rest[ne + nd:ne + nd + no]
        r_refs = rest[ne + nd + no:ne + nd + no + nr]
        scratch = rest[ne + nd + no + nr:]
        av = None if a_halves else a_ref[...].astype(BF16)
        if b_stacked and not tb:
            parts = [lax.dot_general(av, b_ref[s].astype(BF16), dims, preferred_element_type=F32) for s in range(gn)]
        elif b_stacked and tb:
            p = None
            for s in range(gk):
                if a_halves:
                    a_s = a_ref[s // 2, :, (s % 2) * ns:(s % 2 + 1) * ns].astype(BF16)
                else:
                    a_s = av[:, s * ns:(s + 1) * ns]
                q = lax.dot_general(a_s, b_ref[s].astype(BF16), dims, preferred_element_type=F32)
                p = q if p is None else p + q
            parts = [p]
        else:
            parts = [lax.dot_general(av, b_ref[...].astype(BF16), dims, preferred_element_type=F32)]

        def finish(acc_parts):
            if epi is None and out_stacked:
                acc = acc_parts[0]
                for s in range(gn):
                    o_refs[0][s] = acc[:, s * ns:(s + 1) * ns].astype(out_dtype)
            elif epi is None:
                w = acc_parts[0].shape[1]
                for s, part in enumerate(acc_parts):
                    o_refs[0][:, s * w:(s + 1) * w] = part.astype(out_dtype)
            else:
                acc = acc_parts[0] if len(acc_parts) == 1 else jnp.concatenate(acc_parts, axis=1)
                vals = epi(acc, *[r[...] for r in e_refs])
                if nr:
                    vals, reds = vals

                    @pl.when(pl.program_id(0) == 0)
                    def _():
                        for r in r_refs:
                            r[...] = jnp.zeros_like(r)
                    for r, val in zip(r_refs, reds):
                        r[...] += val
                for o_ref, val in zip(o_refs, vals):
                    if isinstance(val, tuple) and val[0] == "at":
                        o_ref[:, val[1]:val[1] + val[2].shape[1]] = val[2].astype(o_ref.dtype)
                    else:
                        o_ref[...] = val.astype(o_ref.dtype)

        if nk == 1:
            finish(parts)
        else:
            acc_ref = scratch[0]
            k = pl.program_id(2)
            w = parts[0].shape[1]

            @pl.when(k == 0)
            def _():
                for s, part in enumerate(parts):
                    acc_ref[:, s * w:(s + 1) * w] = part

            @pl.when(k > 0)
            def _():
                for s, part in enumerate(parts):
                    acc_ref[:, s * w:(s + 1) * w] += part

            @pl.when(k == nk - 1)
            def _():
                finish([acc_ref[...]])

    def _ij(f):
        return ij(lambda i, j, k: f(i, j))

    if b_resident:
        assert nk == 1 and (nn == 1 or j_outer), name
        b_spec = pl.BlockSpec(b_spec.block_shape, b_spec.index_map, pipeline_mode=pl.Buffered(1))
    in_specs = [a_spec, b_spec] + [pl.BlockSpec(blk, _ij(f)) for (_, blk, f) in epi_ins] + [ANY_SPEC] * nd
    out_specs = [pl.BlockSpec(blk, _ij(f)) for (_, _, blk, f) in epi_outs]
    out_specs += [pl.BlockSpec((1, w), lambda *_: (0, 0)) for w in epi_reds]
    out_shape = [jax.ShapeDtypeStruct(s, dt) for (s, dt, _, _) in epi_outs] + [jax.ShapeDtypeStruct((1, w), F32) for w in epi_reds]
    scratch = [pltpu.VMEM((tm, tn), F32)] if nk > 1 else []
    grid = (nn, nm, nk) if j_outer else (nm, nn, nk)
    res = _pcall(body, name=name, grid=grid, in_specs=in_specs, out_specs=out_specs, out_shape=out_shape, scratch_shapes=scratch,
                 compiler_params=_params(*(("arbitrary",) * 3 if nr else ("parallel", "parallel", "arbitrary"))))(
                     a, b, *[x for x, _, _ in epi_ins], *deps)
    return res[0] if len(res) == 1 else res


def _norm_fwd(name, x, g, sc, sh, deps=()):
    d = x.shape[1]

    def fn(iv, pv):
        (xv,), (gv, scv, shv) = iv, pv
        r = lax.rsqrt(jnp.mean(xv * xv, axis=-1, keepdims=True) + EPS)
        return [xv * r * gv * (1.0 + scv) + shv], []

    return _rowwise(name, fn, [(x, d, 0)], [g, sc, sh], [(d, BF16)], [], 512, deps=deps)[0]


def _norm_bwd(name, dh, x, dres, g, sc, y_up, gt_up, factor_up):
    d = x.shape[1]
    up = y_up is not None

    def fn(iv, pv):
        dhv, xv, drv = iv[:3]
        gv, scv = pv[:2]
        dhv = dhv.astype(F32)
        r = lax.rsqrt(jnp.mean(xv * xv, axis=-1, keepdims=True) + EPS)
        xr = xv * r
        dn = dhv * (1.0 + scv)
        gd = gv * dn
        dx = drv + r * (gd - xr * jnp.mean(gd * xr, axis=-1, keepdims=True))
        outs, reds = [dx], [_colsum(dhv), _colsum(dhv * xr * gv), _colsum(dn * xr)]
        if up:
            fdx = factor_up * dx
            outs.append(pv[2] * fdx)
            reds.append(_colsum(fdx * iv[3].astype(F32)))
        return outs, reds

    ins = [(dh, d, 0), (x, d, 0), (dres, d, 0)] + ([(y_up, d, 0)] if up else [])
    res = _rowwise(name, fn, ins, [g, sc] + ([gt_up] if up else []), [(d, F32)] + ([(d, BF16)] if up else []),
                   [d] * (4 if up else 3), 256)
    if up:
        dx, dy_up, dsh, dsc, dg, dgt_up = res
        return dx, dy_up, dsh, dsc, dg, dgt_up
    dx, dsh, dsc, dg = res
    return dx, None, dsh, dsc, dg, None


def _loss_bwd(name, x, tgt, g, y_up, gt_up, factor_up):
    d = x.shape[1]

    def fn(iv, pv):
        (xv, tv, yv), (gv, gtv) = iv, pv
        r = lax.rsqrt(jnp.mean(xv * xv, axis=-1, keepdims=True) + EPS)
        xr = xv * r
        e = xr * gv - tv
        loss_row = 0.5 * jnp.mean(e * e, axis=-1, keepdims=True)
        dout = e * (1.0 / d)
        gd = gv * dout
        dx = r * (gd - xr * jnp.mean(gd * xr, axis=-1, keepdims=True))
        fdx = factor_up * dx
        return [dx, gtv * fdx], [_colsum(dout * xr), _colsum(loss_row * jnp.ones((1, 128), F32)), _colsum(fdx * yv.astype(F32))]

    return _rowwise(name, fn, [(x, d, 0), (tgt, d, 0), (y_up, d, 0)], [g, gt_up], [(d, F32), (d, BF16)], [d, 128, d], 256)


def _resgate_bwd(name, dx, y, gt, factor, deps=()):
    d = dx.shape[1]

    def fn(iv, pv):
        (dxv, yv), (gtv,) = iv, pv
        return [factor * gtv * dxv], [_colsum(factor * dxv * yv.astype(F32))]

    return _rowwise(name, fn, [(dx, d, 0), (y, d, 0)], [gt], [(d, BF16)], [d], 512, deps=deps)


def _swiglu_fwd(name, ab):
    f = ab.shape[1] // 2

    def fn(iv, pv):
        a, b = iv[0].astype(F32), iv[1].astype(F32)
        return [a * _sigmoid(a) * b], []

    return _rowwise(name, fn, [(ab, f, 0), (ab, f, 1)], [], [(f, BF16)], [], 256)[0]


def _swiglu_bwd(name, dact, ab, deps=()):
    f = ab.shape[1] // 2

    def fn(iv, pv):
        dv, a, b = iv[0].astype(F32), iv[1].astype(F32), iv[2].astype(F32)
        s = _sigmoid(a)
        return [[dv * b * (s * (1.0 + a * (1.0 - s))), dv * (a * s)]], []

    return _rowwise(name, fn, [(dact, f, 0), (ab, f, 0), (ab, f, 1)], [], [(2 * f, BF16)], [], 256, deps=deps)[0]


def _gates_fwd(name, y_pool, y_ssm, uin, d, pw):
    cb = (2 * pw) // d

    def fn(iv, pv):
        yp, ys, glp, gls = [v.astype(F32) for v in iv]
        return [_sigmoid(glp) * yp + _sigmoid(gls) * ys], []

    return _rowwise(name, fn, [(y_pool, d, 0), (y_ssm, d, 0), (uin, d, cb), (uin, d, cb + 1)], [], [(d, BF16)], [], 256)[0]


def _gates_bwd(name, dm, y_pool, y_ssm, uin, d, pw):
    cb = (2 * pw) // d

    def fn(iv, pv):
        dmv, yp, ys, glp, gls = [v.astype(F32) for v in iv]
        sp, ss = _sigmoid(glp), _sigmoid(gls)
        return [dmv * sp, dmv * ss, ("at", 2 * pw, [dmv * yp * sp * (1.0 - sp), dmv * ys * ss * (1.0 - ss)])], []

    return _rowwise(name, fn, [(dm, d, 0), (y_pool, d, 0), (y_ssm, d, 0), (uin, d, cb), (uin, d, cb + 1)], [],
                    [(d, BF16), (d, BF16), (2 * pw + 2 * d, BF16)], [], 256)


def _glu_fwd(name, gv, b_glu):
    w = gv.shape[1] // 2

    def fn(iv, pv):
        gvv, (bv,) = iv[0].astype(F32) + pv[0], pv
        return [gvv[:, :w] * _sigmoid(gvv[:, w:])], []

    return _rowwise(name, fn, [(gv, 2 * w, 0)], [b_glu], [(w, BF16)], [], 512)[0]


def _glu_bwd(name, dsg, gv, b_glu, deps=()):
    w = gv.shape[1] // 2

    def fn(iv, pv):
        dv = iv[0].astype(F32)
        gvv = iv[1].astype(F32) + pv[0]
        val, s = gvv[:, :w], _sigmoid(gvv[:, w:])
        dval = dv * s
        dgate = dv * val * s * (1.0 - s)
        return [[dval, dgate]], [jnp.concatenate([_colsum(dval), _colsum(dgate)], axis=1)]

    return _rowwise(name, fn, [(dsg, w, 0), (gv, 2 * w, 0)], [b_glu], [(2 * w, BF16)], [2 * w], 512, deps=deps)


def _adamw(name, w, g, m, v, tm=256, with_g=True):
    c = w.shape[1]

    def fn(iv, pv):
        wv, gv, mv, vv = iv
        mn = ADAM_B1 * mv + (1.0 - ADAM_B1) * gv
        vn = ADAM_B2 * vv + (1.0 - ADAM_B2) * (gv * gv)
        m_hat = mn / (1.0 - ADAM_B1 ** ADAM_STEP)
        v_hat = vn / (1.0 - ADAM_B2 ** ADAM_STEP)
        delta = -ADAM_LR * (m_hat / (jnp.sqrt(v_hat) + ADAM_EPS) + ADAM_WD * wv)
        return ([gv] if with_g else []) + [delta, mn, vn], []

    return _rowwise(name, fn, [(w, c, 0), (g, c, 0), (m, c, 0), (v, c, 0)], [], [(c, F32)] * (4 if with_g else 3), [],
                    _pick(w.shape[0], tm, 8))


def _unpack_plan(shape):
    n = math.prod(shape)
    if len(shape) == 2 and shape[0] == 1 and n % 128 == 0:
        return [((slice(None), slice(128 * r, 128 * (r + 1))), slice(r, r + 1), slice(None)) for r in range(n // 128)]
    if len(shape) == 2 and shape[0] == 1 and n < 128:
        return [((slice(None), slice(None)), slice(0, 1), slice(0, n))]
    if len(shape) == 1 and n % 128 == 0:
        return [((slice(128 * r, 128 * (r + 1)),), r, slice(None)) for r in range(n // 128)]
    if len(shape) == 3 and shape[0] == 1 and shape[2] == 64:
        return [((0, slice(2 * r + h, 2 * r + h + 1), slice(None)), slice(r, r + 1), slice(64 * h, 64 * (h + 1)))
                for r in range(n // 128) for h in range(2)]
    if len(shape) == 4 and shape[0] == 1 and shape[2:] == (128, 128):
        return [((0, k), slice(128 * k, 128 * (k + 1)), slice(None)) for k in range(shape[1])]
    return None


def _adamw_small(name, w, g8, m, v, shapes):
    r = w.shape[0]
    plans, rows0, off = [], [], 0
    for s in shapes:
        plans.append(_unpack_plan(s))
        rows0.append(off // 128)
        off += math.prod(s) + (-math.prod(s)) % 128
    direct = [i for i, p in enumerate(plans) if p is not None]

    def body(w_ref, g_ref, m_ref, v_ref, *out):
        packed = out[4 * len(direct):]
        gv = g_ref[0:r, :]
        for k in range(1, N_DEV):
            gv = gv + g_ref[k * r:(k + 1) * r, :]
        mn = ADAM_B1 * m_ref[...] + (1.0 - ADAM_B1) * gv
        vn = ADAM_B2 * v_ref[...] + (1.0 - ADAM_B2) * (gv * gv)
        m_hat = mn / (1.0 - ADAM_B1 ** ADAM_STEP)
        v_hat = vn / (1.0 - ADAM_B2 ** ADAM_STEP)
        packed[0][...] = gv
        packed[1][...] = -ADAM_LR * (m_hat / (jnp.sqrt(v_hat) + ADAM_EPS) + ADAM_WD * w_ref[...])
        packed[2][...] = mn
        packed[3][...] = vn
        for di, i in enumerate(direct):
            for kind in range(4):
                o_ref, src = out[4 * di + kind], packed[kind]
                for o_idx, row, lanes in plans[i]:
                    row = (rows0[i] + row) if isinstance(row, int) else slice(rows0[i] + row.start, rows0[i] + row.stop)
                    o_ref[o_idx] = src[row, lanes]

    out_shape = [jax.ShapeDtypeStruct(shapes[i], F32) for i in direct for _ in range(4)]
    out_shape += [jax.ShapeDtypeStruct((r, 128), F32)] * 4
    res = _pcall(body, name=name, out_shape=out_shape,
                 compiler_params=pltpu.CompilerParams(vmem_limit_bytes=VMEM_LIMIT))(w, g8, m, v)
    per_param = [None] * len(shapes)
    for di, i in enumerate(direct):
        per_param[i] = res[4 * di:4 * di + 4]
    return per_param, res[4 * len(direct):]


def _pool_fwd(name, uin, pool_w, pool_b, pool_scale, pw, tm=512):
    t = uin.shape[0]
    tm = min(tm, t)
    ng = len(POOL_WINDOWS)
    gw = pw // ng

    def body(u_ref, w_ref, b_ref, s_ref, p_ref, z_ref, ext):
        i = pl.program_id(0)

        @pl.when(i == 0)
        def _():
            ext[0:POOL_HALO, :] = jnp.zeros((POOL_HALO, pw), F32)

        u = u_ref[...]
        ext[POOL_HALO:POOL_HALO + tm, :] = u
        pos = i * tm + lax.broadcasted_iota(jnp.int32, (tm, 1), 0)
        for k, win in enumerate(POOL_WINDOWS):
            cols = slice(k * gw, (k + 1) * gw)
            acc = u[:, cols]
            for j in range(1, win):
                acc = acc + ext[POOL_HALO - j:POOL_HALO - j + tm, cols]
            cnt = jnp.minimum(pos + 1, win).astype(F32)
            z = acc / cnt - u[:, cols]
            zp = jnp.dot(z.astype(BF16), w_ref[k].astype(BF16), preferred_element_type=F32) + b_ref[:, cols]
            p_ref[:, cols] = (zp * s_ref[:, cols]).astype(BF16)
            z_ref[:, cols] = z.astype(BF16)
        ext[0:POOL_HALO, :] = u[tm - POOL_HALO:tm, :]

    return _pcall(
        body, name=name, grid=(t // tm,),
        in_specs=[pl.BlockSpec((tm, pw), lambda i: (i, 0)), pl.BlockSpec(pool_w.shape, lambda i: (0, 0, 0)),
                  pl.BlockSpec(pool_b.shape, lambda i: (0, 0)), pl.BlockSpec(pool_scale.shape, lambda i: (0, 0))],
        out_specs=[pl.BlockSpec((tm, pw), lambda i: (i, 0))] * 2,
        out_shape=[jax.ShapeDtypeStruct((t, pw), BF16)] * 2,
        scratch_shapes=[pltpu.VMEM((POOL_HALO + tm, pw), F32)],
        compiler_params=_params("arbitrary"))(uin, pool_w, pool_b, pool_scale)


def _pool_bwd(name, dp, z, pool_w, pool_b, pool_scale, duin, tm=512):
    t, pw = z.shape
    tm = min(tm, t)
    nb = t // tm
    ng = len(POOL_WINDOWS)
    gw = pw // ng

    def body(dp_ref, z_ref, w_ref, b_ref, s_ref, duin_ref, du_ref, dw_ref, db_ref, ds_ref, ext):
        i = pl.program_id(0)

        @pl.when(i == 0)
        def _():
            ext[tm:tm + POOL_HALO, :] = jnp.zeros((POOL_HALO, pw), F32)
            dw_ref[...] = jnp.zeros_like(dw_ref)
            db_ref[...] = jnp.zeros_like(db_ref)
            ds_ref[...] = jnp.zeros_like(ds_ref)

        pos = (nb - 1 - i) * tm + lax.broadcasted_iota(jnp.int32, (tm, 1), 0)
        for k, win in enumerate(POOL_WINDOWS):
            cols = slice(k * gw, (k + 1) * gw)
            zk = z_ref[:, cols]
            dpk = dp_ref[:, cols].astype(F32)
            wk = w_ref[k].astype(BF16)
            zp = jnp.dot(zk, wk, preferred_element_type=F32) + b_ref[:, cols]
            ds_ref[:, cols] += _colsum(dpk * zp)
            dzp = dpk * s_ref[:, cols]
            db_ref[:, cols] += _colsum(dzp)
            dzpb = dzp.astype(BF16)
            dz = lax.dot_general(dzpb, wk, (((1,), (1,)), ((), ())), preferred_element_type=F32)
            dw_ref[k] += lax.dot_general(zk, dzpb, (((0,), (0,)), ((), ())), preferred_element_type=F32)
            cnt = jnp.minimum(pos + 1, win).astype(F32)
            r = dz / cnt
            ext[0:tm, cols] = r
            acc = r - dz
            for j in range(1, win):
                acc = acc + ext[j:j + tm, cols]
            du_ref[:, cols] = acc.astype(BF16)
        ext[tm:tm + POOL_HALO, :] = ext[0:POOL_HALO, :]

    rev = lambda i: (nb - 1 - i, 0)
    return _pcall(
        body, name=name, grid=(nb,),
        in_specs=[pl.BlockSpec((tm, pw), rev), pl.BlockSpec((tm, pw), rev), pl.BlockSpec(pool_w.shape, lambda i: (0, 0, 0)),
                  pl.BlockSpec(pool_b.shape, lambda i: (0, 0)), pl.BlockSpec(pool_scale.shape, lambda i: (0, 0)), ANY_SPEC],
        out_specs=[pl.BlockSpec((tm, pw), rev), pl.BlockSpec(pool_w.shape, lambda i: (0, 0, 0)),
                   pl.BlockSpec((1, pw), lambda i: (0, 0)), pl.BlockSpec((1, pw), lambda i: (0, 0))],
        out_shape=[jax.ShapeDtypeStruct(duin.shape, BF16), jax.ShapeDtypeStruct(pool_w.shape, F32),
                   jax.ShapeDtypeStruct((1, pw), F32), jax.ShapeDtypeStruct((1, pw), F32)],
        scratch_shapes=[pltpu.VMEM((tm + POOL_HALO, pw), F32)], input_output_aliases={5: 0},
        compiler_params=_params("arbitrary"))(dp, z, pool_w, pool_b, pool_scale, duin)


def _ssm_disc(lrl, li, ldt):
    lr = -jnp.exp(lrl)
    dt = jnp.exp(ldt)
    mag = jnp.exp(lr * dt)
    ang = li * dt
    ab_re = mag * jnp.cos(ang)
    ab_im = mag * jnp.sin(ang)
    num_re = ab_re - 1.0
    num_im = ab_im
    den = lr * lr + li * li
    f_re = (num_re * lr + num_im * li) / den
    f_im = (num_im * lr - num_re * li) / den
    return lr, dt, mag, ang, ab_re, ab_im, num_re, num_im, den, f_re, f_im


def _ssm_prep(name, lrl, li, ldt, b_re, b_im):
    gn, h = b_re.shape

    def body(lrl_ref, li_ref, ldt_ref, br_ref, bi_ref, lrdt_ref, ang_ref, bbr_ref, bbi_ref):
        lr, dt, _, ang, _, _, _, _, _, f_re, f_im = _ssm_disc(lrl_ref[...], li_ref[...], ldt_ref[...])
        lrdt_ref[...] = lr * dt
        ang_ref[...] = ang
        br, bi = br_ref[...], bi_ref[...]
        bbr_ref[...] = f_re * br - f_im * bi
        bbi_ref[...] = f_re * bi + f_im * br

    col = jax.ShapeDtypeStruct((gn, 1), F32)
    mat = jax.ShapeDtypeStruct((gn, h), F32)
    return _pcall(body, name=name, out_shape=[col, col, mat, mat])(lrl, li, ldt, b_re, b_im)


def _ssm_param_bwd(name, lrl, li, ldt, b_re, b_im, g_abre, g_abim, g_bbre, g_bbim):
    gn, h = b_re.shape

    def body(lrl_ref, li_ref, ldt_ref, br_ref, bi_ref, gar_ref, gai_ref, gbr_ref, gbi_ref,
             dlrl_ref, dli_ref, dldt_ref, dbr_ref, dbi_ref):
        li_v = li_ref[...]
        lr, dt, mag, ang, ab_re, ab_im, num_re, num_im, den, f_re, f_im = _ssm_disc(lrl_ref[...], li_v, ldt_ref[...])
        br, bi = br_ref[...], bi_ref[...]
        gbr, gbi = gbr_ref[...], gbi_ref[...]
        g_fre = jnp.sum(gbr * br + gbi * bi, axis=1, keepdims=True)
        g_fim = jnp.sum(gbi * br - gbr * bi, axis=1, keepdims=True)
        dbr_ref[...] = gbr * f_re + gbi * f_im
        dbi_ref[...] = gbi * f_re - gbr * f_im
        g_num_re = (g_fre * lr - g_fim * li_v) / den
        g_num_im = (g_fre * li_v + g_fim * lr) / den
        g_den = -(g_fre * f_re + g_fim * f_im) / den
        g_lr = (g_fre * num_re + g_fim * num_im) / den + g_den * 2.0 * lr
        g_li = (g_fre * num_im - g_fim * num_re) / den + g_den * 2.0 * li_v
        g_are = gar_ref[...] + g_num_re
        g_aim = gai_ref[...] + g_num_im
        g_mag = g_are * jnp.cos(ang) + g_aim * jnp.sin(ang)
        g_ang = g_aim * ab_re - g_are * ab_im
        g_lrdt = g_mag * mag
        g_lr = g_lr + g_lrdt * dt
        g_dt = g_lrdt * lr + g_ang * li_v
        g_li = g_li + g_ang * dt
        dlrl_ref[...] = g_lr * lr
        dli_ref[...] = g_li
        dldt_ref[...] = g_dt * dt

    col = jax.ShapeDtypeStruct((gn, 1), F32)
    mat = jax.ShapeDtypeStruct((gn, h), F32)
    return _pcall(body, name=name, out_shape=[col, col, col, mat, mat])(lrl, li, ldt, b_re, b_im, g_abre, g_abim, g_bbre, g_bbim)


def _pow_rows(lrdt, ang, k):
    mag = jnp.exp(k * lrdt)
    return mag * jnp.cos(k * ang), mag * jnp.sin(k * ang)


def _ssm_chunk(t):
    return 256 if t >= 2048 else 128


def _to_segments(dst, srcs, jn):
    for q, src in enumerate(srcs):
        for j in range(jn):
            dst[8 * j:8 * j + 8, 128 * q:128 * (q + 1)] = src[pl.ds(j, 8, stride=jn), :]


def _from_segments(dst, src, q, jn, dtype):
    for s in range(8):
        dst[s * jn:(s + 1) * jn, 128 * q:128 * (q + 1)] = src[q, pl.ds(s, jn, stride=8), :].astype(dtype)


def _fill_rows8(dst_re, dst_im, v_re, v_im):
    for j in range(v_re.shape[0]):
        dst_re[8 * j:8 * j + 8, :] = jnp.broadcast_to(v_re[j:j + 1, :], (8, v_re.shape[1]))
        dst_im[8 * j:8 * j + 8, :] = jnp.broadcast_to(v_im[j:j + 1, :], (8, v_im.shape[1]))


def _cmul(ar, ai, br, bi):
    return ar * br - ai * bi, ar * bi + ai * br


def _cmul_conj(ar, ai, br, bi):
    return ar * br + ai * bi, ar * bi - ai * br


def _ssm_fwd(name, uin, lrdt, ang, bb_re, bb_im, cc_re, cc_im, d_skip, sw):
    t = uin.shape[0]
    gn = lrdt.shape[1]
    lc = _ssm_chunk(t)
    jn = lc // 8
    ub, sb = sw // SSM_BLOCKS, gn // SSM_BLOCKS
    nq = sw // 128
    assert ub == 128 and nq == SSM_BLOCKS

    def body(u_ref, lrdt_ref, ang_ref, bbr_ref, bbi_ref, ccr_ref, cci_ref, d_ref, y_ref, ge_ref, sre_ref, sim_ref,
             p_re, p_im, a_re, a_im, up, yp, cst_re, cst_im, car_re, car_im):
        i = pl.program_id(0)
        lrdt_v, ang_v = lrdt_ref[...], ang_ref[...]

        @pl.when(i == 0)
        def _():
            k = (lax.broadcasted_iota(jnp.int32, (jn, 1), 0) + 1).astype(F32)
            _fill_rows8(p_re, p_im, *_pow_rows(lrdt_v, ang_v, k))
            car_re[...] = jnp.zeros_like(car_re)
            car_im[...] = jnp.zeros_like(car_im)

        for q in range(nq):
            yp[q] = u_ref[:, 128 * q:128 * (q + 1)]
        _to_segments(up, [yp.at[q] for q in range(nq)], jn)
        u = up[...]
        ubf = u.astype(BF16)
        for q in range(SSM_BLOCKS):
            uq = ubf[:, q * ub:(q + 1) * ub]
            a_re[:, q * sb:(q + 1) * sb] = jnp.dot(uq, bbr_ref[q], preferred_element_type=F32)
            a_im[:, q * sb:(q + 1) * sb] = jnp.dot(uq, bbi_ref[q], preferred_element_type=F32)
        a1r, a1i = _pow_rows(lrdt_v, ang_v, 1.0)
        ajr, aji = _pow_rows(lrdt_v, ang_v, float(jn))
        for q in range(SSM_BLOCKS):
            cols = slice(q * sb, (q + 1) * sb)
            ar8 = jnp.broadcast_to(a1r[:, cols], (8, sb))
            ai8 = jnp.broadcast_to(a1i[:, cols], (8, sb))

            def step(j, carry, cols=cols, ar8=ar8, ai8=ai8):
                sr, si = carry
                rows = pl.ds(pl.multiple_of(j * 8, 8), 8)
                mr, mi = _cmul(ar8, ai8, sr, si)
                nr, ni = mr + a_re[rows, cols], mi + a_im[rows, cols]
                a_re[rows, cols] = nr
                a_im[rows, cols] = ni
                return nr, ni

            lax.fori_loop(1, jn, step, (a_re[0:8, cols], a_im[0:8, cols]), unroll=4)
        er, ei = a_re[lc - 8:lc, :], a_im[lc - 8:lc, :]
        gr, gi = car_re[...], car_im[...]
        for s in range(8):
            cst_re[s:s + 1, :] = gr
            cst_im[s:s + 1, :] = gi
            mr, mi = _cmul(ajr, aji, gr, gi)
            gr, gi = mr + er[s:s + 1, :], mi + ei[s:s + 1, :]
        car_re[...] = gr
        car_im[...] = gi
        for q in range(SSM_BLOCKS):
            cols = slice(q * sb, (q + 1) * sb)
            cr = jnp.tile(cst_re[:, cols], (jn, 1))
            ci = jnp.tile(cst_im[:, cols], (jn, 1))
            mr, mi = _cmul(p_re[:, cols], p_im[:, cols], cr, ci)
            srb, sib = (a_re[:, cols] + mr).astype(BF16), (a_im[:, cols] + mi).astype(BF16)
            sre_ref[:, cols] = srb
            sim_ref[:, cols] = sib
            ycols = slice(q * ub, (q + 1) * ub)
            y = (jnp.dot(srb, ccr_ref[q], preferred_element_type=F32) - jnp.dot(sib, cci_ref[q], preferred_element_type=F32)
                 + d_ref[:, ycols] * u[:, ycols])
            yp[q] = y
            _from_segments(y_ref, yp, q, jn, F32)
            yt = y_ref[:, ycols]
            ge_ref[:, ycols] = (0.5 * yt * (1.0 + lax.erf(yt * (1.0 / math.sqrt(2.0))))).astype(BF16)

    row = lambda i: (0, 0)
    blk3 = lambda i: (0, 0, 0)
    return _pcall(
        body, name=name, grid=(t // lc,),
        in_specs=[pl.BlockSpec((lc, sw), lambda i: (i, 1)), pl.BlockSpec((1, gn), row), pl.BlockSpec((1, gn), row),
                  pl.BlockSpec(bb_re.shape, blk3), pl.BlockSpec(bb_im.shape, blk3),
                  pl.BlockSpec(cc_re.shape, blk3), pl.BlockSpec(cc_im.shape, blk3), pl.BlockSpec((1, sw), row)],
        out_specs=[pl.BlockSpec((lc, sw), lambda i: (i, 0)), pl.BlockSpec((lc, sw), lambda i: (i, 0)),
                   pl.BlockSpec((lc, gn), lambda i: (i, 0)), pl.BlockSpec((lc, gn), lambda i: (i, 0))],
        out_shape=[jax.ShapeDtypeStruct((t, sw), F32), jax.ShapeDtypeStruct((t, sw), BF16),
                   jax.ShapeDtypeStruct((t, gn), BF16), jax.ShapeDtypeStruct((t, gn), BF16)],
        scratch_shapes=[pltpu.VMEM((lc, gn), F32), pltpu.VMEM((lc, gn), F32), pltpu.VMEM((lc, gn), F32), pltpu.VMEM((lc, gn), F32),
                        pltpu.VMEM((lc, sw), F32), pltpu.VMEM((nq, lc, 128), F32),
                        pltpu.VMEM((8, gn), F32), pltpu.VMEM((8, gn), F32), pltpu.VMEM((1, gn), F32), pltpu.VMEM((1, gn), F32)],
        compiler_params=_params("arbitrary"))(uin, lrdt, ang, bb_re, bb_im, cc_re, cc_im, d_skip)


def _ssm_bwd(name, dge, y, uin, s_re, s_im, lrdt, ang, bbt_re, bbt_im, cct_re, cct_im, d_skip, sw, duin):
    t = uin.shape[0]
    gn = lrdt.shape[1]
    lc = _ssm_chunk(t)
    nb = t // lc
    jn = lc // 8
    ub, sb = sw // SSM_BLOCKS, gn // SSM_BLOCKS
    nq = sw // 128
    tail = 16

    def body(dge_ref, y_ref, u_ref, sre_ref, sim_ref, tre_ref, tim_ref, lrdt_ref, ang_ref, btr_ref, bti_ref, ctr_ref, cti_ref,
             d_ref, duin_ref, du_ref, dar_ref, dai_ref, dbr_ref, dbi_ref, dcr_ref, dci_ref, dd_ref,
             q_re, q_im, a_re, a_im, dyp, up, dys, us, dup, cst_re, cst_im, sp_re, sp_im, car_re, car_im):
        i = pl.program_id(0)
        lrdt_v, ang_v = lrdt_ref[...], ang_ref[...]

        @pl.when(i == 0)
        def _():
            k = (jn - lax.broadcasted_iota(jnp.int32, (jn, 1), 0)).astype(F32)
            _fill_rows8(q_re, q_im, *_pow_rows(lrdt_v, ang_v, k))
            car_re[...] = jnp.zeros_like(car_re)
            car_im[...] = jnp.zeros_like(car_im)
            for r in (dar_ref, dai_ref, dbr_ref, dbi_ref, dcr_ref, dci_ref, dd_ref):
                r[...] = jnp.zeros_like(r)

        yv = y_ref[...]
        ut = u_ref[...]
        cdf = 0.5 * (1.0 + lax.erf(yv * (1.0 / math.sqrt(2.0))))
        pdf = jnp.exp(-0.5 * yv * yv) * (1.0 / math.sqrt(2.0 * math.pi))
        dyt = dge_ref[...].astype(F32) * (cdf + yv * pdf)
        dd_ref[...] += _colsum(dyt * ut)
        for q in range(nq):
            dys[q] = dyt[:, 128 * q:128 * (q + 1)]
            us[q] = ut[:, 128 * q:128 * (q + 1)]
        _to_segments(dyp, [dys.at[q] for q in range(nq)], jn)
        _to_segments(up, [us.at[q] for q in range(nq)], jn)
        dy = dyp[...]
        u = up[...]
        dyb = dy.astype(BF16)
        ubf = u.astype(BF16)
        for q in range(SSM_BLOCKS):
            dq = dyb[:, q * ub:(q + 1) * ub]
            a_re[:, q * sb:(q + 1) * sb] = jnp.dot(dq, ctr_ref[q], preferred_element_type=F32)
            a_im[:, q * sb:(q + 1) * sb] = -jnp.dot(dq, cti_ref[q], preferred_element_type=F32)
        a1r, a1i = _pow_rows(lrdt_v, ang_v, 1.0)
        ajr, aji = _pow_rows(lrdt_v, ang_v, float(jn))
        for q in range(SSM_BLOCKS):
            cols = slice(q * sb, (q + 1) * sb)
            ar8 = jnp.broadcast_to(a1r[:, cols], (8, sb))
            ai8 = jnp.broadcast_to(a1i[:, cols], (8, sb))

            def step(jj, carry, cols=cols, ar8=ar8, ai8=ai8):
                sr, si = carry
                rows = pl.ds(pl.multiple_of((jn - 2 - jj) * 8, 8), 8)
                mr, mi = _cmul_conj(ar8, ai8, sr, si)
                nr, ni = mr + a_re[rows, cols], mi + a_im[rows, cols]
                a_re[rows, cols] = nr
                a_im[rows, cols] = ni
                return nr, ni

            lax.fori_loop(0, jn - 1, step, (a_re[lc - 8:lc, cols], a_im[lc - 8:lc, cols]), unroll=4)
        er, ei = a_re[0:8, :], a_im[0:8, :]
        hr, hi = car_re[...], car_im[...]
        for s in range(7, -1, -1):
            cst_re[s:s + 1, :] = hr
            cst_im[s:s + 1, :] = hi
            mr, mi = _cmul_conj(ajr, aji, hr, hi)
            hr, hi = mr + er[s:s + 1, :], mi + ei[s:s + 1, :]
        car_re[...] = hr
        car_im[...] = hi
        first = (i == nb - 1).astype(F32)
        sp_re[0:tail, :] = tre_ref[...].astype(F32) * (1.0 - first)
        sp_im[0:tail, :] = tim_ref[...].astype(F32) * (1.0 - first)
        sp_re[tail:tail + 8, :] = sre_ref[lc - tail:lc, :].astype(F32)[tail - 8:tail]
        sp_im[tail:tail + 8, :] = sim_ref[lc - tail:lc, :].astype(F32)[tail - 8:tail]
        tn_dims = (((0,), (0,)), ((), ()))
        for q in range(SSM_BLOCKS):
            cols = slice(q * sb, (q + 1) * sb)
            ycols = slice(q * ub, (q + 1) * ub)
            cr = jnp.tile(cst_re[:, cols], (jn, 1))
            ci = jnp.tile(cst_im[:, cols], (jn, 1))
            mr, mi = _cmul_conj(q_re[:, cols], q_im[:, cols], cr, ci)
            lam_r, lam_i = a_re[:, cols] + mr, a_im[:, cols] + mi
            s_r, s_i = sre_ref[:, cols], sim_ref[:, cols]
            p0r, p0i = sp_re[tail - 1:tail + 7, cols], sp_im[tail - 1:tail + 7, cols]
            l0r, l0i, l1r, l1i = lam_r[0:8], lam_i[0:8], lam_r[8:lc], lam_i[8:lc]
            pvr, pvi = s_r.astype(F32)[0:lc - 8], s_i.astype(F32)[0:lc - 8]
            dar_ref[:, cols] += _colsum(l1r * pvr + l1i * pvi) + _colsum(l0r * p0r + l0i * p0i)
            dai_ref[:, cols] += _colsum(l1i * pvr - l1r * pvi) + _colsum(l0i * p0r - l0r * p0i)
            lrb, lib = lam_r.astype(BF16), lam_i.astype(BF16)
            dup[q] = (jnp.dot(lrb, btr_ref[q], preferred_element_type=F32)
                      + jnp.dot(lib, bti_ref[q], preferred_element_type=F32) + d_ref[:, ycols] * dy[:, ycols])
            _from_segments(du_ref, dup, q, jn, BF16)
            uq = ubf[:, ycols]
            dbr_ref[q] += lax.dot_general(uq, lrb, tn_dims, preferred_element_type=F32)
            dbi_ref[q] += lax.dot_general(uq, lib, tn_dims, preferred_element_type=F32)
            dq = dyb[:, ycols]
            dcr_ref[q] += lax.dot_general(s_r.astype(BF16), dq, tn_dims, preferred_element_type=F32)
            dci_ref[q] -= lax.dot_general(s_i.astype(BF16), dq, tn_dims, preferred_element_type=F32)

    rev = lambda i: (nb - 1 - i, 0)
    tailmap = lambda i: (jnp.maximum((nb - 1 - i) * (lc // tail) - 1, 0), 0)
    row = lambda i: (0, 0)
    blk3 = lambda i: (0, 0, 0)
    return _pcall(
        body, name=name, grid=(nb,),
        in_specs=[pl.BlockSpec((lc, sw), rev), pl.BlockSpec((lc, sw), rev), pl.BlockSpec((lc, sw), lambda i: (nb - 1 - i, 1)),
                  pl.BlockSpec((lc, gn), rev), pl.BlockSpec((lc, gn), rev),
                  pl.BlockSpec((tail, gn), tailmap), pl.BlockSpec((tail, gn), tailmap),
                  pl.BlockSpec((1, gn), row), pl.BlockSpec((1, gn), row),
                  pl.BlockSpec(bbt_re.shape, blk3), pl.BlockSpec(bbt_im.shape, blk3),
                  pl.BlockSpec(cct_re.shape, blk3), pl.BlockSpec(cct_im.shape, blk3), pl.BlockSpec((1, sw), row), ANY_SPEC],
        out_specs=[pl.BlockSpec((lc, sw), lambda i: (nb - 1 - i, 1)), pl.BlockSpec((1, gn), row), pl.BlockSpec((1, gn), row),
                   pl.BlockSpec((SSM_BLOCKS, ub, sb), blk3), pl.BlockSpec((SSM_BLOCKS, ub, sb), blk3),
                   pl.BlockSpec((SSM_BLOCKS, sb, ub), blk3), pl.BlockSpec((SSM_BLOCKS, sb, ub), blk3),
                   pl.BlockSpec((1, sw), row)],
        out_shape=[jax.ShapeDtypeStruct(duin.shape, BF16), jax.ShapeDtypeStruct((1, gn), F32), jax.ShapeDtypeStruct((1, gn), F32),
                   jax.ShapeDtypeStruct((SSM_BLOCKS, ub, sb), F32), jax.ShapeDtypeStruct((SSM_BLOCKS, ub, sb), F32),
                   jax.ShapeDtypeStruct((SSM_BLOCKS, sb, ub), F32), jax.ShapeDtypeStruct((SSM_BLOCKS, sb, ub), F32),
                   jax.ShapeDtypeStruct((1, sw), F32)],
        scratch_shapes=[pltpu.VMEM((lc, gn), F32), pltpu.VMEM((lc, gn), F32), pltpu.VMEM((lc, gn), F32), pltpu.VMEM((lc, gn), F32),
                        pltpu.VMEM((lc, sw), F32), pltpu.VMEM((lc, sw), F32),
                        pltpu.VMEM((nq, lc, 128), F32), pltpu.VMEM((nq, lc, 128), F32), pltpu.VMEM((nq, lc, 128), F32),
                        pltpu.VMEM((8, gn), F32), pltpu.VMEM((8, gn), F32),
                        pltpu.VMEM((tail + 8, gn), F32), pltpu.VMEM((tail + 8, gn), F32),
                        pltpu.VMEM((1, gn), F32), pltpu.VMEM((1, gn), F32)],
        input_output_aliases={14: 0},
        compiler_params=_params("arbitrary"))(dge, y, uin, s_re, s_im, s_re, s_im, lrdt, ang,
                                               bbt_re, bbt_im, cct_re, cct_im, d_skip, duin)


def _ssm_fwd_doubling(name, uin, lrdt, ang, bb_re, bb_im, cc_re, cc_im, d_skip, sw):
    t = uin.shape[0]
    gn = lrdt.shape[1]
    lc = _ssm_chunk(t)
    nsteps = int(math.log2(lc))
    pad = lc // 2
    ub, sb = sw // SSM_BLOCKS, gn // SSM_BLOCKS

    def body(u_ref, lrdt_ref, ang_ref, bbr_ref, bbi_ref, ccr_ref, cci_ref, d_ref, y_ref, ge_ref, sre_ref, sim_ref,
             p_re, p_im, a_re, a_im, b_re, b_im, car_re, car_im):
        i = pl.program_id(0)
        lrdt_v, ang_v = lrdt_ref[...], ang_ref[...]

        @pl.when(i == 0)
        def _():
            k = (lax.broadcasted_iota(jnp.int32, (lc, 1), 0) + 1).astype(F32)
            pr, pi = _pow_rows(lrdt_v, ang_v, k)
            p_re[...] = pr
            p_im[...] = pi
            zeros = jnp.zeros((pad, gn), F32)
            a_re[0:pad, :] = zeros
            a_im[0:pad, :] = zeros
            b_re[0:pad, :] = zeros
            b_im[0:pad, :] = zeros
            car_re[...] = jnp.zeros_like(car_re)
            car_im[...] = jnp.zeros_like(car_im)

        u = u_ref[...]
        ubf = u.astype(BF16)
        for q in range(SSM_BLOCKS):
            uq = ubf[:, q * ub:(q + 1) * ub]
            a_re[pad:pad + lc, q * sb:(q + 1) * sb] = jnp.dot(uq, bbr_ref[q], preferred_element_type=F32)
            a_im[pad:pad + lc, q * sb:(q + 1) * sb] = jnp.dot(uq, bbi_ref[q], preferred_element_type=F32)
        bufs = [(a_re, a_im), (b_re, b_im)]
        for q in range(SSM_BLOCKS):
            cols = slice(q * sb, (q + 1) * sb)
            for j in range(nsteps):
                dd = 1 << j
                (s_re, s_im), (d_re, d_im) = bufs[j % 2], bufs[(j + 1) % 2]
                pr, pi = _pow_rows(lrdt_v[:, cols], ang_v[:, cols], float(dd))
                cr, ci = s_re[pad:pad + lc, cols], s_im[pad:pad + lc, cols]
                hr, hi = s_re[pad - dd:pad - dd + lc, cols], s_im[pad - dd:pad - dd + lc, cols]
                d_re[pad:pad + lc, cols] = cr + (pr * hr - pi * hi)
                d_im[pad:pad + lc, cols] = ci + (pr * hi + pi * hr)
            f_re, f_im = bufs[nsteps % 2]
            cr, ci = car_re[:, cols], car_im[:, cols]
            pr, pi = p_re[:, cols], p_im[:, cols]
            sr = f_re[pad:pad + lc, cols] + (pr * cr - pi * ci)
            si = f_im[pad:pad + lc, cols] + (pr * ci + pi * cr)
            sre_ref[:, cols] = sr
            sim_ref[:, cols] = si
            car_re[:, cols] = sr[lc - 1:lc, :]
            car_im[:, cols] = si[lc - 1:lc, :]
            ycols = slice(q * ub, (q + 1) * ub)
            y = (jnp.dot(sr.astype(BF16), ccr_ref[q], preferred_element_type=F32)
                 - jnp.dot(si.astype(BF16), cci_ref[q], preferred_element_type=F32)
                 + d_ref[:, ycols] * u[:, ycols])
            y_ref[:, ycols] = y
            ge_ref[:, ycols] = (0.5 * y * (1.0 + lax.erf(y * (1.0 / math.sqrt(2.0))))).astype(BF16)

    cblk = sw // sw
    row = lambda i: (0, 0)
    blk3 = lambda i: (0, 0, 0)
    return _pcall(
        body, name=name, grid=(t // lc,),
        in_specs=[pl.BlockSpec((lc, sw), lambda i: (i, cblk)), pl.BlockSpec((1, gn), row), pl.BlockSpec((1, gn), row),
                  pl.BlockSpec(bb_re.shape, blk3), pl.BlockSpec(bb_im.shape, blk3),
                  pl.BlockSpec(cc_re.shape, blk3), pl.BlockSpec(cc_im.shape, blk3), pl.BlockSpec((1, sw), row)],
        out_specs=[pl.BlockSpec((lc, sw), lambda i: (i, 0)), pl.BlockSpec((lc, sw), lambda i: (i, 0)),
                   pl.BlockSpec((lc, gn), lambda i: (i, 0)), pl.BlockSpec((lc, gn), lambda i: (i, 0))],
        out_shape=[jax.ShapeDtypeStruct((t, sw), F32), jax.ShapeDtypeStruct((t, sw), BF16),
                   jax.ShapeDtypeStruct((t, gn), F32), jax.ShapeDtypeStruct((t, gn), F32)],
        scratch_shapes=[pltpu.VMEM((lc, gn), F32), pltpu.VMEM((lc, gn), F32),
                        pltpu.VMEM((pad + lc, gn), F32), pltpu.VMEM((pad + lc, gn), F32),
                        pltpu.VMEM((pad + lc, gn), F32), pltpu.VMEM((pad + lc, gn), F32),
                        pltpu.VMEM((1, gn), F32), pltpu.VMEM((1, gn), F32)],
        compiler_params=_params("arbitrary"))(uin, lrdt, ang, bb_re, bb_im, cc_re, cc_im, d_skip)


def _ssm_bwd_doubling(name, dge, y, uin, s_re, s_im, lrdt, ang, bbt_re, bbt_im, cct_re, cct_im, d_skip, sw):
    t = uin.shape[0]
    gn = lrdt.shape[1]
    lc = _ssm_chunk(t)
    nb = t // lc
    nsteps = int(math.log2(lc))
    pad = lc // 2
    ub, sb = sw // SSM_BLOCKS, gn // SSM_BLOCKS
    tail = 8

    def body(dge_ref, y_ref, u_ref, sre_ref, sim_ref, tre_ref, tim_ref, lrdt_ref, ang_ref, btr_ref, bti_ref, ctr_ref, cti_ref,
             d_ref, du_ref, dar_ref, dai_ref, dbr_ref, dbi_ref, dcr_ref, dci_ref, dd_ref,
             q_re, q_im, a_re, a_im, b_re, b_im, sp_re, sp_im, car_re, car_im):
        i = pl.program_id(0)
        lrdt_v, ang_v = lrdt_ref[...], ang_ref[...]

        @pl.when(i == 0)
        def _():
            k = (lc - lax.broadcasted_iota(jnp.int32, (lc, 1), 0)).astype(F32)
            qr, qi = _pow_rows(lrdt_v, ang_v, k)
            q_re[...] = qr
            q_im[...] = qi
            zeros = jnp.zeros((pad, gn), F32)
            a_re[lc:lc + pad, :] = zeros
            a_im[lc:lc + pad, :] = zeros
            b_re[lc:lc + pad, :] = zeros
            b_im[lc:lc + pad, :] = zeros
            car_re[...] = jnp.zeros_like(car_re)
            car_im[...] = jnp.zeros_like(car_im)
            for r in (dar_ref, dai_ref, dbr_ref, dbi_ref, dcr_ref, dci_ref, dd_ref):
                r[...] = jnp.zeros_like(r)

        first = (i == nb - 1).astype(F32)
        sp_re[0:tail, :] = tre_ref[...] * (1.0 - first)
        sp_im[0:tail, :] = tim_ref[...] * (1.0 - first)
        sp_re[tail:tail + lc, :] = sre_ref[...]
        sp_im[tail:tail + lc, :] = sim_ref[...]

        yv = y_ref[...]
        u = u_ref[...]
        cdf = 0.5 * (1.0 + lax.erf(yv * (1.0 / math.sqrt(2.0))))
        pdf = jnp.exp(-0.5 * yv * yv) * (1.0 / math.sqrt(2.0 * math.pi))
        dy = dge_ref[...].astype(F32) * (cdf + yv * pdf)
        dd_ref[...] += _colsum(dy * u)
        dyb = dy.astype(BF16)
        ubf = u.astype(BF16)
        for q in range(SSM_BLOCKS):
            dq = dyb[:, q * ub:(q + 1) * ub]
            a_re[0:lc, q * sb:(q + 1) * sb] = jnp.dot(dq, ctr_ref[q], preferred_element_type=F32)
            a_im[0:lc, q * sb:(q + 1) * sb] = -jnp.dot(dq, cti_ref[q], preferred_element_type=F32)
        bufs = [(a_re, a_im), (b_re, b_im)]
        tn_dims = (((0,), (0,)), ((), ()))
        for q in range(SSM_BLOCKS):
            cols = slice(q * sb, (q + 1) * sb)
            ycols = slice(q * ub, (q + 1) * ub)
            for j in range(nsteps):
                ds = 1 << j
                (s_r, s_i), (d_r, d_i) = bufs[j % 2], bufs[(j + 1) % 2]
                pr, pi = _pow_rows(lrdt_v[:, cols], ang_v[:, cols], float(ds))
                cr, ci = s_r[0:lc, cols], s_i[0:lc, cols]
                hr, hi = s_r[ds:ds + lc, cols], s_i[ds:ds + lc, cols]
                d_r[0:lc, cols] = cr + (pr * hr + pi * hi)
                d_i[0:lc, cols] = ci + (pr * hi - pi * hr)
            f_r, f_i = bufs[nsteps % 2]
            cr, ci = car_re[:, cols], car_im[:, cols]
            qr, qi = q_re[:, cols], q_im[:, cols]
            lam_r = f_r[0:lc, cols] + (qr * cr + qi * ci)
            lam_i = f_i[0:lc, cols] + (qr * ci - qi * cr)
            car_re[:, cols] = lam_r[0:1, :]
            car_im[:, cols] = lam_i[0:1, :]
            pvr, pvi = sp_re[tail - 1:tail - 1 + lc, cols], sp_im[tail - 1:tail - 1 + lc, cols]
            dar_ref[:, cols] += _colsum(lam_r * pvr + lam_i * pvi)
            dai_ref[:, cols] += _colsum(lam_i * pvr - lam_r * pvi)
            lrb, lib = lam_r.astype(BF16), lam_i.astype(BF16)
            du_ref[:, ycols] = (jnp.dot(lrb, btr_ref[q], preferred_element_type=F32)
                                + jnp.dot(lib, bti_ref[q], preferred_element_type=F32)
                                + d_ref[:, ycols] * dy[:, ycols]).astype(BF16)
            uq = ubf[:, ycols]
            dbr_ref[q] += lax.dot_general(uq, lrb, tn_dims, preferred_element_type=F32)
            dbi_ref[q] += lax.dot_general(uq, lib, tn_dims, preferred_element_type=F32)
            dq = dyb[:, ycols]
            dcr_ref[q] += lax.dot_general(sre_ref[:, cols].astype(BF16), dq, tn_dims, preferred_element_type=F32)
            dci_ref[q] -= lax.dot_general(sim_ref[:, cols].astype(BF16), dq, tn_dims, preferred_element_type=F32)

    cblk = 1
    rev = lambda i: (nb - 1 - i, 0)
    revu = lambda i: (nb - 1 - i, cblk)
    tailmap = lambda i: (jnp.maximum((nb - 1 - i) * (lc // tail) - 1, 0), 0)
    row = lambda i: (0, 0)
    blk3 = lambda i: (0, 0, 0)
    return _pcall(
        body, name=name, grid=(nb,),
        in_specs=[pl.BlockSpec((lc, sw), rev), pl.BlockSpec((lc, sw), rev), pl.BlockSpec((lc, sw), revu),
                  pl.BlockSpec((lc, gn), rev), pl.BlockSpec((lc, gn), rev),
                  pl.BlockSpec((tail, gn), tailmap), pl.BlockSpec((tail, gn), tailmap),
                  pl.BlockSpec((1, gn), row), pl.BlockSpec((1, gn), row),
                  pl.BlockSpec(bbt_re.shape, blk3), pl.BlockSpec(bbt_im.shape, blk3),
                  pl.BlockSpec(cct_re.shape, blk3), pl.BlockSpec(cct_im.shape, blk3), pl.BlockSpec((1, sw), row)],
        out_specs=[pl.BlockSpec((lc, sw), rev), pl.BlockSpec((1, gn), row), pl.BlockSpec((1, gn), row),
                   pl.BlockSpec((SSM_BLOCKS, ub, sb), blk3), pl.BlockSpec((SSM_BLOCKS, ub, sb), blk3),
                   pl.BlockSpec((SSM_BLOCKS, sb, ub), blk3), pl.BlockSpec((SSM_BLOCKS, sb, ub), blk3),
                   pl.BlockSpec((1, sw), row)],
        out_shape=[jax.ShapeDtypeStruct((t, sw), BF16), jax.ShapeDtypeStruct((1, gn), F32), jax.ShapeDtypeStruct((1, gn), F32),
                   jax.ShapeDtypeStruct((SSM_BLOCKS, ub, sb), F32), jax.ShapeDtypeStruct((SSM_BLOCKS, ub, sb), F32),
                   jax.ShapeDtypeStruct((SSM_BLOCKS, sb, ub), F32), jax.ShapeDtypeStruct((SSM_BLOCKS, sb, ub), F32),
                   jax.ShapeDtypeStruct((1, sw), F32)],
        scratch_shapes=[pltpu.VMEM((lc, gn), F32), pltpu.VMEM((lc, gn), F32),
                        pltpu.VMEM((lc + pad, gn), F32), pltpu.VMEM((lc + pad, gn), F32),
                        pltpu.VMEM((lc + pad, gn), F32), pltpu.VMEM((lc + pad, gn), F32),
                        pltpu.VMEM((tail + lc, gn), F32), pltpu.VMEM((tail + lc, gn), F32),
                        pltpu.VMEM((1, gn), F32), pltpu.VMEM((1, gn), F32)],
        compiler_params=_params("arbitrary"))(dge, y, uin, s_re, s_im, s_re, s_im, lrdt, ang, bbt_re, bbt_im, cct_re, cct_im, d_skip)


def _blockdiag_b(bb, sw):
    gpb = (sw // SSM_GROUP) // SSM_BLOCKS
    b4 = bb.reshape(SSM_BLOCKS, gpb, SSM_STATE, SSM_GROUP)
    eye = jnp.eye(gpb, dtype=bb.dtype)
    out = jnp.einsum('qgnh,gk->qghkn', b4, eye)
    return out.reshape(SSM_BLOCKS, gpb * SSM_GROUP, gpb * SSM_STATE)


def _blockdiag_c(cc, sw):
    gpb = (sw // SSM_GROUP) // SSM_BLOCKS
    c4 = cc.reshape(SSM_BLOCKS, gpb, SSM_GROUP, SSM_STATE)
    eye = jnp.eye(gpb, dtype=cc.dtype)
    out = jnp.einsum('qghn,gk->qgnkh', c4, eye)
    return out.reshape(SSM_BLOCKS, gpb * SSM_STATE, gpb * SSM_GROUP)


def _diag_of_b(dbb, sw):
    gpb = (sw // SSM_GROUP) // SSM_BLOCKS
    d5 = dbb.reshape(SSM_BLOCKS, gpb, SSM_GROUP, gpb, SSM_STATE)
    return jnp.einsum('qghgn->qgnh', d5).reshape(SSM_BLOCKS * gpb * SSM_STATE, SSM_GROUP)


def _diag_of_c(dcc, sw):
    gpb = (sw // SSM_GROUP) // SSM_BLOCKS
    d5 = dcc.reshape(SSM_BLOCKS, gpb, SSM_STATE, gpb, SSM_GROUP)
    return jnp.einsum('qgngh->qghn', d5).reshape(SSM_BLOCKS * gpb, SSM_GROUP, SSM_STATE)


def _ada_fwd(name, c_all, w_sh, b_sh):
    nb, d = c_all.shape
    ncol = w_sh.shape[1]
    tn = _pick(ncol, 768)

    def body(c_ref, w_ref, b_ref, o_ref):
        cv = c_ref[...]
        sil = cv * _sigmoid(cv)
        o_ref[...] = jnp.dot(sil, w_ref[...], preferred_element_type=F32, precision=lax.Precision.HIGHEST) + b_ref[...]

    return _pcall(body, name=name, grid=(ncol // tn,),
                  in_specs=[pl.BlockSpec((nb, d), lambda j: (0, 0)), pl.BlockSpec((d, tn), lambda j: (0, j)),
                            pl.BlockSpec((1, tn), lambda j: (0, j))],
                  out_specs=pl.BlockSpec((nb, tn), lambda j: (0, j)),
                  out_shape=jax.ShapeDtypeStruct((nb, ncol), F32), compiler_params=_params("parallel"))(c_all, w_sh, b_sh)


def _ada_bwd(name, c_all, dmod_sh):
    nb, d = c_all.shape
    ncol = dmod_sh.shape[1]
    tn = _pick(ncol, 768)

    def body(c_ref, g_ref, o_ref):
        cv = c_ref[...]
        sil = cv * _sigmoid(cv)
        o_ref[...] = lax.dot_general(sil, g_ref[...], (((0,), (0,)), ((), ())), preferred_element_type=F32,
                                     precision=lax.Precision.HIGHEST)

    return _pcall(body, name=name, grid=(ncol // tn,),
                  in_specs=[pl.BlockSpec((nb, d), lambda j: (0, 0)), pl.BlockSpec((nb, tn), lambda j: (0, j))],
                  out_specs=pl.BlockSpec((d, tn), lambda j: (0, j)),
                  out_shape=jax.ShapeDtypeStruct((d, ncol), F32), compiler_params=_params("parallel"))(c_all, dmod_sh)


def _place():
    return lax.axis_index("x"), lax.axis_index("y"), lax.axis_index("c")


def _allgather_small(name, blk, deps=()):
    m_per, n = blk.shape

    def body(x_ref, *rest):
        out_ref, send_sems, recv_sems, local_sem = rest[len(deps):]
        x, y, c = _place()
        me, sibling = (x, y, c), (x, y, 1 - c)
        chips = [(1 - x, y), (x, 1 - y), (1 - x, 1 - y)]

        def rows(px, py, pc):
            return out_ref.at[pl.ds((4 * px + 2 * py + pc) * m_per, m_per), :]

        def copy(k, block, to, src=None):
            return pltpu.make_async_remote_copy(
                src_ref=rows(*block) if src is None else src, dst_ref=rows(*block),
                send_sem=send_sems.at[k], recv_sem=recv_sems.at[k], device_id=to, device_id_type=MESH)

        mine = pltpu.make_async_copy(x_ref, rows(*me), local_sem)
        mine.start()
        first = [copy(0, me, sibling, src=x_ref)]
        first += [copy(1 + j, me, (*chip, c), src=x_ref) for j, chip in enumerate(chips)]
        for cp in first:
            cp.start()
        passed = [copy(4 + j, (*chip, c), sibling) for j, chip in enumerate(chips)]
        for j, chip in enumerate(chips):
            copy(1 + j, (*chip, c), me).wait_recv()
            passed[j].start()
        copy(0, sibling, me).wait_recv()
        for j, chip in enumerate(chips):
            copy(4 + j, (*chip, 1 - c), me).wait_recv()
        for cp in first + passed:
            cp.wait_send()
        mine.wait()

    return _pcall(body, name=name, out_shape=jax.ShapeDtypeStruct((N_DEV * m_per, n), blk.dtype),
                  in_specs=[pl.BlockSpec(memory_space=pltpu.VMEM)] + [ANY_SPEC] * len(deps),
                  out_specs=pl.BlockSpec(memory_space=pltpu.VMEM),
                  scratch_shapes=[pltpu.SemaphoreType.DMA((7,)), pltpu.SemaphoreType.DMA((7,)), pltpu.SemaphoreType.DMA],
                  compiler_params=pltpu.CompilerParams(vmem_limit_bytes=VMEM_LIMIT))(blk, *deps)


def _other_chips(x, y):
    return [(1 - x, y), (x, 1 - y), (1 - x, 1 - y)]


def _place_shards(shards, s_me):
    return [lax.dynamic_update_slice(lax.empty((N_CHIPS,) + s.shape, s.dtype), s[None], (s_me, 0, 0)) for s in shards]


def _ag_copy(src, land, send, recv, wi, j, chip, x, y, c, tc, both):
    hr = src.shape[0] // 2
    half = pl.ds(pl.multiple_of(c * hr, 16), hr)
    k = 3 * wi + j
    return pltpu.make_async_remote_copy(
        src_ref=src.at[half, :], dst_ref=land.at[2 * x + y, half, :],
        send_sem=send.at[2 * k + tc if both else k], recv_sem=recv.at[2 * k + c if both else k],
        device_id=(chip[0], chip[1], tc), device_id_type=MESH)


def _ag_targets(c, both):
    return (0, 1) if both else (c,)


def _ag_start(name, shards, lands, groups, direct, deps=()):
    nw, ng, nd = len(shards), len(groups), len(deps)

    def body(*refs):
        src, land = refs[:nw], refs[nw:2 * nw]
        sems = refs[2 * nw + nd:2 * nw + nd + 2 * ng]
        token = refs[-1]
        x, y, c = _place()
        for gi, grp in enumerate(groups):
            for wi, w in enumerate(grp):
                for j, chip in enumerate(_other_chips(x, y)):
                    for tc in _ag_targets(c, direct[gi]):
                        _ag_copy(src[w], land[w], sems[2 * gi], sems[2 * gi + 1], wi, j, chip, x, y, c, tc, direct[gi]).start()
        token[...] = jnp.zeros_like(token)

    sem_shapes = []
    for gi, grp in enumerate(groups):
        sem_shapes += [pltpu.SemaphoreType.DMA(((6 if direct[gi] else 3) * len(grp),))] * 2
    out_shape = sem_shapes + [pltpu.HBM(s.shape, s.dtype) for s in shards] + [pltpu.HBM(l.shape, l.dtype) for l in lands]
    out_shape += [jax.ShapeDtypeStruct((8, 128), F32)]
    res = _pcall(body, name=name, out_shape=out_shape, in_specs=[HBM_SPEC] * (2 * nw) + [ANY_SPEC] * nd,
                 out_specs=[SEM_SPEC] * (2 * ng) + [HBM_SPEC] * (2 * nw) + [pl.BlockSpec(memory_space=pltpu.VMEM)],
                 input_output_aliases={i: 2 * ng + i for i in range(2 * nw)},
                 compiler_params=pltpu.CompilerParams(has_side_effects=EFFECT))(
                     *[_hbm(s) for s in shards], *[_hbm(l) for l in lands], *deps)
    sems = [(res[2 * gi], res[2 * gi + 1]) for gi in range(ng)]
    return sems, list(res[2 * ng:2 * ng + nw]), list(res[2 * ng + nw:2 * ng + 2 * nw]), res[-1]


def _ag_wait(name, shards, lands, send, recv, both, after):
    n = len(shards)

    def body(*refs):
        src, land = refs[:n], refs[n:2 * n]
        send_sem, recv_sem = refs[2 * n], refs[2 * n + 1]
        x, y, c = _place()
        for wi in range(n):
            for j, chip in enumerate(_other_chips(x, y)):
                for tc in _ag_targets(c, both):
                    _ag_copy(src[wi], land[wi], send_sem, recv_sem, wi, j, chip, x, y, c, tc, both).wait_send()
                    _ag_copy(src[wi], land[wi], send_sem, recv_sem, wi, j, chip, chip[0], chip[1], tc, c, both).wait_recv()

    res = _pcall(body, name=name, out_shape=[pltpu.HBM(a.shape, a.dtype) for a in list(shards) + list(lands)],
                 in_specs=[HBM_SPEC] * (2 * n) + [SEM_SPEC, SEM_SPEC] + [ANY_SPEC] * len(after), out_specs=[HBM_SPEC] * (2 * n),
                 input_output_aliases={i: i for i in range(2 * n)},
                 compiler_params=pltpu.CompilerParams(has_side_effects=EFFECT))(*shards, *lands, send, recv, *after)
    return list(res[n:])


def _ag_forward(name, lands):
    n = len(lands)

    def body(*refs):
        out = refs[n:2 * n]
        send, recv = refs[2 * n], refs[2 * n + 1]
        x, y, c = _place()
        sib = (x, y, 1 - c)
        cps = []
        for wi in range(n):
            hr = out[wi].shape[1] // 2
            for j, (cx, cy) in enumerate(_other_chips(x, y)):
                got = out[wi].at[2 * cx + cy, pl.ds(pl.multiple_of(c * hr, 16), hr), :]
                cp = pltpu.make_async_remote_copy(src_ref=got, dst_ref=got, send_sem=send.at[3 * wi + j], recv_sem=recv.at[3 * wi + j],
                                                  device_id=sib, device_id_type=MESH)
                cp.start()
                cps.append(cp)
        for wi in range(n):
            hr = out[wi].shape[1] // 2
            for j, (cx, cy) in enumerate(_other_chips(x, y)):
                got = out[wi].at[2 * cx + cy, pl.ds(pl.multiple_of((1 - c) * hr, 16), hr), :]
                pltpu.make_async_remote_copy(src_ref=got, dst_ref=got, send_sem=send.at[3 * wi + j], recv_sem=recv.at[3 * wi + j],
                                             device_id=sib, device_id_type=MESH).wait_recv()
        for cp in cps:
            cp.wait_send()

    res = _pcall(body, name=name, out_shape=[jax.ShapeDtypeStruct(l.shape, l.dtype) for l in lands],
                 in_specs=[ANY_SPEC] * n, out_specs=[ANY_SPEC] * n, input_output_aliases={i: i for i in range(n)},
                 scratch_shapes=[pltpu.SemaphoreType.DMA((3 * n,)), pltpu.SemaphoreType.DMA((3 * n,))])(*lands)
    return list(res)


def _peers(x, y, c):
    offs = [(dx, dy, dc) for dx in (0, 1) for dy in (0, 1) for dc in (0, 1)][1:]
    return [(1 - x if dx else x, 1 - y if dy else y, 1 - c if dc else c) for dx, dy, dc in offs]


def _rs_copy(g_ref, land_ref, send, recv, wi, k, to, sender):
    hr = g_ref.shape[1] // 2
    return pltpu.make_async_remote_copy(
        src_ref=g_ref.at[2 * to[0] + to[1], pl.ds(pl.multiple_of(to[2] * hr, 16), hr), :], dst_ref=land_ref.at[sender],
        send_sem=send.at[7 * wi + k], recv_sem=recv.at[7 * wi + k], device_id=to, device_id_type=MESH)


def _rs_start(name, gs):
    n = len(gs)
    lands = [lax.empty((N_DEV, g.shape[1] // 2, g.shape[2]), BF16) for g in gs]

    def body(*refs):
        g, land = refs[:n], refs[n:2 * n]
        send, recv = refs[2 * n], refs[2 * n + 1]
        token = refs[-1]
        x, y, c = _place()
        me = 4 * x + 2 * y + c
        for wi in range(n):
            for k, to in enumerate(_peers(x, y, c)):
                _rs_copy(g[wi], land[wi], send, recv, wi, k, to, me).start()
        token[...] = jnp.zeros_like(token)

    out_shape = [pltpu.SemaphoreType.DMA((7 * n,))] * 2 + [pltpu.HBM(a.shape, a.dtype) for a in list(gs) + lands]
    out_shape += [jax.ShapeDtypeStruct((8, 128), F32)]
    res = _pcall(body, name=name, out_shape=out_shape, in_specs=[HBM_SPEC] * (2 * n),
                 out_specs=[SEM_SPEC] * 2 + [HBM_SPEC] * (2 * n) + [pl.BlockSpec(memory_space=pltpu.VMEM)],
                 input_output_aliases={i: 2 + i for i in range(2 * n)},
                 compiler_params=pltpu.CompilerParams(has_side_effects=EFFECT))(
                     *[_hbm(a) for a in gs], *[_hbm(a) for a in lands])
    return res[0], res[1], list(res[2:2 + n]), list(res[2 + n:2 + 2 * n]), res[-1]


def _rs_wait(name, gs, lands, send, recv, after):
    n = len(gs)

    def body(*refs):
        g, land = refs[:n], refs[n:2 * n]
        send_sem, recv_sem = refs[2 * n], refs[2 * n + 1]
        x, y, c = _place()
        me = 4 * x + 2 * y + c
        for wi in range(n):
            for k, to in enumerate(_peers(x, y, c)):
                _rs_copy(g[wi], land[wi], send_sem, recv_sem, wi, k, to, me).wait_send()
                _rs_copy(g[wi], land[wi], send_sem, recv_sem, wi, k, (x, y, c), 4 * to[0] + 2 * to[1] + to[2]).wait_recv()

    res = _pcall(body, name=name, out_shape=[pltpu.HBM(a.shape, a.dtype) for a in list(gs) + list(lands)],
                 in_specs=[HBM_SPEC] * (2 * n) + [SEM_SPEC, SEM_SPEC, ANY_SPEC], out_specs=[HBM_SPEC] * (2 * n),
                 input_output_aliases={i: i for i in range(2 * n)},
                 compiler_params=pltpu.CompilerParams(has_side_effects=EFFECT))(*gs, *lands, send, recv, after)
    return list(res[:n]), list(res[n:])


def _bc_copy(blk_ref, land_ref, send, recv, k, to, slot):
    return pltpu.make_async_remote_copy(src_ref=blk_ref, dst_ref=land_ref.at[slot], send_sem=send.at[k], recv_sem=recv.at[k],
                                        device_id=to, device_id_type=MESH)


def _bcast_start(name, blk):
    land = lax.empty((N_DEV,) + blk.shape, blk.dtype)

    def body(blk_ref, land_ref, send, recv, blk_thru, land_thru, token):
        x, y, c = _place()
        for k, to in enumerate(_peers(x, y, c)):
            _bc_copy(blk_ref, land_ref, send, recv, k, to, 4 * x + 2 * y + c).start()
        token[...] = jnp.zeros_like(token)

    return _pcall(body, name=name,
                  out_shape=[pltpu.SemaphoreType.DMA((7,)), pltpu.SemaphoreType.DMA((7,)), pltpu.HBM(blk.shape, blk.dtype),
                             pltpu.HBM(land.shape, land.dtype), jax.ShapeDtypeStruct((8, 128), F32)],
                  in_specs=[HBM_SPEC, HBM_SPEC], out_specs=[SEM_SPEC, SEM_SPEC, HBM_SPEC, HBM_SPEC, pl.BlockSpec(memory_space=pltpu.VMEM)],
                  input_output_aliases={0: 2, 1: 3}, compiler_params=pltpu.CompilerParams(has_side_effects=EFFECT))(_hbm(blk), _hbm(land))


def _bcast_wait(name, blk, land, send, recv, after):
    def body(blk_ref, land_ref, send_sem, recv_sem, after_ref, blk_thru, land_thru):
        x, y, c = _place()
        for k, to in enumerate(_peers(x, y, c)):
            _bc_copy(blk_ref, land_ref, send_sem, recv_sem, k, to, 4 * x + 2 * y + c).wait_send()
            _bc_copy(blk_ref, land_ref, send_sem, recv_sem, k, to, 4 * to[0] + 2 * to[1] + to[2]).wait_recv()

    return _pcall(body, name=name, out_shape=[pltpu.HBM(blk.shape, blk.dtype), pltpu.HBM(land.shape, land.dtype)],
                  in_specs=[HBM_SPEC, HBM_SPEC, SEM_SPEC, SEM_SPEC, ANY_SPEC], out_specs=[HBM_SPEC, HBM_SPEC],
                  input_output_aliases={0: 0, 1: 1}, compiler_params=pltpu.CompilerParams(has_side_effects=EFFECT))(
                      blk, land, send, recv, after)[1]


def _rs_sum(name, gs, lands):
    n = len(gs)

    def body(*refs):
        g_refs, land_refs, out_refs = refs[:n], refs[n:2 * n], refs[2 * n:3 * n]
        recvs = refs[3 * n:4 * n]
        local_sems, sib_send, sib_recv = refs[4 * n:]
        x, y, c = _place()
        me = 4 * x + 2 * y + c
        cps = []
        for wi in range(n):
            hr = g_refs[wi].shape[1] // 2
            own = g_refs[wi].at[2 * x + y, pl.ds(pl.multiple_of(c * hr, 16), hr), :]
            cps.append(pltpu.make_async_copy(own, recvs[wi].at[me], local_sems.at[8 * wi + 7]))
            for k, (tx, ty, tc) in enumerate(_peers(x, y, c)):
                slot = 4 * tx + 2 * ty + tc
                cps.append(pltpu.make_async_copy(land_refs[wi].at[slot], recvs[wi].at[slot], local_sems.at[8 * wi + k]))
        for cp in cps:
            cp.start()
        sibs = []
        for wi in range(n):
            hr = g_refs[wi].shape[1] // 2
            ch = _pick(hr, 64, 16)
            for cp in cps[8 * wi:8 * wi + 8]:
                cp.wait()
            base = pl.multiple_of(c * hr, 16)
            for r0 in range(0, hr, ch):
                acc = recvs[wi][0, r0:r0 + ch, :].astype(F32)
                for k in range(1, N_DEV):
                    acc = acc + recvs[wi][k, r0:r0 + ch, :].astype(F32)
                out_refs[wi][pl.ds(base + r0, ch), :] = acc
            half = out_refs[wi].at[pl.ds(base, hr), :]
            sib = pltpu.make_async_remote_copy(src_ref=half, dst_ref=half, send_sem=sib_send.at[wi], recv_sem=sib_recv.at[wi],
                                               device_id=(x, y, 1 - c), device_id_type=MESH)
            sib.start()
            sibs.append(sib)
        for wi in range(n):
            hr = g_refs[wi].shape[1] // 2
            other = out_refs[wi].at[pl.ds(pl.multiple_of((1 - c) * hr, 16), hr), :]
            pltpu.make_async_remote_copy(src_ref=other, dst_ref=other, send_sem=sib_send.at[wi], recv_sem=sib_recv.at[wi],
                                         device_id=(x, y, 1 - c), device_id_type=MESH).wait_recv()
            sibs[wi].wait_send()

    res = _pcall(
        body, name=name, out_shape=[jax.ShapeDtypeStruct(g.shape[1:], F32) for g in gs],
        in_specs=[ANY_SPEC] * (2 * n), out_specs=[pl.BlockSpec(memory_space=pltpu.VMEM)] * n,
        scratch_shapes=[pltpu.VMEM((N_DEV, g.shape[1] // 2, g.shape[2]), BF16) for g in gs]
        + [pltpu.SemaphoreType.DMA((8 * n,)), pltpu.SemaphoreType.DMA((n,)), pltpu.SemaphoreType.DMA((n,))],
        compiler_params=pltpu.CompilerParams(vmem_limit_bytes=VMEM_LIMIT))(*gs, *lands)
    return list(res)


def _gate_norm_epilogue(factor, x_in, gt, nxt, t, d):
    blk = (512, d)
    full = lambda i, j: (i, j)
    rowv = lambda i, j: (0, j)

    def epi(acc, res, scale, *norm):
        x_new = res + (factor * scale) * acc
        if not norm:
            return x_new, acc
        gv, scv, shv = norm
        r = lax.rsqrt(jnp.mean(x_new * x_new, axis=-1, keepdims=True) + EPS)
        return x_new, acc, x_new * r * gv * (1.0 + scv) + shv

    ins = [(x_in, blk, full), (gt, (1, d), rowv)] + [(v, (1, d), rowv) for v in (nxt or ())]
    outs = [((t, d), F32, blk, full), ((t, d), BF16, blk, full)] + ([((t, d), BF16, blk, full)] if nxt else [])
    return epi, ins, outs


_FULL = lambda i, j: (i, j)
_ROWV = lambda i, j: (0, j)


def _glu_epilogue(b_glu, t, w):
    def epi(acc, bv):
        g = acc + bv
        return acc, g[:, :w] * _sigmoid_fast(g[:, w:])

    return epi, [(b_glu, (1, 2 * w), _ROWV)], [((t, 2 * w), BF16, (512, 2 * w), _FULL), ((t, w), BF16, (512, w), _FULL)]


def _glu_bwd_epilogue(gv, b_glu, t, w):
    def epi(dsg, gvv, bv):
        g = gvv.astype(F32) + bv
        val, s = g[:, :w], _sigmoid_fast(g[:, w:])
        dval, dgate = dsg * s, dsg * val * s * (1.0 - s)
        return [jnp.concatenate([dval, dgate], axis=1)], [jnp.concatenate([_colsum(dval), _colsum(dgate)], axis=1)]

    return epi, [(gv, (512, 2 * w), _FULL), (b_glu, (1, 2 * w), _ROWV)], [((t, 2 * w), BF16, (512, 2 * w), _FULL)], [2 * w]


def _gates_epilogue(y_pool, uin, t, d, pw):
    cb = (2 * pw) // d

    def epi(acc, yp, glp, gls):
        return acc, _sigmoid_fast(glp) * yp.astype(F32) + _sigmoid_fast(gls) * acc

    ins = [(y_pool, (512, d), _FULL), (uin, (512, d), lambda i, j: (i, cb)), (uin, (512, d), lambda i, j: (i, cb + 1))]
    return epi, ins, [((t, d), BF16, (512, d), _FULL)] * 2


def _gates_bwd_epilogue(y_pool, y_ssm, uin, t, d, pw):
    cb = (2 * pw) // d

    def epi(dm, yp, ys, glp, gls):
        sp, ss = _sigmoid_fast(glp), _sigmoid_fast(gls)
        dgl = jnp.concatenate([dm * yp.astype(F32) * sp * (1.0 - sp), dm * ys.astype(F32) * ss * (1.0 - ss)], axis=1)
        return dm * sp, dm * ss, ("at", 2 * pw, dgl)

    ins = [(y_pool, (512, d), _FULL), (y_ssm, (512, d), _FULL),
           (uin, (512, d), lambda i, j: (i, cb)), (uin, (512, d), lambda i, j: (i, cb + 1))]
    wide = 2 * pw + 2 * d
    return epi, ins, [((t, d), BF16, (512, d), _FULL)] * 2 + [((t, wide), BF16, (512, wide), _FULL)]


def _loss_epilogue(x_in, gt, tgt, g_final, t, d):
    blk = (512, d)
    full = lambda i, j: (i, j)
    rowv = lambda i, j: (0, j)

    def epi(acc, res, scale, tv, gv):
        xv = res + (0.5 * scale) * acc
        r = lax.rsqrt(jnp.mean(xv * xv, axis=-1, keepdims=True) + EPS)
        xr = xv * r
        e = xr * gv - tv
        loss_row = 0.5 * jnp.mean(e * e, axis=-1, keepdims=True)
        dout = e * (1.0 / d)
        gd = gv * dout
        dx = r * (gd - xr * jnp.mean(gd * xr, axis=-1, keepdims=True))
        fdx = 0.5 * dx
        return [dx, scale * fdx], [_colsum(dout * xr), _colsum(loss_row * jnp.ones((1, 128), F32)), _colsum(fdx * acc)]

    ins = [(x_in, blk, full), (gt, (1, d), rowv), (tgt, blk, full), (g_final, (1, d), rowv)]
    return epi, ins, [((t, d), F32, blk, full), ((t, d), BF16, blk, full)], [d, 128, d]


def _norm_bwd_epilogue(x, dres, g, sc, up, t, d):
    blk = (512, d)
    full = lambda i, j: (i, j)
    rowv = lambda i, j: (0, j)

    def epi(dhv, xv, drv, gv, scv, *rest):
        r = lax.rsqrt(jnp.mean(xv * xv, axis=-1, keepdims=True) + EPS)
        xr = xv * r
        dn = dhv * (1.0 + scv)
        gd = gv * dn
        dx = drv + r * (gd - xr * jnp.mean(gd * xr, axis=-1, keepdims=True))
        outs, reds = [dx], [_colsum(dhv), _colsum(dhv * xr * gv), _colsum(dn * xr)]
        if up:
            yv, gtv = rest
            fdx = up[2] * dx
            outs.append(gtv * fdx)
            reds.append(_colsum(fdx * yv.astype(F32)))
        return outs, reds

    ins = [(x, blk, full), (dres, blk, full), (g, (1, d), rowv), (sc, (1, d), rowv)]
    ins += [(up[0], blk, full), (up[1], (1, d), rowv)] if up else []
    outs = [((t, d), F32, blk, full)] + ([((t, d), BF16, blk, full)] if up else [])
    return epi, ins, outs, [d] * (4 if up else 3)


def _ffn_in_act(name, h, w_in, tm=512):
    t, d = h.shape
    ns = w_in.shape[2]
    tm = _pick(t, tm, 8)
    w4 = w_in.reshape(2, 2, d, ns)

    def body(h_ref, w_ref, ab_ref, act_ref):
        hv = h_ref[...]
        a = jnp.dot(hv, w_ref[0], preferred_element_type=F32)
        b = jnp.dot(hv, w_ref[1], preferred_element_type=F32)
        ab_ref[0] = a.astype(BF16)
        ab_ref[1] = b.astype(BF16)
        act_ref[...] = (a * _sigmoid_fast(a) * b).astype(BF16)

    return _pcall(body, name=name, grid=(2, t // tm),
                  in_specs=[pl.BlockSpec((tm, d), lambda c, i: (i, 0)), pl.BlockSpec((2, None, d, ns), lambda c, i: (0, c, 0, 0))],
                  out_specs=[pl.BlockSpec((2, tm, ns), lambda c, i: (0, i, c)), pl.BlockSpec((tm, ns), lambda c, i: (i, c))],
                  out_shape=[jax.ShapeDtypeStruct((2, t, 2 * ns), BF16), jax.ShapeDtypeStruct((t, 2 * ns), BF16)],
                  compiler_params=_params("parallel", "parallel"))(h, w4)


def _dswiglu_epilogue(ab, t, f, tn):
    blk = (2, 512, tn)
    idx = lambda i, j: (0, i, j)

    def epi(dact, abv):
        a, b = abv[0].astype(F32), abv[1].astype(F32)
        s = _sigmoid_fast(a)
        return (jnp.stack([dact * b * (s * (1.0 + a * (1.0 - s))), dact * (a * s)]),)

    return epi, [(ab, blk, idx)], [((2, t, f), BF16, blk, idx)]


def _ffn_fwd(tag, x, h, gt, get_w_in, get_w_out, nxt=None, loss=None):
    t, d = x.shape
    ab, act = _ffn_in_act(tag + "_in", h, get_w_in(h))
    if loss:
        epi, epi_ins, epi_outs, epi_reds = _loss_epilogue(x, gt, *loss, t, d)
        res = _mm(tag + "_out", act, get_w_out(act), tm=512, tn=d, epi=epi, epi_ins=epi_ins, epi_outs=epi_outs,
                  epi_reds=epi_reds, b_resident=True)
        return res, (x, h, ab, act, None)
    epi, epi_ins, epi_outs = _gate_norm_epilogue(0.5, x, gt, nxt, t, d)
    res = _mm(tag + "_out", act, get_w_out(act), tm=512, tn=d, epi=epi, epi_ins=epi_ins, epi_outs=epi_outs, b_resident=True)
    return res[0], res[2], (x, h, ab, act, res[1])


def _ffn_bwd(tag, dx_new, dy, saved, g, sc, w_in, w_out, start_rs, up, deps=()):
    x, h, ab, act, _ = saved
    d = x.shape[1]
    f = act.shape[1]
    dw_out = _mm(tag + "_dwout", act, dy, ta=True, tm=1408, tn=d, tk=2048, deps=deps)
    tok = start_rs("out", dw_out.reshape(N_CHIPS, f // N_CHIPS, d))
    epi, epi_ins, epi_outs = _dswiglu_epilogue(ab, x.shape[0], f, w_in.shape[2])
    dab = _mm(tag + "_dact", dy, w_out, tb=True, tm=512, tn=w_in.shape[2], epi=epi, epi_ins=epi_ins, epi_outs=epi_outs,
              deps=(tok,), j_outer=True)
    dw_in = _mm(tag + "_dwin", h, dab, ta=True, out_stacked=True, b_halves=True, tm=d, tn=1408, tk=2048)
    tok = start_rs("in", dw_in)
    epi, epi_ins, epi_outs, epi_reds = _norm_bwd_epilogue(x, dx_new, g, sc, up, x.shape[0], d)
    res = _mm(tag + "_dh", dab, w_in, tb=True, b_stacked=True, a_halves=True, tn=d, tk=5632, deps=(tok,), epi=epi,
              epi_ins=epi_ins, epi_outs=epi_outs, epi_reds=epi_reds, b_resident=True)
    if up:
        return res
    return res[0], None, res[1], res[2], res[3], None


def _row(v):
    return v.reshape(1, -1)


def _pack(parts):
    cols = []
    for p in parts:
        flat = p.reshape(-1).astype(F32)
        padn = (-flat.shape[0]) % 128
        cols.append(jnp.pad(flat, (0, padn)) if padn else flat)
    flat = jnp.concatenate(cols)
    padn = (-flat.shape[0]) % 1024
    if padn:
        flat = jnp.pad(flat, (0, padn))
    return flat.reshape(-1, 128)


def _unpack(packed, shapes):
    flat = packed.reshape(-1)
    out, off = [], 0
    for s in shapes:
        n = math.prod(s)
        out.append(flat[off:off + n].reshape(s))
        off += n + ((-n) % 128)
    return out


SMALL = ['b_ada', 'g_ffn1', 'g_mix', 'pool_w', 'pool_b', 'pool_scale', 'ssm_lam_re_log', 'ssm_lam_im', 'ssm_log_dt',
         'ssm_b_re', 'ssm_b_im', 'ssm_c_re', 'ssm_c_im', 'ssm_d', 'b_glu', 'g_ffn2', 'g_final']
BIG = ['w_ffn1_in', 'w_ffn1_out', 'w_in', 'w_pool_up', 'w_glu', 'w_ssm_up', 'w_out', 'w_ffn2_in', 'w_ffn2_out']
AG_GROUPS = [[0], [1], [2, 3, 4, 5, 6], [7, 8]]
AG_DIRECT = [False, False, False, True]
SMALL_LATE = ['b_ada', 'g_ffn1']
RS_SUM_GROUPS = [[0, 1], [2, 3], [4, 5]]
WEIGHTS = ['w_ada', 'b_ada', 'g_ffn1', 'w_ffn1_in', 'w_ffn1_out', 'g_mix', 'w_in', 'pool_w', 'pool_b', 'pool_scale', 'w_pool_up',
           'ssm_lam_re_log', 'ssm_lam_im', 'ssm_log_dt', 'ssm_b_re', 'ssm_b_im', 'ssm_c_re', 'ssm_c_im', 'ssm_d', 'w_glu', 'b_glu',
           'w_ssm_up', 'w_out', 'g_ffn2', 'w_ffn2_in', 'w_ffn2_out', 'g_final']


def kernel(x, c, w_ada, b_ada, g_ffn1, w_ffn1_in, w_ffn1_out, g_mix, w_in, pool_w, pool_b, pool_scale, w_pool_up, ssm_lam_re_log, ssm_lam_im, ssm_log_dt, ssm_b_re, ssm_b_im, ssm_c_re, ssm_c_im, ssm_d, w_glu, b_glu, w_ssm_up, w_out, g_ffn2, w_ffn2_in, w_ffn2_out, g_final, loss_target, m_w_ada, m_b_ada, m_g_ffn1, m_w_ffn1_in, m_w_ffn1_out, m_g_mix, m_w_in, m_pool_w, m_pool_b, m_pool_scale, m_w_pool_up, m_ssm_lam_re_log, m_ssm_lam_im, m_ssm_log_dt, m_ssm_b_re, m_ssm_b_im, m_ssm_c_re, m_ssm_c_im, m_ssm_d, m_w_glu, m_b_glu, m_w_ssm_up, m_w_out, m_g_ffn2, m_w_ffn2_in, m_w_ffn2_out, m_g_final, v_w_ada, v_b_ada, v_g_ffn1, v_w_ffn1_in, v_w_ffn1_out, v_g_mix, v_w_in, v_pool_w, v_pool_b, v_pool_scale, v_w_pool_up, v_ssm_lam_re_log, v_ssm_lam_im, v_ssm_log_dt, v_ssm_b_re, v_ssm_b_im, v_ssm_c_re, v_ssm_c_im, v_ssm_d, v_w_glu, v_b_glu, v_w_ssm_up, v_w_out, v_g_ffn2, v_w_ffn2_in, v_w_ffn2_out, v_g_final):
    args = dict(locals())
    wt = {n: args[n] for n in WEIGHTS}
    mom = {n: args["m_" + n] for n in WEIGHTS}
    var = {n: args["v_" + n] for n in WEIGHTS}

    t, d = x.shape[1], x.shape[2]
    pw = pool_b.shape[1]
    sw = ssm_d.shape[1]
    ngrp = sw // SSM_GROUP
    gn = ngrp * SSM_STATE
    xi, yi, ci = _place()
    b_me = 4 * xi + 2 * yi + ci
    s_me = 2 * xi + yi
    x2d = x[0]
    tgt = loss_target[0]

    c_all = _allgather_small("ag_c", c.reshape(8, d // 8)).reshape(N_DEV, d)
    ncol = w_ada.shape[2]
    b_sh = lax.dynamic_slice(b_ada, (0, s_me * ncol), (1, ncol))
    mod_sh = _ada_fwd("ada_fwd", c_all, w_ada[0], b_sh)
    md_send, md_recv, mod_sh, md_land, tok = _bcast_start("mod_start", mod_sh)

    shards = [wt[n][0].astype(BF16) for n in BIG]
    ag_sems, shards_t, lands_t, tok = _ag_start("ag_start", shards, _place_shards(shards, s_me), AG_GROUPS, AG_DIRECT, deps=(tok,))
    md_land = _bcast_wait("mod_wait", mod_sh, md_land, md_send, md_recv, tok)
    mod_all = lax.dynamic_update_slice(md_land, mod_sh[None], (b_me, 0, 0))
    full = {}

    def weights(gi, *after):
        grp = AG_GROUPS[gi]
        if BIG[grp[0]] not in full:
            ls = _ag_wait("ag_wait%d" % gi, [shards_t[w] for w in grp], [lands_t[w] for w in grp], *ag_sems[gi],
                          AG_DIRECT[gi], after)
            for w, l in zip(grp, ls if AG_DIRECT[gi] else _ag_forward("ag_fwd%d" % gi, ls)):
                full[BIG[w]] = l
        return full

    mod_me = jnp.concatenate([lax.dynamic_slice(mod_all, (2 * s, b_me, 0), (1, 1, ncol))[0] for s in range(N_CHIPS)], axis=1)
    mod = mod_me.reshape(9, d)
    sh1, sc1, gt1, sh2, sc2, gt2, sh3, sc3, gt3 = [mod[k:k + 1] for k in range(9)]

    f = w_ffn1_out.shape[1] * N_CHIPS

    col = lambda a: a.reshape(gn, 1)
    lrl_c, li_c = col(ssm_lam_re_log), col(ssm_lam_im)
    ldt_c = col(jnp.broadcast_to(ssm_log_dt.reshape(ngrp, 1), (ngrp, SSM_STATE)))
    b_re2, b_im2 = ssm_b_re.reshape(gn, SSM_GROUP), ssm_b_im.reshape(gn, SSM_GROUP)
    lrdt_c, ang_c, bb_re, bb_im = _ssm_prep("ssm_prep", lrl_c, li_c, ldt_c, b_re2, b_im2)
    lrdt, ang = lrdt_c.reshape(1, gn), ang_c.reshape(1, gn)
    tr = lambda a: jnp.swapaxes(a, 1, 2)
    bbd = [_blockdiag_b(v, sw).astype(BF16) for v in (bb_re, bb_im)]
    ccd = [_blockdiag_c(v[0], sw).astype(BF16) for v in (ssm_c_re, ssm_c_im)]
    bbd_t, ccd_t = [tr(v) for v in bbd], [tr(v) for v in ccd]
    early = [n for n in SMALL if n not in SMALL_LATE]
    packs = {}
    for tag, names, extra in (("early", early, []), ("late", SMALL_LATE, [jnp.zeros((128,), F32)])):
        packs[tag] = [_pack([src[n] for n in names] + extra) for src in (wt, mom, var)]
    shadow = [lrdt, ang, ldt_c, *bbd, *ccd, *bbd_t, *ccd_t, *packs["early"], *packs["late"]]

    h1 = _norm_fwd("ffn1_norm", x2d, g_ffn1, sc1, sh1, deps=(tok,))
    x1, h2, sav1 = _ffn_fwd("ffn1", x2d, h1, gt1, lambda after: weights(0, after, *shadow)['w_ffn1_in'],
                            lambda after: weights(1, after)['w_ffn1_out'].reshape(f, d), (g_mix, sc2, sh2))
    weights(2, h2)
    wo = full['w_out'].reshape(d, d)
    uin = _mm("mix_in", h2, full['w_in'], b_stacked=True, tm=1024, tn=768, out_dtype=F32, j_outer=True)
    p_pool, z_pool = _pool_fwd("pool_fwd", uin, pool_w[0], pool_b, pool_scale, pw)
    y_pool = _mm("pool_up", p_pool, full['w_pool_up'], b_stacked=True, tm=1024, tn=d)
    y_s, ge, s_re, s_im = _ssm_fwd("ssm_fwd", uin, lrdt, ang, *bbd, *ccd, ssm_d, sw)
    epi, epi_ins, epi_outs = _glu_epilogue(b_glu, t, sw)
    gv, sg = _mm("glu_in", ge, full['w_glu'], b_stacked=True, tm=512, tn=d, epi=epi, epi_ins=epi_ins, epi_outs=epi_outs,
                 b_resident=True)
    epi, epi_ins, epi_outs = _gates_epilogue(y_pool, uin, t, d, pw)
    y_ssm, merged = _mm("ssm_up", sg, full['w_ssm_up'], b_stacked=True, tm=512, tn=d, epi=epi, epi_ins=epi_ins,
                        epi_outs=epi_outs, b_resident=True)
    epi, epi_ins, epi_outs = _gate_norm_epilogue(1.0, x1, gt2, (g_ffn2, sc3, sh3), t, d)
    x2, y2, h3 = _mm("mix_out", merged, wo, tm=512, tn=d, epi=epi, epi_ins=epi_ins, epi_outs=epi_outs, b_resident=True)

    (dx3, dy3, dg_final, loss_v, dgt3), sav3 = _ffn_fwd(
        "ffn2", x2, h3, gt3, lambda after: weights(3, after)['w_ffn2_in'],
        lambda after: weights(3, after)['w_ffn2_out'].reshape(f, d), loss=(tgt, _row(g_final)))

    gs = {}
    rs_open = []

    def start_rs(names, gbs):
        send, recv, g_thru, land_thru, token = _rs_start("rs_start_" + names[0], gbs)
        rs_open.append((names, send, recv, g_thru, land_thru))
        return token

    dx2, dy2, dsh3, dsc3, gs['g_ffn2'], dgt2 = _ffn_bwd(
        "ffn2b", dx3, dy3, sav3, g_ffn2, sc3, full['w_ffn2_in'], full['w_ffn2_out'].reshape(f, d),
        lambda key, g: start_rs(['w_ffn2_' + key], [g]), (y2, gt2, 1.0))

    epi, epi_ins, epi_outs = _gates_bwd_epilogue(y_pool, y_ssm, uin, t, d, pw)
    dy_pool, dy_ssm, duin = _mm("mix_dmerged", dy2, wo, tb=True, tm=512, tn=d, epi=epi, epi_ins=epi_ins, epi_outs=epi_outs,
                                b_resident=True)
    g_wo = _mm("mix_dwout", merged, dy2, ta=True, tm=d, tn=d, tk=2048).reshape(N_CHIPS, d // N_CHIPS, d)

    dp = _mm("pool_dup", dy_pool, full['w_pool_up'], tb=True, b_stacked=True, tm=1024, tn=pw, tk=d)
    g_wpu = _mm("pool_dwup", p_pool, dy_pool, ta=True, out_stacked=True, tm=pw, tn=d, tk=4096)
    duin, gs['pool_w'], gs['pool_b'], gs['pool_scale'] = _pool_bwd("pool_bwd", dp, z_pool, pool_w[0], pool_b, pool_scale, duin)

    g_wsu = _mm("ssm_dwup", sg, dy_ssm, ta=True, out_stacked=True, tm=sw, tn=d, tk=4096)
    tok = start_rs(['w_out', 'w_pool_up', 'w_ssm_up'], [g_wo, g_wpu, g_wsu])
    epi, epi_ins, epi_outs, epi_reds = _glu_bwd_epilogue(gv, b_glu, t, sw)
    dgv, gs['b_glu'] = _mm("ssm_dup", dy_ssm, full['w_ssm_up'], tb=True, b_stacked=True, tm=512, tn=sw, tk=d, deps=(tok,),
                           epi=epi, epi_ins=epi_ins, epi_outs=epi_outs, epi_reds=epi_reds, b_resident=True)
    dge = _mm("glu_dge", dgv, full['w_glu'], tb=True, b_stacked=True, tm=1024, tn=sw, tk=d)
    g_wglu = _mm("glu_dw", ge, dgv, ta=True, out_stacked=True, tm=sw, tn=d, tk=4096)
    (duin, g_abre, g_abim, g_bbd_re, g_bbd_im, g_ccd_re, g_ccd_im, gs['ssm_d']) = _ssm_bwd(
        "ssm_bwd", dge, y_s, uin, s_re, s_im, lrdt, ang, *bbd_t, *ccd_t, ssm_d, sw, duin)
    gs['ssm_c_re'], gs['ssm_c_im'] = _diag_of_c(g_ccd_re, sw), _diag_of_c(g_ccd_im, sw)
    d_lrl, d_li, d_ldt, d_bre, d_bim = _ssm_param_bwd(
        "ssm_param_bwd", lrl_c, li_c, ldt_c, b_re2, b_im2, g_abre.reshape(gn, 1), g_abim.reshape(gn, 1),
        _diag_of_b(g_bbd_re, sw), _diag_of_b(g_bbd_im, sw))
    gs['ssm_lam_re_log'], gs['ssm_lam_im'] = d_lrl, d_li
    gs['ssm_log_dt'] = jnp.sum(d_ldt.reshape(ngrp, SSM_STATE), axis=1)
    gs['ssm_b_re'], gs['ssm_b_im'] = d_bre, d_bim

    g_win =_mm("mix_dwin", h2, duin, ta=True, out_stacked=True, tm=d, tn=768, tk=4096)
    tok = start_rs(['w_glu', 'w_in'], [g_wglu, g_win])
    epi, epi_ins, epi_outs, epi_reds = _norm_bwd_epilogue(x1, dx2, g_mix, sc2, (sav1[4], gt1, 0.5), t, d)
    dx1, dy1, dsh2, dsc2, gs['g_mix'], dgt1 = _mm(
        "mix_dh", duin, full['w_in'], tb=True, b_stacked=True, tn=d, tk=3072, deps=(tok,), epi=epi, epi_ins=epi_ins,
        epi_outs=epi_outs, epi_reds=epi_reds, b_resident=True)

    gs['g_final'] = dg_final
    sg_blk = _pack([gs[n] for n in early])
    sg_send, sg_recv, sg_blk, sg_land, tok = _bcast_start("sg_start", sg_blk)

    dx0, _, dsh1, dsc1, gs['g_ffn1'], _ = _ffn_bwd(
        "ffn1b", dx1, dy1, sav1, g_ffn1, sc1, full['w_ffn1_in'], full['w_ffn1_out'].reshape(f, d),
        lambda key, g: start_rs(['w_ffn1_' + key], [g]), None, deps=(tok,))

    gs['b_ada'] = jnp.concatenate([dsh1, dsc1, dgt1, dsh2, dsc2, dgt2, dsh3, dsc3, dgt3], axis=1)
    lt_blk = _pack([gs[n] for n in SMALL_LATE] + [loss_v])
    lt_send, lt_recv, lt_blk, lt_land, tok = _bcast_start("late_start", lt_blk)

    grads, delta, new_m, new_v = {}, {}, {}, {}

    def small_update(tag, names, blk, land):
        g8 = lax.dynamic_update_slice(land, blk[None], (b_me, 0, 0)).reshape(-1, 128)
        w_pack, m_pack, v_pack = packs[tag]
        shapes = [wt[n].shape for n in names]
        per_param, packed = _adamw_small("adamw_small_" + tag, w_pack, g8, m_pack, v_pack, shapes)
        for k, dst in enumerate((grads, delta, new_m, new_v)):
            rest = _unpack(packed[k], shapes)
            for i, n in enumerate(names):
                dst[n] = per_param[i][k] if per_param[i] is not None else rest[i]
        return g8

    after = tok
    for group in RS_SUM_GROUPS:
        names, g_done, land_done = [], [], []
        for k in group:
            nk, send, recv, g_thru, land_thru = rs_open[k]
            gd, ld = _rs_wait("rs_wait_" + nk[0], g_thru, land_thru, send, recv, after)
            names, g_done, land_done = names + nk, g_done + gd, land_done + ld
        for n, g_sum in zip(names, _rs_sum("rs_sum_" + names[0], g_done, land_done)):
            g_out, dl, mn, vn = _adamw("adamw_" + n, wt[n][0], g_sum, mom[n][0], var[n][0])
            grads[n], delta[n], new_m[n], new_v[n] = g_out[None], dl[None], mn[None], vn[None]
            after = dl
        if group is RS_SUM_GROUPS[-2]:
            small_update("early", early, sg_blk, _bcast_wait("sg_wait", sg_blk, sg_land, sg_send, sg_recv, after))

    lt_land = _bcast_wait("late_wait", lt_blk, lt_land, lt_send, lt_recv, after)
    late8 = small_update("late", SMALL_LATE, lt_blk, lt_land).reshape(N_DEV, -1)
    loss = jnp.sum(late8[:, 10 * d])
    dmod_sh = lax.dynamic_slice(late8, (0, s_me * ncol), (N_DEV, ncol))
    g_w_ada = _ada_bwd("ada_bwd", c_all, dmod_sh)
    dl, mn, vn = _adamw("adamw_w_ada", w_ada[0], g_w_ada, m_w_ada[0], v_w_ada[0], with_g=False)
    grads['w_ada'], delta['w_ada'], new_m['w_ada'], new_v['w_ada'] = g_w_ada[None], dl[None], mn[None], vn[None]

    return (loss, dx0[None], *[grads[n] for n in WEIGHTS], *[delta[n] for n in WEIGHTS],
            *[new_m[n] for n in WEIGHTS], *[new_v[n] for n in WEIGHTS])
```

```python
import functools
import math

import jax
import jax.numpy as jnp
from jax import lax
from jax.experimental import pallas as pl
from jax.experimental.pallas import tpu as pltpu

F32 = jnp.float32
BF16 = jnp.bfloat16
MESH = pl.DeviceIdType.MESH

EPS = 1e-6
POOL_WINDOWS = (2, 4, 8, 16)
POOL_HALO = 16
SSM_GROUP = 16
SSM_STATE = 64
SSM_BLOCKS = 4
N_DEV = 8
N_CHIPS = 4
ADAM_LR = 0.001
ADAM_B1 = 0.9
ADAM_B2 = 0.999
ADAM_EPS = 1e-08
ADAM_WD = 0.01
ADAM_STEP = 10
VMEM_LIMIT = 56 * 1024 * 1024


ANY_SPEC = pl.BlockSpec(memory_space=pl.ANY)
HBM_SPEC = pl.BlockSpec(memory_space=pltpu.HBM)
SEM_SPEC = pl.BlockSpec(memory_space=pltpu.SEMAPHORE)
EFFECT = pltpu.SideEffectType.DATAFLOW_SIDE_EFFECTING


def _hbm(a):
    return pltpu.with_memory_space_constraint(a, pltpu.HBM)


def _pcall(body, **kw):
    return pl.pallas_call(body, **kw)


def _params(*sem):
    return pltpu.CompilerParams(dimension_semantics=sem, vmem_limit_bytes=VMEM_LIMIT)


def _pick(n, cap, mult=128):
    if n <= cap:
        return n
    best = None
    for d in range(mult, cap + 1, mult):
        if n % d == 0:
            best = d
    assert best is not None, (n, cap, mult)
    return best


def _sigmoid(v):
    return 1.0 / (1.0 + jnp.exp(-v))


def _rowwise(name, fn, ins, params, outs, reds, tm, deps=()):
    t = ins[0][0].shape[0]
    tm = min(tm, t)
    nb = t // tm
    ni, npar, no, nd = len(ins), len(params), len(outs), len(deps)

    def body(*refs):
        iv = [r[...] for r in refs[:ni]]
        pv = [r[...] for r in refs[ni:ni + npar]]
        o_refs = refs[ni + npar + nd:ni + npar + nd + no]
        r_refs = refs[ni + npar + nd + no:]
        ovals, rvals = fn(iv, pv)
        for o_ref, val in zip(o_refs, ovals):
            off = 0
            if isinstance(val, tuple) and val[0] == "at":
                _, off, val = val
            parts = val if isinstance(val, (list, tuple)) else [val]
            for p in parts:
                o_ref[:, off:off + p.shape[1]] = p.astype(o_ref.dtype)
                off += p.shape[1]
        if r_refs:
            @pl.when(pl.program_id(0) == 0)
            def _():
                for r in r_refs:
                    r[...] = jnp.zeros_like(r)
            for r, val in zip(r_refs, rvals):
                r[...] += val

    in_specs = [pl.BlockSpec((tm, w), functools.partial(lambda i, cb: (i, cb), cb=cb)) for (_, w, cb) in ins]
    in_specs += [pl.BlockSpec(p.shape, lambda i: (0, 0)) for p in params]
    in_specs += [ANY_SPEC] * nd
    out_shape = [jax.ShapeDtypeStruct((t, w), dt) for (w, dt) in outs]
    out_shape += [jax.ShapeDtypeStruct((1, w), F32) for w in reds]
    out_specs = [pl.BlockSpec((tm, w), lambda i: (i, 0)) for (w, _) in outs]
    out_specs += [pl.BlockSpec((1, w), lambda i: (0, 0)) for w in reds]
    res = _pcall(body, name=name, grid=(nb,), in_specs=in_specs, out_specs=out_specs, out_shape=out_shape,
                 compiler_params=_params("arbitrary"))(*[a for a, _, _ in ins], *params, *deps)
    return res


def _colsum(v):
    return jnp.sum(v, axis=0, keepdims=True)


def _mm(name, a, b, *, ta=False, tb=False, b_stacked=False, out_stacked=False, tm=512, tn=1024, tk=2816,
        out_dtype=BF16, epi=None, epi_ins=(), epi_outs=None, epi_reds=(), deps=(), j_outer=False, a_halves=False,
        b_halves=False, b_resident=False):
    if a_halves:
        _, m, kdim = a.shape
        kdim *= 2
    elif ta:
        kdim, m = a.shape
    else:
        m, kdim = a.shape
    ns = None
    if b_stacked:
        ns = b.shape[2]
        n = b.shape[1] if tb else N_CHIPS * ns
        assert kdim == (N_CHIPS * ns if tb else b.shape[1]), (name, a.shape, b.shape)
    else:
        if b_halves:
            n = 2 * b.shape[2]
            assert kdim == b.shape[1] and not tb, (name, a.shape, b.shape)
        else:
            n = b.shape[0] if tb else b.shape[1]
            assert kdim == (b.shape[1] if tb else b.shape[0]), (name, a.shape, b.shape)
        if out_stacked:
            ns = n // N_CHIPS

    def shards(want):
        return max(g for g in (1, 2, 4) if g * ns <= max(want, ns))

    tm = _pick(m, tm, 128 if ta else 8)
    gn = gk = 1
    if (b_stacked and not tb) or out_stacked:
        gn = shards(min(tn, n // 2) if b_halves else tn)
        tn = gn * ns
    else:
        tn = _pick(n, tn)
    if b_stacked and tb:
        gk = shards(tk)
        tk = gk * ns
    else:
        tk = _pick(kdim, tk, 8 if ta else 128)
    nm, nn, nk = m // tm, n // tn, kdim // tk

    def ij(f):
        return (lambda g0, g1, k: f(g1, g0, k)) if j_outer else f

    if a_halves:
        assert b_stacked and tb and gk == N_CHIPS and nk == 1, name
        a_spec = pl.BlockSpec((2, tm, kdim // 2), ij(lambda i, j, k: (0, i, 0)))
    elif ta:
        a_spec = pl.BlockSpec((tk, tm), ij(lambda i, j, k: (k, i)))
    else:
        a_spec = pl.BlockSpec((tm, tk), ij(lambda i, j, k: (i, k)))
    if b_stacked and not tb:
        b_spec = pl.BlockSpec((gn, tk, ns), ij(lambda i, j, k: (j, k, 0)))
    elif b_stacked and tb:
        b_spec = pl.BlockSpec((gk, tn, ns), ij(lambda i, j, k: (k, j, 0)))
    elif b_halves:
        bph = (n // 2) // tn
        b_spec = pl.BlockSpec((None, tk, tn), ij(lambda i, j, k: (j // bph, k, j % bph)))
    elif tb:
        b_spec = pl.BlockSpec((tn, tk), ij(lambda i, j, k: (j, k)))
    else:
        b_spec = pl.BlockSpec((tk, tn), ij(lambda i, j, k: (k, j)))
    dims = (((0 if ta else 1,), (1 if tb else 0,)), ((), ()))

    if epi_outs is None:
        if out_stacked:
            epi_outs = [((N_CHIPS, m, ns), out_dtype, (gn, tm, ns), lambda i, j: (j, i, 0))]
        else:
            epi_outs = [((m, n), out_dtype, (tm, tn), lambda i, j: (i, j))]
    ne, no, nd, nr = len(epi_ins), len(epi_outs), len(deps), len(epi_reds)
    assert not nr or (nn == 1 and not j_outer), name

    def body(a_ref, b_ref, *rest):
        e_refs = rest[:ne]
        o_refs = rest[ne + nd:ne + nd + no]
        r_refs = rest[ne + nd + no:ne + nd + no + nr]
        scratch = rest[ne + nd + no + nr:]
        av = None if a_halves else a_ref[...].astype(BF16)
        if b_stacked and not tb:
            parts = [lax.dot_general(av, b_ref[s].astype(BF16), dims, preferred_element_type=F32) for s in range(gn)]
        elif b_stacked and tb:
            p = None
            for s in range(gk):
                if a_halves:
                    a_s = a_ref[s // 2, :, (s % 2) * ns:(s % 2 + 1) * ns].astype(BF16)
                else:
                    a_s = av[:, s * ns:(s + 1) * ns]
                q = lax.dot_general(a_s, b_ref[s].astype(BF16), dims, preferred_element_type=F32)
                p = q if p is None else p + q
            parts = [p]
        else:
            parts = [lax.dot_general(av, b_ref[...].astype(BF16), dims, preferred_element_type=F32)]

        def finish(acc_parts):
            if epi is None and out_stacked:
                acc = acc_parts[0]
                for s in range(gn):
                    o_refs[0][s] = acc[:, s * ns:(s + 1) * ns].astype(out_dtype)
            elif epi is None:
                w = acc_parts[0].shape[1]
                for s, part in enumerate(acc_parts):
                    o_refs[0][:, s * w:(s + 1) * w] = part.astype(out_dtype)
            else:
                acc = acc_parts[0] if len(acc_parts) == 1 else jnp.concatenate(acc_parts, axis=1)
                vals = epi(acc, *[r[...] for r in e_refs])
                if nr:
                    vals, reds = vals

                    @pl.when(pl.program_id(0) == 0)
                    def _():
                        for r in r_refs:
                            r[...] = jnp.zeros_like(r)
                    for r, val in zip(r_refs, reds):
                        r[...] += val
                for o_ref, val in zip(o_refs, vals):
                    if isinstance(val, tuple) and val[0] == "at":
                        o_ref[:, val[1]:val[1] + val[2].shape[1]] = val[2].astype(o_ref.dtype)
                    else:
                        o_ref[...] = val.astype(o_ref.dtype)

        if nk == 1:
            finish(parts)
        else:
            acc_ref = scratch[0]
            k = pl.program_id(2)
            w = parts[0].shape[1]

            @pl.when(k == 0)
            def _():
                for s, part in enumerate(parts):
                    acc_ref[:, s * w:(s + 1) * w] = part

            @pl.when(k > 0)
            def _():
                for s, part in enumerate(parts):
                    acc_ref[:, s * w:(s + 1) * w] += part

            @pl.when(k == nk - 1)
            def _():
                finish([acc_ref[...]])

    def _ij(f):
        return ij(lambda i, j, k: f(i, j))

    if b_resident:
        assert nk == 1 and (nn == 1 or j_outer), name
        b_spec = pl.BlockSpec(b_spec.block_shape, b_spec.index_map, pipeline_mode=pl.Buffered(1))
    in_specs = [a_spec, b_spec] + [pl.BlockSpec(blk, _ij(f)) for (_, blk, f) in epi_ins] + [ANY_SPEC] * nd
    out_specs = [pl.BlockSpec(blk, _ij(f)) for (_, _, blk, f) in epi_outs]
    out_specs += [pl.BlockSpec((1, w), lambda *_: (0, 0)) for w in epi_reds]
    out_shape = [jax.ShapeDtypeStruct(s, dt) for (s, dt, _, _) in epi_outs] + [jax.ShapeDtypeStruct((1, w), F32) for w in epi_reds]
    scratch = [pltpu.VMEM((tm, tn), F32)] if nk > 1 else []
    grid = (nn, nm, nk) if j_outer else (nm, nn, nk)
    res = _pcall(body, name=name, grid=grid, in_specs=in_specs, out_specs=out_specs, out_shape=out_shape, scratch_shapes=scratch,
                 compiler_params=_params(*(("arbitrary",) * 3 if nr else ("parallel", "parallel", "arbitrary"))))(
                     a, b, *[x for x, _, _ in epi_ins], *deps)
    return res[0] if len(res) == 1 else res


def _norm_fwd(name, x, g, sc, sh, deps=()):
    d = x.shape[1]

    def fn(iv, pv):
        (xv,), (gv, scv, shv) = iv, pv
        r = lax.rsqrt(jnp.mean(xv * xv, axis=-1, keepdims=True) + EPS)
        return [xv * r * gv * (1.0 + scv) + shv], []

    return _rowwise(name, fn, [(x, d, 0)], [g, sc, sh], [(d, BF16)], [], 512, deps=deps)[0]


def _norm_bwd(name, dh, x, dres, g, sc, y_up, gt_up, factor_up):
    d = x.shape[1]
    up = y_up is not None

    def fn(iv, pv):
        dhv, xv, drv = iv[:3]
        gv, scv = pv[:2]
        dhv = dhv.astype(F32)
        r = lax.rsqrt(jnp.mean(xv * xv, axis=-1, keepdims=True) + EPS)
        xr = xv * r
        dn = dhv * (1.0 + scv)
        gd = gv * dn
        dx = drv + r * (gd - xr * jnp.mean(gd * xr, axis=-1, keepdims=True))
        outs, reds = [dx], [_colsum(dhv), _colsum(dhv * xr * gv), _colsum(dn * xr)]
        if up:
            fdx = factor_up * dx
            outs.append(pv[2] * fdx)
            reds.append(_colsum(fdx * iv[3].astype(F32)))
        return outs, reds

    ins = [(dh, d, 0), (x, d, 0), (dres, d, 0)] + ([(y_up, d, 0)] if up else [])
    res = _rowwise(name, fn, ins, [g, sc] + ([gt_up] if up else []), [(d, F32)] + ([(d, BF16)] if up else []),
                   [d] * (4 if up else 3), 256)
    if up:
        dx, dy_up, dsh, dsc, dg, dgt_up = res
        return dx, dy_up, dsh, dsc, dg, dgt_up
    dx, dsh, dsc, dg = res
    return dx, None, dsh, dsc, dg, None


def _loss_bwd(name, x, tgt, g, y_up, gt_up, factor_up):
    d = x.shape[1]

    def fn(iv, pv):
        (xv, tv, yv), (gv, gtv) = iv, pv
        r = lax.rsqrt(jnp.mean(xv * xv, axis=-1, keepdims=True) + EPS)
        xr = xv * r
        e = xr * gv - tv
        loss_row = 0.5 * jnp.mean(e * e, axis=-1, keepdims=True)
        dout = e * (1.0 / d)
        gd = gv * dout
        dx = r * (gd - xr * jnp.mean(gd * xr, axis=-1, keepdims=True))
        fdx = factor_up * dx
        return [dx, gtv * fdx], [_colsum(dout * xr), _colsum(loss_row * jnp.ones((1, 128), F32)), _colsum(fdx * yv.astype(F32))]

    return _rowwise(name, fn, [(x, d, 0), (tgt, d, 0), (y_up, d, 0)], [g, gt_up], [(d, F32), (d, BF16)], [d, 128, d], 256)


def _resgate_bwd(name, dx, y, gt, factor, deps=()):
    d = dx.shape[1]

    def fn(iv, pv):
        (dxv, yv), (gtv,) = iv, pv
        return [factor * gtv * dxv], [_colsum(factor * dxv * yv.astype(F32))]

    return _rowwise(name, fn, [(dx, d, 0), (y, d, 0)], [gt], [(d, BF16)], [d], 512, deps=deps)


def _swiglu_fwd(name, ab):
    f = ab.shape[1] // 2

    def fn(iv, pv):
        a, b = iv[0].astype(F32), iv[1].astype(F32)
        return [a * _sigmoid(a) * b], []

    return _rowwise(name, fn, [(ab, f, 0), (ab, f, 1)], [], [(f, BF16)], [], 256)[0]


def _swiglu_bwd(name, dact, ab, deps=()):
    f = ab.shape[1] // 2

    def fn(iv, pv):
        dv, a, b = iv[0].astype(F32), iv[1].astype(F32), iv[2].astype(F32)
        s = _sigmoid(a)
        return [[dv * b * (s * (1.0 + a * (1.0 - s))), dv * (a * s)]], []

    return _rowwise(name, fn, [(dact, f, 0), (ab, f, 0), (ab, f, 1)], [], [(2 * f, BF16)], [], 256, deps=deps)[0]


def _gates_fwd(name, y_pool, y_ssm, uin, d, pw):
    cb = (2 * pw) // d

    def fn(iv, pv):
        yp, ys, glp, gls = [v.astype(F32) for v in iv]
        return [_sigmoid(glp) * yp + _sigmoid(gls) * ys], []

    return _rowwise(name, fn, [(y_pool, d, 0), (y_ssm, d, 0), (uin, d, cb), (uin, d, cb + 1)], [], [(d, BF16)], [], 256)[0]


def _gates_bwd(name, dm, y_pool, y_ssm, uin, d, pw):
    cb = (2 * pw) // d

    def fn(iv, pv):
        dmv, yp, ys, glp, gls = [v.astype(F32) for v in iv]
        sp, ss = _sigmoid(glp), _sigmoid(gls)
        return [dmv * sp, dmv * ss, ("at", 2 * pw, [dmv * yp * sp * (1.0 - sp), dmv * ys * ss * (1.0 - ss)])], []

    return _rowwise(name, fn, [(dm, d, 0), (y_pool, d, 0), (y_ssm, d, 0), (uin, d, cb), (uin, d, cb + 1)], [],
                    [(d, BF16), (d, BF16), (2 * pw + 2 * d, BF16)], [], 256)


def _glu_fwd(name, gv, b_glu):
    w = gv.shape[1] // 2

    def fn(iv, pv):
        gvv, (bv,) = iv[0].astype(F32) + pv[0], pv
        return [gvv[:, :w] * _sigmoid(gvv[:, w:])], []

    return _rowwise(name, fn, [(gv, 2 * w, 0)], [b_glu], [(w, BF16)], [], 512)[0]


def _glu_bwd(name, dsg, gv, b_glu, deps=()):
    w = gv.shape[1] // 2

    def fn(iv, pv):
        dv = iv[0].astype(F32)
        gvv = iv[1].astype(F32) + pv[0]
        val, s = gvv[:, :w], _sigmoid(gvv[:, w:])
        dval = dv * s
        dgate = dv * val * s * (1.0 - s)
        return [[dval, dgate]], [jnp.concatenate([_colsum(dval), _colsum(dgate)], axis=1)]

    return _rowwise(name, fn, [(dsg, w, 0), (gv, 2 * w, 0)], [b_glu], [(2 * w, BF16)], [2 * w], 512, deps=deps)


def _adamw(name, w, g, m, v, tm=256, with_g=True):
    c = w.shape[1]

    def fn(iv, pv):
        wv, gv, mv, vv = iv
        mn = ADAM_B1 * mv + (1.0 - ADAM_B1) * gv
        vn = ADAM_B2 * vv + (1.0 - ADAM_B2) * (gv * gv)
        m_hat = mn / (1.0 - ADAM_B1 ** ADAM_STEP)
        v_hat = vn / (1.0 - ADAM_B2 ** ADAM_STEP)
        delta = -ADAM_LR * (m_hat / (jnp.sqrt(v_hat) + ADAM_EPS) + ADAM_WD * wv)
        return ([gv] if with_g else []) + [delta, mn, vn], []

    return _rowwise(name, fn, [(w, c, 0), (g, c, 0), (m, c, 0), (v, c, 0)], [], [(c, F32)] * (4 if with_g else 3), [],
                    _pick(w.shape[0], tm, 8))


def _unpack_plan(shape):
    n = math.prod(shape)
    if len(shape) == 2 and shape[0] == 1 and n % 128 == 0:
        return [((slice(None), slice(128 * r, 128 * (r + 1))), slice(r, r + 1), slice(None)) for r in range(n // 128)]
    if len(shape) == 2 and shape[0] == 1 and n < 128:
        return [((slice(None), slice(None)), slice(0, 1), slice(0, n))]
    if len(shape) == 1 and n % 128 == 0:
        return [((slice(128 * r, 128 * (r + 1)),), r, slice(None)) for r in range(n // 128)]
    if len(shape) == 3 and shape[0] == 1 and shape[2] == 64:
        return [((0, slice(2 * r + h, 2 * r + h + 1), slice(None)), slice(r, r + 1), slice(64 * h, 64 * (h + 1)))
                for r in range(n // 128) for h in range(2)]
    if len(shape) == 4 and shape[0] == 1 and shape[2:] == (128, 128):
        return [((0, k), slice(128 * k, 128 * (k + 1)), slice(None)) for k in range(shape[1])]
    return None


def _adamw_small(name, w, g8, m, v, shapes):
    r = w.shape[0]
    plans, rows0, off = [], [], 0
    for s in shapes:
        plans.append(_unpack_plan(s))
        rows0.append(off // 128)
        off += math.prod(s) + (-math.prod(s)) % 128
    direct = [i for i, p in enumerate(plans) if p is not None]

    def body(w_ref, g_ref, m_ref, v_ref, *out):
        packed = out[4 * len(direct):]
        gv = g_ref[0:r, :]
        for k in range(1, N_DEV):
            gv = gv + g_ref[k * r:(k + 1) * r, :]
        mn = ADAM_B1 * m_ref[...] + (1.0 - ADAM_B1) * gv
        vn = ADAM_B2 * v_ref[...] + (1.0 - ADAM_B2) * (gv * gv)
        m_hat = mn / (1.0 - ADAM_B1 ** ADAM_STEP)
        v_hat = vn / (1.0 - ADAM_B2 ** ADAM_STEP)
        packed[0][...] = gv
        packed[1][...] = -ADAM_LR * (m_hat / (jnp.sqrt(v_hat) + ADAM_EPS) + ADAM_WD * w_ref[...])
        packed[2][...] = mn
        packed[3][...] = vn
        for di, i in enumerate(direct):
            for kind in range(4):
                o_ref, src = out[4 * di + kind], packed[kind]
                for o_idx, row, lanes in plans[i]:
                    row = (rows0[i] + row) if isinstance(row, int) else slice(rows0[i] + row.start, rows0[i] + row.stop)
                    o_ref[o_idx] = src[row, lanes]

    out_shape = [jax.ShapeDtypeStruct(shapes[i], F32) for i in direct for _ in range(4)]
    out_shape += [jax.ShapeDtypeStruct((r, 128), F32)] * 4
    res = _pcall(body, name=name, out_shape=out_shape,
                 compiler_params=pltpu.CompilerParams(vmem_limit_bytes=VMEM_LIMIT))(w, g8, m, v)
    per_param = [None] * len(shapes)
    for di, i in enumerate(direct):
        per_param[i] = res[4 * di:4 * di + 4]
    return per_param, res[4 * len(direct):]


def _pool_fwd(name, uin, pool_w, pool_b, pool_scale, pw, tm=512):
    t = uin.shape[0]
    tm = min(tm, t)
    ng = len(POOL_WINDOWS)
    gw = pw // ng

    def body(u_ref, w_ref, b_ref, s_ref, p_ref, z_ref, ext):
        i = pl.program_id(0)

        @pl.when(i == 0)
        def _():
            ext[0:POOL_HALO, :] = jnp.zeros((POOL_HALO, pw), F32)

        u = u_ref[...].astype(F32)
        ext[POOL_HALO:POOL_HALO + tm, :] = u
        pos = i * tm + lax.broadcasted_iota(jnp.int32, (tm, 1), 0)
        for k, win in enumerate(POOL_WINDOWS):
            cols = slice(k * gw, (k + 1) * gw)
            acc = u[:, cols]
            for j in range(1, win):
                acc = acc + ext[POOL_HALO - j:POOL_HALO - j + tm, cols]
            cnt = jnp.minimum(pos + 1, win).astype(F32)
            z = acc / cnt - u[:, cols]
            zp = jnp.dot(z.astype(BF16), w_ref[k].astype(BF16), preferred_element_type=F32) + b_ref[:, cols]
            p_ref[:, cols] = (zp * s_ref[:, cols]).astype(BF16)
            z_ref[:, cols] = z.astype(BF16)
        ext[0:POOL_HALO, :] = u[tm - POOL_HALO:tm, :]

    return _pcall(
        body, name=name, grid=(t // tm,),
        in_specs=[pl.BlockSpec((tm, pw), lambda i: (i, 0)), pl.BlockSpec(pool_w.shape, lambda i: (0, 0, 0)),
                  pl.BlockSpec(pool_b.shape, lambda i: (0, 0)), pl.BlockSpec(pool_scale.shape, lambda i: (0, 0))],
        out_specs=[pl.BlockSpec((tm, pw), lambda i: (i, 0))] * 2,
        out_shape=[jax.ShapeDtypeStruct((t, pw), BF16)] * 2,
        scratch_shapes=[pltpu.VMEM((POOL_HALO + tm, pw), F32)],
        compiler_params=_params("arbitrary"))(uin, pool_w, pool_b, pool_scale)


def _pool_bwd(name, dp, z, pool_w, pool_b, pool_scale, duin, tm=512):
    t, pw = z.shape
    tm = min(tm, t)
    nb = t // tm
    ng = len(POOL_WINDOWS)
    gw = pw // ng

    def body(dp_ref, z_ref, w_ref, b_ref, s_ref, duin_ref, du_ref, dw_ref, db_ref, ds_ref, ext):
        i = pl.program_id(0)

        @pl.when(i == 0)
        def _():
            ext[tm:tm + POOL_HALO, :] = jnp.zeros((POOL_HALO, pw), F32)
            dw_ref[...] = jnp.zeros_like(dw_ref)
            db_ref[...] = jnp.zeros_like(db_ref)
            ds_ref[...] = jnp.zeros_like(ds_ref)

        pos = (nb - 1 - i) * tm + lax.broadcasted_iota(jnp.int32, (tm, 1), 0)
        for k, win in enumerate(POOL_WINDOWS):
            cols = slice(k * gw, (k + 1) * gw)
            zk = z_ref[:, cols]
            dpk = dp_ref[:, cols].astype(F32)
            wk = w_ref[k].astype(BF16)
            zp = jnp.dot(zk, wk, preferred_element_type=F32) + b_ref[:, cols]
            ds_ref[:, cols] += _colsum(dpk * zp)
            dzp = dpk * s_ref[:, cols]
            db_ref[:, cols] += _colsum(dzp)
            dzpb = dzp.astype(BF16)
            dz = lax.dot_general(dzpb, wk, (((1,), (1,)), ((), ())), preferred_element_type=F32)
            dw_ref[k] += lax.dot_general(zk, dzpb, (((0,), (0,)), ((), ())), preferred_element_type=F32)
            cnt = jnp.minimum(pos + 1, win).astype(F32)
            r = dz / cnt
            ext[0:tm, cols] = r
            acc = r - dz
            for j in range(1, win):
                acc = acc + ext[j:j + tm, cols]
            du_ref[:, cols] = acc.astype(BF16)
        ext[tm:tm + POOL_HALO, :] = ext[0:POOL_HALO, :]

    rev = lambda i: (nb - 1 - i, 0)
    return _pcall(
        body, name=name, grid=(nb,),
        in_specs=[pl.BlockSpec((tm, pw), rev), pl.BlockSpec((tm, pw), rev), pl.BlockSpec(pool_w.shape, lambda i: (0, 0, 0)),
                  pl.BlockSpec(pool_b.shape, lambda i: (0, 0)), pl.BlockSpec(pool_scale.shape, lambda i: (0, 0)), ANY_SPEC],
        out_specs=[pl.BlockSpec((tm, pw), rev), pl.BlockSpec(pool_w.shape, lambda i: (0, 0, 0)),
                   pl.BlockSpec((1, pw), lambda i: (0, 0)), pl.BlockSpec((1, pw), lambda i: (0, 0))],
        out_shape=[jax.ShapeDtypeStruct(duin.shape, BF16), jax.ShapeDtypeStruct(pool_w.shape, F32),
                   jax.ShapeDtypeStruct((1, pw), F32), jax.ShapeDtypeStruct((1, pw), F32)],
        scratch_shapes=[pltpu.VMEM((tm + POOL_HALO, pw), F32)], input_output_aliases={5: 0},
        compiler_params=_params("arbitrary"))(dp, z, pool_w, pool_b, pool_scale, duin)


def _ssm_disc(lrl, li, ldt):
    lr = -jnp.exp(lrl)
    dt = jnp.exp(ldt)
    mag = jnp.exp(lr * dt)
    ang = li * dt
    ab_re = mag * jnp.cos(ang)
    ab_im = mag * jnp.sin(ang)
    num_re = ab_re - 1.0
    num_im = ab_im
    den = lr * lr + li * li
    f_re = (num_re * lr + num_im * li) / den
    f_im = (num_im * lr - num_re * li) / den
    return lr, dt, mag, ang, ab_re, ab_im, num_re, num_im, den, f_re, f_im


def _ssm_prep(name, lrl, li, ldt, b_re, b_im):
    gn, h = b_re.shape

    def body(lrl_ref, li_ref, ldt_ref, br_ref, bi_ref, lrdt_ref, ang_ref, bbr_ref, bbi_ref):
        lr, dt, _, ang, _, _, _, _, _, f_re, f_im = _ssm_disc(lrl_ref[...], li_ref[...], ldt_ref[...])
        lrdt_ref[...] = lr * dt
        ang_ref[...] = ang
        br, bi = br_ref[...], bi_ref[...]
        bbr_ref[...] = f_re * br - f_im * bi
        bbi_ref[...] = f_re * bi + f_im * br

    col = jax.ShapeDtypeStruct((gn, 1), F32)
    mat = jax.ShapeDtypeStruct((gn, h), F32)
    return _pcall(body, name=name, out_shape=[col, col, mat, mat])(lrl, li, ldt, b_re, b_im)


def _ssm_param_bwd(name, lrl, li, ldt, b_re, b_im, g_abre, g_abim, g_bbre, g_bbim):
    gn, h = b_re.shape

    def body(lrl_ref, li_ref, ldt_ref, br_ref, bi_ref, gar_ref, gai_ref, gbr_ref, gbi_ref,
             dlrl_ref, dli_ref, dldt_ref, dbr_ref, dbi_ref):
        li_v = li_ref[...]
        lr, dt, mag, ang, ab_re, ab_im, num_re, num_im, den, f_re, f_im = _ssm_disc(lrl_ref[...], li_v, ldt_ref[...])
        br, bi = br_ref[...], bi_ref[...]
        gbr, gbi = gbr_ref[...], gbi_ref[...]
        g_fre = jnp.sum(gbr * br + gbi * bi, axis=1, keepdims=True)
        g_fim = jnp.sum(gbi * br - gbr * bi, axis=1, keepdims=True)
        dbr_ref[...] = gbr * f_re + gbi * f_im
        dbi_ref[...] = gbi * f_re - gbr * f_im
        g_num_re = (g_fre * lr - g_fim * li_v) / den
        g_num_im = (g_fre * li_v + g_fim * lr) / den
        g_den = -(g_fre * f_re + g_fim * f_im) / den
        g_lr = (g_fre * num_re + g_fim * num_im) / den + g_den * 2.0 * lr
        g_li = (g_fre * num_im - g_fim * num_re) / den + g_den * 2.0 * li_v
        g_are = gar_ref[...] + g_num_re
        g_aim = gai_ref[...] + g_num_im
        g_mag = g_are * jnp.cos(ang) + g_aim * jnp.sin(ang)
        g_ang = g_aim * ab_re - g_are * ab_im
        g_lrdt = g_mag * mag
        g_lr = g_lr + g_lrdt * dt
        g_dt = g_lrdt * lr + g_ang * li_v
        g_li = g_li + g_ang * dt
        dlrl_ref[...] = g_lr * lr
        dli_ref[...] = g_li
        dldt_ref[...] = g_dt * dt

    col = jax.ShapeDtypeStruct((gn, 1), F32)
    mat = jax.ShapeDtypeStruct((gn, h), F32)
    return _pcall(body, name=name, out_shape=[col, col, col, mat, mat])(lrl, li, ldt, b_re, b_im, g_abre, g_abim, g_bbre, g_bbim)


def _pow_rows(lrdt, ang, k):
    mag = jnp.exp(k * lrdt)
    return mag * jnp.cos(k * ang), mag * jnp.sin(k * ang)


def _ssm_chunk(t):
    return 256 if t >= 2048 else 128


def _to_segments(dst, srcs, jn):
    for q, src in enumerate(srcs):
        for j in range(jn):
            dst[8 * j:8 * j + 8, 128 * q:128 * (q + 1)] = src[pl.ds(j, 8, stride=jn), :]


def _from_segments(dst, src, q, jn, dtype):
    for s in range(8):
        dst[s * jn:(s + 1) * jn, 128 * q:128 * (q + 1)] = src[q, pl.ds(s, jn, stride=8), :].astype(dtype)


def _fill_rows8(dst_re, dst_im, v_re, v_im):
    for j in range(v_re.shape[0]):
        dst_re[8 * j:8 * j + 8, :] = jnp.broadcast_to(v_re[j:j + 1, :], (8, v_re.shape[1]))
        dst_im[8 * j:8 * j + 8, :] = jnp.broadcast_to(v_im[j:j + 1, :], (8, v_im.shape[1]))


def _cmul(ar, ai, br, bi):
    return ar * br - ai * bi, ar * bi + ai * br


def _cmul_conj(ar, ai, br, bi):
    return ar * br + ai * bi, ar * bi - ai * br


def _ssm_fwd(name, uin, lrdt, ang, bb_re, bb_im, cc_re, cc_im, d_skip, sw):
    t = uin.shape[0]
    gn = lrdt.shape[1]
    lc = _ssm_chunk(t)
    jn = lc // 8
    ub, sb = sw // SSM_BLOCKS, gn // SSM_BLOCKS
    nq = sw // 128
    assert ub == 128 and nq == SSM_BLOCKS

    def body(u_ref, lrdt_ref, ang_ref, bbr_ref, bbi_ref, ccr_ref, cci_ref, d_ref, y_ref, ge_ref, sre_ref, sim_ref,
             p_re, p_im, a_re, a_im, up, yp, cst_re, cst_im, car_re, car_im):
        i = pl.program_id(0)
        lrdt_v, ang_v = lrdt_ref[...], ang_ref[...]

        @pl.when(i == 0)
        def _():
            k = (lax.broadcasted_iota(jnp.int32, (jn, 1), 0) + 1).astype(F32)
            _fill_rows8(p_re, p_im, *_pow_rows(lrdt_v, ang_v, k))
            car_re[...] = jnp.zeros_like(car_re)
            car_im[...] = jnp.zeros_like(car_im)

        for q in range(nq):
            yp[q] = u_ref[:, 128 * q:128 * (q + 1)].astype(F32)
        _to_segments(up, [yp.at[q] for q in range(nq)], jn)
        u = up[...]
        ubf = u.astype(BF16)
        for q in range(SSM_BLOCKS):
            uq = ubf[:, q * ub:(q + 1) * ub]
            a_re[:, q * sb:(q + 1) * sb] = jnp.dot(uq, bbr_ref[q], preferred_element_type=F32)
            a_im[:, q * sb:(q + 1) * sb] = jnp.dot(uq, bbi_ref[q], preferred_element_type=F32)
        a1r, a1i = _pow_rows(lrdt_v, ang_v, 1.0)
        ajr, aji = _pow_rows(lrdt_v, ang_v, float(jn))
        for q in range(0, SSM_BLOCKS, 2):
            cols = slice(q * sb, (q + 2) * sb)
            ar8 = jnp.broadcast_to(a1r[:, cols], (8, 2 * sb))
            ai8 = jnp.broadcast_to(a1i[:, cols], (8, 2 * sb))

            def step(j, carry, cols=cols, ar8=ar8, ai8=ai8):
                sr, si = carry
                rows = pl.ds(pl.multiple_of(j * 8, 8), 8)
                mr, mi = _cmul(ar8, ai8, sr, si)
                nr, ni = mr + a_re[rows, cols], mi + a_im[rows, cols]
                a_re[rows, cols] = nr
                a_im[rows, cols] = ni
                return nr, ni

            lax.fori_loop(1, jn, step, (a_re[0:8, cols], a_im[0:8, cols]), unroll=4)
        er, ei = a_re[lc - 8:lc, :], a_im[lc - 8:lc, :]
        gr, gi = car_re[...], car_im[...]
        for s in range(8):
            cst_re[s:s + 1, :] = gr
            cst_im[s:s + 1, :] = gi
            mr, mi = _cmul(ajr, aji, gr, gi)
            gr, gi = mr + er[s:s + 1, :], mi + ei[s:s + 1, :]
        car_re[...] = gr
        car_im[...] = gi
        for q in range(SSM_BLOCKS):
            cols = slice(q * sb, (q + 1) * sb)
            cr = jnp.tile(cst_re[:, cols], (jn, 1))
            ci = jnp.tile(cst_im[:, cols], (jn, 1))
            mr, mi = _cmul(p_re[:, cols], p_im[:, cols], cr, ci)
            srb, sib = (a_re[:, cols] + mr).astype(BF16), (a_im[:, cols] + mi).astype(BF16)
            sre_ref[:, cols] = srb
            sim_ref[:, cols] = sib
            ycols = slice(q * ub, (q + 1) * ub)
            y = (jnp.dot(srb, ccr_ref[q], preferred_element_type=F32) - jnp.dot(sib, cci_ref[q], preferred_element_type=F32)
                 + d_ref[:, ycols] * u[:, ycols])
            yp[q] = y
            _from_segments(y_ref, yp, q, jn, F32)
            yt = y_ref[:, ycols]
            ge_ref[:, ycols] = (0.5 * yt * (1.0 + lax.erf(yt * (1.0 / math.sqrt(2.0))))).astype(BF16)

    row = lambda i: (0, 0)
    blk3 = lambda i: (0, 0, 0)
    return _pcall(
        body, name=name, grid=(t // lc,),
        in_specs=[pl.BlockSpec((lc, sw), lambda i: (i, 1)), pl.BlockSpec((1, gn), row), pl.BlockSpec((1, gn), row),
                  pl.BlockSpec(bb_re.shape, blk3), pl.BlockSpec(bb_im.shape, blk3),
                  pl.BlockSpec(cc_re.shape, blk3), pl.BlockSpec(cc_im.shape, blk3), pl.BlockSpec((1, sw), row)],
        out_specs=[pl.BlockSpec((lc, sw), lambda i: (i, 0)), pl.BlockSpec((lc, sw), lambda i: (i, 0)),
                   pl.BlockSpec((lc, gn), lambda i: (i, 0)), pl.BlockSpec((lc, gn), lambda i: (i, 0))],
        out_shape=[jax.ShapeDtypeStruct((t, sw), F32), jax.ShapeDtypeStruct((t, sw), BF16),
                   jax.ShapeDtypeStruct((t, gn), BF16), jax.ShapeDtypeStruct((t, gn), BF16)],
        scratch_shapes=[pltpu.VMEM((lc, gn), F32), pltpu.VMEM((lc, gn), F32), pltpu.VMEM((lc, gn), F32), pltpu.VMEM((lc, gn), F32),
                        pltpu.VMEM((lc, sw), F32), pltpu.VMEM((nq, lc, 128), F32),
                        pltpu.VMEM((8, gn), F32), pltpu.VMEM((8, gn), F32), pltpu.VMEM((1, gn), F32), pltpu.VMEM((1, gn), F32)],
        compiler_params=_params("arbitrary"))(uin, lrdt, ang, bb_re, bb_im, cc_re, cc_im, d_skip)


def _ssm_bwd(name, dge, y, uin, s_re, s_im, lrdt, ang, bbt_re, bbt_im, cct_re, cct_im, d_skip, sw, duin):
    t = uin.shape[0]
    gn = lrdt.shape[1]
    lc = _ssm_chunk(t)
    nb = t // lc
    jn = lc // 8
    ub, sb = sw // SSM_BLOCKS, gn // SSM_BLOCKS
    nq = sw // 128
    tail = 16

    def body(dge_ref, y_ref, u_ref, sre_ref, sim_ref, tre_ref, tim_ref, lrdt_ref, ang_ref, btr_ref, bti_ref, ctr_ref, cti_ref,
             d_ref, duin_ref, du_ref, dar_ref, dai_ref, dbr_ref, dbi_ref, dcr_ref, dci_ref, dd_ref,
             q_re, q_im, a_re, a_im, dyp, up, dys, us, dup, cst_re, cst_im, sp_re, sp_im, car_re, car_im):
        i = pl.program_id(0)
        lrdt_v, ang_v = lrdt_ref[...], ang_ref[...]

        @pl.when(i == 0)
        def _():
            k = (jn - lax.broadcasted_iota(jnp.int32, (jn, 1), 0)).astype(F32)
            _fill_rows8(q_re, q_im, *_pow_rows(lrdt_v, ang_v, k))
            car_re[...] = jnp.zeros_like(car_re)
            car_im[...] = jnp.zeros_like(car_im)
            for r in (dar_ref, dai_ref, dbr_ref, dbi_ref, dcr_ref, dci_ref, dd_ref):
                r[...] = jnp.zeros_like(r)

        yv = y_ref[...]
        ut = u_ref[...].astype(F32)
        cdf =0.5 * (1.0 + lax.erf(yv * (1.0 / math.sqrt(2.0))))
        pdf = jnp.exp(-0.5 * yv * yv) * (1.0 / math.sqrt(2.0 * math.pi))
        dyt = dge_ref[...].astype(F32) * (cdf + yv * pdf)
        dd_ref[...] += _colsum(dyt * ut)
        for q in range(nq):
            dys[q] = dyt[:, 128 * q:128 * (q + 1)]
            us[q] = ut[:, 128 * q:128 * (q + 1)]
        _to_segments(dyp, [dys.at[q] for q in range(nq)], jn)
        _to_segments(up, [us.at[q] for q in range(nq)], jn)
        dy = dyp[...]
        u = up[...]
        dyb = dy.astype(BF16)
        ubf = u.astype(BF16)
        for q in range(SSM_BLOCKS):
            dq = dyb[:, q * ub:(q + 1) * ub]
            a_re[:, q * sb:(q + 1) * sb] = jnp.dot(dq, ctr_ref[q], preferred_element_type=F32)
            a_im[:, q * sb:(q + 1) * sb] = -jnp.dot(dq, cti_ref[q], preferred_element_type=F32)
        a1r, a1i = _pow_rows(lrdt_v, ang_v, 1.0)
        ajr, aji = _pow_rows(lrdt_v, ang_v, float(jn))
        for q in range(0, SSM_BLOCKS, 2):
            cols = slice(q * sb, (q + 2) * sb)
            ar8 = jnp.broadcast_to(a1r[:, cols], (8, 2 * sb))
            ai8 = jnp.broadcast_to(a1i[:, cols], (8, 2 * sb))

            def step(jj, carry, cols=cols, ar8=ar8, ai8=ai8):
                sr, si = carry
                rows = pl.ds(pl.multiple_of((jn - 2 - jj) * 8, 8), 8)
                mr, mi = _cmul_conj(ar8, ai8, sr, si)
                nr, ni = mr + a_re[rows, cols], mi + a_im[rows, cols]
                a_re[rows, cols] = nr
                a_im[rows, cols] = ni
                return nr, ni

            lax.fori_loop(0, jn - 1, step, (a_re[lc - 8:lc, cols], a_im[lc - 8:lc, cols]), unroll=4)
        er, ei = a_re[0:8, :], a_im[0:8, :]
        hr, hi = car_re[...], car_im[...]
        for s in range(7, -1, -1):
            cst_re[s:s + 1, :] = hr
            cst_im[s:s + 1, :] = hi
            mr, mi = _cmul_conj(ajr, aji, hr, hi)
            hr, hi = mr + er[s:s + 1, :], mi + ei[s:s + 1, :]
        car_re[...] = hr
        car_im[...] = hi
        first = (i == nb - 1).astype(F32)
        sp_re[0:tail, :] = tre_ref[...].astype(F32) * (1.0 - first)
        sp_im[0:tail, :] = tim_ref[...].astype(F32) * (1.0 - first)
        sp_re[tail:tail + 8, :] = sre_ref[lc - tail:lc, :].astype(F32)[tail - 8:tail]
        sp_im[tail:tail + 8, :] = sim_ref[lc - tail:lc, :].astype(F32)[tail - 8:tail]
        tn_dims = (((0,), (0,)), ((), ()))
        for q in range(SSM_BLOCKS):
            cols = slice(q * sb, (q + 1) * sb)
            ycols = slice(q * ub, (q + 1) * ub)
            cr = jnp.tile(cst_re[:, cols], (jn, 1))
            ci = jnp.tile(cst_im[:, cols], (jn, 1))
            mr, mi = _cmul_conj(q_re[:, cols], q_im[:, cols], cr, ci)
            lam_r, lam_i = a_re[:, cols] + mr, a_im[:, cols] + mi
            s_r, s_i = sre_ref[:, cols], sim_ref[:, cols]
            p0r, p0i = sp_re[tail - 1:tail + 7, cols], sp_im[tail - 1:tail + 7, cols]
            l0r, l0i, l1r, l1i = lam_r[0:8], lam_i[0:8], lam_r[8:lc], lam_i[8:lc]
            pvr, pvi = s_r.astype(F32)[0:lc - 8], s_i.astype(F32)[0:lc - 8]
            dar_ref[:, cols] += _colsum(l1r * pvr + l1i * pvi) + _colsum(l0r * p0r + l0i * p0i)
            dai_ref[:, cols] += _colsum(l1i * pvr - l1r * pvi) + _colsum(l0i * p0r - l0r * p0i)
            lrb, lib = lam_r.astype(BF16), lam_i.astype(BF16)
            dup[q] = (jnp.dot(lrb, btr_ref[q], preferred_element_type=F32)
                      + jnp.dot(lib, bti_ref[q], preferred_element_type=F32) + d_ref[:, ycols] * dy[:, ycols])
            _from_segments(du_ref, dup, q, jn, BF16)
            uq = ubf[:, ycols]
            dbr_ref[q] += lax.dot_general(uq, lrb, tn_dims, preferred_element_type=F32)
            dbi_ref[q] += lax.dot_general(uq, lib, tn_dims, preferred_element_type=F32)
            dq = dyb[:, ycols]
            dcr_ref[q] += lax.dot_general(s_r.astype(BF16), dq, tn_dims, preferred_element_type=F32)
            dci_ref[q] -= lax.dot_general(s_i.astype(BF16), dq, tn_dims, preferred_element_type=F32)

    rev = lambda i: (nb - 1 - i, 0)
    tailmap = lambda i: (jnp.maximum((nb - 1 - i) * (lc // tail) - 1, 0), 0)
    row = lambda i: (0, 0)
    blk3 = lambda i: (0, 0, 0)
    return _pcall(
        body, name=name, grid=(nb,),
        in_specs=[pl.BlockSpec((lc, sw), rev), pl.BlockSpec((lc, sw), rev), pl.BlockSpec((lc, sw), lambda i: (nb - 1 - i, 1)),
                  pl.BlockSpec((lc, gn), rev), pl.BlockSpec((lc, gn), rev),
                  pl.BlockSpec((tail, gn), tailmap), pl.BlockSpec((tail, gn), tailmap),
                  pl.BlockSpec((1, gn), row), pl.BlockSpec((1, gn), row),
                  pl.BlockSpec(bbt_re.shape, blk3), pl.BlockSpec(bbt_im.shape, blk3),
                  pl.BlockSpec(cct_re.shape, blk3), pl.BlockSpec(cct_im.shape, blk3), pl.BlockSpec((1, sw), row), ANY_SPEC],
        out_specs=[pl.BlockSpec((lc, sw), lambda i: (nb - 1 - i, 1)), pl.BlockSpec((1, gn), row), pl.BlockSpec((1, gn), row),
                   pl.BlockSpec((SSM_BLOCKS, ub, sb), blk3), pl.BlockSpec((SSM_BLOCKS, ub, sb), blk3),
                   pl.BlockSpec((SSM_BLOCKS, sb, ub), blk3), pl.BlockSpec((SSM_BLOCKS, sb, ub), blk3),
                   pl.BlockSpec((1, sw), row)],
        out_shape=[jax.ShapeDtypeStruct(duin.shape, BF16), jax.ShapeDtypeStruct((1, gn), F32), jax.ShapeDtypeStruct((1, gn), F32),
                   jax.ShapeDtypeStruct((SSM_BLOCKS, ub, sb), F32), jax.ShapeDtypeStruct((SSM_BLOCKS, ub, sb), F32),
                   jax.ShapeDtypeStruct((SSM_BLOCKS, sb, ub), F32), jax.ShapeDtypeStruct((SSM_BLOCKS, sb, ub), F32),
                   jax.ShapeDtypeStruct((1, sw), F32)],
        scratch_shapes=[pltpu.VMEM((lc, gn), F32), pltpu.VMEM((lc, gn), F32), pltpu.VMEM((lc, gn), F32), pltpu.VMEM((lc, gn), F32),
                        pltpu.VMEM((lc, sw), F32), pltpu.VMEM((lc, sw), F32),
                        pltpu.VMEM((nq, lc, 128), F32), pltpu.VMEM((nq, lc, 128), F32), pltpu.VMEM((nq, lc, 128), F32),
                        pltpu.VMEM((8, gn), F32), pltpu.VMEM((8, gn), F32),
                        pltpu.VMEM((tail + 8, gn), F32), pltpu.VMEM((tail + 8, gn), F32),
                        pltpu.VMEM((1, gn), F32), pltpu.VMEM((1, gn), F32)],
        input_output_aliases={14: 0},
        compiler_params=_params("arbitrary"))(dge, y, uin, s_re, s_im, s_re, s_im, lrdt, ang,
                                               bbt_re, bbt_im, cct_re, cct_im, d_skip, duin)


def _ssm_fwd_doubling(name, uin, lrdt, ang, bb_re, bb_im, cc_re, cc_im, d_skip, sw):
    t = uin.shape[0]
    gn = lrdt.shape[1]
    lc = _ssm_chunk(t)
    nsteps = int(math.log2(lc))
    pad = lc // 2
    ub, sb = sw // SSM_BLOCKS, gn // SSM_BLOCKS

    def body(u_ref, lrdt_ref, ang_ref, bbr_ref, bbi_ref, ccr_ref, cci_ref, d_ref, y_ref, ge_ref, sre_ref, sim_ref,
             p_re, p_im, a_re, a_im, b_re, b_im, car_re, car_im):
        i = pl.program_id(0)
        lrdt_v, ang_v = lrdt_ref[...], ang_ref[...]

        @pl.when(i == 0)
        def _():
            k = (lax.broadcasted_iota(jnp.int32, (lc, 1), 0) + 1).astype(F32)
            pr, pi = _pow_rows(lrdt_v, ang_v, k)
            p_re[...] = pr
            p_im[...] = pi
            zeros = jnp.zeros((pad, gn), F32)
            a_re[0:pad, :] = zeros
            a_im[0:pad, :] = zeros
            b_re[0:pad, :] = zeros
            b_im[0:pad, :] = zeros
            car_re[...] = jnp.zeros_like(car_re)
            car_im[...] = jnp.zeros_like(car_im)

        u = u_ref[...]
        ubf = u.astype(BF16)
        for q in range(SSM_BLOCKS):
            uq = ubf[:, q * ub:(q + 1) * ub]
            a_re[pad:pad + lc, q * sb:(q + 1) * sb] = jnp.dot(uq, bbr_ref[q], preferred_element_type=F32)
            a_im[pad:pad + lc, q * sb:(q + 1) * sb] = jnp.dot(uq, bbi_ref[q], preferred_element_type=F32)
        bufs = [(a_re, a_im), (b_re, b_im)]
        for q in range(SSM_BLOCKS):
            cols = slice(q * sb, (q + 1) * sb)
            for j in range(nsteps):
                dd = 1 << j
                (s_re, s_im), (d_re, d_im) = bufs[j % 2], bufs[(j + 1) % 2]
                pr, pi = _pow_rows(lrdt_v[:, cols], ang_v[:, cols], float(dd))
                cr, ci = s_re[pad:pad + lc, cols], s_im[pad:pad + lc, cols]
                hr, hi = s_re[pad - dd:pad - dd + lc, cols], s_im[pad - dd:pad - dd + lc, cols]
                d_re[pad:pad + lc, cols] = cr + (pr * hr - pi * hi)
                d_im[pad:pad + lc, cols] = ci + (pr * hi + pi * hr)
            f_re, f_im = bufs[nsteps % 2]
            cr, ci = car_re[:, cols], car_im[:, cols]
            pr, pi = p_re[:, cols], p_im[:, cols]
            sr = f_re[pad:pad + lc, cols] + (pr * cr - pi * ci)
            si = f_im[pad:pad + lc, cols] + (pr * ci + pi * cr)
            sre_ref[:, cols] = sr
            sim_ref[:, cols] = si
            car_re[:, cols] = sr[lc - 1:lc, :]
            car_im[:, cols] = si[lc - 1:lc, :]
            ycols = slice(q * ub, (q + 1) * ub)
            y = (jnp.dot(sr.astype(BF16), ccr_ref[q], preferred_element_type=F32)
                 - jnp.dot(si.astype(BF16), cci_ref[q], preferred_element_type=F32)
                 + d_ref[:, ycols] * u[:, ycols])
            y_ref[:, ycols] = y
            ge_ref[:, ycols] = (0.5 * y * (1.0 + lax.erf(y * (1.0 / math.sqrt(2.0))))).astype(BF16)

    cblk = sw // sw
    row = lambda i: (0, 0)
    blk3 = lambda i: (0, 0, 0)
    return _pcall(
        body, name=name, grid=(t // lc,),
        in_specs=[pl.BlockSpec((lc, sw), lambda i: (i, cblk)), pl.BlockSpec((1, gn), row), pl.BlockSpec((1, gn), row),
                  pl.BlockSpec(bb_re.shape, blk3), pl.BlockSpec(bb_im.shape, blk3),
                  pl.BlockSpec(cc_re.shape, blk3), pl.BlockSpec(cc_im.shape, blk3), pl.BlockSpec((1, sw), row)],
        out_specs=[pl.BlockSpec((lc, sw), lambda i: (i, 0)), pl.BlockSpec((lc, sw), lambda i: (i, 0)),
                   pl.BlockSpec((lc, gn), lambda i: (i, 0)), pl.BlockSpec((lc, gn), lambda i: (i, 0))],
        out_shape=[jax.ShapeDtypeStruct((t, sw), F32), jax.ShapeDtypeStruct((t, sw), BF16),
                   jax.ShapeDtypeStruct((t, gn), F32), jax.ShapeDtypeStruct((t, gn), F32)],
        scratch_shapes=[pltpu.VMEM((lc, gn), F32), pltpu.VMEM((lc, gn), F32),
                        pltpu.VMEM((pad + lc, gn), F32), pltpu.VMEM((pad + lc, gn), F32),
                        pltpu.VMEM((pad + lc, gn), F32), pltpu.VMEM((pad + lc, gn), F32),
                        pltpu.VMEM((1, gn), F32), pltpu.VMEM((1, gn), F32)],
        compiler_params=_params("arbitrary"))(uin, lrdt, ang, bb_re, bb_im, cc_re, cc_im, d_skip)


def _ssm_bwd_doubling(name, dge, y, uin, s_re, s_im, lrdt, ang, bbt_re, bbt_im, cct_re, cct_im, d_skip, sw):
    t = uin.shape[0]
    gn = lrdt.shape[1]
    lc = _ssm_chunk(t)
    nb = t // lc
    nsteps = int(math.log2(lc))
    pad = lc // 2
    ub, sb = sw // SSM_BLOCKS, gn // SSM_BLOCKS
    tail = 8

    def body(dge_ref, y_ref, u_ref, sre_ref, sim_ref, tre_ref, tim_ref, lrdt_ref, ang_ref, btr_ref, bti_ref, ctr_ref, cti_ref,
             d_ref, du_ref, dar_ref, dai_ref, dbr_ref, dbi_ref, dcr_ref, dci_ref, dd_ref,
             q_re, q_im, a_re, a_im, b_re, b_im, sp_re, sp_im, car_re, car_im):
        i = pl.program_id(0)
        lrdt_v, ang_v = lrdt_ref[...], ang_ref[...]

        @pl.when(i == 0)
        def _():
            k = (lc - lax.broadcasted_iota(jnp.int32, (lc, 1), 0)).astype(F32)
            qr, qi = _pow_rows(lrdt_v, ang_v, k)
            q_re[...] = qr
            q_im[...] = qi
            zeros = jnp.zeros((pad, gn), F32)
            a_re[lc:lc + pad, :] = zeros
            a_im[lc:lc + pad, :] = zeros
            b_re[lc:lc + pad, :] = zeros
            b_im[lc:lc + pad, :] = zeros
            car_re[...] = jnp.zeros_like(car_re)
            car_im[...] = jnp.zeros_like(car_im)
            for r in (dar_ref, dai_ref, dbr_ref, dbi_ref, dcr_ref, dci_ref, dd_ref):
                r[...] = jnp.zeros_like(r)

        first = (i == nb - 1).astype(F32)
        sp_re[0:tail, :] = tre_ref[...] * (1.0 - first)
        sp_im[0:tail, :] = tim_ref[...] * (1.0 - first)
        sp_re[tail:tail + lc, :] = sre_ref[...]
        sp_im[tail:tail + lc, :] = sim_ref[...]

        yv = y_ref[...]
        u = u_ref[...]
        cdf = 0.5 * (1.0 + lax.erf(yv * (1.0 / math.sqrt(2.0))))
        pdf = jnp.exp(-0.5 * yv * yv) * (1.0 / math.sqrt(2.0 * math.pi))
        dy = dge_ref[...].astype(F32) * (cdf + yv * pdf)
        dd_ref[...] += _colsum(dy * u)
        dyb = dy.astype(BF16)
        ubf = u.astype(BF16)
        for q in range(SSM_BLOCKS):
            dq = dyb[:, q * ub:(q + 1) * ub]
            a_re[0:lc, q * sb:(q + 1) * sb] = jnp.dot(dq, ctr_ref[q], preferred_element_type=F32)
            a_im[0:lc, q * sb:(q + 1) * sb] = -jnp.dot(dq, cti_ref[q], preferred_element_type=F32)
        bufs = [(a_re, a_im), (b_re, b_im)]
        tn_dims = (((0,), (0,)), ((), ()))
        for q in range(SSM_BLOCKS):
            cols = slice(q * sb, (q + 1) * sb)
            ycols = slice(q * ub, (q + 1) * ub)
            for j in range(nsteps):
                ds = 1 << j
                (s_r, s_i), (d_r, d_i) = bufs[j % 2], bufs[(j + 1) % 2]
                pr, pi = _pow_rows(lrdt_v[:, cols], ang_v[:, cols], float(ds))
                cr, ci = s_r[0:lc, cols], s_i[0:lc, cols]
                hr, hi = s_r[ds:ds + lc, cols], s_i[ds:ds + lc, cols]
                d_r[0:lc, cols] = cr + (pr * hr + pi * hi)
                d_i[0:lc, cols] = ci + (pr * hi - pi * hr)
            f_r, f_i = bufs[nsteps % 2]
            cr, ci = car_re[:, cols], car_im[:, cols]
            qr, qi = q_re[:, cols], q_im[:, cols]
            lam_r = f_r[0:lc, cols] + (qr * cr + qi * ci)
            lam_i = f_i[0:lc, cols] + (qr * ci - qi * cr)
            car_re[:, cols] = lam_r[0:1, :]
            car_im[:, cols] = lam_i[0:1, :]
            pvr, pvi = sp_re[tail - 1:tail - 1 + lc, cols], sp_im[tail - 1:tail - 1 + lc, cols]
            dar_ref[:, cols] += _colsum(lam_r * pvr + lam_i * pvi)
            dai_ref[:, cols] += _colsum(lam_i * pvr - lam_r * pvi)
            lrb, lib = lam_r.astype(BF16), lam_i.astype(BF16)
            du_ref[:, ycols] = (jnp.dot(lrb, btr_ref[q], preferred_element_type=F32)
                                + jnp.dot(lib, bti_ref[q], preferred_element_type=F32)
                                + d_ref[:, ycols] * dy[:, ycols]).astype(BF16)
            uq = ubf[:, ycols]
            dbr_ref[q] += lax.dot_general(uq, lrb, tn_dims, preferred_element_type=F32)
            dbi_ref[q] += lax.dot_general(uq, lib, tn_dims, preferred_element_type=F32)
            dq = dyb[:, ycols]
            dcr_ref[q] += lax.dot_general(sre_ref[:, cols].astype(BF16), dq, tn_dims, preferred_element_type=F32)
            dci_ref[q] -= lax.dot_general(sim_ref[:, cols].astype(BF16), dq, tn_dims, preferred_element_type=F32)

    cblk = 1
    rev = lambda i: (nb - 1 - i, 0)
    revu = lambda i: (nb - 1 - i, cblk)
    tailmap = lambda i: (jnp.maximum((nb - 1 - i) * (lc // tail) - 1, 0), 0)
    row = lambda i: (0, 0)
    blk3 = lambda i: (0, 0, 0)
    return _pcall(
        body, name=name, grid=(nb,),
        in_specs=[pl.BlockSpec((lc, sw), rev), pl.BlockSpec((lc, sw), rev), pl.BlockSpec((lc, sw), revu),
                  pl.BlockSpec((lc, gn), rev), pl.BlockSpec((lc, gn), rev),
                  pl.BlockSpec((tail, gn), tailmap), pl.BlockSpec((tail, gn), tailmap),
                  pl.BlockSpec((1, gn), row), pl.BlockSpec((1, gn), row),
                  pl.BlockSpec(bbt_re.shape, blk3), pl.BlockSpec(bbt_im.shape, blk3),
                  pl.BlockSpec(cct_re.shape, blk3), pl.BlockSpec(cct_im.shape, blk3), pl.BlockSpec((1, sw), row)],
        out_specs=[pl.BlockSpec((lc, sw), rev), pl.BlockSpec((1, gn), row), pl.BlockSpec((1, gn), row),
                   pl.BlockSpec((SSM_BLOCKS, ub, sb), blk3), pl.BlockSpec((SSM_BLOCKS, ub, sb), blk3),
                   pl.BlockSpec((SSM_BLOCKS, sb, ub), blk3), pl.BlockSpec((SSM_BLOCKS, sb, ub), blk3),
                   pl.BlockSpec((1, sw), row)],
        out_shape=[jax.ShapeDtypeStruct((t, sw), BF16), jax.ShapeDtypeStruct((1, gn), F32), jax.ShapeDtypeStruct((1, gn), F32),
                   jax.ShapeDtypeStruct((SSM_BLOCKS, ub, sb), F32), jax.ShapeDtypeStruct((SSM_BLOCKS, ub, sb), F32),
                   jax.ShapeDtypeStruct((SSM_BLOCKS, sb, ub), F32), jax.ShapeDtypeStruct((SSM_BLOCKS, sb, ub), F32),
                   jax.ShapeDtypeStruct((1, sw), F32)],
        scratch_shapes=[pltpu.VMEM((lc, gn), F32), pltpu.VMEM((lc, gn), F32),
                        pltpu.VMEM((lc + pad, gn), F32), pltpu.VMEM((lc + pad, gn), F32),
                        pltpu.VMEM((lc + pad, gn), F32), pltpu.VMEM((lc + pad, gn), F32),
                        pltpu.VMEM((tail + lc, gn), F32), pltpu.VMEM((tail + lc, gn), F32),
                        pltpu.VMEM((1, gn), F32), pltpu.VMEM((1, gn), F32)],
        compiler_params=_params("arbitrary"))(dge, y, uin, s_re, s_im, s_re, s_im, lrdt, ang, bbt_re, bbt_im, cct_re, cct_im, d_skip)


def _blockdiag_b(bb, sw):
    gpb = (sw // SSM_GROUP) // SSM_BLOCKS
    b4 = bb.reshape(SSM_BLOCKS, gpb, SSM_STATE, SSM_GROUP)
    eye = jnp.eye(gpb, dtype=bb.dtype)
    out = jnp.einsum('qgnh,gk->qghkn', b4, eye)
    return out.reshape(SSM_BLOCKS, gpb * SSM_GROUP, gpb * SSM_STATE)


def _blockdiag_c(cc, sw):
    gpb = (sw // SSM_GROUP) // SSM_BLOCKS
    c4 = cc.reshape(SSM_BLOCKS, gpb, SSM_GROUP, SSM_STATE)
    eye = jnp.eye(gpb, dtype=cc.dtype)
    out = jnp.einsum('qghn,gk->qgnkh', c4, eye)
    return out.reshape(SSM_BLOCKS, gpb * SSM_STATE, gpb * SSM_GROUP)


def _diag_of_b(dbb, sw):
    gpb = (sw // SSM_GROUP) // SSM_BLOCKS
    d5 = dbb.reshape(SSM_BLOCKS, gpb, SSM_GROUP, gpb, SSM_STATE)
    return jnp.einsum('qghgn->qgnh', d5).reshape(SSM_BLOCKS * gpb * SSM_STATE, SSM_GROUP)


def _diag_of_c(dcc, sw):
    gpb = (sw // SSM_GROUP) // SSM_BLOCKS
    d5 = dcc.reshape(SSM_BLOCKS, gpb, SSM_STATE, gpb, SSM_GROUP)
    return jnp.einsum('qgngh->qghn', d5).reshape(SSM_BLOCKS * gpb, SSM_GROUP, SSM_STATE)


def _ada_fwd(name, c_all, w_sh, b_sh):
    nb, d = c_all.shape
    ncol = w_sh.shape[1]
    tn = _pick(ncol, 768)

    def body(c_ref, w_ref, b_ref, o_ref):
        cv = c_ref[...]
        sil = cv * _sigmoid(cv)
        o_ref[...] = jnp.dot(sil, w_ref[...], preferred_element_type=F32, precision=lax.Precision.HIGHEST) + b_ref[...]

    return _pcall(body, name=name, grid=(ncol // tn,),
                  in_specs=[pl.BlockSpec((nb, d), lambda j: (0, 0)), pl.BlockSpec((d, tn), lambda j: (0, j)),
                            pl.BlockSpec((1, tn), lambda j: (0, j))],
                  out_specs=pl.BlockSpec((nb, tn), lambda j: (0, j)),
                  out_shape=jax.ShapeDtypeStruct((nb, ncol), F32), compiler_params=_params("parallel"))(c_all, w_sh, b_sh)


def _ada_bwd(name, c_all, dmod_sh):
    nb, d = c_all.shape
    ncol = dmod_sh.shape[1]
    tn = _pick(ncol, 768)

    def body(c_ref, g_ref, o_ref):
        cv = c_ref[...]
        sil = cv * _sigmoid(cv)
        o_ref[...] = lax.dot_general(sil, g_ref[...], (((0,), (0,)), ((), ())), preferred_element_type=F32,
                                     precision=lax.Precision.HIGHEST)

    return _pcall(body, name=name, grid=(ncol // tn,),
                  in_specs=[pl.BlockSpec((nb, d), lambda j: (0, 0)), pl.BlockSpec((nb, tn), lambda j: (0, j))],
                  out_specs=pl.BlockSpec((d, tn), lambda j: (0, j)),
                  out_shape=jax.ShapeDtypeStruct((d, ncol), F32), compiler_params=_params("parallel"))(c_all, dmod_sh)


def _place():
    return lax.axis_index("x"), lax.axis_index("y"), lax.axis_index("c")


def _allgather_small(name, blk, deps=()):
    m_per, n = blk.shape

    def body(x_ref, *rest):
        out_ref, send_sems, recv_sems, local_sem = rest[len(deps):]
        x, y, c = _place()
        me, sibling = (x, y, c), (x, y, 1 - c)
        chips = [(1 - x, y), (x, 1 - y), (1 - x, 1 - y)]

        def rows(px, py, pc):
            return out_ref.at[pl.ds((4 * px + 2 * py + pc) * m_per, m_per), :]

        def copy(k, block, to, src=None):
            return pltpu.make_async_remote_copy(
                src_ref=rows(*block) if src is None else src, dst_ref=rows(*block),
                send_sem=send_sems.at[k], recv_sem=recv_sems.at[k], device_id=to, device_id_type=MESH)

        mine = pltpu.make_async_copy(x_ref, rows(*me), local_sem)
        mine.start()
        first = [copy(0, me, sibling, src=x_ref)]
        first += [copy(1 + j, me, (*chip, c), src=x_ref) for j, chip in enumerate(chips)]
        for cp in first:
            cp.start()
        passed = [copy(4 + j, (*chip, c), sibling) for j, chip in enumerate(chips)]
        for j, chip in enumerate(chips):
            copy(1 + j, (*chip, c), me).wait_recv()
            passed[j].start()
        copy(0, sibling, me).wait_recv()
        for j, chip in enumerate(chips):
            copy(4 + j, (*chip, 1 - c), me).wait_recv()
        for cp in first + passed:
            cp.wait_send()
        mine.wait()

    return _pcall(body, name=name, out_shape=jax.ShapeDtypeStruct((N_DEV * m_per, n), blk.dtype),
                  in_specs=[pl.BlockSpec(memory_space=pltpu.VMEM)] + [ANY_SPEC] * len(deps),
                  out_specs=pl.BlockSpec(memory_space=pltpu.VMEM),
                  scratch_shapes=[pltpu.SemaphoreType.DMA((7,)), pltpu.SemaphoreType.DMA((7,)), pltpu.SemaphoreType.DMA],
                  compiler_params=pltpu.CompilerParams(vmem_limit_bytes=VMEM_LIMIT))(blk, *deps)


def _other_chips(x, y):
    return [(1 - x, y), (x, 1 - y), (1 - x, 1 - y)]


def _place_shards(shards, s_me):
    return [lax.dynamic_update_slice(lax.empty((N_CHIPS,) + s.shape, s.dtype), s[None], (s_me, 0, 0)) for s in shards]


def _ag_copy(src, land, send, recv, wi, j, chip, x, y, c, tc, both):
    hr = src.shape[0] // 2
    half = pl.ds(pl.multiple_of(c * hr, 16), hr)
    k = 3 * wi + j
    return pltpu.make_async_remote_copy(
        src_ref=src.at[half, :], dst_ref=land.at[2 * x + y, half, :],
        send_sem=send.at[2 * k + tc if both else k], recv_sem=recv.at[2 * k + c if both else k],
        device_id=(chip[0], chip[1], tc), device_id_type=MESH)


def _ag_targets(c, both):
    return (0, 1) if both else (c,)


def _ag_start(name, shards, lands, groups, direct, deps=()):
    nw, ng, nd = len(shards), len(groups), len(deps)

    def body(*refs):
        src, land = refs[:nw], refs[nw:2 * nw]
        sems = refs[2 * nw + nd:2 * nw + nd + 2 * ng]
        token = refs[-1]
        x, y, c = _place()
        for gi, grp in enumerate(groups):
            for wi, w in enumerate(grp):
                for j, chip in enumerate(_other_chips(x, y)):
                    for tc in _ag_targets(c, direct[gi]):
                        _ag_copy(src[w], land[w], sems[2 * gi], sems[2 * gi + 1], wi, j, chip, x, y, c, tc, direct[gi]).start()
        token[...] = jnp.zeros_like(token)

    sem_shapes = []
    for gi, grp in enumerate(groups):
        sem_shapes += [pltpu.SemaphoreType.DMA(((6 if direct[gi] else 3) * len(grp),))] * 2
    out_shape = sem_shapes + [pltpu.HBM(s.shape, s.dtype) for s in shards] + [pltpu.HBM(l.shape, l.dtype) for l in lands]
    out_shape += [jax.ShapeDtypeStruct((8, 128), F32)]
    res = _pcall(body, name=name, out_shape=out_shape, in_specs=[HBM_SPEC] * (2 * nw) + [ANY_SPEC] * nd,
                 out_specs=[SEM_SPEC] * (2 * ng) + [HBM_SPEC] * (2 * nw) + [pl.BlockSpec(memory_space=pltpu.VMEM)],
                 input_output_aliases={i: 2 * ng + i for i in range(2 * nw)},
                 compiler_params=pltpu.CompilerParams(has_side_effects=EFFECT))(
                     *[_hbm(s) for s in shards], *[_hbm(l) for l in lands], *deps)
    sems = [(res[2 * gi], res[2 * gi + 1]) for gi in range(ng)]
    return sems, list(res[2 * ng:2 * ng + nw]), list(res[2 * ng + nw:2 * ng + 2 * nw]), res[-1]


def _ag_wait(name, shards, lands, send, recv, both, after):
    n = len(shards)

    def body(*refs):
        src, land = refs[:n], refs[n:2 * n]
        send_sem, recv_sem = refs[2 * n], refs[2 * n + 1]
        x, y, c = _place()
        for wi in range(n):
            for j, chip in enumerate(_other_chips(x, y)):
                for tc in _ag_targets(c, both):
                    _ag_copy(src[wi], land[wi], send_sem, recv_sem, wi, j, chip, x, y, c, tc, both).wait_send()
                    _ag_copy(src[wi], land[wi], send_sem, recv_sem, wi, j, chip, chip[0], chip[1], tc, c, both).wait_recv()

    res = _pcall(body, name=name, out_shape=[pltpu.HBM(a.shape, a.dtype) for a in list(shards) + list(lands)],
                 in_specs=[HBM_SPEC] * (2 * n) + [SEM_SPEC, SEM_SPEC] + [ANY_SPEC] * len(after), out_specs=[HBM_SPEC] * (2 * n),
                 input_output_aliases={i: i for i in range(2 * n)},
                 compiler_params=pltpu.CompilerParams(has_side_effects=EFFECT))(*shards, *lands, send, recv, *after)
    return list(res[n:])


def _ag_forward(name, lands):
    n = len(lands)

    def body(*refs):
        out = refs[n:2 * n]
        send, recv = refs[2 * n], refs[2 * n + 1]
        x, y, c = _place()
        sib = (x, y, 1 - c)
        cps = []
        for wi in range(n):
            hr = out[wi].shape[1] // 2
            for j, (cx, cy) in enumerate(_other_chips(x, y)):
                got = out[wi].at[2 * cx + cy, pl.ds(pl.multiple_of(c * hr, 16), hr), :]
                cp = pltpu.make_async_remote_copy(src_ref=got, dst_ref=got, send_sem=send.at[3 * wi + j], recv_sem=recv.at[3 * wi + j],
                                                  device_id=sib, device_id_type=MESH)
                cp.start()
                cps.append(cp)
        for wi in range(n):
            hr = out[wi].shape[1] // 2
            for j, (cx, cy) in enumerate(_other_chips(x, y)):
                got = out[wi].at[2 * cx + cy, pl.ds(pl.multiple_of((1 - c) * hr, 16), hr), :]
                pltpu.make_async_remote_copy(src_ref=got, dst_ref=got, send_sem=send.at[3 * wi + j], recv_sem=recv.at[3 * wi + j],
                                             device_id=sib, device_id_type=MESH).wait_recv()
        for cp in cps:
            cp.wait_send()

    res = _pcall(body, name=name, out_shape=[jax.ShapeDtypeStruct(l.shape, l.dtype) for l in lands],
                 in_specs=[ANY_SPEC] * n, out_specs=[ANY_SPEC] * n, input_output_aliases={i: i for i in range(n)},
                 scratch_shapes=[pltpu.SemaphoreType.DMA((3 * n,)), pltpu.SemaphoreType.DMA((3 * n,))])(*lands)
    return list(res)


def _peers(x, y, c):
    offs = [(dx, dy, dc) for dx in (0, 1) for dy in (0, 1) for dc in (0, 1)][1:]
    return [(1 - x if dx else x, 1 - y if dy else y, 1 - c if dc else c) for dx, dy, dc in offs]


def _rs_copy(g_ref, land_ref, send, recv, wi, k, to, sender):
    hr = g_ref.shape[1] // 2
    return pltpu.make_async_remote_copy(
        src_ref=g_ref.at[2 * to[0] + to[1], pl.ds(pl.multiple_of(to[2] * hr, 16), hr), :], dst_ref=land_ref.at[sender],
        send_sem=send.at[7 * wi + k], recv_sem=recv.at[7 * wi + k], device_id=to, device_id_type=MESH)


def _rs_start(name, gs):
    n = len(gs)
    lands = [lax.empty((N_DEV, g.shape[1] // 2, g.shape[2]), BF16) for g in gs]

    def body(*refs):
        g, land = refs[:n], refs[n:2 * n]
        send, recv = refs[2 * n], refs[2 * n + 1]
        token = refs[-1]
        x, y, c = _place()
        me = 4 * x + 2 * y + c
        for wi in range(n):
            for k, to in enumerate(_peers(x, y, c)):
                _rs_copy(g[wi], land[wi], send, recv, wi, k, to, me).start()
        token[...] = jnp.zeros_like(token)

    out_shape = [pltpu.SemaphoreType.DMA((7 * n,))] * 2 + [pltpu.HBM(a.shape, a.dtype) for a in list(gs) + lands]
    out_shape += [jax.ShapeDtypeStruct((8, 128), F32)]
    res = _pcall(body, name=name, out_shape=out_shape, in_specs=[HBM_SPEC] * (2 * n),
                 out_specs=[SEM_SPEC] * 2 + [HBM_SPEC] * (2 * n) + [pl.BlockSpec(memory_space=pltpu.VMEM)],
                 input_output_aliases={i: 2 + i for i in range(2 * n)},
                 compiler_params=pltpu.CompilerParams(has_side_effects=EFFECT))(
                     *[_hbm(a) for a in gs], *[_hbm(a) for a in lands])
    return res[0], res[1], list(res[2:2 + n]), list(res[2 + n:2 + 2 * n]), res[-1]


def _rs_wait(name, gs, lands, send, recv, after):
    n = len(gs)

    def body(*refs):
        g, land = refs[:n], refs[n:2 * n]
        send_sem, recv_sem = refs[2 * n], refs[2 * n + 1]
        x, y, c = _place()
        me = 4 * x + 2 * y + c
        for wi in range(n):
            for k, to in enumerate(_peers(x, y, c)):
                _rs_copy(g[wi], land[wi], send_sem, recv_sem, wi, k, to, me).wait_send()
                _rs_copy(g[wi], land[wi], send_sem, recv_sem, wi, k, (x, y, c), 4 * to[0] + 2 * to[1] + to[2]).wait_recv()

    res = _pcall(body, name=name, out_shape=[pltpu.HBM(a.shape, a.dtype) for a in list(gs) + list(lands)],
                 in_specs=[HBM_SPEC] * (2 * n) + [SEM_SPEC, SEM_SPEC, ANY_SPEC], out_specs=[HBM_SPEC] * (2 * n),
                 input_output_aliases={i: i for i in range(2 * n)},
                 compiler_params=pltpu.CompilerParams(has_side_effects=EFFECT))(*gs, *lands, send, recv, after)
    return list(res[:n]), list(res[n:])


def _bc_copy(blk_ref, land_ref, send, recv, k, to, slot):
    return pltpu.make_async_remote_copy(src_ref=blk_ref, dst_ref=land_ref.at[slot], send_sem=send.at[k], recv_sem=recv.at[k],
                                        device_id=to, device_id_type=MESH)


def _bcast_start(name, blk):
    land = lax.empty((N_DEV,) + blk.shape, blk.dtype)

    def body(blk_ref, land_ref, send, recv, blk_thru, land_thru, token):
        x, y, c = _place()
        for k, to in enumerate(_peers(x, y, c)):
            _bc_copy(blk_ref, land_ref, send, recv, k, to, 4 * x + 2 * y + c).start()
        token[...] = jnp.zeros_like(token)

    return _pcall(body, name=name,
                  out_shape=[pltpu.SemaphoreType.DMA((7,)), pltpu.SemaphoreType.DMA((7,)), pltpu.HBM(blk.shape, blk.dtype),
                             pltpu.HBM(land.shape, land.dtype), jax.ShapeDtypeStruct((8, 128), F32)],
                  in_specs=[HBM_SPEC, HBM_SPEC], out_specs=[SEM_SPEC, SEM_SPEC, HBM_SPEC, HBM_SPEC, pl.BlockSpec(memory_space=pltpu.VMEM)],
                  input_output_aliases={0: 2, 1: 3}, compiler_params=pltpu.CompilerParams(has_side_effects=EFFECT))(_hbm(blk), _hbm(land))


def _bcast_wait(name, blk, land, send, recv, after):
    def body(blk_ref, land_ref, send_sem, recv_sem, after_ref, blk_thru, land_thru):
        x, y, c = _place()
        for k, to in enumerate(_peers(x, y, c)):
            _bc_copy(blk_ref, land_ref, send_sem, recv_sem, k, to, 4 * x + 2 * y + c).wait_send()
            _bc_copy(blk_ref, land_ref, send_sem, recv_sem, k, to, 4 * to[0] + 2 * to[1] + to[2]).wait_recv()

    return _pcall(body, name=name, out_shape=[pltpu.HBM(blk.shape, blk.dtype), pltpu.HBM(land.shape, land.dtype)],
                  in_specs=[HBM_SPEC, HBM_SPEC, SEM_SPEC, SEM_SPEC, ANY_SPEC], out_specs=[HBM_SPEC, HBM_SPEC],
                  input_output_aliases={0: 0, 1: 1}, compiler_params=pltpu.CompilerParams(has_side_effects=EFFECT))(
                      blk, land, send, recv, after)[1]


def _rs_sum(name, gs, lands):
    n = len(gs)

    def body(*refs):
        g_refs, land_refs, out_refs = refs[:n], refs[n:2 * n], refs[2 * n:3 * n]
        recvs = refs[3 * n:4 * n]
        local_sems, sib_send, sib_recv = refs[4 * n:]
        x, y, c = _place()
        me = 4 * x + 2 * y + c
        cps = []
        for wi in range(n):
            hr = g_refs[wi].shape[1] // 2
            own = g_refs[wi].at[2 * x + y, pl.ds(pl.multiple_of(c * hr, 16), hr), :]
            cps.append(pltpu.make_async_copy(own, recvs[wi].at[me], local_sems.at[8 * wi + 7]))
            for k, (tx, ty, tc) in enumerate(_peers(x, y, c)):
                slot = 4 * tx + 2 * ty + tc
                cps.append(pltpu.make_async_copy(land_refs[wi].at[slot], recvs[wi].at[slot], local_sems.at[8 * wi + k]))
        for cp in cps:
            cp.start()
        sibs = []
        for wi in range(n):
            hr = g_refs[wi].shape[1] // 2
            ch = _pick(hr, 64, 16)
            for cp in cps[8 * wi:8 * wi + 8]:
                cp.wait()
            base = pl.multiple_of(c * hr, 16)
            for r0 in range(0, hr, ch):
                acc = recvs[wi][0, r0:r0 + ch, :].astype(F32)
                for k in range(1, N_DEV):
                    acc = acc + recvs[wi][k, r0:r0 + ch, :].astype(F32)
                out_refs[wi][pl.ds(base + r0, ch), :] = acc
            half = out_refs[wi].at[pl.ds(base, hr), :]
            sib = pltpu.make_async_remote_copy(src_ref=half, dst_ref=half, send_sem=sib_send.at[wi], recv_sem=sib_recv.at[wi],
                                               device_id=(x, y, 1 - c), device_id_type=MESH)
            sib.start()
            sibs.append(sib)
        for wi in range(n):
            hr = g_refs[wi].shape[1] // 2
            other = out_refs[wi].at[pl.ds(pl.multiple_of((1 - c) * hr, 16), hr), :]
            pltpu.make_async_remote_copy(src_ref=other, dst_ref=other, send_sem=sib_send.at[wi], recv_sem=sib_recv.at[wi],
                                         device_id=(x, y, 1 - c), device_id_type=MESH).wait_recv()
            sibs[wi].wait_send()

    res = _pcall(
        body, name=name, out_shape=[jax.ShapeDtypeStruct(g.shape[1:], F32) for g in gs],
        in_specs=[ANY_SPEC] * (2 * n), out_specs=[pl.BlockSpec(memory_space=pltpu.VMEM)] * n,
        scratch_shapes=[pltpu.VMEM((N_DEV, g.shape[1] // 2, g.shape[2]), BF16) for g in gs]
        + [pltpu.SemaphoreType.DMA((8 * n,)), pltpu.SemaphoreType.DMA((n,)), pltpu.SemaphoreType.DMA((n,))],
        compiler_params=pltpu.CompilerParams(vmem_limit_bytes=VMEM_LIMIT))(*gs, *lands)
    return list(res)


def _gate_norm_epilogue(factor, x_in, gt, nxt, t, d):
    blk = (512, d)
    full = lambda i, j: (i, j)
    rowv = lambda i, j: (0, j)

    def epi(acc, res, scale, *norm):
        x_new = res + (factor * scale) * acc
        if not norm:
            return x_new, acc
        gv, scv, shv = norm
        r = lax.rsqrt(jnp.mean(x_new * x_new, axis=-1, keepdims=True) + EPS)
        return x_new, acc, x_new * r * gv * (1.0 + scv) + shv

    ins = [(x_in, blk, full), (gt, (1, d), rowv)] + [(v, (1, d), rowv) for v in (nxt or ())]
    outs = [((t, d), F32, blk, full), ((t, d), BF16, blk, full)] + ([((t, d), BF16, blk, full)] if nxt else [])
    return epi, ins, outs


_FULL = lambda i, j: (i, j)
_ROWV = lambda i, j: (0, j)


def _glu_epilogue(b_glu, t, w):
    def epi(acc, bv):
        g = acc + bv
        return acc, g[:, :w] * _sigmoid(g[:, w:])

    return epi, [(b_glu, (1, 2 * w), _ROWV)], [((t, 2 * w), BF16, (512, 2 * w), _FULL), ((t, w), BF16, (512, w), _FULL)]


def _glu_bwd_epilogue(gv, b_glu, t, w):
    def epi(dsg, gvv, bv):
        g = gvv.astype(F32) + bv
        val, s = g[:, :w], _sigmoid(g[:, w:])
        dval, dgate = dsg * s, dsg * val * s * (1.0 - s)
        return [jnp.concatenate([dval, dgate], axis=1)], [jnp.concatenate([_colsum(dval), _colsum(dgate)], axis=1)]

    return epi, [(gv, (512, 2 * w), _FULL), (b_glu, (1, 2 * w), _ROWV)], [((t, 2 * w), BF16, (512, 2 * w), _FULL)], [2 * w]


def _gates_epilogue(y_pool, uin, t, d, pw):
    cb = (2 * pw) // d

    def epi(acc, yp, glp, gls):
        return acc, _sigmoid(glp.astype(F32)) * yp.astype(F32) + _sigmoid(gls.astype(F32)) * acc

    ins = [(y_pool, (512, d), _FULL), (uin, (512, d), lambda i, j: (i, cb)), (uin, (512, d), lambda i, j: (i, cb + 1))]
    return epi, ins, [((t, d), BF16, (512, d), _FULL)] * 2


def _gates_bwd_epilogue(y_pool, y_ssm, uin, t, d, pw):
    cb = (2 * pw) // d

    def epi(dm, yp, ys, glp, gls):
        sp, ss = _sigmoid(glp.astype(F32)), _sigmoid(gls.astype(F32))
        dgl = jnp.concatenate([dm * yp.astype(F32) * sp * (1.0 - sp), dm * ys.astype(F32) * ss * (1.0 - ss)], axis=1)
        return dm * sp, dm * ss, ("at", 2 * pw, dgl)

    ins = [(y_pool, (512, d), _FULL), (y_ssm, (512, d), _FULL),
           (uin, (512, d), lambda i, j: (i, cb)), (uin, (512, d), lambda i, j: (i, cb + 1))]
    wide = 2 * pw + 2 * d
    return epi, ins, [((t, d), BF16, (512, d), _FULL)] * 2 + [((t, wide), BF16, (512, wide), _FULL)]


def _loss_epilogue(x_in, gt, tgt, g_final, t, d):
    blk = (512, d)
    full = lambda i, j: (i, j)
    rowv = lambda i, j: (0, j)

    def epi(acc, res, scale, tv, gv):
        xv = res + (0.5 * scale) * acc
        r = lax.rsqrt(jnp.mean(xv * xv, axis=-1, keepdims=True) + EPS)
        xr = xv * r
        e = xr * gv - tv
        loss_row = 0.5 * jnp.mean(e * e, axis=-1, keepdims=True)
        dout = e * (1.0 / d)
        gd = gv * dout
        dx = r * (gd - xr * jnp.mean(gd * xr, axis=-1, keepdims=True))
        fdx = 0.5 * dx
        return [dx, scale * fdx], [_colsum(dout * xr), _colsum(loss_row * jnp.ones((1, 128), F32)), _colsum(fdx * acc)]

    ins = [(x_in, blk, full), (gt, (1, d), rowv), (tgt, blk, full), (g_final, (1, d), rowv)]
    return epi, ins, [((t, d), F32, blk, full), ((t, d), BF16, blk, full)], [d, 128, d]


def _norm_bwd_epilogue(x, dres, g, sc, up, t, d):
    blk = (512, d)
    full = lambda i, j: (i, j)
    rowv = lambda i, j: (0, j)

    def epi(dhv, xv, drv, gv, scv, *rest):
        r = lax.rsqrt(jnp.mean(xv * xv, axis=-1, keepdims=True) + EPS)
        xr = xv * r
        dn = dhv * (1.0 + scv)
        gd = gv * dn
        dx = drv + r * (gd - xr * jnp.mean(gd * xr, axis=-1, keepdims=True))
        outs, reds = [dx], [_colsum(dhv), _colsum(dhv * xr * gv), _colsum(dn * xr)]
        if up:
            yv, gtv = rest
            fdx = up[2] * dx
            outs.append(gtv * fdx)
            reds.append(_colsum(fdx * yv.astype(F32)))
        return outs, reds

    ins = [(x, blk, full), (dres, blk, full), (g, (1, d), rowv), (sc, (1, d), rowv)]
    ins += [(up[0], blk, full), (up[1], (1, d), rowv)] if up else []
    outs = [((t, d), F32, blk, full)] + ([((t, d), BF16, blk, full)] if up else [])
    return epi, ins, outs, [d] * (4 if up else 3)


def _ffn_in_act(name, h, w_in, tm=512):
    t, d = h.shape
    ns = w_in.shape[2]
    tm = _pick(t, tm, 8)
    w4 = w_in.reshape(2, 2, d, ns)

    def body(h_ref, w_ref, ab_ref, act_ref):
        hv = h_ref[...]
        a = jnp.dot(hv, w_ref[0], preferred_element_type=F32)
        b = jnp.dot(hv, w_ref[1], preferred_element_type=F32)
        ab_ref[0] = a.astype(BF16)
        ab_ref[1] = b.astype(BF16)
        act_ref[...] = (a * _sigmoid(a) * b).astype(BF16)

    return _pcall(body, name=name, grid=(2, t // tm),
                  in_specs=[pl.BlockSpec((tm, d), lambda c, i: (i, 0)), pl.BlockSpec((2, None, d, ns), lambda c, i: (0, c, 0, 0))],
                  out_specs=[pl.BlockSpec((2, tm, ns), lambda c, i: (0, i, c)), pl.BlockSpec((tm, ns), lambda c, i: (i, c))],
                  out_shape=[jax.ShapeDtypeStruct((2, t, 2 * ns), BF16), jax.ShapeDtypeStruct((t, 2 * ns), BF16)],
                  compiler_params=_params("parallel", "parallel"))(h, w4)


def _dswiglu_epilogue(ab, t, f, tn):
    blk = (2, 512, tn)
    idx = lambda i, j: (0, i, j)

    def epi(dact, abv):
        a, b = abv[0].astype(F32), abv[1].astype(F32)
        s = _sigmoid(a)
        return (jnp.stack([dact * b * (s * (1.0 + a * (1.0 - s))), dact * (a * s)]),)

    return epi, [(ab, blk, idx)], [((2, t, f), BF16, blk, idx)]


def _ffn_fwd(tag, x, h, gt, get_w_in, get_w_out, nxt=None, loss=None):
    t, d = x.shape
    ab, act = _ffn_in_act(tag + "_in", h, get_w_in(h))
    if loss:
        epi, epi_ins, epi_outs, epi_reds = _loss_epilogue(x, gt, *loss, t, d)
        res = _mm(tag + "_out", act, get_w_out(act), tm=512, tn=d, epi=epi, epi_ins=epi_ins, epi_outs=epi_outs,
                  epi_reds=epi_reds, b_resident=True)
        return res, (x, h, ab, act, None)
    epi, epi_ins, epi_outs = _gate_norm_epilogue(0.5, x, gt, nxt, t, d)
    res = _mm(tag + "_out", act, get_w_out(act), tm=512, tn=d, epi=epi, epi_ins=epi_ins, epi_outs=epi_outs, b_resident=True)
    return res[0], res[2], (x, h, ab, act, res[1])


def _ffn_bwd(tag, dx_new, dy, saved, g, sc, w_in, w_out, start_rs, up, deps=()):
    x, h, ab, act, _ = saved
    d = x.shape[1]
    f = act.shape[1]
    dw_out = _mm(tag + "_dwout", act, dy, ta=True, tm=1408, tn=d, tk=2048, deps=deps)
    tok = start_rs("out", dw_out.reshape(N_CHIPS, f // N_CHIPS, d))
    epi, epi_ins, epi_outs = _dswiglu_epilogue(ab, x.shape[0], f, w_in.shape[2])
    dab = _mm(tag + "_dact", dy, w_out, tb=True, tm=512, tn=w_in.shape[2], epi=epi, epi_ins=epi_ins, epi_outs=epi_outs,
              deps=(tok,), j_outer=True)
    dw_in = _mm(tag + "_dwin", h, dab, ta=True, out_stacked=True, b_halves=True, tm=d, tn=1408, tk=2048)
    tok = start_rs("in", dw_in)
    epi, epi_ins, epi_outs, epi_reds = _norm_bwd_epilogue(x, dx_new, g, sc, up, x.shape[0], d)
    res = _mm(tag + "_dh", dab, w_in, tb=True, b_stacked=True, a_halves=True, tn=d, tk=5632, deps=(tok,), epi=epi,
              epi_ins=epi_ins, epi_outs=epi_outs, epi_reds=epi_reds, b_resident=True)
    if up:
        return res
    return res[0], None, res[1], res[2], res[3], None


def _row(v):
    return v.reshape(1, -1)


def _pack(parts):
    cols = []
    for p in parts:
        flat = p.reshape(-1).astype(F32)
        padn = (-flat.shape[0]) % 128
        cols.append(jnp.pad(flat, (0, padn)) if padn else flat)
    flat = jnp.concatenate(cols)
    padn = (-flat.shape[0]) % 1024
    if padn:
        flat = jnp.pad(flat, (0, padn))
    return flat.reshape(-1, 128)


def _unpack(packed, shapes):
    flat = packed.reshape(-1)
    out, off = [], 0
    for s in shapes:
        n = math.prod(s)
        out.append(flat[off:off + n].reshape(s))
        off += n + ((-n) % 128)
    return out


SMALL = ['b_ada', 'g_ffn1', 'g_mix', 'pool_w', 'pool_b', 'pool_scale', 'ssm_lam_re_log', 'ssm_lam_im', 'ssm_log_dt',
         'ssm_b_re', 'ssm_b_im', 'ssm_c_re', 'ssm_c_im', 'ssm_d', 'b_glu', 'g_ffn2', 'g_final']
BIG = ['w_ffn1_in', 'w_ffn1_out', 'w_in', 'w_pool_up', 'w_glu', 'w_ssm_up', 'w_out', 'w_ffn2_in', 'w_ffn2_out']
AG_GROUPS = [[0], [1], [2, 3, 4, 5, 6], [7, 8]]
AG_DIRECT = [False, False, False, True]
SMALL_LATE = ['b_ada', 'g_ffn1']
RS_SUM_GROUPS = [[0, 1], [2, 3], [4, 5]]
WEIGHTS = ['w_ada', 'b_ada', 'g_ffn1', 'w_ffn1_in', 'w_ffn1_out', 'g_mix', 'w_in', 'pool_w', 'pool_b', 'pool_scale', 'w_pool_up',
           'ssm_lam_re_log', 'ssm_lam_im', 'ssm_log_dt', 'ssm_b_re', 'ssm_b_im', 'ssm_c_re', 'ssm_c_im', 'ssm_d', 'w_glu', 'b_glu',
           'w_ssm_up', 'w_out', 'g_ffn2', 'w_ffn2_in', 'w_ffn2_out', 'g_final']


def kernel(x, c, w_ada, b_ada, g_ffn1, w_ffn1_in, w_ffn1_out, g_mix, w_in, pool_w, pool_b, pool_scale, w_pool_up, ssm_lam_re_log, ssm_lam_im, ssm_log_dt, ssm_b_re, ssm_b_im, ssm_c_re, ssm_c_im, ssm_d, w_glu, b_glu, w_ssm_up, w_out, g_ffn2, w_ffn2_in, w_ffn2_out, g_final, loss_target, m_w_ada, m_b_ada, m_g_ffn1, m_w_ffn1_in, m_w_ffn1_out, m_g_mix, m_w_in, m_pool_w, m_pool_b, m_pool_scale, m_w_pool_up, m_ssm_lam_re_log, m_ssm_lam_im, m_ssm_log_dt, m_ssm_b_re, m_ssm_b_im, m_ssm_c_re, m_ssm_c_im, m_ssm_d, m_w_glu, m_b_glu, m_w_ssm_up, m_w_out, m_g_ffn2, m_w_ffn2_in, m_w_ffn2_out, m_g_final, v_w_ada, v_b_ada, v_g_ffn1, v_w_ffn1_in, v_w_ffn1_out, v_g_mix, v_w_in, v_pool_w, v_pool_b, v_pool_scale, v_w_pool_up, v_ssm_lam_re_log, v_ssm_lam_im, v_ssm_log_dt, v_ssm_b_re, v_ssm_b_im, v_ssm_c_re, v_ssm_c_im, v_ssm_d, v_w_glu, v_b_glu, v_w_ssm_up, v_w_out, v_g_ffn2, v_w_ffn2_in, v_w_ffn2_out, v_g_final):
    args = dict(locals())
    wt = {n: args[n] for n in WEIGHTS}
    mom = {n: args["m_" + n] for n in WEIGHTS}
    var = {n: args["v_" + n] for n in WEIGHTS}

    t, d = x.shape[1], x.shape[2]
    pw = pool_b.shape[1]
    sw = ssm_d.shape[1]
    ngrp = sw // SSM_GROUP
    gn = ngrp * SSM_STATE
    xi, yi, ci = _place()
    b_me = 4 * xi + 2 * yi + ci
    s_me = 2 * xi + yi
    x2d = x[0]
    tgt = loss_target[0]

    c_all = _allgather_small("ag_c", c.reshape(8, d // 8)).reshape(N_DEV, d)
    ncol = w_ada.shape[2]
    b_sh = lax.dynamic_slice(b_ada, (0, s_me * ncol), (1, ncol))
    mod_sh = _ada_fwd("ada_fwd", c_all, w_ada[0], b_sh)
    md_send, md_recv, mod_sh, md_land, tok = _bcast_start("mod_start", mod_sh)

    shards = [wt[n][0].astype(BF16) for n in BIG]
    ag_sems, shards_t, lands_t, tok = _ag_start("ag_start", shards, _place_shards(shards, s_me), AG_GROUPS, AG_DIRECT, deps=(tok,))
    md_land = _bcast_wait("mod_wait", mod_sh, md_land, md_send, md_recv, tok)
    mod_all = lax.dynamic_update_slice(md_land, mod_sh[None], (b_me, 0, 0))
    full = {}

    def weights(gi, *after):
        grp = AG_GROUPS[gi]
        if BIG[grp[0]] not in full:
            ls = _ag_wait("ag_wait%d" % gi, [shards_t[w] for w in grp], [lands_t[w] for w in grp], *ag_sems[gi],
                          AG_DIRECT[gi], after)
            for w, l in zip(grp, ls if AG_DIRECT[gi] else _ag_forward("ag_fwd%d" % gi, ls)):
                full[BIG[w]] = l
        return full

    mod_me = jnp.concatenate([lax.dynamic_slice(mod_all, (2 * s, b_me, 0), (1, 1, ncol))[0] for s in range(N_CHIPS)], axis=1)
    mod = mod_me.reshape(9, d)
    sh1, sc1, gt1, sh2, sc2, gt2, sh3, sc3, gt3 = [mod[k:k + 1] for k in range(9)]

    f = w_ffn1_out.shape[1] * N_CHIPS

    col = lambda a: a.reshape(gn, 1)
    lrl_c, li_c = col(ssm_lam_re_log), col(ssm_lam_im)
    ldt_c = col(jnp.broadcast_to(ssm_log_dt.reshape(ngrp, 1), (ngrp, SSM_STATE)))
    b_re2, b_im2 = ssm_b_re.reshape(gn, SSM_GROUP), ssm_b_im.reshape(gn, SSM_GROUP)
    lrdt_c, ang_c, bb_re, bb_im = _ssm_prep("ssm_prep", lrl_c, li_c, ldt_c, b_re2, b_im2)
    lrdt, ang = lrdt_c.reshape(1, gn), ang_c.reshape(1, gn)
    tr = lambda a: jnp.swapaxes(a, 1, 2)
    bbd = [_blockdiag_b(v, sw).astype(BF16) for v in (bb_re, bb_im)]
    ccd = [_blockdiag_c(v[0], sw).astype(BF16) for v in (ssm_c_re, ssm_c_im)]
    bbd_t, ccd_t = [tr(v) for v in bbd], [tr(v) for v in ccd]
    early = [n for n in SMALL if n not in SMALL_LATE]
    packs = {}
    for tag, names, extra in (("early", early, []), ("late", SMALL_LATE, [jnp.zeros((128,), F32)])):
        packs[tag] = [_pack([src[n] for n in names] + extra) for src in (wt, mom, var)]
    shadow = [lrdt, ang, ldt_c, *bbd, *ccd, *bbd_t, *ccd_t, *packs["early"], *packs["late"]]

    h1 = _norm_fwd("ffn1_norm", x2d, g_ffn1, sc1, sh1, deps=(tok,))
    x1, h2, sav1 = _ffn_fwd("ffn1", x2d, h1, gt1, lambda after: weights(0, after, *shadow)['w_ffn1_in'],
                            lambda after: weights(1, after)['w_ffn1_out'].reshape(f, d), (g_mix, sc2, sh2))
    weights(2, h2)
    wo = full['w_out'].reshape(d, d)
    uin = _mm("mix_in", h2, full['w_in'], b_stacked=True, tm=1024, tn=768, j_outer=True)
    p_pool, z_pool = _pool_fwd("pool_fwd", uin, pool_w[0], pool_b, pool_scale, pw)
    y_pool = _mm("pool_up", p_pool, full['w_pool_up'], b_stacked=True, tm=1024, tn=d)
    y_s, ge, s_re, s_im = _ssm_fwd("ssm_fwd", uin, lrdt, ang, *bbd, *ccd, ssm_d, sw)
    epi, epi_ins, epi_outs = _glu_epilogue(b_glu, t, sw)
    gv, sg = _mm("glu_in", ge, full['w_glu'], b_stacked=True, tm=512, tn=d, epi=epi, epi_ins=epi_ins, epi_outs=epi_outs,
                 b_resident=True)
    epi, epi_ins, epi_outs = _gates_epilogue(y_pool, uin, t, d, pw)
    y_ssm, merged = _mm("ssm_up", sg, full['w_ssm_up'], b_stacked=True, tm=512, tn=d, epi=epi, epi_ins=epi_ins,
                        epi_outs=epi_outs, b_resident=True)
    epi, epi_ins, epi_outs = _gate_norm_epilogue(1.0, x1, gt2, (g_ffn2, sc3, sh3), t, d)
    x2, y2, h3 = _mm("mix_out", merged, wo, tm=512, tn=d, epi=epi, epi_ins=epi_ins, epi_outs=epi_outs, b_resident=True)

    (dx3, dy3, dg_final, loss_v, dgt3), sav3 = _ffn_fwd(
        "ffn2", x2, h3, gt3, lambda after: weights(3, after)['w_ffn2_in'],
        lambda after: weights(3, after)['w_ffn2_out'].reshape(f, d), loss=(tgt, _row(g_final)))

    gs = {}
    rs_open = []

    def start_rs(names, gbs):
        send, recv, g_thru, land_thru, token = _rs_start("rs_start_" + names[0], gbs)
        rs_open.append((names, send, recv, g_thru, land_thru))
        return token

    dx2, dy2, dsh3, dsc3, gs['g_ffn2'], dgt2 = _ffn_bwd(
        "ffn2b", dx3, dy3, sav3, g_ffn2, sc3, full['w_ffn2_in'], full['w_ffn2_out'].reshape(f, d),
        lambda key, g: start_rs(['w_ffn2_' + key], [g]), (y2, gt2, 1.0))

    epi, epi_ins, epi_outs = _gates_bwd_epilogue(y_pool, y_ssm, uin, t, d, pw)
    dy_pool, dy_ssm, duin = _mm("mix_dmerged", dy2, wo, tb=True, tm=512, tn=d, epi=epi, epi_ins=epi_ins, epi_outs=epi_outs,
                                b_resident=True)
    g_wo = _mm("mix_dwout", merged, dy2, ta=True, tm=d, tn=d, tk=2048).reshape(N_CHIPS, d // N_CHIPS, d)

    dp = _mm("pool_dup", dy_pool, full['w_pool_up'], tb=True, b_stacked=True, tm=1024, tn=pw, tk=d)
    g_wpu = _mm("pool_dwup", p_pool, dy_pool, ta=True, out_stacked=True, tm=pw, tn=d, tk=4096)
    duin, gs['pool_w'], gs['pool_b'], gs['pool_scale'] = _pool_bwd("pool_bwd", dp, z_pool, pool_w[0], pool_b, pool_scale, duin)

    g_wsu = _mm("ssm_dwup", sg, dy_ssm, ta=True, out_stacked=True, tm=sw, tn=d, tk=4096)
    tok = start_rs(['w_out', 'w_pool_up', 'w_ssm_up'], [g_wo, g_wpu, g_wsu])
    epi, epi_ins, epi_outs, epi_reds = _glu_bwd_epilogue(gv, b_glu, t, sw)
    dgv, gs['b_glu'] = _mm("ssm_dup", dy_ssm, full['w_ssm_up'], tb=True, b_stacked=True, tm=512, tn=sw, tk=d, deps=(tok,),
                           epi=epi, epi_ins=epi_ins, epi_outs=epi_outs, epi_reds=epi_reds, b_resident=True)
    dge = _mm("glu_dge", dgv, full['w_glu'], tb=True, b_stacked=True, tm=1024, tn=sw, tk=d)
    g_wglu = _mm("glu_dw", ge, dgv, ta=True, out_stacked=True, tm=sw, tn=d, tk=4096)
    (duin, g_abre, g_abim, g_bbd_re, g_bbd_im, g_ccd_re, g_ccd_im, gs['ssm_d']) = _ssm_bwd(
        "ssm_bwd", dge, y_s, uin, s_re, s_im, lrdt, ang, *bbd_t, *ccd_t, ssm_d, sw, duin)
    gs['ssm_c_re'], gs['ssm_c_im'] = _diag_of_c(g_ccd_re, sw), _diag_of_c(g_ccd_im, sw)
    d_lrl, d_li, d_ldt, d_bre, d_bim = _ssm_param_bwd(
        "ssm_param_bwd", lrl_c, li_c, ldt_c, b_re2, b_im2, g_abre.reshape(gn, 1), g_abim.reshape(gn, 1),
        _diag_of_b(g_bbd_re, sw), _diag_of_b(g_bbd_im, sw))
    gs['ssm_lam_re_log'], gs['ssm_lam_im'] = d_lrl, d_li
    gs['ssm_log_dt'] = jnp.sum(d_ldt.reshape(ngrp, SSM_STATE), axis=1)
    gs['ssm_b_re'], gs['ssm_b_im'] = d_bre, d_bim

    g_win =_mm("mix_dwin", h2, duin, ta=True, out_stacked=True, tm=d, tn=768, tk=4096)
    tok = start_rs(['w_glu', 'w_in'], [g_wglu, g_win])
    epi, epi_ins, epi_outs, epi_reds = _norm_bwd_epilogue(x1, dx2, g_mix, sc2, (sav1[4], gt1, 0.5), t, d)
    dx1, dy1, dsh2, dsc2, gs['g_mix'], dgt1 = _mm(
        "mix_dh", duin, full['w_in'], tb=True, b_stacked=True, tn=d, tk=3072, deps=(tok,), epi=epi, epi_ins=epi_ins,
        epi_outs=epi_outs, epi_reds=epi_reds, b_resident=True)

    gs['g_final'] = dg_final
    sg_blk = _pack([gs[n] for n in early])
    sg_send, sg_recv, sg_blk, sg_land, tok = _bcast_start("sg_start", sg_blk)

    dx0, _, dsh1, dsc1, gs['g_ffn1'], _ = _ffn_bwd(
        "ffn1b", dx1, dy1, sav1, g_ffn1, sc1, full['w_ffn1_in'], full['w_ffn1_out'].reshape(f, d),
        lambda key, g: start_rs(['w_ffn1_' + key], [g]), None, deps=(tok,))

    gs['b_ada'] = jnp.concatenate([dsh1, dsc1, dgt1, dsh2, dsc2, dgt2, dsh3, dsc3, dgt3], axis=1)
    lt_blk = _pack([gs[n] for n in SMALL_LATE] + [loss_v])
    lt_send, lt_recv, lt_blk, lt_land, tok = _bcast_start("late_start", lt_blk)

    grads, delta, new_m, new_v = {}, {}, {}, {}

    def small_update(tag, names, blk, land):
        g8 = lax.dynamic_update_slice(land, blk[None], (b_me, 0, 0)).reshape(-1, 128)
        w_pack, m_pack, v_pack = packs[tag]
        shapes = [wt[n].shape for n in names]
        per_param, packed = _adamw_small("adamw_small_" + tag, w_pack, g8, m_pack, v_pack, shapes)
        for k, dst in enumerate((grads, delta, new_m, new_v)):
            rest = _unpack(packed[k], shapes)
            for i, n in enumerate(names):
                dst[n] = per_param[i][k] if per_param[i] is not None else rest[i]
        return g8

    after = tok
    for group in RS_SUM_GROUPS:
        names, g_done, land_done = [], [], []
        for k in group:
            nk, send, recv, g_thru, land_thru = rs_open[k]
            gd, ld = _rs_wait("rs_wait_" + nk[0], g_thru, land_thru, send, recv, after)
            names, g_done, land_done = names + nk, g_done + gd, land_done + ld
        for n, g_sum in zip(names, _rs_sum("rs_sum_" + names[0], g_done, land_done)):
            g_out, dl, mn, vn = _adamw("adamw_" + n, wt[n][0], g_sum, mom[n][0], var[n][0])
            grads[n], delta[n], new_m[n], new_v[n] = g_out[None], dl[None], mn[None], vn[None]
            after = dl
        if group is RS_SUM_GROUPS[-2]:
            small_update("early", early, sg_blk, _bcast_wait("sg_wait", sg_blk, sg_land, sg_send, sg_recv, after))

    lt_land = _bcast_wait("late_wait", lt_blk, lt_land, lt_send, lt_recv, after)
    late8 = small_update("late", SMALL_LATE, lt_blk, lt_land).reshape(N_DEV, -1)
    loss = jnp.sum(late8[:, 10 * d])
    dmod_sh = lax.dynamic_slice(late8, (0, s_me * ncol), (N_DEV, ncol))
    g_w_ada = _ada_bwd("ada_bwd", c_all, dmod_sh)
    dl, mn, vn = _adamw("adamw_w_ada", w_ada[0], g_w_ada, m_w_ada[0], v_w_ada[0], with_g=False)
    grads['w_ada'], delta['w_ada'], new_m['w_ada'], new_v['w_ada'] = g_w_ada[None], dl[None], mn[None], vn[None]

    return (loss, dx0[None], *[grads[n] for n in WEIGHTS], *[delta[n] for n in WEIGHTS],
            *[new_m[n] for n in WEIGHTS], *[new_v[n] for n in WEIGHTS])
```

```python
import functools
import math

import jax
import jax.numpy as jnp
from jax import lax
from jax.experimental import pallas as pl
from jax.experimental.pallas import tpu as pltpu

F32 = jnp.float32
BF16 = jnp.bfloat16
MESH = pl.DeviceIdType.MESH

EPS = 1e-6
POOL_WINDOWS = (2, 4, 8, 16)
POOL_HALO = 16
SSM_GROUP = 16
SSM_STATE = 64
SSM_BLOCKS = 4
N_DEV = 8
N_CHIPS = 4
ADAM_LR = 0.001
ADAM_B1 = 0.9
ADAM_B2 = 0.999
ADAM_EPS = 1e-08
ADAM_WD = 0.01
ADAM_STEP = 10
VMEM_LIMIT = 56 * 1024 * 1024
ROWS = 512


ANY_SPEC = pl.BlockSpec(memory_space=pl.ANY)
HBM_SPEC = pl.BlockSpec(memory_space=pltpu.HBM)
SEM_SPEC = pl.BlockSpec(memory_space=pltpu.SEMAPHORE)
EFFECT = pltpu.SideEffectType.DATAFLOW_SIDE_EFFECTING


def _hbm(a):
    return pltpu.with_memory_space_constraint(a, pltpu.HBM)


def _pcall(body, **kw):
    return pl.pallas_call(body, **kw)


def _params(*sem):
    return pltpu.CompilerParams(dimension_semantics=sem, vmem_limit_bytes=VMEM_LIMIT)


def _pick(n, cap, mult=128):
    if n <= cap:
        return n
    best = None
    for d in range(mult, cap + 1, mult):
        if n % d == 0:
            best = d
    assert best is not None, (n, cap, mult)
    return best


def _sigmoid(v):
    return 1.0 / (1.0 + jnp.exp(-v))


def _rowwise(name, fn, ins, params, outs, reds, tm, deps=()):
    t = ins[0][0].shape[0]
    tm = min(tm, t)
    nb = t // tm
    ni, npar, no, nd = len(ins), len(params), len(outs), len(deps)

    def body(*refs):
        iv = [r[...] for r in refs[:ni]]
        pv = [r[...] for r in refs[ni:ni + npar]]
        o_refs = refs[ni + npar + nd:ni + npar + nd + no]
        r_refs = refs[ni + npar + nd + no:]
        ovals, rvals = fn(iv, pv)
        for o_ref, val in zip(o_refs, ovals):
            off = 0
            if isinstance(val, tuple) and val[0] == "at":
                _, off, val = val
            parts = val if isinstance(val, (list, tuple)) else [val]
            for p in parts:
                o_ref[:, off:off + p.shape[1]] = p.astype(o_ref.dtype)
                off += p.shape[1]
        if r_refs:
            @pl.when(pl.program_id(0) == 0)
            def _():
                for r in r_refs:
                    r[...] = jnp.zeros_like(r)
            for r, val in zip(r_refs, rvals):
                r[...] += val

    in_specs = [pl.BlockSpec((tm, w), functools.partial(lambda i, cb: (i, cb), cb=cb)) for (_, w, cb) in ins]
    in_specs += [pl.BlockSpec(p.shape, lambda i: (0, 0)) for p in params]
    in_specs += [ANY_SPEC] * nd
    out_shape = [jax.ShapeDtypeStruct((t, w), dt) for (w, dt) in outs]
    out_shape += [jax.ShapeDtypeStruct((1, w), F32) for w in reds]
    out_specs = [pl.BlockSpec((tm, w), lambda i: (i, 0)) for (w, _) in outs]
    out_specs += [pl.BlockSpec((1, w), lambda i: (0, 0)) for w in reds]
    res = _pcall(body, name=name, grid=(nb,), in_specs=in_specs, out_specs=out_specs, out_shape=out_shape,
                 compiler_params=_params("arbitrary"))(*[a for a, _, _ in ins], *params, *deps)
    return res


def _colsum(v):
    return jnp.sum(v, axis=0, keepdims=True)


def _mm(name, a, b, *, ta=False, tb=False, b_stacked=False, out_stacked=False, tm=ROWS, tn=1024, tk=2816,
        out_dtype=BF16, epi=None, epi_ins=(), epi_outs=None, epi_reds=(), deps=(), j_outer=False, a_halves=False,
        b_halves=False, b_resident=False):
    if a_halves:
        _, m, kdim = a.shape
        kdim *= 2
    elif ta:
        kdim, m = a.shape
    else:
        m, kdim = a.shape
    ns = None
    if b_stacked:
        ns = b.shape[2]
        n = b.shape[1] if tb else N_CHIPS * ns
        assert kdim == (N_CHIPS * ns if tb else b.shape[1]), (name, a.shape, b.shape)
    else:
        if b_halves:
            n = 2 * b.shape[2]
            assert kdim == b.shape[1] and not tb, (name, a.shape, b.shape)
        else:
            n = b.shape[0] if tb else b.shape[1]
            assert kdim == (b.shape[1] if tb else b.shape[0]), (name, a.shape, b.shape)
        if out_stacked:
            ns = n // N_CHIPS

    def shards(want):
        return max(g for g in (1, 2, 4) if g * ns <= max(want, ns))

    tm = _pick(m, tm, 128 if ta else 8)
    gn = gk = 1
    if (b_stacked and not tb) or out_stacked:
        gn = shards(min(tn, n // 2) if b_halves else tn)
        tn = gn * ns
    else:
        tn = _pick(n, tn)
    if b_stacked and tb:
        gk = shards(tk)
        tk = gk * ns
    else:
        tk = _pick(kdim, tk, 8 if ta else 128)
    nm, nn, nk = m // tm, n // tn, kdim // tk

    def ij(f):
        return (lambda g0, g1, k: f(g1, g0, k)) if j_outer else f

    if a_halves:
        assert b_stacked and tb and gk == N_CHIPS and nk == 1, name
        a_spec = pl.BlockSpec((2, tm, kdim // 2), ij(lambda i, j, k: (0, i, 0)))
    elif ta:
        a_spec = pl.BlockSpec((tk, tm), ij(lambda i, j, k: (k, i)))
    else:
        a_spec = pl.BlockSpec((tm, tk), ij(lambda i, j, k: (i, k)))
    if b_stacked and not tb:
        b_spec = pl.BlockSpec((gn, tk, ns), ij(lambda i, j, k: (j, k, 0)))
    elif b_stacked and tb:
        b_spec = pl.BlockSpec((gk, tn, ns), ij(lambda i, j, k: (k, j, 0)))
    elif b_halves:
        bph = (n // 2) // tn
        b_spec = pl.BlockSpec((None, tk, tn), ij(lambda i, j, k: (j // bph, k, j % bph)))
    elif tb:
        b_spec = pl.BlockSpec((tn, tk), ij(lambda i, j, k: (j, k)))
    else:
        b_spec = pl.BlockSpec((tk, tn), ij(lambda i, j, k: (k, j)))
    dims = (((0 if ta else 1,), (1 if tb else 0,)), ((), ()))

    if epi_outs is None:
        if out_stacked:
            epi_outs = [((N_CHIPS, m, ns), out_dtype, (gn, tm, ns), lambda i, j: (j, i, 0))]
        else:
            epi_outs = [((m, n), out_dtype, (tm, tn), lambda i, j: (i, j))]
    ne, no, nd, nr = len(epi_ins), len(epi_outs), len(deps), len(epi_reds)
    assert not nr or (nn == 1 and not j_outer), name

    def body(a_ref, b_ref, *rest):
        e_refs = rest[:ne]
        o_refs = rest[ne + nd:ne + nd + no]
        r_refs = rest[ne + nd + no:ne + nd + no + nr]
        scratch = rest[ne + nd + no + nr:]
        av = None if a_halves else a_ref[...].astype(BF16)
        if b_stacked and not tb:
            parts = [lax.dot_general(av, b_ref[s].astype(BF16), dims, preferred_element_type=F32) for s in range(gn)]
        elif b_stacked and tb:
            p = None
            for s in range(gk):
                if a_halves:
                    a_s = a_ref[s // 2, :, (s % 2) * ns:(s % 2 + 1) * ns].astype(BF16)
                else:
                    a_s = av[:, s * ns:(s + 1) * ns]
                q = lax.dot_general(a_s, b_ref[s].astype(BF16), dims, preferred_element_type=F32)
                p = q if p is None else p + q
            parts = [p]
        else:
            parts = [lax.dot_general(av, b_ref[...].astype(BF16), dims, preferred_element_type=F32)]

        def finish(acc_parts):
            if epi is None and out_stacked:
                acc = acc_parts[0]
                for s in range(gn):
                    o_refs[0][s] = acc[:, s * ns:(s + 1) * ns].astype(out_dtype)
            elif epi is None:
                w = acc_parts[0].shape[1]
                for s, part in enumerate(acc_parts):
                    o_refs[0][:, s * w:(s + 1) * w] = part.astype(out_dtype)
            else:
                acc = acc_parts[0] if len(acc_parts) == 1 else jnp.concatenate(acc_parts, axis=1)
                vals = epi(acc, *[r[...] for r in e_refs])
                if nr:
                    vals, reds = vals

                    @pl.when(pl.program_id(0) == 0)
                    def _():
                        for r in r_refs:
                            r[...] = jnp.zeros_like(r)
                    for r, val in zip(r_refs, reds):
                        r[...] += val
                for o_ref, val in zip(o_refs, vals):
                    if isinstance(val, tuple) and val[0] == "at":
                        o_ref[:, val[1]:val[1] + val[2].shape[1]] = val[2].astype(o_ref.dtype)
                    else:
                        o_ref[...] = val.astype(o_ref.dtype)

        if nk == 1:
            finish(parts)
        else:
            acc_ref = scratch[0]
            k = pl.program_id(2)
            w = parts[0].shape[1]

            @pl.when(k == 0)
            def _():
                for s, part in enumerate(parts):
                    acc_ref[:, s * w:(s + 1) * w] = part

            @pl.when(k > 0)
            def _():
                for s, part in enumerate(parts):
                    acc_ref[:, s * w:(s + 1) * w] += part

            @pl.when(k == nk - 1)
            def _():
                finish([acc_ref[...]])

    def _ij(f):
        return ij(lambda i, j, k: f(i, j))

    if b_resident:
        assert nk == 1 and (nn == 1 or j_outer), name
        b_spec = pl.BlockSpec(b_spec.block_shape, b_spec.index_map, pipeline_mode=pl.Buffered(1))
    in_specs = [a_spec, b_spec] + [pl.BlockSpec(blk, _ij(f)) for (_, blk, f) in epi_ins] + [ANY_SPEC] * nd
    out_specs = [pl.BlockSpec(blk, _ij(f)) for (_, _, blk, f) in epi_outs]
    out_specs += [pl.BlockSpec((1, w), lambda *_: (0, 0)) for w in epi_reds]
    out_shape = [jax.ShapeDtypeStruct(s, dt) for (s, dt, _, _) in epi_outs] + [jax.ShapeDtypeStruct((1, w), F32) for w in epi_reds]
    scratch = [pltpu.VMEM((tm, tn), F32)] if nk > 1 else []
    grid = (nn, nm, nk) if j_outer else (nm, nn, nk)
    res = _pcall(body, name=name, grid=grid, in_specs=in_specs, out_specs=out_specs, out_shape=out_shape, scratch_shapes=scratch,
                 compiler_params=_params(*(("arbitrary",) * 3 if nr else ("parallel", "parallel", "arbitrary"))))(
                     a, b, *[x for x, _, _ in epi_ins], *deps)
    return res[0] if len(res) == 1 else res


def _norm_fwd(name, x, g, sc, sh, deps=()):
    d = x.shape[1]

    def fn(iv, pv):
        (xv,), (gv, scv, shv) = iv, pv
        r = lax.rsqrt(jnp.mean(xv * xv, axis=-1, keepdims=True) + EPS)
        return [xv * r * gv * (1.0 + scv) + shv], []

    return _rowwise(name, fn, [(x, d, 0)], [g, sc, sh], [(d, BF16)], [], ROWS, deps=deps)[0]


def _adamw(name, w, g, m, v, tm=256, with_g=True):
    c = w.shape[1]

    def fn(iv, pv):
        wv, gv, mv, vv = iv
        mn = ADAM_B1 * mv + (1.0 - ADAM_B1) * gv
        vn = ADAM_B2 * vv + (1.0 - ADAM_B2) * (gv * gv)
        m_hat = mn / (1.0 - ADAM_B1 ** ADAM_STEP)
        v_hat = vn / (1.0 - ADAM_B2 ** ADAM_STEP)
        delta = -ADAM_LR * (m_hat / (jnp.sqrt(v_hat) + ADAM_EPS) + ADAM_WD * wv)
        return ([gv] if with_g else []) + [delta, mn, vn], []

    return _rowwise(name, fn, [(w, c, 0), (g, c, 0), (m, c, 0), (v, c, 0)], [], [(c, F32)] * (4 if with_g else 3), [],
                    _pick(w.shape[0], tm, 8))


def _unpack_plan(shape):
    n = math.prod(shape)
    if len(shape) == 2 and shape[0] == 1 and n % 128 == 0:
        return [((slice(None), slice(128 * r, 128 * (r + 1))), slice(r, r + 1), slice(None)) for r in range(n // 128)]
    if len(shape) == 2 and shape[0] == 1 and n < 128:
        return [((slice(None), slice(None)), slice(0, 1), slice(0, n))]
    if len(shape) == 1 and n % 128 == 0:
        return [((slice(128 * r, 128 * (r + 1)),), r, slice(None)) for r in range(n // 128)]
    if len(shape) == 3 and shape[0] == 1 and shape[2] == 64:
        return [((0, slice(2 * r + h, 2 * r + h + 1), slice(None)), slice(r, r + 1), slice(64 * h, 64 * (h + 1)))
                for r in range(n // 128) for h in range(2)]
    if len(shape) == 4 and shape[0] == 1 and shape[2:] == (128, 128):
        return [((0, k), slice(128 * k, 128 * (k + 1)), slice(None)) for k in range(shape[1])]
    return None


def _adamw_small(name, w, g8, m, v, shapes):
    r = w.shape[0]
    plans, rows0, off = [], [], 0
    for s in shapes:
        plans.append(_unpack_plan(s))
        rows0.append(off // 128)
        off += math.prod(s) + (-math.prod(s)) % 128
    direct = [i for i, p in enumerate(plans) if p is not None]

    def body(w_ref, g_ref, m_ref, v_ref, *out):
        packed = out[4 * len(direct):]
        gv = g_ref[0:r, :]
        for k in range(1, N_DEV):
            gv = gv + g_ref[k * r:(k + 1) * r, :]
        mn = ADAM_B1 * m_ref[...] + (1.0 - ADAM_B1) * gv
        vn = ADAM_B2 * v_ref[...] + (1.0 - ADAM_B2) * (gv * gv)
        m_hat = mn / (1.0 - ADAM_B1 ** ADAM_STEP)
        v_hat = vn / (1.0 - ADAM_B2 ** ADAM_STEP)
        packed[0][...] = gv
        packed[1][...] = -ADAM_LR * (m_hat / (jnp.sqrt(v_hat) + ADAM_EPS) + ADAM_WD * w_ref[...])
        packed[2][...] = mn
        packed[3][...] = vn
        for di, i in enumerate(direct):
            for kind in range(4):
                o_ref, src = out[4 * di + kind], packed[kind]
                for o_idx, row, lanes in plans[i]:
                    row = (rows0[i] + row) if isinstance(row, int) else slice(rows0[i] + row.start, rows0[i] + row.stop)
                    o_ref[o_idx] = src[row, lanes]

    out_shape = [jax.ShapeDtypeStruct(shapes[i], F32) for i in direct for _ in range(4)]
    out_shape += [jax.ShapeDtypeStruct((r, 128), F32)] * 4
    res = _pcall(body, name=name, out_shape=out_shape,
                 compiler_params=pltpu.CompilerParams(vmem_limit_bytes=VMEM_LIMIT))(w, g8, m, v)
    per_param = [None] * len(shapes)
    for di, i in enumerate(direct):
        per_param[i] = res[4 * di:4 * di + 4]
    return per_param, res[4 * len(direct):]


def _pool_fwd(name, uin, pool_w, pool_b, pool_scale, w_up, pw, tm=ROWS):
    t = uin.shape[0]
    tm = min(tm, t)
    ng = len(POOL_WINDOWS)
    gw = pw // ng
    ns = w_up.shape[2]

    def body(u_ref, w_ref, b_ref, s_ref, wu_ref, p_ref, z_ref, y_ref, ext):
        i = pl.program_id(0)

        @pl.when(i == 0)
        def _():
            ext[0:POOL_HALO, :] = jnp.zeros((POOL_HALO, pw), F32)

        u = u_ref[...].astype(F32)
        ext[POOL_HALO:POOL_HALO + tm, :] = u
        pos = i * tm + lax.broadcasted_iota(jnp.int32, (tm, 1), 0)
        for k, win in enumerate(POOL_WINDOWS):
            cols = slice(k * gw, (k + 1) * gw)
            acc = u[:, cols]
            for j in range(1, win):
                acc = acc + ext[POOL_HALO - j:POOL_HALO - j + tm, cols]
            cnt = jnp.minimum(pos + 1, win).astype(F32)
            z = acc / cnt - u[:, cols]
            zp = jnp.dot(z.astype(BF16), w_ref[k].astype(BF16), preferred_element_type=F32) + b_ref[:, cols]
            p_ref[:, cols] = (zp * s_ref[:, cols]).astype(BF16)
            z_ref[:, cols] = z.astype(BF16)
        ext[0:POOL_HALO, :] = u[tm - POOL_HALO:tm, :]
        pv = p_ref[...]
        for s in range(N_CHIPS):
            y_ref[:, s * ns:(s + 1) * ns] = jnp.dot(pv, wu_ref[s], preferred_element_type=F32).astype(BF16)

    full3 = lambda i: (0, 0, 0)
    return _pcall(
        body, name=name, grid=(t // tm,),
        in_specs=[pl.BlockSpec((tm, pw), lambda i: (i, 0)), pl.BlockSpec(pool_w.shape, full3),
                  pl.BlockSpec(pool_b.shape, lambda i: (0, 0)), pl.BlockSpec(pool_scale.shape, lambda i: (0, 0)),
                  pl.BlockSpec(w_up.shape, full3)],
        out_specs=[pl.BlockSpec((tm, pw), lambda i: (i, 0))] * 2 + [pl.BlockSpec((tm, N_CHIPS * ns), lambda i: (i, 0))],
        out_shape=[jax.ShapeDtypeStruct((t, pw), BF16)] * 2 + [jax.ShapeDtypeStruct((t, N_CHIPS * ns), BF16)],
        scratch_shapes=[pltpu.VMEM((POOL_HALO + tm, pw), F32)],
        compiler_params=_params("arbitrary"))(uin, pool_w, pool_b, pool_scale, w_up)


def _pool_bwd(name, dy, z, pool_w, pool_b, pool_scale, w_up, duin, tm=ROWS):
    t, pw = z.shape
    tm = min(tm, t)
    nb = t // tm
    ng = len(POOL_WINDOWS)
    gw = pw // ng
    ns = w_up.shape[2]

    def body(dy_ref, z_ref, w_ref, b_ref, s_ref, wu_ref, duin_ref, du_ref, dw_ref, db_ref, ds_ref, ext):
        i = pl.program_id(0)

        @pl.when(i == 0)
        def _():
            ext[tm:tm + POOL_HALO, :] = jnp.zeros((POOL_HALO, pw), F32)
            dw_ref[...] = jnp.zeros_like(dw_ref)
            db_ref[...] = jnp.zeros_like(db_ref)
            ds_ref[...] = jnp.zeros_like(ds_ref)

        pos = (nb - 1 - i) * tm + lax.broadcasted_iota(jnp.int32, (tm, 1), 0)
        dp = None
        for s in range(N_CHIPS):
            q = lax.dot_general(dy_ref[:, s * ns:(s + 1) * ns], wu_ref[s], (((1,), (1,)), ((), ())), preferred_element_type=F32)
            dp = q if dp is None else dp + q
        for k, win in enumerate(POOL_WINDOWS):
            cols = slice(k * gw, (k + 1) * gw)
            zk = z_ref[:, cols]
            dpk = dp[:, cols]
            wk = w_ref[k].astype(BF16)
            zp = jnp.dot(zk, wk, preferred_element_type=F32) + b_ref[:, cols]
            ds_ref[:, cols] += _colsum(dpk * zp)
            dzp = dpk * s_ref[:, cols]
            db_ref[:, cols] += _colsum(dzp)
            dzpb = dzp.astype(BF16)
            dz = lax.dot_general(dzpb, wk, (((1,), (1,)), ((), ())), preferred_element_type=F32)
            dw_ref[k] += lax.dot_general(zk, dzpb, (((0,), (0,)), ((), ())), preferred_element_type=F32)
            cnt = jnp.minimum(pos + 1, win).astype(F32)
            r = dz / cnt
            ext[0:tm, cols] = r
            acc = r - dz
            for j in range(1, win):
                acc = acc + ext[j:j + tm, cols]
            du_ref[:, cols] = acc.astype(BF16)
        ext[tm:tm + POOL_HALO, :] = ext[0:POOL_HALO, :]

    rev = lambda i: (nb - 1 - i, 0)
    full3 = lambda i: (0, 0, 0)
    return _pcall(
        body, name=name, grid=(nb,),
        in_specs=[pl.BlockSpec((tm, N_CHIPS * ns), rev), pl.BlockSpec((tm, pw), rev), pl.BlockSpec(pool_w.shape, full3),
                  pl.BlockSpec(pool_b.shape, lambda i: (0, 0)), pl.BlockSpec(pool_scale.shape, lambda i: (0, 0)),
                  pl.BlockSpec(w_up.shape, full3), ANY_SPEC],
        out_specs=[pl.BlockSpec((tm, pw), rev), pl.BlockSpec(pool_w.shape, full3),
                   pl.BlockSpec((1, pw), lambda i: (0, 0)), pl.BlockSpec((1, pw), lambda i: (0, 0))],
        out_shape=[jax.ShapeDtypeStruct(duin.shape, BF16), jax.ShapeDtypeStruct(pool_w.shape, F32),
                   jax.ShapeDtypeStruct((1, pw), F32), jax.ShapeDtypeStruct((1, pw), F32)],
        scratch_shapes=[pltpu.VMEM((tm + POOL_HALO, pw), F32)], input_output_aliases={6: 0},
        compiler_params=_params("arbitrary"))(dy, z, pool_w, pool_b, pool_scale, w_up, duin)


def _ssm_disc(lrl, li, ldt):
    lr = -jnp.exp(lrl)
    dt = jnp.exp(ldt)
    mag = jnp.exp(lr * dt)
    ang = li * dt
    ab_re = mag * jnp.cos(ang)
    ab_im = mag * jnp.sin(ang)
    num_re = ab_re - 1.0
    num_im = ab_im
    den = lr * lr + li * li
    f_re = (num_re * lr + num_im * li) / den
    f_im = (num_im * lr - num_re * li) / den
    return lr, dt, mag, ang, ab_re, ab_im, num_re, num_im, den, f_re, f_im


def _ssm_prep(name, lrl, li, ldt, b_re, b_im):
    gn, h = b_re.shape

    def body(lrl_ref, li_ref, ldt_ref, br_ref, bi_ref, lrdt_ref, ang_ref, bbr_ref, bbi_ref):
        lr, dt, _, ang, _, _, _, _, _, f_re, f_im = _ssm_disc(lrl_ref[...], li_ref[...], ldt_ref[...])
        lrdt_ref[...] = lr * dt
        ang_ref[...] = ang
        br, bi = br_ref[...], bi_ref[...]
        bbr_ref[...] = f_re * br - f_im * bi
        bbi_ref[...] = f_re * bi + f_im * br

    col = jax.ShapeDtypeStruct((gn, 1), F32)
    mat = jax.ShapeDtypeStruct((gn, h), F32)
    return _pcall(body, name=name, out_shape=[col, col, mat, mat])(lrl, li, ldt, b_re, b_im)


def _ssm_param_bwd(name, lrl, li, ldt, b_re, b_im, g_abre, g_abim, g_bbre, g_bbim):
    gn, h = b_re.shape

    def body(lrl_ref, li_ref, ldt_ref, br_ref, bi_ref, gar_ref, gai_ref, gbr_ref, gbi_ref,
             dlrl_ref, dli_ref, dldt_ref, dbr_ref, dbi_ref):
        li_v = li_ref[...]
        lr, dt, mag, ang, ab_re, ab_im, num_re, num_im, den, f_re, f_im = _ssm_disc(lrl_ref[...], li_v, ldt_ref[...])
        br, bi = br_ref[...], bi_ref[...]
        gbr, gbi = gbr_ref[...], gbi_ref[...]
        g_fre = jnp.sum(gbr * br + gbi * bi, axis=1, keepdims=True)
        g_fim = jnp.sum(gbi * br - gbr * bi, axis=1, keepdims=True)
        dbr_ref[...] = gbr * f_re + gbi * f_im
        dbi_ref[...] = gbi * f_re - gbr * f_im
        g_num_re = (g_fre * lr - g_fim * li_v) / den
        g_num_im = (g_fre * li_v + g_fim * lr) / den
        g_den = -(g_fre * f_re + g_fim * f_im) / den
        g_lr = (g_fre * num_re + g_fim * num_im) / den + g_den * 2.0 * lr
        g_li = (g_fre * num_im - g_fim * num_re) / den + g_den * 2.0 * li_v
        g_are = gar_ref[...] + g_num_re
        g_aim = gai_ref[...] + g_num_im
        g_mag = g_are * jnp.cos(ang) + g_aim * jnp.sin(ang)
        g_ang = g_aim * ab_re - g_are * ab_im
        g_lrdt = g_mag * mag
        g_lr = g_lr + g_lrdt * dt
        g_dt = g_lrdt * lr + g_ang * li_v
        g_li = g_li + g_ang * dt
        dlrl_ref[...] = g_lr * lr
        dli_ref[...] = g_li
        dldt_ref[...] = g_dt * dt

    col = jax.ShapeDtypeStruct((gn, 1), F32)
    mat = jax.ShapeDtypeStruct((gn, h), F32)
    return _pcall(body, name=name, out_shape=[col, col, col, mat, mat])(lrl, li, ldt, b_re, b_im, g_abre, g_abim, g_bbre, g_bbim)


def _pow_rows(lrdt, ang, k):
    mag = jnp.exp(k * lrdt)
    return mag * jnp.cos(k * ang), mag * jnp.sin(k * ang)


def _ssm_chunk(t):
    return 256 if t >= 2048 else 128


def _to_segments(dst, srcs, jn):
    for q, src in enumerate(srcs):
        for j in range(jn):
            dst[8 * j:8 * j + 8, 128 * q:128 * (q + 1)] = src[pl.ds(j, 8, stride=jn), :]


def _from_segments(dst, src, q, jn, dtype):
    for s in range(8):
        dst[s * jn:(s + 1) * jn, 128 * q:128 * (q + 1)] = src[q, pl.ds(s, jn, stride=8), :].astype(dtype)


def _fill_rows8(dst_re, dst_im, v_re, v_im):
    for j in range(v_re.shape[0]):
        dst_re[8 * j:8 * j + 8, :] = jnp.broadcast_to(v_re[j:j + 1, :], (8, v_re.shape[1]))
        dst_im[8 * j:8 * j + 8, :] = jnp.broadcast_to(v_im[j:j + 1, :], (8, v_im.shape[1]))


def _cmul(ar, ai, br, bi):
    return ar * br - ai * bi, ar * bi + ai * br


def _cmul_conj(ar, ai, br, bi):
    return ar * br + ai * bi, ar * bi - ai * br


def _ssm_fwd(name, uin, lrdt, ang, bb_re, bb_im, cc_re, cc_im, d_skip, sw):
    t = uin.shape[0]
    gn = lrdt.shape[1]
    lc = _ssm_chunk(t)
    jn = lc // 8
    ub, sb = sw // SSM_BLOCKS, gn // SSM_BLOCKS
    nq = sw // 128
    assert ub == 128 and nq == SSM_BLOCKS

    def body(u_ref, lrdt_ref, ang_ref, bbr_ref, bbi_ref, ccr_ref, cci_ref, d_ref, y_ref, ge_ref, sre_ref, sim_ref,
             p_re, p_im, a_re, a_im, up, yp, cst_re, cst_im, car_re, car_im):
        i = pl.program_id(0)
        lrdt_v, ang_v = lrdt_ref[...], ang_ref[...]

        @pl.when(i == 0)
        def _():
            k = (lax.broadcasted_iota(jnp.int32, (jn, 1), 0) + 1).astype(F32)
            _fill_rows8(p_re, p_im, *_pow_rows(lrdt_v, ang_v, k))
            car_re[...] = jnp.zeros_like(car_re)
            car_im[...] = jnp.zeros_like(car_im)

        for q in range(nq):
            yp[q] = u_ref[:, 128 * q:128 * (q + 1)].astype(F32)
        _to_segments(up, [yp.at[q] for q in range(nq)], jn)
        u = up[...]
        ubf = u.astype(BF16)
        for q in range(SSM_BLOCKS):
            uq = ubf[:, q * ub:(q + 1) * ub]
            a_re[:, q * sb:(q + 1) * sb] = jnp.dot(uq, bbr_ref[q], preferred_element_type=F32)
            a_im[:, q * sb:(q + 1) * sb] = jnp.dot(uq, bbi_ref[q], preferred_element_type=F32)
        a1r, a1i = _pow_rows(lrdt_v, ang_v, 1.0)
        ajr, aji = _pow_rows(lrdt_v, ang_v, float(jn))
        for q in range(0, SSM_BLOCKS, 2):
            cols = slice(q * sb, (q + 2) * sb)
            ar8 = jnp.broadcast_to(a1r[:, cols], (8, 2 * sb))
            ai8 = jnp.broadcast_to(a1i[:, cols], (8, 2 * sb))

            def step(j, carry, cols=cols, ar8=ar8, ai8=ai8):
                sr, si = carry
                rows = pl.ds(pl.multiple_of(j * 8, 8), 8)
                mr, mi = _cmul(ar8, ai8, sr, si)
                nr, ni = mr + a_re[rows, cols], mi + a_im[rows, cols]
                a_re[rows, cols] = nr
                a_im[rows, cols] = ni
                return nr, ni

            lax.fori_loop(1, jn, step, (a_re[0:8, cols], a_im[0:8, cols]), unroll=4)
        er, ei = a_re[lc - 8:lc, :], a_im[lc - 8:lc, :]
        gr, gi = car_re[...], car_im[...]
        for s in range(8):
            cst_re[s:s + 1, :] = gr
            cst_im[s:s + 1, :] = gi
            mr, mi = _cmul(ajr, aji, gr, gi)
            gr, gi = mr + er[s:s + 1, :], mi + ei[s:s + 1, :]
        car_re[...] = gr
        car_im[...] = gi
        for q in range(SSM_BLOCKS):
            cols = slice(q * sb, (q + 1) * sb)
            cr = jnp.tile(cst_re[:, cols], (jn, 1))
            ci = jnp.tile(cst_im[:, cols], (jn, 1))
            mr, mi = _cmul(p_re[:, cols], p_im[:, cols], cr, ci)
            srb, sib = (a_re[:, cols] + mr).astype(BF16), (a_im[:, cols] + mi).astype(BF16)
            sre_ref[:, cols] = srb
            sim_ref[:, cols] = sib
            ycols = slice(q * ub, (q + 1) * ub)
            y = (jnp.dot(srb, ccr_ref[q], preferred_element_type=F32) - jnp.dot(sib, cci_ref[q], preferred_element_type=F32)
                 + d_ref[:, ycols] * u[:, ycols])
            yp[q] = y
            _from_segments(y_ref, yp, q, jn, F32)
            yt = y_ref[:, ycols]
            ge_ref[:, ycols] = (0.5 * yt * (1.0 + lax.erf(yt * (1.0 / math.sqrt(2.0))))).astype(BF16)

    row = lambda i: (0, 0)
    blk3 = lambda i: (0, 0, 0)
    return _pcall(
        body, name=name, grid=(t // lc,),
        in_specs=[pl.BlockSpec((lc, sw), lambda i: (i, 1)), pl.BlockSpec((1, gn), row), pl.BlockSpec((1, gn), row),
                  pl.BlockSpec(bb_re.shape, blk3), pl.BlockSpec(bb_im.shape, blk3),
                  pl.BlockSpec(cc_re.shape, blk3), pl.BlockSpec(cc_im.shape, blk3), pl.BlockSpec((1, sw), row)],
        out_specs=[pl.BlockSpec((lc, sw), lambda i: (i, 0)), pl.BlockSpec((lc, sw), lambda i: (i, 0)),
                   pl.BlockSpec((lc, gn), lambda i: (i, 0)), pl.BlockSpec((lc, gn), lambda i: (i, 0))],
        out_shape=[jax.ShapeDtypeStruct((t, sw), F32), jax.ShapeDtypeStruct((t, sw), BF16),
                   jax.ShapeDtypeStruct((t, gn), BF16), jax.ShapeDtypeStruct((t, gn), BF16)],
        scratch_shapes=[pltpu.VMEM((lc, gn), F32), pltpu.VMEM((lc, gn), F32), pltpu.VMEM((lc, gn), F32), pltpu.VMEM((lc, gn), F32),
                        pltpu.VMEM((lc, sw), F32), pltpu.VMEM((nq, lc, 128), F32),
                        pltpu.VMEM((8, gn), F32), pltpu.VMEM((8, gn), F32), pltpu.VMEM((1, gn), F32), pltpu.VMEM((1, gn), F32)],
        compiler_params=_params("arbitrary"))(uin, lrdt, ang, bb_re, bb_im, cc_re, cc_im, d_skip)


def _ssm_bwd(name, dge, y, uin, s_re, s_im, lrdt, ang, bbt_re, bbt_im, cct_re, cct_im, d_skip, sw, duin):
    t = uin.shape[0]
    gn = lrdt.shape[1]
    lc = _ssm_chunk(t)
    nb = t // lc
    jn = lc // 8
    ub, sb = sw // SSM_BLOCKS, gn // SSM_BLOCKS
    nq = sw // 128
    tail = 16

    def body(dge_ref, y_ref, u_ref, sre_ref, sim_ref, tre_ref, tim_ref, lrdt_ref, ang_ref, btr_ref, bti_ref, ctr_ref, cti_ref,
             d_ref, duin_ref, du_ref, dar_ref, dai_ref, dbr_ref, dbi_ref, dcr_ref, dci_ref, dd_ref,
             q_re, q_im, a_re, a_im, dyp, up, dys, us, dup, cst_re, cst_im, sp_re, sp_im, car_re, car_im):
        i = pl.program_id(0)
        lrdt_v, ang_v = lrdt_ref[...], ang_ref[...]

        @pl.when(i == 0)
        def _():
            k = (jn - lax.broadcasted_iota(jnp.int32, (jn, 1), 0)).astype(F32)
            _fill_rows8(q_re, q_im, *_pow_rows(lrdt_v, ang_v, k))
            car_re[...] = jnp.zeros_like(car_re)
            car_im[...] = jnp.zeros_like(car_im)
            for r in (dar_ref, dai_ref, dbr_ref, dbi_ref, dcr_ref, dci_ref, dd_ref):
                r[...] = jnp.zeros_like(r)

        yv = y_ref[...]
        ut = u_ref[...].astype(F32)
        cdf =0.5 * (1.0 + lax.erf(yv * (1.0 / math.sqrt(2.0))))
        pdf = jnp.exp(-0.5 * yv * yv) * (1.0 / math.sqrt(2.0 * math.pi))
        dyt = dge_ref[...].astype(F32) * (cdf + yv * pdf)
        dd_ref[...] += _colsum(dyt * ut)
        for q in range(nq):
            dys[q] = dyt[:, 128 * q:128 * (q + 1)]
            us[q] = ut[:, 128 * q:128 * (q + 1)]
        _to_segments(dyp, [dys.at[q] for q in range(nq)], jn)
        _to_segments(up, [us.at[q] for q in range(nq)], jn)
        dy = dyp[...]
        u = up[...]
        dyb = dy.astype(BF16)
        ubf = u.astype(BF16)
        for q in range(SSM_BLOCKS):
            dq = dyb[:, q * ub:(q + 1) * ub]
            a_re[:, q * sb:(q + 1) * sb] = jnp.dot(dq, ctr_ref[q], preferred_element_type=F32)
            a_im[:, q * sb:(q + 1) * sb] = -jnp.dot(dq, cti_ref[q], preferred_element_type=F32)
        a1r, a1i = _pow_rows(lrdt_v, ang_v, 1.0)
        ajr, aji = _pow_rows(lrdt_v, ang_v, float(jn))
        for q in range(0, SSM_BLOCKS, 2):
            cols = slice(q * sb, (q + 2) * sb)
            ar8 = jnp.broadcast_to(a1r[:, cols], (8, 2 * sb))
            ai8 = jnp.broadcast_to(a1i[:, cols], (8, 2 * sb))

            def step(jj, carry, cols=cols, ar8=ar8, ai8=ai8):
                sr, si = carry
                rows = pl.ds(pl.multiple_of((jn - 2 - jj) * 8, 8), 8)
                mr, mi = _cmul_conj(ar8, ai8, sr, si)
                nr, ni = mr + a_re[rows, cols], mi + a_im[rows, cols]
                a_re[rows, cols] = nr
                a_im[rows, cols] = ni
                return nr, ni

            lax.fori_loop(0, jn - 1, step, (a_re[lc - 8:lc, cols], a_im[lc - 8:lc, cols]), unroll=4)
        er, ei = a_re[0:8, :], a_im[0:8, :]
        hr, hi = car_re[...], car_im[...]
        for s in range(7, -1, -1):
            cst_re[s:s + 1, :] = hr
            cst_im[s:s + 1, :] = hi
            mr, mi = _cmul_conj(ajr, aji, hr, hi)
            hr, hi = mr + er[s:s + 1, :], mi + ei[s:s + 1, :]
        car_re[...] = hr
        car_im[...] = hi
        first = (i == nb - 1).astype(F32)
        sp_re[0:tail, :] = tre_ref[...].astype(F32) * (1.0 - first)
        sp_im[0:tail, :] = tim_ref[...].astype(F32) * (1.0 - first)
        sp_re[tail:tail + 8, :] = sre_ref[lc - tail:lc, :].astype(F32)[tail - 8:tail]
        sp_im[tail:tail + 8, :] = sim_ref[lc - tail:lc, :].astype(F32)[tail - 8:tail]
        tn_dims = (((0,), (0,)), ((), ()))
        for q in range(SSM_BLOCKS):
            cols = slice(q * sb, (q + 1) * sb)
            ycols = slice(q * ub, (q + 1) * ub)
            cr = jnp.tile(cst_re[:, cols], (jn, 1))
            ci = jnp.tile(cst_im[:, cols], (jn, 1))
            mr, mi = _cmul_conj(q_re[:, cols], q_im[:, cols], cr, ci)
            lam_r, lam_i = a_re[:, cols] + mr, a_im[:, cols] + mi
            s_r, s_i = sre_ref[:, cols], sim_ref[:, cols]
            p0r, p0i = sp_re[tail - 1:tail + 7, cols], sp_im[tail - 1:tail + 7, cols]
            l0r, l0i, l1r, l1i = lam_r[0:8], lam_i[0:8], lam_r[8:lc], lam_i[8:lc]
            pvr, pvi = s_r.astype(F32)[0:lc - 8], s_i.astype(F32)[0:lc - 8]
            dar_ref[:, cols] += _colsum(l1r * pvr + l1i * pvi) + _colsum(l0r * p0r + l0i * p0i)
            dai_ref[:, cols] += _colsum(l1i * pvr - l1r * pvi) + _colsum(l0i * p0r - l0r * p0i)
            lrb, lib = lam_r.astype(BF16), lam_i.astype(BF16)
            dup[q] = (jnp.dot(lrb, btr_ref[q], preferred_element_type=F32)
                      + jnp.dot(lib, bti_ref[q], preferred_element_type=F32) + d_ref[:, ycols] * dy[:, ycols])
            _from_segments(du_ref, dup, q, jn, BF16)
            uq = ubf[:, ycols]
            dbr_ref[q] += lax.dot_general(uq, lrb, tn_dims, preferred_element_type=F32)
            dbi_ref[q] += lax.dot_general(uq, lib, tn_dims, preferred_element_type=F32)
            dq = dyb[:, ycols]
            dcr_ref[q] += lax.dot_general(s_r.astype(BF16), dq, tn_dims, preferred_element_type=F32)
            dci_ref[q] -= lax.dot_general(s_i.astype(BF16), dq, tn_dims, preferred_element_type=F32)

    rev = lambda i: (nb - 1 - i, 0)
    tailmap = lambda i: (jnp.maximum((nb - 1 - i) * (lc // tail) - 1, 0), 0)
    row = lambda i: (0, 0)
    blk3 = lambda i: (0, 0, 0)
    return _pcall(
        body, name=name, grid=(nb,),
        in_specs=[pl.BlockSpec((lc, sw), rev), pl.BlockSpec((lc, sw), rev), pl.BlockSpec((lc, sw), lambda i: (nb - 1 - i, 1)),
                  pl.BlockSpec((lc, gn), rev), pl.BlockSpec((lc, gn), rev),
                  pl.BlockSpec((tail, gn), tailmap), pl.BlockSpec((tail, gn), tailmap),
                  pl.BlockSpec((1, gn), row), pl.BlockSpec((1, gn), row),
                  pl.BlockSpec(bbt_re.shape, blk3), pl.BlockSpec(bbt_im.shape, blk3),
                  pl.BlockSpec(cct_re.shape, blk3), pl.BlockSpec(cct_im.shape, blk3), pl.BlockSpec((1, sw), row), ANY_SPEC],
        out_specs=[pl.BlockSpec((lc, sw), lambda i: (nb - 1 - i, 1)), pl.BlockSpec((1, gn), row), pl.BlockSpec((1, gn), row),
                   pl.BlockSpec((SSM_BLOCKS, ub, sb), blk3), pl.BlockSpec((SSM_BLOCKS, ub, sb), blk3),
                   pl.BlockSpec((SSM_BLOCKS, sb, ub), blk3), pl.BlockSpec((SSM_BLOCKS, sb, ub), blk3),
                   pl.BlockSpec((1, sw), row)],
        out_shape=[jax.ShapeDtypeStruct(duin.shape, BF16), jax.ShapeDtypeStruct((1, gn), F32), jax.ShapeDtypeStruct((1, gn), F32),
                   jax.ShapeDtypeStruct((SSM_BLOCKS, ub, sb), F32), jax.ShapeDtypeStruct((SSM_BLOCKS, ub, sb), F32),
                   jax.ShapeDtypeStruct((SSM_BLOCKS, sb, ub), F32), jax.ShapeDtypeStruct((SSM_BLOCKS, sb, ub), F32),
                   jax.ShapeDtypeStruct((1, sw), F32)],
        scratch_shapes=[pltpu.VMEM((lc, gn), F32), pltpu.VMEM((lc, gn), F32), pltpu.VMEM((lc, gn), F32), pltpu.VMEM((lc, gn), F32),
                        pltpu.VMEM((lc, sw), F32), pltpu.VMEM((lc, sw), F32),
                        pltpu.VMEM((nq, lc, 128), F32), pltpu.VMEM((nq, lc, 128), F32), pltpu.VMEM((nq, lc, 128), F32),
                        pltpu.VMEM((8, gn), F32), pltpu.VMEM((8, gn), F32),
                        pltpu.VMEM((tail + 8, gn), F32), pltpu.VMEM((tail + 8, gn), F32),
                        pltpu.VMEM((1, gn), F32), pltpu.VMEM((1, gn), F32)],
        input_output_aliases={14: 0},
        compiler_params=_params("arbitrary"))(dge, y, uin, s_re, s_im, s_re, s_im, lrdt, ang,
                                               bbt_re, bbt_im, cct_re, cct_im, d_skip, duin)


def _blockdiag_b(bb, sw):
    gpb = (sw // SSM_GROUP) // SSM_BLOCKS
    b4 = bb.reshape(SSM_BLOCKS, gpb, SSM_STATE, SSM_GROUP)
    eye = jnp.eye(gpb, dtype=bb.dtype)
    out = jnp.einsum('qgnh,gk->qghkn', b4, eye)
    return out.reshape(SSM_BLOCKS, gpb * SSM_GROUP, gpb * SSM_STATE)


def _blockdiag_c(cc, sw):
    gpb = (sw // SSM_GROUP) // SSM_BLOCKS
    c4 = cc.reshape(SSM_BLOCKS, gpb, SSM_GROUP, SSM_STATE)
    eye = jnp.eye(gpb, dtype=cc.dtype)
    out = jnp.einsum('qghn,gk->qgnkh', c4, eye)
    return out.reshape(SSM_BLOCKS, gpb * SSM_STATE, gpb * SSM_GROUP)


def _diag_of_b(dbb, sw):
    gpb = (sw // SSM_GROUP) // SSM_BLOCKS
    d5 = dbb.reshape(SSM_BLOCKS, gpb, SSM_GROUP, gpb, SSM_STATE)
    return jnp.einsum('qghgn->qgnh', d5).reshape(SSM_BLOCKS * gpb * SSM_STATE, SSM_GROUP)


def _diag_of_c(dcc, sw):
    gpb = (sw // SSM_GROUP) // SSM_BLOCKS
    d5 = dcc.reshape(SSM_BLOCKS, gpb, SSM_STATE, gpb, SSM_GROUP)
    return jnp.einsum('qgngh->qghn', d5).reshape(SSM_BLOCKS * gpb, SSM_GROUP, SSM_STATE)


def _ada_fwd(name, c_all, w_sh, b_sh):
    nb, d = c_all.shape
    ncol = w_sh.shape[1]
    tn = _pick(ncol, 768)

    def body(c_ref, w_ref, b_ref, o_ref):
        cv = c_ref[...]
        sil = cv * _sigmoid(cv)
        o_ref[...] = jnp.dot(sil, w_ref[...], preferred_element_type=F32, precision=lax.Precision.HIGHEST) + b_ref[...]

    return _pcall(body, name=name, grid=(ncol // tn,),
                  in_specs=[pl.BlockSpec((nb, d), lambda j: (0, 0)), pl.BlockSpec((d, tn), lambda j: (0, j)),
                            pl.BlockSpec((1, tn), lambda j: (0, j))],
                  out_specs=pl.BlockSpec((nb, tn), lambda j: (0, j)),
                  out_shape=jax.ShapeDtypeStruct((nb, ncol), F32), compiler_params=_params("parallel"))(c_all, w_sh, b_sh)


def _ada_bwd(name, c_all, dmod_sh):
    nb, d = c_all.shape
    ncol = dmod_sh.shape[1]
    tn = _pick(ncol, 768)

    def body(c_ref, g_ref, o_ref):
        cv = c_ref[...]
        sil = cv * _sigmoid(cv)
        o_ref[...] = lax.dot_general(sil, g_ref[...], (((0,), (0,)), ((), ())), preferred_element_type=F32,
                                     precision=lax.Precision.HIGHEST)

    return _pcall(body, name=name, grid=(ncol // tn,),
                  in_specs=[pl.BlockSpec((nb, d), lambda j: (0, 0)), pl.BlockSpec((nb, tn), lambda j: (0, j))],
                  out_specs=pl.BlockSpec((d, tn), lambda j: (0, j)),
                  out_shape=jax.ShapeDtypeStruct((d, ncol), F32), compiler_params=_params("parallel"))(c_all, dmod_sh)


def _place():
    return lax.axis_index("x"), lax.axis_index("y"), lax.axis_index("c")


def _allgather_small(name, blk, deps=()):
    m_per, n = blk.shape

    def body(x_ref, *rest):
        out_ref, send_sems, recv_sems, local_sem = rest[len(deps):]
        x, y, c = _place()
        me, sibling = (x, y, c), (x, y, 1 - c)
        chips = [(1 - x, y), (x, 1 - y), (1 - x, 1 - y)]

        def rows(px, py, pc):
            return out_ref.at[pl.ds((4 * px + 2 * py + pc) * m_per, m_per), :]

        def copy(k, block, to, src=None):
            return pltpu.make_async_remote_copy(
                src_ref=rows(*block) if src is None else src, dst_ref=rows(*block),
                send_sem=send_sems.at[k], recv_sem=recv_sems.at[k], device_id=to, device_id_type=MESH)

        mine = pltpu.make_async_copy(x_ref, rows(*me), local_sem)
        mine.start()
        first = [copy(0, me, sibling, src=x_ref)]
        first += [copy(1 + j, me, (*chip, c), src=x_ref) for j, chip in enumerate(chips)]
        for cp in first:
            cp.start()
        passed = [copy(4 + j, (*chip, c), sibling) for j, chip in enumerate(chips)]
        for j, chip in enumerate(chips):
            copy(1 + j, (*chip, c), me).wait_recv()
            passed[j].start()
        copy(0, sibling, me).wait_recv()
        for j, chip in enumerate(chips):
            copy(4 + j, (*chip, 1 - c), me).wait_recv()
        for cp in first + passed:
            cp.wait_send()
        mine.wait()

    return _pcall(body, name=name, out_shape=jax.ShapeDtypeStruct((N_DEV * m_per, n), blk.dtype),
                  in_specs=[pl.BlockSpec(memory_space=pltpu.VMEM)] + [ANY_SPEC] * len(deps),
                  out_specs=pl.BlockSpec(memory_space=pltpu.VMEM),
                  scratch_shapes=[pltpu.SemaphoreType.DMA((7,)), pltpu.SemaphoreType.DMA((7,)), pltpu.SemaphoreType.DMA],
                  compiler_params=pltpu.CompilerParams(vmem_limit_bytes=VMEM_LIMIT))(blk, *deps)


def _other_chips(x, y):
    return [(1 - x, y), (x, 1 - y), (1 - x, 1 - y)]


def _place_shards(shards, s_me):
    return [lax.dynamic_update_slice(lax.empty((N_CHIPS,) + s.shape, s.dtype), s[None], (s_me, 0, 0)) for s in shards]


def _ag_copy(src, land, send, recv, wi, j, chip, x, y, c, tc, both):
    hr = src.shape[0] // 2
    half = pl.ds(pl.multiple_of(c * hr, 16), hr)
    k = 3 * wi + j
    return pltpu.make_async_remote_copy(
        src_ref=src.at[half, :], dst_ref=land.at[2 * x + y, half, :],
        send_sem=send.at[2 * k + tc if both else k], recv_sem=recv.at[2 * k + c if both else k],
        device_id=(chip[0], chip[1], tc), device_id_type=MESH)


def _ag_targets(c, both):
    return (0, 1) if both else (c,)


def _ag_start(name, shards, lands, groups, direct, deps=()):
    nw, ng, nd = len(shards), len(groups), len(deps)

    def body(*refs):
        src, land = refs[:nw], refs[nw:2 * nw]
        sems = refs[2 * nw + nd:2 * nw + nd + 2 * ng]
        token = refs[-1]
        x, y, c = _place()
        for gi, grp in enumerate(groups):
            for wi, w in enumerate(grp):
                for j, chip in enumerate(_other_chips(x, y)):
                    for tc in _ag_targets(c, direct[gi]):
                        _ag_copy(src[w], land[w], sems[2 * gi], sems[2 * gi + 1], wi, j, chip, x, y, c, tc, direct[gi]).start()
        token[...] = jnp.zeros_like(token)

    sem_shapes = []
    for gi, grp in enumerate(groups):
        sem_shapes += [pltpu.SemaphoreType.DMA(((6 if direct[gi] else 3) * len(grp),))] * 2
    out_shape = sem_shapes + [pltpu.HBM(s.shape, s.dtype) for s in shards] + [pltpu.HBM(l.shape, l.dtype) for l in lands]
    out_shape += [jax.ShapeDtypeStruct((8, 128), F32)]
    res = _pcall(body, name=name, out_shape=out_shape, in_specs=[HBM_SPEC] * (2 * nw) + [ANY_SPEC] * nd,
                 out_specs=[SEM_SPEC] * (2 * ng) + [HBM_SPEC] * (2 * nw) + [pl.BlockSpec(memory_space=pltpu.VMEM)],
                 input_output_aliases={i: 2 * ng + i for i in range(2 * nw)},
                 compiler_params=pltpu.CompilerParams(has_side_effects=EFFECT))(
                     *[_hbm(s) for s in shards], *[_hbm(l) for l in lands], *deps)
    sems = [(res[2 * gi], res[2 * gi + 1]) for gi in range(ng)]
    return sems, list(res[2 * ng:2 * ng + nw]), list(res[2 * ng + nw:2 * ng + 2 * nw]), res[-1]


def _ag_wait(name, shards, lands, send, recv, both, after):
    n = len(shards)

    def body(*refs):
        src, land = refs[:n], refs[n:2 * n]
        send_sem, recv_sem = refs[2 * n], refs[2 * n + 1]
        x, y, c = _place()
        for wi in range(n):
            for j, chip in enumerate(_other_chips(x, y)):
                for tc in _ag_targets(c, both):
                    _ag_copy(src[wi], land[wi], send_sem, recv_sem, wi, j, chip, x, y, c, tc, both).wait_send()
                    _ag_copy(src[wi], land[wi], send_sem, recv_sem, wi, j, chip, chip[0], chip[1], tc, c, both).wait_recv()

    res = _pcall(body, name=name, out_shape=[pltpu.HBM(a.shape, a.dtype) for a in list(shards) + list(lands)],
                 in_specs=[HBM_SPEC] * (2 * n) + [SEM_SPEC, SEM_SPEC] + [ANY_SPEC] * len(after), out_specs=[HBM_SPEC] * (2 * n),
                 input_output_aliases={i: i for i in range(2 * n)},
                 compiler_params=pltpu.CompilerParams(has_side_effects=EFFECT))(*shards, *lands, send, recv, *after)
    return list(res[n:])


def _ag_forward(name, lands):
    n = len(lands)

    def body(*refs):
        out = refs[n:2 * n]
        send, recv = refs[2 * n], refs[2 * n + 1]
        x, y, c = _place()
        sib = (x, y, 1 - c)
        cps = []
        for wi in range(n):
            hr = out[wi].shape[1] // 2
            for j, (cx, cy) in enumerate(_other_chips(x, y)):
                got = out[wi].at[2 * cx + cy, pl.ds(pl.multiple_of(c * hr, 16), hr), :]
                cp = pltpu.make_async_remote_copy(src_ref=got, dst_ref=got, send_sem=send.at[3 * wi + j], recv_sem=recv.at[3 * wi + j],
                                                  device_id=sib, device_id_type=MESH)
                cp.start()
                cps.append(cp)
        for wi in range(n):
            hr = out[wi].shape[1] // 2
            for j, (cx, cy) in enumerate(_other_chips(x, y)):
                got = out[wi].at[2 * cx + cy, pl.ds(pl.multiple_of((1 - c) * hr, 16), hr), :]
                pltpu.make_async_remote_copy(src_ref=got, dst_ref=got, send_sem=send.at[3 * wi + j], recv_sem=recv.at[3 * wi + j],
                                             device_id=sib, device_id_type=MESH).wait_recv()
        for cp in cps:
            cp.wait_send()

    res = _pcall(body, name=name, out_shape=[jax.ShapeDtypeStruct(l.shape, l.dtype) for l in lands],
                 in_specs=[ANY_SPEC] * n, out_specs=[ANY_SPEC] * n, input_output_aliases={i: i for i in range(n)},
                 scratch_shapes=[pltpu.SemaphoreType.DMA((3 * n,)), pltpu.SemaphoreType.DMA((3 * n,))])(*lands)
    return list(res)


def _peers(x, y, c):
    offs = [(dx, dy, dc) for dx in (0, 1) for dy in (0, 1) for dc in (0, 1)][1:]
    return [(1 - x if dx else x, 1 - y if dy else y, 1 - c if dc else c) for dx, dy, dc in offs]


def _rs_copy(g_ref, land_ref, send, recv, wi, k, to, sender):
    hr = g_ref.shape[1] // 2
    return pltpu.make_async_remote_copy(
        src_ref=g_ref.at[2 * to[0] + to[1], pl.ds(pl.multiple_of(to[2] * hr, 16), hr), :], dst_ref=land_ref.at[sender],
        send_sem=send.at[7 * wi + k], recv_sem=recv.at[7 * wi + k], device_id=to, device_id_type=MESH)


def _rs_start(name, gs):
    n = len(gs)
    lands = [lax.empty((N_DEV, g.shape[1] // 2, g.shape[2]), BF16) for g in gs]

    def body(*refs):
        g, land = refs[:n], refs[n:2 * n]
        send, recv = refs[2 * n], refs[2 * n + 1]
        token = refs[-1]
        x, y, c = _place()
        me = 4 * x + 2 * y + c
        for wi in range(n):
            for k, to in enumerate(_peers(x, y, c)):
                _rs_copy(g[wi], land[wi], send, recv, wi, k, to, me).start()
        token[...] = jnp.zeros_like(token)

    out_shape = [pltpu.SemaphoreType.DMA((7 * n,))] * 2 + [pltpu.HBM(a.shape, a.dtype) for a in list(gs) + lands]
    out_shape += [jax.ShapeDtypeStruct((8, 128), F32)]
    res = _pcall(body, name=name, out_shape=out_shape, in_specs=[HBM_SPEC] * (2 * n),
                 out_specs=[SEM_SPEC] * 2 + [HBM_SPEC] * (2 * n) + [pl.BlockSpec(memory_space=pltpu.VMEM)],
                 input_output_aliases={i: 2 + i for i in range(2 * n)},
                 compiler_params=pltpu.CompilerParams(has_side_effects=EFFECT))(
                     *[_hbm(a) for a in gs], *[_hbm(a) for a in lands])
    return res[0], res[1], list(res[2:2 + n]), list(res[2 + n:2 + 2 * n]), res[-1]


def _rs_wait(name, gs, lands, send, recv, after):
    n = len(gs)

    def body(*refs):
        g, land = refs[:n], refs[n:2 * n]
        send_sem, recv_sem = refs[2 * n], refs[2 * n + 1]
        x, y, c = _place()
        me = 4 * x + 2 * y + c
        for wi in range(n):
            for k, to in enumerate(_peers(x, y, c)):
                _rs_copy(g[wi], land[wi], send_sem, recv_sem, wi, k, to, me).wait_send()
                _rs_copy(g[wi], land[wi], send_sem, recv_sem, wi, k, (x, y, c), 4 * to[0] + 2 * to[1] + to[2]).wait_recv()

    res = _pcall(body, name=name, out_shape=[pltpu.HBM(a.shape, a.dtype) for a in list(gs) + list(lands)],
                 in_specs=[HBM_SPEC] * (2 * n) + [SEM_SPEC, SEM_SPEC, ANY_SPEC], out_specs=[HBM_SPEC] * (2 * n),
                 input_output_aliases={i: i for i in range(2 * n)},
                 compiler_params=pltpu.CompilerParams(has_side_effects=EFFECT))(*gs, *lands, send, recv, after)
    return list(res[:n]), list(res[n:])


def _bc_copy(blk_ref, land_ref, send, recv, k, to, slot):
    return pltpu.make_async_remote_copy(src_ref=blk_ref, dst_ref=land_ref.at[slot], send_sem=send.at[k], recv_sem=recv.at[k],
                                        device_id=to, device_id_type=MESH)


def _bcast_start(name, blk):
    land = lax.empty((N_DEV,) + blk.shape, blk.dtype)

    def body(blk_ref, land_ref, send, recv, blk_thru, land_thru, token):
        x, y, c = _place()
        for k, to in enumerate(_peers(x, y, c)):
            _bc_copy(blk_ref, land_ref, send, recv, k, to, 4 * x + 2 * y + c).start()
        token[...] = jnp.zeros_like(token)

    return _pcall(body, name=name,
                  out_shape=[pltpu.SemaphoreType.DMA((7,)), pltpu.SemaphoreType.DMA((7,)), pltpu.HBM(blk.shape, blk.dtype),
                             pltpu.HBM(land.shape, land.dtype), jax.ShapeDtypeStruct((8, 128), F32)],
                  in_specs=[HBM_SPEC, HBM_SPEC], out_specs=[SEM_SPEC, SEM_SPEC, HBM_SPEC, HBM_SPEC, pl.BlockSpec(memory_space=pltpu.VMEM)],
                  input_output_aliases={0: 2, 1: 3}, compiler_params=pltpu.CompilerParams(has_side_effects=EFFECT))(_hbm(blk), _hbm(land))


def _bcast_wait(name, blk, land, send, recv, after):
    def body(blk_ref, land_ref, send_sem, recv_sem, after_ref, blk_thru, land_thru):
        x, y, c = _place()
        for k, to in enumerate(_peers(x, y, c)):
            _bc_copy(blk_ref, land_ref, send_sem, recv_sem, k, to, 4 * x + 2 * y + c).wait_send()
            _bc_copy(blk_ref, land_ref, send_sem, recv_sem, k, to, 4 * to[0] + 2 * to[1] + to[2]).wait_recv()

    return _pcall(body, name=name, out_shape=[pltpu.HBM(blk.shape, blk.dtype), pltpu.HBM(land.shape, land.dtype)],
                  in_specs=[HBM_SPEC, HBM_SPEC, SEM_SPEC, SEM_SPEC, ANY_SPEC], out_specs=[HBM_SPEC, HBM_SPEC],
                  input_output_aliases={0: 0, 1: 1}, compiler_params=pltpu.CompilerParams(has_side_effects=EFFECT))(
                      blk, land, send, recv, after)[1]


def _rs_sum(name, gs, lands):
    n = len(gs)

    def body(*refs):
        g_refs, land_refs, out_refs = refs[:n], refs[n:2 * n], refs[2 * n:3 * n]
        recvs = refs[3 * n:4 * n]
        local_sems, sib_send, sib_recv = refs[4 * n:]
        x, y, c = _place()
        me = 4 * x + 2 * y + c
        cps = []
        for wi in range(n):
            hr = g_refs[wi].shape[1] // 2
            own = g_refs[wi].at[2 * x + y, pl.ds(pl.multiple_of(c * hr, 16), hr), :]
            cps.append(pltpu.make_async_copy(own, recvs[wi].at[me], local_sems.at[8 * wi + 7]))
            for k, (tx, ty, tc) in enumerate(_peers(x, y, c)):
                slot = 4 * tx + 2 * ty + tc
                cps.append(pltpu.make_async_copy(land_refs[wi].at[slot], recvs[wi].at[slot], local_sems.at[8 * wi + k]))
        for cp in cps:
            cp.start()
        sibs = []
        for wi in range(n):
            hr = g_refs[wi].shape[1] // 2
            ch = _pick(hr, 64, 16)
            for cp in cps[8 * wi:8 * wi + 8]:
                cp.wait()
            base = pl.multiple_of(c * hr, 16)
            for r0 in range(0, hr, ch):
                acc = recvs[wi][0, r0:r0 + ch, :].astype(F32)
                for k in range(1, N_DEV):
                    acc = acc + recvs[wi][k, r0:r0 + ch, :].astype(F32)
                out_refs[wi][pl.ds(base + r0, ch), :] = acc
            half = out_refs[wi].at[pl.ds(base, hr), :]
            sib = pltpu.make_async_remote_copy(src_ref=half, dst_ref=half, send_sem=sib_send.at[wi], recv_sem=sib_recv.at[wi],
                                               device_id=(x, y, 1 - c), device_id_type=MESH)
            sib.start()
            sibs.append(sib)
        for wi in range(n):
            hr = g_refs[wi].shape[1] // 2
            other = out_refs[wi].at[pl.ds(pl.multiple_of((1 - c) * hr, 16), hr), :]
            pltpu.make_async_remote_copy(src_ref=other, dst_ref=other, send_sem=sib_send.at[wi], recv_sem=sib_recv.at[wi],
                                         device_id=(x, y, 1 - c), device_id_type=MESH).wait_recv()
            sibs[wi].wait_send()

    res = _pcall(
        body, name=name, out_shape=[jax.ShapeDtypeStruct(g.shape[1:], F32) for g in gs],
        in_specs=[ANY_SPEC] * (2 * n), out_specs=[pl.BlockSpec(memory_space=pltpu.VMEM)] * n,
        scratch_shapes=[pltpu.VMEM((N_DEV, g.shape[1] // 2, g.shape[2]), BF16) for g in gs]
        + [pltpu.SemaphoreType.DMA((8 * n,)), pltpu.SemaphoreType.DMA((n,)), pltpu.SemaphoreType.DMA((n,))],
        compiler_params=pltpu.CompilerParams(vmem_limit_bytes=VMEM_LIMIT))(*gs, *lands)
    return list(res)


def _gate_norm_epilogue(factor, x_in, gt, nxt, t, d):
    blk = (ROWS, d)
    full = lambda i, j: (i, j)
    rowv = lambda i, j: (0, j)

    def epi(acc, res, scale, *norm):
        x_new = res + (factor * scale) * acc
        if not norm:
            return x_new, acc
        gv, scv, shv = norm
        r = lax.rsqrt(jnp.mean(x_new * x_new, axis=-1, keepdims=True) + EPS)
        return x_new, acc, x_new * r * gv * (1.0 + scv) + shv

    ins = [(x_in, blk, full), (gt, (1, d), rowv)] + [(v, (1, d), rowv) for v in (nxt or ())]
    outs = [((t, d), F32, blk, full), ((t, d), BF16, blk, full)] + ([((t, d), BF16, blk, full)] if nxt else [])
    return epi, ins, outs


_FULL = lambda i, j: (i, j)
_ROWV = lambda i, j: (0, j)


def _glu_epilogue(b_glu, t, w):
    def epi(acc, bv):
        g = acc + bv
        return acc, g[:, :w] * _sigmoid(g[:, w:])

    return epi, [(b_glu, (1, 2 * w), _ROWV)], [((t, 2 * w), BF16, (ROWS, 2 * w), _FULL), ((t, w), BF16, (ROWS, w), _FULL)]


def _glu_bwd_epilogue(gv, b_glu, t, w):
    def epi(dsg, gvv, bv):
        g = gvv.astype(F32) + bv
        val, s = g[:, :w], _sigmoid(g[:, w:])
        dval, dgate = dsg * s, dsg * val * s * (1.0 - s)
        return [jnp.concatenate([dval, dgate], axis=1)], [jnp.concatenate([_colsum(dval), _colsum(dgate)], axis=1)]

    return epi, [(gv, (ROWS, 2 * w), _FULL), (b_glu, (1, 2 * w), _ROWV)], [((t, 2 * w), BF16, (ROWS, 2 * w), _FULL)], [2 * w]


def _gates_epilogue(y_pool, uin, t, d, pw):
    cb = (2 * pw) // d

    def epi(acc, yp, glp, gls):
        return acc, _sigmoid(glp.astype(F32)) * yp.astype(F32) + _sigmoid(gls.astype(F32)) * acc

    ins = [(y_pool, (ROWS, d), _FULL), (uin, (ROWS, d), lambda i, j: (i, cb)), (uin, (ROWS, d), lambda i, j: (i, cb + 1))]
    return epi, ins, [((t, d), BF16, (ROWS, d), _FULL)] * 2


def _gates_bwd_epilogue(y_pool, y_ssm, uin, t, d, pw):
    cb = (2 * pw) // d

    def epi(dm, yp, ys, glp, gls):
        sp, ss = _sigmoid(glp.astype(F32)), _sigmoid(gls.astype(F32))
        dgl = jnp.concatenate([dm * yp.astype(F32) * sp * (1.0 - sp), dm * ys.astype(F32) * ss * (1.0 - ss)], axis=1)
        return dm * sp, dm * ss, ("at", 2 * pw, dgl)

    ins = [(y_pool, (ROWS, d), _FULL), (y_ssm, (ROWS, d), _FULL),
           (uin, (ROWS, d), lambda i, j: (i, cb)), (uin, (ROWS, d), lambda i, j: (i, cb + 1))]
    wide = 2 * pw + 2 * d
    return epi, ins, [((t, d), BF16, (ROWS, d), _FULL)] * 2 + [((t, wide), BF16, (ROWS, wide), _FULL)]


def _loss_epilogue(x_in, gt, tgt, g_final, t, d):
    blk = (ROWS, d)
    full = lambda i, j: (i, j)
    rowv = lambda i, j: (0, j)

    def epi(acc, res, scale, tv, gv):
        xv = res + (0.5 * scale) * acc
        r = lax.rsqrt(jnp.mean(xv * xv, axis=-1, keepdims=True) + EPS)
        xr = xv * r
        e = xr * gv - tv
        loss_row = 0.5 * jnp.mean(e * e, axis=-1, keepdims=True)
        dout = e * (1.0 / d)
        gd = gv * dout
        dx = r * (gd - xr * jnp.mean(gd * xr, axis=-1, keepdims=True))
        fdx = 0.5 * dx
        return [dx, scale * fdx], [_colsum(dout * xr), _colsum(loss_row * jnp.ones((1, 128), F32)), _colsum(fdx * acc)]

    ins = [(x_in, blk, full), (gt, (1, d), rowv), (tgt, blk, full), (g_final, (1, d), rowv)]
    return epi, ins, [((t, d), F32, blk, full), ((t, d), BF16, blk, full)], [d, 128, d]


def _norm_bwd_epilogue(x, dres, g, sc, up, t, d):
    blk = (ROWS, d)
    full = lambda i, j: (i, j)
    rowv = lambda i, j: (0, j)

    def epi(dhv, xv, drv, gv, scv, *rest):
        r = lax.rsqrt(jnp.mean(xv * xv, axis=-1, keepdims=True) + EPS)
        xr = xv * r
        dn = dhv * (1.0 + scv)
        gd = gv * dn
        dx = drv + r * (gd - xr * jnp.mean(gd * xr, axis=-1, keepdims=True))
        outs, reds = [dx], [_colsum(dhv), _colsum(dhv * xr * gv), _colsum(dn * xr)]
        if up:
            yv, gtv = rest
            fdx = up[2] * dx
            outs.append(gtv * fdx)
            reds.append(_colsum(fdx * yv.astype(F32)))
        return outs, reds

    ins = [(x, blk, full), (dres, blk, full), (g, (1, d), rowv), (sc, (1, d), rowv)]
    ins += [(up[0], blk, full), (up[1], (1, d), rowv)] if up else []
    outs = [((t, d), F32, blk, full)] + ([((t, d), BF16, blk, full)] if up else [])
    return epi, ins, outs, [d] * (4 if up else 3)


def _ffn_in_act(name, h, w_in, tm=ROWS):
    t, d = h.shape
    ns = w_in.shape[2]
    tm = _pick(t, tm, 8)
    w4 = w_in.reshape(2, 2, d, ns)

    def body(h_ref, w_ref, ab_ref, act_ref):
        hv = h_ref[...]
        a = jnp.dot(hv, w_ref[0], preferred_element_type=F32)
        b = jnp.dot(hv, w_ref[1], preferred_element_type=F32)
        ab_ref[0] = a.astype(BF16)
        ab_ref[1] = b.astype(BF16)
        act_ref[...] = (a * _sigmoid(a) * b).astype(BF16)

    return _pcall(body, name=name, grid=(2, t // tm),
                  in_specs=[pl.BlockSpec((tm, d), lambda c, i: (i, 0)), pl.BlockSpec((2, None, d, ns), lambda c, i: (0, c, 0, 0))],
                  out_specs=[pl.BlockSpec((2, tm, ns), lambda c, i: (0, i, c)), pl.BlockSpec((tm, ns), lambda c, i: (i, c))],
                  out_shape=[jax.ShapeDtypeStruct((2, t, 2 * ns), BF16), jax.ShapeDtypeStruct((t, 2 * ns), BF16)],
                  compiler_params=_params("parallel", "parallel"))(h, w4)


def _dswiglu_epilogue(ab, t, f, tn):
    blk = (2, ROWS, tn)
    idx = lambda i, j: (0, i, j)

    def epi(dact, abv):
        a, b = abv[0].astype(F32), abv[1].astype(F32)
        s = _sigmoid(a)
        return (jnp.stack([dact * b * (s * (1.0 + a * (1.0 - s))), dact * (a * s)]),)

    return epi, [(ab, blk, idx)], [((2, t, f), BF16, blk, idx)]


def _ffn_fwd(tag, x, h, gt, get_w_in, get_w_out, nxt=None, loss=None):
    t, d = x.shape
    ab, act = _ffn_in_act(tag + "_in", h, get_w_in(h))
    if loss:
        epi, epi_ins, epi_outs, epi_reds = _loss_epilogue(x, gt, *loss, t, d)
        res = _mm(tag + "_out", act, get_w_out(act), tm=ROWS, tn=d, epi=epi, epi_ins=epi_ins, epi_outs=epi_outs,
                  epi_reds=epi_reds, b_resident=True)
        return res, (x, h, ab, act, None)
    epi, epi_ins, epi_outs = _gate_norm_epilogue(0.5, x, gt, nxt, t, d)
    res = _mm(tag + "_out", act, get_w_out(act), tm=ROWS, tn=d, epi=epi, epi_ins=epi_ins, epi_outs=epi_outs, b_resident=True)
    return res[0], res[2], (x, h, ab, act, res[1])


def _ffn_bwd(tag, dx_new, dy, saved, g, sc, w_in, w_out, start_rs, up, deps=()):
    x, h, ab, act, _ = saved
    d = x.shape[1]
    f = act.shape[1]
    dw_out = _mm(tag + "_dwout", act, dy, ta=True, tm=1408, tn=d, tk=2048, deps=deps)
    tok = start_rs("out", dw_out.reshape(N_CHIPS, f // N_CHIPS, d))
    epi, epi_ins, epi_outs = _dswiglu_epilogue(ab, x.shape[0], f, w_in.shape[2])
    dab = _mm(tag + "_dact", dy, w_out, tb=True, tm=ROWS, tn=w_in.shape[2], epi=epi, epi_ins=epi_ins, epi_outs=epi_outs,
              deps=(tok,), j_outer=True)
    dw_in = _mm(tag + "_dwin", h, dab, ta=True, out_stacked=True, b_halves=True, tm=d, tn=1408, tk=2048)
    tok = start_rs("in", dw_in)
    epi, epi_ins, epi_outs, epi_reds = _norm_bwd_epilogue(x, dx_new, g, sc, up, x.shape[0], d)
    res = _mm(tag + "_dh", dab, w_in, tb=True, b_stacked=True, a_halves=True, tn=d, tk=5632, deps=(tok,), epi=epi,
              epi_ins=epi_ins, epi_outs=epi_outs, epi_reds=epi_reds, b_resident=True)
    if up:
        return res
    return res[0], None, res[1], res[2], res[3], None


def _row(v):
    return v.reshape(1, -1)


def _pack(parts):
    cols = []
    for p in parts:
        flat = p.reshape(-1).astype(F32)
        padn = (-flat.shape[0]) % 128
        cols.append(jnp.pad(flat, (0, padn)) if padn else flat)
    flat = jnp.concatenate(cols)
    padn = (-flat.shape[0]) % 1024
    if padn:
        flat = jnp.pad(flat, (0, padn))
    return flat.reshape(-1, 128)


def _unpack(packed, shapes):
    flat = packed.reshape(-1)
    out, off = [], 0
    for s in shapes:
        n = math.prod(s)
        out.append(flat[off:off + n].reshape(s))
        off += n + ((-n) % 128)
    return out


SMALL = ['b_ada', 'g_ffn1', 'g_mix', 'pool_w', 'pool_b', 'pool_scale', 'ssm_lam_re_log', 'ssm_lam_im', 'ssm_log_dt',
         'ssm_b_re', 'ssm_b_im', 'ssm_c_re', 'ssm_c_im', 'ssm_d', 'b_glu', 'g_ffn2', 'g_final']
BIG = ['w_ffn1_in', 'w_ffn1_out', 'w_in', 'w_pool_up', 'w_glu', 'w_ssm_up', 'w_out', 'w_ffn2_in', 'w_ffn2_out']
AG_GROUPS = [[0], [1], [2, 3, 4, 5, 6], [7, 8]]
AG_DIRECT = [False, False, False, True]
SMALL_LATE = ['b_ada', 'g_ffn1']
RS_SUM_GROUPS = [[0, 1], [2, 3], [4, 5]]
WEIGHTS = ['w_ada', 'b_ada', 'g_ffn1', 'w_ffn1_in', 'w_ffn1_out', 'g_mix', 'w_in', 'pool_w', 'pool_b', 'pool_scale', 'w_pool_up',
           'ssm_lam_re_log', 'ssm_lam_im', 'ssm_log_dt', 'ssm_b_re', 'ssm_b_im', 'ssm_c_re', 'ssm_c_im', 'ssm_d', 'w_glu', 'b_glu',
           'w_ssm_up', 'w_out', 'g_ffn2', 'w_ffn2_in', 'w_ffn2_out', 'g_final']


def kernel(x, c, w_ada, b_ada, g_ffn1, w_ffn1_in, w_ffn1_out, g_mix, w_in, pool_w, pool_b, pool_scale, w_pool_up, ssm_lam_re_log, ssm_lam_im, ssm_log_dt, ssm_b_re, ssm_b_im, ssm_c_re, ssm_c_im, ssm_d, w_glu, b_glu, w_ssm_up, w_out, g_ffn2, w_ffn2_in, w_ffn2_out, g_final, loss_target, m_w_ada, m_b_ada, m_g_ffn1, m_w_ffn1_in, m_w_ffn1_out, m_g_mix, m_w_in, m_pool_w, m_pool_b, m_pool_scale, m_w_pool_up, m_ssm_lam_re_log, m_ssm_lam_im, m_ssm_log_dt, m_ssm_b_re, m_ssm_b_im, m_ssm_c_re, m_ssm_c_im, m_ssm_d, m_w_glu, m_b_glu, m_w_ssm_up, m_w_out, m_g_ffn2, m_w_ffn2_in, m_w_ffn2_out, m_g_final, v_w_ada, v_b_ada, v_g_ffn1, v_w_ffn1_in, v_w_ffn1_out, v_g_mix, v_w_in, v_pool_w, v_pool_b, v_pool_scale, v_w_pool_up, v_ssm_lam_re_log, v_ssm_lam_im, v_ssm_log_dt, v_ssm_b_re, v_ssm_b_im, v_ssm_c_re, v_ssm_c_im, v_ssm_d, v_w_glu, v_b_glu, v_w_ssm_up, v_w_out, v_g_ffn2, v_w_ffn2_in, v_w_ffn2_out, v_g_final):
    args = dict(locals())
    wt = {n: args[n] for n in WEIGHTS}
    mom = {n: args["m_" + n] for n in WEIGHTS}
    var = {n: args["v_" + n] for n in WEIGHTS}

    t, d = x.shape[1], x.shape[2]
    pw = pool_b.shape[1]
    sw = ssm_d.shape[1]
    ngrp = sw // SSM_GROUP
    gn = ngrp * SSM_STATE
    xi, yi, ci = _place()
    b_me = 4 * xi + 2 * yi + ci
    s_me = 2 * xi + yi
    x2d = x[0]
    tgt = loss_target[0]

    c_all = _allgather_small("ag_c", c.reshape(8, d // 8)).reshape(N_DEV, d)
    ncol = w_ada.shape[2]
    b_sh = lax.dynamic_slice(b_ada, (0, s_me * ncol), (1, ncol))
    mod_sh = _ada_fwd("ada_fwd", c_all, w_ada[0], b_sh)
    md_send, md_recv, mod_sh, md_land, tok = _bcast_start("mod_start", mod_sh)

    shards = [wt[n][0].astype(BF16) for n in BIG]
    ag_sems, shards_t, lands_t, tok = _ag_start("ag_start", shards, _place_shards(shards, s_me), AG_GROUPS, AG_DIRECT, deps=(tok,))
    md_land = _bcast_wait("mod_wait", mod_sh, md_land, md_send, md_recv, tok)
    mod_all = lax.dynamic_update_slice(md_land, mod_sh[None], (b_me, 0, 0))
    full = {}

    def weights(gi, *after):
        grp = AG_GROUPS[gi]
        if BIG[grp[0]] not in full:
            ls = _ag_wait("ag_wait%d" % gi, [shards_t[w] for w in grp], [lands_t[w] for w in grp], *ag_sems[gi],
                          AG_DIRECT[gi], after)
            for w, l in zip(grp, ls if AG_DIRECT[gi] else _ag_forward("ag_fwd%d" % gi, ls)):
                full[BIG[w]] = l
        return full

    mod_me = jnp.concatenate([lax.dynamic_slice(mod_all, (2 * s, b_me, 0), (1, 1, ncol))[0] for s in range(N_CHIPS)], axis=1)
    mod = mod_me.reshape(9, d)
    sh1, sc1, gt1, sh2, sc2, gt2, sh3, sc3, gt3 = [mod[k:k + 1] for k in range(9)]

    f = w_ffn1_out.shape[1] * N_CHIPS

    col = lambda a: a.reshape(gn, 1)
    lrl_c, li_c = col(ssm_lam_re_log), col(ssm_lam_im)
    ldt_c = col(jnp.broadcast_to(ssm_log_dt.reshape(ngrp, 1), (ngrp, SSM_STATE)))
    b_re2, b_im2 = ssm_b_re.reshape(gn, SSM_GROUP), ssm_b_im.reshape(gn, SSM_GROUP)
    lrdt_c, ang_c, bb_re, bb_im = _ssm_prep("ssm_prep", lrl_c, li_c, ldt_c, b_re2, b_im2)
    lrdt, ang = lrdt_c.reshape(1, gn), ang_c.reshape(1, gn)
    tr = lambda a: jnp.swapaxes(a, 1, 2)
    bbd = [_blockdiag_b(v, sw).astype(BF16) for v in (bb_re, bb_im)]
    ccd = [_blockdiag_c(v[0], sw).astype(BF16) for v in (ssm_c_re, ssm_c_im)]
    bbd_t, ccd_t = [tr(v) for v in bbd], [tr(v) for v in ccd]
    early = [n for n in SMALL if n not in SMALL_LATE]
    packs = {}
    for tag, names, extra in (("early", early, []), ("late", SMALL_LATE, [jnp.zeros((128,), F32)])):
        packs[tag] = [_pack([src[n] for n in names] + extra) for src in (wt, mom, var)]
    shadow = [lrdt, ang, ldt_c, *bbd, *ccd, *bbd_t, *ccd_t, *packs["early"], *packs["late"]]

    h1 = _norm_fwd("ffn1_norm", x2d, g_ffn1, sc1, sh1, deps=(tok,))
    x1, h2, sav1 = _ffn_fwd("ffn1", x2d, h1, gt1, lambda after: weights(0, after, *shadow)['w_ffn1_in'],
                            lambda after: weights(1, after)['w_ffn1_out'].reshape(f, d), (g_mix, sc2, sh2))
    weights(2, h2)
    wo = full['w_out'].reshape(d, d)
    uin = _mm("mix_in", h2, full['w_in'], b_stacked=True, tm=1024, tn=768, j_outer=True)
    p_pool, z_pool, y_pool = _pool_fwd("pool_fwd", uin, pool_w[0], pool_b, pool_scale, full['w_pool_up'], pw)
    y_s, ge, s_re, s_im = _ssm_fwd("ssm_fwd", uin, lrdt, ang, *bbd, *ccd, ssm_d, sw)
    epi, epi_ins, epi_outs = _glu_epilogue(b_glu, t, sw)
    gv, sg = _mm("glu_in", ge, full['w_glu'], b_stacked=True, tm=ROWS, tn=d, epi=epi, epi_ins=epi_ins, epi_outs=epi_outs,
                 b_resident=True)
    epi, epi_ins, epi_outs = _gates_epilogue(y_pool, uin, t, d, pw)
    y_ssm, merged = _mm("ssm_up", sg, full['w_ssm_up'], b_stacked=True, tm=ROWS, tn=d, epi=epi, epi_ins=epi_ins,
                        epi_outs=epi_outs, b_resident=True)
    epi, epi_ins, epi_outs = _gate_norm_epilogue(1.0, x1, gt2, (g_ffn2, sc3, sh3), t, d)
    x2, y2, h3 = _mm("mix_out", merged, wo, tm=ROWS, tn=d, epi=epi, epi_ins=epi_ins, epi_outs=epi_outs, b_resident=True)

    (dx3, dy3, dg_final, loss_v, dgt3), sav3 = _ffn_fwd(
        "ffn2", x2, h3, gt3, lambda after: weights(3, after)['w_ffn2_in'],
        lambda after: weights(3, after)['w_ffn2_out'].reshape(f, d), loss=(tgt, _row(g_final)))

    gs = {}
    rs_open = []

    def start_rs(names, gbs):
        send, recv, g_thru, land_thru, token = _rs_start("rs_start_" + names[0], gbs)
        rs_open.append((names, send, recv, g_thru, land_thru))
        return token

    dx2, dy2, dsh3, dsc3, gs['g_ffn2'], dgt2 = _ffn_bwd(
        "ffn2b", dx3, dy3, sav3, g_ffn2, sc3, full['w_ffn2_in'], full['w_ffn2_out'].reshape(f, d),
        lambda key, g: start_rs(['w_ffn2_' + key], [g]), (y2, gt2, 1.0))

    epi, epi_ins, epi_outs = _gates_bwd_epilogue(y_pool, y_ssm, uin, t, d, pw)
    dy_pool, dy_ssm, duin = _mm("mix_dmerged", dy2, wo, tb=True, tm=ROWS, tn=d, epi=epi, epi_ins=epi_ins, epi_outs=epi_outs,
                                b_resident=True)
    g_wo = _mm("mix_dwout", merged, dy2, ta=True, tm=d, tn=d, tk=2048).reshape(N_CHIPS, d // N_CHIPS, d)

    g_wpu = _mm("pool_dwup", p_pool, dy_pool, ta=True, out_stacked=True, tm=pw, tn=d, tk=4096)
    duin, gs['pool_w'], gs['pool_b'], gs['pool_scale'] = _pool_bwd(
        "pool_bwd", dy_pool, z_pool, pool_w[0], pool_b, pool_scale, full['w_pool_up'], duin)

    g_wsu = _mm("ssm_dwup", sg, dy_ssm, ta=True, out_stacked=True, tm=sw, tn=d, tk=4096)
    tok = start_rs(['w_out', 'w_pool_up', 'w_ssm_up'], [g_wo, g_wpu, g_wsu])
    epi, epi_ins, epi_outs, epi_reds = _glu_bwd_epilogue(gv, b_glu, t, sw)
    dgv, gs['b_glu'] = _mm("ssm_dup", dy_ssm, full['w_ssm_up'], tb=True, b_stacked=True, tm=ROWS, tn=sw, tk=d, deps=(tok,),
                           epi=epi, epi_ins=epi_ins, epi_outs=epi_outs, epi_reds=epi_reds, b_resident=True)
    dge = _mm("glu_dge", dgv, full['w_glu'], tb=True, b_stacked=True, tm=1024, tn=sw, tk=d)
    g_wglu = _mm("glu_dw", ge, dgv, ta=True, out_stacked=True, tm=sw, tn=d, tk=4096)
    (duin, g_abre, g_abim, g_bbd_re, g_bbd_im, g_ccd_re, g_ccd_im, gs['ssm_d']) = _ssm_bwd(
        "ssm_bwd", dge, y_s, uin, s_re, s_im, lrdt, ang, *bbd_t, *ccd_t, ssm_d, sw, duin)
    gs['ssm_c_re'], gs['ssm_c_im'] = _diag_of_c(g_ccd_re, sw), _diag_of_c(g_ccd_im, sw)
    d_lrl, d_li, d_ldt, d_bre, d_bim = _ssm_param_bwd(
        "ssm_param_bwd", lrl_c, li_c, ldt_c, b_re2, b_im2, g_abre.reshape(gn, 1), g_abim.reshape(gn, 1),
        _diag_of_b(g_bbd_re, sw), _diag_of_b(g_bbd_im, sw))
    gs['ssm_lam_re_log'], gs['ssm_lam_im'] = d_lrl, d_li
    gs['ssm_log_dt'] = jnp.sum(d_ldt.reshape(ngrp, SSM_STATE), axis=1)
    gs['ssm_b_re'], gs['ssm_b_im'] = d_bre, d_bim

    g_win =_mm("mix_dwin", h2, duin, ta=True, out_stacked=True, tm=d, tn=768, tk=4096)
    tok = start_rs(['w_glu', 'w_in'], [g_wglu, g_win])
    epi, epi_ins, epi_outs, epi_reds = _norm_bwd_epilogue(x1, dx2, g_mix, sc2, (sav1[4], gt1, 0.5), t, d)
    dx1, dy1, dsh2, dsc2, gs['g_mix'], dgt1 = _mm(
        "mix_dh", duin, full['w_in'], tb=True, b_stacked=True, tn=d, tk=3072, deps=(tok,), epi=epi, epi_ins=epi_ins,
        epi_outs=epi_outs, epi_reds=epi_reds, b_resident=True)

    gs['g_final'] = dg_final
    sg_blk = _pack([gs[n] for n in early])
    sg_send, sg_recv, sg_blk, sg_land, tok = _bcast_start("sg_start", sg_blk)

    dx0, _, dsh1, dsc1, gs['g_ffn1'], _ = _ffn_bwd(
        "ffn1b", dx1, dy1, sav1, g_ffn1, sc1, full['w_ffn1_in'], full['w_ffn1_out'].reshape(f, d),
        lambda key, g: start_rs(['w_ffn1_' + key], [g]), None, deps=(tok,))

    gs['b_ada'] = jnp.concatenate([dsh1, dsc1, dgt1, dsh2, dsc2, dgt2, dsh3, dsc3, dgt3], axis=1)
    lt_blk = _pack([gs[n] for n in SMALL_LATE] + [loss_v])
    lt_send, lt_recv, lt_blk, lt_land, tok = _bcast_start("late_start", lt_blk)

    grads, delta, new_m, new_v = {}, {}, {}, {}

    def small_update(tag, names, blk, land):
        g8 = lax.dynamic_update_slice(land, blk[None], (b_me, 0, 0)).reshape(-1, 128)
        w_pack, m_pack, v_pack = packs[tag]
        shapes = [wt[n].shape for n in names]
        per_param, packed = _adamw_small("adamw_small_" + tag, w_pack, g8, m_pack, v_pack, shapes)
        for k, dst in enumerate((grads, delta, new_m, new_v)):
            rest = _unpack(packed[k], shapes)
            for i, n in enumerate(names):
                dst[n] = per_param[i][k] if per_param[i] is not None else rest[i]
        return g8

    after = tok
    for group in RS_SUM_GROUPS:
        names, g_done, land_done = [], [], []
        for k in group:
            nk, send, recv, g_thru, land_thru = rs_open[k]
            gd, ld = _rs_wait("rs_wait_" + nk[0], g_thru, land_thru, send, recv, after)
            names, g_done, land_done = names + nk, g_done + gd, land_done + ld
        for n, g_sum in zip(names, _rs_sum("rs_sum_" + names[0], g_done, land_done)):
            g_out, dl, mn, vn = _adamw("adamw_" + n, wt[n][0], g_sum, mom[n][0], var[n][0])
            grads[n], delta[n], new_m[n], new_v[n] = g_out[None], dl[None], mn[None], vn[None]
            after = dl
        if group is RS_SUM_GROUPS[-2]:
            small_update("early", early, sg_blk, _bcast_wait("sg_wait", sg_blk, sg_land, sg_send, sg_recv, after))

    lt_land = _bcast_wait("late_wait", lt_blk, lt_land, lt_send, lt_recv, after)
    late8 = small_update("late", SMALL_LATE, lt_blk, lt_land).reshape(N_DEV, -1)
    loss = jnp.sum(late8[:, 10 * d])
    dmod_sh = lax.dynamic_slice(late8, (0, s_me * ncol), (N_DEV, ncol))
    g_w_ada = _ada_bwd("ada_bwd", c_all, dmod_sh)
    dl, mn, vn = _adamw("adamw_w_ada", w_ada[0], g_w_ada, m_w_ada[0], v_w_ada[0], with_g=False)
    grads['w_ada'], delta['w_ada'], new_m['w_ada'], new_v['w_ada'] = g_w_ada[None], dl[None], mn[None], vn[None]

    return (loss, dx0[None], *[grads[n] for n in WEIGHTS], *[delta[n] for n in WEIGHTS],
            *[new_m[n] for n in WEIGHTS], *[new_v[n] for n in WEIGHTS])
```

```python
import functools
import math

import jax
import jax.numpy as jnp
from jax import lax
from jax.experimental import pallas as pl
from jax.experimental.pallas import tpu as pltpu

F32 = jnp.float32
BF16 = jnp.bfloat16
MESH = pl.DeviceIdType.MESH

EPS = 1e-6
POOL_WINDOWS = (2, 4, 8, 16)
POOL_HALO = 16
SSM_GROUP = 16
SSM_STATE = 64
SSM_BLOCKS = 4
N_DEV = 8
N_CHIPS = 4
ADAM_LR = 0.001
ADAM_B1 = 0.9
ADAM_B2 = 0.999
ADAM_EPS = 1e-08
ADAM_WD = 0.01
ADAM_STEP = 10
VMEM_LIMIT = 56 * 1024 * 1024
ROWS = 512


ANY_SPEC = pl.BlockSpec(memory_space=pl.ANY)
HBM_SPEC = pl.BlockSpec(memory_space=pltpu.HBM)
SEM_SPEC = pl.BlockSpec(memory_space=pltpu.SEMAPHORE)
EFFECT = pltpu.SideEffectType.DATAFLOW_SIDE_EFFECTING


def _hbm(a):
    return pltpu.with_memory_space_constraint(a, pltpu.HBM)


def _pcall(body, **kw):
    return pl.pallas_call(body, **kw)


def _params(*sem):
    return pltpu.CompilerParams(dimension_semantics=sem, vmem_limit_bytes=VMEM_LIMIT)


def _pick(n, cap, mult=128):
    if n <= cap:
        return n
    best = None
    for d in range(mult, cap + 1, mult):
        if n % d == 0:
            best = d
    assert best is not None, (n, cap, mult)
    return best


def _sigmoid(v):
    return 1.0 / (1.0 + jnp.exp(-v))


def _rowwise(name, fn, ins, params, outs, reds, tm, deps=()):
    t = ins[0][0].shape[0]
    tm = min(tm, t)
    nb = t // tm
    ni, npar, no, nd = len(ins), len(params), len(outs), len(deps)

    def body(*refs):
        iv = [r[...] for r in refs[:ni]]
        pv = [r[...] for r in refs[ni:ni + npar]]
        o_refs = refs[ni + npar + nd:ni + npar + nd + no]
        r_refs = refs[ni + npar + nd + no:]
        ovals, rvals = fn(iv, pv)
        for o_ref, val in zip(o_refs, ovals):
            off = 0
            if isinstance(val, tuple) and val[0] == "at":
                _, off, val = val
            parts = val if isinstance(val, (list, tuple)) else [val]
            for p in parts:
                o_ref[:, off:off + p.shape[1]] = p.astype(o_ref.dtype)
                off += p.shape[1]
        if r_refs:
            @pl.when(pl.program_id(0) == 0)
            def _():
                for r in r_refs:
                    r[...] = jnp.zeros_like(r)
            for r, val in zip(r_refs, rvals):
                r[...] += val

    in_specs = [pl.BlockSpec((tm, w), functools.partial(lambda i, cb: (i, cb), cb=cb)) for (_, w, cb) in ins]
    in_specs += [pl.BlockSpec(p.shape, lambda i: (0, 0)) for p in params]
    in_specs += [ANY_SPEC] * nd
    out_shape = [jax.ShapeDtypeStruct((t, w), dt) for (w, dt) in outs]
    out_shape += [jax.ShapeDtypeStruct((1, w), F32) for w in reds]
    out_specs = [pl.BlockSpec((tm, w), lambda i: (i, 0)) for (w, _) in outs]
    out_specs += [pl.BlockSpec((1, w), lambda i: (0, 0)) for w in reds]
    res = _pcall(body, name=name, grid=(nb,), in_specs=in_specs, out_specs=out_specs, out_shape=out_shape,
                 compiler_params=_params("arbitrary"))(*[a for a, _, _ in ins], *params, *deps)
    return res


def _colsum(v):
    return jnp.sum(v, axis=0, keepdims=True)


def _mm(name, a, b, *, ta=False, tb=False, b_stacked=False, out_stacked=False, tm=ROWS, tn=1024, tk=2816,
        out_dtype=BF16, epi=None, epi_ins=(), epi_outs=None, epi_reds=(), deps=(), j_outer=False, a_halves=False,
        b_halves=False, b_resident=False, wgrad=None, wgrad_rows=False):
    if a_halves:
        _, m, kdim = a.shape
        kdim *= 2
    elif ta:
        kdim, m = a.shape
    else:
        m, kdim = a.shape
    ns = None
    if b_stacked:
        ns = b.shape[2]
        n = b.shape[1] if tb else N_CHIPS * ns
        assert kdim == (N_CHIPS * ns if tb else b.shape[1]), (name, a.shape, b.shape)
    else:
        if b_halves:
            n = 2 * b.shape[2]
            assert kdim == b.shape[1] and not tb, (name, a.shape, b.shape)
        else:
            n = b.shape[0] if tb else b.shape[1]
            assert kdim == (b.shape[1] if tb else b.shape[0]), (name, a.shape, b.shape)
        if out_stacked:
            ns = n // N_CHIPS

    def shards(want):
        return max(g for g in (1, 2, 4) if g * ns <= max(want, ns))

    tm = _pick(m, tm, 128 if ta else 8)
    gn = gk = 1
    if (b_stacked and not tb) or out_stacked:
        gn = shards(min(tn, n // 2) if b_halves else tn)
        tn = gn * ns
    else:
        tn = _pick(n, tn)
    if b_stacked and tb:
        gk = shards(tk)
        tk = gk * ns
    else:
        tk = _pick(kdim, tk, 8 if ta else 128)
    nm, nn, nk = m // tm, n // tn, kdim // tk

    def ij(f):
        return (lambda g0, g1, k: f(g1, g0, k)) if j_outer else f

    if a_halves:
        assert b_stacked and tb and gk == N_CHIPS and nk == 1, name
        a_spec = pl.BlockSpec((2, tm, kdim // 2), ij(lambda i, j, k: (0, i, 0)))
    elif ta:
        a_spec = pl.BlockSpec((tk, tm), ij(lambda i, j, k: (k, i)))
    else:
        a_spec = pl.BlockSpec((tm, tk), ij(lambda i, j, k: (i, k)))
    if b_stacked and not tb:
        b_spec = pl.BlockSpec((gn, tk, ns), ij(lambda i, j, k: (j, k, 0)))
    elif b_stacked and tb:
        b_spec = pl.BlockSpec((gk, tn, ns), ij(lambda i, j, k: (k, j, 0)))
    elif b_halves:
        bph = (n // 2) // tn
        b_spec = pl.BlockSpec((None, tk, tn), ij(lambda i, j, k: (j // bph, k, j % bph)))
    elif tb:
        b_spec = pl.BlockSpec((tn, tk), ij(lambda i, j, k: (j, k)))
    else:
        b_spec = pl.BlockSpec((tk, tn), ij(lambda i, j, k: (k, j)))
    dims = (((0 if ta else 1,), (1 if tb else 0,)), ((), ()))

    if epi_outs is None:
        if out_stacked:
            epi_outs = [((N_CHIPS, m, ns), out_dtype, (gn, tm, ns), lambda i, j: (j, i, 0))]
        else:
            epi_outs = [((m, n), out_dtype, (tm, tn), lambda i, j: (i, j))]
    ne, no, nd, nr = len(epi_ins), len(epi_outs), len(deps), len(epi_reds)
    nx = 0 if wgrad is None else 1
    assert not (nr or nx) or (nn == 1 and not j_outer), name
    assert not nx or (nk == 1 and not ta and not a_halves and kdim % N_CHIPS == 0), name

    def body(a_ref, b_ref, *rest):
        e_refs = rest[:ne]
        x_refs = rest[ne:ne + nx]
        o_refs = rest[ne + nx + nd:ne + nx + nd + no]
        r_refs = rest[ne + nx + nd + no:ne + nx + nd + no + nr]
        wg_refs = rest[ne + nx + nd + no + nr:ne + nx + nd + no + nr + nx]
        scratch = rest[ne + nx + nd + no + nr + nx:]
        av = None if a_halves else a_ref[...].astype(BF16)
        if nx:
            wg_acc = scratch[-1]
            i = pl.program_id(0)
            pw_ = lax.dot_general(x_refs[0][...].astype(BF16), av, (((0,), (0,)), ((), ())), preferred_element_type=F32)

            @pl.when(i == 0)
            def _():
                wg_acc[...] = pw_

            @pl.when(i > 0)
            def _():
                wg_acc[...] += pw_

            @pl.when(i == nm - 1)
            def _():
                if wgrad_rows:
                    wg_refs[0][...] = wg_acc[...].astype(BF16)
                else:
                    ks = kdim // N_CHIPS
                    for s in range(N_CHIPS):
                        wg_refs[0][s] = wg_acc[:, s * ks:(s + 1) * ks].astype(BF16)
        if b_stacked and not tb:
            parts = [lax.dot_general(av, b_ref[s].astype(BF16), dims, preferred_element_type=F32) for s in range(gn)]
        elif b_stacked and tb:
            p = None
            for s in range(gk):
                if a_halves:
                    a_s = a_ref[s // 2, :, (s % 2) * ns:(s % 2 + 1) * ns].astype(BF16)
                else:
                    a_s = av[:, s * ns:(s + 1) * ns]
                q = lax.dot_general(a_s, b_ref[s].astype(BF16), dims, preferred_element_type=F32)
                p = q if p is None else p + q
            parts = [p]
        else:
            parts = [lax.dot_general(av, b_ref[...].astype(BF16), dims, preferred_element_type=F32)]

        def finish(acc_parts):
            if epi is None and out_stacked:
                acc = acc_parts[0]
                for s in range(gn):
                    o_refs[0][s] = acc[:, s * ns:(s + 1) * ns].astype(out_dtype)
            elif epi is None:
                w = acc_parts[0].shape[1]
                for s, part in enumerate(acc_parts):
                    o_refs[0][:, s * w:(s + 1) * w] = part.astype(out_dtype)
            else:
                acc = acc_parts[0] if len(acc_parts) == 1 else jnp.concatenate(acc_parts, axis=1)
                vals = epi(acc, *[r[...] for r in e_refs])
                if nr:
                    vals, reds = vals

                    @pl.when(pl.program_id(0) == 0)
                    def _():
                        for r in r_refs:
                            r[...] = jnp.zeros_like(r)
                    for r, val in zip(r_refs, reds):
                        r[...] += val
                for o_ref, val in zip(o_refs, vals):
                    if isinstance(val, tuple) and val[0] == "at":
                        o_ref[:, val[1]:val[1] + val[2].shape[1]] = val[2].astype(o_ref.dtype)
                    else:
                        o_ref[...] = val.astype(o_ref.dtype)

        if nk == 1:
            finish(parts)
        else:
            acc_ref = scratch[0]
            k = pl.program_id(2)
            w = parts[0].shape[1]

            @pl.when(k == 0)
            def _():
                for s, part in enumerate(parts):
                    acc_ref[:, s * w:(s + 1) * w] = part

            @pl.when(k > 0)
            def _():
                for s, part in enumerate(parts):
                    acc_ref[:, s * w:(s + 1) * w] += part

            @pl.when(k == nk - 1)
            def _():
                finish([acc_ref[...]])

    def _ij(f):
        return ij(lambda i, j, k: f(i, j))

    if b_resident:
        assert nk == 1 and (nn == 1 or j_outer), name
        b_spec = pl.BlockSpec(b_spec.block_shape, b_spec.index_map, pipeline_mode=pl.Buffered(1))
    in_specs = [a_spec, b_spec] + [pl.BlockSpec(blk, _ij(f)) for (_, blk, f) in epi_ins]
    if nx:
        in_specs.append(pl.BlockSpec((tm, wgrad.shape[1]), lambda i, j, k: (i, 0)))
    in_specs += [ANY_SPEC] * nd
    out_specs = [pl.BlockSpec(blk, _ij(f)) for (_, _, blk, f) in epi_outs]
    out_specs += [pl.BlockSpec((1, w), lambda *_: (0, 0)) for w in epi_reds]
    out_shape = [jax.ShapeDtypeStruct(s, dt) for (s, dt, _, _) in epi_outs] + [jax.ShapeDtypeStruct((1, w), F32) for w in epi_reds]
    scratch = [pltpu.VMEM((tm, tn), F32)] if nk > 1 else []
    if nx:
        wg_shape = (wgrad.shape[1], kdim) if wgrad_rows else (N_CHIPS, wgrad.shape[1], kdim // N_CHIPS)
        out_specs.append(pl.BlockSpec(wg_shape, lambda *_: (0,) * len(wg_shape)))
        out_shape.append(jax.ShapeDtypeStruct(wg_shape, BF16))
        scratch.append(pltpu.VMEM((wgrad.shape[1], kdim), F32))
    grid = (nn, nm, nk) if j_outer else (nm, nn, nk)
    sem = ("arbitrary",) * 3 if (nr or nx) else ("parallel", "parallel", "arbitrary")
    res = _pcall(body, name=name, grid=grid, in_specs=in_specs, out_specs=out_specs, out_shape=out_shape, scratch_shapes=scratch,
                 compiler_params=_params(*sem))(a, b, *[x for x, _, _ in epi_ins], *([wgrad] * nx), *deps)
    return res[0] if len(res) == 1 else res


def _norm_fwd(name, x, g, sc, sh, deps=()):
    d = x.shape[1]

    def fn(iv, pv):
        (xv,), (gv, scv, shv) = iv, pv
        r = lax.rsqrt(jnp.mean(xv * xv, axis=-1, keepdims=True) + EPS)
        return [xv * r * gv * (1.0 + scv) + shv], []

    return _rowwise(name, fn, [(x, d, 0)], [g, sc, sh], [(d, BF16)], [], ROWS, deps=deps)[0]


def _adamw(name, w, g, m, v, tm=256, with_g=True):
    c = w.shape[1]

    def fn(iv, pv):
        wv, gv, mv, vv = iv
        mn = ADAM_B1 * mv + (1.0 - ADAM_B1) * gv
        vn = ADAM_B2 * vv + (1.0 - ADAM_B2) * (gv * gv)
        m_hat = mn / (1.0 - ADAM_B1 ** ADAM_STEP)
        v_hat = vn / (1.0 - ADAM_B2 ** ADAM_STEP)
        delta = -ADAM_LR * (m_hat / (jnp.sqrt(v_hat) + ADAM_EPS) + ADAM_WD * wv)
        return ([gv] if with_g else []) + [delta, mn, vn], []

    return _rowwise(name, fn, [(w, c, 0), (g, c, 0), (m, c, 0), (v, c, 0)], [], [(c, F32)] * (4 if with_g else 3), [],
                    _pick(w.shape[0], tm, 8))


def _unpack_plan(shape):
    n = math.prod(shape)
    if len(shape) == 2 and shape[0] == 1 and n % 128 == 0:
        return [((slice(None), slice(128 * r, 128 * (r + 1))), slice(r, r + 1), slice(None)) for r in range(n // 128)]
    if len(shape) == 2 and shape[0] == 1 and n < 128:
        return [((slice(None), slice(None)), slice(0, 1), slice(0, n))]
    if len(shape) == 1 and n % 128 == 0:
        return [((slice(128 * r, 128 * (r + 1)),), r, slice(None)) for r in range(n // 128)]
    if len(shape) == 3 and shape[0] == 1 and shape[2] == 64:
        return [((0, slice(2 * r + h, 2 * r + h + 1), slice(None)), slice(r, r + 1), slice(64 * h, 64 * (h + 1)))
                for r in range(n // 128) for h in range(2)]
    if len(shape) == 4 and shape[0] == 1 and shape[2:] == (128, 128):
        return [((0, k), slice(128 * k, 128 * (k + 1)), slice(None)) for k in range(shape[1])]
    return None


def _adamw_small(name, w, g8, m, v, shapes):
    r = w.shape[0]
    plans, rows0, off = [], [], 0
    for s in shapes:
        plans.append(_unpack_plan(s))
        rows0.append(off // 128)
        off += math.prod(s) + (-math.prod(s)) % 128
    direct = [i for i, p in enumerate(plans) if p is not None]

    def body(w_ref, g_ref, m_ref, v_ref, *out):
        packed = out[4 * len(direct):]
        gv = g_ref[0:r, :]
        for k in range(1, N_DEV):
            gv = gv + g_ref[k * r:(k + 1) * r, :]
        mn = ADAM_B1 * m_ref[...] + (1.0 - ADAM_B1) * gv
        vn = ADAM_B2 * v_ref[...] + (1.0 - ADAM_B2) * (gv * gv)
        m_hat = mn / (1.0 - ADAM_B1 ** ADAM_STEP)
        v_hat = vn / (1.0 - ADAM_B2 ** ADAM_STEP)
        packed[0][...] = gv
        packed[1][...] = -ADAM_LR * (m_hat / (jnp.sqrt(v_hat) + ADAM_EPS) + ADAM_WD * w_ref[...])
        packed[2][...] = mn
        packed[3][...] = vn
        for di, i in enumerate(direct):
            for kind in range(4):
                o_ref, src = out[4 * di + kind], packed[kind]
                for o_idx, row, lanes in plans[i]:
                    row = (rows0[i] + row) if isinstance(row, int) else slice(rows0[i] + row.start, rows0[i] + row.stop)
                    o_ref[o_idx] = src[row, lanes]

    out_shape = [jax.ShapeDtypeStruct(shapes[i], F32) for i in direct for _ in range(4)]
    out_shape += [jax.ShapeDtypeStruct((r, 128), F32)] * 4
    res = _pcall(body, name=name, out_shape=out_shape,
                 compiler_params=pltpu.CompilerParams(vmem_limit_bytes=VMEM_LIMIT))(w, g8, m, v)
    per_param = [None] * len(shapes)
    for di, i in enumerate(direct):
        per_param[i] = res[4 * di:4 * di + 4]
    return per_param, res[4 * len(direct):]


def _pool_fwd(name, uin, pool_w, pool_b, pool_scale, w_up, pw, tm=ROWS):
    t = uin.shape[0]
    tm = min(tm, t)
    ng = len(POOL_WINDOWS)
    gw = pw // ng
    ns = w_up.shape[2]

    def body(u_ref, w_ref, b_ref, s_ref, wu_ref, p_ref, z_ref, y_ref, ext):
        i = pl.program_id(0)

        @pl.when(i == 0)
        def _():
            ext[0:POOL_HALO, :] = jnp.zeros((POOL_HALO, pw), F32)

        u = u_ref[...].astype(F32)
        ext[POOL_HALO:POOL_HALO + tm, :] = u
        pos = i * tm + lax.broadcasted_iota(jnp.int32, (tm, 1), 0)
        for k, win in enumerate(POOL_WINDOWS):
            cols = slice(k * gw, (k + 1) * gw)
            acc = u[:, cols]
            for j in range(1, win):
                acc = acc + ext[POOL_HALO - j:POOL_HALO - j + tm, cols]
            cnt = jnp.minimum(pos + 1, win).astype(F32)
            z = acc / cnt - u[:, cols]
            zp = jnp.dot(z.astype(BF16), w_ref[k].astype(BF16), preferred_element_type=F32) + b_ref[:, cols]
            p_ref[:, cols] = (zp * s_ref[:, cols]).astype(BF16)
            z_ref[:, cols] = z.astype(BF16)
        ext[0:POOL_HALO, :] = u[tm - POOL_HALO:tm, :]
        pv = p_ref[...]
        for s in range(N_CHIPS):
            y_ref[:, s * ns:(s + 1) * ns] = jnp.dot(pv, wu_ref[s], preferred_element_type=F32).astype(BF16)

    full3 = lambda i: (0, 0, 0)
    return _pcall(
        body, name=name, grid=(t // tm,),
        in_specs=[pl.BlockSpec((tm, pw), lambda i: (i, 0)), pl.BlockSpec(pool_w.shape, full3),
                  pl.BlockSpec(pool_b.shape, lambda i: (0, 0)), pl.BlockSpec(pool_scale.shape, lambda i: (0, 0)),
                  pl.BlockSpec(w_up.shape, full3)],
        out_specs=[pl.BlockSpec((tm, pw), lambda i: (i, 0))] * 2 + [pl.BlockSpec((tm, N_CHIPS * ns), lambda i: (i, 0))],
        out_shape=[jax.ShapeDtypeStruct((t, pw), BF16)] * 2 + [jax.ShapeDtypeStruct((t, N_CHIPS * ns), BF16)],
        scratch_shapes=[pltpu.VMEM((POOL_HALO + tm, pw), F32)],
        compiler_params=_params("arbitrary"))(uin, pool_w, pool_b, pool_scale, w_up)


def _pool_bwd(name, dy, z, pool_w, pool_b, pool_scale, w_up, duin, tm=ROWS):
    t, pw = z.shape
    tm = min(tm, t)
    nb = t // tm
    ng = len(POOL_WINDOWS)
    gw = pw // ng
    ns = w_up.shape[2]

    def body(dy_ref, z_ref, w_ref, b_ref, s_ref, wu_ref, duin_ref, du_ref, dw_ref, db_ref, ds_ref, ext):
        i = pl.program_id(0)

        @pl.when(i == 0)
        def _():
            ext[tm:tm + POOL_HALO, :] = jnp.zeros((POOL_HALO, pw), F32)
            dw_ref[...] = jnp.zeros_like(dw_ref)
            db_ref[...] = jnp.zeros_like(db_ref)
            ds_ref[...] = jnp.zeros_like(ds_ref)

        pos = (nb - 1 - i) * tm + lax.broadcasted_iota(jnp.int32, (tm, 1), 0)
        dp = None
        for s in range(N_CHIPS):
            q = lax.dot_general(dy_ref[:, s * ns:(s + 1) * ns], wu_ref[s], (((1,), (1,)), ((), ())), preferred_element_type=F32)
            dp = q if dp is None else dp + q
        for k, win in enumerate(POOL_WINDOWS):
            cols = slice(k * gw, (k + 1) * gw)
            zk = z_ref[:, cols]
            dpk = dp[:, cols]
            wk = w_ref[k].astype(BF16)
            zp = jnp.dot(zk, wk, preferred_element_type=F32) + b_ref[:, cols]
            ds_ref[:, cols] += _colsum(dpk * zp)
            dzp = dpk * s_ref[:, cols]
            db_ref[:, cols] += _colsum(dzp)
            dzpb = dzp.astype(BF16)
            dz = lax.dot_general(dzpb, wk, (((1,), (1,)), ((), ())), preferred_element_type=F32)
            dw_ref[k] += lax.dot_general(zk, dzpb, (((0,), (0,)), ((), ())), preferred_element_type=F32)
            cnt = jnp.minimum(pos + 1, win).astype(F32)
            r = dz / cnt
            ext[0:tm, cols] = r
            acc = r - dz
            for j in range(1, win):
                acc = acc + ext[j:j + tm, cols]
            du_ref[:, cols] = acc.astype(BF16)
        ext[tm:tm + POOL_HALO, :] = ext[0:POOL_HALO, :]

    rev = lambda i: (nb - 1 - i, 0)
    full3 = lambda i: (0, 0, 0)
    return _pcall(
        body, name=name, grid=(nb,),
        in_specs=[pl.BlockSpec((tm, N_CHIPS * ns), rev), pl.BlockSpec((tm, pw), rev), pl.BlockSpec(pool_w.shape, full3),
                  pl.BlockSpec(pool_b.shape, lambda i: (0, 0)), pl.BlockSpec(pool_scale.shape, lambda i: (0, 0)),
                  pl.BlockSpec(w_up.shape, full3), ANY_SPEC],
        out_specs=[pl.BlockSpec((tm, pw), rev), pl.BlockSpec(pool_w.shape, full3),
                   pl.BlockSpec((1, pw), lambda i: (0, 0)), pl.BlockSpec((1, pw), lambda i: (0, 0))],
        out_shape=[jax.ShapeDtypeStruct(duin.shape, BF16), jax.ShapeDtypeStruct(pool_w.shape, F32),
                   jax.ShapeDtypeStruct((1, pw), F32), jax.ShapeDtypeStruct((1, pw), F32)],
        scratch_shapes=[pltpu.VMEM((tm + POOL_HALO, pw), F32)], input_output_aliases={6: 0},
        compiler_params=_params("arbitrary"))(dy, z, pool_w, pool_b, pool_scale, w_up, duin)


def _ssm_disc(lrl, li, ldt):
    lr = -jnp.exp(lrl)
    dt = jnp.exp(ldt)
    mag = jnp.exp(lr * dt)
    ang = li * dt
    ab_re = mag * jnp.cos(ang)
    ab_im = mag * jnp.sin(ang)
    num_re = ab_re - 1.0
    num_im = ab_im
    den = lr * lr + li * li
    f_re = (num_re * lr + num_im * li) / den
    f_im = (num_im * lr - num_re * li) / den
    return lr, dt, mag, ang, ab_re, ab_im, num_re, num_im, den, f_re, f_im


def _ssm_prep(name, lrl, li, ldt, b_re, b_im):
    gn, h = b_re.shape

    def body(lrl_ref, li_ref, ldt_ref, br_ref, bi_ref, lrdt_ref, ang_ref, bbr_ref, bbi_ref):
        lr, dt, _, ang, _, _, _, _, _, f_re, f_im = _ssm_disc(lrl_ref[...], li_ref[...], ldt_ref[...])
        lrdt_ref[...] = lr * dt
        ang_ref[...] = ang
        br, bi = br_ref[...], bi_ref[...]
        bbr_ref[...] = f_re * br - f_im * bi
        bbi_ref[...] = f_re * bi + f_im * br

    col = jax.ShapeDtypeStruct((gn, 1), F32)
    mat = jax.ShapeDtypeStruct((gn, h), F32)
    return _pcall(body, name=name, out_shape=[col, col, mat, mat])(lrl, li, ldt, b_re, b_im)


def _ssm_param_bwd(name, lrl, li, ldt, b_re, b_im, g_abre, g_abim, g_bbre, g_bbim):
    gn, h = b_re.shape

    def body(lrl_ref, li_ref, ldt_ref, br_ref, bi_ref, gar_ref, gai_ref, gbr_ref, gbi_ref,
             dlrl_ref, dli_ref, dldt_ref, dbr_ref, dbi_ref):
        li_v = li_ref[...]
        lr, dt, mag, ang, ab_re, ab_im, num_re, num_im, den, f_re, f_im = _ssm_disc(lrl_ref[...], li_v, ldt_ref[...])
        br, bi = br_ref[...], bi_ref[...]
        gbr, gbi = gbr_ref[...], gbi_ref[...]
        g_fre = jnp.sum(gbr * br + gbi * bi, axis=1, keepdims=True)
        g_fim = jnp.sum(gbi * br - gbr * bi, axis=1, keepdims=True)
        dbr_ref[...] = gbr * f_re + gbi * f_im
        dbi_ref[...] = gbi * f_re - gbr * f_im
        g_num_re = (g_fre * lr - g_fim * li_v) / den
        g_num_im = (g_fre * li_v + g_fim * lr) / den
        g_den = -(g_fre * f_re + g_fim * f_im) / den
        g_lr = (g_fre * num_re + g_fim * num_im) / den + g_den * 2.0 * lr
        g_li = (g_fre * num_im - g_fim * num_re) / den + g_den * 2.0 * li_v
        g_are = gar_ref[...] + g_num_re
        g_aim = gai_ref[...] + g_num_im
        g_mag = g_are * jnp.cos(ang) + g_aim * jnp.sin(ang)
        g_ang = g_aim * ab_re - g_are * ab_im
        g_lrdt = g_mag * mag
        g_lr = g_lr + g_lrdt * dt
        g_dt = g_lrdt * lr + g_ang * li_v
        g_li = g_li + g_ang * dt
        dlrl_ref[...] = g_lr * lr
        dli_ref[...] = g_li
        dldt_ref[...] = g_dt * dt

    col = jax.ShapeDtypeStruct((gn, 1), F32)
    mat = jax.ShapeDtypeStruct((gn, h), F32)
    return _pcall(body, name=name, out_shape=[col, col, col, mat, mat])(lrl, li, ldt, b_re, b_im, g_abre, g_abim, g_bbre, g_bbim)


def _pow_rows(lrdt, ang, k):
    mag = jnp.exp(k * lrdt)
    return mag * jnp.cos(k * ang), mag * jnp.sin(k * ang)


def _ssm_chunk(t):
    return 256 if t >= 2048 else 128


def _to_segments(dst, srcs, jn):
    for q, src in enumerate(srcs):
        for j in range(jn):
            dst[8 * j:8 * j + 8, 128 * q:128 * (q + 1)] = src[pl.ds(j, 8, stride=jn), :]


def _from_segments(dst, src, q, jn, dtype):
    for s in range(8):
        dst[s * jn:(s + 1) * jn, 128 * q:128 * (q + 1)] = src[q, pl.ds(s, jn, stride=8), :].astype(dtype)


def _fill_rows8(dst_re, dst_im, v_re, v_im):
    for j in range(v_re.shape[0]):
        dst_re[8 * j:8 * j + 8, :] = jnp.broadcast_to(v_re[j:j + 1, :], (8, v_re.shape[1]))
        dst_im[8 * j:8 * j + 8, :] = jnp.broadcast_to(v_im[j:j + 1, :], (8, v_im.shape[1]))


def _cmul(ar, ai, br, bi):
    return ar * br - ai * bi, ar * bi + ai * br


def _cmul_conj(ar, ai, br, bi):
    return ar * br + ai * bi, ar * bi - ai * br


def _ssm_fwd(name, uin, lrdt, ang, bb_re, bb_im, cc_re, cc_im, d_skip, sw):
    t = uin.shape[0]
    gn = lrdt.shape[1]
    lc = _ssm_chunk(t)
    jn = lc // 8
    ub, sb = sw // SSM_BLOCKS, gn // SSM_BLOCKS
    nq = sw // 128
    assert ub == 128 and nq == SSM_BLOCKS

    def body(u_ref, lrdt_ref, ang_ref, bbr_ref, bbi_ref, ccr_ref, cci_ref, d_ref, y_ref, ge_ref, sre_ref, sim_ref,
             p_re, p_im, a_re, a_im, up, yp, cst_re, cst_im, car_re, car_im):
        i = pl.program_id(0)
        lrdt_v, ang_v = lrdt_ref[...], ang_ref[...]

        @pl.when(i == 0)
        def _():
            k = (lax.broadcasted_iota(jnp.int32, (jn, 1), 0) + 1).astype(F32)
            _fill_rows8(p_re, p_im, *_pow_rows(lrdt_v, ang_v, k))
            car_re[...] = jnp.zeros_like(car_re)
            car_im[...] = jnp.zeros_like(car_im)

        for q in range(nq):
            yp[q] = u_ref[:, 128 * q:128 * (q + 1)].astype(F32)
        _to_segments(up, [yp.at[q] for q in range(nq)], jn)
        u = up[...]
        ubf = u.astype(BF16)
        for q in range(SSM_BLOCKS):
            uq = ubf[:, q * ub:(q + 1) * ub]
            a_re[:, q * sb:(q + 1) * sb] = jnp.dot(uq, bbr_ref[q], preferred_element_type=F32)
            a_im[:, q * sb:(q + 1) * sb] = jnp.dot(uq, bbi_ref[q], preferred_element_type=F32)
        a1r, a1i = _pow_rows(lrdt_v, ang_v, 1.0)
        ajr, aji = _pow_rows(lrdt_v, ang_v, float(jn))
        for q in range(0, SSM_BLOCKS, 2):
            cols = slice(q * sb, (q + 2) * sb)
            ar8 = jnp.broadcast_to(a1r[:, cols], (8, 2 * sb))
            ai8 = jnp.broadcast_to(a1i[:, cols], (8, 2 * sb))

            def step(j, carry, cols=cols, ar8=ar8, ai8=ai8):
                sr, si = carry
                rows = pl.ds(pl.multiple_of(j * 8, 8), 8)
                mr, mi = _cmul(ar8, ai8, sr, si)
                nr, ni = mr + a_re[rows, cols], mi + a_im[rows, cols]
                a_re[rows, cols] = nr
                a_im[rows, cols] = ni
                return nr, ni

            lax.fori_loop(1, jn, step, (a_re[0:8, cols], a_im[0:8, cols]), unroll=4)
        er, ei = a_re[lc - 8:lc, :], a_im[lc - 8:lc, :]
        gr, gi = car_re[...], car_im[...]
        for s in range(8):
            cst_re[s:s + 1, :] = gr
            cst_im[s:s + 1, :] = gi
            mr, mi = _cmul(ajr, aji, gr, gi)
            gr, gi = mr + er[s:s + 1, :], mi + ei[s:s + 1, :]
        car_re[...] = gr
        car_im[...] = gi
        for q in range(SSM_BLOCKS):
            cols = slice(q * sb, (q + 1) * sb)
            cr = jnp.tile(cst_re[:, cols], (jn, 1))
            ci = jnp.tile(cst_im[:, cols], (jn, 1))
            mr, mi = _cmul(p_re[:, cols], p_im[:, cols], cr, ci)
            srb, sib = (a_re[:, cols] + mr).astype(BF16), (a_im[:, cols] + mi).astype(BF16)
            sre_ref[:, cols] = srb
            sim_ref[:, cols] = sib
            ycols = slice(q * ub, (q + 1) * ub)
            y = (jnp.dot(srb, ccr_ref[q], preferred_element_type=F32) - jnp.dot(sib, cci_ref[q], preferred_element_type=F32)
                 + d_ref[:, ycols] * u[:, ycols])
            yp[q] = y
            _from_segments(y_ref, yp, q, jn, F32)
            yt = y_ref[:, ycols]
            ge_ref[:, ycols] = (0.5 * yt * (1.0 + lax.erf(yt * (1.0 / math.sqrt(2.0))))).astype(BF16)

    row = lambda i: (0, 0)
    blk3 = lambda i: (0, 0, 0)
    return _pcall(
        body, name=name, grid=(t // lc,),
        in_specs=[pl.BlockSpec((lc, sw), lambda i: (i, 1)), pl.BlockSpec((1, gn), row), pl.BlockSpec((1, gn), row),
                  pl.BlockSpec(bb_re.shape, blk3), pl.BlockSpec(bb_im.shape, blk3),
                  pl.BlockSpec(cc_re.shape, blk3), pl.BlockSpec(cc_im.shape, blk3), pl.BlockSpec((1, sw), row)],
        out_specs=[pl.BlockSpec((lc, sw), lambda i: (i, 0)), pl.BlockSpec((lc, sw), lambda i: (i, 0)),
                   pl.BlockSpec((lc, gn), lambda i: (i, 0)), pl.BlockSpec((lc, gn), lambda i: (i, 0))],
        out_shape=[jax.ShapeDtypeStruct((t, sw), F32), jax.ShapeDtypeStruct((t, sw), BF16),
                   jax.ShapeDtypeStruct((t, gn), BF16), jax.ShapeDtypeStruct((t, gn), BF16)],
        scratch_shapes=[pltpu.VMEM((lc, gn), F32), pltpu.VMEM((lc, gn), F32), pltpu.VMEM((lc, gn), F32), pltpu.VMEM((lc, gn), F32),
                        pltpu.VMEM((lc, sw), F32), pltpu.VMEM((nq, lc, 128), F32),
                        pltpu.VMEM((8, gn), F32), pltpu.VMEM((8, gn), F32), pltpu.VMEM((1, gn), F32), pltpu.VMEM((1, gn), F32)],
        compiler_params=_params("arbitrary"))(uin, lrdt, ang, bb_re, bb_im, cc_re, cc_im, d_skip)


def _ssm_bwd(name, dge, y, uin, s_re, s_im, lrdt, ang, bbt_re, bbt_im, cct_re, cct_im, d_skip, sw, duin):
    t = uin.shape[0]
    gn = lrdt.shape[1]
    lc = _ssm_chunk(t)
    nb = t // lc
    jn = lc // 8
    ub, sb = sw // SSM_BLOCKS, gn // SSM_BLOCKS
    nq = sw // 128
    tail = 16

    def body(dge_ref, y_ref, u_ref, sre_ref, sim_ref, tre_ref, tim_ref, lrdt_ref, ang_ref, btr_ref, bti_ref, ctr_ref, cti_ref,
             d_ref, duin_ref, du_ref, dar_ref, dai_ref, dbr_ref, dbi_ref, dcr_ref, dci_ref, dd_ref,
             q_re, q_im, a_re, a_im, dyp, up, dys, us, dup, cst_re, cst_im, sp_re, sp_im, car_re, car_im):
        i = pl.program_id(0)
        lrdt_v, ang_v = lrdt_ref[...], ang_ref[...]

        @pl.when(i == 0)
        def _():
            k = (jn - lax.broadcasted_iota(jnp.int32, (jn, 1), 0)).astype(F32)
            _fill_rows8(q_re, q_im, *_pow_rows(lrdt_v, ang_v, k))
            car_re[...] = jnp.zeros_like(car_re)
            car_im[...] = jnp.zeros_like(car_im)
            for r in (dar_ref, dai_ref, dbr_ref, dbi_ref, dcr_ref, dci_ref, dd_ref):
                r[...] = jnp.zeros_like(r)

        yv = y_ref[...]
        ut = u_ref[...].astype(F32)
        cdf =0.5 * (1.0 + lax.erf(yv * (1.0 / math.sqrt(2.0))))
        pdf = jnp.exp(-0.5 * yv * yv) * (1.0 / math.sqrt(2.0 * math.pi))
        dyt = dge_ref[...].astype(F32) * (cdf + yv * pdf)
        dd_ref[...] += _colsum(dyt * ut)
        for q in range(nq):
            dys[q] = dyt[:, 128 * q:128 * (q + 1)]
            us[q] = ut[:, 128 * q:128 * (q + 1)]
        _to_segments(dyp, [dys.at[q] for q in range(nq)], jn)
        _to_segments(up, [us.at[q] for q in range(nq)], jn)
        dy = dyp[...]
        u = up[...]
        dyb = dy.astype(BF16)
        ubf = u.astype(BF16)
        for q in range(SSM_BLOCKS):
            dq = dyb[:, q * ub:(q + 1) * ub]
            a_re[:, q * sb:(q + 1) * sb] = jnp.dot(dq, ctr_ref[q], preferred_element_type=F32)
            a_im[:, q * sb:(q + 1) * sb] = -jnp.dot(dq, cti_ref[q], preferred_element_type=F32)
        a1r, a1i = _pow_rows(lrdt_v, ang_v, 1.0)
        ajr, aji = _pow_rows(lrdt_v, ang_v, float(jn))
        for q in range(0, SSM_BLOCKS, 2):
            cols = slice(q * sb, (q + 2) * sb)
            ar8 = jnp.broadcast_to(a1r[:, cols], (8, 2 * sb))
            ai8 = jnp.broadcast_to(a1i[:, cols], (8, 2 * sb))

            def step(jj, carry, cols=cols, ar8=ar8, ai8=ai8):
                sr, si = carry
                rows = pl.ds(pl.multiple_of((jn - 2 - jj) * 8, 8), 8)
                mr, mi = _cmul_conj(ar8, ai8, sr, si)
                nr, ni = mr + a_re[rows, cols], mi + a_im[rows, cols]
                a_re[rows, cols] = nr
                a_im[rows, cols] = ni
                return nr, ni

            lax.fori_loop(0, jn - 1, step, (a_re[lc - 8:lc, cols], a_im[lc - 8:lc, cols]), unroll=4)
        er, ei = a_re[0:8, :], a_im[0:8, :]
        hr, hi = car_re[...], car_im[...]
        for s in range(7, -1, -1):
            cst_re[s:s + 1, :] = hr
            cst_im[s:s + 1, :] = hi
            mr, mi = _cmul_conj(ajr, aji, hr, hi)
            hr, hi = mr + er[s:s + 1, :], mi + ei[s:s + 1, :]
        car_re[...] = hr
        car_im[...] = hi
        first = (i == nb - 1).astype(F32)
        sp_re[0:tail, :] = tre_ref[...].astype(F32) * (1.0 - first)
        sp_im[0:tail, :] = tim_ref[...].astype(F32) * (1.0 - first)
        sp_re[tail:tail + 8, :] = sre_ref[lc - tail:lc, :].astype(F32)[tail - 8:tail]
        sp_im[tail:tail + 8, :] = sim_ref[lc - tail:lc, :].astype(F32)[tail - 8:tail]
        tn_dims = (((0,), (0,)), ((), ()))
        for q in range(SSM_BLOCKS):
            cols = slice(q * sb, (q + 1) * sb)
            ycols = slice(q * ub, (q + 1) * ub)
            cr = jnp.tile(cst_re[:, cols], (jn, 1))
            ci = jnp.tile(cst_im[:, cols], (jn, 1))
            mr, mi = _cmul_conj(q_re[:, cols], q_im[:, cols], cr, ci)
            lam_r, lam_i = a_re[:, cols] + mr, a_im[:, cols] + mi
            s_r, s_i = sre_ref[:, cols], sim_ref[:, cols]
            p0r, p0i = sp_re[tail - 1:tail + 7, cols], sp_im[tail - 1:tail + 7, cols]
            l0r, l0i, l1r, l1i = lam_r[0:8], lam_i[0:8], lam_r[8:lc], lam_i[8:lc]
            pvr, pvi = s_r.astype(F32)[0:lc - 8], s_i.astype(F32)[0:lc - 8]
            dar_ref[:, cols] += _colsum(l1r * pvr + l1i * pvi) + _colsum(l0r * p0r + l0i * p0i)
            dai_ref[:, cols] += _colsum(l1i * pvr - l1r * pvi) + _colsum(l0i * p0r - l0r * p0i)
            lrb, lib = lam_r.astype(BF16), lam_i.astype(BF16)
            dup[q] = (jnp.dot(lrb, btr_ref[q], preferred_element_type=F32)
                      + jnp.dot(lib, bti_ref[q], preferred_element_type=F32) + d_ref[:, ycols] * dy[:, ycols])
            _from_segments(du_ref, dup, q, jn, BF16)
            uq = ubf[:, ycols]
            dbr_ref[q] += lax.dot_general(uq, lrb, tn_dims, preferred_element_type=F32)
            dbi_ref[q] += lax.dot_general(uq, lib, tn_dims, preferred_element_type=F32)
            dq = dyb[:, ycols]
            dcr_ref[q] += lax.dot_general(s_r.astype(BF16), dq, tn_dims, preferred_element_type=F32)
            dci_ref[q] -= lax.dot_general(s_i.astype(BF16), dq, tn_dims, preferred_element_type=F32)

    rev = lambda i: (nb - 1 - i, 0)
    tailmap = lambda i: (jnp.maximum((nb - 1 - i) * (lc // tail) - 1, 0), 0)
    row = lambda i: (0, 0)
    blk3 = lambda i: (0, 0, 0)
    return _pcall(
        body, name=name, grid=(nb,),
        in_specs=[pl.BlockSpec((lc, sw), rev), pl.BlockSpec((lc, sw), rev), pl.BlockSpec((lc, sw), lambda i: (nb - 1 - i, 1)),
                  pl.BlockSpec((lc, gn), rev), pl.BlockSpec((lc, gn), rev),
                  pl.BlockSpec((tail, gn), tailmap), pl.BlockSpec((tail, gn), tailmap),
                  pl.BlockSpec((1, gn), row), pl.BlockSpec((1, gn), row),
                  pl.BlockSpec(bbt_re.shape, blk3), pl.BlockSpec(bbt_im.shape, blk3),
                  pl.BlockSpec(cct_re.shape, blk3), pl.BlockSpec(cct_im.shape, blk3), pl.BlockSpec((1, sw), row), ANY_SPEC],
        out_specs=[pl.BlockSpec((lc, sw), lambda i: (nb - 1 - i, 1)), pl.BlockSpec((1, gn), row), pl.BlockSpec((1, gn), row),
                   pl.BlockSpec((SSM_BLOCKS, ub, sb), blk3), pl.BlockSpec((SSM_BLOCKS, ub, sb), blk3),
                   pl.BlockSpec((SSM_BLOCKS, sb, ub), blk3), pl.BlockSpec((SSM_BLOCKS, sb, ub), blk3),
                   pl.BlockSpec((1, sw), row)],
        out_shape=[jax.ShapeDtypeStruct(duin.shape, BF16), jax.ShapeDtypeStruct((1, gn), F32), jax.ShapeDtypeStruct((1, gn), F32),
                   jax.ShapeDtypeStruct((SSM_BLOCKS, ub, sb), F32), jax.ShapeDtypeStruct((SSM_BLOCKS, ub, sb), F32),
                   jax.ShapeDtypeStruct((SSM_BLOCKS, sb, ub), F32), jax.ShapeDtypeStruct((SSM_BLOCKS, sb, ub), F32),
                   jax.ShapeDtypeStruct((1, sw), F32)],
        scratch_shapes=[pltpu.VMEM((lc, gn), F32), pltpu.VMEM((lc, gn), F32), pltpu.VMEM((lc, gn), F32), pltpu.VMEM((lc, gn), F32),
                        pltpu.VMEM((lc, sw), F32), pltpu.VMEM((lc, sw), F32),
                        pltpu.VMEM((nq, lc, 128), F32), pltpu.VMEM((nq, lc, 128), F32), pltpu.VMEM((nq, lc, 128), F32),
                        pltpu.VMEM((8, gn), F32), pltpu.VMEM((8, gn), F32),
                        pltpu.VMEM((tail + 8, gn), F32), pltpu.VMEM((tail + 8, gn), F32),
                        pltpu.VMEM((1, gn), F32), pltpu.VMEM((1, gn), F32)],
        input_output_aliases={14: 0},
        compiler_params=_params("arbitrary"))(dge, y, uin, s_re, s_im, s_re, s_im, lrdt, ang,
                                               bbt_re, bbt_im, cct_re, cct_im, d_skip, duin)


def _blockdiag_b(bb, sw):
    gpb = (sw // SSM_GROUP) // SSM_BLOCKS
    b4 = bb.reshape(SSM_BLOCKS, gpb, SSM_STATE, SSM_GROUP)
    eye = jnp.eye(gpb, dtype=bb.dtype)
    out = jnp.einsum('qgnh,gk->qghkn', b4, eye)
    return out.reshape(SSM_BLOCKS, gpb * SSM_GROUP, gpb * SSM_STATE)


def _blockdiag_c(cc, sw):
    gpb = (sw // SSM_GROUP) // SSM_BLOCKS
    c4 = cc.reshape(SSM_BLOCKS, gpb, SSM_GROUP, SSM_STATE)
    eye = jnp.eye(gpb, dtype=cc.dtype)
    out = jnp.einsum('qghn,gk->qgnkh', c4, eye)
    return out.reshape(SSM_BLOCKS, gpb * SSM_STATE, gpb * SSM_GROUP)


def _diag_of_b(dbb, sw):
    gpb = (sw // SSM_GROUP) // SSM_BLOCKS
    d5 = dbb.reshape(SSM_BLOCKS, gpb, SSM_GROUP, gpb, SSM_STATE)
    return jnp.einsum('qghgn->qgnh', d5).reshape(SSM_BLOCKS * gpb * SSM_STATE, SSM_GROUP)


def _diag_of_c(dcc, sw):
    gpb = (sw // SSM_GROUP) // SSM_BLOCKS
    d5 = dcc.reshape(SSM_BLOCKS, gpb, SSM_STATE, gpb, SSM_GROUP)
    return jnp.einsum('qgngh->qghn', d5).reshape(SSM_BLOCKS * gpb, SSM_GROUP, SSM_STATE)


def _ada_fwd(name, c_all, w_sh, b_sh):
    nb, d = c_all.shape
    ncol = w_sh.shape[1]
    tn = _pick(ncol, 768)

    def body(c_ref, w_ref, b_ref, o_ref):
        cv = c_ref[...]
        sil = cv * _sigmoid(cv)
        o_ref[...] = jnp.dot(sil, w_ref[...], preferred_element_type=F32, precision=lax.Precision.HIGHEST) + b_ref[...]

    return _pcall(body, name=name, grid=(ncol // tn,),
                  in_specs=[pl.BlockSpec((nb, d), lambda j: (0, 0)), pl.BlockSpec((d, tn), lambda j: (0, j)),
                            pl.BlockSpec((1, tn), lambda j: (0, j))],
                  out_specs=pl.BlockSpec((nb, tn), lambda j: (0, j)),
                  out_shape=jax.ShapeDtypeStruct((nb, ncol), F32), compiler_params=_params("parallel"))(c_all, w_sh, b_sh)


def _ada_bwd(name, c_all, dmod_sh):
    nb, d = c_all.shape
    ncol = dmod_sh.shape[1]
    tn = _pick(ncol, 768)

    def body(c_ref, g_ref, o_ref):
        cv = c_ref[...]
        sil = cv * _sigmoid(cv)
        o_ref[...] = lax.dot_general(sil, g_ref[...], (((0,), (0,)), ((), ())), preferred_element_type=F32,
                                     precision=lax.Precision.HIGHEST)

    return _pcall(body, name=name, grid=(ncol // tn,),
                  in_specs=[pl.BlockSpec((nb, d), lambda j: (0, 0)), pl.BlockSpec((nb, tn), lambda j: (0, j))],
                  out_specs=pl.BlockSpec((d, tn), lambda j: (0, j)),
                  out_shape=jax.ShapeDtypeStruct((d, ncol), F32), compiler_params=_params("parallel"))(c_all, dmod_sh)


def _place():
    return lax.axis_index("x"), lax.axis_index("y"), lax.axis_index("c")


def _allgather_small(name, blk, deps=()):
    m_per, n = blk.shape

    def body(x_ref, *rest):
        out_ref, send_sems, recv_sems, local_sem = rest[len(deps):]
        x, y, c = _place()
        me, sibling = (x, y, c), (x, y, 1 - c)
        chips = [(1 - x, y), (x, 1 - y), (1 - x, 1 - y)]

        def rows(px, py, pc):
            return out_ref.at[pl.ds((4 * px + 2 * py + pc) * m_per, m_per), :]

        def copy(k, block, to, src=None):
            return pltpu.make_async_remote_copy(
                src_ref=rows(*block) if src is None else src, dst_ref=rows(*block),
                send_sem=send_sems.at[k], recv_sem=recv_sems.at[k], device_id=to, device_id_type=MESH)

        mine = pltpu.make_async_copy(x_ref, rows(*me), local_sem)
        mine.start()
        first = [copy(0, me, sibling, src=x_ref)]
        first += [copy(1 + j, me, (*chip, c), src=x_ref) for j, chip in enumerate(chips)]
        for cp in first:
            cp.start()
        passed = [copy(4 + j, (*chip, c), sibling) for j, chip in enumerate(chips)]
        for j, chip in enumerate(chips):
            copy(1 + j, (*chip, c), me).wait_recv()
            passed[j].start()
        copy(0, sibling, me).wait_recv()
        for j, chip in enumerate(chips):
            copy(4 + j, (*chip, 1 - c), me).wait_recv()
        for cp in first + passed:
            cp.wait_send()
        mine.wait()

    return _pcall(body, name=name, out_shape=jax.ShapeDtypeStruct((N_DEV * m_per, n), blk.dtype),
                  in_specs=[pl.BlockSpec(memory_space=pltpu.VMEM)] + [ANY_SPEC] * len(deps),
                  out_specs=pl.BlockSpec(memory_space=pltpu.VMEM),
                  scratch_shapes=[pltpu.SemaphoreType.DMA((7,)), pltpu.SemaphoreType.DMA((7,)), pltpu.SemaphoreType.DMA],
                  compiler_params=pltpu.CompilerParams(vmem_limit_bytes=VMEM_LIMIT))(blk, *deps)


def _other_chips(x, y):
    return [(1 - x, y), (x, 1 - y), (1 - x, 1 - y)]


def _place_shards(shards, s_me):
    return [lax.dynamic_update_slice(lax.empty((N_CHIPS,) + s.shape, s.dtype), s[None], (s_me, 0, 0)) for s in shards]


def _ag_copy(src, land, send, recv, wi, j, chip, x, y, c, tc, both):
    hr = src.shape[0] // 2
    half = pl.ds(pl.multiple_of(c * hr, 16), hr)
    k = 3 * wi + j
    return pltpu.make_async_remote_copy(
        src_ref=src.at[half, :], dst_ref=land.at[2 * x + y, half, :],
        send_sem=send.at[2 * k + tc if both else k], recv_sem=recv.at[2 * k + c if both else k],
        device_id=(chip[0], chip[1], tc), device_id_type=MESH)


def _ag_targets(c, both):
    return (0, 1) if both else (c,)


def _ag_start(name, shards, lands, groups, direct, deps=()):
    nw, ng, nd = len(shards), len(groups), len(deps)

    def body(*refs):
        src, land = refs[:nw], refs[nw:2 * nw]
        sems = refs[2 * nw + nd:2 * nw + nd + 2 * ng]
        token = refs[-1]
        x, y, c = _place()
        for gi, grp in enumerate(groups):
            for wi, w in enumerate(grp):
                for j, chip in enumerate(_other_chips(x, y)):
                    for tc in _ag_targets(c, direct[gi]):
                        _ag_copy(src[w], land[w], sems[2 * gi], sems[2 * gi + 1], wi, j, chip, x, y, c, tc, direct[gi]).start()
        token[...] = jnp.zeros_like(token)

    sem_shapes = []
    for gi, grp in enumerate(groups):
        sem_shapes += [pltpu.SemaphoreType.DMA(((6 if direct[gi] else 3) * len(grp),))] * 2
    out_shape = sem_shapes + [pltpu.HBM(s.shape, s.dtype) for s in shards] + [pltpu.HBM(l.shape, l.dtype) for l in lands]
    out_shape += [jax.ShapeDtypeStruct((8, 128), F32)]
    res = _pcall(body, name=name, out_shape=out_shape, in_specs=[HBM_SPEC] * (2 * nw) + [ANY_SPEC] * nd,
                 out_specs=[SEM_SPEC] * (2 * ng) + [HBM_SPEC] * (2 * nw) + [pl.BlockSpec(memory_space=pltpu.VMEM)],
                 input_output_aliases={i: 2 * ng + i for i in range(2 * nw)},
                 compiler_params=pltpu.CompilerParams(has_side_effects=EFFECT))(
                     *[_hbm(s) for s in shards], *[_hbm(l) for l in lands], *deps)
    sems = [(res[2 * gi], res[2 * gi + 1]) for gi in range(ng)]
    return sems, list(res[2 * ng:2 * ng + nw]), list(res[2 * ng + nw:2 * ng + 2 * nw]), res[-1]


def _ag_wait(name, shards, lands, send, recv, both, after):
    n = len(shards)

    def body(*refs):
        src, land = refs[:n], refs[n:2 * n]
        send_sem, recv_sem = refs[2 * n], refs[2 * n + 1]
        x, y, c = _place()
        for wi in range(n):
            for j, chip in enumerate(_other_chips(x, y)):
                for tc in _ag_targets(c, both):
                    _ag_copy(src[wi], land[wi], send_sem, recv_sem, wi, j, chip, x, y, c, tc, both).wait_send()
                    _ag_copy(src[wi], land[wi], send_sem, recv_sem, wi, j, chip, chip[0], chip[1], tc, c, both).wait_recv()

    res = _pcall(body, name=name, out_shape=[pltpu.HBM(a.shape, a.dtype) for a in list(shards) + list(lands)],
                 in_specs=[HBM_SPEC] * (2 * n) + [SEM_SPEC, SEM_SPEC] + [ANY_SPEC] * len(after), out_specs=[HBM_SPEC] * (2 * n),
                 input_output_aliases={i: i for i in range(2 * n)},
                 compiler_params=pltpu.CompilerParams(has_side_effects=EFFECT))(*shards, *lands, send, recv, *after)
    return list(res[n:])


def _ag_forward(name, lands):
    n = len(lands)

    def body(*refs):
        out = refs[n:2 * n]
        send, recv = refs[2 * n], refs[2 * n + 1]
        x, y, c = _place()
        sib = (x, y, 1 - c)
        cps = []
        for wi in range(n):
            hr = out[wi].shape[1] // 2
            for j, (cx, cy) in enumerate(_other_chips(x, y)):
                got = out[wi].at[2 * cx + cy, pl.ds(pl.multiple_of(c * hr, 16), hr), :]
                cp = pltpu.make_async_remote_copy(src_ref=got, dst_ref=got, send_sem=send.at[3 * wi + j], recv_sem=recv.at[3 * wi + j],
                                                  device_id=sib, device_id_type=MESH)
                cp.start()
                cps.append(cp)
        for wi in range(n):
            hr = out[wi].shape[1] // 2
            for j, (cx, cy) in enumerate(_other_chips(x, y)):
                got = out[wi].at[2 * cx + cy, pl.ds(pl.multiple_of((1 - c) * hr, 16), hr), :]
                pltpu.make_async_remote_copy(src_ref=got, dst_ref=got, send_sem=send.at[3 * wi + j], recv_sem=recv.at[3 * wi + j],
                                             device_id=sib, device_id_type=MESH).wait_recv()
        for cp in cps:
            cp.wait_send()

    res = _pcall(body, name=name, out_shape=[jax.ShapeDtypeStruct(l.shape, l.dtype) for l in lands],
                 in_specs=[ANY_SPEC] * n, out_specs=[ANY_SPEC] * n, input_output_aliases={i: i for i in range(n)},
                 scratch_shapes=[pltpu.SemaphoreType.DMA((3 * n,)), pltpu.SemaphoreType.DMA((3 * n,))])(*lands)
    return list(res)


def _peers(x, y, c):
    offs = [(dx, dy, dc) for dx in (0, 1) for dy in (0, 1) for dc in (0, 1)][1:]
    return [(1 - x if dx else x, 1 - y if dy else y, 1 - c if dc else c) for dx, dy, dc in offs]


def _rs_copy(g_ref, land_ref, send, recv, wi, k, to, sender):
    hr = g_ref.shape[1] // 2
    return pltpu.make_async_remote_copy(
        src_ref=g_ref.at[2 * to[0] + to[1], pl.ds(pl.multiple_of(to[2] * hr, 16), hr), :], dst_ref=land_ref.at[sender],
        send_sem=send.at[7 * wi + k], recv_sem=recv.at[7 * wi + k], device_id=to, device_id_type=MESH)


def _rs_start(name, gs):
    n = len(gs)
    lands = [lax.empty((N_DEV, g.shape[1] // 2, g.shape[2]), BF16) for g in gs]

    def body(*refs):
        g, land = refs[:n], refs[n:2 * n]
        send, recv = refs[2 * n], refs[2 * n + 1]
        token = refs[-1]
        x, y, c = _place()
        me = 4 * x + 2 * y + c
        for wi in range(n):
            for k, to in enumerate(_peers(x, y, c)):
                _rs_copy(g[wi], land[wi], send, recv, wi, k, to, me).start()
        token[...] = jnp.zeros_like(token)

    out_shape = [pltpu.SemaphoreType.DMA((7 * n,))] * 2 + [pltpu.HBM(a.shape, a.dtype) for a in list(gs) + lands]
    out_shape += [jax.ShapeDtypeStruct((8, 128), F32)]
    res = _pcall(body, name=name, out_shape=out_shape, in_specs=[HBM_SPEC] * (2 * n),
                 out_specs=[SEM_SPEC] * 2 + [HBM_SPEC] * (2 * n) + [pl.BlockSpec(memory_space=pltpu.VMEM)],
                 input_output_aliases={i: 2 + i for i in range(2 * n)},
                 compiler_params=pltpu.CompilerParams(has_side_effects=EFFECT))(
                     *[_hbm(a) for a in gs], *[_hbm(a) for a in lands])
    return res[0], res[1], list(res[2:2 + n]), list(res[2 + n:2 + 2 * n]), res[-1]


def _rs_wait(name, gs, lands, send, recv, after):
    n = len(gs)

    def body(*refs):
        g, land = refs[:n], refs[n:2 * n]
        send_sem, recv_sem = refs[2 * n], refs[2 * n + 1]
        x, y, c = _place()
        me = 4 * x + 2 * y + c
        for wi in range(n):
            for k, to in enumerate(_peers(x, y, c)):
                _rs_copy(g[wi], land[wi], send_sem, recv_sem, wi, k, to, me).wait_send()
                _rs_copy(g[wi], land[wi], send_sem, recv_sem, wi, k, (x, y, c), 4 * to[0] + 2 * to[1] + to[2]).wait_recv()

    res = _pcall(body, name=name, out_shape=[pltpu.HBM(a.shape, a.dtype) for a in list(gs) + list(lands)],
                 in_specs=[HBM_SPEC] * (2 * n) + [SEM_SPEC, SEM_SPEC, ANY_SPEC], out_specs=[HBM_SPEC] * (2 * n),
                 input_output_aliases={i: i for i in range(2 * n)},
                 compiler_params=pltpu.CompilerParams(has_side_effects=EFFECT))(*gs, *lands, send, recv, after)
    return list(res[:n]), list(res[n:])


def _bc_copy(blk_ref, land_ref, send, recv, k, to, slot):
    return pltpu.make_async_remote_copy(src_ref=blk_ref, dst_ref=land_ref.at[slot], send_sem=send.at[k], recv_sem=recv.at[k],
                                        device_id=to, device_id_type=MESH)


def _bcast_start(name, blk):
    land = lax.empty((N_DEV,) + blk.shape, blk.dtype)

    def body(blk_ref, land_ref, send, recv, blk_thru, land_thru, token):
        x, y, c = _place()
        for k, to in enumerate(_peers(x, y, c)):
            _bc_copy(blk_ref, land_ref, send, recv, k, to, 4 * x + 2 * y + c).start()
        token[...] = jnp.zeros_like(token)

    return _pcall(body, name=name,
                  out_shape=[pltpu.SemaphoreType.DMA((7,)), pltpu.SemaphoreType.DMA((7,)), pltpu.HBM(blk.shape, blk.dtype),
                             pltpu.HBM(land.shape, land.dtype), jax.ShapeDtypeStruct((8, 128), F32)],
                  in_specs=[HBM_SPEC, HBM_SPEC], out_specs=[SEM_SPEC, SEM_SPEC, HBM_SPEC, HBM_SPEC, pl.BlockSpec(memory_space=pltpu.VMEM)],
                  input_output_aliases={0: 2, 1: 3}, compiler_params=pltpu.CompilerParams(has_side_effects=EFFECT))(_hbm(blk), _hbm(land))


def _bcast_wait(name, blk, land, send, recv, after):
    def body(blk_ref, land_ref, send_sem, recv_sem, after_ref, blk_thru, land_thru):
        x, y, c = _place()
        for k, to in enumerate(_peers(x, y, c)):
            _bc_copy(blk_ref, land_ref, send_sem, recv_sem, k, to, 4 * x + 2 * y + c).wait_send()
            _bc_copy(blk_ref, land_ref, send_sem, recv_sem, k, to, 4 * to[0] + 2 * to[1] + to[2]).wait_recv()

    return _pcall(body, name=name, out_shape=[pltpu.HBM(blk.shape, blk.dtype), pltpu.HBM(land.shape, land.dtype)],
                  in_specs=[HBM_SPEC, HBM_SPEC, SEM_SPEC, SEM_SPEC, ANY_SPEC], out_specs=[HBM_SPEC, HBM_SPEC],
                  input_output_aliases={0: 0, 1: 1}, compiler_params=pltpu.CompilerParams(has_side_effects=EFFECT))(
                      blk, land, send, recv, after)[1]


def _rs_sum(name, gs, lands):
    n = len(gs)

    def body(*refs):
        g_refs, land_refs, out_refs = refs[:n], refs[n:2 * n], refs[2 * n:3 * n]
        recvs = refs[3 * n:4 * n]
        local_sems, sib_send, sib_recv = refs[4 * n:]
        x, y, c = _place()
        me = 4 * x + 2 * y + c
        cps = []
        for wi in range(n):
            hr = g_refs[wi].shape[1] // 2
            own = g_refs[wi].at[2 * x + y, pl.ds(pl.multiple_of(c * hr, 16), hr), :]
            cps.append(pltpu.make_async_copy(own, recvs[wi].at[me], local_sems.at[8 * wi + 7]))
            for k, (tx, ty, tc) in enumerate(_peers(x, y, c)):
                slot = 4 * tx + 2 * ty + tc
                cps.append(pltpu.make_async_copy(land_refs[wi].at[slot], recvs[wi].at[slot], local_sems.at[8 * wi + k]))
        for cp in cps:
            cp.start()
        sibs = []
        for wi in range(n):
            hr = g_refs[wi].shape[1] // 2
            ch = _pick(hr, 64, 16)
            for cp in cps[8 * wi:8 * wi + 8]:
                cp.wait()
            base = pl.multiple_of(c * hr, 16)
            for r0 in range(0, hr, ch):
                acc = recvs[wi][0, r0:r0 + ch, :].astype(F32)
                for k in range(1, N_DEV):
                    acc = acc + recvs[wi][k, r0:r0 + ch, :].astype(F32)
                out_refs[wi][pl.ds(base + r0, ch), :] = acc
            half = out_refs[wi].at[pl.ds(base, hr), :]
            sib = pltpu.make_async_remote_copy(src_ref=half, dst_ref=half, send_sem=sib_send.at[wi], recv_sem=sib_recv.at[wi],
                                               device_id=(x, y, 1 - c), device_id_type=MESH)
            sib.start()
            sibs.append(sib)
        for wi in range(n):
            hr = g_refs[wi].shape[1] // 2
            other = out_refs[wi].at[pl.ds(pl.multiple_of((1 - c) * hr, 16), hr), :]
            pltpu.make_async_remote_copy(src_ref=other, dst_ref=other, send_sem=sib_send.at[wi], recv_sem=sib_recv.at[wi],
                                         device_id=(x, y, 1 - c), device_id_type=MESH).wait_recv()
            sibs[wi].wait_send()

    res = _pcall(
        body, name=name, out_shape=[jax.ShapeDtypeStruct(g.shape[1:], F32) for g in gs],
        in_specs=[ANY_SPEC] * (2 * n), out_specs=[pl.BlockSpec(memory_space=pltpu.VMEM)] * n,
        scratch_shapes=[pltpu.VMEM((N_DEV, g.shape[1] // 2, g.shape[2]), BF16) for g in gs]
        + [pltpu.SemaphoreType.DMA((8 * n,)), pltpu.SemaphoreType.DMA((n,)), pltpu.SemaphoreType.DMA((n,))],
        compiler_params=pltpu.CompilerParams(vmem_limit_bytes=VMEM_LIMIT))(*gs, *lands)
    return list(res)


def _gate_norm_epilogue(factor, x_in, gt, nxt, t, d):
    blk = (ROWS, d)
    full = lambda i, j: (i, j)
    rowv = lambda i, j: (0, j)

    def epi(acc, res, scale, *norm):
        x_new = res + (factor * scale) * acc
        if not norm:
            return x_new, acc
        gv, scv, shv = norm
        r = lax.rsqrt(jnp.mean(x_new * x_new, axis=-1, keepdims=True) + EPS)
        return x_new, acc, x_new * r * gv * (1.0 + scv) + shv

    ins = [(x_in, blk, full), (gt, (1, d), rowv)] + [(v, (1, d), rowv) for v in (nxt or ())]
    outs = [((t, d), F32, blk, full), ((t, d), BF16, blk, full)] + ([((t, d), BF16, blk, full)] if nxt else [])
    return epi, ins, outs


_FULL = lambda i, j: (i, j)
_ROWV = lambda i, j: (0, j)


def _glu_epilogue(b_glu, t, w):
    def epi(acc, bv):
        g = acc + bv
        return acc, g[:, :w] * _sigmoid(g[:, w:])

    return epi, [(b_glu, (1, 2 * w), _ROWV)], [((t, 2 * w), BF16, (ROWS, 2 * w), _FULL), ((t, w), BF16, (ROWS, w), _FULL)]


def _glu_bwd_epilogue(gv, b_glu, t, w):
    def epi(dsg, gvv, bv):
        g = gvv.astype(F32) + bv
        val, s = g[:, :w], _sigmoid(g[:, w:])
        dval, dgate = dsg * s, dsg * val * s * (1.0 - s)
        return [jnp.concatenate([dval, dgate], axis=1)], [jnp.concatenate([_colsum(dval), _colsum(dgate)], axis=1)]

    return epi, [(gv, (ROWS, 2 * w), _FULL), (b_glu, (1, 2 * w), _ROWV)], [((t, 2 * w), BF16, (ROWS, 2 * w), _FULL)], [2 * w]


def _gates_epilogue(y_pool, uin, t, d, pw):
    cb = (2 * pw) // d

    def epi(acc, yp, glp, gls):
        return acc, _sigmoid(glp.astype(F32)) * yp.astype(F32) + _sigmoid(gls.astype(F32)) * acc

    ins = [(y_pool, (ROWS, d), _FULL), (uin, (ROWS, d), lambda i, j: (i, cb)), (uin, (ROWS, d), lambda i, j: (i, cb + 1))]
    return epi, ins, [((t, d), BF16, (ROWS, d), _FULL)] * 2


def _gates_bwd_epilogue(y_pool, y_ssm, uin, t, d, pw):
    cb = (2 * pw) // d

    def epi(dm, yp, ys, glp, gls):
        sp, ss = _sigmoid(glp.astype(F32)), _sigmoid(gls.astype(F32))
        dgl = jnp.concatenate([dm * yp.astype(F32) * sp * (1.0 - sp), dm * ys.astype(F32) * ss * (1.0 - ss)], axis=1)
        return dm * sp, dm * ss, ("at", 2 * pw, dgl)

    ins = [(y_pool, (ROWS, d), _FULL), (y_ssm, (ROWS, d), _FULL),
           (uin, (ROWS, d), lambda i, j: (i, cb)), (uin, (ROWS, d), lambda i, j: (i, cb + 1))]
    wide = 2 * pw + 2 * d
    return epi, ins, [((t, d), BF16, (ROWS, d), _FULL)] * 2 + [((t, wide), BF16, (ROWS, wide), _FULL)]


def _loss_epilogue(x_in, gt, tgt, g_final, t, d):
    blk = (ROWS, d)
    full = lambda i, j: (i, j)
    rowv = lambda i, j: (0, j)

    def epi(acc, res, scale, tv, gv):
        xv = res + (0.5 * scale) * acc
        r = lax.rsqrt(jnp.mean(xv * xv, axis=-1, keepdims=True) + EPS)
        xr = xv * r
        e = xr * gv - tv
        loss_row = 0.5 * jnp.mean(e * e, axis=-1, keepdims=True)
        dout = e * (1.0 / d)
        gd = gv * dout
        dx = r * (gd - xr * jnp.mean(gd * xr, axis=-1, keepdims=True))
        fdx = 0.5 * dx
        return [dx, scale * fdx], [_colsum(dout * xr), _colsum(loss_row * jnp.ones((1, 128), F32)), _colsum(fdx * acc)]

    ins = [(x_in, blk, full), (gt, (1, d), rowv), (tgt, blk, full), (g_final, (1, d), rowv)]
    return epi, ins, [((t, d), F32, blk, full), ((t, d), BF16, blk, full)], [d, 128, d]


def _norm_bwd_epilogue(x, dres, g, sc, up, t, d):
    blk = (ROWS, d)
    full = lambda i, j: (i, j)
    rowv = lambda i, j: (0, j)

    def epi(dhv, xv, drv, gv, scv, *rest):
        r = lax.rsqrt(jnp.mean(xv * xv, axis=-1, keepdims=True) + EPS)
        xr = xv * r
        dn = dhv * (1.0 + scv)
        gd = gv * dn
        dx = drv + r * (gd - xr * jnp.mean(gd * xr, axis=-1, keepdims=True))
        outs, reds = [dx], [_colsum(dhv), _colsum(dhv * xr * gv), _colsum(dn * xr)]
        if up:
            yv, gtv = rest
            fdx = up[2] * dx
            outs.append(gtv * fdx)
            reds.append(_colsum(fdx * yv.astype(F32)))
        return outs, reds

    ins = [(x, blk, full), (dres, blk, full), (g, (1, d), rowv), (sc, (1, d), rowv)]
    ins += [(up[0], blk, full), (up[1], (1, d), rowv)] if up else []
    outs = [((t, d), F32, blk, full)] + ([((t, d), BF16, blk, full)] if up else [])
    return epi, ins, outs, [d] * (4 if up else 3)


def _ffn_in_act(name, h, w_in, tm=ROWS):
    t, d = h.shape
    ns = w_in.shape[2]
    tm = _pick(t, tm, 8)
    w4 = w_in.reshape(2, 2, d, ns)

    def body(h_ref, w_ref, ab_ref, act_ref):
        hv = h_ref[...]
        a = jnp.dot(hv, w_ref[0], preferred_element_type=F32)
        b = jnp.dot(hv, w_ref[1], preferred_element_type=F32)
        ab_ref[0] = a.astype(BF16)
        ab_ref[1] = b.astype(BF16)
        act_ref[...] = (a * _sigmoid(a) * b).astype(BF16)

    return _pcall(body, name=name, grid=(2, t // tm),
                  in_specs=[pl.BlockSpec((tm, d), lambda c, i: (i, 0)), pl.BlockSpec((2, None, d, ns), lambda c, i: (0, c, 0, 0))],
                  out_specs=[pl.BlockSpec((2, tm, ns), lambda c, i: (0, i, c)), pl.BlockSpec((tm, ns), lambda c, i: (i, c))],
                  out_shape=[jax.ShapeDtypeStruct((2, t, 2 * ns), BF16), jax.ShapeDtypeStruct((t, 2 * ns), BF16)],
                  compiler_params=_params("parallel", "parallel"))(h, w4)


def _dswiglu_epilogue(ab, t, f, tn):
    blk = (2, ROWS, tn)
    idx = lambda i, j: (0, i, j)

    def epi(dact, abv):
        a, b = abv[0].astype(F32), abv[1].astype(F32)
        s = _sigmoid(a)
        return (jnp.stack([dact * b * (s * (1.0 + a * (1.0 - s))), dact * (a * s)]),)

    return epi, [(ab, blk, idx)], [((2, t, f), BF16, blk, idx)]


def _ffn_fwd(tag, x, h, gt, get_w_in, get_w_out, nxt=None, loss=None):
    t, d = x.shape
    ab, act = _ffn_in_act(tag + "_in", h, get_w_in(h))
    if loss:
        epi, epi_ins, epi_outs, epi_reds = _loss_epilogue(x, gt, *loss, t, d)
        res = _mm(tag + "_out", act, get_w_out(act), tm=ROWS, tn=d, epi=epi, epi_ins=epi_ins, epi_outs=epi_outs,
                  epi_reds=epi_reds, b_resident=True)
        return res, (x, h, ab, act, None)
    epi, epi_ins, epi_outs = _gate_norm_epilogue(0.5, x, gt, nxt, t, d)
    res = _mm(tag + "_out", act, get_w_out(act), tm=ROWS, tn=d, epi=epi, epi_ins=epi_ins, epi_outs=epi_outs, b_resident=True)
    return res[0], res[2], (x, h, ab, act, res[1])


def _ffn_bwd(tag, dx_new, dy, saved, g, sc, w_in, w_out, start_rs, up, deps=()):
    x, h, ab, act, _ = saved
    d = x.shape[1]
    f = act.shape[1]
    dw_out = _mm(tag + "_dwout", act, dy, ta=True, tm=1408, tn=d, tk=2048, deps=deps)
    tok = start_rs("out", dw_out.reshape(N_CHIPS, f // N_CHIPS, d))
    epi, epi_ins, epi_outs = _dswiglu_epilogue(ab, x.shape[0], f, w_in.shape[2])
    dab = _mm(tag + "_dact", dy, w_out, tb=True, tm=ROWS, tn=w_in.shape[2], epi=epi, epi_ins=epi_ins, epi_outs=epi_outs,
              deps=(tok,), j_outer=True)
    dw_in = _mm(tag + "_dwin", h, dab, ta=True, out_stacked=True, b_halves=True, tm=d, tn=1408, tk=2048)
    tok = start_rs("in", dw_in)
    epi, epi_ins, epi_outs, epi_reds = _norm_bwd_epilogue(x, dx_new, g, sc, up, x.shape[0], d)
    res = _mm(tag + "_dh", dab, w_in, tb=True, b_stacked=True, a_halves=True, tn=d, tk=5632, deps=(tok,), epi=epi,
              epi_ins=epi_ins, epi_outs=epi_outs, epi_reds=epi_reds, b_resident=True)
    if up:
        return res
    return res[0], None, res[1], res[2], res[3], None


def _row(v):
    return v.reshape(1, -1)


def _pack(parts):
    cols = []
    for p in parts:
        flat = p.reshape(-1).astype(F32)
        padn = (-flat.shape[0]) % 128
        cols.append(jnp.pad(flat, (0, padn)) if padn else flat)
    flat = jnp.concatenate(cols)
    padn = (-flat.shape[0]) % 1024
    if padn:
        flat = jnp.pad(flat, (0, padn))
    return flat.reshape(-1, 128)


def _unpack(packed, shapes):
    flat = packed.reshape(-1)
    out, off = [], 0
    for s in shapes:
        n = math.prod(s)
        out.append(flat[off:off + n].reshape(s))
        off += n + ((-n) % 128)
    return out


SMALL = ['b_ada', 'g_ffn1', 'g_mix', 'pool_w', 'pool_b', 'pool_scale', 'ssm_lam_re_log', 'ssm_lam_im', 'ssm_log_dt',
         'ssm_b_re', 'ssm_b_im', 'ssm_c_re', 'ssm_c_im', 'ssm_d', 'b_glu', 'g_ffn2', 'g_final']
BIG = ['w_ffn1_in', 'w_ffn1_out', 'w_in', 'w_pool_up', 'w_glu', 'w_ssm_up', 'w_out', 'w_ffn2_in', 'w_ffn2_out']
AG_GROUPS = [[0], [1], [2, 3, 4, 5, 6], [7, 8]]
AG_DIRECT = [False, False, False, True]
SMALL_LATE = ['b_ada', 'g_ffn1']
RS_SUM_GROUPS = [[0, 1], [2, 3], [4, 5]]
WEIGHTS = ['w_ada', 'b_ada', 'g_ffn1', 'w_ffn1_in', 'w_ffn1_out', 'g_mix', 'w_in', 'pool_w', 'pool_b', 'pool_scale', 'w_pool_up',
           'ssm_lam_re_log', 'ssm_lam_im', 'ssm_log_dt', 'ssm_b_re', 'ssm_b_im', 'ssm_c_re', 'ssm_c_im', 'ssm_d', 'w_glu', 'b_glu',
           'w_ssm_up', 'w_out', 'g_ffn2', 'w_ffn2_in', 'w_ffn2_out', 'g_final']


def kernel(x, c, w_ada, b_ada, g_ffn1, w_ffn1_in, w_ffn1_out, g_mix, w_in, pool_w, pool_b, pool_scale, w_pool_up, ssm_lam_re_log, ssm_lam_im, ssm_log_dt, ssm_b_re, ssm_b_im, ssm_c_re, ssm_c_im, ssm_d, w_glu, b_glu, w_ssm_up, w_out, g_ffn2, w_ffn2_in, w_ffn2_out, g_final, loss_target, m_w_ada, m_b_ada, m_g_ffn1, m_w_ffn1_in, m_w_ffn1_out, m_g_mix, m_w_in, m_pool_w, m_pool_b, m_pool_scale, m_w_pool_up, m_ssm_lam_re_log, m_ssm_lam_im, m_ssm_log_dt, m_ssm_b_re, m_ssm_b_im, m_ssm_c_re, m_ssm_c_im, m_ssm_d, m_w_glu, m_b_glu, m_w_ssm_up, m_w_out, m_g_ffn2, m_w_ffn2_in, m_w_ffn2_out, m_g_final, v_w_ada, v_b_ada, v_g_ffn1, v_w_ffn1_in, v_w_ffn1_out, v_g_mix, v_w_in, v_pool_w, v_pool_b, v_pool_scale, v_w_pool_up, v_ssm_lam_re_log, v_ssm_lam_im, v_ssm_log_dt, v_ssm_b_re, v_ssm_b_im, v_ssm_c_re, v_ssm_c_im, v_ssm_d, v_w_glu, v_b_glu, v_w_ssm_up, v_w_out, v_g_ffn2, v_w_ffn2_in, v_w_ffn2_out, v_g_final):
    args = dict(locals())
    wt = {n: args[n] for n in WEIGHTS}
    mom = {n: args["m_" + n] for n in WEIGHTS}
    var = {n: args["v_" + n] for n in WEIGHTS}

    t, d = x.shape[1], x.shape[2]
    pw = pool_b.shape[1]
    sw = ssm_d.shape[1]
    ngrp = sw // SSM_GROUP
    gn = ngrp * SSM_STATE
    xi, yi, ci = _place()
    b_me = 4 * xi + 2 * yi + ci
    s_me = 2 * xi + yi
    x2d = x[0]
    tgt = loss_target[0]

    c_all = _allgather_small("ag_c", c.reshape(8, d // 8)).reshape(N_DEV, d)
    ncol = w_ada.shape[2]
    b_sh = lax.dynamic_slice(b_ada, (0, s_me * ncol), (1, ncol))
    mod_sh = _ada_fwd("ada_fwd", c_all, w_ada[0], b_sh)
    md_send, md_recv, mod_sh, md_land, tok = _bcast_start("mod_start", mod_sh)

    shards = [wt[n][0].astype(BF16) for n in BIG]
    ag_sems, shards_t, lands_t, tok = _ag_start("ag_start", shards, _place_shards(shards, s_me), AG_GROUPS, AG_DIRECT, deps=(tok,))
    md_land = _bcast_wait("mod_wait", mod_sh, md_land, md_send, md_recv, tok)
    mod_all = lax.dynamic_update_slice(md_land, mod_sh[None], (b_me, 0, 0))
    full = {}

    def weights(gi, *after):
        grp = AG_GROUPS[gi]
        if BIG[grp[0]] not in full:
            ls = _ag_wait("ag_wait%d" % gi, [shards_t[w] for w in grp], [lands_t[w] for w in grp], *ag_sems[gi],
                          AG_DIRECT[gi], after)
            for w, l in zip(grp, ls if AG_DIRECT[gi] else _ag_forward("ag_fwd%d" % gi, ls)):
                full[BIG[w]] = l
        return full

    mod_me = jnp.concatenate([lax.dynamic_slice(mod_all, (2 * s, b_me, 0), (1, 1, ncol))[0] for s in range(N_CHIPS)], axis=1)
    mod = mod_me.reshape(9, d)
    sh1, sc1, gt1, sh2, sc2, gt2, sh3, sc3, gt3 = [mod[k:k + 1] for k in range(9)]

    f = w_ffn1_out.shape[1] * N_CHIPS

    col = lambda a: a.reshape(gn, 1)
    lrl_c, li_c = col(ssm_lam_re_log), col(ssm_lam_im)
    ldt_c = col(jnp.broadcast_to(ssm_log_dt.reshape(ngrp, 1), (ngrp, SSM_STATE)))
    b_re2, b_im2 = ssm_b_re.reshape(gn, SSM_GROUP), ssm_b_im.reshape(gn, SSM_GROUP)
    lrdt_c, ang_c, bb_re, bb_im = _ssm_prep("ssm_prep", lrl_c, li_c, ldt_c, b_re2, b_im2)
    lrdt, ang = lrdt_c.reshape(1, gn), ang_c.reshape(1, gn)
    tr = lambda a: jnp.swapaxes(a, 1, 2)
    bbd = [_blockdiag_b(v, sw).astype(BF16) for v in (bb_re, bb_im)]
    ccd = [_blockdiag_c(v[0], sw).astype(BF16) for v in (ssm_c_re, ssm_c_im)]
    bbd_t, ccd_t = [tr(v) for v in bbd], [tr(v) for v in ccd]
    early = [n for n in SMALL if n not in SMALL_LATE]
    packs = {}
    for tag, names, extra in (("early", early, []), ("late", SMALL_LATE, [jnp.zeros((128,), F32)])):
        packs[tag] = [_pack([src[n] for n in names] + extra) for src in (wt, mom, var)]
    shadow = [lrdt, ang, ldt_c, *bbd, *ccd, *bbd_t, *ccd_t, *packs["early"], *packs["late"]]

    h1 = _norm_fwd("ffn1_norm", x2d, g_ffn1, sc1, sh1, deps=(tok,))
    x1, h2, sav1 = _ffn_fwd("ffn1", x2d, h1, gt1, lambda after: weights(0, after, *shadow)['w_ffn1_in'],
                            lambda after: weights(1, after)['w_ffn1_out'].reshape(f, d), (g_mix, sc2, sh2))
    weights(2, h2)
    wo = full['w_out'].reshape(d, d)
    uin = _mm("mix_in", h2, full['w_in'], b_stacked=True, tm=1024, tn=768, j_outer=True)
    p_pool, z_pool, y_pool = _pool_fwd("pool_fwd", uin, pool_w[0], pool_b, pool_scale, full['w_pool_up'], pw)
    y_s, ge, s_re, s_im = _ssm_fwd("ssm_fwd", uin, lrdt, ang, *bbd, *ccd, ssm_d, sw)
    epi, epi_ins, epi_outs = _glu_epilogue(b_glu, t, sw)
    gv, sg = _mm("glu_in", ge, full['w_glu'], b_stacked=True, tm=ROWS, tn=d, epi=epi, epi_ins=epi_ins, epi_outs=epi_outs,
                 b_resident=True)
    epi, epi_ins, epi_outs = _gates_epilogue(y_pool, uin, t, d, pw)
    y_ssm, merged = _mm("ssm_up", sg, full['w_ssm_up'], b_stacked=True, tm=ROWS, tn=d, epi=epi, epi_ins=epi_ins,
                        epi_outs=epi_outs, b_resident=True)
    epi, epi_ins, epi_outs = _gate_norm_epilogue(1.0, x1, gt2, (g_ffn2, sc3, sh3), t, d)
    x2, y2, h3 = _mm("mix_out", merged, wo, tm=ROWS, tn=d, epi=epi, epi_ins=epi_ins, epi_outs=epi_outs, b_resident=True)

    (dx3, dy3, dg_final, loss_v, dgt3), sav3 = _ffn_fwd(
        "ffn2", x2, h3, gt3, lambda after: weights(3, after)['w_ffn2_in'],
        lambda after: weights(3, after)['w_ffn2_out'].reshape(f, d), loss=(tgt, _row(g_final)))

    gs = {}
    rs_open = []

    def start_rs(names, gbs):
        send, recv, g_thru, land_thru, token = _rs_start("rs_start_" + names[0], gbs)
        rs_open.append((names, send, recv, g_thru, land_thru))
        return token

    dx2, dy2, dsh3, dsc3, gs['g_ffn2'], dgt2 = _ffn_bwd(
        "ffn2b", dx3, dy3, sav3, g_ffn2, sc3, full['w_ffn2_in'], full['w_ffn2_out'].reshape(f, d),
        lambda key, g: start_rs(['w_ffn2_' + key], [g]), (y2, gt2, 1.0))

    epi, epi_ins, epi_outs = _gates_bwd_epilogue(y_pool, y_ssm, uin, t, d, pw)
    dy_pool, dy_ssm, duin, g_wo = _mm("mix_dmerged", dy2, wo, tb=True, tm=ROWS, tn=d, epi=epi, epi_ins=epi_ins,
                                      epi_outs=epi_outs, b_resident=True, wgrad=merged, wgrad_rows=True)
    g_wo = g_wo.reshape(N_CHIPS, d // N_CHIPS, d)

    g_wpu = _mm("pool_dwup", p_pool, dy_pool, ta=True, out_stacked=True, tm=pw, tn=d, tk=4096)
    duin, gs['pool_w'], gs['pool_b'], gs['pool_scale'] = _pool_bwd(
        "pool_bwd", dy_pool, z_pool, pool_w[0], pool_b, pool_scale, full['w_pool_up'], duin)

    epi, epi_ins, epi_outs, epi_reds = _glu_bwd_epilogue(gv, b_glu, t, sw)
    dgv, gs['b_glu'], g_wsu = _mm("ssm_dup", dy_ssm, full['w_ssm_up'], tb=True, b_stacked=True, tm=ROWS, tn=sw, tk=d,
                                  epi=epi, epi_ins=epi_ins, epi_outs=epi_outs, epi_reds=epi_reds, b_resident=True, wgrad=sg)
    tok = start_rs(['w_out', 'w_pool_up', 'w_ssm_up'], [g_wo, g_wpu, g_wsu])
    dge, g_wglu = _mm("glu_dge", dgv, full['w_glu'], tb=True, b_stacked=True, tm=1024, tn=sw, tk=d, deps=(tok,), wgrad=ge)
    (duin, g_abre, g_abim, g_bbd_re, g_bbd_im, g_ccd_re, g_ccd_im, gs['ssm_d']) = _ssm_bwd(
        "ssm_bwd", dge, y_s, uin, s_re, s_im, lrdt, ang, *bbd_t, *ccd_t, ssm_d, sw, duin)
    gs['ssm_c_re'], gs['ssm_c_im'] = _diag_of_c(g_ccd_re, sw), _diag_of_c(g_ccd_im, sw)
    d_lrl, d_li, d_ldt, d_bre, d_bim = _ssm_param_bwd(
        "ssm_param_bwd", lrl_c, li_c, ldt_c, b_re2, b_im2, g_abre.reshape(gn, 1), g_abim.reshape(gn, 1),
        _diag_of_b(g_bbd_re, sw), _diag_of_b(g_bbd_im, sw))
    gs['ssm_lam_re_log'], gs['ssm_lam_im'] = d_lrl, d_li
    gs['ssm_log_dt'] = jnp.sum(d_ldt.reshape(ngrp, SSM_STATE), axis=1)
    gs['ssm_b_re'], gs['ssm_b_im'] = d_bre, d_bim

    g_win =_mm("mix_dwin", h2, duin, ta=True, out_stacked=True, tm=d, tn=768, tk=4096)
    tok = start_rs(['w_glu', 'w_in'], [g_wglu, g_win])
    epi, epi_ins, epi_outs, epi_reds = _norm_bwd_epilogue(x1, dx2, g_mix, sc2, (sav1[4], gt1, 0.5), t, d)
    dx1, dy1, dsh2, dsc2, gs['g_mix'], dgt1 = _mm(
        "mix_dh", duin, full['w_in'], tb=True, b_stacked=True, tn=d, tk=3072, deps=(tok,), epi=epi, epi_ins=epi_ins,
        epi_outs=epi_outs, epi_reds=epi_reds, b_resident=True)

    gs['g_final'] = dg_final
    sg_blk = _pack([gs[n] for n in early])
    sg_send, sg_recv, sg_blk, sg_land, tok = _bcast_start("sg_start", sg_blk)

    dx0, _, dsh1, dsc1, gs['g_ffn1'], _ = _ffn_bwd(
        "ffn1b", dx1, dy1, sav1, g_ffn1, sc1, full['w_ffn1_in'], full['w_ffn1_out'].reshape(f, d),
        lambda key, g: start_rs(['w_ffn1_' + key], [g]), None, deps=(tok,))

    gs['b_ada'] = jnp.concatenate([dsh1, dsc1, dgt1, dsh2, dsc2, dgt2, dsh3, dsc3, dgt3], axis=1)
    lt_blk = _pack([gs[n] for n in SMALL_LATE] + [loss_v])
    lt_send, lt_recv, lt_blk, lt_land, tok = _bcast_start("late_start", lt_blk)

    grads, delta, new_m, new_v = {}, {}, {}, {}

    def small_update(tag, names, blk, land):
        g8 = lax.dynamic_update_slice(land, blk[None], (b_me, 0, 0)).reshape(-1, 128)
        w_pack, m_pack, v_pack = packs[tag]
        shapes = [wt[n].shape for n in names]
        per_param, packed = _adamw_small("adamw_small_" + tag, w_pack, g8, m_pack, v_pack, shapes)
        for k, dst in enumerate((grads, delta, new_m, new_v)):
            rest = _unpack(packed[k], shapes)
            for i, n in enumerate(names):
                dst[n] = per_param[i][k] if per_param[i] is not None else rest[i]
        return g8

    after = tok
    for group in RS_SUM_GROUPS:
        names, g_done, land_done = [], [], []
        for k in group:
            nk, send, recv, g_thru, land_thru = rs_open[k]
            gd, ld = _rs_wait("rs_wait_" + nk[0], g_thru, land_thru, send, recv, after)
            names, g_done, land_done = names + nk, g_done + gd, land_done + ld
        for n, g_sum in zip(names, _rs_sum("rs_sum_" + names[0], g_done, land_done)):
            g_out, dl, mn, vn = _adamw("adamw_" + n, wt[n][0], g_sum, mom[n][0], var[n][0])
            grads[n], delta[n], new_m[n], new_v[n] = g_out[None], dl[None], mn[None], vn[None]
            after = dl
        if group is RS_SUM_GROUPS[-2]:
            small_update("early", early, sg_blk, _bcast_wait("sg_wait", sg_blk, sg_land, sg_send, sg_recv, after))

    lt_land = _bcast_wait("late_wait", lt_blk, lt_land, lt_send, lt_recv, after)
    late8 = small_update("late", SMALL_LATE, lt_blk, lt_land).reshape(N_DEV, -1)
    loss = jnp.sum(late8[:, 10 * d])
    dmod_sh = lax.dynamic_slice(late8, (0, s_me * ncol), (N_DEV, ncol))
    g_w_ada = _ada_bwd("ada_bwd", c_all, dmod_sh)
    dl, mn, vn = _adamw("adamw_w_ada", w_ada[0], g_w_ada, m_w_ada[0], v_w_ada[0], with_g=False)
    grads['w_ada'], delta['w_ada'], new_m['w_ada'], new_v['w_ada'] = g_w_ada[None], dl[None], mn[None], vn[None]

    return (loss, dx0[None], *[grads[n] for n in WEIGHTS], *[delta[n] for n in WEIGHTS],
            *[new_m[n] for n in WEIGHTS], *[new_v[n] for n in WEIGHTS])
```

```python
import functools
import math

import jax
import jax.numpy as jnp
from jax import lax
from jax.experimental import pallas as pl
from jax.experimental.pallas import tpu as pltpu

F32 = jnp.float32
BF16 = jnp.bfloat16
MESH = pl.DeviceIdType.MESH

EPS = 1e-6
POOL_WINDOWS = (2, 4, 8, 16)
POOL_HALO = 16
SSM_GROUP = 16
SSM_STATE = 64
SSM_BLOCKS = 4
N_DEV = 8
N_CHIPS = 4
ADAM_LR = 0.001
ADAM_B1 = 0.9
ADAM_B2 = 0.999
ADAM_EPS = 1e-08
ADAM_WD = 0.01
ADAM_STEP = 10
VMEM_LIMIT = 56 * 1024 * 1024
ROWS = 512


ANY_SPEC = pl.BlockSpec(memory_space=pl.ANY)
HBM_SPEC = pl.BlockSpec(memory_space=pltpu.HBM)
SEM_SPEC = pl.BlockSpec(memory_space=pltpu.SEMAPHORE)
EFFECT = pltpu.SideEffectType.DATAFLOW_SIDE_EFFECTING


def _hbm(a):
    return pltpu.with_memory_space_constraint(a, pltpu.HBM)


def _pcall(body, **kw):
    return pl.pallas_call(body, **kw)


def _params(*sem):
    return pltpu.CompilerParams(dimension_semantics=sem, vmem_limit_bytes=VMEM_LIMIT)


def _pick(n, cap, mult=128):
    if n <= cap:
        return n
    best = None
    for d in range(mult, cap + 1, mult):
        if n % d == 0:
            best = d
    assert best is not None, (n, cap, mult)
    return best


def _sigmoid(v):
    return 1.0 / (1.0 + jnp.exp(-v))


def _rowwise(name, fn, ins, params, outs, reds, tm, deps=()):
    t = ins[0][0].shape[0]
    tm = min(tm, t)
    nb = t // tm
    ni, npar, no, nd = len(ins), len(params), len(outs), len(deps)

    def body(*refs):
        iv = [r[...] for r in refs[:ni]]
        pv = [r[...] for r in refs[ni:ni + npar]]
        o_refs = refs[ni + npar + nd:ni + npar + nd + no]
        r_refs = refs[ni + npar + nd + no:]
        ovals, rvals = fn(iv, pv)
        for o_ref, val in zip(o_refs, ovals):
            off = 0
            if isinstance(val, tuple) and val[0] == "at":
                _, off, val = val
            parts = val if isinstance(val, (list, tuple)) else [val]
            for p in parts:
                o_ref[:, off:off + p.shape[1]] = p.astype(o_ref.dtype)
                off += p.shape[1]
        if r_refs:
            @pl.when(pl.program_id(0) == 0)
            def _():
                for r in r_refs:
                    r[...] = jnp.zeros_like(r)
            for r, val in zip(r_refs, rvals):
                r[...] += val

    in_specs = [pl.BlockSpec((tm, w), functools.partial(lambda i, cb: (i, cb), cb=cb)) for (_, w, cb) in ins]
    in_specs += [pl.BlockSpec(p.shape, lambda i: (0, 0)) for p in params]
    in_specs += [ANY_SPEC] * nd
    out_shape = [jax.ShapeDtypeStruct((t, w), dt) for (w, dt) in outs]
    out_shape += [jax.ShapeDtypeStruct((1, w), F32) for w in reds]
    out_specs = [pl.BlockSpec((tm, w), lambda i: (i, 0)) for (w, _) in outs]
    out_specs += [pl.BlockSpec((1, w), lambda i: (0, 0)) for w in reds]
    res = _pcall(body, name=name, grid=(nb,), in_specs=in_specs, out_specs=out_specs, out_shape=out_shape,
                 compiler_params=_params("arbitrary"))(*[a for a, _, _ in ins], *params, *deps)
    return res


def _colsum(v):
    return jnp.sum(v, axis=0, keepdims=True)


def _mm(name, a, b, *, ta=False, tb=False, b_stacked=False, out_stacked=False, tm=ROWS, tn=1024, tk=2816,
        out_dtype=BF16, epi=None, epi_ins=(), epi_outs=None, epi_reds=(), deps=(), j_outer=False, a_halves=False,
        b_halves=False, b_resident=False, wgrad=None, wgrad_rows=False):
    if a_halves:
        _, m, kdim = a.shape
        kdim *= 2
    elif ta:
        kdim, m = a.shape
    else:
        m, kdim = a.shape
    ns = None
    if b_stacked:
        ns = b.shape[2]
        n = b.shape[1] if tb else N_CHIPS * ns
        assert kdim == (N_CHIPS * ns if tb else b.shape[1]), (name, a.shape, b.shape)
    else:
        if b_halves:
            n = 2 * b.shape[2]
            assert kdim == b.shape[1] and not tb, (name, a.shape, b.shape)
        else:
            n = b.shape[0] if tb else b.shape[1]
            assert kdim == (b.shape[1] if tb else b.shape[0]), (name, a.shape, b.shape)
        if out_stacked:
            ns = n // N_CHIPS

    def shards(want):
        return max(g for g in (1, 2, 4) if g * ns <= max(want, ns))

    tm = _pick(m, tm, 128 if ta else 8)
    gn = gk = 1
    if (b_stacked and not tb) or out_stacked:
        gn = shards(min(tn, n // 2) if b_halves else tn)
        tn = gn * ns
    else:
        tn = _pick(n, tn)
    if b_stacked and tb:
        gk = shards(tk)
        tk = gk * ns
    else:
        tk = _pick(kdim, tk, 8 if ta else 128)
    nm, nn, nk = m // tm, n // tn, kdim // tk

    def ij(f):
        return (lambda g0, g1, k: f(g1, g0, k)) if j_outer else f

    if a_halves:
        assert b_stacked and tb and gk == N_CHIPS and nk == 1, name
        a_spec = pl.BlockSpec((2, tm, kdim // 2), ij(lambda i, j, k: (0, i, 0)))
    elif ta:
        a_spec = pl.BlockSpec((tk, tm), ij(lambda i, j, k: (k, i)))
    else:
        a_spec = pl.BlockSpec((tm, tk), ij(lambda i, j, k: (i, k)))
    if b_stacked and not tb:
        b_spec = pl.BlockSpec((gn, tk, ns), ij(lambda i, j, k: (j, k, 0)))
    elif b_stacked and tb:
        b_spec = pl.BlockSpec((gk, tn, ns), ij(lambda i, j, k: (k, j, 0)))
    elif b_halves:
        bph = (n // 2) // tn
        b_spec = pl.BlockSpec((None, tk, tn), ij(lambda i, j, k: (j // bph, k, j % bph)))
    elif tb:
        b_spec = pl.BlockSpec((tn, tk), ij(lambda i, j, k: (j, k)))
    else:
        b_spec = pl.BlockSpec((tk, tn), ij(lambda i, j, k: (k, j)))
    dims = (((0 if ta else 1,), (1 if tb else 0,)), ((), ()))

    if epi_outs is None:
        if out_stacked:
            epi_outs = [((N_CHIPS, m, ns), out_dtype, (gn, tm, ns), lambda i, j: (j, i, 0))]
        else:
            epi_outs = [((m, n), out_dtype, (tm, tn), lambda i, j: (i, j))]
    ne, no, nd, nr = len(epi_ins), len(epi_outs), len(deps), len(epi_reds)
    nx = 0 if wgrad is None else 1
    assert not (nr or nx) or (nn == 1 and not j_outer), name
    assert not nx or (nk == 1 and not ta and not a_halves and kdim % N_CHIPS == 0), name

    def body(a_ref, b_ref, *rest):
        e_refs = rest[:ne]
        x_refs = rest[ne:ne + nx]
        o_refs = rest[ne + nx + nd:ne + nx + nd + no]
        r_refs = rest[ne + nx + nd + no:ne + nx + nd + no + nr]
        wg_refs = rest[ne + nx + nd + no + nr:ne + nx + nd + no + nr + nx]
        scratch = rest[ne + nx + nd + no + nr + nx:]
        av = None if a_halves else a_ref[...].astype(BF16)
        if nx:
            wg_acc = scratch[-1]
            i = pl.program_id(0)
            pw_ = lax.dot_general(x_refs[0][...].astype(BF16), av, (((0,), (0,)), ((), ())), preferred_element_type=F32)

            @pl.when(i == 0)
            def _():
                wg_acc[...] = pw_

            @pl.when(i > 0)
            def _():
                wg_acc[...] += pw_

            @pl.when(i == nm - 1)
            def _():
                if wgrad_rows:
                    wg_refs[0][...] = wg_acc[...].astype(BF16)
                else:
                    ks = kdim // N_CHIPS
                    for s in range(N_CHIPS):
                        wg_refs[0][s] = wg_acc[:, s * ks:(s + 1) * ks].astype(BF16)
        if b_stacked and not tb:
            parts = [lax.dot_general(av, b_ref[s].astype(BF16), dims, preferred_element_type=F32) for s in range(gn)]
        elif b_stacked and tb:
            p = None
            for s in range(gk):
                if a_halves:
                    a_s = a_ref[s // 2, :, (s % 2) * ns:(s % 2 + 1) * ns].astype(BF16)
                else:
                    a_s = av[:, s * ns:(s + 1) * ns]
                q = lax.dot_general(a_s, b_ref[s].astype(BF16), dims, preferred_element_type=F32)
                p = q if p is None else p + q
            parts = [p]
        else:
            parts = [lax.dot_general(av, b_ref[...].astype(BF16), dims, preferred_element_type=F32)]

        def finish(acc_parts):
            if epi is None and out_stacked:
                acc = acc_parts[0]
                for s in range(gn):
                    o_refs[0][s] = acc[:, s * ns:(s + 1) * ns].astype(out_dtype)
            elif epi is None:
                w = acc_parts[0].shape[1]
                for s, part in enumerate(acc_parts):
                    o_refs[0][:, s * w:(s + 1) * w] = part.astype(out_dtype)
            else:
                acc = acc_parts[0] if len(acc_parts) == 1 else jnp.concatenate(acc_parts, axis=1)
                vals = epi(acc, *[r[...] for r in e_refs])
                if nr:
                    vals, reds = vals

                    @pl.when(pl.program_id(0) == 0)
                    def _():
                        for r in r_refs:
                            r[...] = jnp.zeros_like(r)
                    for r, val in zip(r_refs, reds):
                        r[...] += val
                for o_ref, val in zip(o_refs, vals):
                    if isinstance(val, tuple) and val[0] == "at":
                        o_ref[:, val[1]:val[1] + val[2].shape[1]] = val[2].astype(o_ref.dtype)
                    else:
                        o_ref[...] = val.astype(o_ref.dtype)

        if nk == 1:
            finish(parts)
        else:
            acc_ref = scratch[0]
            k = pl.program_id(2)
            w = parts[0].shape[1]

            @pl.when(k == 0)
            def _():
                for s, part in enumerate(parts):
                    acc_ref[:, s * w:(s + 1) * w] = part

            @pl.when(k > 0)
            def _():
                for s, part in enumerate(parts):
                    acc_ref[:, s * w:(s + 1) * w] += part

            @pl.when(k == nk - 1)
            def _():
                finish([acc_ref[...]])

    def _ij(f):
        return ij(lambda i, j, k: f(i, j))

    if b_resident:
        assert nk == 1 and (nn == 1 or j_outer), name
        b_spec = pl.BlockSpec(b_spec.block_shape, b_spec.index_map, pipeline_mode=pl.Buffered(1))
    in_specs = [a_spec, b_spec] + [pl.BlockSpec(blk, _ij(f)) for (_, blk, f) in epi_ins]
    if nx:
        in_specs.append(pl.BlockSpec((tm, wgrad.shape[1]), lambda i, j, k: (i, 0)))
    in_specs += [ANY_SPEC] * nd
    out_specs = [pl.BlockSpec(blk, _ij(f)) for (_, _, blk, f) in epi_outs]
    out_specs += [pl.BlockSpec((1, w), lambda *_: (0, 0)) for w in epi_reds]
    out_shape = [jax.ShapeDtypeStruct(s, dt) for (s, dt, _, _) in epi_outs] + [jax.ShapeDtypeStruct((1, w), F32) for w in epi_reds]
    scratch = [pltpu.VMEM((tm, tn), F32)] if nk > 1 else []
    if nx:
        wg_shape = (wgrad.shape[1], kdim) if wgrad_rows else (N_CHIPS, wgrad.shape[1], kdim // N_CHIPS)
        out_specs.append(pl.BlockSpec(wg_shape, lambda *_: (0,) * len(wg_shape)))
        out_shape.append(jax.ShapeDtypeStruct(wg_shape, BF16))
        scratch.append(pltpu.VMEM((wgrad.shape[1], kdim), F32))
    grid = (nn, nm, nk) if j_outer else (nm, nn, nk)
    sem = ("arbitrary",) * 3 if (nr or nx) else ("parallel", "parallel", "arbitrary")
    res = _pcall(body, name=name, grid=grid, in_specs=in_specs, out_specs=out_specs, out_shape=out_shape, scratch_shapes=scratch,
                 compiler_params=_params(*sem))(a, b, *[x for x, _, _ in epi_ins], *([wgrad] * nx), *deps)
    return res[0] if len(res) == 1 else res


def _norm_fwd(name, x, g, sc, sh, deps=()):
    d = x.shape[1]

    def fn(iv, pv):
        (xv,), (gv, scv, shv) = iv, pv
        r = lax.rsqrt(jnp.mean(xv * xv, axis=-1, keepdims=True) + EPS)
        return [xv * r * gv * (1.0 + scv) + shv], []

    return _rowwise(name, fn, [(x, d, 0)], [g, sc, sh], [(d, BF16)], [], ROWS, deps=deps)[0]


def _adamw(name, w, g, m, v, tm=256, with_g=True):
    c = w.shape[1]

    def fn(iv, pv):
        wv, gv, mv, vv = iv
        mn = ADAM_B1 * mv + (1.0 - ADAM_B1) * gv
        vn = ADAM_B2 * vv + (1.0 - ADAM_B2) * (gv * gv)
        m_hat = mn / (1.0 - ADAM_B1 ** ADAM_STEP)
        v_hat = vn / (1.0 - ADAM_B2 ** ADAM_STEP)
        delta = -ADAM_LR * (m_hat / (jnp.sqrt(v_hat) + ADAM_EPS) + ADAM_WD * wv)
        return ([gv] if with_g else []) + [delta, mn, vn], []

    return _rowwise(name, fn, [(w, c, 0), (g, c, 0), (m, c, 0), (v, c, 0)], [], [(c, F32)] * (4 if with_g else 3), [],
                    _pick(w.shape[0], tm, 8))


def _unpack_plan(shape):
    n = math.prod(shape)
    if len(shape) == 2 and shape[0] == 1 and n % 128 == 0:
        return [((slice(None), slice(128 * r, 128 * (r + 1))), slice(r, r + 1), slice(None)) for r in range(n // 128)]
    if len(shape) == 2 and shape[0] == 1 and n < 128:
        return [((slice(None), slice(None)), slice(0, 1), slice(0, n))]
    if len(shape) == 1 and n % 128 == 0:
        return [((slice(128 * r, 128 * (r + 1)),), r, slice(None)) for r in range(n // 128)]
    if len(shape) == 3 and shape[0] == 1 and shape[2] == 64:
        return [((0, slice(2 * r + h, 2 * r + h + 1), slice(None)), slice(r, r + 1), slice(64 * h, 64 * (h + 1)))
                for r in range(n // 128) for h in range(2)]
    if len(shape) == 4 and shape[0] == 1 and shape[2:] == (128, 128):
        return [((0, k), slice(128 * k, 128 * (k + 1)), slice(None)) for k in range(shape[1])]
    return None


def _adamw_small(name, w, g8, m, v, shapes):
    r = w.shape[0]
    plans, rows0, off = [], [], 0
    for s in shapes:
        plans.append(_unpack_plan(s))
        rows0.append(off // 128)
        off += math.prod(s) + (-math.prod(s)) % 128
    direct = [i for i, p in enumerate(plans) if p is not None]

    def body(w_ref, g_ref, m_ref, v_ref, *out):
        packed = out[4 * len(direct):]
        gv = g_ref[0:r, :]
        for k in range(1, N_DEV):
            gv = gv + g_ref[k * r:(k + 1) * r, :]
        mn = ADAM_B1 * m_ref[...] + (1.0 - ADAM_B1) * gv
        vn = ADAM_B2 * v_ref[...] + (1.0 - ADAM_B2) * (gv * gv)
        m_hat = mn / (1.0 - ADAM_B1 ** ADAM_STEP)
        v_hat = vn / (1.0 - ADAM_B2 ** ADAM_STEP)
        packed[0][...] = gv
        packed[1][...] = -ADAM_LR * (m_hat / (jnp.sqrt(v_hat) + ADAM_EPS) + ADAM_WD * w_ref[...])
        packed[2][...] = mn
        packed[3][...] = vn
        for di, i in enumerate(direct):
            for kind in range(4):
                o_ref, src = out[4 * di + kind], packed[kind]
                for o_idx, row, lanes in plans[i]:
                    row = (rows0[i] + row) if isinstance(row, int) else slice(rows0[i] + row.start, rows0[i] + row.stop)
                    o_ref[o_idx] = src[row, lanes]

    out_shape = [jax.ShapeDtypeStruct(shapes[i], F32) for i in direct for _ in range(4)]
    out_shape += [jax.ShapeDtypeStruct((r, 128), F32)] * 4
    res = _pcall(body, name=name, out_shape=out_shape,
                 compiler_params=pltpu.CompilerParams(vmem_limit_bytes=VMEM_LIMIT))(w, g8, m, v)
    per_param = [None] * len(shapes)
    for di, i in enumerate(direct):
        per_param[i] = res[4 * di:4 * di + 4]
    return per_param, res[4 * len(direct):]


def _pool_fwd(name, uin, pool_w, pool_b, pool_scale, w_up, pw, tm=ROWS):
    t = uin.shape[0]
    tm = min(tm, t)
    ng = len(POOL_WINDOWS)
    gw = pw // ng
    ns = w_up.shape[2]

    def body(u_ref, w_ref, b_ref, s_ref, wu_ref, p_ref, z_ref, y_ref, ext):
        i = pl.program_id(0)

        @pl.when(i == 0)
        def _():
            ext[0:POOL_HALO, :] = jnp.zeros((POOL_HALO, pw), F32)

        u = u_ref[...].astype(F32)
        ext[POOL_HALO:POOL_HALO + tm, :] = u
        pos = i * tm + lax.broadcasted_iota(jnp.int32, (tm, 1), 0)
        for k, win in enumerate(POOL_WINDOWS):
            cols = slice(k * gw, (k + 1) * gw)
            acc = u[:, cols]
            for j in range(1, win):
                acc = acc + ext[POOL_HALO - j:POOL_HALO - j + tm, cols]
            cnt = jnp.minimum(pos + 1, win).astype(F32)
            z = acc / cnt - u[:, cols]
            zp = jnp.dot(z.astype(BF16), w_ref[k].astype(BF16), preferred_element_type=F32) + b_ref[:, cols]
            p_ref[:, cols] = (zp * s_ref[:, cols]).astype(BF16)
            z_ref[:, cols] = z.astype(BF16)
        ext[0:POOL_HALO, :] = u[tm - POOL_HALO:tm, :]
        pv = p_ref[...]
        for s in range(N_CHIPS):
            y_ref[:, s * ns:(s + 1) * ns] = jnp.dot(pv, wu_ref[s], preferred_element_type=F32).astype(BF16)

    full3 = lambda i: (0, 0, 0)
    return _pcall(
        body, name=name, grid=(t // tm,),
        in_specs=[pl.BlockSpec((tm, pw), lambda i: (i, 0)), pl.BlockSpec(pool_w.shape, full3),
                  pl.BlockSpec(pool_b.shape, lambda i: (0, 0)), pl.BlockSpec(pool_scale.shape, lambda i: (0, 0)),
                  pl.BlockSpec(w_up.shape, full3)],
        out_specs=[pl.BlockSpec((tm, pw), lambda i: (i, 0))] * 2 + [pl.BlockSpec((tm, N_CHIPS * ns), lambda i: (i, 0))],
        out_shape=[jax.ShapeDtypeStruct((t, pw), BF16)] * 2 + [jax.ShapeDtypeStruct((t, N_CHIPS * ns), BF16)],
        scratch_shapes=[pltpu.VMEM((POOL_HALO + tm, pw), F32)],
        compiler_params=_params("arbitrary"))(uin, pool_w, pool_b, pool_scale, w_up)


def _pool_bwd(name, dy, z, pool_w, pool_b, pool_scale, w_up, duin, tm=ROWS):
    t, pw = z.shape
    tm = min(tm, t)
    nb = t // tm
    ng = len(POOL_WINDOWS)
    gw = pw // ng
    ns = w_up.shape[2]

    def body(dy_ref, z_ref, w_ref, b_ref, s_ref, wu_ref, duin_ref, du_ref, dw_ref, db_ref, ds_ref, ext):
        i = pl.program_id(0)

        @pl.when(i == 0)
        def _():
            ext[tm:tm + POOL_HALO, :] = jnp.zeros((POOL_HALO, pw), F32)
            dw_ref[...] = jnp.zeros_like(dw_ref)
            db_ref[...] = jnp.zeros_like(db_ref)
            ds_ref[...] = jnp.zeros_like(ds_ref)

        pos = (nb - 1 - i) * tm + lax.broadcasted_iota(jnp.int32, (tm, 1), 0)
        dp = None
        for s in range(N_CHIPS):
            q = lax.dot_general(dy_ref[:, s * ns:(s + 1) * ns], wu_ref[s], (((1,), (1,)), ((), ())), preferred_element_type=F32)
            dp = q if dp is None else dp + q
        for k, win in enumerate(POOL_WINDOWS):
            cols = slice(k * gw, (k + 1) * gw)
            zk = z_ref[:, cols]
            dpk = dp[:, cols]
            wk = w_ref[k].astype(BF16)
            zp = jnp.dot(zk, wk, preferred_element_type=F32) + b_ref[:, cols]
            ds_ref[:, cols] += _colsum(dpk * zp)
            dzp = dpk * s_ref[:, cols]
            db_ref[:, cols] += _colsum(dzp)
            dzpb = dzp.astype(BF16)
            dz = lax.dot_general(dzpb, wk, (((1,), (1,)), ((), ())), preferred_element_type=F32)
            dw_ref[k] += lax.dot_general(zk, dzpb, (((0,), (0,)), ((), ())), preferred_element_type=F32)
            cnt = jnp.minimum(pos + 1, win).astype(F32)
            r = dz / cnt
            ext[0:tm, cols] = r
            acc = r - dz
            for j in range(1, win):
                acc = acc + ext[j:j + tm, cols]
            du_ref[:, cols] = acc.astype(BF16)
        ext[tm:tm + POOL_HALO, :] = ext[0:POOL_HALO, :]

    rev = lambda i: (nb - 1 - i, 0)
    full3 = lambda i: (0, 0, 0)
    return _pcall(
        body, name=name, grid=(nb,),
        in_specs=[pl.BlockSpec((tm, N_CHIPS * ns), rev), pl.BlockSpec((tm, pw), rev), pl.BlockSpec(pool_w.shape, full3),
                  pl.BlockSpec(pool_b.shape, lambda i: (0, 0)), pl.BlockSpec(pool_scale.shape, lambda i: (0, 0)),
                  pl.BlockSpec(w_up.shape, full3), ANY_SPEC],
        out_specs=[pl.BlockSpec((tm, pw), rev), pl.BlockSpec(pool_w.shape, full3),
                   pl.BlockSpec((1, pw), lambda i: (0, 0)), pl.BlockSpec((1, pw), lambda i: (0, 0))],
        out_shape=[jax.ShapeDtypeStruct(duin.shape, BF16), jax.ShapeDtypeStruct(pool_w.shape, F32),
                   jax.ShapeDtypeStruct((1, pw), F32), jax.ShapeDtypeStruct((1, pw), F32)],
        scratch_shapes=[pltpu.VMEM((tm + POOL_HALO, pw), F32)], input_output_aliases={6: 0},
        compiler_params=_params("arbitrary"))(dy, z, pool_w, pool_b, pool_scale, w_up, duin)


def _ssm_disc(lrl, li, ldt):
    lr = -jnp.exp(lrl)
    dt = jnp.exp(ldt)
    mag = jnp.exp(lr * dt)
    ang = li * dt
    ab_re = mag * jnp.cos(ang)
    ab_im = mag * jnp.sin(ang)
    num_re = ab_re - 1.0
    num_im = ab_im
    den = lr * lr + li * li
    f_re = (num_re * lr + num_im * li) / den
    f_im = (num_im * lr - num_re * li) / den
    return lr, dt, mag, ang, ab_re, ab_im, num_re, num_im, den, f_re, f_im


def _ssm_prep(name, lrl, li, ldt, b_re, b_im):
    gn, h = b_re.shape

    def body(lrl_ref, li_ref, ldt_ref, br_ref, bi_ref, lrdt_ref, ang_ref, bbr_ref, bbi_ref):
        lr, dt, _, ang, _, _, _, _, _, f_re, f_im = _ssm_disc(lrl_ref[...], li_ref[...], ldt_ref[...])
        lrdt_ref[...] = lr * dt
        ang_ref[...] = ang
        br, bi = br_ref[...], bi_ref[...]
        bbr_ref[...] = f_re * br - f_im * bi
        bbi_ref[...] = f_re * bi + f_im * br

    col = jax.ShapeDtypeStruct((gn, 1), F32)
    mat = jax.ShapeDtypeStruct((gn, h), F32)
    return _pcall(body, name=name, out_shape=[col, col, mat, mat])(lrl, li, ldt, b_re, b_im)


def _ssm_param_bwd(name, lrl, li, ldt, b_re, b_im, g_abre, g_abim, g_bbre, g_bbim):
    gn, h = b_re.shape

    def body(lrl_ref, li_ref, ldt_ref, br_ref, bi_ref, gar_ref, gai_ref, gbr_ref, gbi_ref,
             dlrl_ref, dli_ref, dldt_ref, dbr_ref, dbi_ref):
        li_v = li_ref[...]
        lr, dt, mag, ang, ab_re, ab_im, num_re, num_im, den, f_re, f_im = _ssm_disc(lrl_ref[...], li_v, ldt_ref[...])
        br, bi = br_ref[...], bi_ref[...]
        gbr, gbi = gbr_ref[...], gbi_ref[...]
        g_fre = jnp.sum(gbr * br + gbi * bi, axis=1, keepdims=True)
        g_fim = jnp.sum(gbi * br - gbr * bi, axis=1, keepdims=True)
        dbr_ref[...] = gbr * f_re + gbi * f_im
        dbi_ref[...] = gbi * f_re - gbr * f_im
        g_num_re = (g_fre * lr - g_fim * li_v) / den
        g_num_im = (g_fre * li_v + g_fim * lr) / den
        g_den = -(g_fre * f_re + g_fim * f_im) / den
        g_lr = (g_fre * num_re + g_fim * num_im) / den + g_den * 2.0 * lr
        g_li = (g_fre * num_im - g_fim * num_re) / den + g_den * 2.0 * li_v
        g_are = gar_ref[...] + g_num_re
        g_aim = gai_ref[...] + g_num_im
        g_mag = g_are * jnp.cos(ang) + g_aim * jnp.sin(ang)
        g_ang = g_aim * ab_re - g_are * ab_im
        g_lrdt = g_mag * mag
        g_lr = g_lr + g_lrdt * dt
        g_dt = g_lrdt * lr + g_ang * li_v
        g_li = g_li + g_ang * dt
        dlrl_ref[...] = g_lr * lr
        dli_ref[...] = g_li
        dldt_ref[...] = g_dt * dt

    col = jax.ShapeDtypeStruct((gn, 1), F32)
    mat = jax.ShapeDtypeStruct((gn, h), F32)
    return _pcall(body, name=name, out_shape=[col, col, col, mat, mat])(lrl, li, ldt, b_re, b_im, g_abre, g_abim, g_bbre, g_bbim)


def _pow_rows(lrdt, ang, k):
    mag = jnp.exp(k * lrdt)
    return mag * jnp.cos(k * ang), mag * jnp.sin(k * ang)


def _ssm_chunk(t):
    return 256 if t >= 2048 else 128


def _to_segments(dst, srcs, jn):
    for q, src in enumerate(srcs):
        for j in range(jn):
            dst[8 * j:8 * j + 8, 128 * q:128 * (q + 1)] = src[pl.ds(j, 8, stride=jn), :]


def _from_segments(dst, src, q, jn, dtype):
    for s in range(8):
        dst[s * jn:(s + 1) * jn, 128 * q:128 * (q + 1)] = src[q, pl.ds(s, jn, stride=8), :].astype(dtype)


def _fill_rows8(dst_re, dst_im, v_re, v_im):
    for j in range(v_re.shape[0]):
        dst_re[8 * j:8 * j + 8, :] = jnp.broadcast_to(v_re[j:j + 1, :], (8, v_re.shape[1]))
        dst_im[8 * j:8 * j + 8, :] = jnp.broadcast_to(v_im[j:j + 1, :], (8, v_im.shape[1]))


def _cmul(ar, ai, br, bi):
    return ar * br - ai * bi, ar * bi + ai * br


def _cmul_conj(ar, ai, br, bi):
    return ar * br + ai * bi, ar * bi - ai * br


def _ssm_fwd(name, uin, lrdt, ang, bb_re, bb_im, cc_re, cc_im, d_skip, sw):
    t = uin.shape[0]
    gn = lrdt.shape[1]
    lc = _ssm_chunk(t)
    jn = lc // 8
    ub, sb = sw // SSM_BLOCKS, gn // SSM_BLOCKS
    nq = sw // 128
    assert ub == 128 and nq == SSM_BLOCKS

    def body(u_ref, lrdt_ref, ang_ref, bbr_ref, bbi_ref, ccr_ref, cci_ref, d_ref, y_ref, ge_ref, sre_ref, sim_ref,
             p_re, p_im, a_re, a_im, up, yp, cst_re, cst_im, car_re, car_im):
        i = pl.program_id(0)
        lrdt_v, ang_v = lrdt_ref[...], ang_ref[...]

        @pl.when(i == 0)
        def _():
            k = (lax.broadcasted_iota(jnp.int32, (jn, 1), 0) + 1).astype(F32)
            _fill_rows8(p_re, p_im, *_pow_rows(lrdt_v, ang_v, k))
            car_re[...] = jnp.zeros_like(car_re)
            car_im[...] = jnp.zeros_like(car_im)

        for q in range(nq):
            yp[q] = u_ref[:, 128 * q:128 * (q + 1)].astype(F32)
        _to_segments(up, [yp.at[q] for q in range(nq)], jn)
        u = up[...]
        ubf = u.astype(BF16)
        for q in range(SSM_BLOCKS):
            uq = ubf[:, q * ub:(q + 1) * ub]
            a_re[:, q * sb:(q + 1) * sb] = jnp.dot(uq, bbr_ref[q], preferred_element_type=F32)
            a_im[:, q * sb:(q + 1) * sb] = jnp.dot(uq, bbi_ref[q], preferred_element_type=F32)
        a1r, a1i = _pow_rows(lrdt_v, ang_v, 1.0)
        ajr, aji = _pow_rows(lrdt_v, ang_v, float(jn))
        for q in range(0, SSM_BLOCKS, 2):
            cols = slice(q * sb, (q + 2) * sb)
            ar8 = jnp.broadcast_to(a1r[:, cols], (8, 2 * sb))
            ai8 = jnp.broadcast_to(a1i[:, cols], (8, 2 * sb))

            def step(j, carry, cols=cols, ar8=ar8, ai8=ai8):
                sr, si = carry
                rows = pl.ds(pl.multiple_of(j * 8, 8), 8)
                mr, mi = _cmul(ar8, ai8, sr, si)
                nr, ni = mr + a_re[rows, cols], mi + a_im[rows, cols]
                a_re[rows, cols] = nr
                a_im[rows, cols] = ni
                return nr, ni

            lax.fori_loop(1, jn, step, (a_re[0:8, cols], a_im[0:8, cols]), unroll=4)
        er, ei = a_re[lc - 8:lc, :], a_im[lc - 8:lc, :]
        gr, gi = car_re[...], car_im[...]
        for s in range(8):
            cst_re[s:s + 1, :] = gr
            cst_im[s:s + 1, :] = gi
            mr, mi = _cmul(ajr, aji, gr, gi)
            gr, gi = mr + er[s:s + 1, :], mi + ei[s:s + 1, :]
        car_re[...] = gr
        car_im[...] = gi
        for q in range(SSM_BLOCKS):
            cols = slice(q * sb, (q + 1) * sb)
            cr = jnp.tile(cst_re[:, cols], (jn, 1))
            ci = jnp.tile(cst_im[:, cols], (jn, 1))
            mr, mi = _cmul(p_re[:, cols], p_im[:, cols], cr, ci)
            srb, sib = (a_re[:, cols] + mr).astype(BF16), (a_im[:, cols] + mi).astype(BF16)
            sre_ref[:, cols] = srb
            sim_ref[:, cols] = sib
            ycols = slice(q * ub, (q + 1) * ub)
            y = (jnp.dot(srb, ccr_ref[q], preferred_element_type=F32) - jnp.dot(sib, cci_ref[q], preferred_element_type=F32)
                 + d_ref[:, ycols] * u[:, ycols])
            yp[q] = y
            _from_segments(y_ref, yp, q, jn, F32)
            yt = y_ref[:, ycols]
            ge_ref[:, ycols] = (0.5 * yt * (1.0 + lax.erf(yt * (1.0 / math.sqrt(2.0))))).astype(BF16)

    row = lambda i: (0, 0)
    blk3 = lambda i: (0, 0, 0)
    return _pcall(
        body, name=name, grid=(t // lc,),
        in_specs=[pl.BlockSpec((lc, sw), lambda i: (i, 1)), pl.BlockSpec((1, gn), row), pl.BlockSpec((1, gn), row),
                  pl.BlockSpec(bb_re.shape, blk3), pl.BlockSpec(bb_im.shape, blk3),
                  pl.BlockSpec(cc_re.shape, blk3), pl.BlockSpec(cc_im.shape, blk3), pl.BlockSpec((1, sw), row)],
        out_specs=[pl.BlockSpec((lc, sw), lambda i: (i, 0)), pl.BlockSpec((lc, sw), lambda i: (i, 0)),
                   pl.BlockSpec((lc, gn), lambda i: (i, 0)), pl.BlockSpec((lc, gn), lambda i: (i, 0))],
        out_shape=[jax.ShapeDtypeStruct((t, sw), F32), jax.ShapeDtypeStruct((t, sw), BF16),
                   jax.ShapeDtypeStruct((t, gn), BF16), jax.ShapeDtypeStruct((t, gn), BF16)],
        scratch_shapes=[pltpu.VMEM((lc, gn), F32), pltpu.VMEM((lc, gn), F32), pltpu.VMEM((lc, gn), F32), pltpu.VMEM((lc, gn), F32),
                        pltpu.VMEM((lc, sw), F32), pltpu.VMEM((nq, lc, 128), F32),
                        pltpu.VMEM((8, gn), F32), pltpu.VMEM((8, gn), F32), pltpu.VMEM((1, gn), F32), pltpu.VMEM((1, gn), F32)],
        compiler_params=_params("arbitrary"))(uin, lrdt, ang, bb_re, bb_im, cc_re, cc_im, d_skip)


def _ssm_bwd(name, dge, y, uin, s_re, s_im, lrdt, ang, bbt_re, bbt_im, cct_re, cct_im, d_skip, sw, duin):
    t = uin.shape[0]
    gn = lrdt.shape[1]
    lc = _ssm_chunk(t)
    nb = t // lc
    jn = lc // 8
    ub, sb = sw // SSM_BLOCKS, gn // SSM_BLOCKS
    nq = sw // 128
    tail = 16

    def body(dge_ref, y_ref, u_ref, sre_ref, sim_ref, tre_ref, tim_ref, lrdt_ref, ang_ref, btr_ref, bti_ref, ctr_ref, cti_ref,
             d_ref, duin_ref, du_ref, dar_ref, dai_ref, dbr_ref, dbi_ref, dcr_ref, dci_ref, dd_ref,
             q_re, q_im, a_re, a_im, dyp, up, dys, us, dup, cst_re, cst_im, sp_re, sp_im, car_re, car_im):
        i = pl.program_id(0)
        lrdt_v, ang_v = lrdt_ref[...], ang_ref[...]

        @pl.when(i == 0)
        def _():
            k = (jn - lax.broadcasted_iota(jnp.int32, (jn, 1), 0)).astype(F32)
            _fill_rows8(q_re, q_im, *_pow_rows(lrdt_v, ang_v, k))
            car_re[...] = jnp.zeros_like(car_re)
            car_im[...] = jnp.zeros_like(car_im)
            for r in (dar_ref, dai_ref, dbr_ref, dbi_ref, dcr_ref, dci_ref, dd_ref):
                r[...] = jnp.zeros_like(r)

        yv = y_ref[...]
        ut = u_ref[...].astype(F32)
        cdf =0.5 * (1.0 + lax.erf(yv * (1.0 / math.sqrt(2.0))))
        pdf = jnp.exp(-0.5 * yv * yv) * (1.0 / math.sqrt(2.0 * math.pi))
        dyt = dge_ref[...].astype(F32) * (cdf + yv * pdf)
        dd_ref[...] += _colsum(dyt * ut)
        for q in range(nq):
            dys[q] = dyt[:, 128 * q:128 * (q + 1)]
            us[q] = ut[:, 128 * q:128 * (q + 1)]
        _to_segments(dyp, [dys.at[q] for q in range(nq)], jn)
        _to_segments(up, [us.at[q] for q in range(nq)], jn)
        dy = dyp[...]
        u = up[...]
        dyb = dy.astype(BF16)
        ubf = u.astype(BF16)
        for q in range(SSM_BLOCKS):
            dq = dyb[:, q * ub:(q + 1) * ub]
            a_re[:, q * sb:(q + 1) * sb] = jnp.dot(dq, ctr_ref[q], preferred_element_type=F32)
            a_im[:, q * sb:(q + 1) * sb] = -jnp.dot(dq, cti_ref[q], preferred_element_type=F32)
        a1r, a1i = _pow_rows(lrdt_v, ang_v, 1.0)
        ajr, aji = _pow_rows(lrdt_v, ang_v, float(jn))
        for q in range(0, SSM_BLOCKS, 2):
            cols = slice(q * sb, (q + 2) * sb)
            ar8 = jnp.broadcast_to(a1r[:, cols], (8, 2 * sb))
            ai8 = jnp.broadcast_to(a1i[:, cols], (8, 2 * sb))

            def step(jj, carry, cols=cols, ar8=ar8, ai8=ai8):
                sr, si = carry
                rows = pl.ds(pl.multiple_of((jn - 2 - jj) * 8, 8), 8)
                mr, mi = _cmul_conj(ar8, ai8, sr, si)
                nr, ni = mr + a_re[rows, cols], mi + a_im[rows, cols]
                a_re[rows, cols] = nr
                a_im[rows, cols] = ni
                return nr, ni

            lax.fori_loop(0, jn - 1, step, (a_re[lc - 8:lc, cols], a_im[lc - 8:lc, cols]), unroll=4)
        er, ei = a_re[0:8, :], a_im[0:8, :]
        hr, hi = car_re[...], car_im[...]
        for s in range(7, -1, -1):
            cst_re[s:s + 1, :] = hr
            cst_im[s:s + 1, :] = hi
            mr, mi = _cmul_conj(ajr, aji, hr, hi)
            hr, hi = mr + er[s:s + 1, :], mi + ei[s:s + 1, :]
        car_re[...] = hr
        car_im[...] = hi
        first = (i == nb - 1).astype(F32)
        sp_re[0:tail, :] = tre_ref[...].astype(F32) * (1.0 - first)
        sp_im[0:tail, :] = tim_ref[...].astype(F32) * (1.0 - first)
        sp_re[tail:tail + 8, :] = sre_ref[lc - tail:lc, :].astype(F32)[tail - 8:tail]
        sp_im[tail:tail + 8, :] = sim_ref[lc - tail:lc, :].astype(F32)[tail - 8:tail]
        tn_dims = (((0,), (0,)), ((), ()))
        for q in range(SSM_BLOCKS):
            cols = slice(q * sb, (q + 1) * sb)
            ycols = slice(q * ub, (q + 1) * ub)
            cr = jnp.tile(cst_re[:, cols], (jn, 1))
            ci = jnp.tile(cst_im[:, cols], (jn, 1))
            mr, mi = _cmul_conj(q_re[:, cols], q_im[:, cols], cr, ci)
            lam_r, lam_i = a_re[:, cols] + mr, a_im[:, cols] + mi
            s_r, s_i = sre_ref[:, cols], sim_ref[:, cols]
            p0r, p0i = sp_re[tail - 1:tail + 7, cols], sp_im[tail - 1:tail + 7, cols]
            l0r, l0i, l1r, l1i = lam_r[0:8], lam_i[0:8], lam_r[8:lc], lam_i[8:lc]
            pvr, pvi = s_r.astype(F32)[0:lc - 8], s_i.astype(F32)[0:lc - 8]
            dar_ref[:, cols] += _colsum(l1r * pvr + l1i * pvi) + _colsum(l0r * p0r + l0i * p0i)
            dai_ref[:, cols] += _colsum(l1i * pvr - l1r * pvi) + _colsum(l0i * p0r - l0r * p0i)
            lrb, lib = lam_r.astype(BF16), lam_i.astype(BF16)
            dup[q] = (jnp.dot(lrb, btr_ref[q], preferred_element_type=F32)
                      + jnp.dot(lib, bti_ref[q], preferred_element_type=F32) + d_ref[:, ycols] * dy[:, ycols])
            _from_segments(du_ref, dup, q, jn, BF16)
            uq = ubf[:, ycols]
            dbr_ref[q] += lax.dot_general(uq, lrb, tn_dims, preferred_element_type=F32)
            dbi_ref[q] += lax.dot_general(uq, lib, tn_dims, preferred_element_type=F32)
            dq = dyb[:, ycols]
            dcr_ref[q] += lax.dot_general(s_r.astype(BF16), dq, tn_dims, preferred_element_type=F32)
            dci_ref[q] -= lax.dot_general(s_i.astype(BF16), dq, tn_dims, preferred_element_type=F32)

    rev = lambda i: (nb - 1 - i, 0)
    tailmap = lambda i: (jnp.maximum((nb - 1 - i) * (lc // tail) - 1, 0), 0)
    row = lambda i: (0, 0)
    blk3 = lambda i: (0, 0, 0)
    return _pcall(
        body, name=name, grid=(nb,),
        in_specs=[pl.BlockSpec((lc, sw), rev), pl.BlockSpec((lc, sw), rev), pl.BlockSpec((lc, sw), lambda i: (nb - 1 - i, 1)),
                  pl.BlockSpec((lc, gn), rev), pl.BlockSpec((lc, gn), rev),
                  pl.BlockSpec((tail, gn), tailmap), pl.BlockSpec((tail, gn), tailmap),
                  pl.BlockSpec((1, gn), row), pl.BlockSpec((1, gn), row),
                  pl.BlockSpec(bbt_re.shape, blk3), pl.BlockSpec(bbt_im.shape, blk3),
                  pl.BlockSpec(cct_re.shape, blk3), pl.BlockSpec(cct_im.shape, blk3), pl.BlockSpec((1, sw), row), ANY_SPEC],
        out_specs=[pl.BlockSpec((lc, sw), lambda i: (nb - 1 - i, 1)), pl.BlockSpec((1, gn), row), pl.BlockSpec((1, gn), row),
                   pl.BlockSpec((SSM_BLOCKS, ub, sb), blk3), pl.BlockSpec((SSM_BLOCKS, ub, sb), blk3),
                   pl.BlockSpec((SSM_BLOCKS, sb, ub), blk3), pl.BlockSpec((SSM_BLOCKS, sb, ub), blk3),
                   pl.BlockSpec((1, sw), row)],
        out_shape=[jax.ShapeDtypeStruct(duin.shape, BF16), jax.ShapeDtypeStruct((1, gn), F32), jax.ShapeDtypeStruct((1, gn), F32),
                   jax.ShapeDtypeStruct((SSM_BLOCKS, ub, sb), F32), jax.ShapeDtypeStruct((SSM_BLOCKS, ub, sb), F32),
                   jax.ShapeDtypeStruct((SSM_BLOCKS, sb, ub), F32), jax.ShapeDtypeStruct((SSM_BLOCKS, sb, ub), F32),
                   jax.ShapeDtypeStruct((1, sw), F32)],
        scratch_shapes=[pltpu.VMEM((lc, gn), F32), pltpu.VMEM((lc, gn), F32), pltpu.VMEM((lc, gn), F32), pltpu.VMEM((lc, gn), F32),
                        pltpu.VMEM((lc, sw), F32), pltpu.VMEM((lc, sw), F32),
                        pltpu.VMEM((nq, lc, 128), F32), pltpu.VMEM((nq, lc, 128), F32), pltpu.VMEM((nq, lc, 128), F32),
                        pltpu.VMEM((8, gn), F32), pltpu.VMEM((8, gn), F32),
                        pltpu.VMEM((tail + 8, gn), F32), pltpu.VMEM((tail + 8, gn), F32),
                        pltpu.VMEM((1, gn), F32), pltpu.VMEM((1, gn), F32)],
        input_output_aliases={14: 0},
        compiler_params=_params("arbitrary"))(dge, y, uin, s_re, s_im, s_re, s_im, lrdt, ang,
                                               bbt_re, bbt_im, cct_re, cct_im, d_skip, duin)


def _blockdiag_b(bb, sw):
    gpb = (sw // SSM_GROUP) // SSM_BLOCKS
    b4 = bb.reshape(SSM_BLOCKS, gpb, SSM_STATE, SSM_GROUP)
    eye = jnp.eye(gpb, dtype=bb.dtype)
    out = jnp.einsum('qgnh,gk->qghkn', b4, eye)
    return out.reshape(SSM_BLOCKS, gpb * SSM_GROUP, gpb * SSM_STATE)


def _blockdiag_c(cc, sw):
    gpb = (sw // SSM_GROUP) // SSM_BLOCKS
    c4 = cc.reshape(SSM_BLOCKS, gpb, SSM_GROUP, SSM_STATE)
    eye = jnp.eye(gpb, dtype=cc.dtype)
    out = jnp.einsum('qghn,gk->qgnkh', c4, eye)
    return out.reshape(SSM_BLOCKS, gpb * SSM_STATE, gpb * SSM_GROUP)


def _diag_of_b(dbb, sw):
    gpb = (sw // SSM_GROUP) // SSM_BLOCKS
    d5 = dbb.reshape(SSM_BLOCKS, gpb, SSM_GROUP, gpb, SSM_STATE)
    return jnp.einsum('qghgn->qgnh', d5).reshape(SSM_BLOCKS * gpb * SSM_STATE, SSM_GROUP)


def _diag_of_c(dcc, sw):
    gpb = (sw // SSM_GROUP) // SSM_BLOCKS
    d5 = dcc.reshape(SSM_BLOCKS, gpb, SSM_STATE, gpb, SSM_GROUP)
    return jnp.einsum('qgngh->qghn', d5).reshape(SSM_BLOCKS * gpb, SSM_GROUP, SSM_STATE)


def _ada_fwd(name, c_all, w_sh, b_sh):
    nb, d = c_all.shape
    ncol = w_sh.shape[1]
    tn = _pick(ncol, 768)

    def body(c_ref, w_ref, b_ref, o_ref):
        cv = c_ref[...]
        sil = cv * _sigmoid(cv)
        o_ref[...] = jnp.dot(sil, w_ref[...], preferred_element_type=F32, precision=lax.Precision.HIGHEST) + b_ref[...]

    return _pcall(body, name=name, grid=(ncol // tn,),
                  in_specs=[pl.BlockSpec((nb, d), lambda j: (0, 0)), pl.BlockSpec((d, tn), lambda j: (0, j)),
                            pl.BlockSpec((1, tn), lambda j: (0, j))],
                  out_specs=pl.BlockSpec((nb, tn), lambda j: (0, j)),
                  out_shape=jax.ShapeDtypeStruct((nb, ncol), F32), compiler_params=_params("parallel"))(c_all, w_sh, b_sh)


def _ada_bwd(name, c_all, dmod_sh):
    nb, d = c_all.shape
    ncol = dmod_sh.shape[1]
    tn = _pick(ncol, 768)

    def body(c_ref, g_ref, o_ref):
        cv = c_ref[...]
        sil = cv * _sigmoid(cv)
        o_ref[...] = lax.dot_general(sil, g_ref[...], (((0,), (0,)), ((), ())), preferred_element_type=F32,
                                     precision=lax.Precision.HIGHEST)

    return _pcall(body, name=name, grid=(ncol // tn,),
                  in_specs=[pl.BlockSpec((nb, d), lambda j: (0, 0)), pl.BlockSpec((nb, tn), lambda j: (0, j))],
                  out_specs=pl.BlockSpec((d, tn), lambda j: (0, j)),
                  out_shape=jax.ShapeDtypeStruct((d, ncol), F32), compiler_params=_params("parallel"))(c_all, dmod_sh)


def _place():
    return lax.axis_index("x"), lax.axis_index("y"), lax.axis_index("c")


def _allgather_small(name, blk, deps=()):
    m_per, n = blk.shape

    def body(x_ref, *rest):
        out_ref, send_sems, recv_sems, local_sem = rest[len(deps):]
        x, y, c = _place()
        me, sibling = (x, y, c), (x, y, 1 - c)
        chips = [(1 - x, y), (x, 1 - y), (1 - x, 1 - y)]

        def rows(px, py, pc):
            return out_ref.at[pl.ds((4 * px + 2 * py + pc) * m_per, m_per), :]

        def copy(k, block, to, src=None):
            return pltpu.make_async_remote_copy(
                src_ref=rows(*block) if src is None else src, dst_ref=rows(*block),
                send_sem=send_sems.at[k], recv_sem=recv_sems.at[k], device_id=to, device_id_type=MESH)

        mine = pltpu.make_async_copy(x_ref, rows(*me), local_sem)
        mine.start()
        first = [copy(0, me, sibling, src=x_ref)]
        first += [copy(1 + j, me, (*chip, c), src=x_ref) for j, chip in enumerate(chips)]
        for cp in first:
            cp.start()
        passed = [copy(4 + j, (*chip, c), sibling) for j, chip in enumerate(chips)]
        for j, chip in enumerate(chips):
            copy(1 + j, (*chip, c), me).wait_recv()
            passed[j].start()
        copy(0, sibling, me).wait_recv()
        for j, chip in enumerate(chips):
            copy(4 + j, (*chip, 1 - c), me).wait_recv()
        for cp in first + passed:
            cp.wait_send()
        mine.wait()

    return _pcall(body, name=name, out_shape=jax.ShapeDtypeStruct((N_DEV * m_per, n), blk.dtype),
                  in_specs=[pl.BlockSpec(memory_space=pltpu.VMEM)] + [ANY_SPEC] * len(deps),
                  out_specs=pl.BlockSpec(memory_space=pltpu.VMEM),
                  scratch_shapes=[pltpu.SemaphoreType.DMA((7,)), pltpu.SemaphoreType.DMA((7,)), pltpu.SemaphoreType.DMA],
                  compiler_params=pltpu.CompilerParams(vmem_limit_bytes=VMEM_LIMIT))(blk, *deps)


def _other_chips(x, y):
    return [(1 - x, y), (x, 1 - y), (1 - x, 1 - y)]


def _place_shards(shards, s_me):
    return [lax.dynamic_update_slice(lax.empty((N_CHIPS,) + s.shape, s.dtype), s[None], (s_me, 0, 0)) for s in shards]


def _ag_copy(src, land, send, recv, wi, j, chip, x, y, c, tc, both):
    hr = src.shape[0] // 2
    half = pl.ds(pl.multiple_of(c * hr, 16), hr)
    k = 3 * wi + j
    return pltpu.make_async_remote_copy(
        src_ref=src.at[half, :], dst_ref=land.at[2 * x + y, half, :],
        send_sem=send.at[2 * k + tc if both else k], recv_sem=recv.at[2 * k + c if both else k],
        device_id=(chip[0], chip[1], tc), device_id_type=MESH)


def _ag_targets(c, both):
    return (0, 1) if both else (c,)


def _ag_start(name, shards, lands, groups, direct, deps=()):
    nw, ng, nd = len(shards), len(groups), len(deps)

    def body(*refs):
        src, land = refs[:nw], refs[nw:2 * nw]
        sems = refs[2 * nw + nd:2 * nw + nd + 2 * ng]
        token = refs[-1]
        x, y, c = _place()
        for gi, grp in enumerate(groups):
            for wi, w in enumerate(grp):
                for j, chip in enumerate(_other_chips(x, y)):
                    for tc in _ag_targets(c, direct[gi]):
                        _ag_copy(src[w], land[w], sems[2 * gi], sems[2 * gi + 1], wi, j, chip, x, y, c, tc, direct[gi]).start()
        token[...] = jnp.zeros_like(token)

    sem_shapes = []
    for gi, grp in enumerate(groups):
        sem_shapes += [pltpu.SemaphoreType.DMA(((6 if direct[gi] else 3) * len(grp),))] * 2
    out_shape = sem_shapes + [pltpu.HBM(s.shape, s.dtype) for s in shards] + [pltpu.HBM(l.shape, l.dtype) for l in lands]
    out_shape += [jax.ShapeDtypeStruct((8, 128), F32)]
    res = _pcall(body, name=name, out_shape=out_shape, in_specs=[HBM_SPEC] * (2 * nw) + [ANY_SPEC] * nd,
                 out_specs=[SEM_SPEC] * (2 * ng) + [HBM_SPEC] * (2 * nw) + [pl.BlockSpec(memory_space=pltpu.VMEM)],
                 input_output_aliases={i: 2 * ng + i for i in range(2 * nw)},
                 compiler_params=pltpu.CompilerParams(has_side_effects=EFFECT))(
                     *[_hbm(s) for s in shards], *[_hbm(l) for l in lands], *deps)
    sems = [(res[2 * gi], res[2 * gi + 1]) for gi in range(ng)]
    return sems, list(res[2 * ng:2 * ng + nw]), list(res[2 * ng + nw:2 * ng + 2 * nw]), res[-1]


def _ag_wait(name, shards, lands, send, recv, both, after):
    n = len(shards)

    def body(*refs):
        src, land = refs[:n], refs[n:2 * n]
        send_sem, recv_sem = refs[2 * n], refs[2 * n + 1]
        x, y, c = _place()
        for wi in range(n):
            for j, chip in enumerate(_other_chips(x, y)):
                for tc in _ag_targets(c, both):
                    _ag_copy(src[wi], land[wi], send_sem, recv_sem, wi, j, chip, x, y, c, tc, both).wait_send()
                    _ag_copy(src[wi], land[wi], send_sem, recv_sem, wi, j, chip, chip[0], chip[1], tc, c, both).wait_recv()

    res = _pcall(body, name=name, out_shape=[pltpu.HBM(a.shape, a.dtype) for a in list(shards) + list(lands)],
                 in_specs=[HBM_SPEC] * (2 * n) + [SEM_SPEC, SEM_SPEC] + [ANY_SPEC] * len(after), out_specs=[HBM_SPEC] * (2 * n),
                 input_output_aliases={i: i for i in range(2 * n)},
                 compiler_params=pltpu.CompilerParams(has_side_effects=EFFECT))(*shards, *lands, send, recv, *after)
    return list(res[n:])


def _ag_forward(name, lands):
    n = len(lands)

    def body(*refs):
        out = refs[n:2 * n]
        send, recv = refs[2 * n], refs[2 * n + 1]
        x, y, c = _place()
        sib = (x, y, 1 - c)
        cps = []
        for wi in range(n):
            hr = out[wi].shape[1] // 2
            for j, (cx, cy) in enumerate(_other_chips(x, y)):
                got = out[wi].at[2 * cx + cy, pl.ds(pl.multiple_of(c * hr, 16), hr), :]
                cp = pltpu.make_async_remote_copy(src_ref=got, dst_ref=got, send_sem=send.at[3 * wi + j], recv_sem=recv.at[3 * wi + j],
                                                  device_id=sib, device_id_type=MESH)
                cp.start()
                cps.append(cp)
        for wi in range(n):
            hr = out[wi].shape[1] // 2
            for j, (cx, cy) in enumerate(_other_chips(x, y)):
                got = out[wi].at[2 * cx + cy, pl.ds(pl.multiple_of((1 - c) * hr, 16), hr), :]
                pltpu.make_async_remote_copy(src_ref=got, dst_ref=got, send_sem=send.at[3 * wi + j], recv_sem=recv.at[3 * wi + j],
                                             device_id=sib, device_id_type=MESH).wait_recv()
        for cp in cps:
            cp.wait_send()

    res = _pcall(body, name=name, out_shape=[jax.ShapeDtypeStruct(l.shape, l.dtype) for l in lands],
                 in_specs=[ANY_SPEC] * n, out_specs=[ANY_SPEC] * n, input_output_aliases={i: i for i in range(n)},
                 scratch_shapes=[pltpu.SemaphoreType.DMA((3 * n,)), pltpu.SemaphoreType.DMA((3 * n,))])(*lands)
    return list(res)


def _peers(x, y, c):
    offs = [(dx, dy, dc) for dx in (0, 1) for dy in (0, 1) for dc in (0, 1)][1:]
    return [(1 - x if dx else x, 1 - y if dy else y, 1 - c if dc else c) for dx, dy, dc in offs]


def _rs_copy(g_ref, land_ref, send, recv, wi, k, to, sender):
    hr = g_ref.shape[1] // 2
    return pltpu.make_async_remote_copy(
        src_ref=g_ref.at[2 * to[0] + to[1], pl.ds(pl.multiple_of(to[2] * hr, 16), hr), :], dst_ref=land_ref.at[sender],
        send_sem=send.at[7 * wi + k], recv_sem=recv.at[7 * wi + k], device_id=to, device_id_type=MESH)


def _rs_start(name, gs):
    n = len(gs)
    lands = [lax.empty((N_DEV, g.shape[1] // 2, g.shape[2]), BF16) for g in gs]

    def body(*refs):
        g, land = refs[:n], refs[n:2 * n]
        send, recv = refs[2 * n], refs[2 * n + 1]
        token = refs[-1]
        x, y, c = _place()
        me = 4 * x + 2 * y + c
        for wi in range(n):
            for k, to in enumerate(_peers(x, y, c)):
                _rs_copy(g[wi], land[wi], send, recv, wi, k, to, me).start()
        token[...] = jnp.zeros_like(token)

    out_shape = [pltpu.SemaphoreType.DMA((7 * n,))] * 2 + [pltpu.HBM(a.shape, a.dtype) for a in list(gs) + lands]
    out_shape += [jax.ShapeDtypeStruct((8, 128), F32)]
    res = _pcall(body, name=name, out_shape=out_shape, in_specs=[HBM_SPEC] * (2 * n),
                 out_specs=[SEM_SPEC] * 2 + [HBM_SPEC] * (2 * n) + [pl.BlockSpec(memory_space=pltpu.VMEM)],
                 input_output_aliases={i: 2 + i for i in range(2 * n)},
                 compiler_params=pltpu.CompilerParams(has_side_effects=EFFECT))(
                     *[_hbm(a) for a in gs], *[_hbm(a) for a in lands])
    return res[0], res[1], list(res[2:2 + n]), list(res[2 + n:2 + 2 * n]), res[-1]


def _rs_wait(name, gs, lands, send, recv, after):
    n = len(gs)

    def body(*refs):
        g, land = refs[:n], refs[n:2 * n]
        send_sem, recv_sem = refs[2 * n], refs[2 * n + 1]
        x, y, c = _place()
        me = 4 * x + 2 * y + c
        for wi in range(n):
            for k, to in enumerate(_peers(x, y, c)):
                _rs_copy(g[wi], land[wi], send_sem, recv_sem, wi, k, to, me).wait_send()
                _rs_copy(g[wi], land[wi], send_sem, recv_sem, wi, k, (x, y, c), 4 * to[0] + 2 * to[1] + to[2]).wait_recv()

    res = _pcall(body, name=name, out_shape=[pltpu.HBM(a.shape, a.dtype) for a in list(gs) + list(lands)],
                 in_specs=[HBM_SPEC] * (2 * n) + [SEM_SPEC, SEM_SPEC, ANY_SPEC], out_specs=[HBM_SPEC] * (2 * n),
                 input_output_aliases={i: i for i in range(2 * n)},
                 compiler_params=pltpu.CompilerParams(has_side_effects=EFFECT))(*gs, *lands, send, recv, after)
    return list(res[:n]), list(res[n:])


def _bc_copy(blk_ref, land_ref, send, recv, k, to, slot):
    return pltpu.make_async_remote_copy(src_ref=blk_ref, dst_ref=land_ref.at[slot], send_sem=send.at[k], recv_sem=recv.at[k],
                                        device_id=to, device_id_type=MESH)


def _bcast_start(name, blk):
    land = lax.empty((N_DEV,) + blk.shape, blk.dtype)

    def body(blk_ref, land_ref, send, recv, blk_thru, land_thru, token):
        x, y, c = _place()
        for k, to in enumerate(_peers(x, y, c)):
            _bc_copy(blk_ref, land_ref, send, recv, k, to, 4 * x + 2 * y + c).start()
        token[...] = jnp.zeros_like(token)

    return _pcall(body, name=name,
                  out_shape=[pltpu.SemaphoreType.DMA((7,)), pltpu.SemaphoreType.DMA((7,)), pltpu.HBM(blk.shape, blk.dtype),
                             pltpu.HBM(land.shape, land.dtype), jax.ShapeDtypeStruct((8, 128), F32)],
                  in_specs=[HBM_SPEC, HBM_SPEC], out_specs=[SEM_SPEC, SEM_SPEC, HBM_SPEC, HBM_SPEC, pl.BlockSpec(memory_space=pltpu.VMEM)],
                  input_output_aliases={0: 2, 1: 3}, compiler_params=pltpu.CompilerParams(has_side_effects=EFFECT))(_hbm(blk), _hbm(land))


def _bcast_wait(name, blk, land, send, recv, after):
    def body(blk_ref, land_ref, send_sem, recv_sem, after_ref, blk_thru, land_thru):
        x, y, c = _place()
        for k, to in enumerate(_peers(x, y, c)):
            _bc_copy(blk_ref, land_ref, send_sem, recv_sem, k, to, 4 * x + 2 * y + c).wait_send()
            _bc_copy(blk_ref, land_ref, send_sem, recv_sem, k, to, 4 * to[0] + 2 * to[1] + to[2]).wait_recv()

    return _pcall(body, name=name, out_shape=[pltpu.HBM(blk.shape, blk.dtype), pltpu.HBM(land.shape, land.dtype)],
                  in_specs=[HBM_SPEC, HBM_SPEC, SEM_SPEC, SEM_SPEC, ANY_SPEC], out_specs=[HBM_SPEC, HBM_SPEC],
                  input_output_aliases={0: 0, 1: 1}, compiler_params=pltpu.CompilerParams(has_side_effects=EFFECT))(
                      blk, land, send, recv, after)[1]


def _rs_sum(name, gs, lands):
    n = len(gs)

    def body(*refs):
        g_refs, land_refs, out_refs = refs[:n], refs[n:2 * n], refs[2 * n:3 * n]
        recvs = refs[3 * n:4 * n]
        local_sems, sib_send, sib_recv = refs[4 * n:]
        x, y, c = _place()
        me = 4 * x + 2 * y + c
        cps = []
        for wi in range(n):
            hr = g_refs[wi].shape[1] // 2
            own = g_refs[wi].at[2 * x + y, pl.ds(pl.multiple_of(c * hr, 16), hr), :]
            cps.append(pltpu.make_async_copy(own, recvs[wi].at[me], local_sems.at[8 * wi + 7]))
            for k, (tx, ty, tc) in enumerate(_peers(x, y, c)):
                slot = 4 * tx + 2 * ty + tc
                cps.append(pltpu.make_async_copy(land_refs[wi].at[slot], recvs[wi].at[slot], local_sems.at[8 * wi + k]))
        for cp in cps:
            cp.start()
        sibs = []
        for wi in range(n):
            hr = g_refs[wi].shape[1] // 2
            ch = _pick(hr, 64, 16)
            for cp in cps[8 * wi:8 * wi + 8]:
                cp.wait()
            base = pl.multiple_of(c * hr, 16)
            for r0 in range(0, hr, ch):
                acc = recvs[wi][0, r0:r0 + ch, :].astype(F32)
                for k in range(1, N_DEV):
                    acc = acc + recvs[wi][k, r0:r0 + ch, :].astype(F32)
                out_refs[wi][pl.ds(base + r0, ch), :] = acc
            half = out_refs[wi].at[pl.ds(base, hr), :]
            sib = pltpu.make_async_remote_copy(src_ref=half, dst_ref=half, send_sem=sib_send.at[wi], recv_sem=sib_recv.at[wi],
                                               device_id=(x, y, 1 - c), device_id_type=MESH)
            sib.start()
            sibs.append(sib)
        for wi in range(n):
            hr = g_refs[wi].shape[1] // 2
            other = out_refs[wi].at[pl.ds(pl.multiple_of((1 - c) * hr, 16), hr), :]
            pltpu.make_async_remote_copy(src_ref=other, dst_ref=other, send_sem=sib_send.at[wi], recv_sem=sib_recv.at[wi],
                                         device_id=(x, y, 1 - c), device_id_type=MESH).wait_recv()
            sibs[wi].wait_send()

    res = _pcall(
        body, name=name, out_shape=[jax.ShapeDtypeStruct(g.shape[1:], F32) for g in gs],
        in_specs=[ANY_SPEC] * (2 * n), out_specs=[pl.BlockSpec(memory_space=pltpu.VMEM)] * n,
        scratch_shapes=[pltpu.VMEM((N_DEV, g.shape[1] // 2, g.shape[2]), BF16) for g in gs]
        + [pltpu.SemaphoreType.DMA((8 * n,)), pltpu.SemaphoreType.DMA((n,)), pltpu.SemaphoreType.DMA((n,))],
        compiler_params=pltpu.CompilerParams(vmem_limit_bytes=VMEM_LIMIT))(*gs, *lands)
    return list(res)


def _gate_norm_epilogue(factor, x_in, gt, nxt, t, d):
    blk = (ROWS, d)
    full = lambda i, j: (i, j)
    rowv = lambda i, j: (0, j)

    def epi(acc, res, scale, *norm):
        x_new = res + (factor * scale) * acc
        if not norm:
            return x_new, acc
        gv, scv, shv = norm
        r = lax.rsqrt(jnp.mean(x_new * x_new, axis=-1, keepdims=True) + EPS)
        return x_new, acc, x_new * r * gv * (1.0 + scv) + shv

    ins = [(x_in, blk, full), (gt, (1, d), rowv)] + [(v, (1, d), rowv) for v in (nxt or ())]
    outs = [((t, d), F32, blk, full), ((t, d), BF16, blk, full)] + ([((t, d), BF16, blk, full)] if nxt else [])
    return epi, ins, outs


_FULL = lambda i, j: (i, j)
_ROWV = lambda i, j: (0, j)


def _glu_epilogue(b_glu, t, w):
    def epi(acc, bv):
        g = acc + bv
        return acc, g[:, :w] * _sigmoid(g[:, w:])

    return epi, [(b_glu, (1, 2 * w), _ROWV)], [((t, 2 * w), BF16, (ROWS, 2 * w), _FULL), ((t, w), BF16, (ROWS, w), _FULL)]


def _glu_bwd_epilogue(gv, b_glu, t, w):
    def epi(dsg, gvv, bv):
        g = gvv.astype(F32) + bv
        val, s = g[:, :w], _sigmoid(g[:, w:])
        dval, dgate = dsg * s, dsg * val * s * (1.0 - s)
        return [jnp.concatenate([dval, dgate], axis=1)], [jnp.concatenate([_colsum(dval), _colsum(dgate)], axis=1)]

    return epi, [(gv, (ROWS, 2 * w), _FULL), (b_glu, (1, 2 * w), _ROWV)], [((t, 2 * w), BF16, (ROWS, 2 * w), _FULL)], [2 * w]


def _gates_epilogue(y_pool, uin, t, d, pw):
    cb = (2 * pw) // d

    def epi(acc, yp, glp, gls):
        return acc, _sigmoid(glp.astype(F32)) * yp.astype(F32) + _sigmoid(gls.astype(F32)) * acc

    ins = [(y_pool, (ROWS, d), _FULL), (uin, (ROWS, d), lambda i, j: (i, cb)), (uin, (ROWS, d), lambda i, j: (i, cb + 1))]
    return epi, ins, [((t, d), BF16, (ROWS, d), _FULL)] * 2


def _gates_bwd_epilogue(y_pool, y_ssm, uin, t, d, pw):
    cb = (2 * pw) // d

    def epi(dm, yp, ys, glp, gls):
        sp, ss = _sigmoid(glp.astype(F32)), _sigmoid(gls.astype(F32))
        dgl = jnp.concatenate([dm * yp.astype(F32) * sp * (1.0 - sp), dm * ys.astype(F32) * ss * (1.0 - ss)], axis=1)
        return dm * sp, dm * ss, ("at", 2 * pw, dgl)

    ins = [(y_pool, (ROWS, d), _FULL), (y_ssm, (ROWS, d), _FULL),
           (uin, (ROWS, d), lambda i, j: (i, cb)), (uin, (ROWS, d), lambda i, j: (i, cb + 1))]
    wide = 2 * pw + 2 * d
    return epi, ins, [((t, d), BF16, (ROWS, d), _FULL)] * 2 + [((t, wide), BF16, (ROWS, wide), _FULL)]


def _loss_epilogue(x_in, gt, tgt, g_final, t, d):
    blk = (ROWS, d)
    full = lambda i, j: (i, j)
    rowv = lambda i, j: (0, j)

    def epi(acc, res, scale, tv, gv):
        xv = res + (0.5 * scale) * acc
        r = lax.rsqrt(jnp.mean(xv * xv, axis=-1, keepdims=True) + EPS)
        xr = xv * r
        e = xr * gv - tv
        loss_row = 0.5 * jnp.mean(e * e, axis=-1, keepdims=True)
        dout = e * (1.0 / d)
        gd = gv * dout
        dx = r * (gd - xr * jnp.mean(gd * xr, axis=-1, keepdims=True))
        fdx = 0.5 * dx
        return [dx, scale * fdx], [_colsum(dout * xr), _colsum(loss_row * jnp.ones((1, 128), F32)), _colsum(fdx * acc)]

    ins = [(x_in, blk, full), (gt, (1, d), rowv), (tgt, blk, full), (g_final, (1, d), rowv)]
    return epi, ins, [((t, d), F32, blk, full), ((t, d), BF16, blk, full)], [d, 128, d]


def _norm_bwd_epilogue(x, dres, g, sc, up, t, d):
    blk = (ROWS, d)
    full = lambda i, j: (i, j)
    rowv = lambda i, j: (0, j)

    def epi(dhv, xv, drv, gv, scv, *rest):
        r = lax.rsqrt(jnp.mean(xv * xv, axis=-1, keepdims=True) + EPS)
        xr = xv * r
        dn = dhv * (1.0 + scv)
        gd = gv * dn
        dx = drv + r * (gd - xr * jnp.mean(gd * xr, axis=-1, keepdims=True))
        outs, reds = [dx], [_colsum(dhv), _colsum(dhv * xr * gv), _colsum(dn * xr)]
        if up:
            yv, gtv = rest
            fdx = up[2] * dx
            outs.append(gtv * fdx)
            reds.append(_colsum(fdx * yv.astype(F32)))
        return outs, reds

    ins = [(x, blk, full), (dres, blk, full), (g, (1, d), rowv), (sc, (1, d), rowv)]
    ins += [(up[0], blk, full), (up[1], (1, d), rowv)] if up else []
    outs = [((t, d), F32, blk, full)] + ([((t, d), BF16, blk, full)] if up else [])
    return epi, ins, outs, [d] * (4 if up else 3)


def _ffn_in_act(name, h, w_in, tm=ROWS):
    t, d = h.shape
    ns = w_in.shape[2]
    tm = _pick(t, tm, 8)
    w4 = w_in.reshape(2, 2, d, ns)

    def body(h_ref, w_ref, ab_ref, act_ref):
        hv = h_ref[...]
        a = jnp.dot(hv, w_ref[0], preferred_element_type=F32)
        b = jnp.dot(hv, w_ref[1], preferred_element_type=F32)
        ab_ref[0] = a.astype(BF16)
        ab_ref[1] = b.astype(BF16)
        act_ref[...] = (a * _sigmoid(a) * b).astype(BF16)

    return _pcall(body, name=name, grid=(2, t // tm),
                  in_specs=[pl.BlockSpec((tm, d), lambda c, i: (i, 0)), pl.BlockSpec((2, None, d, ns), lambda c, i: (0, c, 0, 0))],
                  out_specs=[pl.BlockSpec((2, tm, ns), lambda c, i: (0, i, c)), pl.BlockSpec((tm, ns), lambda c, i: (i, c))],
                  out_shape=[jax.ShapeDtypeStruct((2, t, 2 * ns), BF16), jax.ShapeDtypeStruct((t, 2 * ns), BF16)],
                  compiler_params=_params("parallel", "parallel"))(h, w4)


def _dswiglu_epilogue(ab, t, f, tn):
    blk = (2, ROWS, tn)
    idx = lambda i, j: (0, i, j)

    def epi(dact, abv):
        a, b = abv[0].astype(F32), abv[1].astype(F32)
        s = _sigmoid(a)
        return (jnp.stack([dact * b * (s * (1.0 + a * (1.0 - s))), dact * (a * s)]),)

    return epi, [(ab, blk, idx)], [((2, t, f), BF16, blk, idx)]


def _ffn_fwd(tag, x, h, gt, get_w_in, get_w_out, nxt=None, loss=None):
    t, d = x.shape
    ab, act = _ffn_in_act(tag + "_in", h, get_w_in(h))
    if loss:
        epi, epi_ins, epi_outs, epi_reds = _loss_epilogue(x, gt, *loss, t, d)
        res = _mm(tag + "_out", act, get_w_out(act), tm=ROWS, tn=d, epi=epi, epi_ins=epi_ins, epi_outs=epi_outs,
                  epi_reds=epi_reds, b_resident=True)
        return res, (x, h, ab, act, None)
    epi, epi_ins, epi_outs = _gate_norm_epilogue(0.5, x, gt, nxt, t, d)
    res = _mm(tag + "_out", act, get_w_out(act), tm=ROWS, tn=d, epi=epi, epi_ins=epi_ins, epi_outs=epi_outs, b_resident=True)
    return res[0], res[2], (x, h, ab, act, res[1])


def _ffn_bwd(tag, dx_new, dy, saved, g, sc, w_in, w_out, start_rs, up, deps=()):
    x, h, ab, act, _ = saved
    d = x.shape[1]
    f = act.shape[1]
    dw_out = _mm(tag + "_dwout", act, dy, ta=True, tm=1408, tn=d, tk=2048, deps=deps)
    tok = start_rs("out", dw_out.reshape(N_CHIPS, f // N_CHIPS, d))
    epi, epi_ins, epi_outs = _dswiglu_epilogue(ab, x.shape[0], f, w_in.shape[2])
    dab = _mm(tag + "_dact", dy, w_out, tb=True, tm=ROWS, tn=w_in.shape[2], epi=epi, epi_ins=epi_ins, epi_outs=epi_outs,
              deps=(tok,), j_outer=True)
    dw_in = _mm(tag + "_dwin", h, dab, ta=True, out_stacked=True, b_halves=True, tm=d, tn=1408, tk=2048)
    tok = start_rs("in", dw_in)
    epi, epi_ins, epi_outs, epi_reds = _norm_bwd_epilogue(x, dx_new, g, sc, up, x.shape[0], d)
    res = _mm(tag + "_dh", dab, w_in, tb=True, b_stacked=True, a_halves=True, tn=d, tk=5632, deps=(tok,), epi=epi,
              epi_ins=epi_ins, epi_outs=epi_outs, epi_reds=epi_reds, b_resident=True)
    if up:
        return res
    return res[0], None, res[1], res[2], res[3], None


def _row(v):
    return v.reshape(1, -1)


def _pack(parts):
    cols = []
    for p in parts:
        flat = p.reshape(-1).astype(F32)
        padn = (-flat.shape[0]) % 128
        cols.append(jnp.pad(flat, (0, padn)) if padn else flat)
    flat = jnp.concatenate(cols)
    padn = (-flat.shape[0]) % 1024
    if padn:
        flat = jnp.pad(flat, (0, padn))
    return flat.reshape(-1, 128)


def _unpack(packed, shapes):
    flat = packed.reshape(-1)
    out, off = [], 0
    for s in shapes:
        n = math.prod(s)
        out.append(flat[off:off + n].reshape(s))
        off += n + ((-n) % 128)
    return out


SMALL = ['b_ada', 'g_ffn1', 'g_mix', 'pool_w', 'pool_b', 'pool_scale', 'ssm_lam_re_log', 'ssm_lam_im', 'ssm_log_dt',
         'ssm_b_re', 'ssm_b_im', 'ssm_c_re', 'ssm_c_im', 'ssm_d', 'b_glu', 'g_ffn2', 'g_final']
BIG = ['w_ffn1_in', 'w_ffn1_out', 'w_in', 'w_pool_up', 'w_glu', 'w_ssm_up', 'w_out', 'w_ffn2_in', 'w_ffn2_out']
AG_GROUPS = [[0], [1], [2, 3, 4, 5, 6], [7, 8]]
AG_DIRECT = [False, False, False, True]
SMALL_LATE = ['b_ada', 'g_ffn1']
RS_SUM_GROUPS = [[0, 1], [2, 3], [4, 5]]
WEIGHTS = ['w_ada', 'b_ada', 'g_ffn1', 'w_ffn1_in', 'w_ffn1_out', 'g_mix', 'w_in', 'pool_w', 'pool_b', 'pool_scale', 'w_pool_up',
           'ssm_lam_re_log', 'ssm_lam_im', 'ssm_log_dt', 'ssm_b_re', 'ssm_b_im', 'ssm_c_re', 'ssm_c_im', 'ssm_d', 'w_glu', 'b_glu',
           'w_ssm_up', 'w_out', 'g_ffn2', 'w_ffn2_in', 'w_ffn2_out', 'g_final']


def kernel(x, c, w_ada, b_ada, g_ffn1, w_ffn1_in, w_ffn1_out, g_mix, w_in, pool_w, pool_b, pool_scale, w_pool_up, ssm_lam_re_log, ssm_lam_im, ssm_log_dt, ssm_b_re, ssm_b_im, ssm_c_re, ssm_c_im, ssm_d, w_glu, b_glu, w_ssm_up, w_out, g_ffn2, w_ffn2_in, w_ffn2_out, g_final, loss_target, m_w_ada, m_b_ada, m_g_ffn1, m_w_ffn1_in, m_w_ffn1_out, m_g_mix, m_w_in, m_pool_w, m_pool_b, m_pool_scale, m_w_pool_up, m_ssm_lam_re_log, m_ssm_lam_im, m_ssm_log_dt, m_ssm_b_re, m_ssm_b_im, m_ssm_c_re, m_ssm_c_im, m_ssm_d, m_w_glu, m_b_glu, m_w_ssm_up, m_w_out, m_g_ffn2, m_w_ffn2_in, m_w_ffn2_out, m_g_final, v_w_ada, v_b_ada, v_g_ffn1, v_w_ffn1_in, v_w_ffn1_out, v_g_mix, v_w_in, v_pool_w, v_pool_b, v_pool_scale, v_w_pool_up, v_ssm_lam_re_log, v_ssm_lam_im, v_ssm_log_dt, v_ssm_b_re, v_ssm_b_im, v_ssm_c_re, v_ssm_c_im, v_ssm_d, v_w_glu, v_b_glu, v_w_ssm_up, v_w_out, v_g_ffn2, v_w_ffn2_in, v_w_ffn2_out, v_g_final):
    args = dict(locals())
    wt = {n: args[n] for n in WEIGHTS}
    mom = {n: args["m_" + n] for n in WEIGHTS}
    var = {n: args["v_" + n] for n in WEIGHTS}

    t, d = x.shape[1], x.shape[2]
    pw = pool_b.shape[1]
    sw = ssm_d.shape[1]
    ngrp = sw // SSM_GROUP
    gn = ngrp * SSM_STATE
    xi, yi, ci = _place()
    b_me = 4 * xi + 2 * yi + ci
    s_me = 2 * xi + yi
    x2d = x[0]
    tgt = loss_target[0]

    c_all = _allgather_small("ag_c", c.reshape(8, d // 8)).reshape(N_DEV, d)
    ncol = w_ada.shape[2]
    b_sh = lax.dynamic_slice(b_ada, (0, s_me * ncol), (1, ncol))
    mod_sh = _ada_fwd("ada_fwd", c_all, w_ada[0], b_sh)
    md_send, md_recv, mod_sh, md_land, tok = _bcast_start("mod_start", mod_sh)

    shards = [wt[n][0].astype(BF16) for n in BIG]
    lands = _place_shards(shards, s_me)
    n0 = len(AG_GROUPS[0])
    sems_a, shards_a, lands_a, tok = _ag_start("ag_start_first", shards[:n0], lands[:n0], AG_GROUPS[:1], AG_DIRECT[:1], deps=(tok,))
    rest = [[w - n0 for w in grp] for grp in AG_GROUPS[1:]]
    sems_b, shards_b, lands_b, tok = _ag_start("ag_start_rest", shards[n0:], lands[n0:], rest, AG_DIRECT[1:], deps=(tok,))
    ag_sems, shards_t, lands_t = sems_a + sems_b, shards_a + shards_b, lands_a + lands_b
    md_land = _bcast_wait("mod_wait", mod_sh, md_land, md_send, md_recv, tok)
    mod_all = lax.dynamic_update_slice(md_land, mod_sh[None], (b_me, 0, 0))
    full = {}

    def weights(gi, *after):
        grp = AG_GROUPS[gi]
        if BIG[grp[0]] not in full:
            ls = _ag_wait("ag_wait%d" % gi, [shards_t[w] for w in grp], [lands_t[w] for w in grp], *ag_sems[gi],
                          AG_DIRECT[gi], after)
            for w, l in zip(grp, ls if AG_DIRECT[gi] else _ag_forward("ag_fwd%d" % gi, ls)):
                full[BIG[w]] = l
        return full

    mod_me = jnp.concatenate([lax.dynamic_slice(mod_all, (2 * s, b_me, 0), (1, 1, ncol))[0] for s in range(N_CHIPS)], axis=1)
    mod = mod_me.reshape(9, d)
    sh1, sc1, gt1, sh2, sc2, gt2, sh3, sc3, gt3 = [mod[k:k + 1] for k in range(9)]

    f = w_ffn1_out.shape[1] * N_CHIPS

    col = lambda a: a.reshape(gn, 1)
    lrl_c, li_c = col(ssm_lam_re_log), col(ssm_lam_im)
    ldt_c = col(jnp.broadcast_to(ssm_log_dt.reshape(ngrp, 1), (ngrp, SSM_STATE)))
    b_re2, b_im2 = ssm_b_re.reshape(gn, SSM_GROUP), ssm_b_im.reshape(gn, SSM_GROUP)
    lrdt_c, ang_c, bb_re, bb_im = _ssm_prep("ssm_prep", lrl_c, li_c, ldt_c, b_re2, b_im2)
    lrdt, ang = lrdt_c.reshape(1, gn), ang_c.reshape(1, gn)
    tr = lambda a: jnp.swapaxes(a, 1, 2)
    bbd = [_blockdiag_b(v, sw).astype(BF16) for v in (bb_re, bb_im)]
    ccd = [_blockdiag_c(v[0], sw).astype(BF16) for v in (ssm_c_re, ssm_c_im)]
    bbd_t, ccd_t = [tr(v) for v in bbd], [tr(v) for v in ccd]
    early = [n for n in SMALL if n not in SMALL_LATE]
    packs = {}
    for tag, names, extra in (("early", early, []), ("late", SMALL_LATE, [jnp.zeros((128,), F32)])):
        packs[tag] = [_pack([src[n] for n in names] + extra) for src in (wt, mom, var)]
    shadow = [lrdt, ang, ldt_c, *bbd, *ccd, *bbd_t, *ccd_t, *packs["early"], *packs["late"]]

    h1 = _norm_fwd("ffn1_norm", x2d, g_ffn1, sc1, sh1, deps=(tok,))
    x1, h2, sav1 = _ffn_fwd("ffn1", x2d, h1, gt1, lambda after: weights(0, after, *shadow)['w_ffn1_in'],
                            lambda after: weights(1, after)['w_ffn1_out'].reshape(f, d), (g_mix, sc2, sh2))
    weights(2, h2)
    wo = full['w_out'].reshape(d, d)
    uin = _mm("mix_in", h2, full['w_in'], b_stacked=True, tm=1024, tn=768, j_outer=True)
    p_pool, z_pool, y_pool = _pool_fwd("pool_fwd", uin, pool_w[0], pool_b, pool_scale, full['w_pool_up'], pw)
    y_s, ge, s_re, s_im = _ssm_fwd("ssm_fwd", uin, lrdt, ang, *bbd, *ccd, ssm_d, sw)
    epi, epi_ins, epi_outs = _glu_epilogue(b_glu, t, sw)
    gv, sg = _mm("glu_in", ge, full['w_glu'], b_stacked=True, tm=ROWS, tn=d, epi=epi, epi_ins=epi_ins, epi_outs=epi_outs,
                 b_resident=True)
    epi, epi_ins, epi_outs = _gates_epilogue(y_pool, uin, t, d, pw)
    y_ssm, merged = _mm("ssm_up", sg, full['w_ssm_up'], b_stacked=True, tm=ROWS, tn=d, epi=epi, epi_ins=epi_ins,
                        epi_outs=epi_outs, b_resident=True)
    epi, epi_ins, epi_outs = _gate_norm_epilogue(1.0, x1, gt2, (g_ffn2, sc3, sh3), t, d)
    x2, y2, h3 = _mm("mix_out", merged, wo, tm=ROWS, tn=d, epi=epi, epi_ins=epi_ins, epi_outs=epi_outs, b_resident=True)

    (dx3, dy3, dg_final, loss_v, dgt3), sav3 = _ffn_fwd(
        "ffn2", x2, h3, gt3, lambda after: weights(3, after)['w_ffn2_in'],
        lambda after: weights(3, after)['w_ffn2_out'].reshape(f, d), loss=(tgt, _row(g_final)))

    gs = {}
    rs_open = []

    def start_rs(names, gbs):
        send, recv, g_thru, land_thru, token = _rs_start("rs_start_" + names[0], gbs)
        rs_open.append((names, send, recv, g_thru, land_thru))
        return token

    dx2, dy2, dsh3, dsc3, gs['g_ffn2'], dgt2 = _ffn_bwd(
        "ffn2b", dx3, dy3, sav3, g_ffn2, sc3, full['w_ffn2_in'], full['w_ffn2_out'].reshape(f, d),
        lambda key, g: start_rs(['w_ffn2_' + key], [g]), (y2, gt2, 1.0))

    epi, epi_ins, epi_outs = _gates_bwd_epilogue(y_pool, y_ssm, uin, t, d, pw)
    dy_pool, dy_ssm, duin, g_wo = _mm("mix_dmerged", dy2, wo, tb=True, tm=ROWS, tn=d, epi=epi, epi_ins=epi_ins,
                                      epi_outs=epi_outs, b_resident=True, wgrad=merged, wgrad_rows=True)
    g_wo = g_wo.reshape(N_CHIPS, d // N_CHIPS, d)

    g_wpu = _mm("pool_dwup", p_pool, dy_pool, ta=True, out_stacked=True, tm=pw, tn=d, tk=4096)
    duin, gs['pool_w'], gs['pool_b'], gs['pool_scale'] = _pool_bwd(
        "pool_bwd", dy_pool, z_pool, pool_w[0], pool_b, pool_scale, full['w_pool_up'], duin)

    epi, epi_ins, epi_outs, epi_reds = _glu_bwd_epilogue(gv, b_glu, t, sw)
    dgv, gs['b_glu'], g_wsu = _mm("ssm_dup", dy_ssm, full['w_ssm_up'], tb=True, b_stacked=True, tm=ROWS, tn=sw, tk=d,
                                  epi=epi, epi_ins=epi_ins, epi_outs=epi_outs, epi_reds=epi_reds, b_resident=True, wgrad=sg)
    tok = start_rs(['w_out', 'w_pool_up', 'w_ssm_up'], [g_wo, g_wpu, g_wsu])
    dge, g_wglu = _mm("glu_dge", dgv, full['w_glu'], tb=True, b_stacked=True, tm=1024, tn=sw, tk=d, deps=(tok,), wgrad=ge)
    (duin, g_abre, g_abim, g_bbd_re, g_bbd_im, g_ccd_re, g_ccd_im, gs['ssm_d']) = _ssm_bwd(
        "ssm_bwd", dge, y_s, uin, s_re, s_im, lrdt, ang, *bbd_t, *ccd_t, ssm_d, sw, duin)
    gs['ssm_c_re'], gs['ssm_c_im'] = _diag_of_c(g_ccd_re, sw), _diag_of_c(g_ccd_im, sw)
    d_lrl, d_li, d_ldt, d_bre, d_bim = _ssm_param_bwd(
        "ssm_param_bwd", lrl_c, li_c, ldt_c, b_re2, b_im2, g_abre.reshape(gn, 1), g_abim.reshape(gn, 1),
        _diag_of_b(g_bbd_re, sw), _diag_of_b(g_bbd_im, sw))
    gs['ssm_lam_re_log'], gs['ssm_lam_im'] = d_lrl, d_li
    gs['ssm_log_dt'] = jnp.sum(d_ldt.reshape(ngrp, SSM_STATE), axis=1)
    gs['ssm_b_re'], gs['ssm_b_im'] = d_bre, d_bim

    g_win =_mm("mix_dwin", h2, duin, ta=True, out_stacked=True, tm=d, tn=768, tk=4096)
    tok = start_rs(['w_glu', 'w_in'], [g_wglu, g_win])
    epi, epi_ins, epi_outs, epi_reds = _norm_bwd_epilogue(x1, dx2, g_mix, sc2, (sav1[4], gt1, 0.5), t, d)
    dx1, dy1, dsh2, dsc2, gs['g_mix'], dgt1 = _mm(
        "mix_dh", duin, full['w_in'], tb=True, b_stacked=True, tn=d, tk=3072, deps=(tok,), epi=epi, epi_ins=epi_ins,
        epi_outs=epi_outs, epi_reds=epi_reds, b_resident=True)

    gs['g_final'] = dg_final
    sg_blk = _pack([gs[n] for n in early])
    sg_send, sg_recv, sg_blk, sg_land, tok = _bcast_start("sg_start", sg_blk)

    dx0, _, dsh1, dsc1, gs['g_ffn1'], _ = _ffn_bwd(
        "ffn1b", dx1, dy1, sav1, g_ffn1, sc1, full['w_ffn1_in'], full['w_ffn1_out'].reshape(f, d),
        lambda key, g: start_rs(['w_ffn1_' + key], [g]), None, deps=(tok,))

    gs['b_ada'] = jnp.concatenate([dsh1, dsc1, dgt1, dsh2, dsc2, dgt2, dsh3, dsc3, dgt3], axis=1)
    lt_blk = _pack([gs[n] for n in SMALL_LATE] + [loss_v])
    lt_send, lt_recv, lt_blk, lt_land, tok = _bcast_start("late_start", lt_blk)

    grads, delta, new_m, new_v = {}, {}, {}, {}

    def small_update(tag, names, blk, land):
        g8 = lax.dynamic_update_slice(land, blk[None], (b_me, 0, 0)).reshape(-1, 128)
        w_pack, m_pack, v_pack = packs[tag]
        shapes = [wt[n].shape for n in names]
        per_param, packed = _adamw_small("adamw_small_" + tag, w_pack, g8, m_pack, v_pack, shapes)
        for k, dst in enumerate((grads, delta, new_m, new_v)):
            rest = _unpack(packed[k], shapes)
            for i, n in enumerate(names):
                dst[n] = per_param[i][k] if per_param[i] is not None else rest[i]
        return g8

    after = tok
    for group in RS_SUM_GROUPS:
        names, g_done, land_done = [], [], []
        for k in group:
            nk, send, recv, g_thru, land_thru = rs_open[k]
            gd, ld = _rs_wait("rs_wait_" + nk[0], g_thru, land_thru, send, recv, after)
            names, g_done, land_done = names + nk, g_done + gd, land_done + ld
        for n, g_sum in zip(names, _rs_sum("rs_sum_" + names[0], g_done, land_done)):
            g_out, dl, mn, vn = _adamw("adamw_" + n, wt[n][0], g_sum, mom[n][0], var[n][0])
            grads[n], delta[n], new_m[n], new_v[n] = g_out[None], dl[None], mn[None], vn[None]
            after = dl
        if group is RS_SUM_GROUPS[-2]:
            small_update("early", early, sg_blk, _bcast_wait("sg_wait", sg_blk, sg_land, sg_send, sg_recv, after))

    lt_land = _bcast_wait("late_wait", lt_blk, lt_land, lt_send, lt_recv, after)
    late8 = small_update("late", SMALL_LATE, lt_blk, lt_land).reshape(N_DEV, -1)
    loss = jnp.sum(late8[:, 10 * d])
    dmod_sh = lax.dynamic_slice(late8, (0, s_me * ncol), (N_DEV, ncol))
    g_w_ada = _ada_bwd("ada_bwd", c_all, dmod_sh)
    dl, mn, vn = _adamw("adamw_w_ada", w_ada[0], g_w_ada, m_w_ada[0], v_w_ada[0], with_g=False)
    grads['w_ada'], delta['w_ada'], new_m['w_ada'], new_v['w_ada'] = g_w_ada[None], dl[None], mn[None], vn[None]

    return (loss, dx0[None], *[grads[n] for n in WEIGHTS], *[delta[n] for n in WEIGHTS],
            *[new_m[n] for n in WEIGHTS], *[new_v[n] for n in WEIGHTS])
```

```python
import functools
import math

import jax
import jax.numpy as jnp
from jax import lax
from jax.experimental import pallas as pl
from jax.experimental.pallas import tpu as pltpu

F32 = jnp.float32
BF16 = jnp.bfloat16
MESH = pl.DeviceIdType.MESH

EPS = 1e-6
POOL_WINDOWS = (2, 4, 8, 16)
POOL_HALO = 16
SSM_GROUP = 16
SSM_STATE = 64
SSM_BLOCKS = 4
N_DEV = 8
N_CHIPS = 4
ADAM_LR = 0.001
ADAM_B1 = 0.9
ADAM_B2 = 0.999
ADAM_EPS = 1e-08
ADAM_WD = 0.01
ADAM_STEP = 10
VMEM_LIMIT = 56 * 1024 * 1024
ROWS = 512


ANY_SPEC = pl.BlockSpec(memory_space=pl.ANY)
HBM_SPEC = pl.BlockSpec(memory_space=pltpu.HBM)
SEM_SPEC = pl.BlockSpec(memory_space=pltpu.SEMAPHORE)
EFFECT = pltpu.SideEffectType.DATAFLOW_SIDE_EFFECTING


def _hbm(a):
    return pltpu.with_memory_space_constraint(a, pltpu.HBM)


def _pcall(body, **kw):
    return pl.pallas_call(body, **kw)


def _params(*sem):
    return pltpu.CompilerParams(dimension_semantics=sem, vmem_limit_bytes=VMEM_LIMIT)


def _pick(n, cap, mult=128):
    if n <= cap:
        return n
    best = None
    for d in range(mult, cap + 1, mult):
        if n % d == 0:
            best = d
    assert best is not None, (n, cap, mult)
    return best


def _sigmoid(v):
    return 1.0 / (1.0 + jnp.exp(-v))


def _rowwise(name, fn, ins, params, outs, reds, tm, deps=()):
    t = ins[0][0].shape[0]
    tm = min(tm, t)
    nb = t // tm
    ni, npar, no, nd = len(ins), len(params), len(outs), len(deps)

    def body(*refs):
        iv = [r[...] for r in refs[:ni]]
        pv = [r[...] for r in refs[ni:ni + npar]]
        o_refs = refs[ni + npar + nd:ni + npar + nd + no]
        r_refs = refs[ni + npar + nd + no:]
        ovals, rvals = fn(iv, pv)
        for o_ref, val in zip(o_refs, ovals):
            off = 0
            if isinstance(val, tuple) and val[0] == "at":
                _, off, val = val
            parts = val if isinstance(val, (list, tuple)) else [val]
            for p in parts:
                o_ref[:, off:off + p.shape[1]] = p.astype(o_ref.dtype)
                off += p.shape[1]
        if r_refs:
            @pl.when(pl.program_id(0) == 0)
            def _():
                for r in r_refs:
                    r[...] = jnp.zeros_like(r)
            for r, val in zip(r_refs, rvals):
                r[...] += val

    in_specs = [pl.BlockSpec((tm, w), functools.partial(lambda i, cb: (i, cb), cb=cb)) for (_, w, cb) in ins]
    in_specs += [pl.BlockSpec(p.shape, lambda i: (0, 0)) for p in params]
    in_specs += [ANY_SPEC] * nd
    out_shape = [jax.ShapeDtypeStruct((t, w), dt) for (w, dt) in outs]
    out_shape += [jax.ShapeDtypeStruct((1, w), F32) for w in reds]
    out_specs = [pl.BlockSpec((tm, w), lambda i: (i, 0)) for (w, _) in outs]
    out_specs += [pl.BlockSpec((1, w), lambda i: (0, 0)) for w in reds]
    res = _pcall(body, name=name, grid=(nb,), in_specs=in_specs, out_specs=out_specs, out_shape=out_shape,
                 compiler_params=_params("arbitrary"))(*[a for a, _, _ in ins], *params, *deps)
    return res


def _colsum(v):
    return jnp.sum(v, axis=0, keepdims=True)


def _mm(name, a, b, *, ta=False, tb=False, b_stacked=False, out_stacked=False, tm=ROWS, tn=1024, tk=2816,
        out_dtype=BF16, epi=None, epi_ins=(), epi_outs=None, epi_reds=(), deps=(), j_outer=False, a_halves=False,
        b_halves=False, b_resident=False, wgrad=None, wgrad_rows=False):
    if a_halves:
        _, m, kdim = a.shape
        kdim *= 2
    elif ta:
        kdim, m = a.shape
    else:
        m, kdim = a.shape
    ns = None
    if b_stacked:
        ns = b.shape[2]
        n = b.shape[1] if tb else N_CHIPS * ns
        assert kdim == (N_CHIPS * ns if tb else b.shape[1]), (name, a.shape, b.shape)
    else:
        if b_halves:
            n = 2 * b.shape[2]
            assert kdim == b.shape[1] and not tb, (name, a.shape, b.shape)
        else:
            n = b.shape[0] if tb else b.shape[1]
            assert kdim == (b.shape[1] if tb else b.shape[0]), (name, a.shape, b.shape)
        if out_stacked:
            ns = n // N_CHIPS

    def shards(want):
        return max(g for g in (1, 2, 4) if g * ns <= max(want, ns))

    tm = _pick(m, tm, 128 if ta else 8)
    gn = gk = 1
    if (b_stacked and not tb) or out_stacked:
        gn = shards(min(tn, n // 2) if b_halves else tn)
        tn = gn * ns
    else:
        tn = _pick(n, tn)
    if b_stacked and tb:
        gk = shards(tk)
        tk = gk * ns
    else:
        tk = _pick(kdim, tk, 8 if ta else 128)
    nm, nn, nk = m // tm, n // tn, kdim // tk

    def ij(f):
        return (lambda g0, g1, k: f(g1, g0, k)) if j_outer else f

    if a_halves:
        assert b_stacked and tb and gk == N_CHIPS and nk == 1, name
        a_spec = pl.BlockSpec((2, tm, kdim // 2), ij(lambda i, j, k: (0, i, 0)))
    elif ta:
        a_spec = pl.BlockSpec((tk, tm), ij(lambda i, j, k: (k, i)))
    else:
        a_spec = pl.BlockSpec((tm, tk), ij(lambda i, j, k: (i, k)))
    if b_stacked and not tb:
        b_spec = pl.BlockSpec((gn, tk, ns), ij(lambda i, j, k: (j, k, 0)))
    elif b_stacked and tb:
        b_spec = pl.BlockSpec((gk, tn, ns), ij(lambda i, j, k: (k, j, 0)))
    elif b_halves:
        bph = (n // 2) // tn
        b_spec = pl.BlockSpec((None, tk, tn), ij(lambda i, j, k: (j // bph, k, j % bph)))
    elif tb:
        b_spec = pl.BlockSpec((tn, tk), ij(lambda i, j, k: (j, k)))
    else:
        b_spec = pl.BlockSpec((tk, tn), ij(lambda i, j, k: (k, j)))
    dims = (((0 if ta else 1,), (1 if tb else 0,)), ((), ()))

    if epi_outs is None:
        if out_stacked:
            epi_outs = [((N_CHIPS, m, ns), out_dtype, (gn, tm, ns), lambda i, j: (j, i, 0))]
        else:
            epi_outs = [((m, n), out_dtype, (tm, tn), lambda i, j: (i, j))]
    ne, no, nd, nr = len(epi_ins), len(epi_outs), len(deps), len(epi_reds)
    nx = 0 if wgrad is None else 1
    assert not (nr or nx) or (nn == 1 and not j_outer), name
    assert not nx or (nk == 1 and not ta and not a_halves and kdim % N_CHIPS == 0), name

    def body(a_ref, b_ref, *rest):
        e_refs = rest[:ne]
        x_refs = rest[ne:ne + nx]
        o_refs = rest[ne + nx + nd:ne + nx + nd + no]
        r_refs = rest[ne + nx + nd + no:ne + nx + nd + no + nr]
        wg_refs = rest[ne + nx + nd + no + nr:ne + nx + nd + no + nr + nx]
        scratch = rest[ne + nx + nd + no + nr + nx:]
        av = None if a_halves else a_ref[...].astype(BF16)
        if nx:
            wg_acc = scratch[-1]
            i = pl.program_id(0)
            pw_ = lax.dot_general(x_refs[0][...].astype(BF16), av, (((0,), (0,)), ((), ())), preferred_element_type=F32)

            @pl.when(i == 0)
            def _():
                wg_acc[...] = pw_

            @pl.when(i > 0)
            def _():
                wg_acc[...] += pw_

            @pl.when(i == nm - 1)
            def _():
                if wgrad_rows:
                    wg_refs[0][...] = wg_acc[...].astype(BF16)
                else:
                    ks = kdim // N_CHIPS
                    for s in range(N_CHIPS):
                        wg_refs[0][s] = wg_acc[:, s * ks:(s + 1) * ks].astype(BF16)
        if b_stacked and not tb:
            parts = [lax.dot_general(av, b_ref[s].astype(BF16), dims, preferred_element_type=F32) for s in range(gn)]
        elif b_stacked and tb:
            p = None
            for s in range(gk):
                if a_halves:
                    a_s = a_ref[s // 2, :, (s % 2) * ns:(s % 2 + 1) * ns].astype(BF16)
                else:
                    a_s = av[:, s * ns:(s + 1) * ns]
                q = lax.dot_general(a_s, b_ref[s].astype(BF16), dims, preferred_element_type=F32)
                p = q if p is None else p + q
            parts = [p]
        else:
            parts = [lax.dot_general(av, b_ref[...].astype(BF16), dims, preferred_element_type=F32)]

        def finish(acc_parts):
            if epi is None and out_stacked:
                acc = acc_parts[0]
                for s in range(gn):
                    o_refs[0][s] = acc[:, s * ns:(s + 1) * ns].astype(out_dtype)
            elif epi is None:
                w = acc_parts[0].shape[1]
                for s, part in enumerate(acc_parts):
                    o_refs[0][:, s * w:(s + 1) * w] = part.astype(out_dtype)
            else:
                acc = acc_parts[0] if len(acc_parts) == 1 else jnp.concatenate(acc_parts, axis=1)
                vals = epi(acc, *[r[...] for r in e_refs])
                if nr:
                    vals, reds = vals

                    @pl.when(pl.program_id(0) == 0)
                    def _():
                        for r in r_refs:
                            r[...] = jnp.zeros_like(r)
                    for r, val in zip(r_refs, reds):
                        r[...] += val
                for o_ref, val in zip(o_refs, vals):
                    if isinstance(val, tuple) and val[0] == "at":
                        o_ref[:, val[1]:val[1] + val[2].shape[1]] = val[2].astype(o_ref.dtype)
                    else:
                        o_ref[...] = val.astype(o_ref.dtype)

        if nk == 1:
            finish(parts)
        else:
            acc_ref = scratch[0]
            k = pl.program_id(2)
            w = parts[0].shape[1]

            @pl.when(k == 0)
            def _():
                for s, part in enumerate(parts):
                    acc_ref[:, s * w:(s + 1) * w] = part

            @pl.when(k > 0)
            def _():
                for s, part in enumerate(parts):
                    acc_ref[:, s * w:(s + 1) * w] += part

            @pl.when(k == nk - 1)
            def _():
                finish([acc_ref[...]])

    def _ij(f):
        return ij(lambda i, j, k: f(i, j))

    if b_resident:
        assert nk == 1 and (nn == 1 or j_outer), name
        b_spec = pl.BlockSpec(b_spec.block_shape, b_spec.index_map, pipeline_mode=pl.Buffered(1))
    in_specs = [a_spec, b_spec] + [pl.BlockSpec(blk, _ij(f)) for (_, blk, f) in epi_ins]
    if nx:
        in_specs.append(pl.BlockSpec((tm, wgrad.shape[1]), lambda i, j, k: (i, 0)))
    in_specs += [ANY_SPEC] * nd
    out_specs = [pl.BlockSpec(blk, _ij(f)) for (_, _, blk, f) in epi_outs]
    out_specs += [pl.BlockSpec((1, w), lambda *_: (0, 0)) for w in epi_reds]
    out_shape = [jax.ShapeDtypeStruct(s, dt) for (s, dt, _, _) in epi_outs] + [jax.ShapeDtypeStruct((1, w), F32) for w in epi_reds]
    scratch = [pltpu.VMEM((tm, tn), F32)] if nk > 1 else []
    if nx:
        wg_shape = (wgrad.shape[1], kdim) if wgrad_rows else (N_CHIPS, wgrad.shape[1], kdim // N_CHIPS)
        out_specs.append(pl.BlockSpec(wg_shape, lambda *_: (0,) * len(wg_shape)))
        out_shape.append(jax.ShapeDtypeStruct(wg_shape, BF16))
        scratch.append(pltpu.VMEM((wgrad.shape[1], kdim), F32))
    grid = (nn, nm, nk) if j_outer else (nm, nn, nk)
    sem = ("arbitrary",) * 3 if (nr or nx) else ("parallel", "parallel", "arbitrary")
    res = _pcall(body, name=name, grid=grid, in_specs=in_specs, out_specs=out_specs, out_shape=out_shape, scratch_shapes=scratch,
                 compiler_params=_params(*sem))(a, b, *[x for x, _, _ in epi_ins], *([wgrad] * nx), *deps)
    return res[0] if len(res) == 1 else res


def _norm_fwd(name, x, g, sc, sh, deps=()):
    d = x.shape[1]

    def fn(iv, pv):
        (xv,), (gv, scv, shv) = iv, pv
        r = lax.rsqrt(jnp.mean(xv * xv, axis=-1, keepdims=True) + EPS)
        return [xv * r * gv * (1.0 + scv) + shv], []

    return _rowwise(name, fn, [(x, d, 0)], [g, sc, sh], [(d, BF16)], [], ROWS, deps=deps)[0]


def _adamw(name, w, g, m, v, tm=256, with_g=True):
    c = w.shape[1]

    def fn(iv, pv):
        wv, gv, mv, vv = iv
        mn = ADAM_B1 * mv + (1.0 - ADAM_B1) * gv
        vn = ADAM_B2 * vv + (1.0 - ADAM_B2) * (gv * gv)
        m_hat = mn / (1.0 - ADAM_B1 ** ADAM_STEP)
        v_hat = vn / (1.0 - ADAM_B2 ** ADAM_STEP)
        delta = -ADAM_LR * (m_hat / (jnp.sqrt(v_hat) + ADAM_EPS) + ADAM_WD * wv)
        return ([gv] if with_g else []) + [delta, mn, vn], []

    return _rowwise(name, fn, [(w, c, 0), (g, c, 0), (m, c, 0), (v, c, 0)], [], [(c, F32)] * (4 if with_g else 3), [],
                    _pick(w.shape[0], tm, 8))


def _unpack_plan(shape):
    n = math.prod(shape)
    if len(shape) == 2 and shape[0] == 1 and n % 128 == 0:
        return [((slice(None), slice(128 * r, 128 * (r + 1))), slice(r, r + 1), slice(None)) for r in range(n // 128)]
    if len(shape) == 2 and shape[0] == 1 and n < 128:
        return [((slice(None), slice(None)), slice(0, 1), slice(0, n))]
    if len(shape) == 1 and n % 128 == 0:
        return [((slice(128 * r, 128 * (r + 1)),), r, slice(None)) for r in range(n // 128)]
    if len(shape) == 3 and shape[0] == 1 and shape[2] == 64:
        return [((0, slice(2 * r + h, 2 * r + h + 1), slice(None)), slice(r, r + 1), slice(64 * h, 64 * (h + 1)))
                for r in range(n // 128) for h in range(2)]
    if len(shape) == 4 and shape[0] == 1 and shape[2:] == (128, 128):
        return [((0, k), slice(128 * k, 128 * (k + 1)), slice(None)) for k in range(shape[1])]
    return None


def _adamw_small(name, w, g8, m, v, shapes):
    r = w.shape[0]
    plans, rows0, off = [], [], 0
    for s in shapes:
        plans.append(_unpack_plan(s))
        rows0.append(off // 128)
        off += math.prod(s) + (-math.prod(s)) % 128
    direct = [i for i, p in enumerate(plans) if p is not None]

    def body(w_ref, g_ref, m_ref, v_ref, *out):
        packed = out[4 * len(direct):]
        gv = g_ref[0:r, :]
        for k in range(1, N_DEV):
            gv = gv + g_ref[k * r:(k + 1) * r, :]
        mn = ADAM_B1 * m_ref[...] + (1.0 - ADAM_B1) * gv
        vn = ADAM_B2 * v_ref[...] + (1.0 - ADAM_B2) * (gv * gv)
        m_hat = mn / (1.0 - ADAM_B1 ** ADAM_STEP)
        v_hat = vn / (1.0 - ADAM_B2 ** ADAM_STEP)
        packed[0][...] = gv
        packed[1][...] = -ADAM_LR * (m_hat / (jnp.sqrt(v_hat) + ADAM_EPS) + ADAM_WD * w_ref[...])
        packed[2][...] = mn
        packed[3][...] = vn
        for di, i in enumerate(direct):
            for kind in range(4):
                o_ref, src = out[4 * di + kind], packed[kind]
                for o_idx, row, lanes in plans[i]:
                    row = (rows0[i] + row) if isinstance(row, int) else slice(rows0[i] + row.start, rows0[i] + row.stop)
                    o_ref[o_idx] = src[row, lanes]

    out_shape = [jax.ShapeDtypeStruct(shapes[i], F32) for i in direct for _ in range(4)]
    out_shape += [jax.ShapeDtypeStruct((r, 128), F32)] * 4
    res = _pcall(body, name=name, out_shape=out_shape,
                 compiler_params=pltpu.CompilerParams(vmem_limit_bytes=VMEM_LIMIT))(w, g8, m, v)
    per_param = [None] * len(shapes)
    for di, i in enumerate(direct):
        per_param[i] = res[4 * di:4 * di + 4]
    return per_param, res[4 * len(direct):]


def _pool_fwd(name, uin, pool_w, pool_b, pool_scale, w_up, pw, tm=ROWS):
    t = uin.shape[0]
    tm = min(tm, t)
    ng = len(POOL_WINDOWS)
    gw = pw // ng
    ns = w_up.shape[2]

    def body(u_ref, w_ref, b_ref, s_ref, wu_ref, p_ref, z_ref, y_ref, ext):
        i = pl.program_id(0)

        @pl.when(i == 0)
        def _():
            ext[0:POOL_HALO, :] = jnp.zeros((POOL_HALO, pw), F32)

        u = u_ref[...].astype(F32)
        ext[POOL_HALO:POOL_HALO + tm, :] = u
        pos = i * tm + lax.broadcasted_iota(jnp.int32, (tm, 1), 0)
        for k, win in enumerate(POOL_WINDOWS):
            cols = slice(k * gw, (k + 1) * gw)
            acc = u[:, cols]
            for j in range(1, win):
                acc = acc + ext[POOL_HALO - j:POOL_HALO - j + tm, cols]
            cnt = jnp.minimum(pos + 1, win).astype(F32)
            z = acc / cnt - u[:, cols]
            zp = jnp.dot(z.astype(BF16), w_ref[k].astype(BF16), preferred_element_type=F32) + b_ref[:, cols]
            p_ref[:, cols] = (zp * s_ref[:, cols]).astype(BF16)
            z_ref[:, cols] = z.astype(BF16)
        ext[0:POOL_HALO, :] = u[tm - POOL_HALO:tm, :]
        pv = p_ref[...]
        for s in range(N_CHIPS):
            y_ref[:, s * ns:(s + 1) * ns] = jnp.dot(pv, wu_ref[s], preferred_element_type=F32).astype(BF16)

    full3 = lambda i: (0, 0, 0)
    return _pcall(
        body, name=name, grid=(t // tm,),
        in_specs=[pl.BlockSpec((tm, pw), lambda i: (i, 0)), pl.BlockSpec(pool_w.shape, full3),
                  pl.BlockSpec(pool_b.shape, lambda i: (0, 0)), pl.BlockSpec(pool_scale.shape, lambda i: (0, 0)),
                  pl.BlockSpec(w_up.shape, full3)],
        out_specs=[pl.BlockSpec((tm, pw), lambda i: (i, 0))] * 2 + [pl.BlockSpec((tm, N_CHIPS * ns), lambda i: (i, 0))],
        out_shape=[jax.ShapeDtypeStruct((t, pw), BF16)] * 2 + [jax.ShapeDtypeStruct((t, N_CHIPS * ns), BF16)],
        scratch_shapes=[pltpu.VMEM((POOL_HALO + tm, pw), F32)],
        compiler_params=_params("arbitrary"))(uin, pool_w, pool_b, pool_scale, w_up)


def _pool_bwd(name, dy, z, p, pool_w, pool_b, pool_scale, w_up, duin, tm=ROWS):
    t, pw = z.shape
    tm = min(tm, t)
    nb = t // tm
    ng = len(POOL_WINDOWS)
    gw = pw // ng
    ns = w_up.shape[2]

    def body(dy_ref, z_ref, p_ref, w_ref, b_ref, s_ref, wu_ref, duin_ref, du_ref, dw_ref, db_ref, ds_ref, dwu_ref, ext, dwu_acc):
        i = pl.program_id(0)

        @pl.when(i == 0)
        def _():
            ext[tm:tm + POOL_HALO, :] = jnp.zeros((POOL_HALO, pw), F32)
            dw_ref[...] = jnp.zeros_like(dw_ref)
            db_ref[...] = jnp.zeros_like(db_ref)
            ds_ref[...] = jnp.zeros_like(ds_ref)
            dwu_acc[...] = jnp.zeros_like(dwu_acc)

        dwu_acc[...] += lax.dot_general(p_ref[...], dy_ref[...], (((0,), (0,)), ((), ())), preferred_element_type=F32)

        @pl.when(i == nb - 1)
        def _():
            for s in range(N_CHIPS):
                dwu_ref[s] = dwu_acc[:, s * ns:(s + 1) * ns].astype(BF16)

        pos = (nb - 1 - i) * tm + lax.broadcasted_iota(jnp.int32, (tm, 1), 0)
        dp = None
        for s in range(N_CHIPS):
            q = lax.dot_general(dy_ref[:, s * ns:(s + 1) * ns], wu_ref[s], (((1,), (1,)), ((), ())), preferred_element_type=F32)
            dp = q if dp is None else dp + q
        for k, win in enumerate(POOL_WINDOWS):
            cols = slice(k * gw, (k + 1) * gw)
            zk = z_ref[:, cols]
            dpk = dp[:, cols]
            wk = w_ref[k].astype(BF16)
            zp = jnp.dot(zk, wk, preferred_element_type=F32) + b_ref[:, cols]
            ds_ref[:, cols] += _colsum(dpk * zp)
            dzp = dpk * s_ref[:, cols]
            db_ref[:, cols] += _colsum(dzp)
            dzpb = dzp.astype(BF16)
            dz = lax.dot_general(dzpb, wk, (((1,), (1,)), ((), ())), preferred_element_type=F32)
            dw_ref[k] += lax.dot_general(zk, dzpb, (((0,), (0,)), ((), ())), preferred_element_type=F32)
            cnt = jnp.minimum(pos + 1, win).astype(F32)
            r = dz / cnt
            ext[0:tm, cols] = r
            acc = r - dz
            for j in range(1, win):
                acc = acc + ext[j:j + tm, cols]
            du_ref[:, cols] = acc.astype(BF16)
        ext[tm:tm + POOL_HALO, :] = ext[0:POOL_HALO, :]

    rev = lambda i: (nb - 1 - i, 0)
    full3 = lambda i: (0, 0, 0)
    return _pcall(
        body, name=name, grid=(nb,),
        in_specs=[pl.BlockSpec((tm, N_CHIPS * ns), rev), pl.BlockSpec((tm, pw), rev), pl.BlockSpec((tm, pw), rev),
                  pl.BlockSpec(pool_w.shape, full3), pl.BlockSpec(pool_b.shape, lambda i: (0, 0)),
                  pl.BlockSpec(pool_scale.shape, lambda i: (0, 0)), pl.BlockSpec(w_up.shape, full3), ANY_SPEC],
        out_specs=[pl.BlockSpec((tm, pw), rev), pl.BlockSpec(pool_w.shape, full3),
                   pl.BlockSpec((1, pw), lambda i: (0, 0)), pl.BlockSpec((1, pw), lambda i: (0, 0)), pl.BlockSpec(w_up.shape, full3)],
        out_shape=[jax.ShapeDtypeStruct(duin.shape, BF16), jax.ShapeDtypeStruct(pool_w.shape, F32),
                   jax.ShapeDtypeStruct((1, pw), F32), jax.ShapeDtypeStruct((1, pw), F32), jax.ShapeDtypeStruct(w_up.shape, BF16)],
        scratch_shapes=[pltpu.VMEM((tm + POOL_HALO, pw), F32), pltpu.VMEM((pw, N_CHIPS * ns), F32)], input_output_aliases={7: 0},
        compiler_params=_params("arbitrary"))(dy, z, p, pool_w, pool_b, pool_scale, w_up, duin)


def _ssm_disc(lrl, li, ldt):
    lr = -jnp.exp(lrl)
    dt = jnp.exp(ldt)
    mag = jnp.exp(lr * dt)
    ang = li * dt
    ab_re = mag * jnp.cos(ang)
    ab_im = mag * jnp.sin(ang)
    num_re = ab_re - 1.0
    num_im = ab_im
    den = lr * lr + li * li
    f_re = (num_re * lr + num_im * li) / den
    f_im = (num_im * lr - num_re * li) / den
    return lr, dt, mag, ang, ab_re, ab_im, num_re, num_im, den, f_re, f_im


def _ssm_prep(name, lrl, li, ldt, b_re, b_im):
    gn, h = b_re.shape

    def body(lrl_ref, li_ref, ldt_ref, br_ref, bi_ref, lrdt_ref, ang_ref, bbr_ref, bbi_ref):
        lr, dt, _, ang, _, _, _, _, _, f_re, f_im = _ssm_disc(lrl_ref[...], li_ref[...], ldt_ref[...])
        lrdt_ref[...] = lr * dt
        ang_ref[...] = ang
        br, bi = br_ref[...], bi_ref[...]
        bbr_ref[...] = f_re * br - f_im * bi
        bbi_ref[...] = f_re * bi + f_im * br

    col = jax.ShapeDtypeStruct((gn, 1), F32)
    mat = jax.ShapeDtypeStruct((gn, h), F32)
    return _pcall(body, name=name, out_shape=[col, col, mat, mat])(lrl, li, ldt, b_re, b_im)


def _ssm_param_bwd(name, lrl, li, ldt, b_re, b_im, g_abre, g_abim, g_bbre, g_bbim):
    gn, h = b_re.shape

    def body(lrl_ref, li_ref, ldt_ref, br_ref, bi_ref, gar_ref, gai_ref, gbr_ref, gbi_ref,
             dlrl_ref, dli_ref, dldt_ref, dbr_ref, dbi_ref):
        li_v = li_ref[...]
        lr, dt, mag, ang, ab_re, ab_im, num_re, num_im, den, f_re, f_im = _ssm_disc(lrl_ref[...], li_v, ldt_ref[...])
        br, bi = br_ref[...], bi_ref[...]
        gbr, gbi = gbr_ref[...], gbi_ref[...]
        g_fre = jnp.sum(gbr * br + gbi * bi, axis=1, keepdims=True)
        g_fim = jnp.sum(gbi * br - gbr * bi, axis=1, keepdims=True)
        dbr_ref[...] = gbr * f_re + gbi * f_im
        dbi_ref[...] = gbi * f_re - gbr * f_im
        g_num_re = (g_fre * lr - g_fim * li_v) / den
        g_num_im = (g_fre * li_v + g_fim * lr) / den
        g_den = -(g_fre * f_re + g_fim * f_im) / den
        g_lr = (g_fre * num_re + g_fim * num_im) / den + g_den * 2.0 * lr
        g_li = (g_fre * num_im - g_fim * num_re) / den + g_den * 2.0 * li_v
        g_are = gar_ref[...] + g_num_re
        g_aim = gai_ref[...] + g_num_im
        g_mag = g_are * jnp.cos(ang) + g_aim * jnp.sin(ang)
        g_ang = g_aim * ab_re - g_are * ab_im
        g_lrdt = g_mag * mag
        g_lr = g_lr + g_lrdt * dt
        g_dt = g_lrdt * lr + g_ang * li_v
        g_li = g_li + g_ang * dt
        dlrl_ref[...] = g_lr * lr
        dli_ref[...] = g_li
        dldt_ref[...] = g_dt * dt

    col = jax.ShapeDtypeStruct((gn, 1), F32)
    mat = jax.ShapeDtypeStruct((gn, h), F32)
    return _pcall(body, name=name, out_shape=[col, col, col, mat, mat])(lrl, li, ldt, b_re, b_im, g_abre, g_abim, g_bbre, g_bbim)


def _pow_rows(lrdt, ang, k):
    mag = jnp.exp(k * lrdt)
    return mag * jnp.cos(k * ang), mag * jnp.sin(k * ang)


def _ssm_chunk(t):
    return 256 if t >= 2048 else 128


def _to_segments(dst, srcs, jn):
    for q, src in enumerate(srcs):
        for j in range(jn):
            dst[8 * j:8 * j + 8, 128 * q:128 * (q + 1)] = src[pl.ds(j, 8, stride=jn), :]


def _from_segments(dst, src, q, jn, dtype):
    for s in range(8):
        dst[s * jn:(s + 1) * jn, 128 * q:128 * (q + 1)] = src[q, pl.ds(s, jn, stride=8), :].astype(dtype)


def _fill_rows8(dst_re, dst_im, v_re, v_im):
    for j in range(v_re.shape[0]):
        dst_re[8 * j:8 * j + 8, :] = jnp.broadcast_to(v_re[j:j + 1, :], (8, v_re.shape[1]))
        dst_im[8 * j:8 * j + 8, :] = jnp.broadcast_to(v_im[j:j + 1, :], (8, v_im.shape[1]))


def _cmul(ar, ai, br, bi):
    return ar * br - ai * bi, ar * bi + ai * br


def _cmul_conj(ar, ai, br, bi):
    return ar * br + ai * bi, ar * bi - ai * br


def _ssm_fwd(name, uin, lrdt, ang, bb_re, bb_im, cc_re, cc_im, d_skip, sw):
    t = uin.shape[0]
    gn = lrdt.shape[1]
    lc = _ssm_chunk(t)
    jn = lc // 8
    ub, sb = sw // SSM_BLOCKS, gn // SSM_BLOCKS
    nq = sw // 128
    assert ub == 128 and nq == SSM_BLOCKS

    def body(u_ref, lrdt_ref, ang_ref, bbr_ref, bbi_ref, ccr_ref, cci_ref, d_ref, y_ref, ge_ref, sre_ref, sim_ref,
             p_re, p_im, a_re, a_im, up, yp, cst_re, cst_im, car_re, car_im):
        i = pl.program_id(0)
        lrdt_v, ang_v = lrdt_ref[...], ang_ref[...]

        @pl.when(i == 0)
        def _():
            k = (lax.broadcasted_iota(jnp.int32, (jn, 1), 0) + 1).astype(F32)
            _fill_rows8(p_re, p_im, *_pow_rows(lrdt_v, ang_v, k))
            car_re[...] = jnp.zeros_like(car_re)
            car_im[...] = jnp.zeros_like(car_im)

        for q in range(nq):
            yp[q] = u_ref[:, 128 * q:128 * (q + 1)].astype(F32)
        _to_segments(up, [yp.at[q] for q in range(nq)], jn)
        u = up[...]
        ubf = u.astype(BF16)
        for q in range(SSM_BLOCKS):
            uq = ubf[:, q * ub:(q + 1) * ub]
            a_re[:, q * sb:(q + 1) * sb] = jnp.dot(uq, bbr_ref[q], preferred_element_type=F32)
            a_im[:, q * sb:(q + 1) * sb] = jnp.dot(uq, bbi_ref[q], preferred_element_type=F32)
        a1r, a1i = _pow_rows(lrdt_v, ang_v, 1.0)
        ajr, aji = _pow_rows(lrdt_v, ang_v, float(jn))
        for q in range(0, SSM_BLOCKS, 2):
            cols = slice(q * sb, (q + 2) * sb)
            ar8 = jnp.broadcast_to(a1r[:, cols], (8, 2 * sb))
            ai8 = jnp.broadcast_to(a1i[:, cols], (8, 2 * sb))

            def step(j, carry, cols=cols, ar8=ar8, ai8=ai8):
                sr, si = carry
                rows = pl.ds(pl.multiple_of(j * 8, 8), 8)
                mr, mi = _cmul(ar8, ai8, sr, si)
                nr, ni = mr + a_re[rows, cols], mi + a_im[rows, cols]
                a_re[rows, cols] = nr
                a_im[rows, cols] = ni
                return nr, ni

            lax.fori_loop(1, jn, step, (a_re[0:8, cols], a_im[0:8, cols]), unroll=4)
        er, ei = a_re[lc - 8:lc, :], a_im[lc - 8:lc, :]
        gr, gi = car_re[...], car_im[...]
        for s in range(8):
            cst_re[s:s + 1, :] = gr
            cst_im[s:s + 1, :] = gi
            mr, mi = _cmul(ajr, aji, gr, gi)
            gr, gi = mr + er[s:s + 1, :], mi + ei[s:s + 1, :]
        car_re[...] = gr
        car_im[...] = gi
        for q in range(SSM_BLOCKS):
            cols = slice(q * sb, (q + 1) * sb)
            cr = jnp.tile(cst_re[:, cols], (jn, 1))
            ci = jnp.tile(cst_im[:, cols], (jn, 1))
            mr, mi = _cmul(p_re[:, cols], p_im[:, cols], cr, ci)
            srb, sib = (a_re[:, cols] + mr).astype(BF16), (a_im[:, cols] + mi).astype(BF16)
            sre_ref[:, cols] = srb
            sim_ref[:, cols] = sib
            ycols = slice(q * ub, (q + 1) * ub)
            y = (jnp.dot(srb, ccr_ref[q], preferred_element_type=F32) - jnp.dot(sib, cci_ref[q], preferred_element_type=F32)
                 + d_ref[:, ycols] * u[:, ycols])
            yp[q] = y
            _from_segments(y_ref, yp, q, jn, F32)
            yt = y_ref[:, ycols]
            ge_ref[:, ycols] = (0.5 * yt * (1.0 + lax.erf(yt * (1.0 / math.sqrt(2.0))))).astype(BF16)

    row = lambda i: (0, 0)
    blk3 = lambda i: (0, 0, 0)
    return _pcall(
        body, name=name, grid=(t // lc,),
        in_specs=[pl.BlockSpec((lc, sw), lambda i: (i, 1)), pl.BlockSpec((1, gn), row), pl.BlockSpec((1, gn), row),
                  pl.BlockSpec(bb_re.shape, blk3), pl.BlockSpec(bb_im.shape, blk3),
                  pl.BlockSpec(cc_re.shape, blk3), pl.BlockSpec(cc_im.shape, blk3), pl.BlockSpec((1, sw), row)],
        out_specs=[pl.BlockSpec((lc, sw), lambda i: (i, 0)), pl.BlockSpec((lc, sw), lambda i: (i, 0)),
                   pl.BlockSpec((lc, gn), lambda i: (i, 0)), pl.BlockSpec((lc, gn), lambda i: (i, 0))],
        out_shape=[jax.ShapeDtypeStruct((t, sw), F32), jax.ShapeDtypeStruct((t, sw), BF16),
                   jax.ShapeDtypeStruct((t, gn), BF16), jax.ShapeDtypeStruct((t, gn), BF16)],
        scratch_shapes=[pltpu.VMEM((lc, gn), F32), pltpu.VMEM((lc, gn), F32), pltpu.VMEM((lc, gn), F32), pltpu.VMEM((lc, gn), F32),
                        pltpu.VMEM((lc, sw), F32), pltpu.VMEM((nq, lc, 128), F32),
                        pltpu.VMEM((8, gn), F32), pltpu.VMEM((8, gn), F32), pltpu.VMEM((1, gn), F32), pltpu.VMEM((1, gn), F32)],
        compiler_params=_params("arbitrary"))(uin, lrdt, ang, bb_re, bb_im, cc_re, cc_im, d_skip)


def _ssm_bwd(name, dge, y, uin, s_re, s_im, lrdt, ang, bbt_re, bbt_im, cct_re, cct_im, d_skip, sw, duin):
    t = uin.shape[0]
    gn = lrdt.shape[1]
    lc = _ssm_chunk(t)
    nb = t // lc
    jn = lc // 8
    ub, sb = sw // SSM_BLOCKS, gn // SSM_BLOCKS
    nq = sw // 128
    tail = 16

    def body(dge_ref, y_ref, u_ref, sre_ref, sim_ref, tre_ref, tim_ref, lrdt_ref, ang_ref, btr_ref, bti_ref, ctr_ref, cti_ref,
             d_ref, duin_ref, du_ref, dar_ref, dai_ref, dbr_ref, dbi_ref, dcr_ref, dci_ref, dd_ref,
             q_re, q_im, a_re, a_im, dyp, up, dys, us, dup, cst_re, cst_im, sp_re, sp_im, car_re, car_im):
        i = pl.program_id(0)
        lrdt_v, ang_v = lrdt_ref[...], ang_ref[...]

        @pl.when(i == 0)
        def _():
            k = (jn - lax.broadcasted_iota(jnp.int32, (jn, 1), 0)).astype(F32)
            _fill_rows8(q_re, q_im, *_pow_rows(lrdt_v, ang_v, k))
            car_re[...] = jnp.zeros_like(car_re)
            car_im[...] = jnp.zeros_like(car_im)
            for r in (dar_ref, dai_ref, dbr_ref, dbi_ref, dcr_ref, dci_ref, dd_ref):
                r[...] = jnp.zeros_like(r)

        yv = y_ref[...]
        ut = u_ref[...].astype(F32)
        cdf =0.5 * (1.0 + lax.erf(yv * (1.0 / math.sqrt(2.0))))
        pdf = jnp.exp(-0.5 * yv * yv) * (1.0 / math.sqrt(2.0 * math.pi))
        dyt = dge_ref[...].astype(F32) * (cdf + yv * pdf)
        dd_ref[...] += _colsum(dyt * ut)
        for q in range(nq):
            dys[q] = dyt[:, 128 * q:128 * (q + 1)]
            us[q] = ut[:, 128 * q:128 * (q + 1)]
        _to_segments(dyp, [dys.at[q] for q in range(nq)], jn)
        _to_segments(up, [us.at[q] for q in range(nq)], jn)
        dy = dyp[...]
        u = up[...]
        dyb = dy.astype(BF16)
        ubf = u.astype(BF16)
        for q in range(SSM_BLOCKS):
            dq = dyb[:, q * ub:(q + 1) * ub]
            a_re[:, q * sb:(q + 1) * sb] = jnp.dot(dq, ctr_ref[q], preferred_element_type=F32)
            a_im[:, q * sb:(q + 1) * sb] = -jnp.dot(dq, cti_ref[q], preferred_element_type=F32)
        a1r, a1i = _pow_rows(lrdt_v, ang_v, 1.0)
        ajr, aji = _pow_rows(lrdt_v, ang_v, float(jn))
        for q in range(0, SSM_BLOCKS, 2):
            cols = slice(q * sb, (q + 2) * sb)
            ar8 = jnp.broadcast_to(a1r[:, cols], (8, 2 * sb))
            ai8 = jnp.broadcast_to(a1i[:, cols], (8, 2 * sb))

            def step(jj, carry, cols=cols, ar8=ar8, ai8=ai8):
                sr, si = carry
                rows = pl.ds(pl.multiple_of((jn - 2 - jj) * 8, 8), 8)
                mr, mi = _cmul_conj(ar8, ai8, sr, si)
                nr, ni = mr + a_re[rows, cols], mi + a_im[rows, cols]
                a_re[rows, cols] = nr
                a_im[rows, cols] = ni
                return nr, ni

            lax.fori_loop(0, jn - 1, step, (a_re[lc - 8:lc, cols], a_im[lc - 8:lc, cols]), unroll=4)
        er, ei = a_re[0:8, :], a_im[0:8, :]
        hr, hi = car_re[...], car_im[...]
        for s in range(7, -1, -1):
            cst_re[s:s + 1, :] = hr
            cst_im[s:s + 1, :] = hi
            mr, mi = _cmul_conj(ajr, aji, hr, hi)
            hr, hi = mr + er[s:s + 1, :], mi + ei[s:s + 1, :]
        car_re[...] = hr
        car_im[...] = hi
        first = (i == nb - 1).astype(F32)
        sp_re[0:tail, :] = tre_ref[...].astype(F32) * (1.0 - first)
        sp_im[0:tail, :] = tim_ref[...].astype(F32) * (1.0 - first)
        sp_re[tail:tail + 8, :] = sre_ref[lc - tail:lc, :].astype(F32)[tail - 8:tail]
        sp_im[tail:tail + 8, :] = sim_ref[lc - tail:lc, :].astype(F32)[tail - 8:tail]
        tn_dims = (((0,), (0,)), ((), ()))
        for q in range(SSM_BLOCKS):
            cols = slice(q * sb, (q + 1) * sb)
            ycols = slice(q * ub, (q + 1) * ub)
            cr = jnp.tile(cst_re[:, cols], (jn, 1))
            ci = jnp.tile(cst_im[:, cols], (jn, 1))
            mr, mi = _cmul_conj(q_re[:, cols], q_im[:, cols], cr, ci)
            lam_r, lam_i = a_re[:, cols] + mr, a_im[:, cols] + mi
            s_r, s_i = sre_ref[:, cols], sim_ref[:, cols]
            p0r, p0i = sp_re[tail - 1:tail + 7, cols], sp_im[tail - 1:tail + 7, cols]
            l0r, l0i, l1r, l1i = lam_r[0:8], lam_i[0:8], lam_r[8:lc], lam_i[8:lc]
            pvr, pvi = s_r.astype(F32)[0:lc - 8], s_i.astype(F32)[0:lc - 8]
            dar_ref[:, cols] += _colsum(l1r * pvr + l1i * pvi) + _colsum(l0r * p0r + l0i * p0i)
            dai_ref[:, cols] += _colsum(l1i * pvr - l1r * pvi) + _colsum(l0i * p0r - l0r * p0i)
            lrb, lib = lam_r.astype(BF16), lam_i.astype(BF16)
            dup[q] = (jnp.dot(lrb, btr_ref[q], preferred_element_type=F32)
                      + jnp.dot(lib, bti_ref[q], preferred_element_type=F32) + d_ref[:, ycols] * dy[:, ycols])
            _from_segments(du_ref, dup, q, jn, BF16)
            uq = ubf[:, ycols]
            dbr_ref[q] += lax.dot_general(uq, lrb, tn_dims, preferred_element_type=F32)
            dbi_ref[q] += lax.dot_general(uq, lib, tn_dims, preferred_element_type=F32)
            dq = dyb[:, ycols]
            dcr_ref[q] += lax.dot_general(s_r.astype(BF16), dq, tn_dims, preferred_element_type=F32)
            dci_ref[q] -= lax.dot_general(s_i.astype(BF16), dq, tn_dims, preferred_element_type=F32)

    rev = lambda i: (nb - 1 - i, 0)
    tailmap = lambda i: (jnp.maximum((nb - 1 - i) * (lc // tail) - 1, 0), 0)
    row = lambda i: (0, 0)
    blk3 = lambda i: (0, 0, 0)
    return _pcall(
        body, name=name, grid=(nb,),
        in_specs=[pl.BlockSpec((lc, sw), rev), pl.BlockSpec((lc, sw), rev), pl.BlockSpec((lc, sw), lambda i: (nb - 1 - i, 1)),
                  pl.BlockSpec((lc, gn), rev), pl.BlockSpec((lc, gn), rev),
                  pl.BlockSpec((tail, gn), tailmap), pl.BlockSpec((tail, gn), tailmap),
                  pl.BlockSpec((1, gn), row), pl.BlockSpec((1, gn), row),
                  pl.BlockSpec(bbt_re.shape, blk3), pl.BlockSpec(bbt_im.shape, blk3),
                  pl.BlockSpec(cct_re.shape, blk3), pl.BlockSpec(cct_im.shape, blk3), pl.BlockSpec((1, sw), row), ANY_SPEC],
        out_specs=[pl.BlockSpec((lc, sw), lambda i: (nb - 1 - i, 1)), pl.BlockSpec((1, gn), row), pl.BlockSpec((1, gn), row),
                   pl.BlockSpec((SSM_BLOCKS, ub, sb), blk3), pl.BlockSpec((SSM_BLOCKS, ub, sb), blk3),
                   pl.BlockSpec((SSM_BLOCKS, sb, ub), blk3), pl.BlockSpec((SSM_BLOCKS, sb, ub), blk3),
                   pl.BlockSpec((1, sw), row)],
        out_shape=[jax.ShapeDtypeStruct(duin.shape, BF16), jax.ShapeDtypeStruct((1, gn), F32), jax.ShapeDtypeStruct((1, gn), F32),
                   jax.ShapeDtypeStruct((SSM_BLOCKS, ub, sb), F32), jax.ShapeDtypeStruct((SSM_BLOCKS, ub, sb), F32),
                   jax.ShapeDtypeStruct((SSM_BLOCKS, sb, ub), F32), jax.ShapeDtypeStruct((SSM_BLOCKS, sb, ub), F32),
                   jax.ShapeDtypeStruct((1, sw), F32)],
        scratch_shapes=[pltpu.VMEM((lc, gn), F32), pltpu.VMEM((lc, gn), F32), pltpu.VMEM((lc, gn), F32), pltpu.VMEM((lc, gn), F32),
                        pltpu.VMEM((lc, sw), F32), pltpu.VMEM((lc, sw), F32),
                        pltpu.VMEM((nq, lc, 128), F32), pltpu.VMEM((nq, lc, 128), F32), pltpu.VMEM((nq, lc, 128), F32),
                        pltpu.VMEM((8, gn), F32), pltpu.VMEM((8, gn), F32),
                        pltpu.VMEM((tail + 8, gn), F32), pltpu.VMEM((tail + 8, gn), F32),
                        pltpu.VMEM((1, gn), F32), pltpu.VMEM((1, gn), F32)],
        input_output_aliases={14: 0},
        compiler_params=_params("arbitrary"))(dge, y, uin, s_re, s_im, s_re, s_im, lrdt, ang,
                                               bbt_re, bbt_im, cct_re, cct_im, d_skip, duin)


def _blockdiag_b(bb, sw):
    gpb = (sw // SSM_GROUP) // SSM_BLOCKS
    b4 = bb.reshape(SSM_BLOCKS, gpb, SSM_STATE, SSM_GROUP)
    eye = jnp.eye(gpb, dtype=bb.dtype)
    out = jnp.einsum('qgnh,gk->qghkn', b4, eye)
    return out.reshape(SSM_BLOCKS, gpb * SSM_GROUP, gpb * SSM_STATE)


def _blockdiag_c(cc, sw):
    gpb = (sw // SSM_GROUP) // SSM_BLOCKS
    c4 = cc.reshape(SSM_BLOCKS, gpb, SSM_GROUP, SSM_STATE)
    eye = jnp.eye(gpb, dtype=cc.dtype)
    out = jnp.einsum('qghn,gk->qgnkh', c4, eye)
    return out.reshape(SSM_BLOCKS, gpb * SSM_STATE, gpb * SSM_GROUP)


def _diag_of_b(dbb, sw):
    gpb = (sw // SSM_GROUP) // SSM_BLOCKS
    d5 = dbb.reshape(SSM_BLOCKS, gpb, SSM_GROUP, gpb, SSM_STATE)
    return jnp.einsum('qghgn->qgnh', d5).reshape(SSM_BLOCKS * gpb * SSM_STATE, SSM_GROUP)


def _diag_of_c(dcc, sw):
    gpb = (sw // SSM_GROUP) // SSM_BLOCKS
    d5 = dcc.reshape(SSM_BLOCKS, gpb, SSM_STATE, gpb, SSM_GROUP)
    return jnp.einsum('qgngh->qghn', d5).reshape(SSM_BLOCKS * gpb, SSM_GROUP, SSM_STATE)


def _ada_fwd(name, c_all, w_sh, b_sh):
    nb, d = c_all.shape
    ncol = w_sh.shape[1]
    tn = _pick(ncol, 768)

    def body(c_ref, w_ref, b_ref, o_ref):
        cv = c_ref[...]
        sil = cv * _sigmoid(cv)
        o_ref[...] = jnp.dot(sil, w_ref[...], preferred_element_type=F32, precision=lax.Precision.HIGHEST) + b_ref[...]

    return _pcall(body, name=name, grid=(ncol // tn,),
                  in_specs=[pl.BlockSpec((nb, d), lambda j: (0, 0)), pl.BlockSpec((d, tn), lambda j: (0, j)),
                            pl.BlockSpec((1, tn), lambda j: (0, j))],
                  out_specs=pl.BlockSpec((nb, tn), lambda j: (0, j)),
                  out_shape=jax.ShapeDtypeStruct((nb, ncol), F32), compiler_params=_params("parallel"))(c_all, w_sh, b_sh)


def _ada_bwd(name, c_all, dmod_sh):
    nb, d = c_all.shape
    ncol = dmod_sh.shape[1]
    tn = _pick(ncol, 768)

    def body(c_ref, g_ref, o_ref):
        cv = c_ref[...]
        sil = cv * _sigmoid(cv)
        o_ref[...] = lax.dot_general(sil, g_ref[...], (((0,), (0,)), ((), ())), preferred_element_type=F32,
                                     precision=lax.Precision.HIGHEST)

    return _pcall(body, name=name, grid=(ncol // tn,),
                  in_specs=[pl.BlockSpec((nb, d), lambda j: (0, 0)), pl.BlockSpec((nb, tn), lambda j: (0, j))],
                  out_specs=pl.BlockSpec((d, tn), lambda j: (0, j)),
                  out_shape=jax.ShapeDtypeStruct((d, ncol), F32), compiler_params=_params("parallel"))(c_all, dmod_sh)


def _place():
    return lax.axis_index("x"), lax.axis_index("y"), lax.axis_index("c")


def _allgather_small(name, blk, deps=()):
    m_per, n = blk.shape

    def body(x_ref, *rest):
        out_ref, send_sems, recv_sems, local_sem = rest[len(deps):]
        x, y, c = _place()
        me, sibling = (x, y, c), (x, y, 1 - c)
        chips = [(1 - x, y), (x, 1 - y), (1 - x, 1 - y)]

        def rows(px, py, pc):
            return out_ref.at[pl.ds((4 * px + 2 * py + pc) * m_per, m_per), :]

        def copy(k, block, to, src=None):
            return pltpu.make_async_remote_copy(
                src_ref=rows(*block) if src is None else src, dst_ref=rows(*block),
                send_sem=send_sems.at[k], recv_sem=recv_sems.at[k], device_id=to, device_id_type=MESH)

        mine = pltpu.make_async_copy(x_ref, rows(*me), local_sem)
        mine.start()
        first = [copy(0, me, sibling, src=x_ref)]
        first += [copy(1 + j, me, (*chip, c), src=x_ref) for j, chip in enumerate(chips)]
        for cp in first:
            cp.start()
        passed = [copy(4 + j, (*chip, c), sibling) for j, chip in enumerate(chips)]
        for j, chip in enumerate(chips):
            copy(1 + j, (*chip, c), me).wait_recv()
            passed[j].start()
        copy(0, sibling, me).wait_recv()
        for j, chip in enumerate(chips):
            copy(4 + j, (*chip, 1 - c), me).wait_recv()
        for cp in first + passed:
            cp.wait_send()
        mine.wait()

    return _pcall(body, name=name, out_shape=jax.ShapeDtypeStruct((N_DEV * m_per, n), blk.dtype),
                  in_specs=[pl.BlockSpec(memory_space=pltpu.VMEM)] + [ANY_SPEC] * len(deps),
                  out_specs=pl.BlockSpec(memory_space=pltpu.VMEM),
                  scratch_shapes=[pltpu.SemaphoreType.DMA((7,)), pltpu.SemaphoreType.DMA((7,)), pltpu.SemaphoreType.DMA],
                  compiler_params=pltpu.CompilerParams(vmem_limit_bytes=VMEM_LIMIT))(blk, *deps)


def _other_chips(x, y):
    return [(1 - x, y), (x, 1 - y), (1 - x, 1 - y)]


def _place_shards(shards, s_me):
    return [lax.dynamic_update_slice(lax.empty((N_CHIPS,) + s.shape, s.dtype), s[None], (s_me, 0, 0)) for s in shards]


def _ag_copy(src, land, send, recv, wi, j, chip, x, y, c, tc, both):
    hr = src.shape[0] // 2
    half = pl.ds(pl.multiple_of(c * hr, 16), hr)
    k = 3 * wi + j
    return pltpu.make_async_remote_copy(
        src_ref=src.at[half, :], dst_ref=land.at[2 * x + y, half, :],
        send_sem=send.at[2 * k + tc if both else k], recv_sem=recv.at[2 * k + c if both else k],
        device_id=(chip[0], chip[1], tc), device_id_type=MESH)


def _ag_targets(c, both):
    return (0, 1) if both else (c,)


def _ag_start(name, shards, lands, groups, direct, deps=()):
    nw, ng, nd = len(shards), len(groups), len(deps)

    def body(*refs):
        src, land = refs[:nw], refs[nw:2 * nw]
        sems = refs[2 * nw + nd:2 * nw + nd + 2 * ng]
        token = refs[-1]
        x, y, c = _place()
        for gi, grp in enumerate(groups):
            for wi, w in enumerate(grp):
                for j, chip in enumerate(_other_chips(x, y)):
                    for tc in _ag_targets(c, direct[gi]):
                        _ag_copy(src[w], land[w], sems[2 * gi], sems[2 * gi + 1], wi, j, chip, x, y, c, tc, direct[gi]).start()
        token[...] = jnp.zeros_like(token)

    sem_shapes = []
    for gi, grp in enumerate(groups):
        sem_shapes += [pltpu.SemaphoreType.DMA(((6 if direct[gi] else 3) * len(grp),))] * 2
    out_shape = sem_shapes + [pltpu.HBM(s.shape, s.dtype) for s in shards] + [pltpu.HBM(l.shape, l.dtype) for l in lands]
    out_shape += [jax.ShapeDtypeStruct((8, 128), F32)]
    res = _pcall(body, name=name, out_shape=out_shape, in_specs=[HBM_SPEC] * (2 * nw) + [ANY_SPEC] * nd,
                 out_specs=[SEM_SPEC] * (2 * ng) + [HBM_SPEC] * (2 * nw) + [pl.BlockSpec(memory_space=pltpu.VMEM)],
                 input_output_aliases={i: 2 * ng + i for i in range(2 * nw)},
                 compiler_params=pltpu.CompilerParams(has_side_effects=EFFECT))(
                     *[_hbm(s) for s in shards], *[_hbm(l) for l in lands], *deps)
    sems = [(res[2 * gi], res[2 * gi + 1]) for gi in range(ng)]
    return sems, list(res[2 * ng:2 * ng + nw]), list(res[2 * ng + nw:2 * ng + 2 * nw]), res[-1]


def _ag_wait(name, shards, lands, send, recv, both, after):
    n = len(shards)

    def body(*refs):
        src, land = refs[:n], refs[n:2 * n]
        send_sem, recv_sem = refs[2 * n], refs[2 * n + 1]
        x, y, c = _place()
        for wi in range(n):
            for j, chip in enumerate(_other_chips(x, y)):
                for tc in _ag_targets(c, both):
                    _ag_copy(src[wi], land[wi], send_sem, recv_sem, wi, j, chip, x, y, c, tc, both).wait_send()
                    _ag_copy(src[wi], land[wi], send_sem, recv_sem, wi, j, chip, chip[0], chip[1], tc, c, both).wait_recv()

    res = _pcall(body, name=name, out_shape=[pltpu.HBM(a.shape, a.dtype) for a in list(shards) + list(lands)],
                 in_specs=[HBM_SPEC] * (2 * n) + [SEM_SPEC, SEM_SPEC] + [ANY_SPEC] * len(after), out_specs=[HBM_SPEC] * (2 * n),
                 input_output_aliases={i: i for i in range(2 * n)},
                 compiler_params=pltpu.CompilerParams(has_side_effects=EFFECT))(*shards, *lands, send, recv, *after)
    return list(res[n:])


def _ag_forward(name, lands):
    n = len(lands)

    def body(*refs):
        out = refs[n:2 * n]
        send, recv = refs[2 * n], refs[2 * n + 1]
        x, y, c = _place()
        sib = (x, y, 1 - c)
        cps = []
        for wi in range(n):
            hr = out[wi].shape[1] // 2
            for j, (cx, cy) in enumerate(_other_chips(x, y)):
                got = out[wi].at[2 * cx + cy, pl.ds(pl.multiple_of(c * hr, 16), hr), :]
                cp = pltpu.make_async_remote_copy(src_ref=got, dst_ref=got, send_sem=send.at[3 * wi + j], recv_sem=recv.at[3 * wi + j],
                                                  device_id=sib, device_id_type=MESH)
                cp.start()
                cps.append(cp)
        for wi in range(n):
            hr = out[wi].shape[1] // 2
            for j, (cx, cy) in enumerate(_other_chips(x, y)):
                got = out[wi].at[2 * cx + cy, pl.ds(pl.multiple_of((1 - c) * hr, 16), hr), :]
                pltpu.make_async_remote_copy(src_ref=got, dst_ref=got, send_sem=send.at[3 * wi + j], recv_sem=recv.at[3 * wi + j],
                                             device_id=sib, device_id_type=MESH).wait_recv()
        for cp in cps:
            cp.wait_send()

    res = _pcall(body, name=name, out_shape=[jax.ShapeDtypeStruct(l.shape, l.dtype) for l in lands],
                 in_specs=[ANY_SPEC] * n, out_specs=[ANY_SPEC] * n, input_output_aliases={i: i for i in range(n)},
                 scratch_shapes=[pltpu.SemaphoreType.DMA((3 * n,)), pltpu.SemaphoreType.DMA((3 * n,))])(*lands)
    return list(res)


def _peers(x, y, c):
    offs = [(dx, dy, dc) for dx in (0, 1) for dy in (0, 1) for dc in (0, 1)][1:]
    return [(1 - x if dx else x, 1 - y if dy else y, 1 - c if dc else c) for dx, dy, dc in offs]


def _rs_copy(g_ref, land_ref, send, recv, wi, k, to, sender):
    hr = g_ref.shape[1] // 2
    return pltpu.make_async_remote_copy(
        src_ref=g_ref.at[2 * to[0] + to[1], pl.ds(pl.multiple_of(to[2] * hr, 16), hr), :], dst_ref=land_ref.at[sender],
        send_sem=send.at[7 * wi + k], recv_sem=recv.at[7 * wi + k], device_id=to, device_id_type=MESH)


def _rs_start(name, gs):
    n = len(gs)
    lands = [lax.empty((N_DEV, g.shape[1] // 2, g.shape[2]), BF16) for g in gs]

    def body(*refs):
        g, land = refs[:n], refs[n:2 * n]
        send, recv = refs[2 * n], refs[2 * n + 1]
        token = refs[-1]
        x, y, c = _place()
        me = 4 * x + 2 * y + c
        for wi in range(n):
            for k, to in enumerate(_peers(x, y, c)):
                _rs_copy(g[wi], land[wi], send, recv, wi, k, to, me).start()
        token[...] = jnp.zeros_like(token)

    out_shape = [pltpu.SemaphoreType.DMA((7 * n,))] * 2 + [pltpu.HBM(a.shape, a.dtype) for a in list(gs) + lands]
    out_shape += [jax.ShapeDtypeStruct((8, 128), F32)]
    res = _pcall(body, name=name, out_shape=out_shape, in_specs=[HBM_SPEC] * (2 * n),
                 out_specs=[SEM_SPEC] * 2 + [HBM_SPEC] * (2 * n) + [pl.BlockSpec(memory_space=pltpu.VMEM)],
                 input_output_aliases={i: 2 + i for i in range(2 * n)},
                 compiler_params=pltpu.CompilerParams(has_side_effects=EFFECT))(
                     *[_hbm(a) for a in gs], *[_hbm(a) for a in lands])
    return res[0], res[1], list(res[2:2 + n]), list(res[2 + n:2 + 2 * n]), res[-1]


def _rs_wait(name, gs, lands, send, recv, after):
    n = len(gs)

    def body(*refs):
        g, land = refs[:n], refs[n:2 * n]
        send_sem, recv_sem = refs[2 * n], refs[2 * n + 1]
        x, y, c = _place()
        me = 4 * x + 2 * y + c
        for wi in range(n):
            for k, to in enumerate(_peers(x, y, c)):
                _rs_copy(g[wi], land[wi], send_sem, recv_sem, wi, k, to, me).wait_send()
                _rs_copy(g[wi], land[wi], send_sem, recv_sem, wi, k, (x, y, c), 4 * to[0] + 2 * to[1] + to[2]).wait_recv()

    res = _pcall(body, name=name, out_shape=[pltpu.HBM(a.shape, a.dtype) for a in list(gs) + list(lands)],
                 in_specs=[HBM_SPEC] * (2 * n) + [SEM_SPEC, SEM_SPEC, ANY_SPEC], out_specs=[HBM_SPEC] * (2 * n),
                 input_output_aliases={i: i for i in range(2 * n)},
                 compiler_params=pltpu.CompilerParams(has_side_effects=EFFECT))(*gs, *lands, send, recv, after)
    return list(res[:n]), list(res[n:])


def _bc_copy(blk_ref, land_ref, send, recv, k, to, slot):
    return pltpu.make_async_remote_copy(src_ref=blk_ref, dst_ref=land_ref.at[slot], send_sem=send.at[k], recv_sem=recv.at[k],
                                        device_id=to, device_id_type=MESH)


def _bcast_start(name, blk):
    land = lax.empty((N_DEV,) + blk.shape, blk.dtype)

    def body(blk_ref, land_ref, send, recv, blk_thru, land_thru, token):
        x, y, c = _place()
        for k, to in enumerate(_peers(x, y, c)):
            _bc_copy(blk_ref, land_ref, send, recv, k, to, 4 * x + 2 * y + c).start()
        token[...] = jnp.zeros_like(token)

    return _pcall(body, name=name,
                  out_shape=[pltpu.SemaphoreType.DMA((7,)), pltpu.SemaphoreType.DMA((7,)), pltpu.HBM(blk.shape, blk.dtype),
                             pltpu.HBM(land.shape, land.dtype), jax.ShapeDtypeStruct((8, 128), F32)],
                  in_specs=[HBM_SPEC, HBM_SPEC], out_specs=[SEM_SPEC, SEM_SPEC, HBM_SPEC, HBM_SPEC, pl.BlockSpec(memory_space=pltpu.VMEM)],
                  input_output_aliases={0: 2, 1: 3}, compiler_params=pltpu.CompilerParams(has_side_effects=EFFECT))(_hbm(blk), _hbm(land))


def _bcast_wait(name, blk, land, send, recv, after):
    def body(blk_ref, land_ref, send_sem, recv_sem, after_ref, blk_thru, land_thru):
        x, y, c = _place()
        for k, to in enumerate(_peers(x, y, c)):
            _bc_copy(blk_ref, land_ref, send_sem, recv_sem, k, to, 4 * x + 2 * y + c).wait_send()
            _bc_copy(blk_ref, land_ref, send_sem, recv_sem, k, to, 4 * to[0] + 2 * to[1] + to[2]).wait_recv()

    return _pcall(body, name=name, out_shape=[pltpu.HBM(blk.shape, blk.dtype), pltpu.HBM(land.shape, land.dtype)],
                  in_specs=[HBM_SPEC, HBM_SPEC, SEM_SPEC, SEM_SPEC, ANY_SPEC], out_specs=[HBM_SPEC, HBM_SPEC],
                  input_output_aliases={0: 0, 1: 1}, compiler_params=pltpu.CompilerParams(has_side_effects=EFFECT))(
                      blk, land, send, recv, after)[1]


def _rs_sum(name, gs, lands):
    n = len(gs)

    def body(*refs):
        g_refs, land_refs, out_refs = refs[:n], refs[n:2 * n], refs[2 * n:3 * n]
        recvs = refs[3 * n:4 * n]
        local_sems, sib_send, sib_recv = refs[4 * n:]
        x, y, c = _place()
        me = 4 * x + 2 * y + c
        cps = []
        for wi in range(n):
            hr = g_refs[wi].shape[1] // 2
            own = g_refs[wi].at[2 * x + y, pl.ds(pl.multiple_of(c * hr, 16), hr), :]
            cps.append(pltpu.make_async_copy(own, recvs[wi].at[me], local_sems.at[8 * wi + 7]))
            for k, (tx, ty, tc) in enumerate(_peers(x, y, c)):
                slot = 4 * tx + 2 * ty + tc
                cps.append(pltpu.make_async_copy(land_refs[wi].at[slot], recvs[wi].at[slot], local_sems.at[8 * wi + k]))
        for cp in cps:
            cp.start()
        sibs = []
        for wi in range(n):
            hr = g_refs[wi].shape[1] // 2
            ch = _pick(hr, 64, 16)
            for cp in cps[8 * wi:8 * wi + 8]:
                cp.wait()
            base = pl.multiple_of(c * hr, 16)
            for r0 in range(0, hr, ch):
                acc = recvs[wi][0, r0:r0 + ch, :].astype(F32)
                for k in range(1, N_DEV):
                    acc = acc + recvs[wi][k, r0:r0 + ch, :].astype(F32)
                out_refs[wi][pl.ds(base + r0, ch), :] = acc
            half = out_refs[wi].at[pl.ds(base, hr), :]
            sib = pltpu.make_async_remote_copy(src_ref=half, dst_ref=half, send_sem=sib_send.at[wi], recv_sem=sib_recv.at[wi],
                                               device_id=(x, y, 1 - c), device_id_type=MESH)
            sib.start()
            sibs.append(sib)
        for wi in range(n):
            hr = g_refs[wi].shape[1] // 2
            other = out_refs[wi].at[pl.ds(pl.multiple_of((1 - c) * hr, 16), hr), :]
            pltpu.make_async_remote_copy(src_ref=other, dst_ref=other, send_sem=sib_send.at[wi], recv_sem=sib_recv.at[wi],
                                         device_id=(x, y, 1 - c), device_id_type=MESH).wait_recv()
            sibs[wi].wait_send()

    res = _pcall(
        body, name=name, out_shape=[jax.ShapeDtypeStruct(g.shape[1:], F32) for g in gs],
        in_specs=[ANY_SPEC] * (2 * n), out_specs=[pl.BlockSpec(memory_space=pltpu.VMEM)] * n,
        scratch_shapes=[pltpu.VMEM((N_DEV, g.shape[1] // 2, g.shape[2]), BF16) for g in gs]
        + [pltpu.SemaphoreType.DMA((8 * n,)), pltpu.SemaphoreType.DMA((n,)), pltpu.SemaphoreType.DMA((n,))],
        compiler_params=pltpu.CompilerParams(vmem_limit_bytes=VMEM_LIMIT))(*gs, *lands)
    return list(res)


def _gate_norm_epilogue(factor, x_in, gt, nxt, t, d):
    blk = (ROWS, d)
    full = lambda i, j: (i, j)
    rowv = lambda i, j: (0, j)

    def epi(acc, res, scale, *norm):
        x_new = res + (factor * scale) * acc
        if not norm:
            return x_new, acc
        gv, scv, shv = norm
        r = lax.rsqrt(jnp.mean(x_new * x_new, axis=-1, keepdims=True) + EPS)
        return x_new, acc, x_new * r * gv * (1.0 + scv) + shv

    ins = [(x_in, blk, full), (gt, (1, d), rowv)] + [(v, (1, d), rowv) for v in (nxt or ())]
    outs = [((t, d), F32, blk, full), ((t, d), BF16, blk, full)] + ([((t, d), BF16, blk, full)] if nxt else [])
    return epi, ins, outs


_FULL = lambda i, j: (i, j)
_ROWV = lambda i, j: (0, j)


def _glu_epilogue(b_glu, t, w):
    def epi(acc, bv):
        g = acc + bv
        return acc, g[:, :w] * _sigmoid(g[:, w:])

    return epi, [(b_glu, (1, 2 * w), _ROWV)], [((t, 2 * w), BF16, (ROWS, 2 * w), _FULL), ((t, w), BF16, (ROWS, w), _FULL)]


def _glu_bwd_epilogue(gv, b_glu, t, w):
    def epi(dsg, gvv, bv):
        g = gvv.astype(F32) + bv
        val, s = g[:, :w], _sigmoid(g[:, w:])
        dval, dgate = dsg * s, dsg * val * s * (1.0 - s)
        return [jnp.concatenate([dval, dgate], axis=1)], [jnp.concatenate([_colsum(dval), _colsum(dgate)], axis=1)]

    return epi, [(gv, (ROWS, 2 * w), _FULL), (b_glu, (1, 2 * w), _ROWV)], [((t, 2 * w), BF16, (ROWS, 2 * w), _FULL)], [2 * w]


def _gates_epilogue(y_pool, uin, t, d, pw):
    cb = (2 * pw) // d

    def epi(acc, yp, glp, gls):
        return acc, _sigmoid(glp.astype(F32)) * yp.astype(F32) + _sigmoid(gls.astype(F32)) * acc

    ins = [(y_pool, (ROWS, d), _FULL), (uin, (ROWS, d), lambda i, j: (i, cb)), (uin, (ROWS, d), lambda i, j: (i, cb + 1))]
    return epi, ins, [((t, d), BF16, (ROWS, d), _FULL)] * 2


def _gates_bwd_epilogue(y_pool, y_ssm, uin, t, d, pw):
    cb = (2 * pw) // d

    def epi(dm, yp, ys, glp, gls):
        sp, ss = _sigmoid(glp.astype(F32)), _sigmoid(gls.astype(F32))
        dgl = jnp.concatenate([dm * yp.astype(F32) * sp * (1.0 - sp), dm * ys.astype(F32) * ss * (1.0 - ss)], axis=1)
        return dm * sp, dm * ss, ("at", 2 * pw, dgl)

    ins = [(y_pool, (ROWS, d), _FULL), (y_ssm, (ROWS, d), _FULL),
           (uin, (ROWS, d), lambda i, j: (i, cb)), (uin, (ROWS, d), lambda i, j: (i, cb + 1))]
    wide = 2 * pw + 2 * d
    return epi, ins, [((t, d), BF16, (ROWS, d), _FULL)] * 2 + [((t, wide), BF16, (ROWS, wide), _FULL)]


def _loss_epilogue(x_in, gt, tgt, g_final, t, d):
    blk = (ROWS, d)
    full = lambda i, j: (i, j)
    rowv = lambda i, j: (0, j)

    def epi(acc, res, scale, tv, gv):
        xv = res + (0.5 * scale) * acc
        r = lax.rsqrt(jnp.mean(xv * xv, axis=-1, keepdims=True) + EPS)
        xr = xv * r
        e = xr * gv - tv
        loss_row = 0.5 * jnp.mean(e * e, axis=-1, keepdims=True)
        dout = e * (1.0 / d)
        gd = gv * dout
        dx = r * (gd - xr * jnp.mean(gd * xr, axis=-1, keepdims=True))
        fdx = 0.5 * dx
        return [dx, scale * fdx], [_colsum(dout * xr), _colsum(loss_row * jnp.ones((1, 128), F32)), _colsum(fdx * acc)]

    ins = [(x_in, blk, full), (gt, (1, d), rowv), (tgt, blk, full), (g_final, (1, d), rowv)]
    return epi, ins, [((t, d), F32, blk, full), ((t, d), BF16, blk, full)], [d, 128, d]


def _norm_bwd_epilogue(x, dres, g, sc, up, t, d):
    blk = (ROWS, d)
    full = lambda i, j: (i, j)
    rowv = lambda i, j: (0, j)

    def epi(dhv, xv, drv, gv, scv, *rest):
        r = lax.rsqrt(jnp.mean(xv * xv, axis=-1, keepdims=True) + EPS)
        xr = xv * r
        dn = dhv * (1.0 + scv)
        gd = gv * dn
        dx = drv + r * (gd - xr * jnp.mean(gd * xr, axis=-1, keepdims=True))
        outs, reds = [dx], [_colsum(dhv), _colsum(dhv * xr * gv), _colsum(dn * xr)]
        if up:
            yv, gtv = rest
            fdx = up[2] * dx
            outs.append(gtv * fdx)
            reds.append(_colsum(fdx * yv.astype(F32)))
        return outs, reds

    ins = [(x, blk, full), (dres, blk, full), (g, (1, d), rowv), (sc, (1, d), rowv)]
    ins += [(up[0], blk, full), (up[1], (1, d), rowv)] if up else []
    outs = [((t, d), F32, blk, full)] + ([((t, d), BF16, blk, full)] if up else [])
    return epi, ins, outs, [d] * (4 if up else 3)


def _ffn_in_act(name, h, w_in, tm=ROWS):
    t, d = h.shape
    ns = w_in.shape[2]
    tm = _pick(t, tm, 8)
    w4 = w_in.reshape(2, 2, d, ns)

    def body(h_ref, w_ref, ab_ref, act_ref):
        hv = h_ref[...]
        a = jnp.dot(hv, w_ref[0], preferred_element_type=F32)
        b = jnp.dot(hv, w_ref[1], preferred_element_type=F32)
        ab_ref[0] = a.astype(BF16)
        ab_ref[1] = b.astype(BF16)
        act_ref[...] = (a * _sigmoid(a) * b).astype(BF16)

    return _pcall(body, name=name, grid=(2, t // tm),
                  in_specs=[pl.BlockSpec((tm, d), lambda c, i: (i, 0)), pl.BlockSpec((2, None, d, ns), lambda c, i: (0, c, 0, 0))],
                  out_specs=[pl.BlockSpec((2, tm, ns), lambda c, i: (0, i, c)), pl.BlockSpec((tm, ns), lambda c, i: (i, c))],
                  out_shape=[jax.ShapeDtypeStruct((2, t, 2 * ns), BF16), jax.ShapeDtypeStruct((t, 2 * ns), BF16)],
                  compiler_params=_params("parallel", "parallel"))(h, w4)


def _dswiglu_epilogue(ab, t, f, tn):
    blk = (2, ROWS, tn)
    idx = lambda i, j: (0, i, j)

    def epi(dact, abv):
        a, b = abv[0].astype(F32), abv[1].astype(F32)
        s = _sigmoid(a)
        return (jnp.stack([dact * b * (s * (1.0 + a * (1.0 - s))), dact * (a * s)]),)

    return epi, [(ab, blk, idx)], [((2, t, f), BF16, blk, idx)]


def _ffn_fwd(tag, x, h, gt, get_w_in, get_w_out, nxt=None, loss=None):
    t, d = x.shape
    ab, act = _ffn_in_act(tag + "_in", h, get_w_in(h))
    if loss:
        epi, epi_ins, epi_outs, epi_reds = _loss_epilogue(x, gt, *loss, t, d)
        res = _mm(tag + "_out", act, get_w_out(act), tm=ROWS, tn=d, epi=epi, epi_ins=epi_ins, epi_outs=epi_outs,
                  epi_reds=epi_reds, b_resident=True)
        return res, (x, h, ab, act, None)
    epi, epi_ins, epi_outs = _gate_norm_epilogue(0.5, x, gt, nxt, t, d)
    res = _mm(tag + "_out", act, get_w_out(act), tm=ROWS, tn=d, epi=epi, epi_ins=epi_ins, epi_outs=epi_outs, b_resident=True)
    return res[0], res[2], (x, h, ab, act, res[1])


def _ffn_bwd(tag, dx_new, dy, saved, g, sc, w_in, w_out, start_rs, up, deps=()):
    x, h, ab, act, _ = saved
    d = x.shape[1]
    f = act.shape[1]
    dw_out = _mm(tag + "_dwout", act, dy, ta=True, tm=1408, tn=d, tk=2048, deps=deps)
    tok = start_rs("out", dw_out.reshape(N_CHIPS, f // N_CHIPS, d))
    epi, epi_ins, epi_outs = _dswiglu_epilogue(ab, x.shape[0], f, w_in.shape[2])
    dab = _mm(tag + "_dact", dy, w_out, tb=True, tm=ROWS, tn=w_in.shape[2], epi=epi, epi_ins=epi_ins, epi_outs=epi_outs,
              deps=(tok,), j_outer=True)
    dw_in = _mm(tag + "_dwin", h, dab, ta=True, out_stacked=True, b_halves=True, tm=d, tn=1408, tk=2048)
    tok = start_rs("in", dw_in)
    epi, epi_ins, epi_outs, epi_reds = _norm_bwd_epilogue(x, dx_new, g, sc, up, x.shape[0], d)
    res = _mm(tag + "_dh", dab, w_in, tb=True, b_stacked=True, a_halves=True, tn=d, tk=5632, deps=(tok,), epi=epi,
              epi_ins=epi_ins, epi_outs=epi_outs, epi_reds=epi_reds, b_resident=True)
    if up:
        return res
    return res[0], None, res[1], res[2], res[3], None


def _row(v):
    return v.reshape(1, -1)


def _pack(parts):
    cols = []
    for p in parts:
        flat = p.reshape(-1).astype(F32)
        padn = (-flat.shape[0]) % 128
        cols.append(jnp.pad(flat, (0, padn)) if padn else flat)
    flat = jnp.concatenate(cols)
    padn = (-flat.shape[0]) % 1024
    if padn:
        flat = jnp.pad(flat, (0, padn))
    return flat.reshape(-1, 128)


def _unpack(packed, shapes):
    flat = packed.reshape(-1)
    out, off = [], 0
    for s in shapes:
        n = math.prod(s)
        out.append(flat[off:off + n].reshape(s))
        off += n + ((-n) % 128)
    return out


SMALL = ['b_ada', 'g_ffn1', 'g_mix', 'pool_w', 'pool_b', 'pool_scale', 'ssm_lam_re_log', 'ssm_lam_im', 'ssm_log_dt',
         'ssm_b_re', 'ssm_b_im', 'ssm_c_re', 'ssm_c_im', 'ssm_d', 'b_glu', 'g_ffn2', 'g_final']
BIG = ['w_ffn1_in', 'w_ffn1_out', 'w_in', 'w_pool_up', 'w_glu', 'w_ssm_up', 'w_out', 'w_ffn2_in', 'w_ffn2_out']
AG_GROUPS = [[0], [1], [2, 3, 4, 5, 6], [7, 8]]
AG_DIRECT = [False, True, False, True]
SMALL_LATE = ['b_ada', 'g_ffn1']
RS_SUM_GROUPS = [[0, 1], [2, 3], [4, 5]]
WEIGHTS = ['w_ada', 'b_ada', 'g_ffn1', 'w_ffn1_in', 'w_ffn1_out', 'g_mix', 'w_in', 'pool_w', 'pool_b', 'pool_scale', 'w_pool_up',
           'ssm_lam_re_log', 'ssm_lam_im', 'ssm_log_dt', 'ssm_b_re', 'ssm_b_im', 'ssm_c_re', 'ssm_c_im', 'ssm_d', 'w_glu', 'b_glu',
           'w_ssm_up', 'w_out', 'g_ffn2', 'w_ffn2_in', 'w_ffn2_out', 'g_final']


def kernel(x, c, w_ada, b_ada, g_ffn1, w_ffn1_in, w_ffn1_out, g_mix, w_in, pool_w, pool_b, pool_scale, w_pool_up, ssm_lam_re_log, ssm_lam_im, ssm_log_dt, ssm_b_re, ssm_b_im, ssm_c_re, ssm_c_im, ssm_d, w_glu, b_glu, w_ssm_up, w_out, g_ffn2, w_ffn2_in, w_ffn2_out, g_final, loss_target, m_w_ada, m_b_ada, m_g_ffn1, m_w_ffn1_in, m_w_ffn1_out, m_g_mix, m_w_in, m_pool_w, m_pool_b, m_pool_scale, m_w_pool_up, m_ssm_lam_re_log, m_ssm_lam_im, m_ssm_log_dt, m_ssm_b_re, m_ssm_b_im, m_ssm_c_re, m_ssm_c_im, m_ssm_d, m_w_glu, m_b_glu, m_w_ssm_up, m_w_out, m_g_ffn2, m_w_ffn2_in, m_w_ffn2_out, m_g_final, v_w_ada, v_b_ada, v_g_ffn1, v_w_ffn1_in, v_w_ffn1_out, v_g_mix, v_w_in, v_pool_w, v_pool_b, v_pool_scale, v_w_pool_up, v_ssm_lam_re_log, v_ssm_lam_im, v_ssm_log_dt, v_ssm_b_re, v_ssm_b_im, v_ssm_c_re, v_ssm_c_im, v_ssm_d, v_w_glu, v_b_glu, v_w_ssm_up, v_w_out, v_g_ffn2, v_w_ffn2_in, v_w_ffn2_out, v_g_final):
    args = dict(locals())
    wt = {n: args[n] for n in WEIGHTS}
    mom = {n: args["m_" + n] for n in WEIGHTS}
    var = {n: args["v_" + n] for n in WEIGHTS}

    t, d = x.shape[1], x.shape[2]
    pw = pool_b.shape[1]
    sw = ssm_d.shape[1]
    ngrp = sw // SSM_GROUP
    gn = ngrp * SSM_STATE
    xi, yi, ci = _place()
    b_me = 4 * xi + 2 * yi + ci
    s_me = 2 * xi + yi
    x2d = x[0]
    tgt = loss_target[0]

    c_all = _allgather_small("ag_c", c.reshape(8, d // 8)).reshape(N_DEV, d)
    ncol = w_ada.shape[2]
    b_sh = lax.dynamic_slice(b_ada, (0, s_me * ncol), (1, ncol))
    mod_sh = _ada_fwd("ada_fwd", c_all, w_ada[0], b_sh)
    md_send, md_recv, mod_sh, md_land, tok = _bcast_start("mod_start", mod_sh)

    shards = [wt[n][0].astype(BF16) for n in BIG]
    lands = _place_shards(shards, s_me)
    n0 = len(AG_GROUPS[0])
    sems_a, shards_a, lands_a, tok = _ag_start("ag_start_first", shards[:n0], lands[:n0], AG_GROUPS[:1], AG_DIRECT[:1], deps=(tok,))
    rest = [[w - n0 for w in grp] for grp in AG_GROUPS[1:]]
    sems_b, shards_b, lands_b, tok = _ag_start("ag_start_rest", shards[n0:], lands[n0:], rest, AG_DIRECT[1:], deps=(tok,))
    ag_sems, shards_t, lands_t = sems_a + sems_b, shards_a + shards_b, lands_a + lands_b
    md_land = _bcast_wait("mod_wait", mod_sh, md_land, md_send, md_recv, tok)
    mod_all = lax.dynamic_update_slice(md_land, mod_sh[None], (b_me, 0, 0))
    full = {}

    def weights(gi, *after):
        grp = AG_GROUPS[gi]
        if BIG[grp[0]] not in full:
            ls = _ag_wait("ag_wait%d" % gi, [shards_t[w] for w in grp], [lands_t[w] for w in grp], *ag_sems[gi],
                          AG_DIRECT[gi], after)
            for w, l in zip(grp, ls if AG_DIRECT[gi] else _ag_forward("ag_fwd%d" % gi, ls)):
                full[BIG[w]] = l
        return full

    mod_me = jnp.concatenate([lax.dynamic_slice(mod_all, (2 * s, b_me, 0), (1, 1, ncol))[0] for s in range(N_CHIPS)], axis=1)
    mod = mod_me.reshape(9, d)
    sh1, sc1, gt1, sh2, sc2, gt2, sh3, sc3, gt3 = [mod[k:k + 1] for k in range(9)]

    f = w_ffn1_out.shape[1] * N_CHIPS

    col = lambda a: a.reshape(gn, 1)
    lrl_c, li_c = col(ssm_lam_re_log), col(ssm_lam_im)
    ldt_c = col(jnp.broadcast_to(ssm_log_dt.reshape(ngrp, 1), (ngrp, SSM_STATE)))
    b_re2, b_im2 = ssm_b_re.reshape(gn, SSM_GROUP), ssm_b_im.reshape(gn, SSM_GROUP)
    lrdt_c, ang_c, bb_re, bb_im = _ssm_prep("ssm_prep", lrl_c, li_c, ldt_c, b_re2, b_im2)
    lrdt, ang = lrdt_c.reshape(1, gn), ang_c.reshape(1, gn)
    tr = lambda a: jnp.swapaxes(a, 1, 2)
    bbd = [_blockdiag_b(v, sw).astype(BF16) for v in (bb_re, bb_im)]
    ccd = [_blockdiag_c(v[0], sw).astype(BF16) for v in (ssm_c_re, ssm_c_im)]
    bbd_t, ccd_t = [tr(v) for v in bbd], [tr(v) for v in ccd]
    early = [n for n in SMALL if n not in SMALL_LATE]
    packs = {}
    for tag, names, extra in (("early", early, []), ("late", SMALL_LATE, [jnp.zeros((128,), F32)])):
        packs[tag] = [_pack([src[n] for n in names] + extra) for src in (wt, mom, var)]
    shadow = [lrdt, ang, ldt_c, *bbd, *ccd, *bbd_t, *ccd_t, *packs["early"], *packs["late"]]

    h1 = _norm_fwd("ffn1_norm", x2d, g_ffn1, sc1, sh1, deps=(tok,))
    x1, h2, sav1 = _ffn_fwd("ffn1", x2d, h1, gt1, lambda after: weights(0, after, *shadow)['w_ffn1_in'],
                            lambda after: weights(1, after)['w_ffn1_out'].reshape(f, d), (g_mix, sc2, sh2))
    weights(2, h2)
    wo = full['w_out'].reshape(d, d)
    uin = _mm("mix_in", h2, full['w_in'], b_stacked=True, tm=1024, tn=768, j_outer=True)
    p_pool, z_pool, y_pool = _pool_fwd("pool_fwd", uin, pool_w[0], pool_b, pool_scale, full['w_pool_up'], pw)
    y_s, ge, s_re, s_im = _ssm_fwd("ssm_fwd", uin, lrdt, ang, *bbd, *ccd, ssm_d, sw)
    epi, epi_ins, epi_outs = _glu_epilogue(b_glu, t, sw)
    gv, sg = _mm("glu_in", ge, full['w_glu'], b_stacked=True, tm=ROWS, tn=d, epi=epi, epi_ins=epi_ins, epi_outs=epi_outs,
                 b_resident=True)
    epi, epi_ins, epi_outs = _gates_epilogue(y_pool, uin, t, d, pw)
    y_ssm, merged = _mm("ssm_up", sg, full['w_ssm_up'], b_stacked=True, tm=ROWS, tn=d, epi=epi, epi_ins=epi_ins,
                        epi_outs=epi_outs, b_resident=True)
    epi, epi_ins, epi_outs = _gate_norm_epilogue(1.0, x1, gt2, (g_ffn2, sc3, sh3), t, d)
    x2, y2, h3 = _mm("mix_out", merged, wo, tm=ROWS, tn=d, epi=epi, epi_ins=epi_ins, epi_outs=epi_outs, b_resident=True)

    (dx3, dy3, dg_final, loss_v, dgt3), sav3 = _ffn_fwd(
        "ffn2", x2, h3, gt3, lambda after: weights(3, after)['w_ffn2_in'],
        lambda after: weights(3, after)['w_ffn2_out'].reshape(f, d), loss=(tgt, _row(g_final)))

    gs = {}
    rs_open = []

    def start_rs(names, gbs):
        send, recv, g_thru, land_thru, token = _rs_start("rs_start_" + names[0], gbs)
        rs_open.append((names, send, recv, g_thru, land_thru))
        return token

    dx2, dy2, dsh3, dsc3, gs['g_ffn2'], dgt2 = _ffn_bwd(
        "ffn2b", dx3, dy3, sav3, g_ffn2, sc3, full['w_ffn2_in'], full['w_ffn2_out'].reshape(f, d),
        lambda key, g: start_rs(['w_ffn2_' + key], [g]), (y2, gt2, 1.0))

    epi, epi_ins, epi_outs = _gates_bwd_epilogue(y_pool, y_ssm, uin, t, d, pw)
    dy_pool, dy_ssm, duin, g_wo = _mm("mix_dmerged", dy2, wo, tb=True, tm=ROWS, tn=d, epi=epi, epi_ins=epi_ins,
                                      epi_outs=epi_outs, b_resident=True, wgrad=merged, wgrad_rows=True)
    g_wo = g_wo.reshape(N_CHIPS, d // N_CHIPS, d)

    duin, gs['pool_w'], gs['pool_b'], gs['pool_scale'], g_wpu = _pool_bwd(
        "pool_bwd", dy_pool, z_pool, p_pool, pool_w[0], pool_b, pool_scale, full['w_pool_up'], duin)

    epi, epi_ins, epi_outs, epi_reds = _glu_bwd_epilogue(gv, b_glu, t, sw)
    dgv, gs['b_glu'], g_wsu = _mm("ssm_dup", dy_ssm, full['w_ssm_up'], tb=True, b_stacked=True, tm=ROWS, tn=sw, tk=d,
                                  epi=epi, epi_ins=epi_ins, epi_outs=epi_outs, epi_reds=epi_reds, b_resident=True, wgrad=sg)
    tok = start_rs(['w_out', 'w_pool_up', 'w_ssm_up'], [g_wo, g_wpu, g_wsu])
    dge, g_wglu = _mm("glu_dge", dgv, full['w_glu'], tb=True, b_stacked=True, tm=1024, tn=sw, tk=d, deps=(tok,), wgrad=ge)
    (duin, g_abre, g_abim, g_bbd_re, g_bbd_im, g_ccd_re, g_ccd_im, gs['ssm_d']) = _ssm_bwd(
        "ssm_bwd", dge, y_s, uin, s_re, s_im, lrdt, ang, *bbd_t, *ccd_t, ssm_d, sw, duin)
    gs['ssm_c_re'], gs['ssm_c_im'] = _diag_of_c(g_ccd_re, sw), _diag_of_c(g_ccd_im, sw)
    d_lrl, d_li, d_ldt, d_bre, d_bim = _ssm_param_bwd(
        "ssm_param_bwd", lrl_c, li_c, ldt_c, b_re2, b_im2, g_abre.reshape(gn, 1), g_abim.reshape(gn, 1),
        _diag_of_b(g_bbd_re, sw), _diag_of_b(g_bbd_im, sw))
    gs['ssm_lam_re_log'], gs['ssm_lam_im'] = d_lrl, d_li
    gs['ssm_log_dt'] = jnp.sum(d_ldt.reshape(ngrp, SSM_STATE), axis=1)
    gs['ssm_b_re'], gs['ssm_b_im'] = d_bre, d_bim

    g_win =_mm("mix_dwin", h2, duin, ta=True, out_stacked=True, tm=d, tn=768, tk=4096)
    tok = start_rs(['w_glu', 'w_in'], [g_wglu, g_win])
    epi, epi_ins, epi_outs, epi_reds = _norm_bwd_epilogue(x1, dx2, g_mix, sc2, (sav1[4], gt1, 0.5), t, d)
    dx1, dy1, dsh2, dsc2, gs['g_mix'], dgt1 = _mm(
        "mix_dh", duin, full['w_in'], tb=True, b_stacked=True, tn=d, tk=3072, deps=(tok,), epi=epi, epi_ins=epi_ins,
        epi_outs=epi_outs, epi_reds=epi_reds, b_resident=True)

    gs['g_final'] = dg_final
    sg_blk = _pack([gs[n] for n in early])
    sg_send, sg_recv, sg_blk, sg_land, tok = _bcast_start("sg_start", sg_blk)

    dx0, _, dsh1, dsc1, gs['g_ffn1'], _ = _ffn_bwd(
        "ffn1b", dx1, dy1, sav1, g_ffn1, sc1, full['w_ffn1_in'], full['w_ffn1_out'].reshape(f, d),
        lambda key, g: start_rs(['w_ffn1_' + key], [g]), None, deps=(tok,))

    gs['b_ada'] = jnp.concatenate([dsh1, dsc1, dgt1, dsh2, dsc2, dgt2, dsh3, dsc3, dgt3], axis=1)
    lt_blk = _pack([gs[n] for n in SMALL_LATE] + [loss_v])
    lt_send, lt_recv, lt_blk, lt_land, tok = _bcast_start("late_start", lt_blk)

    grads, delta, new_m, new_v = {}, {}, {}, {}

    def small_update(tag, names, blk, land):
        g8 = lax.dynamic_update_slice(land, blk[None], (b_me, 0, 0)).reshape(-1, 128)
        w_pack, m_pack, v_pack = packs[tag]
        shapes = [wt[n].shape for n in names]
        per_param, packed = _adamw_small("adamw_small_" + tag, w_pack, g8, m_pack, v_pack, shapes)
        for k, dst in enumerate((grads, delta, new_m, new_v)):
            rest = _unpack(packed[k], shapes)
            for i, n in enumerate(names):
                dst[n] = per_param[i][k] if per_param[i] is not None else rest[i]
        return g8

    after = tok
    for group in RS_SUM_GROUPS:
        names, g_done, land_done = [], [], []
        for k in group:
            nk, send, recv, g_thru, land_thru = rs_open[k]
            gd, ld = _rs_wait("rs_wait_" + nk[0], g_thru, land_thru, send, recv, after)
            names, g_done, land_done = names + nk, g_done + gd, land_done + ld
        for n, g_sum in zip(names, _rs_sum("rs_sum_" + names[0], g_done, land_done)):
            g_out, dl, mn, vn = _adamw("adamw_" + n, wt[n][0], g_sum, mom[n][0], var[n][0])
            grads[n], delta[n], new_m[n], new_v[n] = g_out[None], dl[None], mn[None], vn[None]
            after = dl
        if group is RS_SUM_GROUPS[-2]:
            small_update("early", early, sg_blk, _bcast_wait("sg_wait", sg_blk, sg_land, sg_send, sg_recv, after))

    lt_land = _bcast_wait("late_wait", lt_blk, lt_land, lt_send, lt_recv, after)
    late8 = small_update("late", SMALL_LATE, lt_blk, lt_land).reshape(N_DEV, -1)
    loss = jnp.sum(late8[:, 10 * d])
    dmod_sh = lax.dynamic_slice(late8, (0, s_me * ncol), (N_DEV, ncol))
    g_w_ada = _ada_bwd("ada_bwd", c_all, dmod_sh)
    dl, mn, vn = _adamw("adamw_w_ada", w_ada[0], g_w_ada, m_w_ada[0], v_w_ada[0], with_g=False)
    grads['w_ada'], delta['w_ada'], new_m['w_ada'], new_v['w_ada'] = g_w_ada[None], dl[None], mn[None], vn[None]

    return (loss, dx0[None], *[grads[n] for n in WEIGHTS], *[delta[n] for n in WEIGHTS],
            *[new_m[n] for n in WEIGHTS], *[new_v[n] for n in WEIGHTS])
```

```python
import functools
import math

import jax
import jax.numpy as jnp
from jax import lax
from jax.experimental import pallas as pl
from jax.experimental.pallas import tpu as pltpu

F32 = jnp.float32
BF16 = jnp.bfloat16
MESH = pl.DeviceIdType.MESH

EPS = 1e-6
POOL_WINDOWS = (2, 4, 8, 16)
POOL_HALO = 16
SSM_GROUP = 16
SSM_STATE = 64
SSM_BLOCKS = 4
N_DEV = 8
N_CHIPS = 4
ADAM_LR = 0.001
ADAM_B1 = 0.9
ADAM_B2 = 0.999
ADAM_EPS = 1e-08
ADAM_WD = 0.01
ADAM_STEP = 10
VMEM_LIMIT = 56 * 1024 * 1024
ROWS = 512


ANY_SPEC = pl.BlockSpec(memory_space=pl.ANY)
HBM_SPEC = pl.BlockSpec(memory_space=pltpu.HBM)
SEM_SPEC = pl.BlockSpec(memory_space=pltpu.SEMAPHORE)
EFFECT = pltpu.SideEffectType.DATAFLOW_SIDE_EFFECTING


def _hbm(a):
    return pltpu.with_memory_space_constraint(a, pltpu.HBM)


def _pcall(body, **kw):
    return pl.pallas_call(body, **kw)


def _params(*sem):
    return pltpu.CompilerParams(dimension_semantics=sem, vmem_limit_bytes=VMEM_LIMIT)


def _pick(n, cap, mult=128):
    if n <= cap:
        return n
    best = None
    for d in range(mult, cap + 1, mult):
        if n % d == 0:
            best = d
    assert best is not None, (n, cap, mult)
    return best


def _sigmoid(v):
    return 1.0 / (1.0 + jnp.exp(-v))


def _rowwise(name, fn, ins, params, outs, reds, tm, deps=()):
    t = ins[0][0].shape[0]
    tm = min(tm, t)
    nb = t // tm
    ni, npar, no, nd = len(ins), len(params), len(outs), len(deps)

    def body(*refs):
        iv = [r[...] for r in refs[:ni]]
        pv = [r[...] for r in refs[ni:ni + npar]]
        o_refs = refs[ni + npar + nd:ni + npar + nd + no]
        r_refs = refs[ni + npar + nd + no:]
        ovals, rvals = fn(iv, pv)
        for o_ref, val in zip(o_refs, ovals):
            off = 0
            if isinstance(val, tuple) and val[0] == "at":
                _, off, val = val
            parts = val if isinstance(val, (list, tuple)) else [val]
            for p in parts:
                o_ref[:, off:off + p.shape[1]] = p.astype(o_ref.dtype)
                off += p.shape[1]
        if r_refs:
            @pl.when(pl.program_id(0) == 0)
            def _():
                for r in r_refs:
                    r[...] = jnp.zeros_like(r)
            for r, val in zip(r_refs, rvals):
                r[...] += val

    in_specs = [pl.BlockSpec((tm, w), functools.partial(lambda i, cb: (i, cb), cb=cb)) for (_, w, cb) in ins]
    in_specs += [pl.BlockSpec(p.shape, lambda i: (0, 0)) for p in params]
    in_specs += [ANY_SPEC] * nd
    out_shape = [jax.ShapeDtypeStruct((t, w), dt) for (w, dt) in outs]
    out_shape += [jax.ShapeDtypeStruct((1, w), F32) for w in reds]
    out_specs = [pl.BlockSpec((tm, w), lambda i: (i, 0)) for (w, _) in outs]
    out_specs += [pl.BlockSpec((1, w), lambda i: (0, 0)) for w in reds]
    res = _pcall(body, name=name, grid=(nb,), in_specs=in_specs, out_specs=out_specs, out_shape=out_shape,
                 compiler_params=_params("arbitrary"))(*[a for a, _, _ in ins], *params, *deps)
    return res


def _colsum(v):
    return jnp.sum(v, axis=0, keepdims=True)


def _mm(name, a, b, *, ta=False, tb=False, b_stacked=False, out_stacked=False, tm=ROWS, tn=1024, tk=2816,
        out_dtype=BF16, epi=None, epi_ins=(), epi_outs=None, epi_reds=(), deps=(), j_outer=False, a_halves=False,
        b_halves=False, b_resident=False, wgrad=None, wgrad_rows=False):
    if a_halves:
        _, m, kdim = a.shape
        kdim *= 2
    elif ta:
        kdim, m = a.shape
    else:
        m, kdim = a.shape
    ns = None
    if b_stacked:
        ns = b.shape[2]
        n = b.shape[1] if tb else N_CHIPS * ns
        assert kdim == (N_CHIPS * ns if tb else b.shape[1]), (name, a.shape, b.shape)
    else:
        if b_halves:
            n = 2 * b.shape[2]
            assert kdim == b.shape[1] and not tb, (name, a.shape, b.shape)
        else:
            n = b.shape[0] if tb else b.shape[1]
            assert kdim == (b.shape[1] if tb else b.shape[0]), (name, a.shape, b.shape)
        if out_stacked:
            ns = n // N_CHIPS

    def shards(want):
        return max(g for g in (1, 2, 4) if g * ns <= max(want, ns))

    tm = _pick(m, tm, 128 if ta else 8)
    gn = gk = 1
    if (b_stacked and not tb) or out_stacked:
        gn = shards(min(tn, n // 2) if b_halves else tn)
        tn = gn * ns
    else:
        tn = _pick(n, tn)
    if b_stacked and tb:
        gk = shards(tk)
        tk = gk * ns
    else:
        tk = _pick(kdim, tk, 8 if ta else 128)
    nm, nn, nk = m // tm, n // tn, kdim // tk

    def ij(f):
        return (lambda g0, g1, k: f(g1, g0, k)) if j_outer else f

    if a_halves:
        assert b_stacked and tb and gk == N_CHIPS and nk == 1, name
        a_spec = pl.BlockSpec((2, tm, kdim // 2), ij(lambda i, j, k: (0, i, 0)))
    elif ta:
        a_spec = pl.BlockSpec((tk, tm), ij(lambda i, j, k: (k, i)))
    else:
        a_spec = pl.BlockSpec((tm, tk), ij(lambda i, j, k: (i, k)))
    if b_stacked and not tb:
        b_spec = pl.BlockSpec((gn, tk, ns), ij(lambda i, j, k: (j, k, 0)))
    elif b_stacked and tb:
        b_spec = pl.BlockSpec((gk, tn, ns), ij(lambda i, j, k: (k, j, 0)))
    elif b_halves:
        bph = (n // 2) // tn
        b_spec = pl.BlockSpec((None, tk, tn), ij(lambda i, j, k: (j // bph, k, j % bph)))
    elif tb:
        b_spec = pl.BlockSpec((tn, tk), ij(lambda i, j, k: (j, k)))
    else:
        b_spec = pl.BlockSpec((tk, tn), ij(lambda i, j, k: (k, j)))
    dims = (((0 if ta else 1,), (1 if tb else 0,)), ((), ()))

    if epi_outs is None:
        if out_stacked:
            epi_outs = [((N_CHIPS, m, ns), out_dtype, (gn, tm, ns), lambda i, j: (j, i, 0))]
        else:
            epi_outs = [((m, n), out_dtype, (tm, tn), lambda i, j: (i, j))]
    ne, no, nd, nr = len(epi_ins), len(epi_outs), len(deps), len(epi_reds)
    nx = 0 if wgrad is None else 1
    assert not (nr or nx) or (nn == 1 and not j_outer), name
    assert not nx or (nk == 1 and not ta and not a_halves and kdim % N_CHIPS == 0), name

    def body(a_ref, b_ref, *rest):
        e_refs = rest[:ne]
        x_refs = rest[ne:ne + nx]
        o_refs = rest[ne + nx + nd:ne + nx + nd + no]
        r_refs = rest[ne + nx + nd + no:ne + nx + nd + no + nr]
        wg_refs = rest[ne + nx + nd + no + nr:ne + nx + nd + no + nr + nx]
        scratch = rest[ne + nx + nd + no + nr + nx:]
        av = None if a_halves else a_ref[...].astype(BF16)
        if nx:
            wg_acc = scratch[-1]
            i = pl.program_id(0)
            pw_ = lax.dot_general(x_refs[0][...].astype(BF16), av, (((0,), (0,)), ((), ())), preferred_element_type=F32)

            @pl.when(i == 0)
            def _():
                wg_acc[...] = pw_

            @pl.when(i > 0)
            def _():
                wg_acc[...] += pw_

            @pl.when(i == nm - 1)
            def _():
                if wgrad_rows:
                    wg_refs[0][...] = wg_acc[...].astype(BF16)
                else:
                    ks = kdim // N_CHIPS
                    for s in range(N_CHIPS):
                        wg_refs[0][s] = wg_acc[:, s * ks:(s + 1) * ks].astype(BF16)
        if b_stacked and not tb:
            parts = [lax.dot_general(av, b_ref[s].astype(BF16), dims, preferred_element_type=F32) for s in range(gn)]
        elif b_stacked and tb:
            p = None
            for s in range(gk):
                if a_halves:
                    a_s = a_ref[s // 2, :, (s % 2) * ns:(s % 2 + 1) * ns].astype(BF16)
                else:
                    a_s = av[:, s * ns:(s + 1) * ns]
                q = lax.dot_general(a_s, b_ref[s].astype(BF16), dims, preferred_element_type=F32)
                p = q if p is None else p + q
            parts = [p]
        else:
            parts = [lax.dot_general(av, b_ref[...].astype(BF16), dims, preferred_element_type=F32)]

        def finish(acc_parts):
            if epi is None and out_stacked:
                acc = acc_parts[0]
                for s in range(gn):
                    o_refs[0][s] = acc[:, s * ns:(s + 1) * ns].astype(out_dtype)
            elif epi is None:
                w = acc_parts[0].shape[1]
                for s, part in enumerate(acc_parts):
                    o_refs[0][:, s * w:(s + 1) * w] = part.astype(out_dtype)
            else:
                acc = acc_parts[0] if len(acc_parts) == 1 else jnp.concatenate(acc_parts, axis=1)
                vals = epi(acc, *[r[...] for r in e_refs])
                if nr:
                    vals, reds = vals

                    @pl.when(pl.program_id(0) == 0)
                    def _():
                        for r in r_refs:
                            r[...] = jnp.zeros_like(r)
                    for r, val in zip(r_refs, reds):
                        r[...] += val
                for o_ref, val in zip(o_refs, vals):
                    if isinstance(val, tuple) and val[0] == "at":
                        o_ref[:, val[1]:val[1] + val[2].shape[1]] = val[2].astype(o_ref.dtype)
                    else:
                        o_ref[...] = val.astype(o_ref.dtype)

        if nk == 1:
            finish(parts)
        else:
            acc_ref = scratch[0]
            k = pl.program_id(2)
            w = parts[0].shape[1]

            @pl.when(k == 0)
            def _():
                for s, part in enumerate(parts):
                    acc_ref[:, s * w:(s + 1) * w] = part

            @pl.when(k > 0)
            def _():
                for s, part in enumerate(parts):
                    acc_ref[:, s * w:(s + 1) * w] += part

            @pl.when(k == nk - 1)
            def _():
                finish([acc_ref[...]])

    def _ij(f):
        return ij(lambda i, j, k: f(i, j))

    if b_resident:
        assert nk == 1 and (nn == 1 or j_outer), name
        b_spec = pl.BlockSpec(b_spec.block_shape, b_spec.index_map, pipeline_mode=pl.Buffered(1))
    in_specs = [a_spec, b_spec] + [pl.BlockSpec(blk, _ij(f)) for (_, blk, f) in epi_ins]
    if nx:
        in_specs.append(pl.BlockSpec((tm, wgrad.shape[1]), lambda i, j, k: (i, 0)))
    in_specs += [ANY_SPEC] * nd
    out_specs = [pl.BlockSpec(blk, _ij(f)) for (_, _, blk, f) in epi_outs]
    out_specs += [pl.BlockSpec((1, w), lambda *_: (0, 0)) for w in epi_reds]
    out_shape = [jax.ShapeDtypeStruct(s, dt) for (s, dt, _, _) in epi_outs] + [jax.ShapeDtypeStruct((1, w), F32) for w in epi_reds]
    scratch = [pltpu.VMEM((tm, tn), F32)] if nk > 1 else []
    if nx:
        wg_shape = (wgrad.shape[1], kdim) if wgrad_rows else (N_CHIPS, wgrad.shape[1], kdim // N_CHIPS)
        out_specs.append(pl.BlockSpec(wg_shape, lambda *_: (0,) * len(wg_shape)))
        out_shape.append(jax.ShapeDtypeStruct(wg_shape, BF16))
        scratch.append(pltpu.VMEM((wgrad.shape[1], kdim), F32))
    grid = (nn, nm, nk) if j_outer else (nm, nn, nk)
    sem = ("arbitrary",) * 3 if (nr or nx) else ("parallel", "parallel", "arbitrary")
    res = _pcall(body, name=name, grid=grid, in_specs=in_specs, out_specs=out_specs, out_shape=out_shape, scratch_shapes=scratch,
                 compiler_params=_params(*sem))(a, b, *[x for x, _, _ in epi_ins], *([wgrad] * nx), *deps)
    return res[0] if len(res) == 1 else res


def _norm_fwd(name, x, g, sc, sh, deps=()):
    d = x.shape[1]

    def fn(iv, pv):
        (xv,), (gv, scv, shv) = iv, pv
        r = lax.rsqrt(jnp.mean(xv * xv, axis=-1, keepdims=True) + EPS)
        return [xv * r * gv * (1.0 + scv) + shv], []

    return _rowwise(name, fn, [(x, d, 0)], [g, sc, sh], [(d, BF16)], [], ROWS, deps=deps)[0]


def _adamw(name, w, g, m, v, tm=256, with_g=True):
    c = w.shape[1]

    def fn(iv, pv):
        wv, gv, mv, vv = iv
        mn = ADAM_B1 * mv + (1.0 - ADAM_B1) * gv
        vn = ADAM_B2 * vv + (1.0 - ADAM_B2) * (gv * gv)
        m_hat = mn / (1.0 - ADAM_B1 ** ADAM_STEP)
        v_hat = vn / (1.0 - ADAM_B2 ** ADAM_STEP)
        delta = -ADAM_LR * (m_hat / (jnp.sqrt(v_hat) + ADAM_EPS) + ADAM_WD * wv)
        return ([gv] if with_g else []) + [delta, mn, vn], []

    return _rowwise(name, fn, [(w, c, 0), (g, c, 0), (m, c, 0), (v, c, 0)], [], [(c, F32)] * (4 if with_g else 3), [],
                    _pick(w.shape[0], tm, 8))


def _unpack_plan(shape):
    n = math.prod(shape)
    if len(shape) == 2 and shape[0] == 1 and n % 128 == 0:
        return [((slice(None), slice(128 * r, 128 * (r + 1))), slice(r, r + 1), slice(None)) for r in range(n // 128)]
    if len(shape) == 2 and shape[0] == 1 and n < 128:
        return [((slice(None), slice(None)), slice(0, 1), slice(0, n))]
    if len(shape) == 1 and n % 128 == 0:
        return [((slice(128 * r, 128 * (r + 1)),), r, slice(None)) for r in range(n // 128)]
    if len(shape) == 3 and shape[0] == 1 and shape[2] == 64:
        return [((0, slice(2 * r + h, 2 * r + h + 1), slice(None)), slice(r, r + 1), slice(64 * h, 64 * (h + 1)))
                for r in range(n // 128) for h in range(2)]
    if len(shape) == 4 and shape[0] == 1 and shape[2:] == (128, 128):
        return [((0, k), slice(128 * k, 128 * (k + 1)), slice(None)) for k in range(shape[1])]
    return None


def _adamw_small(name, w, g8, m, v, shapes):
    r = w.shape[0]
    plans, rows0, off = [], [], 0
    for s in shapes:
        plans.append(_unpack_plan(s))
        rows0.append(off // 128)
        off += math.prod(s) + (-math.prod(s)) % 128
    direct = [i for i, p in enumerate(plans) if p is not None]

    def body(w_ref, g_ref, m_ref, v_ref, *out):
        packed = out[4 * len(direct):]
        gv = g_ref[0:r, :]
        for k in range(1, N_DEV):
            gv = gv + g_ref[k * r:(k + 1) * r, :]
        mn = ADAM_B1 * m_ref[...] + (1.0 - ADAM_B1) * gv
        vn = ADAM_B2 * v_ref[...] + (1.0 - ADAM_B2) * (gv * gv)
        m_hat = mn / (1.0 - ADAM_B1 ** ADAM_STEP)
        v_hat = vn / (1.0 - ADAM_B2 ** ADAM_STEP)
        packed[0][...] = gv
        packed[1][...] = -ADAM_LR * (m_hat / (jnp.sqrt(v_hat) + ADAM_EPS) + ADAM_WD * w_ref[...])
        packed[2][...] = mn
        packed[3][...] = vn
        for di, i in enumerate(direct):
            for kind in range(4):
                o_ref, src = out[4 * di + kind], packed[kind]
                for o_idx, row, lanes in plans[i]:
                    row = (rows0[i] + row) if isinstance(row, int) else slice(rows0[i] + row.start, rows0[i] + row.stop)
                    o_ref[o_idx] = src[row, lanes]

    out_shape = [jax.ShapeDtypeStruct(shapes[i], F32) for i in direct for _ in range(4)]
    out_shape += [jax.ShapeDtypeStruct((r, 128), F32)] * 4
    res = _pcall(body, name=name, out_shape=out_shape,
                 compiler_params=pltpu.CompilerParams(vmem_limit_bytes=VMEM_LIMIT))(w, g8, m, v)
    per_param = [None] * len(shapes)
    for di, i in enumerate(direct):
        per_param[i] = res[4 * di:4 * di + 4]
    return per_param, res[4 * len(direct):]


def _pool_fwd(name, uin, pool_w, pool_b, pool_scale, w_up, pw, tm=ROWS):
    t = uin.shape[0]
    tm = min(tm, t)
    ng = len(POOL_WINDOWS)
    gw = pw // ng
    ns = w_up.shape[2]

    def body(u_ref, w_ref, b_ref, s_ref, wu_ref, p_ref, z_ref, y_ref, ext):
        i = pl.program_id(0)

        @pl.when(i == 0)
        def _():
            ext[0:POOL_HALO, :] = jnp.zeros((POOL_HALO, pw), F32)

        u = u_ref[...].astype(F32)
        ext[POOL_HALO:POOL_HALO + tm, :] = u
        pos = i * tm + lax.broadcasted_iota(jnp.int32, (tm, 1), 0)
        for k, win in enumerate(POOL_WINDOWS):
            cols = slice(k * gw, (k + 1) * gw)
            acc = u[:, cols]
            for j in range(1, win):
                acc = acc + ext[POOL_HALO - j:POOL_HALO - j + tm, cols]
            cnt = jnp.minimum(pos + 1, win).astype(F32)
            z = acc / cnt - u[:, cols]
            zp = jnp.dot(z.astype(BF16), w_ref[k].astype(BF16), preferred_element_type=F32) + b_ref[:, cols]
            p_ref[:, cols] = (zp * s_ref[:, cols]).astype(BF16)
            z_ref[:, cols] = z.astype(BF16)
        ext[0:POOL_HALO, :] = u[tm - POOL_HALO:tm, :]
        pv = p_ref[...]
        for s in range(N_CHIPS):
            y_ref[:, s * ns:(s + 1) * ns] = jnp.dot(pv, wu_ref[s], preferred_element_type=F32).astype(BF16)

    full3 = lambda i: (0, 0, 0)
    return _pcall(
        body, name=name, grid=(t // tm,),
        in_specs=[pl.BlockSpec((tm, pw), lambda i: (i, 0)), pl.BlockSpec(pool_w.shape, full3),
                  pl.BlockSpec(pool_b.shape, lambda i: (0, 0)), pl.BlockSpec(pool_scale.shape, lambda i: (0, 0)),
                  pl.BlockSpec(w_up.shape, full3)],
        out_specs=[pl.BlockSpec((tm, pw), lambda i: (i, 0))] * 2 + [pl.BlockSpec((tm, N_CHIPS * ns), lambda i: (i, 0))],
        out_shape=[jax.ShapeDtypeStruct((t, pw), BF16)] * 2 + [jax.ShapeDtypeStruct((t, N_CHIPS * ns), BF16)],
        scratch_shapes=[pltpu.VMEM((POOL_HALO + tm, pw), F32)],
        compiler_params=_params("arbitrary"))(uin, pool_w, pool_b, pool_scale, w_up)


def _pool_bwd(name, dy, z, p, pool_w, pool_b, pool_scale, w_up, duin, tm=ROWS):
    t, pw = z.shape
    tm = min(tm, t)
    nb = t // tm
    ng = len(POOL_WINDOWS)
    gw = pw // ng
    ns = w_up.shape[2]

    def body(dy_ref, z_ref, p_ref, w_ref, b_ref, s_ref, wu_ref, duin_ref, du_ref, dw_ref, db_ref, ds_ref, dwu_ref, ext, dwu_acc):
        i = pl.program_id(0)

        @pl.when(i == 0)
        def _():
            ext[tm:tm + POOL_HALO, :] = jnp.zeros((POOL_HALO, pw), F32)
            dw_ref[...] = jnp.zeros_like(dw_ref)
            db_ref[...] = jnp.zeros_like(db_ref)
            ds_ref[...] = jnp.zeros_like(ds_ref)
            dwu_acc[...] = jnp.zeros_like(dwu_acc)

        dwu_acc[...] += lax.dot_general(p_ref[...], dy_ref[...], (((0,), (0,)), ((), ())), preferred_element_type=F32)

        @pl.when(i == nb - 1)
        def _():
            for s in range(N_CHIPS):
                dwu_ref[s] = dwu_acc[:, s * ns:(s + 1) * ns].astype(BF16)

        pos = (nb - 1 - i) * tm + lax.broadcasted_iota(jnp.int32, (tm, 1), 0)
        dp = None
        for s in range(N_CHIPS):
            q = lax.dot_general(dy_ref[:, s * ns:(s + 1) * ns], wu_ref[s], (((1,), (1,)), ((), ())), preferred_element_type=F32)
            dp = q if dp is None else dp + q
        for k, win in enumerate(POOL_WINDOWS):
            cols = slice(k * gw, (k + 1) * gw)
            zk = z_ref[:, cols]
            dpk = dp[:, cols]
            wk = w_ref[k].astype(BF16)
            zp = jnp.dot(zk, wk, preferred_element_type=F32) + b_ref[:, cols]
            ds_ref[:, cols] += _colsum(dpk * zp)
            dzp = dpk * s_ref[:, cols]
            db_ref[:, cols] += _colsum(dzp)
            dzpb = dzp.astype(BF16)
            dz = lax.dot_general(dzpb, wk, (((1,), (1,)), ((), ())), preferred_element_type=F32)
            dw_ref[k] += lax.dot_general(zk, dzpb, (((0,), (0,)), ((), ())), preferred_element_type=F32)
            cnt = jnp.minimum(pos + 1, win).astype(F32)
            r = dz / cnt
            ext[0:tm, cols] = r
            acc = r - dz
            for j in range(1, win):
                acc = acc + ext[j:j + tm, cols]
            du_ref[:, cols] = acc.astype(BF16)
        ext[tm:tm + POOL_HALO, :] = ext[0:POOL_HALO, :]

    rev = lambda i: (nb - 1 - i, 0)
    full3 = lambda i: (0, 0, 0)
    return _pcall(
        body, name=name, grid=(nb,),
        in_specs=[pl.BlockSpec((tm, N_CHIPS * ns), rev), pl.BlockSpec((tm, pw), rev), pl.BlockSpec((tm, pw), rev),
                  pl.BlockSpec(pool_w.shape, full3), pl.BlockSpec(pool_b.shape, lambda i: (0, 0)),
                  pl.BlockSpec(pool_scale.shape, lambda i: (0, 0)), pl.BlockSpec(w_up.shape, full3), ANY_SPEC],
        out_specs=[pl.BlockSpec((tm, pw), rev), pl.BlockSpec(pool_w.shape, full3),
                   pl.BlockSpec((1, pw), lambda i: (0, 0)), pl.BlockSpec((1, pw), lambda i: (0, 0)), pl.BlockSpec(w_up.shape, full3)],
        out_shape=[jax.ShapeDtypeStruct(duin.shape, BF16), jax.ShapeDtypeStruct(pool_w.shape, F32),
                   jax.ShapeDtypeStruct((1, pw), F32), jax.ShapeDtypeStruct((1, pw), F32), jax.ShapeDtypeStruct(w_up.shape, BF16)],
        scratch_shapes=[pltpu.VMEM((tm + POOL_HALO, pw), F32), pltpu.VMEM((pw, N_CHIPS * ns), F32)], input_output_aliases={7: 0},
        compiler_params=_params("arbitrary"))(dy, z, p, pool_w, pool_b, pool_scale, w_up, duin)


def _ssm_disc(lrl, li, ldt):
    lr = -jnp.exp(lrl)
    dt = jnp.exp(ldt)
    mag = jnp.exp(lr * dt)
    ang = li * dt
    ab_re = mag * jnp.cos(ang)
    ab_im = mag * jnp.sin(ang)
    num_re = ab_re - 1.0
    num_im = ab_im
    den = lr * lr + li * li
    f_re = (num_re * lr + num_im * li) / den
    f_im = (num_im * lr - num_re * li) / den
    return lr, dt, mag, ang, ab_re, ab_im, num_re, num_im, den, f_re, f_im


def _ssm_prep(name, lrl, li, ldt, b_re, b_im):
    gn, h = b_re.shape

    def body(lrl_ref, li_ref, ldt_ref, br_ref, bi_ref, lrdt_ref, ang_ref, bbr_ref, bbi_ref):
        lr, dt, _, ang, _, _, _, _, _, f_re, f_im = _ssm_disc(lrl_ref[...], li_ref[...], ldt_ref[...])
        lrdt_ref[...] = lr * dt
        ang_ref[...] = ang
        br, bi = br_ref[...], bi_ref[...]
        bbr_ref[...] = f_re * br - f_im * bi
        bbi_ref[...] = f_re * bi + f_im * br

    col = jax.ShapeDtypeStruct((gn, 1), F32)
    mat = jax.ShapeDtypeStruct((gn, h), F32)
    return _pcall(body, name=name, out_shape=[col, col, mat, mat])(lrl, li, ldt, b_re, b_im)


def _ssm_param_bwd(name, lrl, li, ldt, b_re, b_im, g_abre, g_abim, g_bbre, g_bbim):
    gn, h = b_re.shape

    def body(lrl_ref, li_ref, ldt_ref, br_ref, bi_ref, gar_ref, gai_ref, gbr_ref, gbi_ref,
             dlrl_ref, dli_ref, dldt_ref, dbr_ref, dbi_ref):
        li_v = li_ref[...]
        lr, dt, mag, ang, ab_re, ab_im, num_re, num_im, den, f_re, f_im = _ssm_disc(lrl_ref[...], li_v, ldt_ref[...])
        br, bi = br_ref[...], bi_ref[...]
        gbr, gbi = gbr_ref[...], gbi_ref[...]
        g_fre = jnp.sum(gbr * br + gbi * bi, axis=1, keepdims=True)
        g_fim = jnp.sum(gbi * br - gbr * bi, axis=1, keepdims=True)
        dbr_ref[...] = gbr * f_re + gbi * f_im
        dbi_ref[...] = gbi * f_re - gbr * f_im
        g_num_re = (g_fre * lr - g_fim * li_v) / den
        g_num_im = (g_fre * li_v + g_fim * lr) / den
        g_den = -(g_fre * f_re + g_fim * f_im) / den
        g_lr = (g_fre * num_re + g_fim * num_im) / den + g_den * 2.0 * lr
        g_li = (g_fre * num_im - g_fim * num_re) / den + g_den * 2.0 * li_v
        g_are = gar_ref[...] + g_num_re
        g_aim = gai_ref[...] + g_num_im
        g_mag = g_are * jnp.cos(ang) + g_aim * jnp.sin(ang)
        g_ang = g_aim * ab_re - g_are * ab_im
        g_lrdt = g_mag * mag
        g_lr = g_lr + g_lrdt * dt
        g_dt = g_lrdt * lr + g_ang * li_v
        g_li = g_li + g_ang * dt
        dlrl_ref[...] = g_lr * lr
        dli_ref[...] = g_li
        dldt_ref[...] = g_dt * dt

    col = jax.ShapeDtypeStruct((gn, 1), F32)
    mat = jax.ShapeDtypeStruct((gn, h), F32)
    return _pcall(body, name=name, out_shape=[col, col, col, mat, mat])(lrl, li, ldt, b_re, b_im, g_abre, g_abim, g_bbre, g_bbim)


def _pow_rows(lrdt, ang, k):
    mag = jnp.exp(k * lrdt)
    return mag * jnp.cos(k * ang), mag * jnp.sin(k * ang)


def _ssm_chunk(t):
    return 256 if t >= 2048 else 128


def _to_segments(dst, srcs, jn):
    for q, src in enumerate(srcs):
        for j in range(jn):
            dst[8 * j:8 * j + 8, 128 * q:128 * (q + 1)] = src[pl.ds(j, 8, stride=jn), :]


def _from_segments(dst, src, q, jn, dtype):
    for s in range(8):
        dst[s * jn:(s + 1) * jn, 128 * q:128 * (q + 1)] = src[q, pl.ds(s, jn, stride=8), :].astype(dtype)


def _fill_rows8(dst_re, dst_im, v_re, v_im):
    for j in range(v_re.shape[0]):
        dst_re[8 * j:8 * j + 8, :] = jnp.broadcast_to(v_re[j:j + 1, :], (8, v_re.shape[1]))
        dst_im[8 * j:8 * j + 8, :] = jnp.broadcast_to(v_im[j:j + 1, :], (8, v_im.shape[1]))


def _cmul(ar, ai, br, bi):
    return ar * br - ai * bi, ar * bi + ai * br


def _cmul_conj(ar, ai, br, bi):
    return ar * br + ai * bi, ar * bi - ai * br


def _ssm_fwd(name, uin, lrdt, ang, bb_re, bb_im, cc_re, cc_im, d_skip, sw):
    t = uin.shape[0]
    gn = lrdt.shape[1]
    lc = _ssm_chunk(t)
    jn = lc // 8
    ub, sb = sw // SSM_BLOCKS, gn // SSM_BLOCKS
    nq = sw // 128
    assert ub == 128 and nq == SSM_BLOCKS

    def body(u_ref, lrdt_ref, ang_ref, bbr_ref, bbi_ref, ccr_ref, cci_ref, d_ref, y_ref, ge_ref, sre_ref, sim_ref,
             p_re, p_im, a_re, a_im, up, yp, cst_re, cst_im, car_re, car_im):
        i = pl.program_id(0)
        lrdt_v, ang_v = lrdt_ref[...], ang_ref[...]

        @pl.when(i == 0)
        def _():
            k = (lax.broadcasted_iota(jnp.int32, (jn, 1), 0) + 1).astype(F32)
            _fill_rows8(p_re, p_im, *_pow_rows(lrdt_v, ang_v, k))
            car_re[...] = jnp.zeros_like(car_re)
            car_im[...] = jnp.zeros_like(car_im)

        for q in range(nq):
            yp[q] = u_ref[:, 128 * q:128 * (q + 1)].astype(F32)
        _to_segments(up, [yp.at[q] for q in range(nq)], jn)
        u = up[...]
        ubf = u.astype(BF16)
        for q in range(SSM_BLOCKS):
            uq = ubf[:, q * ub:(q + 1) * ub]
            a_re[:, q * sb:(q + 1) * sb] = jnp.dot(uq, bbr_ref[q], preferred_element_type=F32)
            a_im[:, q * sb:(q + 1) * sb] = jnp.dot(uq, bbi_ref[q], preferred_element_type=F32)
        a1r, a1i = _pow_rows(lrdt_v, ang_v, 1.0)
        ajr, aji = _pow_rows(lrdt_v, ang_v, float(jn))
        for q in range(0, SSM_BLOCKS, 2):
            cols = slice(q * sb, (q + 2) * sb)
            ar8 = jnp.broadcast_to(a1r[:, cols], (8, 2 * sb))
            ai8 = jnp.broadcast_to(a1i[:, cols], (8, 2 * sb))

            def step(j, carry, cols=cols, ar8=ar8, ai8=ai8):
                sr, si = carry
                rows = pl.ds(pl.multiple_of(j * 8, 8), 8)
                mr, mi = _cmul(ar8, ai8, sr, si)
                nr, ni = mr + a_re[rows, cols], mi + a_im[rows, cols]
                a_re[rows, cols] = nr
                a_im[rows, cols] = ni
                return nr, ni

            lax.fori_loop(1, jn, step, (a_re[0:8, cols], a_im[0:8, cols]), unroll=4)
        er, ei = a_re[lc - 8:lc, :], a_im[lc - 8:lc, :]
        gr, gi = car_re[...], car_im[...]
        for s in range(8):
            cst_re[s:s + 1, :] = gr
            cst_im[s:s + 1, :] = gi
            mr, mi = _cmul(ajr, aji, gr, gi)
            gr, gi = mr + er[s:s + 1, :], mi + ei[s:s + 1, :]
        car_re[...] = gr
        car_im[...] = gi
        for q in range(SSM_BLOCKS):
            cols = slice(q * sb, (q + 1) * sb)
            cr = jnp.tile(cst_re[:, cols], (jn, 1))
            ci = jnp.tile(cst_im[:, cols], (jn, 1))
            mr, mi = _cmul(p_re[:, cols], p_im[:, cols], cr, ci)
            srb, sib = (a_re[:, cols] + mr).astype(BF16), (a_im[:, cols] + mi).astype(BF16)
            sre_ref[:, cols] = srb
            sim_ref[:, cols] = sib
            ycols = slice(q * ub, (q + 1) * ub)
            y = (jnp.dot(srb, ccr_ref[q], preferred_element_type=F32) - jnp.dot(sib, cci_ref[q], preferred_element_type=F32)
                 + d_ref[:, ycols] * u[:, ycols])
            yp[q] = y
            _from_segments(y_ref, yp, q, jn, F32)
            yt = y_ref[:, ycols]
            ge_ref[:, ycols] = (0.5 * yt * (1.0 + lax.erf(yt * (1.0 / math.sqrt(2.0))))).astype(BF16)

    row = lambda i: (0, 0)
    blk3 = lambda i: (0, 0, 0)
    return _pcall(
        body, name=name, grid=(t // lc,),
        in_specs=[pl.BlockSpec((lc, sw), lambda i: (i, 1)), pl.BlockSpec((1, gn), row), pl.BlockSpec((1, gn), row),
                  pl.BlockSpec(bb_re.shape, blk3), pl.BlockSpec(bb_im.shape, blk3),
                  pl.BlockSpec(cc_re.shape, blk3), pl.BlockSpec(cc_im.shape, blk3), pl.BlockSpec((1, sw), row)],
        out_specs=[pl.BlockSpec((lc, sw), lambda i: (i, 0)), pl.BlockSpec((lc, sw), lambda i: (i, 0)),
                   pl.BlockSpec((lc, gn), lambda i: (i, 0)), pl.BlockSpec((lc, gn), lambda i: (i, 0))],
        out_shape=[jax.ShapeDtypeStruct((t, sw), F32), jax.ShapeDtypeStruct((t, sw), BF16),
                   jax.ShapeDtypeStruct((t, gn), BF16), jax.ShapeDtypeStruct((t, gn), BF16)],
        scratch_shapes=[pltpu.VMEM((lc, gn), F32), pltpu.VMEM((lc, gn), F32), pltpu.VMEM((lc, gn), F32), pltpu.VMEM((lc, gn), F32),
                        pltpu.VMEM((lc, sw), F32), pltpu.VMEM((nq, lc, 128), F32),
                        pltpu.VMEM((8, gn), F32), pltpu.VMEM((8, gn), F32), pltpu.VMEM((1, gn), F32), pltpu.VMEM((1, gn), F32)],
        compiler_params=_params("arbitrary"))(uin, lrdt, ang, bb_re, bb_im, cc_re, cc_im, d_skip)


def _ssm_bwd(name, dge, y, uin, s_re, s_im, lrdt, ang, bbt_re, bbt_im, cct_re, cct_im, d_skip, sw, duin):
    t = uin.shape[0]
    gn = lrdt.shape[1]
    lc = _ssm_chunk(t)
    nb = t // lc
    jn = lc // 8
    ub, sb = sw // SSM_BLOCKS, gn // SSM_BLOCKS
    nq = sw // 128
    tail = 16

    def body(dge_ref, y_ref, u_ref, sre_ref, sim_ref, tre_ref, tim_ref, lrdt_ref, ang_ref, btr_ref, bti_ref, ctr_ref, cti_ref,
             d_ref, duin_ref, du_ref, dar_ref, dai_ref, dbr_ref, dbi_ref, dcr_ref, dci_ref, dd_ref,
             q_re, q_im, a_re, a_im, dyp, up, dys, us, dup, cst_re, cst_im, sp_re, sp_im, car_re, car_im):
        i = pl.program_id(0)
        lrdt_v, ang_v = lrdt_ref[...], ang_ref[...]

        @pl.when(i == 0)
        def _():
            k = (jn - lax.broadcasted_iota(jnp.int32, (jn, 1), 0)).astype(F32)
            _fill_rows8(q_re, q_im, *_pow_rows(lrdt_v, ang_v, k))
            car_re[...] = jnp.zeros_like(car_re)
            car_im[...] = jnp.zeros_like(car_im)
            for r in (dar_ref, dai_ref, dbr_ref, dbi_ref, dcr_ref, dci_ref, dd_ref):
                r[...] = jnp.zeros_like(r)

        yv = y_ref[...]
        ut = u_ref[...].astype(F32)
        cdf =0.5 * (1.0 + lax.erf(yv * (1.0 / math.sqrt(2.0))))
        pdf = jnp.exp(-0.5 * yv * yv) * (1.0 / math.sqrt(2.0 * math.pi))
        dyt = dge_ref[...].astype(F32) * (cdf + yv * pdf)
        dd_ref[...] += _colsum(dyt * ut)
        for q in range(nq):
            dys[q] = dyt[:, 128 * q:128 * (q + 1)]
            us[q] = ut[:, 128 * q:128 * (q + 1)]
        _to_segments(dyp, [dys.at[q] for q in range(nq)], jn)
        _to_segments(up, [us.at[q] for q in range(nq)], jn)
        dy = dyp[...]
        u = up[...]
        dyb = dy.astype(BF16)
        ubf = u.astype(BF16)
        for q in range(SSM_BLOCKS):
            dq = dyb[:, q * ub:(q + 1) * ub]
            a_re[:, q * sb:(q + 1) * sb] = jnp.dot(dq, ctr_ref[q], preferred_element_type=F32)
            a_im[:, q * sb:(q + 1) * sb] = -jnp.dot(dq, cti_ref[q], preferred_element_type=F32)
        a1r, a1i = _pow_rows(lrdt_v, ang_v, 1.0)
        ajr, aji = _pow_rows(lrdt_v, ang_v, float(jn))
        for q in range(0, SSM_BLOCKS, 2):
            cols = slice(q * sb, (q + 2) * sb)
            ar8 = jnp.broadcast_to(a1r[:, cols], (8, 2 * sb))
            ai8 = jnp.broadcast_to(a1i[:, cols], (8, 2 * sb))

            def step(jj, carry, cols=cols, ar8=ar8, ai8=ai8):
                sr, si = carry
                rows = pl.ds(pl.multiple_of((jn - 2 - jj) * 8, 8), 8)
                mr, mi = _cmul_conj(ar8, ai8, sr, si)
                nr, ni = mr + a_re[rows, cols], mi + a_im[rows, cols]
                a_re[rows, cols] = nr
                a_im[rows, cols] = ni
                return nr, ni

            lax.fori_loop(0, jn - 1, step, (a_re[lc - 8:lc, cols], a_im[lc - 8:lc, cols]), unroll=4)
        er, ei = a_re[0:8, :], a_im[0:8, :]
        hr, hi = car_re[...], car_im[...]
        for s in range(7, -1, -1):
            cst_re[s:s + 1, :] = hr
            cst_im[s:s + 1, :] = hi
            mr, mi = _cmul_conj(ajr, aji, hr, hi)
            hr, hi = mr + er[s:s + 1, :], mi + ei[s:s + 1, :]
        car_re[...] = hr
        car_im[...] = hi
        first = (i == nb - 1).astype(F32)
        sp_re[0:tail, :] = tre_ref[...].astype(F32) * (1.0 - first)
        sp_im[0:tail, :] = tim_ref[...].astype(F32) * (1.0 - first)
        sp_re[tail:tail + 8, :] = sre_ref[lc - tail:lc, :].astype(F32)[tail - 8:tail]
        sp_im[tail:tail + 8, :] = sim_ref[lc - tail:lc, :].astype(F32)[tail - 8:tail]
        tn_dims = (((0,), (0,)), ((), ()))
        for q in range(SSM_BLOCKS):
            cols = slice(q * sb, (q + 1) * sb)
            ycols = slice(q * ub, (q + 1) * ub)
            cr = jnp.tile(cst_re[:, cols], (jn, 1))
            ci = jnp.tile(cst_im[:, cols], (jn, 1))
            mr, mi = _cmul_conj(q_re[:, cols], q_im[:, cols], cr, ci)
            lam_r, lam_i = a_re[:, cols] + mr, a_im[:, cols] + mi
            s_r, s_i = sre_ref[:, cols], sim_ref[:, cols]
            p0r, p0i = sp_re[tail - 1:tail + 7, cols], sp_im[tail - 1:tail + 7, cols]
            l0r, l0i, l1r, l1i = lam_r[0:8], lam_i[0:8], lam_r[8:lc], lam_i[8:lc]
            pvr, pvi = s_r.astype(F32)[0:lc - 8], s_i.astype(F32)[0:lc - 8]
            dar_ref[:, cols] += _colsum(l1r * pvr + l1i * pvi) + _colsum(l0r * p0r + l0i * p0i)
            dai_ref[:, cols] += _colsum(l1i * pvr - l1r * pvi) + _colsum(l0i * p0r - l0r * p0i)
            lrb, lib = lam_r.astype(BF16), lam_i.astype(BF16)
            dup[q] = (jnp.dot(lrb, btr_ref[q], preferred_element_type=F32)
                      + jnp.dot(lib, bti_ref[q], preferred_element_type=F32) + d_ref[:, ycols] * dy[:, ycols])
            _from_segments(du_ref, dup, q, jn, BF16)
            uq = ubf[:, ycols]
            dbr_ref[q] += lax.dot_general(uq, lrb, tn_dims, preferred_element_type=F32)
            dbi_ref[q] += lax.dot_general(uq, lib, tn_dims, preferred_element_type=F32)
            dq = dyb[:, ycols]
            dcr_ref[q] += lax.dot_general(s_r.astype(BF16), dq, tn_dims, preferred_element_type=F32)
            dci_ref[q] -= lax.dot_general(s_i.astype(BF16), dq, tn_dims, preferred_element_type=F32)

    rev = lambda i: (nb - 1 - i, 0)
    tailmap = lambda i: (jnp.maximum((nb - 1 - i) * (lc // tail) - 1, 0), 0)
    row = lambda i: (0, 0)
    blk3 = lambda i: (0, 0, 0)
    return _pcall(
        body, name=name, grid=(nb,),
        in_specs=[pl.BlockSpec((lc, sw), rev), pl.BlockSpec((lc, sw), rev), pl.BlockSpec((lc, sw), lambda i: (nb - 1 - i, 1)),
                  pl.BlockSpec((lc, gn), rev), pl.BlockSpec((lc, gn), rev),
                  pl.BlockSpec((tail, gn), tailmap), pl.BlockSpec((tail, gn), tailmap),
                  pl.BlockSpec((1, gn), row), pl.BlockSpec((1, gn), row),
                  pl.BlockSpec(bbt_re.shape, blk3), pl.BlockSpec(bbt_im.shape, blk3),
                  pl.BlockSpec(cct_re.shape, blk3), pl.BlockSpec(cct_im.shape, blk3), pl.BlockSpec((1, sw), row), ANY_SPEC],
        out_specs=[pl.BlockSpec((lc, sw), lambda i: (nb - 1 - i, 1)), pl.BlockSpec((1, gn), row), pl.BlockSpec((1, gn), row),
                   pl.BlockSpec((SSM_BLOCKS, ub, sb), blk3), pl.BlockSpec((SSM_BLOCKS, ub, sb), blk3),
                   pl.BlockSpec((SSM_BLOCKS, sb, ub), blk3), pl.BlockSpec((SSM_BLOCKS, sb, ub), blk3),
                   pl.BlockSpec((1, sw), row)],
        out_shape=[jax.ShapeDtypeStruct(duin.shape, BF16), jax.ShapeDtypeStruct((1, gn), F32), jax.ShapeDtypeStruct((1, gn), F32),
                   jax.ShapeDtypeStruct((SSM_BLOCKS, ub, sb), F32), jax.ShapeDtypeStruct((SSM_BLOCKS, ub, sb), F32),
                   jax.ShapeDtypeStruct((SSM_BLOCKS, sb, ub), F32), jax.ShapeDtypeStruct((SSM_BLOCKS, sb, ub), F32),
                   jax.ShapeDtypeStruct((1, sw), F32)],
        scratch_shapes=[pltpu.VMEM((lc, gn), F32), pltpu.VMEM((lc, gn), F32), pltpu.VMEM((lc, gn), F32), pltpu.VMEM((lc, gn), F32),
                        pltpu.VMEM((lc, sw), F32), pltpu.VMEM((lc, sw), F32),
                        pltpu.VMEM((nq, lc, 128), F32), pltpu.VMEM((nq, lc, 128), F32), pltpu.VMEM((nq, lc, 128), F32),
                        pltpu.VMEM((8, gn), F32), pltpu.VMEM((8, gn), F32),
                        pltpu.VMEM((tail + 8, gn), F32), pltpu.VMEM((tail + 8, gn), F32),
                        pltpu.VMEM((1, gn), F32), pltpu.VMEM((1, gn), F32)],
        input_output_aliases={14: 0},
        compiler_params=_params("arbitrary"))(dge, y, uin, s_re, s_im, s_re, s_im, lrdt, ang,
                                               bbt_re, bbt_im, cct_re, cct_im, d_skip, duin)


def _blockdiag_b(bb, sw):
    gpb = (sw // SSM_GROUP) // SSM_BLOCKS
    b4 = bb.reshape(SSM_BLOCKS, gpb, SSM_STATE, SSM_GROUP)
    eye = jnp.eye(gpb, dtype=bb.dtype)
    out = jnp.einsum('qgnh,gk->qghkn', b4, eye)
    return out.reshape(SSM_BLOCKS, gpb * SSM_GROUP, gpb * SSM_STATE)


def _blockdiag_c(cc, sw):
    gpb = (sw // SSM_GROUP) // SSM_BLOCKS
    c4 = cc.reshape(SSM_BLOCKS, gpb, SSM_GROUP, SSM_STATE)
    eye = jnp.eye(gpb, dtype=cc.dtype)
    out = jnp.einsum('qghn,gk->qgnkh', c4, eye)
    return out.reshape(SSM_BLOCKS, gpb * SSM_STATE, gpb * SSM_GROUP)


def _diag_of_b(dbb, sw):
    gpb = (sw // SSM_GROUP) // SSM_BLOCKS
    d5 = dbb.reshape(SSM_BLOCKS, gpb, SSM_GROUP, gpb, SSM_STATE)
    return jnp.einsum('qghgn->qgnh', d5).reshape(SSM_BLOCKS * gpb * SSM_STATE, SSM_GROUP)


def _diag_of_c(dcc, sw):
    gpb = (sw // SSM_GROUP) // SSM_BLOCKS
    d5 = dcc.reshape(SSM_BLOCKS, gpb, SSM_STATE, gpb, SSM_GROUP)
    return jnp.einsum('qgngh->qghn', d5).reshape(SSM_BLOCKS * gpb, SSM_GROUP, SSM_STATE)


def _ada_fwd(name, c_all, w_sh, b_sh):
    nb, d = c_all.shape
    ncol = w_sh.shape[1]
    tn = _pick(ncol, 768)

    def body(c_ref, w_ref, b_ref, o_ref):
        cv = c_ref[...]
        sil = cv * _sigmoid(cv)
        o_ref[...] = jnp.dot(sil, w_ref[...], preferred_element_type=F32, precision=lax.Precision.HIGHEST) + b_ref[...]

    return _pcall(body, name=name, grid=(ncol // tn,),
                  in_specs=[pl.BlockSpec((nb, d), lambda j: (0, 0)), pl.BlockSpec((d, tn), lambda j: (0, j)),
                            pl.BlockSpec((1, tn), lambda j: (0, j))],
                  out_specs=pl.BlockSpec((nb, tn), lambda j: (0, j)),
                  out_shape=jax.ShapeDtypeStruct((nb, ncol), F32), compiler_params=_params("parallel"))(c_all, w_sh, b_sh)


def _ada_bwd(name, c_all, dmod_sh):
    nb, d = c_all.shape
    ncol = dmod_sh.shape[1]
    tn = _pick(ncol, 768)

    def body(c_ref, g_ref, o_ref):
        cv = c_ref[...]
        sil = cv * _sigmoid(cv)
        o_ref[...] = lax.dot_general(sil, g_ref[...], (((0,), (0,)), ((), ())), preferred_element_type=F32,
                                     precision=lax.Precision.HIGHEST)

    return _pcall(body, name=name, grid=(ncol // tn,),
                  in_specs=[pl.BlockSpec((nb, d), lambda j: (0, 0)), pl.BlockSpec((nb, tn), lambda j: (0, j))],
                  out_specs=pl.BlockSpec((d, tn), lambda j: (0, j)),
                  out_shape=jax.ShapeDtypeStruct((d, ncol), F32), compiler_params=_params("parallel"))(c_all, dmod_sh)


def _place():
    return lax.axis_index("x"), lax.axis_index("y"), lax.axis_index("c")


def _allgather_small(name, blk, deps=()):
    m_per, n = blk.shape

    def body(x_ref, *rest):
        out_ref, send_sems, recv_sems, local_sem = rest[len(deps):]
        x, y, c = _place()
        me, sibling = (x, y, c), (x, y, 1 - c)
        chips = [(1 - x, y), (x, 1 - y), (1 - x, 1 - y)]

        def rows(px, py, pc):
            return out_ref.at[pl.ds((4 * px + 2 * py + pc) * m_per, m_per), :]

        def copy(k, block, to, src=None):
            return pltpu.make_async_remote_copy(
                src_ref=rows(*block) if src is None else src, dst_ref=rows(*block),
                send_sem=send_sems.at[k], recv_sem=recv_sems.at[k], device_id=to, device_id_type=MESH)

        mine = pltpu.make_async_copy(x_ref, rows(*me), local_sem)
        mine.start()
        first = [copy(0, me, sibling, src=x_ref)]
        first += [copy(1 + j, me, (*chip, c), src=x_ref) for j, chip in enumerate(chips)]
        for cp in first:
            cp.start()
        passed = [copy(4 + j, (*chip, c), sibling) for j, chip in enumerate(chips)]
        for j, chip in enumerate(chips):
            copy(1 + j, (*chip, c), me).wait_recv()
            passed[j].start()
        copy(0, sibling, me).wait_recv()
        for j, chip in enumerate(chips):
            copy(4 + j, (*chip, 1 - c), me).wait_recv()
        for cp in first + passed:
            cp.wait_send()
        mine.wait()

    return _pcall(body, name=name, out_shape=jax.ShapeDtypeStruct((N_DEV * m_per, n), blk.dtype),
                  in_specs=[pl.BlockSpec(memory_space=pltpu.VMEM)] + [ANY_SPEC] * len(deps),
                  out_specs=pl.BlockSpec(memory_space=pltpu.VMEM),
                  scratch_shapes=[pltpu.SemaphoreType.DMA((7,)), pltpu.SemaphoreType.DMA((7,)), pltpu.SemaphoreType.DMA],
                  compiler_params=pltpu.CompilerParams(vmem_limit_bytes=VMEM_LIMIT))(blk, *deps)


def _other_chips(x, y):
    return [(1 - x, y), (x, 1 - y), (1 - x, 1 - y)]


def _place_shards(shards, s_me):
    return [lax.dynamic_update_slice(lax.empty((N_CHIPS,) + s.shape, s.dtype), s[None], (s_me, 0, 0)) for s in shards]


def _ag_copy(src, land, send, recv, wi, j, chip, x, y, c, tc, both):
    hr = src.shape[0] // 2
    half = pl.ds(pl.multiple_of(c * hr, 16), hr)
    k = 3 * wi + j
    return pltpu.make_async_remote_copy(
        src_ref=src.at[half, :], dst_ref=land.at[2 * x + y, half, :],
        send_sem=send.at[2 * k + tc if both else k], recv_sem=recv.at[2 * k + c if both else k],
        device_id=(chip[0], chip[1], tc), device_id_type=MESH)


def _ag_targets(c, both):
    return (0, 1) if both else (c,)


def _ag_start(name, shards, lands, groups, direct, deps=()):
    nw, ng, nd = len(shards), len(groups), len(deps)

    def body(*refs):
        src, land = refs[:nw], refs[nw:2 * nw]
        sems = refs[2 * nw + nd:2 * nw + nd + 2 * ng]
        token = refs[-1]
        x, y, c = _place()
        for gi, grp in enumerate(groups):
            for wi, w in enumerate(grp):
                for j, chip in enumerate(_other_chips(x, y)):
                    for tc in _ag_targets(c, direct[gi]):
                        _ag_copy(src[w], land[w], sems[2 * gi], sems[2 * gi + 1], wi, j, chip, x, y, c, tc, direct[gi]).start()
        token[...] = jnp.zeros_like(token)

    sem_shapes = []
    for gi, grp in enumerate(groups):
        sem_shapes += [pltpu.SemaphoreType.DMA(((6 if direct[gi] else 3) * len(grp),))] * 2
    out_shape = sem_shapes + [pltpu.HBM(s.shape, s.dtype) for s in shards] + [pltpu.HBM(l.shape, l.dtype) for l in lands]
    out_shape += [jax.ShapeDtypeStruct((8, 128), F32)]
    res = _pcall(body, name=name, out_shape=out_shape, in_specs=[HBM_SPEC] * (2 * nw) + [ANY_SPEC] * nd,
                 out_specs=[SEM_SPEC] * (2 * ng) + [HBM_SPEC] * (2 * nw) + [pl.BlockSpec(memory_space=pltpu.VMEM)],
                 input_output_aliases={i: 2 * ng + i for i in range(2 * nw)},
                 compiler_params=pltpu.CompilerParams(has_side_effects=EFFECT))(
                     *[_hbm(s) for s in shards], *[_hbm(l) for l in lands], *deps)
    sems = [(res[2 * gi], res[2 * gi + 1]) for gi in range(ng)]
    return sems, list(res[2 * ng:2 * ng + nw]), list(res[2 * ng + nw:2 * ng + 2 * nw]), res[-1]


def _ag_wait(name, shards, lands, send, recv, both, after):
    n = len(shards)

    def body(*refs):
        src, land = refs[:n], refs[n:2 * n]
        send_sem, recv_sem = refs[2 * n], refs[2 * n + 1]
        x, y, c = _place()
        for wi in range(n):
            for j, chip in enumerate(_other_chips(x, y)):
                for tc in _ag_targets(c, both):
                    _ag_copy(src[wi], land[wi], send_sem, recv_sem, wi, j, chip, x, y, c, tc, both).wait_send()
                    _ag_copy(src[wi], land[wi], send_sem, recv_sem, wi, j, chip, chip[0], chip[1], tc, c, both).wait_recv()

    res = _pcall(body, name=name, out_shape=[pltpu.HBM(a.shape, a.dtype) for a in list(shards) + list(lands)],
                 in_specs=[HBM_SPEC] * (2 * n) + [SEM_SPEC, SEM_SPEC] + [ANY_SPEC] * len(after), out_specs=[HBM_SPEC] * (2 * n),
                 input_output_aliases={i: i for i in range(2 * n)},
                 compiler_params=pltpu.CompilerParams(has_side_effects=EFFECT))(*shards, *lands, send, recv, *after)
    return list(res[n:])


def _ag_forward(name, lands):
    n = len(lands)

    def body(*refs):
        out = refs[n:2 * n]
        send, recv = refs[2 * n], refs[2 * n + 1]
        x, y, c = _place()
        sib = (x, y, 1 - c)
        cps = []
        for wi in range(n):
            hr = out[wi].shape[1] // 2
            for j, (cx, cy) in enumerate(_other_chips(x, y)):
                got = out[wi].at[2 * cx + cy, pl.ds(pl.multiple_of(c * hr, 16), hr), :]
                cp = pltpu.make_async_remote_copy(src_ref=got, dst_ref=got, send_sem=send.at[3 * wi + j], recv_sem=recv.at[3 * wi + j],
                                                  device_id=sib, device_id_type=MESH)
                cp.start()
                cps.append(cp)
        for wi in range(n):
            hr = out[wi].shape[1] // 2
            for j, (cx, cy) in enumerate(_other_chips(x, y)):
                got = out[wi].at[2 * cx + cy, pl.ds(pl.multiple_of((1 - c) * hr, 16), hr), :]
                pltpu.make_async_remote_copy(src_ref=got, dst_ref=got, send_sem=send.at[3 * wi + j], recv_sem=recv.at[3 * wi + j],
                                             device_id=sib, device_id_type=MESH).wait_recv()
        for cp in cps:
            cp.wait_send()

    res = _pcall(body, name=name, out_shape=[jax.ShapeDtypeStruct(l.shape, l.dtype) for l in lands],
                 in_specs=[ANY_SPEC] * n, out_specs=[ANY_SPEC] * n, input_output_aliases={i: i for i in range(n)},
                 scratch_shapes=[pltpu.SemaphoreType.DMA((3 * n,)), pltpu.SemaphoreType.DMA((3 * n,))])(*lands)
    return list(res)


def _peers(x, y, c):
    offs = [(dx, dy, dc) for dx in (0, 1) for dy in (0, 1) for dc in (0, 1)][1:]
    return [(1 - x if dx else x, 1 - y if dy else y, 1 - c if dc else c) for dx, dy, dc in offs]


def _rs_copy(g_ref, land_ref, send, recv, wi, k, to, sender):
    hr = g_ref.shape[1] // 2
    return pltpu.make_async_remote_copy(
        src_ref=g_ref.at[2 * to[0] + to[1], pl.ds(pl.multiple_of(to[2] * hr, 16), hr), :], dst_ref=land_ref.at[sender],
        send_sem=send.at[7 * wi + k], recv_sem=recv.at[7 * wi + k], device_id=to, device_id_type=MESH)


def _rs_start(name, gs):
    n = len(gs)
    lands = [lax.empty((N_DEV, g.shape[1] // 2, g.shape[2]), BF16) for g in gs]

    def body(*refs):
        g, land = refs[:n], refs[n:2 * n]
        send, recv = refs[2 * n], refs[2 * n + 1]
        token = refs[-1]
        x, y, c = _place()
        me = 4 * x + 2 * y + c
        for wi in range(n):
            for k, to in enumerate(_peers(x, y, c)):
                _rs_copy(g[wi], land[wi], send, recv, wi, k, to, me).start()
        token[...] = jnp.zeros_like(token)

    out_shape = [pltpu.SemaphoreType.DMA((7 * n,))] * 2 + [pltpu.HBM(a.shape, a.dtype) for a in list(gs) + lands]
    out_shape += [jax.ShapeDtypeStruct((8, 128), F32)]
    res = _pcall(body, name=name, out_shape=out_shape, in_specs=[HBM_SPEC] * (2 * n),
                 out_specs=[SEM_SPEC] * 2 + [HBM_SPEC] * (2 * n) + [pl.BlockSpec(memory_space=pltpu.VMEM)],
                 input_output_aliases={i: 2 + i for i in range(2 * n)},
                 compiler_params=pltpu.CompilerParams(has_side_effects=EFFECT))(
                     *[_hbm(a) for a in gs], *[_hbm(a) for a in lands])
    return res[0], res[1], list(res[2:2 + n]), list(res[2 + n:2 + 2 * n]), res[-1]


def _rs_wait(name, gs, lands, send, recv, after):
    n = len(gs)

    def body(*refs):
        g, land = refs[:n], refs[n:2 * n]
        send_sem, recv_sem = refs[2 * n], refs[2 * n + 1]
        x, y, c = _place()
        me = 4 * x + 2 * y + c
        for wi in range(n):
            for k, to in enumerate(_peers(x, y, c)):
                _rs_copy(g[wi], land[wi], send_sem, recv_sem, wi, k, to, me).wait_send()
                _rs_copy(g[wi], land[wi], send_sem, recv_sem, wi, k, (x, y, c), 4 * to[0] + 2 * to[1] + to[2]).wait_recv()

    res = _pcall(body, name=name, out_shape=[pltpu.HBM(a.shape, a.dtype) for a in list(gs) + list(lands)],
                 in_specs=[HBM_SPEC] * (2 * n) + [SEM_SPEC, SEM_SPEC, ANY_SPEC], out_specs=[HBM_SPEC] * (2 * n),
                 input_output_aliases={i: i for i in range(2 * n)},
                 compiler_params=pltpu.CompilerParams(has_side_effects=EFFECT))(*gs, *lands, send, recv, after)
    return list(res[:n]), list(res[n:])


def _bc_copy(blk_ref, land_ref, send, recv, k, to, slot):
    return pltpu.make_async_remote_copy(src_ref=blk_ref, dst_ref=land_ref.at[slot], send_sem=send.at[k], recv_sem=recv.at[k],
                                        device_id=to, device_id_type=MESH)


def _bcast_start(name, blk):
    land = lax.empty((N_DEV,) + blk.shape, blk.dtype)

    def body(blk_ref, land_ref, send, recv, blk_thru, land_thru, token):
        x, y, c = _place()
        for k, to in enumerate(_peers(x, y, c)):
            _bc_copy(blk_ref, land_ref, send, recv, k, to, 4 * x + 2 * y + c).start()
        token[...] = jnp.zeros_like(token)

    return _pcall(body, name=name,
                  out_shape=[pltpu.SemaphoreType.DMA((7,)), pltpu.SemaphoreType.DMA((7,)), pltpu.HBM(blk.shape, blk.dtype),
                             pltpu.HBM(land.shape, land.dtype), jax.ShapeDtypeStruct((8, 128), F32)],
                  in_specs=[HBM_SPEC, HBM_SPEC], out_specs=[SEM_SPEC, SEM_SPEC, HBM_SPEC, HBM_SPEC, pl.BlockSpec(memory_space=pltpu.VMEM)],
                  input_output_aliases={0: 2, 1: 3}, compiler_params=pltpu.CompilerParams(has_side_effects=EFFECT))(_hbm(blk), _hbm(land))


def _bcast_wait(name, blk, land, send, recv, after):
    def body(blk_ref, land_ref, send_sem, recv_sem, after_ref, blk_thru, land_thru):
        x, y, c = _place()
        for k, to in enumerate(_peers(x, y, c)):
            _bc_copy(blk_ref, land_ref, send_sem, recv_sem, k, to, 4 * x + 2 * y + c).wait_send()
            _bc_copy(blk_ref, land_ref, send_sem, recv_sem, k, to, 4 * to[0] + 2 * to[1] + to[2]).wait_recv()

    return _pcall(body, name=name, out_shape=[pltpu.HBM(blk.shape, blk.dtype), pltpu.HBM(land.shape, land.dtype)],
                  in_specs=[HBM_SPEC, HBM_SPEC, SEM_SPEC, SEM_SPEC, ANY_SPEC], out_specs=[HBM_SPEC, HBM_SPEC],
                  input_output_aliases={0: 0, 1: 1}, compiler_params=pltpu.CompilerParams(has_side_effects=EFFECT))(
                      blk, land, send, recv, after)[1]


def _rs_sum(name, gs, lands):
    n = len(gs)

    def body(*refs):
        g_refs, land_refs, out_refs = refs[:n], refs[n:2 * n], refs[2 * n:3 * n]
        recvs = refs[3 * n:4 * n]
        local_sems, sib_send, sib_recv = refs[4 * n:]
        x, y, c = _place()
        me = 4 * x + 2 * y + c
        cps = []
        for wi in range(n):
            hr = g_refs[wi].shape[1] // 2
            own = g_refs[wi].at[2 * x + y, pl.ds(pl.multiple_of(c * hr, 16), hr), :]
            cps.append(pltpu.make_async_copy(own, recvs[wi].at[me], local_sems.at[8 * wi + 7]))
            for k, (tx, ty, tc) in enumerate(_peers(x, y, c)):
                slot = 4 * tx + 2 * ty + tc
                cps.append(pltpu.make_async_copy(land_refs[wi].at[slot], recvs[wi].at[slot], local_sems.at[8 * wi + k]))
        for cp in cps:
            cp.start()
        sibs = []
        for wi in range(n):
            hr = g_refs[wi].shape[1] // 2
            ch = _pick(hr, 64, 16)
            for cp in cps[8 * wi:8 * wi + 8]:
                cp.wait()
            base = pl.multiple_of(c * hr, 16)
            for r0 in range(0, hr, ch):
                acc = recvs[wi][0, r0:r0 + ch, :].astype(F32)
                for k in range(1, N_DEV):
                    acc = acc + recvs[wi][k, r0:r0 + ch, :].astype(F32)
                out_refs[wi][pl.ds(base + r0, ch), :] = acc
            half = out_refs[wi].at[pl.ds(base, hr), :]
            sib = pltpu.make_async_remote_copy(src_ref=half, dst_ref=half, send_sem=sib_send.at[wi], recv_sem=sib_recv.at[wi],
                                               device_id=(x, y, 1 - c), device_id_type=MESH)
            sib.start()
            sibs.append(sib)
        for wi in range(n):
            hr = g_refs[wi].shape[1] // 2
            other = out_refs[wi].at[pl.ds(pl.multiple_of((1 - c) * hr, 16), hr), :]
            pltpu.make_async_remote_copy(src_ref=other, dst_ref=other, send_sem=sib_send.at[wi], recv_sem=sib_recv.at[wi],
                                         device_id=(x, y, 1 - c), device_id_type=MESH).wait_recv()
            sibs[wi].wait_send()

    res = _pcall(
        body, name=name, out_shape=[jax.ShapeDtypeStruct(g.shape[1:], F32) for g in gs],
        in_specs=[ANY_SPEC] * (2 * n), out_specs=[pl.BlockSpec(memory_space=pltpu.VMEM)] * n,
        scratch_shapes=[pltpu.VMEM((N_DEV, g.shape[1] // 2, g.shape[2]), BF16) for g in gs]
        + [pltpu.SemaphoreType.DMA((8 * n,)), pltpu.SemaphoreType.DMA((n,)), pltpu.SemaphoreType.DMA((n,))],
        compiler_params=pltpu.CompilerParams(vmem_limit_bytes=VMEM_LIMIT))(*gs, *lands)
    return list(res)


def _gate_norm_epilogue(factor, x_in, gt, nxt, t, d):
    blk = (ROWS, d)
    full = lambda i, j: (i, j)
    rowv = lambda i, j: (0, j)

    def epi(acc, res, scale, *norm):
        x_new = res + (factor * scale) * acc
        if not norm:
            return x_new, acc
        gv, scv, shv = norm
        r = lax.rsqrt(jnp.mean(x_new * x_new, axis=-1, keepdims=True) + EPS)
        return x_new, acc, x_new * r * gv * (1.0 + scv) + shv

    ins = [(x_in, blk, full), (gt, (1, d), rowv)] + [(v, (1, d), rowv) for v in (nxt or ())]
    outs = [((t, d), F32, blk, full), ((t, d), BF16, blk, full)] + ([((t, d), BF16, blk, full)] if nxt else [])
    return epi, ins, outs


def _glu_up_gates(name, ge, w_glu, b_glu, w_up, y_pool, uin, d, pw, tm=ROWS):
    t, sw = ge.shape
    tm = min(tm, t)
    ns = w_glu.shape[2]
    cb = (2 * pw) // d

    def body(ge_ref, wg_ref, b_ref, wu_ref, yp_ref, glp_ref, gls_ref, gv_ref, sg_ref, ys_ref, mg_ref):
        gev = ge_ref[...]
        gv = jnp.concatenate([jnp.dot(gev, wg_ref[s], preferred_element_type=F32) for s in range(N_CHIPS)], axis=1)
        gv_ref[...] = gv.astype(BF16)
        g = gv + b_ref[...]
        sg = (g[:, :sw] * _sigmoid(g[:, sw:])).astype(BF16)
        sg_ref[...] = sg
        ys = jnp.concatenate([jnp.dot(sg, wu_ref[s], preferred_element_type=F32) for s in range(N_CHIPS)], axis=1)
        ys_ref[...] = ys.astype(BF16)
        mg_ref[...] = (_sigmoid(glp_ref[...].astype(F32)) * yp_ref[...].astype(F32)
                       + _sigmoid(gls_ref[...].astype(F32)) * ys).astype(BF16)

    full3 = lambda i: (0, 0, 0)
    rows = lambda w: pl.BlockSpec((tm, w), lambda i: (i, 0))
    return _pcall(
        body, name=name, grid=(t // tm,),
        in_specs=[rows(sw), pl.BlockSpec(w_glu.shape, full3), pl.BlockSpec(b_glu.shape, lambda i: (0, 0)),
                  pl.BlockSpec(w_up.shape, full3), rows(d), pl.BlockSpec((tm, d), lambda i: (i, cb)),
                  pl.BlockSpec((tm, d), lambda i: (i, cb + 1))],
        out_specs=[rows(N_CHIPS * ns), rows(sw), rows(d), rows(d)],
        out_shape=[jax.ShapeDtypeStruct((t, N_CHIPS * ns), BF16), jax.ShapeDtypeStruct((t, sw), BF16),
                   jax.ShapeDtypeStruct((t, d), BF16), jax.ShapeDtypeStruct((t, d), BF16)],
        compiler_params=_params("parallel"))(ge, w_glu, b_glu, w_up, y_pool, uin, uin)


def _glu_up_bwd(name, dy, w_up, gv, b_glu, w_glu, sg, ge, tm=ROWS, deps=()):
    t, sw = ge.shape
    tm = min(tm, t)
    nb = t // tm
    ns = w_up.shape[2]
    nt = (((1,), (1,)), ((), ()))
    tn_ = (((0,), (0,)), ((), ()))

    def body(dy_ref, wu_ref, gv_ref, b_ref, wg_ref, sg_ref, ge_ref, *rest):
        dge_ref, db_ref, dwu_ref, dwg_ref, dwu_acc, dwg_acc = rest[len(deps):]
        i = pl.program_id(0)

        @pl.when(i == 0)
        def _():
            db_ref[...] = jnp.zeros_like(db_ref)
            dwu_acc[...] = jnp.zeros_like(dwu_acc)
            dwg_acc[...] = jnp.zeros_like(dwg_acc)

        dyv = dy_ref[...]
        dsg = None
        for s in range(N_CHIPS):
            q = lax.dot_general(dyv[:, s * ns:(s + 1) * ns], wu_ref[s], nt, preferred_element_type=F32)
            dsg = q if dsg is None else dsg + q
        g = gv_ref[...].astype(F32) + b_ref[...]
        val, sgm = g[:, :sw], _sigmoid(g[:, sw:])
        dval, dgate = dsg * sgm, dsg * val * sgm * (1.0 - sgm)
        db_ref[...] += jnp.concatenate([_colsum(dval), _colsum(dgate)], axis=1)
        dgv = jnp.concatenate([dval, dgate], axis=1).astype(BF16)
        dge = None
        for s in range(N_CHIPS):
            q = lax.dot_general(dgv[:, s * ns:(s + 1) * ns], wg_ref[s], nt, preferred_element_type=F32)
            dge = q if dge is None else dge + q
        dge_ref[...] = dge.astype(BF16)
        dwu_acc[...] += lax.dot_general(sg_ref[...], dyv, tn_, preferred_element_type=F32)
        dwg_acc[...] += lax.dot_general(ge_ref[...], dgv, tn_, preferred_element_type=F32)

        @pl.when(i == nb - 1)
        def _():
            for s in range(N_CHIPS):
                dwu_ref[s] = dwu_acc[:, s * ns:(s + 1) * ns].astype(BF16)
                dwg_ref[s] = dwg_acc[:, s * ns:(s + 1) * ns].astype(BF16)

    full3 = lambda i: (0, 0, 0)
    rows = lambda w: pl.BlockSpec((tm, w), lambda i: (i, 0))
    wide = N_CHIPS * ns
    return _pcall(
        body, name=name, grid=(nb,),
        in_specs=[rows(wide), pl.BlockSpec(w_up.shape, full3), rows(wide), pl.BlockSpec(b_glu.shape, lambda i: (0, 0)),
                  pl.BlockSpec(w_glu.shape, full3), rows(sw), rows(sw)] + [ANY_SPEC] * len(deps),
        out_specs=[rows(sw), pl.BlockSpec((1, wide), lambda i: (0, 0)), pl.BlockSpec(w_up.shape, full3),
                   pl.BlockSpec(w_glu.shape, full3)],
        out_shape=[jax.ShapeDtypeStruct((t, sw), BF16), jax.ShapeDtypeStruct((1, wide), F32),
                   jax.ShapeDtypeStruct(w_up.shape, BF16), jax.ShapeDtypeStruct(w_glu.shape, BF16)],
        scratch_shapes=[pltpu.VMEM((sw, wide), F32), pltpu.VMEM((sw, wide), F32)],
        compiler_params=_params("arbitrary"))(dy, w_up, gv, b_glu, w_glu, sg, ge, *deps)


_FULL = lambda i, j: (i, j)
_ROWV = lambda i, j: (0, j)


def _gates_bwd_epilogue(y_pool, y_ssm, uin, t, d, pw):
    cb = (2 * pw) // d

    def epi(dm, yp, ys, glp, gls):
        sp, ss = _sigmoid(glp.astype(F32)), _sigmoid(gls.astype(F32))
        dgl = jnp.concatenate([dm * yp.astype(F32) * sp * (1.0 - sp), dm * ys.astype(F32) * ss * (1.0 - ss)], axis=1)
        return dm * sp, dm * ss, ("at", 2 * pw, dgl)

    ins = [(y_pool, (ROWS, d), _FULL), (y_ssm, (ROWS, d), _FULL),
           (uin, (ROWS, d), lambda i, j: (i, cb)), (uin, (ROWS, d), lambda i, j: (i, cb + 1))]
    wide = 2 * pw + 2 * d
    return epi, ins, [((t, d), BF16, (ROWS, d), _FULL)] * 2 + [((t, wide), BF16, (ROWS, wide), _FULL)]


def _loss_epilogue(x_in, gt, tgt, g_final, t, d):
    blk = (ROWS, d)
    full = lambda i, j: (i, j)
    rowv = lambda i, j: (0, j)

    def epi(acc, res, scale, tv, gv):
        xv = res + (0.5 * scale) * acc
        r = lax.rsqrt(jnp.mean(xv * xv, axis=-1, keepdims=True) + EPS)
        xr = xv * r
        e = xr * gv - tv
        loss_row = 0.5 * jnp.mean(e * e, axis=-1, keepdims=True)
        dout = e * (1.0 / d)
        gd = gv * dout
        dx = r * (gd - xr * jnp.mean(gd * xr, axis=-1, keepdims=True))
        fdx = 0.5 * dx
        return [dx, scale * fdx], [_colsum(dout * xr), _colsum(loss_row * jnp.ones((1, 128), F32)), _colsum(fdx * acc)]

    ins = [(x_in, blk, full), (gt, (1, d), rowv), (tgt, blk, full), (g_final, (1, d), rowv)]
    return epi, ins, [((t, d), F32, blk, full), ((t, d), BF16, blk, full)], [d, 128, d]


def _norm_bwd_epilogue(x, dres, g, sc, up, t, d):
    blk = (ROWS, d)
    full = lambda i, j: (i, j)
    rowv = lambda i, j: (0, j)

    def epi(dhv, xv, drv, gv, scv, *rest):
        r = lax.rsqrt(jnp.mean(xv * xv, axis=-1, keepdims=True) + EPS)
        xr = xv * r
        dn = dhv * (1.0 + scv)
        gd = gv * dn
        dx = drv + r * (gd - xr * jnp.mean(gd * xr, axis=-1, keepdims=True))
        outs, reds = [dx], [_colsum(dhv), _colsum(dhv * xr * gv), _colsum(dn * xr)]
        if up:
            yv, gtv = rest
            fdx = up[2] * dx
            outs.append(gtv * fdx)
            reds.append(_colsum(fdx * yv.astype(F32)))
        return outs, reds

    ins = [(x, blk, full), (dres, blk, full), (g, (1, d), rowv), (sc, (1, d), rowv)]
    ins += [(up[0], blk, full), (up[1], (1, d), rowv)] if up else []
    outs = [((t, d), F32, blk, full)] + ([((t, d), BF16, blk, full)] if up else [])
    return epi, ins, outs, [d] * (4 if up else 3)


def _ffn_in_act(name, h, w_in, tm=ROWS):
    t, d = h.shape
    ns = w_in.shape[2]
    tm = _pick(t, tm, 8)
    w4 = w_in.reshape(2, 2, d, ns)

    def body(h_ref, w_ref, ab_ref, act_ref):
        hv = h_ref[...]
        a = jnp.dot(hv, w_ref[0], preferred_element_type=F32)
        b = jnp.dot(hv, w_ref[1], preferred_element_type=F32)
        ab_ref[0] = a.astype(BF16)
        ab_ref[1] = b.astype(BF16)
        act_ref[...] = (a * _sigmoid(a) * b).astype(BF16)

    return _pcall(body, name=name, grid=(2, t // tm),
                  in_specs=[pl.BlockSpec((tm, d), lambda c, i: (i, 0)), pl.BlockSpec((2, None, d, ns), lambda c, i: (0, c, 0, 0))],
                  out_specs=[pl.BlockSpec((2, tm, ns), lambda c, i: (0, i, c)), pl.BlockSpec((tm, ns), lambda c, i: (i, c))],
                  out_shape=[jax.ShapeDtypeStruct((2, t, 2 * ns), BF16), jax.ShapeDtypeStruct((t, 2 * ns), BF16)],
                  compiler_params=_params("parallel", "parallel"))(h, w4)


def _dswiglu_epilogue(ab, t, f, tn):
    blk = (2, ROWS, tn)
    idx = lambda i, j: (0, i, j)

    def epi(dact, abv):
        a, b = abv[0].astype(F32), abv[1].astype(F32)
        s = _sigmoid(a)
        return (jnp.stack([dact * b * (s * (1.0 + a * (1.0 - s))), dact * (a * s)]),)

    return epi, [(ab, blk, idx)], [((2, t, f), BF16, blk, idx)]


def _ffn_fwd(tag, x, h, gt, get_w_in, get_w_out, nxt=None, loss=None):
    t, d = x.shape
    ab, act = _ffn_in_act(tag + "_in", h, get_w_in(h))
    if loss:
        epi, epi_ins, epi_outs, epi_reds = _loss_epilogue(x, gt, *loss, t, d)
        res = _mm(tag + "_out", act, get_w_out(act), tm=ROWS, tn=d, epi=epi, epi_ins=epi_ins, epi_outs=epi_outs,
                  epi_reds=epi_reds, b_resident=True)
        return res, (x, h, ab, act, None)
    epi, epi_ins, epi_outs = _gate_norm_epilogue(0.5, x, gt, nxt, t, d)
    res = _mm(tag + "_out", act, get_w_out(act), tm=ROWS, tn=d, epi=epi, epi_ins=epi_ins, epi_outs=epi_outs, b_resident=True)
    return res[0], res[2], (x, h, ab, act, res[1])


def _ffn_bwd(tag, dx_new, dy, saved, g, sc, w_in, w_out, start_rs, up, deps=()):
    x, h, ab, act, _ = saved
    d = x.shape[1]
    f = act.shape[1]
    dw_out = _mm(tag + "_dwout", act, dy, ta=True, tm=1408, tn=d, tk=2048, deps=deps)
    tok = start_rs("out", dw_out.reshape(N_CHIPS, f // N_CHIPS, d))
    epi, epi_ins, epi_outs = _dswiglu_epilogue(ab, x.shape[0], f, w_in.shape[2])
    dab = _mm(tag + "_dact", dy, w_out, tb=True, tm=ROWS, tn=w_in.shape[2], epi=epi, epi_ins=epi_ins, epi_outs=epi_outs,
              deps=(tok,), j_outer=True)
    dw_in = _mm(tag + "_dwin", h, dab, ta=True, out_stacked=True, b_halves=True, tm=d, tn=1408, tk=2048)
    tok = start_rs("in", dw_in)
    epi, epi_ins, epi_outs, epi_reds = _norm_bwd_epilogue(x, dx_new, g, sc, up, x.shape[0], d)
    res = _mm(tag + "_dh", dab, w_in, tb=True, b_stacked=True, a_halves=True, tn=d, tk=5632, deps=(tok,), epi=epi,
              epi_ins=epi_ins, epi_outs=epi_outs, epi_reds=epi_reds, b_resident=True)
    if up:
        return res
    return res[0], None, res[1], res[2], res[3], None


def _row(v):
    return v.reshape(1, -1)


def _pack(parts):
    cols = []
    for p in parts:
        flat = p.reshape(-1).astype(F32)
        padn = (-flat.shape[0]) % 128
        cols.append(jnp.pad(flat, (0, padn)) if padn else flat)
    flat = jnp.concatenate(cols)
    padn = (-flat.shape[0]) % 1024
    if padn:
        flat = jnp.pad(flat, (0, padn))
    return flat.reshape(-1, 128)


def _unpack(packed, shapes):
    flat = packed.reshape(-1)
    out, off = [], 0
    for s in shapes:
        n = math.prod(s)
        out.append(flat[off:off + n].reshape(s))
        off += n + ((-n) % 128)
    return out


SMALL = ['b_ada', 'g_ffn1', 'g_mix', 'pool_w', 'pool_b', 'pool_scale', 'ssm_lam_re_log', 'ssm_lam_im', 'ssm_log_dt',
         'ssm_b_re', 'ssm_b_im', 'ssm_c_re', 'ssm_c_im', 'ssm_d', 'b_glu', 'g_ffn2', 'g_final']
BIG = ['w_ffn1_in', 'w_ffn1_out', 'w_in', 'w_pool_up', 'w_glu', 'w_ssm_up', 'w_out', 'w_ffn2_in', 'w_ffn2_out']
AG_GROUPS = [[0], [1], [2, 3, 4, 5, 6], [7, 8]]
AG_DIRECT = [False, True, False, True]
SMALL_LATE = ['b_ada', 'g_ffn1']
RS_SUM_GROUPS = [[0, 1], [2, 3], [4, 5]]
WEIGHTS = ['w_ada', 'b_ada', 'g_ffn1', 'w_ffn1_in', 'w_ffn1_out', 'g_mix', 'w_in', 'pool_w', 'pool_b', 'pool_scale', 'w_pool_up',
           'ssm_lam_re_log', 'ssm_lam_im', 'ssm_log_dt', 'ssm_b_re', 'ssm_b_im', 'ssm_c_re', 'ssm_c_im', 'ssm_d', 'w_glu', 'b_glu',
           'w_ssm_up', 'w_out', 'g_ffn2', 'w_ffn2_in', 'w_ffn2_out', 'g_final']


def kernel(x, c, w_ada, b_ada, g_ffn1, w_ffn1_in, w_ffn1_out, g_mix, w_in, pool_w, pool_b, pool_scale, w_pool_up, ssm_lam_re_log, ssm_lam_im, ssm_log_dt, ssm_b_re, ssm_b_im, ssm_c_re, ssm_c_im, ssm_d, w_glu, b_glu, w_ssm_up, w_out, g_ffn2, w_ffn2_in, w_ffn2_out, g_final, loss_target, m_w_ada, m_b_ada, m_g_ffn1, m_w_ffn1_in, m_w_ffn1_out, m_g_mix, m_w_in, m_pool_w, m_pool_b, m_pool_scale, m_w_pool_up, m_ssm_lam_re_log, m_ssm_lam_im, m_ssm_log_dt, m_ssm_b_re, m_ssm_b_im, m_ssm_c_re, m_ssm_c_im, m_ssm_d, m_w_glu, m_b_glu, m_w_ssm_up, m_w_out, m_g_ffn2, m_w_ffn2_in, m_w_ffn2_out, m_g_final, v_w_ada, v_b_ada, v_g_ffn1, v_w_ffn1_in, v_w_ffn1_out, v_g_mix, v_w_in, v_pool_w, v_pool_b, v_pool_scale, v_w_pool_up, v_ssm_lam_re_log, v_ssm_lam_im, v_ssm_log_dt, v_ssm_b_re, v_ssm_b_im, v_ssm_c_re, v_ssm_c_im, v_ssm_d, v_w_glu, v_b_glu, v_w_ssm_up, v_w_out, v_g_ffn2, v_w_ffn2_in, v_w_ffn2_out, v_g_final):
    args = dict(locals())
    wt = {n: args[n] for n in WEIGHTS}
    mom = {n: args["m_" + n] for n in WEIGHTS}
    var = {n: args["v_" + n] for n in WEIGHTS}

    t, d = x.shape[1], x.shape[2]
    pw = pool_b.shape[1]
    sw = ssm_d.shape[1]
    ngrp = sw // SSM_GROUP
    gn = ngrp * SSM_STATE
    xi, yi, ci = _place()
    b_me = 4 * xi + 2 * yi + ci
    s_me = 2 * xi + yi
    x2d = x[0]
    tgt = loss_target[0]

    c_all = _allgather_small("ag_c", c.reshape(8, d // 8)).reshape(N_DEV, d)
    ncol = w_ada.shape[2]
    b_sh = lax.dynamic_slice(b_ada, (0, s_me * ncol), (1, ncol))
    mod_sh = _ada_fwd("ada_fwd", c_all, w_ada[0], b_sh)
    md_send, md_recv, mod_sh, md_land, tok = _bcast_start("mod_start", mod_sh)

    shards = [wt[n][0].astype(BF16) for n in BIG]
    lands = _place_shards(shards, s_me)
    n0 = len(AG_GROUPS[0])
    sems_a, shards_a, lands_a, tok = _ag_start("ag_start_first", shards[:n0], lands[:n0], AG_GROUPS[:1], AG_DIRECT[:1], deps=(tok,))
    rest = [[w - n0 for w in grp] for grp in AG_GROUPS[1:]]
    sems_b, shards_b, lands_b, tok = _ag_start("ag_start_rest", shards[n0:], lands[n0:], rest, AG_DIRECT[1:], deps=(tok,))
    ag_sems, shards_t, lands_t = sems_a + sems_b, shards_a + shards_b, lands_a + lands_b
    md_land = _bcast_wait("mod_wait", mod_sh, md_land, md_send, md_recv, tok)
    mod_all = lax.dynamic_update_slice(md_land, mod_sh[None], (b_me, 0, 0))
    full = {}

    def weights(gi, *after):
        grp = AG_GROUPS[gi]
        if BIG[grp[0]] not in full:
            ls = _ag_wait("ag_wait%d" % gi, [shards_t[w] for w in grp], [lands_t[w] for w in grp], *ag_sems[gi],
                          AG_DIRECT[gi], after)
            for w, l in zip(grp, ls if AG_DIRECT[gi] else _ag_forward("ag_fwd%d" % gi, ls)):
                full[BIG[w]] = l
        return full

    mod_me = jnp.concatenate([lax.dynamic_slice(mod_all, (2 * s, b_me, 0), (1, 1, ncol))[0] for s in range(N_CHIPS)], axis=1)
    mod = mod_me.reshape(9, d)
    sh1, sc1, gt1, sh2, sc2, gt2, sh3, sc3, gt3 = [mod[k:k + 1] for k in range(9)]

    f = w_ffn1_out.shape[1] * N_CHIPS

    col = lambda a: a.reshape(gn, 1)
    lrl_c, li_c = col(ssm_lam_re_log), col(ssm_lam_im)
    ldt_c = col(jnp.broadcast_to(ssm_log_dt.reshape(ngrp, 1), (ngrp, SSM_STATE)))
    b_re2, b_im2 = ssm_b_re.reshape(gn, SSM_GROUP), ssm_b_im.reshape(gn, SSM_GROUP)
    lrdt_c, ang_c, bb_re, bb_im = _ssm_prep("ssm_prep", lrl_c, li_c, ldt_c, b_re2, b_im2)
    lrdt, ang = lrdt_c.reshape(1, gn), ang_c.reshape(1, gn)
    tr = lambda a: jnp.swapaxes(a, 1, 2)
    bbd = [_blockdiag_b(v, sw).astype(BF16) for v in (bb_re, bb_im)]
    ccd = [_blockdiag_c(v[0], sw).astype(BF16) for v in (ssm_c_re, ssm_c_im)]
    bbd_t, ccd_t = [tr(v) for v in bbd], [tr(v) for v in ccd]
    early = [n for n in SMALL if n not in SMALL_LATE]
    packs = {}
    for tag, names, extra in (("early", early, []), ("late", SMALL_LATE, [jnp.zeros((128,), F32)])):
        packs[tag] = [_pack([src[n] for n in names] + extra) for src in (wt, mom, var)]
    shadow = [lrdt, ang, ldt_c, *bbd, *ccd, *bbd_t, *ccd_t, *packs["early"], *packs["late"]]

    h1 = _norm_fwd("ffn1_norm", x2d, g_ffn1, sc1, sh1, deps=(tok,))
    x1, h2, sav1 = _ffn_fwd("ffn1", x2d, h1, gt1, lambda after: weights(0, after, *shadow)['w_ffn1_in'],
                            lambda after: weights(1, after)['w_ffn1_out'].reshape(f, d), (g_mix, sc2, sh2))
    weights(2, h2)
    wo = full['w_out'].reshape(d, d)
    uin = _mm("mix_in", h2, full['w_in'], b_stacked=True, tm=1024, tn=768, j_outer=True)
    p_pool, z_pool, y_pool = _pool_fwd("pool_fwd", uin, pool_w[0], pool_b, pool_scale, full['w_pool_up'], pw)
    y_s, ge, s_re, s_im = _ssm_fwd("ssm_fwd", uin, lrdt, ang, *bbd, *ccd, ssm_d, sw)
    gv, sg, y_ssm, merged = _glu_up_gates("glu_up_gates", ge, full['w_glu'], b_glu, full['w_ssm_up'], y_pool, uin, d, pw)
    epi, epi_ins, epi_outs = _gate_norm_epilogue(1.0, x1, gt2, (g_ffn2, sc3, sh3), t, d)
    x2, y2, h3 = _mm("mix_out", merged, wo, tm=ROWS, tn=d, epi=epi, epi_ins=epi_ins, epi_outs=epi_outs, b_resident=True)

    (dx3, dy3, dg_final, loss_v, dgt3), sav3 = _ffn_fwd(
        "ffn2", x2, h3, gt3, lambda after: weights(3, after)['w_ffn2_in'],
        lambda after: weights(3, after)['w_ffn2_out'].reshape(f, d), loss=(tgt, _row(g_final)))

    gs = {}
    rs_open = []

    def start_rs(names, gbs):
        send, recv, g_thru, land_thru, token = _rs_start("rs_start_" + names[0], gbs)
        rs_open.append((names, send, recv, g_thru, land_thru))
        return token

    dx2, dy2, dsh3, dsc3, gs['g_ffn2'], dgt2 = _ffn_bwd(
        "ffn2b", dx3, dy3, sav3, g_ffn2, sc3, full['w_ffn2_in'], full['w_ffn2_out'].reshape(f, d),
        lambda key, g: start_rs(['w_ffn2_' + key], [g]), (y2, gt2, 1.0))

    epi, epi_ins, epi_outs = _gates_bwd_epilogue(y_pool, y_ssm, uin, t, d, pw)
    dy_pool, dy_ssm, duin, g_wo = _mm("mix_dmerged", dy2, wo, tb=True, tm=ROWS, tn=d, epi=epi, epi_ins=epi_ins,
                                      epi_outs=epi_outs, b_resident=True, wgrad=merged, wgrad_rows=True)
    g_wo = g_wo.reshape(N_CHIPS, d // N_CHIPS, d)

    duin, gs['pool_w'], gs['pool_b'], gs['pool_scale'], g_wpu = _pool_bwd(
        "pool_bwd", dy_pool, z_pool, p_pool, pool_w[0], pool_b, pool_scale, full['w_pool_up'], duin)

    tok = start_rs(['w_out', 'w_pool_up'], [g_wo, g_wpu])
    dge, gs['b_glu'], g_wsu, g_wglu = _glu_up_bwd("glu_up_bwd", dy_ssm, full['w_ssm_up'], gv, b_glu, full['w_glu'], sg, ge,
                                                  deps=(tok,))
    (duin, g_abre, g_abim, g_bbd_re, g_bbd_im, g_ccd_re, g_ccd_im, gs['ssm_d']) = _ssm_bwd(
        "ssm_bwd", dge, y_s, uin, s_re, s_im, lrdt, ang, *bbd_t, *ccd_t, ssm_d, sw, duin)
    gs['ssm_c_re'], gs['ssm_c_im'] = _diag_of_c(g_ccd_re, sw), _diag_of_c(g_ccd_im, sw)
    d_lrl, d_li, d_ldt, d_bre, d_bim = _ssm_param_bwd(
        "ssm_param_bwd", lrl_c, li_c, ldt_c, b_re2, b_im2, g_abre.reshape(gn, 1), g_abim.reshape(gn, 1),
        _diag_of_b(g_bbd_re, sw), _diag_of_b(g_bbd_im, sw))
    gs['ssm_lam_re_log'], gs['ssm_lam_im'] = d_lrl, d_li
    gs['ssm_log_dt'] = jnp.sum(d_ldt.reshape(ngrp, SSM_STATE), axis=1)
    gs['ssm_b_re'], gs['ssm_b_im'] = d_bre, d_bim

    g_win =_mm("mix_dwin", h2, duin, ta=True, out_stacked=True, tm=d, tn=768, tk=4096)
    tok = start_rs(['w_ssm_up', 'w_glu', 'w_in'], [g_wsu, g_wglu, g_win])
    epi, epi_ins, epi_outs, epi_reds = _norm_bwd_epilogue(x1, dx2, g_mix, sc2, (sav1[4], gt1, 0.5), t, d)
    dx1, dy1, dsh2, dsc2, gs['g_mix'], dgt1 = _mm(
        "mix_dh", duin, full['w_in'], tb=True, b_stacked=True, tn=d, tk=3072, deps=(tok,), epi=epi, epi_ins=epi_ins,
        epi_outs=epi_outs, epi_reds=epi_reds, b_resident=True)

    gs['g_final'] = dg_final
    sg_blk = _pack([gs[n] for n in early])
    sg_send, sg_recv, sg_blk, sg_land, tok = _bcast_start("sg_start", sg_blk)

    dx0, _, dsh1, dsc1, gs['g_ffn1'], _ = _ffn_bwd(
        "ffn1b", dx1, dy1, sav1, g_ffn1, sc1, full['w_ffn1_in'], full['w_ffn1_out'].reshape(f, d),
        lambda key, g: start_rs(['w_ffn1_' + key], [g]), None, deps=(tok,))

    gs['b_ada'] = jnp.concatenate([dsh1, dsc1, dgt1, dsh2, dsc2, dgt2, dsh3, dsc3, dgt3], axis=1)
    lt_blk = _pack([gs[n] for n in SMALL_LATE] + [loss_v])
    lt_send, lt_recv, lt_blk, lt_land, tok = _bcast_start("late_start", lt_blk)

    grads, delta, new_m, new_v = {}, {}, {}, {}

    def small_update(tag, names, blk, land):
        g8 = lax.dynamic_update_slice(land, blk[None], (b_me, 0, 0)).reshape(-1, 128)
        w_pack, m_pack, v_pack = packs[tag]
        shapes = [wt[n].shape for n in names]
        per_param, packed = _adamw_small("adamw_small_" + tag, w_pack, g8, m_pack, v_pack, shapes)
        for k, dst in enumerate((grads, delta, new_m, new_v)):
            rest = _unpack(packed[k], shapes)
            for i, n in enumerate(names):
                dst[n] = per_param[i][k] if per_param[i] is not None else rest[i]
        return g8

    after = tok
    for group in RS_SUM_GROUPS:
        names, g_done, land_done = [], [], []
        for k in group:
            nk, send, recv, g_thru, land_thru = rs_open[k]
            gd, ld = _rs_wait("rs_wait_" + nk[0], g_thru, land_thru, send, recv, after)
            names, g_done, land_done = names + nk, g_done + gd, land_done + ld
        for n, g_sum in zip(names, _rs_sum("rs_sum_" + names[0], g_done, land_done)):
            g_out, dl, mn, vn = _adamw("adamw_" + n, wt[n][0], g_sum, mom[n][0], var[n][0])
            grads[n], delta[n], new_m[n], new_v[n] = g_out[None], dl[None], mn[None], vn[None]
            after = dl
        if group is RS_SUM_GROUPS[-2]:
            small_update("early", early, sg_blk, _bcast_wait("sg_wait", sg_blk, sg_land, sg_send, sg_recv, after))

    lt_land = _bcast_wait("late_wait", lt_blk, lt_land, lt_send, lt_recv, after)
    late8 = small_update("late", SMALL_LATE, lt_blk, lt_land).reshape(N_DEV, -1)
    loss = jnp.sum(late8[:, 10 * d])
    dmod_sh = lax.dynamic_slice(late8, (0, s_me * ncol), (N_DEV, ncol))
    g_w_ada = _ada_bwd("ada_bwd", c_all, dmod_sh)
    dl, mn, vn = _adamw("adamw_w_ada", w_ada[0], g_w_ada, m_w_ada[0], v_w_ada[0], with_g=False)
    grads['w_ada'], delta['w_ada'], new_m['w_ada'], new_v['w_ada'] = g_w_ada[None], dl[None], mn[None], vn[None]

    return (loss, dx0[None], *[grads[n] for n in WEIGHTS], *[delta[n] for n in WEIGHTS],
            *[new_m[n] for n in WEIGHTS], *[new_v[n] for n in WEIGHTS])
```

```python
import functools
import math

import jax
import jax.numpy as jnp
from jax import lax
from jax.experimental import pallas as pl
from jax.experimental.pallas import tpu as pltpu

F32 = jnp.float32
BF16 = jnp.bfloat16
MESH = pl.DeviceIdType.MESH

EPS = 1e-6
POOL_WINDOWS = (2, 4, 8, 16)
POOL_HALO = 16
SSM_GROUP = 16
SSM_STATE = 64
SSM_BLOCKS = 4
N_DEV = 8
N_CHIPS = 4
ADAM_LR = 0.001
ADAM_B1 = 0.9
ADAM_B2 = 0.999
ADAM_EPS = 1e-08
ADAM_WD = 0.01
ADAM_STEP = 10
VMEM_LIMIT = 56 * 1024 * 1024
ROWS = 512


ANY_SPEC = pl.BlockSpec(memory_space=pl.ANY)
HBM_SPEC = pl.BlockSpec(memory_space=pltpu.HBM)
SEM_SPEC = pl.BlockSpec(memory_space=pltpu.SEMAPHORE)
EFFECT = pltpu.SideEffectType.DATAFLOW_SIDE_EFFECTING


def _hbm(a):
    return pltpu.with_memory_space_constraint(a, pltpu.HBM)


def _pcall(body, **kw):
    return pl.pallas_call(body, **kw)


def _params(*sem):
    return pltpu.CompilerParams(dimension_semantics=sem, vmem_limit_bytes=VMEM_LIMIT)


def _pick(n, cap, mult=128):
    if n <= cap:
        return n
    best = None
    for d in range(mult, cap + 1, mult):
        if n % d == 0:
            best = d
    assert best is not None, (n, cap, mult)
    return best


def _sigmoid(v):
    return 1.0 / (1.0 + jnp.exp(-v))


def _rowwise(name, fn, ins, params, outs, reds, tm, deps=()):
    t = ins[0][0].shape[0]
    tm = min(tm, t)
    nb = t // tm
    ni, npar, no, nd = len(ins), len(params), len(outs), len(deps)

    def body(*refs):
        iv = [r[...] for r in refs[:ni]]
        pv = [r[...] for r in refs[ni:ni + npar]]
        o_refs = refs[ni + npar + nd:ni + npar + nd + no]
        r_refs = refs[ni + npar + nd + no:]
        ovals, rvals = fn(iv, pv)
        for o_ref, val in zip(o_refs, ovals):
            off = 0
            if isinstance(val, tuple) and val[0] == "at":
                _, off, val = val
            parts = val if isinstance(val, (list, tuple)) else [val]
            for p in parts:
                o_ref[:, off:off + p.shape[1]] = p.astype(o_ref.dtype)
                off += p.shape[1]
        if r_refs:
            @pl.when(pl.program_id(0) == 0)
            def _():
                for r in r_refs:
                    r[...] = jnp.zeros_like(r)
            for r, val in zip(r_refs, rvals):
                r[...] += val

    in_specs = [pl.BlockSpec((tm, w), functools.partial(lambda i, cb: (i, cb), cb=cb)) for (_, w, cb) in ins]
    in_specs += [pl.BlockSpec(p.shape, lambda i: (0, 0)) for p in params]
    in_specs += [ANY_SPEC] * nd
    out_shape = [jax.ShapeDtypeStruct((t, w), dt) for (w, dt) in outs]
    out_shape += [jax.ShapeDtypeStruct((1, w), F32) for w in reds]
    out_specs = [pl.BlockSpec((tm, w), lambda i: (i, 0)) for (w, _) in outs]
    out_specs += [pl.BlockSpec((1, w), lambda i: (0, 0)) for w in reds]
    res = _pcall(body, name=name, grid=(nb,), in_specs=in_specs, out_specs=out_specs, out_shape=out_shape,
                 compiler_params=_params("arbitrary"))(*[a for a, _, _ in ins], *params, *deps)
    return res


def _colsum(v):
    return jnp.sum(v, axis=0, keepdims=True)


def _mm(name, a, b, *, ta=False, tb=False, b_stacked=False, out_stacked=False, tm=ROWS, tn=1024, tk=2816,
        out_dtype=BF16, epi=None, epi_ins=(), epi_outs=None, epi_reds=(), deps=(), j_outer=False, a_halves=False,
        b_halves=False, b_resident=False, wgrad=None, wgrad_rows=False):
    if a_halves:
        _, m, kdim = a.shape
        kdim *= 2
    elif ta:
        kdim, m = a.shape
    else:
        m, kdim = a.shape
    ns = None
    if b_stacked:
        ns = b.shape[2]
        n = b.shape[1] if tb else N_CHIPS * ns
        assert kdim == (N_CHIPS * ns if tb else b.shape[1]), (name, a.shape, b.shape)
    else:
        if b_halves:
            n = 2 * b.shape[2]
            assert kdim == b.shape[1] and not tb, (name, a.shape, b.shape)
        else:
            n = b.shape[0] if tb else b.shape[1]
            assert kdim == (b.shape[1] if tb else b.shape[0]), (name, a.shape, b.shape)
        if out_stacked:
            ns = n // N_CHIPS

    def shards(want):
        return max(g for g in (1, 2, 4) if g * ns <= max(want, ns))

    tm = _pick(m, tm, 128 if ta else 8)
    gn = gk = 1
    if (b_stacked and not tb) or out_stacked:
        gn = shards(min(tn, n // 2) if b_halves else tn)
        tn = gn * ns
    else:
        tn = _pick(n, tn)
    if b_stacked and tb:
        gk = shards(tk)
        tk = gk * ns
    else:
        tk = _pick(kdim, tk, 8 if ta else 128)
    nm, nn, nk = m // tm, n // tn, kdim // tk

    def ij(f):
        return (lambda g0, g1, k: f(g1, g0, k)) if j_outer else f

    if a_halves:
        assert b_stacked and tb and gk == N_CHIPS and nk == 1, name
        a_spec = pl.BlockSpec((2, tm, kdim // 2), ij(lambda i, j, k: (0, i, 0)))
    elif ta:
        a_spec = pl.BlockSpec((tk, tm), ij(lambda i, j, k: (k, i)))
    else:
        a_spec = pl.BlockSpec((tm, tk), ij(lambda i, j, k: (i, k)))
    if b_stacked and not tb:
        b_spec = pl.BlockSpec((gn, tk, ns), ij(lambda i, j, k: (j, k, 0)))
    elif b_stacked and tb:
        b_spec = pl.BlockSpec((gk, tn, ns), ij(lambda i, j, k: (k, j, 0)))
    elif b_halves:
        bph = (n // 2) // tn
        b_spec = pl.BlockSpec((None, tk, tn), ij(lambda i, j, k: (j // bph, k, j % bph)))
    elif tb:
        b_spec = pl.BlockSpec((tn, tk), ij(lambda i, j, k: (j, k)))
    else:
        b_spec = pl.BlockSpec((tk, tn), ij(lambda i, j, k: (k, j)))
    dims = (((0 if ta else 1,), (1 if tb else 0,)), ((), ()))

    if epi_outs is None:
        if out_stacked:
            epi_outs = [((N_CHIPS, m, ns), out_dtype, (gn, tm, ns), lambda i, j: (j, i, 0))]
        else:
            epi_outs = [((m, n), out_dtype, (tm, tn), lambda i, j: (i, j))]
    ne, no, nd, nr = len(epi_ins), len(epi_outs), len(deps), len(epi_reds)
    nx = 0 if wgrad is None else 1
    assert not (nr or nx) or (nn == 1 and not j_outer), name
    assert not nx or (nk == 1 and not ta and not a_halves and kdim % N_CHIPS == 0), name

    def body(a_ref, b_ref, *rest):
        e_refs = rest[:ne]
        x_refs = rest[ne:ne + nx]
        o_refs = rest[ne + nx + nd:ne + nx + nd + no]
        r_refs = rest[ne + nx + nd + no:ne + nx + nd + no + nr]
        wg_refs = rest[ne + nx + nd + no + nr:ne + nx + nd + no + nr + nx]
        scratch = rest[ne + nx + nd + no + nr + nx:]
        av = None if a_halves else a_ref[...].astype(BF16)
        if nx:
            wg_acc = scratch[-1]
            i = pl.program_id(0)
            pw_ = lax.dot_general(x_refs[0][...].astype(BF16), av, (((0,), (0,)), ((), ())), preferred_element_type=F32)

            @pl.when(i == 0)
            def _():
                wg_acc[...] = pw_

            @pl.when(i > 0)
            def _():
                wg_acc[...] += pw_

            @pl.when(i == nm - 1)
            def _():
                if wgrad_rows:
                    wg_refs[0][...] = wg_acc[...].astype(BF16)
                else:
                    ks = kdim // N_CHIPS
                    for s in range(N_CHIPS):
                        wg_refs[0][s] = wg_acc[:, s * ks:(s + 1) * ks].astype(BF16)
        if b_stacked and not tb:
            parts = [lax.dot_general(av, b_ref[s].astype(BF16), dims, preferred_element_type=F32) for s in range(gn)]
        elif b_stacked and tb:
            p = None
            for s in range(gk):
                if a_halves:
                    a_s = a_ref[s // 2, :, (s % 2) * ns:(s % 2 + 1) * ns].astype(BF16)
                else:
                    a_s = av[:, s * ns:(s + 1) * ns]
                q = lax.dot_general(a_s, b_ref[s].astype(BF16), dims, preferred_element_type=F32)
                p = q if p is None else p + q
            parts = [p]
        else:
            parts = [lax.dot_general(av, b_ref[...].astype(BF16), dims, preferred_element_type=F32)]

        def finish(acc_parts):
            if epi is None and out_stacked:
                acc = acc_parts[0]
                for s in range(gn):
                    o_refs[0][s] = acc[:, s * ns:(s + 1) * ns].astype(out_dtype)
            elif epi is None:
                w = acc_parts[0].shape[1]
                for s, part in enumerate(acc_parts):
                    o_refs[0][:, s * w:(s + 1) * w] = part.astype(out_dtype)
            else:
                acc = acc_parts[0] if len(acc_parts) == 1 else jnp.concatenate(acc_parts, axis=1)
                vals = epi(acc, *[r[...] for r in e_refs])
                if nr:
                    vals, reds = vals

                    @pl.when(pl.program_id(0) == 0)
                    def _():
                        for r in r_refs:
                            r[...] = jnp.zeros_like(r)
                    for r, val in zip(r_refs, reds):
                        r[...] += val
                for o_ref, val in zip(o_refs, vals):
                    if isinstance(val, tuple) and val[0] == "at":
                        o_ref[:, val[1]:val[1] + val[2].shape[1]] = val[2].astype(o_ref.dtype)
                    else:
                        o_ref[...] = val.astype(o_ref.dtype)

        if nk == 1:
            finish(parts)
        else:
            acc_ref = scratch[0]
            k = pl.program_id(2)
            w = parts[0].shape[1]

            @pl.when(k == 0)
            def _():
                for s, part in enumerate(parts):
                    acc_ref[:, s * w:(s + 1) * w] = part

            @pl.when(k > 0)
            def _():
                for s, part in enumerate(parts):
                    acc_ref[:, s * w:(s + 1) * w] += part

            @pl.when(k == nk - 1)
            def _():
                finish([acc_ref[...]])

    def _ij(f):
        return ij(lambda i, j, k: f(i, j))

    if b_resident:
        assert nk == 1 and (nn == 1 or j_outer), name
        b_spec = pl.BlockSpec(b_spec.block_shape, b_spec.index_map, pipeline_mode=pl.Buffered(1))
    in_specs = [a_spec, b_spec] + [pl.BlockSpec(blk, _ij(f)) for (_, blk, f) in epi_ins]
    if nx:
        in_specs.append(pl.BlockSpec((tm, wgrad.shape[1]), lambda i, j, k: (i, 0)))
    in_specs += [ANY_SPEC] * nd
    out_specs = [pl.BlockSpec(blk, _ij(f)) for (_, _, blk, f) in epi_outs]
    out_specs += [pl.BlockSpec((1, w), lambda *_: (0, 0)) for w in epi_reds]
    out_shape = [jax.ShapeDtypeStruct(s, dt) for (s, dt, _, _) in epi_outs] + [jax.ShapeDtypeStruct((1, w), F32) for w in epi_reds]
    scratch = [pltpu.VMEM((tm, tn), F32)] if nk > 1 else []
    if nx:
        wg_shape = (wgrad.shape[1], kdim) if wgrad_rows else (N_CHIPS, wgrad.shape[1], kdim // N_CHIPS)
        out_specs.append(pl.BlockSpec(wg_shape, lambda *_: (0,) * len(wg_shape)))
        out_shape.append(jax.ShapeDtypeStruct(wg_shape, BF16))
        scratch.append(pltpu.VMEM((wgrad.shape[1], kdim), F32))
    grid = (nn, nm, nk) if j_outer else (nm, nn, nk)
    sem = ("arbitrary",) * 3 if (nr or nx) else ("parallel", "parallel", "arbitrary")
    res = _pcall(body, name=name, grid=grid, in_specs=in_specs, out_specs=out_specs, out_shape=out_shape, scratch_shapes=scratch,
                 compiler_params=_params(*sem))(a, b, *[x for x, _, _ in epi_ins], *([wgrad] * nx), *deps)
    return res[0] if len(res) == 1 else res


def _norm_fwd(name, x, g, sc, sh, deps=()):
    d = x.shape[1]

    def fn(iv, pv):
        (xv,), (gv, scv, shv) = iv, pv
        r = lax.rsqrt(jnp.mean(xv * xv, axis=-1, keepdims=True) + EPS)
        return [xv * r * gv * (1.0 + scv) + shv], []

    return _rowwise(name, fn, [(x, d, 0)], [g, sc, sh], [(d, BF16)], [], ROWS, deps=deps)[0]


def _adamw(name, w, g, m, v, tm=256, with_g=True):
    c = w.shape[1]

    def fn(iv, pv):
        wv, gv, mv, vv = iv
        mn = ADAM_B1 * mv + (1.0 - ADAM_B1) * gv
        vn = ADAM_B2 * vv + (1.0 - ADAM_B2) * (gv * gv)
        m_hat = mn / (1.0 - ADAM_B1 ** ADAM_STEP)
        v_hat = vn / (1.0 - ADAM_B2 ** ADAM_STEP)
        delta = -ADAM_LR * (m_hat / (jnp.sqrt(v_hat) + ADAM_EPS) + ADAM_WD * wv)
        return ([gv] if with_g else []) + [delta, mn, vn], []

    return _rowwise(name, fn, [(w, c, 0), (g, c, 0), (m, c, 0), (v, c, 0)], [], [(c, F32)] * (4 if with_g else 3), [],
                    _pick(w.shape[0], tm, 8))


def _unpack_plan(shape):
    n = math.prod(shape)
    if len(shape) == 2 and shape[0] == 1 and n % 128 == 0:
        return [((slice(None), slice(128 * r, 128 * (r + 1))), slice(r, r + 1), slice(None)) for r in range(n // 128)]
    if len(shape) == 2 and shape[0] == 1 and n < 128:
        return [((slice(None), slice(None)), slice(0, 1), slice(0, n))]
    if len(shape) == 1 and n % 128 == 0:
        return [((slice(128 * r, 128 * (r + 1)),), r, slice(None)) for r in range(n // 128)]
    if len(shape) == 3 and shape[0] == 1 and shape[2] == 64:
        return [((0, slice(2 * r + h, 2 * r + h + 1), slice(None)), slice(r, r + 1), slice(64 * h, 64 * (h + 1)))
                for r in range(n // 128) for h in range(2)]
    if len(shape) == 4 and shape[0] == 1 and shape[2:] == (128, 128):
        return [((0, k), slice(128 * k, 128 * (k + 1)), slice(None)) for k in range(shape[1])]
    return None


def _adamw_small(name, w, g8, m, v, shapes):
    r = w.shape[0]
    plans, rows0, off = [], [], 0
    for s in shapes:
        plans.append(_unpack_plan(s))
        rows0.append(off // 128)
        off += math.prod(s) + (-math.prod(s)) % 128
    direct = [i for i, p in enumerate(plans) if p is not None]

    def body(w_ref, g_ref, m_ref, v_ref, *out):
        packed = out[4 * len(direct):]
        gv = g_ref[0:r, :]
        for k in range(1, N_DEV):
            gv = gv + g_ref[k * r:(k + 1) * r, :]
        mn = ADAM_B1 * m_ref[...] + (1.0 - ADAM_B1) * gv
        vn = ADAM_B2 * v_ref[...] + (1.0 - ADAM_B2) * (gv * gv)
        m_hat = mn / (1.0 - ADAM_B1 ** ADAM_STEP)
        v_hat = vn / (1.0 - ADAM_B2 ** ADAM_STEP)
        packed[0][...] = gv
        packed[1][...] = -ADAM_LR * (m_hat / (jnp.sqrt(v_hat) + ADAM_EPS) + ADAM_WD * w_ref[...])
        packed[2][...] = mn
        packed[3][...] = vn
        for di, i in enumerate(direct):
            for kind in range(4):
                o_ref, src = out[4 * di + kind], packed[kind]
                for o_idx, row, lanes in plans[i]:
                    row = (rows0[i] + row) if isinstance(row, int) else slice(rows0[i] + row.start, rows0[i] + row.stop)
                    o_ref[o_idx] = src[row, lanes]

    out_shape = [jax.ShapeDtypeStruct(shapes[i], F32) for i in direct for _ in range(4)]
    out_shape += [jax.ShapeDtypeStruct((r, 128), F32)] * 4
    res = _pcall(body, name=name, out_shape=out_shape,
                 compiler_params=pltpu.CompilerParams(vmem_limit_bytes=VMEM_LIMIT))(w, g8, m, v)
    per_param = [None] * len(shapes)
    for di, i in enumerate(direct):
        per_param[i] = res[4 * di:4 * di + 4]
    return per_param, res[4 * len(direct):]


def _pool_fwd(name, uin, pool_w, pool_b, pool_scale, w_up, pw, tm=ROWS):
    t = uin.shape[0]
    tm = min(tm, t)
    ng = len(POOL_WINDOWS)
    gw = pw // ng
    ns = w_up.shape[2]

    def body(u_ref, w_ref, b_ref, s_ref, wu_ref, p_ref, z_ref, y_ref, ext):
        i = pl.program_id(0)

        @pl.when(i == 0)
        def _():
            ext[0:POOL_HALO, :] = jnp.zeros((POOL_HALO, pw), F32)

        u = u_ref[...].astype(F32)
        ext[POOL_HALO:POOL_HALO + tm, :] = u
        pos = i * tm + lax.broadcasted_iota(jnp.int32, (tm, 1), 0)
        for k, win in enumerate(POOL_WINDOWS):
            cols = slice(k * gw, (k + 1) * gw)
            acc = u[:, cols]
            for j in range(1, win):
                acc = acc + ext[POOL_HALO - j:POOL_HALO - j + tm, cols]
            cnt = jnp.minimum(pos + 1, win).astype(F32)
            z = acc / cnt - u[:, cols]
            zp = jnp.dot(z.astype(BF16), w_ref[k].astype(BF16), preferred_element_type=F32) + b_ref[:, cols]
            p_ref[:, cols] = (zp * s_ref[:, cols]).astype(BF16)
            z_ref[:, cols] = z.astype(BF16)
        ext[0:POOL_HALO, :] = u[tm - POOL_HALO:tm, :]
        pv = p_ref[...]
        for s in range(N_CHIPS):
            y_ref[:, s * ns:(s + 1) * ns] = jnp.dot(pv, wu_ref[s], preferred_element_type=F32).astype(BF16)

    full3 = lambda i: (0, 0, 0)
    return _pcall(
        body, name=name, grid=(t // tm,),
        in_specs=[pl.BlockSpec((tm, pw), lambda i: (i, 0)), pl.BlockSpec(pool_w.shape, full3),
                  pl.BlockSpec(pool_b.shape, lambda i: (0, 0)), pl.BlockSpec(pool_scale.shape, lambda i: (0, 0)),
                  pl.BlockSpec(w_up.shape, full3)],
        out_specs=[pl.BlockSpec((tm, pw), lambda i: (i, 0))] * 2 + [pl.BlockSpec((tm, N_CHIPS * ns), lambda i: (i, 0))],
        out_shape=[jax.ShapeDtypeStruct((t, pw), BF16)] * 2 + [jax.ShapeDtypeStruct((t, N_CHIPS * ns), BF16)],
        scratch_shapes=[pltpu.VMEM((POOL_HALO + tm, pw), F32)],
        compiler_params=_params("arbitrary"))(uin, pool_w, pool_b, pool_scale, w_up)


def _pool_bwd(name, dy, z, p, pool_w, pool_b, pool_scale, w_up, duin, tm=ROWS):
    t, pw = z.shape
    tm = min(tm, t)
    nb = t // tm
    ng = len(POOL_WINDOWS)
    gw = pw // ng
    ns = w_up.shape[2]

    def body(dy_ref, z_ref, p_ref, w_ref, b_ref, s_ref, wu_ref, duin_ref, du_ref, dw_ref, db_ref, ds_ref, dwu_ref, ext, dwu_acc):
        i = pl.program_id(0)

        @pl.when(i == 0)
        def _():
            ext[tm:tm + POOL_HALO, :] = jnp.zeros((POOL_HALO, pw), F32)
            dw_ref[...] = jnp.zeros_like(dw_ref)
            db_ref[...] = jnp.zeros_like(db_ref)
            ds_ref[...] = jnp.zeros_like(ds_ref)
            dwu_acc[...] = jnp.zeros_like(dwu_acc)

        dwu_acc[...] += lax.dot_general(p_ref[...], dy_ref[...], (((0,), (0,)), ((), ())), preferred_element_type=F32)

        @pl.when(i == nb - 1)
        def _():
            for s in range(N_CHIPS):
                dwu_ref[s] = dwu_acc[:, s * ns:(s + 1) * ns].astype(BF16)

        pos = (nb - 1 - i) * tm + lax.broadcasted_iota(jnp.int32, (tm, 1), 0)
        dp = None
        for s in range(N_CHIPS):
            q = lax.dot_general(dy_ref[:, s * ns:(s + 1) * ns], wu_ref[s], (((1,), (1,)), ((), ())), preferred_element_type=F32)
            dp = q if dp is None else dp + q
        for k, win in enumerate(POOL_WINDOWS):
            cols = slice(k * gw, (k + 1) * gw)
            zk = z_ref[:, cols]
            dpk = dp[:, cols]
            wk = w_ref[k].astype(BF16)
            zp = jnp.dot(zk, wk, preferred_element_type=F32) + b_ref[:, cols]
            ds_ref[:, cols] += _colsum(dpk * zp)
            dzp = dpk * s_ref[:, cols]
            db_ref[:, cols] += _colsum(dzp)
            dzpb = dzp.astype(BF16)
            dz = lax.dot_general(dzpb, wk, (((1,), (1,)), ((), ())), preferred_element_type=F32)
            dw_ref[k] += lax.dot_general(zk, dzpb, (((0,), (0,)), ((), ())), preferred_element_type=F32)
            cnt = jnp.minimum(pos + 1, win).astype(F32)
            r = dz / cnt
            ext[0:tm, cols] = r
            acc = r - dz
            for j in range(1, win):
                acc = acc + ext[j:j + tm, cols]
            du_ref[:, cols] = acc.astype(BF16)
        ext[tm:tm + POOL_HALO, :] = ext[0:POOL_HALO, :]

    rev = lambda i: (nb - 1 - i, 0)
    full3 = lambda i: (0, 0, 0)
    return _pcall(
        body, name=name, grid=(nb,),
        in_specs=[pl.BlockSpec((tm, N_CHIPS * ns), rev), pl.BlockSpec((tm, pw), rev), pl.BlockSpec((tm, pw), rev),
                  pl.BlockSpec(pool_w.shape, full3), pl.BlockSpec(pool_b.shape, lambda i: (0, 0)),
                  pl.BlockSpec(pool_scale.shape, lambda i: (0, 0)), pl.BlockSpec(w_up.shape, full3), ANY_SPEC],
        out_specs=[pl.BlockSpec((tm, pw), rev), pl.BlockSpec(pool_w.shape, full3),
                   pl.BlockSpec((1, pw), lambda i: (0, 0)), pl.BlockSpec((1, pw), lambda i: (0, 0)), pl.BlockSpec(w_up.shape, full3)],
        out_shape=[jax.ShapeDtypeStruct(duin.shape, BF16), jax.ShapeDtypeStruct(pool_w.shape, F32),
                   jax.ShapeDtypeStruct((1, pw), F32), jax.ShapeDtypeStruct((1, pw), F32), jax.ShapeDtypeStruct(w_up.shape, BF16)],
        scratch_shapes=[pltpu.VMEM((tm + POOL_HALO, pw), F32), pltpu.VMEM((pw, N_CHIPS * ns), F32)], input_output_aliases={7: 0},
        compiler_params=_params("arbitrary"))(dy, z, p, pool_w, pool_b, pool_scale, w_up, duin)


def _ssm_disc(lrl, li, ldt):
    lr = -jnp.exp(lrl)
    dt = jnp.exp(ldt)
    mag = jnp.exp(lr * dt)
    ang = li * dt
    ab_re = mag * jnp.cos(ang)
    ab_im = mag * jnp.sin(ang)
    num_re = ab_re - 1.0
    num_im = ab_im
    den = lr * lr + li * li
    f_re = (num_re * lr + num_im * li) / den
    f_im = (num_im * lr - num_re * li) / den
    return lr, dt, mag, ang, ab_re, ab_im, num_re, num_im, den, f_re, f_im


def _ssm_prep(name, lrl, li, ldt, b_re, b_im):
    gn, h = b_re.shape

    def body(lrl_ref, li_ref, ldt_ref, br_ref, bi_ref, lrdt_ref, ang_ref, bbr_ref, bbi_ref):
        lr, dt, _, ang, _, _, _, _, _, f_re, f_im = _ssm_disc(lrl_ref[...], li_ref[...], ldt_ref[...])
        lrdt_ref[...] = lr * dt
        ang_ref[...] = ang
        br, bi = br_ref[...], bi_ref[...]
        bbr_ref[...] = f_re * br - f_im * bi
        bbi_ref[...] = f_re * bi + f_im * br

    col = jax.ShapeDtypeStruct((gn, 1), F32)
    mat = jax.ShapeDtypeStruct((gn, h), F32)
    return _pcall(body, name=name, out_shape=[col, col, mat, mat])(lrl, li, ldt, b_re, b_im)


def _ssm_param_bwd(name, lrl, li, ldt, b_re, b_im, g_abre, g_abim, g_bbre, g_bbim):
    gn, h = b_re.shape

    def body(lrl_ref, li_ref, ldt_ref, br_ref, bi_ref, gar_ref, gai_ref, gbr_ref, gbi_ref,
             dlrl_ref, dli_ref, dldt_ref, dbr_ref, dbi_ref):
        li_v = li_ref[...]
        lr, dt, mag, ang, ab_re, ab_im, num_re, num_im, den, f_re, f_im = _ssm_disc(lrl_ref[...], li_v, ldt_ref[...])
        br, bi = br_ref[...], bi_ref[...]
        gbr, gbi = gbr_ref[...], gbi_ref[...]
        g_fre = jnp.sum(gbr * br + gbi * bi, axis=1, keepdims=True)
        g_fim = jnp.sum(gbi * br - gbr * bi, axis=1, keepdims=True)
        dbr_ref[...] = gbr * f_re + gbi * f_im
        dbi_ref[...] = gbi * f_re - gbr * f_im
        g_num_re = (g_fre * lr - g_fim * li_v) / den
        g_num_im = (g_fre * li_v + g_fim * lr) / den
        g_den = -(g_fre * f_re + g_fim * f_im) / den
        g_lr = (g_fre * num_re + g_fim * num_im) / den + g_den * 2.0 * lr
        g_li = (g_fre * num_im - g_fim * num_re) / den + g_den * 2.0 * li_v
        g_are = gar_ref[...] + g_num_re
        g_aim = gai_ref[...] + g_num_im
        g_mag = g_are * jnp.cos(ang) + g_aim * jnp.sin(ang)
        g_ang = g_aim * ab_re - g_are * ab_im
        g_lrdt = g_mag * mag
        g_lr = g_lr + g_lrdt * dt
        g_dt = g_lrdt * lr + g_ang * li_v
        g_li = g_li + g_ang * dt
        dlrl_ref[...] = g_lr * lr
        dli_ref[...] = g_li
        dldt_ref[...] = g_dt * dt

    col = jax.ShapeDtypeStruct((gn, 1), F32)
    mat = jax.ShapeDtypeStruct((gn, h), F32)
    return _pcall(body, name=name, out_shape=[col, col, col, mat, mat])(lrl, li, ldt, b_re, b_im, g_abre, g_abim, g_bbre, g_bbim)


def _pow_rows(lrdt, ang, k):
    mag = jnp.exp(k * lrdt)
    return mag * jnp.cos(k * ang), mag * jnp.sin(k * ang)


def _ssm_chunk(t):
    return 256 if t >= 2048 else 128


def _to_segments(dst, srcs, jn):
    for q, src in enumerate(srcs):
        for j in range(jn):
            dst[8 * j:8 * j + 8, 128 * q:128 * (q + 1)] = src[pl.ds(j, 8, stride=jn), :]


def _from_segments(dst, src, q, jn, dtype):
    for s in range(8):
        dst[s * jn:(s + 1) * jn, 128 * q:128 * (q + 1)] = src[q, pl.ds(s, jn, stride=8), :].astype(dtype)


def _fill_rows8(dst_re, dst_im, v_re, v_im):
    for j in range(v_re.shape[0]):
        dst_re[8 * j:8 * j + 8, :] = jnp.broadcast_to(v_re[j:j + 1, :], (8, v_re.shape[1]))
        dst_im[8 * j:8 * j + 8, :] = jnp.broadcast_to(v_im[j:j + 1, :], (8, v_im.shape[1]))


def _cmul(ar, ai, br, bi):
    return ar * br - ai * bi, ar * bi + ai * br


def _cmul_conj(ar, ai, br, bi):
    return ar * br + ai * bi, ar * bi - ai * br


def _ssm_fwd(name, uin, lrdt, ang, bb_re, bb_im, cc_re, cc_im, d_skip, sw):
    t = uin.shape[0]
    gn = lrdt.shape[1]
    lc = _ssm_chunk(t)
    jn = lc // 8
    ub, sb = sw // SSM_BLOCKS, gn // SSM_BLOCKS
    nq = sw // 128
    assert ub == 128 and nq == SSM_BLOCKS

    def body(u_ref, lrdt_ref, ang_ref, bbr_ref, bbi_ref, ccr_ref, cci_ref, d_ref, y_ref, ge_ref, sre_ref, sim_ref,
             p_re, p_im, a_re, a_im, up, yp, cst_re, cst_im, car_re, car_im):
        i = pl.program_id(0)
        lrdt_v, ang_v = lrdt_ref[...], ang_ref[...]

        @pl.when(i == 0)
        def _():
            k = (lax.broadcasted_iota(jnp.int32, (jn, 1), 0) + 1).astype(F32)
            _fill_rows8(p_re, p_im, *_pow_rows(lrdt_v, ang_v, k))
            car_re[...] = jnp.zeros_like(car_re)
            car_im[...] = jnp.zeros_like(car_im)

        for q in range(nq):
            yp[q] = u_ref[:, 128 * q:128 * (q + 1)].astype(F32)
        _to_segments(up, [yp.at[q] for q in range(nq)], jn)
        u = up[...]
        ubf = u.astype(BF16)
        for q in range(SSM_BLOCKS):
            uq = ubf[:, q * ub:(q + 1) * ub]
            a_re[:, q * sb:(q + 1) * sb] = jnp.dot(uq, bbr_ref[q], preferred_element_type=F32)
            a_im[:, q * sb:(q + 1) * sb] = jnp.dot(uq, bbi_ref[q], preferred_element_type=F32)
        a1r, a1i = _pow_rows(lrdt_v, ang_v, 1.0)
        ajr, aji = _pow_rows(lrdt_v, ang_v, float(jn))
        for q in range(0, SSM_BLOCKS, 2):
            cols = slice(q * sb, (q + 2) * sb)
            ar8 = jnp.broadcast_to(a1r[:, cols], (8, 2 * sb))
            ai8 = jnp.broadcast_to(a1i[:, cols], (8, 2 * sb))

            def step(j, carry, cols=cols, ar8=ar8, ai8=ai8):
                sr, si = carry
                rows = pl.ds(pl.multiple_of(j * 8, 8), 8)
                mr, mi = _cmul(ar8, ai8, sr, si)
                nr, ni = mr + a_re[rows, cols], mi + a_im[rows, cols]
                a_re[rows, cols] = nr
                a_im[rows, cols] = ni
                return nr, ni

            lax.fori_loop(1, jn, step, (a_re[0:8, cols], a_im[0:8, cols]), unroll=4)
        er, ei = a_re[lc - 8:lc, :], a_im[lc - 8:lc, :]
        gr, gi = car_re[...], car_im[...]
        for s in range(8):
            cst_re[s:s + 1, :] = gr
            cst_im[s:s + 1, :] = gi
            mr, mi = _cmul(ajr, aji, gr, gi)
            gr, gi = mr + er[s:s + 1, :], mi + ei[s:s + 1, :]
        car_re[...] = gr
        car_im[...] = gi
        for q in range(SSM_BLOCKS):
            cols = slice(q * sb, (q + 1) * sb)
            cr = jnp.tile(cst_re[:, cols], (jn, 1))
            ci = jnp.tile(cst_im[:, cols], (jn, 1))
            mr, mi = _cmul(p_re[:, cols], p_im[:, cols], cr, ci)
            srb, sib = (a_re[:, cols] + mr).astype(BF16), (a_im[:, cols] + mi).astype(BF16)
            sre_ref[:, cols] = srb
            sim_ref[:, cols] = sib
            ycols = slice(q * ub, (q + 1) * ub)
            y = (jnp.dot(srb, ccr_ref[q], preferred_element_type=F32) - jnp.dot(sib, cci_ref[q], preferred_element_type=F32)
                 + d_ref[:, ycols] * u[:, ycols])
            yp[q] = y
            _from_segments(y_ref, yp, q, jn, F32)
            yt = y_ref[:, ycols]
            ge_ref[:, ycols] = (0.5 * yt * (1.0 + lax.erf(yt * (1.0 / math.sqrt(2.0))))).astype(BF16)

    row = lambda i: (0, 0)
    blk3 = lambda i: (0, 0, 0)
    return _pcall(
        body, name=name, grid=(t // lc,),
        in_specs=[pl.BlockSpec((lc, sw), lambda i: (i, 1)), pl.BlockSpec((1, gn), row), pl.BlockSpec((1, gn), row),
                  pl.BlockSpec(bb_re.shape, blk3), pl.BlockSpec(bb_im.shape, blk3),
                  pl.BlockSpec(cc_re.shape, blk3), pl.BlockSpec(cc_im.shape, blk3), pl.BlockSpec((1, sw), row)],
        out_specs=[pl.BlockSpec((lc, sw), lambda i: (i, 0)), pl.BlockSpec((lc, sw), lambda i: (i, 0)),
                   pl.BlockSpec((lc, gn), lambda i: (i, 0)), pl.BlockSpec((lc, gn), lambda i: (i, 0))],
        out_shape=[jax.ShapeDtypeStruct((t, sw), F32), jax.ShapeDtypeStruct((t, sw), BF16),
                   jax.ShapeDtypeStruct((t, gn), BF16), jax.ShapeDtypeStruct((t, gn), BF16)],
        scratch_shapes=[pltpu.VMEM((lc, gn), F32), pltpu.VMEM((lc, gn), F32), pltpu.VMEM((lc, gn), F32), pltpu.VMEM((lc, gn), F32),
                        pltpu.VMEM((lc, sw), F32), pltpu.VMEM((nq, lc, 128), F32),
                        pltpu.VMEM((8, gn), F32), pltpu.VMEM((8, gn), F32), pltpu.VMEM((1, gn), F32), pltpu.VMEM((1, gn), F32)],
        compiler_params=_params("arbitrary"))(uin, lrdt, ang, bb_re, bb_im, cc_re, cc_im, d_skip)


def _ssm_bwd(name, dge, y, uin, s_re, s_im, lrdt, ang, bbt_re, bbt_im, cct_re, cct_im, d_skip, sw, duin):
    t = uin.shape[0]
    gn = lrdt.shape[1]
    lc = _ssm_chunk(t)
    nb = t // lc
    jn = lc // 8
    ub, sb = sw // SSM_BLOCKS, gn // SSM_BLOCKS
    nq = sw // 128
    tail = 16

    def body(dge_ref, y_ref, u_ref, sre_ref, sim_ref, tre_ref, tim_ref, lrdt_ref, ang_ref, btr_ref, bti_ref, ctr_ref, cti_ref,
             d_ref, duin_ref, du_ref, dar_ref, dai_ref, dbr_ref, dbi_ref, dcr_ref, dci_ref, dd_ref,
             q_re, q_im, a_re, a_im, dyp, up, dys, us, dup, cst_re, cst_im, sp_re, sp_im, car_re, car_im):
        i = pl.program_id(0)
        lrdt_v, ang_v = lrdt_ref[...], ang_ref[...]

        @pl.when(i == 0)
        def _():
            k = (jn - lax.broadcasted_iota(jnp.int32, (jn, 1), 0)).astype(F32)
            _fill_rows8(q_re, q_im, *_pow_rows(lrdt_v, ang_v, k))
            car_re[...] = jnp.zeros_like(car_re)
            car_im[...] = jnp.zeros_like(car_im)
            for r in (dar_ref, dai_ref, dbr_ref, dbi_ref, dcr_ref, dci_ref, dd_ref):
                r[...] = jnp.zeros_like(r)

        yv = y_ref[...]
        ut = u_ref[...].astype(F32)
        cdf =0.5 * (1.0 + lax.erf(yv * (1.0 / math.sqrt(2.0))))
        pdf = jnp.exp(-0.5 * yv * yv) * (1.0 / math.sqrt(2.0 * math.pi))
        dyt = dge_ref[...].astype(F32) * (cdf + yv * pdf)
        dd_ref[...] += _colsum(dyt * ut)
        for q in range(nq):
            dys[q] = dyt[:, 128 * q:128 * (q + 1)]
            us[q] = ut[:, 128 * q:128 * (q + 1)]
        _to_segments(dyp, [dys.at[q] for q in range(nq)], jn)
        _to_segments(up, [us.at[q] for q in range(nq)], jn)
        dy = dyp[...]
        u = up[...]
        dyb = dy.astype(BF16)
        ubf = u.astype(BF16)
        for q in range(SSM_BLOCKS):
            dq = dyb[:, q * ub:(q + 1) * ub]
            a_re[:, q * sb:(q + 1) * sb] = jnp.dot(dq, ctr_ref[q], preferred_element_type=F32)
            a_im[:, q * sb:(q + 1) * sb] = -jnp.dot(dq, cti_ref[q], preferred_element_type=F32)
        a1r, a1i = _pow_rows(lrdt_v, ang_v, 1.0)
        ajr, aji = _pow_rows(lrdt_v, ang_v, float(jn))
        for q in range(0, SSM_BLOCKS, 2):
            cols = slice(q * sb, (q + 2) * sb)
            ar8 = jnp.broadcast_to(a1r[:, cols], (8, 2 * sb))
            ai8 = jnp.broadcast_to(a1i[:, cols], (8, 2 * sb))

            def step(jj, carry, cols=cols, ar8=ar8, ai8=ai8):
                sr, si = carry
                rows = pl.ds(pl.multiple_of((jn - 2 - jj) * 8, 8), 8)
                mr, mi = _cmul_conj(ar8, ai8, sr, si)
                nr, ni = mr + a_re[rows, cols], mi + a_im[rows, cols]
                a_re[rows, cols] = nr
                a_im[rows, cols] = ni
                return nr, ni

            lax.fori_loop(0, jn - 1, step, (a_re[lc - 8:lc, cols], a_im[lc - 8:lc, cols]), unroll=4)
        er, ei = a_re[0:8, :], a_im[0:8, :]
        hr, hi = car_re[...], car_im[...]
        for s in range(7, -1, -1):
            cst_re[s:s + 1, :] = hr
            cst_im[s:s + 1, :] = hi
            mr, mi = _cmul_conj(ajr, aji, hr, hi)
            hr, hi = mr + er[s:s + 1, :], mi + ei[s:s + 1, :]
        car_re[...] = hr
        car_im[...] = hi
        first = (i == nb - 1).astype(F32)
        sp_re[0:tail, :] = tre_ref[...].astype(F32) * (1.0 - first)
        sp_im[0:tail, :] = tim_ref[...].astype(F32) * (1.0 - first)
        sp_re[tail:tail + 8, :] = sre_ref[lc - tail:lc, :].astype(F32)[tail - 8:tail]
        sp_im[tail:tail + 8, :] = sim_ref[lc - tail:lc, :].astype(F32)[tail - 8:tail]
        tn_dims = (((0,), (0,)), ((), ()))
        for q in range(SSM_BLOCKS):
            cols = slice(q * sb, (q + 1) * sb)
            ycols = slice(q * ub, (q + 1) * ub)
            cr = jnp.tile(cst_re[:, cols], (jn, 1))
            ci = jnp.tile(cst_im[:, cols], (jn, 1))
            mr, mi = _cmul_conj(q_re[:, cols], q_im[:, cols], cr, ci)
            lam_r, lam_i = a_re[:, cols] + mr, a_im[:, cols] + mi
            s_r, s_i = sre_ref[:, cols], sim_ref[:, cols]
            p0r, p0i = sp_re[tail - 1:tail + 7, cols], sp_im[tail - 1:tail + 7, cols]
            l0r, l0i, l1r, l1i = lam_r[0:8], lam_i[0:8], lam_r[8:lc], lam_i[8:lc]
            pvr, pvi = s_r.astype(F32)[0:lc - 8], s_i.astype(F32)[0:lc - 8]
            dar_ref[:, cols] += _colsum(l1r * pvr + l1i * pvi) + _colsum(l0r * p0r + l0i * p0i)
            dai_ref[:, cols] += _colsum(l1i * pvr - l1r * pvi) + _colsum(l0i * p0r - l0r * p0i)
            lrb, lib = lam_r.astype(BF16), lam_i.astype(BF16)
            dup[q] = (jnp.dot(lrb, btr_ref[q], preferred_element_type=F32)
                      + jnp.dot(lib, bti_ref[q], preferred_element_type=F32) + d_ref[:, ycols] * dy[:, ycols])
            _from_segments(du_ref, dup, q, jn, BF16)
            uq = ubf[:, ycols]
            dbr_ref[q] += lax.dot_general(uq, lrb, tn_dims, preferred_element_type=F32)
            dbi_ref[q] += lax.dot_general(uq, lib, tn_dims, preferred_element_type=F32)
            dq = dyb[:, ycols]
            dcr_ref[q] += lax.dot_general(s_r.astype(BF16), dq, tn_dims, preferred_element_type=F32)
            dci_ref[q] -= lax.dot_general(s_i.astype(BF16), dq, tn_dims, preferred_element_type=F32)

    rev = lambda i: (nb - 1 - i, 0)
    tailmap = lambda i: (jnp.maximum((nb - 1 - i) * (lc // tail) - 1, 0), 0)
    row = lambda i: (0, 0)
    blk3 = lambda i: (0, 0, 0)
    return _pcall(
        body, name=name, grid=(nb,),
        in_specs=[pl.BlockSpec((lc, sw), rev), pl.BlockSpec((lc, sw), rev), pl.BlockSpec((lc, sw), lambda i: (nb - 1 - i, 1)),
                  pl.BlockSpec((lc, gn), rev), pl.BlockSpec((lc, gn), rev),
                  pl.BlockSpec((tail, gn), tailmap), pl.BlockSpec((tail, gn), tailmap),
                  pl.BlockSpec((1, gn), row), pl.BlockSpec((1, gn), row),
                  pl.BlockSpec(bbt_re.shape, blk3), pl.BlockSpec(bbt_im.shape, blk3),
                  pl.BlockSpec(cct_re.shape, blk3), pl.BlockSpec(cct_im.shape, blk3), pl.BlockSpec((1, sw), row), ANY_SPEC],
        out_specs=[pl.BlockSpec((lc, sw), lambda i: (nb - 1 - i, 1)), pl.BlockSpec((1, gn), row), pl.BlockSpec((1, gn), row),
                   pl.BlockSpec((SSM_BLOCKS, ub, sb), blk3), pl.BlockSpec((SSM_BLOCKS, ub, sb), blk3),
                   pl.BlockSpec((SSM_BLOCKS, sb, ub), blk3), pl.BlockSpec((SSM_BLOCKS, sb, ub), blk3),
                   pl.BlockSpec((1, sw), row)],
        out_shape=[jax.ShapeDtypeStruct(duin.shape, BF16), jax.ShapeDtypeStruct((1, gn), F32), jax.ShapeDtypeStruct((1, gn), F32),
                   jax.ShapeDtypeStruct((SSM_BLOCKS, ub, sb), F32), jax.ShapeDtypeStruct((SSM_BLOCKS, ub, sb), F32),
                   jax.ShapeDtypeStruct((SSM_BLOCKS, sb, ub), F32), jax.ShapeDtypeStruct((SSM_BLOCKS, sb, ub), F32),
                   jax.ShapeDtypeStruct((1, sw), F32)],
        scratch_shapes=[pltpu.VMEM((lc, gn), F32), pltpu.VMEM((lc, gn), F32), pltpu.VMEM((lc, gn), F32), pltpu.VMEM((lc, gn), F32),
                        pltpu.VMEM((lc, sw), F32), pltpu.VMEM((lc, sw), F32),
                        pltpu.VMEM((nq, lc, 128), F32), pltpu.VMEM((nq, lc, 128), F32), pltpu.VMEM((nq, lc, 128), F32),
                        pltpu.VMEM((8, gn), F32), pltpu.VMEM((8, gn), F32),
                        pltpu.VMEM((tail + 8, gn), F32), pltpu.VMEM((tail + 8, gn), F32),
                        pltpu.VMEM((1, gn), F32), pltpu.VMEM((1, gn), F32)],
        input_output_aliases={14: 0},
        compiler_params=_params("arbitrary"))(dge, y, uin, s_re, s_im, s_re, s_im, lrdt, ang,
                                               bbt_re, bbt_im, cct_re, cct_im, d_skip, duin)


def _blockdiag_b(bb, sw):
    gpb = (sw // SSM_GROUP) // SSM_BLOCKS
    b4 = bb.reshape(SSM_BLOCKS, gpb, SSM_STATE, SSM_GROUP)
    eye = jnp.eye(gpb, dtype=bb.dtype)
    out = jnp.einsum('qgnh,gk->qghkn', b4, eye)
    return out.reshape(SSM_BLOCKS, gpb * SSM_GROUP, gpb * SSM_STATE)


def _blockdiag_c(cc, sw):
    gpb = (sw // SSM_GROUP) // SSM_BLOCKS
    c4 = cc.reshape(SSM_BLOCKS, gpb, SSM_GROUP, SSM_STATE)
    eye = jnp.eye(gpb, dtype=cc.dtype)
    out = jnp.einsum('qghn,gk->qgnkh', c4, eye)
    return out.reshape(SSM_BLOCKS, gpb * SSM_STATE, gpb * SSM_GROUP)


def _diag_of_b(dbb, sw):
    gpb = (sw // SSM_GROUP) // SSM_BLOCKS
    d5 = dbb.reshape(SSM_BLOCKS, gpb, SSM_GROUP, gpb, SSM_STATE)
    return jnp.einsum('qghgn->qgnh', d5).reshape(SSM_BLOCKS * gpb * SSM_STATE, SSM_GROUP)


def _diag_of_c(dcc, sw):
    gpb = (sw // SSM_GROUP) // SSM_BLOCKS
    d5 = dcc.reshape(SSM_BLOCKS, gpb, SSM_STATE, gpb, SSM_GROUP)
    return jnp.einsum('qgngh->qghn', d5).reshape(SSM_BLOCKS * gpb, SSM_GROUP, SSM_STATE)


def _ada_fwd(name, c_all, w_sh, b_sh):
    nb, d = c_all.shape
    ncol = w_sh.shape[1]
    tn = _pick(ncol, 768)

    def body(c_ref, w_ref, b_ref, o_ref):
        cv = c_ref[...]
        sil = cv * _sigmoid(cv)
        o_ref[...] = jnp.dot(sil, w_ref[...], preferred_element_type=F32, precision=lax.Precision.HIGHEST) + b_ref[...]

    return _pcall(body, name=name, grid=(ncol // tn,),
                  in_specs=[pl.BlockSpec((nb, d), lambda j: (0, 0)), pl.BlockSpec((d, tn), lambda j: (0, j)),
                            pl.BlockSpec((1, tn), lambda j: (0, j))],
                  out_specs=pl.BlockSpec((nb, tn), lambda j: (0, j)),
                  out_shape=jax.ShapeDtypeStruct((nb, ncol), F32), compiler_params=_params("parallel"))(c_all, w_sh, b_sh)


def _ada_bwd(name, c_all, dmod_sh):
    nb, d = c_all.shape
    ncol = dmod_sh.shape[1]
    tn = _pick(ncol, 768)

    def body(c_ref, g_ref, o_ref):
        cv = c_ref[...]
        sil = cv * _sigmoid(cv)
        o_ref[...] = lax.dot_general(sil, g_ref[...], (((0,), (0,)), ((), ())), preferred_element_type=F32,
                                     precision=lax.Precision.HIGHEST)

    return _pcall(body, name=name, grid=(ncol // tn,),
                  in_specs=[pl.BlockSpec((nb, d), lambda j: (0, 0)), pl.BlockSpec((nb, tn), lambda j: (0, j))],
                  out_specs=pl.BlockSpec((d, tn), lambda j: (0, j)),
                  out_shape=jax.ShapeDtypeStruct((d, ncol), F32), compiler_params=_params("parallel"))(c_all, dmod_sh)


def _place():
    return lax.axis_index("x"), lax.axis_index("y"), lax.axis_index("c")


def _allgather_small(name, blk, deps=()):
    m_per, n = blk.shape

    def body(x_ref, *rest):
        out_ref, send_sems, recv_sems, local_sem = rest[len(deps):]
        x, y, c = _place()
        me, sibling = (x, y, c), (x, y, 1 - c)
        chips = [(1 - x, y), (x, 1 - y), (1 - x, 1 - y)]

        def rows(px, py, pc):
            return out_ref.at[pl.ds((4 * px + 2 * py + pc) * m_per, m_per), :]

        def copy(k, block, to, src=None):
            return pltpu.make_async_remote_copy(
                src_ref=rows(*block) if src is None else src, dst_ref=rows(*block),
                send_sem=send_sems.at[k], recv_sem=recv_sems.at[k], device_id=to, device_id_type=MESH)

        mine = pltpu.make_async_copy(x_ref, rows(*me), local_sem)
        mine.start()
        first = [copy(0, me, sibling, src=x_ref)]
        first += [copy(1 + j, me, (*chip, c), src=x_ref) for j, chip in enumerate(chips)]
        for cp in first:
            cp.start()
        passed = [copy(4 + j, (*chip, c), sibling) for j, chip in enumerate(chips)]
        for j, chip in enumerate(chips):
            copy(1 + j, (*chip, c), me).wait_recv()
            passed[j].start()
        copy(0, sibling, me).wait_recv()
        for j, chip in enumerate(chips):
            copy(4 + j, (*chip, 1 - c), me).wait_recv()
        for cp in first + passed:
            cp.wait_send()
        mine.wait()

    return _pcall(body, name=name, out_shape=jax.ShapeDtypeStruct((N_DEV * m_per, n), blk.dtype),
                  in_specs=[pl.BlockSpec(memory_space=pltpu.VMEM)] + [ANY_SPEC] * len(deps),
                  out_specs=pl.BlockSpec(memory_space=pltpu.VMEM),
                  scratch_shapes=[pltpu.SemaphoreType.DMA((7,)), pltpu.SemaphoreType.DMA((7,)), pltpu.SemaphoreType.DMA],
                  compiler_params=pltpu.CompilerParams(vmem_limit_bytes=VMEM_LIMIT))(blk, *deps)


def _other_chips(x, y):
    return [(1 - x, y), (x, 1 - y), (1 - x, 1 - y)]


def _place_shards(shards, s_me):
    return [lax.dynamic_update_slice(lax.empty((N_CHIPS,) + s.shape, s.dtype), s[None], (s_me, 0, 0)) for s in shards]


def _ag_copy(src, land, send, recv, wi, j, chip, x, y, c, tc, both):
    hr = src.shape[0] // 2
    half = pl.ds(pl.multiple_of(c * hr, 16), hr)
    k = 3 * wi + j
    return pltpu.make_async_remote_copy(
        src_ref=src.at[half, :], dst_ref=land.at[2 * x + y, half, :],
        send_sem=send.at[2 * k + tc if both else k], recv_sem=recv.at[2 * k + c if both else k],
        device_id=(chip[0], chip[1], tc), device_id_type=MESH)


def _ag_targets(c, both):
    return (0, 1) if both else (c,)


def _ag_start(name, shards, lands, groups, direct, deps=()):
    nw, ng, nd = len(shards), len(groups), len(deps)

    def body(*refs):
        src, land = refs[:nw], refs[nw:2 * nw]
        sems = refs[2 * nw + nd:2 * nw + nd + 2 * ng]
        token = refs[-1]
        x, y, c = _place()
        for gi, grp in enumerate(groups):
            for wi, w in enumerate(grp):
                for j, chip in enumerate(_other_chips(x, y)):
                    for tc in _ag_targets(c, direct[gi]):
                        _ag_copy(src[w], land[w], sems[2 * gi], sems[2 * gi + 1], wi, j, chip, x, y, c, tc, direct[gi]).start()
        token[...] = jnp.zeros_like(token)

    sem_shapes = []
    for gi, grp in enumerate(groups):
        sem_shapes += [pltpu.SemaphoreType.DMA(((6 if direct[gi] else 3) * len(grp),))] * 2
    out_shape = sem_shapes + [pltpu.HBM(s.shape, s.dtype) for s in shards] + [pltpu.HBM(l.shape, l.dtype) for l in lands]
    out_shape += [jax.ShapeDtypeStruct((8, 128), F32)]
    res = _pcall(body, name=name, out_shape=out_shape, in_specs=[HBM_SPEC] * (2 * nw) + [ANY_SPEC] * nd,
                 out_specs=[SEM_SPEC] * (2 * ng) + [HBM_SPEC] * (2 * nw) + [pl.BlockSpec(memory_space=pltpu.VMEM)],
                 input_output_aliases={i: 2 * ng + i for i in range(2 * nw)},
                 compiler_params=pltpu.CompilerParams(has_side_effects=EFFECT))(
                     *[_hbm(s) for s in shards], *[_hbm(l) for l in lands], *deps)
    sems = [(res[2 * gi], res[2 * gi + 1]) for gi in range(ng)]
    return sems, list(res[2 * ng:2 * ng + nw]), list(res[2 * ng + nw:2 * ng + 2 * nw]), res[-1]


def _ag_wait(name, shards, lands, send, recv, both, after):
    n = len(shards)

    def body(*refs):
        src, land = refs[:n], refs[n:2 * n]
        send_sem, recv_sem = refs[2 * n], refs[2 * n + 1]
        x, y, c = _place()
        for wi in range(n):
            for j, chip in enumerate(_other_chips(x, y)):
                for tc in _ag_targets(c, both):
                    _ag_copy(src[wi], land[wi], send_sem, recv_sem, wi, j, chip, x, y, c, tc, both).wait_send()
                    _ag_copy(src[wi], land[wi], send_sem, recv_sem, wi, j, chip, chip[0], chip[1], tc, c, both).wait_recv()

    res = _pcall(body, name=name, out_shape=[pltpu.HBM(a.shape, a.dtype) for a in list(shards) + list(lands)],
                 in_specs=[HBM_SPEC] * (2 * n) + [SEM_SPEC, SEM_SPEC] + [ANY_SPEC] * len(after), out_specs=[HBM_SPEC] * (2 * n),
                 input_output_aliases={i: i for i in range(2 * n)},
                 compiler_params=pltpu.CompilerParams(has_side_effects=EFFECT))(*shards, *lands, send, recv, *after)
    return list(res[n:])


def _ag_forward(name, lands):
    n = len(lands)

    def body(*refs):
        out = refs[n:2 * n]
        send, recv = refs[2 * n], refs[2 * n + 1]
        x, y, c = _place()
        sib = (x, y, 1 - c)
        cps = []
        for wi in range(n):
            hr = out[wi].shape[1] // 2
            for j, (cx, cy) in enumerate(_other_chips(x, y)):
                got = out[wi].at[2 * cx + cy, pl.ds(pl.multiple_of(c * hr, 16), hr), :]
                cp = pltpu.make_async_remote_copy(src_ref=got, dst_ref=got, send_sem=send.at[3 * wi + j], recv_sem=recv.at[3 * wi + j],
                                                  device_id=sib, device_id_type=MESH)
                cp.start()
                cps.append(cp)
        for wi in range(n):
            hr = out[wi].shape[1] // 2
            for j, (cx, cy) in enumerate(_other_chips(x, y)):
                got = out[wi].at[2 * cx + cy, pl.ds(pl.multiple_of((1 - c) * hr, 16), hr), :]
                pltpu.make_async_remote_copy(src_ref=got, dst_ref=got, send_sem=send.at[3 * wi + j], recv_sem=recv.at[3 * wi + j],
                                             device_id=sib, device_id_type=MESH).wait_recv()
        for cp in cps:
            cp.wait_send()

    res = _pcall(body, name=name, out_shape=[jax.ShapeDtypeStruct(l.shape, l.dtype) for l in lands],
                 in_specs=[ANY_SPEC] * n, out_specs=[ANY_SPEC] * n, input_output_aliases={i: i for i in range(n)},
                 scratch_shapes=[pltpu.SemaphoreType.DMA((3 * n,)), pltpu.SemaphoreType.DMA((3 * n,))])(*lands)
    return list(res)


def _peers(x, y, c):
    offs = [(dx, dy, dc) for dx in (0, 1) for dy in (0, 1) for dc in (0, 1)][1:]
    return [(1 - x if dx else x, 1 - y if dy else y, 1 - c if dc else c) for dx, dy, dc in offs]


def _rs_copy(g_ref, land_ref, send, recv, wi, k, to, sender):
    hr = g_ref.shape[1] // 2
    return pltpu.make_async_remote_copy(
        src_ref=g_ref.at[2 * to[0] + to[1], pl.ds(pl.multiple_of(to[2] * hr, 16), hr), :], dst_ref=land_ref.at[sender],
        send_sem=send.at[7 * wi + k], recv_sem=recv.at[7 * wi + k], device_id=to, device_id_type=MESH)


def _rs_start(name, gs):
    n = len(gs)
    lands = [lax.empty((N_DEV, g.shape[1] // 2, g.shape[2]), BF16) for g in gs]

    def body(*refs):
        g, land = refs[:n], refs[n:2 * n]
        send, recv = refs[2 * n], refs[2 * n + 1]
        token = refs[-1]
        x, y, c = _place()
        me = 4 * x + 2 * y + c
        for wi in range(n):
            for k, to in enumerate(_peers(x, y, c)):
                _rs_copy(g[wi], land[wi], send, recv, wi, k, to, me).start()
        token[...] = jnp.zeros_like(token)

    out_shape = [pltpu.SemaphoreType.DMA((7 * n,))] * 2 + [pltpu.HBM(a.shape, a.dtype) for a in list(gs) + lands]
    out_shape += [jax.ShapeDtypeStruct((8, 128), F32)]
    res = _pcall(body, name=name, out_shape=out_shape, in_specs=[HBM_SPEC] * (2 * n),
                 out_specs=[SEM_SPEC] * 2 + [HBM_SPEC] * (2 * n) + [pl.BlockSpec(memory_space=pltpu.VMEM)],
                 input_output_aliases={i: 2 + i for i in range(2 * n)},
                 compiler_params=pltpu.CompilerParams(has_side_effects=EFFECT))(
                     *[_hbm(a) for a in gs], *[_hbm(a) for a in lands])
    return res[0], res[1], list(res[2:2 + n]), list(res[2 + n:2 + 2 * n]), res[-1]


def _rs_wait(name, gs, lands, send, recv, after):
    n = len(gs)

    def body(*refs):
        g, land = refs[:n], refs[n:2 * n]
        send_sem, recv_sem = refs[2 * n], refs[2 * n + 1]
        x, y, c = _place()
        me = 4 * x + 2 * y + c
        for wi in range(n):
            for k, to in enumerate(_peers(x, y, c)):
                _rs_copy(g[wi], land[wi], send_sem, recv_sem, wi, k, to, me).wait_send()
                _rs_copy(g[wi], land[wi], send_sem, recv_sem, wi, k, (x, y, c), 4 * to[0] + 2 * to[1] + to[2]).wait_recv()

    res = _pcall(body, name=name, out_shape=[pltpu.HBM(a.shape, a.dtype) for a in list(gs) + list(lands)],
                 in_specs=[HBM_SPEC] * (2 * n) + [SEM_SPEC, SEM_SPEC, ANY_SPEC], out_specs=[HBM_SPEC] * (2 * n),
                 input_output_aliases={i: i for i in range(2 * n)},
                 compiler_params=pltpu.CompilerParams(has_side_effects=EFFECT))(*gs, *lands, send, recv, after)
    return list(res[:n]), list(res[n:])


def _bc_copy(blk_ref, land_ref, send, recv, k, to, slot):
    return pltpu.make_async_remote_copy(src_ref=blk_ref, dst_ref=land_ref.at[slot], send_sem=send.at[k], recv_sem=recv.at[k],
                                        device_id=to, device_id_type=MESH)


def _bcast_start(name, blk):
    land = lax.empty((N_DEV,) + blk.shape, blk.dtype)

    def body(blk_ref, land_ref, send, recv, blk_thru, land_thru, token):
        x, y, c = _place()
        for k, to in enumerate(_peers(x, y, c)):
            _bc_copy(blk_ref, land_ref, send, recv, k, to, 4 * x + 2 * y + c).start()
        token[...] = jnp.zeros_like(token)

    return _pcall(body, name=name,
                  out_shape=[pltpu.SemaphoreType.DMA((7,)), pltpu.SemaphoreType.DMA((7,)), pltpu.HBM(blk.shape, blk.dtype),
                             pltpu.HBM(land.shape, land.dtype), jax.ShapeDtypeStruct((8, 128), F32)],
                  in_specs=[HBM_SPEC, HBM_SPEC], out_specs=[SEM_SPEC, SEM_SPEC, HBM_SPEC, HBM_SPEC, pl.BlockSpec(memory_space=pltpu.VMEM)],
                  input_output_aliases={0: 2, 1: 3}, compiler_params=pltpu.CompilerParams(has_side_effects=EFFECT))(_hbm(blk), _hbm(land))


def _bcast_wait(name, blk, land, send, recv, after):
    def body(blk_ref, land_ref, send_sem, recv_sem, after_ref, blk_thru, land_thru):
        x, y, c = _place()
        for k, to in enumerate(_peers(x, y, c)):
            _bc_copy(blk_ref, land_ref, send_sem, recv_sem, k, to, 4 * x + 2 * y + c).wait_send()
            _bc_copy(blk_ref, land_ref, send_sem, recv_sem, k, to, 4 * to[0] + 2 * to[1] + to[2]).wait_recv()

    return _pcall(body, name=name, out_shape=[pltpu.HBM(blk.shape, blk.dtype), pltpu.HBM(land.shape, land.dtype)],
                  in_specs=[HBM_SPEC, HBM_SPEC, SEM_SPEC, SEM_SPEC, ANY_SPEC], out_specs=[HBM_SPEC, HBM_SPEC],
                  input_output_aliases={0: 0, 1: 1}, compiler_params=pltpu.CompilerParams(has_side_effects=EFFECT))(
                      blk, land, send, recv, after)[1]


def _rs_sum(name, gs, lands):
    n = len(gs)

    def body(*refs):
        g_refs, land_refs, out_refs = refs[:n], refs[n:2 * n], refs[2 * n:3 * n]
        recvs = refs[3 * n:4 * n]
        local_sems, sib_send, sib_recv = refs[4 * n:]
        x, y, c = _place()
        me = 4 * x + 2 * y + c
        cps = []
        for wi in range(n):
            hr = g_refs[wi].shape[1] // 2
            own = g_refs[wi].at[2 * x + y, pl.ds(pl.multiple_of(c * hr, 16), hr), :]
            cps.append(pltpu.make_async_copy(own, recvs[wi].at[me], local_sems.at[8 * wi + 7]))
            for k, (tx, ty, tc) in enumerate(_peers(x, y, c)):
                slot = 4 * tx + 2 * ty + tc
                cps.append(pltpu.make_async_copy(land_refs[wi].at[slot], recvs[wi].at[slot], local_sems.at[8 * wi + k]))
        for cp in cps:
            cp.start()
        sibs = []
        for wi in range(n):
            hr = g_refs[wi].shape[1] // 2
            ch = _pick(hr, 64, 16)
            for cp in cps[8 * wi:8 * wi + 8]:
                cp.wait()
            base = pl.multiple_of(c * hr, 16)
            for r0 in range(0, hr, ch):
                acc = recvs[wi][0, r0:r0 + ch, :].astype(F32)
                for k in range(1, N_DEV):
                    acc = acc + recvs[wi][k, r0:r0 + ch, :].astype(F32)
                out_refs[wi][pl.ds(base + r0, ch), :] = acc
            half = out_refs[wi].at[pl.ds(base, hr), :]
            sib = pltpu.make_async_remote_copy(src_ref=half, dst_ref=half, send_sem=sib_send.at[wi], recv_sem=sib_recv.at[wi],
                                               device_id=(x, y, 1 - c), device_id_type=MESH)
            sib.start()
            sibs.append(sib)
        for wi in range(n):
            hr = g_refs[wi].shape[1] // 2
            other = out_refs[wi].at[pl.ds(pl.multiple_of((1 - c) * hr, 16), hr), :]
            pltpu.make_async_remote_copy(src_ref=other, dst_ref=other, send_sem=sib_send.at[wi], recv_sem=sib_recv.at[wi],
                                         device_id=(x, y, 1 - c), device_id_type=MESH).wait_recv()
            sibs[wi].wait_send()

    res = _pcall(
        body, name=name, out_shape=[jax.ShapeDtypeStruct(g.shape[1:], F32) for g in gs],
        in_specs=[ANY_SPEC] * (2 * n), out_specs=[pl.BlockSpec(memory_space=pltpu.VMEM)] * n,
        scratch_shapes=[pltpu.VMEM((N_DEV, g.shape[1] // 2, g.shape[2]), BF16) for g in gs]
        + [pltpu.SemaphoreType.DMA((8 * n,)), pltpu.SemaphoreType.DMA((n,)), pltpu.SemaphoreType.DMA((n,))],
        compiler_params=pltpu.CompilerParams(vmem_limit_bytes=VMEM_LIMIT))(*gs, *lands)
    return list(res)


def _gate_norm_epilogue(factor, x_in, gt, nxt, t, d):
    blk = (ROWS, d)
    full = lambda i, j: (i, j)
    rowv = lambda i, j: (0, j)

    def epi(acc, res, scale, *norm):
        x_new = res + (factor * scale) * acc
        if not norm:
            return x_new, acc
        gv, scv, shv = norm
        r = lax.rsqrt(jnp.mean(x_new * x_new, axis=-1, keepdims=True) + EPS)
        return x_new, acc, x_new * r * gv * (1.0 + scv) + shv

    ins = [(x_in, blk, full), (gt, (1, d), rowv)] + [(v, (1, d), rowv) for v in (nxt or ())]
    outs = [((t, d), F32, blk, full), ((t, d), BF16, blk, full)] + ([((t, d), BF16, blk, full)] if nxt else [])
    return epi, ins, outs


def _mixer_tail(name, ge, w_glu, b_glu, w_up, y_pool, uin, w_out, x_in, gt, nxt, d, pw, tm=ROWS):
    t, sw = ge.shape
    tm = min(tm, t)
    ns = w_glu.shape[2]
    cb = (2 * pw) // d

    def body(ge_ref, wg_ref, b_ref, wu_ref, yp_ref, glp_ref, gls_ref, wo_ref, x_ref, gt_ref, gn_ref, sc_ref, sh_ref,
             gv_ref, sg_ref, ys_ref, mg_ref, y_ref, h_ref, xn_ref):
        gev = ge_ref[...]
        gv = jnp.concatenate([jnp.dot(gev, wg_ref[s], preferred_element_type=F32) for s in range(N_CHIPS)], axis=1)
        gv_ref[...] = gv.astype(BF16)
        g = gv + b_ref[...]
        sg = (g[:, :sw] * _sigmoid(g[:, sw:])).astype(BF16)
        sg_ref[...] = sg
        ys = jnp.concatenate([jnp.dot(sg, wu_ref[s], preferred_element_type=F32) for s in range(N_CHIPS)], axis=1)
        ys_ref[...] = ys.astype(BF16)
        mg = (_sigmoid(glp_ref[...].astype(F32)) * yp_ref[...].astype(F32) + _sigmoid(gls_ref[...].astype(F32)) * ys).astype(BF16)
        mg_ref[...] = mg
        y = jnp.dot(mg, wo_ref[...], preferred_element_type=F32)
        y_ref[...] = y.astype(BF16)
        xn = x_ref[...] + gt_ref[...] * y
        xn_ref[...] = xn
        r = lax.rsqrt(jnp.mean(xn * xn, axis=-1, keepdims=True) + EPS)
        h_ref[...] = (xn * r * gn_ref[...] * (1.0 + sc_ref[...]) + sh_ref[...]).astype(BF16)

    full3 = lambda i: (0, 0, 0)
    full2 = lambda i: (0, 0)
    rows = lambda w: pl.BlockSpec((tm, w), lambda i: (i, 0))
    rowv = pl.BlockSpec((1, d), full2)
    return _pcall(
        body, name=name, grid=(t // tm,),
        in_specs=[rows(sw), pl.BlockSpec(w_glu.shape, full3), pl.BlockSpec(b_glu.shape, full2),
                  pl.BlockSpec(w_up.shape, full3), rows(d), pl.BlockSpec((tm, d), lambda i: (i, cb)),
                  pl.BlockSpec((tm, d), lambda i: (i, cb + 1)), pl.BlockSpec(w_out.shape, full2), rows(d), rowv, rowv, rowv, rowv],
        out_specs=[rows(N_CHIPS * ns), rows(sw), rows(d), rows(d), rows(d), rows(d), rows(d)],
        out_shape=[jax.ShapeDtypeStruct((t, N_CHIPS * ns), BF16), jax.ShapeDtypeStruct((t, sw), BF16)]
        + [jax.ShapeDtypeStruct((t, d), BF16)] * 4 + [jax.ShapeDtypeStruct((t, d), F32)],
        compiler_params=_params("parallel"))(ge, w_glu, b_glu, w_up, y_pool, uin, uin, w_out, x_in, gt, *nxt)


def _glu_up_bwd(name, dy, w_up, gv, b_glu, w_glu, sg, ge, tm=ROWS, deps=()):
    t, sw = ge.shape
    tm = min(tm, t)
    nb = t // tm
    ns = w_up.shape[2]
    nt = (((1,), (1,)), ((), ()))
    tn_ = (((0,), (0,)), ((), ()))

    def body(dy_ref, wu_ref, gv_ref, b_ref, wg_ref, sg_ref, ge_ref, *rest):
        dge_ref, db_ref, dwu_ref, dwg_ref, dwu_acc, dwg_acc = rest[len(deps):]
        i = pl.program_id(0)

        @pl.when(i == 0)
        def _():
            db_ref[...] = jnp.zeros_like(db_ref)
            dwu_acc[...] = jnp.zeros_like(dwu_acc)
            dwg_acc[...] = jnp.zeros_like(dwg_acc)

        dyv = dy_ref[...]
        dsg = None
        for s in range(N_CHIPS):
            q = lax.dot_general(dyv[:, s * ns:(s + 1) * ns], wu_ref[s], nt, preferred_element_type=F32)
            dsg = q if dsg is None else dsg + q
        g = gv_ref[...].astype(F32) + b_ref[...]
        val, sgm = g[:, :sw], _sigmoid(g[:, sw:])
        dval, dgate = dsg * sgm, dsg * val * sgm * (1.0 - sgm)
        db_ref[...] += jnp.concatenate([_colsum(dval), _colsum(dgate)], axis=1)
        dgv = jnp.concatenate([dval, dgate], axis=1).astype(BF16)
        dge = None
        for s in range(N_CHIPS):
            q = lax.dot_general(dgv[:, s * ns:(s + 1) * ns], wg_ref[s], nt, preferred_element_type=F32)
            dge = q if dge is None else dge + q
        dge_ref[...] = dge.astype(BF16)
        dwu_acc[...] += lax.dot_general(sg_ref[...], dyv, tn_, preferred_element_type=F32)
        dwg_acc[...] += lax.dot_general(ge_ref[...], dgv, tn_, preferred_element_type=F32)

        @pl.when(i == nb - 1)
        def _():
            for s in range(N_CHIPS):
                dwu_ref[s] = dwu_acc[:, s * ns:(s + 1) * ns].astype(BF16)
                dwg_ref[s] = dwg_acc[:, s * ns:(s + 1) * ns].astype(BF16)

    full3 = lambda i: (0, 0, 0)
    rows = lambda w: pl.BlockSpec((tm, w), lambda i: (i, 0))
    wide = N_CHIPS * ns
    return _pcall(
        body, name=name, grid=(nb,),
        in_specs=[rows(wide), pl.BlockSpec(w_up.shape, full3), rows(wide), pl.BlockSpec(b_glu.shape, lambda i: (0, 0)),
                  pl.BlockSpec(w_glu.shape, full3), rows(sw), rows(sw)] + [ANY_SPEC] * len(deps),
        out_specs=[rows(sw), pl.BlockSpec((1, wide), lambda i: (0, 0)), pl.BlockSpec(w_up.shape, full3),
                   pl.BlockSpec(w_glu.shape, full3)],
        out_shape=[jax.ShapeDtypeStruct((t, sw), BF16), jax.ShapeDtypeStruct((1, wide), F32),
                   jax.ShapeDtypeStruct(w_up.shape, BF16), jax.ShapeDtypeStruct(w_glu.shape, BF16)],
        scratch_shapes=[pltpu.VMEM((sw, wide), F32), pltpu.VMEM((sw, wide), F32)],
        compiler_params=_params("arbitrary"))(dy, w_up, gv, b_glu, w_glu, sg, ge, *deps)


_FULL = lambda i, j: (i, j)
_ROWV = lambda i, j: (0, j)


def _gates_bwd_epilogue(y_pool, y_ssm, uin, t, d, pw):
    cb = (2 * pw) // d

    def epi(dm, yp, ys, glp, gls):
        sp, ss = _sigmoid(glp.astype(F32)), _sigmoid(gls.astype(F32))
        dgl = jnp.concatenate([dm * yp.astype(F32) * sp * (1.0 - sp), dm * ys.astype(F32) * ss * (1.0 - ss)], axis=1)
        return dm * sp, dm * ss, ("at", 2 * pw, dgl)

    ins = [(y_pool, (ROWS, d), _FULL), (y_ssm, (ROWS, d), _FULL),
           (uin, (ROWS, d), lambda i, j: (i, cb)), (uin, (ROWS, d), lambda i, j: (i, cb + 1))]
    wide = 2 * pw + 2 * d
    return epi, ins, [((t, d), BF16, (ROWS, d), _FULL)] * 2 + [((t, wide), BF16, (ROWS, wide), _FULL)]


def _loss_epilogue(x_in, gt, tgt, g_final, t, d):
    blk = (ROWS, d)
    full = lambda i, j: (i, j)
    rowv = lambda i, j: (0, j)

    def epi(acc, res, scale, tv, gv):
        xv = res + (0.5 * scale) * acc
        r = lax.rsqrt(jnp.mean(xv * xv, axis=-1, keepdims=True) + EPS)
        xr = xv * r
        e = xr * gv - tv
        loss_row = 0.5 * jnp.mean(e * e, axis=-1, keepdims=True)
        dout = e * (1.0 / d)
        gd = gv * dout
        dx = r * (gd - xr * jnp.mean(gd * xr, axis=-1, keepdims=True))
        fdx = 0.5 * dx
        return [dx, scale * fdx], [_colsum(dout * xr), _colsum(loss_row * jnp.ones((1, 128), F32)), _colsum(fdx * acc)]

    ins = [(x_in, blk, full), (gt, (1, d), rowv), (tgt, blk, full), (g_final, (1, d), rowv)]
    return epi, ins, [((t, d), F32, blk, full), ((t, d), BF16, blk, full)], [d, 128, d]


def _norm_bwd_epilogue(x, dres, g, sc, up, t, d):
    blk = (ROWS, d)
    full = lambda i, j: (i, j)
    rowv = lambda i, j: (0, j)

    def epi(dhv, xv, drv, gv, scv, *rest):
        r = lax.rsqrt(jnp.mean(xv * xv, axis=-1, keepdims=True) + EPS)
        xr = xv * r
        dn = dhv * (1.0 + scv)
        gd = gv * dn
        dx = drv + r * (gd - xr * jnp.mean(gd * xr, axis=-1, keepdims=True))
        outs, reds = [dx], [_colsum(dhv), _colsum(dhv * xr * gv), _colsum(dn * xr)]
        if up:
            yv, gtv = rest
            fdx = up[2] * dx
            outs.append(gtv * fdx)
            reds.append(_colsum(fdx * yv.astype(F32)))
        return outs, reds

    ins = [(x, blk, full), (dres, blk, full), (g, (1, d), rowv), (sc, (1, d), rowv)]
    ins += [(up[0], blk, full), (up[1], (1, d), rowv)] if up else []
    outs = [((t, d), F32, blk, full)] + ([((t, d), BF16, blk, full)] if up else [])
    return epi, ins, outs, [d] * (4 if up else 3)


def _ffn_in_act(name, h, w_in, tm=ROWS):
    t, d = h.shape
    ns = w_in.shape[2]
    tm = _pick(t, tm, 8)
    w4 = w_in.reshape(2, 2, d, ns)

    def body(h_ref, w_ref, ab_ref, act_ref):
        hv = h_ref[...]
        a = jnp.dot(hv, w_ref[0], preferred_element_type=F32)
        b = jnp.dot(hv, w_ref[1], preferred_element_type=F32)
        ab_ref[0] = a.astype(BF16)
        ab_ref[1] = b.astype(BF16)
        act_ref[...] = (a * _sigmoid(a) * b).astype(BF16)

    return _pcall(body, name=name, grid=(2, t // tm),
                  in_specs=[pl.BlockSpec((tm, d), lambda c, i: (i, 0)), pl.BlockSpec((2, None, d, ns), lambda c, i: (0, c, 0, 0))],
                  out_specs=[pl.BlockSpec((2, tm, ns), lambda c, i: (0, i, c)), pl.BlockSpec((tm, ns), lambda c, i: (i, c))],
                  out_shape=[jax.ShapeDtypeStruct((2, t, 2 * ns), BF16), jax.ShapeDtypeStruct((t, 2 * ns), BF16)],
                  compiler_params=_params("parallel", "parallel"))(h, w4)


def _dswiglu_epilogue(ab, t, f, tn):
    blk = (2, ROWS, tn)
    idx = lambda i, j: (0, i, j)

    def epi(dact, abv):
        a, b = abv[0].astype(F32), abv[1].astype(F32)
        s = _sigmoid(a)
        return (jnp.stack([dact * b * (s * (1.0 + a * (1.0 - s))), dact * (a * s)]),)

    return epi, [(ab, blk, idx)], [((2, t, f), BF16, blk, idx)]


def _ffn_fwd(tag, x, h, gt, get_w_in, get_w_out, nxt=None, loss=None):
    t, d = x.shape
    ab, act = _ffn_in_act(tag + "_in", h, get_w_in(h))
    if loss:
        epi, epi_ins, epi_outs, epi_reds = _loss_epilogue(x, gt, *loss, t, d)
        res = _mm(tag + "_out", act, get_w_out(act), tm=ROWS, tn=d, epi=epi, epi_ins=epi_ins, epi_outs=epi_outs,
                  epi_reds=epi_reds, b_resident=True)
        return res, (x, h, ab, act, None)
    epi, epi_ins, epi_outs = _gate_norm_epilogue(0.5, x, gt, nxt, t, d)
    res = _mm(tag + "_out", act, get_w_out(act), tm=ROWS, tn=d, epi=epi, epi_ins=epi_ins, epi_outs=epi_outs, b_resident=True)
    return res[0], res[2], (x, h, ab, act, res[1])


def _ffn_bwd(tag, dx_new, dy, saved, g, sc, w_in, w_out, start_rs, up, deps=()):
    x, h, ab, act, _ = saved
    d = x.shape[1]
    f = act.shape[1]
    dw_out = _mm(tag + "_dwout", act, dy, ta=True, tm=1408, tn=d, tk=2048, deps=deps)
    tok = start_rs("out", dw_out.reshape(N_CHIPS, f // N_CHIPS, d))
    epi, epi_ins, epi_outs = _dswiglu_epilogue(ab, x.shape[0], f, w_in.shape[2])
    dab = _mm(tag + "_dact", dy, w_out, tb=True, tm=ROWS, tn=w_in.shape[2], epi=epi, epi_ins=epi_ins, epi_outs=epi_outs,
              deps=(tok,), j_outer=True)
    dw_in = _mm(tag + "_dwin", h, dab, ta=True, out_stacked=True, b_halves=True, tm=d, tn=1408, tk=2048)
    tok = start_rs("in", dw_in)
    epi, epi_ins, epi_outs, epi_reds = _norm_bwd_epilogue(x, dx_new, g, sc, up, x.shape[0], d)
    res = _mm(tag + "_dh", dab, w_in, tb=True, b_stacked=True, a_halves=True, tn=d, tk=5632, deps=(tok,), epi=epi,
              epi_ins=epi_ins, epi_outs=epi_outs, epi_reds=epi_reds, b_resident=True)
    if up:
        return res
    return res[0], None, res[1], res[2], res[3], None


def _row(v):
    return v.reshape(1, -1)


def _pack(parts):
    cols = []
    for p in parts:
        flat = p.reshape(-1).astype(F32)
        padn = (-flat.shape[0]) % 128
        cols.append(jnp.pad(flat, (0, padn)) if padn else flat)
    flat = jnp.concatenate(cols)
    padn = (-flat.shape[0]) % 1024
    if padn:
        flat = jnp.pad(flat, (0, padn))
    return flat.reshape(-1, 128)


def _unpack(packed, shapes):
    flat = packed.reshape(-1)
    out, off = [], 0
    for s in shapes:
        n = math.prod(s)
        out.append(flat[off:off + n].reshape(s))
        off += n + ((-n) % 128)
    return out


SMALL = ['b_ada', 'g_ffn1', 'g_mix', 'pool_w', 'pool_b', 'pool_scale', 'ssm_lam_re_log', 'ssm_lam_im', 'ssm_log_dt',
         'ssm_b_re', 'ssm_b_im', 'ssm_c_re', 'ssm_c_im', 'ssm_d', 'b_glu', 'g_ffn2', 'g_final']
BIG = ['w_ffn1_in', 'w_ffn1_out', 'w_in', 'w_pool_up', 'w_glu', 'w_ssm_up', 'w_out', 'w_ffn2_in', 'w_ffn2_out']
AG_GROUPS = [[0], [1], [2, 3, 4, 5, 6], [7, 8]]
AG_DIRECT = [False, True, False, True]
SMALL_LATE = ['b_ada', 'g_ffn1']
RS_SUM_GROUPS = [[0, 1], [2, 3], [4, 5]]
WEIGHTS = ['w_ada', 'b_ada', 'g_ffn1', 'w_ffn1_in', 'w_ffn1_out', 'g_mix', 'w_in', 'pool_w', 'pool_b', 'pool_scale', 'w_pool_up',
           'ssm_lam_re_log', 'ssm_lam_im', 'ssm_log_dt', 'ssm_b_re', 'ssm_b_im', 'ssm_c_re', 'ssm_c_im', 'ssm_d', 'w_glu', 'b_glu',
           'w_ssm_up', 'w_out', 'g_ffn2', 'w_ffn2_in', 'w_ffn2_out', 'g_final']


def kernel(x, c, w_ada, b_ada, g_ffn1, w_ffn1_in, w_ffn1_out, g_mix, w_in, pool_w, pool_b, pool_scale, w_pool_up, ssm_lam_re_log, ssm_lam_im, ssm_log_dt, ssm_b_re, ssm_b_im, ssm_c_re, ssm_c_im, ssm_d, w_glu, b_glu, w_ssm_up, w_out, g_ffn2, w_ffn2_in, w_ffn2_out, g_final, loss_target, m_w_ada, m_b_ada, m_g_ffn1, m_w_ffn1_in, m_w_ffn1_out, m_g_mix, m_w_in, m_pool_w, m_pool_b, m_pool_scale, m_w_pool_up, m_ssm_lam_re_log, m_ssm_lam_im, m_ssm_log_dt, m_ssm_b_re, m_ssm_b_im, m_ssm_c_re, m_ssm_c_im, m_ssm_d, m_w_glu, m_b_glu, m_w_ssm_up, m_w_out, m_g_ffn2, m_w_ffn2_in, m_w_ffn2_out, m_g_final, v_w_ada, v_b_ada, v_g_ffn1, v_w_ffn1_in, v_w_ffn1_out, v_g_mix, v_w_in, v_pool_w, v_pool_b, v_pool_scale, v_w_pool_up, v_ssm_lam_re_log, v_ssm_lam_im, v_ssm_log_dt, v_ssm_b_re, v_ssm_b_im, v_ssm_c_re, v_ssm_c_im, v_ssm_d, v_w_glu, v_b_glu, v_w_ssm_up, v_w_out, v_g_ffn2, v_w_ffn2_in, v_w_ffn2_out, v_g_final):
    args = dict(locals())
    wt = {n: args[n] for n in WEIGHTS}
    mom = {n: args["m_" + n] for n in WEIGHTS}
    var = {n: args["v_" + n] for n in WEIGHTS}

    t, d = x.shape[1], x.shape[2]
    pw = pool_b.shape[1]
    sw = ssm_d.shape[1]
    ngrp = sw // SSM_GROUP
    gn = ngrp * SSM_STATE
    xi, yi, ci = _place()
    b_me = 4 * xi + 2 * yi + ci
    s_me = 2 * xi + yi
    x2d = x[0]
    tgt = loss_target[0]

    c_all = _allgather_small("ag_c", c.reshape(8, d // 8)).reshape(N_DEV, d)
    ncol = w_ada.shape[2]
    b_sh = lax.dynamic_slice(b_ada, (0, s_me * ncol), (1, ncol))
    mod_sh = _ada_fwd("ada_fwd", c_all, w_ada[0], b_sh)
    md_send, md_recv, mod_sh, md_land, tok = _bcast_start("mod_start", mod_sh)

    shards = [wt[n][0].astype(BF16) for n in BIG]
    lands = _place_shards(shards, s_me)
    n0 = len(AG_GROUPS[0])
    sems_a, shards_a, lands_a, tok = _ag_start("ag_start_first", shards[:n0], lands[:n0], AG_GROUPS[:1], AG_DIRECT[:1], deps=(tok,))
    rest = [[w - n0 for w in grp] for grp in AG_GROUPS[1:]]
    sems_b, shards_b, lands_b, tok = _ag_start("ag_start_rest", shards[n0:], lands[n0:], rest, AG_DIRECT[1:], deps=(tok,))
    ag_sems, shards_t, lands_t = sems_a + sems_b, shards_a + shards_b, lands_a + lands_b
    md_land = _bcast_wait("mod_wait", mod_sh, md_land, md_send, md_recv, tok)
    mod_all = lax.dynamic_update_slice(md_land, mod_sh[None], (b_me, 0, 0))
    full = {}

    def weights(gi, *after):
        grp = AG_GROUPS[gi]
        if BIG[grp[0]] not in full:
            ls = _ag_wait("ag_wait%d" % gi, [shards_t[w] for w in grp], [lands_t[w] for w in grp], *ag_sems[gi],
                          AG_DIRECT[gi], after)
            for w, l in zip(grp, ls if AG_DIRECT[gi] else _ag_forward("ag_fwd%d" % gi, ls)):
                full[BIG[w]] = l
        return full

    mod_me = jnp.concatenate([lax.dynamic_slice(mod_all, (2 * s, b_me, 0), (1, 1, ncol))[0] for s in range(N_CHIPS)], axis=1)
    mod = mod_me.reshape(9, d)
    sh1, sc1, gt1, sh2, sc2, gt2, sh3, sc3, gt3 = [mod[k:k + 1] for k in range(9)]

    f = w_ffn1_out.shape[1] * N_CHIPS

    col = lambda a: a.reshape(gn, 1)
    lrl_c, li_c = col(ssm_lam_re_log), col(ssm_lam_im)
    ldt_c = col(jnp.broadcast_to(ssm_log_dt.reshape(ngrp, 1), (ngrp, SSM_STATE)))
    b_re2, b_im2 = ssm_b_re.reshape(gn, SSM_GROUP), ssm_b_im.reshape(gn, SSM_GROUP)
    lrdt_c, ang_c, bb_re, bb_im = _ssm_prep("ssm_prep", lrl_c, li_c, ldt_c, b_re2, b_im2)
    lrdt, ang = lrdt_c.reshape(1, gn), ang_c.reshape(1, gn)
    tr = lambda a: jnp.swapaxes(a, 1, 2)
    bbd = [_blockdiag_b(v, sw).astype(BF16) for v in (bb_re, bb_im)]
    ccd = [_blockdiag_c(v[0], sw).astype(BF16) for v in (ssm_c_re, ssm_c_im)]
    bbd_t, ccd_t = [tr(v) for v in bbd], [tr(v) for v in ccd]
    early = [n for n in SMALL if n not in SMALL_LATE]
    packs = {}
    for tag, names, extra in (("early", early, []), ("late", SMALL_LATE, [jnp.zeros((128,), F32)])):
        packs[tag] = [_pack([src[n] for n in names] + extra) for src in (wt, mom, var)]
    shadow = [lrdt, ang, ldt_c, *bbd, *ccd, *bbd_t, *ccd_t, *packs["early"], *packs["late"]]

    h1 = _norm_fwd("ffn1_norm", x2d, g_ffn1, sc1, sh1, deps=(tok,))
    x1, h2, sav1 = _ffn_fwd("ffn1", x2d, h1, gt1, lambda after: weights(0, after, *shadow)['w_ffn1_in'],
                            lambda after: weights(1, after)['w_ffn1_out'].reshape(f, d), (g_mix, sc2, sh2))
    weights(2, h2)
    wo = full['w_out'].reshape(d, d)
    uin = _mm("mix_in", h2, full['w_in'], b_stacked=True, tm=1024, tn=768, j_outer=True)
    p_pool, z_pool, y_pool = _pool_fwd("pool_fwd", uin, pool_w[0], pool_b, pool_scale, full['w_pool_up'], pw)
    y_s, ge, s_re, s_im = _ssm_fwd("ssm_fwd", uin, lrdt, ang, *bbd, *ccd, ssm_d, sw)
    gv, sg, y_ssm, merged, y2, h3, x2 = _mixer_tail("mixer_tail", ge, full['w_glu'], b_glu, full['w_ssm_up'], y_pool, uin, wo,
                                                    x1, gt2, (g_ffn2, sc3, sh3), d, pw)

    (dx3, dy3, dg_final, loss_v, dgt3), sav3 = _ffn_fwd(
        "ffn2", x2, h3, gt3, lambda after: weights(3, after)['w_ffn2_in'],
        lambda after: weights(3, after)['w_ffn2_out'].reshape(f, d), loss=(tgt, _row(g_final)))

    gs = {}
    rs_open = []

    def start_rs(names, gbs):
        send, recv, g_thru, land_thru, token = _rs_start("rs_start_" + names[0], gbs)
        rs_open.append((names, send, recv, g_thru, land_thru))
        return token

    dx2, dy2, dsh3, dsc3, gs['g_ffn2'], dgt2 = _ffn_bwd(
        "ffn2b", dx3, dy3, sav3, g_ffn2, sc3, full['w_ffn2_in'], full['w_ffn2_out'].reshape(f, d),
        lambda key, g: start_rs(['w_ffn2_' + key], [g]), (y2, gt2, 1.0))

    epi, epi_ins, epi_outs = _gates_bwd_epilogue(y_pool, y_ssm, uin, t, d, pw)
    dy_pool, dy_ssm, duin, g_wo = _mm("mix_dmerged", dy2, wo, tb=True, tm=ROWS, tn=d, epi=epi, epi_ins=epi_ins,
                                      epi_outs=epi_outs, b_resident=True, wgrad=merged, wgrad_rows=True)
    g_wo = g_wo.reshape(N_CHIPS, d // N_CHIPS, d)

    duin, gs['pool_w'], gs['pool_b'], gs['pool_scale'], g_wpu = _pool_bwd(
        "pool_bwd", dy_pool, z_pool, p_pool, pool_w[0], pool_b, pool_scale, full['w_pool_up'], duin)

    tok = start_rs(['w_out', 'w_pool_up'], [g_wo, g_wpu])
    dge, gs['b_glu'], g_wsu, g_wglu = _glu_up_bwd("glu_up_bwd", dy_ssm, full['w_ssm_up'], gv, b_glu, full['w_glu'], sg, ge,
                                                  deps=(tok,))
    (duin, g_abre, g_abim, g_bbd_re, g_bbd_im, g_ccd_re, g_ccd_im, gs['ssm_d']) = _ssm_bwd(
        "ssm_bwd", dge, y_s, uin, s_re, s_im, lrdt, ang, *bbd_t, *ccd_t, ssm_d, sw, duin)
    gs['ssm_c_re'], gs['ssm_c_im'] = _diag_of_c(g_ccd_re, sw), _diag_of_c(g_ccd_im, sw)
    d_lrl, d_li, d_ldt, d_bre, d_bim = _ssm_param_bwd(
        "ssm_param_bwd", lrl_c, li_c, ldt_c, b_re2, b_im2, g_abre.reshape(gn, 1), g_abim.reshape(gn, 1),
        _diag_of_b(g_bbd_re, sw), _diag_of_b(g_bbd_im, sw))
    gs['ssm_lam_re_log'], gs['ssm_lam_im'] = d_lrl, d_li
    gs['ssm_log_dt'] = jnp.sum(d_ldt.reshape(ngrp, SSM_STATE), axis=1)
    gs['ssm_b_re'], gs['ssm_b_im'] = d_bre, d_bim

    g_win =_mm("mix_dwin", h2, duin, ta=True, out_stacked=True, tm=d, tn=768, tk=4096)
    tok = start_rs(['w_ssm_up', 'w_glu', 'w_in'], [g_wsu, g_wglu, g_win])
    epi, epi_ins, epi_outs, epi_reds = _norm_bwd_epilogue(x1, dx2, g_mix, sc2, (sav1[4], gt1, 0.5), t, d)
    dx1, dy1, dsh2, dsc2, gs['g_mix'], dgt1 = _mm(
        "mix_dh", duin, full['w_in'], tb=True, b_stacked=True, tn=d, tk=3072, deps=(tok,), epi=epi, epi_ins=epi_ins,
        epi_outs=epi_outs, epi_reds=epi_reds, b_resident=True)

    gs['g_final'] = dg_final
    sg_blk = _pack([gs[n] for n in early])
    sg_send, sg_recv, sg_blk, sg_land, tok = _bcast_start("sg_start", sg_blk)

    dx0, _, dsh1, dsc1, gs['g_ffn1'], _ = _ffn_bwd(
        "ffn1b", dx1, dy1, sav1, g_ffn1, sc1, full['w_ffn1_in'], full['w_ffn1_out'].reshape(f, d),
        lambda key, g: start_rs(['w_ffn1_' + key], [g]), None, deps=(tok,))

    gs['b_ada'] = jnp.concatenate([dsh1, dsc1, dgt1, dsh2, dsc2, dgt2, dsh3, dsc3, dgt3], axis=1)
    lt_blk = _pack([gs[n] for n in SMALL_LATE] + [loss_v])
    lt_send, lt_recv, lt_blk, lt_land, tok = _bcast_start("late_start", lt_blk)

    grads, delta, new_m, new_v = {}, {}, {}, {}

    def small_update(tag, names, blk, land):
        g8 = lax.dynamic_update_slice(land, blk[None], (b_me, 0, 0)).reshape(-1, 128)
        w_pack, m_pack, v_pack = packs[tag]
        shapes = [wt[n].shape for n in names]
        per_param, packed = _adamw_small("adamw_small_" + tag, w_pack, g8, m_pack, v_pack, shapes)
        for k, dst in enumerate((grads, delta, new_m, new_v)):
            rest = _unpack(packed[k], shapes)
            for i, n in enumerate(names):
                dst[n] = per_param[i][k] if per_param[i] is not None else rest[i]
        return g8

    after = tok
    for group in RS_SUM_GROUPS:
        names, g_done, land_done = [], [], []
        for k in group:
            nk, send, recv, g_thru, land_thru = rs_open[k]
            gd, ld = _rs_wait("rs_wait_" + nk[0], g_thru, land_thru, send, recv, after)
            names, g_done, land_done = names + nk, g_done + gd, land_done + ld
        for n, g_sum in zip(names, _rs_sum("rs_sum_" + names[0], g_done, land_done)):
            g_out, dl, mn, vn = _adamw("adamw_" + n, wt[n][0], g_sum, mom[n][0], var[n][0])
            grads[n], delta[n], new_m[n], new_v[n] = g_out[None], dl[None], mn[None], vn[None]
            after = dl
        if group is RS_SUM_GROUPS[-2]:
            small_update("early", early, sg_blk, _bcast_wait("sg_wait", sg_blk, sg_land, sg_send, sg_recv, after))

    lt_land = _bcast_wait("late_wait", lt_blk, lt_land, lt_send, lt_recv, after)
    late8 = small_update("late", SMALL_LATE, lt_blk, lt_land).reshape(N_DEV, -1)
    loss = jnp.sum(late8[:, 10 * d])
    dmod_sh = lax.dynamic_slice(late8, (0, s_me * ncol), (N_DEV, ncol))
    g_w_ada = _ada_bwd("ada_bwd", c_all, dmod_sh)
    dl, mn, vn = _adamw("adamw_w_ada", w_ada[0], g_w_ada, m_w_ada[0], v_w_ada[0], with_g=False)
    grads['w_ada'], delta['w_ada'], new_m['w_ada'], new_v['w_ada'] = g_w_ada[None], dl[None], mn[None], vn[None]

    return (loss, dx0[None], *[grads[n] for n in WEIGHTS], *[delta[n] for n in WEIGHTS],
            *[new_m[n] for n in WEIGHTS], *[new_v[n] for n in WEIGHTS])
```

```python
import functools
import math

import jax
import jax.numpy as jnp
from jax import lax
from jax.experimental import pallas as pl
from jax.experimental.pallas import tpu as pltpu

F32 = jnp.float32
BF16 = jnp.bfloat16
MESH = pl.DeviceIdType.MESH

EPS = 1e-6
POOL_WINDOWS = (2, 4, 8, 16)
POOL_HALO = 16
SSM_GROUP = 16
SSM_STATE = 64
SSM_BLOCKS = 4
N_DEV = 8
N_CHIPS = 4
ADAM_LR = 0.001
ADAM_B1 = 0.9
ADAM_B2 = 0.999
ADAM_EPS = 1e-08
ADAM_WD = 0.01
ADAM_STEP = 10
VMEM_LIMIT = 56 * 1024 * 1024
ROWS = 512


ANY_SPEC = pl.BlockSpec(memory_space=pl.ANY)
HBM_SPEC = pl.BlockSpec(memory_space=pltpu.HBM)
SEM_SPEC = pl.BlockSpec(memory_space=pltpu.SEMAPHORE)
EFFECT = pltpu.SideEffectType.DATAFLOW_SIDE_EFFECTING


def _hbm(a):
    return pltpu.with_memory_space_constraint(a, pltpu.HBM)


def _pcall(body, **kw):
    return pl.pallas_call(body, **kw)


def _params(*sem):
    return pltpu.CompilerParams(dimension_semantics=sem, vmem_limit_bytes=VMEM_LIMIT)


def _pick(n, cap, mult=128):
    if n <= cap:
        return n
    best = None
    for d in range(mult, cap + 1, mult):
        if n % d == 0:
            best = d
    assert best is not None, (n, cap, mult)
    return best


def _sigmoid(v):
    return 1.0 / (1.0 + jnp.exp(-v))


def _rowwise(name, fn, ins, params, outs, reds, tm, deps=()):
    t = ins[0][0].shape[0]
    tm = min(tm, t)
    nb = t // tm
    ni, npar, no, nd = len(ins), len(params), len(outs), len(deps)

    def body(*refs):
        iv = [r[...] for r in refs[:ni]]
        pv = [r[...] for r in refs[ni:ni + npar]]
        o_refs = refs[ni + npar + nd:ni + npar + nd + no]
        r_refs = refs[ni + npar + nd + no:]
        ovals, rvals = fn(iv, pv)
        for o_ref, val in zip(o_refs, ovals):
            off = 0
            if isinstance(val, tuple) and val[0] == "at":
                _, off, val = val
            parts = val if isinstance(val, (list, tuple)) else [val]
            for p in parts:
                o_ref[:, off:off + p.shape[1]] = p.astype(o_ref.dtype)
                off += p.shape[1]
        if r_refs:
            @pl.when(pl.program_id(0) == 0)
            def _():
                for r in r_refs:
                    r[...] = jnp.zeros_like(r)
            for r, val in zip(r_refs, rvals):
                r[...] += val

    in_specs = [pl.BlockSpec((tm, w), functools.partial(lambda i, cb: (i, cb), cb=cb)) for (_, w, cb) in ins]
    in_specs += [pl.BlockSpec(p.shape, lambda i: (0, 0)) for p in params]
    in_specs += [ANY_SPEC] * nd
    out_shape = [jax.ShapeDtypeStruct((t, w), dt) for (w, dt) in outs]
    out_shape += [jax.ShapeDtypeStruct((1, w), F32) for w in reds]
    out_specs = [pl.BlockSpec((tm, w), lambda i: (i, 0)) for (w, _) in outs]
    out_specs += [pl.BlockSpec((1, w), lambda i: (0, 0)) for w in reds]
    res = _pcall(body, name=name, grid=(nb,), in_specs=in_specs, out_specs=out_specs, out_shape=out_shape,
                 compiler_params=_params("arbitrary"))(*[a for a, _, _ in ins], *params, *deps)
    return res


def _colsum(v):
    return jnp.sum(v, axis=0, keepdims=True)


def _mm(name, a, b, *, ta=False, tb=False, b_stacked=False, out_stacked=False, tm=ROWS, tn=1024, tk=2816,
        out_dtype=BF16, epi=None, epi_ins=(), epi_outs=None, epi_reds=(), deps=(), j_outer=False, a_halves=False,
        b_halves=False, b_resident=False, wgrad=None, wgrad_rows=False):
    if a_halves:
        _, m, kdim = a.shape
        kdim *= 2
    elif ta:
        kdim, m = a.shape
    else:
        m, kdim = a.shape
    ns = None
    if b_stacked:
        ns = b.shape[2]
        n = b.shape[1] if tb else N_CHIPS * ns
        assert kdim == (N_CHIPS * ns if tb else b.shape[1]), (name, a.shape, b.shape)
    else:
        if b_halves:
            n = 2 * b.shape[2]
            assert kdim == b.shape[1] and not tb, (name, a.shape, b.shape)
        else:
            n = b.shape[0] if tb else b.shape[1]
            assert kdim == (b.shape[1] if tb else b.shape[0]), (name, a.shape, b.shape)
        if out_stacked:
            ns = n // N_CHIPS

    def shards(want):
        return max(g for g in (1, 2, 4) if g * ns <= max(want, ns))

    tm = _pick(m, tm, 128 if ta else 8)
    gn = gk = 1
    if (b_stacked and not tb) or out_stacked:
        gn = shards(min(tn, n // 2) if b_halves else tn)
        tn = gn * ns
    else:
        tn = _pick(n, tn)
    if b_stacked and tb:
        gk = shards(tk)
        tk = gk * ns
    else:
        tk = _pick(kdim, tk, 8 if ta else 128)
    nm, nn, nk = m // tm, n // tn, kdim // tk

    def ij(f):
        return (lambda g0, g1, k: f(g1, g0, k)) if j_outer else f

    if a_halves:
        assert b_stacked and tb and gk == N_CHIPS and nk == 1, name
        a_spec = pl.BlockSpec((2, tm, kdim // 2), ij(lambda i, j, k: (0, i, 0)))
    elif ta:
        a_spec = pl.BlockSpec((tk, tm), ij(lambda i, j, k: (k, i)))
    else:
        a_spec = pl.BlockSpec((tm, tk), ij(lambda i, j, k: (i, k)))
    if b_stacked and not tb:
        b_spec = pl.BlockSpec((gn, tk, ns), ij(lambda i, j, k: (j, k, 0)))
    elif b_stacked and tb:
        b_spec = pl.BlockSpec((gk, tn, ns), ij(lambda i, j, k: (k, j, 0)))
    elif b_halves:
        bph = (n // 2) // tn
        b_spec = pl.BlockSpec((None, tk, tn), ij(lambda i, j, k: (j // bph, k, j % bph)))
    elif tb:
        b_spec = pl.BlockSpec((tn, tk), ij(lambda i, j, k: (j, k)))
    else:
        b_spec = pl.BlockSpec((tk, tn), ij(lambda i, j, k: (k, j)))
    dims = (((0 if ta else 1,), (1 if tb else 0,)), ((), ()))

    if epi_outs is None:
        if out_stacked:
            epi_outs = [((N_CHIPS, m, ns), out_dtype, (gn, tm, ns), lambda i, j: (j, i, 0))]
        else:
            epi_outs = [((m, n), out_dtype, (tm, tn), lambda i, j: (i, j))]
    ne, no, nd, nr = len(epi_ins), len(epi_outs), len(deps), len(epi_reds)
    nx = 0 if wgrad is None else 1
    assert not (nr or nx) or (nn == 1 and not j_outer), name
    assert not nx or (nk == 1 and not ta and not a_halves and kdim % N_CHIPS == 0), name

    def body(a_ref, b_ref, *rest):
        e_refs = rest[:ne]
        x_refs = rest[ne:ne + nx]
        o_refs = rest[ne + nx + nd:ne + nx + nd + no]
        r_refs = rest[ne + nx + nd + no:ne + nx + nd + no + nr]
        wg_refs = rest[ne + nx + nd + no + nr:ne + nx + nd + no + nr + nx]
        scratch = rest[ne + nx + nd + no + nr + nx:]
        av = None if a_halves else a_ref[...].astype(BF16)
        if nx:
            wg_acc = scratch[-1]
            i = pl.program_id(0)
            pw_ = lax.dot_general(x_refs[0][...].astype(BF16), av, (((0,), (0,)), ((), ())), preferred_element_type=F32)

            @pl.when(i == 0)
            def _():
                wg_acc[...] = pw_

            @pl.when(i > 0)
            def _():
                wg_acc[...] += pw_

            @pl.when(i == nm - 1)
            def _():
                if wgrad_rows:
                    wg_refs[0][...] = wg_acc[...].astype(BF16)
                else:
                    ks = kdim // N_CHIPS
                    for s in range(N_CHIPS):
                        wg_refs[0][s] = wg_acc[:, s * ks:(s + 1) * ks].astype(BF16)
        if b_stacked and not tb:
            parts = [lax.dot_general(av, b_ref[s].astype(BF16), dims, preferred_element_type=F32) for s in range(gn)]
        elif b_stacked and tb:
            p = None
            for s in range(gk):
                if a_halves:
                    a_s = a_ref[s // 2, :, (s % 2) * ns:(s % 2 + 1) * ns].astype(BF16)
                else:
                    a_s = av[:, s * ns:(s + 1) * ns]
                q = lax.dot_general(a_s, b_ref[s].astype(BF16), dims, preferred_element_type=F32)
                p = q if p is None else p + q
            parts = [p]
        else:
            parts = [lax.dot_general(av, b_ref[...].astype(BF16), dims, preferred_element_type=F32)]

        def finish(acc_parts):
            if epi is None and out_stacked:
                acc = acc_parts[0]
                for s in range(gn):
                    o_refs[0][s] = acc[:, s * ns:(s + 1) * ns].astype(out_dtype)
            elif epi is None:
                w = acc_parts[0].shape[1]
                for s, part in enumerate(acc_parts):
                    o_refs[0][:, s * w:(s + 1) * w] = part.astype(out_dtype)
            else:
                acc = acc_parts[0] if len(acc_parts) == 1 else jnp.concatenate(acc_parts, axis=1)
                vals = epi(acc, *[r[...] for r in e_refs])
                if nr:
                    vals, reds = vals

                    @pl.when(pl.program_id(0) == 0)
                    def _():
                        for r in r_refs:
                            r[...] = jnp.zeros_like(r)
                    for r, val in zip(r_refs, reds):
                        r[...] += val
                for o_ref, val in zip(o_refs, vals):
                    if isinstance(val, tuple) and val[0] == "at":
                        o_ref[:, val[1]:val[1] + val[2].shape[1]] = val[2].astype(o_ref.dtype)
                    else:
                        o_ref[...] = val.astype(o_ref.dtype)

        if nk == 1:
            finish(parts)
        else:
            acc_ref = scratch[0]
            k = pl.program_id(2)
            w = parts[0].shape[1]

            @pl.when(k == 0)
            def _():
                for s, part in enumerate(parts):
                    acc_ref[:, s * w:(s + 1) * w] = part

            @pl.when(k > 0)
            def _():
                for s, part in enumerate(parts):
                    acc_ref[:, s * w:(s + 1) * w] += part

            @pl.when(k == nk - 1)
            def _():
                finish([acc_ref[...]])

    def _ij(f):
        return ij(lambda i, j, k: f(i, j))

    if b_resident:
        assert nk == 1 and (nn == 1 or j_outer), name
        b_spec = pl.BlockSpec(b_spec.block_shape, b_spec.index_map, pipeline_mode=pl.Buffered(1))
    in_specs = [a_spec, b_spec] + [pl.BlockSpec(blk, _ij(f)) for (_, blk, f) in epi_ins]
    if nx:
        in_specs.append(pl.BlockSpec((tm, wgrad.shape[1]), lambda i, j, k: (i, 0)))
    in_specs += [ANY_SPEC] * nd
    out_specs = [pl.BlockSpec(blk, _ij(f)) for (_, _, blk, f) in epi_outs]
    out_specs += [pl.BlockSpec((1, w), lambda *_: (0, 0)) for w in epi_reds]
    out_shape = [jax.ShapeDtypeStruct(s, dt) for (s, dt, _, _) in epi_outs] + [jax.ShapeDtypeStruct((1, w), F32) for w in epi_reds]
    scratch = [pltpu.VMEM((tm, tn), F32)] if nk > 1 else []
    if nx:
        wg_shape = (wgrad.shape[1], kdim) if wgrad_rows else (N_CHIPS, wgrad.shape[1], kdim // N_CHIPS)
        out_specs.append(pl.BlockSpec(wg_shape, lambda *_: (0,) * len(wg_shape)))
        out_shape.append(jax.ShapeDtypeStruct(wg_shape, BF16))
        scratch.append(pltpu.VMEM((wgrad.shape[1], kdim), F32))
    grid = (nn, nm, nk) if j_outer else (nm, nn, nk)
    sem = ("arbitrary",) * 3 if (nr or nx) else ("parallel", "parallel", "arbitrary")
    res = _pcall(body, name=name, grid=grid, in_specs=in_specs, out_specs=out_specs, out_shape=out_shape, scratch_shapes=scratch,
                 compiler_params=_params(*sem))(a, b, *[x for x, _, _ in epi_ins], *([wgrad] * nx), *deps)
    return res[0] if len(res) == 1 else res


def _norm_fwd(name, x, g, sc, sh, deps=()):
    d = x.shape[1]

    def fn(iv, pv):
        (xv,), (gv, scv, shv) = iv, pv
        r = lax.rsqrt(jnp.mean(xv * xv, axis=-1, keepdims=True) + EPS)
        return [xv * r * gv * (1.0 + scv) + shv], []

    return _rowwise(name, fn, [(x, d, 0)], [g, sc, sh], [(d, BF16)], [], ROWS, deps=deps)[0]


def _adamw(name, w, g, m, v, tm=256, with_g=True):
    c = w.shape[1]

    def fn(iv, pv):
        wv, gv, mv, vv = iv
        mn = ADAM_B1 * mv + (1.0 - ADAM_B1) * gv
        vn = ADAM_B2 * vv + (1.0 - ADAM_B2) * (gv * gv)
        m_hat = mn / (1.0 - ADAM_B1 ** ADAM_STEP)
        v_hat = vn / (1.0 - ADAM_B2 ** ADAM_STEP)
        delta = -ADAM_LR * (m_hat / (jnp.sqrt(v_hat) + ADAM_EPS) + ADAM_WD * wv)
        return ([gv] if with_g else []) + [delta, mn, vn], []

    return _rowwise(name, fn, [(w, c, 0), (g, c, 0), (m, c, 0), (v, c, 0)], [], [(c, F32)] * (4 if with_g else 3), [],
                    _pick(w.shape[0], tm, 8))


def _unpack_plan(shape):
    n = math.prod(shape)
    if len(shape) == 2 and shape[0] == 1 and n % 128 == 0:
        return [((slice(None), slice(128 * r, 128 * (r + 1))), slice(r, r + 1), slice(None)) for r in range(n // 128)]
    if len(shape) == 2 and shape[0] == 1 and n < 128:
        return [((slice(None), slice(None)), slice(0, 1), slice(0, n))]
    if len(shape) == 1 and n % 128 == 0:
        return [((slice(128 * r, 128 * (r + 1)),), r, slice(None)) for r in range(n // 128)]
    if len(shape) == 3 and shape[0] == 1 and shape[2] == 64:
        return [((0, slice(2 * r + h, 2 * r + h + 1), slice(None)), slice(r, r + 1), slice(64 * h, 64 * (h + 1)))
                for r in range(n // 128) for h in range(2)]
    if len(shape) == 4 and shape[0] == 1 and shape[2:] == (128, 128):
        return [((0, k), slice(128 * k, 128 * (k + 1)), slice(None)) for k in range(shape[1])]
    return None


def _adamw_small(name, w, g8, m, v, shapes):
    r = w.shape[0]
    plans, rows0, off = [], [], 0
    for s in shapes:
        plans.append(_unpack_plan(s))
        rows0.append(off // 128)
        off += math.prod(s) + (-math.prod(s)) % 128
    direct = [i for i, p in enumerate(plans) if p is not None]

    def body(w_ref, g_ref, m_ref, v_ref, *out):
        packed = out[4 * len(direct):]
        gv = g_ref[0:r, :]
        for k in range(1, N_DEV):
            gv = gv + g_ref[k * r:(k + 1) * r, :]
        mn = ADAM_B1 * m_ref[...] + (1.0 - ADAM_B1) * gv
        vn = ADAM_B2 * v_ref[...] + (1.0 - ADAM_B2) * (gv * gv)
        m_hat = mn / (1.0 - ADAM_B1 ** ADAM_STEP)
        v_hat = vn / (1.0 - ADAM_B2 ** ADAM_STEP)
        packed[0][...] = gv
        packed[1][...] = -ADAM_LR * (m_hat / (jnp.sqrt(v_hat) + ADAM_EPS) + ADAM_WD * w_ref[...])
        packed[2][...] = mn
        packed[3][...] = vn
        for di, i in enumerate(direct):
            for kind in range(4):
                o_ref, src = out[4 * di + kind], packed[kind]
                for o_idx, row, lanes in plans[i]:
                    row = (rows0[i] + row) if isinstance(row, int) else slice(rows0[i] + row.start, rows0[i] + row.stop)
                    o_ref[o_idx] = src[row, lanes]

    out_shape = [jax.ShapeDtypeStruct(shapes[i], F32) for i in direct for _ in range(4)]
    out_shape += [jax.ShapeDtypeStruct((r, 128), F32)] * 4
    res = _pcall(body, name=name, out_shape=out_shape,
                 compiler_params=pltpu.CompilerParams(vmem_limit_bytes=VMEM_LIMIT))(w, g8, m, v)
    per_param = [None] * len(shapes)
    for di, i in enumerate(direct):
        per_param[i] = res[4 * di:4 * di + 4]
    return per_param, res[4 * len(direct):]


def _pool_fwd(name, uin, pool_w, pool_b, pool_scale, w_up, pw, tm=ROWS):
    t = uin.shape[0]
    tm = min(tm, t)
    ng = len(POOL_WINDOWS)
    gw = pw // ng
    ns = w_up.shape[2]

    def body(u_ref, w_ref, b_ref, s_ref, wu_ref, p_ref, z_ref, y_ref, ext):
        i = pl.program_id(0)

        @pl.when(i == 0)
        def _():
            ext[0:POOL_HALO, :] = jnp.zeros((POOL_HALO, pw), F32)

        u = u_ref[...].astype(F32)
        ext[POOL_HALO:POOL_HALO + tm, :] = u
        pos = i * tm + lax.broadcasted_iota(jnp.int32, (tm, 1), 0)
        for k, win in enumerate(POOL_WINDOWS):
            cols = slice(k * gw, (k + 1) * gw)
            acc = u[:, cols]
            for j in range(1, win):
                acc = acc + ext[POOL_HALO - j:POOL_HALO - j + tm, cols]
            cnt = jnp.minimum(pos + 1, win).astype(F32)
            z = acc / cnt - u[:, cols]
            zp = jnp.dot(z.astype(BF16), w_ref[k].astype(BF16), preferred_element_type=F32) + b_ref[:, cols]
            p_ref[:, cols] = (zp * s_ref[:, cols]).astype(BF16)
            z_ref[:, cols] = z.astype(BF16)
        ext[0:POOL_HALO, :] = u[tm - POOL_HALO:tm, :]
        pv = p_ref[...]
        for s in range(N_CHIPS):
            y_ref[:, s * ns:(s + 1) * ns] = jnp.dot(pv, wu_ref[s], preferred_element_type=F32).astype(BF16)

    full3 = lambda i: (0, 0, 0)
    return _pcall(
        body, name=name, grid=(t // tm,),
        in_specs=[pl.BlockSpec((tm, pw), lambda i: (i, 0)), pl.BlockSpec(pool_w.shape, full3),
                  pl.BlockSpec(pool_b.shape, lambda i: (0, 0)), pl.BlockSpec(pool_scale.shape, lambda i: (0, 0)),
                  pl.BlockSpec(w_up.shape, full3)],
        out_specs=[pl.BlockSpec((tm, pw), lambda i: (i, 0))] * 2 + [pl.BlockSpec((tm, N_CHIPS * ns), lambda i: (i, 0))],
        out_shape=[jax.ShapeDtypeStruct((t, pw), BF16)] * 2 + [jax.ShapeDtypeStruct((t, N_CHIPS * ns), BF16)],
        scratch_shapes=[pltpu.VMEM((POOL_HALO + tm, pw), F32)],
        compiler_params=_params("arbitrary"))(uin, pool_w, pool_b, pool_scale, w_up)


def _pool_bwd(name, dy, z, p, pool_w, pool_b, pool_scale, w_up, duin, tm=ROWS):
    t, pw = z.shape
    tm = min(tm, t)
    nb = t // tm
    ng = len(POOL_WINDOWS)
    gw = pw // ng
    ns = w_up.shape[2]

    def body(dy_ref, z_ref, p_ref, w_ref, b_ref, s_ref, wu_ref, duin_ref, du_ref, dw_ref, db_ref, ds_ref, dwu_ref, ext, dwu_acc):
        i = pl.program_id(0)

        @pl.when(i == 0)
        def _():
            ext[tm:tm + POOL_HALO, :] = jnp.zeros((POOL_HALO, pw), F32)
            dw_ref[...] = jnp.zeros_like(dw_ref)
            db_ref[...] = jnp.zeros_like(db_ref)
            ds_ref[...] = jnp.zeros_like(ds_ref)
            dwu_acc[...] = jnp.zeros_like(dwu_acc)

        dwu_acc[...] += lax.dot_general(p_ref[...], dy_ref[...], (((0,), (0,)), ((), ())), preferred_element_type=F32)

        @pl.when(i == nb - 1)
        def _():
            for s in range(N_CHIPS):
                dwu_ref[s] = dwu_acc[:, s * ns:(s + 1) * ns].astype(BF16)

        pos = (nb - 1 - i) * tm + lax.broadcasted_iota(jnp.int32, (tm, 1), 0)
        dp = None
        for s in range(N_CHIPS):
            q = lax.dot_general(dy_ref[:, s * ns:(s + 1) * ns], wu_ref[s], (((1,), (1,)), ((), ())), preferred_element_type=F32)
            dp = q if dp is None else dp + q
        for k, win in enumerate(POOL_WINDOWS):
            cols = slice(k * gw, (k + 1) * gw)
            zk = z_ref[:, cols]
            dpk = dp[:, cols]
            wk = w_ref[k].astype(BF16)
            zp = jnp.dot(zk, wk, preferred_element_type=F32) + b_ref[:, cols]
            ds_ref[:, cols] += _colsum(dpk * zp)
            dzp = dpk * s_ref[:, cols]
            db_ref[:, cols] += _colsum(dzp)
            dzpb = dzp.astype(BF16)
            dz = lax.dot_general(dzpb, wk, (((1,), (1,)), ((), ())), preferred_element_type=F32)
            dw_ref[k] += lax.dot_general(zk, dzpb, (((0,), (0,)), ((), ())), preferred_element_type=F32)
            cnt = jnp.minimum(pos + 1, win).astype(F32)
            r = dz / cnt
            ext[0:tm, cols] = r
            acc = r - dz
            for j in range(1, win):
                acc = acc + ext[j:j + tm, cols]
            du_ref[:, cols] = acc.astype(BF16)
        ext[tm:tm + POOL_HALO, :] = ext[0:POOL_HALO, :]

    rev = lambda i: (nb - 1 - i, 0)
    full3 = lambda i: (0, 0, 0)
    return _pcall(
        body, name=name, grid=(nb,),
        in_specs=[pl.BlockSpec((tm, N_CHIPS * ns), rev), pl.BlockSpec((tm, pw), rev), pl.BlockSpec((tm, pw), rev),
                  pl.BlockSpec(pool_w.shape, full3), pl.BlockSpec(pool_b.shape, lambda i: (0, 0)),
                  pl.BlockSpec(pool_scale.shape, lambda i: (0, 0)), pl.BlockSpec(w_up.shape, full3), ANY_SPEC],
        out_specs=[pl.BlockSpec((tm, pw), rev), pl.BlockSpec(pool_w.shape, full3),
                   pl.BlockSpec((1, pw), lambda i: (0, 0)), pl.BlockSpec((1, pw), lambda i: (0, 0)), pl.BlockSpec(w_up.shape, full3)],
        out_shape=[jax.ShapeDtypeStruct(duin.shape, BF16), jax.ShapeDtypeStruct(pool_w.shape, F32),
                   jax.ShapeDtypeStruct((1, pw), F32), jax.ShapeDtypeStruct((1, pw), F32), jax.ShapeDtypeStruct(w_up.shape, BF16)],
        scratch_shapes=[pltpu.VMEM((tm + POOL_HALO, pw), F32), pltpu.VMEM((pw, N_CHIPS * ns), F32)], input_output_aliases={7: 0},
        compiler_params=_params("arbitrary"))(dy, z, p, pool_w, pool_b, pool_scale, w_up, duin)


def _ssm_disc(lrl, li, ldt):
    lr = -jnp.exp(lrl)
    dt = jnp.exp(ldt)
    mag = jnp.exp(lr * dt)
    ang = li * dt
    ab_re = mag * jnp.cos(ang)
    ab_im = mag * jnp.sin(ang)
    num_re = ab_re - 1.0
    num_im = ab_im
    den = lr * lr + li * li
    f_re = (num_re * lr + num_im * li) / den
    f_im = (num_im * lr - num_re * li) / den
    return lr, dt, mag, ang, ab_re, ab_im, num_re, num_im, den, f_re, f_im


def _ssm_prep(name, lrl, li, ldt, b_re, b_im):
    gn, h = b_re.shape

    def body(lrl_ref, li_ref, ldt_ref, br_ref, bi_ref, lrdt_ref, ang_ref, bbr_ref, bbi_ref):
        lr, dt, _, ang, _, _, _, _, _, f_re, f_im = _ssm_disc(lrl_ref[...], li_ref[...], ldt_ref[...])
        lrdt_ref[...] = lr * dt
        ang_ref[...] = ang
        br, bi = br_ref[...], bi_ref[...]
        bbr_ref[...] = f_re * br - f_im * bi
        bbi_ref[...] = f_re * bi + f_im * br

    col = jax.ShapeDtypeStruct((gn, 1), F32)
    mat = jax.ShapeDtypeStruct((gn, h), F32)
    return _pcall(body, name=name, out_shape=[col, col, mat, mat])(lrl, li, ldt, b_re, b_im)


def _ssm_param_bwd(name, lrl, li, ldt, b_re, b_im, g_abre, g_abim, g_bbre, g_bbim):
    gn, h = b_re.shape

    def body(lrl_ref, li_ref, ldt_ref, br_ref, bi_ref, gar_ref, gai_ref, gbr_ref, gbi_ref,
             dlrl_ref, dli_ref, dldt_ref, dbr_ref, dbi_ref):
        li_v = li_ref[...]
        lr, dt, mag, ang, ab_re, ab_im, num_re, num_im, den, f_re, f_im = _ssm_disc(lrl_ref[...], li_v, ldt_ref[...])
        br, bi = br_ref[...], bi_ref[...]
        gbr, gbi = gbr_ref[...], gbi_ref[...]
        g_fre = jnp.sum(gbr * br + gbi * bi, axis=1, keepdims=True)
        g_fim = jnp.sum(gbi * br - gbr * bi, axis=1, keepdims=True)
        dbr_ref[...] = gbr * f_re + gbi * f_im
        dbi_ref[...] = gbi * f_re - gbr * f_im
        g_num_re = (g_fre * lr - g_fim * li_v) / den
        g_num_im = (g_fre * li_v + g_fim * lr) / den
        g_den = -(g_fre * f_re + g_fim * f_im) / den
        g_lr = (g_fre * num_re + g_fim * num_im) / den + g_den * 2.0 * lr
        g_li = (g_fre * num_im - g_fim * num_re) / den + g_den * 2.0 * li_v
        g_are = gar_ref[...] + g_num_re
        g_aim = gai_ref[...] + g_num_im
        g_mag = g_are * jnp.cos(ang) + g_aim * jnp.sin(ang)
        g_ang = g_aim * ab_re - g_are * ab_im
        g_lrdt = g_mag * mag
        g_lr = g_lr + g_lrdt * dt
        g_dt = g_lrdt * lr + g_ang * li_v
        g_li = g_li + g_ang * dt
        dlrl_ref[...] = g_lr * lr
        dli_ref[...] = g_li
        dldt_ref[...] = g_dt * dt

    col = jax.ShapeDtypeStruct((gn, 1), F32)
    mat = jax.ShapeDtypeStruct((gn, h), F32)
    return _pcall(body, name=name, out_shape=[col, col, col, mat, mat])(lrl, li, ldt, b_re, b_im, g_abre, g_abim, g_bbre, g_bbim)


def _pow_rows(lrdt, ang, k):
    mag = jnp.exp(k * lrdt)
    return mag * jnp.cos(k * ang), mag * jnp.sin(k * ang)


def _ssm_chunk(t):
    return 256 if t >= 2048 else 128


def _to_segments(dst, srcs, jn):
    for q, src in enumerate(srcs):
        for j in range(jn):
            dst[8 * j:8 * j + 8, 128 * q:128 * (q + 1)] = src[pl.ds(j, 8, stride=jn), :]


def _from_segments(dst, src, q, jn, dtype):
    for s in range(8):
        dst[s * jn:(s + 1) * jn, 128 * q:128 * (q + 1)] = src[q, pl.ds(s, jn, stride=8), :].astype(dtype)


def _fill_rows8(dst_re, dst_im, v_re, v_im):
    for j in range(v_re.shape[0]):
        dst_re[8 * j:8 * j + 8, :] = jnp.broadcast_to(v_re[j:j + 1, :], (8, v_re.shape[1]))
        dst_im[8 * j:8 * j + 8, :] = jnp.broadcast_to(v_im[j:j + 1, :], (8, v_im.shape[1]))


def _cmul(ar, ai, br, bi):
    return ar * br - ai * bi, ar * bi + ai * br


def _cmul_conj(ar, ai, br, bi):
    return ar * br + ai * bi, ar * bi - ai * br


def _ssm_fwd(name, uin, lrdt, ang, bb_re, bb_im, cc_re, cc_im, d_skip, sw):
    t = uin.shape[0]
    gn = lrdt.shape[1]
    lc = _ssm_chunk(t)
    jn = lc // 8
    ub, sb = sw // SSM_BLOCKS, gn // SSM_BLOCKS
    nq = sw // 128
    assert ub == 128 and nq == SSM_BLOCKS

    def body(u_ref, lrdt_ref, ang_ref, bbr_ref, bbi_ref, ccr_ref, cci_ref, d_ref, y_ref, ge_ref, sre_ref, sim_ref,
             p_re, p_im, a_re, a_im, up, yp, cst_re, cst_im, car_re, car_im):
        i = pl.program_id(0)
        lrdt_v, ang_v = lrdt_ref[...], ang_ref[...]

        @pl.when(i == 0)
        def _():
            k = (lax.broadcasted_iota(jnp.int32, (jn, 1), 0) + 1).astype(F32)
            _fill_rows8(p_re, p_im, *_pow_rows(lrdt_v, ang_v, k))
            car_re[...] = jnp.zeros_like(car_re)
            car_im[...] = jnp.zeros_like(car_im)

        for q in range(nq):
            yp[q] = u_ref[:, 128 * q:128 * (q + 1)].astype(F32)
        _to_segments(up, [yp.at[q] for q in range(nq)], jn)
        u = up[...]
        ubf = u.astype(BF16)
        for q in range(SSM_BLOCKS):
            uq = ubf[:, q * ub:(q + 1) * ub]
            a_re[:, q * sb:(q + 1) * sb] = jnp.dot(uq, bbr_ref[q], preferred_element_type=F32)
            a_im[:, q * sb:(q + 1) * sb] = jnp.dot(uq, bbi_ref[q], preferred_element_type=F32)
        a1r, a1i = _pow_rows(lrdt_v, ang_v, 1.0)
        ajr, aji = _pow_rows(lrdt_v, ang_v, float(jn))
        for q in range(0, SSM_BLOCKS, 2):
            cols = slice(q * sb, (q + 2) * sb)
            ar8 = jnp.broadcast_to(a1r[:, cols], (8, 2 * sb))
            ai8 = jnp.broadcast_to(a1i[:, cols], (8, 2 * sb))

            def step(j, carry, cols=cols, ar8=ar8, ai8=ai8):
                sr, si = carry
                rows = pl.ds(pl.multiple_of(j * 8, 8), 8)
                mr, mi = _cmul(ar8, ai8, sr, si)
                nr, ni = mr + a_re[rows, cols], mi + a_im[rows, cols]
                a_re[rows, cols] = nr
                a_im[rows, cols] = ni
                return nr, ni

            lax.fori_loop(1, jn, step, (a_re[0:8, cols], a_im[0:8, cols]), unroll=4)
        er, ei = a_re[lc - 8:lc, :], a_im[lc - 8:lc, :]
        gr, gi = car_re[...], car_im[...]
        for s in range(8):
            cst_re[s:s + 1, :] = gr
            cst_im[s:s + 1, :] = gi
            mr, mi = _cmul(ajr, aji, gr, gi)
            gr, gi = mr + er[s:s + 1, :], mi + ei[s:s + 1, :]
        car_re[...] = gr
        car_im[...] = gi
        for q in range(SSM_BLOCKS):
            cols = slice(q * sb, (q + 1) * sb)
            cr = jnp.tile(cst_re[:, cols], (jn, 1))
            ci = jnp.tile(cst_im[:, cols], (jn, 1))
            mr, mi = _cmul(p_re[:, cols], p_im[:, cols], cr, ci)
            srb, sib = (a_re[:, cols] + mr).astype(BF16), (a_im[:, cols] + mi).astype(BF16)
            sre_ref[:, cols] = srb
            sim_ref[:, cols] = sib
            ycols = slice(q * ub, (q + 1) * ub)
            y = (jnp.dot(srb, ccr_ref[q], preferred_element_type=F32) - jnp.dot(sib, cci_ref[q], preferred_element_type=F32)
                 + d_ref[:, ycols] * u[:, ycols])
            yp[q] = y
            _from_segments(y_ref, yp, q, jn, F32)
            yt = y_ref[:, ycols]
            ge_ref[:, ycols] = (0.5 * yt * (1.0 + lax.erf(yt * (1.0 / math.sqrt(2.0))))).astype(BF16)

    row = lambda i: (0, 0)
    blk3 = lambda i: (0, 0, 0)
    return _pcall(
        body, name=name, grid=(t // lc,),
        in_specs=[pl.BlockSpec((lc, sw), lambda i: (i, 1)), pl.BlockSpec((1, gn), row), pl.BlockSpec((1, gn), row),
                  pl.BlockSpec(bb_re.shape, blk3), pl.BlockSpec(bb_im.shape, blk3),
                  pl.BlockSpec(cc_re.shape, blk3), pl.BlockSpec(cc_im.shape, blk3), pl.BlockSpec((1, sw), row)],
        out_specs=[pl.BlockSpec((lc, sw), lambda i: (i, 0)), pl.BlockSpec((lc, sw), lambda i: (i, 0)),
                   pl.BlockSpec((lc, gn), lambda i: (i, 0)), pl.BlockSpec((lc, gn), lambda i: (i, 0))],
        out_shape=[jax.ShapeDtypeStruct((t, sw), F32), jax.ShapeDtypeStruct((t, sw), BF16),
                   jax.ShapeDtypeStruct((t, gn), BF16), jax.ShapeDtypeStruct((t, gn), BF16)],
        scratch_shapes=[pltpu.VMEM((lc, gn), F32), pltpu.VMEM((lc, gn), F32), pltpu.VMEM((lc, gn), F32), pltpu.VMEM((lc, gn), F32),
                        pltpu.VMEM((lc, sw), F32), pltpu.VMEM((nq, lc, 128), F32),
                        pltpu.VMEM((8, gn), F32), pltpu.VMEM((8, gn), F32), pltpu.VMEM((1, gn), F32), pltpu.VMEM((1, gn), F32)],
        compiler_params=_params("arbitrary"))(uin, lrdt, ang, bb_re, bb_im, cc_re, cc_im, d_skip)


def _ssm_bwd(name, dge, y, uin, s_re, s_im, lrdt, ang, bbt_re, bbt_im, cct_re, cct_im, d_skip, sw, duin):
    t = uin.shape[0]
    gn = lrdt.shape[1]
    lc = _ssm_chunk(t)
    nb = t // lc
    jn = lc // 8
    ub, sb = sw // SSM_BLOCKS, gn // SSM_BLOCKS
    nq = sw // 128
    tail = 16

    def body(dge_ref, y_ref, u_ref, sre_ref, sim_ref, tre_ref, tim_ref, lrdt_ref, ang_ref, btr_ref, bti_ref, ctr_ref, cti_ref,
             d_ref, duin_ref, du_ref, dar_ref, dai_ref, dbr_ref, dbi_ref, dcr_ref, dci_ref, dd_ref,
             q_re, q_im, a_re, a_im, dyp, up, dys, us, dup, cst_re, cst_im, sp_re, sp_im, car_re, car_im):
        i = pl.program_id(0)
        lrdt_v, ang_v = lrdt_ref[...], ang_ref[...]

        @pl.when(i == 0)
        def _():
            k = (jn - lax.broadcasted_iota(jnp.int32, (jn, 1), 0)).astype(F32)
            _fill_rows8(q_re, q_im, *_pow_rows(lrdt_v, ang_v, k))
            car_re[...] = jnp.zeros_like(car_re)
            car_im[...] = jnp.zeros_like(car_im)
            for r in (dar_ref, dai_ref, dbr_ref, dbi_ref, dcr_ref, dci_ref, dd_ref):
                r[...] = jnp.zeros_like(r)

        yv = y_ref[...]
        ut = u_ref[...].astype(F32)
        cdf =0.5 * (1.0 + lax.erf(yv * (1.0 / math.sqrt(2.0))))
        pdf = jnp.exp(-0.5 * yv * yv) * (1.0 / math.sqrt(2.0 * math.pi))
        dyt = dge_ref[...].astype(F32) * (cdf + yv * pdf)
        dd_ref[...] += _colsum(dyt * ut)
        for q in range(nq):
            dys[q] = dyt[:, 128 * q:128 * (q + 1)]
            us[q] = ut[:, 128 * q:128 * (q + 1)]
        _to_segments(dyp, [dys.at[q] for q in range(nq)], jn)
        _to_segments(up, [us.at[q] for q in range(nq)], jn)
        dy = dyp[...]
        u = up[...]
        dyb = dy.astype(BF16)
        ubf = u.astype(BF16)
        for q in range(SSM_BLOCKS):
            dq = dyb[:, q * ub:(q + 1) * ub]
            a_re[:, q * sb:(q + 1) * sb] = jnp.dot(dq, ctr_ref[q], preferred_element_type=F32)
            a_im[:, q * sb:(q + 1) * sb] = -jnp.dot(dq, cti_ref[q], preferred_element_type=F32)
        a1r, a1i = _pow_rows(lrdt_v, ang_v, 1.0)
        ajr, aji = _pow_rows(lrdt_v, ang_v, float(jn))
        for q in range(0, SSM_BLOCKS, 2):
            cols = slice(q * sb, (q + 2) * sb)
            ar8 = jnp.broadcast_to(a1r[:, cols], (8, 2 * sb))
            ai8 = jnp.broadcast_to(a1i[:, cols], (8, 2 * sb))

            def step(jj, carry, cols=cols, ar8=ar8, ai8=ai8):
                sr, si = carry
                rows = pl.ds(pl.multiple_of((jn - 2 - jj) * 8, 8), 8)
                mr, mi = _cmul_conj(ar8, ai8, sr, si)
                nr, ni = mr + a_re[rows, cols], mi + a_im[rows, cols]
                a_re[rows, cols] = nr
                a_im[rows, cols] = ni
                return nr, ni

            lax.fori_loop(0, jn - 1, step, (a_re[lc - 8:lc, cols], a_im[lc - 8:lc, cols]), unroll=4)
        er, ei = a_re[0:8, :], a_im[0:8, :]
        hr, hi = car_re[...], car_im[...]
        for s in range(7, -1, -1):
            cst_re[s:s + 1, :] = hr
            cst_im[s:s + 1, :] = hi
            mr, mi = _cmul_conj(ajr, aji, hr, hi)
            hr, hi = mr + er[s:s + 1, :], mi + ei[s:s + 1, :]
        car_re[...] = hr
        car_im[...] = hi
        first = (i == nb - 1).astype(F32)
        sp_re[0:tail, :] = tre_ref[...].astype(F32) * (1.0 - first)
        sp_im[0:tail, :] = tim_ref[...].astype(F32) * (1.0 - first)
        sp_re[tail:tail + 8, :] = sre_ref[lc - tail:lc, :].astype(F32)[tail - 8:tail]
        sp_im[tail:tail + 8, :] = sim_ref[lc - tail:lc, :].astype(F32)[tail - 8:tail]
        tn_dims = (((0,), (0,)), ((), ()))
        for q in range(SSM_BLOCKS):
            cols = slice(q * sb, (q + 1) * sb)
            ycols = slice(q * ub, (q + 1) * ub)
            cr = jnp.tile(cst_re[:, cols], (jn, 1))
            ci = jnp.tile(cst_im[:, cols], (jn, 1))
            mr, mi = _cmul_conj(q_re[:, cols], q_im[:, cols], cr, ci)
            lam_r, lam_i = a_re[:, cols] + mr, a_im[:, cols] + mi
            s_r, s_i = sre_ref[:, cols], sim_ref[:, cols]
            p0r, p0i = sp_re[tail - 1:tail + 7, cols], sp_im[tail - 1:tail + 7, cols]
            l0r, l0i, l1r, l1i = lam_r[0:8], lam_i[0:8], lam_r[8:lc], lam_i[8:lc]
            pvr, pvi = s_r.astype(F32)[0:lc - 8], s_i.astype(F32)[0:lc - 8]
            dar_ref[:, cols] += _colsum(l1r * pvr + l1i * pvi) + _colsum(l0r * p0r + l0i * p0i)
            dai_ref[:, cols] += _colsum(l1i * pvr - l1r * pvi) + _colsum(l0i * p0r - l0r * p0i)
            lrb, lib = lam_r.astype(BF16), lam_i.astype(BF16)
            dup[q] = (jnp.dot(lrb, btr_ref[q], preferred_element_type=F32)
                      + jnp.dot(lib, bti_ref[q], preferred_element_type=F32) + d_ref[:, ycols] * dy[:, ycols])
            _from_segments(du_ref, dup, q, jn, BF16)
            uq = ubf[:, ycols]
            dbr_ref[q] += lax.dot_general(uq, lrb, tn_dims, preferred_element_type=F32)
            dbi_ref[q] += lax.dot_general(uq, lib, tn_dims, preferred_element_type=F32)
            dq = dyb[:, ycols]
            dcr_ref[q] += lax.dot_general(s_r.astype(BF16), dq, tn_dims, preferred_element_type=F32)
            dci_ref[q] -= lax.dot_general(s_i.astype(BF16), dq, tn_dims, preferred_element_type=F32)

    rev = lambda i: (nb - 1 - i, 0)
    tailmap = lambda i: (jnp.maximum((nb - 1 - i) * (lc // tail) - 1, 0), 0)
    row = lambda i: (0, 0)
    blk3 = lambda i: (0, 0, 0)
    return _pcall(
        body, name=name, grid=(nb,),
        in_specs=[pl.BlockSpec((lc, sw), rev), pl.BlockSpec((lc, sw), rev), pl.BlockSpec((lc, sw), lambda i: (nb - 1 - i, 1)),
                  pl.BlockSpec((lc, gn), rev), pl.BlockSpec((lc, gn), rev),
                  pl.BlockSpec((tail, gn), tailmap), pl.BlockSpec((tail, gn), tailmap),
                  pl.BlockSpec((1, gn), row), pl.BlockSpec((1, gn), row),
                  pl.BlockSpec(bbt_re.shape, blk3), pl.BlockSpec(bbt_im.shape, blk3),
                  pl.BlockSpec(cct_re.shape, blk3), pl.BlockSpec(cct_im.shape, blk3), pl.BlockSpec((1, sw), row), ANY_SPEC],
        out_specs=[pl.BlockSpec((lc, sw), lambda i: (nb - 1 - i, 1)), pl.BlockSpec((1, gn), row), pl.BlockSpec((1, gn), row),
                   pl.BlockSpec((SSM_BLOCKS, ub, sb), blk3), pl.BlockSpec((SSM_BLOCKS, ub, sb), blk3),
                   pl.BlockSpec((SSM_BLOCKS, sb, ub), blk3), pl.BlockSpec((SSM_BLOCKS, sb, ub), blk3),
                   pl.BlockSpec((1, sw), row)],
        out_shape=[jax.ShapeDtypeStruct(duin.shape, BF16), jax.ShapeDtypeStruct((1, gn), F32), jax.ShapeDtypeStruct((1, gn), F32),
                   jax.ShapeDtypeStruct((SSM_BLOCKS, ub, sb), F32), jax.ShapeDtypeStruct((SSM_BLOCKS, ub, sb), F32),
                   jax.ShapeDtypeStruct((SSM_BLOCKS, sb, ub), F32), jax.ShapeDtypeStruct((SSM_BLOCKS, sb, ub), F32),
                   jax.ShapeDtypeStruct((1, sw), F32)],
        scratch_shapes=[pltpu.VMEM((lc, gn), F32), pltpu.VMEM((lc, gn), F32), pltpu.VMEM((lc, gn), F32), pltpu.VMEM((lc, gn), F32),
                        pltpu.VMEM((lc, sw), F32), pltpu.VMEM((lc, sw), F32),
                        pltpu.VMEM((nq, lc, 128), F32), pltpu.VMEM((nq, lc, 128), F32), pltpu.VMEM((nq, lc, 128), F32),
                        pltpu.VMEM((8, gn), F32), pltpu.VMEM((8, gn), F32),
                        pltpu.VMEM((tail + 8, gn), F32), pltpu.VMEM((tail + 8, gn), F32),
                        pltpu.VMEM((1, gn), F32), pltpu.VMEM((1, gn), F32)],
        input_output_aliases={14: 0},
        compiler_params=_params("arbitrary"))(dge, y, uin, s_re, s_im, s_re, s_im, lrdt, ang,
                                               bbt_re, bbt_im, cct_re, cct_im, d_skip, duin)


def _blockdiag_b(bb, sw):
    gpb = (sw // SSM_GROUP) // SSM_BLOCKS
    b4 = bb.reshape(SSM_BLOCKS, gpb, SSM_STATE, SSM_GROUP)
    eye = jnp.eye(gpb, dtype=bb.dtype)
    out = jnp.einsum('qgnh,gk->qghkn', b4, eye)
    return out.reshape(SSM_BLOCKS, gpb * SSM_GROUP, gpb * SSM_STATE)


def _blockdiag_c(cc, sw):
    gpb = (sw // SSM_GROUP) // SSM_BLOCKS
    c4 = cc.reshape(SSM_BLOCKS, gpb, SSM_GROUP, SSM_STATE)
    eye = jnp.eye(gpb, dtype=cc.dtype)
    out = jnp.einsum('qghn,gk->qgnkh', c4, eye)
    return out.reshape(SSM_BLOCKS, gpb * SSM_STATE, gpb * SSM_GROUP)


def _diag_of_b(dbb, sw):
    gpb = (sw // SSM_GROUP) // SSM_BLOCKS
    d5 = dbb.reshape(SSM_BLOCKS, gpb, SSM_GROUP, gpb, SSM_STATE)
    return jnp.einsum('qghgn->qgnh', d5).reshape(SSM_BLOCKS * gpb * SSM_STATE, SSM_GROUP)


def _diag_of_c(dcc, sw):
    gpb = (sw // SSM_GROUP) // SSM_BLOCKS
    d5 = dcc.reshape(SSM_BLOCKS, gpb, SSM_STATE, gpb, SSM_GROUP)
    return jnp.einsum('qgngh->qghn', d5).reshape(SSM_BLOCKS * gpb, SSM_GROUP, SSM_STATE)


def _ada_fwd(name, c_all, w_sh, b_sh):
    nb, d = c_all.shape
    ncol = w_sh.shape[1]
    tn = _pick(ncol, 768)

    def body(c_ref, w_ref, b_ref, o_ref):
        cv = c_ref[...]
        sil = cv * _sigmoid(cv)
        o_ref[...] = jnp.dot(sil, w_ref[...], preferred_element_type=F32, precision=lax.Precision.HIGHEST) + b_ref[...]

    return _pcall(body, name=name, grid=(ncol // tn,),
                  in_specs=[pl.BlockSpec((nb, d), lambda j: (0, 0)), pl.BlockSpec((d, tn), lambda j: (0, j)),
                            pl.BlockSpec((1, tn), lambda j: (0, j))],
                  out_specs=pl.BlockSpec((nb, tn), lambda j: (0, j)),
                  out_shape=jax.ShapeDtypeStruct((nb, ncol), F32), compiler_params=_params("parallel"))(c_all, w_sh, b_sh)


def _ada_bwd(name, c_all, dmod_sh):
    nb, d = c_all.shape
    ncol = dmod_sh.shape[1]
    tn = _pick(ncol, 768)

    def body(c_ref, g_ref, o_ref):
        cv = c_ref[...]
        sil = cv * _sigmoid(cv)
        o_ref[...] = lax.dot_general(sil, g_ref[...], (((0,), (0,)), ((), ())), preferred_element_type=F32,
                                     precision=lax.Precision.HIGHEST)

    return _pcall(body, name=name, grid=(ncol // tn,),
                  in_specs=[pl.BlockSpec((nb, d), lambda j: (0, 0)), pl.BlockSpec((nb, tn), lambda j: (0, j))],
                  out_specs=pl.BlockSpec((d, tn), lambda j: (0, j)),
                  out_shape=jax.ShapeDtypeStruct((d, ncol), F32), compiler_params=_params("parallel"))(c_all, dmod_sh)


def _place():
    return lax.axis_index("x"), lax.axis_index("y"), lax.axis_index("c")


def _allgather_small(name, blk, deps=()):
    m_per, n = blk.shape

    def body(x_ref, *rest):
        out_ref, send_sems, recv_sems, local_sem = rest[len(deps):]
        x, y, c = _place()
        me, sibling = (x, y, c), (x, y, 1 - c)
        chips = [(1 - x, y), (x, 1 - y), (1 - x, 1 - y)]

        def rows(px, py, pc):
            return out_ref.at[pl.ds((4 * px + 2 * py + pc) * m_per, m_per), :]

        def copy(k, block, to, src=None):
            return pltpu.make_async_remote_copy(
                src_ref=rows(*block) if src is None else src, dst_ref=rows(*block),
                send_sem=send_sems.at[k], recv_sem=recv_sems.at[k], device_id=to, device_id_type=MESH)

        mine = pltpu.make_async_copy(x_ref, rows(*me), local_sem)
        mine.start()
        first = [copy(0, me, sibling, src=x_ref)]
        first += [copy(1 + j, me, (*chip, c), src=x_ref) for j, chip in enumerate(chips)]
        for cp in first:
            cp.start()
        passed = [copy(4 + j, (*chip, c), sibling) for j, chip in enumerate(chips)]
        for j, chip in enumerate(chips):
            copy(1 + j, (*chip, c), me).wait_recv()
            passed[j].start()
        copy(0, sibling, me).wait_recv()
        for j, chip in enumerate(chips):
            copy(4 + j, (*chip, 1 - c), me).wait_recv()
        for cp in first + passed:
            cp.wait_send()
        mine.wait()

    return _pcall(body, name=name, out_shape=jax.ShapeDtypeStruct((N_DEV * m_per, n), blk.dtype),
                  in_specs=[pl.BlockSpec(memory_space=pltpu.VMEM)] + [ANY_SPEC] * len(deps),
                  out_specs=pl.BlockSpec(memory_space=pltpu.VMEM),
                  scratch_shapes=[pltpu.SemaphoreType.DMA((7,)), pltpu.SemaphoreType.DMA((7,)), pltpu.SemaphoreType.DMA],
                  compiler_params=pltpu.CompilerParams(vmem_limit_bytes=VMEM_LIMIT))(blk, *deps)


def _other_chips(x, y):
    return [(1 - x, y), (x, 1 - y), (1 - x, 1 - y)]


def _place_shards(shards, s_me):
    return [lax.dynamic_update_slice(lax.empty((N_CHIPS,) + s.shape, s.dtype), s[None], (s_me, 0, 0)) for s in shards]


def _ag_copy(src, land, send, recv, wi, j, chip, x, y, c, tc, both):
    hr = src.shape[0] // 2
    half = pl.ds(pl.multiple_of(c * hr, 16), hr)
    k = 3 * wi + j
    return pltpu.make_async_remote_copy(
        src_ref=src.at[half, :], dst_ref=land.at[2 * x + y, half, :],
        send_sem=send.at[2 * k + tc if both else k], recv_sem=recv.at[2 * k + c if both else k],
        device_id=(chip[0], chip[1], tc), device_id_type=MESH)


def _ag_targets(c, both):
    return (0, 1) if both else (c,)


def _ag_start(name, shards, lands, groups, direct, deps=()):
    nw, ng, nd = len(shards), len(groups), len(deps)

    def body(*refs):
        src, land = refs[:nw], refs[nw:2 * nw]
        sems = refs[2 * nw + nd:2 * nw + nd + 2 * ng]
        token = refs[-1]
        x, y, c = _place()
        for gi, grp in enumerate(groups):
            for wi, w in enumerate(grp):
                for j, chip in enumerate(_other_chips(x, y)):
                    for tc in _ag_targets(c, direct[gi]):
                        _ag_copy(src[w], land[w], sems[2 * gi], sems[2 * gi + 1], wi, j, chip, x, y, c, tc, direct[gi]).start()
        token[...] = jnp.zeros_like(token)

    sem_shapes = []
    for gi, grp in enumerate(groups):
        sem_shapes += [pltpu.SemaphoreType.DMA(((6 if direct[gi] else 3) * len(grp),))] * 2
    out_shape = sem_shapes + [pltpu.HBM(s.shape, s.dtype) for s in shards] + [pltpu.HBM(l.shape, l.dtype) for l in lands]
    out_shape += [jax.ShapeDtypeStruct((8, 128), F32)]
    res = _pcall(body, name=name, out_shape=out_shape, in_specs=[HBM_SPEC] * (2 * nw) + [ANY_SPEC] * nd,
                 out_specs=[SEM_SPEC] * (2 * ng) + [HBM_SPEC] * (2 * nw) + [pl.BlockSpec(memory_space=pltpu.VMEM)],
                 input_output_aliases={i: 2 * ng + i for i in range(2 * nw)},
                 compiler_params=pltpu.CompilerParams(has_side_effects=EFFECT))(
                     *[_hbm(s) for s in shards], *[_hbm(l) for l in lands], *deps)
    sems = [(res[2 * gi], res[2 * gi + 1]) for gi in range(ng)]
    return sems, list(res[2 * ng:2 * ng + nw]), list(res[2 * ng + nw:2 * ng + 2 * nw]), res[-1]


def _ag_wait(name, shards, lands, send, recv, both, after):
    n = len(shards)

    def body(*refs):
        src, land = refs[:n], refs[n:2 * n]
        send_sem, recv_sem = refs[2 * n], refs[2 * n + 1]
        x, y, c = _place()
        for wi in range(n):
            for j, chip in enumerate(_other_chips(x, y)):
                for tc in _ag_targets(c, both):
                    _ag_copy(src[wi], land[wi], send_sem, recv_sem, wi, j, chip, x, y, c, tc, both).wait_send()
                    _ag_copy(src[wi], land[wi], send_sem, recv_sem, wi, j, chip, chip[0], chip[1], tc, c, both).wait_recv()

    res = _pcall(body, name=name, out_shape=[pltpu.HBM(a.shape, a.dtype) for a in list(shards) + list(lands)],
                 in_specs=[HBM_SPEC] * (2 * n) + [SEM_SPEC, SEM_SPEC] + [ANY_SPEC] * len(after), out_specs=[HBM_SPEC] * (2 * n),
                 input_output_aliases={i: i for i in range(2 * n)},
                 compiler_params=pltpu.CompilerParams(has_side_effects=EFFECT))(*shards, *lands, send, recv, *after)
    return list(res[n:])


def _ag_forward(name, lands):
    n = len(lands)

    def body(*refs):
        out = refs[n:2 * n]
        send, recv = refs[2 * n], refs[2 * n + 1]
        x, y, c = _place()
        sib = (x, y, 1 - c)
        cps = []
        for wi in range(n):
            hr = out[wi].shape[1] // 2
            for j, (cx, cy) in enumerate(_other_chips(x, y)):
                got = out[wi].at[2 * cx + cy, pl.ds(pl.multiple_of(c * hr, 16), hr), :]
                cp = pltpu.make_async_remote_copy(src_ref=got, dst_ref=got, send_sem=send.at[3 * wi + j], recv_sem=recv.at[3 * wi + j],
                                                  device_id=sib, device_id_type=MESH)
                cp.start()
                cps.append(cp)
        for wi in range(n):
            hr = out[wi].shape[1] // 2
            for j, (cx, cy) in enumerate(_other_chips(x, y)):
                got = out[wi].at[2 * cx + cy, pl.ds(pl.multiple_of((1 - c) * hr, 16), hr), :]
                pltpu.make_async_remote_copy(src_ref=got, dst_ref=got, send_sem=send.at[3 * wi + j], recv_sem=recv.at[3 * wi + j],
                                             device_id=sib, device_id_type=MESH).wait_recv()
        for cp in cps:
            cp.wait_send()

    res = _pcall(body, name=name, out_shape=[jax.ShapeDtypeStruct(l.shape, l.dtype) for l in lands],
                 in_specs=[ANY_SPEC] * n, out_specs=[ANY_SPEC] * n, input_output_aliases={i: i for i in range(n)},
                 scratch_shapes=[pltpu.SemaphoreType.DMA((3 * n,)), pltpu.SemaphoreType.DMA((3 * n,))])(*lands)
    return list(res)


def _peers(x, y, c):
    offs = [(dx, dy, dc) for dx in (0, 1) for dy in (0, 1) for dc in (0, 1)][1:]
    return [(1 - x if dx else x, 1 - y if dy else y, 1 - c if dc else c) for dx, dy, dc in offs]


def _rs_copy(g_ref, land_ref, send, recv, wi, k, to, sender):
    hr = g_ref.shape[1] // 2
    return pltpu.make_async_remote_copy(
        src_ref=g_ref.at[2 * to[0] + to[1], pl.ds(pl.multiple_of(to[2] * hr, 16), hr), :], dst_ref=land_ref.at[sender],
        send_sem=send.at[7 * wi + k], recv_sem=recv.at[7 * wi + k], device_id=to, device_id_type=MESH)


def _rs_start(name, gs):
    n = len(gs)
    lands = [lax.empty((N_DEV, g.shape[1] // 2, g.shape[2]), BF16) for g in gs]

    def body(*refs):
        g, land = refs[:n], refs[n:2 * n]
        send, recv = refs[2 * n], refs[2 * n + 1]
        token = refs[-1]
        x, y, c = _place()
        me = 4 * x + 2 * y + c
        for wi in range(n):
            for k, to in enumerate(_peers(x, y, c)):
                _rs_copy(g[wi], land[wi], send, recv, wi, k, to, me).start()
        token[...] = jnp.zeros_like(token)

    out_shape = [pltpu.SemaphoreType.DMA((7 * n,))] * 2 + [pltpu.HBM(a.shape, a.dtype) for a in list(gs) + lands]
    out_shape += [jax.ShapeDtypeStruct((8, 128), F32)]
    res = _pcall(body, name=name, out_shape=out_shape, in_specs=[HBM_SPEC] * (2 * n),
                 out_specs=[SEM_SPEC] * 2 + [HBM_SPEC] * (2 * n) + [pl.BlockSpec(memory_space=pltpu.VMEM)],
                 input_output_aliases={i: 2 + i for i in range(2 * n)},
                 compiler_params=pltpu.CompilerParams(has_side_effects=EFFECT))(
                     *[_hbm(a) for a in gs], *[_hbm(a) for a in lands])
    return res[0], res[1], list(res[2:2 + n]), list(res[2 + n:2 + 2 * n]), res[-1]


def _rs_wait(name, gs, lands, send, recv, after):
    n = len(gs)

    def body(*refs):
        g, land = refs[:n], refs[n:2 * n]
        send_sem, recv_sem = refs[2 * n], refs[2 * n + 1]
        x, y, c = _place()
        me = 4 * x + 2 * y + c
        for wi in range(n):
            for k, to in enumerate(_peers(x, y, c)):
                _rs_copy(g[wi], land[wi], send_sem, recv_sem, wi, k, to, me).wait_send()
                _rs_copy(g[wi], land[wi], send_sem, recv_sem, wi, k, (x, y, c), 4 * to[0] + 2 * to[1] + to[2]).wait_recv()

    res = _pcall(body, name=name, out_shape=[pltpu.HBM(a.shape, a.dtype) for a in list(gs) + list(lands)],
                 in_specs=[HBM_SPEC] * (2 * n) + [SEM_SPEC, SEM_SPEC, ANY_SPEC], out_specs=[HBM_SPEC] * (2 * n),
                 input_output_aliases={i: i for i in range(2 * n)},
                 compiler_params=pltpu.CompilerParams(has_side_effects=EFFECT))(*gs, *lands, send, recv, after)
    return list(res[:n]), list(res[n:])


def _bc_copy(blk_ref, land_ref, send, recv, k, to, slot):
    return pltpu.make_async_remote_copy(src_ref=blk_ref, dst_ref=land_ref.at[slot], send_sem=send.at[k], recv_sem=recv.at[k],
                                        device_id=to, device_id_type=MESH)


def _bcast_start(name, blk):
    land = lax.empty((N_DEV,) + blk.shape, blk.dtype)

    def body(blk_ref, land_ref, send, recv, blk_thru, land_thru, token):
        x, y, c = _place()
        for k, to in enumerate(_peers(x, y, c)):
            _bc_copy(blk_ref, land_ref, send, recv, k, to, 4 * x + 2 * y + c).start()
        token[...] = jnp.zeros_like(token)

    return _pcall(body, name=name,
                  out_shape=[pltpu.SemaphoreType.DMA((7,)), pltpu.SemaphoreType.DMA((7,)), pltpu.HBM(blk.shape, blk.dtype),
                             pltpu.HBM(land.shape, land.dtype), jax.ShapeDtypeStruct((8, 128), F32)],
                  in_specs=[HBM_SPEC, HBM_SPEC], out_specs=[SEM_SPEC, SEM_SPEC, HBM_SPEC, HBM_SPEC, pl.BlockSpec(memory_space=pltpu.VMEM)],
                  input_output_aliases={0: 2, 1: 3}, compiler_params=pltpu.CompilerParams(has_side_effects=EFFECT))(_hbm(blk), _hbm(land))


def _bcast_wait(name, blk, land, send, recv, after):
    def body(blk_ref, land_ref, send_sem, recv_sem, after_ref, blk_thru, land_thru):
        x, y, c = _place()
        for k, to in enumerate(_peers(x, y, c)):
            _bc_copy(blk_ref, land_ref, send_sem, recv_sem, k, to, 4 * x + 2 * y + c).wait_send()
            _bc_copy(blk_ref, land_ref, send_sem, recv_sem, k, to, 4 * to[0] + 2 * to[1] + to[2]).wait_recv()

    return _pcall(body, name=name, out_shape=[pltpu.HBM(blk.shape, blk.dtype), pltpu.HBM(land.shape, land.dtype)],
                  in_specs=[HBM_SPEC, HBM_SPEC, SEM_SPEC, SEM_SPEC, ANY_SPEC], out_specs=[HBM_SPEC, HBM_SPEC],
                  input_output_aliases={0: 0, 1: 1}, compiler_params=pltpu.CompilerParams(has_side_effects=EFFECT))(
                      blk, land, send, recv, after)[1]


def _rs_sum(name, gs, lands):
    n = len(gs)

    def body(*refs):
        g_refs, land_refs, out_refs = refs[:n], refs[n:2 * n], refs[2 * n:3 * n]
        recvs = refs[3 * n:4 * n]
        local_sems, sib_send, sib_recv = refs[4 * n:]
        x, y, c = _place()
        me = 4 * x + 2 * y + c
        cps = []
        for wi in range(n):
            hr = g_refs[wi].shape[1] // 2
            own = g_refs[wi].at[2 * x + y, pl.ds(pl.multiple_of(c * hr, 16), hr), :]
            cps.append(pltpu.make_async_copy(own, recvs[wi].at[me], local_sems.at[8 * wi + 7]))
            for k, (tx, ty, tc) in enumerate(_peers(x, y, c)):
                slot = 4 * tx + 2 * ty + tc
                cps.append(pltpu.make_async_copy(land_refs[wi].at[slot], recvs[wi].at[slot], local_sems.at[8 * wi + k]))
        for cp in cps:
            cp.start()
        sibs = []
        for wi in range(n):
            hr = g_refs[wi].shape[1] // 2
            ch = _pick(hr, 64, 16)
            for cp in cps[8 * wi:8 * wi + 8]:
                cp.wait()
            base = pl.multiple_of(c * hr, 16)
            for r0 in range(0, hr, ch):
                acc = recvs[wi][0, r0:r0 + ch, :].astype(F32)
                for k in range(1, N_DEV):
                    acc = acc + recvs[wi][k, r0:r0 + ch, :].astype(F32)
                out_refs[wi][pl.ds(base + r0, ch), :] = acc
            half = out_refs[wi].at[pl.ds(base, hr), :]
            sib = pltpu.make_async_remote_copy(src_ref=half, dst_ref=half, send_sem=sib_send.at[wi], recv_sem=sib_recv.at[wi],
                                               device_id=(x, y, 1 - c), device_id_type=MESH)
            sib.start()
            sibs.append(sib)
        for wi in range(n):
            hr = g_refs[wi].shape[1] // 2
            other = out_refs[wi].at[pl.ds(pl.multiple_of((1 - c) * hr, 16), hr), :]
            pltpu.make_async_remote_copy(src_ref=other, dst_ref=other, send_sem=sib_send.at[wi], recv_sem=sib_recv.at[wi],
                                         device_id=(x, y, 1 - c), device_id_type=MESH).wait_recv()
            sibs[wi].wait_send()

    res = _pcall(
        body, name=name, out_shape=[jax.ShapeDtypeStruct(g.shape[1:], F32) for g in gs],
        in_specs=[ANY_SPEC] * (2 * n), out_specs=[pl.BlockSpec(memory_space=pltpu.VMEM)] * n,
        scratch_shapes=[pltpu.VMEM((N_DEV, g.shape[1] // 2, g.shape[2]), BF16) for g in gs]
        + [pltpu.SemaphoreType.DMA((8 * n,)), pltpu.SemaphoreType.DMA((n,)), pltpu.SemaphoreType.DMA((n,))],
        compiler_params=pltpu.CompilerParams(vmem_limit_bytes=VMEM_LIMIT))(*gs, *lands)
    return list(res)


def _gate_norm_epilogue(factor, x_in, gt, nxt, t, d):
    blk = (ROWS, d)
    full = lambda i, j: (i, j)
    rowv = lambda i, j: (0, j)

    def epi(acc, res, scale, *norm):
        x_new = res + (factor * scale) * acc
        if not norm:
            return x_new, acc
        gv, scv, shv = norm
        r = lax.rsqrt(jnp.mean(x_new * x_new, axis=-1, keepdims=True) + EPS)
        return x_new, acc, x_new * r * gv * (1.0 + scv) + shv

    ins = [(x_in, blk, full), (gt, (1, d), rowv)] + [(v, (1, d), rowv) for v in (nxt or ())]
    outs = [((t, d), F32, blk, full), ((t, d), BF16, blk, full)] + ([((t, d), BF16, blk, full)] if nxt else [])
    return epi, ins, outs


def _mixer_tail(name, ge, w_glu, b_glu, w_up, y_pool, uin, w_out, x_in, gt, nxt, d, pw, tm=ROWS):
    t, sw = ge.shape
    tm = min(tm, t)
    ns = w_glu.shape[2]
    cb = (2 * pw) // d

    def body(ge_ref, wg_ref, b_ref, wu_ref, yp_ref, glp_ref, gls_ref, wo_ref, x_ref, gt_ref, gn_ref, sc_ref, sh_ref,
             gv_ref, sg_ref, ys_ref, mg_ref, y_ref, h_ref, xn_ref):
        gev = ge_ref[...]
        gv = jnp.concatenate([jnp.dot(gev, wg_ref[s], preferred_element_type=F32) for s in range(N_CHIPS)], axis=1)
        gv_ref[...] = gv.astype(BF16)
        g = gv + b_ref[...]
        sg = (g[:, :sw] * _sigmoid(g[:, sw:])).astype(BF16)
        sg_ref[...] = sg
        ys = jnp.concatenate([jnp.dot(sg, wu_ref[s], preferred_element_type=F32) for s in range(N_CHIPS)], axis=1)
        ys_ref[...] = ys.astype(BF16)
        mg = (_sigmoid(glp_ref[...].astype(F32)) * yp_ref[...].astype(F32) + _sigmoid(gls_ref[...].astype(F32)) * ys).astype(BF16)
        mg_ref[...] = mg
        y = jnp.dot(mg, wo_ref[...], preferred_element_type=F32)
        y_ref[...] = y.astype(BF16)
        xn = x_ref[...] + gt_ref[...] * y
        xn_ref[...] = xn
        r = lax.rsqrt(jnp.mean(xn * xn, axis=-1, keepdims=True) + EPS)
        h_ref[...] = (xn * r * gn_ref[...] * (1.0 + sc_ref[...]) + sh_ref[...]).astype(BF16)

    full3 = lambda i: (0, 0, 0)
    full2 = lambda i: (0, 0)
    rows = lambda w: pl.BlockSpec((tm, w), lambda i: (i, 0))
    rowv = pl.BlockSpec((1, d), full2)
    return _pcall(
        body, name=name, grid=(t // tm,),
        in_specs=[rows(sw), pl.BlockSpec(w_glu.shape, full3), pl.BlockSpec(b_glu.shape, full2),
                  pl.BlockSpec(w_up.shape, full3), rows(d), pl.BlockSpec((tm, d), lambda i: (i, cb)),
                  pl.BlockSpec((tm, d), lambda i: (i, cb + 1)), pl.BlockSpec(w_out.shape, full2), rows(d), rowv, rowv, rowv, rowv],
        out_specs=[rows(N_CHIPS * ns), rows(sw), rows(d), rows(d), rows(d), rows(d), rows(d)],
        out_shape=[jax.ShapeDtypeStruct((t, N_CHIPS * ns), BF16), jax.ShapeDtypeStruct((t, sw), BF16)]
        + [jax.ShapeDtypeStruct((t, d), BF16)] * 4 + [jax.ShapeDtypeStruct((t, d), F32)],
        compiler_params=_params("parallel"))(ge, w_glu, b_glu, w_up, y_pool, uin, uin, w_out, x_in, gt, *nxt)


def _glu_up_bwd(name, dy, w_up, gv, b_glu, w_glu, sg, ge, tm=ROWS, deps=()):
    t, sw = ge.shape
    tm = min(tm, t)
    nb = t // tm
    ns = w_up.shape[2]
    nt = (((1,), (1,)), ((), ()))
    tn_ = (((0,), (0,)), ((), ()))

    def body(dy_ref, wu_ref, gv_ref, b_ref, wg_ref, sg_ref, ge_ref, *rest):
        dge_ref, db_ref, dwu_ref, dwg_ref, dwu_acc, dwg_acc = rest[len(deps):]
        i = pl.program_id(0)

        @pl.when(i == 0)
        def _():
            db_ref[...] = jnp.zeros_like(db_ref)
            dwu_acc[...] = jnp.zeros_like(dwu_acc)
            dwg_acc[...] = jnp.zeros_like(dwg_acc)

        dyv = dy_ref[...]
        dsg = None
        for s in range(N_CHIPS):
            q = lax.dot_general(dyv[:, s * ns:(s + 1) * ns], wu_ref[s], nt, preferred_element_type=F32)
            dsg = q if dsg is None else dsg + q
        g = gv_ref[...].astype(F32) + b_ref[...]
        val, sgm = g[:, :sw], _sigmoid(g[:, sw:])
        dval, dgate = dsg * sgm, dsg * val * sgm * (1.0 - sgm)
        db_ref[...] += jnp.concatenate([_colsum(dval), _colsum(dgate)], axis=1)
        dgv = jnp.concatenate([dval, dgate], axis=1).astype(BF16)
        dge = None
        for s in range(N_CHIPS):
            q = lax.dot_general(dgv[:, s * ns:(s + 1) * ns], wg_ref[s], nt, preferred_element_type=F32)
            dge = q if dge is None else dge + q
        dge_ref[...] = dge.astype(BF16)
        dwu_acc[...] += lax.dot_general(sg_ref[...], dyv, tn_, preferred_element_type=F32)
        dwg_acc[...] += lax.dot_general(ge_ref[...], dgv, tn_, preferred_element_type=F32)

        @pl.when(i == nb - 1)
        def _():
            for s in range(N_CHIPS):
                dwu_ref[s] = dwu_acc[:, s * ns:(s + 1) * ns].astype(BF16)
                dwg_ref[s] = dwg_acc[:, s * ns:(s + 1) * ns].astype(BF16)

    full3 = lambda i: (0, 0, 0)
    rows = lambda w: pl.BlockSpec((tm, w), lambda i: (i, 0))
    wide = N_CHIPS * ns
    return _pcall(
        body, name=name, grid=(nb,),
        in_specs=[rows(wide), pl.BlockSpec(w_up.shape, full3), rows(wide), pl.BlockSpec(b_glu.shape, lambda i: (0, 0)),
                  pl.BlockSpec(w_glu.shape, full3), rows(sw), rows(sw)] + [ANY_SPEC] * len(deps),
        out_specs=[rows(sw), pl.BlockSpec((1, wide), lambda i: (0, 0)), pl.BlockSpec(w_up.shape, full3),
                   pl.BlockSpec(w_glu.shape, full3)],
        out_shape=[jax.ShapeDtypeStruct((t, sw), BF16), jax.ShapeDtypeStruct((1, wide), F32),
                   jax.ShapeDtypeStruct(w_up.shape, BF16), jax.ShapeDtypeStruct(w_glu.shape, BF16)],
        scratch_shapes=[pltpu.VMEM((sw, wide), F32), pltpu.VMEM((sw, wide), F32)],
        compiler_params=_params("arbitrary"))(dy, w_up, gv, b_glu, w_glu, sg, ge, *deps)


_FULL = lambda i, j: (i, j)
_ROWV = lambda i, j: (0, j)


def _gates_bwd_epilogue(y_pool, y_ssm, uin, t, d, pw):
    cb = (2 * pw) // d

    def epi(dm, yp, ys, glp, gls):
        sp, ss = _sigmoid(glp.astype(F32)), _sigmoid(gls.astype(F32))
        dgl = jnp.concatenate([dm * yp.astype(F32) * sp * (1.0 - sp), dm * ys.astype(F32) * ss * (1.0 - ss)], axis=1)
        return dm * sp, dm * ss, ("at", 2 * pw, dgl)

    ins = [(y_pool, (ROWS, d), _FULL), (y_ssm, (ROWS, d), _FULL),
           (uin, (ROWS, d), lambda i, j: (i, cb)), (uin, (ROWS, d), lambda i, j: (i, cb + 1))]
    wide = 2 * pw + 2 * d
    return epi, ins, [((t, d), BF16, (ROWS, d), _FULL)] * 2 + [((t, wide), BF16, (ROWS, wide), _FULL)]


def _loss_epilogue(x_in, gt, tgt, g_final, t, d):
    blk = (ROWS, d)
    full = lambda i, j: (i, j)
    rowv = lambda i, j: (0, j)

    def epi(acc, res, scale, tv, gv):
        xv = res + (0.5 * scale) * acc
        r = lax.rsqrt(jnp.mean(xv * xv, axis=-1, keepdims=True) + EPS)
        xr = xv * r
        e = xr * gv - tv
        loss_row = 0.5 * jnp.mean(e * e, axis=-1, keepdims=True)
        dout = e * (1.0 / d)
        gd = gv * dout
        dx = r * (gd - xr * jnp.mean(gd * xr, axis=-1, keepdims=True))
        fdx = 0.5 * dx
        return [dx, scale * fdx], [_colsum(dout * xr), _colsum(loss_row * jnp.ones((1, 128), F32)), _colsum(fdx * acc)]

    ins = [(x_in, blk, full), (gt, (1, d), rowv), (tgt, blk, full), (g_final, (1, d), rowv)]
    return epi, ins, [((t, d), F32, blk, full), ((t, d), BF16, blk, full)], [d, 128, d]


def _norm_bwd_epilogue(x, dres, g, sc, up, t, d):
    blk = (ROWS, d)
    full = lambda i, j: (i, j)
    rowv = lambda i, j: (0, j)

    def epi(dhv, xv, drv, gv, scv, *rest):
        r = lax.rsqrt(jnp.mean(xv * xv, axis=-1, keepdims=True) + EPS)
        xr = xv * r
        dn = dhv * (1.0 + scv)
        gd = gv * dn
        dx = drv + r * (gd - xr * jnp.mean(gd * xr, axis=-1, keepdims=True))
        outs, reds = [dx], [_colsum(dhv), _colsum(dhv * xr * gv), _colsum(dn * xr)]
        if up:
            yv, gtv = rest
            fdx = up[2] * dx
            outs.append(gtv * fdx)
            reds.append(_colsum(fdx * yv.astype(F32)))
        return outs, reds

    ins = [(x, blk, full), (dres, blk, full), (g, (1, d), rowv), (sc, (1, d), rowv)]
    ins += [(up[0], blk, full), (up[1], (1, d), rowv)] if up else []
    outs = [((t, d), F32, blk, full)] + ([((t, d), BF16, blk, full)] if up else [])
    return epi, ins, outs, [d] * (4 if up else 3)


def _ffn_in_act(name, h, w_in, tm=ROWS):
    t, d = h.shape
    ns = w_in.shape[2]
    tm = _pick(t, tm, 8)
    w4 = w_in.reshape(2, 2, d, ns)

    def body(h_ref, w_ref, ab_ref, act_ref):
        hv = h_ref[...]
        a = jnp.dot(hv, w_ref[0], preferred_element_type=F32)
        b = jnp.dot(hv, w_ref[1], preferred_element_type=F32)
        ab_ref[0] = a.astype(BF16)
        ab_ref[1] = b.astype(BF16)
        act_ref[...] = (a * _sigmoid(a) * b).astype(BF16)

    return _pcall(body, name=name, grid=(2, t // tm),
                  in_specs=[pl.BlockSpec((tm, d), lambda c, i: (i, 0)), pl.BlockSpec((2, None, d, ns), lambda c, i: (0, c, 0, 0))],
                  out_specs=[pl.BlockSpec((2, tm, ns), lambda c, i: (0, i, c)), pl.BlockSpec((tm, ns), lambda c, i: (i, c))],
                  out_shape=[jax.ShapeDtypeStruct((2, t, 2 * ns), BF16), jax.ShapeDtypeStruct((t, 2 * ns), BF16)],
                  compiler_params=_params("parallel", "parallel"))(h, w4)


def _dswiglu_epilogue(ab, t, f, tn):
    blk = (2, ROWS, tn)
    idx = lambda i, j: (0, i, j)

    def epi(dact, abv):
        a, b = abv[0].astype(F32), abv[1].astype(F32)
        s = _sigmoid(a)
        return (jnp.stack([dact * b * (s * (1.0 + a * (1.0 - s))), dact * (a * s)]),)

    return epi, [(ab, blk, idx)], [((2, t, f), BF16, blk, idx)]


def _ffn_fwd(tag, x, h, gt, get_w_in, get_w_out, nxt=None, loss=None):
    t, d = x.shape
    ab, act = _ffn_in_act(tag + "_in", h, get_w_in(h))
    if loss:
        epi, epi_ins, epi_outs, epi_reds = _loss_epilogue(x, gt, *loss, t, d)
        res = _mm(tag + "_out", act, get_w_out(act), tm=ROWS, tn=d, epi=epi, epi_ins=epi_ins, epi_outs=epi_outs,
                  epi_reds=epi_reds, b_resident=True)
        return res, (x, h, ab, act, None)
    epi, epi_ins, epi_outs = _gate_norm_epilogue(0.5, x, gt, nxt, t, d)
    res = _mm(tag + "_out", act, get_w_out(act), tm=ROWS, tn=d, epi=epi, epi_ins=epi_ins, epi_outs=epi_outs, b_resident=True)
    return res[0], res[2], (x, h, ab, act, res[1])


def _ffn_bwd(tag, dx_new, dy, saved, g, sc, w_in, w_out, start_rs, up, deps=()):
    x, h, ab, act, _ = saved
    d = x.shape[1]
    f = act.shape[1]
    dw_out = _mm(tag + "_dwout", act, dy, ta=True, tm=1408, tn=d, tk=2048, deps=deps)
    tok = start_rs("out", dw_out.reshape(N_CHIPS, f // N_CHIPS, d))
    epi, epi_ins, epi_outs = _dswiglu_epilogue(ab, x.shape[0], f, w_in.shape[2])
    dab = _mm(tag + "_dact", dy, w_out, tb=True, tm=ROWS, tn=w_in.shape[2], epi=epi, epi_ins=epi_ins, epi_outs=epi_outs,
              deps=(tok,), j_outer=True)
    dw_in = _mm(tag + "_dwin", h, dab, ta=True, out_stacked=True, b_halves=True, tm=d, tn=1408, tk=2048)
    tok = start_rs("in", dw_in)
    epi, epi_ins, epi_outs, epi_reds = _norm_bwd_epilogue(x, dx_new, g, sc, up, x.shape[0], d)
    res = _mm(tag + "_dh", dab, w_in, tb=True, b_stacked=True, a_halves=True, tn=d, tk=5632, deps=(tok,), epi=epi,
              epi_ins=epi_ins, epi_outs=epi_outs, epi_reds=epi_reds, b_resident=True)
    if up:
        return res
    return res[0], None, res[1], res[2], res[3], None


def _row(v):
    return v.reshape(1, -1)


def _pack(parts):
    cols = []
    for p in parts:
        flat = p.reshape(-1).astype(F32)
        padn = (-flat.shape[0]) % 128
        cols.append(jnp.pad(flat, (0, padn)) if padn else flat)
    flat = jnp.concatenate(cols)
    padn = (-flat.shape[0]) % 1024
    if padn:
        flat = jnp.pad(flat, (0, padn))
    return flat.reshape(-1, 128)


def _unpack(packed, shapes):
    flat = packed.reshape(-1)
    out, off = [], 0
    for s in shapes:
        n = math.prod(s)
        out.append(flat[off:off + n].reshape(s))
        off += n + ((-n) % 128)
    return out


SMALL = ['b_ada', 'g_ffn1', 'g_mix', 'pool_w', 'pool_b', 'pool_scale', 'ssm_lam_re_log', 'ssm_lam_im', 'ssm_log_dt',
         'ssm_b_re', 'ssm_b_im', 'ssm_c_re', 'ssm_c_im', 'ssm_d', 'b_glu', 'g_ffn2', 'g_final']
BIG = ['w_ffn1_in', 'w_ffn1_out', 'w_in', 'w_pool_up', 'w_glu', 'w_ssm_up', 'w_out', 'w_ffn2_in', 'w_ffn2_out']
AG_GROUPS = [[0], [1], [2, 3, 4, 5, 6], [7, 8]]
AG_DIRECT = [False, True, False, True]
SMALL_LATE = ['b_ada', 'g_ffn1']
RS_SUM_GROUPS = [[0, 1, 2, 3], [4, 5]]
WEIGHTS = ['w_ada', 'b_ada', 'g_ffn1', 'w_ffn1_in', 'w_ffn1_out', 'g_mix', 'w_in', 'pool_w', 'pool_b', 'pool_scale', 'w_pool_up',
           'ssm_lam_re_log', 'ssm_lam_im', 'ssm_log_dt', 'ssm_b_re', 'ssm_b_im', 'ssm_c_re', 'ssm_c_im', 'ssm_d', 'w_glu', 'b_glu',
           'w_ssm_up', 'w_out', 'g_ffn2', 'w_ffn2_in', 'w_ffn2_out', 'g_final']


def kernel(x, c, w_ada, b_ada, g_ffn1, w_ffn1_in, w_ffn1_out, g_mix, w_in, pool_w, pool_b, pool_scale, w_pool_up, ssm_lam_re_log, ssm_lam_im, ssm_log_dt, ssm_b_re, ssm_b_im, ssm_c_re, ssm_c_im, ssm_d, w_glu, b_glu, w_ssm_up, w_out, g_ffn2, w_ffn2_in, w_ffn2_out, g_final, loss_target, m_w_ada, m_b_ada, m_g_ffn1, m_w_ffn1_in, m_w_ffn1_out, m_g_mix, m_w_in, m_pool_w, m_pool_b, m_pool_scale, m_w_pool_up, m_ssm_lam_re_log, m_ssm_lam_im, m_ssm_log_dt, m_ssm_b_re, m_ssm_b_im, m_ssm_c_re, m_ssm_c_im, m_ssm_d, m_w_glu, m_b_glu, m_w_ssm_up, m_w_out, m_g_ffn2, m_w_ffn2_in, m_w_ffn2_out, m_g_final, v_w_ada, v_b_ada, v_g_ffn1, v_w_ffn1_in, v_w_ffn1_out, v_g_mix, v_w_in, v_pool_w, v_pool_b, v_pool_scale, v_w_pool_up, v_ssm_lam_re_log, v_ssm_lam_im, v_ssm_log_dt, v_ssm_b_re, v_ssm_b_im, v_ssm_c_re, v_ssm_c_im, v_ssm_d, v_w_glu, v_b_glu, v_w_ssm_up, v_w_out, v_g_ffn2, v_w_ffn2_in, v_w_ffn2_out, v_g_final):
    args = dict(locals())
    wt = {n: args[n] for n in WEIGHTS}
    mom = {n: args["m_" + n] for n in WEIGHTS}
    var = {n: args["v_" + n] for n in WEIGHTS}

    t, d = x.shape[1], x.shape[2]
    pw = pool_b.shape[1]
    sw = ssm_d.shape[1]
    ngrp = sw // SSM_GROUP
    gn = ngrp * SSM_STATE
    xi, yi, ci = _place()
    b_me = 4 * xi + 2 * yi + ci
    s_me = 2 * xi + yi
    x2d = x[0]
    tgt = loss_target[0]

    c_all = _allgather_small("ag_c", c.reshape(8, d // 8)).reshape(N_DEV, d)
    ncol = w_ada.shape[2]
    b_sh = lax.dynamic_slice(b_ada, (0, s_me * ncol), (1, ncol))
    mod_sh = _ada_fwd("ada_fwd", c_all, w_ada[0], b_sh)
    md_send, md_recv, mod_sh, md_land, tok = _bcast_start("mod_start", mod_sh)

    shards = [wt[n][0].astype(BF16) for n in BIG]
    lands = _place_shards(shards, s_me)
    n0 = len(AG_GROUPS[0])
    sems_a, shards_a, lands_a, tok = _ag_start("ag_start_first", shards[:n0], lands[:n0], AG_GROUPS[:1], AG_DIRECT[:1], deps=(tok,))
    rest = [[w - n0 for w in grp] for grp in AG_GROUPS[1:]]
    sems_b, shards_b, lands_b, tok = _ag_start("ag_start_rest", shards[n0:], lands[n0:], rest, AG_DIRECT[1:], deps=(tok,))
    ag_sems, shards_t, lands_t = sems_a + sems_b, shards_a + shards_b, lands_a + lands_b
    md_land = _bcast_wait("mod_wait", mod_sh, md_land, md_send, md_recv, tok)
    mod_all = lax.dynamic_update_slice(md_land, mod_sh[None], (b_me, 0, 0))
    full = {}

    def weights(gi, *after):
        grp = AG_GROUPS[gi]
        if BIG[grp[0]] not in full:
            ls = _ag_wait("ag_wait%d" % gi, [shards_t[w] for w in grp], [lands_t[w] for w in grp], *ag_sems[gi],
                          AG_DIRECT[gi], after)
            for w, l in zip(grp, ls if AG_DIRECT[gi] else _ag_forward("ag_fwd%d" % gi, ls)):
                full[BIG[w]] = l
        return full

    mod_me = jnp.concatenate([lax.dynamic_slice(mod_all, (2 * s, b_me, 0), (1, 1, ncol))[0] for s in range(N_CHIPS)], axis=1)
    mod = mod_me.reshape(9, d)
    sh1, sc1, gt1, sh2, sc2, gt2, sh3, sc3, gt3 = [mod[k:k + 1] for k in range(9)]

    f = w_ffn1_out.shape[1] * N_CHIPS

    col = lambda a: a.reshape(gn, 1)
    lrl_c, li_c = col(ssm_lam_re_log), col(ssm_lam_im)
    ldt_c = col(jnp.broadcast_to(ssm_log_dt.reshape(ngrp, 1), (ngrp, SSM_STATE)))
    b_re2, b_im2 = ssm_b_re.reshape(gn, SSM_GROUP), ssm_b_im.reshape(gn, SSM_GROUP)
    lrdt_c, ang_c, bb_re, bb_im = _ssm_prep("ssm_prep", lrl_c, li_c, ldt_c, b_re2, b_im2)
    lrdt, ang = lrdt_c.reshape(1, gn), ang_c.reshape(1, gn)
    tr = lambda a: jnp.swapaxes(a, 1, 2)
    bbd = [_blockdiag_b(v, sw).astype(BF16) for v in (bb_re, bb_im)]
    ccd = [_blockdiag_c(v[0], sw).astype(BF16) for v in (ssm_c_re, ssm_c_im)]
    bbd_t, ccd_t = [tr(v) for v in bbd], [tr(v) for v in ccd]
    early = [n for n in SMALL if n not in SMALL_LATE]
    packs = {}
    for tag, names, extra in (("early", early, []), ("late", SMALL_LATE, [jnp.zeros((128,), F32)])):
        packs[tag] = [_pack([src[n] for n in names] + extra) for src in (wt, mom, var)]
    shadow = [lrdt, ang, ldt_c, *bbd, *ccd, *bbd_t, *ccd_t, *packs["early"], *packs["late"]]

    h1 = _norm_fwd("ffn1_norm", x2d, g_ffn1, sc1, sh1, deps=(tok,))
    x1, h2, sav1 = _ffn_fwd("ffn1", x2d, h1, gt1, lambda after: weights(0, after, *shadow)['w_ffn1_in'],
                            lambda after: weights(1, after)['w_ffn1_out'].reshape(f, d), (g_mix, sc2, sh2))
    weights(2, h2)
    wo = full['w_out'].reshape(d, d)
    uin = _mm("mix_in", h2, full['w_in'], b_stacked=True, tm=2048, tn=768, j_outer=True)
    p_pool, z_pool, y_pool = _pool_fwd("pool_fwd", uin, pool_w[0], pool_b, pool_scale, full['w_pool_up'], pw)
    y_s, ge, s_re, s_im = _ssm_fwd("ssm_fwd", uin, lrdt, ang, *bbd, *ccd, ssm_d, sw)
    gv, sg, y_ssm, merged, y2, h3, x2 = _mixer_tail("mixer_tail", ge, full['w_glu'], b_glu, full['w_ssm_up'], y_pool, uin, wo,
                                                    x1, gt2, (g_ffn2, sc3, sh3), d, pw)

    (dx3, dy3, dg_final, loss_v, dgt3), sav3 = _ffn_fwd(
        "ffn2", x2, h3, gt3, lambda after: weights(3, after)['w_ffn2_in'],
        lambda after: weights(3, after)['w_ffn2_out'].reshape(f, d), loss=(tgt, _row(g_final)))

    gs = {}
    rs_open = []

    def start_rs(names, gbs):
        send, recv, g_thru, land_thru, token = _rs_start("rs_start_" + names[0], gbs)
        rs_open.append((names, send, recv, g_thru, land_thru))
        return token

    dx2, dy2, dsh3, dsc3, gs['g_ffn2'], dgt2 = _ffn_bwd(
        "ffn2b", dx3, dy3, sav3, g_ffn2, sc3, full['w_ffn2_in'], full['w_ffn2_out'].reshape(f, d),
        lambda key, g: start_rs(['w_ffn2_' + key], [g]), (y2, gt2, 1.0))

    epi, epi_ins, epi_outs = _gates_bwd_epilogue(y_pool, y_ssm, uin, t, d, pw)
    dy_pool, dy_ssm, duin, g_wo = _mm("mix_dmerged", dy2, wo, tb=True, tm=ROWS, tn=d, epi=epi, epi_ins=epi_ins,
                                      epi_outs=epi_outs, b_resident=True, wgrad=merged, wgrad_rows=True)
    g_wo = g_wo.reshape(N_CHIPS, d // N_CHIPS, d)

    duin, gs['pool_w'], gs['pool_b'], gs['pool_scale'], g_wpu = _pool_bwd(
        "pool_bwd", dy_pool, z_pool, p_pool, pool_w[0], pool_b, pool_scale, full['w_pool_up'], duin)

    tok = start_rs(['w_out', 'w_pool_up'], [g_wo, g_wpu])
    dge, gs['b_glu'], g_wsu, g_wglu = _glu_up_bwd("glu_up_bwd", dy_ssm, full['w_ssm_up'], gv, b_glu, full['w_glu'], sg, ge,
                                                  deps=(tok,))
    (duin, g_abre, g_abim, g_bbd_re, g_bbd_im, g_ccd_re, g_ccd_im, gs['ssm_d']) = _ssm_bwd(
        "ssm_bwd", dge, y_s, uin, s_re, s_im, lrdt, ang, *bbd_t, *ccd_t, ssm_d, sw, duin)
    gs['ssm_c_re'], gs['ssm_c_im'] = _diag_of_c(g_ccd_re, sw), _diag_of_c(g_ccd_im, sw)
    d_lrl, d_li, d_ldt, d_bre, d_bim = _ssm_param_bwd(
        "ssm_param_bwd", lrl_c, li_c, ldt_c, b_re2, b_im2, g_abre.reshape(gn, 1), g_abim.reshape(gn, 1),
        _diag_of_b(g_bbd_re, sw), _diag_of_b(g_bbd_im, sw))
    gs['ssm_lam_re_log'], gs['ssm_lam_im'] = d_lrl, d_li
    gs['ssm_log_dt'] = jnp.sum(d_ldt.reshape(ngrp, SSM_STATE), axis=1)
    gs['ssm_b_re'], gs['ssm_b_im'] = d_bre, d_bim

    g_win =_mm("mix_dwin", h2, duin, ta=True, out_stacked=True, tm=d, tn=768, tk=4096)
    tok = start_rs(['w_ssm_up', 'w_glu', 'w_in'], [g_wsu, g_wglu, g_win])
    epi, epi_ins, epi_outs, epi_reds = _norm_bwd_epilogue(x1, dx2, g_mix, sc2, (sav1[4], gt1, 0.5), t, d)
    dx1, dy1, dsh2, dsc2, gs['g_mix'], dgt1 = _mm(
        "mix_dh", duin, full['w_in'], tb=True, b_stacked=True, tn=d, tk=3072, deps=(tok,), epi=epi, epi_ins=epi_ins,
        epi_outs=epi_outs, epi_reds=epi_reds, b_resident=True)

    gs['g_final'] = dg_final
    sg_blk = _pack([gs[n] for n in early])
    sg_send, sg_recv, sg_blk, sg_land, tok = _bcast_start("sg_start", sg_blk)

    dx0, _, dsh1, dsc1, gs['g_ffn1'], _ = _ffn_bwd(
        "ffn1b", dx1, dy1, sav1, g_ffn1, sc1, full['w_ffn1_in'], full['w_ffn1_out'].reshape(f, d),
        lambda key, g: start_rs(['w_ffn1_' + key], [g]), None, deps=(tok,))

    gs['b_ada'] = jnp.concatenate([dsh1, dsc1, dgt1, dsh2, dsc2, dgt2, dsh3, dsc3, dgt3], axis=1)
    lt_blk = _pack([gs[n] for n in SMALL_LATE] + [loss_v])
    lt_send, lt_recv, lt_blk, lt_land, tok = _bcast_start("late_start", lt_blk)

    grads, delta, new_m, new_v = {}, {}, {}, {}

    def small_update(tag, names, blk, land):
        g8 = lax.dynamic_update_slice(land, blk[None], (b_me, 0, 0)).reshape(-1, 128)
        w_pack, m_pack, v_pack = packs[tag]
        shapes = [wt[n].shape for n in names]
        per_param, packed = _adamw_small("adamw_small_" + tag, w_pack, g8, m_pack, v_pack, shapes)
        for k, dst in enumerate((grads, delta, new_m, new_v)):
            rest = _unpack(packed[k], shapes)
            for i, n in enumerate(names):
                dst[n] = per_param[i][k] if per_param[i] is not None else rest[i]
        return g8

    after = tok
    for group in RS_SUM_GROUPS:
        names, g_done, land_done = [], [], []
        for k in group:
            nk, send, recv, g_thru, land_thru = rs_open[k]
            gd, ld = _rs_wait("rs_wait_" + nk[0], g_thru, land_thru, send, recv, after)
            names, g_done, land_done = names + nk, g_done + gd, land_done + ld
        for n, g_sum in zip(names, _rs_sum("rs_sum_" + names[0], g_done, land_done)):
            g_out, dl, mn, vn = _adamw("adamw_" + n, wt[n][0], g_sum, mom[n][0], var[n][0])
            grads[n], delta[n], new_m[n], new_v[n] = g_out[None], dl[None], mn[None], vn[None]
            after = dl
        if group is RS_SUM_GROUPS[-2]:
            small_update("early", early, sg_blk, _bcast_wait("sg_wait", sg_blk, sg_land, sg_send, sg_recv, after))

    lt_land = _bcast_wait("late_wait", lt_blk, lt_land, lt_send, lt_recv, after)
    late8 = small_update("late", SMALL_LATE, lt_blk, lt_land).reshape(N_DEV, -1)
    loss = jnp.sum(late8[:, 10 * d])
    dmod_sh = lax.dynamic_slice(late8, (0, s_me * ncol), (N_DEV, ncol))
    g_w_ada = _ada_bwd("ada_bwd", c_all, dmod_sh)
    dl, mn, vn = _adamw("adamw_w_ada", w_ada[0], g_w_ada, m_w_ada[0], v_w_ada[0], with_g=False)
    grads['w_ada'], delta['w_ada'], new_m['w_ada'], new_v['w_ada'] = g_w_ada[None], dl[None], mn[None], vn[None]

    return (loss, dx0[None], *[grads[n] for n in WEIGHTS], *[delta[n] for n in WEIGHTS],
            *[new_m[n] for n in WEIGHTS], *[new_v[n] for n in WEIGHTS])
```

```python
import functools
import math

import jax
import jax.numpy as jnp
from jax import lax
from jax.experimental import pallas as pl
from jax.experimental.pallas import tpu as pltpu

F32 = jnp.float32
BF16 = jnp.bfloat16
MESH = pl.DeviceIdType.MESH

EPS = 1e-6
POOL_WINDOWS = (2, 4, 8, 16)
POOL_HALO = 16
SSM_GROUP = 16
SSM_STATE = 64
SSM_BLOCKS = 4
N_DEV = 8
N_CHIPS = 4
ADAM_LR = 0.001
ADAM_B1 = 0.9
ADAM_B2 = 0.999
ADAM_EPS = 1e-08
ADAM_WD = 0.01
ADAM_STEP = 10
VMEM_LIMIT = 56 * 1024 * 1024
ROWS = 512


ANY_SPEC = pl.BlockSpec(memory_space=pl.ANY)
HBM_SPEC = pl.BlockSpec(memory_space=pltpu.HBM)
SEM_SPEC = pl.BlockSpec(memory_space=pltpu.SEMAPHORE)
EFFECT = pltpu.SideEffectType.DATAFLOW_SIDE_EFFECTING


def _hbm(a):
    return pltpu.with_memory_space_constraint(a, pltpu.HBM)


def _pcall(body, **kw):
    return pl.pallas_call(body, **kw)


def _params(*sem):
    return pltpu.CompilerParams(dimension_semantics=sem, vmem_limit_bytes=VMEM_LIMIT)


def _pick(n, cap, mult=128):
    if n <= cap:
        return n
    best = None
    for d in range(mult, cap + 1, mult):
        if n % d == 0:
            best = d
    assert best is not None, (n, cap, mult)
    return best


def _sigmoid(v):
    return 1.0 / (1.0 + jnp.exp(-v))


def _rowwise(name, fn, ins, params, outs, reds, tm, deps=()):
    t = ins[0][0].shape[0]
    tm = min(tm, t)
    nb = t // tm
    ni, npar, no, nd = len(ins), len(params), len(outs), len(deps)

    def body(*refs):
        iv = [r[...] for r in refs[:ni]]
        pv = [r[...] for r in refs[ni:ni + npar]]
        o_refs = refs[ni + npar + nd:ni + npar + nd + no]
        r_refs = refs[ni + npar + nd + no:]
        ovals, rvals = fn(iv, pv)
        for o_ref, val in zip(o_refs, ovals):
            off = 0
            if isinstance(val, tuple) and val[0] == "at":
                _, off, val = val
            parts = val if isinstance(val, (list, tuple)) else [val]
            for p in parts:
                o_ref[:, off:off + p.shape[1]] = p.astype(o_ref.dtype)
                off += p.shape[1]
        if r_refs:
            @pl.when(pl.program_id(0) == 0)
            def _():
                for r in r_refs:
                    r[...] = jnp.zeros_like(r)
            for r, val in zip(r_refs, rvals):
                r[...] += val

    in_specs = [pl.BlockSpec((tm, w), functools.partial(lambda i, cb: (i, cb), cb=cb)) for (_, w, cb) in ins]
    in_specs += [pl.BlockSpec(p.shape, lambda i: (0, 0)) for p in params]
    in_specs += [ANY_SPEC] * nd
    out_shape = [jax.ShapeDtypeStruct((t, w), dt) for (w, dt) in outs]
    out_shape += [jax.ShapeDtypeStruct((1, w), F32) for w in reds]
    out_specs = [pl.BlockSpec((tm, w), lambda i: (i, 0)) for (w, _) in outs]
    out_specs += [pl.BlockSpec((1, w), lambda i: (0, 0)) for w in reds]
    res = _pcall(body, name=name, grid=(nb,), in_specs=in_specs, out_specs=out_specs, out_shape=out_shape,
                 compiler_params=_params("arbitrary"))(*[a for a, _, _ in ins], *params, *deps)
    return res


def _colsum(v):
    return jnp.sum(v, axis=0, keepdims=True)


def _mm(name, a, b, *, ta=False, tb=False, b_stacked=False, out_stacked=False, tm=ROWS, tn=1024, tk=2816,
        out_dtype=BF16, epi=None, epi_ins=(), epi_outs=None, epi_reds=(), deps=(), j_outer=False, a_halves=False,
        b_halves=False, b_resident=False, wgrad=None, wgrad_rows=False):
    if a_halves:
        _, m, kdim = a.shape
        kdim *= 2
    elif ta:
        kdim, m = a.shape
    else:
        m, kdim = a.shape
    ns = None
    if b_stacked:
        ns = b.shape[2]
        n = b.shape[1] if tb else N_CHIPS * ns
        assert kdim == (N_CHIPS * ns if tb else b.shape[1]), (name, a.shape, b.shape)
    else:
        if b_halves:
            n = 2 * b.shape[2]
            assert kdim == b.shape[1] and not tb, (name, a.shape, b.shape)
        else:
            n = b.shape[0] if tb else b.shape[1]
            assert kdim == (b.shape[1] if tb else b.shape[0]), (name, a.shape, b.shape)
        if out_stacked:
            ns = n // N_CHIPS

    def shards(want):
        return max(g for g in (1, 2, 4) if g * ns <= max(want, ns))

    tm = _pick(m, tm, 128 if ta else 8)
    gn = gk = 1
    if (b_stacked and not tb) or out_stacked:
        gn = shards(min(tn, n // 2) if b_halves else tn)
        tn = gn * ns
    else:
        tn = _pick(n, tn)
    if b_stacked and tb:
        gk = shards(tk)
        tk = gk * ns
    else:
        tk = _pick(kdim, tk, 8 if ta else 128)
    nm, nn, nk = m // tm, n // tn, kdim // tk

    def ij(f):
        return (lambda g0, g1, k: f(g1, g0, k)) if j_outer else f

    if a_halves:
        assert b_stacked and tb and gk == N_CHIPS and nk == 1, name
        a_spec = pl.BlockSpec((2, tm, kdim // 2), ij(lambda i, j, k: (0, i, 0)))
    elif ta:
        a_spec = pl.BlockSpec((tk, tm), ij(lambda i, j, k: (k, i)))
    else:
        a_spec = pl.BlockSpec((tm, tk), ij(lambda i, j, k: (i, k)))
    if b_stacked and not tb:
        b_spec = pl.BlockSpec((gn, tk, ns), ij(lambda i, j, k: (j, k, 0)))
    elif b_stacked and tb:
        b_spec = pl.BlockSpec((gk, tn, ns), ij(lambda i, j, k: (k, j, 0)))
    elif b_halves:
        bph = (n // 2) // tn
        b_spec = pl.BlockSpec((None, tk, tn), ij(lambda i, j, k: (j // bph, k, j % bph)))
    elif tb:
        b_spec = pl.BlockSpec((tn, tk), ij(lambda i, j, k: (j, k)))
    else:
        b_spec = pl.BlockSpec((tk, tn), ij(lambda i, j, k: (k, j)))
    dims = (((0 if ta else 1,), (1 if tb else 0,)), ((), ()))

    if epi_outs is None:
        if out_stacked:
            epi_outs = [((N_CHIPS, m, ns), out_dtype, (gn, tm, ns), lambda i, j: (j, i, 0))]
        else:
            epi_outs = [((m, n), out_dtype, (tm, tn), lambda i, j: (i, j))]
    ne, no, nd, nr = len(epi_ins), len(epi_outs), len(deps), len(epi_reds)
    nx = 0 if wgrad is None else 1
    assert not (nr or nx) or (nn == 1 and not j_outer), name
    assert not nx or (nk == 1 and not ta and not a_halves and kdim % N_CHIPS == 0), name

    def body(a_ref, b_ref, *rest):
        e_refs = rest[:ne]
        x_refs = rest[ne:ne + nx]
        o_refs = rest[ne + nx + nd:ne + nx + nd + no]
        r_refs = rest[ne + nx + nd + no:ne + nx + nd + no + nr]
        wg_refs = rest[ne + nx + nd + no + nr:ne + nx + nd + no + nr + nx]
        scratch = rest[ne + nx + nd + no + nr + nx:]
        av = None if a_halves else a_ref[...].astype(BF16)
        if nx:
            wg_acc = scratch[-1]
            i = pl.program_id(0)
            pw_ = lax.dot_general(x_refs[0][...].astype(BF16), av, (((0,), (0,)), ((), ())), preferred_element_type=F32)

            @pl.when(i == 0)
            def _():
                wg_acc[...] = pw_

            @pl.when(i > 0)
            def _():
                wg_acc[...] += pw_

            @pl.when(i == nm - 1)
            def _():
                if wgrad_rows:
                    wg_refs[0][...] = wg_acc[...].astype(BF16)
                else:
                    ks = kdim // N_CHIPS
                    for s in range(N_CHIPS):
                        wg_refs[0][s] = wg_acc[:, s * ks:(s + 1) * ks].astype(BF16)
        if b_stacked and not tb:
            parts = [lax.dot_general(av, b_ref[s].astype(BF16), dims, preferred_element_type=F32) for s in range(gn)]
        elif b_stacked and tb:
            p = None
            for s in range(gk):
                if a_halves:
                    a_s = a_ref[s // 2, :, (s % 2) * ns:(s % 2 + 1) * ns].astype(BF16)
                else:
                    a_s = av[:, s * ns:(s + 1) * ns]
                q = lax.dot_general(a_s, b_ref[s].astype(BF16), dims, preferred_element_type=F32)
                p = q if p is None else p + q
            parts = [p]
        else:
            parts = [lax.dot_general(av, b_ref[...].astype(BF16), dims, preferred_element_type=F32)]

        def finish(acc_parts):
            if epi is None and out_stacked:
                acc = acc_parts[0]
                for s in range(gn):
                    o_refs[0][s] = acc[:, s * ns:(s + 1) * ns].astype(out_dtype)
            elif epi is None:
                w = acc_parts[0].shape[1]
                for s, part in enumerate(acc_parts):
                    o_refs[0][:, s * w:(s + 1) * w] = part.astype(out_dtype)
            else:
                acc = acc_parts[0] if len(acc_parts) == 1 else jnp.concatenate(acc_parts, axis=1)
                vals = epi(acc, *[r[...] for r in e_refs])
                if nr:
                    vals, reds = vals

                    @pl.when(pl.program_id(0) == 0)
                    def _():
                        for r in r_refs:
                            r[...] = jnp.zeros_like(r)
                    for r, val in zip(r_refs, reds):
                        r[...] += val
                for o_ref, val in zip(o_refs, vals):
                    if isinstance(val, tuple) and val[0] == "at":
                        o_ref[:, val[1]:val[1] + val[2].shape[1]] = val[2].astype(o_ref.dtype)
                    else:
                        o_ref[...] = val.astype(o_ref.dtype)

        if nk == 1:
            finish(parts)
        else:
            acc_ref = scratch[0]
            k = pl.program_id(2)
            w = parts[0].shape[1]

            @pl.when(k == 0)
            def _():
                for s, part in enumerate(parts):
                    acc_ref[:, s * w:(s + 1) * w] = part

            @pl.when(k > 0)
            def _():
                for s, part in enumerate(parts):
                    acc_ref[:, s * w:(s + 1) * w] += part

            @pl.when(k == nk - 1)
            def _():
                finish([acc_ref[...]])

    def _ij(f):
        return ij(lambda i, j, k: f(i, j))

    if b_resident:
        assert nk == 1 and (nn == 1 or j_outer), name
        b_spec = pl.BlockSpec(b_spec.block_shape, b_spec.index_map, pipeline_mode=pl.Buffered(1))
    in_specs = [a_spec, b_spec] + [pl.BlockSpec(blk, _ij(f)) for (_, blk, f) in epi_ins]
    if nx:
        in_specs.append(pl.BlockSpec((tm, wgrad.shape[1]), lambda i, j, k: (i, 0)))
    in_specs += [ANY_SPEC] * nd
    out_specs = [pl.BlockSpec(blk, _ij(f)) for (_, _, blk, f) in epi_outs]
    out_specs += [pl.BlockSpec((1, w), lambda *_: (0, 0)) for w in epi_reds]
    out_shape = [jax.ShapeDtypeStruct(s, dt) for (s, dt, _, _) in epi_outs] + [jax.ShapeDtypeStruct((1, w), F32) for w in epi_reds]
    scratch = [pltpu.VMEM((tm, tn), F32)] if nk > 1 else []
    if nx:
        wg_shape = (wgrad.shape[1], kdim) if wgrad_rows else (N_CHIPS, wgrad.shape[1], kdim // N_CHIPS)
        out_specs.append(pl.BlockSpec(wg_shape, lambda *_: (0,) * len(wg_shape)))
        out_shape.append(jax.ShapeDtypeStruct(wg_shape, BF16))
        scratch.append(pltpu.VMEM((wgrad.shape[1], kdim), F32))
    grid = (nn, nm, nk) if j_outer else (nm, nn, nk)
    sem = ("arbitrary",) * 3 if (nr or nx) else ("parallel", "parallel", "arbitrary")
    res = _pcall(body, name=name, grid=grid, in_specs=in_specs, out_specs=out_specs, out_shape=out_shape, scratch_shapes=scratch,
                 compiler_params=_params(*sem))(a, b, *[x for x, _, _ in epi_ins], *([wgrad] * nx), *deps)
    return res[0] if len(res) == 1 else res


def _norm_fwd(name, x, g, sc, sh, deps=()):
    d = x.shape[1]

    def fn(iv, pv):
        (xv,), (gv, scv, shv) = iv, pv
        r = lax.rsqrt(jnp.mean(xv * xv, axis=-1, keepdims=True) + EPS)
        return [xv * r * gv * (1.0 + scv) + shv], []

    return _rowwise(name, fn, [(x, d, 0)], [g, sc, sh], [(d, BF16)], [], ROWS, deps=deps)[0]


def _adamw(name, w, g, m, v, tm=256):
    c = w.shape[1]

    def fn(iv, pv):
        wv, gv, mv, vv = iv
        mn = ADAM_B1 * mv + (1.0 - ADAM_B1) * gv
        vn = ADAM_B2 * vv + (1.0 - ADAM_B2) * (gv * gv)
        m_hat = mn / (1.0 - ADAM_B1 ** ADAM_STEP)
        v_hat = vn / (1.0 - ADAM_B2 ** ADAM_STEP)
        delta = -ADAM_LR * (m_hat / (jnp.sqrt(v_hat) + ADAM_EPS) + ADAM_WD * wv)
        return [gv, delta, mn, vn], []

    return _rowwise(name, fn, [(w, c, 0), (g, c, 0), (m, c, 0), (v, c, 0)], [], [(c, F32)] * 4, [], _pick(w.shape[0], tm, 8))


def _unpack_plan(shape):
    n = math.prod(shape)
    if len(shape) == 2 and shape[0] == 1 and n % 128 == 0:
        return [((slice(None), slice(128 * r, 128 * (r + 1))), slice(r, r + 1), slice(None)) for r in range(n // 128)]
    if len(shape) == 2 and shape[0] == 1 and n < 128:
        return [((slice(None), slice(None)), slice(0, 1), slice(0, n))]
    if len(shape) == 1 and n % 128 == 0:
        return [((slice(128 * r, 128 * (r + 1)),), r, slice(None)) for r in range(n // 128)]
    if len(shape) == 3 and shape[0] == 1 and shape[2] == 64:
        return [((0, slice(2 * r + h, 2 * r + h + 1), slice(None)), slice(r, r + 1), slice(64 * h, 64 * (h + 1)))
                for r in range(n // 128) for h in range(2)]
    if len(shape) == 4 and shape[0] == 1 and shape[2:] == (128, 128):
        return [((0, k), slice(128 * k, 128 * (k + 1)), slice(None)) for k in range(shape[1])]
    return None


def _adamw_small(name, w, g8, m, v, shapes):
    r = w.shape[0]
    plans, rows0, off = [], [], 0
    for s in shapes:
        plans.append(_unpack_plan(s))
        rows0.append(off // 128)
        off += math.prod(s) + (-math.prod(s)) % 128
    direct = [i for i, p in enumerate(plans) if p is not None]

    def body(w_ref, g_ref, m_ref, v_ref, *out):
        packed = out[4 * len(direct):]
        gv = g_ref[0:r, :]
        for k in range(1, N_DEV):
            gv = gv + g_ref[k * r:(k + 1) * r, :]
        mn = ADAM_B1 * m_ref[...] + (1.0 - ADAM_B1) * gv
        vn = ADAM_B2 * v_ref[...] + (1.0 - ADAM_B2) * (gv * gv)
        m_hat = mn / (1.0 - ADAM_B1 ** ADAM_STEP)
        v_hat = vn / (1.0 - ADAM_B2 ** ADAM_STEP)
        packed[0][...] = gv
        packed[1][...] = -ADAM_LR * (m_hat / (jnp.sqrt(v_hat) + ADAM_EPS) + ADAM_WD * w_ref[...])
        packed[2][...] = mn
        packed[3][...] = vn
        for di, i in enumerate(direct):
            for kind in range(4):
                o_ref, src = out[4 * di + kind], packed[kind]
                for o_idx, row, lanes in plans[i]:
                    row = (rows0[i] + row) if isinstance(row, int) else slice(rows0[i] + row.start, rows0[i] + row.stop)
                    o_ref[o_idx] = src[row, lanes]

    out_shape = [jax.ShapeDtypeStruct(shapes[i], F32) for i in direct for _ in range(4)]
    out_shape += [jax.ShapeDtypeStruct((r, 128), F32)] * 4
    res = _pcall(body, name=name, out_shape=out_shape,
                 compiler_params=pltpu.CompilerParams(vmem_limit_bytes=VMEM_LIMIT))(w, g8, m, v)
    per_param = [None] * len(shapes)
    for di, i in enumerate(direct):
        per_param[i] = res[4 * di:4 * di + 4]
    return per_param, res[4 * len(direct):]


def _pool_fwd(name, uin, pool_w, pool_b, pool_scale, w_up, pw, tm=ROWS):
    t = uin.shape[0]
    tm = min(tm, t)
    ng = len(POOL_WINDOWS)
    gw = pw // ng
    ns = w_up.shape[2]

    def body(u_ref, w_ref, b_ref, s_ref, wu_ref, p_ref, z_ref, y_ref, ext):
        i = pl.program_id(0)

        @pl.when(i == 0)
        def _():
            ext[0:POOL_HALO, :] = jnp.zeros((POOL_HALO, pw), F32)

        u = u_ref[...].astype(F32)
        ext[POOL_HALO:POOL_HALO + tm, :] = u
        pos = i * tm + lax.broadcasted_iota(jnp.int32, (tm, 1), 0)
        for k, win in enumerate(POOL_WINDOWS):
            cols = slice(k * gw, (k + 1) * gw)
            acc = u[:, cols]
            for j in range(1, win):
                acc = acc + ext[POOL_HALO - j:POOL_HALO - j + tm, cols]
            cnt = jnp.minimum(pos + 1, win).astype(F32)
            z = acc / cnt - u[:, cols]
            zp = jnp.dot(z.astype(BF16), w_ref[k].astype(BF16), preferred_element_type=F32) + b_ref[:, cols]
            p_ref[:, cols] = (zp * s_ref[:, cols]).astype(BF16)
            z_ref[:, cols] = z.astype(BF16)
        ext[0:POOL_HALO, :] = u[tm - POOL_HALO:tm, :]
        pv = p_ref[...]
        for s in range(N_CHIPS):
            y_ref[:, s * ns:(s + 1) * ns] = jnp.dot(pv, wu_ref[s], preferred_element_type=F32).astype(BF16)

    full3 = lambda i: (0, 0, 0)
    return _pcall(
        body, name=name, grid=(t // tm,),
        in_specs=[pl.BlockSpec((tm, pw), lambda i: (i, 0)), pl.BlockSpec(pool_w.shape, full3),
                  pl.BlockSpec(pool_b.shape, lambda i: (0, 0)), pl.BlockSpec(pool_scale.shape, lambda i: (0, 0)),
                  pl.BlockSpec(w_up.shape, full3)],
        out_specs=[pl.BlockSpec((tm, pw), lambda i: (i, 0))] * 2 + [pl.BlockSpec((tm, N_CHIPS * ns), lambda i: (i, 0))],
        out_shape=[jax.ShapeDtypeStruct((t, pw), BF16)] * 2 + [jax.ShapeDtypeStruct((t, N_CHIPS * ns), BF16)],
        scratch_shapes=[pltpu.VMEM((POOL_HALO + tm, pw), F32)],
        compiler_params=_params("arbitrary"))(uin, pool_w, pool_b, pool_scale, w_up)


def _pool_bwd(name, dy, z, p, pool_w, pool_b, pool_scale, w_up, duin, tm=ROWS):
    t, pw = z.shape
    tm = min(tm, t)
    nb = t // tm
    ng = len(POOL_WINDOWS)
    gw = pw // ng
    ns = w_up.shape[2]

    def body(dy_ref, z_ref, p_ref, w_ref, b_ref, s_ref, wu_ref, duin_ref, du_ref, dw_ref, db_ref, ds_ref, dwu_ref, ext, dwu_acc):
        i = pl.program_id(0)

        @pl.when(i == 0)
        def _():
            ext[tm:tm + POOL_HALO, :] = jnp.zeros((POOL_HALO, pw), F32)
            dw_ref[...] = jnp.zeros_like(dw_ref)
            db_ref[...] = jnp.zeros_like(db_ref)
            ds_ref[...] = jnp.zeros_like(ds_ref)
            dwu_acc[...] = jnp.zeros_like(dwu_acc)

        dwu_acc[...] += lax.dot_general(p_ref[...], dy_ref[...], (((0,), (0,)), ((), ())), preferred_element_type=F32)

        @pl.when(i == nb - 1)
        def _():
            for s in range(N_CHIPS):
                dwu_ref[s] = dwu_acc[:, s * ns:(s + 1) * ns].astype(BF16)

        pos = (nb - 1 - i) * tm + lax.broadcasted_iota(jnp.int32, (tm, 1), 0)
        dp = None
        for s in range(N_CHIPS):
            q = lax.dot_general(dy_ref[:, s * ns:(s + 1) * ns], wu_ref[s], (((1,), (1,)), ((), ())), preferred_element_type=F32)
            dp = q if dp is None else dp + q
        for k, win in enumerate(POOL_WINDOWS):
            cols = slice(k * gw, (k + 1) * gw)
            zk = z_ref[:, cols]
            dpk = dp[:, cols]
            wk = w_ref[k].astype(BF16)
            zp = jnp.dot(zk, wk, preferred_element_type=F32) + b_ref[:, cols]
            ds_ref[:, cols] += _colsum(dpk * zp)
            dzp = dpk * s_ref[:, cols]
            db_ref[:, cols] += _colsum(dzp)
            dzpb = dzp.astype(BF16)
            dz = lax.dot_general(dzpb, wk, (((1,), (1,)), ((), ())), preferred_element_type=F32)
            dw_ref[k] += lax.dot_general(zk, dzpb, (((0,), (0,)), ((), ())), preferred_element_type=F32)
            cnt = jnp.minimum(pos + 1, win).astype(F32)
            r = dz / cnt
            ext[0:tm, cols] = r
            acc = r - dz
            for j in range(1, win):
                acc = acc + ext[j:j + tm, cols]
            du_ref[:, cols] = acc.astype(BF16)
        ext[tm:tm + POOL_HALO, :] = ext[0:POOL_HALO, :]

    rev = lambda i: (nb - 1 - i, 0)
    full3 = lambda i: (0, 0, 0)
    return _pcall(
        body, name=name, grid=(nb,),
        in_specs=[pl.BlockSpec((tm, N_CHIPS * ns), rev), pl.BlockSpec((tm, pw), rev), pl.BlockSpec((tm, pw), rev),
                  pl.BlockSpec(pool_w.shape, full3), pl.BlockSpec(pool_b.shape, lambda i: (0, 0)),
                  pl.BlockSpec(pool_scale.shape, lambda i: (0, 0)), pl.BlockSpec(w_up.shape, full3), ANY_SPEC],
        out_specs=[pl.BlockSpec((tm, pw), rev), pl.BlockSpec(pool_w.shape, full3),
                   pl.BlockSpec((1, pw), lambda i: (0, 0)), pl.BlockSpec((1, pw), lambda i: (0, 0)), pl.BlockSpec(w_up.shape, full3)],
        out_shape=[jax.ShapeDtypeStruct(duin.shape, BF16), jax.ShapeDtypeStruct(pool_w.shape, F32),
                   jax.ShapeDtypeStruct((1, pw), F32), jax.ShapeDtypeStruct((1, pw), F32), jax.ShapeDtypeStruct(w_up.shape, BF16)],
        scratch_shapes=[pltpu.VMEM((tm + POOL_HALO, pw), F32), pltpu.VMEM((pw, N_CHIPS * ns), F32)], input_output_aliases={7: 0},
        compiler_params=_params("arbitrary"))(dy, z, p, pool_w, pool_b, pool_scale, w_up, duin)


def _ssm_disc(lrl, li, ldt):
    lr = -jnp.exp(lrl)
    dt = jnp.exp(ldt)
    mag = jnp.exp(lr * dt)
    ang = li * dt
    ab_re = mag * jnp.cos(ang)
    ab_im = mag * jnp.sin(ang)
    num_re = ab_re - 1.0
    num_im = ab_im
    den = lr * lr + li * li
    f_re = (num_re * lr + num_im * li) / den
    f_im = (num_im * lr - num_re * li) / den
    return lr, dt, mag, ang, ab_re, ab_im, num_re, num_im, den, f_re, f_im


def _ssm_prep(name, lrl, li, ldt, b_re, b_im):
    gn, h = b_re.shape

    def body(lrl_ref, li_ref, ldt_ref, br_ref, bi_ref, lrdt_ref, ang_ref, bbr_ref, bbi_ref):
        lr, dt, _, ang, _, _, _, _, _, f_re, f_im = _ssm_disc(lrl_ref[...], li_ref[...], ldt_ref[...])
        lrdt_ref[...] = lr * dt
        ang_ref[...] = ang
        br, bi = br_ref[...], bi_ref[...]
        bbr_ref[...] = f_re * br - f_im * bi
        bbi_ref[...] = f_re * bi + f_im * br

    col = jax.ShapeDtypeStruct((gn, 1), F32)
    mat = jax.ShapeDtypeStruct((gn, h), F32)
    return _pcall(body, name=name, out_shape=[col, col, mat, mat])(lrl, li, ldt, b_re, b_im)


def _ssm_param_bwd(name, lrl, li, ldt, b_re, b_im, g_abre, g_abim, g_bbre, g_bbim):
    gn, h = b_re.shape

    def body(lrl_ref, li_ref, ldt_ref, br_ref, bi_ref, gar_ref, gai_ref, gbr_ref, gbi_ref,
             dlrl_ref, dli_ref, dldt_ref, dbr_ref, dbi_ref):
        li_v = li_ref[...]
        lr, dt, mag, ang, ab_re, ab_im, num_re, num_im, den, f_re, f_im = _ssm_disc(lrl_ref[...], li_v, ldt_ref[...])
        br, bi = br_ref[...], bi_ref[...]
        gbr, gbi = gbr_ref[...], gbi_ref[...]
        g_fre = jnp.sum(gbr * br + gbi * bi, axis=1, keepdims=True)
        g_fim = jnp.sum(gbi * br - gbr * bi, axis=1, keepdims=True)
        dbr_ref[...] = gbr * f_re + gbi * f_im
        dbi_ref[...] = gbi * f_re - gbr * f_im
        g_num_re = (g_fre * lr - g_fim * li_v) / den
        g_num_im = (g_fre * li_v + g_fim * lr) / den
        g_den = -(g_fre * f_re + g_fim * f_im) / den
        g_lr = (g_fre * num_re + g_fim * num_im) / den + g_den * 2.0 * lr
        g_li = (g_fre * num_im - g_fim * num_re) / den + g_den * 2.0 * li_v
        g_are = gar_ref[...] + g_num_re
        g_aim = gai_ref[...] + g_num_im
        g_mag = g_are * jnp.cos(ang) + g_aim * jnp.sin(ang)
        g_ang = g_aim * ab_re - g_are * ab_im
        g_lrdt = g_mag * mag
        g_lr = g_lr + g_lrdt * dt
        g_dt = g_lrdt * lr + g_ang * li_v
        g_li = g_li + g_ang * dt
        dlrl_ref[...] = g_lr * lr
        dli_ref[...] = g_li
        dldt_ref[...] = g_dt * dt

    col = jax.ShapeDtypeStruct((gn, 1), F32)
    mat = jax.ShapeDtypeStruct((gn, h), F32)
    return _pcall(body, name=name, out_shape=[col, col, col, mat, mat])(lrl, li, ldt, b_re, b_im, g_abre, g_abim, g_bbre, g_bbim)


def _pow_rows(lrdt, ang, k):
    mag = jnp.exp(k * lrdt)
    return mag * jnp.cos(k * ang), mag * jnp.sin(k * ang)


def _ssm_chunk(t):
    return 256 if t >= 2048 else 128


def _to_segments(dst, srcs, jn):
    for q, src in enumerate(srcs):
        for j in range(jn):
            dst[8 * j:8 * j + 8, 128 * q:128 * (q + 1)] = src[pl.ds(j, 8, stride=jn), :]


def _from_segments(dst, src, q, jn, dtype):
    for s in range(8):
        dst[s * jn:(s + 1) * jn, 128 * q:128 * (q + 1)] = src[q, pl.ds(s, jn, stride=8), :].astype(dtype)


def _fill_rows8(dst_re, dst_im, v_re, v_im):
    for j in range(v_re.shape[0]):
        dst_re[8 * j:8 * j + 8, :] = jnp.broadcast_to(v_re[j:j + 1, :], (8, v_re.shape[1]))
        dst_im[8 * j:8 * j + 8, :] = jnp.broadcast_to(v_im[j:j + 1, :], (8, v_im.shape[1]))


def _cmul(ar, ai, br, bi):
    return ar * br - ai * bi, ar * bi + ai * br


def _cmul_conj(ar, ai, br, bi):
    return ar * br + ai * bi, ar * bi - ai * br


def _ssm_fwd(name, uin, lrdt, ang, bb_re, bb_im, cc_re, cc_im, d_skip, sw):
    t = uin.shape[0]
    gn = lrdt.shape[1]
    lc = _ssm_chunk(t)
    jn = lc // 8
    ub, sb = sw // SSM_BLOCKS, gn // SSM_BLOCKS
    nq = sw // 128
    assert ub == 128 and nq == SSM_BLOCKS

    def body(u_ref, lrdt_ref, ang_ref, bbr_ref, bbi_ref, ccr_ref, cci_ref, d_ref, y_ref, ge_ref, sre_ref, sim_ref,
             p_re, p_im, a_re, a_im, up, yp, cst_re, cst_im, car_re, car_im):
        i = pl.program_id(0)
        lrdt_v, ang_v = lrdt_ref[...], ang_ref[...]

        @pl.when(i == 0)
        def _():
            k = (lax.broadcasted_iota(jnp.int32, (jn, 1), 0) + 1).astype(F32)
            _fill_rows8(p_re, p_im, *_pow_rows(lrdt_v, ang_v, k))
            car_re[...] = jnp.zeros_like(car_re)
            car_im[...] = jnp.zeros_like(car_im)

        for q in range(nq):
            yp[q] = u_ref[:, 128 * q:128 * (q + 1)].astype(F32)
        _to_segments(up, [yp.at[q] for q in range(nq)], jn)
        u = up[...]
        ubf = u.astype(BF16)
        for q in range(SSM_BLOCKS):
            uq = ubf[:, q * ub:(q + 1) * ub]
            a_re[:, q * sb:(q + 1) * sb] = jnp.dot(uq, bbr_ref[q], preferred_element_type=F32)
            a_im[:, q * sb:(q + 1) * sb] = jnp.dot(uq, bbi_ref[q], preferred_element_type=F32)
        a1r, a1i = _pow_rows(lrdt_v, ang_v, 1.0)
        ajr, aji = _pow_rows(lrdt_v, ang_v, float(jn))
        for q in range(0, SSM_BLOCKS, 2):
            cols = slice(q * sb, (q + 2) * sb)
            ar8 = jnp.broadcast_to(a1r[:, cols], (8, 2 * sb))
            ai8 = jnp.broadcast_to(a1i[:, cols], (8, 2 * sb))

            def step(j, carry, cols=cols, ar8=ar8, ai8=ai8):
                sr, si = carry
                rows = pl.ds(pl.multiple_of(j * 8, 8), 8)
                mr, mi = _cmul(ar8, ai8, sr, si)
                nr, ni = mr + a_re[rows, cols], mi + a_im[rows, cols]
                a_re[rows, cols] = nr
                a_im[rows, cols] = ni
                return nr, ni

            lax.fori_loop(1, jn, step, (a_re[0:8, cols], a_im[0:8, cols]), unroll=8)
        er, ei = a_re[lc - 8:lc, :], a_im[lc - 8:lc, :]
        gr, gi = car_re[...], car_im[...]
        for s in range(8):
            cst_re[s:s + 1, :] = gr
            cst_im[s:s + 1, :] = gi
            mr, mi = _cmul(ajr, aji, gr, gi)
            gr, gi = mr + er[s:s + 1, :], mi + ei[s:s + 1, :]
        car_re[...] = gr
        car_im[...] = gi
        for q in range(SSM_BLOCKS):
            cols = slice(q * sb, (q + 1) * sb)
            cr = jnp.tile(cst_re[:, cols], (jn, 1))
            ci = jnp.tile(cst_im[:, cols], (jn, 1))
            mr, mi = _cmul(p_re[:, cols], p_im[:, cols], cr, ci)
            srb, sib = (a_re[:, cols] + mr).astype(BF16), (a_im[:, cols] + mi).astype(BF16)
            sre_ref[:, cols] = srb
            sim_ref[:, cols] = sib
            ycols = slice(q * ub, (q + 1) * ub)
            y = (jnp.dot(srb, ccr_ref[q], preferred_element_type=F32) - jnp.dot(sib, cci_ref[q], preferred_element_type=F32)
                 + d_ref[:, ycols] * u[:, ycols])
            yp[q] = y
            _from_segments(y_ref, yp, q, jn, F32)
            yt = y_ref[:, ycols]
            ge_ref[:, ycols] = (0.5 * yt * (1.0 + lax.erf(yt * (1.0 / math.sqrt(2.0))))).astype(BF16)

    row = lambda i: (0, 0)
    blk3 = lambda i: (0, 0, 0)
    return _pcall(
        body, name=name, grid=(t // lc,),
        in_specs=[pl.BlockSpec((lc, sw), lambda i: (i, 1)), pl.BlockSpec((1, gn), row), pl.BlockSpec((1, gn), row),
                  pl.BlockSpec(bb_re.shape, blk3), pl.BlockSpec(bb_im.shape, blk3),
                  pl.BlockSpec(cc_re.shape, blk3), pl.BlockSpec(cc_im.shape, blk3), pl.BlockSpec((1, sw), row)],
        out_specs=[pl.BlockSpec((lc, sw), lambda i: (i, 0)), pl.BlockSpec((lc, sw), lambda i: (i, 0)),
                   pl.BlockSpec((lc, gn), lambda i: (i, 0)), pl.BlockSpec((lc, gn), lambda i: (i, 0))],
        out_shape=[jax.ShapeDtypeStruct((t, sw), F32), jax.ShapeDtypeStruct((t, sw), BF16),
                   jax.ShapeDtypeStruct((t, gn), BF16), jax.ShapeDtypeStruct((t, gn), BF16)],
        scratch_shapes=[pltpu.VMEM((lc, gn), F32), pltpu.VMEM((lc, gn), F32), pltpu.VMEM((lc, gn), F32), pltpu.VMEM((lc, gn), F32),
                        pltpu.VMEM((lc, sw), F32), pltpu.VMEM((nq, lc, 128), F32),
                        pltpu.VMEM((8, gn), F32), pltpu.VMEM((8, gn), F32), pltpu.VMEM((1, gn), F32), pltpu.VMEM((1, gn), F32)],
        compiler_params=_params("arbitrary"))(uin, lrdt, ang, bb_re, bb_im, cc_re, cc_im, d_skip)


def _ssm_bwd(name, dge, y, uin, s_re, s_im, lrdt, ang, bbt_re, bbt_im, cct_re, cct_im, d_skip, sw, duin):
    t = uin.shape[0]
    gn = lrdt.shape[1]
    lc = _ssm_chunk(t)
    nb = t // lc
    jn = lc // 8
    ub, sb = sw // SSM_BLOCKS, gn // SSM_BLOCKS
    nq = sw // 128
    tail = 16

    def body(dge_ref, y_ref, u_ref, sre_ref, sim_ref, tre_ref, tim_ref, lrdt_ref, ang_ref, btr_ref, bti_ref, ctr_ref, cti_ref,
             d_ref, duin_ref, du_ref, dar_ref, dai_ref, dbr_ref, dbi_ref, dcr_ref, dci_ref, dd_ref,
             q_re, q_im, a_re, a_im, dyp, up, dys, us, dup, cst_re, cst_im, sp_re, sp_im, car_re, car_im):
        i = pl.program_id(0)
        lrdt_v, ang_v = lrdt_ref[...], ang_ref[...]

        @pl.when(i == 0)
        def _():
            k = (jn - lax.broadcasted_iota(jnp.int32, (jn, 1), 0)).astype(F32)
            _fill_rows8(q_re, q_im, *_pow_rows(lrdt_v, ang_v, k))
            car_re[...] = jnp.zeros_like(car_re)
            car_im[...] = jnp.zeros_like(car_im)
            for r in (dar_ref, dai_ref, dbr_ref, dbi_ref, dcr_ref, dci_ref, dd_ref):
                r[...] = jnp.zeros_like(r)

        yv = y_ref[...]
        ut = u_ref[...].astype(F32)
        cdf =0.5 * (1.0 + lax.erf(yv * (1.0 / math.sqrt(2.0))))
        pdf = jnp.exp(-0.5 * yv * yv) * (1.0 / math.sqrt(2.0 * math.pi))
        dyt = dge_ref[...].astype(F32) * (cdf + yv * pdf)
        dd_ref[...] += _colsum(dyt * ut)
        for q in range(nq):
            dys[q] = dyt[:, 128 * q:128 * (q + 1)]
            us[q] = ut[:, 128 * q:128 * (q + 1)]
        _to_segments(dyp, [dys.at[q] for q in range(nq)], jn)
        _to_segments(up, [us.at[q] for q in range(nq)], jn)
        dy = dyp[...]
        u = up[...]
        dyb = dy.astype(BF16)
        ubf = u.astype(BF16)
        for q in range(SSM_BLOCKS):
            dq = dyb[:, q * ub:(q + 1) * ub]
            a_re[:, q * sb:(q + 1) * sb] = jnp.dot(dq, ctr_ref[q], preferred_element_type=F32)
            a_im[:, q * sb:(q + 1) * sb] = -jnp.dot(dq, cti_ref[q], preferred_element_type=F32)
        a1r, a1i = _pow_rows(lrdt_v, ang_v, 1.0)
        ajr, aji = _pow_rows(lrdt_v, ang_v, float(jn))
        for q in range(0, SSM_BLOCKS, 2):
            cols = slice(q * sb, (q + 2) * sb)
            ar8 = jnp.broadcast_to(a1r[:, cols], (8, 2 * sb))
            ai8 = jnp.broadcast_to(a1i[:, cols], (8, 2 * sb))

            def step(jj, carry, cols=cols, ar8=ar8, ai8=ai8):
                sr, si = carry
                rows = pl.ds(pl.multiple_of((jn - 2 - jj) * 8, 8), 8)
                mr, mi = _cmul_conj(ar8, ai8, sr, si)
                nr, ni = mr + a_re[rows, cols], mi + a_im[rows, cols]
                a_re[rows, cols] = nr
                a_im[rows, cols] = ni
                return nr, ni

            lax.fori_loop(0, jn - 1, step, (a_re[lc - 8:lc, cols], a_im[lc - 8:lc, cols]), unroll=8)
        er, ei = a_re[0:8, :], a_im[0:8, :]
        hr, hi = car_re[...], car_im[...]
        for s in range(7, -1, -1):
            cst_re[s:s + 1, :] = hr
            cst_im[s:s + 1, :] = hi
            mr, mi = _cmul_conj(ajr, aji, hr, hi)
            hr, hi = mr + er[s:s + 1, :], mi + ei[s:s + 1, :]
        car_re[...] = hr
        car_im[...] = hi
        first = (i == nb - 1).astype(F32)
        sp_re[0:tail, :] = tre_ref[...].astype(F32) * (1.0 - first)
        sp_im[0:tail, :] = tim_ref[...].astype(F32) * (1.0 - first)
        sp_re[tail:tail + 8, :] = sre_ref[lc - tail:lc, :].astype(F32)[tail - 8:tail]
        sp_im[tail:tail + 8, :] = sim_ref[lc - tail:lc, :].astype(F32)[tail - 8:tail]
        tn_dims = (((0,), (0,)), ((), ()))
        for q in range(SSM_BLOCKS):
            cols = slice(q * sb, (q + 1) * sb)
            ycols = slice(q * ub, (q + 1) * ub)
            cr = jnp.tile(cst_re[:, cols], (jn, 1))
            ci = jnp.tile(cst_im[:, cols], (jn, 1))
            mr, mi = _cmul_conj(q_re[:, cols], q_im[:, cols], cr, ci)
            lam_r, lam_i = a_re[:, cols] + mr, a_im[:, cols] + mi
            s_r, s_i = sre_ref[:, cols], sim_ref[:, cols]
            p0r, p0i = sp_re[tail - 1:tail + 7, cols], sp_im[tail - 1:tail + 7, cols]
            l0r, l0i, l1r, l1i = lam_r[0:8], lam_i[0:8], lam_r[8:lc], lam_i[8:lc]
            pvr, pvi = s_r.astype(F32)[0:lc - 8], s_i.astype(F32)[0:lc - 8]
            dar_ref[:, cols] += _colsum(l1r * pvr + l1i * pvi) + _colsum(l0r * p0r + l0i * p0i)
            dai_ref[:, cols] += _colsum(l1i * pvr - l1r * pvi) + _colsum(l0i * p0r - l0r * p0i)
            lrb, lib = lam_r.astype(BF16), lam_i.astype(BF16)
            dup[q] = (jnp.dot(lrb, btr_ref[q], preferred_element_type=F32)
                      + jnp.dot(lib, bti_ref[q], preferred_element_type=F32) + d_ref[:, ycols] * dy[:, ycols])
            _from_segments(du_ref, dup, q, jn, BF16)
            uq = ubf[:, ycols]
            dbr_ref[q] += lax.dot_general(uq, lrb, tn_dims, preferred_element_type=F32)
            dbi_ref[q] += lax.dot_general(uq, lib, tn_dims, preferred_element_type=F32)
            dq = dyb[:, ycols]
            dcr_ref[q] += lax.dot_general(s_r.astype(BF16), dq, tn_dims, preferred_element_type=F32)
            dci_ref[q] -= lax.dot_general(s_i.astype(BF16), dq, tn_dims, preferred_element_type=F32)

    rev = lambda i: (nb - 1 - i, 0)
    tailmap = lambda i: (jnp.maximum((nb - 1 - i) * (lc // tail) - 1, 0), 0)
    row = lambda i: (0, 0)
    blk3 = lambda i: (0, 0, 0)
    return _pcall(
        body, name=name, grid=(nb,),
        in_specs=[pl.BlockSpec((lc, sw), rev), pl.BlockSpec((lc, sw), rev), pl.BlockSpec((lc, sw), lambda i: (nb - 1 - i, 1)),
                  pl.BlockSpec((lc, gn), rev), pl.BlockSpec((lc, gn), rev),
                  pl.BlockSpec((tail, gn), tailmap), pl.BlockSpec((tail, gn), tailmap),
                  pl.BlockSpec((1, gn), row), pl.BlockSpec((1, gn), row),
                  pl.BlockSpec(bbt_re.shape, blk3), pl.BlockSpec(bbt_im.shape, blk3),
                  pl.BlockSpec(cct_re.shape, blk3), pl.BlockSpec(cct_im.shape, blk3), pl.BlockSpec((1, sw), row), ANY_SPEC],
        out_specs=[pl.BlockSpec((lc, sw), lambda i: (nb - 1 - i, 1)), pl.BlockSpec((1, gn), row), pl.BlockSpec((1, gn), row),
                   pl.BlockSpec((SSM_BLOCKS, ub, sb), blk3), pl.BlockSpec((SSM_BLOCKS, ub, sb), blk3),
                   pl.BlockSpec((SSM_BLOCKS, sb, ub), blk3), pl.BlockSpec((SSM_BLOCKS, sb, ub), blk3),
                   pl.BlockSpec((1, sw), row)],
        out_shape=[jax.ShapeDtypeStruct(duin.shape, BF16), jax.ShapeDtypeStruct((1, gn), F32), jax.ShapeDtypeStruct((1, gn), F32),
                   jax.ShapeDtypeStruct((SSM_BLOCKS, ub, sb), F32), jax.ShapeDtypeStruct((SSM_BLOCKS, ub, sb), F32),
                   jax.ShapeDtypeStruct((SSM_BLOCKS, sb, ub), F32), jax.ShapeDtypeStruct((SSM_BLOCKS, sb, ub), F32),
                   jax.ShapeDtypeStruct((1, sw), F32)],
        scratch_shapes=[pltpu.VMEM((lc, gn), F32), pltpu.VMEM((lc, gn), F32), pltpu.VMEM((lc, gn), F32), pltpu.VMEM((lc, gn), F32),
                        pltpu.VMEM((lc, sw), F32), pltpu.VMEM((lc, sw), F32),
                        pltpu.VMEM((nq, lc, 128), F32), pltpu.VMEM((nq, lc, 128), F32), pltpu.VMEM((nq, lc, 128), F32),
                        pltpu.VMEM((8, gn), F32), pltpu.VMEM((8, gn), F32),
                        pltpu.VMEM((tail + 8, gn), F32), pltpu.VMEM((tail + 8, gn), F32),
                        pltpu.VMEM((1, gn), F32), pltpu.VMEM((1, gn), F32)],
        input_output_aliases={14: 0},
        compiler_params=_params("arbitrary"))(dge, y, uin, s_re, s_im, s_re, s_im, lrdt, ang,
                                               bbt_re, bbt_im, cct_re, cct_im, d_skip, duin)


def _blockdiag_b(bb, sw):
    gpb = (sw // SSM_GROUP) // SSM_BLOCKS
    b4 = bb.reshape(SSM_BLOCKS, gpb, SSM_STATE, SSM_GROUP)
    eye = jnp.eye(gpb, dtype=bb.dtype)
    out = jnp.einsum('qgnh,gk->qghkn', b4, eye)
    return out.reshape(SSM_BLOCKS, gpb * SSM_GROUP, gpb * SSM_STATE)


def _blockdiag_c(cc, sw):
    gpb = (sw // SSM_GROUP) // SSM_BLOCKS
    c4 = cc.reshape(SSM_BLOCKS, gpb, SSM_GROUP, SSM_STATE)
    eye = jnp.eye(gpb, dtype=cc.dtype)
    out = jnp.einsum('qghn,gk->qgnkh', c4, eye)
    return out.reshape(SSM_BLOCKS, gpb * SSM_STATE, gpb * SSM_GROUP)


def _diag_of_b(dbb, sw):
    gpb = (sw // SSM_GROUP) // SSM_BLOCKS
    d5 = dbb.reshape(SSM_BLOCKS, gpb, SSM_GROUP, gpb, SSM_STATE)
    return jnp.einsum('qghgn->qgnh', d5).reshape(SSM_BLOCKS * gpb * SSM_STATE, SSM_GROUP)


def _diag_of_c(dcc, sw):
    gpb = (sw // SSM_GROUP) // SSM_BLOCKS
    d5 = dcc.reshape(SSM_BLOCKS, gpb, SSM_STATE, gpb, SSM_GROUP)
    return jnp.einsum('qgngh->qghn', d5).reshape(SSM_BLOCKS * gpb, SSM_GROUP, SSM_STATE)


def _ada_fwd(name, c_all, w_sh, b_sh):
    nb, d = c_all.shape
    ncol = w_sh.shape[1]
    tn = _pick(ncol, 768)

    def body(c_ref, w_ref, b_ref, o_ref):
        cv = c_ref[...]
        sil = cv * _sigmoid(cv)
        o_ref[...] = jnp.dot(sil, w_ref[...], preferred_element_type=F32, precision=lax.Precision.HIGHEST) + b_ref[...]

    return _pcall(body, name=name, grid=(ncol // tn,),
                  in_specs=[pl.BlockSpec((nb, d), lambda j: (0, 0)), pl.BlockSpec((d, tn), lambda j: (0, j)),
                            pl.BlockSpec((1, tn), lambda j: (0, j))],
                  out_specs=pl.BlockSpec((nb, tn), lambda j: (0, j)),
                  out_shape=jax.ShapeDtypeStruct((nb, ncol), F32), compiler_params=_params("parallel"))(c_all, w_sh, b_sh)


def _ada_bwd_adamw(name, c_all, dmod_sh, w, m, v):
    nb, d = c_all.shape
    ncol = dmod_sh.shape[1]
    tr, tn = _pick(d, 512), _pick(ncol, 768)

    def body(c_ref, g_ref, w_ref, m_ref, v_ref, go_ref, d_ref, mo_ref, vo_ref):
        cv = c_ref[...]
        sil = cv * _sigmoid(cv)
        gv = lax.dot_general(sil, g_ref[...], (((0,), (0,)), ((), ())), preferred_element_type=F32,
                             precision=lax.Precision.HIGHEST)
        mn = ADAM_B1 * m_ref[...] + (1.0 - ADAM_B1) * gv
        vn = ADAM_B2 * v_ref[...] + (1.0 - ADAM_B2) * (gv * gv)
        m_hat = mn / (1.0 - ADAM_B1 ** ADAM_STEP)
        v_hat = vn / (1.0 - ADAM_B2 ** ADAM_STEP)
        go_ref[...] = gv
        d_ref[...] = -ADAM_LR * (m_hat / (jnp.sqrt(v_hat) + ADAM_EPS) + ADAM_WD * w_ref[...])
        mo_ref[...] = mn
        vo_ref[...] = vn

    tile = pl.BlockSpec((tr, tn), lambda i, j: (i, j))
    return _pcall(body, name=name, grid=(d // tr, ncol // tn),
                  in_specs=[pl.BlockSpec((nb, tr), lambda i, j: (0, i)), pl.BlockSpec((nb, tn), lambda i, j: (0, j)), tile, tile, tile],
                  out_specs=[tile] * 4, out_shape=[jax.ShapeDtypeStruct((d, ncol), F32)] * 4,
                  compiler_params=_params("parallel", "parallel"))(c_all, dmod_sh, w, m, v)


def _place():
    return lax.axis_index("x"), lax.axis_index("y"), lax.axis_index("c")


def _allgather_small(name, blk, deps=()):
    m_per, n = blk.shape

    def body(x_ref, *rest):
        out_ref, send_sems, recv_sems, local_sem = rest[len(deps):]
        x, y, c = _place()
        me, sibling = (x, y, c), (x, y, 1 - c)
        chips = [(1 - x, y), (x, 1 - y), (1 - x, 1 - y)]

        def rows(px, py, pc):
            return out_ref.at[pl.ds((4 * px + 2 * py + pc) * m_per, m_per), :]

        def copy(k, block, to, src=None):
            return pltpu.make_async_remote_copy(
                src_ref=rows(*block) if src is None else src, dst_ref=rows(*block),
                send_sem=send_sems.at[k], recv_sem=recv_sems.at[k], device_id=to, device_id_type=MESH)

        mine = pltpu.make_async_copy(x_ref, rows(*me), local_sem)
        mine.start()
        first = [copy(0, me, sibling, src=x_ref)]
        first += [copy(1 + j, me, (*chip, c), src=x_ref) for j, chip in enumerate(chips)]
        for cp in first:
            cp.start()
        passed = [copy(4 + j, (*chip, c), sibling) for j, chip in enumerate(chips)]
        for j, chip in enumerate(chips):
            copy(1 + j, (*chip, c), me).wait_recv()
            passed[j].start()
        copy(0, sibling, me).wait_recv()
        for j, chip in enumerate(chips):
            copy(4 + j, (*chip, 1 - c), me).wait_recv()
        for cp in first + passed:
            cp.wait_send()
        mine.wait()

    return _pcall(body, name=name, out_shape=jax.ShapeDtypeStruct((N_DEV * m_per, n), blk.dtype),
                  in_specs=[pl.BlockSpec(memory_space=pltpu.VMEM)] + [ANY_SPEC] * len(deps),
                  out_specs=pl.BlockSpec(memory_space=pltpu.VMEM),
                  scratch_shapes=[pltpu.SemaphoreType.DMA((7,)), pltpu.SemaphoreType.DMA((7,)), pltpu.SemaphoreType.DMA],
                  compiler_params=pltpu.CompilerParams(vmem_limit_bytes=VMEM_LIMIT))(blk, *deps)


def _other_chips(x, y):
    return [(1 - x, y), (x, 1 - y), (1 - x, 1 - y)]


def _place_shards(shards, s_me):
    return [lax.dynamic_update_slice(lax.empty((N_CHIPS,) + s.shape, s.dtype), s[None], (s_me, 0, 0)) for s in shards]


def _ag_copy(src, land, send, recv, wi, j, chip, x, y, c, tc, both):
    hr = src.shape[0] // 2
    half = pl.ds(pl.multiple_of(c * hr, 16), hr)
    k = 3 * wi + j
    return pltpu.make_async_remote_copy(
        src_ref=src.at[half, :], dst_ref=land.at[2 * x + y, half, :],
        send_sem=send.at[2 * k + tc if both else k], recv_sem=recv.at[2 * k + c if both else k],
        device_id=(chip[0], chip[1], tc), device_id_type=MESH)


def _ag_targets(c, both):
    return (0, 1) if both else (c,)


def _ag_start(name, shards, lands, groups, direct, deps=()):
    nw, ng, nd = len(shards), len(groups), len(deps)

    def body(*refs):
        src, land = refs[:nw], refs[nw:2 * nw]
        sems = refs[2 * nw + nd:2 * nw + nd + 2 * ng]
        token = refs[-1]
        x, y, c = _place()
        for gi, grp in enumerate(groups):
            for wi, w in enumerate(grp):
                for j, chip in enumerate(_other_chips(x, y)):
                    for tc in _ag_targets(c, direct[gi]):
                        _ag_copy(src[w], land[w], sems[2 * gi], sems[2 * gi + 1], wi, j, chip, x, y, c, tc, direct[gi]).start()
        token[...] = jnp.zeros_like(token)

    sem_shapes = []
    for gi, grp in enumerate(groups):
        sem_shapes += [pltpu.SemaphoreType.DMA(((6 if direct[gi] else 3) * len(grp),))] * 2
    out_shape = sem_shapes + [pltpu.HBM(s.shape, s.dtype) for s in shards] + [pltpu.HBM(l.shape, l.dtype) for l in lands]
    out_shape += [jax.ShapeDtypeStruct((8, 128), F32)]
    res = _pcall(body, name=name, out_shape=out_shape, in_specs=[HBM_SPEC] * (2 * nw) + [ANY_SPEC] * nd,
                 out_specs=[SEM_SPEC] * (2 * ng) + [HBM_SPEC] * (2 * nw) + [pl.BlockSpec(memory_space=pltpu.VMEM)],
                 input_output_aliases={i: 2 * ng + i for i in range(2 * nw)},
                 compiler_params=pltpu.CompilerParams(has_side_effects=EFFECT))(
                     *[_hbm(s) for s in shards], *[_hbm(l) for l in lands], *deps)
    sems = [(res[2 * gi], res[2 * gi + 1]) for gi in range(ng)]
    return sems, list(res[2 * ng:2 * ng + nw]), list(res[2 * ng + nw:2 * ng + 2 * nw]), res[-1]


def _ag_wait(name, shards, lands, send, recv, both, after):
    n = len(shards)

    def body(*refs):
        src, land = refs[:n], refs[n:2 * n]
        send_sem, recv_sem = refs[2 * n], refs[2 * n + 1]
        x, y, c = _place()
        for wi in range(n):
            for j, chip in enumerate(_other_chips(x, y)):
                for tc in _ag_targets(c, both):
                    _ag_copy(src[wi], land[wi], send_sem, recv_sem, wi, j, chip, x, y, c, tc, both).wait_send()
                    _ag_copy(src[wi], land[wi], send_sem, recv_sem, wi, j, chip, chip[0], chip[1], tc, c, both).wait_recv()

    res = _pcall(body, name=name, out_shape=[pltpu.HBM(a.shape, a.dtype) for a in list(shards) + list(lands)],
                 in_specs=[HBM_SPEC] * (2 * n) + [SEM_SPEC, SEM_SPEC] + [ANY_SPEC] * len(after), out_specs=[HBM_SPEC] * (2 * n),
                 input_output_aliases={i: i for i in range(2 * n)},
                 compiler_params=pltpu.CompilerParams(has_side_effects=EFFECT))(*shards, *lands, send, recv, *after)
    return list(res[n:])


def _ag_forward(name, lands):
    n = len(lands)

    def body(*refs):
        out = refs[n:2 * n]
        send, recv = refs[2 * n], refs[2 * n + 1]
        x, y, c = _place()
        sib = (x, y, 1 - c)
        cps = []
        for wi in range(n):
            hr = out[wi].shape[1] // 2
            for j, (cx, cy) in enumerate(_other_chips(x, y)):
                got = out[wi].at[2 * cx + cy, pl.ds(pl.multiple_of(c * hr, 16), hr), :]
                cp = pltpu.make_async_remote_copy(src_ref=got, dst_ref=got, send_sem=send.at[3 * wi + j], recv_sem=recv.at[3 * wi + j],
                                                  device_id=sib, device_id_type=MESH)
                cp.start()
                cps.append(cp)
        for wi in range(n):
            hr = out[wi].shape[1] // 2
            for j, (cx, cy) in enumerate(_other_chips(x, y)):
                got = out[wi].at[2 * cx + cy, pl.ds(pl.multiple_of((1 - c) * hr, 16), hr), :]
                pltpu.make_async_remote_copy(src_ref=got, dst_ref=got, send_sem=send.at[3 * wi + j], recv_sem=recv.at[3 * wi + j],
                                             device_id=sib, device_id_type=MESH).wait_recv()
        for cp in cps:
            cp.wait_send()

    res = _pcall(body, name=name, out_shape=[jax.ShapeDtypeStruct(l.shape, l.dtype) for l in lands],
                 in_specs=[ANY_SPEC] * n, out_specs=[ANY_SPEC] * n, input_output_aliases={i: i for i in range(n)},
                 scratch_shapes=[pltpu.SemaphoreType.DMA((3 * n,)), pltpu.SemaphoreType.DMA((3 * n,))])(*lands)
    return list(res)


def _peers(x, y, c):
    offs = [(dx, dy, dc) for dx in (0, 1) for dy in (0, 1) for dc in (0, 1)][1:]
    return [(1 - x if dx else x, 1 - y if dy else y, 1 - c if dc else c) for dx, dy, dc in offs]


def _rs_copy(g_ref, land_ref, send, recv, wi, k, to, sender):
    hr = g_ref.shape[1] // 2
    return pltpu.make_async_remote_copy(
        src_ref=g_ref.at[2 * to[0] + to[1], pl.ds(pl.multiple_of(to[2] * hr, 16), hr), :], dst_ref=land_ref.at[sender],
        send_sem=send.at[7 * wi + k], recv_sem=recv.at[7 * wi + k], device_id=to, device_id_type=MESH)


def _rs_start(name, gs):
    n = len(gs)
    lands = [lax.empty((N_DEV, g.shape[1] // 2, g.shape[2]), BF16) for g in gs]

    def body(*refs):
        g, land = refs[:n], refs[n:2 * n]
        send, recv = refs[2 * n], refs[2 * n + 1]
        token = refs[-1]
        x, y, c = _place()
        me = 4 * x + 2 * y + c
        for wi in range(n):
            for k, to in enumerate(_peers(x, y, c)):
                _rs_copy(g[wi], land[wi], send, recv, wi, k, to, me).start()
        token[...] = jnp.zeros_like(token)

    out_shape = [pltpu.SemaphoreType.DMA((7 * n,))] * 2 + [pltpu.HBM(a.shape, a.dtype) for a in list(gs) + lands]
    out_shape += [jax.ShapeDtypeStruct((8, 128), F32)]
    res = _pcall(body, name=name, out_shape=out_shape, in_specs=[HBM_SPEC] * (2 * n),
                 out_specs=[SEM_SPEC] * 2 + [HBM_SPEC] * (2 * n) + [pl.BlockSpec(memory_space=pltpu.VMEM)],
                 input_output_aliases={i: 2 + i for i in range(2 * n)},
                 compiler_params=pltpu.CompilerParams(has_side_effects=EFFECT))(
                     *[_hbm(a) for a in gs], *[_hbm(a) for a in lands])
    return res[0], res[1], list(res[2:2 + n]), list(res[2 + n:2 + 2 * n]), res[-1]


def _rs_wait(name, gs, lands, send, recv, after):
    n = len(gs)

    def body(*refs):
        g, land = refs[:n], refs[n:2 * n]
        send_sem, recv_sem = refs[2 * n], refs[2 * n + 1]
        x, y, c = _place()
        me = 4 * x + 2 * y + c
        for wi in range(n):
            for k, to in enumerate(_peers(x, y, c)):
                _rs_copy(g[wi], land[wi], send_sem, recv_sem, wi, k, to, me).wait_send()
                _rs_copy(g[wi], land[wi], send_sem, recv_sem, wi, k, (x, y, c), 4 * to[0] + 2 * to[1] + to[2]).wait_recv()

    res = _pcall(body, name=name, out_shape=[pltpu.HBM(a.shape, a.dtype) for a in list(gs) + list(lands)],
                 in_specs=[HBM_SPEC] * (2 * n) + [SEM_SPEC, SEM_SPEC, ANY_SPEC], out_specs=[HBM_SPEC] * (2 * n),
                 input_output_aliases={i: i for i in range(2 * n)},
                 compiler_params=pltpu.CompilerParams(has_side_effects=EFFECT))(*gs, *lands, send, recv, after)
    return list(res[:n]), list(res[n:])


def _bc_copy(blk_ref, land_ref, send, recv, k, to, slot):
    return pltpu.make_async_remote_copy(src_ref=blk_ref, dst_ref=land_ref.at[slot], send_sem=send.at[k], recv_sem=recv.at[k],
                                        device_id=to, device_id_type=MESH)


def _bcast_start(name, blk):
    land = lax.empty((N_DEV,) + blk.shape, blk.dtype)

    def body(blk_ref, land_ref, send, recv, blk_thru, land_thru, token):
        x, y, c = _place()
        for k, to in enumerate(_peers(x, y, c)):
            _bc_copy(blk_ref, land_ref, send, recv, k, to, 4 * x + 2 * y + c).start()
        token[...] = jnp.zeros_like(token)

    return _pcall(body, name=name,
                  out_shape=[pltpu.SemaphoreType.DMA((7,)), pltpu.SemaphoreType.DMA((7,)), pltpu.HBM(blk.shape, blk.dtype),
                             pltpu.HBM(land.shape, land.dtype), jax.ShapeDtypeStruct((8, 128), F32)],
                  in_specs=[HBM_SPEC, HBM_SPEC], out_specs=[SEM_SPEC, SEM_SPEC, HBM_SPEC, HBM_SPEC, pl.BlockSpec(memory_space=pltpu.VMEM)],
                  input_output_aliases={0: 2, 1: 3}, compiler_params=pltpu.CompilerParams(has_side_effects=EFFECT))(_hbm(blk), _hbm(land))


def _bcast_wait(name, blk, land, send, recv, after):
    def body(blk_ref, land_ref, send_sem, recv_sem, after_ref, blk_thru, land_thru):
        x, y, c = _place()
        for k, to in enumerate(_peers(x, y, c)):
            _bc_copy(blk_ref, land_ref, send_sem, recv_sem, k, to, 4 * x + 2 * y + c).wait_send()
            _bc_copy(blk_ref, land_ref, send_sem, recv_sem, k, to, 4 * to[0] + 2 * to[1] + to[2]).wait_recv()

    return _pcall(body, name=name, out_shape=[pltpu.HBM(blk.shape, blk.dtype), pltpu.HBM(land.shape, land.dtype)],
                  in_specs=[HBM_SPEC, HBM_SPEC, SEM_SPEC, SEM_SPEC, ANY_SPEC], out_specs=[HBM_SPEC, HBM_SPEC],
                  input_output_aliases={0: 0, 1: 1}, compiler_params=pltpu.CompilerParams(has_side_effects=EFFECT))(
                      blk, land, send, recv, after)[1]


def _rs_sum(name, gs, lands):
    n = len(gs)

    def body(*refs):
        g_refs, land_refs, out_refs = refs[:n], refs[n:2 * n], refs[2 * n:3 * n]
        recvs = refs[3 * n:4 * n]
        local_sems, sib_send, sib_recv = refs[4 * n:]
        x, y, c = _place()
        me = 4 * x + 2 * y + c
        cps = []
        for wi in range(n):
            hr = g_refs[wi].shape[1] // 2
            own = g_refs[wi].at[2 * x + y, pl.ds(pl.multiple_of(c * hr, 16), hr), :]
            cps.append(pltpu.make_async_copy(own, recvs[wi].at[me], local_sems.at[8 * wi + 7]))
            for k, (tx, ty, tc) in enumerate(_peers(x, y, c)):
                slot = 4 * tx + 2 * ty + tc
                cps.append(pltpu.make_async_copy(land_refs[wi].at[slot], recvs[wi].at[slot], local_sems.at[8 * wi + k]))
        for cp in cps:
            cp.start()
        sibs = []
        for wi in range(n):
            hr = g_refs[wi].shape[1] // 2
            ch = _pick(hr, 64, 16)
            for cp in cps[8 * wi:8 * wi + 8]:
                cp.wait()
            base = pl.multiple_of(c * hr, 16)
            for r0 in range(0, hr, ch):
                acc = recvs[wi][0, r0:r0 + ch, :].astype(F32)
                for k in range(1, N_DEV):
                    acc = acc + recvs[wi][k, r0:r0 + ch, :].astype(F32)
                out_refs[wi][pl.ds(base + r0, ch), :] = acc
            half = out_refs[wi].at[pl.ds(base, hr), :]
            sib = pltpu.make_async_remote_copy(src_ref=half, dst_ref=half, send_sem=sib_send.at[wi], recv_sem=sib_recv.at[wi],
                                               device_id=(x, y, 1 - c), device_id_type=MESH)
            sib.start()
            sibs.append(sib)
        for wi in range(n):
            hr = g_refs[wi].shape[1] // 2
            other = out_refs[wi].at[pl.ds(pl.multiple_of((1 - c) * hr, 16), hr), :]
            pltpu.make_async_remote_copy(src_ref=other, dst_ref=other, send_sem=sib_send.at[wi], recv_sem=sib_recv.at[wi],
                                         device_id=(x, y, 1 - c), device_id_type=MESH).wait_recv()
            sibs[wi].wait_send()

    res = _pcall(
        body, name=name, out_shape=[jax.ShapeDtypeStruct(g.shape[1:], F32) for g in gs],
        in_specs=[ANY_SPEC] * (2 * n), out_specs=[pl.BlockSpec(memory_space=pltpu.VMEM)] * n,
        scratch_shapes=[pltpu.VMEM((N_DEV, g.shape[1] // 2, g.shape[2]), BF16) for g in gs]
        + [pltpu.SemaphoreType.DMA((8 * n,)), pltpu.SemaphoreType.DMA((n,)), pltpu.SemaphoreType.DMA((n,))],
        compiler_params=pltpu.CompilerParams(vmem_limit_bytes=VMEM_LIMIT))(*gs, *lands)
    return list(res)


def _gate_norm_epilogue(factor, x_in, gt, nxt, t, d):
    blk = (ROWS, d)
    full = lambda i, j: (i, j)
    rowv = lambda i, j: (0, j)

    def epi(acc, res, scale, *norm):
        x_new = res + (factor * scale) * acc
        if not norm:
            return x_new, acc
        gv, scv, shv = norm
        r = lax.rsqrt(jnp.mean(x_new * x_new, axis=-1, keepdims=True) + EPS)
        return x_new, acc, x_new * r * gv * (1.0 + scv) + shv

    ins = [(x_in, blk, full), (gt, (1, d), rowv)] + [(v, (1, d), rowv) for v in (nxt or ())]
    outs = [((t, d), F32, blk, full), ((t, d), BF16, blk, full)] + ([((t, d), BF16, blk, full)] if nxt else [])
    return epi, ins, outs


def _mixer_tail(name, ge, w_glu, b_glu, w_up, y_pool, uin, w_out, x_in, gt, nxt, d, pw, tm=ROWS):
    t, sw = ge.shape
    tm = min(tm, t)
    ns = w_glu.shape[2]
    cb = (2 * pw) // d

    def body(ge_ref, wg_ref, b_ref, wu_ref, yp_ref, glp_ref, gls_ref, wo_ref, x_ref, gt_ref, gn_ref, sc_ref, sh_ref,
             gv_ref, sg_ref, ys_ref, mg_ref, y_ref, h_ref, xn_ref):
        gev = ge_ref[...]
        gv = jnp.concatenate([jnp.dot(gev, wg_ref[s], preferred_element_type=F32) for s in range(N_CHIPS)], axis=1)
        gv_ref[...] = gv.astype(BF16)
        g = gv + b_ref[...]
        sg = (g[:, :sw] * _sigmoid(g[:, sw:])).astype(BF16)
        sg_ref[...] = sg
        ys = jnp.concatenate([jnp.dot(sg, wu_ref[s], preferred_element_type=F32) for s in range(N_CHIPS)], axis=1)
        ys_ref[...] = ys.astype(BF16)
        mg = (_sigmoid(glp_ref[...].astype(F32)) * yp_ref[...].astype(F32) + _sigmoid(gls_ref[...].astype(F32)) * ys).astype(BF16)
        mg_ref[...] = mg
        y = jnp.dot(mg, wo_ref[...], preferred_element_type=F32)
        y_ref[...] = y.astype(BF16)
        xn = x_ref[...] + gt_ref[...] * y
        xn_ref[...] = xn
        r = lax.rsqrt(jnp.mean(xn * xn, axis=-1, keepdims=True) + EPS)
        h_ref[...] = (xn * r * gn_ref[...] * (1.0 + sc_ref[...]) + sh_ref[...]).astype(BF16)

    full3 = lambda i: (0, 0, 0)
    full2 = lambda i: (0, 0)
    rows = lambda w: pl.BlockSpec((tm, w), lambda i: (i, 0))
    rowv = pl.BlockSpec((1, d), full2)
    return _pcall(
        body, name=name, grid=(t // tm,),
        in_specs=[rows(sw), pl.BlockSpec(w_glu.shape, full3), pl.BlockSpec(b_glu.shape, full2),
                  pl.BlockSpec(w_up.shape, full3), rows(d), pl.BlockSpec((tm, d), lambda i: (i, cb)),
                  pl.BlockSpec((tm, d), lambda i: (i, cb + 1)), pl.BlockSpec(w_out.shape, full2), rows(d), rowv, rowv, rowv, rowv],
        out_specs=[rows(N_CHIPS * ns), rows(sw), rows(d), rows(d), rows(d), rows(d), rows(d)],
        out_shape=[jax.ShapeDtypeStruct((t, N_CHIPS * ns), BF16), jax.ShapeDtypeStruct((t, sw), BF16)]
        + [jax.ShapeDtypeStruct((t, d), BF16)] * 4 + [jax.ShapeDtypeStruct((t, d), F32)],
        compiler_params=_params("parallel"))(ge, w_glu, b_glu, w_up, y_pool, uin, uin, w_out, x_in, gt, *nxt)


def _glu_up_bwd(name, dy, w_up, gv, b_glu, w_glu, sg, ge, tm=ROWS, deps=()):
    t, sw = ge.shape
    tm = min(tm, t)
    nb = t // tm
    ns = w_up.shape[2]
    nt = (((1,), (1,)), ((), ()))
    tn_ = (((0,), (0,)), ((), ()))

    def body(dy_ref, wu_ref, gv_ref, b_ref, wg_ref, sg_ref, ge_ref, *rest):
        dge_ref, db_ref, dwu_ref, dwg_ref, dwu_acc, dwg_acc = rest[len(deps):]
        i = pl.program_id(0)

        @pl.when(i == 0)
        def _():
            db_ref[...] = jnp.zeros_like(db_ref)
            dwu_acc[...] = jnp.zeros_like(dwu_acc)
            dwg_acc[...] = jnp.zeros_like(dwg_acc)

        dyv = dy_ref[...]
        dsg = None
        for s in range(N_CHIPS):
            q = lax.dot_general(dyv[:, s * ns:(s + 1) * ns], wu_ref[s], nt, preferred_element_type=F32)
            dsg = q if dsg is None else dsg + q
        g = gv_ref[...].astype(F32) + b_ref[...]
        val, sgm = g[:, :sw], _sigmoid(g[:, sw:])
        dval, dgate = dsg * sgm, dsg * val * sgm * (1.0 - sgm)
        db_ref[...] += jnp.concatenate([_colsum(dval), _colsum(dgate)], axis=1)
        dgv = jnp.concatenate([dval, dgate], axis=1).astype(BF16)
        dge = None
        for s in range(N_CHIPS):
            q = lax.dot_general(dgv[:, s * ns:(s + 1) * ns], wg_ref[s], nt, preferred_element_type=F32)
            dge = q if dge is None else dge + q
        dge_ref[...] = dge.astype(BF16)
        dwu_acc[...] += lax.dot_general(sg_ref[...], dyv, tn_, preferred_element_type=F32)
        dwg_acc[...] += lax.dot_general(ge_ref[...], dgv, tn_, preferred_element_type=F32)

        @pl.when(i == nb - 1)
        def _():
            for s in range(N_CHIPS):
                dwu_ref[s] = dwu_acc[:, s * ns:(s + 1) * ns].astype(BF16)
                dwg_ref[s] = dwg_acc[:, s * ns:(s + 1) * ns].astype(BF16)

    full3 = lambda i: (0, 0, 0)
    rows = lambda w: pl.BlockSpec((tm, w), lambda i: (i, 0))
    wide = N_CHIPS * ns
    return _pcall(
        body, name=name, grid=(nb,),
        in_specs=[rows(wide), pl.BlockSpec(w_up.shape, full3), rows(wide), pl.BlockSpec(b_glu.shape, lambda i: (0, 0)),
                  pl.BlockSpec(w_glu.shape, full3), rows(sw), rows(sw)] + [ANY_SPEC] * len(deps),
        out_specs=[rows(sw), pl.BlockSpec((1, wide), lambda i: (0, 0)), pl.BlockSpec(w_up.shape, full3),
                   pl.BlockSpec(w_glu.shape, full3)],
        out_shape=[jax.ShapeDtypeStruct((t, sw), BF16), jax.ShapeDtypeStruct((1, wide), F32),
                   jax.ShapeDtypeStruct(w_up.shape, BF16), jax.ShapeDtypeStruct(w_glu.shape, BF16)],
        scratch_shapes=[pltpu.VMEM((sw, wide), F32), pltpu.VMEM((sw, wide), F32)],
        compiler_params=_params("arbitrary"))(dy, w_up, gv, b_glu, w_glu, sg, ge, *deps)


_FULL = lambda i, j: (i, j)
_ROWV = lambda i, j: (0, j)


def _gates_bwd_epilogue(y_pool, y_ssm, uin, t, d, pw):
    cb = (2 * pw) // d

    def epi(dm, yp, ys, glp, gls):
        sp, ss = _sigmoid(glp.astype(F32)), _sigmoid(gls.astype(F32))
        dgl = jnp.concatenate([dm * yp.astype(F32) * sp * (1.0 - sp), dm * ys.astype(F32) * ss * (1.0 - ss)], axis=1)
        return dm * sp, dm * ss, ("at", 2 * pw, dgl)

    ins = [(y_pool, (ROWS, d), _FULL), (y_ssm, (ROWS, d), _FULL),
           (uin, (ROWS, d), lambda i, j: (i, cb)), (uin, (ROWS, d), lambda i, j: (i, cb + 1))]
    wide = 2 * pw + 2 * d
    return epi, ins, [((t, d), BF16, (ROWS, d), _FULL)] * 2 + [((t, wide), BF16, (ROWS, wide), _FULL)]


def _loss_epilogue(x_in, gt, tgt, g_final, t, d):
    blk = (ROWS, d)
    full = lambda i, j: (i, j)
    rowv = lambda i, j: (0, j)

    def epi(acc, res, scale, tv, gv):
        xv = res + (0.5 * scale) * acc
        r = lax.rsqrt(jnp.mean(xv * xv, axis=-1, keepdims=True) + EPS)
        xr = xv * r
        e = xr * gv - tv
        loss_row = 0.5 * jnp.mean(e * e, axis=-1, keepdims=True)
        dout = e * (1.0 / d)
        gd = gv * dout
        dx = r * (gd - xr * jnp.mean(gd * xr, axis=-1, keepdims=True))
        fdx = 0.5 * dx
        return [dx, scale * fdx], [_colsum(dout * xr), _colsum(loss_row * jnp.ones((1, 128), F32)), _colsum(fdx * acc)]

    ins = [(x_in, blk, full), (gt, (1, d), rowv), (tgt, blk, full), (g_final, (1, d), rowv)]
    return epi, ins, [((t, d), F32, blk, full), ((t, d), BF16, blk, full)], [d, 128, d]


def _norm_bwd_epilogue(x, dres, g, sc, up, t, d):
    blk = (ROWS, d)
    full = lambda i, j: (i, j)
    rowv = lambda i, j: (0, j)

    def epi(dhv, xv, drv, gv, scv, *rest):
        r = lax.rsqrt(jnp.mean(xv * xv, axis=-1, keepdims=True) + EPS)
        xr = xv * r
        dn = dhv * (1.0 + scv)
        gd = gv * dn
        dx = drv + r * (gd - xr * jnp.mean(gd * xr, axis=-1, keepdims=True))
        outs, reds = [dx], [_colsum(dhv), _colsum(dhv * xr * gv), _colsum(dn * xr)]
        if up:
            yv, gtv = rest
            fdx = up[2] * dx
            outs.append(gtv * fdx)
            reds.append(_colsum(fdx * yv.astype(F32)))
        return outs, reds

    ins = [(x, blk, full), (dres, blk, full), (g, (1, d), rowv), (sc, (1, d), rowv)]
    ins += [(up[0], blk, full), (up[1], (1, d), rowv)] if up else []
    outs = [((t, d), F32, blk, full)] + ([((t, d), BF16, blk, full)] if up else [])
    return epi, ins, outs, [d] * (4 if up else 3)


def _ffn_in_act(name, h, w_in, tm=ROWS):
    t, d = h.shape
    ns = w_in.shape[2]
    tm = _pick(t, tm, 8)
    w4 = w_in.reshape(2, 2, d, ns)

    def body(h_ref, w_ref, ab_ref, act_ref):
        hv = h_ref[...]
        a = jnp.dot(hv, w_ref[0], preferred_element_type=F32)
        b = jnp.dot(hv, w_ref[1], preferred_element_type=F32)
        ab_ref[0] = a.astype(BF16)
        ab_ref[1] = b.astype(BF16)
        act_ref[...] = (a * _sigmoid(a) * b).astype(BF16)

    return _pcall(body, name=name, grid=(2, t // tm),
                  in_specs=[pl.BlockSpec((tm, d), lambda c, i: (i, 0)), pl.BlockSpec((2, None, d, ns), lambda c, i: (0, c, 0, 0))],
                  out_specs=[pl.BlockSpec((2, tm, ns), lambda c, i: (0, i, c)), pl.BlockSpec((tm, ns), lambda c, i: (i, c))],
                  out_shape=[jax.ShapeDtypeStruct((2, t, 2 * ns), BF16), jax.ShapeDtypeStruct((t, 2 * ns), BF16)],
                  compiler_params=_params("parallel", "parallel"))(h, w4)


def _dswiglu_epilogue(ab, t, f, tn):
    blk = (2, ROWS, tn)
    idx = lambda i, j: (0, i, j)

    def epi(dact, abv):
        a, b = abv[0].astype(F32), abv[1].astype(F32)
        s = _sigmoid(a)
        return (jnp.stack([dact * b * (s * (1.0 + a * (1.0 - s))), dact * (a * s)]),)

    return epi, [(ab, blk, idx)], [((2, t, f), BF16, blk, idx)]


def _ffn_fwd(tag, x, h, gt, get_w_in, get_w_out, nxt=None, loss=None):
    t, d = x.shape
    ab, act = _ffn_in_act(tag + "_in", h, get_w_in(h))
    if loss:
        epi, epi_ins, epi_outs, epi_reds = _loss_epilogue(x, gt, *loss, t, d)
        res = _mm(tag + "_out", act, get_w_out(act), tm=ROWS, tn=d, epi=epi, epi_ins=epi_ins, epi_outs=epi_outs,
                  epi_reds=epi_reds, b_resident=True)
        return res, (x, h, ab, act, None)
    epi, epi_ins, epi_outs = _gate_norm_epilogue(0.5, x, gt, nxt, t, d)
    res = _mm(tag + "_out", act, get_w_out(act), tm=ROWS, tn=d, epi=epi, epi_ins=epi_ins, epi_outs=epi_outs, b_resident=True)
    return res[0], res[2], (x, h, ab, act, res[1])


def _ffn_bwd(tag, dx_new, dy, saved, g, sc, w_in, w_out, start_rs, up, deps=()):
    x, h, ab, act, _ = saved
    d = x.shape[1]
    f = act.shape[1]
    dw_out = _mm(tag + "_dwout", act, dy, ta=True, tm=1408, tn=d, tk=2048, deps=deps)
    tok = start_rs("out", dw_out.reshape(N_CHIPS, f // N_CHIPS, d))
    epi, epi_ins, epi_outs = _dswiglu_epilogue(ab, x.shape[0], f, w_in.shape[2])
    dab = _mm(tag + "_dact", dy, w_out, tb=True, tm=ROWS, tn=w_in.shape[2], epi=epi, epi_ins=epi_ins, epi_outs=epi_outs,
              deps=(tok,), j_outer=True)
    dw_in = _mm(tag + "_dwin", h, dab, ta=True, out_stacked=True, b_halves=True, tm=d, tn=1408, tk=2048)
    tok = start_rs("in", dw_in)
    epi, epi_ins, epi_outs, epi_reds = _norm_bwd_epilogue(x, dx_new, g, sc, up, x.shape[0], d)
    res = _mm(tag + "_dh", dab, w_in, tb=True, b_stacked=True, a_halves=True, tn=d, tk=5632, deps=(tok,), epi=epi,
              epi_ins=epi_ins, epi_outs=epi_outs, epi_reds=epi_reds, b_resident=True)
    if up:
        return res
    return res[0], None, res[1], res[2], res[3], None


def _row(v):
    return v.reshape(1, -1)


def _pack(parts):
    cols = []
    for p in parts:
        flat = p.reshape(-1).astype(F32)
        padn = (-flat.shape[0]) % 128
        cols.append(jnp.pad(flat, (0, padn)) if padn else flat)
    flat = jnp.concatenate(cols)
    padn = (-flat.shape[0]) % 1024
    if padn:
        flat = jnp.pad(flat, (0, padn))
    return flat.reshape(-1, 128)


def _unpack(packed, shapes):
    flat = packed.reshape(-1)
    out, off = [], 0
    for s in shapes:
        n = math.prod(s)
        out.append(flat[off:off + n].reshape(s))
        off += n + ((-n) % 128)
    return out


SMALL = ['b_ada', 'g_ffn1', 'g_mix', 'pool_w', 'pool_b', 'pool_scale', 'ssm_lam_re_log', 'ssm_lam_im', 'ssm_log_dt',
         'ssm_b_re', 'ssm_b_im', 'ssm_c_re', 'ssm_c_im', 'ssm_d', 'b_glu', 'g_ffn2', 'g_final']
BIG = ['w_ffn1_in', 'w_ffn1_out', 'w_in', 'w_pool_up', 'w_glu', 'w_ssm_up', 'w_out', 'w_ffn2_in', 'w_ffn2_out']
AG_GROUPS = [[0], [1], [2, 3, 4, 5, 6], [7, 8]]
AG_DIRECT = [False, True, False, True]
SMALL_LATE = ['b_ada', 'g_ffn1']
RS_SUM_GROUPS = [[0, 1, 2, 3], [4, 5]]
WEIGHTS = ['w_ada', 'b_ada', 'g_ffn1', 'w_ffn1_in', 'w_ffn1_out', 'g_mix', 'w_in', 'pool_w', 'pool_b', 'pool_scale', 'w_pool_up',
           'ssm_lam_re_log', 'ssm_lam_im', 'ssm_log_dt', 'ssm_b_re', 'ssm_b_im', 'ssm_c_re', 'ssm_c_im', 'ssm_d', 'w_glu', 'b_glu',
           'w_ssm_up', 'w_out', 'g_ffn2', 'w_ffn2_in', 'w_ffn2_out', 'g_final']


def kernel(x, c, w_ada, b_ada, g_ffn1, w_ffn1_in, w_ffn1_out, g_mix, w_in, pool_w, pool_b, pool_scale, w_pool_up, ssm_lam_re_log, ssm_lam_im, ssm_log_dt, ssm_b_re, ssm_b_im, ssm_c_re, ssm_c_im, ssm_d, w_glu, b_glu, w_ssm_up, w_out, g_ffn2, w_ffn2_in, w_ffn2_out, g_final, loss_target, m_w_ada, m_b_ada, m_g_ffn1, m_w_ffn1_in, m_w_ffn1_out, m_g_mix, m_w_in, m_pool_w, m_pool_b, m_pool_scale, m_w_pool_up, m_ssm_lam_re_log, m_ssm_lam_im, m_ssm_log_dt, m_ssm_b_re, m_ssm_b_im, m_ssm_c_re, m_ssm_c_im, m_ssm_d, m_w_glu, m_b_glu, m_w_ssm_up, m_w_out, m_g_ffn2, m_w_ffn2_in, m_w_ffn2_out, m_g_final, v_w_ada, v_b_ada, v_g_ffn1, v_w_ffn1_in, v_w_ffn1_out, v_g_mix, v_w_in, v_pool_w, v_pool_b, v_pool_scale, v_w_pool_up, v_ssm_lam_re_log, v_ssm_lam_im, v_ssm_log_dt, v_ssm_b_re, v_ssm_b_im, v_ssm_c_re, v_ssm_c_im, v_ssm_d, v_w_glu, v_b_glu, v_w_ssm_up, v_w_out, v_g_ffn2, v_w_ffn2_in, v_w_ffn2_out, v_g_final):
    args = dict(locals())
    wt = {n: args[n] for n in WEIGHTS}
    mom = {n: args["m_" + n] for n in WEIGHTS}
    var = {n: args["v_" + n] for n in WEIGHTS}

    t, d = x.shape[1], x.shape[2]
    pw = pool_b.shape[1]
    sw = ssm_d.shape[1]
    ngrp = sw // SSM_GROUP
    gn = ngrp * SSM_STATE
    xi, yi, ci = _place()
    b_me = 4 * xi + 2 * yi + ci
    s_me = 2 * xi + yi
    x2d = x[0]
    tgt = loss_target[0]

    c_all = _allgather_small("ag_c", c.reshape(8, d // 8)).reshape(N_DEV, d)
    ncol = w_ada.shape[2]
    b_sh = lax.dynamic_slice(b_ada, (0, s_me * ncol), (1, ncol))
    mod_sh = _ada_fwd("ada_fwd", c_all, w_ada[0], b_sh)
    md_send, md_recv, mod_sh, md_land, tok = _bcast_start("mod_start", mod_sh)

    shards = [wt[n][0].astype(BF16) for n in BIG]
    lands = _place_shards(shards, s_me)
    n0 = len(AG_GROUPS[0])
    sems_a, shards_a, lands_a, tok = _ag_start("ag_start_first", shards[:n0], lands[:n0], AG_GROUPS[:1], AG_DIRECT[:1], deps=(tok,))
    rest = [[w - n0 for w in grp] for grp in AG_GROUPS[1:]]
    sems_b, shards_b, lands_b, tok = _ag_start("ag_start_rest", shards[n0:], lands[n0:], rest, AG_DIRECT[1:], deps=(tok,))
    ag_sems, shards_t, lands_t = sems_a + sems_b, shards_a + shards_b, lands_a + lands_b
    md_land = _bcast_wait("mod_wait", mod_sh, md_land, md_send, md_recv, tok)
    mod_all = lax.dynamic_update_slice(md_land, mod_sh[None], (b_me, 0, 0))
    full = {}

    def weights(gi, *after):
        grp = AG_GROUPS[gi]
        if BIG[grp[0]] not in full:
            ls = _ag_wait("ag_wait%d" % gi, [shards_t[w] for w in grp], [lands_t[w] for w in grp], *ag_sems[gi],
                          AG_DIRECT[gi], after)
            for w, l in zip(grp, ls if AG_DIRECT[gi] else _ag_forward("ag_fwd%d" % gi, ls)):
                full[BIG[w]] = l
        return full

    mod_me = jnp.concatenate([lax.dynamic_slice(mod_all, (2 * s, b_me, 0), (1, 1, ncol))[0] for s in range(N_CHIPS)], axis=1)
    mod = mod_me.reshape(9, d)
    sh1, sc1, gt1, sh2, sc2, gt2, sh3, sc3, gt3 = [mod[k:k + 1] for k in range(9)]

    f = w_ffn1_out.shape[1] * N_CHIPS

    col = lambda a: a.reshape(gn, 1)
    lrl_c, li_c = col(ssm_lam_re_log), col(ssm_lam_im)
    ldt_c = col(jnp.broadcast_to(ssm_log_dt.reshape(ngrp, 1), (ngrp, SSM_STATE)))
    b_re2, b_im2 = ssm_b_re.reshape(gn, SSM_GROUP), ssm_b_im.reshape(gn, SSM_GROUP)
    lrdt_c, ang_c, bb_re, bb_im = _ssm_prep("ssm_prep", lrl_c, li_c, ldt_c, b_re2, b_im2)
    lrdt, ang = lrdt_c.reshape(1, gn), ang_c.reshape(1, gn)
    tr = lambda a: jnp.swapaxes(a, 1, 2)
    bbd = [_blockdiag_b(v, sw).astype(BF16) for v in (bb_re, bb_im)]
    ccd = [_blockdiag_c(v[0], sw).astype(BF16) for v in (ssm_c_re, ssm_c_im)]
    bbd_t, ccd_t = [tr(v) for v in bbd], [tr(v) for v in ccd]
    early = [n for n in SMALL if n not in SMALL_LATE]
    packs = {}
    for tag, names, extra in (("early", early, []), ("late", SMALL_LATE, [jnp.zeros((128,), F32)])):
        packs[tag] = [_pack([src[n] for n in names] + extra) for src in (wt, mom, var)]
    shadow = [lrdt, ang, ldt_c, *bbd, *ccd, *bbd_t, *ccd_t, *packs["early"], *packs["late"]]

    h1 = _norm_fwd("ffn1_norm", x2d, g_ffn1, sc1, sh1, deps=(tok,))
    x1, h2, sav1 = _ffn_fwd("ffn1", x2d, h1, gt1, lambda after: weights(0, after, *shadow)['w_ffn1_in'],
                            lambda after: weights(1, after)['w_ffn1_out'].reshape(f, d), (g_mix, sc2, sh2))
    weights(2, h2)
    wo = full['w_out'].reshape(d, d)
    uin = _mm("mix_in", h2, full['w_in'], b_stacked=True, tm=2048, tn=768, j_outer=True)
    p_pool, z_pool, y_pool = _pool_fwd("pool_fwd", uin, pool_w[0], pool_b, pool_scale, full['w_pool_up'], pw)
    y_s, ge, s_re, s_im = _ssm_fwd("ssm_fwd", uin, lrdt, ang, *bbd, *ccd, ssm_d, sw)
    gv, sg, y_ssm, merged, y2, h3, x2 = _mixer_tail("mixer_tail", ge, full['w_glu'], b_glu, full['w_ssm_up'], y_pool, uin, wo,
                                                    x1, gt2, (g_ffn2, sc3, sh3), d, pw)

    (dx3, dy3, dg_final, loss_v, dgt3), sav3 = _ffn_fwd(
        "ffn2", x2, h3, gt3, lambda after: weights(3, after)['w_ffn2_in'],
        lambda after: weights(3, after)['w_ffn2_out'].reshape(f, d), loss=(tgt, _row(g_final)))

    gs = {}
    rs_open = []

    def start_rs(names, gbs):
        send, recv, g_thru, land_thru, token = _rs_start("rs_start_" + names[0], gbs)
        rs_open.append((names, send, recv, g_thru, land_thru))
        return token

    dx2, dy2, dsh3, dsc3, gs['g_ffn2'], dgt2 = _ffn_bwd(
        "ffn2b", dx3, dy3, sav3, g_ffn2, sc3, full['w_ffn2_in'], full['w_ffn2_out'].reshape(f, d),
        lambda key, g: start_rs(['w_ffn2_' + key], [g]), (y2, gt2, 1.0))

    epi, epi_ins, epi_outs = _gates_bwd_epilogue(y_pool, y_ssm, uin, t, d, pw)
    dy_pool, dy_ssm, duin, g_wo = _mm("mix_dmerged", dy2, wo, tb=True, tm=ROWS, tn=d, epi=epi, epi_ins=epi_ins,
                                      epi_outs=epi_outs, b_resident=True, wgrad=merged, wgrad_rows=True)
    g_wo = g_wo.reshape(N_CHIPS, d // N_CHIPS, d)

    duin, gs['pool_w'], gs['pool_b'], gs['pool_scale'], g_wpu = _pool_bwd(
        "pool_bwd", dy_pool, z_pool, p_pool, pool_w[0], pool_b, pool_scale, full['w_pool_up'], duin)

    tok = start_rs(['w_out', 'w_pool_up'], [g_wo, g_wpu])
    dge, gs['b_glu'], g_wsu, g_wglu = _glu_up_bwd("glu_up_bwd", dy_ssm, full['w_ssm_up'], gv, b_glu, full['w_glu'], sg, ge,
                                                  deps=(tok,))
    (duin, g_abre, g_abim, g_bbd_re, g_bbd_im, g_ccd_re, g_ccd_im, gs['ssm_d']) = _ssm_bwd(
        "ssm_bwd", dge, y_s, uin, s_re, s_im, lrdt, ang, *bbd_t, *ccd_t, ssm_d, sw, duin)
    gs['ssm_c_re'], gs['ssm_c_im'] = _diag_of_c(g_ccd_re, sw), _diag_of_c(g_ccd_im, sw)
    d_lrl, d_li, d_ldt, d_bre, d_bim = _ssm_param_bwd(
        "ssm_param_bwd", lrl_c, li_c, ldt_c, b_re2, b_im2, g_abre.reshape(gn, 1), g_abim.reshape(gn, 1),
        _diag_of_b(g_bbd_re, sw), _diag_of_b(g_bbd_im, sw))
    gs['ssm_lam_re_log'], gs['ssm_lam_im'] = d_lrl, d_li
    gs['ssm_log_dt'] = jnp.sum(d_ldt.reshape(ngrp, SSM_STATE), axis=1)
    gs['ssm_b_re'], gs['ssm_b_im'] = d_bre, d_bim

    g_win =_mm("mix_dwin", h2, duin, ta=True, out_stacked=True, tm=d, tn=768, tk=4096)
    tok = start_rs(['w_ssm_up', 'w_glu', 'w_in'], [g_wsu, g_wglu, g_win])
    epi, epi_ins, epi_outs, epi_reds = _norm_bwd_epilogue(x1, dx2, g_mix, sc2, (sav1[4], gt1, 0.5), t, d)
    dx1, dy1, dsh2, dsc2, gs['g_mix'], dgt1 = _mm(
        "mix_dh", duin, full['w_in'], tb=True, b_stacked=True, tn=d, tk=3072, deps=(tok,), epi=epi, epi_ins=epi_ins,
        epi_outs=epi_outs, epi_reds=epi_reds, b_resident=True)

    gs['g_final'] = dg_final
    sg_blk = _pack([gs[n] for n in early])
    sg_send, sg_recv, sg_blk, sg_land, tok = _bcast_start("sg_start", sg_blk)

    dx0, _, dsh1, dsc1, gs['g_ffn1'], _ = _ffn_bwd(
        "ffn1b", dx1, dy1, sav1, g_ffn1, sc1, full['w_ffn1_in'], full['w_ffn1_out'].reshape(f, d),
        lambda key, g: start_rs(['w_ffn1_' + key], [g]), None, deps=(tok,))

    gs['b_ada'] = jnp.concatenate([dsh1, dsc1, dgt1, dsh2, dsc2, dgt2, dsh3, dsc3, dgt3], axis=1)
    lt_blk = _pack([gs[n] for n in SMALL_LATE] + [loss_v])
    lt_send, lt_recv, lt_blk, lt_land, tok = _bcast_start("late_start", lt_blk)

    grads, delta, new_m, new_v = {}, {}, {}, {}

    def small_update(tag, names, blk, land):
        g8 = lax.dynamic_update_slice(land, blk[None], (b_me, 0, 0)).reshape(-1, 128)
        w_pack, m_pack, v_pack = packs[tag]
        shapes = [wt[n].shape for n in names]
        per_param, packed = _adamw_small("adamw_small_" + tag, w_pack, g8, m_pack, v_pack, shapes)
        for k, dst in enumerate((grads, delta, new_m, new_v)):
            rest = _unpack(packed[k], shapes)
            for i, n in enumerate(names):
                dst[n] = per_param[i][k] if per_param[i] is not None else rest[i]
        return g8

    after = tok
    for group in RS_SUM_GROUPS:
        names, g_done, land_done = [], [], []
        for k in group:
            nk, send, recv, g_thru, land_thru = rs_open[k]
            gd, ld = _rs_wait("rs_wait_" + nk[0], g_thru, land_thru, send, recv, after)
            names, g_done, land_done = names + nk, g_done + gd, land_done + ld
        for n, g_sum in zip(names, _rs_sum("rs_sum_" + names[0], g_done, land_done)):
            g_out, dl, mn, vn = _adamw("adamw_" + n, wt[n][0], g_sum, mom[n][0], var[n][0])
            grads[n], delta[n], new_m[n], new_v[n] = g_out[None], dl[None], mn[None], vn[None]
            after = dl
        if group is RS_SUM_GROUPS[-2]:
            small_update("early", early, sg_blk, _bcast_wait("sg_wait", sg_blk, sg_land, sg_send, sg_recv, after))

    lt_land = _bcast_wait("late_wait", lt_blk, lt_land, lt_send, lt_recv, after)
    late8 = small_update("late", SMALL_LATE, lt_blk, lt_land).reshape(N_DEV, -1)
    loss = jnp.sum(late8[:, 10 * d])
    dmod_sh = lax.dynamic_slice(late8, (0, s_me * ncol), (N_DEV, ncol))
    g_w_ada, dl, mn, vn = _ada_bwd_adamw("ada_bwd_adamw", c_all, dmod_sh, w_ada[0], m_w_ada[0], v_w_ada[0])
    grads['w_ada'], delta['w_ada'], new_m['w_ada'], new_v['w_ada'] = g_w_ada[None], dl[None], mn[None], vn[None]

    return (loss, dx0[None], *[grads[n] for n in WEIGHTS], *[delta[n] for n in WEIGHTS],
            *[new_m[n] for n in WEIGHTS], *[new_v[n] for n in WEIGHTS])
```

```python
import functools
import math

import jax
import jax.numpy as jnp
from jax import lax
from jax.experimental import pallas as pl
from jax.experimental.pallas import tpu as pltpu

F32 = jnp.float32
BF16 = jnp.bfloat16
MESH = pl.DeviceIdType.MESH

EPS = 1e-6
POOL_WINDOWS = (2, 4, 8, 16)
POOL_HALO = 16
SSM_GROUP = 16
SSM_STATE = 64
SSM_BLOCKS = 4
N_DEV = 8
N_CHIPS = 4
ADAM_LR = 0.001
ADAM_B1 = 0.9
ADAM_B2 = 0.999
ADAM_EPS = 1e-08
ADAM_WD = 0.01
ADAM_STEP = 10
VMEM_LIMIT = 56 * 1024 * 1024
ROWS = 512


ANY_SPEC = pl.BlockSpec(memory_space=pl.ANY)
HBM_SPEC = pl.BlockSpec(memory_space=pltpu.HBM)
SEM_SPEC = pl.BlockSpec(memory_space=pltpu.SEMAPHORE)
EFFECT = pltpu.SideEffectType.DATAFLOW_SIDE_EFFECTING


def _hbm(a):
    return pltpu.with_memory_space_constraint(a, pltpu.HBM)


def _pcall(body, **kw):
    return pl.pallas_call(body, **kw)


def _params(*sem):
    return pltpu.CompilerParams(dimension_semantics=sem, vmem_limit_bytes=VMEM_LIMIT)


def _pick(n, cap, mult=128):
    if n <= cap:
        return n
    best = None
    for d in range(mult, cap + 1, mult):
        if n % d == 0:
            best = d
    assert best is not None, (n, cap, mult)
    return best


def _sigmoid(v):
    return 1.0 / (1.0 + jnp.exp(-v))


def _rowwise(name, fn, ins, params, outs, reds, tm, deps=()):
    t = ins[0][0].shape[0]
    tm = min(tm, t)
    nb = t // tm
    ni, npar, no, nd = len(ins), len(params), len(outs), len(deps)

    def body(*refs):
        iv = [r[...] for r in refs[:ni]]
        pv = [r[...] for r in refs[ni:ni + npar]]
        o_refs = refs[ni + npar + nd:ni + npar + nd + no]
        r_refs = refs[ni + npar + nd + no:]
        ovals, rvals = fn(iv, pv)
        for o_ref, val in zip(o_refs, ovals):
            off = 0
            if isinstance(val, tuple) and val[0] == "at":
                _, off, val = val
            parts = val if isinstance(val, (list, tuple)) else [val]
            for p in parts:
                o_ref[:, off:off + p.shape[1]] = p.astype(o_ref.dtype)
                off += p.shape[1]
        if r_refs:
            @pl.when(pl.program_id(0) == 0)
            def _():
                for r in r_refs:
                    r[...] = jnp.zeros_like(r)
            for r, val in zip(r_refs, rvals):
                r[...] += val

    in_specs = [pl.BlockSpec((tm, w), functools.partial(lambda i, cb: (i, cb), cb=cb)) for (_, w, cb) in ins]
    in_specs += [pl.BlockSpec(p.shape, lambda i: (0, 0)) for p in params]
    in_specs += [ANY_SPEC] * nd
    out_shape = [jax.ShapeDtypeStruct((t, w), dt) for (w, dt) in outs]
    out_shape += [jax.ShapeDtypeStruct((1, w), F32) for w in reds]
    out_specs = [pl.BlockSpec((tm, w), lambda i: (i, 0)) for (w, _) in outs]
    out_specs += [pl.BlockSpec((1, w), lambda i: (0, 0)) for w in reds]
    res = _pcall(body, name=name, grid=(nb,), in_specs=in_specs, out_specs=out_specs, out_shape=out_shape,
                 compiler_params=_params("arbitrary"))(*[a for a, _, _ in ins], *params, *deps)
    return res


def _colsum(v):
    return jnp.sum(v, axis=0, keepdims=True)


def _mm(name, a, b, *, ta=False, tb=False, b_stacked=False, out_stacked=False, tm=ROWS, tn=1024, tk=2816,
        out_dtype=BF16, epi=None, epi_ins=(), epi_outs=None, epi_reds=(), deps=(), j_outer=False, a_halves=False,
        b_halves=False, b_resident=False, wgrad=None, wgrad_rows=False):
    if a_halves:
        _, m, kdim = a.shape
        kdim *= 2
    elif ta:
        kdim, m = a.shape
    else:
        m, kdim = a.shape
    ns = None
    if b_stacked:
        ns = b.shape[2]
        n = b.shape[1] if tb else N_CHIPS * ns
        assert kdim == (N_CHIPS * ns if tb else b.shape[1]), (name, a.shape, b.shape)
    else:
        if b_halves:
            n = 2 * b.shape[2]
            assert kdim == b.shape[1] and not tb, (name, a.shape, b.shape)
        else:
            n = b.shape[0] if tb else b.shape[1]
            assert kdim == (b.shape[1] if tb else b.shape[0]), (name, a.shape, b.shape)
        if out_stacked:
            ns = n // N_CHIPS

    def shards(want):
        return max(g for g in (1, 2, 4) if g * ns <= max(want, ns))

    tm = _pick(m, tm, 128 if ta else 8)
    gn = gk = 1
    if (b_stacked and not tb) or out_stacked:
        gn = shards(min(tn, n // 2) if b_halves else tn)
        tn = gn * ns
    else:
        tn = _pick(n, tn)
    if b_stacked and tb:
        gk = shards(tk)
        tk = gk * ns
    else:
        tk = _pick(kdim, tk, 8 if ta else 128)
    nm, nn, nk = m // tm, n // tn, kdim // tk

    def ij(f):
        return (lambda g0, g1, k: f(g1, g0, k)) if j_outer else f

    if a_halves:
        assert b_stacked and tb and gk == N_CHIPS and nk == 1, name
        a_spec = pl.BlockSpec((2, tm, kdim // 2), ij(lambda i, j, k: (0, i, 0)))
    elif ta:
        a_spec = pl.BlockSpec((tk, tm), ij(lambda i, j, k: (k, i)))
    else:
        a_spec = pl.BlockSpec((tm, tk), ij(lambda i, j, k: (i, k)))
    if b_stacked and not tb:
        b_spec = pl.BlockSpec((gn, tk, ns), ij(lambda i, j, k: (j, k, 0)))
    elif b_stacked and tb:
        b_spec = pl.BlockSpec((gk, tn, ns), ij(lambda i, j, k: (k, j, 0)))
    elif b_halves:
        bph = (n // 2) // tn
        b_spec = pl.BlockSpec((None, tk, tn), ij(lambda i, j, k: (j // bph, k, j % bph)))
    elif tb:
        b_spec = pl.BlockSpec((tn, tk), ij(lambda i, j, k: (j, k)))
    else:
        b_spec = pl.BlockSpec((tk, tn), ij(lambda i, j, k: (k, j)))
    dims = (((0 if ta else 1,), (1 if tb else 0,)), ((), ()))

    if epi_outs is None:
        if out_stacked:
            epi_outs = [((N_CHIPS, m, ns), out_dtype, (gn, tm, ns), lambda i, j: (j, i, 0))]
        else:
            epi_outs = [((m, n), out_dtype, (tm, tn), lambda i, j: (i, j))]
    ne, no, nd, nr = len(epi_ins), len(epi_outs), len(deps), len(epi_reds)
    nx = 0 if wgrad is None else 1
    assert not (nr or nx) or (nn == 1 and not j_outer), name
    assert not nx or (nk == 1 and not ta and not a_halves and kdim % N_CHIPS == 0), name

    def body(a_ref, b_ref, *rest):
        e_refs = rest[:ne]
        x_refs = rest[ne:ne + nx]
        o_refs = rest[ne + nx + nd:ne + nx + nd + no]
        r_refs = rest[ne + nx + nd + no:ne + nx + nd + no + nr]
        wg_refs = rest[ne + nx + nd + no + nr:ne + nx + nd + no + nr + nx]
        scratch = rest[ne + nx + nd + no + nr + nx:]
        av = None if a_halves else a_ref[...].astype(BF16)
        if nx:
            wg_acc = scratch[-1]
            i = pl.program_id(0)
            pw_ = lax.dot_general(x_refs[0][...].astype(BF16), av, (((0,), (0,)), ((), ())), preferred_element_type=F32)

            @pl.when(i == 0)
            def _():
                wg_acc[...] = pw_

            @pl.when(i > 0)
            def _():
                wg_acc[...] += pw_

            @pl.when(i == nm - 1)
            def _():
                if wgrad_rows:
                    wg_refs[0][...] = wg_acc[...].astype(BF16)
                else:
                    ks = kdim // N_CHIPS
                    for s in range(N_CHIPS):
                        wg_refs[0][s] = wg_acc[:, s * ks:(s + 1) * ks].astype(BF16)
        if b_stacked and not tb:
            parts = [lax.dot_general(av, b_ref[s].astype(BF16), dims, preferred_element_type=F32) for s in range(gn)]
        elif b_stacked and tb:
            p = None
            for s in range(gk):
                if a_halves:
                    a_s = a_ref[s // 2, :, (s % 2) * ns:(s % 2 + 1) * ns].astype(BF16)
                else:
                    a_s = av[:, s * ns:(s + 1) * ns]
                q = lax.dot_general(a_s, b_ref[s].astype(BF16), dims, preferred_element_type=F32)
                p = q if p is None else p + q
            parts = [p]
        else:
            parts = [lax.dot_general(av, b_ref[...].astype(BF16), dims, preferred_element_type=F32)]

        def finish(acc_parts):
            if epi is None and out_stacked:
                acc = acc_parts[0]
                for s in range(gn):
                    o_refs[0][s] = acc[:, s * ns:(s + 1) * ns].astype(out_dtype)
            elif epi is None:
                w = acc_parts[0].shape[1]
                for s, part in enumerate(acc_parts):
                    o_refs[0][:, s * w:(s + 1) * w] = part.astype(out_dtype)
            else:
                acc = acc_parts[0] if len(acc_parts) == 1 else jnp.concatenate(acc_parts, axis=1)
                vals = epi(acc, *[r[...] for r in e_refs])
                if nr:
                    vals, reds = vals

                    @pl.when(pl.program_id(0) == 0)
                    def _():
                        for r in r_refs:
                            r[...] = jnp.zeros_like(r)
                    for r, val in zip(r_refs, reds):
                        r[...] += val
                for o_ref, val in zip(o_refs, vals):
                    if isinstance(val, tuple) and val[0] == "at":
                        o_ref[:, val[1]:val[1] + val[2].shape[1]] = val[2].astype(o_ref.dtype)
                    else:
                        o_ref[...] = val.astype(o_ref.dtype)

        if nk == 1:
            finish(parts)
        else:
            acc_ref = scratch[0]
            k = pl.program_id(2)
            w = parts[0].shape[1]

            @pl.when(k == 0)
            def _():
                for s, part in enumerate(parts):
                    acc_ref[:, s * w:(s + 1) * w] = part

            @pl.when(k > 0)
            def _():
                for s, part in enumerate(parts):
                    acc_ref[:, s * w:(s + 1) * w] += part

            @pl.when(k == nk - 1)
            def _():
                finish([acc_ref[...]])

    def _ij(f):
        return ij(lambda i, j, k: f(i, j))

    if b_resident:
        assert nk == 1 and (nn == 1 or j_outer), name
        b_spec = pl.BlockSpec(b_spec.block_shape, b_spec.index_map, pipeline_mode=pl.Buffered(1))
    in_specs = [a_spec, b_spec] + [pl.BlockSpec(blk, _ij(f)) for (_, blk, f) in epi_ins]
    if nx:
        in_specs.append(pl.BlockSpec((tm, wgrad.shape[1]), lambda i, j, k: (i, 0)))
    in_specs += [ANY_SPEC] * nd
    out_specs = [pl.BlockSpec(blk, _ij(f)) for (_, _, blk, f) in epi_outs]
    out_specs += [pl.BlockSpec((1, w), lambda *_: (0, 0)) for w in epi_reds]
    out_shape = [jax.ShapeDtypeStruct(s, dt) for (s, dt, _, _) in epi_outs] + [jax.ShapeDtypeStruct((1, w), F32) for w in epi_reds]
    scratch = [pltpu.VMEM((tm, tn), F32)] if nk > 1 else []
    if nx:
        wg_shape = (wgrad.shape[1], kdim) if wgrad_rows else (N_CHIPS, wgrad.shape[1], kdim // N_CHIPS)
        out_specs.append(pl.BlockSpec(wg_shape, lambda *_: (0,) * len(wg_shape)))
        out_shape.append(jax.ShapeDtypeStruct(wg_shape, BF16))
        scratch.append(pltpu.VMEM((wgrad.shape[1], kdim), F32))
    grid = (nn, nm, nk) if j_outer else (nm, nn, nk)
    sem = ("arbitrary",) * 3 if (nr or nx) else ("parallel", "parallel", "arbitrary")
    res = _pcall(body, name=name, grid=grid, in_specs=in_specs, out_specs=out_specs, out_shape=out_shape, scratch_shapes=scratch,
                 compiler_params=_params(*sem))(a, b, *[x for x, _, _ in epi_ins], *([wgrad] * nx), *deps)
    return res[0] if len(res) == 1 else res


def _norm_fwd(name, x, g, sc, sh, deps=()):
    d = x.shape[1]

    def fn(iv, pv):
        (xv,), (gv, scv, shv) = iv, pv
        r = lax.rsqrt(jnp.mean(xv * xv, axis=-1, keepdims=True) + EPS)
        return [xv * r * gv * (1.0 + scv) + shv], []

    return _rowwise(name, fn, [(x, d, 0)], [g, sc, sh], [(d, BF16)], [], ROWS, deps=deps)[0]


def _adamw(name, w, g, m, v, tm=256):
    c = w.shape[1]

    def fn(iv, pv):
        wv, gv, mv, vv = iv
        mn = ADAM_B1 * mv + (1.0 - ADAM_B1) * gv
        vn = ADAM_B2 * vv + (1.0 - ADAM_B2) * (gv * gv)
        m_hat = mn / (1.0 - ADAM_B1 ** ADAM_STEP)
        v_hat = vn / (1.0 - ADAM_B2 ** ADAM_STEP)
        delta = -ADAM_LR * (m_hat / (jnp.sqrt(v_hat) + ADAM_EPS) + ADAM_WD * wv)
        return [gv, delta, mn, vn], []

    return _rowwise(name, fn, [(w, c, 0), (g, c, 0), (m, c, 0), (v, c, 0)], [], [(c, F32)] * 4, [], _pick(w.shape[0], tm, 8))


def _unpack_plan(shape):
    n = math.prod(shape)
    if len(shape) == 2 and shape[0] == 1 and n % 128 == 0:
        return [((slice(None), slice(128 * r, 128 * (r + 1))), slice(r, r + 1), slice(None)) for r in range(n // 128)]
    if len(shape) == 2 and shape[0] == 1 and n < 128:
        return [((slice(None), slice(None)), slice(0, 1), slice(0, n))]
    if len(shape) == 1 and n % 128 == 0:
        return [((slice(128 * r, 128 * (r + 1)),), r, slice(None)) for r in range(n // 128)]
    if len(shape) == 3 and shape[0] == 1 and shape[2] == 64:
        return [((0, slice(2 * r + h, 2 * r + h + 1), slice(None)), slice(r, r + 1), slice(64 * h, 64 * (h + 1)))
                for r in range(n // 128) for h in range(2)]
    if len(shape) == 4 and shape[0] == 1 and shape[2:] == (128, 128):
        return [((0, k), slice(128 * k, 128 * (k + 1)), slice(None)) for k in range(shape[1])]
    return None


def _adamw_small(name, w, g8, m, v, shapes):
    r = w.shape[0]
    plans, rows0, off = [], [], 0
    for s in shapes:
        plans.append(_unpack_plan(s))
        rows0.append(off // 128)
        off += math.prod(s) + (-math.prod(s)) % 128
    direct = [i for i, p in enumerate(plans) if p is not None]

    def body(w_ref, g_ref, m_ref, v_ref, *out):
        packed = out[4 * len(direct):]
        gv = g_ref[0:r, :]
        for k in range(1, N_DEV):
            gv = gv + g_ref[k * r:(k + 1) * r, :]
        mn = ADAM_B1 * m_ref[...] + (1.0 - ADAM_B1) * gv
        vn = ADAM_B2 * v_ref[...] + (1.0 - ADAM_B2) * (gv * gv)
        m_hat = mn / (1.0 - ADAM_B1 ** ADAM_STEP)
        v_hat = vn / (1.0 - ADAM_B2 ** ADAM_STEP)
        packed[0][...] = gv
        packed[1][...] = -ADAM_LR * (m_hat / (jnp.sqrt(v_hat) + ADAM_EPS) + ADAM_WD * w_ref[...])
        packed[2][...] = mn
        packed[3][...] = vn
        for di, i in enumerate(direct):
            for kind in range(4):
                o_ref, src = out[4 * di + kind], packed[kind]
                for o_idx, row, lanes in plans[i]:
                    row = (rows0[i] + row) if isinstance(row, int) else slice(rows0[i] + row.start, rows0[i] + row.stop)
                    o_ref[o_idx] = src[row, lanes]

    out_shape = [jax.ShapeDtypeStruct(shapes[i], F32) for i in direct for _ in range(4)]
    out_shape += [jax.ShapeDtypeStruct((r, 128), F32)] * 4
    res = _pcall(body, name=name, out_shape=out_shape,
                 compiler_params=pltpu.CompilerParams(vmem_limit_bytes=VMEM_LIMIT))(w, g8, m, v)
    per_param = [None] * len(shapes)
    for di, i in enumerate(direct):
        per_param[i] = res[4 * di:4 * di + 4]
    return per_param, res[4 * len(direct):]


def _pool_fwd(name, uin, pool_w, pool_b, pool_scale, w_up, pw, tm=ROWS):
    t = uin.shape[0]
    tm = min(tm, t)
    ng = len(POOL_WINDOWS)
    gw = pw // ng
    ns = w_up.shape[2]

    def body(u_ref, w_ref, b_ref, s_ref, wu_ref, p_ref, z_ref, y_ref, ext):
        i = pl.program_id(0)

        @pl.when(i == 0)
        def _():
            ext[0:POOL_HALO, :] = jnp.zeros((POOL_HALO, pw), F32)

        u = u_ref[...].astype(F32)
        ext[POOL_HALO:POOL_HALO + tm, :] = u
        pos = i * tm + lax.broadcasted_iota(jnp.int32, (tm, 1), 0)
        for k, win in enumerate(POOL_WINDOWS):
            cols = slice(k * gw, (k + 1) * gw)
            acc = u[:, cols]
            for j in range(1, win):
                acc = acc + ext[POOL_HALO - j:POOL_HALO - j + tm, cols]
            cnt = jnp.minimum(pos + 1, win).astype(F32)
            z = acc / cnt - u[:, cols]
            zp = jnp.dot(z.astype(BF16), w_ref[k].astype(BF16), preferred_element_type=F32) + b_ref[:, cols]
            p_ref[:, cols] = (zp * s_ref[:, cols]).astype(BF16)
            z_ref[:, cols] = z.astype(BF16)
        ext[0:POOL_HALO, :] = u[tm - POOL_HALO:tm, :]
        pv = p_ref[...]
        for s in range(N_CHIPS):
            y_ref[:, s * ns:(s + 1) * ns] = jnp.dot(pv, wu_ref[s], preferred_element_type=F32).astype(BF16)

    full3 = lambda i: (0, 0, 0)
    return _pcall(
        body, name=name, grid=(t // tm,),
        in_specs=[pl.BlockSpec((tm, pw), lambda i: (i, 0)), pl.BlockSpec(pool_w.shape, full3),
                  pl.BlockSpec(pool_b.shape, lambda i: (0, 0)), pl.BlockSpec(pool_scale.shape, lambda i: (0, 0)),
                  pl.BlockSpec(w_up.shape, full3)],
        out_specs=[pl.BlockSpec((tm, pw), lambda i: (i, 0))] * 2 + [pl.BlockSpec((tm, N_CHIPS * ns), lambda i: (i, 0))],
        out_shape=[jax.ShapeDtypeStruct((t, pw), BF16)] * 2 + [jax.ShapeDtypeStruct((t, N_CHIPS * ns), BF16)],
        scratch_shapes=[pltpu.VMEM((POOL_HALO + tm, pw), F32)],
        compiler_params=_params("arbitrary"))(uin, pool_w, pool_b, pool_scale, w_up)


def _pool_bwd(name, dy, z, p, pool_w, pool_b, pool_scale, w_up, duin, tm=ROWS):
    t, pw = z.shape
    tm = min(tm, t)
    nb = t // tm
    ng = len(POOL_WINDOWS)
    gw = pw // ng
    ns = w_up.shape[2]

    def body(dy_ref, z_ref, p_ref, w_ref, b_ref, s_ref, wu_ref, duin_ref, du_ref, dw_ref, db_ref, ds_ref, dwu_ref, ext, dwu_acc):
        i = pl.program_id(0)

        @pl.when(i == 0)
        def _():
            ext[tm:tm + POOL_HALO, :] = jnp.zeros((POOL_HALO, pw), F32)
            dw_ref[...] = jnp.zeros_like(dw_ref)
            db_ref[...] = jnp.zeros_like(db_ref)
            ds_ref[...] = jnp.zeros_like(ds_ref)
            dwu_acc[...] = jnp.zeros_like(dwu_acc)

        dwu_acc[...] += lax.dot_general(p_ref[...], dy_ref[...], (((0,), (0,)), ((), ())), preferred_element_type=F32)

        @pl.when(i == nb - 1)
        def _():
            for s in range(N_CHIPS):
                dwu_ref[s] = dwu_acc[:, s * ns:(s + 1) * ns].astype(BF16)

        pos = (nb - 1 - i) * tm + lax.broadcasted_iota(jnp.int32, (tm, 1), 0)
        dp = None
        for s in range(N_CHIPS):
            q = lax.dot_general(dy_ref[:, s * ns:(s + 1) * ns], wu_ref[s], (((1,), (1,)), ((), ())), preferred_element_type=F32)
            dp = q if dp is None else dp + q
        for k, win in enumerate(POOL_WINDOWS):
            cols = slice(k * gw, (k + 1) * gw)
            zk = z_ref[:, cols]
            dpk = dp[:, cols]
            wk = w_ref[k].astype(BF16)
            zp = jnp.dot(zk, wk, preferred_element_type=F32) + b_ref[:, cols]
            ds_ref[:, cols] += _colsum(dpk * zp)
            dzp = dpk * s_ref[:, cols]
            db_ref[:, cols] += _colsum(dzp)
            dzpb = dzp.astype(BF16)
            dz = lax.dot_general(dzpb, wk, (((1,), (1,)), ((), ())), preferred_element_type=F32)
            dw_ref[k] += lax.dot_general(zk, dzpb, (((0,), (0,)), ((), ())), preferred_element_type=F32)
            cnt = jnp.minimum(pos + 1, win).astype(F32)
            r = dz / cnt
            ext[0:tm, cols] = r
            acc = r - dz
            for j in range(1, win):
                acc = acc + ext[j:j + tm, cols]
            du_ref[:, cols] = acc.astype(BF16)
        ext[tm:tm + POOL_HALO, :] = ext[0:POOL_HALO, :]

    rev = lambda i: (nb - 1 - i, 0)
    full3 = lambda i: (0, 0, 0)
    return _pcall(
        body, name=name, grid=(nb,),
        in_specs=[pl.BlockSpec((tm, N_CHIPS * ns), rev), pl.BlockSpec((tm, pw), rev), pl.BlockSpec((tm, pw), rev),
                  pl.BlockSpec(pool_w.shape, full3), pl.BlockSpec(pool_b.shape, lambda i: (0, 0)),
                  pl.BlockSpec(pool_scale.shape, lambda i: (0, 0)), pl.BlockSpec(w_up.shape, full3), ANY_SPEC],
        out_specs=[pl.BlockSpec((tm, pw), rev), pl.BlockSpec(pool_w.shape, full3),
                   pl.BlockSpec((1, pw), lambda i: (0, 0)), pl.BlockSpec((1, pw), lambda i: (0, 0)), pl.BlockSpec(w_up.shape, full3)],
        out_shape=[jax.ShapeDtypeStruct(duin.shape, BF16), jax.ShapeDtypeStruct(pool_w.shape, F32),
                   jax.ShapeDtypeStruct((1, pw), F32), jax.ShapeDtypeStruct((1, pw), F32), jax.ShapeDtypeStruct(w_up.shape, BF16)],
        scratch_shapes=[pltpu.VMEM((tm + POOL_HALO, pw), F32), pltpu.VMEM((pw, N_CHIPS * ns), F32)], input_output_aliases={7: 0},
        compiler_params=_params("arbitrary"))(dy, z, p, pool_w, pool_b, pool_scale, w_up, duin)


def _ssm_disc(lrl, li, ldt):
    lr = -jnp.exp(lrl)
    dt = jnp.exp(ldt)
    mag = jnp.exp(lr * dt)
    ang = li * dt
    ab_re = mag * jnp.cos(ang)
    ab_im = mag * jnp.sin(ang)
    num_re = ab_re - 1.0
    num_im = ab_im
    den = lr * lr + li * li
    f_re = (num_re * lr + num_im * li) / den
    f_im = (num_im * lr - num_re * li) / den
    return lr, dt, mag, ang, ab_re, ab_im, num_re, num_im, den, f_re, f_im


def _ssm_prep(name, lrl, li, ldt, b_re, b_im):
    gn, h = b_re.shape

    def body(lrl_ref, li_ref, ldt_ref, br_ref, bi_ref, lrdt_ref, ang_ref, bbr_ref, bbi_ref):
        lr, dt, _, ang, _, _, _, _, _, f_re, f_im = _ssm_disc(lrl_ref[...], li_ref[...], ldt_ref[...])
        lrdt_ref[...] = lr * dt
        ang_ref[...] = ang
        br, bi = br_ref[...], bi_ref[...]
        bbr_ref[...] = f_re * br - f_im * bi
        bbi_ref[...] = f_re * bi + f_im * br

    col = jax.ShapeDtypeStruct((gn, 1), F32)
    mat = jax.ShapeDtypeStruct((gn, h), F32)
    return _pcall(body, name=name, out_shape=[col, col, mat, mat])(lrl, li, ldt, b_re, b_im)


def _ssm_param_bwd(name, lrl, li, ldt, b_re, b_im, g_abre, g_abim, g_bbre, g_bbim):
    gn, h = b_re.shape

    def body(lrl_ref, li_ref, ldt_ref, br_ref, bi_ref, gar_ref, gai_ref, gbr_ref, gbi_ref,
             dlrl_ref, dli_ref, dldt_ref, dbr_ref, dbi_ref):
        li_v = li_ref[...]
        lr, dt, mag, ang, ab_re, ab_im, num_re, num_im, den, f_re, f_im = _ssm_disc(lrl_ref[...], li_v, ldt_ref[...])
        br, bi = br_ref[...], bi_ref[...]
        gbr, gbi = gbr_ref[...], gbi_ref[...]
        g_fre = jnp.sum(gbr * br + gbi * bi, axis=1, keepdims=True)
        g_fim = jnp.sum(gbi * br - gbr * bi, axis=1, keepdims=True)
        dbr_ref[...] = gbr * f_re + gbi * f_im
        dbi_ref[...] = gbi * f_re - gbr * f_im
        g_num_re = (g_fre * lr - g_fim * li_v) / den
        g_num_im = (g_fre * li_v + g_fim * lr) / den
        g_den = -(g_fre * f_re + g_fim * f_im) / den
        g_lr = (g_fre * num_re + g_fim * num_im) / den + g_den * 2.0 * lr
        g_li = (g_fre * num_im - g_fim * num_re) / den + g_den * 2.0 * li_v
        g_are = gar_ref[...] + g_num_re
        g_aim = gai_ref[...] + g_num_im
        g_mag = g_are * jnp.cos(ang) + g_aim * jnp.sin(ang)
        g_ang = g_aim * ab_re - g_are * ab_im
        g_lrdt = g_mag * mag
        g_lr = g_lr + g_lrdt * dt
        g_dt = g_lrdt * lr + g_ang * li_v
        g_li = g_li + g_ang * dt
        dlrl_ref[...] = g_lr * lr
        dli_ref[...] = g_li
        dldt_ref[...] = g_dt * dt

    col = jax.ShapeDtypeStruct((gn, 1), F32)
    mat = jax.ShapeDtypeStruct((gn, h), F32)
    return _pcall(body, name=name, out_shape=[col, col, col, mat, mat])(lrl, li, ldt, b_re, b_im, g_abre, g_abim, g_bbre, g_bbim)


def _pow_rows(lrdt, ang, k):
    mag = jnp.exp(k * lrdt)
    return mag * jnp.cos(k * ang), mag * jnp.sin(k * ang)


def _ssm_chunk(t):
    return 256 if t >= 2048 else 128


def _to_segments(dst, srcs, jn):
    for q, src in enumerate(srcs):
        for j in range(jn):
            dst[8 * j:8 * j + 8, 128 * q:128 * (q + 1)] = src[pl.ds(j, 8, stride=jn), :]


def _from_segments(dst, src, q, jn, dtype):
    for s in range(8):
        dst[s * jn:(s + 1) * jn, 128 * q:128 * (q + 1)] = src[q, pl.ds(s, jn, stride=8), :].astype(dtype)


def _fill_rows8(dst_re, dst_im, v_re, v_im):
    for j in range(v_re.shape[0]):
        dst_re[8 * j:8 * j + 8, :] = jnp.broadcast_to(v_re[j:j + 1, :], (8, v_re.shape[1]))
        dst_im[8 * j:8 * j + 8, :] = jnp.broadcast_to(v_im[j:j + 1, :], (8, v_im.shape[1]))


def _cmul(ar, ai, br, bi):
    return ar * br - ai * bi, ar * bi + ai * br


def _cmul_conj(ar, ai, br, bi):
    return ar * br + ai * bi, ar * bi - ai * br


def _ssm_fwd(name, uin, lrdt, ang, bb_re, bb_im, cc_re, cc_im, d_skip, sw):
    t = uin.shape[0]
    gn = lrdt.shape[1]
    lc = _ssm_chunk(t)
    jn = lc // 8
    ub, sb = sw // SSM_BLOCKS, gn // SSM_BLOCKS
    nq = sw // 128
    assert ub == 128 and nq == SSM_BLOCKS

    def body(u_ref, lrdt_ref, ang_ref, bbr_ref, bbi_ref, ccr_ref, cci_ref, d_ref, y_ref, ge_ref, sre_ref, sim_ref,
             p_re, p_im, a_re, a_im, up, yp, cst_re, cst_im, car_re, car_im):
        i = pl.program_id(0)
        lrdt_v, ang_v = lrdt_ref[...], ang_ref[...]

        @pl.when(i == 0)
        def _():
            k = (lax.broadcasted_iota(jnp.int32, (jn, 1), 0) + 1).astype(F32)
            _fill_rows8(p_re, p_im, *_pow_rows(lrdt_v, ang_v, k))
            car_re[...] = jnp.zeros_like(car_re)
            car_im[...] = jnp.zeros_like(car_im)

        for q in range(nq):
            yp[q] = u_ref[:, 128 * q:128 * (q + 1)].astype(F32)
        _to_segments(up, [yp.at[q] for q in range(nq)], jn)
        u = up[...]
        ubf = u.astype(BF16)
        for q in range(SSM_BLOCKS):
            uq = ubf[:, q * ub:(q + 1) * ub]
            a_re[:, q * sb:(q + 1) * sb] = jnp.dot(uq, bbr_ref[q], preferred_element_type=F32)
            a_im[:, q * sb:(q + 1) * sb] = jnp.dot(uq, bbi_ref[q], preferred_element_type=F32)
        a1r, a1i = _pow_rows(lrdt_v, ang_v, 1.0)
        ajr, aji = _pow_rows(lrdt_v, ang_v, float(jn))
        for q in range(0, SSM_BLOCKS, 2):
            cols = slice(q * sb, (q + 2) * sb)
            ar8 = jnp.broadcast_to(a1r[:, cols], (8, 2 * sb))
            ai8 = jnp.broadcast_to(a1i[:, cols], (8, 2 * sb))

            def step(j, carry, cols=cols, ar8=ar8, ai8=ai8):
                sr, si = carry
                rows = pl.ds(pl.multiple_of(j * 8, 8), 8)
                mr, mi = _cmul(ar8, ai8, sr, si)
                nr, ni = mr + a_re[rows, cols], mi + a_im[rows, cols]
                a_re[rows, cols] = nr
                a_im[rows, cols] = ni
                return nr, ni

            lax.fori_loop(1, jn, step, (a_re[0:8, cols], a_im[0:8, cols]), unroll=8)
        er, ei = a_re[lc - 8:lc, :], a_im[lc - 8:lc, :]
        gr, gi = car_re[...], car_im[...]
        for s in range(8):
            cst_re[s:s + 1, :] = gr
            cst_im[s:s + 1, :] = gi
            mr, mi = _cmul(ajr, aji, gr, gi)
            gr, gi = mr + er[s:s + 1, :], mi + ei[s:s + 1, :]
        car_re[...] = gr
        car_im[...] = gi
        for q in range(SSM_BLOCKS):
            cols = slice(q * sb, (q + 1) * sb)
            cr = jnp.tile(cst_re[:, cols], (jn, 1))
            ci = jnp.tile(cst_im[:, cols], (jn, 1))
            mr, mi = _cmul(p_re[:, cols], p_im[:, cols], cr, ci)
            srb, sib = (a_re[:, cols] + mr).astype(BF16), (a_im[:, cols] + mi).astype(BF16)
            sre_ref[:, cols] = srb
            sim_ref[:, cols] = sib
            ycols = slice(q * ub, (q + 1) * ub)
            y = (jnp.dot(srb, ccr_ref[q], preferred_element_type=F32) - jnp.dot(sib, cci_ref[q], preferred_element_type=F32)
                 + d_ref[:, ycols] * u[:, ycols])
            yp[q] = y
            _from_segments(y_ref, yp, q, jn, F32)
            yt = y_ref[:, ycols]
            ge_ref[:, ycols] = (0.5 * yt * (1.0 + lax.erf(yt * (1.0 / math.sqrt(2.0))))).astype(BF16)

    row = lambda i: (0, 0)
    blk3 = lambda i: (0, 0, 0)
    return _pcall(
        body, name=name, grid=(t // lc,),
        in_specs=[pl.BlockSpec((lc, sw), lambda i: (i, 1)), pl.BlockSpec((1, gn), row), pl.BlockSpec((1, gn), row),
                  pl.BlockSpec(bb_re.shape, blk3), pl.BlockSpec(bb_im.shape, blk3),
                  pl.BlockSpec(cc_re.shape, blk3), pl.BlockSpec(cc_im.shape, blk3), pl.BlockSpec((1, sw), row)],
        out_specs=[pl.BlockSpec((lc, sw), lambda i: (i, 0)), pl.BlockSpec((lc, sw), lambda i: (i, 0)),
                   pl.BlockSpec((lc, gn), lambda i: (i, 0)), pl.BlockSpec((lc, gn), lambda i: (i, 0))],
        out_shape=[jax.ShapeDtypeStruct((t, sw), F32), jax.ShapeDtypeStruct((t, sw), BF16),
                   jax.ShapeDtypeStruct((t, gn), BF16), jax.ShapeDtypeStruct((t, gn), BF16)],
        scratch_shapes=[pltpu.VMEM((lc, gn), F32), pltpu.VMEM((lc, gn), F32), pltpu.VMEM((lc, gn), F32), pltpu.VMEM((lc, gn), F32),
                        pltpu.VMEM((lc, sw), F32), pltpu.VMEM((nq, lc, 128), F32),
                        pltpu.VMEM((8, gn), F32), pltpu.VMEM((8, gn), F32), pltpu.VMEM((1, gn), F32), pltpu.VMEM((1, gn), F32)],
        compiler_params=_params("arbitrary"))(uin, lrdt, ang, bb_re, bb_im, cc_re, cc_im, d_skip)


def _ssm_bwd(name, dge, y, uin, s_re, s_im, lrdt, ang, bbt_re, bbt_im, cct_re, cct_im, d_skip, sw, duin):
    t = uin.shape[0]
    gn = lrdt.shape[1]
    lc = _ssm_chunk(t)
    nb = t // lc
    jn = lc // 8
    ub, sb = sw // SSM_BLOCKS, gn // SSM_BLOCKS
    nq = sw // 128
    tail = 16

    def body(dge_ref, y_ref, u_ref, sre_ref, sim_ref, tre_ref, tim_ref, lrdt_ref, ang_ref, btr_ref, bti_ref, ctr_ref, cti_ref,
             d_ref, duin_ref, du_ref, dar_ref, dai_ref, dbr_ref, dbi_ref, dcr_ref, dci_ref, dd_ref,
             q_re, q_im, a_re, a_im, dyp, up, dys, us, dup, cst_re, cst_im, sp_re, sp_im, car_re, car_im):
        i = pl.program_id(0)
        lrdt_v, ang_v = lrdt_ref[...], ang_ref[...]

        @pl.when(i == 0)
        def _():
            k = (jn - lax.broadcasted_iota(jnp.int32, (jn, 1), 0)).astype(F32)
            _fill_rows8(q_re, q_im, *_pow_rows(lrdt_v, ang_v, k))
            car_re[...] = jnp.zeros_like(car_re)
            car_im[...] = jnp.zeros_like(car_im)
            for r in (dar_ref, dai_ref, dbr_ref, dbi_ref, dcr_ref, dci_ref, dd_ref):
                r[...] = jnp.zeros_like(r)

        yv = y_ref[...]
        ut = u_ref[...].astype(F32)
        cdf =0.5 * (1.0 + lax.erf(yv * (1.0 / math.sqrt(2.0))))
        pdf = jnp.exp(-0.5 * yv * yv) * (1.0 / math.sqrt(2.0 * math.pi))
        dyt = dge_ref[...].astype(F32) * (cdf + yv * pdf)
        dd_ref[...] += _colsum(dyt * ut)
        for q in range(nq):
            dys[q] = dyt[:, 128 * q:128 * (q + 1)]
            us[q] = ut[:, 128 * q:128 * (q + 1)]
        _to_segments(dyp, [dys.at[q] for q in range(nq)], jn)
        _to_segments(up, [us.at[q] for q in range(nq)], jn)
        dy = dyp[...]
        u = up[...]
        dyb = dy.astype(BF16)
        ubf = u.astype(BF16)
        for q in range(SSM_BLOCKS):
            dq = dyb[:, q * ub:(q + 1) * ub]
            a_re[:, q * sb:(q + 1) * sb] = jnp.dot(dq, ctr_ref[q], preferred_element_type=F32)
            a_im[:, q * sb:(q + 1) * sb] = -jnp.dot(dq, cti_ref[q], preferred_element_type=F32)
        a1r, a1i = _pow_rows(lrdt_v, ang_v, 1.0)
        ajr, aji = _pow_rows(lrdt_v, ang_v, float(jn))
        for q in range(0, SSM_BLOCKS, 2):
            cols = slice(q * sb, (q + 2) * sb)
            ar8 = jnp.broadcast_to(a1r[:, cols], (8, 2 * sb))
            ai8 = jnp.broadcast_to(a1i[:, cols], (8, 2 * sb))

            def step(jj, carry, cols=cols, ar8=ar8, ai8=ai8):
                sr, si = carry
                rows = pl.ds(pl.multiple_of((jn - 2 - jj) * 8, 8), 8)
                mr, mi = _cmul_conj(ar8, ai8, sr, si)
                nr, ni = mr + a_re[rows, cols], mi + a_im[rows, cols]
                a_re[rows, cols] = nr
                a_im[rows, cols] = ni
                return nr, ni

            lax.fori_loop(0, jn - 1, step, (a_re[lc - 8:lc, cols], a_im[lc - 8:lc, cols]), unroll=8)
        er, ei = a_re[0:8, :], a_im[0:8, :]
        hr, hi = car_re[...], car_im[...]
        for s in range(7, -1, -1):
            cst_re[s:s + 1, :] = hr
            cst_im[s:s + 1, :] = hi
            mr, mi = _cmul_conj(ajr, aji, hr, hi)
            hr, hi = mr + er[s:s + 1, :], mi + ei[s:s + 1, :]
        car_re[...] = hr
        car_im[...] = hi
        first = (i == nb - 1).astype(F32)
        sp_re[0:tail, :] = tre_ref[...].astype(F32) * (1.0 - first)
        sp_im[0:tail, :] = tim_ref[...].astype(F32) * (1.0 - first)
        sp_re[tail:tail + 8, :] = sre_ref[lc - tail:lc, :].astype(F32)[tail - 8:tail]
        sp_im[tail:tail + 8, :] = sim_ref[lc - tail:lc, :].astype(F32)[tail - 8:tail]
        tn_dims = (((0,), (0,)), ((), ()))
        for q in range(SSM_BLOCKS):
            cols = slice(q * sb, (q + 1) * sb)
            ycols = slice(q * ub, (q + 1) * ub)
            cr = jnp.tile(cst_re[:, cols], (jn, 1))
            ci = jnp.tile(cst_im[:, cols], (jn, 1))
            mr, mi = _cmul_conj(q_re[:, cols], q_im[:, cols], cr, ci)
            lam_r, lam_i = a_re[:, cols] + mr, a_im[:, cols] + mi
            s_r, s_i = sre_ref[:, cols], sim_ref[:, cols]
            p0r, p0i = sp_re[tail - 1:tail + 7, cols], sp_im[tail - 1:tail + 7, cols]
            l0r, l0i, l1r, l1i = lam_r[0:8], lam_i[0:8], lam_r[8:lc], lam_i[8:lc]
            pvr, pvi = s_r.astype(F32)[0:lc - 8], s_i.astype(F32)[0:lc - 8]
            dar_ref[:, cols] += _colsum(l1r * pvr + l1i * pvi) + _colsum(l0r * p0r + l0i * p0i)
            dai_ref[:, cols] += _colsum(l1i * pvr - l1r * pvi) + _colsum(l0i * p0r - l0r * p0i)
            lrb, lib = lam_r.astype(BF16), lam_i.astype(BF16)
            dup[q] = (jnp.dot(lrb, btr_ref[q], preferred_element_type=F32)
                      + jnp.dot(lib, bti_ref[q], preferred_element_type=F32) + d_ref[:, ycols] * dy[:, ycols])
            _from_segments(du_ref, dup, q, jn, BF16)
            uq = ubf[:, ycols]
            dbr_ref[q] += lax.dot_general(uq, lrb, tn_dims, preferred_element_type=F32)
            dbi_ref[q] += lax.dot_general(uq, lib, tn_dims, preferred_element_type=F32)
            dq = dyb[:, ycols]
            dcr_ref[q] += lax.dot_general(s_r.astype(BF16), dq, tn_dims, preferred_element_type=F32)
            dci_ref[q] -= lax.dot_general(s_i.astype(BF16), dq, tn_dims, preferred_element_type=F32)

    rev = lambda i: (nb - 1 - i, 0)
    tailmap = lambda i: (jnp.maximum((nb - 1 - i) * (lc // tail) - 1, 0), 0)
    row = lambda i: (0, 0)
    blk3 = lambda i: (0, 0, 0)
    return _pcall(
        body, name=name, grid=(nb,),
        in_specs=[pl.BlockSpec((lc, sw), rev), pl.BlockSpec((lc, sw), rev), pl.BlockSpec((lc, sw), lambda i: (nb - 1 - i, 1)),
                  pl.BlockSpec((lc, gn), rev), pl.BlockSpec((lc, gn), rev),
                  pl.BlockSpec((tail, gn), tailmap), pl.BlockSpec((tail, gn), tailmap),
                  pl.BlockSpec((1, gn), row), pl.BlockSpec((1, gn), row),
                  pl.BlockSpec(bbt_re.shape, blk3), pl.BlockSpec(bbt_im.shape, blk3),
                  pl.BlockSpec(cct_re.shape, blk3), pl.BlockSpec(cct_im.shape, blk3), pl.BlockSpec((1, sw), row), ANY_SPEC],
        out_specs=[pl.BlockSpec((lc, sw), lambda i: (nb - 1 - i, 1)), pl.BlockSpec((1, gn), row), pl.BlockSpec((1, gn), row),
                   pl.BlockSpec((SSM_BLOCKS, ub, sb), blk3), pl.BlockSpec((SSM_BLOCKS, ub, sb), blk3),
                   pl.BlockSpec((SSM_BLOCKS, sb, ub), blk3), pl.BlockSpec((SSM_BLOCKS, sb, ub), blk3),
                   pl.BlockSpec((1, sw), row)],
        out_shape=[jax.ShapeDtypeStruct(duin.shape, BF16), jax.ShapeDtypeStruct((1, gn), F32), jax.ShapeDtypeStruct((1, gn), F32),
                   jax.ShapeDtypeStruct((SSM_BLOCKS, ub, sb), F32), jax.ShapeDtypeStruct((SSM_BLOCKS, ub, sb), F32),
                   jax.ShapeDtypeStruct((SSM_BLOCKS, sb, ub), F32), jax.ShapeDtypeStruct((SSM_BLOCKS, sb, ub), F32),
                   jax.ShapeDtypeStruct((1, sw), F32)],
        scratch_shapes=[pltpu.VMEM((lc, gn), F32), pltpu.VMEM((lc, gn), F32), pltpu.VMEM((lc, gn), F32), pltpu.VMEM((lc, gn), F32),
                        pltpu.VMEM((lc, sw), F32), pltpu.VMEM((lc, sw), F32),
                        pltpu.VMEM((nq, lc, 128), F32), pltpu.VMEM((nq, lc, 128), F32), pltpu.VMEM((nq, lc, 128), F32),
                        pltpu.VMEM((8, gn), F32), pltpu.VMEM((8, gn), F32),
                        pltpu.VMEM((tail + 8, gn), F32), pltpu.VMEM((tail + 8, gn), F32),
                        pltpu.VMEM((1, gn), F32), pltpu.VMEM((1, gn), F32)],
        input_output_aliases={14: 0},
        compiler_params=_params("arbitrary"))(dge, y, uin, s_re, s_im, s_re, s_im, lrdt, ang,
                                               bbt_re, bbt_im, cct_re, cct_im, d_skip, duin)


def _blockdiag_b(bb, sw):
    gpb = (sw // SSM_GROUP) // SSM_BLOCKS
    b4 = bb.reshape(SSM_BLOCKS, gpb, SSM_STATE, SSM_GROUP)
    eye = jnp.eye(gpb, dtype=bb.dtype)
    out = jnp.einsum('qgnh,gk->qghkn', b4, eye)
    return out.reshape(SSM_BLOCKS, gpb * SSM_GROUP, gpb * SSM_STATE)


def _blockdiag_c(cc, sw):
    gpb = (sw // SSM_GROUP) // SSM_BLOCKS
    c4 = cc.reshape(SSM_BLOCKS, gpb, SSM_GROUP, SSM_STATE)
    eye = jnp.eye(gpb, dtype=cc.dtype)
    out = jnp.einsum('qghn,gk->qgnkh', c4, eye)
    return out.reshape(SSM_BLOCKS, gpb * SSM_STATE, gpb * SSM_GROUP)


def _diag_of_b(dbb, sw):
    gpb = (sw // SSM_GROUP) // SSM_BLOCKS
    d5 = dbb.reshape(SSM_BLOCKS, gpb, SSM_GROUP, gpb, SSM_STATE)
    return jnp.einsum('qghgn->qgnh', d5).reshape(SSM_BLOCKS * gpb * SSM_STATE, SSM_GROUP)


def _diag_of_c(dcc, sw):
    gpb = (sw // SSM_GROUP) // SSM_BLOCKS
    d5 = dcc.reshape(SSM_BLOCKS, gpb, SSM_STATE, gpb, SSM_GROUP)
    return jnp.einsum('qgngh->qghn', d5).reshape(SSM_BLOCKS * gpb, SSM_GROUP, SSM_STATE)


def _ada_fwd(name, c_all, w_sh, b_sh):
    nb, d = c_all.shape
    ncol = w_sh.shape[1]
    tn = _pick(ncol, 768)

    def body(c_ref, w_ref, b_ref, o_ref):
        cv = c_ref[...]
        sil = cv * _sigmoid(cv)
        o_ref[...] = jnp.dot(sil, w_ref[...], preferred_element_type=F32, precision=lax.Precision.HIGHEST) + b_ref[...]

    return _pcall(body, name=name, grid=(ncol // tn,),
                  in_specs=[pl.BlockSpec((nb, d), lambda j: (0, 0)), pl.BlockSpec((d, tn), lambda j: (0, j)),
                            pl.BlockSpec((1, tn), lambda j: (0, j))],
                  out_specs=pl.BlockSpec((nb, tn), lambda j: (0, j)),
                  out_shape=jax.ShapeDtypeStruct((nb, ncol), F32), compiler_params=_params("parallel"))(c_all, w_sh, b_sh)


def _ada_bwd_adamw(name, c_all, dmod_sh, w, m, v):
    nb, d = c_all.shape
    ncol = dmod_sh.shape[1]
    tr, tn = _pick(d, 512), _pick(ncol, 768)

    def body(c_ref, g_ref, w_ref, m_ref, v_ref, go_ref, d_ref, mo_ref, vo_ref):
        cv = c_ref[...]
        sil = cv * _sigmoid(cv)
        gv = lax.dot_general(sil, g_ref[...], (((0,), (0,)), ((), ())), preferred_element_type=F32,
                             precision=lax.Precision.HIGHEST)
        mn = ADAM_B1 * m_ref[...] + (1.0 - ADAM_B1) * gv
        vn = ADAM_B2 * v_ref[...] + (1.0 - ADAM_B2) * (gv * gv)
        m_hat = mn / (1.0 - ADAM_B1 ** ADAM_STEP)
        v_hat = vn / (1.0 - ADAM_B2 ** ADAM_STEP)
        go_ref[...] = gv
        d_ref[...] = -ADAM_LR * (m_hat / (jnp.sqrt(v_hat) + ADAM_EPS) + ADAM_WD * w_ref[...])
        mo_ref[...] = mn
        vo_ref[...] = vn

    tile = pl.BlockSpec((tr, tn), lambda i, j: (i, j))
    return _pcall(body, name=name, grid=(d // tr, ncol // tn),
                  in_specs=[pl.BlockSpec((nb, tr), lambda i, j: (0, i)), pl.BlockSpec((nb, tn), lambda i, j: (0, j)), tile, tile, tile],
                  out_specs=[tile] * 4, out_shape=[jax.ShapeDtypeStruct((d, ncol), F32)] * 4,
                  compiler_params=_params("parallel", "parallel"))(c_all, dmod_sh, w, m, v)


def _place():
    return lax.axis_index("x"), lax.axis_index("y"), lax.axis_index("c")


def _allgather_small(name, blk, deps=()):
    m_per, n = blk.shape

    def body(x_ref, *rest):
        out_ref, send_sems, recv_sems, local_sem = rest[len(deps):]
        x, y, c = _place()
        me, sibling = (x, y, c), (x, y, 1 - c)
        chips = [(1 - x, y), (x, 1 - y), (1 - x, 1 - y)]

        def rows(px, py, pc):
            return out_ref.at[pl.ds((4 * px + 2 * py + pc) * m_per, m_per), :]

        def copy(k, block, to, src=None):
            return pltpu.make_async_remote_copy(
                src_ref=rows(*block) if src is None else src, dst_ref=rows(*block),
                send_sem=send_sems.at[k], recv_sem=recv_sems.at[k], device_id=to, device_id_type=MESH)

        mine = pltpu.make_async_copy(x_ref, rows(*me), local_sem)
        mine.start()
        first = [copy(0, me, sibling, src=x_ref)]
        first += [copy(1 + j, me, (*chip, c), src=x_ref) for j, chip in enumerate(chips)]
        for cp in first:
            cp.start()
        passed = [copy(4 + j, (*chip, c), sibling) for j, chip in enumerate(chips)]
        for j, chip in enumerate(chips):
            copy(1 + j, (*chip, c), me).wait_recv()
            passed[j].start()
        copy(0, sibling, me).wait_recv()
        for j, chip in enumerate(chips):
            copy(4 + j, (*chip, 1 - c), me).wait_recv()
        for cp in first + passed:
            cp.wait_send()
        mine.wait()

    return _pcall(body, name=name, out_shape=jax.ShapeDtypeStruct((N_DEV * m_per, n), blk.dtype),
                  in_specs=[pl.BlockSpec(memory_space=pltpu.VMEM)] + [ANY_SPEC] * len(deps),
                  out_specs=pl.BlockSpec(memory_space=pltpu.VMEM),
                  scratch_shapes=[pltpu.SemaphoreType.DMA((7,)), pltpu.SemaphoreType.DMA((7,)), pltpu.SemaphoreType.DMA],
                  compiler_params=pltpu.CompilerParams(vmem_limit_bytes=VMEM_LIMIT))(blk, *deps)


def _other_chips(x, y):
    return [(1 - x, y), (x, 1 - y), (1 - x, 1 - y)]


def _place_shards(shards, s_me):
    return [lax.dynamic_update_slice(lax.empty((N_CHIPS,) + s.shape, s.dtype), s[None], (s_me, 0, 0)) for s in shards]


def _ag_copy(src, land, send, recv, wi, j, chip, x, y, c, tc, both):
    hr = src.shape[0] // 2
    half = pl.ds(pl.multiple_of(c * hr, 16), hr)
    k = 3 * wi + j
    return pltpu.make_async_remote_copy(
        src_ref=src.at[half, :], dst_ref=land.at[2 * x + y, half, :],
        send_sem=send.at[2 * k + tc if both else k], recv_sem=recv.at[2 * k + c if both else k],
        device_id=(chip[0], chip[1], tc), device_id_type=MESH)


def _ag_targets(c, both):
    return (0, 1) if both else (c,)


def _ag_start(name, shards, lands, groups, direct, deps=()):
    nw, ng, nd = len(shards), len(groups), len(deps)

    def body(*refs):
        src, land = refs[:nw], refs[nw:2 * nw]
        sems = refs[2 * nw + nd:2 * nw + nd + 2 * ng]
        token = refs[-1]
        x, y, c = _place()
        for gi, grp in enumerate(groups):
            for wi, w in enumerate(grp):
                for j, chip in enumerate(_other_chips(x, y)):
                    for tc in _ag_targets(c, direct[gi]):
                        _ag_copy(src[w], land[w], sems[2 * gi], sems[2 * gi + 1], wi, j, chip, x, y, c, tc, direct[gi]).start()
        token[...] = jnp.zeros_like(token)

    sem_shapes = []
    for gi, grp in enumerate(groups):
        sem_shapes += [pltpu.SemaphoreType.DMA(((6 if direct[gi] else 3) * len(grp),))] * 2
    out_shape = sem_shapes + [pltpu.HBM(s.shape, s.dtype) for s in shards] + [pltpu.HBM(l.shape, l.dtype) for l in lands]
    out_shape += [jax.ShapeDtypeStruct((8, 128), F32)]
    res = _pcall(body, name=name, out_shape=out_shape, in_specs=[HBM_SPEC] * (2 * nw) + [ANY_SPEC] * nd,
                 out_specs=[SEM_SPEC] * (2 * ng) + [HBM_SPEC] * (2 * nw) + [pl.BlockSpec(memory_space=pltpu.VMEM)],
                 input_output_aliases={i: 2 * ng + i for i in range(2 * nw)},
                 compiler_params=pltpu.CompilerParams(has_side_effects=EFFECT))(
                     *[_hbm(s) for s in shards], *[_hbm(l) for l in lands], *deps)
    sems = [(res[2 * gi], res[2 * gi + 1]) for gi in range(ng)]
    return sems, list(res[2 * ng:2 * ng + nw]), list(res[2 * ng + nw:2 * ng + 2 * nw]), res[-1]


def _ag_wait(name, shards, lands, send, recv, both, after):
    n = len(shards)

    def body(*refs):
        src, land = refs[:n], refs[n:2 * n]
        send_sem, recv_sem = refs[2 * n], refs[2 * n + 1]
        x, y, c = _place()
        for wi in range(n):
            for j, chip in enumerate(_other_chips(x, y)):
                for tc in _ag_targets(c, both):
                    _ag_copy(src[wi], land[wi], send_sem, recv_sem, wi, j, chip, x, y, c, tc, both).wait_send()
                    _ag_copy(src[wi], land[wi], send_sem, recv_sem, wi, j, chip, chip[0], chip[1], tc, c, both).wait_recv()

    res = _pcall(body, name=name, out_shape=[pltpu.HBM(a.shape, a.dtype) for a in list(shards) + list(lands)],
                 in_specs=[HBM_SPEC] * (2 * n) + [SEM_SPEC, SEM_SPEC] + [ANY_SPEC] * len(after), out_specs=[HBM_SPEC] * (2 * n),
                 input_output_aliases={i: i for i in range(2 * n)},
                 compiler_params=pltpu.CompilerParams(has_side_effects=EFFECT))(*shards, *lands, send, recv, *after)
    return list(res[n:])


def _ag_forward(name, lands):
    n = len(lands)

    def body(*refs):
        out = refs[n:2 * n]
        send, recv = refs[2 * n], refs[2 * n + 1]
        x, y, c = _place()
        sib = (x, y, 1 - c)
        cps = []
        for wi in range(n):
            hr = out[wi].shape[1] // 2
            for j, (cx, cy) in enumerate(_other_chips(x, y)):
                got = out[wi].at[2 * cx + cy, pl.ds(pl.multiple_of(c * hr, 16), hr), :]
                cp = pltpu.make_async_remote_copy(src_ref=got, dst_ref=got, send_sem=send.at[3 * wi + j], recv_sem=recv.at[3 * wi + j],
                                                  device_id=sib, device_id_type=MESH)
                cp.start()
                cps.append(cp)
        for wi in range(n):
            hr = out[wi].shape[1] // 2
            for j, (cx, cy) in enumerate(_other_chips(x, y)):
                got = out[wi].at[2 * cx + cy, pl.ds(pl.multiple_of((1 - c) * hr, 16), hr), :]
                pltpu.make_async_remote_copy(src_ref=got, dst_ref=got, send_sem=send.at[3 * wi + j], recv_sem=recv.at[3 * wi + j],
                                             device_id=sib, device_id_type=MESH).wait_recv()
        for cp in cps:
            cp.wait_send()

    res = _pcall(body, name=name, out_shape=[jax.ShapeDtypeStruct(l.shape, l.dtype) for l in lands],
                 in_specs=[ANY_SPEC] * n, out_specs=[ANY_SPEC] * n, input_output_aliases={i: i for i in range(n)},
                 scratch_shapes=[pltpu.SemaphoreType.DMA((3 * n,)), pltpu.SemaphoreType.DMA((3 * n,))])(*lands)
    return list(res)


def _peers(x, y, c):
    offs = [(dx, dy, dc) for dx in (0, 1) for dy in (0, 1) for dc in (0, 1)][1:]
    return [(1 - x if dx else x, 1 - y if dy else y, 1 - c if dc else c) for dx, dy, dc in offs]


def _rs_copy(g_ref, land_ref, send, recv, wi, k, to, sender):
    hr = g_ref.shape[1] // 2
    return pltpu.make_async_remote_copy(
        src_ref=g_ref.at[2 * to[0] + to[1], pl.ds(pl.multiple_of(to[2] * hr, 16), hr), :], dst_ref=land_ref.at[sender],
        send_sem=send.at[7 * wi + k], recv_sem=recv.at[7 * wi + k], device_id=to, device_id_type=MESH)


def _rs_start(name, gs):
    n = len(gs)
    lands = [lax.empty((N_DEV, g.shape[1] // 2, g.shape[2]), BF16) for g in gs]

    def body(*refs):
        g, land = refs[:n], refs[n:2 * n]
        send, recv = refs[2 * n], refs[2 * n + 1]
        token = refs[-1]
        x, y, c = _place()
        me = 4 * x + 2 * y + c
        for wi in range(n):
            for k, to in enumerate(_peers(x, y, c)):
                _rs_copy(g[wi], land[wi], send, recv, wi, k, to, me).start()
        token[...] = jnp.zeros_like(token)

    out_shape = [pltpu.SemaphoreType.DMA((7 * n,))] * 2 + [pltpu.HBM(a.shape, a.dtype) for a in list(gs) + lands]
    out_shape += [jax.ShapeDtypeStruct((8, 128), F32)]
    res = _pcall(body, name=name, out_shape=out_shape, in_specs=[HBM_SPEC] * (2 * n),
                 out_specs=[SEM_SPEC] * 2 + [HBM_SPEC] * (2 * n) + [pl.BlockSpec(memory_space=pltpu.VMEM)],
                 input_output_aliases={i: 2 + i for i in range(2 * n)},
                 compiler_params=pltpu.CompilerParams(has_side_effects=EFFECT))(
                     *[_hbm(a) for a in gs], *[_hbm(a) for a in lands])
    return res[0], res[1], list(res[2:2 + n]), list(res[2 + n:2 + 2 * n]), res[-1]


def _rs_wait(name, gs, lands, send, recv, after):
    n = len(gs)

    def body(*refs):
        g, land = refs[:n], refs[n:2 * n]
        send_sem, recv_sem = refs[2 * n], refs[2 * n + 1]
        x, y, c = _place()
        me = 4 * x + 2 * y + c
        for wi in range(n):
            for k, to in enumerate(_peers(x, y, c)):
                _rs_copy(g[wi], land[wi], send_sem, recv_sem, wi, k, to, me).wait_send()
                _rs_copy(g[wi], land[wi], send_sem, recv_sem, wi, k, (x, y, c), 4 * to[0] + 2 * to[1] + to[2]).wait_recv()

    res = _pcall(body, name=name, out_shape=[pltpu.HBM(a.shape, a.dtype) for a in list(gs) + list(lands)],
                 in_specs=[HBM_SPEC] * (2 * n) + [SEM_SPEC, SEM_SPEC, ANY_SPEC], out_specs=[HBM_SPEC] * (2 * n),
                 input_output_aliases={i: i for i in range(2 * n)},
                 compiler_params=pltpu.CompilerParams(has_side_effects=EFFECT))(*gs, *lands, send, recv, after)
    return list(res[:n]), list(res[n:])


def _bc_copy(blk_ref, land_ref, send, recv, k, to, slot):
    return pltpu.make_async_remote_copy(src_ref=blk_ref, dst_ref=land_ref.at[slot], send_sem=send.at[k], recv_sem=recv.at[k],
                                        device_id=to, device_id_type=MESH)


def _bcast_start(name, blk):
    land = lax.empty((N_DEV,) + blk.shape, blk.dtype)

    def body(blk_ref, land_ref, send, recv, blk_thru, land_thru, token):
        x, y, c = _place()
        for k, to in enumerate(_peers(x, y, c)):
            _bc_copy(blk_ref, land_ref, send, recv, k, to, 4 * x + 2 * y + c).start()
        token[...] = jnp.zeros_like(token)

    return _pcall(body, name=name,
                  out_shape=[pltpu.SemaphoreType.DMA((7,)), pltpu.SemaphoreType.DMA((7,)), pltpu.HBM(blk.shape, blk.dtype),
                             pltpu.HBM(land.shape, land.dtype), jax.ShapeDtypeStruct((8, 128), F32)],
                  in_specs=[HBM_SPEC, HBM_SPEC], out_specs=[SEM_SPEC, SEM_SPEC, HBM_SPEC, HBM_SPEC, pl.BlockSpec(memory_space=pltpu.VMEM)],
                  input_output_aliases={0: 2, 1: 3}, compiler_params=pltpu.CompilerParams(has_side_effects=EFFECT))(_hbm(blk), _hbm(land))


def _bcast_wait(name, blk, land, send, recv, after):
    def body(blk_ref, land_ref, send_sem, recv_sem, after_ref, blk_thru, land_thru):
        x, y, c = _place()
        for k, to in enumerate(_peers(x, y, c)):
            _bc_copy(blk_ref, land_ref, send_sem, recv_sem, k, to, 4 * x + 2 * y + c).wait_send()
            _bc_copy(blk_ref, land_ref, send_sem, recv_sem, k, to, 4 * to[0] + 2 * to[1] + to[2]).wait_recv()

    return _pcall(body, name=name, out_shape=[pltpu.HBM(blk.shape, blk.dtype), pltpu.HBM(land.shape, land.dtype)],
                  in_specs=[HBM_SPEC, HBM_SPEC, SEM_SPEC, SEM_SPEC, ANY_SPEC], out_specs=[HBM_SPEC, HBM_SPEC],
                  input_output_aliases={0: 0, 1: 1}, compiler_params=pltpu.CompilerParams(has_side_effects=EFFECT))(
                      blk, land, send, recv, after)[1]


def _rs_sum(name, gs, lands):
    n = len(gs)

    def body(*refs):
        g_refs, land_refs, out_refs = refs[:n], refs[n:2 * n], refs[2 * n:3 * n]
        recvs = refs[3 * n:4 * n]
        local_sems, sib_send, sib_recv = refs[4 * n:]
        x, y, c = _place()
        me = 4 * x + 2 * y + c
        cps = []
        for wi in range(n):
            hr = g_refs[wi].shape[1] // 2
            own = g_refs[wi].at[2 * x + y, pl.ds(pl.multiple_of(c * hr, 16), hr), :]
            cps.append(pltpu.make_async_copy(own, recvs[wi].at[me], local_sems.at[8 * wi + 7]))
            for k, (tx, ty, tc) in enumerate(_peers(x, y, c)):
                slot = 4 * tx + 2 * ty + tc
                cps.append(pltpu.make_async_copy(land_refs[wi].at[slot], recvs[wi].at[slot], local_sems.at[8 * wi + k]))
        for cp in cps:
            cp.start()
        sibs = []
        for wi in range(n):
            hr = g_refs[wi].shape[1] // 2
            ch = _pick(hr, 64, 16)
            for cp in cps[8 * wi:8 * wi + 8]:
                cp.wait()
            base = pl.multiple_of(c * hr, 16)
            for r0 in range(0, hr, ch):
                acc = recvs[wi][0, r0:r0 + ch, :].astype(F32)
                for k in range(1, N_DEV):
                    acc = acc + recvs[wi][k, r0:r0 + ch, :].astype(F32)
                out_refs[wi][pl.ds(base + r0, ch), :] = acc
            half = out_refs[wi].at[pl.ds(base, hr), :]
            sib = pltpu.make_async_remote_copy(src_ref=half, dst_ref=half, send_sem=sib_send.at[wi], recv_sem=sib_recv.at[wi],
                                               device_id=(x, y, 1 - c), device_id_type=MESH)
            sib.start()
            sibs.append(sib)
        for wi in range(n):
            hr = g_refs[wi].shape[1] // 2
            other = out_refs[wi].at[pl.ds(pl.multiple_of((1 - c) * hr, 16), hr), :]
            pltpu.make_async_remote_copy(src_ref=other, dst_ref=other, send_sem=sib_send.at[wi], recv_sem=sib_recv.at[wi],
                                         device_id=(x, y, 1 - c), device_id_type=MESH).wait_recv()
            sibs[wi].wait_send()

    res = _pcall(
        body, name=name, out_shape=[jax.ShapeDtypeStruct(g.shape[1:], F32) for g in gs],
        in_specs=[ANY_SPEC] * (2 * n), out_specs=[pl.BlockSpec(memory_space=pltpu.VMEM)] * n,
        scratch_shapes=[pltpu.VMEM((N_DEV, g.shape[1] // 2, g.shape[2]), BF16) for g in gs]
        + [pltpu.SemaphoreType.DMA((8 * n,)), pltpu.SemaphoreType.DMA((n,)), pltpu.SemaphoreType.DMA((n,))],
        compiler_params=pltpu.CompilerParams(vmem_limit_bytes=VMEM_LIMIT))(*gs, *lands)
    return list(res)


def _gate_norm_epilogue(factor, x_in, gt, nxt, t, d):
    blk = (ROWS, d)
    full = lambda i, j: (i, j)
    rowv = lambda i, j: (0, j)

    def epi(acc, res, scale, *norm):
        x_new = res + (factor * scale) * acc
        if not norm:
            return x_new, acc
        gv, scv, shv = norm
        r = lax.rsqrt(jnp.mean(x_new * x_new, axis=-1, keepdims=True) + EPS)
        return x_new, acc, x_new * r * gv * (1.0 + scv) + shv

    ins = [(x_in, blk, full), (gt, (1, d), rowv)] + [(v, (1, d), rowv) for v in (nxt or ())]
    outs = [((t, d), F32, blk, full), ((t, d), BF16, blk, full)] + ([((t, d), BF16, blk, full)] if nxt else [])
    return epi, ins, outs


def _mixer_tail(name, ge, w_glu, b_glu, w_up, y_pool, uin, w_out, x_in, gt, nxt, d, pw, tm=ROWS):
    t, sw = ge.shape
    tm = min(tm, t)
    ns = w_glu.shape[2]
    cb = (2 * pw) // d

    def body(ge_ref, wg_ref, b_ref, wu_ref, yp_ref, glp_ref, gls_ref, wo_ref, x_ref, gt_ref, gn_ref, sc_ref, sh_ref,
             gv_ref, sg_ref, ys_ref, mg_ref, y_ref, h_ref, xn_ref):
        gev = ge_ref[...]
        gv = jnp.concatenate([jnp.dot(gev, wg_ref[s], preferred_element_type=F32) for s in range(N_CHIPS)], axis=1)
        gv_ref[...] = gv.astype(BF16)
        g = gv + b_ref[...]
        sg = (g[:, :sw] * _sigmoid(g[:, sw:])).astype(BF16)
        sg_ref[...] = sg
        ys = jnp.concatenate([jnp.dot(sg, wu_ref[s], preferred_element_type=F32) for s in range(N_CHIPS)], axis=1)
        ys_ref[...] = ys.astype(BF16)
        mg = (_sigmoid(glp_ref[...].astype(F32)) * yp_ref[...].astype(F32) + _sigmoid(gls_ref[...].astype(F32)) * ys).astype(BF16)
        mg_ref[...] = mg
        y = jnp.dot(mg, wo_ref[...], preferred_element_type=F32)
        y_ref[...] = y.astype(BF16)
        xn = x_ref[...] + gt_ref[...] * y
        xn_ref[...] = xn
        r = lax.rsqrt(jnp.mean(xn * xn, axis=-1, keepdims=True) + EPS)
        h_ref[...] = (xn * r * gn_ref[...] * (1.0 + sc_ref[...]) + sh_ref[...]).astype(BF16)

    full3 = lambda i: (0, 0, 0)
    full2 = lambda i: (0, 0)
    rows = lambda w: pl.BlockSpec((tm, w), lambda i: (i, 0))
    rowv = pl.BlockSpec((1, d), full2)
    return _pcall(
        body, name=name, grid=(t // tm,),
        in_specs=[rows(sw), pl.BlockSpec(w_glu.shape, full3), pl.BlockSpec(b_glu.shape, full2),
                  pl.BlockSpec(w_up.shape, full3), rows(d), pl.BlockSpec((tm, d), lambda i: (i, cb)),
                  pl.BlockSpec((tm, d), lambda i: (i, cb + 1)), pl.BlockSpec(w_out.shape, full2), rows(d), rowv, rowv, rowv, rowv],
        out_specs=[rows(N_CHIPS * ns), rows(sw), rows(d), rows(d), rows(d), rows(d), rows(d)],
        out_shape=[jax.ShapeDtypeStruct((t, N_CHIPS * ns), BF16), jax.ShapeDtypeStruct((t, sw), BF16)]
        + [jax.ShapeDtypeStruct((t, d), BF16)] * 4 + [jax.ShapeDtypeStruct((t, d), F32)],
        compiler_params=_params("parallel"))(ge, w_glu, b_glu, w_up, y_pool, uin, uin, w_out, x_in, gt, *nxt)


def _glu_up_bwd(name, dy, w_up, gv, b_glu, w_glu, sg, ge, tm=ROWS, deps=()):
    t, sw = ge.shape
    tm = min(tm, t)
    nb = t // tm
    ns = w_up.shape[2]
    nt = (((1,), (1,)), ((), ()))
    tn_ = (((0,), (0,)), ((), ()))

    def body(dy_ref, wu_ref, gv_ref, b_ref, wg_ref, sg_ref, ge_ref, *rest):
        dge_ref, db_ref, dwu_ref, dwg_ref, dwu_acc, dwg_acc = rest[len(deps):]
        i = pl.program_id(0)

        @pl.when(i == 0)
        def _():
            db_ref[...] = jnp.zeros_like(db_ref)
            dwu_acc[...] = jnp.zeros_like(dwu_acc)
            dwg_acc[...] = jnp.zeros_like(dwg_acc)

        dyv = dy_ref[...]
        dsg = None
        for s in range(N_CHIPS):
            q = lax.dot_general(dyv[:, s * ns:(s + 1) * ns], wu_ref[s], nt, preferred_element_type=F32)
            dsg = q if dsg is None else dsg + q
        g = gv_ref[...].astype(F32) + b_ref[...]
        val, sgm = g[:, :sw], _sigmoid(g[:, sw:])
        dval, dgate = dsg * sgm, dsg * val * sgm * (1.0 - sgm)
        db_ref[...] += jnp.concatenate([_colsum(dval), _colsum(dgate)], axis=1)
        dgv = jnp.concatenate([dval, dgate], axis=1).astype(BF16)
        dge = None
        for s in range(N_CHIPS):
            q = lax.dot_general(dgv[:, s * ns:(s + 1) * ns], wg_ref[s], nt, preferred_element_type=F32)
            dge = q if dge is None else dge + q
        dge_ref[...] = dge.astype(BF16)
        dwu_acc[...] += lax.dot_general(sg_ref[...], dyv, tn_, preferred_element_type=F32)
        dwg_acc[...] += lax.dot_general(ge_ref[...], dgv, tn_, preferred_element_type=F32)

        @pl.when(i == nb - 1)
        def _():
            for s in range(N_CHIPS):
                dwu_ref[s] = dwu_acc[:, s * ns:(s + 1) * ns].astype(BF16)
                dwg_ref[s] = dwg_acc[:, s * ns:(s + 1) * ns].astype(BF16)

    full3 = lambda i: (0, 0, 0)
    rows = lambda w: pl.BlockSpec((tm, w), lambda i: (i, 0))
    wide = N_CHIPS * ns
    return _pcall(
        body, name=name, grid=(nb,),
        in_specs=[rows(wide), pl.BlockSpec(w_up.shape, full3), rows(wide), pl.BlockSpec(b_glu.shape, lambda i: (0, 0)),
                  pl.BlockSpec(w_glu.shape, full3), rows(sw), rows(sw)] + [ANY_SPEC] * len(deps),
        out_specs=[rows(sw), pl.BlockSpec((1, wide), lambda i: (0, 0)), pl.BlockSpec(w_up.shape, full3),
                   pl.BlockSpec(w_glu.shape, full3)],
        out_shape=[jax.ShapeDtypeStruct((t, sw), BF16), jax.ShapeDtypeStruct((1, wide), F32),
                   jax.ShapeDtypeStruct(w_up.shape, BF16), jax.ShapeDtypeStruct(w_glu.shape, BF16)],
        scratch_shapes=[pltpu.VMEM((sw, wide), F32), pltpu.VMEM((sw, wide), F32)],
        compiler_params=_params("arbitrary"))(dy, w_up, gv, b_glu, w_glu, sg, ge, *deps)


_FULL = lambda i, j: (i, j)
_ROWV = lambda i, j: (0, j)


def _gates_bwd_epilogue(y_pool, y_ssm, uin, t, d, pw):
    cb = (2 * pw) // d

    def epi(dm, yp, ys, glp, gls):
        sp, ss = _sigmoid(glp.astype(F32)), _sigmoid(gls.astype(F32))
        dgl = jnp.concatenate([dm * yp.astype(F32) * sp * (1.0 - sp), dm * ys.astype(F32) * ss * (1.0 - ss)], axis=1)
        return dm * sp, dm * ss, ("at", 2 * pw, dgl)

    ins = [(y_pool, (ROWS, d), _FULL), (y_ssm, (ROWS, d), _FULL),
           (uin, (ROWS, d), lambda i, j: (i, cb)), (uin, (ROWS, d), lambda i, j: (i, cb + 1))]
    wide = 2 * pw + 2 * d
    return epi, ins, [((t, d), BF16, (ROWS, d), _FULL)] * 2 + [((t, wide), BF16, (ROWS, wide), _FULL)]


def _loss_epilogue(x_in, gt, tgt, g_final, t, d):
    blk = (ROWS, d)
    full = lambda i, j: (i, j)
    rowv = lambda i, j: (0, j)

    def epi(acc, res, scale, tv, gv):
        xv = res + (0.5 * scale) * acc
        r = lax.rsqrt(jnp.mean(xv * xv, axis=-1, keepdims=True) + EPS)
        xr = xv * r
        e = xr * gv - tv
        loss_row = 0.5 * jnp.mean(e * e, axis=-1, keepdims=True)
        dout = e * (1.0 / d)
        gd = gv * dout
        dx = r * (gd - xr * jnp.mean(gd * xr, axis=-1, keepdims=True))
        fdx = 0.5 * dx
        return [dx, scale * fdx], [_colsum(dout * xr), _colsum(loss_row * jnp.ones((1, 128), F32)), _colsum(fdx * acc)]

    ins = [(x_in, blk, full), (gt, (1, d), rowv), (tgt, blk, full), (g_final, (1, d), rowv)]
    return epi, ins, [((t, d), F32, blk, full), ((t, d), BF16, blk, full)], [d, 128, d]


def _norm_bwd_epilogue(x, dres, g, sc, up, t, d):
    blk = (ROWS, d)
    full = lambda i, j: (i, j)
    rowv = lambda i, j: (0, j)

    def epi(dhv, xv, drv, gv, scv, *rest):
        r = lax.rsqrt(jnp.mean(xv * xv, axis=-1, keepdims=True) + EPS)
        xr = xv * r
        dn = dhv * (1.0 + scv)
        gd = gv * dn
        dx = drv + r * (gd - xr * jnp.mean(gd * xr, axis=-1, keepdims=True))
        outs, reds = [dx], [_colsum(dhv), _colsum(dhv * xr * gv), _colsum(dn * xr)]
        if up:
            yv, gtv = rest
            fdx = up[2] * dx
            outs.append(gtv * fdx)
            reds.append(_colsum(fdx * yv.astype(F32)))
        return outs, reds

    ins = [(x, blk, full), (dres, blk, full), (g, (1, d), rowv), (sc, (1, d), rowv)]
    ins += [(up[0], blk, full), (up[1], (1, d), rowv)] if up else []
    outs = [((t, d), F32, blk, full)] + ([((t, d), BF16, blk, full)] if up else [])
    return epi, ins, outs, [d] * (4 if up else 3)


def _ffn_in_act(name, h, w_in, tm=ROWS):
    t, d = h.shape
    ns = w_in.shape[2]
    tm = _pick(t, tm, 8)
    w4 = w_in.reshape(2, 2, d, ns)

    def body(h_ref, w_ref, ab_ref, act_ref):
        hv = h_ref[...]
        a = jnp.dot(hv, w_ref[0], preferred_element_type=F32)
        b = jnp.dot(hv, w_ref[1], preferred_element_type=F32)
        ab_ref[0] = a.astype(BF16)
        ab_ref[1] = b.astype(BF16)
        act_ref[...] = (a * _sigmoid(a) * b).astype(BF16)

    return _pcall(body, name=name, grid=(2, t // tm),
                  in_specs=[pl.BlockSpec((tm, d), lambda c, i: (i, 0)), pl.BlockSpec((2, None, d, ns), lambda c, i: (0, c, 0, 0))],
                  out_specs=[pl.BlockSpec((2, tm, ns), lambda c, i: (0, i, c)), pl.BlockSpec((tm, ns), lambda c, i: (i, c))],
                  out_shape=[jax.ShapeDtypeStruct((2, t, 2 * ns), BF16), jax.ShapeDtypeStruct((t, 2 * ns), BF16)],
                  compiler_params=_params("parallel", "parallel"))(h, w4)


def _dswiglu_epilogue(ab, t, f, tn):
    blk = (2, ROWS, tn)
    idx = lambda i, j: (0, i, j)

    def epi(dact, abv):
        a, b = abv[0].astype(F32), abv[1].astype(F32)
        s = _sigmoid(a)
        return (jnp.stack([dact * b * (s * (1.0 + a * (1.0 - s))), dact * (a * s)]),)

    return epi, [(ab, blk, idx)], [((2, t, f), BF16, blk, idx)]


def _ffn_fwd(tag, x, h, gt, get_w_in, get_w_out, nxt=None, loss=None):
    t, d = x.shape
    ab, act = _ffn_in_act(tag + "_in", h, get_w_in(h))
    if loss:
        epi, epi_ins, epi_outs, epi_reds = _loss_epilogue(x, gt, *loss, t, d)
        res = _mm(tag + "_out", act, get_w_out(act), tm=ROWS, tn=d, epi=epi, epi_ins=epi_ins, epi_outs=epi_outs,
                  epi_reds=epi_reds, b_resident=True)
        return res, (x, h, ab, act, None)
    epi, epi_ins, epi_outs = _gate_norm_epilogue(0.5, x, gt, nxt, t, d)
    res = _mm(tag + "_out", act, get_w_out(act), tm=ROWS, tn=d, epi=epi, epi_ins=epi_ins, epi_outs=epi_outs, b_resident=True)
    return res[0], res[2], (x, h, ab, act, res[1])


def _ffn_bwd(tag, dx_new, dy, saved, g, sc, w_in, w_out, start_rs, up, deps=()):
    x, h, ab, act, _ = saved
    d = x.shape[1]
    f = act.shape[1]
    dw_out = _mm(tag + "_dwout", act, dy, ta=True, tm=1408, tn=d, tk=2048, deps=deps)
    tok = start_rs("out", dw_out.reshape(N_CHIPS, f // N_CHIPS, d))
    epi, epi_ins, epi_outs = _dswiglu_epilogue(ab, x.shape[0], f, w_in.shape[2])
    dab = _mm(tag + "_dact", dy, w_out, tb=True, tm=ROWS, tn=w_in.shape[2], epi=epi, epi_ins=epi_ins, epi_outs=epi_outs,
              deps=(tok,), j_outer=True)
    dw_in = _mm(tag + "_dwin", h, dab, ta=True, out_stacked=True, b_halves=True, tm=ROWS, tn=1408, tk=4096, j_outer=True)
    tok = start_rs("in", dw_in)
    epi, epi_ins, epi_outs, epi_reds = _norm_bwd_epilogue(x, dx_new, g, sc, up, x.shape[0], d)
    res = _mm(tag + "_dh", dab, w_in, tb=True, b_stacked=True, a_halves=True, tn=d, tk=5632, deps=(tok,), epi=epi,
              epi_ins=epi_ins, epi_outs=epi_outs, epi_reds=epi_reds, b_resident=True)
    if up:
        return res
    return res[0], None, res[1], res[2], res[3], None


def _row(v):
    return v.reshape(1, -1)


def _pack(parts):
    cols = []
    for p in parts:
        flat = p.reshape(-1).astype(F32)
        padn = (-flat.shape[0]) % 128
        cols.append(jnp.pad(flat, (0, padn)) if padn else flat)
    flat = jnp.concatenate(cols)
    padn = (-flat.shape[0]) % 1024
    if padn:
        flat = jnp.pad(flat, (0, padn))
    return flat.reshape(-1, 128)


def _unpack(packed, shapes):
    flat = packed.reshape(-1)
    out, off = [], 0
    for s in shapes:
        n = math.prod(s)
        out.append(flat[off:off + n].reshape(s))
        off += n + ((-n) % 128)
    return out


SMALL = ['b_ada', 'g_ffn1', 'g_mix', 'pool_w', 'pool_b', 'pool_scale', 'ssm_lam_re_log', 'ssm_lam_im', 'ssm_log_dt',
         'ssm_b_re', 'ssm_b_im', 'ssm_c_re', 'ssm_c_im', 'ssm_d', 'b_glu', 'g_ffn2', 'g_final']
BIG = ['w_ffn1_in', 'w_ffn1_out', 'w_in', 'w_pool_up', 'w_glu', 'w_ssm_up', 'w_out', 'w_ffn2_in', 'w_ffn2_out']
AG_GROUPS = [[0], [1], [2, 3, 4, 5, 6], [7, 8]]
AG_DIRECT = [False, True, False, True]
SMALL_LATE = ['b_ada', 'g_ffn1']
RS_SUM_GROUPS = [[0, 1, 2, 3], [4, 5]]
WEIGHTS = ['w_ada', 'b_ada', 'g_ffn1', 'w_ffn1_in', 'w_ffn1_out', 'g_mix', 'w_in', 'pool_w', 'pool_b', 'pool_scale', 'w_pool_up',
           'ssm_lam_re_log', 'ssm_lam_im', 'ssm_log_dt', 'ssm_b_re', 'ssm_b_im', 'ssm_c_re', 'ssm_c_im', 'ssm_d', 'w_glu', 'b_glu',
           'w_ssm_up', 'w_out', 'g_ffn2', 'w_ffn2_in', 'w_ffn2_out', 'g_final']


def kernel(x, c, w_ada, b_ada, g_ffn1, w_ffn1_in, w_ffn1_out, g_mix, w_in, pool_w, pool_b, pool_scale, w_pool_up, ssm_lam_re_log, ssm_lam_im, ssm_log_dt, ssm_b_re, ssm_b_im, ssm_c_re, ssm_c_im, ssm_d, w_glu, b_glu, w_ssm_up, w_out, g_ffn2, w_ffn2_in, w_ffn2_out, g_final, loss_target, m_w_ada, m_b_ada, m_g_ffn1, m_w_ffn1_in, m_w_ffn1_out, m_g_mix, m_w_in, m_pool_w, m_pool_b, m_pool_scale, m_w_pool_up, m_ssm_lam_re_log, m_ssm_lam_im, m_ssm_log_dt, m_ssm_b_re, m_ssm_b_im, m_ssm_c_re, m_ssm_c_im, m_ssm_d, m_w_glu, m_b_glu, m_w_ssm_up, m_w_out, m_g_ffn2, m_w_ffn2_in, m_w_ffn2_out, m_g_final, v_w_ada, v_b_ada, v_g_ffn1, v_w_ffn1_in, v_w_ffn1_out, v_g_mix, v_w_in, v_pool_w, v_pool_b, v_pool_scale, v_w_pool_up, v_ssm_lam_re_log, v_ssm_lam_im, v_ssm_log_dt, v_ssm_b_re, v_ssm_b_im, v_ssm_c_re, v_ssm_c_im, v_ssm_d, v_w_glu, v_b_glu, v_w_ssm_up, v_w_out, v_g_ffn2, v_w_ffn2_in, v_w_ffn2_out, v_g_final):
    args = dict(locals())
    wt = {n: args[n] for n in WEIGHTS}
    mom = {n: args["m_" + n] for n in WEIGHTS}
    var = {n: args["v_" + n] for n in WEIGHTS}

    t, d = x.shape[1], x.shape[2]
    pw = pool_b.shape[1]
    sw = ssm_d.shape[1]
    ngrp = sw // SSM_GROUP
    gn = ngrp * SSM_STATE
    xi, yi, ci = _place()
    b_me = 4 * xi + 2 * yi + ci
    s_me = 2 * xi + yi
    x2d = x[0]
    tgt = loss_target[0]

    c_all = _allgather_small("ag_c", c.reshape(8, d // 8)).reshape(N_DEV, d)
    ncol = w_ada.shape[2]
    b_sh = lax.dynamic_slice(b_ada, (0, s_me * ncol), (1, ncol))
    mod_sh = _ada_fwd("ada_fwd", c_all, w_ada[0], b_sh)
    md_send, md_recv, mod_sh, md_land, tok = _bcast_start("mod_start", mod_sh)

    shards = [wt[n][0].astype(BF16) for n in BIG]
    lands = _place_shards(shards, s_me)
    n0 = len(AG_GROUPS[0])
    sems_a, shards_a, lands_a, tok = _ag_start("ag_start_first", shards[:n0], lands[:n0], AG_GROUPS[:1], AG_DIRECT[:1], deps=(tok,))
    rest = [[w - n0 for w in grp] for grp in AG_GROUPS[1:]]
    sems_b, shards_b, lands_b, tok = _ag_start("ag_start_rest", shards[n0:], lands[n0:], rest, AG_DIRECT[1:], deps=(tok,))
    ag_sems, shards_t, lands_t = sems_a + sems_b, shards_a + shards_b, lands_a + lands_b
    md_land = _bcast_wait("mod_wait", mod_sh, md_land, md_send, md_recv, tok)
    mod_all = lax.dynamic_update_slice(md_land, mod_sh[None], (b_me, 0, 0))
    full = {}

    def weights(gi, *after):
        grp = AG_GROUPS[gi]
        if BIG[grp[0]] not in full:
            ls = _ag_wait("ag_wait%d" % gi, [shards_t[w] for w in grp], [lands_t[w] for w in grp], *ag_sems[gi],
                          AG_DIRECT[gi], after)
            for w, l in zip(grp, ls if AG_DIRECT[gi] else _ag_forward("ag_fwd%d" % gi, ls)):
                full[BIG[w]] = l
        return full

    mod_me = jnp.concatenate([lax.dynamic_slice(mod_all, (2 * s, b_me, 0), (1, 1, ncol))[0] for s in range(N_CHIPS)], axis=1)
    mod = mod_me.reshape(9, d)
    sh1, sc1, gt1, sh2, sc2, gt2, sh3, sc3, gt3 = [mod[k:k + 1] for k in range(9)]

    f = w_ffn1_out.shape[1] * N_CHIPS

    col = lambda a: a.reshape(gn, 1)
    lrl_c, li_c = col(ssm_lam_re_log), col(ssm_lam_im)
    ldt_c = col(jnp.broadcast_to(ssm_log_dt.reshape(ngrp, 1), (ngrp, SSM_STATE)))
    b_re2, b_im2 = ssm_b_re.reshape(gn, SSM_GROUP), ssm_b_im.reshape(gn, SSM_GROUP)
    lrdt_c, ang_c, bb_re, bb_im = _ssm_prep("ssm_prep", lrl_c, li_c, ldt_c, b_re2, b_im2)
    lrdt, ang = lrdt_c.reshape(1, gn), ang_c.reshape(1, gn)
    tr = lambda a: jnp.swapaxes(a, 1, 2)
    bbd = [_blockdiag_b(v, sw).astype(BF16) for v in (bb_re, bb_im)]
    ccd = [_blockdiag_c(v[0], sw).astype(BF16) for v in (ssm_c_re, ssm_c_im)]
    bbd_t, ccd_t = [tr(v) for v in bbd], [tr(v) for v in ccd]
    early = [n for n in SMALL if n not in SMALL_LATE]
    packs = {}
    for tag, names, extra in (("early", early, []), ("late", SMALL_LATE, [jnp.zeros((128,), F32)])):
        packs[tag] = [_pack([src[n] for n in names] + extra) for src in (wt, mom, var)]
    shadow = [lrdt, ang, ldt_c, *bbd, *ccd, *bbd_t, *ccd_t, *packs["early"], *packs["late"]]

    h1 = _norm_fwd("ffn1_norm", x2d, g_ffn1, sc1, sh1, deps=(tok,))
    x1, h2, sav1 = _ffn_fwd("ffn1", x2d, h1, gt1, lambda after: weights(0, after, *shadow)['w_ffn1_in'],
                            lambda after: weights(1, after)['w_ffn1_out'].reshape(f, d), (g_mix, sc2, sh2))
    weights(2, h2)
    wo = full['w_out'].reshape(d, d)
    uin = _mm("mix_in", h2, full['w_in'], b_stacked=True, tm=2048, tn=768, j_outer=True)
    p_pool, z_pool, y_pool = _pool_fwd("pool_fwd", uin, pool_w[0], pool_b, pool_scale, full['w_pool_up'], pw)
    y_s, ge, s_re, s_im = _ssm_fwd("ssm_fwd", uin, lrdt, ang, *bbd, *ccd, ssm_d, sw)
    gv, sg, y_ssm, merged, y2, h3, x2 = _mixer_tail("mixer_tail", ge, full['w_glu'], b_glu, full['w_ssm_up'], y_pool, uin, wo,
                                                    x1, gt2, (g_ffn2, sc3, sh3), d, pw)

    (dx3, dy3, dg_final, loss_v, dgt3), sav3 = _ffn_fwd(
        "ffn2", x2, h3, gt3, lambda after: weights(3, after)['w_ffn2_in'],
        lambda after: weights(3, after)['w_ffn2_out'].reshape(f, d), loss=(tgt, _row(g_final)))

    gs = {}
    rs_open = []

    def start_rs(names, gbs):
        send, recv, g_thru, land_thru, token = _rs_start("rs_start_" + names[0], gbs)
        rs_open.append((names, send, recv, g_thru, land_thru))
        return token

    dx2, dy2, dsh3, dsc3, gs['g_ffn2'], dgt2 = _ffn_bwd(
        "ffn2b", dx3, dy3, sav3, g_ffn2, sc3, full['w_ffn2_in'], full['w_ffn2_out'].reshape(f, d),
        lambda key, g: start_rs(['w_ffn2_' + key], [g]), (y2, gt2, 1.0))

    epi, epi_ins, epi_outs = _gates_bwd_epilogue(y_pool, y_ssm, uin, t, d, pw)
    dy_pool, dy_ssm, duin, g_wo = _mm("mix_dmerged", dy2, wo, tb=True, tm=ROWS, tn=d, epi=epi, epi_ins=epi_ins,
                                      epi_outs=epi_outs, b_resident=True, wgrad=merged, wgrad_rows=True)
    g_wo = g_wo.reshape(N_CHIPS, d // N_CHIPS, d)

    duin, gs['pool_w'], gs['pool_b'], gs['pool_scale'], g_wpu = _pool_bwd(
        "pool_bwd", dy_pool, z_pool, p_pool, pool_w[0], pool_b, pool_scale, full['w_pool_up'], duin)

    tok = start_rs(['w_out', 'w_pool_up'], [g_wo, g_wpu])
    dge, gs['b_glu'], g_wsu, g_wglu = _glu_up_bwd("glu_up_bwd", dy_ssm, full['w_ssm_up'], gv, b_glu, full['w_glu'], sg, ge,
                                                  deps=(tok,))
    (duin, g_abre, g_abim, g_bbd_re, g_bbd_im, g_ccd_re, g_ccd_im, gs['ssm_d']) = _ssm_bwd(
        "ssm_bwd", dge, y_s, uin, s_re, s_im, lrdt, ang, *bbd_t, *ccd_t, ssm_d, sw, duin)
    gs['ssm_c_re'], gs['ssm_c_im'] = _diag_of_c(g_ccd_re, sw), _diag_of_c(g_ccd_im, sw)
    d_lrl, d_li, d_ldt, d_bre, d_bim = _ssm_param_bwd(
        "ssm_param_bwd", lrl_c, li_c, ldt_c, b_re2, b_im2, g_abre.reshape(gn, 1), g_abim.reshape(gn, 1),
        _diag_of_b(g_bbd_re, sw), _diag_of_b(g_bbd_im, sw))
    gs['ssm_lam_re_log'], gs['ssm_lam_im'] = d_lrl, d_li
    gs['ssm_log_dt'] = jnp.sum(d_ldt.reshape(ngrp, SSM_STATE), axis=1)
    gs['ssm_b_re'], gs['ssm_b_im'] = d_bre, d_bim

    g_win =_mm("mix_dwin", h2, duin, ta=True, out_stacked=True, tm=d, tn=768, tk=4096)
    tok = start_rs(['w_ssm_up', 'w_glu', 'w_in'], [g_wsu, g_wglu, g_win])
    epi, epi_ins, epi_outs, epi_reds = _norm_bwd_epilogue(x1, dx2, g_mix, sc2, (sav1[4], gt1, 0.5), t, d)
    dx1, dy1, dsh2, dsc2, gs['g_mix'], dgt1 = _mm(
        "mix_dh", duin, full['w_in'], tb=True, b_stacked=True, tn=d, tk=3072, deps=(tok,), epi=epi, epi_ins=epi_ins,
        epi_outs=epi_outs, epi_reds=epi_reds, b_resident=True)

    gs['g_final'] = dg_final
    sg_blk = _pack([gs[n] for n in early])
    sg_send, sg_recv, sg_blk, sg_land, tok = _bcast_start("sg_start", sg_blk)

    dx0, _, dsh1, dsc1, gs['g_ffn1'], _ = _ffn_bwd(
        "ffn1b", dx1, dy1, sav1, g_ffn1, sc1, full['w_ffn1_in'], full['w_ffn1_out'].reshape(f, d),
        lambda key, g: start_rs(['w_ffn1_' + key], [g]), None, deps=(tok,))

    gs['b_ada'] = jnp.concatenate([dsh1, dsc1, dgt1, dsh2, dsc2, dgt2, dsh3, dsc3, dgt3], axis=1)
    lt_blk = _pack([gs[n] for n in SMALL_LATE] + [loss_v])
    lt_send, lt_recv, lt_blk, lt_land, tok = _bcast_start("late_start", lt_blk)

    grads, delta, new_m, new_v = {}, {}, {}, {}

    def small_update(tag, names, blk, land):
        g8 = lax.dynamic_update_slice(land, blk[None], (b_me, 0, 0)).reshape(-1, 128)
        w_pack, m_pack, v_pack = packs[tag]
        shapes = [wt[n].shape for n in names]
        per_param, packed = _adamw_small("adamw_small_" + tag, w_pack, g8, m_pack, v_pack, shapes)
        for k, dst in enumerate((grads, delta, new_m, new_v)):
            rest = _unpack(packed[k], shapes)
            for i, n in enumerate(names):
                dst[n] = per_param[i][k] if per_param[i] is not None else rest[i]
        return g8

    after = tok
    for group in RS_SUM_GROUPS:
        names, g_done, land_done = [], [], []
        for k in group:
            nk, send, recv, g_thru, land_thru = rs_open[k]
            gd, ld = _rs_wait("rs_wait_" + nk[0], g_thru, land_thru, send, recv, after)
            names, g_done, land_done = names + nk, g_done + gd, land_done + ld
        for n, g_sum in zip(names, _rs_sum("rs_sum_" + names[0], g_done, land_done)):
            g_out, dl, mn, vn = _adamw("adamw_" + n, wt[n][0], g_sum, mom[n][0], var[n][0])
            grads[n], delta[n], new_m[n], new_v[n] = g_out[None], dl[None], mn[None], vn[None]
            after = dl
        if group is RS_SUM_GROUPS[-2]:
            small_update("early", early, sg_blk, _bcast_wait("sg_wait", sg_blk, sg_land, sg_send, sg_recv, after))

    lt_land = _bcast_wait("late_wait", lt_blk, lt_land, lt_send, lt_recv, after)
    late8 = small_update("late", SMALL_LATE, lt_blk, lt_land).reshape(N_DEV, -1)
    loss = jnp.sum(late8[:, 10 * d])
    dmod_sh = lax.dynamic_slice(late8, (0, s_me * ncol), (N_DEV, ncol))
    g_w_ada, dl, mn, vn = _ada_bwd_adamw("ada_bwd_adamw", c_all, dmod_sh, w_ada[0], m_w_ada[0], v_w_ada[0])
    grads['w_ada'], delta['w_ada'], new_m['w_ada'], new_v['w_ada'] = g_w_ada[None], dl[None], mn[None], vn[None]

    return (loss, dx0[None], *[grads[n] for n in WEIGHTS], *[delta[n] for n in WEIGHTS],
            *[new_m[n] for n in WEIGHTS], *[new_v[n] for n in WEIGHTS])
```

```python
import functools
import math

import jax
import jax.numpy as jnp
from jax import lax
from jax.experimental import pallas as pl
from jax.experimental.pallas import tpu as pltpu

F32 = jnp.float32
BF16 = jnp.bfloat16
MESH = pl.DeviceIdType.MESH

EPS = 1e-6
POOL_WINDOWS = (2, 4, 8, 16)
POOL_HALO = 16
SSM_GROUP = 16
SSM_STATE = 64
SSM_BLOCKS = 4
N_DEV = 8
N_CHIPS = 4
ADAM_LR = 0.001
ADAM_B1 = 0.9
ADAM_B2 = 0.999
ADAM_EPS = 1e-08
ADAM_WD = 0.01
ADAM_STEP = 10
VMEM_LIMIT = 56 * 1024 * 1024
ROWS = 512


ANY_SPEC = pl.BlockSpec(memory_space=pl.ANY)
HBM_SPEC = pl.BlockSpec(memory_space=pltpu.HBM)
SEM_SPEC = pl.BlockSpec(memory_space=pltpu.SEMAPHORE)
EFFECT = pltpu.SideEffectType.DATAFLOW_SIDE_EFFECTING


def _hbm(a):
    return pltpu.with_memory_space_constraint(a, pltpu.HBM)


def _pcall(body, **kw):
    return pl.pallas_call(body, **kw)


def _params(*sem):
    return pltpu.CompilerParams(dimension_semantics=sem, vmem_limit_bytes=VMEM_LIMIT)


def _pick(n, cap, mult=128):
    if n <= cap:
        return n
    best = None
    for d in range(mult, cap + 1, mult):
        if n % d == 0:
            best = d
    assert best is not None, (n, cap, mult)
    return best


def _sigmoid(v):
    return 1.0 / (1.0 + jnp.exp(-v))


def _rowwise(name, fn, ins, params, outs, reds, tm, deps=()):
    t = ins[0][0].shape[0]
    tm = min(tm, t)
    nb = t // tm
    ni, npar, no, nd = len(ins), len(params), len(outs), len(deps)

    def body(*refs):
        iv = [r[...] for r in refs[:ni]]
        pv = [r[...] for r in refs[ni:ni + npar]]
        o_refs = refs[ni + npar + nd:ni + npar + nd + no]
        r_refs = refs[ni + npar + nd + no:]
        ovals, rvals = fn(iv, pv)
        for o_ref, val in zip(o_refs, ovals):
            off = 0
            if isinstance(val, tuple) and val[0] == "at":
                _, off, val = val
            parts = val if isinstance(val, (list, tuple)) else [val]
            for p in parts:
                o_ref[:, off:off + p.shape[1]] = p.astype(o_ref.dtype)
                off += p.shape[1]
        if r_refs:
            @pl.when(pl.program_id(0) == 0)
            def _():
                for r in r_refs:
                    r[...] = jnp.zeros_like(r)
            for r, val in zip(r_refs, rvals):
                r[...] += val

    in_specs = [pl.BlockSpec((tm, w), functools.partial(lambda i, cb: (i, cb), cb=cb)) for (_, w, cb) in ins]
    in_specs += [pl.BlockSpec(p.shape, lambda i: (0, 0)) for p in params]
    in_specs += [ANY_SPEC] * nd
    out_shape = [jax.ShapeDtypeStruct((t, w), dt) for (w, dt) in outs]
    out_shape += [jax.ShapeDtypeStruct((1, w), F32) for w in reds]
    out_specs = [pl.BlockSpec((tm, w), lambda i: (i, 0)) for (w, _) in outs]
    out_specs += [pl.BlockSpec((1, w), lambda i: (0, 0)) for w in reds]
    res = _pcall(body, name=name, grid=(nb,), in_specs=in_specs, out_specs=out_specs, out_shape=out_shape,
                 compiler_params=_params("arbitrary"))(*[a for a, _, _ in ins], *params, *deps)
    return res


def _colsum(v):
    return jnp.sum(v, axis=0, keepdims=True)


def _mm(name, a, b, *, ta=False, tb=False, b_stacked=False, out_stacked=False, tm=ROWS, tn=1024, tk=2816,
        out_dtype=BF16, epi=None, epi_ins=(), epi_outs=None, epi_reds=(), deps=(), j_outer=False, a_halves=False,
        b_halves=False, b_resident=False, wgrad=None, wgrad_rows=False):
    if a_halves:
        _, m, kdim = a.shape
        kdim *= 2
    elif ta:
        kdim, m = a.shape
    else:
        m, kdim = a.shape
    ns = None
    if b_stacked:
        ns = b.shape[2]
        n = b.shape[1] if tb else N_CHIPS * ns
        assert kdim == (N_CHIPS * ns if tb else b.shape[1]), (name, a.shape, b.shape)
    else:
        if b_halves:
            n = 2 * b.shape[2]
            assert kdim == b.shape[1] and not tb, (name, a.shape, b.shape)
        else:
            n = b.shape[0] if tb else b.shape[1]
            assert kdim == (b.shape[1] if tb else b.shape[0]), (name, a.shape, b.shape)
        if out_stacked:
            ns = n // N_CHIPS

    def shards(want):
        return max(g for g in (1, 2, 4) if g * ns <= max(want, ns))

    tm = _pick(m, tm, 128 if ta else 8)
    gn = gk = 1
    if (b_stacked and not tb) or out_stacked:
        gn = shards(min(tn, n // 2) if b_halves else tn)
        tn = gn * ns
    else:
        tn = _pick(n, tn)
    if b_stacked and tb:
        gk = shards(tk)
        tk = gk * ns
    else:
        tk = _pick(kdim, tk, 8 if ta else 128)
    nm, nn, nk = m // tm, n // tn, kdim // tk

    def ij(f):
        return (lambda g0, g1, k: f(g1, g0, k)) if j_outer else f

    if a_halves:
        assert b_stacked and tb and gk == N_CHIPS and nk == 1, name
        a_spec = pl.BlockSpec((2, tm, kdim // 2), ij(lambda i, j, k: (0, i, 0)))
    elif ta:
        a_spec = pl.BlockSpec((tk, tm), ij(lambda i, j, k: (k, i)))
    else:
        a_spec = pl.BlockSpec((tm, tk), ij(lambda i, j, k: (i, k)))
    if b_stacked and not tb:
        b_spec = pl.BlockSpec((gn, tk, ns), ij(lambda i, j, k: (j, k, 0)))
    elif b_stacked and tb:
        b_spec = pl.BlockSpec((gk, tn, ns), ij(lambda i, j, k: (k, j, 0)))
    elif b_halves:
        bph = (n // 2) // tn
        b_spec = pl.BlockSpec((None, tk, tn), ij(lambda i, j, k: (j // bph, k, j % bph)))
    elif tb:
        b_spec = pl.BlockSpec((tn, tk), ij(lambda i, j, k: (j, k)))
    else:
        b_spec = pl.BlockSpec((tk, tn), ij(lambda i, j, k: (k, j)))
    dims = (((0 if ta else 1,), (1 if tb else 0,)), ((), ()))

    if epi_outs is None:
        if out_stacked:
            epi_outs = [((N_CHIPS, m, ns), out_dtype, (gn, tm, ns), lambda i, j: (j, i, 0))]
        else:
            epi_outs = [((m, n), out_dtype, (tm, tn), lambda i, j: (i, j))]
    ne, no, nd, nr = len(epi_ins), len(epi_outs), len(deps), len(epi_reds)
    nx = 0 if wgrad is None else 1
    assert not (nr or nx) or (nn == 1 and not j_outer), name
    assert not nx or (nk == 1 and not ta and not a_halves and kdim % N_CHIPS == 0), name

    def body(a_ref, b_ref, *rest):
        e_refs = rest[:ne]
        x_refs = rest[ne:ne + nx]
        o_refs = rest[ne + nx + nd:ne + nx + nd + no]
        r_refs = rest[ne + nx + nd + no:ne + nx + nd + no + nr]
        wg_refs = rest[ne + nx + nd + no + nr:ne + nx + nd + no + nr + nx]
        scratch = rest[ne + nx + nd + no + nr + nx:]
        av = None if a_halves else a_ref[...].astype(BF16)
        if nx:
            wg_acc = scratch[-1]
            i = pl.program_id(0)
            pw_ = lax.dot_general(x_refs[0][...].astype(BF16), av, (((0,), (0,)), ((), ())), preferred_element_type=F32)

            @pl.when(i == 0)
            def _():
                wg_acc[...] = pw_

            @pl.when(i > 0)
            def _():
                wg_acc[...] += pw_

            @pl.when(i == nm - 1)
            def _():
                if wgrad_rows:
                    wg_refs[0][...] = wg_acc[...].astype(BF16)
                else:
                    ks = kdim // N_CHIPS
                    for s in range(N_CHIPS):
                        wg_refs[0][s] = wg_acc[:, s * ks:(s + 1) * ks].astype(BF16)
        if b_stacked and not tb:
            parts = [lax.dot_general(av, b_ref[s].astype(BF16), dims, preferred_element_type=F32) for s in range(gn)]
        elif b_stacked and tb:
            p = None
            for s in range(gk):
                if a_halves:
                    a_s = a_ref[s // 2, :, (s % 2) * ns:(s % 2 + 1) * ns].astype(BF16)
                else:
                    a_s = av[:, s * ns:(s + 1) * ns]
                q = lax.dot_general(a_s, b_ref[s].astype(BF16), dims, preferred_element_type=F32)
                p = q if p is None else p + q
            parts = [p]
        else:
            parts = [lax.dot_general(av, b_ref[...].astype(BF16), dims, preferred_element_type=F32)]

        def finish(acc_parts):
            if epi is None and out_stacked:
                acc = acc_parts[0]
                for s in range(gn):
                    o_refs[0][s] = acc[:, s * ns:(s + 1) * ns].astype(out_dtype)
            elif epi is None:
                w = acc_parts[0].shape[1]
                for s, part in enumerate(acc_parts):
                    o_refs[0][:, s * w:(s + 1) * w] = part.astype(out_dtype)
            else:
                acc = acc_parts[0] if len(acc_parts) == 1 else jnp.concatenate(acc_parts, axis=1)
                vals = epi(acc, *[r[...] for r in e_refs])
                if nr:
                    vals, reds = vals

                    @pl.when(pl.program_id(0) == 0)
                    def _():
                        for r in r_refs:
                            r[...] = jnp.zeros_like(r)
                    for r, val in zip(r_refs, reds):
                        r[...] += val
                for o_ref, val in zip(o_refs, vals):
                    if isinstance(val, tuple) and val[0] == "at":
                        o_ref[:, val[1]:val[1] + val[2].shape[1]] = val[2].astype(o_ref.dtype)
                    else:
                        o_ref[...] = val.astype(o_ref.dtype)

        if nk == 1:
            finish(parts)
        else:
            acc_ref = scratch[0]
            k = pl.program_id(2)
            w = parts[0].shape[1]

            @pl.when(k == 0)
            def _():
                for s, part in enumerate(parts):
                    acc_ref[:, s * w:(s + 1) * w] = part

            @pl.when(k > 0)
            def _():
                for s, part in enumerate(parts):
                    acc_ref[:, s * w:(s + 1) * w] += part

            @pl.when(k == nk - 1)
            def _():
                finish([acc_ref[...]])

    def _ij(f):
        return ij(lambda i, j, k: f(i, j))

    if b_resident:
        assert nk == 1 and (nn == 1 or j_outer), name
        b_spec = pl.BlockSpec(b_spec.block_shape, b_spec.index_map, pipeline_mode=pl.Buffered(1))
    in_specs = [a_spec, b_spec] + [pl.BlockSpec(blk, _ij(f)) for (_, blk, f) in epi_ins]
    if nx:
        in_specs.append(pl.BlockSpec((tm, wgrad.shape[1]), lambda i, j, k: (i, 0)))
    in_specs += [ANY_SPEC] * nd
    out_specs = [pl.BlockSpec(blk, _ij(f)) for (_, _, blk, f) in epi_outs]
    out_specs += [pl.BlockSpec((1, w), lambda *_: (0, 0)) for w in epi_reds]
    out_shape = [jax.ShapeDtypeStruct(s, dt) for (s, dt, _, _) in epi_outs] + [jax.ShapeDtypeStruct((1, w), F32) for w in epi_reds]
    scratch = [pltpu.VMEM((tm, tn), F32)] if nk > 1 else []
    if nx:
        wg_shape = (wgrad.shape[1], kdim) if wgrad_rows else (N_CHIPS, wgrad.shape[1], kdim // N_CHIPS)
        out_specs.append(pl.BlockSpec(wg_shape, lambda *_: (0,) * len(wg_shape)))
        out_shape.append(jax.ShapeDtypeStruct(wg_shape, BF16))
        scratch.append(pltpu.VMEM((wgrad.shape[1], kdim), F32))
    grid = (nn, nm, nk) if j_outer else (nm, nn, nk)
    sem = ("arbitrary",) * 3 if (nr or nx) else ("parallel", "parallel", "arbitrary")
    res = _pcall(body, name=name, grid=grid, in_specs=in_specs, out_specs=out_specs, out_shape=out_shape, scratch_shapes=scratch,
                 compiler_params=_params(*sem))(a, b, *[x for x, _, _ in epi_ins], *([wgrad] * nx), *deps)
    return res[0] if len(res) == 1 else res


def _norm_fwd(name, x, g, sc, sh, deps=()):
    d = x.shape[1]

    def fn(iv, pv):
        (xv,), (gv, scv, shv) = iv, pv
        r = lax.rsqrt(jnp.mean(xv * xv, axis=-1, keepdims=True) + EPS)
        return [xv * r * gv * (1.0 + scv) + shv], []

    return _rowwise(name, fn, [(x, d, 0)], [g, sc, sh], [(d, BF16)], [], ROWS, deps=deps)[0]


def _adamw(name, w, g, m, v, tm=256):
    c = w.shape[1]

    def fn(iv, pv):
        wv, gv, mv, vv = iv
        mn = ADAM_B1 * mv + (1.0 - ADAM_B1) * gv
        vn = ADAM_B2 * vv + (1.0 - ADAM_B2) * (gv * gv)
        m_hat = mn / (1.0 - ADAM_B1 ** ADAM_STEP)
        v_hat = vn / (1.0 - ADAM_B2 ** ADAM_STEP)
        delta = -ADAM_LR * (m_hat / (jnp.sqrt(v_hat) + ADAM_EPS) + ADAM_WD * wv)
        return [gv, delta, mn, vn], []

    return _rowwise(name, fn, [(w, c, 0), (g, c, 0), (m, c, 0), (v, c, 0)], [], [(c, F32)] * 4, [], _pick(w.shape[0], tm, 8))


def _unpack_plan(shape):
    n = math.prod(shape)
    if len(shape) == 2 and shape[0] == 1 and n % 128 == 0:
        return [((slice(None), slice(128 * r, 128 * (r + 1))), slice(r, r + 1), slice(None)) for r in range(n // 128)]
    if len(shape) == 2 and shape[0] == 1 and n < 128:
        return [((slice(None), slice(None)), slice(0, 1), slice(0, n))]
    if len(shape) == 1 and n % 128 == 0:
        return [((slice(128 * r, 128 * (r + 1)),), r, slice(None)) for r in range(n // 128)]
    if len(shape) == 3 and shape[0] == 1 and shape[2] == 64:
        return [((0, slice(2 * r + h, 2 * r + h + 1), slice(None)), slice(r, r + 1), slice(64 * h, 64 * (h + 1)))
                for r in range(n // 128) for h in range(2)]
    if len(shape) == 4 and shape[0] == 1 and shape[2:] == (128, 128):
        return [((0, k), slice(128 * k, 128 * (k + 1)), slice(None)) for k in range(shape[1])]
    return None


def _adamw_small(name, w, g8, m, v, shapes):
    r = w.shape[0]
    plans, rows0, off = [], [], 0
    for s in shapes:
        plans.append(_unpack_plan(s))
        rows0.append(off // 128)
        off += math.prod(s) + (-math.prod(s)) % 128
    direct = [i for i, p in enumerate(plans) if p is not None]

    def body(w_ref, g_ref, m_ref, v_ref, *out):
        packed = out[4 * len(direct):]
        gv = g_ref[0:r, :]
        for k in range(1, N_DEV):
            gv = gv + g_ref[k * r:(k + 1) * r, :]
        mn = ADAM_B1 * m_ref[...] + (1.0 - ADAM_B1) * gv
        vn = ADAM_B2 * v_ref[...] + (1.0 - ADAM_B2) * (gv * gv)
        m_hat = mn / (1.0 - ADAM_B1 ** ADAM_STEP)
        v_hat = vn / (1.0 - ADAM_B2 ** ADAM_STEP)
        packed[0][...] = gv
        packed[1][...] = -ADAM_LR * (m_hat / (jnp.sqrt(v_hat) + ADAM_EPS) + ADAM_WD * w_ref[...])
        packed[2][...] = mn
        packed[3][...] = vn
        for di, i in enumerate(direct):
            for kind in range(4):
                o_ref, src = out[4 * di + kind], packed[kind]
                for o_idx, row, lanes in plans[i]:
                    row = (rows0[i] + row) if isinstance(row, int) else slice(rows0[i] + row.start, rows0[i] + row.stop)
                    o_ref[o_idx] = src[row, lanes]

    out_shape = [jax.ShapeDtypeStruct(shapes[i], F32) for i in direct for _ in range(4)]
    out_shape += [jax.ShapeDtypeStruct((r, 128), F32)] * 4
    res = _pcall(body, name=name, out_shape=out_shape,
                 compiler_params=pltpu.CompilerParams(vmem_limit_bytes=VMEM_LIMIT))(w, g8, m, v)
    per_param = [None] * len(shapes)
    for di, i in enumerate(direct):
        per_param[i] = res[4 * di:4 * di + 4]
    return per_param, res[4 * len(direct):]


def _pool_fwd(name, uin, pool_w, pool_b, pool_scale, w_up, pw, tm=ROWS):
    t = uin.shape[0]
    tm = min(tm, t)
    ng = len(POOL_WINDOWS)
    gw = pw // ng
    ns = w_up.shape[2]

    def body(u_ref, w_ref, b_ref, s_ref, wu_ref, p_ref, z_ref, y_ref, ext):
        i = pl.program_id(0)

        @pl.when(i == 0)
        def _():
            ext[0:POOL_HALO, :] = jnp.zeros((POOL_HALO, pw), F32)

        u = u_ref[...].astype(F32)
        ext[POOL_HALO:POOL_HALO + tm, :] = u
        pos = i * tm + lax.broadcasted_iota(jnp.int32, (tm, 1), 0)
        for k, win in enumerate(POOL_WINDOWS):
            cols = slice(k * gw, (k + 1) * gw)
            acc = u[:, cols]
            for j in range(1, win):
                acc = acc + ext[POOL_HALO - j:POOL_HALO - j + tm, cols]
            cnt = jnp.minimum(pos + 1, win).astype(F32)
            z = acc / cnt - u[:, cols]
            zp = jnp.dot(z.astype(BF16), w_ref[k].astype(BF16), preferred_element_type=F32) + b_ref[:, cols]
            p_ref[:, cols] = (zp * s_ref[:, cols]).astype(BF16)
            z_ref[:, cols] = z.astype(BF16)
        ext[0:POOL_HALO, :] = u[tm - POOL_HALO:tm, :]
        pv = p_ref[...]
        for s in range(N_CHIPS):
            y_ref[:, s * ns:(s + 1) * ns] = jnp.dot(pv, wu_ref[s], preferred_element_type=F32).astype(BF16)

    full3 = lambda i: (0, 0, 0)
    return _pcall(
        body, name=name, grid=(t // tm,),
        in_specs=[pl.BlockSpec((tm, pw), lambda i: (i, 0)), pl.BlockSpec(pool_w.shape, full3),
                  pl.BlockSpec(pool_b.shape, lambda i: (0, 0)), pl.BlockSpec(pool_scale.shape, lambda i: (0, 0)),
                  pl.BlockSpec(w_up.shape, full3)],
        out_specs=[pl.BlockSpec((tm, pw), lambda i: (i, 0))] * 2 + [pl.BlockSpec((tm, N_CHIPS * ns), lambda i: (i, 0))],
        out_shape=[jax.ShapeDtypeStruct((t, pw), BF16)] * 2 + [jax.ShapeDtypeStruct((t, N_CHIPS * ns), BF16)],
        scratch_shapes=[pltpu.VMEM((POOL_HALO + tm, pw), F32)],
        compiler_params=_params("arbitrary"))(uin, pool_w, pool_b, pool_scale, w_up)


def _pool_bwd(name, dy, z, p, pool_w, pool_b, pool_scale, w_up, duin, tm=ROWS):
    t, pw = z.shape
    tm = min(tm, t)
    nb = t // tm
    ng = len(POOL_WINDOWS)
    gw = pw // ng
    ns = w_up.shape[2]

    def body(dy_ref, z_ref, p_ref, w_ref, b_ref, s_ref, wu_ref, duin_ref, du_ref, dw_ref, db_ref, ds_ref, dwu_ref, ext, dwu_acc):
        i = pl.program_id(0)

        @pl.when(i == 0)
        def _():
            ext[tm:tm + POOL_HALO, :] = jnp.zeros((POOL_HALO, pw), F32)
            dw_ref[...] = jnp.zeros_like(dw_ref)
            db_ref[...] = jnp.zeros_like(db_ref)
            ds_ref[...] = jnp.zeros_like(ds_ref)
            dwu_acc[...] = jnp.zeros_like(dwu_acc)

        dwu_acc[...] += lax.dot_general(p_ref[...], dy_ref[...], (((0,), (0,)), ((), ())), preferred_element_type=F32)

        @pl.when(i == nb - 1)
        def _():
            for s in range(N_CHIPS):
                dwu_ref[s] = dwu_acc[:, s * ns:(s + 1) * ns].astype(BF16)

        pos = (nb - 1 - i) * tm + lax.broadcasted_iota(jnp.int32, (tm, 1), 0)
        dp = None
        for s in range(N_CHIPS):
            q = lax.dot_general(dy_ref[:, s * ns:(s + 1) * ns], wu_ref[s], (((1,), (1,)), ((), ())), preferred_element_type=F32)
            dp = q if dp is None else dp + q
        for k, win in enumerate(POOL_WINDOWS):
            cols = slice(k * gw, (k + 1) * gw)
            zk = z_ref[:, cols]
            dpk = dp[:, cols]
            wk = w_ref[k].astype(BF16)
            zp = jnp.dot(zk, wk, preferred_element_type=F32) + b_ref[:, cols]
            ds_ref[:, cols] += _colsum(dpk * zp)
            dzp = dpk * s_ref[:, cols]
            db_ref[:, cols] += _colsum(dzp)
            dzpb = dzp.astype(BF16)
            dz = lax.dot_general(dzpb, wk, (((1,), (1,)), ((), ())), preferred_element_type=F32)
            dw_ref[k] += lax.dot_general(zk, dzpb, (((0,), (0,)), ((), ())), preferred_element_type=F32)
            cnt = jnp.minimum(pos + 1, win).astype(F32)
            r = dz / cnt
            ext[0:tm, cols] = r
            acc = r - dz
            for j in range(1, win):
                acc = acc + ext[j:j + tm, cols]
            du_ref[:, cols] = acc.astype(BF16)
        ext[tm:tm + POOL_HALO, :] = ext[0:POOL_HALO, :]

    rev = lambda i: (nb - 1 - i, 0)
    full3 = lambda i: (0, 0, 0)
    return _pcall(
        body, name=name, grid=(nb,),
        in_specs=[pl.BlockSpec((tm, N_CHIPS * ns), rev), pl.BlockSpec((tm, pw), rev), pl.BlockSpec((tm, pw), rev),
                  pl.BlockSpec(pool_w.shape, full3), pl.BlockSpec(pool_b.shape, lambda i: (0, 0)),
                  pl.BlockSpec(pool_scale.shape, lambda i: (0, 0)), pl.BlockSpec(w_up.shape, full3), ANY_SPEC],
        out_specs=[pl.BlockSpec((tm, pw), rev), pl.BlockSpec(pool_w.shape, full3),
                   pl.BlockSpec((1, pw), lambda i: (0, 0)), pl.BlockSpec((1, pw), lambda i: (0, 0)), pl.BlockSpec(w_up.shape, full3)],
        out_shape=[jax.ShapeDtypeStruct(duin.shape, BF16), jax.ShapeDtypeStruct(pool_w.shape, F32),
                   jax.ShapeDtypeStruct((1, pw), F32), jax.ShapeDtypeStruct((1, pw), F32), jax.ShapeDtypeStruct(w_up.shape, BF16)],
        scratch_shapes=[pltpu.VMEM((tm + POOL_HALO, pw), F32), pltpu.VMEM((pw, N_CHIPS * ns), F32)], input_output_aliases={7: 0},
        compiler_params=_params("arbitrary"))(dy, z, p, pool_w, pool_b, pool_scale, w_up, duin)


def _ssm_disc(lrl, li, ldt):
    lr = -jnp.exp(lrl)
    dt = jnp.exp(ldt)
    mag = jnp.exp(lr * dt)
    ang = li * dt
    ab_re = mag * jnp.cos(ang)
    ab_im = mag * jnp.sin(ang)
    num_re = ab_re - 1.0
    num_im = ab_im
    den = lr * lr + li * li
    f_re = (num_re * lr + num_im * li) / den
    f_im = (num_im * lr - num_re * li) / den
    return lr, dt, mag, ang, ab_re, ab_im, num_re, num_im, den, f_re, f_im


def _ssm_prep(name, lrl, li, ldt, b_re, b_im):
    gn, h = b_re.shape

    def body(lrl_ref, li_ref, ldt_ref, br_ref, bi_ref, lrdt_ref, ang_ref, bbr_ref, bbi_ref):
        lr, dt, _, ang, _, _, _, _, _, f_re, f_im = _ssm_disc(lrl_ref[...], li_ref[...], ldt_ref[...])
        lrdt_ref[...] = lr * dt
        ang_ref[...] = ang
        br, bi = br_ref[...], bi_ref[...]
        bbr_ref[...] = f_re * br - f_im * bi
        bbi_ref[...] = f_re * bi + f_im * br

    col = jax.ShapeDtypeStruct((gn, 1), F32)
    mat = jax.ShapeDtypeStruct((gn, h), F32)
    return _pcall(body, name=name, out_shape=[col, col, mat, mat])(lrl, li, ldt, b_re, b_im)


def _ssm_param_bwd(name, lrl, li, ldt, b_re, b_im, g_abre, g_abim, g_bbre, g_bbim):
    gn, h = b_re.shape

    def body(lrl_ref, li_ref, ldt_ref, br_ref, bi_ref, gar_ref, gai_ref, gbr_ref, gbi_ref,
             dlrl_ref, dli_ref, dldt_ref, dbr_ref, dbi_ref):
        li_v = li_ref[...]
        lr, dt, mag, ang, ab_re, ab_im, num_re, num_im, den, f_re, f_im = _ssm_disc(lrl_ref[...], li_v, ldt_ref[...])
        br, bi = br_ref[...], bi_ref[...]
        gbr, gbi = gbr_ref[...], gbi_ref[...]
        g_fre = jnp.sum(gbr * br + gbi * bi, axis=1, keepdims=True)
        g_fim = jnp.sum(gbi * br - gbr * bi, axis=1, keepdims=True)
        dbr_ref[...] = gbr * f_re + gbi * f_im
        dbi_ref[...] = gbi * f_re - gbr * f_im
        g_num_re = (g_fre * lr - g_fim * li_v) / den
        g_num_im = (g_fre * li_v + g_fim * lr) / den
        g_den = -(g_fre * f_re + g_fim * f_im) / den
        g_lr = (g_fre * num_re + g_fim * num_im) / den + g_den * 2.0 * lr
        g_li = (g_fre * num_im - g_fim * num_re) / den + g_den * 2.0 * li_v
        g_are = gar_ref[...] + g_num_re
        g_aim = gai_ref[...] + g_num_im
        g_mag = g_are * jnp.cos(ang) + g_aim * jnp.sin(ang)
        g_ang = g_aim * ab_re - g_are * ab_im
        g_lrdt = g_mag * mag
        g_lr = g_lr + g_lrdt * dt
        g_dt = g_lrdt * lr + g_ang * li_v
        g_li = g_li + g_ang * dt
        dlrl_ref[...] = g_lr * lr
        dli_ref[...] = g_li
        dldt_ref[...] = g_dt * dt

    col = jax.ShapeDtypeStruct((gn, 1), F32)
    mat = jax.ShapeDtypeStruct((gn, h), F32)
    return _pcall(body, name=name, out_shape=[col, col, col, mat, mat])(lrl, li, ldt, b_re, b_im, g_abre, g_abim, g_bbre, g_bbim)


def _pow_rows(lrdt, ang, k):
    mag = jnp.exp(k * lrdt)
    return mag * jnp.cos(k * ang), mag * jnp.sin(k * ang)


def _ssm_chunk(t):
    return 256 if t >= 2048 else 128


def _to_segments(dst, srcs, jn):
    for q, src in enumerate(srcs):
        for j in range(jn):
            dst[8 * j:8 * j + 8, 128 * q:128 * (q + 1)] = src[pl.ds(j, 8, stride=jn), :]


def _from_segments(dst, src, q, jn, dtype):
    for s in range(8):
        dst[s * jn:(s + 1) * jn, 128 * q:128 * (q + 1)] = src[q, pl.ds(s, jn, stride=8), :].astype(dtype)


def _fill_rows8(dst_re, dst_im, v_re, v_im):
    for j in range(v_re.shape[0]):
        dst_re[8 * j:8 * j + 8, :] = jnp.broadcast_to(v_re[j:j + 1, :], (8, v_re.shape[1]))
        dst_im[8 * j:8 * j + 8, :] = jnp.broadcast_to(v_im[j:j + 1, :], (8, v_im.shape[1]))


def _cmul(ar, ai, br, bi):
    return ar * br - ai * bi, ar * bi + ai * br


def _cmul_conj(ar, ai, br, bi):
    return ar * br + ai * bi, ar * bi - ai * br


def _ssm_fwd(name, uin, lrdt, ang, bb_re, bb_im, cc_re, cc_im, d_skip, sw):
    t = uin.shape[0]
    gn = lrdt.shape[1]
    lc = _ssm_chunk(t)
    jn = lc // 8
    ub, sb = sw // SSM_BLOCKS, gn // SSM_BLOCKS
    nq = sw // 128
    assert ub == 128 and nq == SSM_BLOCKS

    def body(u_ref, lrdt_ref, ang_ref, bbr_ref, bbi_ref, ccr_ref, cci_ref, d_ref, y_ref, ge_ref, sre_ref, sim_ref,
             p_re, p_im, a_re, a_im, up, yp, cst_re, cst_im, car_re, car_im):
        i = pl.program_id(0)
        lrdt_v, ang_v = lrdt_ref[...], ang_ref[...]

        @pl.when(i == 0)
        def _():
            k = (lax.broadcasted_iota(jnp.int32, (jn, 1), 0) + 1).astype(F32)
            _fill_rows8(p_re, p_im, *_pow_rows(lrdt_v, ang_v, k))
            car_re[...] = jnp.zeros_like(car_re)
            car_im[...] = jnp.zeros_like(car_im)

        for q in range(nq):
            yp[q] = u_ref[:, 128 * q:128 * (q + 1)].astype(F32)
        _to_segments(up, [yp.at[q] for q in range(nq)], jn)
        u = up[...]
        ubf = u.astype(BF16)
        for q in range(SSM_BLOCKS):
            uq = ubf[:, q * ub:(q + 1) * ub]
            a_re[:, q * sb:(q + 1) * sb] = jnp.dot(uq, bbr_ref[q], preferred_element_type=F32)
            a_im[:, q * sb:(q + 1) * sb] = jnp.dot(uq, bbi_ref[q], preferred_element_type=F32)
        a1r, a1i = _pow_rows(lrdt_v, ang_v, 1.0)
        ajr, aji = _pow_rows(lrdt_v, ang_v, float(jn))
        for q in range(0, SSM_BLOCKS, 2):
            cols = slice(q * sb, (q + 2) * sb)
            ar8 = jnp.broadcast_to(a1r[:, cols], (8, 2 * sb))
            ai8 = jnp.broadcast_to(a1i[:, cols], (8, 2 * sb))

            def step(j, carry, cols=cols, ar8=ar8, ai8=ai8):
                sr, si = carry
                rows = pl.ds(pl.multiple_of(j * 8, 8), 8)
                mr, mi = _cmul(ar8, ai8, sr, si)
                nr, ni = mr + a_re[rows, cols], mi + a_im[rows, cols]
                a_re[rows, cols] = nr
                a_im[rows, cols] = ni
                return nr, ni

            lax.fori_loop(1, jn, step, (a_re[0:8, cols], a_im[0:8, cols]), unroll=8)
        er, ei = a_re[lc - 8:lc, :], a_im[lc - 8:lc, :]
        gr, gi = car_re[...], car_im[...]
        for s in range(8):
            cst_re[s:s + 1, :] = gr
            cst_im[s:s + 1, :] = gi
            mr, mi = _cmul(ajr, aji, gr, gi)
            gr, gi = mr + er[s:s + 1, :], mi + ei[s:s + 1, :]
        car_re[...] = gr
        car_im[...] = gi
        for q in range(SSM_BLOCKS):
            cols = slice(q * sb, (q + 1) * sb)
            cr = jnp.tile(cst_re[:, cols], (jn, 1))
            ci = jnp.tile(cst_im[:, cols], (jn, 1))
            mr, mi = _cmul(p_re[:, cols], p_im[:, cols], cr, ci)
            srb, sib = (a_re[:, cols] + mr).astype(BF16), (a_im[:, cols] + mi).astype(BF16)
            sre_ref[:, cols] = srb
            sim_ref[:, cols] = sib
            ycols = slice(q * ub, (q + 1) * ub)
            y = (jnp.dot(srb, ccr_ref[q], preferred_element_type=F32) - jnp.dot(sib, cci_ref[q], preferred_element_type=F32)
                 + d_ref[:, ycols] * u[:, ycols])
            yp[q] = y
            _from_segments(y_ref, yp, q, jn, F32)
            yt = y_ref[:, ycols]
            ge_ref[:, ycols] = (0.5 * yt * (1.0 + lax.erf(yt * (1.0 / math.sqrt(2.0))))).astype(BF16)

    row = lambda i: (0, 0)
    blk3 = lambda i: (0, 0, 0)
    return _pcall(
        body, name=name, grid=(t // lc,),
        in_specs=[pl.BlockSpec((lc, sw), lambda i: (i, 1)), pl.BlockSpec((1, gn), row), pl.BlockSpec((1, gn), row),
                  pl.BlockSpec(bb_re.shape, blk3), pl.BlockSpec(bb_im.shape, blk3),
                  pl.BlockSpec(cc_re.shape, blk3), pl.BlockSpec(cc_im.shape, blk3), pl.BlockSpec((1, sw), row)],
        out_specs=[pl.BlockSpec((lc, sw), lambda i: (i, 0)), pl.BlockSpec((lc, sw), lambda i: (i, 0)),
                   pl.BlockSpec((lc, gn), lambda i: (i, 0)), pl.BlockSpec((lc, gn), lambda i: (i, 0))],
        out_shape=[jax.ShapeDtypeStruct((t, sw), F32), jax.ShapeDtypeStruct((t, sw), BF16),
                   jax.ShapeDtypeStruct((t, gn), BF16), jax.ShapeDtypeStruct((t, gn), BF16)],
        scratch_shapes=[pltpu.VMEM((lc, gn), F32), pltpu.VMEM((lc, gn), F32), pltpu.VMEM((lc, gn), F32), pltpu.VMEM((lc, gn), F32),
                        pltpu.VMEM((lc, sw), F32), pltpu.VMEM((nq, lc, 128), F32),
                        pltpu.VMEM((8, gn), F32), pltpu.VMEM((8, gn), F32), pltpu.VMEM((1, gn), F32), pltpu.VMEM((1, gn), F32)],
        compiler_params=_params("arbitrary"))(uin, lrdt, ang, bb_re, bb_im, cc_re, cc_im, d_skip)


def _ssm_bwd(name, dge, y, uin, s_re, s_im, lrdt, ang, bbt_re, bbt_im, cct_re, cct_im, d_skip, sw, duin):
    t = uin.shape[0]
    gn = lrdt.shape[1]
    lc = _ssm_chunk(t)
    nb = t // lc
    jn = lc // 8
    ub, sb = sw // SSM_BLOCKS, gn // SSM_BLOCKS
    nq = sw // 128
    tail = 16

    def body(dge_ref, y_ref, u_ref, sre_ref, sim_ref, tre_ref, tim_ref, lrdt_ref, ang_ref, btr_ref, bti_ref, ctr_ref, cti_ref,
             d_ref, duin_ref, du_ref, dar_ref, dai_ref, dbr_ref, dbi_ref, dcr_ref, dci_ref, dd_ref,
             q_re, q_im, a_re, a_im, dyp, up, dys, us, dup, cst_re, cst_im, sp_re, sp_im, car_re, car_im):
        i = pl.program_id(0)
        lrdt_v, ang_v = lrdt_ref[...], ang_ref[...]

        @pl.when(i == 0)
        def _():
            k = (jn - lax.broadcasted_iota(jnp.int32, (jn, 1), 0)).astype(F32)
            _fill_rows8(q_re, q_im, *_pow_rows(lrdt_v, ang_v, k))
            car_re[...] = jnp.zeros_like(car_re)
            car_im[...] = jnp.zeros_like(car_im)
            for r in (dar_ref, dai_ref, dbr_ref, dbi_ref, dcr_ref, dci_ref, dd_ref):
                r[...] = jnp.zeros_like(r)

        yv = y_ref[...]
        ut = u_ref[...].astype(F32)
        cdf =0.5 * (1.0 + lax.erf(yv * (1.0 / math.sqrt(2.0))))
        pdf = jnp.exp(-0.5 * yv * yv) * (1.0 / math.sqrt(2.0 * math.pi))
        dyt = dge_ref[...].astype(F32) * (cdf + yv * pdf)
        dd_ref[...] += _colsum(dyt * ut)
        for q in range(nq):
            dys[q] = dyt[:, 128 * q:128 * (q + 1)]
            us[q] = ut[:, 128 * q:128 * (q + 1)]
        _to_segments(dyp, [dys.at[q] for q in range(nq)], jn)
        _to_segments(up, [us.at[q] for q in range(nq)], jn)
        dy = dyp[...]
        u = up[...]
        dyb = dy.astype(BF16)
        ubf = u.astype(BF16)
        for q in range(SSM_BLOCKS):
            dq = dyb[:, q * ub:(q + 1) * ub]
            a_re[:, q * sb:(q + 1) * sb] = jnp.dot(dq, ctr_ref[q], preferred_element_type=F32)
            a_im[:, q * sb:(q + 1) * sb] = -jnp.dot(dq, cti_ref[q], preferred_element_type=F32)
        a1r, a1i = _pow_rows(lrdt_v, ang_v, 1.0)
        ajr, aji = _pow_rows(lrdt_v, ang_v, float(jn))
        for q in range(0, SSM_BLOCKS, 2):
            cols = slice(q * sb, (q + 2) * sb)
            ar8 = jnp.broadcast_to(a1r[:, cols], (8, 2 * sb))
            ai8 = jnp.broadcast_to(a1i[:, cols], (8, 2 * sb))

            def step(jj, carry, cols=cols, ar8=ar8, ai8=ai8):
                sr, si = carry
                rows = pl.ds(pl.multiple_of((jn - 2 - jj) * 8, 8), 8)
                mr, mi = _cmul_conj(ar8, ai8, sr, si)
                nr, ni = mr + a_re[rows, cols], mi + a_im[rows, cols]
                a_re[rows, cols] = nr
                a_im[rows, cols] = ni
                return nr, ni

            lax.fori_loop(0, jn - 1, step, (a_re[lc - 8:lc, cols], a_im[lc - 8:lc, cols]), unroll=8)
        er, ei = a_re[0:8, :], a_im[0:8, :]
        hr, hi = car_re[...], car_im[...]
        for s in range(7, -1, -1):
            cst_re[s:s + 1, :] = hr
            cst_im[s:s + 1, :] = hi
            mr, mi = _cmul_conj(ajr, aji, hr, hi)
            hr, hi = mr + er[s:s + 1, :], mi + ei[s:s + 1, :]
        car_re[...] = hr
        car_im[...] = hi
        first = (i == nb - 1).astype(F32)
        sp_re[0:tail, :] = tre_ref[...].astype(F32) * (1.0 - first)
        sp_im[0:tail, :] = tim_ref[...].astype(F32) * (1.0 - first)
        sp_re[tail:tail + 8, :] = sre_ref[lc - tail:lc, :].astype(F32)[tail - 8:tail]
        sp_im[tail:tail + 8, :] = sim_ref[lc - tail:lc, :].astype(F32)[tail - 8:tail]
        tn_dims = (((0,), (0,)), ((), ()))
        for q in range(SSM_BLOCKS):
            cols = slice(q * sb, (q + 1) * sb)
            ycols = slice(q * ub, (q + 1) * ub)
            cr = jnp.tile(cst_re[:, cols], (jn, 1))
            ci = jnp.tile(cst_im[:, cols], (jn, 1))
            mr, mi = _cmul_conj(q_re[:, cols], q_im[:, cols], cr, ci)
            lam_r, lam_i = a_re[:, cols] + mr, a_im[:, cols] + mi
            s_r, s_i = sre_ref[:, cols], sim_ref[:, cols]
            p0r, p0i = sp_re[tail - 1:tail + 7, cols], sp_im[tail - 1:tail + 7, cols]
            l0r, l0i, l1r, l1i = lam_r[0:8], lam_i[0:8], lam_r[8:lc], lam_i[8:lc]
            pvr, pvi = s_r.astype(F32)[0:lc - 8], s_i.astype(F32)[0:lc - 8]
            dar_ref[:, cols] += _colsum(l1r * pvr + l1i * pvi) + _colsum(l0r * p0r + l0i * p0i)
            dai_ref[:, cols] += _colsum(l1i * pvr - l1r * pvi) + _colsum(l0i * p0r - l0r * p0i)
            lrb, lib = lam_r.astype(BF16), lam_i.astype(BF16)
            dup[q] = (jnp.dot(lrb, btr_ref[q], preferred_element_type=F32)
                      + jnp.dot(lib, bti_ref[q], preferred_element_type=F32) + d_ref[:, ycols] * dy[:, ycols])
            _from_segments(du_ref, dup, q, jn, BF16)
            uq = ubf[:, ycols]
            dbr_ref[q] += lax.dot_general(uq, lrb, tn_dims, preferred_element_type=F32)
            dbi_ref[q] += lax.dot_general(uq, lib, tn_dims, preferred_element_type=F32)
            dq = dyb[:, ycols]
            dcr_ref[q] += lax.dot_general(s_r.astype(BF16), dq, tn_dims, preferred_element_type=F32)
            dci_ref[q] -= lax.dot_general(s_i.astype(BF16), dq, tn_dims, preferred_element_type=F32)

    rev = lambda i: (nb - 1 - i, 0)
    tailmap = lambda i: (jnp.maximum((nb - 1 - i) * (lc // tail) - 1, 0), 0)
    row = lambda i: (0, 0)
    blk3 = lambda i: (0, 0, 0)
    return _pcall(
        body, name=name, grid=(nb,),
        in_specs=[pl.BlockSpec((lc, sw), rev), pl.BlockSpec((lc, sw), rev), pl.BlockSpec((lc, sw), lambda i: (nb - 1 - i, 1)),
                  pl.BlockSpec((lc, gn), rev), pl.BlockSpec((lc, gn), rev),
                  pl.BlockSpec((tail, gn), tailmap), pl.BlockSpec((tail, gn), tailmap),
                  pl.BlockSpec((1, gn), row), pl.BlockSpec((1, gn), row),
                  pl.BlockSpec(bbt_re.shape, blk3), pl.BlockSpec(bbt_im.shape, blk3),
                  pl.BlockSpec(cct_re.shape, blk3), pl.BlockSpec(cct_im.shape, blk3), pl.BlockSpec((1, sw), row), ANY_SPEC],
        out_specs=[pl.BlockSpec((lc, sw), lambda i: (nb - 1 - i, 1)), pl.BlockSpec((1, gn), row), pl.BlockSpec((1, gn), row),
                   pl.BlockSpec((SSM_BLOCKS, ub, sb), blk3), pl.BlockSpec((SSM_BLOCKS, ub, sb), blk3),
                   pl.BlockSpec((SSM_BLOCKS, sb, ub), blk3), pl.BlockSpec((SSM_BLOCKS, sb, ub), blk3),
                   pl.BlockSpec((1, sw), row)],
        out_shape=[jax.ShapeDtypeStruct(duin.shape, BF16), jax.ShapeDtypeStruct((1, gn), F32), jax.ShapeDtypeStruct((1, gn), F32),
                   jax.ShapeDtypeStruct((SSM_BLOCKS, ub, sb), F32), jax.ShapeDtypeStruct((SSM_BLOCKS, ub, sb), F32),
                   jax.ShapeDtypeStruct((SSM_BLOCKS, sb, ub), F32), jax.ShapeDtypeStruct((SSM_BLOCKS, sb, ub), F32),
                   jax.ShapeDtypeStruct((1, sw), F32)],
        scratch_shapes=[pltpu.VMEM((lc, gn), F32), pltpu.VMEM((lc, gn), F32), pltpu.VMEM((lc, gn), F32), pltpu.VMEM((lc, gn), F32),
                        pltpu.VMEM((lc, sw), F32), pltpu.VMEM((lc, sw), F32),
                        pltpu.VMEM((nq, lc, 128), F32), pltpu.VMEM((nq, lc, 128), F32), pltpu.VMEM((nq, lc, 128), F32),
                        pltpu.VMEM((8, gn), F32), pltpu.VMEM((8, gn), F32),
                        pltpu.VMEM((tail + 8, gn), F32), pltpu.VMEM((tail + 8, gn), F32),
                        pltpu.VMEM((1, gn), F32), pltpu.VMEM((1, gn), F32)],
        input_output_aliases={14: 0},
        compiler_params=_params("arbitrary"))(dge, y, uin, s_re, s_im, s_re, s_im, lrdt, ang,
                                               bbt_re, bbt_im, cct_re, cct_im, d_skip, duin)


def _blockdiag_b(bb, sw):
    gpb = (sw // SSM_GROUP) // SSM_BLOCKS
    b4 = bb.reshape(SSM_BLOCKS, gpb, SSM_STATE, SSM_GROUP)
    eye = jnp.eye(gpb, dtype=bb.dtype)
    out = jnp.einsum('qgnh,gk->qghkn', b4, eye)
    return out.reshape(SSM_BLOCKS, gpb * SSM_GROUP, gpb * SSM_STATE)


def _blockdiag_c(cc, sw):
    gpb = (sw // SSM_GROUP) // SSM_BLOCKS
    c4 = cc.reshape(SSM_BLOCKS, gpb, SSM_GROUP, SSM_STATE)
    eye = jnp.eye(gpb, dtype=cc.dtype)
    out = jnp.einsum('qghn,gk->qgnkh', c4, eye)
    return out.reshape(SSM_BLOCKS, gpb * SSM_STATE, gpb * SSM_GROUP)


def _diag_of_b(dbb, sw):
    gpb = (sw // SSM_GROUP) // SSM_BLOCKS
    d5 = dbb.reshape(SSM_BLOCKS, gpb, SSM_GROUP, gpb, SSM_STATE)
    return jnp.einsum('qghgn->qgnh', d5).reshape(SSM_BLOCKS * gpb * SSM_STATE, SSM_GROUP)


def _diag_of_c(dcc, sw):
    gpb = (sw // SSM_GROUP) // SSM_BLOCKS
    d5 = dcc.reshape(SSM_BLOCKS, gpb, SSM_STATE, gpb, SSM_GROUP)
    return jnp.einsum('qgngh->qghn', d5).reshape(SSM_BLOCKS * gpb, SSM_GROUP, SSM_STATE)


def _ada_fwd(name, c_all, w_sh, b_sh):
    nb, d = c_all.shape
    ncol = w_sh.shape[1]
    tn = _pick(ncol, 768)

    def body(c_ref, w_ref, b_ref, o_ref):
        cv = c_ref[...]
        sil = cv * _sigmoid(cv)
        o_ref[...] = jnp.dot(sil, w_ref[...], preferred_element_type=F32, precision=lax.Precision.HIGHEST) + b_ref[...]

    return _pcall(body, name=name, grid=(ncol // tn,),
                  in_specs=[pl.BlockSpec((nb, d), lambda j: (0, 0)), pl.BlockSpec((d, tn), lambda j: (0, j)),
                            pl.BlockSpec((1, tn), lambda j: (0, j))],
                  out_specs=pl.BlockSpec((nb, tn), lambda j: (0, j)),
                  out_shape=jax.ShapeDtypeStruct((nb, ncol), F32), compiler_params=_params("parallel"))(c_all, w_sh, b_sh)


def _ada_bwd_adamw(name, c_all, dmod_sh, w, m, v):
    nb, d = c_all.shape
    ncol = dmod_sh.shape[1]
    tr, tn = _pick(d, 512), _pick(ncol, 768)

    def body(c_ref, g_ref, w_ref, m_ref, v_ref, go_ref, d_ref, mo_ref, vo_ref):
        cv = c_ref[...]
        sil = cv * _sigmoid(cv)
        gv = lax.dot_general(sil, g_ref[...], (((0,), (0,)), ((), ())), preferred_element_type=F32,
                             precision=lax.Precision.HIGHEST)
        mn = ADAM_B1 * m_ref[...] + (1.0 - ADAM_B1) * gv
        vn = ADAM_B2 * v_ref[...] + (1.0 - ADAM_B2) * (gv * gv)
        m_hat = mn / (1.0 - ADAM_B1 ** ADAM_STEP)
        v_hat = vn / (1.0 - ADAM_B2 ** ADAM_STEP)
        go_ref[...] = gv
        d_ref[...] = -ADAM_LR * (m_hat / (jnp.sqrt(v_hat) + ADAM_EPS) + ADAM_WD * w_ref[...])
        mo_ref[...] = mn
        vo_ref[...] = vn

    tile = pl.BlockSpec((tr, tn), lambda i, j: (i, j))
    return _pcall(body, name=name, grid=(d // tr, ncol // tn),
                  in_specs=[pl.BlockSpec((nb, tr), lambda i, j: (0, i)), pl.BlockSpec((nb, tn), lambda i, j: (0, j)), tile, tile, tile],
                  out_specs=[tile] * 4, out_shape=[jax.ShapeDtypeStruct((d, ncol), F32)] * 4,
                  compiler_params=_params("parallel", "parallel"))(c_all, dmod_sh, w, m, v)


def _place():
    return lax.axis_index("x"), lax.axis_index("y"), lax.axis_index("c")


def _allgather_small(name, blk, deps=()):
    m_per, n = blk.shape

    def body(x_ref, *rest):
        out_ref, send_sems, recv_sems, local_sem = rest[len(deps):]
        x, y, c = _place()
        me, sibling = (x, y, c), (x, y, 1 - c)
        chips = [(1 - x, y), (x, 1 - y), (1 - x, 1 - y)]

        def rows(px, py, pc):
            return out_ref.at[pl.ds((4 * px + 2 * py + pc) * m_per, m_per), :]

        def copy(k, block, to, src=None):
            return pltpu.make_async_remote_copy(
                src_ref=rows(*block) if src is None else src, dst_ref=rows(*block),
                send_sem=send_sems.at[k], recv_sem=recv_sems.at[k], device_id=to, device_id_type=MESH)

        mine = pltpu.make_async_copy(x_ref, rows(*me), local_sem)
        mine.start()
        first = [copy(0, me, sibling, src=x_ref)]
        first += [copy(1 + j, me, (*chip, c), src=x_ref) for j, chip in enumerate(chips)]
        for cp in first:
            cp.start()
        passed = [copy(4 + j, (*chip, c), sibling) for j, chip in enumerate(chips)]
        for j, chip in enumerate(chips):
            copy(1 + j, (*chip, c), me).wait_recv()
            passed[j].start()
        copy(0, sibling, me).wait_recv()
        for j, chip in enumerate(chips):
            copy(4 + j, (*chip, 1 - c), me).wait_recv()
        for cp in first + passed:
            cp.wait_send()
        mine.wait()

    return _pcall(body, name=name, out_shape=jax.ShapeDtypeStruct((N_DEV * m_per, n), blk.dtype),
                  in_specs=[pl.BlockSpec(memory_space=pltpu.VMEM)] + [ANY_SPEC] * len(deps),
                  out_specs=pl.BlockSpec(memory_space=pltpu.VMEM),
                  scratch_shapes=[pltpu.SemaphoreType.DMA((7,)), pltpu.SemaphoreType.DMA((7,)), pltpu.SemaphoreType.DMA],
                  compiler_params=pltpu.CompilerParams(vmem_limit_bytes=VMEM_LIMIT))(blk, *deps)


def _other_chips(x, y):
    return [(1 - x, y), (x, 1 - y), (1 - x, 1 - y)]


def _place_shards(shards, s_me):
    return [lax.dynamic_update_slice(lax.empty((N_CHIPS,) + s.shape, s.dtype), s[None], (s_me, 0, 0)) for s in shards]


def _ag_copy(src, land, send, recv, wi, j, chip, x, y, c, tc, both):
    hr = src.shape[0] // 2
    half = pl.ds(pl.multiple_of(c * hr, 16), hr)
    k = 3 * wi + j
    return pltpu.make_async_remote_copy(
        src_ref=src.at[half, :], dst_ref=land.at[2 * x + y, half, :],
        send_sem=send.at[2 * k + tc if both else k], recv_sem=recv.at[2 * k + c if both else k],
        device_id=(chip[0], chip[1], tc), device_id_type=MESH)


def _ag_targets(c, both):
    return (0, 1) if both else (c,)


def _ag_start(name, shards, lands, groups, direct, deps=()):
    nw, ng, nd = len(shards), len(groups), len(deps)

    def body(*refs):
        src, land = refs[:nw], refs[nw:2 * nw]
        sems = refs[2 * nw + nd:2 * nw + nd + 2 * ng]
        token = refs[-1]
        x, y, c = _place()
        for gi, grp in enumerate(groups):
            for wi, w in enumerate(grp):
                for j, chip in enumerate(_other_chips(x, y)):
                    for tc in _ag_targets(c, direct[gi]):
                        _ag_copy(src[w], land[w], sems[2 * gi], sems[2 * gi + 1], wi, j, chip, x, y, c, tc, direct[gi]).start()
        token[...] = jnp.zeros_like(token)

    sem_shapes = []
    for gi, grp in enumerate(groups):
        sem_shapes += [pltpu.SemaphoreType.DMA(((6 if direct[gi] else 3) * len(grp),))] * 2
    out_shape = sem_shapes + [pltpu.HBM(s.shape, s.dtype) for s in shards] + [pltpu.HBM(l.shape, l.dtype) for l in lands]
    out_shape += [jax.ShapeDtypeStruct((8, 128), F32)]
    res = _pcall(body, name=name, out_shape=out_shape, in_specs=[HBM_SPEC] * (2 * nw) + [ANY_SPEC] * nd,
                 out_specs=[SEM_SPEC] * (2 * ng) + [HBM_SPEC] * (2 * nw) + [pl.BlockSpec(memory_space=pltpu.VMEM)],
                 input_output_aliases={i: 2 * ng + i for i in range(2 * nw)},
                 compiler_params=pltpu.CompilerParams(has_side_effects=EFFECT))(
                     *[_hbm(s) for s in shards], *[_hbm(l) for l in lands], *deps)
    sems = [(res[2 * gi], res[2 * gi + 1]) for gi in range(ng)]
    return sems, list(res[2 * ng:2 * ng + nw]), list(res[2 * ng + nw:2 * ng + 2 * nw]), res[-1]


def _ag_wait(name, shards, lands, send, recv, both, after):
    n = len(shards)

    def body(*refs):
        src, land = refs[:n], refs[n:2 * n]
        send_sem, recv_sem = refs[2 * n], refs[2 * n + 1]
        x, y, c = _place()
        for wi in range(n):
            for j, chip in enumerate(_other_chips(x, y)):
                for tc in _ag_targets(c, both):
                    _ag_copy(src[wi], land[wi], send_sem, recv_sem, wi, j, chip, x, y, c, tc, both).wait_send()
                    _ag_copy(src[wi], land[wi], send_sem, recv_sem, wi, j, chip, chip[0], chip[1], tc, c, both).wait_recv()

    res = _pcall(body, name=name, out_shape=[pltpu.HBM(a.shape, a.dtype) for a in list(shards) + list(lands)],
                 in_specs=[HBM_SPEC] * (2 * n) + [SEM_SPEC, SEM_SPEC] + [ANY_SPEC] * len(after), out_specs=[HBM_SPEC] * (2 * n),
                 input_output_aliases={i: i for i in range(2 * n)},
                 compiler_params=pltpu.CompilerParams(has_side_effects=EFFECT))(*shards, *lands, send, recv, *after)
    return list(res[n:])


def _ag_forward(name, lands):
    n = len(lands)

    def body(*refs):
        out = refs[n:2 * n]
        send, recv = refs[2 * n], refs[2 * n + 1]
        x, y, c = _place()
        sib = (x, y, 1 - c)
        cps = []
        for wi in range(n):
            hr = out[wi].shape[1] // 2
            for j, (cx, cy) in enumerate(_other_chips(x, y)):
                got = out[wi].at[2 * cx + cy, pl.ds(pl.multiple_of(c * hr, 16), hr), :]
                cp = pltpu.make_async_remote_copy(src_ref=got, dst_ref=got, send_sem=send.at[3 * wi + j], recv_sem=recv.at[3 * wi + j],
                                                  device_id=sib, device_id_type=MESH)
                cp.start()
                cps.append(cp)
        for wi in range(n):
            hr = out[wi].shape[1] // 2
            for j, (cx, cy) in enumerate(_other_chips(x, y)):
                got = out[wi].at[2 * cx + cy, pl.ds(pl.multiple_of((1 - c) * hr, 16), hr), :]
                pltpu.make_async_remote_copy(src_ref=got, dst_ref=got, send_sem=send.at[3 * wi + j], recv_sem=recv.at[3 * wi + j],
                                             device_id=sib, device_id_type=MESH).wait_recv()
        for cp in cps:
            cp.wait_send()

    res = _pcall(body, name=name, out_shape=[jax.ShapeDtypeStruct(l.shape, l.dtype) for l in lands],
                 in_specs=[ANY_SPEC] * n, out_specs=[ANY_SPEC] * n, input_output_aliases={i: i for i in range(n)},
                 scratch_shapes=[pltpu.SemaphoreType.DMA((3 * n,)), pltpu.SemaphoreType.DMA((3 * n,))])(*lands)
    return list(res)


def _peers(x, y, c):
    offs = [(dx, dy, dc) for dx in (0, 1) for dy in (0, 1) for dc in (0, 1)][1:]
    return [(1 - x if dx else x, 1 - y if dy else y, 1 - c if dc else c) for dx, dy, dc in offs]


def _rs_copy(g_ref, land_ref, send, recv, wi, k, to, sender):
    hr = g_ref.shape[1] // 2
    return pltpu.make_async_remote_copy(
        src_ref=g_ref.at[2 * to[0] + to[1], pl.ds(pl.multiple_of(to[2] * hr, 16), hr), :], dst_ref=land_ref.at[sender],
        send_sem=send.at[7 * wi + k], recv_sem=recv.at[7 * wi + k], device_id=to, device_id_type=MESH)


def _rs_start(name, gs):
    n = len(gs)
    lands = [lax.empty((N_DEV, g.shape[1] // 2, g.shape[2]), BF16) for g in gs]

    def body(*refs):
        g, land = refs[:n], refs[n:2 * n]
        send, recv = refs[2 * n], refs[2 * n + 1]
        token = refs[-1]
        x, y, c = _place()
        me = 4 * x + 2 * y + c
        for wi in range(n):
            for k, to in enumerate(_peers(x, y, c)):
                _rs_copy(g[wi], land[wi], send, recv, wi, k, to, me).start()
        token[...] = jnp.zeros_like(token)

    out_shape = [pltpu.SemaphoreType.DMA((7 * n,))] * 2 + [pltpu.HBM(a.shape, a.dtype) for a in list(gs) + lands]
    out_shape += [jax.ShapeDtypeStruct((8, 128), F32)]
    res = _pcall(body, name=name, out_shape=out_shape, in_specs=[HBM_SPEC] * (2 * n),
                 out_specs=[SEM_SPEC] * 2 + [HBM_SPEC] * (2 * n) + [pl.BlockSpec(memory_space=pltpu.VMEM)],
                 input_output_aliases={i: 2 + i for i in range(2 * n)},
                 compiler_params=pltpu.CompilerParams(has_side_effects=EFFECT))(
                     *[_hbm(a) for a in gs], *[_hbm(a) for a in lands])
    return res[0], res[1], list(res[2:2 + n]), list(res[2 + n:2 + 2 * n]), res[-1]


def _rs_wait(name, gs, lands, send, recv, after):
    n = len(gs)

    def body(*refs):
        g, land = refs[:n], refs[n:2 * n]
        send_sem, recv_sem = refs[2 * n], refs[2 * n + 1]
        x, y, c = _place()
        me = 4 * x + 2 * y + c
        for wi in range(n):
            for k, to in enumerate(_peers(x, y, c)):
                _rs_copy(g[wi], land[wi], send_sem, recv_sem, wi, k, to, me).wait_send()
                _rs_copy(g[wi], land[wi], send_sem, recv_sem, wi, k, (x, y, c), 4 * to[0] + 2 * to[1] + to[2]).wait_recv()

    res = _pcall(body, name=name, out_shape=[pltpu.HBM(a.shape, a.dtype) for a in list(gs) + list(lands)],
                 in_specs=[HBM_SPEC] * (2 * n) + [SEM_SPEC, SEM_SPEC, ANY_SPEC], out_specs=[HBM_SPEC] * (2 * n),
                 input_output_aliases={i: i for i in range(2 * n)},
                 compiler_params=pltpu.CompilerParams(has_side_effects=EFFECT))(*gs, *lands, send, recv, after)
    return list(res[:n]), list(res[n:])


def _bc_copy(blk_ref, land_ref, send, recv, k, to, slot):
    return pltpu.make_async_remote_copy(src_ref=blk_ref, dst_ref=land_ref.at[slot], send_sem=send.at[k], recv_sem=recv.at[k],
                                        device_id=to, device_id_type=MESH)


def _bcast_start(name, blk):
    land = lax.empty((N_DEV,) + blk.shape, blk.dtype)

    def body(blk_ref, land_ref, send, recv, blk_thru, land_thru, token):
        x, y, c = _place()
        for k, to in enumerate(_peers(x, y, c)):
            _bc_copy(blk_ref, land_ref, send, recv, k, to, 4 * x + 2 * y + c).start()
        token[...] = jnp.zeros_like(token)

    return _pcall(body, name=name,
                  out_shape=[pltpu.SemaphoreType.DMA((7,)), pltpu.SemaphoreType.DMA((7,)), pltpu.HBM(blk.shape, blk.dtype),
                             pltpu.HBM(land.shape, land.dtype), jax.ShapeDtypeStruct((8, 128), F32)],
                  in_specs=[HBM_SPEC, HBM_SPEC], out_specs=[SEM_SPEC, SEM_SPEC, HBM_SPEC, HBM_SPEC, pl.BlockSpec(memory_space=pltpu.VMEM)],
                  input_output_aliases={0: 2, 1: 3}, compiler_params=pltpu.CompilerParams(has_side_effects=EFFECT))(_hbm(blk), _hbm(land))


def _bcast_wait(name, blk, land, send, recv, after):
    def body(blk_ref, land_ref, send_sem, recv_sem, after_ref, blk_thru, land_thru):
        x, y, c = _place()
        for k, to in enumerate(_peers(x, y, c)):
            _bc_copy(blk_ref, land_ref, send_sem, recv_sem, k, to, 4 * x + 2 * y + c).wait_send()
            _bc_copy(blk_ref, land_ref, send_sem, recv_sem, k, to, 4 * to[0] + 2 * to[1] + to[2]).wait_recv()

    return _pcall(body, name=name, out_shape=[pltpu.HBM(blk.shape, blk.dtype), pltpu.HBM(land.shape, land.dtype)],
                  in_specs=[HBM_SPEC, HBM_SPEC, SEM_SPEC, SEM_SPEC, ANY_SPEC], out_specs=[HBM_SPEC, HBM_SPEC],
                  input_output_aliases={0: 0, 1: 1}, compiler_params=pltpu.CompilerParams(has_side_effects=EFFECT))(
                      blk, land, send, recv, after)[1]


def _rs_sum(name, gs, lands):
    n = len(gs)

    def body(*refs):
        g_refs, land_refs, out_refs = refs[:n], refs[n:2 * n], refs[2 * n:3 * n]
        recvs = refs[3 * n:4 * n]
        local_sems, sib_send, sib_recv = refs[4 * n:]
        x, y, c = _place()
        me = 4 * x + 2 * y + c
        cps = []
        for wi in range(n):
            hr = g_refs[wi].shape[1] // 2
            own = g_refs[wi].at[2 * x + y, pl.ds(pl.multiple_of(c * hr, 16), hr), :]
            cps.append(pltpu.make_async_copy(own, recvs[wi].at[me], local_sems.at[8 * wi + 7]))
            for k, (tx, ty, tc) in enumerate(_peers(x, y, c)):
                slot = 4 * tx + 2 * ty + tc
                cps.append(pltpu.make_async_copy(land_refs[wi].at[slot], recvs[wi].at[slot], local_sems.at[8 * wi + k]))
        for cp in cps:
            cp.start()
        sibs = []
        for wi in range(n):
            hr = g_refs[wi].shape[1] // 2
            ch = _pick(hr, 64, 16)
            for cp in cps[8 * wi:8 * wi + 8]:
                cp.wait()
            base = pl.multiple_of(c * hr, 16)
            for r0 in range(0, hr, ch):
                acc = recvs[wi][0, r0:r0 + ch, :].astype(F32)
                for k in range(1, N_DEV):
                    acc = acc + recvs[wi][k, r0:r0 + ch, :].astype(F32)
                out_refs[wi][pl.ds(base + r0, ch), :] = acc
            half = out_refs[wi].at[pl.ds(base, hr), :]
            sib = pltpu.make_async_remote_copy(src_ref=half, dst_ref=half, send_sem=sib_send.at[wi], recv_sem=sib_recv.at[wi],
                                               device_id=(x, y, 1 - c), device_id_type=MESH)
            sib.start()
            sibs.append(sib)
        for wi in range(n):
            hr = g_refs[wi].shape[1] // 2
            other = out_refs[wi].at[pl.ds(pl.multiple_of((1 - c) * hr, 16), hr), :]
            pltpu.make_async_remote_copy(src_ref=other, dst_ref=other, send_sem=sib_send.at[wi], recv_sem=sib_recv.at[wi],
                                         device_id=(x, y, 1 - c), device_id_type=MESH).wait_recv()
            sibs[wi].wait_send()

    res = _pcall(
        body, name=name, out_shape=[jax.ShapeDtypeStruct(g.shape[1:], F32) for g in gs],
        in_specs=[ANY_SPEC] * (2 * n), out_specs=[pl.BlockSpec(memory_space=pltpu.VMEM)] * n,
        scratch_shapes=[pltpu.VMEM((N_DEV, g.shape[1] // 2, g.shape[2]), BF16) for g in gs]
        + [pltpu.SemaphoreType.DMA((8 * n,)), pltpu.SemaphoreType.DMA((n,)), pltpu.SemaphoreType.DMA((n,))],
        compiler_params=pltpu.CompilerParams(vmem_limit_bytes=VMEM_LIMIT))(*gs, *lands)
    return list(res)


def _gate_norm_epilogue(factor, x_in, gt, nxt, t, d):
    blk = (ROWS, d)
    full = lambda i, j: (i, j)
    rowv = lambda i, j: (0, j)

    def epi(acc, res, scale, *norm):
        x_new = res + (factor * scale) * acc
        if not norm:
            return x_new, acc
        gv, scv, shv = norm
        r = lax.rsqrt(jnp.mean(x_new * x_new, axis=-1, keepdims=True) + EPS)
        return x_new, acc, x_new * r * gv * (1.0 + scv) + shv

    ins = [(x_in, blk, full), (gt, (1, d), rowv)] + [(v, (1, d), rowv) for v in (nxt or ())]
    outs = [((t, d), F32, blk, full), ((t, d), BF16, blk, full)] + ([((t, d), BF16, blk, full)] if nxt else [])
    return epi, ins, outs


def _mixer_tail(name, ge, w_glu, b_glu, w_up, y_pool, uin, w_out, x_in, gt, nxt, d, pw, tm=ROWS):
    t, sw = ge.shape
    tm = min(tm, t)
    ns = w_glu.shape[2]
    cb = (2 * pw) // d

    def body(ge_ref, wg_ref, b_ref, wu_ref, yp_ref, glp_ref, gls_ref, wo_ref, x_ref, gt_ref, gn_ref, sc_ref, sh_ref,
             gv_ref, sg_ref, ys_ref, mg_ref, y_ref, h_ref, xn_ref):
        gev = ge_ref[...]
        gv = jnp.concatenate([jnp.dot(gev, wg_ref[s], preferred_element_type=F32) for s in range(N_CHIPS)], axis=1)
        gv_ref[...] = gv.astype(BF16)
        g = gv + b_ref[...]
        sg = (g[:, :sw] * _sigmoid(g[:, sw:])).astype(BF16)
        sg_ref[...] = sg
        ys = jnp.concatenate([jnp.dot(sg, wu_ref[s], preferred_element_type=F32) for s in range(N_CHIPS)], axis=1)
        ys_ref[...] = ys.astype(BF16)
        mg = (_sigmoid(glp_ref[...].astype(F32)) * yp_ref[...].astype(F32) + _sigmoid(gls_ref[...].astype(F32)) * ys).astype(BF16)
        mg_ref[...] = mg
        y = jnp.dot(mg, wo_ref[...], preferred_element_type=F32)
        y_ref[...] = y.astype(BF16)
        xn = x_ref[...] + gt_ref[...] * y
        xn_ref[...] = xn
        r = lax.rsqrt(jnp.mean(xn * xn, axis=-1, keepdims=True) + EPS)
        h_ref[...] = (xn * r * gn_ref[...] * (1.0 + sc_ref[...]) + sh_ref[...]).astype(BF16)

    full3 = lambda i: (0, 0, 0)
    full2 = lambda i: (0, 0)
    rows = lambda w: pl.BlockSpec((tm, w), lambda i: (i, 0))
    rowv = pl.BlockSpec((1, d), full2)
    return _pcall(
        body, name=name, grid=(t // tm,),
        in_specs=[rows(sw), pl.BlockSpec(w_glu.shape, full3), pl.BlockSpec(b_glu.shape, full2),
                  pl.BlockSpec(w_up.shape, full3), rows(d), pl.BlockSpec((tm, d), lambda i: (i, cb)),
                  pl.BlockSpec((tm, d), lambda i: (i, cb + 1)), pl.BlockSpec(w_out.shape, full2), rows(d), rowv, rowv, rowv, rowv],
        out_specs=[rows(N_CHIPS * ns), rows(sw), rows(d), rows(d), rows(d), rows(d), rows(d)],
        out_shape=[jax.ShapeDtypeStruct((t, N_CHIPS * ns), BF16), jax.ShapeDtypeStruct((t, sw), BF16)]
        + [jax.ShapeDtypeStruct((t, d), BF16)] * 4 + [jax.ShapeDtypeStruct((t, d), F32)],
        compiler_params=_params("parallel"))(ge, w_glu, b_glu, w_up, y_pool, uin, uin, w_out, x_in, gt, *nxt)


def _glu_up_bwd(name, dy, w_up, gv, b_glu, w_glu, sg, ge, tm=ROWS, deps=()):
    t, sw = ge.shape
    tm = min(tm, t)
    nb = t // tm
    ns = w_up.shape[2]
    nt = (((1,), (1,)), ((), ()))
    tn_ = (((0,), (0,)), ((), ()))

    def body(dy_ref, wu_ref, gv_ref, b_ref, wg_ref, sg_ref, ge_ref, *rest):
        dge_ref, db_ref, dwu_ref, dwg_ref, dwu_acc, dwg_acc = rest[len(deps):]
        i = pl.program_id(0)

        @pl.when(i == 0)
        def _():
            db_ref[...] = jnp.zeros_like(db_ref)
            dwu_acc[...] = jnp.zeros_like(dwu_acc)
            dwg_acc[...] = jnp.zeros_like(dwg_acc)

        dyv = dy_ref[...]
        dsg = None
        for s in range(N_CHIPS):
            q = lax.dot_general(dyv[:, s * ns:(s + 1) * ns], wu_ref[s], nt, preferred_element_type=F32)
            dsg = q if dsg is None else dsg + q
        g = gv_ref[...].astype(F32) + b_ref[...]
        val, sgm = g[:, :sw], _sigmoid(g[:, sw:])
        dval, dgate = dsg * sgm, dsg * val * sgm * (1.0 - sgm)
        db_ref[...] += jnp.concatenate([_colsum(dval), _colsum(dgate)], axis=1)
        dgv = jnp.concatenate([dval, dgate], axis=1).astype(BF16)
        dge = None
        for s in range(N_CHIPS):
            q = lax.dot_general(dgv[:, s * ns:(s + 1) * ns], wg_ref[s], nt, preferred_element_type=F32)
            dge = q if dge is None else dge + q
        dge_ref[...] = dge.astype(BF16)
        dwu_acc[...] += lax.dot_general(sg_ref[...], dyv, tn_, preferred_element_type=F32)
        dwg_acc[...] += lax.dot_general(ge_ref[...], dgv, tn_, preferred_element_type=F32)

        @pl.when(i == nb - 1)
        def _():
            for s in range(N_CHIPS):
                dwu_ref[s] = dwu_acc[:, s * ns:(s + 1) * ns].astype(BF16)
                dwg_ref[s] = dwg_acc[:, s * ns:(s + 1) * ns].astype(BF16)

    full3 = lambda i: (0, 0, 0)
    rows = lambda w: pl.BlockSpec((tm, w), lambda i: (i, 0))
    wide = N_CHIPS * ns
    return _pcall(
        body, name=name, grid=(nb,),
        in_specs=[rows(wide), pl.BlockSpec(w_up.shape, full3), rows(wide), pl.BlockSpec(b_glu.shape, lambda i: (0, 0)),
                  pl.BlockSpec(w_glu.shape, full3), rows(sw), rows(sw)] + [ANY_SPEC] * len(deps),
        out_specs=[rows(sw), pl.BlockSpec((1, wide), lambda i: (0, 0)), pl.BlockSpec(w_up.shape, full3),
                   pl.BlockSpec(w_glu.shape, full3)],
        out_shape=[jax.ShapeDtypeStruct((t, sw), BF16), jax.ShapeDtypeStruct((1, wide), F32),
                   jax.ShapeDtypeStruct(w_up.shape, BF16), jax.ShapeDtypeStruct(w_glu.shape, BF16)],
        scratch_shapes=[pltpu.VMEM((sw, wide), F32), pltpu.VMEM((sw, wide), F32)],
        compiler_params=_params("arbitrary"))(dy, w_up, gv, b_glu, w_glu, sg, ge, *deps)


_FULL = lambda i, j: (i, j)
_ROWV = lambda i, j: (0, j)


def _gates_bwd_epilogue(y_pool, y_ssm, uin, t, d, pw):
    cb = (2 * pw) // d

    def epi(dm, yp, ys, glp, gls):
        sp, ss = _sigmoid(glp.astype(F32)), _sigmoid(gls.astype(F32))
        dgl = jnp.concatenate([dm * yp.astype(F32) * sp * (1.0 - sp), dm * ys.astype(F32) * ss * (1.0 - ss)], axis=1)
        return dm * sp, dm * ss, ("at", 2 * pw, dgl)

    ins = [(y_pool, (ROWS, d), _FULL), (y_ssm, (ROWS, d), _FULL),
           (uin, (ROWS, d), lambda i, j: (i, cb)), (uin, (ROWS, d), lambda i, j: (i, cb + 1))]
    wide = 2 * pw + 2 * d
    return epi, ins, [((t, d), BF16, (ROWS, d), _FULL)] * 2 + [((t, wide), BF16, (ROWS, wide), _FULL)]


def _loss_epilogue(x_in, gt, tgt, g_final, t, d):
    blk = (ROWS, d)
    full = lambda i, j: (i, j)
    rowv = lambda i, j: (0, j)

    def epi(acc, res, scale, tv, gv):
        xv = res + (0.5 * scale) * acc
        r = lax.rsqrt(jnp.mean(xv * xv, axis=-1, keepdims=True) + EPS)
        xr = xv * r
        e = xr * gv - tv
        loss_row = 0.5 * jnp.mean(e * e, axis=-1, keepdims=True)
        dout = e * (1.0 / d)
        gd = gv * dout
        dx = r * (gd - xr * jnp.mean(gd * xr, axis=-1, keepdims=True))
        fdx = 0.5 * dx
        return [dx, scale * fdx], [_colsum(dout * xr), _colsum(loss_row * jnp.ones((1, 128), F32)), _colsum(fdx * acc)]

    ins = [(x_in, blk, full), (gt, (1, d), rowv), (tgt, blk, full), (g_final, (1, d), rowv)]
    return epi, ins, [((t, d), F32, blk, full), ((t, d), BF16, blk, full)], [d, 128, d]


def _norm_bwd_epilogue(x, dres, g, sc, up, t, d):
    blk = (ROWS, d)
    full = lambda i, j: (i, j)
    rowv = lambda i, j: (0, j)

    def epi(dhv, xv, drv, gv, scv, *rest):
        r = lax.rsqrt(jnp.mean(xv * xv, axis=-1, keepdims=True) + EPS)
        xr = xv * r
        dn = dhv * (1.0 + scv)
        gd = gv * dn
        dx = drv + r * (gd - xr * jnp.mean(gd * xr, axis=-1, keepdims=True))
        outs, reds = [dx], [_colsum(dhv), _colsum(dhv * xr * gv), _colsum(dn * xr)]
        if up:
            yv, gtv = rest
            fdx = up[2] * dx
            outs.append(gtv * fdx)
            reds.append(_colsum(fdx * yv.astype(F32)))
        return outs, reds

    ins = [(x, blk, full), (dres, blk, full), (g, (1, d), rowv), (sc, (1, d), rowv)]
    ins += [(up[0], blk, full), (up[1], (1, d), rowv)] if up else []
    outs = [((t, d), F32, blk, full)] + ([((t, d), BF16, blk, full)] if up else [])
    return epi, ins, outs, [d] * (4 if up else 3)


def _ffn_in_act(name, h, w_in, tm=ROWS):
    t, d = h.shape
    ns = w_in.shape[2]
    tm = _pick(t, tm, 8)
    w4 = w_in.reshape(2, 2, d, ns)

    def body(h_ref, w_ref, ab_ref, act_ref):
        hv = h_ref[...]
        a = jnp.dot(hv, w_ref[0], preferred_element_type=F32)
        b = jnp.dot(hv, w_ref[1], preferred_element_type=F32)
        ab_ref[0] = a.astype(BF16)
        ab_ref[1] = b.astype(BF16)
        act_ref[...] = (a * _sigmoid(a) * b).astype(BF16)

    return _pcall(body, name=name, grid=(2, t // tm),
                  in_specs=[pl.BlockSpec((tm, d), lambda c, i: (i, 0)), pl.BlockSpec((2, None, d, ns), lambda c, i: (0, c, 0, 0))],
                  out_specs=[pl.BlockSpec((2, tm, ns), lambda c, i: (0, i, c)), pl.BlockSpec((tm, ns), lambda c, i: (i, c))],
                  out_shape=[jax.ShapeDtypeStruct((2, t, 2 * ns), BF16), jax.ShapeDtypeStruct((t, 2 * ns), BF16)],
                  compiler_params=_params("parallel", "parallel"))(h, w4)


def _dswiglu_epilogue(ab, t, f, tn):
    blk = (2, ROWS, tn)
    idx = lambda i, j: (0, i, j)

    def epi(dact, abv):
        a, b = abv[0].astype(F32), abv[1].astype(F32)
        s = _sigmoid(a)
        return (jnp.stack([dact * b * (s * (1.0 + a * (1.0 - s))), dact * (a * s)]),)

    return epi, [(ab, blk, idx)], [((2, t, f), BF16, blk, idx)]


def _ffn_fwd(tag, x, h, gt, get_w_in, get_w_out, nxt=None, loss=None):
    t, d = x.shape
    ab, act = _ffn_in_act(tag + "_in", h, get_w_in(h), tm=1024)
    if loss:
        epi, epi_ins, epi_outs, epi_reds = _loss_epilogue(x, gt, *loss, t, d)
        res = _mm(tag + "_out", act, get_w_out(act), tm=ROWS, tn=d, epi=epi, epi_ins=epi_ins, epi_outs=epi_outs,
                  epi_reds=epi_reds, b_resident=True)
        return res, (x, h, ab, act, None)
    epi, epi_ins, epi_outs = _gate_norm_epilogue(0.5, x, gt, nxt, t, d)
    res = _mm(tag + "_out", act, get_w_out(act), tm=ROWS, tn=d, epi=epi, epi_ins=epi_ins, epi_outs=epi_outs, b_resident=True)
    return res[0], res[2], (x, h, ab, act, res[1])


def _ffn_bwd(tag, dx_new, dy, saved, g, sc, w_in, w_out, start_rs, up, deps=()):
    x, h, ab, act, _ = saved
    d = x.shape[1]
    f = act.shape[1]
    dw_out = _mm(tag + "_dwout", act, dy, ta=True, tm=1408, tn=d, tk=2048, deps=deps)
    tok = start_rs("out", dw_out.reshape(N_CHIPS, f // N_CHIPS, d))
    epi, epi_ins, epi_outs = _dswiglu_epilogue(ab, x.shape[0], f, w_in.shape[2])
    dab = _mm(tag + "_dact", dy, w_out, tb=True, tm=ROWS, tn=w_in.shape[2], epi=epi, epi_ins=epi_ins, epi_outs=epi_outs,
              deps=(tok,), j_outer=True)
    dw_in = _mm(tag + "_dwin", h, dab, ta=True, out_stacked=True, b_halves=True, tm=ROWS, tn=1408, tk=4096, j_outer=True)
    tok = start_rs("in", dw_in)
    epi, epi_ins, epi_outs, epi_reds = _norm_bwd_epilogue(x, dx_new, g, sc, up, x.shape[0], d)
    res = _mm(tag + "_dh", dab, w_in, tb=True, b_stacked=True, a_halves=True, tn=d, tk=5632, deps=(tok,), epi=epi,
              epi_ins=epi_ins, epi_outs=epi_outs, epi_reds=epi_reds, b_resident=True)
    if up:
        return res
    return res[0], None, res[1], res[2], res[3], None


def _row(v):
    return v.reshape(1, -1)


def _pack(parts):
    cols = []
    for p in parts:
        flat = p.reshape(-1).astype(F32)
        padn = (-flat.shape[0]) % 128
        cols.append(jnp.pad(flat, (0, padn)) if padn else flat)
    flat = jnp.concatenate(cols)
    padn = (-flat.shape[0]) % 1024
    if padn:
        flat = jnp.pad(flat, (0, padn))
    return flat.reshape(-1, 128)


def _unpack(packed, shapes):
    flat = packed.reshape(-1)
    out, off = [], 0
    for s in shapes:
        n = math.prod(s)
        out.append(flat[off:off + n].reshape(s))
        off += n + ((-n) % 128)
    return out


SMALL = ['b_ada', 'g_ffn1', 'g_mix', 'pool_w', 'pool_b', 'pool_scale', 'ssm_lam_re_log', 'ssm_lam_im', 'ssm_log_dt',
         'ssm_b_re', 'ssm_b_im', 'ssm_c_re', 'ssm_c_im', 'ssm_d', 'b_glu', 'g_ffn2', 'g_final']
BIG = ['w_ffn1_in', 'w_ffn1_out', 'w_in', 'w_pool_up', 'w_glu', 'w_ssm_up', 'w_out', 'w_ffn2_in', 'w_ffn2_out']
AG_GROUPS = [[0], [1], [2, 3, 4, 5, 6], [7, 8]]
AG_DIRECT = [False, True, False, True]
SMALL_LATE = ['b_ada', 'g_ffn1']
RS_SUM_GROUPS = [[0, 1, 2, 3], [4, 5]]
WEIGHTS = ['w_ada', 'b_ada', 'g_ffn1', 'w_ffn1_in', 'w_ffn1_out', 'g_mix', 'w_in', 'pool_w', 'pool_b', 'pool_scale', 'w_pool_up',
           'ssm_lam_re_log', 'ssm_lam_im', 'ssm_log_dt', 'ssm_b_re', 'ssm_b_im', 'ssm_c_re', 'ssm_c_im', 'ssm_d', 'w_glu', 'b_glu',
           'w_ssm_up', 'w_out', 'g_ffn2', 'w_ffn2_in', 'w_ffn2_out', 'g_final']


def kernel(x, c, w_ada, b_ada, g_ffn1, w_ffn1_in, w_ffn1_out, g_mix, w_in, pool_w, pool_b, pool_scale, w_pool_up, ssm_lam_re_log, ssm_lam_im, ssm_log_dt, ssm_b_re, ssm_b_im, ssm_c_re, ssm_c_im, ssm_d, w_glu, b_glu, w_ssm_up, w_out, g_ffn2, w_ffn2_in, w_ffn2_out, g_final, loss_target, m_w_ada, m_b_ada, m_g_ffn1, m_w_ffn1_in, m_w_ffn1_out, m_g_mix, m_w_in, m_pool_w, m_pool_b, m_pool_scale, m_w_pool_up, m_ssm_lam_re_log, m_ssm_lam_im, m_ssm_log_dt, m_ssm_b_re, m_ssm_b_im, m_ssm_c_re, m_ssm_c_im, m_ssm_d, m_w_glu, m_b_glu, m_w_ssm_up, m_w_out, m_g_ffn2, m_w_ffn2_in, m_w_ffn2_out, m_g_final, v_w_ada, v_b_ada, v_g_ffn1, v_w_ffn1_in, v_w_ffn1_out, v_g_mix, v_w_in, v_pool_w, v_pool_b, v_pool_scale, v_w_pool_up, v_ssm_lam_re_log, v_ssm_lam_im, v_ssm_log_dt, v_ssm_b_re, v_ssm_b_im, v_ssm_c_re, v_ssm_c_im, v_ssm_d, v_w_glu, v_b_glu, v_w_ssm_up, v_w_out, v_g_ffn2, v_w_ffn2_in, v_w_ffn2_out, v_g_final):
    args = dict(locals())
    wt = {n: args[n] for n in WEIGHTS}
    mom = {n: args["m_" + n] for n in WEIGHTS}
    var = {n: args["v_" + n] for n in WEIGHTS}

    t, d = x.shape[1], x.shape[2]
    pw = pool_b.shape[1]
    sw = ssm_d.shape[1]
    ngrp = sw // SSM_GROUP
    gn = ngrp * SSM_STATE
    xi, yi, ci = _place()
    b_me = 4 * xi + 2 * yi + ci
    s_me = 2 * xi + yi
    x2d = x[0]
    tgt = loss_target[0]

    c_all = _allgather_small("ag_c", c.reshape(8, d // 8)).reshape(N_DEV, d)
    ncol = w_ada.shape[2]
    b_sh = lax.dynamic_slice(b_ada, (0, s_me * ncol), (1, ncol))
    mod_sh = _ada_fwd("ada_fwd", c_all, w_ada[0], b_sh)
    md_send, md_recv, mod_sh, md_land, tok = _bcast_start("mod_start", mod_sh)

    shards = [wt[n][0].astype(BF16) for n in BIG]
    lands = _place_shards(shards, s_me)
    n0 = len(AG_GROUPS[0])
    sems_a, shards_a, lands_a, tok = _ag_start("ag_start_first", shards[:n0], lands[:n0], AG_GROUPS[:1], AG_DIRECT[:1], deps=(tok,))
    rest = [[w - n0 for w in grp] for grp in AG_GROUPS[1:]]
    sems_b, shards_b, lands_b, tok = _ag_start("ag_start_rest", shards[n0:], lands[n0:], rest, AG_DIRECT[1:], deps=(tok,))
    ag_sems, shards_t, lands_t = sems_a + sems_b, shards_a + shards_b, lands_a + lands_b
    md_land = _bcast_wait("mod_wait", mod_sh, md_land, md_send, md_recv, tok)
    mod_all = lax.dynamic_update_slice(md_land, mod_sh[None], (b_me, 0, 0))
    full = {}

    def weights(gi, *after):
        grp = AG_GROUPS[gi]
        if BIG[grp[0]] not in full:
            ls = _ag_wait("ag_wait%d" % gi, [shards_t[w] for w in grp], [lands_t[w] for w in grp], *ag_sems[gi],
                          AG_DIRECT[gi], after)
            for w, l in zip(grp, ls if AG_DIRECT[gi] else _ag_forward("ag_fwd%d" % gi, ls)):
                full[BIG[w]] = l
        return full

    mod_me = jnp.concatenate([lax.dynamic_slice(mod_all, (2 * s, b_me, 0), (1, 1, ncol))[0] for s in range(N_CHIPS)], axis=1)
    mod = mod_me.reshape(9, d)
    sh1, sc1, gt1, sh2, sc2, gt2, sh3, sc3, gt3 = [mod[k:k + 1] for k in range(9)]

    f = w_ffn1_out.shape[1] * N_CHIPS

    col = lambda a: a.reshape(gn, 1)
    lrl_c, li_c = col(ssm_lam_re_log), col(ssm_lam_im)
    ldt_c = col(jnp.broadcast_to(ssm_log_dt.reshape(ngrp, 1), (ngrp, SSM_STATE)))
    b_re2, b_im2 = ssm_b_re.reshape(gn, SSM_GROUP), ssm_b_im.reshape(gn, SSM_GROUP)
    lrdt_c, ang_c, bb_re, bb_im = _ssm_prep("ssm_prep", lrl_c, li_c, ldt_c, b_re2, b_im2)
    lrdt, ang = lrdt_c.reshape(1, gn), ang_c.reshape(1, gn)
    tr = lambda a: jnp.swapaxes(a, 1, 2)
    bbd = [_blockdiag_b(v, sw).astype(BF16) for v in (bb_re, bb_im)]
    ccd = [_blockdiag_c(v[0], sw).astype(BF16) for v in (ssm_c_re, ssm_c_im)]
    bbd_t, ccd_t = [tr(v) for v in bbd], [tr(v) for v in ccd]
    early = [n for n in SMALL if n not in SMALL_LATE]
    packs = {}
    for tag, names, extra in (("early", early, []), ("late", SMALL_LATE, [jnp.zeros((128,), F32)])):
        packs[tag] = [_pack([src[n] for n in names] + extra) for src in (wt, mom, var)]
    shadow = [lrdt, ang, ldt_c, *bbd, *ccd, *bbd_t, *ccd_t, *packs["early"], *packs["late"]]

    h1 = _norm_fwd("ffn1_norm", x2d, g_ffn1, sc1, sh1, deps=(tok,))
    x1, h2, sav1 = _ffn_fwd("ffn1", x2d, h1, gt1, lambda after: weights(0, after, *shadow)['w_ffn1_in'],
                            lambda after: weights(1, after)['w_ffn1_out'].reshape(f, d), (g_mix, sc2, sh2))
    weights(2, h2)
    wo = full['w_out'].reshape(d, d)
    uin = _mm("mix_in", h2, full['w_in'], b_stacked=True, tm=2048, tn=768, j_outer=True)
    p_pool, z_pool, y_pool = _pool_fwd("pool_fwd", uin, pool_w[0], pool_b, pool_scale, full['w_pool_up'], pw)
    y_s, ge, s_re, s_im = _ssm_fwd("ssm_fwd", uin, lrdt, ang, *bbd, *ccd, ssm_d, sw)
    gv, sg, y_ssm, merged, y2, h3, x2 = _mixer_tail("mixer_tail", ge, full['w_glu'], b_glu, full['w_ssm_up'], y_pool, uin, wo,
                                                    x1, gt2, (g_ffn2, sc3, sh3), d, pw)

    (dx3, dy3, dg_final, loss_v, dgt3), sav3 = _ffn_fwd(
        "ffn2", x2, h3, gt3, lambda after: weights(3, after)['w_ffn2_in'],
        lambda after: weights(3, after)['w_ffn2_out'].reshape(f, d), loss=(tgt, _row(g_final)))

    gs = {}
    rs_open = []

    def start_rs(names, gbs):
        send, recv, g_thru, land_thru, token = _rs_start("rs_start_" + names[0], gbs)
        rs_open.append((names, send, recv, g_thru, land_thru))
        return token

    dx2, dy2, dsh3, dsc3, gs['g_ffn2'], dgt2 = _ffn_bwd(
        "ffn2b", dx3, dy3, sav3, g_ffn2, sc3, full['w_ffn2_in'], full['w_ffn2_out'].reshape(f, d),
        lambda key, g: start_rs(['w_ffn2_' + key], [g]), (y2, gt2, 1.0))

    epi, epi_ins, epi_outs = _gates_bwd_epilogue(y_pool, y_ssm, uin, t, d, pw)
    dy_pool, dy_ssm, duin, g_wo = _mm("mix_dmerged", dy2, wo, tb=True, tm=ROWS, tn=d, epi=epi, epi_ins=epi_ins,
                                      epi_outs=epi_outs, b_resident=True, wgrad=merged, wgrad_rows=True)
    g_wo = g_wo.reshape(N_CHIPS, d // N_CHIPS, d)

    duin, gs['pool_w'], gs['pool_b'], gs['pool_scale'], g_wpu = _pool_bwd(
        "pool_bwd", dy_pool, z_pool, p_pool, pool_w[0], pool_b, pool_scale, full['w_pool_up'], duin)

    tok = start_rs(['w_out', 'w_pool_up'], [g_wo, g_wpu])
    dge, gs['b_glu'], g_wsu, g_wglu = _glu_up_bwd("glu_up_bwd", dy_ssm, full['w_ssm_up'], gv, b_glu, full['w_glu'], sg, ge,
                                                  deps=(tok,))
    (duin, g_abre, g_abim, g_bbd_re, g_bbd_im, g_ccd_re, g_ccd_im, gs['ssm_d']) = _ssm_bwd(
        "ssm_bwd", dge, y_s, uin, s_re, s_im, lrdt, ang, *bbd_t, *ccd_t, ssm_d, sw, duin)
    gs['ssm_c_re'], gs['ssm_c_im'] = _diag_of_c(g_ccd_re, sw), _diag_of_c(g_ccd_im, sw)
    d_lrl, d_li, d_ldt, d_bre, d_bim = _ssm_param_bwd(
        "ssm_param_bwd", lrl_c, li_c, ldt_c, b_re2, b_im2, g_abre.reshape(gn, 1), g_abim.reshape(gn, 1),
        _diag_of_b(g_bbd_re, sw), _diag_of_b(g_bbd_im, sw))
    gs['ssm_lam_re_log'], gs['ssm_lam_im'] = d_lrl, d_li
    gs['ssm_log_dt'] = jnp.sum(d_ldt.reshape(ngrp, SSM_STATE), axis=1)
    gs['ssm_b_re'], gs['ssm_b_im'] = d_bre, d_bim

    g_win =_mm("mix_dwin", h2, duin, ta=True, out_stacked=True, tm=d, tn=768, tk=4096)
    tok = start_rs(['w_ssm_up', 'w_glu', 'w_in'], [g_wsu, g_wglu, g_win])
    epi, epi_ins, epi_outs, epi_reds = _norm_bwd_epilogue(x1, dx2, g_mix, sc2, (sav1[4], gt1, 0.5), t, d)
    dx1, dy1, dsh2, dsc2, gs['g_mix'], dgt1 = _mm(
        "mix_dh", duin, full['w_in'], tb=True, b_stacked=True, tn=d, tk=3072, deps=(tok,), epi=epi, epi_ins=epi_ins,
        epi_outs=epi_outs, epi_reds=epi_reds, b_resident=True)

    gs['g_final'] = dg_final
    sg_blk = _pack([gs[n] for n in early])
    sg_send, sg_recv, sg_blk, sg_land, tok = _bcast_start("sg_start", sg_blk)

    dx0, _, dsh1, dsc1, gs['g_ffn1'], _ = _ffn_bwd(
        "ffn1b", dx1, dy1, sav1, g_ffn1, sc1, full['w_ffn1_in'], full['w_ffn1_out'].reshape(f, d),
        lambda key, g: start_rs(['w_ffn1_' + key], [g]), None, deps=(tok,))

    gs['b_ada'] = jnp.concatenate([dsh1, dsc1, dgt1, dsh2, dsc2, dgt2, dsh3, dsc3, dgt3], axis=1)
    lt_blk = _pack([gs[n] for n in SMALL_LATE] + [loss_v])
    lt_send, lt_recv, lt_blk, lt_land, tok = _bcast_start("late_start", lt_blk)

    grads, delta, new_m, new_v = {}, {}, {}, {}

    def small_update(tag, names, blk, land):
        g8 = lax.dynamic_update_slice(land, blk[None], (b_me, 0, 0)).reshape(-1, 128)
        w_pack, m_pack, v_pack = packs[tag]
        shapes = [wt[n].shape for n in names]
        per_param, packed = _adamw_small("adamw_small_" + tag, w_pack, g8, m_pack, v_pack, shapes)
        for k, dst in enumerate((grads, delta, new_m, new_v)):
            rest = _unpack(packed[k], shapes)
            for i, n in enumerate(names):
                dst[n] = per_param[i][k] if per_param[i] is not None else rest[i]
        return g8

    after = tok
    for group in RS_SUM_GROUPS:
        names, g_done, land_done = [], [], []
        for k in group:
            nk, send, recv, g_thru, land_thru = rs_open[k]
            gd, ld = _rs_wait("rs_wait_" + nk[0], g_thru, land_thru, send, recv, after)
            names, g_done, land_done = names + nk, g_done + gd, land_done + ld
        for n, g_sum in zip(names, _rs_sum("rs_sum_" + names[0], g_done, land_done)):
            g_out, dl, mn, vn = _adamw("adamw_" + n, wt[n][0], g_sum, mom[n][0], var[n][0])
            grads[n], delta[n], new_m[n], new_v[n] = g_out[None], dl[None], mn[None], vn[None]
            after = dl
        if group is RS_SUM_GROUPS[-2]:
            small_update("early", early, sg_blk, _bcast_wait("sg_wait", sg_blk, sg_land, sg_send, sg_recv, after))

    lt_land = _bcast_wait("late_wait", lt_blk, lt_land, lt_send, lt_recv, after)
    late8 = small_update("late", SMALL_LATE, lt_blk, lt_land).reshape(N_DEV, -1)
    loss = jnp.sum(late8[:, 10 * d])
    dmod_sh = lax.dynamic_slice(late8, (0, s_me * ncol), (N_DEV, ncol))
    g_w_ada, dl, mn, vn = _ada_bwd_adamw("ada_bwd_adamw", c_all, dmod_sh, w_ada[0], m_w_ada[0], v_w_ada[0])
    grads['w_ada'], delta['w_ada'], new_m['w_ada'], new_v['w_ada'] = g_w_ada[None], dl[None], mn[None], vn[None]

    return (loss, dx0[None], *[grads[n] for n in WEIGHTS], *[delta[n] for n in WEIGHTS],
            *[new_m[n] for n in WEIGHTS], *[new_v[n] for n in WEIGHTS])
```

```python
import functools
import math

import jax
import jax.numpy as jnp
from jax import lax
from jax.experimental import pallas as pl
from jax.experimental.pallas import tpu as pltpu

F32 = jnp.float32
BF16 = jnp.bfloat16
MESH = pl.DeviceIdType.MESH

EPS = 1e-6
POOL_WINDOWS = (2, 4, 8, 16)
POOL_HALO = 16
SSM_GROUP = 16
SSM_STATE = 64
SSM_BLOCKS = 4
N_DEV = 8
N_CHIPS = 4
ADAM_LR = 0.001
ADAM_B1 = 0.9
ADAM_B2 = 0.999
ADAM_EPS = 1e-08
ADAM_WD = 0.01
ADAM_STEP = 10
VMEM_LIMIT = 56 * 1024 * 1024
ROWS = 512


ANY_SPEC = pl.BlockSpec(memory_space=pl.ANY)
HBM_SPEC = pl.BlockSpec(memory_space=pltpu.HBM)
SEM_SPEC = pl.BlockSpec(memory_space=pltpu.SEMAPHORE)
EFFECT = pltpu.SideEffectType.DATAFLOW_SIDE_EFFECTING


def _hbm(a):
    return pltpu.with_memory_space_constraint(a, pltpu.HBM)


def _pcall(body, **kw):
    return pl.pallas_call(body, **kw)


def _params(*sem):
    return pltpu.CompilerParams(dimension_semantics=sem, vmem_limit_bytes=VMEM_LIMIT)


def _pick(n, cap, mult=128):
    if n <= cap:
        return n
    best = None
    for d in range(mult, cap + 1, mult):
        if n % d == 0:
            best = d
    assert best is not None, (n, cap, mult)
    return best


def _sigmoid(v):
    return 1.0 / (1.0 + jnp.exp(-v))


def _rowwise(name, fn, ins, params, outs, reds, tm, deps=()):
    t = ins[0][0].shape[0]
    tm = min(tm, t)
    nb = t // tm
    ni, npar, no, nd = len(ins), len(params), len(outs), len(deps)

    def body(*refs):
        iv = [r[...] for r in refs[:ni]]
        pv = [r[...] for r in refs[ni:ni + npar]]
        o_refs = refs[ni + npar + nd:ni + npar + nd + no]
        r_refs = refs[ni + npar + nd + no:]
        ovals, rvals = fn(iv, pv)
        for o_ref, val in zip(o_refs, ovals):
            off = 0
            if isinstance(val, tuple) and val[0] == "at":
                _, off, val = val
            parts = val if isinstance(val, (list, tuple)) else [val]
            for p in parts:
                o_ref[:, off:off + p.shape[1]] = p.astype(o_ref.dtype)
                off += p.shape[1]
        if r_refs:
            @pl.when(pl.program_id(0) == 0)
            def _():
                for r in r_refs:
                    r[...] = jnp.zeros_like(r)
            for r, val in zip(r_refs, rvals):
                r[...] += val

    in_specs = [pl.BlockSpec((tm, w), functools.partial(lambda i, cb: (i, cb), cb=cb)) for (_, w, cb) in ins]
    in_specs += [pl.BlockSpec(p.shape, lambda i: (0, 0)) for p in params]
    in_specs += [ANY_SPEC] * nd
    out_shape = [jax.ShapeDtypeStruct((t, w), dt) for (w, dt) in outs]
    out_shape += [jax.ShapeDtypeStruct((1, w), F32) for w in reds]
    out_specs = [pl.BlockSpec((tm, w), lambda i: (i, 0)) for (w, _) in outs]
    out_specs += [pl.BlockSpec((1, w), lambda i: (0, 0)) for w in reds]
    res = _pcall(body, name=name, grid=(nb,), in_specs=in_specs, out_specs=out_specs, out_shape=out_shape,
                 compiler_params=_params("arbitrary"))(*[a for a, _, _ in ins], *params, *deps)
    return res


def _colsum(v):
    return jnp.sum(v, axis=0, keepdims=True)


def _mm(name, a, b, *, ta=False, tb=False, b_stacked=False, out_stacked=False, tm=ROWS, tn=1024, tk=2816,
        out_dtype=BF16, epi=None, epi_ins=(), epi_outs=None, epi_reds=(), deps=(), j_outer=False, a_halves=False,
        b_halves=False, b_resident=False, wgrad=None, wgrad_rows=False):
    if a_halves:
        _, m, kdim = a.shape
        kdim *= 2
    elif ta:
        kdim, m = a.shape
    else:
        m, kdim = a.shape
    ns = None
    if b_stacked:
        ns = b.shape[2]
        n = b.shape[1] if tb else N_CHIPS * ns
        assert kdim == (N_CHIPS * ns if tb else b.shape[1]), (name, a.shape, b.shape)
    else:
        if b_halves:
            n = 2 * b.shape[2]
            assert kdim == b.shape[1] and not tb, (name, a.shape, b.shape)
        else:
            n = b.shape[0] if tb else b.shape[1]
            assert kdim == (b.shape[1] if tb else b.shape[0]), (name, a.shape, b.shape)
        if out_stacked:
            ns = n // N_CHIPS

    def shards(want):
        return max(g for g in (1, 2, 4) if g * ns <= max(want, ns))

    tm = _pick(m, tm, 128 if ta else 8)
    gn = gk = 1
    if (b_stacked and not tb) or out_stacked:
        gn = shards(min(tn, n // 2) if b_halves else tn)
        tn = gn * ns
    else:
        tn = _pick(n, tn)
    if b_stacked and tb:
        gk = shards(tk)
        tk = gk * ns
    else:
        tk = _pick(kdim, tk, 8 if ta else 128)
    nm, nn, nk = m // tm, n // tn, kdim // tk

    def ij(f):
        return (lambda g0, g1, k: f(g1, g0, k)) if j_outer else f

    if a_halves:
        assert b_stacked and tb and gk == N_CHIPS and nk == 1, name
        a_spec = pl.BlockSpec((2, tm, kdim // 2), ij(lambda i, j, k: (0, i, 0)))
    elif ta:
        a_spec = pl.BlockSpec((tk, tm), ij(lambda i, j, k: (k, i)))
    else:
        a_spec = pl.BlockSpec((tm, tk), ij(lambda i, j, k: (i, k)))
    if b_stacked and not tb:
        b_spec = pl.BlockSpec((gn, tk, ns), ij(lambda i, j, k: (j, k, 0)))
    elif b_stacked and tb:
        b_spec = pl.BlockSpec((gk, tn, ns), ij(lambda i, j, k: (k, j, 0)))
    elif b_halves:
        bph = (n // 2) // tn
        b_spec = pl.BlockSpec((None, tk, tn), ij(lambda i, j, k: (j // bph, k, j % bph)))
    elif tb:
        b_spec = pl.BlockSpec((tn, tk), ij(lambda i, j, k: (j, k)))
    else:
        b_spec = pl.BlockSpec((tk, tn), ij(lambda i, j, k: (k, j)))
    dims = (((0 if ta else 1,), (1 if tb else 0,)), ((), ()))

    if epi_outs is None:
        if out_stacked:
            epi_outs = [((N_CHIPS, m, ns), out_dtype, (gn, tm, ns), lambda i, j: (j, i, 0))]
        else:
            epi_outs = [((m, n), out_dtype, (tm, tn), lambda i, j: (i, j))]
    ne, no, nd, nr = len(epi_ins), len(epi_outs), len(deps), len(epi_reds)
    nx = 0 if wgrad is None else 1
    assert not (nr or nx) or (nn == 1 and not j_outer), name
    assert not nx or (nk == 1 and not ta and not a_halves and kdim % N_CHIPS == 0), name

    def body(a_ref, b_ref, *rest):
        e_refs = rest[:ne]
        x_refs = rest[ne:ne + nx]
        o_refs = rest[ne + nx + nd:ne + nx + nd + no]
        r_refs = rest[ne + nx + nd + no:ne + nx + nd + no + nr]
        wg_refs = rest[ne + nx + nd + no + nr:ne + nx + nd + no + nr + nx]
        scratch = rest[ne + nx + nd + no + nr + nx:]
        av = None if a_halves else a_ref[...].astype(BF16)
        if nx:
            wg_acc = scratch[-1]
            i = pl.program_id(0)
            pw_ = lax.dot_general(x_refs[0][...].astype(BF16), av, (((0,), (0,)), ((), ())), preferred_element_type=F32)

            @pl.when(i == 0)
            def _():
                wg_acc[...] = pw_

            @pl.when(i > 0)
            def _():
                wg_acc[...] += pw_

            @pl.when(i == nm - 1)
            def _():
                if wgrad_rows:
                    wg_refs[0][...] = wg_acc[...].astype(BF16)
                else:
                    ks = kdim // N_CHIPS
                    for s in range(N_CHIPS):
                        wg_refs[0][s] = wg_acc[:, s * ks:(s + 1) * ks].astype(BF16)
        if b_stacked and not tb:
            parts = [lax.dot_general(av, b_ref[s].astype(BF16), dims, preferred_element_type=F32) for s in range(gn)]
        elif b_stacked and tb:
            p = None
            for s in range(gk):
                if a_halves:
                    a_s = a_ref[s // 2, :, (s % 2) * ns:(s % 2 + 1) * ns].astype(BF16)
                else:
                    a_s = av[:, s * ns:(s + 1) * ns]
                q = lax.dot_general(a_s, b_ref[s].astype(BF16), dims, preferred_element_type=F32)
                p = q if p is None else p + q
            parts = [p]
        else:
            parts = [lax.dot_general(av, b_ref[...].astype(BF16), dims, preferred_element_type=F32)]

        def finish(acc_parts):
            if epi is None and out_stacked:
                acc = acc_parts[0]
                for s in range(gn):
                    o_refs[0][s] = acc[:, s * ns:(s + 1) * ns].astype(out_dtype)
            elif epi is None:
                w = acc_parts[0].shape[1]
                for s, part in enumerate(acc_parts):
                    o_refs[0][:, s * w:(s + 1) * w] = part.astype(out_dtype)
            else:
                acc = acc_parts[0] if len(acc_parts) == 1 else jnp.concatenate(acc_parts, axis=1)
                vals = epi(acc, *[r[...] for r in e_refs])
                if nr:
                    vals, reds = vals

                    @pl.when(pl.program_id(0) == 0)
                    def _():
                        for r in r_refs:
                            r[...] = jnp.zeros_like(r)
                    for r, val in zip(r_refs, reds):
                        r[...] += val
                for o_ref, val in zip(o_refs, vals):
                    if isinstance(val, tuple) and val[0] == "at":
                        o_ref[:, val[1]:val[1] + val[2].shape[1]] = val[2].astype(o_ref.dtype)
                    else:
                        o_ref[...] = val.astype(o_ref.dtype)

        if nk == 1:
            finish(parts)
        else:
            acc_ref = scratch[0]
            k = pl.program_id(2)
            w = parts[0].shape[1]

            @pl.when(k == 0)
            def _():
                for s, part in enumerate(parts):
                    acc_ref[:, s * w:(s + 1) * w] = part

            @pl.when(k > 0)
            def _():
                for s, part in enumerate(parts):
                    acc_ref[:, s * w:(s + 1) * w] += part

            @pl.when(k == nk - 1)
            def _():
                finish([acc_ref[...]])

    def _ij(f):
        return ij(lambda i, j, k: f(i, j))

    if b_resident:
        assert nk == 1 and (nn == 1 or j_outer), name
        b_spec = pl.BlockSpec(b_spec.block_shape, b_spec.index_map, pipeline_mode=pl.Buffered(1))
    in_specs = [a_spec, b_spec] + [pl.BlockSpec(blk, _ij(f)) for (_, blk, f) in epi_ins]
    if nx:
        in_specs.append(pl.BlockSpec((tm, wgrad.shape[1]), lambda i, j, k: (i, 0)))
    in_specs += [ANY_SPEC] * nd
    out_specs = [pl.BlockSpec(blk, _ij(f)) for (_, _, blk, f) in epi_outs]
    out_specs += [pl.BlockSpec((1, w), lambda *_: (0, 0)) for w in epi_reds]
    out_shape = [jax.ShapeDtypeStruct(s, dt) for (s, dt, _, _) in epi_outs] + [jax.ShapeDtypeStruct((1, w), F32) for w in epi_reds]
    scratch = [pltpu.VMEM((tm, tn), F32)] if nk > 1 else []
    if nx:
        wg_shape = (wgrad.shape[1], kdim) if wgrad_rows else (N_CHIPS, wgrad.shape[1], kdim // N_CHIPS)
        out_specs.append(pl.BlockSpec(wg_shape, lambda *_: (0,) * len(wg_shape)))
        out_shape.append(jax.ShapeDtypeStruct(wg_shape, BF16))
        scratch.append(pltpu.VMEM((wgrad.shape[1], kdim), F32))
    grid = (nn, nm, nk) if j_outer else (nm, nn, nk)
    sem = ("arbitrary",) * 3 if (nr or nx) else ("parallel", "parallel", "arbitrary")
    res = _pcall(body, name=name, grid=grid, in_specs=in_specs, out_specs=out_specs, out_shape=out_shape, scratch_shapes=scratch,
                 compiler_params=_params(*sem))(a, b, *[x for x, _, _ in epi_ins], *([wgrad] * nx), *deps)
    return res[0] if len(res) == 1 else res


def _norm_fwd(name, x, g, sc, sh, deps=()):
    d = x.shape[1]

    def fn(iv, pv):
        (xv,), (gv, scv, shv) = iv, pv
        r = lax.rsqrt(jnp.mean(xv * xv, axis=-1, keepdims=True) + EPS)
        return [xv * r * gv * (1.0 + scv) + shv], []

    return _rowwise(name, fn, [(x, d, 0)], [g, sc, sh], [(d, BF16)], [], ROWS, deps=deps)[0]


def _adamw(name, w, g, m, v, tm=256):
    c = w.shape[1]

    def fn(iv, pv):
        wv, gv, mv, vv = iv
        mn = ADAM_B1 * mv + (1.0 - ADAM_B1) * gv
        vn = ADAM_B2 * vv + (1.0 - ADAM_B2) * (gv * gv)
        m_hat = mn / (1.0 - ADAM_B1 ** ADAM_STEP)
        v_hat = vn / (1.0 - ADAM_B2 ** ADAM_STEP)
        delta = -ADAM_LR * (m_hat / (jnp.sqrt(v_hat) + ADAM_EPS) + ADAM_WD * wv)
        return [gv, delta, mn, vn], []

    return _rowwise(name, fn, [(w, c, 0), (g, c, 0), (m, c, 0), (v, c, 0)], [], [(c, F32)] * 4, [], _pick(w.shape[0], tm, 8))


def _unpack_plan(shape):
    n = math.prod(shape)
    if len(shape) == 2 and shape[0] == 1 and n % 128 == 0:
        return [((slice(None), slice(128 * r, 128 * (r + 1))), slice(r, r + 1), slice(None)) for r in range(n // 128)]
    if len(shape) == 2 and shape[0] == 1 and n < 128:
        return [((slice(None), slice(None)), slice(0, 1), slice(0, n))]
    if len(shape) == 1 and n % 128 == 0:
        return [((slice(128 * r, 128 * (r + 1)),), r, slice(None)) for r in range(n // 128)]
    if len(shape) == 3 and shape[0] == 1 and shape[2] == 64:
        return [((0, slice(2 * r + h, 2 * r + h + 1), slice(None)), slice(r, r + 1), slice(64 * h, 64 * (h + 1)))
                for r in range(n // 128) for h in range(2)]
    if len(shape) == 4 and shape[0] == 1 and shape[2:] == (128, 128):
        return [((0, k), slice(128 * k, 128 * (k + 1)), slice(None)) for k in range(shape[1])]
    return None


def _adamw_small(name, w, g8, m, v, shapes):
    r = w.shape[0]
    plans, rows0, off = [], [], 0
    for s in shapes:
        plans.append(_unpack_plan(s))
        rows0.append(off // 128)
        off += math.prod(s) + (-math.prod(s)) % 128
    direct = [i for i, p in enumerate(plans) if p is not None]

    def body(w_ref, g_ref, m_ref, v_ref, *out):
        packed = out[4 * len(direct):]
        gv = g_ref[0:r, :]
        for k in range(1, N_DEV):
            gv = gv + g_ref[k * r:(k + 1) * r, :]
        mn = ADAM_B1 * m_ref[...] + (1.0 - ADAM_B1) * gv
        vn = ADAM_B2 * v_ref[...] + (1.0 - ADAM_B2) * (gv * gv)
        m_hat = mn / (1.0 - ADAM_B1 ** ADAM_STEP)
        v_hat = vn / (1.0 - ADAM_B2 ** ADAM_STEP)
        packed[0][...] = gv
        packed[1][...] = -ADAM_LR * (m_hat / (jnp.sqrt(v_hat) + ADAM_EPS) + ADAM_WD * w_ref[...])
        packed[2][...] = mn
        packed[3][...] = vn
        for di, i in enumerate(direct):
            for kind in range(4):
                o_ref, src = out[4 * di + kind], packed[kind]
                for o_idx, row, lanes in plans[i]:
                    row = (rows0[i] + row) if isinstance(row, int) else slice(rows0[i] + row.start, rows0[i] + row.stop)
                    o_ref[o_idx] = src[row, lanes]

    out_shape = [jax.ShapeDtypeStruct(shapes[i], F32) for i in direct for _ in range(4)]
    out_shape += [jax.ShapeDtypeStruct((r, 128), F32)] * 4
    res = _pcall(body, name=name, out_shape=out_shape,
                 compiler_params=pltpu.CompilerParams(vmem_limit_bytes=VMEM_LIMIT))(w, g8, m, v)
    per_param = [None] * len(shapes)
    for di, i in enumerate(direct):
        per_param[i] = res[4 * di:4 * di + 4]
    return per_param, res[4 * len(direct):]


def _pool_fwd(name, uin, pool_w, pool_b, pool_scale, w_up, pw, tm=ROWS):
    t = uin.shape[0]
    tm = min(tm, t)
    ng = len(POOL_WINDOWS)
    gw = pw // ng
    ns = w_up.shape[2]

    def body(u_ref, w_ref, b_ref, s_ref, wu_ref, p_ref, z_ref, y_ref, ext):
        i = pl.program_id(0)

        @pl.when(i == 0)
        def _():
            ext[0:POOL_HALO, :] = jnp.zeros((POOL_HALO, pw), F32)

        u = u_ref[...].astype(F32)
        ext[POOL_HALO:POOL_HALO + tm, :] = u
        pos = i * tm + lax.broadcasted_iota(jnp.int32, (tm, 1), 0)
        for k, win in enumerate(POOL_WINDOWS):
            cols = slice(k * gw, (k + 1) * gw)
            acc = u[:, cols]
            for j in range(1, win):
                acc = acc + ext[POOL_HALO - j:POOL_HALO - j + tm, cols]
            cnt = jnp.minimum(pos + 1, win).astype(F32)
            z = acc / cnt - u[:, cols]
            zp = jnp.dot(z.astype(BF16), w_ref[k].astype(BF16), preferred_element_type=F32) + b_ref[:, cols]
            p_ref[:, cols] = (zp * s_ref[:, cols]).astype(BF16)
            z_ref[:, cols] = z.astype(BF16)
        ext[0:POOL_HALO, :] = u[tm - POOL_HALO:tm, :]
        pv = p_ref[...]
        for s in range(N_CHIPS):
            y_ref[:, s * ns:(s + 1) * ns] = jnp.dot(pv, wu_ref[s], preferred_element_type=F32).astype(BF16)

    full3 = lambda i: (0, 0, 0)
    return _pcall(
        body, name=name, grid=(t // tm,),
        in_specs=[pl.BlockSpec((tm, pw), lambda i: (i, 0)), pl.BlockSpec(pool_w.shape, full3),
                  pl.BlockSpec(pool_b.shape, lambda i: (0, 0)), pl.BlockSpec(pool_scale.shape, lambda i: (0, 0)),
                  pl.BlockSpec(w_up.shape, full3)],
        out_specs=[pl.BlockSpec((tm, pw), lambda i: (i, 0))] * 2 + [pl.BlockSpec((tm, N_CHIPS * ns), lambda i: (i, 0))],
        out_shape=[jax.ShapeDtypeStruct((t, pw), BF16)] * 2 + [jax.ShapeDtypeStruct((t, N_CHIPS * ns), BF16)],
        scratch_shapes=[pltpu.VMEM((POOL_HALO + tm, pw), F32)],
        compiler_params=_params("arbitrary"))(uin, pool_w, pool_b, pool_scale, w_up)


def _pool_bwd(name, dy, z, p, pool_w, pool_b, pool_scale, w_up, duin, tm=ROWS):
    t, pw = z.shape
    tm = min(tm, t)
    nb = t // tm
    ng = len(POOL_WINDOWS)
    gw = pw // ng
    ns = w_up.shape[2]

    def body(dy_ref, z_ref, p_ref, w_ref, b_ref, s_ref, wu_ref, duin_ref, du_ref, dw_ref, db_ref, ds_ref, dwu_ref, ext, dwu_acc):
        i = pl.program_id(0)

        @pl.when(i == 0)
        def _():
            ext[tm:tm + POOL_HALO, :] = jnp.zeros((POOL_HALO, pw), F32)
            dw_ref[...] = jnp.zeros_like(dw_ref)
            db_ref[...] = jnp.zeros_like(db_ref)
            ds_ref[...] = jnp.zeros_like(ds_ref)
            dwu_acc[...] = jnp.zeros_like(dwu_acc)

        dwu_acc[...] += lax.dot_general(p_ref[...], dy_ref[...], (((0,), (0,)), ((), ())), preferred_element_type=F32)

        @pl.when(i == nb - 1)
        def _():
            for s in range(N_CHIPS):
                dwu_ref[s] = dwu_acc[:, s * ns:(s + 1) * ns].astype(BF16)

        pos = (nb - 1 - i) * tm + lax.broadcasted_iota(jnp.int32, (tm, 1), 0)
        dp = None
        for s in range(N_CHIPS):
            q = lax.dot_general(dy_ref[:, s * ns:(s + 1) * ns], wu_ref[s], (((1,), (1,)), ((), ())), preferred_element_type=F32)
            dp = q if dp is None else dp + q
        for k, win in enumerate(POOL_WINDOWS):
            cols = slice(k * gw, (k + 1) * gw)
            zk = z_ref[:, cols]
            dpk = dp[:, cols]
            wk = w_ref[k].astype(BF16)
            zp = jnp.dot(zk, wk, preferred_element_type=F32) + b_ref[:, cols]
            ds_ref[:, cols] += _colsum(dpk * zp)
            dzp = dpk * s_ref[:, cols]
            db_ref[:, cols] += _colsum(dzp)
            dzpb = dzp.astype(BF16)
            dz = lax.dot_general(dzpb, wk, (((1,), (1,)), ((), ())), preferred_element_type=F32)
            dw_ref[k] += lax.dot_general(zk, dzpb, (((0,), (0,)), ((), ())), preferred_element_type=F32)
            cnt = jnp.minimum(pos + 1, win).astype(F32)
            r = dz / cnt
            ext[0:tm, cols] = r
            acc = r - dz
            for j in range(1, win):
                acc = acc + ext[j:j + tm, cols]
            du_ref[:, cols] = acc.astype(BF16)
        ext[tm:tm + POOL_HALO, :] = ext[0:POOL_HALO, :]

    rev = lambda i: (nb - 1 - i, 0)
    full3 = lambda i: (0, 0, 0)
    return _pcall(
        body, name=name, grid=(nb,),
        in_specs=[pl.BlockSpec((tm, N_CHIPS * ns), rev), pl.BlockSpec((tm, pw), rev), pl.BlockSpec((tm, pw), rev),
                  pl.BlockSpec(pool_w.shape, full3), pl.BlockSpec(pool_b.shape, lambda i: (0, 0)),
                  pl.BlockSpec(pool_scale.shape, lambda i: (0, 0)), pl.BlockSpec(w_up.shape, full3), ANY_SPEC],
        out_specs=[pl.BlockSpec((tm, pw), rev), pl.BlockSpec(pool_w.shape, full3),
                   pl.BlockSpec((1, pw), lambda i: (0, 0)), pl.BlockSpec((1, pw), lambda i: (0, 0)), pl.BlockSpec(w_up.shape, full3)],
        out_shape=[jax.ShapeDtypeStruct(duin.shape, BF16), jax.ShapeDtypeStruct(pool_w.shape, F32),
                   jax.ShapeDtypeStruct((1, pw), F32), jax.ShapeDtypeStruct((1, pw), F32), jax.ShapeDtypeStruct(w_up.shape, BF16)],
        scratch_shapes=[pltpu.VMEM((tm + POOL_HALO, pw), F32), pltpu.VMEM((pw, N_CHIPS * ns), F32)], input_output_aliases={7: 0},
        compiler_params=_params("arbitrary"))(dy, z, p, pool_w, pool_b, pool_scale, w_up, duin)


def _ssm_disc(lrl, li, ldt):
    lr = -jnp.exp(lrl)
    dt = jnp.exp(ldt)
    mag = jnp.exp(lr * dt)
    ang = li * dt
    ab_re = mag * jnp.cos(ang)
    ab_im = mag * jnp.sin(ang)
    num_re = ab_re - 1.0
    num_im = ab_im
    den = lr * lr + li * li
    f_re = (num_re * lr + num_im * li) / den
    f_im = (num_im * lr - num_re * li) / den
    return lr, dt, mag, ang, ab_re, ab_im, num_re, num_im, den, f_re, f_im


def _ssm_prep(name, lrl, li, ldt, b_re, b_im):
    gn, h = b_re.shape

    def body(lrl_ref, li_ref, ldt_ref, br_ref, bi_ref, lrdt_ref, ang_ref, bbr_ref, bbi_ref):
        lr, dt, _, ang, _, _, _, _, _, f_re, f_im = _ssm_disc(lrl_ref[...], li_ref[...], ldt_ref[...])
        lrdt_ref[...] = lr * dt
        ang_ref[...] = ang
        br, bi = br_ref[...], bi_ref[...]
        bbr_ref[...] = f_re * br - f_im * bi
        bbi_ref[...] = f_re * bi + f_im * br

    col = jax.ShapeDtypeStruct((gn, 1), F32)
    mat = jax.ShapeDtypeStruct((gn, h), F32)
    return _pcall(body, name=name, out_shape=[col, col, mat, mat])(lrl, li, ldt, b_re, b_im)


def _ssm_param_bwd(name, lrl, li, ldt, b_re, b_im, g_abre, g_abim, g_bbre, g_bbim):
    gn, h = b_re.shape

    def body(lrl_ref, li_ref, ldt_ref, br_ref, bi_ref, gar_ref, gai_ref, gbr_ref, gbi_ref,
             dlrl_ref, dli_ref, dldt_ref, dbr_ref, dbi_ref):
        li_v = li_ref[...]
        lr, dt, mag, ang, ab_re, ab_im, num_re, num_im, den, f_re, f_im = _ssm_disc(lrl_ref[...], li_v, ldt_ref[...])
        br, bi = br_ref[...], bi_ref[...]
        gbr, gbi = gbr_ref[...], gbi_ref[...]
        g_fre = jnp.sum(gbr * br + gbi * bi, axis=1, keepdims=True)
        g_fim = jnp.sum(gbi * br - gbr * bi, axis=1, keepdims=True)
        dbr_ref[...] = gbr * f_re + gbi * f_im
        dbi_ref[...] = gbi * f_re - gbr * f_im
        g_num_re = (g_fre * lr - g_fim * li_v) / den
        g_num_im = (g_fre * li_v + g_fim * lr) / den
        g_den = -(g_fre * f_re + g_fim * f_im) / den
        g_lr = (g_fre * num_re + g_fim * num_im) / den + g_den * 2.0 * lr
        g_li = (g_fre * num_im - g_fim * num_re) / den + g_den * 2.0 * li_v
        g_are = gar_ref[...] + g_num_re
        g_aim = gai_ref[...] + g_num_im
        g_mag = g_are * jnp.cos(ang) + g_aim * jnp.sin(ang)
        g_ang = g_aim * ab_re - g_are * ab_im
        g_lrdt = g_mag * mag
        g_lr = g_lr + g_lrdt * dt
        g_dt = g_lrdt * lr + g_ang * li_v
        g_li = g_li + g_ang * dt
        dlrl_ref[...] = g_lr * lr
        dli_ref[...] = g_li
        dldt_ref[...] = g_dt * dt

    col = jax.ShapeDtypeStruct((gn, 1), F32)
    mat = jax.ShapeDtypeStruct((gn, h), F32)
    return _pcall(body, name=name, out_shape=[col, col, col, mat, mat])(lrl, li, ldt, b_re, b_im, g_abre, g_abim, g_bbre, g_bbim)


def _pow_rows(lrdt, ang, k):
    mag = jnp.exp(k * lrdt)
    return mag * jnp.cos(k * ang), mag * jnp.sin(k * ang)


def _ssm_chunk(t):
    return 256 if t >= 2048 else 128


def _to_segments(dst, srcs, jn):
    for q, src in enumerate(srcs):
        for j in range(jn):
            dst[8 * j:8 * j + 8, 128 * q:128 * (q + 1)] = src[pl.ds(j, 8, stride=jn), :]


def _from_segments(dst, src, q, jn, dtype):
    for s in range(8):
        dst[s * jn:(s + 1) * jn, 128 * q:128 * (q + 1)] = src[q, pl.ds(s, jn, stride=8), :].astype(dtype)


def _fill_rows8(dst_re, dst_im, v_re, v_im):
    for j in range(v_re.shape[0]):
        dst_re[8 * j:8 * j + 8, :] = jnp.broadcast_to(v_re[j:j + 1, :], (8, v_re.shape[1]))
        dst_im[8 * j:8 * j + 8, :] = jnp.broadcast_to(v_im[j:j + 1, :], (8, v_im.shape[1]))


def _cmul(ar, ai, br, bi):
    return ar * br - ai * bi, ar * bi + ai * br


def _cmul_conj(ar, ai, br, bi):
    return ar * br + ai * bi, ar * bi - ai * br


def _ssm_fwd(name, uin, lrdt, ang, bb_re, bb_im, cc_re, cc_im, d_skip, sw):
    t = uin.shape[0]
    gn = lrdt.shape[1]
    lc = _ssm_chunk(t)
    jn = lc // 8
    ub, sb = sw // SSM_BLOCKS, gn // SSM_BLOCKS
    nq = sw // 128
    assert ub == 128 and nq == SSM_BLOCKS

    def body(u_ref, lrdt_ref, ang_ref, bbr_ref, bbi_ref, ccr_ref, cci_ref, d_ref, y_ref, ge_ref, sre_ref, sim_ref,
             p_re, p_im, a_re, a_im, up, yp, cst_re, cst_im, car_re, car_im):
        i = pl.program_id(0)
        lrdt_v, ang_v = lrdt_ref[...], ang_ref[...]

        @pl.when(i == 0)
        def _():
            k = (lax.broadcasted_iota(jnp.int32, (jn, 1), 0) + 1).astype(F32)
            _fill_rows8(p_re, p_im, *_pow_rows(lrdt_v, ang_v, k))
            car_re[...] = jnp.zeros_like(car_re)
            car_im[...] = jnp.zeros_like(car_im)

        for q in range(nq):
            yp[q] = u_ref[:, 128 * q:128 * (q + 1)].astype(F32)
        _to_segments(up, [yp.at[q] for q in range(nq)], jn)
        u = up[...]
        ubf = u.astype(BF16)
        for q in range(SSM_BLOCKS):
            uq = ubf[:, q * ub:(q + 1) * ub]
            a_re[:, q * sb:(q + 1) * sb] = jnp.dot(uq, bbr_ref[q], preferred_element_type=F32)
            a_im[:, q * sb:(q + 1) * sb] = jnp.dot(uq, bbi_ref[q], preferred_element_type=F32)
        a1r, a1i = _pow_rows(lrdt_v, ang_v, 1.0)
        ajr, aji = _pow_rows(lrdt_v, ang_v, float(jn))
        for q in range(0, SSM_BLOCKS, 2):
            cols = slice(q * sb, (q + 2) * sb)
            ar8 = jnp.broadcast_to(a1r[:, cols], (8, 2 * sb))
            ai8 = jnp.broadcast_to(a1i[:, cols], (8, 2 * sb))

            def step(j, carry, cols=cols, ar8=ar8, ai8=ai8):
                sr, si = carry
                rows = pl.ds(pl.multiple_of(j * 8, 8), 8)
                mr, mi = _cmul(ar8, ai8, sr, si)
                nr, ni = mr + a_re[rows, cols], mi + a_im[rows, cols]
                a_re[rows, cols] = nr
                a_im[rows, cols] = ni
                return nr, ni

            lax.fori_loop(1, jn, step, (a_re[0:8, cols], a_im[0:8, cols]), unroll=8)
        er, ei = a_re[lc - 8:lc, :], a_im[lc - 8:lc, :]
        gr, gi = car_re[...], car_im[...]
        for s in range(8):
            cst_re[s:s + 1, :] = gr
            cst_im[s:s + 1, :] = gi
            mr, mi = _cmul(ajr, aji, gr, gi)
            gr, gi = mr + er[s:s + 1, :], mi + ei[s:s + 1, :]
        car_re[...] = gr
        car_im[...] = gi
        for q in range(SSM_BLOCKS):
            cols = slice(q * sb, (q + 1) * sb)
            cr = jnp.tile(cst_re[:, cols], (jn, 1))
            ci = jnp.tile(cst_im[:, cols], (jn, 1))
            mr, mi = _cmul(p_re[:, cols], p_im[:, cols], cr, ci)
            srb, sib = (a_re[:, cols] + mr).astype(BF16), (a_im[:, cols] + mi).astype(BF16)
            sre_ref[:, cols] = srb
            sim_ref[:, cols] = sib
            ycols = slice(q * ub, (q + 1) * ub)
            y = (jnp.dot(srb, ccr_ref[q], preferred_element_type=F32) - jnp.dot(sib, cci_ref[q], preferred_element_type=F32)
                 + d_ref[:, ycols] * u[:, ycols])
            yp[q] = y
            _from_segments(y_ref, yp, q, jn, F32)
            yt = y_ref[:, ycols]
            ge_ref[:, ycols] = (0.5 * yt * (1.0 + lax.erf(yt * (1.0 / math.sqrt(2.0))))).astype(BF16)

    row = lambda i: (0, 0)
    blk3 = lambda i: (0, 0, 0)
    return _pcall(
        body, name=name, grid=(t // lc,),
        in_specs=[pl.BlockSpec((lc, sw), lambda i: (i, 1)), pl.BlockSpec((1, gn), row), pl.BlockSpec((1, gn), row),
                  pl.BlockSpec(bb_re.shape, blk3), pl.BlockSpec(bb_im.shape, blk3),
                  pl.BlockSpec(cc_re.shape, blk3), pl.BlockSpec(cc_im.shape, blk3), pl.BlockSpec((1, sw), row)],
        out_specs=[pl.BlockSpec((lc, sw), lambda i: (i, 0)), pl.BlockSpec((lc, sw), lambda i: (i, 0)),
                   pl.BlockSpec((lc, gn), lambda i: (i, 0)), pl.BlockSpec((lc, gn), lambda i: (i, 0))],
        out_shape=[jax.ShapeDtypeStruct((t, sw), F32), jax.ShapeDtypeStruct((t, sw), BF16),
                   jax.ShapeDtypeStruct((t, gn), BF16), jax.ShapeDtypeStruct((t, gn), BF16)],
        scratch_shapes=[pltpu.VMEM((lc, gn), F32), pltpu.VMEM((lc, gn), F32), pltpu.VMEM((lc, gn), F32), pltpu.VMEM((lc, gn), F32),
                        pltpu.VMEM((lc, sw), F32), pltpu.VMEM((nq, lc, 128), F32),
                        pltpu.VMEM((8, gn), F32), pltpu.VMEM((8, gn), F32), pltpu.VMEM((1, gn), F32), pltpu.VMEM((1, gn), F32)],
        compiler_params=_params("arbitrary"))(uin, lrdt, ang, bb_re, bb_im, cc_re, cc_im, d_skip)


def _ssm_bwd(name, dge, y, uin, s_re, s_im, lrdt, ang, bbt_re, bbt_im, cct_re, cct_im, d_skip, sw, duin):
    t = uin.shape[0]
    gn = lrdt.shape[1]
    lc = _ssm_chunk(t)
    nb = t // lc
    jn = lc // 8
    ub, sb = sw // SSM_BLOCKS, gn // SSM_BLOCKS
    nq = sw // 128
    tail = 16

    def body(dge_ref, y_ref, u_ref, sre_ref, sim_ref, tre_ref, tim_ref, lrdt_ref, ang_ref, btr_ref, bti_ref, ctr_ref, cti_ref,
             d_ref, duin_ref, du_ref, dar_ref, dai_ref, dbr_ref, dbi_ref, dcr_ref, dci_ref, dd_ref,
             q_re, q_im, a_re, a_im, dyp, up, dys, us, dup, cst_re, cst_im, sp_re, sp_im, car_re, car_im):
        i = pl.program_id(0)
        lrdt_v, ang_v = lrdt_ref[...], ang_ref[...]

        @pl.when(i == 0)
        def _():
            k = (jn - lax.broadcasted_iota(jnp.int32, (jn, 1), 0)).astype(F32)
            _fill_rows8(q_re, q_im, *_pow_rows(lrdt_v, ang_v, k))
            car_re[...] = jnp.zeros_like(car_re)
            car_im[...] = jnp.zeros_like(car_im)
            for r in (dar_ref, dai_ref, dbr_ref, dbi_ref, dcr_ref, dci_ref, dd_ref):
                r[...] = jnp.zeros_like(r)

        yv = y_ref[...]
        ut = u_ref[...].astype(F32)
        cdf =0.5 * (1.0 + lax.erf(yv * (1.0 / math.sqrt(2.0))))
        pdf = jnp.exp(-0.5 * yv * yv) * (1.0 / math.sqrt(2.0 * math.pi))
        dyt = dge_ref[...].astype(F32) * (cdf + yv * pdf)
        dd_ref[...] += _colsum(dyt * ut)
        for q in range(nq):
            dys[q] = dyt[:, 128 * q:128 * (q + 1)]
            us[q] = ut[:, 128 * q:128 * (q + 1)]
        _to_segments(dyp, [dys.at[q] for q in range(nq)], jn)
        _to_segments(up, [us.at[q] for q in range(nq)], jn)
        dy = dyp[...]
        u = up[...]
        dyb = dy.astype(BF16)
        ubf = u.astype(BF16)
        for q in range(SSM_BLOCKS):
            dq = dyb[:, q * ub:(q + 1) * ub]
            a_re[:, q * sb:(q + 1) * sb] = jnp.dot(dq, ctr_ref[q], preferred_element_type=F32)
            a_im[:, q * sb:(q + 1) * sb] = -jnp.dot(dq, cti_ref[q], preferred_element_type=F32)
        a1r, a1i = _pow_rows(lrdt_v, ang_v, 1.0)
        ajr, aji = _pow_rows(lrdt_v, ang_v, float(jn))
        for q in range(0, SSM_BLOCKS, 2):
            cols = slice(q * sb, (q + 2) * sb)
            ar8 = jnp.broadcast_to(a1r[:, cols], (8, 2 * sb))
            ai8 = jnp.broadcast_to(a1i[:, cols], (8, 2 * sb))

            def step(jj, carry, cols=cols, ar8=ar8, ai8=ai8):
                sr, si = carry
                rows = pl.ds(pl.multiple_of((jn - 2 - jj) * 8, 8), 8)
                mr, mi = _cmul_conj(ar8, ai8, sr, si)
                nr, ni = mr + a_re[rows, cols], mi + a_im[rows, cols]
                a_re[rows, cols] = nr
                a_im[rows, cols] = ni
                return nr, ni

            lax.fori_loop(0, jn - 1, step, (a_re[lc - 8:lc, cols], a_im[lc - 8:lc, cols]), unroll=8)
        er, ei = a_re[0:8, :], a_im[0:8, :]
        hr, hi = car_re[...], car_im[...]
        for s in range(7, -1, -1):
            cst_re[s:s + 1, :] = hr
            cst_im[s:s + 1, :] = hi
            mr, mi = _cmul_conj(ajr, aji, hr, hi)
            hr, hi = mr + er[s:s + 1, :], mi + ei[s:s + 1, :]
        car_re[...] = hr
        car_im[...] = hi
        first = (i == nb - 1).astype(F32)
        sp_re[0:tail, :] = tre_ref[...].astype(F32) * (1.0 - first)
        sp_im[0:tail, :] = tim_ref[...].astype(F32) * (1.0 - first)
        sp_re[tail:tail + 8, :] = sre_ref[lc - tail:lc, :].astype(F32)[tail - 8:tail]
        sp_im[tail:tail + 8, :] = sim_ref[lc - tail:lc, :].astype(F32)[tail - 8:tail]
        tn_dims = (((0,), (0,)), ((), ()))
        for q in range(SSM_BLOCKS):
            cols = slice(q * sb, (q + 1) * sb)
            ycols = slice(q * ub, (q + 1) * ub)
            cr = jnp.tile(cst_re[:, cols], (jn, 1))
            ci = jnp.tile(cst_im[:, cols], (jn, 1))
            mr, mi = _cmul_conj(q_re[:, cols], q_im[:, cols], cr, ci)
            lam_r, lam_i = a_re[:, cols] + mr, a_im[:, cols] + mi
            s_r, s_i = sre_ref[:, cols], sim_ref[:, cols]
            p0r, p0i = sp_re[tail - 1:tail + 7, cols], sp_im[tail - 1:tail + 7, cols]
            l0r, l0i, l1r, l1i = lam_r[0:8], lam_i[0:8], lam_r[8:lc], lam_i[8:lc]
            pvr, pvi = s_r.astype(F32)[0:lc - 8], s_i.astype(F32)[0:lc - 8]
            dar_ref[:, cols] += _colsum(l1r * pvr + l1i * pvi) + _colsum(l0r * p0r + l0i * p0i)
            dai_ref[:, cols] += _colsum(l1i * pvr - l1r * pvi) + _colsum(l0i * p0r - l0r * p0i)
            lrb, lib = lam_r.astype(BF16), lam_i.astype(BF16)
            dup[q] = (jnp.dot(lrb, btr_ref[q], preferred_element_type=F32)
                      + jnp.dot(lib, bti_ref[q], preferred_element_type=F32) + d_ref[:, ycols] * dy[:, ycols])
            _from_segments(du_ref, dup, q, jn, BF16)
            uq = ubf[:, ycols]
            dbr_ref[q] += lax.dot_general(uq, lrb, tn_dims, preferred_element_type=F32)
            dbi_ref[q] += lax.dot_general(uq, lib, tn_dims, preferred_element_type=F32)
            dq = dyb[:, ycols]
            dcr_ref[q] += lax.dot_general(s_r.astype(BF16), dq, tn_dims, preferred_element_type=F32)
            dci_ref[q] -= lax.dot_general(s_i.astype(BF16), dq, tn_dims, preferred_element_type=F32)

    rev = lambda i: (nb - 1 - i, 0)
    tailmap = lambda i: (jnp.maximum((nb - 1 - i) * (lc // tail) - 1, 0), 0)
    row = lambda i: (0, 0)
    blk3 = lambda i: (0, 0, 0)
    return _pcall(
        body, name=name, grid=(nb,),
        in_specs=[pl.BlockSpec((lc, sw), rev), pl.BlockSpec((lc, sw), rev), pl.BlockSpec((lc, sw), lambda i: (nb - 1 - i, 1)),
                  pl.BlockSpec((lc, gn), rev), pl.BlockSpec((lc, gn), rev),
                  pl.BlockSpec((tail, gn), tailmap), pl.BlockSpec((tail, gn), tailmap),
                  pl.BlockSpec((1, gn), row), pl.BlockSpec((1, gn), row),
                  pl.BlockSpec(bbt_re.shape, blk3), pl.BlockSpec(bbt_im.shape, blk3),
                  pl.BlockSpec(cct_re.shape, blk3), pl.BlockSpec(cct_im.shape, blk3), pl.BlockSpec((1, sw), row), ANY_SPEC],
        out_specs=[pl.BlockSpec((lc, sw), lambda i: (nb - 1 - i, 1)), pl.BlockSpec((1, gn), row), pl.BlockSpec((1, gn), row),
                   pl.BlockSpec((SSM_BLOCKS, ub, sb), blk3), pl.BlockSpec((SSM_BLOCKS, ub, sb), blk3),
                   pl.BlockSpec((SSM_BLOCKS, sb, ub), blk3), pl.BlockSpec((SSM_BLOCKS, sb, ub), blk3),
                   pl.BlockSpec((1, sw), row)],
        out_shape=[jax.ShapeDtypeStruct(duin.shape, BF16), jax.ShapeDtypeStruct((1, gn), F32), jax.ShapeDtypeStruct((1, gn), F32),
                   jax.ShapeDtypeStruct((SSM_BLOCKS, ub, sb), F32), jax.ShapeDtypeStruct((SSM_BLOCKS, ub, sb), F32),
                   jax.ShapeDtypeStruct((SSM_BLOCKS, sb, ub), F32), jax.ShapeDtypeStruct((SSM_BLOCKS, sb, ub), F32),
                   jax.ShapeDtypeStruct((1, sw), F32)],
        scratch_shapes=[pltpu.VMEM((lc, gn), F32), pltpu.VMEM((lc, gn), F32), pltpu.VMEM((lc, gn), F32), pltpu.VMEM((lc, gn), F32),
                        pltpu.VMEM((lc, sw), F32), pltpu.VMEM((lc, sw), F32),
                        pltpu.VMEM((nq, lc, 128), F32), pltpu.VMEM((nq, lc, 128), F32), pltpu.VMEM((nq, lc, 128), F32),
                        pltpu.VMEM((8, gn), F32), pltpu.VMEM((8, gn), F32),
                        pltpu.VMEM((tail + 8, gn), F32), pltpu.VMEM((tail + 8, gn), F32),
                        pltpu.VMEM((1, gn), F32), pltpu.VMEM((1, gn), F32)],
        input_output_aliases={14: 0},
        compiler_params=_params("arbitrary"))(dge, y, uin, s_re, s_im, s_re, s_im, lrdt, ang,
                                               bbt_re, bbt_im, cct_re, cct_im, d_skip, duin)


def _blockdiag_b(bb, sw):
    gpb = (sw // SSM_GROUP) // SSM_BLOCKS
    b4 = bb.reshape(SSM_BLOCKS, gpb, SSM_STATE, SSM_GROUP)
    eye = jnp.eye(gpb, dtype=bb.dtype)
    out = jnp.einsum('qgnh,gk->qghkn', b4, eye)
    return out.reshape(SSM_BLOCKS, gpb * SSM_GROUP, gpb * SSM_STATE)


def _blockdiag_c(cc, sw):
    gpb = (sw // SSM_GROUP) // SSM_BLOCKS
    c4 = cc.reshape(SSM_BLOCKS, gpb, SSM_GROUP, SSM_STATE)
    eye = jnp.eye(gpb, dtype=cc.dtype)
    out = jnp.einsum('qghn,gk->qgnkh', c4, eye)
    return out.reshape(SSM_BLOCKS, gpb * SSM_STATE, gpb * SSM_GROUP)


def _diag_of_b(dbb, sw):
    gpb = (sw // SSM_GROUP) // SSM_BLOCKS
    d5 = dbb.reshape(SSM_BLOCKS, gpb, SSM_GROUP, gpb, SSM_STATE)
    return jnp.einsum('qghgn->qgnh', d5).reshape(SSM_BLOCKS * gpb * SSM_STATE, SSM_GROUP)


def _diag_of_c(dcc, sw):
    gpb = (sw // SSM_GROUP) // SSM_BLOCKS
    d5 = dcc.reshape(SSM_BLOCKS, gpb, SSM_STATE, gpb, SSM_GROUP)
    return jnp.einsum('qgngh->qghn', d5).reshape(SSM_BLOCKS * gpb, SSM_GROUP, SSM_STATE)


def _ada_fwd(name, c_all, w_sh, b_sh):
    nb, d = c_all.shape
    ncol = w_sh.shape[1]
    tn = _pick(ncol, 768)

    def body(c_ref, w_ref, b_ref, o_ref):
        cv = c_ref[...]
        sil = cv * _sigmoid(cv)
        o_ref[...] = jnp.dot(sil, w_ref[...], preferred_element_type=F32, precision=lax.Precision.HIGHEST) + b_ref[...]

    return _pcall(body, name=name, grid=(ncol // tn,),
                  in_specs=[pl.BlockSpec((nb, d), lambda j: (0, 0)), pl.BlockSpec((d, tn), lambda j: (0, j)),
                            pl.BlockSpec((1, tn), lambda j: (0, j))],
                  out_specs=pl.BlockSpec((nb, tn), lambda j: (0, j)),
                  out_shape=jax.ShapeDtypeStruct((nb, ncol), F32), compiler_params=_params("parallel"))(c_all, w_sh, b_sh)


def _ada_bwd_adamw(name, c_all, dmod_sh, w, m, v):
    nb, d = c_all.shape
    ncol = dmod_sh.shape[1]
    tr, tn = _pick(d, 512), _pick(ncol, 768)

    def body(c_ref, g_ref, w_ref, m_ref, v_ref, go_ref, d_ref, mo_ref, vo_ref):
        cv = c_ref[...]
        sil = cv * _sigmoid(cv)
        gv = lax.dot_general(sil, g_ref[...], (((0,), (0,)), ((), ())), preferred_element_type=F32,
                             precision=lax.Precision.HIGHEST)
        mn = ADAM_B1 * m_ref[...] + (1.0 - ADAM_B1) * gv
        vn = ADAM_B2 * v_ref[...] + (1.0 - ADAM_B2) * (gv * gv)
        m_hat = mn / (1.0 - ADAM_B1 ** ADAM_STEP)
        v_hat = vn / (1.0 - ADAM_B2 ** ADAM_STEP)
        go_ref[...] = gv
        d_ref[...] = -ADAM_LR * (m_hat / (jnp.sqrt(v_hat) + ADAM_EPS) + ADAM_WD * w_ref[...])
        mo_ref[...] = mn
        vo_ref[...] = vn

    tile = pl.BlockSpec((tr, tn), lambda i, j: (i, j))
    return _pcall(body, name=name, grid=(d // tr, ncol // tn),
                  in_specs=[pl.BlockSpec((nb, tr), lambda i, j: (0, i)), pl.BlockSpec((nb, tn), lambda i, j: (0, j)), tile, tile, tile],
                  out_specs=[tile] * 4, out_shape=[jax.ShapeDtypeStruct((d, ncol), F32)] * 4,
                  compiler_params=_params("parallel", "parallel"))(c_all, dmod_sh, w, m, v)


def _place():
    return lax.axis_index("x"), lax.axis_index("y"), lax.axis_index("c")


def _allgather_small(name, blk, deps=()):
    m_per, n = blk.shape

    def body(x_ref, *rest):
        out_ref, send_sems, recv_sems, local_sem = rest[len(deps):]
        x, y, c = _place()
        me, sibling = (x, y, c), (x, y, 1 - c)
        chips = [(1 - x, y), (x, 1 - y), (1 - x, 1 - y)]

        def rows(px, py, pc):
            return out_ref.at[pl.ds((4 * px + 2 * py + pc) * m_per, m_per), :]

        def copy(k, block, to, src=None):
            return pltpu.make_async_remote_copy(
                src_ref=rows(*block) if src is None else src, dst_ref=rows(*block),
                send_sem=send_sems.at[k], recv_sem=recv_sems.at[k], device_id=to, device_id_type=MESH)

        mine = pltpu.make_async_copy(x_ref, rows(*me), local_sem)
        mine.start()
        first = [copy(0, me, sibling, src=x_ref)]
        first += [copy(1 + j, me, (*chip, c), src=x_ref) for j, chip in enumerate(chips)]
        for cp in first:
            cp.start()
        passed = [copy(4 + j, (*chip, c), sibling) for j, chip in enumerate(chips)]
        for j, chip in enumerate(chips):
            copy(1 + j, (*chip, c), me).wait_recv()
            passed[j].start()
        copy(0, sibling, me).wait_recv()
        for j, chip in enumerate(chips):
            copy(4 + j, (*chip, 1 - c), me).wait_recv()
        for cp in first + passed:
            cp.wait_send()
        mine.wait()

    return _pcall(body, name=name, out_shape=jax.ShapeDtypeStruct((N_DEV * m_per, n), blk.dtype),
                  in_specs=[pl.BlockSpec(memory_space=pltpu.VMEM)] + [ANY_SPEC] * len(deps),
                  out_specs=pl.BlockSpec(memory_space=pltpu.VMEM),
                  scratch_shapes=[pltpu.SemaphoreType.DMA((7,)), pltpu.SemaphoreType.DMA((7,)), pltpu.SemaphoreType.DMA],
                  compiler_params=pltpu.CompilerParams(vmem_limit_bytes=VMEM_LIMIT))(blk, *deps)


def _other_chips(x, y):
    return [(1 - x, y), (x, 1 - y), (1 - x, 1 - y)]


def _place_shards(shards, s_me):
    return [lax.dynamic_update_slice(lax.empty((N_CHIPS,) + s.shape, s.dtype), s[None], (s_me, 0, 0)) for s in shards]


def _ag_copy(src, land, send, recv, wi, j, chip, x, y, c, tc, both):
    hr = src.shape[0] // 2
    half = pl.ds(pl.multiple_of(c * hr, 16), hr)
    k = 3 * wi + j
    return pltpu.make_async_remote_copy(
        src_ref=src.at[half, :], dst_ref=land.at[2 * x + y, half, :],
        send_sem=send.at[2 * k + tc if both else k], recv_sem=recv.at[2 * k + c if both else k],
        device_id=(chip[0], chip[1], tc), device_id_type=MESH)


def _ag_targets(c, both):
    return (0, 1) if both else (c,)


def _ag_start(name, shards, lands, groups, direct, deps=()):
    nw, ng, nd = len(shards), len(groups), len(deps)

    def body(*refs):
        src, land = refs[:nw], refs[nw:2 * nw]
        sems = refs[2 * nw + nd:2 * nw + nd + 2 * ng]
        token = refs[-1]
        x, y, c = _place()
        for gi, grp in enumerate(groups):
            for wi, w in enumerate(grp):
                for j, chip in enumerate(_other_chips(x, y)):
                    for tc in _ag_targets(c, direct[gi]):
                        _ag_copy(src[w], land[w], sems[2 * gi], sems[2 * gi + 1], wi, j, chip, x, y, c, tc, direct[gi]).start()
        token[...] = jnp.zeros_like(token)

    sem_shapes = []
    for gi, grp in enumerate(groups):
        sem_shapes += [pltpu.SemaphoreType.DMA(((6 if direct[gi] else 3) * len(grp),))] * 2
    out_shape = sem_shapes + [pltpu.HBM(s.shape, s.dtype) for s in shards] + [pltpu.HBM(l.shape, l.dtype) for l in lands]
    out_shape += [jax.ShapeDtypeStruct((8, 128), F32)]
    res = _pcall(body, name=name, out_shape=out_shape, in_specs=[HBM_SPEC] * (2 * nw) + [ANY_SPEC] * nd,
                 out_specs=[SEM_SPEC] * (2 * ng) + [HBM_SPEC] * (2 * nw) + [pl.BlockSpec(memory_space=pltpu.VMEM)],
                 input_output_aliases={i: 2 * ng + i for i in range(2 * nw)},
                 compiler_params=pltpu.CompilerParams(has_side_effects=EFFECT))(
                     *[_hbm(s) for s in shards], *[_hbm(l) for l in lands], *deps)
    sems = [(res[2 * gi], res[2 * gi + 1]) for gi in range(ng)]
    return sems, list(res[2 * ng:2 * ng + nw]), list(res[2 * ng + nw:2 * ng + 2 * nw]), res[-1]


def _ag_wait(name, shards, lands, send, recv, both, after):
    n = len(shards)

    def body(*refs):
        src, land = refs[:n], refs[n:2 * n]
        send_sem, recv_sem = refs[2 * n], refs[2 * n + 1]
        x, y, c = _place()
        for wi in range(n):
            for j, chip in enumerate(_other_chips(x, y)):
                for tc in _ag_targets(c, both):
                    _ag_copy(src[wi], land[wi], send_sem, recv_sem, wi, j, chip, x, y, c, tc, both).wait_send()
                    _ag_copy(src[wi], land[wi], send_sem, recv_sem, wi, j, chip, chip[0], chip[1], tc, c, both).wait_recv()

    res = _pcall(body, name=name, out_shape=[pltpu.HBM(a.shape, a.dtype) for a in list(shards) + list(lands)],
                 in_specs=[HBM_SPEC] * (2 * n) + [SEM_SPEC, SEM_SPEC] + [ANY_SPEC] * len(after), out_specs=[HBM_SPEC] * (2 * n),
                 input_output_aliases={i: i for i in range(2 * n)},
                 compiler_params=pltpu.CompilerParams(has_side_effects=EFFECT))(*shards, *lands, send, recv, *after)
    return list(res[n:])


def _ag_forward(name, lands):
    n = len(lands)

    def body(*refs):
        out = refs[n:2 * n]
        send, recv = refs[2 * n], refs[2 * n + 1]
        x, y, c = _place()
        sib = (x, y, 1 - c)
        cps = []
        for wi in range(n):
            hr = out[wi].shape[1] // 2
            for j, (cx, cy) in enumerate(_other_chips(x, y)):
                got = out[wi].at[2 * cx + cy, pl.ds(pl.multiple_of(c * hr, 16), hr), :]
                cp = pltpu.make_async_remote_copy(src_ref=got, dst_ref=got, send_sem=send.at[3 * wi + j], recv_sem=recv.at[3 * wi + j],
                                                  device_id=sib, device_id_type=MESH)
                cp.start()
                cps.append(cp)
        for wi in range(n):
            hr = out[wi].shape[1] // 2
            for j, (cx, cy) in enumerate(_other_chips(x, y)):
                got = out[wi].at[2 * cx + cy, pl.ds(pl.multiple_of((1 - c) * hr, 16), hr), :]
                pltpu.make_async_remote_copy(src_ref=got, dst_ref=got, send_sem=send.at[3 * wi + j], recv_sem=recv.at[3 * wi + j],
                                             device_id=sib, device_id_type=MESH).wait_recv()
        for cp in cps:
            cp.wait_send()

    res = _pcall(body, name=name, out_shape=[jax.ShapeDtypeStruct(l.shape, l.dtype) for l in lands],
                 in_specs=[ANY_SPEC] * n, out_specs=[ANY_SPEC] * n, input_output_aliases={i: i for i in range(n)},
                 scratch_shapes=[pltpu.SemaphoreType.DMA((3 * n,)), pltpu.SemaphoreType.DMA((3 * n,))])(*lands)
    return list(res)


def _peers(x, y, c):
    offs = [(dx, dy, dc) for dx in (0, 1) for dy in (0, 1) for dc in (0, 1)][1:]
    return [(1 - x if dx else x, 1 - y if dy else y, 1 - c if dc else c) for dx, dy, dc in offs]


def _rs_copy(g_ref, land_ref, send, recv, wi, k, to, sender):
    hr = g_ref.shape[1] // 2
    return pltpu.make_async_remote_copy(
        src_ref=g_ref.at[2 * to[0] + to[1], pl.ds(pl.multiple_of(to[2] * hr, 16), hr), :], dst_ref=land_ref.at[sender],
        send_sem=send.at[7 * wi + k], recv_sem=recv.at[7 * wi + k], device_id=to, device_id_type=MESH)


def _rs_start(name, gs):
    n = len(gs)
    lands = [lax.empty((N_DEV, g.shape[1] // 2, g.shape[2]), BF16) for g in gs]

    def body(*refs):
        g, land = refs[:n], refs[n:2 * n]
        send, recv = refs[2 * n], refs[2 * n + 1]
        token = refs[-1]
        x, y, c = _place()
        me = 4 * x + 2 * y + c
        for wi in range(n):
            for k, to in enumerate(_peers(x, y, c)):
                _rs_copy(g[wi], land[wi], send, recv, wi, k, to, me).start()
        token[...] = jnp.zeros_like(token)

    out_shape = [pltpu.SemaphoreType.DMA((7 * n,))] * 2 + [pltpu.HBM(a.shape, a.dtype) for a in list(gs) + lands]
    out_shape += [jax.ShapeDtypeStruct((8, 128), F32)]
    res = _pcall(body, name=name, out_shape=out_shape, in_specs=[HBM_SPEC] * (2 * n),
                 out_specs=[SEM_SPEC] * 2 + [HBM_SPEC] * (2 * n) + [pl.BlockSpec(memory_space=pltpu.VMEM)],
                 input_output_aliases={i: 2 + i for i in range(2 * n)},
                 compiler_params=pltpu.CompilerParams(has_side_effects=EFFECT))(
                     *[_hbm(a) for a in gs], *[_hbm(a) for a in lands])
    return res[0], res[1], list(res[2:2 + n]), list(res[2 + n:2 + 2 * n]), res[-1]


def _rs_wait(name, gs, lands, send, recv, after):
    n = len(gs)

    def body(*refs):
        g, land = refs[:n], refs[n:2 * n]
        send_sem, recv_sem = refs[2 * n], refs[2 * n + 1]
        x, y, c = _place()
        me = 4 * x + 2 * y + c
        for wi in range(n):
            for k, to in enumerate(_peers(x, y, c)):
                _rs_copy(g[wi], land[wi], send_sem, recv_sem, wi, k, to, me).wait_send()
                _rs_copy(g[wi], land[wi], send_sem, recv_sem, wi, k, (x, y, c), 4 * to[0] + 2 * to[1] + to[2]).wait_recv()

    res = _pcall(body, name=name, out_shape=[pltpu.HBM(a.shape, a.dtype) for a in list(gs) + list(lands)],
                 in_specs=[HBM_SPEC] * (2 * n) + [SEM_SPEC, SEM_SPEC, ANY_SPEC], out_specs=[HBM_SPEC] * (2 * n),
                 input_output_aliases={i: i for i in range(2 * n)},
                 compiler_params=pltpu.CompilerParams(has_side_effects=EFFECT))(*gs, *lands, send, recv, after)
    return list(res[:n]), list(res[n:])


def _bc_copy(blk_ref, land_ref, send, recv, k, to, slot):
    return pltpu.make_async_remote_copy(src_ref=blk_ref, dst_ref=land_ref.at[slot], send_sem=send.at[k], recv_sem=recv.at[k],
                                        device_id=to, device_id_type=MESH)


def _bcast_start(name, blk):
    land = lax.empty((N_DEV,) + blk.shape, blk.dtype)

    def body(blk_ref, land_ref, send, recv, blk_thru, land_thru, token):
        x, y, c = _place()
        for k, to in enumerate(_peers(x, y, c)):
            _bc_copy(blk_ref, land_ref, send, recv, k, to, 4 * x + 2 * y + c).start()
        token[...] = jnp.zeros_like(token)

    return _pcall(body, name=name,
                  out_shape=[pltpu.SemaphoreType.DMA((7,)), pltpu.SemaphoreType.DMA((7,)), pltpu.HBM(blk.shape, blk.dtype),
                             pltpu.HBM(land.shape, land.dtype), jax.ShapeDtypeStruct((8, 128), F32)],
                  in_specs=[HBM_SPEC, HBM_SPEC], out_specs=[SEM_SPEC, SEM_SPEC, HBM_SPEC, HBM_SPEC, pl.BlockSpec(memory_space=pltpu.VMEM)],
                  input_output_aliases={0: 2, 1: 3}, compiler_params=pltpu.CompilerParams(has_side_effects=EFFECT))(_hbm(blk), _hbm(land))


def _bcast_wait(name, blk, land, send, recv, after):
    def body(blk_ref, land_ref, send_sem, recv_sem, after_ref, blk_thru, land_thru):
        x, y, c = _place()
        for k, to in enumerate(_peers(x, y, c)):
            _bc_copy(blk_ref, land_ref, send_sem, recv_sem, k, to, 4 * x + 2 * y + c).wait_send()
            _bc_copy(blk_ref, land_ref, send_sem, recv_sem, k, to, 4 * to[0] + 2 * to[1] + to[2]).wait_recv()

    return _pcall(body, name=name, out_shape=[pltpu.HBM(blk.shape, blk.dtype), pltpu.HBM(land.shape, land.dtype)],
                  in_specs=[HBM_SPEC, HBM_SPEC, SEM_SPEC, SEM_SPEC, ANY_SPEC], out_specs=[HBM_SPEC, HBM_SPEC],
                  input_output_aliases={0: 0, 1: 1}, compiler_params=pltpu.CompilerParams(has_side_effects=EFFECT))(
                      blk, land, send, recv, after)[1]


def _rs_sum(name, gs, lands):
    n = len(gs)

    def body(*refs):
        g_refs, land_refs, out_refs = refs[:n], refs[n:2 * n], refs[2 * n:3 * n]
        recvs = refs[3 * n:4 * n]
        local_sems, sib_send, sib_recv = refs[4 * n:]
        x, y, c = _place()
        me = 4 * x + 2 * y + c
        cps = []
        for wi in range(n):
            hr = g_refs[wi].shape[1] // 2
            own = g_refs[wi].at[2 * x + y, pl.ds(pl.multiple_of(c * hr, 16), hr), :]
            cps.append(pltpu.make_async_copy(own, recvs[wi].at[me], local_sems.at[8 * wi + 7]))
            for k, (tx, ty, tc) in enumerate(_peers(x, y, c)):
                slot = 4 * tx + 2 * ty + tc
                cps.append(pltpu.make_async_copy(land_refs[wi].at[slot], recvs[wi].at[slot], local_sems.at[8 * wi + k]))
        for cp in cps:
            cp.start()
        sibs = []
        for wi in range(n):
            hr = g_refs[wi].shape[1] // 2
            ch = _pick(hr, 64, 16)
            for cp in cps[8 * wi:8 * wi + 8]:
                cp.wait()
            base = pl.multiple_of(c * hr, 16)
            for r0 in range(0, hr, ch):
                acc = recvs[wi][0, r0:r0 + ch, :].astype(F32)
                for k in range(1, N_DEV):
                    acc = acc + recvs[wi][k, r0:r0 + ch, :].astype(F32)
                out_refs[wi][pl.ds(base + r0, ch), :] = acc
            half = out_refs[wi].at[pl.ds(base, hr), :]
            sib = pltpu.make_async_remote_copy(src_ref=half, dst_ref=half, send_sem=sib_send.at[wi], recv_sem=sib_recv.at[wi],
                                               device_id=(x, y, 1 - c), device_id_type=MESH)
            sib.start()
            sibs.append(sib)
        for wi in range(n):
            hr = g_refs[wi].shape[1] // 2
            other = out_refs[wi].at[pl.ds(pl.multiple_of((1 - c) * hr, 16), hr), :]
            pltpu.make_async_remote_copy(src_ref=other, dst_ref=other, send_sem=sib_send.at[wi], recv_sem=sib_recv.at[wi],
                                         device_id=(x, y, 1 - c), device_id_type=MESH).wait_recv()
            sibs[wi].wait_send()

    res = _pcall(
        body, name=name, out_shape=[jax.ShapeDtypeStruct(g.shape[1:], F32) for g in gs],
        in_specs=[ANY_SPEC] * (2 * n), out_specs=[pl.BlockSpec(memory_space=pltpu.VMEM)] * n,
        scratch_shapes=[pltpu.VMEM((N_DEV, g.shape[1] // 2, g.shape[2]), BF16) for g in gs]
        + [pltpu.SemaphoreType.DMA((8 * n,)), pltpu.SemaphoreType.DMA((n,)), pltpu.SemaphoreType.DMA((n,))],
        compiler_params=pltpu.CompilerParams(vmem_limit_bytes=VMEM_LIMIT))(*gs, *lands)
    return list(res)


def _gate_norm_epilogue(factor, x_in, gt, nxt, t, d):
    blk = (ROWS, d)
    full = lambda i, j: (i, j)
    rowv = lambda i, j: (0, j)

    def epi(acc, res, scale, *norm):
        x_new = res + (factor * scale) * acc
        if not norm:
            return x_new, acc
        gv, scv, shv = norm
        r = lax.rsqrt(jnp.mean(x_new * x_new, axis=-1, keepdims=True) + EPS)
        return x_new, acc, x_new * r * gv * (1.0 + scv) + shv

    ins = [(x_in, blk, full), (gt, (1, d), rowv)] + [(v, (1, d), rowv) for v in (nxt or ())]
    outs = [((t, d), F32, blk, full), ((t, d), BF16, blk, full)] + ([((t, d), BF16, blk, full)] if nxt else [])
    return epi, ins, outs


def _mixer_tail(name, ge, w_glu, b_glu, w_up, y_pool, uin, w_out, x_in, gt, nxt, d, pw, tm=ROWS):
    t, sw = ge.shape
    tm = min(tm, t)
    ns = w_glu.shape[2]
    cb = (2 * pw) // d

    def body(ge_ref, wg_ref, b_ref, wu_ref, yp_ref, glp_ref, gls_ref, wo_ref, x_ref, gt_ref, gn_ref, sc_ref, sh_ref,
             gv_ref, sg_ref, ys_ref, mg_ref, y_ref, h_ref, xn_ref):
        gev = ge_ref[...]
        gv = jnp.concatenate([jnp.dot(gev, wg_ref[s], preferred_element_type=F32) for s in range(N_CHIPS)], axis=1)
        gv_ref[...] = gv.astype(BF16)
        g = gv + b_ref[...]
        sg = (g[:, :sw] * _sigmoid(g[:, sw:])).astype(BF16)
        sg_ref[...] = sg
        ys = jnp.concatenate([jnp.dot(sg, wu_ref[s], preferred_element_type=F32) for s in range(N_CHIPS)], axis=1)
        ys_ref[...] = ys.astype(BF16)
        mg = (_sigmoid(glp_ref[...].astype(F32)) * yp_ref[...].astype(F32) + _sigmoid(gls_ref[...].astype(F32)) * ys).astype(BF16)
        mg_ref[...] = mg
        y = jnp.dot(mg, wo_ref[...], preferred_element_type=F32)
        y_ref[...] = y.astype(BF16)
        xn = x_ref[...] + gt_ref[...] * y
        xn_ref[...] = xn
        r = lax.rsqrt(jnp.mean(xn * xn, axis=-1, keepdims=True) + EPS)
        h_ref[...] = (xn * r * gn_ref[...] * (1.0 + sc_ref[...]) + sh_ref[...]).astype(BF16)

    full3 = lambda i: (0, 0, 0)
    full2 = lambda i: (0, 0)
    rows = lambda w: pl.BlockSpec((tm, w), lambda i: (i, 0))
    rowv = pl.BlockSpec((1, d), full2)
    return _pcall(
        body, name=name, grid=(t // tm,),
        in_specs=[rows(sw), pl.BlockSpec(w_glu.shape, full3), pl.BlockSpec(b_glu.shape, full2),
                  pl.BlockSpec(w_up.shape, full3), rows(d), pl.BlockSpec((tm, d), lambda i: (i, cb)),
                  pl.BlockSpec((tm, d), lambda i: (i, cb + 1)), pl.BlockSpec(w_out.shape, full2), rows(d), rowv, rowv, rowv, rowv],
        out_specs=[rows(N_CHIPS * ns), rows(sw), rows(d), rows(d), rows(d), rows(d), rows(d)],
        out_shape=[jax.ShapeDtypeStruct((t, N_CHIPS * ns), BF16), jax.ShapeDtypeStruct((t, sw), BF16)]
        + [jax.ShapeDtypeStruct((t, d), BF16)] * 4 + [jax.ShapeDtypeStruct((t, d), F32)],
        compiler_params=_params("parallel"))(ge, w_glu, b_glu, w_up, y_pool, uin, uin, w_out, x_in, gt, *nxt)


def _glu_up_bwd(name, dy, w_up, gv, b_glu, w_glu, sg, ge, tm=ROWS, deps=()):
    t, sw = ge.shape
    tm = min(tm, t)
    nb = t // tm
    ns = w_up.shape[2]
    nt = (((1,), (1,)), ((), ()))
    tn_ = (((0,), (0,)), ((), ()))

    def body(dy_ref, wu_ref, gv_ref, b_ref, wg_ref, sg_ref, ge_ref, *rest):
        dge_ref, db_ref, dwu_ref, dwg_ref, dwu_acc, dwg_acc = rest[len(deps):]
        i = pl.program_id(0)

        @pl.when(i == 0)
        def _():
            db_ref[...] = jnp.zeros_like(db_ref)
            dwu_acc[...] = jnp.zeros_like(dwu_acc)
            dwg_acc[...] = jnp.zeros_like(dwg_acc)

        dyv = dy_ref[...]
        dsg = None
        for s in range(N_CHIPS):
            q = lax.dot_general(dyv[:, s * ns:(s + 1) * ns], wu_ref[s], nt, preferred_element_type=F32)
            dsg = q if dsg is None else dsg + q
        g = gv_ref[...].astype(F32) + b_ref[...]
        val, sgm = g[:, :sw], _sigmoid(g[:, sw:])
        dval, dgate = dsg * sgm, dsg * val * sgm * (1.0 - sgm)
        db_ref[...] += jnp.concatenate([_colsum(dval), _colsum(dgate)], axis=1)
        dgv = jnp.concatenate([dval, dgate], axis=1).astype(BF16)
        dge = None
        for s in range(N_CHIPS):
            q = lax.dot_general(dgv[:, s * ns:(s + 1) * ns], wg_ref[s], nt, preferred_element_type=F32)
            dge = q if dge is None else dge + q
        dge_ref[...] = dge.astype(BF16)
        dwu_acc[...] += lax.dot_general(sg_ref[...], dyv, tn_, preferred_element_type=F32)
        dwg_acc[...] += lax.dot_general(ge_ref[...], dgv, tn_, preferred_element_type=F32)

        @pl.when(i == nb - 1)
        def _():
            for s in range(N_CHIPS):
                dwu_ref[s] = dwu_acc[:, s * ns:(s + 1) * ns].astype(BF16)
                dwg_ref[s] = dwg_acc[:, s * ns:(s + 1) * ns].astype(BF16)

    full3 = lambda i: (0, 0, 0)
    rows = lambda w: pl.BlockSpec((tm, w), lambda i: (i, 0))
    wide = N_CHIPS * ns
    return _pcall(
        body, name=name, grid=(nb,),
        in_specs=[rows(wide), pl.BlockSpec(w_up.shape, full3), rows(wide), pl.BlockSpec(b_glu.shape, lambda i: (0, 0)),
                  pl.BlockSpec(w_glu.shape, full3), rows(sw), rows(sw)] + [ANY_SPEC] * len(deps),
        out_specs=[rows(sw), pl.BlockSpec((1, wide), lambda i: (0, 0)), pl.BlockSpec(w_up.shape, full3),
                   pl.BlockSpec(w_glu.shape, full3)],
        out_shape=[jax.ShapeDtypeStruct((t, sw), BF16), jax.ShapeDtypeStruct((1, wide), F32),
                   jax.ShapeDtypeStruct(w_up.shape, BF16), jax.ShapeDtypeStruct(w_glu.shape, BF16)],
        scratch_shapes=[pltpu.VMEM((sw, wide), F32), pltpu.VMEM((sw, wide), F32)],
        compiler_params=_params("arbitrary"))(dy, w_up, gv, b_glu, w_glu, sg, ge, *deps)


_FULL = lambda i, j: (i, j)
_ROWV = lambda i, j: (0, j)


def _gates_bwd_epilogue(y_pool, y_ssm, uin, t, d, pw):
    cb = (2 * pw) // d

    def epi(dm, yp, ys, glp, gls):
        sp, ss = _sigmoid(glp.astype(F32)), _sigmoid(gls.astype(F32))
        dgl = jnp.concatenate([dm * yp.astype(F32) * sp * (1.0 - sp), dm * ys.astype(F32) * ss * (1.0 - ss)], axis=1)
        return dm * sp, dm * ss, ("at", 2 * pw, dgl)

    ins = [(y_pool, (ROWS, d), _FULL), (y_ssm, (ROWS, d), _FULL),
           (uin, (ROWS, d), lambda i, j: (i, cb)), (uin, (ROWS, d), lambda i, j: (i, cb + 1))]
    wide = 2 * pw + 2 * d
    return epi, ins, [((t, d), BF16, (ROWS, d), _FULL)] * 2 + [((t, wide), BF16, (ROWS, wide), _FULL)]


def _loss_epilogue(x_in, gt, tgt, g_final, t, d):
    blk = (ROWS, d)
    full = lambda i, j: (i, j)
    rowv = lambda i, j: (0, j)

    def epi(acc, res, scale, tv, gv):
        xv = res + (0.5 * scale) * acc
        r = lax.rsqrt(jnp.mean(xv * xv, axis=-1, keepdims=True) + EPS)
        xr = xv * r
        e = xr * gv - tv
        loss_row = 0.5 * jnp.mean(e * e, axis=-1, keepdims=True)
        dout = e * (1.0 / d)
        gd = gv * dout
        dx = r * (gd - xr * jnp.mean(gd * xr, axis=-1, keepdims=True))
        fdx = 0.5 * dx
        return [dx, scale * fdx], [_colsum(dout * xr), _colsum(loss_row * jnp.ones((1, 128), F32)), _colsum(fdx * acc)]

    ins = [(x_in, blk, full), (gt, (1, d), rowv), (tgt, blk, full), (g_final, (1, d), rowv)]
    return epi, ins, [((t, d), F32, blk, full), ((t, d), BF16, blk, full)], [d, 128, d]


def _norm_bwd_epilogue(x, dres, g, sc, up, t, d):
    blk = (ROWS, d)
    full = lambda i, j: (i, j)
    rowv = lambda i, j: (0, j)

    def epi(dhv, xv, drv, gv, scv, *rest):
        r = lax.rsqrt(jnp.mean(xv * xv, axis=-1, keepdims=True) + EPS)
        xr = xv * r
        dn = dhv * (1.0 + scv)
        gd = gv * dn
        dx = drv + r * (gd - xr * jnp.mean(gd * xr, axis=-1, keepdims=True))
        outs, reds = [dx], [_colsum(dhv), _colsum(dhv * xr * gv), _colsum(dn * xr)]
        if up:
            yv, gtv = rest
            fdx = up[2] * dx
            outs.append(gtv * fdx)
            reds.append(_colsum(fdx * yv.astype(F32)))
        return outs, reds

    ins = [(x, blk, full), (dres, blk, full), (g, (1, d), rowv), (sc, (1, d), rowv)]
    ins += [(up[0], blk, full), (up[1], (1, d), rowv)] if up else []
    outs = [((t, d), F32, blk, full)] + ([((t, d), BF16, blk, full)] if up else [])
    return epi, ins, outs, [d] * (4 if up else 3)


def _ffn_in_act(name, h, w_in, tm=ROWS):
    t, d = h.shape
    ns = w_in.shape[2]
    tm = _pick(t, tm, 8)
    w4 = w_in.reshape(2, 2, d, ns)

    def body(h_ref, w_ref, ab_ref, act_ref):
        hv = h_ref[...]
        a = jnp.dot(hv, w_ref[0], preferred_element_type=F32)
        b = jnp.dot(hv, w_ref[1], preferred_element_type=F32)
        ab_ref[0] = a.astype(BF16)
        ab_ref[1] = b.astype(BF16)
        act_ref[...] = (a * _sigmoid(a) * b).astype(BF16)

    return _pcall(body, name=name, grid=(2, t // tm),
                  in_specs=[pl.BlockSpec((tm, d), lambda c, i: (i, 0)), pl.BlockSpec((2, None, d, ns), lambda c, i: (0, c, 0, 0))],
                  out_specs=[pl.BlockSpec((2, tm, ns), lambda c, i: (0, i, c)), pl.BlockSpec((tm, ns), lambda c, i: (i, c))],
                  out_shape=[jax.ShapeDtypeStruct((2, t, 2 * ns), BF16), jax.ShapeDtypeStruct((t, 2 * ns), BF16)],
                  compiler_params=_params("parallel", "parallel"))(h, w4)


def _dswiglu_epilogue(ab, t, f, tn):
    blk = (2, ROWS, tn)
    idx = lambda i, j: (0, i, j)

    def epi(dact, abv):
        a, b = abv[0].astype(F32), abv[1].astype(F32)
        s = _sigmoid(a)
        return (jnp.stack([dact * b * (s * (1.0 + a * (1.0 - s))), dact * (a * s)]),)

    return epi, [(ab, blk, idx)], [((2, t, f), BF16, blk, idx)]


def _ffn_fwd(tag, x, h, gt, get_w_in, get_w_out, nxt=None, loss=None):
    t, d = x.shape
    ab, act = _ffn_in_act(tag + "_in", h, get_w_in(h))
    if loss:
        epi, epi_ins, epi_outs, epi_reds = _loss_epilogue(x, gt, *loss, t, d)
        res = _mm(tag + "_out", act, get_w_out(act), tm=ROWS, tn=d, epi=epi, epi_ins=epi_ins, epi_outs=epi_outs,
                  epi_reds=epi_reds, b_resident=True)
        return res, (x, h, ab, act, None)
    epi, epi_ins, epi_outs = _gate_norm_epilogue(0.5, x, gt, nxt, t, d)
    res = _mm(tag + "_out", act, get_w_out(act), tm=ROWS, tn=d, epi=epi, epi_ins=epi_ins, epi_outs=epi_outs, b_resident=True)
    return res[0], res[2], (x, h, ab, act, res[1])


def _ffn_bwd(tag, dx_new, dy, saved, g, sc, w_in, w_out, start_rs, up, deps=()):
    x, h, ab, act, _ = saved
    d = x.shape[1]
    f = act.shape[1]
    dw_out = _mm(tag + "_dwout", act, dy, ta=True, tm=1408, tn=d, tk=2048, deps=deps)
    tok = start_rs("out", dw_out.reshape(N_CHIPS, f // N_CHIPS, d))
    epi, epi_ins, epi_outs = _dswiglu_epilogue(ab, x.shape[0], f, w_in.shape[2])
    dab = _mm(tag + "_dact", dy, w_out, tb=True, tm=ROWS, tn=w_in.shape[2], epi=epi, epi_ins=epi_ins, epi_outs=epi_outs,
              deps=(tok,), j_outer=True)
    dw_in = _mm(tag + "_dwin", h, dab, ta=True, out_stacked=True, b_halves=True, tm=ROWS, tn=1408, tk=4096, j_outer=True)
    tok = start_rs("in", dw_in)
    epi, epi_ins, epi_outs, epi_reds = _norm_bwd_epilogue(x, dx_new, g, sc, up, x.shape[0], d)
    res = _mm(tag + "_dh", dab, w_in, tb=True, b_stacked=True, a_halves=True, tn=d, tk=5632, deps=(tok,), epi=epi,
              epi_ins=epi_ins, epi_outs=epi_outs, epi_reds=epi_reds, b_resident=True)
    if up:
        return res
    return res[0], None, res[1], res[2], res[3], None


def _row(v):
    return v.reshape(1, -1)


def _pack(parts):
    cols = []
    for p in parts:
        flat = p.reshape(-1).astype(F32)
        padn = (-flat.shape[0]) % 128
        cols.append(jnp.pad(flat, (0, padn)) if padn else flat)
    flat = jnp.concatenate(cols)
    padn = (-flat.shape[0]) % 1024
    if padn:
        flat = jnp.pad(flat, (0, padn))
    return flat.reshape(-1, 128)


def _unpack(packed, shapes):
    flat = packed.reshape(-1)
    out, off = [], 0
    for s in shapes:
        n = math.prod(s)
        out.append(flat[off:off + n].reshape(s))
        off += n + ((-n) % 128)
    return out


SMALL = ['b_ada', 'g_ffn1', 'g_mix', 'pool_w', 'pool_b', 'pool_scale', 'ssm_lam_re_log', 'ssm_lam_im', 'ssm_log_dt',
         'ssm_b_re', 'ssm_b_im', 'ssm_c_re', 'ssm_c_im', 'ssm_d', 'b_glu', 'g_ffn2', 'g_final']
BIG = ['w_ffn1_in', 'w_ffn1_out', 'w_in', 'w_pool_up', 'w_glu', 'w_ssm_up', 'w_out', 'w_ffn2_in', 'w_ffn2_out']
AG_GROUPS = [[0], [1], [2, 3, 4, 5, 6], [7, 8]]
AG_DIRECT = [False, True, False, True]
SMALL_LATE = ['b_ada', 'g_ffn1']
RS_SUM_GROUPS = [[0, 1, 2, 3], [4, 5]]
WEIGHTS = ['w_ada', 'b_ada', 'g_ffn1', 'w_ffn1_in', 'w_ffn1_out', 'g_mix', 'w_in', 'pool_w', 'pool_b', 'pool_scale', 'w_pool_up',
           'ssm_lam_re_log', 'ssm_lam_im', 'ssm_log_dt', 'ssm_b_re', 'ssm_b_im', 'ssm_c_re', 'ssm_c_im', 'ssm_d', 'w_glu', 'b_glu',
           'w_ssm_up', 'w_out', 'g_ffn2', 'w_ffn2_in', 'w_ffn2_out', 'g_final']


def kernel(x, c, w_ada, b_ada, g_ffn1, w_ffn1_in, w_ffn1_out, g_mix, w_in, pool_w, pool_b, pool_scale, w_pool_up, ssm_lam_re_log, ssm_lam_im, ssm_log_dt, ssm_b_re, ssm_b_im, ssm_c_re, ssm_c_im, ssm_d, w_glu, b_glu, w_ssm_up, w_out, g_ffn2, w_ffn2_in, w_ffn2_out, g_final, loss_target, m_w_ada, m_b_ada, m_g_ffn1, m_w_ffn1_in, m_w_ffn1_out, m_g_mix, m_w_in, m_pool_w, m_pool_b, m_pool_scale, m_w_pool_up, m_ssm_lam_re_log, m_ssm_lam_im, m_ssm_log_dt, m_ssm_b_re, m_ssm_b_im, m_ssm_c_re, m_ssm_c_im, m_ssm_d, m_w_glu, m_b_glu, m_w_ssm_up, m_w_out, m_g_ffn2, m_w_ffn2_in, m_w_ffn2_out, m_g_final, v_w_ada, v_b_ada, v_g_ffn1, v_w_ffn1_in, v_w_ffn1_out, v_g_mix, v_w_in, v_pool_w, v_pool_b, v_pool_scale, v_w_pool_up, v_ssm_lam_re_log, v_ssm_lam_im, v_ssm_log_dt, v_ssm_b_re, v_ssm_b_im, v_ssm_c_re, v_ssm_c_im, v_ssm_d, v_w_glu, v_b_glu, v_w_ssm_up, v_w_out, v_g_ffn2, v_w_ffn2_in, v_w_ffn2_out, v_g_final):
    args = dict(locals())
    wt = {n: args[n] for n in WEIGHTS}
    mom = {n: args["m_" + n] for n in WEIGHTS}
    var = {n: args["v_" + n] for n in WEIGHTS}

    t, d = x.shape[1], x.shape[2]
    pw = pool_b.shape[1]
    sw = ssm_d.shape[1]
    ngrp = sw // SSM_GROUP
    gn = ngrp * SSM_STATE
    xi, yi, ci = _place()
    b_me = 4 * xi + 2 * yi + ci
    s_me = 2 * xi + yi
    x2d = x[0]
    tgt = loss_target[0]

    c_all = _allgather_small("ag_c", c.reshape(8, d // 8)).reshape(N_DEV, d)
    ncol = w_ada.shape[2]
    b_sh = lax.dynamic_slice(b_ada, (0, s_me * ncol), (1, ncol))
    mod_sh = _ada_fwd("ada_fwd", c_all, w_ada[0], b_sh)
    md_send, md_recv, mod_sh, md_land, tok = _bcast_start("mod_start", mod_sh)

    shards = [wt[n][0].astype(BF16) for n in BIG]
    lands = _place_shards(shards, s_me)
    n0 = len(AG_GROUPS[0])
    sems_a, shards_a, lands_a, tok = _ag_start("ag_start_first", shards[:n0], lands[:n0], AG_GROUPS[:1], AG_DIRECT[:1], deps=(tok,))
    rest = [[w - n0 for w in grp] for grp in AG_GROUPS[1:]]
    sems_b, shards_b, lands_b, tok = _ag_start("ag_start_rest", shards[n0:], lands[n0:], rest, AG_DIRECT[1:], deps=(tok,))
    ag_sems, shards_t, lands_t = sems_a + sems_b, shards_a + shards_b, lands_a + lands_b
    md_land = _bcast_wait("mod_wait", mod_sh, md_land, md_send, md_recv, tok)
    mod_all = lax.dynamic_update_slice(md_land, mod_sh[None], (b_me, 0, 0))
    full = {}

    def weights(gi, *after):
        grp = AG_GROUPS[gi]
        if BIG[grp[0]] not in full:
            ls = _ag_wait("ag_wait%d" % gi, [shards_t[w] for w in grp], [lands_t[w] for w in grp], *ag_sems[gi],
                          AG_DIRECT[gi], after)
            for w, l in zip(grp, ls if AG_DIRECT[gi] else _ag_forward("ag_fwd%d" % gi, ls)):
                full[BIG[w]] = l
        return full

    mod_me = jnp.concatenate([lax.dynamic_slice(mod_all, (2 * s, b_me, 0), (1, 1, ncol))[0] for s in range(N_CHIPS)], axis=1)
    mod = mod_me.reshape(9, d)
    sh1, sc1, gt1, sh2, sc2, gt2, sh3, sc3, gt3 = [mod[k:k + 1] for k in range(9)]

    f = w_ffn1_out.shape[1] * N_CHIPS

    col = lambda a: a.reshape(gn, 1)
    lrl_c, li_c = col(ssm_lam_re_log), col(ssm_lam_im)
    ldt_c = col(jnp.broadcast_to(ssm_log_dt.reshape(ngrp, 1), (ngrp, SSM_STATE)))
    b_re2, b_im2 = ssm_b_re.reshape(gn, SSM_GROUP), ssm_b_im.reshape(gn, SSM_GROUP)
    lrdt_c, ang_c, bb_re, bb_im = _ssm_prep("ssm_prep", lrl_c, li_c, ldt_c, b_re2, b_im2)
    lrdt, ang = lrdt_c.reshape(1, gn), ang_c.reshape(1, gn)
    tr = lambda a: jnp.swapaxes(a, 1, 2)
    bbd = [_blockdiag_b(v, sw).astype(BF16) for v in (bb_re, bb_im)]
    ccd = [_blockdiag_c(v[0], sw).astype(BF16) for v in (ssm_c_re, ssm_c_im)]
    bbd_t, ccd_t = [tr(v) for v in bbd], [tr(v) for v in ccd]
    early = [n for n in SMALL if n not in SMALL_LATE]
    packs = {}
    for tag, names, extra in (("early", early, []), ("late", SMALL_LATE, [jnp.zeros((128,), F32)])):
        packs[tag] = [_pack([src[n] for n in names] + extra) for src in (wt, mom, var)]
    shadow = [lrdt, ang, ldt_c, *bbd, *ccd, *bbd_t, *ccd_t, *packs["early"], *packs["late"]]

    h1 = _norm_fwd("ffn1_norm", x2d, g_ffn1, sc1, sh1, deps=(tok,))
    x1, h2, sav1 = _ffn_fwd("ffn1", x2d, h1, gt1, lambda after: weights(0, after, *shadow)['w_ffn1_in'],
                            lambda after: weights(1, after)['w_ffn1_out'].reshape(f, d), (g_mix, sc2, sh2))
    weights(2, h2)
    wo = full['w_out'].reshape(d, d)
    uin = _mm("mix_in", h2, full['w_in'], b_stacked=True, tm=2048, tn=768, j_outer=True)
    p_pool, z_pool, y_pool = _pool_fwd("pool_fwd", uin, pool_w[0], pool_b, pool_scale, full['w_pool_up'], pw, tm=1024)
    y_s, ge, s_re, s_im = _ssm_fwd("ssm_fwd", uin, lrdt, ang, *bbd, *ccd, ssm_d, sw)
    gv, sg, y_ssm, merged, y2, h3, x2 = _mixer_tail("mixer_tail", ge, full['w_glu'], b_glu, full['w_ssm_up'], y_pool, uin, wo,
                                                    x1, gt2, (g_ffn2, sc3, sh3), d, pw)

    (dx3, dy3, dg_final, loss_v, dgt3), sav3 = _ffn_fwd(
        "ffn2", x2, h3, gt3, lambda after: weights(3, after)['w_ffn2_in'],
        lambda after: weights(3, after)['w_ffn2_out'].reshape(f, d), loss=(tgt, _row(g_final)))

    gs = {}
    rs_open = []

    def start_rs(names, gbs):
        send, recv, g_thru, land_thru, token = _rs_start("rs_start_" + names[0], gbs)
        rs_open.append((names, send, recv, g_thru, land_thru))
        return token

    dx2, dy2, dsh3, dsc3, gs['g_ffn2'], dgt2 = _ffn_bwd(
        "ffn2b", dx3, dy3, sav3, g_ffn2, sc3, full['w_ffn2_in'], full['w_ffn2_out'].reshape(f, d),
        lambda key, g: start_rs(['w_ffn2_' + key], [g]), (y2, gt2, 1.0))

    epi, epi_ins, epi_outs = _gates_bwd_epilogue(y_pool, y_ssm, uin, t, d, pw)
    dy_pool, dy_ssm, duin, g_wo = _mm("mix_dmerged", dy2, wo, tb=True, tm=ROWS, tn=d, epi=epi, epi_ins=epi_ins,
                                      epi_outs=epi_outs, b_resident=True, wgrad=merged, wgrad_rows=True)
    g_wo = g_wo.reshape(N_CHIPS, d // N_CHIPS, d)

    duin, gs['pool_w'], gs['pool_b'], gs['pool_scale'], g_wpu = _pool_bwd(
        "pool_bwd", dy_pool, z_pool, p_pool, pool_w[0], pool_b, pool_scale, full['w_pool_up'], duin, tm=1024)

    tok = start_rs(['w_out', 'w_pool_up'], [g_wo, g_wpu])
    dge, gs['b_glu'], g_wsu, g_wglu = _glu_up_bwd("glu_up_bwd", dy_ssm, full['w_ssm_up'], gv, b_glu, full['w_glu'], sg, ge,
                                                  deps=(tok,))
    (duin, g_abre, g_abim, g_bbd_re, g_bbd_im, g_ccd_re, g_ccd_im, gs['ssm_d']) = _ssm_bwd(
        "ssm_bwd", dge, y_s, uin, s_re, s_im, lrdt, ang, *bbd_t, *ccd_t, ssm_d, sw, duin)
    gs['ssm_c_re'], gs['ssm_c_im'] = _diag_of_c(g_ccd_re, sw), _diag_of_c(g_ccd_im, sw)
    d_lrl, d_li, d_ldt, d_bre, d_bim = _ssm_param_bwd(
        "ssm_param_bwd", lrl_c, li_c, ldt_c, b_re2, b_im2, g_abre.reshape(gn, 1), g_abim.reshape(gn, 1),
        _diag_of_b(g_bbd_re, sw), _diag_of_b(g_bbd_im, sw))
    gs['ssm_lam_re_log'], gs['ssm_lam_im'] = d_lrl, d_li
    gs['ssm_log_dt'] = jnp.sum(d_ldt.reshape(ngrp, SSM_STATE), axis=1)
    gs['ssm_b_re'], gs['ssm_b_im'] = d_bre, d_bim

    g_win =_mm("mix_dwin", h2, duin, ta=True, out_stacked=True, tm=d, tn=768, tk=4096)
    tok = start_rs(['w_ssm_up', 'w_glu', 'w_in'], [g_wsu, g_wglu, g_win])
    epi, epi_ins, epi_outs, epi_reds = _norm_bwd_epilogue(x1, dx2, g_mix, sc2, (sav1[4], gt1, 0.5), t, d)
    dx1, dy1, dsh2, dsc2, gs['g_mix'], dgt1 = _mm(
        "mix_dh", duin, full['w_in'], tb=True, b_stacked=True, tn=d, tk=3072, deps=(tok,), epi=epi, epi_ins=epi_ins,
        epi_outs=epi_outs, epi_reds=epi_reds, b_resident=True)

    gs['g_final'] = dg_final
    sg_blk = _pack([gs[n] for n in early])
    sg_send, sg_recv, sg_blk, sg_land, tok = _bcast_start("sg_start", sg_blk)

    dx0, _, dsh1, dsc1, gs['g_ffn1'], _ = _ffn_bwd(
        "ffn1b", dx1, dy1, sav1, g_ffn1, sc1, full['w_ffn1_in'], full['w_ffn1_out'].reshape(f, d),
        lambda key, g: start_rs(['w_ffn1_' + key], [g]), None, deps=(tok,))

    gs['b_ada'] = jnp.concatenate([dsh1, dsc1, dgt1, dsh2, dsc2, dgt2, dsh3, dsc3, dgt3], axis=1)
    lt_blk = _pack([gs[n] for n in SMALL_LATE] + [loss_v])
    lt_send, lt_recv, lt_blk, lt_land, tok = _bcast_start("late_start", lt_blk)

    grads, delta, new_m, new_v = {}, {}, {}, {}

    def small_update(tag, names, blk, land):
        g8 = lax.dynamic_update_slice(land, blk[None], (b_me, 0, 0)).reshape(-1, 128)
        w_pack, m_pack, v_pack = packs[tag]
        shapes = [wt[n].shape for n in names]
        per_param, packed = _adamw_small("adamw_small_" + tag, w_pack, g8, m_pack, v_pack, shapes)
        for k, dst in enumerate((grads, delta, new_m, new_v)):
            rest = _unpack(packed[k], shapes)
            for i, n in enumerate(names):
                dst[n] = per_param[i][k] if per_param[i] is not None else rest[i]
        return g8

    after = tok
    for group in RS_SUM_GROUPS:
        names, g_done, land_done = [], [], []
        for k in group:
            nk, send, recv, g_thru, land_thru = rs_open[k]
            gd, ld = _rs_wait("rs_wait_" + nk[0], g_thru, land_thru, send, recv, after)
            names, g_done, land_done = names + nk, g_done + gd, land_done + ld
        for n, g_sum in zip(names, _rs_sum("rs_sum_" + names[0], g_done, land_done)):
            g_out, dl, mn, vn = _adamw("adamw_" + n, wt[n][0], g_sum, mom[n][0], var[n][0])
            grads[n], delta[n], new_m[n], new_v[n] = g_out[None], dl[None], mn[None], vn[None]
            after = dl
        if group is RS_SUM_GROUPS[-2]:
            small_update("early", early, sg_blk, _bcast_wait("sg_wait", sg_blk, sg_land, sg_send, sg_recv, after))

    lt_land = _bcast_wait("late_wait", lt_blk, lt_land, lt_send, lt_recv, after)
    late8 = small_update("late", SMALL_LATE, lt_blk, lt_land).reshape(N_DEV, -1)
    loss = jnp.sum(late8[:, 10 * d])
    dmod_sh = lax.dynamic_slice(late8, (0, s_me * ncol), (N_DEV, ncol))
    g_w_ada, dl, mn, vn = _ada_bwd_adamw("ada_bwd_adamw", c_all, dmod_sh, w_ada[0], m_w_ada[0], v_w_ada[0])
    grads['w_ada'], delta['w_ada'], new_m['w_ada'], new_v['w_ada'] = g_w_ada[None], dl[None], mn[None], vn[None]

    return (loss, dx0[None], *[grads[n] for n in WEIGHTS], *[delta[n] for n in WEIGHTS],
            *[new_m[n] for n in WEIGHTS], *[new_v[n] for n in WEIGHTS])
```

```python
import functools
import math

import jax
import jax.numpy as jnp
from jax import lax
from jax.experimental import pallas as pl
from jax.experimental.pallas import tpu as pltpu

F32 = jnp.float32
BF16 = jnp.bfloat16
MESH = pl.DeviceIdType.MESH

EPS = 1e-6
POOL_WINDOWS = (2, 4, 8, 16)
POOL_HALO = 16
SSM_GROUP = 16
SSM_STATE = 64
SSM_BLOCKS = 4
N_DEV = 8
N_CHIPS = 4
ADAM_LR = 0.001
ADAM_B1 = 0.9
ADAM_B2 = 0.999
ADAM_EPS = 1e-08
ADAM_WD = 0.01
ADAM_STEP = 10
VMEM_LIMIT = 56 * 1024 * 1024
ROWS = 512


ANY_SPEC = pl.BlockSpec(memory_space=pl.ANY)
HBM_SPEC = pl.BlockSpec(memory_space=pltpu.HBM)
SEM_SPEC = pl.BlockSpec(memory_space=pltpu.SEMAPHORE)
EFFECT = pltpu.SideEffectType.DATAFLOW_SIDE_EFFECTING


def _hbm(a):
    return pltpu.with_memory_space_constraint(a, pltpu.HBM)


def _pcall(body, **kw):
    return pl.pallas_call(body, **kw)


def _params(*sem):
    return pltpu.CompilerParams(dimension_semantics=sem, vmem_limit_bytes=VMEM_LIMIT)


def _pick(n, cap, mult=128):
    if n <= cap:
        return n
    best = None
    for d in range(mult, cap + 1, mult):
        if n % d == 0:
            best = d
    assert best is not None, (n, cap, mult)
    return best


def _sigmoid(v):
    return 1.0 / (1.0 + jnp.exp(-v))


def _rowwise(name, fn, ins, params, outs, reds, tm, deps=()):
    t = ins[0][0].shape[0]
    tm = min(tm, t)
    nb = t // tm
    ni, npar, no, nd = len(ins), len(params), len(outs), len(deps)

    def body(*refs):
        iv = [r[...] for r in refs[:ni]]
        pv = [r[...] for r in refs[ni:ni + npar]]
        o_refs = refs[ni + npar + nd:ni + npar + nd + no]
        r_refs = refs[ni + npar + nd + no:]
        ovals, rvals = fn(iv, pv)
        for o_ref, val in zip(o_refs, ovals):
            off = 0
            if isinstance(val, tuple) and val[0] == "at":
                _, off, val = val
            parts = val if isinstance(val, (list, tuple)) else [val]
            for p in parts:
                o_ref[:, off:off + p.shape[1]] = p.astype(o_ref.dtype)
                off += p.shape[1]
        if r_refs:
            @pl.when(pl.program_id(0) == 0)
            def _():
                for r in r_refs:
                    r[...] = jnp.zeros_like(r)
            for r, val in zip(r_refs, rvals):
                r[...] += val

    in_specs = [pl.BlockSpec((tm, w), functools.partial(lambda i, cb: (i, cb), cb=cb)) for (_, w, cb) in ins]
    in_specs += [pl.BlockSpec(p.shape, lambda i: (0, 0)) for p in params]
    in_specs += [ANY_SPEC] * nd
    out_shape = [jax.ShapeDtypeStruct((t, w), dt) for (w, dt) in outs]
    out_shape += [jax.ShapeDtypeStruct((1, w), F32) for w in reds]
    out_specs = [pl.BlockSpec((tm, w), lambda i: (i, 0)) for (w, _) in outs]
    out_specs += [pl.BlockSpec((1, w), lambda i: (0, 0)) for w in reds]
    res = _pcall(body, name=name, grid=(nb,), in_specs=in_specs, out_specs=out_specs, out_shape=out_shape,
                 compiler_params=_params("arbitrary"))(*[a for a, _, _ in ins], *params, *deps)
    return res


def _colsum(v):
    return jnp.sum(v, axis=0, keepdims=True)


def _mm(name, a, b, *, ta=False, tb=False, b_stacked=False, out_stacked=False, tm=ROWS, tn=1024, tk=2816,
        out_dtype=BF16, epi=None, epi_ins=(), epi_outs=None, epi_reds=(), deps=(), j_outer=False, a_halves=False,
        b_halves=False, b_resident=False, wgrad=None, wgrad_rows=False):
    if a_halves:
        _, m, kdim = a.shape
        kdim *= 2
    elif ta:
        kdim, m = a.shape
    else:
        m, kdim = a.shape
    ns = None
    if b_stacked:
        ns = b.shape[2]
        n = b.shape[1] if tb else N_CHIPS * ns
        assert kdim == (N_CHIPS * ns if tb else b.shape[1]), (name, a.shape, b.shape)
    else:
        if b_halves:
            n = 2 * b.shape[2]
            assert kdim == b.shape[1] and not tb, (name, a.shape, b.shape)
        else:
            n = b.shape[0] if tb else b.shape[1]
            assert kdim == (b.shape[1] if tb else b.shape[0]), (name, a.shape, b.shape)
        if out_stacked:
            ns = n // N_CHIPS

    def shards(want):
        return max(g for g in (1, 2, 4) if g * ns <= max(want, ns))

    tm = _pick(m, tm, 128 if ta else 8)
    gn = gk = 1
    if (b_stacked and not tb) or out_stacked:
        gn = shards(min(tn, n // 2) if b_halves else tn)
        tn = gn * ns
    else:
        tn = _pick(n, tn)
    if b_stacked and tb:
        gk = shards(tk)
        tk = gk * ns
    else:
        tk = _pick(kdim, tk, 8 if ta else 128)
    nm, nn, nk = m // tm, n // tn, kdim // tk

    def ij(f):
        return (lambda g0, g1, k: f(g1, g0, k)) if j_outer else f

    if a_halves:
        assert b_stacked and tb and gk == N_CHIPS and nk == 1, name
        a_spec = pl.BlockSpec((2, tm, kdim // 2), ij(lambda i, j, k: (0, i, 0)))
    elif ta:
        a_spec = pl.BlockSpec((tk, tm), ij(lambda i, j, k: (k, i)))
    else:
        a_spec = pl.BlockSpec((tm, tk), ij(lambda i, j, k: (i, k)))
    if b_stacked and not tb:
        b_spec = pl.BlockSpec((gn, tk, ns), ij(lambda i, j, k: (j, k, 0)))
    elif b_stacked and tb:
        b_spec = pl.BlockSpec((gk, tn, ns), ij(lambda i, j, k: (k, j, 0)))
    elif b_halves:
        bph = (n // 2) // tn
        b_spec = pl.BlockSpec((None, tk, tn), ij(lambda i, j, k: (j // bph, k, j % bph)))
    elif tb:
        b_spec = pl.BlockSpec((tn, tk), ij(lambda i, j, k: (j, k)))
    else:
        b_spec = pl.BlockSpec((tk, tn), ij(lambda i, j, k: (k, j)))
    dims = (((0 if ta else 1,), (1 if tb else 0,)), ((), ()))

    if epi_outs is None:
        if out_stacked:
            epi_outs = [((N_CHIPS, m, ns), out_dtype, (gn, tm, ns), lambda i, j: (j, i, 0))]
        else:
            epi_outs = [((m, n), out_dtype, (tm, tn), lambda i, j: (i, j))]
    ne, no, nd, nr = len(epi_ins), len(epi_outs), len(deps), len(epi_reds)
    nx = 0 if wgrad is None else 1
    assert not (nr or nx) or (nn == 1 and not j_outer), name
    assert not nx or (nk == 1 and not ta and not a_halves and kdim % N_CHIPS == 0), name

    def body(a_ref, b_ref, *rest):
        e_refs = rest[:ne]
        x_refs = rest[ne:ne + nx]
        o_refs = rest[ne + nx + nd:ne + nx + nd + no]
        r_refs = rest[ne + nx + nd + no:ne + nx + nd + no + nr]
        wg_refs = rest[ne + nx + nd + no + nr:ne + nx + nd + no + nr + nx]
        scratch = rest[ne + nx + nd + no + nr + nx:]
        av = None if a_halves else a_ref[...].astype(BF16)
        if nx:
            wg_acc = scratch[-1]
            i = pl.program_id(0)
            pw_ = lax.dot_general(x_refs[0][...].astype(BF16), av, (((0,), (0,)), ((), ())), preferred_element_type=F32)

            @pl.when(i == 0)
            def _():
                wg_acc[...] = pw_

            @pl.when(i > 0)
            def _():
                wg_acc[...] += pw_

            @pl.when(i == nm - 1)
            def _():
                if wgrad_rows:
                    wg_refs[0][...] = wg_acc[...].astype(BF16)
                else:
                    ks = kdim // N_CHIPS
                    for s in range(N_CHIPS):
                        wg_refs[0][s] = wg_acc[:, s * ks:(s + 1) * ks].astype(BF16)
        if b_stacked and not tb:
            parts = [lax.dot_general(av, b_ref[s].astype(BF16), dims, preferred_element_type=F32) for s in range(gn)]
        elif b_stacked and tb:
            p = None
            for s in range(gk):
                if a_halves:
                    a_s = a_ref[s // 2, :, (s % 2) * ns:(s % 2 + 1) * ns].astype(BF16)
                else:
                    a_s = av[:, s * ns:(s + 1) * ns]
                q = lax.dot_general(a_s, b_ref[s].astype(BF16), dims, preferred_element_type=F32)
                p = q if p is None else p + q
            parts = [p]
        else:
            parts = [lax.dot_general(av, b_ref[...].astype(BF16), dims, preferred_element_type=F32)]

        def finish(acc_parts):
            if epi is None and out_stacked:
                acc = acc_parts[0]
                for s in range(gn):
                    o_refs[0][s] = acc[:, s * ns:(s + 1) * ns].astype(out_dtype)
            elif epi is None:
                w = acc_parts[0].shape[1]
                for s, part in enumerate(acc_parts):
                    o_refs[0][:, s * w:(s + 1) * w] = part.astype(out_dtype)
            else:
                acc = acc_parts[0] if len(acc_parts) == 1 else jnp.concatenate(acc_parts, axis=1)
                vals = epi(acc, *[r[...] for r in e_refs])
                if nr:
                    vals, reds = vals

                    @pl.when(pl.program_id(0) == 0)
                    def _():
                        for r in r_refs:
                            r[...] = jnp.zeros_like(r)
                    for r, val in zip(r_refs, reds):
                        r[...] += val
                for o_ref, val in zip(o_refs, vals):
                    if isinstance(val, tuple) and val[0] == "at":
                        o_ref[:, val[1]:val[1] + val[2].shape[1]] = val[2].astype(o_ref.dtype)
                    else:
                        o_ref[...] = val.astype(o_ref.dtype)

        if nk == 1:
            finish(parts)
        else:
            acc_ref = scratch[0]
            k = pl.program_id(2)
            w = parts[0].shape[1]

            @pl.when(k == 0)
            def _():
                for s, part in enumerate(parts):
                    acc_ref[:, s * w:(s + 1) * w] = part

            @pl.when(k > 0)
            def _():
                for s, part in enumerate(parts):
                    acc_ref[:, s * w:(s + 1) * w] += part

            @pl.when(k == nk - 1)
            def _():
                finish([acc_ref[...]])

    def _ij(f):
        return ij(lambda i, j, k: f(i, j))

    if b_resident:
        assert nk == 1 and (nn == 1 or j_outer), name
        b_spec = pl.BlockSpec(b_spec.block_shape, b_spec.index_map, pipeline_mode=pl.Buffered(1))
    in_specs = [a_spec, b_spec] + [pl.BlockSpec(blk, _ij(f)) for (_, blk, f) in epi_ins]
    if nx:
        in_specs.append(pl.BlockSpec((tm, wgrad.shape[1]), lambda i, j, k: (i, 0)))
    in_specs += [ANY_SPEC] * nd
    out_specs = [pl.BlockSpec(blk, _ij(f)) for (_, _, blk, f) in epi_outs]
    out_specs += [pl.BlockSpec((1, w), lambda *_: (0, 0)) for w in epi_reds]
    out_shape = [jax.ShapeDtypeStruct(s, dt) for (s, dt, _, _) in epi_outs] + [jax.ShapeDtypeStruct((1, w), F32) for w in epi_reds]
    scratch = [pltpu.VMEM((tm, tn), F32)] if nk > 1 else []
    if nx:
        wg_shape = (wgrad.shape[1], kdim) if wgrad_rows else (N_CHIPS, wgrad.shape[1], kdim // N_CHIPS)
        out_specs.append(pl.BlockSpec(wg_shape, lambda *_: (0,) * len(wg_shape)))
        out_shape.append(jax.ShapeDtypeStruct(wg_shape, BF16))
        scratch.append(pltpu.VMEM((wgrad.shape[1], kdim), F32))
    grid = (nn, nm, nk) if j_outer else (nm, nn, nk)
    sem = ("arbitrary",) * 3 if (nr or nx) else ("parallel", "parallel", "arbitrary")
    res = _pcall(body, name=name, grid=grid, in_specs=in_specs, out_specs=out_specs, out_shape=out_shape, scratch_shapes=scratch,
                 compiler_params=_params(*sem))(a, b, *[x for x, _, _ in epi_ins], *([wgrad] * nx), *deps)
    return res[0] if len(res) == 1 else res


def _norm_fwd(name, x, g, sc, sh, deps=()):
    d = x.shape[1]

    def fn(iv, pv):
        (xv,), (gv, scv, shv) = iv, pv
        r = lax.rsqrt(jnp.mean(xv * xv, axis=-1, keepdims=True) + EPS)
        return [xv * r * gv * (1.0 + scv) + shv], []

    return _rowwise(name, fn, [(x, d, 0)], [g, sc, sh], [(d, BF16)], [], ROWS, deps=deps)[0]


def _adamw(name, w, g, m, v, tm=256):
    c = w.shape[1]

    def fn(iv, pv):
        wv, gv, mv, vv = iv
        mn = ADAM_B1 * mv + (1.0 - ADAM_B1) * gv
        vn = ADAM_B2 * vv + (1.0 - ADAM_B2) * (gv * gv)
        m_hat = mn / (1.0 - ADAM_B1 ** ADAM_STEP)
        v_hat = vn / (1.0 - ADAM_B2 ** ADAM_STEP)
        delta = -ADAM_LR * (m_hat / (jnp.sqrt(v_hat) + ADAM_EPS) + ADAM_WD * wv)
        return [gv, delta, mn, vn], []

    return _rowwise(name, fn, [(w, c, 0), (g, c, 0), (m, c, 0), (v, c, 0)], [], [(c, F32)] * 4, [], _pick(w.shape[0], tm, 8))


def _unpack_plan(shape):
    n = math.prod(shape)
    if len(shape) == 2 and shape[0] == 1 and n % 128 == 0:
        return [((slice(None), slice(128 * r, 128 * (r + 1))), slice(r, r + 1), slice(None)) for r in range(n // 128)]
    if len(shape) == 2 and shape[0] == 1 and n < 128:
        return [((slice(None), slice(None)), slice(0, 1), slice(0, n))]
    if len(shape) == 1 and n % 128 == 0:
        return [((slice(128 * r, 128 * (r + 1)),), r, slice(None)) for r in range(n // 128)]
    if len(shape) == 3 and shape[0] == 1 and shape[2] == 64:
        return [((0, slice(2 * r + h, 2 * r + h + 1), slice(None)), slice(r, r + 1), slice(64 * h, 64 * (h + 1)))
                for r in range(n // 128) for h in range(2)]
    if len(shape) == 4 and shape[0] == 1 and shape[2:] == (128, 128):
        return [((0, k), slice(128 * k, 128 * (k + 1)), slice(None)) for k in range(shape[1])]
    return None


def _adamw_small(name, w, g8, m, v, shapes):
    r = w.shape[0]
    plans, rows0, off = [], [], 0
    for s in shapes:
        plans.append(_unpack_plan(s))
        rows0.append(off // 128)
        off += math.prod(s) + (-math.prod(s)) % 128
    direct = [i for i, p in enumerate(plans) if p is not None]

    def body(w_ref, g_ref, m_ref, v_ref, *out):
        packed = out[4 * len(direct):]
        gv = g_ref[0:r, :]
        for k in range(1, N_DEV):
            gv = gv + g_ref[k * r:(k + 1) * r, :]
        mn = ADAM_B1 * m_ref[...] + (1.0 - ADAM_B1) * gv
        vn = ADAM_B2 * v_ref[...] + (1.0 - ADAM_B2) * (gv * gv)
        m_hat = mn / (1.0 - ADAM_B1 ** ADAM_STEP)
        v_hat = vn / (1.0 - ADAM_B2 ** ADAM_STEP)
        packed[0][...] = gv
        packed[1][...] = -ADAM_LR * (m_hat / (jnp.sqrt(v_hat) + ADAM_EPS) + ADAM_WD * w_ref[...])
        packed[2][...] = mn
        packed[3][...] = vn
        for di, i in enumerate(direct):
            for kind in range(4):
                o_ref, src = out[4 * di + kind], packed[kind]
                for o_idx, row, lanes in plans[i]:
                    row = (rows0[i] + row) if isinstance(row, int) else slice(rows0[i] + row.start, rows0[i] + row.stop)
                    o_ref[o_idx] = src[row, lanes]

    out_shape = [jax.ShapeDtypeStruct(shapes[i], F32) for i in direct for _ in range(4)]
    out_shape += [jax.ShapeDtypeStruct((r, 128), F32)] * 4
    res = _pcall(body, name=name, out_shape=out_shape,
                 compiler_params=pltpu.CompilerParams(vmem_limit_bytes=VMEM_LIMIT))(w, g8, m, v)
    per_param = [None] * len(shapes)
    for di, i in enumerate(direct):
        per_param[i] = res[4 * di:4 * di + 4]
    return per_param, res[4 * len(direct):]


def _pool_fwd(name, uin, pool_w, pool_b, pool_scale, w_up, pw, tm=ROWS):
    t = uin.shape[0]
    tm = min(tm, t)
    ng = len(POOL_WINDOWS)
    gw = pw // ng
    ns = w_up.shape[2]

    def body(u_ref, w_ref, b_ref, s_ref, wu_ref, p_ref, z_ref, y_ref, ext):
        i = pl.program_id(0)

        @pl.when(i == 0)
        def _():
            ext[0:POOL_HALO, :] = jnp.zeros((POOL_HALO, pw), F32)

        u = u_ref[...].astype(F32)
        ext[POOL_HALO:POOL_HALO + tm, :] = u
        pos = i * tm + lax.broadcasted_iota(jnp.int32, (tm, 1), 0)
        for k, win in enumerate(POOL_WINDOWS):
            cols = slice(k * gw, (k + 1) * gw)
            acc = u[:, cols]
            for j in range(1, win):
                acc = acc + ext[POOL_HALO - j:POOL_HALO - j + tm, cols]
            cnt = jnp.minimum(pos + 1, win).astype(F32)
            z = acc / cnt - u[:, cols]
            zp = jnp.dot(z.astype(BF16), w_ref[k].astype(BF16), preferred_element_type=F32) + b_ref[:, cols]
            p_ref[:, cols] = (zp * s_ref[:, cols]).astype(BF16)
            z_ref[:, cols] = z.astype(BF16)
        ext[0:POOL_HALO, :] = u[tm - POOL_HALO:tm, :]
        pv = p_ref[...]
        for s in range(N_CHIPS):
            y_ref[:, s * ns:(s + 1) * ns] = jnp.dot(pv, wu_ref[s], preferred_element_type=F32).astype(BF16)

    full3 = lambda i: (0, 0, 0)
    return _pcall(
        body, name=name, grid=(t // tm,),
        in_specs=[pl.BlockSpec((tm, pw), lambda i: (i, 0)), pl.BlockSpec(pool_w.shape, full3),
                  pl.BlockSpec(pool_b.shape, lambda i: (0, 0)), pl.BlockSpec(pool_scale.shape, lambda i: (0, 0)),
                  pl.BlockSpec(w_up.shape, full3)],
        out_specs=[pl.BlockSpec((tm, pw), lambda i: (i, 0))] * 2 + [pl.BlockSpec((tm, N_CHIPS * ns), lambda i: (i, 0))],
        out_shape=[jax.ShapeDtypeStruct((t, pw), BF16)] * 2 + [jax.ShapeDtypeStruct((t, N_CHIPS * ns), BF16)],
        scratch_shapes=[pltpu.VMEM((POOL_HALO + tm, pw), F32)],
        compiler_params=_params("arbitrary"))(uin, pool_w, pool_b, pool_scale, w_up)


def _pool_bwd(name, dy, z, p, pool_w, pool_b, pool_scale, w_up, duin, tm=ROWS):
    t, pw = z.shape
    tm = min(tm, t)
    nb = t // tm
    ng = len(POOL_WINDOWS)
    gw = pw // ng
    ns = w_up.shape[2]

    def body(dy_ref, z_ref, p_ref, w_ref, b_ref, s_ref, wu_ref, duin_ref, du_ref, dw_ref, db_ref, ds_ref, dwu_ref, ext, dwu_acc):
        i = pl.program_id(0)

        @pl.when(i == 0)
        def _():
            ext[tm:tm + POOL_HALO, :] = jnp.zeros((POOL_HALO, pw), F32)
            dw_ref[...] = jnp.zeros_like(dw_ref)
            db_ref[...] = jnp.zeros_like(db_ref)
            ds_ref[...] = jnp.zeros_like(ds_ref)
            dwu_acc[...] = jnp.zeros_like(dwu_acc)

        dwu_acc[...] += lax.dot_general(p_ref[...], dy_ref[...], (((0,), (0,)), ((), ())), preferred_element_type=F32)

        @pl.when(i == nb - 1)
        def _():
            for s in range(N_CHIPS):
                dwu_ref[s] = dwu_acc[:, s * ns:(s + 1) * ns].astype(BF16)

        pos = (nb - 1 - i) * tm + lax.broadcasted_iota(jnp.int32, (tm, 1), 0)
        dp = None
        for s in range(N_CHIPS):
            q = lax.dot_general(dy_ref[:, s * ns:(s + 1) * ns], wu_ref[s], (((1,), (1,)), ((), ())), preferred_element_type=F32)
            dp = q if dp is None else dp + q
        for k, win in enumerate(POOL_WINDOWS):
            cols = slice(k * gw, (k + 1) * gw)
            zk = z_ref[:, cols]
            dpk = dp[:, cols]
            wk = w_ref[k].astype(BF16)
            zp = jnp.dot(zk, wk, preferred_element_type=F32) + b_ref[:, cols]
            ds_ref[:, cols] += _colsum(dpk * zp)
            dzp = dpk * s_ref[:, cols]
            db_ref[:, cols] += _colsum(dzp)
            dzpb = dzp.astype(BF16)
            dz = lax.dot_general(dzpb, wk, (((1,), (1,)), ((), ())), preferred_element_type=F32)
            dw_ref[k] += lax.dot_general(zk, dzpb, (((0,), (0,)), ((), ())), preferred_element_type=F32)
            cnt = jnp.minimum(pos + 1, win).astype(F32)
            r = dz / cnt
            ext[0:tm, cols] = r
            acc = r - dz
            for j in range(1, win):
                acc = acc + ext[j:j + tm, cols]
            du_ref[:, cols] = acc.astype(BF16)
        ext[tm:tm + POOL_HALO, :] = ext[0:POOL_HALO, :]

    rev = lambda i: (nb - 1 - i, 0)
    full3 = lambda i: (0, 0, 0)
    return _pcall(
        body, name=name, grid=(nb,),
        in_specs=[pl.BlockSpec((tm, N_CHIPS * ns), rev), pl.BlockSpec((tm, pw), rev), pl.BlockSpec((tm, pw), rev),
                  pl.BlockSpec(pool_w.shape, full3), pl.BlockSpec(pool_b.shape, lambda i: (0, 0)),
                  pl.BlockSpec(pool_scale.shape, lambda i: (0, 0)), pl.BlockSpec(w_up.shape, full3), ANY_SPEC],
        out_specs=[pl.BlockSpec((tm, pw), rev), pl.BlockSpec(pool_w.shape, full3),
                   pl.BlockSpec((1, pw), lambda i: (0, 0)), pl.BlockSpec((1, pw), lambda i: (0, 0)), pl.BlockSpec(w_up.shape, full3)],
        out_shape=[jax.ShapeDtypeStruct(duin.shape, BF16), jax.ShapeDtypeStruct(pool_w.shape, F32),
                   jax.ShapeDtypeStruct((1, pw), F32), jax.ShapeDtypeStruct((1, pw), F32), jax.ShapeDtypeStruct(w_up.shape, BF16)],
        scratch_shapes=[pltpu.VMEM((tm + POOL_HALO, pw), F32), pltpu.VMEM((pw, N_CHIPS * ns), F32)], input_output_aliases={7: 0},
        compiler_params=_params("arbitrary"))(dy, z, p, pool_w, pool_b, pool_scale, w_up, duin)


def _ssm_disc(lrl, li, ldt):
    lr = -jnp.exp(lrl)
    dt = jnp.exp(ldt)
    mag = jnp.exp(lr * dt)
    ang = li * dt
    ab_re = mag * jnp.cos(ang)
    ab_im = mag * jnp.sin(ang)
    num_re = ab_re - 1.0
    num_im = ab_im
    den = lr * lr + li * li
    f_re = (num_re * lr + num_im * li) / den
    f_im = (num_im * lr - num_re * li) / den
    return lr, dt, mag, ang, ab_re, ab_im, num_re, num_im, den, f_re, f_im


def _ssm_prep(name, lrl, li, ldt, b_re, b_im):
    gn, h = b_re.shape

    def body(lrl_ref, li_ref, ldt_ref, br_ref, bi_ref, lrdt_ref, ang_ref, bbr_ref, bbi_ref):
        lr, dt, _, ang, _, _, _, _, _, f_re, f_im = _ssm_disc(lrl_ref[...], li_ref[...], ldt_ref[...])
        lrdt_ref[...] = lr * dt
        ang_ref[...] = ang
        br, bi = br_ref[...], bi_ref[...]
        bbr_ref[...] = f_re * br - f_im * bi
        bbi_ref[...] = f_re * bi + f_im * br

    col = jax.ShapeDtypeStruct((gn, 1), F32)
    mat = jax.ShapeDtypeStruct((gn, h), F32)
    return _pcall(body, name=name, out_shape=[col, col, mat, mat])(lrl, li, ldt, b_re, b_im)


def _ssm_param_bwd(name, lrl, li, ldt, b_re, b_im, g_abre, g_abim, g_bbre, g_bbim):
    gn, h = b_re.shape

    def body(lrl_ref, li_ref, ldt_ref, br_ref, bi_ref, gar_ref, gai_ref, gbr_ref, gbi_ref,
             dlrl_ref, dli_ref, dldt_ref, dbr_ref, dbi_ref):
        li_v = li_ref[...]
        lr, dt, mag, ang, ab_re, ab_im, num_re, num_im, den, f_re, f_im = _ssm_disc(lrl_ref[...], li_v, ldt_ref[...])
        br, bi = br_ref[...], bi_ref[...]
        gbr, gbi = gbr_ref[...], gbi_ref[...]
        g_fre = jnp.sum(gbr * br + gbi * bi, axis=1, keepdims=True)
        g_fim = jnp.sum(gbi * br - gbr * bi, axis=1, keepdims=True)
        dbr_ref[...] = gbr * f_re + gbi * f_im
        dbi_ref[...] = gbi * f_re - gbr * f_im
        g_num_re = (g_fre * lr - g_fim * li_v) / den
        g_num_im = (g_fre * li_v + g_fim * lr) / den
        g_den = -(g_fre * f_re + g_fim * f_im) / den
        g_lr = (g_fre * num_re + g_fim * num_im) / den + g_den * 2.0 * lr
        g_li = (g_fre * num_im - g_fim * num_re) / den + g_den * 2.0 * li_v
        g_are = gar_ref[...] + g_num_re
        g_aim = gai_ref[...] + g_num_im
        g_mag = g_are * jnp.cos(ang) + g_aim * jnp.sin(ang)
        g_ang = g_aim * ab_re - g_are * ab_im
        g_lrdt = g_mag * mag
        g_lr = g_lr + g_lrdt * dt
        g_dt = g_lrdt * lr + g_ang * li_v
        g_li = g_li + g_ang * dt
        dlrl_ref[...] = g_lr * lr
        dli_ref[...] = g_li
        dldt_ref[...] = g_dt * dt

    col = jax.ShapeDtypeStruct((gn, 1), F32)
    mat = jax.ShapeDtypeStruct((gn, h), F32)
    return _pcall(body, name=name, out_shape=[col, col, col, mat, mat])(lrl, li, ldt, b_re, b_im, g_abre, g_abim, g_bbre, g_bbim)


def _pow_rows(lrdt, ang, k):
    mag = jnp.exp(k * lrdt)
    return mag * jnp.cos(k * ang), mag * jnp.sin(k * ang)


def _ssm_chunk(t):
    return 512 if t >= 4096 else 256 if t >= 2048 else 128


def _to_segments(dst, srcs, jn):
    for q, src in enumerate(srcs):
        for j in range(jn):
            dst[8 * j:8 * j + 8, 128 * q:128 * (q + 1)] = src[pl.ds(j, 8, stride=jn), :]


def _from_segments(dst, src, q, jn, dtype):
    for s in range(8):
        dst[s * jn:(s + 1) * jn, 128 * q:128 * (q + 1)] = src[q, pl.ds(s, jn, stride=8), :].astype(dtype)


def _fill_rows8(dst_re, dst_im, v_re, v_im):
    for j in range(v_re.shape[0]):
        dst_re[8 * j:8 * j + 8, :] = jnp.broadcast_to(v_re[j:j + 1, :], (8, v_re.shape[1]))
        dst_im[8 * j:8 * j + 8, :] = jnp.broadcast_to(v_im[j:j + 1, :], (8, v_im.shape[1]))


def _cmul(ar, ai, br, bi):
    return ar * br - ai * bi, ar * bi + ai * br


def _cmul_conj(ar, ai, br, bi):
    return ar * br + ai * bi, ar * bi - ai * br


def _ssm_fwd(name, uin, lrdt, ang, bb_re, bb_im, cc_re, cc_im, d_skip, sw):
    t = uin.shape[0]
    gn = lrdt.shape[1]
    lc = _ssm_chunk(t)
    jn = lc // 8
    ub, sb = sw // SSM_BLOCKS, gn // SSM_BLOCKS
    nq = sw // 128
    assert ub == 128 and nq == SSM_BLOCKS

    def body(u_ref, lrdt_ref, ang_ref, bbr_ref, bbi_ref, ccr_ref, cci_ref, d_ref, y_ref, ge_ref, sre_ref, sim_ref,
             p_re, p_im, a_re, a_im, up, yp, cst_re, cst_im, car_re, car_im):
        i = pl.program_id(0)
        lrdt_v, ang_v = lrdt_ref[...], ang_ref[...]

        @pl.when(i == 0)
        def _():
            k = (lax.broadcasted_iota(jnp.int32, (jn, 1), 0) + 1).astype(F32)
            _fill_rows8(p_re, p_im, *_pow_rows(lrdt_v, ang_v, k))
            car_re[...] = jnp.zeros_like(car_re)
            car_im[...] = jnp.zeros_like(car_im)

        for q in range(nq):
            yp[q] = u_ref[:, 128 * q:128 * (q + 1)].astype(F32)
        _to_segments(up, [yp.at[q] for q in range(nq)], jn)
        u = up[...]
        ubf = u.astype(BF16)
        for q in range(SSM_BLOCKS):
            uq = ubf[:, q * ub:(q + 1) * ub]
            a_re[:, q * sb:(q + 1) * sb] = jnp.dot(uq, bbr_ref[q], preferred_element_type=F32)
            a_im[:, q * sb:(q + 1) * sb] = jnp.dot(uq, bbi_ref[q], preferred_element_type=F32)
        a1r, a1i = _pow_rows(lrdt_v, ang_v, 1.0)
        ajr, aji = _pow_rows(lrdt_v, ang_v, float(jn))
        for q in range(0, SSM_BLOCKS, 2):
            cols = slice(q * sb, (q + 2) * sb)
            ar8 = jnp.broadcast_to(a1r[:, cols], (8, 2 * sb))
            ai8 = jnp.broadcast_to(a1i[:, cols], (8, 2 * sb))

            def step(j, carry, cols=cols, ar8=ar8, ai8=ai8):
                sr, si = carry
                rows = pl.ds(pl.multiple_of(j * 8, 8), 8)
                mr, mi = _cmul(ar8, ai8, sr, si)
                nr, ni = mr + a_re[rows, cols], mi + a_im[rows, cols]
                a_re[rows, cols] = nr
                a_im[rows, cols] = ni
                return nr, ni

            lax.fori_loop(1, jn, step, (a_re[0:8, cols], a_im[0:8, cols]), unroll=8)
        er, ei = a_re[lc - 8:lc, :], a_im[lc - 8:lc, :]
        gr, gi = car_re[...], car_im[...]
        for s in range(8):
            cst_re[s:s + 1, :] = gr
            cst_im[s:s + 1, :] = gi
            mr, mi = _cmul(ajr, aji, gr, gi)
            gr, gi = mr + er[s:s + 1, :], mi + ei[s:s + 1, :]
        car_re[...] = gr
        car_im[...] = gi
        for q in range(SSM_BLOCKS):
            cols = slice(q * sb, (q + 1) * sb)
            cr = jnp.tile(cst_re[:, cols], (jn, 1))
            ci = jnp.tile(cst_im[:, cols], (jn, 1))
            mr, mi = _cmul(p_re[:, cols], p_im[:, cols], cr, ci)
            srb, sib = (a_re[:, cols] + mr).astype(BF16), (a_im[:, cols] + mi).astype(BF16)
            sre_ref[:, cols] = srb
            sim_ref[:, cols] = sib
            ycols = slice(q * ub, (q + 1) * ub)
            y = (jnp.dot(srb, ccr_ref[q], preferred_element_type=F32) - jnp.dot(sib, cci_ref[q], preferred_element_type=F32)
                 + d_ref[:, ycols] * u[:, ycols])
            yp[q] = y
            _from_segments(y_ref, yp, q, jn, F32)
            yt = y_ref[:, ycols]
            ge_ref[:, ycols] = (0.5 * yt * (1.0 + lax.erf(yt * (1.0 / math.sqrt(2.0))))).astype(BF16)

    row = lambda i: (0, 0)
    blk3 = lambda i: (0, 0, 0)
    return _pcall(
        body, name=name, grid=(t // lc,),
        in_specs=[pl.BlockSpec((lc, sw), lambda i: (i, 1)), pl.BlockSpec((1, gn), row), pl.BlockSpec((1, gn), row),
                  pl.BlockSpec(bb_re.shape, blk3), pl.BlockSpec(bb_im.shape, blk3),
                  pl.BlockSpec(cc_re.shape, blk3), pl.BlockSpec(cc_im.shape, blk3), pl.BlockSpec((1, sw), row)],
        out_specs=[pl.BlockSpec((lc, sw), lambda i: (i, 0)), pl.BlockSpec((lc, sw), lambda i: (i, 0)),
                   pl.BlockSpec((lc, gn), lambda i: (i, 0)), pl.BlockSpec((lc, gn), lambda i: (i, 0))],
        out_shape=[jax.ShapeDtypeStruct((t, sw), F32), jax.ShapeDtypeStruct((t, sw), BF16),
                   jax.ShapeDtypeStruct((t, gn), BF16), jax.ShapeDtypeStruct((t, gn), BF16)],
        scratch_shapes=[pltpu.VMEM((lc, gn), F32), pltpu.VMEM((lc, gn), F32), pltpu.VMEM((lc, gn), F32), pltpu.VMEM((lc, gn), F32),
                        pltpu.VMEM((lc, sw), F32), pltpu.VMEM((nq, lc, 128), F32),
                        pltpu.VMEM((8, gn), F32), pltpu.VMEM((8, gn), F32), pltpu.VMEM((1, gn), F32), pltpu.VMEM((1, gn), F32)],
        compiler_params=_params("arbitrary"))(uin, lrdt, ang, bb_re, bb_im, cc_re, cc_im, d_skip)


def _ssm_bwd(name, dge, y, uin, s_re, s_im, lrdt, ang, bbt_re, bbt_im, cct_re, cct_im, d_skip, sw, duin):
    t = uin.shape[0]
    gn = lrdt.shape[1]
    lc = _ssm_chunk(t)
    nb = t // lc
    jn = lc // 8
    ub, sb = sw // SSM_BLOCKS, gn // SSM_BLOCKS
    nq = sw // 128
    tail = 16

    def body(dge_ref, y_ref, u_ref, sre_ref, sim_ref, tre_ref, tim_ref, lrdt_ref, ang_ref, btr_ref, bti_ref, ctr_ref, cti_ref,
             d_ref, duin_ref, du_ref, dar_ref, dai_ref, dbr_ref, dbi_ref, dcr_ref, dci_ref, dd_ref,
             q_re, q_im, a_re, a_im, dyp, up, dys, us, dup, cst_re, cst_im, sp_re, sp_im, car_re, car_im):
        i = pl.program_id(0)
        lrdt_v, ang_v = lrdt_ref[...], ang_ref[...]

        @pl.when(i == 0)
        def _():
            k = (jn - lax.broadcasted_iota(jnp.int32, (jn, 1), 0)).astype(F32)
            _fill_rows8(q_re, q_im, *_pow_rows(lrdt_v, ang_v, k))
            car_re[...] = jnp.zeros_like(car_re)
            car_im[...] = jnp.zeros_like(car_im)
            for r in (dar_ref, dai_ref, dbr_ref, dbi_ref, dcr_ref, dci_ref, dd_ref):
                r[...] = jnp.zeros_like(r)

        yv = y_ref[...]
        ut = u_ref[...].astype(F32)
        cdf =0.5 * (1.0 + lax.erf(yv * (1.0 / math.sqrt(2.0))))
        pdf = jnp.exp(-0.5 * yv * yv) * (1.0 / math.sqrt(2.0 * math.pi))
        dyt = dge_ref[...].astype(F32) * (cdf + yv * pdf)
        dd_ref[...] += _colsum(dyt * ut)
        for q in range(nq):
            dys[q] = dyt[:, 128 * q:128 * (q + 1)]
            us[q] = ut[:, 128 * q:128 * (q + 1)]
        _to_segments(dyp, [dys.at[q] for q in range(nq)], jn)
        _to_segments(up, [us.at[q] for q in range(nq)], jn)
        dy = dyp[...]
        u = up[...]
        dyb = dy.astype(BF16)
        ubf = u.astype(BF16)
        for q in range(SSM_BLOCKS):
            dq = dyb[:, q * ub:(q + 1) * ub]
            a_re[:, q * sb:(q + 1) * sb] = jnp.dot(dq, ctr_ref[q], preferred_element_type=F32)
            a_im[:, q * sb:(q + 1) * sb] = -jnp.dot(dq, cti_ref[q], preferred_element_type=F32)
        a1r, a1i = _pow_rows(lrdt_v, ang_v, 1.0)
        ajr, aji = _pow_rows(lrdt_v, ang_v, float(jn))
        for q in range(0, SSM_BLOCKS, 2):
            cols = slice(q * sb, (q + 2) * sb)
            ar8 = jnp.broadcast_to(a1r[:, cols], (8, 2 * sb))
            ai8 = jnp.broadcast_to(a1i[:, cols], (8, 2 * sb))

            def step(jj, carry, cols=cols, ar8=ar8, ai8=ai8):
                sr, si = carry
                rows = pl.ds(pl.multiple_of((jn - 2 - jj) * 8, 8), 8)
                mr, mi = _cmul_conj(ar8, ai8, sr, si)
                nr, ni = mr + a_re[rows, cols], mi + a_im[rows, cols]
                a_re[rows, cols] = nr
                a_im[rows, cols] = ni
                return nr, ni

            lax.fori_loop(0, jn - 1, step, (a_re[lc - 8:lc, cols], a_im[lc - 8:lc, cols]), unroll=8)
        er, ei = a_re[0:8, :], a_im[0:8, :]
        hr, hi = car_re[...], car_im[...]
        for s in range(7, -1, -1):
            cst_re[s:s + 1, :] = hr
            cst_im[s:s + 1, :] = hi
            mr, mi = _cmul_conj(ajr, aji, hr, hi)
            hr, hi = mr + er[s:s + 1, :], mi + ei[s:s + 1, :]
        car_re[...] = hr
        car_im[...] = hi
        first = (i == nb - 1).astype(F32)
        sp_re[0:tail, :] = tre_ref[...].astype(F32) * (1.0 - first)
        sp_im[0:tail, :] = tim_ref[...].astype(F32) * (1.0 - first)
        sp_re[tail:tail + 8, :] = sre_ref[lc - tail:lc, :].astype(F32)[tail - 8:tail]
        sp_im[tail:tail + 8, :] = sim_ref[lc - tail:lc, :].astype(F32)[tail - 8:tail]
        tn_dims = (((0,), (0,)), ((), ()))
        for q in range(SSM_BLOCKS):
            cols = slice(q * sb, (q + 1) * sb)
            ycols = slice(q * ub, (q + 1) * ub)
            cr = jnp.tile(cst_re[:, cols], (jn, 1))
            ci = jnp.tile(cst_im[:, cols], (jn, 1))
            mr, mi = _cmul_conj(q_re[:, cols], q_im[:, cols], cr, ci)
            lam_r, lam_i = a_re[:, cols] + mr, a_im[:, cols] + mi
            s_r, s_i = sre_ref[:, cols], sim_ref[:, cols]
            p0r, p0i = sp_re[tail - 1:tail + 7, cols], sp_im[tail - 1:tail + 7, cols]
            l0r, l0i, l1r, l1i = lam_r[0:8], lam_i[0:8], lam_r[8:lc], lam_i[8:lc]
            pvr, pvi = s_r.astype(F32)[0:lc - 8], s_i.astype(F32)[0:lc - 8]
            dar_ref[:, cols] += _colsum(l1r * pvr + l1i * pvi) + _colsum(l0r * p0r + l0i * p0i)
            dai_ref[:, cols] += _colsum(l1i * pvr - l1r * pvi) + _colsum(l0i * p0r - l0r * p0i)
            lrb, lib = lam_r.astype(BF16), lam_i.astype(BF16)
            dup[q] = (jnp.dot(lrb, btr_ref[q], preferred_element_type=F32)
                      + jnp.dot(lib, bti_ref[q], preferred_element_type=F32) + d_ref[:, ycols] * dy[:, ycols])
            _from_segments(du_ref, dup, q, jn, BF16)
            uq = ubf[:, ycols]
            dbr_ref[q] += lax.dot_general(uq, lrb, tn_dims, preferred_element_type=F32)
            dbi_ref[q] += lax.dot_general(uq, lib, tn_dims, preferred_element_type=F32)
            dq = dyb[:, ycols]
            dcr_ref[q] += lax.dot_general(s_r.astype(BF16), dq, tn_dims, preferred_element_type=F32)
            dci_ref[q] -= lax.dot_general(s_i.astype(BF16), dq, tn_dims, preferred_element_type=F32)

    rev = lambda i: (nb - 1 - i, 0)
    tailmap = lambda i: (jnp.maximum((nb - 1 - i) * (lc // tail) - 1, 0), 0)
    row = lambda i: (0, 0)
    blk3 = lambda i: (0, 0, 0)
    return _pcall(
        body, name=name, grid=(nb,),
        in_specs=[pl.BlockSpec((lc, sw), rev), pl.BlockSpec((lc, sw), rev), pl.BlockSpec((lc, sw), lambda i: (nb - 1 - i, 1)),
                  pl.BlockSpec((lc, gn), rev), pl.BlockSpec((lc, gn), rev),
                  pl.BlockSpec((tail, gn), tailmap), pl.BlockSpec((tail, gn), tailmap),
                  pl.BlockSpec((1, gn), row), pl.BlockSpec((1, gn), row),
                  pl.BlockSpec(bbt_re.shape, blk3), pl.BlockSpec(bbt_im.shape, blk3),
                  pl.BlockSpec(cct_re.shape, blk3), pl.BlockSpec(cct_im.shape, blk3), pl.BlockSpec((1, sw), row), ANY_SPEC],
        out_specs=[pl.BlockSpec((lc, sw), lambda i: (nb - 1 - i, 1)), pl.BlockSpec((1, gn), row), pl.BlockSpec((1, gn), row),
                   pl.BlockSpec((SSM_BLOCKS, ub, sb), blk3), pl.BlockSpec((SSM_BLOCKS, ub, sb), blk3),
                   pl.BlockSpec((SSM_BLOCKS, sb, ub), blk3), pl.BlockSpec((SSM_BLOCKS, sb, ub), blk3),
                   pl.BlockSpec((1, sw), row)],
        out_shape=[jax.ShapeDtypeStruct(duin.shape, BF16), jax.ShapeDtypeStruct((1, gn), F32), jax.ShapeDtypeStruct((1, gn), F32),
                   jax.ShapeDtypeStruct((SSM_BLOCKS, ub, sb), F32), jax.ShapeDtypeStruct((SSM_BLOCKS, ub, sb), F32),
                   jax.ShapeDtypeStruct((SSM_BLOCKS, sb, ub), F32), jax.ShapeDtypeStruct((SSM_BLOCKS, sb, ub), F32),
                   jax.ShapeDtypeStruct((1, sw), F32)],
        scratch_shapes=[pltpu.VMEM((lc, gn), F32), pltpu.VMEM((lc, gn), F32), pltpu.VMEM((lc, gn), F32), pltpu.VMEM((lc, gn), F32),
                        pltpu.VMEM((lc, sw), F32), pltpu.VMEM((lc, sw), F32),
                        pltpu.VMEM((nq, lc, 128), F32), pltpu.VMEM((nq, lc, 128), F32), pltpu.VMEM((nq, lc, 128), F32),
                        pltpu.VMEM((8, gn), F32), pltpu.VMEM((8, gn), F32),
                        pltpu.VMEM((tail + 8, gn), F32), pltpu.VMEM((tail + 8, gn), F32),
                        pltpu.VMEM((1, gn), F32), pltpu.VMEM((1, gn), F32)],
        input_output_aliases={14: 0},
        compiler_params=_params("arbitrary"))(dge, y, uin, s_re, s_im, s_re, s_im, lrdt, ang,
                                               bbt_re, bbt_im, cct_re, cct_im, d_skip, duin)


def _blockdiag_b(bb, sw):
    gpb = (sw // SSM_GROUP) // SSM_BLOCKS
    b4 = bb.reshape(SSM_BLOCKS, gpb, SSM_STATE, SSM_GROUP)
    eye = jnp.eye(gpb, dtype=bb.dtype)
    out = jnp.einsum('qgnh,gk->qghkn', b4, eye)
    return out.reshape(SSM_BLOCKS, gpb * SSM_GROUP, gpb * SSM_STATE)


def _blockdiag_c(cc, sw):
    gpb = (sw // SSM_GROUP) // SSM_BLOCKS
    c4 = cc.reshape(SSM_BLOCKS, gpb, SSM_GROUP, SSM_STATE)
    eye = jnp.eye(gpb, dtype=cc.dtype)
    out = jnp.einsum('qghn,gk->qgnkh', c4, eye)
    return out.reshape(SSM_BLOCKS, gpb * SSM_STATE, gpb * SSM_GROUP)


def _diag_of_b(dbb, sw):
    gpb = (sw // SSM_GROUP) // SSM_BLOCKS
    d5 = dbb.reshape(SSM_BLOCKS, gpb, SSM_GROUP, gpb, SSM_STATE)
    return jnp.einsum('qghgn->qgnh', d5).reshape(SSM_BLOCKS * gpb * SSM_STATE, SSM_GROUP)


def _diag_of_c(dcc, sw):
    gpb = (sw // SSM_GROUP) // SSM_BLOCKS
    d5 = dcc.reshape(SSM_BLOCKS, gpb, SSM_STATE, gpb, SSM_GROUP)
    return jnp.einsum('qgngh->qghn', d5).reshape(SSM_BLOCKS * gpb, SSM_GROUP, SSM_STATE)


def _ada_fwd(name, c_all, w_sh, b_sh):
    nb, d = c_all.shape
    ncol = w_sh.shape[1]
    tn = _pick(ncol, 768)

    def body(c_ref, w_ref, b_ref, o_ref):
        cv = c_ref[...]
        sil = cv * _sigmoid(cv)
        o_ref[...] = jnp.dot(sil, w_ref[...], preferred_element_type=F32, precision=lax.Precision.HIGHEST) + b_ref[...]

    return _pcall(body, name=name, grid=(ncol // tn,),
                  in_specs=[pl.BlockSpec((nb, d), lambda j: (0, 0)), pl.BlockSpec((d, tn), lambda j: (0, j)),
                            pl.BlockSpec((1, tn), lambda j: (0, j))],
                  out_specs=pl.BlockSpec((nb, tn), lambda j: (0, j)),
                  out_shape=jax.ShapeDtypeStruct((nb, ncol), F32), compiler_params=_params("parallel"))(c_all, w_sh, b_sh)


def _ada_bwd_adamw(name, c_all, dmod_sh, w, m, v):
    nb, d = c_all.shape
    ncol = dmod_sh.shape[1]
    tr, tn = _pick(d, 512), _pick(ncol, 768)

    def body(c_ref, g_ref, w_ref, m_ref, v_ref, go_ref, d_ref, mo_ref, vo_ref):
        cv = c_ref[...]
        sil = cv * _sigmoid(cv)
        gv = lax.dot_general(sil, g_ref[...], (((0,), (0,)), ((), ())), preferred_element_type=F32,
                             precision=lax.Precision.HIGHEST)
        mn = ADAM_B1 * m_ref[...] + (1.0 - ADAM_B1) * gv
        vn = ADAM_B2 * v_ref[...] + (1.0 - ADAM_B2) * (gv * gv)
        m_hat = mn / (1.0 - ADAM_B1 ** ADAM_STEP)
        v_hat = vn / (1.0 - ADAM_B2 ** ADAM_STEP)
        go_ref[...] = gv
        d_ref[...] = -ADAM_LR * (m_hat / (jnp.sqrt(v_hat) + ADAM_EPS) + ADAM_WD * w_ref[...])
        mo_ref[...] = mn
        vo_ref[...] = vn

    tile = pl.BlockSpec((tr, tn), lambda i, j: (i, j))
    return _pcall(body, name=name, grid=(d // tr, ncol // tn),
                  in_specs=[pl.BlockSpec((nb, tr), lambda i, j: (0, i)), pl.BlockSpec((nb, tn), lambda i, j: (0, j)), tile, tile, tile],
                  out_specs=[tile] * 4, out_shape=[jax.ShapeDtypeStruct((d, ncol), F32)] * 4,
                  compiler_params=_params("parallel", "parallel"))(c_all, dmod_sh, w, m, v)


def _place():
    return lax.axis_index("x"), lax.axis_index("y"), lax.axis_index("c")


def _allgather_small(name, blk, deps=()):
    m_per, n = blk.shape

    def body(x_ref, *rest):
        out_ref, send_sems, recv_sems, local_sem = rest[len(deps):]
        x, y, c = _place()
        me, sibling = (x, y, c), (x, y, 1 - c)
        chips = [(1 - x, y), (x, 1 - y), (1 - x, 1 - y)]

        def rows(px, py, pc):
            return out_ref.at[pl.ds((4 * px + 2 * py + pc) * m_per, m_per), :]

        def copy(k, block, to, src=None):
            return pltpu.make_async_remote_copy(
                src_ref=rows(*block) if src is None else src, dst_ref=rows(*block),
                send_sem=send_sems.at[k], recv_sem=recv_sems.at[k], device_id=to, device_id_type=MESH)

        mine = pltpu.make_async_copy(x_ref, rows(*me), local_sem)
        mine.start()
        first = [copy(0, me, sibling, src=x_ref)]
        first += [copy(1 + j, me, (*chip, c), src=x_ref) for j, chip in enumerate(chips)]
        for cp in first:
            cp.start()
        passed = [copy(4 + j, (*chip, c), sibling) for j, chip in enumerate(chips)]
        for j, chip in enumerate(chips):
            copy(1 + j, (*chip, c), me).wait_recv()
            passed[j].start()
        copy(0, sibling, me).wait_recv()
        for j, chip in enumerate(chips):
            copy(4 + j, (*chip, 1 - c), me).wait_recv()
        for cp in first + passed:
            cp.wait_send()
        mine.wait()

    return _pcall(body, name=name, out_shape=jax.ShapeDtypeStruct((N_DEV * m_per, n), blk.dtype),
                  in_specs=[pl.BlockSpec(memory_space=pltpu.VMEM)] + [ANY_SPEC] * len(deps),
                  out_specs=pl.BlockSpec(memory_space=pltpu.VMEM),
                  scratch_shapes=[pltpu.SemaphoreType.DMA((7,)), pltpu.SemaphoreType.DMA((7,)), pltpu.SemaphoreType.DMA],
                  compiler_params=pltpu.CompilerParams(vmem_limit_bytes=VMEM_LIMIT))(blk, *deps)


def _other_chips(x, y):
    return [(1 - x, y), (x, 1 - y), (1 - x, 1 - y)]


def _place_shards(shards, s_me):
    return [lax.dynamic_update_slice(lax.empty((N_CHIPS,) + s.shape, s.dtype), s[None], (s_me, 0, 0)) for s in shards]


def _ag_copy(src, land, send, recv, wi, j, chip, x, y, c, tc, both):
    hr = src.shape[0] // 2
    half = pl.ds(pl.multiple_of(c * hr, 16), hr)
    k = 3 * wi + j
    return pltpu.make_async_remote_copy(
        src_ref=src.at[half, :], dst_ref=land.at[2 * x + y, half, :],
        send_sem=send.at[2 * k + tc if both else k], recv_sem=recv.at[2 * k + c if both else k],
        device_id=(chip[0], chip[1], tc), device_id_type=MESH)


def _ag_targets(c, both):
    return (0, 1) if both else (c,)


def _ag_start(name, shards, lands, groups, direct, deps=()):
    nw, ng, nd = len(shards), len(groups), len(deps)

    def body(*refs):
        src, land = refs[:nw], refs[nw:2 * nw]
        sems = refs[2 * nw + nd:2 * nw + nd + 2 * ng]
        token = refs[-1]
        x, y, c = _place()
        for gi, grp in enumerate(groups):
            for wi, w in enumerate(grp):
                for j, chip in enumerate(_other_chips(x, y)):
                    for tc in _ag_targets(c, direct[gi]):
                        _ag_copy(src[w], land[w], sems[2 * gi], sems[2 * gi + 1], wi, j, chip, x, y, c, tc, direct[gi]).start()
        token[...] = jnp.zeros_like(token)

    sem_shapes = []
    for gi, grp in enumerate(groups):
        sem_shapes += [pltpu.SemaphoreType.DMA(((6 if direct[gi] else 3) * len(grp),))] * 2
    out_shape = sem_shapes + [pltpu.HBM(s.shape, s.dtype) for s in shards] + [pltpu.HBM(l.shape, l.dtype) for l in lands]
    out_shape += [jax.ShapeDtypeStruct((8, 128), F32)]
    res = _pcall(body, name=name, out_shape=out_shape, in_specs=[HBM_SPEC] * (2 * nw) + [ANY_SPEC] * nd,
                 out_specs=[SEM_SPEC] * (2 * ng) + [HBM_SPEC] * (2 * nw) + [pl.BlockSpec(memory_space=pltpu.VMEM)],
                 input_output_aliases={i: 2 * ng + i for i in range(2 * nw)},
                 compiler_params=pltpu.CompilerParams(has_side_effects=EFFECT))(
                     *[_hbm(s) for s in shards], *[_hbm(l) for l in lands], *deps)
    sems = [(res[2 * gi], res[2 * gi + 1]) for gi in range(ng)]
    return sems, list(res[2 * ng:2 * ng + nw]), list(res[2 * ng + nw:2 * ng + 2 * nw]), res[-1]


def _ag_wait(name, shards, lands, send, recv, both, after):
    n = len(shards)

    def body(*refs):
        src, land = refs[:n], refs[n:2 * n]
        send_sem, recv_sem = refs[2 * n], refs[2 * n + 1]
        x, y, c = _place()
        for wi in range(n):
            for j, chip in enumerate(_other_chips(x, y)):
                for tc in _ag_targets(c, both):
                    _ag_copy(src[wi], land[wi], send_sem, recv_sem, wi, j, chip, x, y, c, tc, both).wait_send()
                    _ag_copy(src[wi], land[wi], send_sem, recv_sem, wi, j, chip, chip[0], chip[1], tc, c, both).wait_recv()

    res = _pcall(body, name=name, out_shape=[pltpu.HBM(a.shape, a.dtype) for a in list(shards) + list(lands)],
                 in_specs=[HBM_SPEC] * (2 * n) + [SEM_SPEC, SEM_SPEC] + [ANY_SPEC] * len(after), out_specs=[HBM_SPEC] * (2 * n),
                 input_output_aliases={i: i for i in range(2 * n)},
                 compiler_params=pltpu.CompilerParams(has_side_effects=EFFECT))(*shards, *lands, send, recv, *after)
    return list(res[n:])


def _ag_forward(name, lands):
    n = len(lands)

    def body(*refs):
        out = refs[n:2 * n]
        send, recv = refs[2 * n], refs[2 * n + 1]
        x, y, c = _place()
        sib = (x, y, 1 - c)
        cps = []
        for wi in range(n):
            hr = out[wi].shape[1] // 2
            for j, (cx, cy) in enumerate(_other_chips(x, y)):
                got = out[wi].at[2 * cx + cy, pl.ds(pl.multiple_of(c * hr, 16), hr), :]
                cp = pltpu.make_async_remote_copy(src_ref=got, dst_ref=got, send_sem=send.at[3 * wi + j], recv_sem=recv.at[3 * wi + j],
                                                  device_id=sib, device_id_type=MESH)
                cp.start()
                cps.append(cp)
        for wi in range(n):
            hr = out[wi].shape[1] // 2
            for j, (cx, cy) in enumerate(_other_chips(x, y)):
                got = out[wi].at[2 * cx + cy, pl.ds(pl.multiple_of((1 - c) * hr, 16), hr), :]
                pltpu.make_async_remote_copy(src_ref=got, dst_ref=got, send_sem=send.at[3 * wi + j], recv_sem=recv.at[3 * wi + j],
                                             device_id=sib, device_id_type=MESH).wait_recv()
        for cp in cps:
            cp.wait_send()

    res = _pcall(body, name=name, out_shape=[jax.ShapeDtypeStruct(l.shape, l.dtype) for l in lands],
                 in_specs=[ANY_SPEC] * n, out_specs=[ANY_SPEC] * n, input_output_aliases={i: i for i in range(n)},
                 scratch_shapes=[pltpu.SemaphoreType.DMA((3 * n,)), pltpu.SemaphoreType.DMA((3 * n,))])(*lands)
    return list(res)


def _peers(x, y, c):
    offs = [(dx, dy, dc) for dx in (0, 1) for dy in (0, 1) for dc in (0, 1)][1:]
    return [(1 - x if dx else x, 1 - y if dy else y, 1 - c if dc else c) for dx, dy, dc in offs]


def _rs_copy(g_ref, land_ref, send, recv, wi, k, to, sender):
    hr = g_ref.shape[1] // 2
    return pltpu.make_async_remote_copy(
        src_ref=g_ref.at[2 * to[0] + to[1], pl.ds(pl.multiple_of(to[2] * hr, 16), hr), :], dst_ref=land_ref.at[sender],
        send_sem=send.at[7 * wi + k], recv_sem=recv.at[7 * wi + k], device_id=to, device_id_type=MESH)


def _rs_start(name, gs):
    n = len(gs)
    lands = [lax.empty((N_DEV, g.shape[1] // 2, g.shape[2]), BF16) for g in gs]

    def body(*refs):
        g, land = refs[:n], refs[n:2 * n]
        send, recv = refs[2 * n], refs[2 * n + 1]
        token = refs[-1]
        x, y, c = _place()
        me = 4 * x + 2 * y + c
        for wi in range(n):
            for k, to in enumerate(_peers(x, y, c)):
                _rs_copy(g[wi], land[wi], send, recv, wi, k, to, me).start()
        token[...] = jnp.zeros_like(token)

    out_shape = [pltpu.SemaphoreType.DMA((7 * n,))] * 2 + [pltpu.HBM(a.shape, a.dtype) for a in list(gs) + lands]
    out_shape += [jax.ShapeDtypeStruct((8, 128), F32)]
    res = _pcall(body, name=name, out_shape=out_shape, in_specs=[HBM_SPEC] * (2 * n),
                 out_specs=[SEM_SPEC] * 2 + [HBM_SPEC] * (2 * n) + [pl.BlockSpec(memory_space=pltpu.VMEM)],
                 input_output_aliases={i: 2 + i for i in range(2 * n)},
                 compiler_params=pltpu.CompilerParams(has_side_effects=EFFECT))(
                     *[_hbm(a) for a in gs], *[_hbm(a) for a in lands])
    return res[0], res[1], list(res[2:2 + n]), list(res[2 + n:2 + 2 * n]), res[-1]


def _rs_wait(name, gs, lands, send, recv, after):
    n = len(gs)

    def body(*refs):
        g, land = refs[:n], refs[n:2 * n]
        send_sem, recv_sem = refs[2 * n], refs[2 * n + 1]
        x, y, c = _place()
        me = 4 * x + 2 * y + c
        for wi in range(n):
            for k, to in enumerate(_peers(x, y, c)):
                _rs_copy(g[wi], land[wi], send_sem, recv_sem, wi, k, to, me).wait_send()
                _rs_copy(g[wi], land[wi], send_sem, recv_sem, wi, k, (x, y, c), 4 * to[0] + 2 * to[1] + to[2]).wait_recv()

    res = _pcall(body, name=name, out_shape=[pltpu.HBM(a.shape, a.dtype) for a in list(gs) + list(lands)],
                 in_specs=[HBM_SPEC] * (2 * n) + [SEM_SPEC, SEM_SPEC, ANY_SPEC], out_specs=[HBM_SPEC] * (2 * n),
                 input_output_aliases={i: i for i in range(2 * n)},
                 compiler_params=pltpu.CompilerParams(has_side_effects=EFFECT))(*gs, *lands, send, recv, after)
    return list(res[:n]), list(res[n:])


def _bc_copy(blk_ref, land_ref, send, recv, k, to, slot):
    return pltpu.make_async_remote_copy(src_ref=blk_ref, dst_ref=land_ref.at[slot], send_sem=send.at[k], recv_sem=recv.at[k],
                                        device_id=to, device_id_type=MESH)


def _bcast_start(name, blk):
    land = lax.empty((N_DEV,) + blk.shape, blk.dtype)

    def body(blk_ref, land_ref, send, recv, blk_thru, land_thru, token):
        x, y, c = _place()
        for k, to in enumerate(_peers(x, y, c)):
            _bc_copy(blk_ref, land_ref, send, recv, k, to, 4 * x + 2 * y + c).start()
        token[...] = jnp.zeros_like(token)

    return _pcall(body, name=name,
                  out_shape=[pltpu.SemaphoreType.DMA((7,)), pltpu.SemaphoreType.DMA((7,)), pltpu.HBM(blk.shape, blk.dtype),
                             pltpu.HBM(land.shape, land.dtype), jax.ShapeDtypeStruct((8, 128), F32)],
                  in_specs=[HBM_SPEC, HBM_SPEC], out_specs=[SEM_SPEC, SEM_SPEC, HBM_SPEC, HBM_SPEC, pl.BlockSpec(memory_space=pltpu.VMEM)],
                  input_output_aliases={0: 2, 1: 3}, compiler_params=pltpu.CompilerParams(has_side_effects=EFFECT))(_hbm(blk), _hbm(land))


def _bcast_wait(name, blk, land, send, recv, after):
    def body(blk_ref, land_ref, send_sem, recv_sem, after_ref, blk_thru, land_thru):
        x, y, c = _place()
        for k, to in enumerate(_peers(x, y, c)):
            _bc_copy(blk_ref, land_ref, send_sem, recv_sem, k, to, 4 * x + 2 * y + c).wait_send()
            _bc_copy(blk_ref, land_ref, send_sem, recv_sem, k, to, 4 * to[0] + 2 * to[1] + to[2]).wait_recv()

    return _pcall(body, name=name, out_shape=[pltpu.HBM(blk.shape, blk.dtype), pltpu.HBM(land.shape, land.dtype)],
                  in_specs=[HBM_SPEC, HBM_SPEC, SEM_SPEC, SEM_SPEC, ANY_SPEC], out_specs=[HBM_SPEC, HBM_SPEC],
                  input_output_aliases={0: 0, 1: 1}, compiler_params=pltpu.CompilerParams(has_side_effects=EFFECT))(
                      blk, land, send, recv, after)[1]


def _rs_sum(name, gs, lands):
    n = len(gs)

    def body(*refs):
        g_refs, land_refs, out_refs = refs[:n], refs[n:2 * n], refs[2 * n:3 * n]
        recvs = refs[3 * n:4 * n]
        local_sems, sib_send, sib_recv = refs[4 * n:]
        x, y, c = _place()
        me = 4 * x + 2 * y + c
        cps = []
        for wi in range(n):
            hr = g_refs[wi].shape[1] // 2
            own = g_refs[wi].at[2 * x + y, pl.ds(pl.multiple_of(c * hr, 16), hr), :]
            cps.append(pltpu.make_async_copy(own, recvs[wi].at[me], local_sems.at[8 * wi + 7]))
            for k, (tx, ty, tc) in enumerate(_peers(x, y, c)):
                slot = 4 * tx + 2 * ty + tc
                cps.append(pltpu.make_async_copy(land_refs[wi].at[slot], recvs[wi].at[slot], local_sems.at[8 * wi + k]))
        for cp in cps:
            cp.start()
        sibs = []
        for wi in range(n):
            hr = g_refs[wi].shape[1] // 2
            ch = _pick(hr, 64, 16)
            for cp in cps[8 * wi:8 * wi + 8]:
                cp.wait()
            base = pl.multiple_of(c * hr, 16)
            for r0 in range(0, hr, ch):
                acc = recvs[wi][0, r0:r0 + ch, :].astype(F32)
                for k in range(1, N_DEV):
                    acc = acc + recvs[wi][k, r0:r0 + ch, :].astype(F32)
                out_refs[wi][pl.ds(base + r0, ch), :] = acc
            half = out_refs[wi].at[pl.ds(base, hr), :]
            sib = pltpu.make_async_remote_copy(src_ref=half, dst_ref=half, send_sem=sib_send.at[wi], recv_sem=sib_recv.at[wi],
                                               device_id=(x, y, 1 - c), device_id_type=MESH)
            sib.start()
            sibs.append(sib)
        for wi in range(n):
            hr = g_refs[wi].shape[1] // 2
            other = out_refs[wi].at[pl.ds(pl.multiple_of((1 - c) * hr, 16), hr), :]
            pltpu.make_async_remote_copy(src_ref=other, dst_ref=other, send_sem=sib_send.at[wi], recv_sem=sib_recv.at[wi],
                                         device_id=(x, y, 1 - c), device_id_type=MESH).wait_recv()
            sibs[wi].wait_send()

    res = _pcall(
        body, name=name, out_shape=[jax.ShapeDtypeStruct(g.shape[1:], F32) for g in gs],
        in_specs=[ANY_SPEC] * (2 * n), out_specs=[pl.BlockSpec(memory_space=pltpu.VMEM)] * n,
        scratch_shapes=[pltpu.VMEM((N_DEV, g.shape[1] // 2, g.shape[2]), BF16) for g in gs]
        + [pltpu.SemaphoreType.DMA((8 * n,)), pltpu.SemaphoreType.DMA((n,)), pltpu.SemaphoreType.DMA((n,))],
        compiler_params=pltpu.CompilerParams(vmem_limit_bytes=VMEM_LIMIT))(*gs, *lands)
    return list(res)


def _gate_norm_epilogue(factor, x_in, gt, nxt, t, d):
    blk = (ROWS, d)
    full = lambda i, j: (i, j)
    rowv = lambda i, j: (0, j)

    def epi(acc, res, scale, *norm):
        x_new = res + (factor * scale) * acc
        if not norm:
            return x_new, acc
        gv, scv, shv = norm
        r = lax.rsqrt(jnp.mean(x_new * x_new, axis=-1, keepdims=True) + EPS)
        return x_new, acc, x_new * r * gv * (1.0 + scv) + shv

    ins = [(x_in, blk, full), (gt, (1, d), rowv)] + [(v, (1, d), rowv) for v in (nxt or ())]
    outs = [((t, d), F32, blk, full), ((t, d), BF16, blk, full)] + ([((t, d), BF16, blk, full)] if nxt else [])
    return epi, ins, outs


def _mixer_tail(name, ge, w_glu, b_glu, w_up, y_pool, uin, w_out, x_in, gt, nxt, d, pw, tm=ROWS):
    t, sw = ge.shape
    tm = min(tm, t)
    ns = w_glu.shape[2]
    cb = (2 * pw) // d

    def body(ge_ref, wg_ref, b_ref, wu_ref, yp_ref, glp_ref, gls_ref, wo_ref, x_ref, gt_ref, gn_ref, sc_ref, sh_ref,
             gv_ref, sg_ref, ys_ref, mg_ref, y_ref, h_ref, xn_ref):
        gev = ge_ref[...]
        gv = jnp.concatenate([jnp.dot(gev, wg_ref[s], preferred_element_type=F32) for s in range(N_CHIPS)], axis=1)
        gv_ref[...] = gv.astype(BF16)
        g = gv + b_ref[...]
        sg = (g[:, :sw] * _sigmoid(g[:, sw:])).astype(BF16)
        sg_ref[...] = sg
        ys = jnp.concatenate([jnp.dot(sg, wu_ref[s], preferred_element_type=F32) for s in range(N_CHIPS)], axis=1)
        ys_ref[...] = ys.astype(BF16)
        mg = (_sigmoid(glp_ref[...].astype(F32)) * yp_ref[...].astype(F32) + _sigmoid(gls_ref[...].astype(F32)) * ys).astype(BF16)
        mg_ref[...] = mg
        y = jnp.dot(mg, wo_ref[...], preferred_element_type=F32)
        y_ref[...] = y.astype(BF16)
        xn = x_ref[...] + gt_ref[...] * y
        xn_ref[...] = xn
        r = lax.rsqrt(jnp.mean(xn * xn, axis=-1, keepdims=True) + EPS)
        h_ref[...] = (xn * r * gn_ref[...] * (1.0 + sc_ref[...]) + sh_ref[...]).astype(BF16)

    full3 = lambda i: (0, 0, 0)
    full2 = lambda i: (0, 0)
    rows = lambda w: pl.BlockSpec((tm, w), lambda i: (i, 0))
    rowv = pl.BlockSpec((1, d), full2)
    return _pcall(
        body, name=name, grid=(t // tm,),
        in_specs=[rows(sw), pl.BlockSpec(w_glu.shape, full3), pl.BlockSpec(b_glu.shape, full2),
                  pl.BlockSpec(w_up.shape, full3), rows(d), pl.BlockSpec((tm, d), lambda i: (i, cb)),
                  pl.BlockSpec((tm, d), lambda i: (i, cb + 1)), pl.BlockSpec(w_out.shape, full2), rows(d), rowv, rowv, rowv, rowv],
        out_specs=[rows(N_CHIPS * ns), rows(sw), rows(d), rows(d), rows(d), rows(d), rows(d)],
        out_shape=[jax.ShapeDtypeStruct((t, N_CHIPS * ns), BF16), jax.ShapeDtypeStruct((t, sw), BF16)]
        + [jax.ShapeDtypeStruct((t, d), BF16)] * 4 + [jax.ShapeDtypeStruct((t, d), F32)],
        compiler_params=_params("parallel"))(ge, w_glu, b_glu, w_up, y_pool, uin, uin, w_out, x_in, gt, *nxt)


def _glu_up_bwd(name, dy, w_up, gv, b_glu, w_glu, sg, ge, tm=ROWS, deps=()):
    t, sw = ge.shape
    tm = min(tm, t)
    nb = t // tm
    ns = w_up.shape[2]
    nt = (((1,), (1,)), ((), ()))
    tn_ = (((0,), (0,)), ((), ()))

    def body(dy_ref, wu_ref, gv_ref, b_ref, wg_ref, sg_ref, ge_ref, *rest):
        dge_ref, db_ref, dwu_ref, dwg_ref, dwu_acc, dwg_acc = rest[len(deps):]
        i = pl.program_id(0)

        @pl.when(i == 0)
        def _():
            db_ref[...] = jnp.zeros_like(db_ref)
            dwu_acc[...] = jnp.zeros_like(dwu_acc)
            dwg_acc[...] = jnp.zeros_like(dwg_acc)

        dyv = dy_ref[...]
        dsg = None
        for s in range(N_CHIPS):
            q = lax.dot_general(dyv[:, s * ns:(s + 1) * ns], wu_ref[s], nt, preferred_element_type=F32)
            dsg = q if dsg is None else dsg + q
        g = gv_ref[...].astype(F32) + b_ref[...]
        val, sgm = g[:, :sw], _sigmoid(g[:, sw:])
        dval, dgate = dsg * sgm, dsg * val * sgm * (1.0 - sgm)
        db_ref[...] += jnp.concatenate([_colsum(dval), _colsum(dgate)], axis=1)
        dgv = jnp.concatenate([dval, dgate], axis=1).astype(BF16)
        dge = None
        for s in range(N_CHIPS):
            q = lax.dot_general(dgv[:, s * ns:(s + 1) * ns], wg_ref[s], nt, preferred_element_type=F32)
            dge = q if dge is None else dge + q
        dge_ref[...] = dge.astype(BF16)
        dwu_acc[...] += lax.dot_general(sg_ref[...], dyv, tn_, preferred_element_type=F32)
        dwg_acc[...] += lax.dot_general(ge_ref[...], dgv, tn_, preferred_element_type=F32)

        @pl.when(i == nb - 1)
        def _():
            for s in range(N_CHIPS):
                dwu_ref[s] = dwu_acc[:, s * ns:(s + 1) * ns].astype(BF16)
                dwg_ref[s] = dwg_acc[:, s * ns:(s + 1) * ns].astype(BF16)

    full3 = lambda i: (0, 0, 0)
    rows = lambda w: pl.BlockSpec((tm, w), lambda i: (i, 0))
    wide = N_CHIPS * ns
    return _pcall(
        body, name=name, grid=(nb,),
        in_specs=[rows(wide), pl.BlockSpec(w_up.shape, full3), rows(wide), pl.BlockSpec(b_glu.shape, lambda i: (0, 0)),
                  pl.BlockSpec(w_glu.shape, full3), rows(sw), rows(sw)] + [ANY_SPEC] * len(deps),
        out_specs=[rows(sw), pl.BlockSpec((1, wide), lambda i: (0, 0)), pl.BlockSpec(w_up.shape, full3),
                   pl.BlockSpec(w_glu.shape, full3)],
        out_shape=[jax.ShapeDtypeStruct((t, sw), BF16), jax.ShapeDtypeStruct((1, wide), F32),
                   jax.ShapeDtypeStruct(w_up.shape, BF16), jax.ShapeDtypeStruct(w_glu.shape, BF16)],
        scratch_shapes=[pltpu.VMEM((sw, wide), F32), pltpu.VMEM((sw, wide), F32)],
        compiler_params=_params("arbitrary"))(dy, w_up, gv, b_glu, w_glu, sg, ge, *deps)


_FULL = lambda i, j: (i, j)
_ROWV = lambda i, j: (0, j)


def _gates_bwd_epilogue(y_pool, y_ssm, uin, t, d, pw):
    cb = (2 * pw) // d

    def epi(dm, yp, ys, glp, gls):
        sp, ss = _sigmoid(glp.astype(F32)), _sigmoid(gls.astype(F32))
        dgl = jnp.concatenate([dm * yp.astype(F32) * sp * (1.0 - sp), dm * ys.astype(F32) * ss * (1.0 - ss)], axis=1)
        return dm * sp, dm * ss, ("at", 2 * pw, dgl)

    ins = [(y_pool, (ROWS, d), _FULL), (y_ssm, (ROWS, d), _FULL),
           (uin, (ROWS, d), lambda i, j: (i, cb)), (uin, (ROWS, d), lambda i, j: (i, cb + 1))]
    wide = 2 * pw + 2 * d
    return epi, ins, [((t, d), BF16, (ROWS, d), _FULL)] * 2 + [((t, wide), BF16, (ROWS, wide), _FULL)]


def _loss_epilogue(x_in, gt, tgt, g_final, t, d):
    blk = (ROWS, d)
    full = lambda i, j: (i, j)
    rowv = lambda i, j: (0, j)

    def epi(acc, res, scale, tv, gv):
        xv = res + (0.5 * scale) * acc
        r = lax.rsqrt(jnp.mean(xv * xv, axis=-1, keepdims=True) + EPS)
        xr = xv * r
        e = xr * gv - tv
        loss_row = 0.5 * jnp.mean(e * e, axis=-1, keepdims=True)
        dout = e * (1.0 / d)
        gd = gv * dout
        dx = r * (gd - xr * jnp.mean(gd * xr, axis=-1, keepdims=True))
        fdx = 0.5 * dx
        return [dx, scale * fdx], [_colsum(dout * xr), _colsum(loss_row * jnp.ones((1, 128), F32)), _colsum(fdx * acc)]

    ins = [(x_in, blk, full), (gt, (1, d), rowv), (tgt, blk, full), (g_final, (1, d), rowv)]
    return epi, ins, [((t, d), F32, blk, full), ((t, d), BF16, blk, full)], [d, 128, d]


def _norm_bwd_epilogue(x, dres, g, sc, up, t, d):
    blk = (ROWS, d)
    full = lambda i, j: (i, j)
    rowv = lambda i, j: (0, j)

    def epi(dhv, xv, drv, gv, scv, *rest):
        r = lax.rsqrt(jnp.mean(xv * xv, axis=-1, keepdims=True) + EPS)
        xr = xv * r
        dn = dhv * (1.0 + scv)
        gd = gv * dn
        dx = drv + r * (gd - xr * jnp.mean(gd * xr, axis=-1, keepdims=True))
        outs, reds = [dx], [_colsum(dhv), _colsum(dhv * xr * gv), _colsum(dn * xr)]
        if up:
            yv, gtv = rest
            fdx = up[2] * dx
            outs.append(gtv * fdx)
            reds.append(_colsum(fdx * yv.astype(F32)))
        return outs, reds

    ins = [(x, blk, full), (dres, blk, full), (g, (1, d), rowv), (sc, (1, d), rowv)]
    ins += [(up[0], blk, full), (up[1], (1, d), rowv)] if up else []
    outs = [((t, d), F32, blk, full)] + ([((t, d), BF16, blk, full)] if up else [])
    return epi, ins, outs, [d] * (4 if up else 3)


def _ffn_in_act(name, h, w_in, tm=ROWS):
    t, d = h.shape
    ns = w_in.shape[2]
    tm = _pick(t, tm, 8)
    w4 = w_in.reshape(2, 2, d, ns)

    def body(h_ref, w_ref, ab_ref, act_ref):
        hv = h_ref[...]
        a = jnp.dot(hv, w_ref[0], preferred_element_type=F32)
        b = jnp.dot(hv, w_ref[1], preferred_element_type=F32)
        ab_ref[0] = a.astype(BF16)
        ab_ref[1] = b.astype(BF16)
        act_ref[...] = (a * _sigmoid(a) * b).astype(BF16)

    return _pcall(body, name=name, grid=(2, t // tm),
                  in_specs=[pl.BlockSpec((tm, d), lambda c, i: (i, 0)), pl.BlockSpec((2, None, d, ns), lambda c, i: (0, c, 0, 0))],
                  out_specs=[pl.BlockSpec((2, tm, ns), lambda c, i: (0, i, c)), pl.BlockSpec((tm, ns), lambda c, i: (i, c))],
                  out_shape=[jax.ShapeDtypeStruct((2, t, 2 * ns), BF16), jax.ShapeDtypeStruct((t, 2 * ns), BF16)],
                  compiler_params=_params("parallel", "parallel"))(h, w4)


def _dswiglu_epilogue(ab, t, f, tn):
    blk = (2, ROWS, tn)
    idx = lambda i, j: (0, i, j)

    def epi(dact, abv):
        a, b = abv[0].astype(F32), abv[1].astype(F32)
        s = _sigmoid(a)
        return (jnp.stack([dact * b * (s * (1.0 + a * (1.0 - s))), dact * (a * s)]),)

    return epi, [(ab, blk, idx)], [((2, t, f), BF16, blk, idx)]


def _ffn_fwd(tag, x, h, gt, get_w_in, get_w_out, nxt=None, loss=None):
    t, d = x.shape
    ab, act = _ffn_in_act(tag + "_in", h, get_w_in(h))
    if loss:
        epi, epi_ins, epi_outs, epi_reds = _loss_epilogue(x, gt, *loss, t, d)
        res = _mm(tag + "_out", act, get_w_out(act), tm=ROWS, tn=d, epi=epi, epi_ins=epi_ins, epi_outs=epi_outs,
                  epi_reds=epi_reds, b_resident=True)
        return res, (x, h, ab, act, None)
    epi, epi_ins, epi_outs = _gate_norm_epilogue(0.5, x, gt, nxt, t, d)
    res = _mm(tag + "_out", act, get_w_out(act), tm=ROWS, tn=d, epi=epi, epi_ins=epi_ins, epi_outs=epi_outs, b_resident=True)
    return res[0], res[2], (x, h, ab, act, res[1])


def _ffn_bwd(tag, dx_new, dy, saved, g, sc, w_in, w_out, start_rs, up, deps=()):
    x, h, ab, act, _ = saved
    d = x.shape[1]
    f = act.shape[1]
    dw_out = _mm(tag + "_dwout", act, dy, ta=True, tm=1408, tn=d, tk=2048, deps=deps)
    tok = start_rs("out", dw_out.reshape(N_CHIPS, f // N_CHIPS, d))
    epi, epi_ins, epi_outs = _dswiglu_epilogue(ab, x.shape[0], f, w_in.shape[2])
    dab = _mm(tag + "_dact", dy, w_out, tb=True, tm=ROWS, tn=w_in.shape[2], epi=epi, epi_ins=epi_ins, epi_outs=epi_outs,
              deps=(tok,), j_outer=True)
    dw_in = _mm(tag + "_dwin", h, dab, ta=True, out_stacked=True, b_halves=True, tm=ROWS, tn=1408, tk=4096, j_outer=True)
    tok = start_rs("in", dw_in)
    epi, epi_ins, epi_outs, epi_reds = _norm_bwd_epilogue(x, dx_new, g, sc, up, x.shape[0], d)
    res = _mm(tag + "_dh", dab, w_in, tb=True, b_stacked=True, a_halves=True, tn=d, tk=5632, deps=(tok,), epi=epi,
              epi_ins=epi_ins, epi_outs=epi_outs, epi_reds=epi_reds, b_resident=True)
    if up:
        return res
    return res[0], None, res[1], res[2], res[3], None


def _row(v):
    return v.reshape(1, -1)


def _pack(parts):
    cols = []
    for p in parts:
        flat = p.reshape(-1).astype(F32)
        padn = (-flat.shape[0]) % 128
        cols.append(jnp.pad(flat, (0, padn)) if padn else flat)
    flat = jnp.concatenate(cols)
    padn = (-flat.shape[0]) % 1024
    if padn:
        flat = jnp.pad(flat, (0, padn))
    return flat.reshape(-1, 128)


def _unpack(packed, shapes):
    flat = packed.reshape(-1)
    out, off = [], 0
    for s in shapes:
        n = math.prod(s)
        out.append(flat[off:off + n].reshape(s))
        off += n + ((-n) % 128)
    return out


SMALL = ['b_ada', 'g_ffn1', 'g_mix', 'pool_w', 'pool_b', 'pool_scale', 'ssm_lam_re_log', 'ssm_lam_im', 'ssm_log_dt',
         'ssm_b_re', 'ssm_b_im', 'ssm_c_re', 'ssm_c_im', 'ssm_d', 'b_glu', 'g_ffn2', 'g_final']
BIG = ['w_ffn1_in', 'w_ffn1_out', 'w_in', 'w_pool_up', 'w_glu', 'w_ssm_up', 'w_out', 'w_ffn2_in', 'w_ffn2_out']
AG_GROUPS = [[0], [1], [2, 3, 4, 5, 6], [7, 8]]
AG_DIRECT = [False, True, False, True]
SMALL_LATE = ['b_ada', 'g_ffn1']
RS_SUM_GROUPS = [[0, 1, 2, 3], [4, 5]]
WEIGHTS = ['w_ada', 'b_ada', 'g_ffn1', 'w_ffn1_in', 'w_ffn1_out', 'g_mix', 'w_in', 'pool_w', 'pool_b', 'pool_scale', 'w_pool_up',
           'ssm_lam_re_log', 'ssm_lam_im', 'ssm_log_dt', 'ssm_b_re', 'ssm_b_im', 'ssm_c_re', 'ssm_c_im', 'ssm_d', 'w_glu', 'b_glu',
           'w_ssm_up', 'w_out', 'g_ffn2', 'w_ffn2_in', 'w_ffn2_out', 'g_final']


def kernel(x, c, w_ada, b_ada, g_ffn1, w_ffn1_in, w_ffn1_out, g_mix, w_in, pool_w, pool_b, pool_scale, w_pool_up, ssm_lam_re_log, ssm_lam_im, ssm_log_dt, ssm_b_re, ssm_b_im, ssm_c_re, ssm_c_im, ssm_d, w_glu, b_glu, w_ssm_up, w_out, g_ffn2, w_ffn2_in, w_ffn2_out, g_final, loss_target, m_w_ada, m_b_ada, m_g_ffn1, m_w_ffn1_in, m_w_ffn1_out, m_g_mix, m_w_in, m_pool_w, m_pool_b, m_pool_scale, m_w_pool_up, m_ssm_lam_re_log, m_ssm_lam_im, m_ssm_log_dt, m_ssm_b_re, m_ssm_b_im, m_ssm_c_re, m_ssm_c_im, m_ssm_d, m_w_glu, m_b_glu, m_w_ssm_up, m_w_out, m_g_ffn2, m_w_ffn2_in, m_w_ffn2_out, m_g_final, v_w_ada, v_b_ada, v_g_ffn1, v_w_ffn1_in, v_w_ffn1_out, v_g_mix, v_w_in, v_pool_w, v_pool_b, v_pool_scale, v_w_pool_up, v_ssm_lam_re_log, v_ssm_lam_im, v_ssm_log_dt, v_ssm_b_re, v_ssm_b_im, v_ssm_c_re, v_ssm_c_im, v_ssm_d, v_w_glu, v_b_glu, v_w_ssm_up, v_w_out, v_g_ffn2, v_w_ffn2_in, v_w_ffn2_out, v_g_final):
    args = dict(locals())
    wt = {n: args[n] for n in WEIGHTS}
    mom = {n: args["m_" + n] for n in WEIGHTS}
    var = {n: args["v_" + n] for n in WEIGHTS}

    t, d = x.shape[1], x.shape[2]
    pw = pool_b.shape[1]
    sw = ssm_d.shape[1]
    ngrp = sw // SSM_GROUP
    gn = ngrp * SSM_STATE
    xi, yi, ci = _place()
    b_me = 4 * xi + 2 * yi + ci
    s_me = 2 * xi + yi
    x2d = x[0]
    tgt = loss_target[0]

    c_all = _allgather_small("ag_c", c.reshape(8, d // 8)).reshape(N_DEV, d)
    ncol = w_ada.shape[2]
    b_sh = lax.dynamic_slice(b_ada, (0, s_me * ncol), (1, ncol))
    mod_sh = _ada_fwd("ada_fwd", c_all, w_ada[0], b_sh)
    md_send, md_recv, mod_sh, md_land, tok = _bcast_start("mod_start", mod_sh)

    shards = [wt[n][0].astype(BF16) for n in BIG]
    lands = _place_shards(shards, s_me)
    n0 = len(AG_GROUPS[0])
    sems_a, shards_a, lands_a, tok = _ag_start("ag_start_first", shards[:n0], lands[:n0], AG_GROUPS[:1], AG_DIRECT[:1], deps=(tok,))
    rest = [[w - n0 for w in grp] for grp in AG_GROUPS[1:]]
    sems_b, shards_b, lands_b, tok = _ag_start("ag_start_rest", shards[n0:], lands[n0:], rest, AG_DIRECT[1:], deps=(tok,))
    ag_sems, shards_t, lands_t = sems_a + sems_b, shards_a + shards_b, lands_a + lands_b
    md_land = _bcast_wait("mod_wait", mod_sh, md_land, md_send, md_recv, tok)
    mod_all = lax.dynamic_update_slice(md_land, mod_sh[None], (b_me, 0, 0))
    full = {}

    def weights(gi, *after):
        grp = AG_GROUPS[gi]
        if BIG[grp[0]] not in full:
            ls = _ag_wait("ag_wait%d" % gi, [shards_t[w] for w in grp], [lands_t[w] for w in grp], *ag_sems[gi],
                          AG_DIRECT[gi], after)
            for w, l in zip(grp, ls if AG_DIRECT[gi] else _ag_forward("ag_fwd%d" % gi, ls)):
                full[BIG[w]] = l
        return full

    mod_me = jnp.concatenate([lax.dynamic_slice(mod_all, (2 * s, b_me, 0), (1, 1, ncol))[0] for s in range(N_CHIPS)], axis=1)
    mod = mod_me.reshape(9, d)
    sh1, sc1, gt1, sh2, sc2, gt2, sh3, sc3, gt3 = [mod[k:k + 1] for k in range(9)]

    f = w_ffn1_out.shape[1] * N_CHIPS

    col = lambda a: a.reshape(gn, 1)
    lrl_c, li_c = col(ssm_lam_re_log), col(ssm_lam_im)
    ldt_c = col(jnp.broadcast_to(ssm_log_dt.reshape(ngrp, 1), (ngrp, SSM_STATE)))
    b_re2, b_im2 = ssm_b_re.reshape(gn, SSM_GROUP), ssm_b_im.reshape(gn, SSM_GROUP)
    lrdt_c, ang_c, bb_re, bb_im = _ssm_prep("ssm_prep", lrl_c, li_c, ldt_c, b_re2, b_im2)
    lrdt, ang = lrdt_c.reshape(1, gn), ang_c.reshape(1, gn)
    tr = lambda a: jnp.swapaxes(a, 1, 2)
    bbd = [_blockdiag_b(v, sw).astype(BF16) for v in (bb_re, bb_im)]
    ccd = [_blockdiag_c(v[0], sw).astype(BF16) for v in (ssm_c_re, ssm_c_im)]
    bbd_t, ccd_t = [tr(v) for v in bbd], [tr(v) for v in ccd]
    early = [n for n in SMALL if n not in SMALL_LATE]
    packs = {}
    for tag, names, extra in (("early", early, []), ("late", SMALL_LATE, [jnp.zeros((128,), F32)])):
        packs[tag] = [_pack([src[n] for n in names] + extra) for src in (wt, mom, var)]
    shadow = [lrdt, ang, ldt_c, *bbd, *ccd, *bbd_t, *ccd_t, *packs["early"], *packs["late"]]

    h1 = _norm_fwd("ffn1_norm", x2d, g_ffn1, sc1, sh1, deps=(tok,))
    x1, h2, sav1 = _ffn_fwd("ffn1", x2d, h1, gt1, lambda after: weights(0, after, *shadow)['w_ffn1_in'],
                            lambda after: weights(1, after)['w_ffn1_out'].reshape(f, d), (g_mix, sc2, sh2))
    weights(2, h2)
    wo = full['w_out'].reshape(d, d)
    uin = _mm("mix_in", h2, full['w_in'], b_stacked=True, tm=2048, tn=768, j_outer=True)
    p_pool, z_pool, y_pool = _pool_fwd("pool_fwd", uin, pool_w[0], pool_b, pool_scale, full['w_pool_up'], pw, tm=1024)
    y_s, ge, s_re, s_im = _ssm_fwd("ssm_fwd", uin, lrdt, ang, *bbd, *ccd, ssm_d, sw)
    gv, sg, y_ssm, merged, y2, h3, x2 = _mixer_tail("mixer_tail", ge, full['w_glu'], b_glu, full['w_ssm_up'], y_pool, uin, wo,
                                                    x1, gt2, (g_ffn2, sc3, sh3), d, pw)

    (dx3, dy3, dg_final, loss_v, dgt3), sav3 = _ffn_fwd(
        "ffn2", x2, h3, gt3, lambda after: weights(3, after)['w_ffn2_in'],
        lambda after: weights(3, after)['w_ffn2_out'].reshape(f, d), loss=(tgt, _row(g_final)))

    gs = {}
    rs_open = []

    def start_rs(names, gbs):
        send, recv, g_thru, land_thru, token = _rs_start("rs_start_" + names[0], gbs)
        rs_open.append((names, send, recv, g_thru, land_thru))
        return token

    dx2, dy2, dsh3, dsc3, gs['g_ffn2'], dgt2 = _ffn_bwd(
        "ffn2b", dx3, dy3, sav3, g_ffn2, sc3, full['w_ffn2_in'], full['w_ffn2_out'].reshape(f, d),
        lambda key, g: start_rs(['w_ffn2_' + key], [g]), (y2, gt2, 1.0))

    epi, epi_ins, epi_outs = _gates_bwd_epilogue(y_pool, y_ssm, uin, t, d, pw)
    dy_pool, dy_ssm, duin, g_wo = _mm("mix_dmerged", dy2, wo, tb=True, tm=ROWS, tn=d, epi=epi, epi_ins=epi_ins,
                                      epi_outs=epi_outs, b_resident=True, wgrad=merged, wgrad_rows=True)
    g_wo = g_wo.reshape(N_CHIPS, d // N_CHIPS, d)

    duin, gs['pool_w'], gs['pool_b'], gs['pool_scale'], g_wpu = _pool_bwd(
        "pool_bwd", dy_pool, z_pool, p_pool, pool_w[0], pool_b, pool_scale, full['w_pool_up'], duin, tm=1024)

    tok = start_rs(['w_out', 'w_pool_up'], [g_wo, g_wpu])
    dge, gs['b_glu'], g_wsu, g_wglu = _glu_up_bwd("glu_up_bwd", dy_ssm, full['w_ssm_up'], gv, b_glu, full['w_glu'], sg, ge,
                                                  deps=(tok,))
    (duin, g_abre, g_abim, g_bbd_re, g_bbd_im, g_ccd_re, g_ccd_im, gs['ssm_d']) = _ssm_bwd(
        "ssm_bwd", dge, y_s, uin, s_re, s_im, lrdt, ang, *bbd_t, *ccd_t, ssm_d, sw, duin)
    gs['ssm_c_re'], gs['ssm_c_im'] = _diag_of_c(g_ccd_re, sw), _diag_of_c(g_ccd_im, sw)
    d_lrl, d_li, d_ldt, d_bre, d_bim = _ssm_param_bwd(
        "ssm_param_bwd", lrl_c, li_c, ldt_c, b_re2, b_im2, g_abre.reshape(gn, 1), g_abim.reshape(gn, 1),
        _diag_of_b(g_bbd_re, sw), _diag_of_b(g_bbd_im, sw))
    gs['ssm_lam_re_log'], gs['ssm_lam_im'] = d_lrl, d_li
    gs['ssm_log_dt'] = jnp.sum(d_ldt.reshape(ngrp, SSM_STATE), axis=1)
    gs['ssm_b_re'], gs['ssm_b_im'] = d_bre, d_bim

    g_win =_mm("mix_dwin", h2, duin, ta=True, out_stacked=True, tm=d, tn=768, tk=4096)
    tok = start_rs(['w_ssm_up', 'w_glu', 'w_in'], [g_wsu, g_wglu, g_win])
    epi, epi_ins, epi_outs, epi_reds = _norm_bwd_epilogue(x1, dx2, g_mix, sc2, (sav1[4], gt1, 0.5), t, d)
    dx1, dy1, dsh2, dsc2, gs['g_mix'], dgt1 = _mm(
        "mix_dh", duin, full['w_in'], tb=True, b_stacked=True, tn=d, tk=3072, deps=(tok,), epi=epi, epi_ins=epi_ins,
        epi_outs=epi_outs, epi_reds=epi_reds, b_resident=True)

    gs['g_final'] = dg_final
    sg_blk = _pack([gs[n] for n in early])
    sg_send, sg_recv, sg_blk, sg_land, tok = _bcast_start("sg_start", sg_blk)

    dx0, _, dsh1, dsc1, gs['g_ffn1'], _ = _ffn_bwd(
        "ffn1b", dx1, dy1, sav1, g_ffn1, sc1, full['w_ffn1_in'], full['w_ffn1_out'].reshape(f, d),
        lambda key, g: start_rs(['w_ffn1_' + key], [g]), None, deps=(tok,))

    gs['b_ada'] = jnp.concatenate([dsh1, dsc1, dgt1, dsh2, dsc2, dgt2, dsh3, dsc3, dgt3], axis=1)
    lt_blk = _pack([gs[n] for n in SMALL_LATE] + [loss_v])
    lt_send, lt_recv, lt_blk, lt_land, tok = _bcast_start("late_start", lt_blk)

    grads, delta, new_m, new_v = {}, {}, {}, {}

    def small_update(tag, names, blk, land):
        g8 = lax.dynamic_update_slice(land, blk[None], (b_me, 0, 0)).reshape(-1, 128)
        w_pack, m_pack, v_pack = packs[tag]
        shapes = [wt[n].shape for n in names]
        per_param, packed = _adamw_small("adamw_small_" + tag, w_pack, g8, m_pack, v_pack, shapes)
        for k, dst in enumerate((grads, delta, new_m, new_v)):
            rest = _unpack(packed[k], shapes)
            for i, n in enumerate(names):
                dst[n] = per_param[i][k] if per_param[i] is not None else rest[i]
        return g8

    after = tok
    for group in RS_SUM_GROUPS:
        names, g_done, land_done = [], [], []
        for k in group:
            nk, send, recv, g_thru, land_thru = rs_open[k]
            gd, ld = _rs_wait("rs_wait_" + nk[0], g_thru, land_thru, send, recv, after)
            names, g_done, land_done = names + nk, g_done + gd, land_done + ld
        for n, g_sum in zip(names, _rs_sum("rs_sum_" + names[0], g_done, land_done)):
            g_out, dl, mn, vn = _adamw("adamw_" + n, wt[n][0], g_sum, mom[n][0], var[n][0])
            grads[n], delta[n], new_m[n], new_v[n] = g_out[None], dl[None], mn[None], vn[None]
            after = dl
        if group is RS_SUM_GROUPS[-2]:
            small_update("early", early, sg_blk, _bcast_wait("sg_wait", sg_blk, sg_land, sg_send, sg_recv, after))

    lt_land = _bcast_wait("late_wait", lt_blk, lt_land, lt_send, lt_recv, after)
    late8 = small_update("late", SMALL_LATE, lt_blk, lt_land).reshape(N_DEV, -1)
    loss = jnp.sum(late8[:, 10 * d])
    dmod_sh = lax.dynamic_slice(late8, (0, s_me * ncol), (N_DEV, ncol))
    g_w_ada, dl, mn, vn = _ada_bwd_adamw("ada_bwd_adamw", c_all, dmod_sh, w_ada[0], m_w_ada[0], v_w_ada[0])
    grads['w_ada'], delta['w_ada'], new_m['w_ada'], new_v['w_ada'] = g_w_ada[None], dl[None], mn[None], vn[None]

    return (loss, dx0[None], *[grads[n] for n in WEIGHTS], *[delta[n] for n in WEIGHTS],
            *[new_m[n] for n in WEIGHTS], *[new_v[n] for n in WEIGHTS])
```

```python
import functools
import math

import jax
import jax.numpy as jnp
from jax import lax
from jax.experimental import pallas as pl
from jax.experimental.pallas import tpu as pltpu

F32 = jnp.float32
BF16 = jnp.bfloat16
MESH = pl.DeviceIdType.MESH

EPS = 1e-6
POOL_WINDOWS = (2, 4, 8, 16)
POOL_HALO = 16
SSM_GROUP = 16
SSM_STATE = 64
SSM_BLOCKS = 4
N_DEV = 8
N_CHIPS = 4
ADAM_LR = 0.001
ADAM_B1 = 0.9
ADAM_B2 = 0.999
ADAM_EPS = 1e-08
ADAM_WD = 0.01
ADAM_STEP = 10
VMEM_LIMIT = 56 * 1024 * 1024
ROWS = 512


ANY_SPEC = pl.BlockSpec(memory_space=pl.ANY)
HBM_SPEC = pl.BlockSpec(memory_space=pltpu.HBM)
SEM_SPEC = pl.BlockSpec(memory_space=pltpu.SEMAPHORE)
EFFECT = pltpu.SideEffectType.DATAFLOW_SIDE_EFFECTING


def _hbm(a):
    return pltpu.with_memory_space_constraint(a, pltpu.HBM)


def _pcall(body, **kw):
    return pl.pallas_call(body, **kw)


def _params(*sem):
    return pltpu.CompilerParams(dimension_semantics=sem, vmem_limit_bytes=VMEM_LIMIT)


def _pick(n, cap, mult=128):
    if n <= cap:
        return n
    best = None
    for d in range(mult, cap + 1, mult):
        if n % d == 0:
            best = d
    assert best is not None, (n, cap, mult)
    return best


def _sigmoid(v):
    return 1.0 / (1.0 + jnp.exp(-v))


def _rowwise(name, fn, ins, params, outs, reds, tm, deps=()):
    t = ins[0][0].shape[0]
    tm = min(tm, t)
    nb = t // tm
    ni, npar, no, nd = len(ins), len(params), len(outs), len(deps)

    def body(*refs):
        iv = [r[...] for r in refs[:ni]]
        pv = [r[...] for r in refs[ni:ni + npar]]
        o_refs = refs[ni + npar + nd:ni + npar + nd + no]
        r_refs = refs[ni + npar + nd + no:]
        ovals, rvals = fn(iv, pv)
        for o_ref, val in zip(o_refs, ovals):
            off = 0
            if isinstance(val, tuple) and val[0] == "at":
                _, off, val = val
            parts = val if isinstance(val, (list, tuple)) else [val]
            for p in parts:
                o_ref[:, off:off + p.shape[1]] = p.astype(o_ref.dtype)
                off += p.shape[1]
        if r_refs:
            @pl.when(pl.program_id(0) == 0)
            def _():
                for r in r_refs:
                    r[...] = jnp.zeros_like(r)
            for r, val in zip(r_refs, rvals):
                r[...] += val

    in_specs = [pl.BlockSpec((tm, w), functools.partial(lambda i, cb: (i, cb), cb=cb)) for (_, w, cb) in ins]
    in_specs += [pl.BlockSpec(p.shape, lambda i: (0, 0)) for p in params]
    in_specs += [ANY_SPEC] * nd
    out_shape = [jax.ShapeDtypeStruct((t, w), dt) for (w, dt) in outs]
    out_shape += [jax.ShapeDtypeStruct((1, w), F32) for w in reds]
    out_specs = [pl.BlockSpec((tm, w), lambda i: (i, 0)) for (w, _) in outs]
    out_specs += [pl.BlockSpec((1, w), lambda i: (0, 0)) for w in reds]
    res = _pcall(body, name=name, grid=(nb,), in_specs=in_specs, out_specs=out_specs, out_shape=out_shape,
                 compiler_params=_params("arbitrary"))(*[a for a, _, _ in ins], *params, *deps)
    return res


def _colsum(v):
    return jnp.sum(v, axis=0, keepdims=True)


def _mm(name, a, b, *, ta=False, tb=False, b_stacked=False, out_stacked=False, tm=ROWS, tn=1024, tk=2816,
        out_dtype=BF16, epi=None, epi_ins=(), epi_outs=None, epi_reds=(), deps=(), j_outer=False, a_halves=False,
        b_halves=False, b_resident=False, wgrad=None, wgrad_rows=False):
    if a_halves:
        _, m, kdim = a.shape
        kdim *= 2
    elif ta:
        kdim, m = a.shape
    else:
        m, kdim = a.shape
    ns = None
    if b_stacked:
        ns = b.shape[2]
        n = b.shape[1] if tb else N_CHIPS * ns
        assert kdim == (N_CHIPS * ns if tb else b.shape[1]), (name, a.shape, b.shape)
    else:
        if b_halves:
            n = 2 * b.shape[2]
            assert kdim == b.shape[1] and not tb, (name, a.shape, b.shape)
        else:
            n = b.shape[0] if tb else b.shape[1]
            assert kdim == (b.shape[1] if tb else b.shape[0]), (name, a.shape, b.shape)
        if out_stacked:
            ns = n // N_CHIPS

    def shards(want):
        return max(g for g in (1, 2, 4) if g * ns <= max(want, ns))

    tm = _pick(m, tm, 128 if ta else 8)
    gn = gk = 1
    if (b_stacked and not tb) or out_stacked:
        gn = shards(min(tn, n // 2) if b_halves else tn)
        tn = gn * ns
    else:
        tn = _pick(n, tn)
    if b_stacked and tb:
        gk = shards(tk)
        tk = gk * ns
    else:
        tk = _pick(kdim, tk, 8 if ta else 128)
    nm, nn, nk = m // tm, n // tn, kdim // tk

    def ij(f):
        return (lambda g0, g1, k: f(g1, g0, k)) if j_outer else f

    if a_halves:
        assert b_stacked and tb and gk == N_CHIPS and nk == 1, name
        a_spec = pl.BlockSpec((2, tm, kdim // 2), ij(lambda i, j, k: (0, i, 0)))
    elif ta:
        a_spec = pl.BlockSpec((tk, tm), ij(lambda i, j, k: (k, i)))
    else:
        a_spec = pl.BlockSpec((tm, tk), ij(lambda i, j, k: (i, k)))
    if b_stacked and not tb:
        b_spec = pl.BlockSpec((gn, tk, ns), ij(lambda i, j, k: (j, k, 0)))
    elif b_stacked and tb:
        b_spec = pl.BlockSpec((gk, tn, ns), ij(lambda i, j, k: (k, j, 0)))
    elif b_halves:
        bph = (n // 2) // tn
        b_spec = pl.BlockSpec((None, tk, tn), ij(lambda i, j, k: (j // bph, k, j % bph)))
    elif tb:
        b_spec = pl.BlockSpec((tn, tk), ij(lambda i, j, k: (j, k)))
    else:
        b_spec = pl.BlockSpec((tk, tn), ij(lambda i, j, k: (k, j)))
    dims = (((0 if ta else 1,), (1 if tb else 0,)), ((), ()))

    if epi_outs is None:
        if out_stacked:
            epi_outs = [((N_CHIPS, m, ns), out_dtype, (gn, tm, ns), lambda i, j: (j, i, 0))]
        else:
            epi_outs = [((m, n), out_dtype, (tm, tn), lambda i, j: (i, j))]
    ne, no, nd, nr = len(epi_ins), len(epi_outs), len(deps), len(epi_reds)
    nx = 0 if wgrad is None else 1
    assert not (nr or nx) or (nn == 1 and not j_outer), name
    assert not nx or (nk == 1 and not ta and not a_halves and kdim % N_CHIPS == 0), name

    def body(a_ref, b_ref, *rest):
        e_refs = rest[:ne]
        x_refs = rest[ne:ne + nx]
        o_refs = rest[ne + nx + nd:ne + nx + nd + no]
        r_refs = rest[ne + nx + nd + no:ne + nx + nd + no + nr]
        wg_refs = rest[ne + nx + nd + no + nr:ne + nx + nd + no + nr + nx]
        scratch = rest[ne + nx + nd + no + nr + nx:]
        av = None if a_halves else a_ref[...].astype(BF16)
        if nx:
            wg_acc = scratch[-1]
            i = pl.program_id(0)
            pw_ = lax.dot_general(x_refs[0][...].astype(BF16), av, (((0,), (0,)), ((), ())), preferred_element_type=F32)

            @pl.when(i == 0)
            def _():
                wg_acc[...] = pw_

            @pl.when(i > 0)
            def _():
                wg_acc[...] += pw_

            @pl.when(i == nm - 1)
            def _():
                if wgrad_rows:
                    wg_refs[0][...] = wg_acc[...].astype(BF16)
                else:
                    ks = kdim // N_CHIPS
                    for s in range(N_CHIPS):
                        wg_refs[0][s] = wg_acc[:, s * ks:(s + 1) * ks].astype(BF16)
        if b_stacked and not tb:
            parts = [lax.dot_general(av, b_ref[s].astype(BF16), dims, preferred_element_type=F32) for s in range(gn)]
        elif b_stacked and tb:
            p = None
            for s in range(gk):
                if a_halves:
                    a_s = a_ref[s // 2, :, (s % 2) * ns:(s % 2 + 1) * ns].astype(BF16)
                else:
                    a_s = av[:, s * ns:(s + 1) * ns]
                q = lax.dot_general(a_s, b_ref[s].astype(BF16), dims, preferred_element_type=F32)
                p = q if p is None else p + q
            parts = [p]
        else:
            parts = [lax.dot_general(av, b_ref[...].astype(BF16), dims, preferred_element_type=F32)]

        def finish(acc_parts):
            if epi is None and out_stacked:
                acc = acc_parts[0]
                for s in range(gn):
                    o_refs[0][s] = acc[:, s * ns:(s + 1) * ns].astype(out_dtype)
            elif epi is None:
                w = acc_parts[0].shape[1]
                for s, part in enumerate(acc_parts):
                    o_refs[0][:, s * w:(s + 1) * w] = part.astype(out_dtype)
            else:
                acc = acc_parts[0] if len(acc_parts) == 1 else jnp.concatenate(acc_parts, axis=1)
                vals = epi(acc, *[r[...] for r in e_refs])
                if nr:
                    vals, reds = vals

                    @pl.when(pl.program_id(0) == 0)
                    def _():
                        for r in r_refs:
                            r[...] = jnp.zeros_like(r)
                    for r, val in zip(r_refs, reds):
                        r[...] += val
                for o_ref, val in zip(o_refs, vals):
                    if isinstance(val, tuple) and val[0] == "at":
                        o_ref[:, val[1]:val[1] + val[2].shape[1]] = val[2].astype(o_ref.dtype)
                    else:
                        o_ref[...] = val.astype(o_ref.dtype)

        if nk == 1:
            finish(parts)
        else:
            acc_ref = scratch[0]
            k = pl.program_id(2)
            w = parts[0].shape[1]

            @pl.when(k == 0)
            def _():
                for s, part in enumerate(parts):
                    acc_ref[:, s * w:(s + 1) * w] = part

            @pl.when(k > 0)
            def _():
                for s, part in enumerate(parts):
                    acc_ref[:, s * w:(s + 1) * w] += part

            @pl.when(k == nk - 1)
            def _():
                finish([acc_ref[...]])

    def _ij(f):
        return ij(lambda i, j, k: f(i, j))

    if b_resident:
        assert nk == 1 and (nn == 1 or j_outer), name
        b_spec = pl.BlockSpec(b_spec.block_shape, b_spec.index_map, pipeline_mode=pl.Buffered(1))
    in_specs = [a_spec, b_spec] + [pl.BlockSpec(blk, _ij(f)) for (_, blk, f) in epi_ins]
    if nx:
        in_specs.append(pl.BlockSpec((tm, wgrad.shape[1]), lambda i, j, k: (i, 0)))
    in_specs += [ANY_SPEC] * nd
    out_specs = [pl.BlockSpec(blk, _ij(f)) for (_, _, blk, f) in epi_outs]
    out_specs += [pl.BlockSpec((1, w), lambda *_: (0, 0)) for w in epi_reds]
    out_shape = [jax.ShapeDtypeStruct(s, dt) for (s, dt, _, _) in epi_outs] + [jax.ShapeDtypeStruct((1, w), F32) for w in epi_reds]
    scratch = [pltpu.VMEM((tm, tn), F32)] if nk > 1 else []
    if nx:
        wg_shape = (wgrad.shape[1], kdim) if wgrad_rows else (N_CHIPS, wgrad.shape[1], kdim // N_CHIPS)
        out_specs.append(pl.BlockSpec(wg_shape, lambda *_: (0,) * len(wg_shape)))
        out_shape.append(jax.ShapeDtypeStruct(wg_shape, BF16))
        scratch.append(pltpu.VMEM((wgrad.shape[1], kdim), F32))
    grid = (nn, nm, nk) if j_outer else (nm, nn, nk)
    sem = ("arbitrary",) * 3 if (nr or nx) else ("parallel", "parallel", "arbitrary")
    res = _pcall(body, name=name, grid=grid, in_specs=in_specs, out_specs=out_specs, out_shape=out_shape, scratch_shapes=scratch,
                 compiler_params=_params(*sem))(a, b, *[x for x, _, _ in epi_ins], *([wgrad] * nx), *deps)
    return res[0] if len(res) == 1 else res


def _norm_fwd(name, x, g, sc, sh, deps=()):
    d = x.shape[1]

    def fn(iv, pv):
        (xv,), (gv, scv, shv) = iv, pv
        r = lax.rsqrt(jnp.mean(xv * xv, axis=-1, keepdims=True) + EPS)
        return [xv * r * gv * (1.0 + scv) + shv], []

    return _rowwise(name, fn, [(x, d, 0)], [g, sc, sh], [(d, BF16)], [], ROWS, deps=deps)[0]


def _adamw(name, w, g, m, v, tm=256):
    c = w.shape[1]

    def fn(iv, pv):
        wv, gv, mv, vv = iv
        mn = ADAM_B1 * mv + (1.0 - ADAM_B1) * gv
        vn = ADAM_B2 * vv + (1.0 - ADAM_B2) * (gv * gv)
        m_hat = mn / (1.0 - ADAM_B1 ** ADAM_STEP)
        v_hat = vn / (1.0 - ADAM_B2 ** ADAM_STEP)
        delta = -ADAM_LR * (m_hat / (jnp.sqrt(v_hat) + ADAM_EPS) + ADAM_WD * wv)
        return [gv, delta, mn, vn], []

    return _rowwise(name, fn, [(w, c, 0), (g, c, 0), (m, c, 0), (v, c, 0)], [], [(c, F32)] * 4, [], _pick(w.shape[0], tm, 8))


def _adamw_group(name, ws, gs, ms, vs, tm=256):
    n = len(ws)

    def update(w_ref, g_ref, m_ref, v_ref, og_ref, od_ref, om_ref, ov_ref):
        wv, gv, mv, vv = w_ref[...], g_ref[...], m_ref[...], v_ref[...]
        mn = ADAM_B1 * mv + (1.0 - ADAM_B1) * gv
        vn = ADAM_B2 * vv + (1.0 - ADAM_B2) * (gv * gv)
        m_hat = mn / (1.0 - ADAM_B1 ** ADAM_STEP)
        v_hat = vn / (1.0 - ADAM_B2 ** ADAM_STEP)
        og_ref[...] = gv
        od_ref[...] = -ADAM_LR * (m_hat / (jnp.sqrt(v_hat) + ADAM_EPS) + ADAM_WD * wv)
        om_ref[...] = mn
        ov_ref[...] = vn

    def body(*refs):
        for i in range(n):
            r, c = ws[i].shape
            rows = _pick(r, tm, 8)
            spec = pl.BlockSpec((rows, c), lambda k: (k, 0))
            pltpu.emit_pipeline(update, grid=(r // rows,), in_specs=[spec] * 4, out_specs=[spec] * 4)(
                refs[i], refs[n + i], refs[2 * n + i], refs[3 * n + i], *refs[4 * n + 4 * i:4 * n + 4 * i + 4])

    outs = _pcall(body, name=name, in_specs=[ANY_SPEC] * (4 * n), out_specs=[ANY_SPEC] * (4 * n),
                  out_shape=[jax.ShapeDtypeStruct(w.shape, F32) for w in ws for _ in range(4)],
                  compiler_params=pltpu.CompilerParams(vmem_limit_bytes=VMEM_LIMIT))(*ws, *gs, *ms, *vs)
    return [tuple(outs[4 * i:4 * i + 4]) for i in range(n)]


def _unpack_plan(shape):
    n = math.prod(shape)
    if len(shape) == 2 and shape[0] == 1 and n % 128 == 0:
        return [((slice(None), slice(128 * r, 128 * (r + 1))), slice(r, r + 1), slice(None)) for r in range(n // 128)]
    if len(shape) == 2 and shape[0] == 1 and n < 128:
        return [((slice(None), slice(None)), slice(0, 1), slice(0, n))]
    if len(shape) == 1 and n % 128 == 0:
        return [((slice(128 * r, 128 * (r + 1)),), r, slice(None)) for r in range(n // 128)]
    if len(shape) == 3 and shape[0] == 1 and shape[2] == 64:
        return [((0, slice(2 * r + h, 2 * r + h + 1), slice(None)), slice(r, r + 1), slice(64 * h, 64 * (h + 1)))
                for r in range(n // 128) for h in range(2)]
    if len(shape) == 4 and shape[0] == 1 and shape[2:] == (128, 128):
        return [((0, k), slice(128 * k, 128 * (k + 1)), slice(None)) for k in range(shape[1])]
    return None


def _adamw_small(name, w, g8, m, v, shapes):
    r = w.shape[0]
    plans, rows0, off = [], [], 0
    for s in shapes:
        plans.append(_unpack_plan(s))
        rows0.append(off // 128)
        off += math.prod(s) + (-math.prod(s)) % 128
    direct = [i for i, p in enumerate(plans) if p is not None]

    def body(w_ref, g_ref, m_ref, v_ref, *out):
        packed = out[4 * len(direct):]
        gv = g_ref[0:r, :]
        for k in range(1, N_DEV):
            gv = gv + g_ref[k * r:(k + 1) * r, :]
        mn = ADAM_B1 * m_ref[...] + (1.0 - ADAM_B1) * gv
        vn = ADAM_B2 * v_ref[...] + (1.0 - ADAM_B2) * (gv * gv)
        m_hat = mn / (1.0 - ADAM_B1 ** ADAM_STEP)
        v_hat = vn / (1.0 - ADAM_B2 ** ADAM_STEP)
        packed[0][...] = gv
        packed[1][...] = -ADAM_LR * (m_hat / (jnp.sqrt(v_hat) + ADAM_EPS) + ADAM_WD * w_ref[...])
        packed[2][...] = mn
        packed[3][...] = vn
        for di, i in enumerate(direct):
            for kind in range(4):
                o_ref, src = out[4 * di + kind], packed[kind]
                for o_idx, row, lanes in plans[i]:
                    row = (rows0[i] + row) if isinstance(row, int) else slice(rows0[i] + row.start, rows0[i] + row.stop)
                    o_ref[o_idx] = src[row, lanes]

    out_shape = [jax.ShapeDtypeStruct(shapes[i], F32) for i in direct for _ in range(4)]
    out_shape += [jax.ShapeDtypeStruct((r, 128), F32)] * 4
    res = _pcall(body, name=name, out_shape=out_shape,
                 compiler_params=pltpu.CompilerParams(vmem_limit_bytes=VMEM_LIMIT))(w, g8, m, v)
    per_param = [None] * len(shapes)
    for di, i in enumerate(direct):
        per_param[i] = res[4 * di:4 * di + 4]
    return per_param, res[4 * len(direct):]


def _pool_fwd(name, uin, pool_w, pool_b, pool_scale, w_up, pw, tm=ROWS):
    t = uin.shape[0]
    tm = min(tm, t)
    ng = len(POOL_WINDOWS)
    gw = pw // ng
    ns = w_up.shape[2]

    def body(u_ref, w_ref, b_ref, s_ref, wu_ref, p_ref, z_ref, y_ref, ext):
        i = pl.program_id(0)

        @pl.when(i == 0)
        def _():
            ext[0:POOL_HALO, :] = jnp.zeros((POOL_HALO, pw), F32)

        u = u_ref[...].astype(F32)
        ext[POOL_HALO:POOL_HALO + tm, :] = u
        pos = i * tm + lax.broadcasted_iota(jnp.int32, (tm, 1), 0)
        for k, win in enumerate(POOL_WINDOWS):
            cols = slice(k * gw, (k + 1) * gw)
            acc = u[:, cols]
            for j in range(1, win):
                acc = acc + ext[POOL_HALO - j:POOL_HALO - j + tm, cols]
            cnt = jnp.minimum(pos + 1, win).astype(F32)
            z = acc / cnt - u[:, cols]
            zp = jnp.dot(z.astype(BF16), w_ref[k].astype(BF16), preferred_element_type=F32) + b_ref[:, cols]
            p_ref[:, cols] = (zp * s_ref[:, cols]).astype(BF16)
            z_ref[:, cols] = z.astype(BF16)
        ext[0:POOL_HALO, :] = u[tm - POOL_HALO:tm, :]
        pv = p_ref[...]
        for s in range(N_CHIPS):
            y_ref[:, s * ns:(s + 1) * ns] = jnp.dot(pv, wu_ref[s], preferred_element_type=F32).astype(BF16)

    full3 = lambda i: (0, 0, 0)
    return _pcall(
        body, name=name, grid=(t // tm,),
        in_specs=[pl.BlockSpec((tm, pw), lambda i: (i, 0)), pl.BlockSpec(pool_w.shape, full3),
                  pl.BlockSpec(pool_b.shape, lambda i: (0, 0)), pl.BlockSpec(pool_scale.shape, lambda i: (0, 0)),
                  pl.BlockSpec(w_up.shape, full3)],
        out_specs=[pl.BlockSpec((tm, pw), lambda i: (i, 0))] * 2 + [pl.BlockSpec((tm, N_CHIPS * ns), lambda i: (i, 0))],
        out_shape=[jax.ShapeDtypeStruct((t, pw), BF16)] * 2 + [jax.ShapeDtypeStruct((t, N_CHIPS * ns), BF16)],
        scratch_shapes=[pltpu.VMEM((POOL_HALO + tm, pw), F32)],
        compiler_params=_params("arbitrary"))(uin, pool_w, pool_b, pool_scale, w_up)


def _pool_bwd(name, dy, z, p, pool_w, pool_b, pool_scale, w_up, duin, tm=ROWS):
    t, pw = z.shape
    tm = min(tm, t)
    nb = t // tm
    ng = len(POOL_WINDOWS)
    gw = pw // ng
    ns = w_up.shape[2]

    def body(dy_ref, z_ref, p_ref, w_ref, b_ref, s_ref, wu_ref, duin_ref, du_ref, dw_ref, db_ref, ds_ref, dwu_ref, ext, dwu_acc):
        i = pl.program_id(0)

        @pl.when(i == 0)
        def _():
            ext[tm:tm + POOL_HALO, :] = jnp.zeros((POOL_HALO, pw), F32)
            dw_ref[...] = jnp.zeros_like(dw_ref)
            db_ref[...] = jnp.zeros_like(db_ref)
            ds_ref[...] = jnp.zeros_like(ds_ref)
            dwu_acc[...] = jnp.zeros_like(dwu_acc)

        dwu_acc[...] += lax.dot_general(p_ref[...], dy_ref[...], (((0,), (0,)), ((), ())), preferred_element_type=F32)

        @pl.when(i == nb - 1)
        def _():
            for s in range(N_CHIPS):
                dwu_ref[s] = dwu_acc[:, s * ns:(s + 1) * ns].astype(BF16)

        pos = (nb - 1 - i) * tm + lax.broadcasted_iota(jnp.int32, (tm, 1), 0)
        dp = None
        for s in range(N_CHIPS):
            q = lax.dot_general(dy_ref[:, s * ns:(s + 1) * ns], wu_ref[s], (((1,), (1,)), ((), ())), preferred_element_type=F32)
            dp = q if dp is None else dp + q
        for k, win in enumerate(POOL_WINDOWS):
            cols = slice(k * gw, (k + 1) * gw)
            zk = z_ref[:, cols]
            dpk = dp[:, cols]
            wk = w_ref[k].astype(BF16)
            zp = jnp.dot(zk, wk, preferred_element_type=F32) + b_ref[:, cols]
            ds_ref[:, cols] += _colsum(dpk * zp)
            dzp = dpk * s_ref[:, cols]
            db_ref[:, cols] += _colsum(dzp)
            dzpb = dzp.astype(BF16)
            dz = lax.dot_general(dzpb, wk, (((1,), (1,)), ((), ())), preferred_element_type=F32)
            dw_ref[k] += lax.dot_general(zk, dzpb, (((0,), (0,)), ((), ())), preferred_element_type=F32)
            cnt = jnp.minimum(pos + 1, win).astype(F32)
            r = dz / cnt
            ext[0:tm, cols] = r
            acc = r - dz
            for j in range(1, win):
                acc = acc + ext[j:j + tm, cols]
            du_ref[:, cols] = acc.astype(BF16)
        ext[tm:tm + POOL_HALO, :] = ext[0:POOL_HALO, :]

    rev = lambda i: (nb - 1 - i, 0)
    full3 = lambda i: (0, 0, 0)
    return _pcall(
        body, name=name, grid=(nb,),
        in_specs=[pl.BlockSpec((tm, N_CHIPS * ns), rev), pl.BlockSpec((tm, pw), rev), pl.BlockSpec((tm, pw), rev),
                  pl.BlockSpec(pool_w.shape, full3), pl.BlockSpec(pool_b.shape, lambda i: (0, 0)),
                  pl.BlockSpec(pool_scale.shape, lambda i: (0, 0)), pl.BlockSpec(w_up.shape, full3), ANY_SPEC],
        out_specs=[pl.BlockSpec((tm, pw), rev), pl.BlockSpec(pool_w.shape, full3),
                   pl.BlockSpec((1, pw), lambda i: (0, 0)), pl.BlockSpec((1, pw), lambda i: (0, 0)), pl.BlockSpec(w_up.shape, full3)],
        out_shape=[jax.ShapeDtypeStruct(duin.shape, BF16), jax.ShapeDtypeStruct(pool_w.shape, F32),
                   jax.ShapeDtypeStruct((1, pw), F32), jax.ShapeDtypeStruct((1, pw), F32), jax.ShapeDtypeStruct(w_up.shape, BF16)],
        scratch_shapes=[pltpu.VMEM((tm + POOL_HALO, pw), F32), pltpu.VMEM((pw, N_CHIPS * ns), F32)], input_output_aliases={7: 0},
        compiler_params=_params("arbitrary"))(dy, z, p, pool_w, pool_b, pool_scale, w_up, duin)


def _ssm_disc(lrl, li, ldt):
    lr = -jnp.exp(lrl)
    dt = jnp.exp(ldt)
    mag = jnp.exp(lr * dt)
    ang = li * dt
    ab_re = mag * jnp.cos(ang)
    ab_im = mag * jnp.sin(ang)
    num_re = ab_re - 1.0
    num_im = ab_im
    den = lr * lr + li * li
    f_re = (num_re * lr + num_im * li) / den
    f_im = (num_im * lr - num_re * li) / den
    return lr, dt, mag, ang, ab_re, ab_im, num_re, num_im, den, f_re, f_im


def _ssm_prep(name, lrl, li, ldt, b_re, b_im):
    gn, h = b_re.shape

    def body(lrl_ref, li_ref, ldt_ref, br_ref, bi_ref, lrdt_ref, ang_ref, bbr_ref, bbi_ref):
        lr, dt, _, ang, _, _, _, _, _, f_re, f_im = _ssm_disc(lrl_ref[...], li_ref[...], ldt_ref[...])
        lrdt_ref[...] = lr * dt
        ang_ref[...] = ang
        br, bi = br_ref[...], bi_ref[...]
        bbr_ref[...] = f_re * br - f_im * bi
        bbi_ref[...] = f_re * bi + f_im * br

    col = jax.ShapeDtypeStruct((gn, 1), F32)
    mat = jax.ShapeDtypeStruct((gn, h), F32)
    return _pcall(body, name=name, out_shape=[col, col, mat, mat])(lrl, li, ldt, b_re, b_im)


def _ssm_param_bwd(name, lrl, li, ldt, b_re, b_im, g_abre, g_abim, g_bbre, g_bbim):
    gn, h = b_re.shape

    def body(lrl_ref, li_ref, ldt_ref, br_ref, bi_ref, gar_ref, gai_ref, gbr_ref, gbi_ref,
             dlrl_ref, dli_ref, dldt_ref, dbr_ref, dbi_ref):
        li_v = li_ref[...]
        lr, dt, mag, ang, ab_re, ab_im, num_re, num_im, den, f_re, f_im = _ssm_disc(lrl_ref[...], li_v, ldt_ref[...])
        br, bi = br_ref[...], bi_ref[...]
        gbr, gbi = gbr_ref[...], gbi_ref[...]
        g_fre = jnp.sum(gbr * br + gbi * bi, axis=1, keepdims=True)
        g_fim = jnp.sum(gbi * br - gbr * bi, axis=1, keepdims=True)
        dbr_ref[...] = gbr * f_re + gbi * f_im
        dbi_ref[...] = gbi * f_re - gbr * f_im
        g_num_re = (g_fre * lr - g_fim * li_v) / den
        g_num_im = (g_fre * li_v + g_fim * lr) / den
        g_den = -(g_fre * f_re + g_fim * f_im) / den
        g_lr = (g_fre * num_re + g_fim * num_im) / den + g_den * 2.0 * lr
        g_li = (g_fre * num_im - g_fim * num_re) / den + g_den * 2.0 * li_v
        g_are = gar_ref[...] + g_num_re
        g_aim = gai_ref[...] + g_num_im
        g_mag = g_are * jnp.cos(ang) + g_aim * jnp.sin(ang)
        g_ang = g_aim * ab_re - g_are * ab_im
        g_lrdt = g_mag * mag
        g_lr = g_lr + g_lrdt * dt
        g_dt = g_lrdt * lr + g_ang * li_v
        g_li = g_li + g_ang * dt
        dlrl_ref[...] = g_lr * lr
        dli_ref[...] = g_li
        dldt_ref[...] = g_dt * dt

    col = jax.ShapeDtypeStruct((gn, 1), F32)
    mat = jax.ShapeDtypeStruct((gn, h), F32)
    return _pcall(body, name=name, out_shape=[col, col, col, mat, mat])(lrl, li, ldt, b_re, b_im, g_abre, g_abim, g_bbre, g_bbim)


def _pow_rows(lrdt, ang, k):
    mag = jnp.exp(k * lrdt)
    return mag * jnp.cos(k * ang), mag * jnp.sin(k * ang)


def _ssm_chunk(t):
    return 256 if t >= 2048 else 128


def _to_segments(dst, srcs, jn):
    for q, src in enumerate(srcs):
        for j in range(jn):
            dst[8 * j:8 * j + 8, 128 * q:128 * (q + 1)] = src[pl.ds(j, 8, stride=jn), :]


def _from_segments(dst, src, q, jn, dtype):
    for s in range(8):
        dst[s * jn:(s + 1) * jn, 128 * q:128 * (q + 1)] = src[q, pl.ds(s, jn, stride=8), :].astype(dtype)


def _fill_rows8(dst_re, dst_im, v_re, v_im):
    for j in range(v_re.shape[0]):
        dst_re[8 * j:8 * j + 8, :] = jnp.broadcast_to(v_re[j:j + 1, :], (8, v_re.shape[1]))
        dst_im[8 * j:8 * j + 8, :] = jnp.broadcast_to(v_im[j:j + 1, :], (8, v_im.shape[1]))


def _cmul(ar, ai, br, bi):
    return ar * br - ai * bi, ar * bi + ai * br


def _cmul_conj(ar, ai, br, bi):
    return ar * br + ai * bi, ar * bi - ai * br


def _ssm_fwd(name, uin, lrdt, ang, bb_re, bb_im, cc_re, cc_im, d_skip, sw):
    t = uin.shape[0]
    gn = lrdt.shape[1]
    lc = _ssm_chunk(t)
    jn = lc // 8
    ub, sb = sw // SSM_BLOCKS, gn // SSM_BLOCKS
    nq = sw // 128
    assert ub == 128 and nq == SSM_BLOCKS

    def body(u_ref, lrdt_ref, ang_ref, bbr_ref, bbi_ref, ccr_ref, cci_ref, d_ref, y_ref, ge_ref, sre_ref, sim_ref,
             p_re, p_im, a_re, a_im, up, yp, cst_re, cst_im, car_re, car_im):
        i = pl.program_id(0)
        lrdt_v, ang_v = lrdt_ref[...], ang_ref[...]

        @pl.when(i == 0)
        def _():
            k = (lax.broadcasted_iota(jnp.int32, (jn, 1), 0) + 1).astype(F32)
            _fill_rows8(p_re, p_im, *_pow_rows(lrdt_v, ang_v, k))
            car_re[...] = jnp.zeros_like(car_re)
            car_im[...] = jnp.zeros_like(car_im)

        for q in range(nq):
            yp[q] = u_ref[:, 128 * q:128 * (q + 1)].astype(F32)
        _to_segments(up, [yp.at[q] for q in range(nq)], jn)
        u = up[...]
        ubf = u.astype(BF16)
        for q in range(SSM_BLOCKS):
            uq = ubf[:, q * ub:(q + 1) * ub]
            a_re[:, q * sb:(q + 1) * sb] = jnp.dot(uq, bbr_ref[q], preferred_element_type=F32)
            a_im[:, q * sb:(q + 1) * sb] = jnp.dot(uq, bbi_ref[q], preferred_element_type=F32)
        a1r, a1i = _pow_rows(lrdt_v, ang_v, 1.0)
        ajr, aji = _pow_rows(lrdt_v, ang_v, float(jn))
        for q in range(0, SSM_BLOCKS, 2):
            cols = slice(q * sb, (q + 2) * sb)
            ar8 = jnp.broadcast_to(a1r[:, cols], (8, 2 * sb))
            ai8 = jnp.broadcast_to(a1i[:, cols], (8, 2 * sb))

            def step(j, carry, cols=cols, ar8=ar8, ai8=ai8):
                sr, si = carry
                rows = pl.ds(pl.multiple_of(j * 8, 8), 8)
                mr, mi = _cmul(ar8, ai8, sr, si)
                nr, ni = mr + a_re[rows, cols], mi + a_im[rows, cols]
                a_re[rows, cols] = nr
                a_im[rows, cols] = ni
                return nr, ni

            lax.fori_loop(1, jn, step, (a_re[0:8, cols], a_im[0:8, cols]), unroll=8)
        er, ei = a_re[lc - 8:lc, :], a_im[lc - 8:lc, :]
        gr, gi = car_re[...], car_im[...]
        for s in range(8):
            cst_re[s:s + 1, :] = gr
            cst_im[s:s + 1, :] = gi
            mr, mi = _cmul(ajr, aji, gr, gi)
            gr, gi = mr + er[s:s + 1, :], mi + ei[s:s + 1, :]
        car_re[...] = gr
        car_im[...] = gi
        for q in range(SSM_BLOCKS):
            cols = slice(q * sb, (q + 1) * sb)
            cr = jnp.tile(cst_re[:, cols], (jn, 1))
            ci = jnp.tile(cst_im[:, cols], (jn, 1))
            mr, mi = _cmul(p_re[:, cols], p_im[:, cols], cr, ci)
            srb, sib = (a_re[:, cols] + mr).astype(BF16), (a_im[:, cols] + mi).astype(BF16)
            sre_ref[:, cols] = srb
            sim_ref[:, cols] = sib
            ycols = slice(q * ub, (q + 1) * ub)
            y = (jnp.dot(srb, ccr_ref[q], preferred_element_type=F32) - jnp.dot(sib, cci_ref[q], preferred_element_type=F32)
                 + d_ref[:, ycols] * u[:, ycols])
            yp[q] = y
            _from_segments(y_ref, yp, q, jn, F32)
            yt = y_ref[:, ycols]
            ge_ref[:, ycols] = (0.5 * yt * (1.0 + lax.erf(yt * (1.0 / math.sqrt(2.0))))).astype(BF16)

    row = lambda i: (0, 0)
    blk3 = lambda i: (0, 0, 0)
    return _pcall(
        body, name=name, grid=(t // lc,),
        in_specs=[pl.BlockSpec((lc, sw), lambda i: (i, 1)), pl.BlockSpec((1, gn), row), pl.BlockSpec((1, gn), row),
                  pl.BlockSpec(bb_re.shape, blk3), pl.BlockSpec(bb_im.shape, blk3),
                  pl.BlockSpec(cc_re.shape, blk3), pl.BlockSpec(cc_im.shape, blk3), pl.BlockSpec((1, sw), row)],
        out_specs=[pl.BlockSpec((lc, sw), lambda i: (i, 0)), pl.BlockSpec((lc, sw), lambda i: (i, 0)),
                   pl.BlockSpec((lc, gn), lambda i: (i, 0)), pl.BlockSpec((lc, gn), lambda i: (i, 0))],
        out_shape=[jax.ShapeDtypeStruct((t, sw), F32), jax.ShapeDtypeStruct((t, sw), BF16),
                   jax.ShapeDtypeStruct((t, gn), BF16), jax.ShapeDtypeStruct((t, gn), BF16)],
        scratch_shapes=[pltpu.VMEM((lc, gn), F32), pltpu.VMEM((lc, gn), F32), pltpu.VMEM((lc, gn), F32), pltpu.VMEM((lc, gn), F32),
                        pltpu.VMEM((lc, sw), F32), pltpu.VMEM((nq, lc, 128), F32),
                        pltpu.VMEM((8, gn), F32), pltpu.VMEM((8, gn), F32), pltpu.VMEM((1, gn), F32), pltpu.VMEM((1, gn), F32)],
        compiler_params=_params("arbitrary"))(uin, lrdt, ang, bb_re, bb_im, cc_re, cc_im, d_skip)


def _ssm_bwd(name, dge, y, uin, s_re, s_im, lrdt, ang, bbt_re, bbt_im, cct_re, cct_im, d_skip, sw, duin):
    t = uin.shape[0]
    gn = lrdt.shape[1]
    lc = _ssm_chunk(t)
    nb = t // lc
    jn = lc // 8
    ub, sb = sw // SSM_BLOCKS, gn // SSM_BLOCKS
    nq = sw // 128
    tail = 16

    def body(dge_ref, y_ref, u_ref, sre_ref, sim_ref, tre_ref, tim_ref, lrdt_ref, ang_ref, btr_ref, bti_ref, ctr_ref, cti_ref,
             d_ref, duin_ref, du_ref, dar_ref, dai_ref, dbr_ref, dbi_ref, dcr_ref, dci_ref, dd_ref,
             q_re, q_im, a_re, a_im, dyp, up, dys, us, dup, cst_re, cst_im, sp_re, sp_im, car_re, car_im):
        i = pl.program_id(0)
        lrdt_v, ang_v = lrdt_ref[...], ang_ref[...]

        @pl.when(i == 0)
        def _():
            k = (jn - lax.broadcasted_iota(jnp.int32, (jn, 1), 0)).astype(F32)
            _fill_rows8(q_re, q_im, *_pow_rows(lrdt_v, ang_v, k))
            car_re[...] = jnp.zeros_like(car_re)
            car_im[...] = jnp.zeros_like(car_im)
            for r in (dar_ref, dai_ref, dbr_ref, dbi_ref, dcr_ref, dci_ref, dd_ref):
                r[...] = jnp.zeros_like(r)

        yv = y_ref[...]
        ut = u_ref[...].astype(F32)
        cdf =0.5 * (1.0 + lax.erf(yv * (1.0 / math.sqrt(2.0))))
        pdf = jnp.exp(-0.5 * yv * yv) * (1.0 / math.sqrt(2.0 * math.pi))
        dyt = dge_ref[...].astype(F32) * (cdf + yv * pdf)
        dd_ref[...] += _colsum(dyt * ut)
        for q in range(nq):
            dys[q] = dyt[:, 128 * q:128 * (q + 1)]
            us[q] = ut[:, 128 * q:128 * (q + 1)]
        _to_segments(dyp, [dys.at[q] for q in range(nq)], jn)
        _to_segments(up, [us.at[q] for q in range(nq)], jn)
        dy = dyp[...]
        u = up[...]
        dyb = dy.astype(BF16)
        ubf = u.astype(BF16)
        for q in range(SSM_BLOCKS):
            dq = dyb[:, q * ub:(q + 1) * ub]
            a_re[:, q * sb:(q + 1) * sb] = jnp.dot(dq, ctr_ref[q], preferred_element_type=F32)
            a_im[:, q * sb:(q + 1) * sb] = -jnp.dot(dq, cti_ref[q], preferred_element_type=F32)
        a1r, a1i = _pow_rows(lrdt_v, ang_v, 1.0)
        ajr, aji = _pow_rows(lrdt_v, ang_v, float(jn))
        for q in range(0, SSM_BLOCKS, 2):
            cols = slice(q * sb, (q + 2) * sb)
            ar8 = jnp.broadcast_to(a1r[:, cols], (8, 2 * sb))
            ai8 = jnp.broadcast_to(a1i[:, cols], (8, 2 * sb))

            def step(jj, carry, cols=cols, ar8=ar8, ai8=ai8):
                sr, si = carry
                rows = pl.ds(pl.multiple_of((jn - 2 - jj) * 8, 8), 8)
                mr, mi = _cmul_conj(ar8, ai8, sr, si)
                nr, ni = mr + a_re[rows, cols], mi + a_im[rows, cols]
                a_re[rows, cols] = nr
                a_im[rows, cols] = ni
                return nr, ni

            lax.fori_loop(0, jn - 1, step, (a_re[lc - 8:lc, cols], a_im[lc - 8:lc, cols]), unroll=8)
        er, ei = a_re[0:8, :], a_im[0:8, :]
        hr, hi = car_re[...], car_im[...]
        for s in range(7, -1, -1):
            cst_re[s:s + 1, :] = hr
            cst_im[s:s + 1, :] = hi
            mr, mi = _cmul_conj(ajr, aji, hr, hi)
            hr, hi = mr + er[s:s + 1, :], mi + ei[s:s + 1, :]
        car_re[...] = hr
        car_im[...] = hi
        first = (i == nb - 1).astype(F32)
        sp_re[0:tail, :] = tre_ref[...].astype(F32) * (1.0 - first)
        sp_im[0:tail, :] = tim_ref[...].astype(F32) * (1.0 - first)
        sp_re[tail:tail + 8, :] = sre_ref[lc - tail:lc, :].astype(F32)[tail - 8:tail]
        sp_im[tail:tail + 8, :] = sim_ref[lc - tail:lc, :].astype(F32)[tail - 8:tail]
        tn_dims = (((0,), (0,)), ((), ()))
        for q in range(SSM_BLOCKS):
            cols = slice(q * sb, (q + 1) * sb)
            ycols = slice(q * ub, (q + 1) * ub)
            cr = jnp.tile(cst_re[:, cols], (jn, 1))
            ci = jnp.tile(cst_im[:, cols], (jn, 1))
            mr, mi = _cmul_conj(q_re[:, cols], q_im[:, cols], cr, ci)
            lam_r, lam_i = a_re[:, cols] + mr, a_im[:, cols] + mi
            s_r, s_i = sre_ref[:, cols], sim_ref[:, cols]
            p0r, p0i = sp_re[tail - 1:tail + 7, cols], sp_im[tail - 1:tail + 7, cols]
            l0r, l0i, l1r, l1i = lam_r[0:8], lam_i[0:8], lam_r[8:lc], lam_i[8:lc]
            pvr, pvi = s_r.astype(F32)[0:lc - 8], s_i.astype(F32)[0:lc - 8]
            dar_ref[:, cols] += _colsum(l1r * pvr + l1i * pvi) + _colsum(l0r * p0r + l0i * p0i)
            dai_ref[:, cols] += _colsum(l1i * pvr - l1r * pvi) + _colsum(l0i * p0r - l0r * p0i)
            lrb, lib = lam_r.astype(BF16), lam_i.astype(BF16)
            dup[q] = (jnp.dot(lrb, btr_ref[q], preferred_element_type=F32)
                      + jnp.dot(lib, bti_ref[q], preferred_element_type=F32) + d_ref[:, ycols] * dy[:, ycols])
            _from_segments(du_ref, dup, q, jn, BF16)
            uq = ubf[:, ycols]
            dbr_ref[q] += lax.dot_general(uq, lrb, tn_dims, preferred_element_type=F32)
            dbi_ref[q] += lax.dot_general(uq, lib, tn_dims, preferred_element_type=F32)
            dq = dyb[:, ycols]
            dcr_ref[q] += lax.dot_general(s_r.astype(BF16), dq, tn_dims, preferred_element_type=F32)
            dci_ref[q] -= lax.dot_general(s_i.astype(BF16), dq, tn_dims, preferred_element_type=F32)

    rev = lambda i: (nb - 1 - i, 0)
    tailmap = lambda i: (jnp.maximum((nb - 1 - i) * (lc // tail) - 1, 0), 0)
    row = lambda i: (0, 0)
    blk3 = lambda i: (0, 0, 0)
    return _pcall(
        body, name=name, grid=(nb,),
        in_specs=[pl.BlockSpec((lc, sw), rev), pl.BlockSpec((lc, sw), rev), pl.BlockSpec((lc, sw), lambda i: (nb - 1 - i, 1)),
                  pl.BlockSpec((lc, gn), rev), pl.BlockSpec((lc, gn), rev),
                  pl.BlockSpec((tail, gn), tailmap), pl.BlockSpec((tail, gn), tailmap),
                  pl.BlockSpec((1, gn), row), pl.BlockSpec((1, gn), row),
                  pl.BlockSpec(bbt_re.shape, blk3), pl.BlockSpec(bbt_im.shape, blk3),
                  pl.BlockSpec(cct_re.shape, blk3), pl.BlockSpec(cct_im.shape, blk3), pl.BlockSpec((1, sw), row), ANY_SPEC],
        out_specs=[pl.BlockSpec((lc, sw), lambda i: (nb - 1 - i, 1)), pl.BlockSpec((1, gn), row), pl.BlockSpec((1, gn), row),
                   pl.BlockSpec((SSM_BLOCKS, ub, sb), blk3), pl.BlockSpec((SSM_BLOCKS, ub, sb), blk3),
                   pl.BlockSpec((SSM_BLOCKS, sb, ub), blk3), pl.BlockSpec((SSM_BLOCKS, sb, ub), blk3),
                   pl.BlockSpec((1, sw), row)],
        out_shape=[jax.ShapeDtypeStruct(duin.shape, BF16), jax.ShapeDtypeStruct((1, gn), F32), jax.ShapeDtypeStruct((1, gn), F32),
                   jax.ShapeDtypeStruct((SSM_BLOCKS, ub, sb), F32), jax.ShapeDtypeStruct((SSM_BLOCKS, ub, sb), F32),
                   jax.ShapeDtypeStruct((SSM_BLOCKS, sb, ub), F32), jax.ShapeDtypeStruct((SSM_BLOCKS, sb, ub), F32),
                   jax.ShapeDtypeStruct((1, sw), F32)],
        scratch_shapes=[pltpu.VMEM((lc, gn), F32), pltpu.VMEM((lc, gn), F32), pltpu.VMEM((lc, gn), F32), pltpu.VMEM((lc, gn), F32),
                        pltpu.VMEM((lc, sw), F32), pltpu.VMEM((lc, sw), F32),
                        pltpu.VMEM((nq, lc, 128), F32), pltpu.VMEM((nq, lc, 128), F32), pltpu.VMEM((nq, lc, 128), F32),
                        pltpu.VMEM((8, gn), F32), pltpu.VMEM((8, gn), F32),
                        pltpu.VMEM((tail + 8, gn), F32), pltpu.VMEM((tail + 8, gn), F32),
                        pltpu.VMEM((1, gn), F32), pltpu.VMEM((1, gn), F32)],
        input_output_aliases={14: 0},
        compiler_params=_params("arbitrary"))(dge, y, uin, s_re, s_im, s_re, s_im, lrdt, ang,
                                               bbt_re, bbt_im, cct_re, cct_im, d_skip, duin)


def _blockdiag_b(bb, sw):
    gpb = (sw // SSM_GROUP) // SSM_BLOCKS
    b4 = bb.reshape(SSM_BLOCKS, gpb, SSM_STATE, SSM_GROUP)
    eye = jnp.eye(gpb, dtype=bb.dtype)
    out = jnp.einsum('qgnh,gk->qghkn', b4, eye)
    return out.reshape(SSM_BLOCKS, gpb * SSM_GROUP, gpb * SSM_STATE)


def _blockdiag_c(cc, sw):
    gpb = (sw // SSM_GROUP) // SSM_BLOCKS
    c4 = cc.reshape(SSM_BLOCKS, gpb, SSM_GROUP, SSM_STATE)
    eye = jnp.eye(gpb, dtype=cc.dtype)
    out = jnp.einsum('qghn,gk->qgnkh', c4, eye)
    return out.reshape(SSM_BLOCKS, gpb * SSM_STATE, gpb * SSM_GROUP)


def _diag_of_b(dbb, sw):
    gpb = (sw // SSM_GROUP) // SSM_BLOCKS
    d5 = dbb.reshape(SSM_BLOCKS, gpb, SSM_GROUP, gpb, SSM_STATE)
    return jnp.einsum('qghgn->qgnh', d5).reshape(SSM_BLOCKS * gpb * SSM_STATE, SSM_GROUP)


def _diag_of_c(dcc, sw):
    gpb = (sw // SSM_GROUP) // SSM_BLOCKS
    d5 = dcc.reshape(SSM_BLOCKS, gpb, SSM_STATE, gpb, SSM_GROUP)
    return jnp.einsum('qgngh->qghn', d5).reshape(SSM_BLOCKS * gpb, SSM_GROUP, SSM_STATE)


def _ada_fwd(name, c_all, w_sh, b_sh):
    nb, d = c_all.shape
    ncol = w_sh.shape[1]
    tn = _pick(ncol, 768)

    def body(c_ref, w_ref, b_ref, o_ref):
        cv = c_ref[...]
        sil = cv * _sigmoid(cv)
        o_ref[...] = jnp.dot(sil, w_ref[...], preferred_element_type=F32, precision=lax.Precision.HIGHEST) + b_ref[...]

    return _pcall(body, name=name, grid=(ncol // tn,),
                  in_specs=[pl.BlockSpec((nb, d), lambda j: (0, 0)), pl.BlockSpec((d, tn), lambda j: (0, j)),
                            pl.BlockSpec((1, tn), lambda j: (0, j))],
                  out_specs=pl.BlockSpec((nb, tn), lambda j: (0, j)),
                  out_shape=jax.ShapeDtypeStruct((nb, ncol), F32), compiler_params=_params("parallel"))(c_all, w_sh, b_sh)


def _ada_bwd_adamw(name, c_all, dmod_sh, w, m, v):
    nb, d = c_all.shape
    ncol = dmod_sh.shape[1]
    tr, tn = _pick(d, 512), _pick(ncol, 768)

    def body(c_ref, g_ref, w_ref, m_ref, v_ref, go_ref, d_ref, mo_ref, vo_ref):
        cv = c_ref[...]
        sil = cv * _sigmoid(cv)
        gv = lax.dot_general(sil, g_ref[...], (((0,), (0,)), ((), ())), preferred_element_type=F32,
                             precision=lax.Precision.HIGHEST)
        mn = ADAM_B1 * m_ref[...] + (1.0 - ADAM_B1) * gv
        vn = ADAM_B2 * v_ref[...] + (1.0 - ADAM_B2) * (gv * gv)
        m_hat = mn / (1.0 - ADAM_B1 ** ADAM_STEP)
        v_hat = vn / (1.0 - ADAM_B2 ** ADAM_STEP)
        go_ref[...] = gv
        d_ref[...] = -ADAM_LR * (m_hat / (jnp.sqrt(v_hat) + ADAM_EPS) + ADAM_WD * w_ref[...])
        mo_ref[...] = mn
        vo_ref[...] = vn

    tile = pl.BlockSpec((tr, tn), lambda i, j: (i, j))
    return _pcall(body, name=name, grid=(d // tr, ncol // tn),
                  in_specs=[pl.BlockSpec((nb, tr), lambda i, j: (0, i)), pl.BlockSpec((nb, tn), lambda i, j: (0, j)), tile, tile, tile],
                  out_specs=[tile] * 4, out_shape=[jax.ShapeDtypeStruct((d, ncol), F32)] * 4,
                  compiler_params=_params("parallel", "parallel"))(c_all, dmod_sh, w, m, v)


def _place():
    return lax.axis_index("x"), lax.axis_index("y"), lax.axis_index("c")


def _allgather_small(name, blk, deps=()):
    m_per, n = blk.shape

    def body(x_ref, *rest):
        out_ref, send_sems, recv_sems, local_sem = rest[len(deps):]
        x, y, c = _place()
        me, sibling = (x, y, c), (x, y, 1 - c)
        chips = [(1 - x, y), (x, 1 - y), (1 - x, 1 - y)]

        def rows(px, py, pc):
            return out_ref.at[pl.ds((4 * px + 2 * py + pc) * m_per, m_per), :]

        def copy(k, block, to, src=None):
            return pltpu.make_async_remote_copy(
                src_ref=rows(*block) if src is None else src, dst_ref=rows(*block),
                send_sem=send_sems.at[k], recv_sem=recv_sems.at[k], device_id=to, device_id_type=MESH)

        mine = pltpu.make_async_copy(x_ref, rows(*me), local_sem)
        mine.start()
        first = [copy(0, me, sibling, src=x_ref)]
        first += [copy(1 + j, me, (*chip, c), src=x_ref) for j, chip in enumerate(chips)]
        for cp in first:
            cp.start()
        passed = [copy(4 + j, (*chip, c), sibling) for j, chip in enumerate(chips)]
        for j, chip in enumerate(chips):
            copy(1 + j, (*chip, c), me).wait_recv()
            passed[j].start()
        copy(0, sibling, me).wait_recv()
        for j, chip in enumerate(chips):
            copy(4 + j, (*chip, 1 - c), me).wait_recv()
        for cp in first + passed:
            cp.wait_send()
        mine.wait()

    return _pcall(body, name=name, out_shape=jax.ShapeDtypeStruct((N_DEV * m_per, n), blk.dtype),
                  in_specs=[pl.BlockSpec(memory_space=pltpu.VMEM)] + [ANY_SPEC] * len(deps),
                  out_specs=pl.BlockSpec(memory_space=pltpu.VMEM),
                  scratch_shapes=[pltpu.SemaphoreType.DMA((7,)), pltpu.SemaphoreType.DMA((7,)), pltpu.SemaphoreType.DMA],
                  compiler_params=pltpu.CompilerParams(vmem_limit_bytes=VMEM_LIMIT))(blk, *deps)


def _other_chips(x, y):
    return [(1 - x, y), (x, 1 - y), (1 - x, 1 - y)]


def _place_shards(shards, s_me):
    return [lax.dynamic_update_slice(lax.empty((N_CHIPS,) + s.shape, s.dtype), s[None], (s_me, 0, 0)) for s in shards]


def _ag_copy(src, land, send, recv, wi, j, chip, x, y, c, tc, both):
    hr = src.shape[0] // 2
    half = pl.ds(pl.multiple_of(c * hr, 16), hr)
    k = 3 * wi + j
    return pltpu.make_async_remote_copy(
        src_ref=src.at[half, :], dst_ref=land.at[2 * x + y, half, :],
        send_sem=send.at[2 * k + tc if both else k], recv_sem=recv.at[2 * k + c if both else k],
        device_id=(chip[0], chip[1], tc), device_id_type=MESH)


def _ag_targets(c, both):
    return (0, 1) if both else (c,)


def _ag_start(name, shards, lands, groups, direct, deps=()):
    nw, ng, nd = len(shards), len(groups), len(deps)

    def body(*refs):
        src, land = refs[:nw], refs[nw:2 * nw]
        sems = refs[2 * nw + nd:2 * nw + nd + 2 * ng]
        token = refs[-1]
        x, y, c = _place()
        for gi, grp in enumerate(groups):
            for wi, w in enumerate(grp):
                for j, chip in enumerate(_other_chips(x, y)):
                    for tc in _ag_targets(c, direct[gi]):
                        _ag_copy(src[w], land[w], sems[2 * gi], sems[2 * gi + 1], wi, j, chip, x, y, c, tc, direct[gi]).start()
        token[...] = jnp.zeros_like(token)

    sem_shapes = []
    for gi, grp in enumerate(groups):
        sem_shapes += [pltpu.SemaphoreType.DMA(((6 if direct[gi] else 3) * len(grp),))] * 2
    out_shape = sem_shapes + [pltpu.HBM(s.shape, s.dtype) for s in shards] + [pltpu.HBM(l.shape, l.dtype) for l in lands]
    out_shape += [jax.ShapeDtypeStruct((8, 128), F32)]
    res = _pcall(body, name=name, out_shape=out_shape, in_specs=[HBM_SPEC] * (2 * nw) + [ANY_SPEC] * nd,
                 out_specs=[SEM_SPEC] * (2 * ng) + [HBM_SPEC] * (2 * nw) + [pl.BlockSpec(memory_space=pltpu.VMEM)],
                 input_output_aliases={i: 2 * ng + i for i in range(2 * nw)},
                 compiler_params=pltpu.CompilerParams(has_side_effects=EFFECT))(
                     *[_hbm(s) for s in shards], *[_hbm(l) for l in lands], *deps)
    sems = [(res[2 * gi], res[2 * gi + 1]) for gi in range(ng)]
    return sems, list(res[2 * ng:2 * ng + nw]), list(res[2 * ng + nw:2 * ng + 2 * nw]), res[-1]


def _ag_wait(name, shards, lands, send, recv, both, after):
    n = len(shards)

    def body(*refs):
        src, land = refs[:n], refs[n:2 * n]
        send_sem, recv_sem = refs[2 * n], refs[2 * n + 1]
        x, y, c = _place()
        for wi in range(n):
            for j, chip in enumerate(_other_chips(x, y)):
                for tc in _ag_targets(c, both):
                    _ag_copy(src[wi], land[wi], send_sem, recv_sem, wi, j, chip, x, y, c, tc, both).wait_send()
                    _ag_copy(src[wi], land[wi], send_sem, recv_sem, wi, j, chip, chip[0], chip[1], tc, c, both).wait_recv()

    res = _pcall(body, name=name, out_shape=[pltpu.HBM(a.shape, a.dtype) for a in list(shards) + list(lands)],
                 in_specs=[HBM_SPEC] * (2 * n) + [SEM_SPEC, SEM_SPEC] + [ANY_SPEC] * len(after), out_specs=[HBM_SPEC] * (2 * n),
                 input_output_aliases={i: i for i in range(2 * n)},
                 compiler_params=pltpu.CompilerParams(has_side_effects=EFFECT))(*shards, *lands, send, recv, *after)
    return list(res[n:])


def _ag_forward(name, lands):
    n = len(lands)

    def body(*refs):
        out = refs[n:2 * n]
        send, recv = refs[2 * n], refs[2 * n + 1]
        x, y, c = _place()
        sib = (x, y, 1 - c)
        cps = []
        for wi in range(n):
            hr = out[wi].shape[1] // 2
            for j, (cx, cy) in enumerate(_other_chips(x, y)):
                got = out[wi].at[2 * cx + cy, pl.ds(pl.multiple_of(c * hr, 16), hr), :]
                cp = pltpu.make_async_remote_copy(src_ref=got, dst_ref=got, send_sem=send.at[3 * wi + j], recv_sem=recv.at[3 * wi + j],
                                                  device_id=sib, device_id_type=MESH)
                cp.start()
                cps.append(cp)
        for wi in range(n):
            hr = out[wi].shape[1] // 2
            for j, (cx, cy) in enumerate(_other_chips(x, y)):
                got = out[wi].at[2 * cx + cy, pl.ds(pl.multiple_of((1 - c) * hr, 16), hr), :]
                pltpu.make_async_remote_copy(src_ref=got, dst_ref=got, send_sem=send.at[3 * wi + j], recv_sem=recv.at[3 * wi + j],
                                             device_id=sib, device_id_type=MESH).wait_recv()
        for cp in cps:
            cp.wait_send()

    res = _pcall(body, name=name, out_shape=[jax.ShapeDtypeStruct(l.shape, l.dtype) for l in lands],
                 in_specs=[ANY_SPEC] * n, out_specs=[ANY_SPEC] * n, input_output_aliases={i: i for i in range(n)},
                 scratch_shapes=[pltpu.SemaphoreType.DMA((3 * n,)), pltpu.SemaphoreType.DMA((3 * n,))])(*lands)
    return list(res)


def _peers(x, y, c):
    offs = [(dx, dy, dc) for dx in (0, 1) for dy in (0, 1) for dc in (0, 1)][1:]
    return [(1 - x if dx else x, 1 - y if dy else y, 1 - c if dc else c) for dx, dy, dc in offs]


def _rs_copy(g_ref, land_ref, send, recv, wi, k, to, sender):
    hr = g_ref.shape[1] // 2
    return pltpu.make_async_remote_copy(
        src_ref=g_ref.at[2 * to[0] + to[1], pl.ds(pl.multiple_of(to[2] * hr, 16), hr), :], dst_ref=land_ref.at[sender],
        send_sem=send.at[7 * wi + k], recv_sem=recv.at[7 * wi + k], device_id=to, device_id_type=MESH)


def _rs_start(name, gs):
    n = len(gs)
    lands = [lax.empty((N_DEV, g.shape[1] // 2, g.shape[2]), BF16) for g in gs]

    def body(*refs):
        g, land = refs[:n], refs[n:2 * n]
        send, recv = refs[2 * n], refs[2 * n + 1]
        token = refs[-1]
        x, y, c = _place()
        me = 4 * x + 2 * y + c
        for wi in range(n):
            for k, to in enumerate(_peers(x, y, c)):
                _rs_copy(g[wi], land[wi], send, recv, wi, k, to, me).start()
        token[...] = jnp.zeros_like(token)

    out_shape = [pltpu.SemaphoreType.DMA((7 * n,))] * 2 + [pltpu.HBM(a.shape, a.dtype) for a in list(gs) + lands]
    out_shape += [jax.ShapeDtypeStruct((8, 128), F32)]
    res = _pcall(body, name=name, out_shape=out_shape, in_specs=[HBM_SPEC] * (2 * n),
                 out_specs=[SEM_SPEC] * 2 + [HBM_SPEC] * (2 * n) + [pl.BlockSpec(memory_space=pltpu.VMEM)],
                 input_output_aliases={i: 2 + i for i in range(2 * n)},
                 compiler_params=pltpu.CompilerParams(has_side_effects=EFFECT))(
                     *[_hbm(a) for a in gs], *[_hbm(a) for a in lands])
    return res[0], res[1], list(res[2:2 + n]), list(res[2 + n:2 + 2 * n]), res[-1]


def _rs_wait(name, gs, lands, send, recv, after):
    n = len(gs)

    def body(*refs):
        g, land = refs[:n], refs[n:2 * n]
        send_sem, recv_sem = refs[2 * n], refs[2 * n + 1]
        x, y, c = _place()
        me = 4 * x + 2 * y + c
        for wi in range(n):
            for k, to in enumerate(_peers(x, y, c)):
                _rs_copy(g[wi], land[wi], send_sem, recv_sem, wi, k, to, me).wait_send()
                _rs_copy(g[wi], land[wi], send_sem, recv_sem, wi, k, (x, y, c), 4 * to[0] + 2 * to[1] + to[2]).wait_recv()

    res = _pcall(body, name=name, out_shape=[pltpu.HBM(a.shape, a.dtype) for a in list(gs) + list(lands)],
                 in_specs=[HBM_SPEC] * (2 * n) + [SEM_SPEC, SEM_SPEC, ANY_SPEC], out_specs=[HBM_SPEC] * (2 * n),
                 input_output_aliases={i: i for i in range(2 * n)},
                 compiler_params=pltpu.CompilerParams(has_side_effects=EFFECT))(*gs, *lands, send, recv, after)
    return list(res[:n]), list(res[n:])


def _bc_copy(blk_ref, land_ref, send, recv, k, to, slot):
    return pltpu.make_async_remote_copy(src_ref=blk_ref, dst_ref=land_ref.at[slot], send_sem=send.at[k], recv_sem=recv.at[k],
                                        device_id=to, device_id_type=MESH)


def _bcast_start(name, blk):
    land = lax.empty((N_DEV,) + blk.shape, blk.dtype)

    def body(blk_ref, land_ref, send, recv, blk_thru, land_thru, token):
        x, y, c = _place()
        for k, to in enumerate(_peers(x, y, c)):
            _bc_copy(blk_ref, land_ref, send, recv, k, to, 4 * x + 2 * y + c).start()
        token[...] = jnp.zeros_like(token)

    return _pcall(body, name=name,
                  out_shape=[pltpu.SemaphoreType.DMA((7,)), pltpu.SemaphoreType.DMA((7,)), pltpu.HBM(blk.shape, blk.dtype),
                             pltpu.HBM(land.shape, land.dtype), jax.ShapeDtypeStruct((8, 128), F32)],
                  in_specs=[HBM_SPEC, HBM_SPEC], out_specs=[SEM_SPEC, SEM_SPEC, HBM_SPEC, HBM_SPEC, pl.BlockSpec(memory_space=pltpu.VMEM)],
                  input_output_aliases={0: 2, 1: 3}, compiler_params=pltpu.CompilerParams(has_side_effects=EFFECT))(_hbm(blk), _hbm(land))


def _bcast_wait(name, blk, land, send, recv, after):
    def body(blk_ref, land_ref, send_sem, recv_sem, after_ref, blk_thru, land_thru):
        x, y, c = _place()
        for k, to in enumerate(_peers(x, y, c)):
            _bc_copy(blk_ref, land_ref, send_sem, recv_sem, k, to, 4 * x + 2 * y + c).wait_send()
            _bc_copy(blk_ref, land_ref, send_sem, recv_sem, k, to, 4 * to[0] + 2 * to[1] + to[2]).wait_recv()

    return _pcall(body, name=name, out_shape=[pltpu.HBM(blk.shape, blk.dtype), pltpu.HBM(land.shape, land.dtype)],
                  in_specs=[HBM_SPEC, HBM_SPEC, SEM_SPEC, SEM_SPEC, ANY_SPEC], out_specs=[HBM_SPEC, HBM_SPEC],
                  input_output_aliases={0: 0, 1: 1}, compiler_params=pltpu.CompilerParams(has_side_effects=EFFECT))(
                      blk, land, send, recv, after)[1]


def _rs_sum(name, gs, lands):
    n = len(gs)

    def body(*refs):
        g_refs, land_refs, out_refs = refs[:n], refs[n:2 * n], refs[2 * n:3 * n]
        recvs = refs[3 * n:4 * n]
        local_sems, sib_send, sib_recv = refs[4 * n:]
        x, y, c = _place()
        me = 4 * x + 2 * y + c
        cps = []
        for wi in range(n):
            hr = g_refs[wi].shape[1] // 2
            own = g_refs[wi].at[2 * x + y, pl.ds(pl.multiple_of(c * hr, 16), hr), :]
            cps.append(pltpu.make_async_copy(own, recvs[wi].at[me], local_sems.at[8 * wi + 7]))
            for k, (tx, ty, tc) in enumerate(_peers(x, y, c)):
                slot = 4 * tx + 2 * ty + tc
                cps.append(pltpu.make_async_copy(land_refs[wi].at[slot], recvs[wi].at[slot], local_sems.at[8 * wi + k]))
        for cp in cps:
            cp.start()
        sibs = []
        for wi in range(n):
            hr = g_refs[wi].shape[1] // 2
            ch = _pick(hr, 64, 16)
            for cp in cps[8 * wi:8 * wi + 8]:
                cp.wait()
            base = pl.multiple_of(c * hr, 16)
            for r0 in range(0, hr, ch):
                acc = recvs[wi][0, r0:r0 + ch, :].astype(F32)
                for k in range(1, N_DEV):
                    acc = acc + recvs[wi][k, r0:r0 + ch, :].astype(F32)
                out_refs[wi][pl.ds(base + r0, ch), :] = acc
            half = out_refs[wi].at[pl.ds(base, hr), :]
            sib = pltpu.make_async_remote_copy(src_ref=half, dst_ref=half, send_sem=sib_send.at[wi], recv_sem=sib_recv.at[wi],
                                               device_id=(x, y, 1 - c), device_id_type=MESH)
            sib.start()
            sibs.append(sib)
        for wi in range(n):
            hr = g_refs[wi].shape[1] // 2
            other = out_refs[wi].at[pl.ds(pl.multiple_of((1 - c) * hr, 16), hr), :]
            pltpu.make_async_remote_copy(src_ref=other, dst_ref=other, send_sem=sib_send.at[wi], recv_sem=sib_recv.at[wi],
                                         device_id=(x, y, 1 - c), device_id_type=MESH).wait_recv()
            sibs[wi].wait_send()

    res = _pcall(
        body, name=name, out_shape=[jax.ShapeDtypeStruct(g.shape[1:], F32) for g in gs],
        in_specs=[ANY_SPEC] * (2 * n), out_specs=[pl.BlockSpec(memory_space=pltpu.VMEM)] * n,
        scratch_shapes=[pltpu.VMEM((N_DEV, g.shape[1] // 2, g.shape[2]), BF16) for g in gs]
        + [pltpu.SemaphoreType.DMA((8 * n,)), pltpu.SemaphoreType.DMA((n,)), pltpu.SemaphoreType.DMA((n,))],
        compiler_params=pltpu.CompilerParams(vmem_limit_bytes=VMEM_LIMIT))(*gs, *lands)
    return list(res)


def _gate_norm_epilogue(factor, x_in, gt, nxt, t, d):
    blk = (ROWS, d)
    full = lambda i, j: (i, j)
    rowv = lambda i, j: (0, j)

    def epi(acc, res, scale, *norm):
        x_new = res + (factor * scale) * acc
        if not norm:
            return x_new, acc
        gv, scv, shv = norm
        r = lax.rsqrt(jnp.mean(x_new * x_new, axis=-1, keepdims=True) + EPS)
        return x_new, acc, x_new * r * gv * (1.0 + scv) + shv

    ins = [(x_in, blk, full), (gt, (1, d), rowv)] + [(v, (1, d), rowv) for v in (nxt or ())]
    outs = [((t, d), F32, blk, full), ((t, d), BF16, blk, full)] + ([((t, d), BF16, blk, full)] if nxt else [])
    return epi, ins, outs


def _mixer_tail(name, ge, w_glu, b_glu, w_up, y_pool, uin, w_out, x_in, gt, nxt, d, pw, tm=ROWS):
    t, sw = ge.shape
    tm = min(tm, t)
    ns = w_glu.shape[2]
    cb = (2 * pw) // d

    def body(ge_ref, wg_ref, b_ref, wu_ref, yp_ref, glp_ref, gls_ref, wo_ref, x_ref, gt_ref, gn_ref, sc_ref, sh_ref,
             gv_ref, sg_ref, ys_ref, mg_ref, y_ref, h_ref, xn_ref):
        gev = ge_ref[...]
        gv = jnp.concatenate([jnp.dot(gev, wg_ref[s], preferred_element_type=F32) for s in range(N_CHIPS)], axis=1)
        gv_ref[...] = gv.astype(BF16)
        g = gv + b_ref[...]
        sg = (g[:, :sw] * _sigmoid(g[:, sw:])).astype(BF16)
        sg_ref[...] = sg
        ys = jnp.concatenate([jnp.dot(sg, wu_ref[s], preferred_element_type=F32) for s in range(N_CHIPS)], axis=1)
        ys_ref[...] = ys.astype(BF16)
        mg = (_sigmoid(glp_ref[...].astype(F32)) * yp_ref[...].astype(F32) + _sigmoid(gls_ref[...].astype(F32)) * ys).astype(BF16)
        mg_ref[...] = mg
        y = jnp.dot(mg, wo_ref[...], preferred_element_type=F32)
        y_ref[...] = y.astype(BF16)
        xn = x_ref[...] + gt_ref[...] * y
        xn_ref[...] = xn
        r = lax.rsqrt(jnp.mean(xn * xn, axis=-1, keepdims=True) + EPS)
        h_ref[...] = (xn * r * gn_ref[...] * (1.0 + sc_ref[...]) + sh_ref[...]).astype(BF16)

    full3 = lambda i: (0, 0, 0)
    full2 = lambda i: (0, 0)
    rows = lambda w: pl.BlockSpec((tm, w), lambda i: (i, 0))
    rowv = pl.BlockSpec((1, d), full2)
    return _pcall(
        body, name=name, grid=(t // tm,),
        in_specs=[rows(sw), pl.BlockSpec(w_glu.shape, full3), pl.BlockSpec(b_glu.shape, full2),
                  pl.BlockSpec(w_up.shape, full3), rows(d), pl.BlockSpec((tm, d), lambda i: (i, cb)),
                  pl.BlockSpec((tm, d), lambda i: (i, cb + 1)), pl.BlockSpec(w_out.shape, full2), rows(d), rowv, rowv, rowv, rowv],
        out_specs=[rows(N_CHIPS * ns), rows(sw), rows(d), rows(d), rows(d), rows(d), rows(d)],
        out_shape=[jax.ShapeDtypeStruct((t, N_CHIPS * ns), BF16), jax.ShapeDtypeStruct((t, sw), BF16)]
        + [jax.ShapeDtypeStruct((t, d), BF16)] * 4 + [jax.ShapeDtypeStruct((t, d), F32)],
        compiler_params=_params("parallel"))(ge, w_glu, b_glu, w_up, y_pool, uin, uin, w_out, x_in, gt, *nxt)


def _glu_up_bwd(name, dy, w_up, gv, b_glu, w_glu, sg, ge, tm=ROWS, deps=()):
    t, sw = ge.shape
    tm = min(tm, t)
    nb = t // tm
    ns = w_up.shape[2]
    nt = (((1,), (1,)), ((), ()))
    tn_ = (((0,), (0,)), ((), ()))

    def body(dy_ref, wu_ref, gv_ref, b_ref, wg_ref, sg_ref, ge_ref, *rest):
        dge_ref, db_ref, dwu_ref, dwg_ref, dwu_acc, dwg_acc = rest[len(deps):]
        i = pl.program_id(0)

        @pl.when(i == 0)
        def _():
            db_ref[...] = jnp.zeros_like(db_ref)
            dwu_acc[...] = jnp.zeros_like(dwu_acc)
            dwg_acc[...] = jnp.zeros_like(dwg_acc)

        dyv = dy_ref[...]
        dsg = None
        for s in range(N_CHIPS):
            q = lax.dot_general(dyv[:, s * ns:(s + 1) * ns], wu_ref[s], nt, preferred_element_type=F32)
            dsg = q if dsg is None else dsg + q
        g = gv_ref[...].astype(F32) + b_ref[...]
        val, sgm = g[:, :sw], _sigmoid(g[:, sw:])
        dval, dgate = dsg * sgm, dsg * val * sgm * (1.0 - sgm)
        db_ref[...] += jnp.concatenate([_colsum(dval), _colsum(dgate)], axis=1)
        dgv = jnp.concatenate([dval, dgate], axis=1).astype(BF16)
        dge = None
        for s in range(N_CHIPS):
            q = lax.dot_general(dgv[:, s * ns:(s + 1) * ns], wg_ref[s], nt, preferred_element_type=F32)
            dge = q if dge is None else dge + q
        dge_ref[...] = dge.astype(BF16)
        dwu_acc[...] += lax.dot_general(sg_ref[...], dyv, tn_, preferred_element_type=F32)
        dwg_acc[...] += lax.dot_general(ge_ref[...], dgv, tn_, preferred_element_type=F32)

        @pl.when(i == nb - 1)
        def _():
            for s in range(N_CHIPS):
                dwu_ref[s] = dwu_acc[:, s * ns:(s + 1) * ns].astype(BF16)
                dwg_ref[s] = dwg_acc[:, s * ns:(s + 1) * ns].astype(BF16)

    full3 = lambda i: (0, 0, 0)
    rows = lambda w: pl.BlockSpec((tm, w), lambda i: (i, 0))
    wide = N_CHIPS * ns
    return _pcall(
        body, name=name, grid=(nb,),
        in_specs=[rows(wide), pl.BlockSpec(w_up.shape, full3), rows(wide), pl.BlockSpec(b_glu.shape, lambda i: (0, 0)),
                  pl.BlockSpec(w_glu.shape, full3), rows(sw), rows(sw)] + [ANY_SPEC] * len(deps),
        out_specs=[rows(sw), pl.BlockSpec((1, wide), lambda i: (0, 0)), pl.BlockSpec(w_up.shape, full3),
                   pl.BlockSpec(w_glu.shape, full3)],
        out_shape=[jax.ShapeDtypeStruct((t, sw), BF16), jax.ShapeDtypeStruct((1, wide), F32),
                   jax.ShapeDtypeStruct(w_up.shape, BF16), jax.ShapeDtypeStruct(w_glu.shape, BF16)],
        scratch_shapes=[pltpu.VMEM((sw, wide), F32), pltpu.VMEM((sw, wide), F32)],
        compiler_params=_params("arbitrary"))(dy, w_up, gv, b_glu, w_glu, sg, ge, *deps)


_FULL = lambda i, j: (i, j)
_ROWV = lambda i, j: (0, j)


def _gates_bwd_epilogue(y_pool, y_ssm, uin, t, d, pw):
    cb = (2 * pw) // d

    def epi(dm, yp, ys, glp, gls):
        sp, ss = _sigmoid(glp.astype(F32)), _sigmoid(gls.astype(F32))
        dgl = jnp.concatenate([dm * yp.astype(F32) * sp * (1.0 - sp), dm * ys.astype(F32) * ss * (1.0 - ss)], axis=1)
        return dm * sp, dm * ss, ("at", 2 * pw, dgl)

    ins = [(y_pool, (ROWS, d), _FULL), (y_ssm, (ROWS, d), _FULL),
           (uin, (ROWS, d), lambda i, j: (i, cb)), (uin, (ROWS, d), lambda i, j: (i, cb + 1))]
    wide = 2 * pw + 2 * d
    return epi, ins, [((t, d), BF16, (ROWS, d), _FULL)] * 2 + [((t, wide), BF16, (ROWS, wide), _FULL)]


def _loss_epilogue(x_in, gt, tgt, g_final, t, d):
    blk = (ROWS, d)
    full = lambda i, j: (i, j)
    rowv = lambda i, j: (0, j)

    def epi(acc, res, scale, tv, gv):
        xv = res + (0.5 * scale) * acc
        r = lax.rsqrt(jnp.mean(xv * xv, axis=-1, keepdims=True) + EPS)
        xr = xv * r
        e = xr * gv - tv
        loss_row = 0.5 * jnp.mean(e * e, axis=-1, keepdims=True)
        dout = e * (1.0 / d)
        gd = gv * dout
        dx = r * (gd - xr * jnp.mean(gd * xr, axis=-1, keepdims=True))
        fdx = 0.5 * dx
        return [dx, scale * fdx], [_colsum(dout * xr), _colsum(loss_row * jnp.ones((1, 128), F32)), _colsum(fdx * acc)]

    ins = [(x_in, blk, full), (gt, (1, d), rowv), (tgt, blk, full), (g_final, (1, d), rowv)]
    return epi, ins, [((t, d), F32, blk, full), ((t, d), BF16, blk, full)], [d, 128, d]


def _norm_bwd_epilogue(x, dres, g, sc, up, t, d):
    blk = (ROWS, d)
    full = lambda i, j: (i, j)
    rowv = lambda i, j: (0, j)

    def epi(dhv, xv, drv, gv, scv, *rest):
        r = lax.rsqrt(jnp.mean(xv * xv, axis=-1, keepdims=True) + EPS)
        xr = xv * r
        dn = dhv * (1.0 + scv)
        gd = gv * dn
        dx = drv + r * (gd - xr * jnp.mean(gd * xr, axis=-1, keepdims=True))
        outs, reds = [dx], [_colsum(dhv), _colsum(dhv * xr * gv), _colsum(dn * xr)]
        if up:
            yv, gtv = rest
            fdx = up[2] * dx
            outs.append(gtv * fdx)
            reds.append(_colsum(fdx * yv.astype(F32)))
        return outs, reds

    ins = [(x, blk, full), (dres, blk, full), (g, (1, d), rowv), (sc, (1, d), rowv)]
    ins += [(up[0], blk, full), (up[1], (1, d), rowv)] if up else []
    outs = [((t, d), F32, blk, full)] + ([((t, d), BF16, blk, full)] if up else [])
    return epi, ins, outs, [d] * (4 if up else 3)


def _ffn_in_act(name, h, w_in, tm=ROWS):
    t, d = h.shape
    ns = w_in.shape[2]
    tm = _pick(t, tm, 8)
    w4 = w_in.reshape(2, 2, d, ns)

    def body(h_ref, w_ref, ab_ref, act_ref):
        hv = h_ref[...]
        a = jnp.dot(hv, w_ref[0], preferred_element_type=F32)
        b = jnp.dot(hv, w_ref[1], preferred_element_type=F32)
        ab_ref[0] = a.astype(BF16)
        ab_ref[1] = b.astype(BF16)
        act_ref[...] = (a * _sigmoid(a) * b).astype(BF16)

    return _pcall(body, name=name, grid=(2, t // tm),
                  in_specs=[pl.BlockSpec((tm, d), lambda c, i: (i, 0)), pl.BlockSpec((2, None, d, ns), lambda c, i: (0, c, 0, 0))],
                  out_specs=[pl.BlockSpec((2, tm, ns), lambda c, i: (0, i, c)), pl.BlockSpec((tm, ns), lambda c, i: (i, c))],
                  out_shape=[jax.ShapeDtypeStruct((2, t, 2 * ns), BF16), jax.ShapeDtypeStruct((t, 2 * ns), BF16)],
                  compiler_params=_params("parallel", "parallel"))(h, w4)


def _dswiglu_epilogue(ab, t, f, tn):
    blk = (2, ROWS, tn)
    idx = lambda i, j: (0, i, j)

    def epi(dact, abv):
        a, b = abv[0].astype(F32), abv[1].astype(F32)
        s = _sigmoid(a)
        return (jnp.stack([dact * b * (s * (1.0 + a * (1.0 - s))), dact * (a * s)]),)

    return epi, [(ab, blk, idx)], [((2, t, f), BF16, blk, idx)]


def _ffn_fwd(tag, x, h, gt, get_w_in, get_w_out, nxt=None, loss=None):
    t, d = x.shape
    ab, act = _ffn_in_act(tag + "_in", h, get_w_in(h))
    if loss:
        epi, epi_ins, epi_outs, epi_reds = _loss_epilogue(x, gt, *loss, t, d)
        res = _mm(tag + "_out", act, get_w_out(act), tm=ROWS, tn=d, epi=epi, epi_ins=epi_ins, epi_outs=epi_outs,
                  epi_reds=epi_reds, b_resident=True)
        return res, (x, h, ab, act, None)
    epi, epi_ins, epi_outs = _gate_norm_epilogue(0.5, x, gt, nxt, t, d)
    res = _mm(tag + "_out", act, get_w_out(act), tm=ROWS, tn=d, epi=epi, epi_ins=epi_ins, epi_outs=epi_outs, b_resident=True)
    return res[0], res[2], (x, h, ab, act, res[1])


def _ffn_bwd(tag, dx_new, dy, saved, g, sc, w_in, w_out, start_rs, up, deps=()):
    x, h, ab, act, _ = saved
    d = x.shape[1]
    f = act.shape[1]
    dw_out = _mm(tag + "_dwout", act, dy, ta=True, tm=1408, tn=d, tk=2048, deps=deps)
    tok = start_rs("out", dw_out.reshape(N_CHIPS, f // N_CHIPS, d))
    epi, epi_ins, epi_outs = _dswiglu_epilogue(ab, x.shape[0], f, w_in.shape[2])
    dab = _mm(tag + "_dact", dy, w_out, tb=True, tm=ROWS, tn=w_in.shape[2], epi=epi, epi_ins=epi_ins, epi_outs=epi_outs,
              deps=(tok,), j_outer=True)
    dw_in = _mm(tag + "_dwin", h, dab, ta=True, out_stacked=True, b_halves=True, tm=ROWS, tn=1408, tk=4096, j_outer=True)
    tok = start_rs("in", dw_in)
    epi, epi_ins, epi_outs, epi_reds = _norm_bwd_epilogue(x, dx_new, g, sc, up, x.shape[0], d)
    res = _mm(tag + "_dh", dab, w_in, tb=True, b_stacked=True, a_halves=True, tn=d, tk=5632, deps=(tok,), epi=epi,
              epi_ins=epi_ins, epi_outs=epi_outs, epi_reds=epi_reds, b_resident=True)
    if up:
        return res
    return res[0], None, res[1], res[2], res[3], None


def _row(v):
    return v.reshape(1, -1)


def _pack(parts):
    cols = []
    for p in parts:
        flat = p.reshape(-1).astype(F32)
        padn = (-flat.shape[0]) % 128
        cols.append(jnp.pad(flat, (0, padn)) if padn else flat)
    flat = jnp.concatenate(cols)
    padn = (-flat.shape[0]) % 1024
    if padn:
        flat = jnp.pad(flat, (0, padn))
    return flat.reshape(-1, 128)


def _unpack(packed, shapes):
    flat = packed.reshape(-1)
    out, off = [], 0
    for s in shapes:
        n = math.prod(s)
        out.append(flat[off:off + n].reshape(s))
        off += n + ((-n) % 128)
    return out


SMALL = ['b_ada', 'g_ffn1', 'g_mix', 'pool_w', 'pool_b', 'pool_scale', 'ssm_lam_re_log', 'ssm_lam_im', 'ssm_log_dt',
         'ssm_b_re', 'ssm_b_im', 'ssm_c_re', 'ssm_c_im', 'ssm_d', 'b_glu', 'g_ffn2', 'g_final']
BIG = ['w_ffn1_in', 'w_ffn1_out', 'w_in', 'w_pool_up', 'w_glu', 'w_ssm_up', 'w_out', 'w_ffn2_in', 'w_ffn2_out']
AG_GROUPS = [[0], [1], [2, 3, 4, 5, 6], [7, 8]]
AG_DIRECT = [False, True, False, True]
SMALL_LATE = ['b_ada', 'g_ffn1']
RS_SUM_GROUPS = [[0, 1, 2, 3], [4, 5]]
WEIGHTS = ['w_ada', 'b_ada', 'g_ffn1', 'w_ffn1_in', 'w_ffn1_out', 'g_mix', 'w_in', 'pool_w', 'pool_b', 'pool_scale', 'w_pool_up',
           'ssm_lam_re_log', 'ssm_lam_im', 'ssm_log_dt', 'ssm_b_re', 'ssm_b_im', 'ssm_c_re', 'ssm_c_im', 'ssm_d', 'w_glu', 'b_glu',
           'w_ssm_up', 'w_out', 'g_ffn2', 'w_ffn2_in', 'w_ffn2_out', 'g_final']


def kernel(x, c, w_ada, b_ada, g_ffn1, w_ffn1_in, w_ffn1_out, g_mix, w_in, pool_w, pool_b, pool_scale, w_pool_up, ssm_lam_re_log, ssm_lam_im, ssm_log_dt, ssm_b_re, ssm_b_im, ssm_c_re, ssm_c_im, ssm_d, w_glu, b_glu, w_ssm_up, w_out, g_ffn2, w_ffn2_in, w_ffn2_out, g_final, loss_target, m_w_ada, m_b_ada, m_g_ffn1, m_w_ffn1_in, m_w_ffn1_out, m_g_mix, m_w_in, m_pool_w, m_pool_b, m_pool_scale, m_w_pool_up, m_ssm_lam_re_log, m_ssm_lam_im, m_ssm_log_dt, m_ssm_b_re, m_ssm_b_im, m_ssm_c_re, m_ssm_c_im, m_ssm_d, m_w_glu, m_b_glu, m_w_ssm_up, m_w_out, m_g_ffn2, m_w_ffn2_in, m_w_ffn2_out, m_g_final, v_w_ada, v_b_ada, v_g_ffn1, v_w_ffn1_in, v_w_ffn1_out, v_g_mix, v_w_in, v_pool_w, v_pool_b, v_pool_scale, v_w_pool_up, v_ssm_lam_re_log, v_ssm_lam_im, v_ssm_log_dt, v_ssm_b_re, v_ssm_b_im, v_ssm_c_re, v_ssm_c_im, v_ssm_d, v_w_glu, v_b_glu, v_w_ssm_up, v_w_out, v_g_ffn2, v_w_ffn2_in, v_w_ffn2_out, v_g_final):
    args = dict(locals())
    wt = {n: args[n] for n in WEIGHTS}
    mom = {n: args["m_" + n] for n in WEIGHTS}
    var = {n: args["v_" + n] for n in WEIGHTS}

    t, d = x.shape[1], x.shape[2]
    pw = pool_b.shape[1]
    sw = ssm_d.shape[1]
    ngrp = sw // SSM_GROUP
    gn = ngrp * SSM_STATE
    xi, yi, ci = _place()
    b_me = 4 * xi + 2 * yi + ci
    s_me = 2 * xi + yi
    x2d = x[0]
    tgt = loss_target[0]

    c_all = _allgather_small("ag_c", c.reshape(8, d // 8)).reshape(N_DEV, d)
    ncol = w_ada.shape[2]
    b_sh = lax.dynamic_slice(b_ada, (0, s_me * ncol), (1, ncol))
    mod_sh = _ada_fwd("ada_fwd", c_all, w_ada[0], b_sh)
    md_send, md_recv, mod_sh, md_land, tok = _bcast_start("mod_start", mod_sh)

    shards = [wt[n][0].astype(BF16) for n in BIG]
    lands = _place_shards(shards, s_me)
    n0 = len(AG_GROUPS[0])
    sems_a, shards_a, lands_a, tok = _ag_start("ag_start_first", shards[:n0], lands[:n0], AG_GROUPS[:1], AG_DIRECT[:1], deps=(tok,))
    rest = [[w - n0 for w in grp] for grp in AG_GROUPS[1:]]
    sems_b, shards_b, lands_b, tok = _ag_start("ag_start_rest", shards[n0:], lands[n0:], rest, AG_DIRECT[1:], deps=(tok,))
    ag_sems, shards_t, lands_t = sems_a + sems_b, shards_a + shards_b, lands_a + lands_b
    md_land = _bcast_wait("mod_wait", mod_sh, md_land, md_send, md_recv, tok)
    mod_all = lax.dynamic_update_slice(md_land, mod_sh[None], (b_me, 0, 0))
    full = {}

    def weights(gi, *after):
        grp = AG_GROUPS[gi]
        if BIG[grp[0]] not in full:
            ls = _ag_wait("ag_wait%d" % gi, [shards_t[w] for w in grp], [lands_t[w] for w in grp], *ag_sems[gi],
                          AG_DIRECT[gi], after)
            for w, l in zip(grp, ls if AG_DIRECT[gi] else _ag_forward("ag_fwd%d" % gi, ls)):
                full[BIG[w]] = l
        return full

    mod_me = jnp.concatenate([lax.dynamic_slice(mod_all, (2 * s, b_me, 0), (1, 1, ncol))[0] for s in range(N_CHIPS)], axis=1)
    mod = mod_me.reshape(9, d)
    sh1, sc1, gt1, sh2, sc2, gt2, sh3, sc3, gt3 = [mod[k:k + 1] for k in range(9)]

    f = w_ffn1_out.shape[1] * N_CHIPS

    col = lambda a: a.reshape(gn, 1)
    lrl_c, li_c = col(ssm_lam_re_log), col(ssm_lam_im)
    ldt_c = col(jnp.broadcast_to(ssm_log_dt.reshape(ngrp, 1), (ngrp, SSM_STATE)))
    b_re2, b_im2 = ssm_b_re.reshape(gn, SSM_GROUP), ssm_b_im.reshape(gn, SSM_GROUP)
    lrdt_c, ang_c, bb_re, bb_im = _ssm_prep("ssm_prep", lrl_c, li_c, ldt_c, b_re2, b_im2)
    lrdt, ang = lrdt_c.reshape(1, gn), ang_c.reshape(1, gn)
    tr = lambda a: jnp.swapaxes(a, 1, 2)
    bbd = [_blockdiag_b(v, sw).astype(BF16) for v in (bb_re, bb_im)]
    ccd = [_blockdiag_c(v[0], sw).astype(BF16) for v in (ssm_c_re, ssm_c_im)]
    bbd_t, ccd_t = [tr(v) for v in bbd], [tr(v) for v in ccd]
    early = [n for n in SMALL if n not in SMALL_LATE]
    packs = {}
    for tag, names, extra in (("early", early, []), ("late", SMALL_LATE, [jnp.zeros((128,), F32)])):
        packs[tag] = [_pack([src[n] for n in names] + extra) for src in (wt, mom, var)]
    shadow = [lrdt, ang, ldt_c, *bbd, *ccd, *bbd_t, *ccd_t, *packs["early"], *packs["late"]]

    h1 = _norm_fwd("ffn1_norm", x2d, g_ffn1, sc1, sh1, deps=(tok,))
    x1, h2, sav1 = _ffn_fwd("ffn1", x2d, h1, gt1, lambda after: weights(0, after, *shadow)['w_ffn1_in'],
                            lambda after: weights(1, after)['w_ffn1_out'].reshape(f, d), (g_mix, sc2, sh2))
    weights(2, h2)
    wo = full['w_out'].reshape(d, d)
    uin = _mm("mix_in", h2, full['w_in'], b_stacked=True, tm=2048, tn=768, j_outer=True)
    p_pool, z_pool, y_pool = _pool_fwd("pool_fwd", uin, pool_w[0], pool_b, pool_scale, full['w_pool_up'], pw, tm=1024)
    y_s, ge, s_re, s_im = _ssm_fwd("ssm_fwd", uin, lrdt, ang, *bbd, *ccd, ssm_d, sw)
    gv, sg, y_ssm, merged, y2, h3, x2 = _mixer_tail("mixer_tail", ge, full['w_glu'], b_glu, full['w_ssm_up'], y_pool, uin, wo,
                                                    x1, gt2, (g_ffn2, sc3, sh3), d, pw)

    (dx3, dy3, dg_final, loss_v, dgt3), sav3 = _ffn_fwd(
        "ffn2", x2, h3, gt3, lambda after: weights(3, after)['w_ffn2_in'],
        lambda after: weights(3, after)['w_ffn2_out'].reshape(f, d), loss=(tgt, _row(g_final)))

    gs = {}
    rs_open = []

    def start_rs(names, gbs):
        send, recv, g_thru, land_thru, token = _rs_start("rs_start_" + names[0], gbs)
        rs_open.append((names, send, recv, g_thru, land_thru))
        return token

    dx2, dy2, dsh3, dsc3, gs['g_ffn2'], dgt2 = _ffn_bwd(
        "ffn2b", dx3, dy3, sav3, g_ffn2, sc3, full['w_ffn2_in'], full['w_ffn2_out'].reshape(f, d),
        lambda key, g: start_rs(['w_ffn2_' + key], [g]), (y2, gt2, 1.0))

    epi, epi_ins, epi_outs = _gates_bwd_epilogue(y_pool, y_ssm, uin, t, d, pw)
    dy_pool, dy_ssm, duin, g_wo = _mm("mix_dmerged", dy2, wo, tb=True, tm=ROWS, tn=d, epi=epi, epi_ins=epi_ins,
                                      epi_outs=epi_outs, b_resident=True, wgrad=merged, wgrad_rows=True)
    g_wo = g_wo.reshape(N_CHIPS, d // N_CHIPS, d)

    duin, gs['pool_w'], gs['pool_b'], gs['pool_scale'], g_wpu = _pool_bwd(
        "pool_bwd", dy_pool, z_pool, p_pool, pool_w[0], pool_b, pool_scale, full['w_pool_up'], duin, tm=1024)

    tok = start_rs(['w_out', 'w_pool_up'], [g_wo, g_wpu])
    dge, gs['b_glu'], g_wsu, g_wglu = _glu_up_bwd("glu_up_bwd", dy_ssm, full['w_ssm_up'], gv, b_glu, full['w_glu'], sg, ge,
                                                  deps=(tok,))
    (duin, g_abre, g_abim, g_bbd_re, g_bbd_im, g_ccd_re, g_ccd_im, gs['ssm_d']) = _ssm_bwd(
        "ssm_bwd", dge, y_s, uin, s_re, s_im, lrdt, ang, *bbd_t, *ccd_t, ssm_d, sw, duin)
    gs['ssm_c_re'], gs['ssm_c_im'] = _diag_of_c(g_ccd_re, sw), _diag_of_c(g_ccd_im, sw)
    d_lrl, d_li, d_ldt, d_bre, d_bim = _ssm_param_bwd(
        "ssm_param_bwd", lrl_c, li_c, ldt_c, b_re2, b_im2, g_abre.reshape(gn, 1), g_abim.reshape(gn, 1),
        _diag_of_b(g_bbd_re, sw), _diag_of_b(g_bbd_im, sw))
    gs['ssm_lam_re_log'], gs['ssm_lam_im'] = d_lrl, d_li
    gs['ssm_log_dt'] = jnp.sum(d_ldt.reshape(ngrp, SSM_STATE), axis=1)
    gs['ssm_b_re'], gs['ssm_b_im'] = d_bre, d_bim

    g_win =_mm("mix_dwin", h2, duin, ta=True, out_stacked=True, tm=d, tn=768, tk=4096)
    tok = start_rs(['w_ssm_up', 'w_glu', 'w_in'], [g_wsu, g_wglu, g_win])
    epi, epi_ins, epi_outs, epi_reds = _norm_bwd_epilogue(x1, dx2, g_mix, sc2, (sav1[4], gt1, 0.5), t, d)
    dx1, dy1, dsh2, dsc2, gs['g_mix'], dgt1 = _mm(
        "mix_dh", duin, full['w_in'], tb=True, b_stacked=True, tn=d, tk=3072, deps=(tok,), epi=epi, epi_ins=epi_ins,
        epi_outs=epi_outs, epi_reds=epi_reds, b_resident=True)

    gs['g_final'] = dg_final
    sg_blk = _pack([gs[n] for n in early])
    sg_send, sg_recv, sg_blk, sg_land, tok = _bcast_start("sg_start", sg_blk)

    dx0, _, dsh1, dsc1, gs['g_ffn1'], _ = _ffn_bwd(
        "ffn1b", dx1, dy1, sav1, g_ffn1, sc1, full['w_ffn1_in'], full['w_ffn1_out'].reshape(f, d),
        lambda key, g: start_rs(['w_ffn1_' + key], [g]), None, deps=(tok,))

    gs['b_ada'] = jnp.concatenate([dsh1, dsc1, dgt1, dsh2, dsc2, dgt2, dsh3, dsc3, dgt3], axis=1)
    lt_blk = _pack([gs[n] for n in SMALL_LATE] + [loss_v])
    lt_send, lt_recv, lt_blk, lt_land, tok = _bcast_start("late_start", lt_blk)

    grads, delta, new_m, new_v = {}, {}, {}, {}

    def small_update(tag, names, blk, land):
        g8 = lax.dynamic_update_slice(land, blk[None], (b_me, 0, 0)).reshape(-1, 128)
        w_pack, m_pack, v_pack = packs[tag]
        shapes = [wt[n].shape for n in names]
        per_param, packed = _adamw_small("adamw_small_" + tag, w_pack, g8, m_pack, v_pack, shapes)
        for k, dst in enumerate((grads, delta, new_m, new_v)):
            rest = _unpack(packed[k], shapes)
            for i, n in enumerate(names):
                dst[n] = per_param[i][k] if per_param[i] is not None else rest[i]
        return g8

    after = tok
    for group in RS_SUM_GROUPS:
        names, g_done, land_done = [], [], []
        for k in group:
            nk, send, recv, g_thru, land_thru = rs_open[k]
            gd, ld = _rs_wait("rs_wait_" + nk[0], g_thru, land_thru, send, recv, after)
            names, g_done, land_done = names + nk, g_done + gd, land_done + ld
        g_sums = _rs_sum("rs_sum_" + names[0], g_done, land_done)
        updated = _adamw_group("adamw_" + names[0], [wt[n][0] for n in names], g_sums, [mom[n][0] for n in names],
                               [var[n][0] for n in names])
        for n, (g_out, dl, mn, vn) in zip(names, updated):
            grads[n], delta[n], new_m[n], new_v[n] = g_out[None], dl[None], mn[None], vn[None]
            after = dl
        if group is RS_SUM_GROUPS[-2]:
            small_update("early", early, sg_blk, _bcast_wait("sg_wait", sg_blk, sg_land, sg_send, sg_recv, after))

    lt_land = _bcast_wait("late_wait", lt_blk, lt_land, lt_send, lt_recv, after)
    late8 = small_update("late", SMALL_LATE, lt_blk, lt_land).reshape(N_DEV, -1)
    loss = jnp.sum(late8[:, 10 * d])
    dmod_sh = lax.dynamic_slice(late8, (0, s_me * ncol), (N_DEV, ncol))
    g_w_ada, dl, mn, vn = _ada_bwd_adamw("ada_bwd_adamw", c_all, dmod_sh, w_ada[0], m_w_ada[0], v_w_ada[0])
    grads['w_ada'], delta['w_ada'], new_m['w_ada'], new_v['w_ada'] = g_w_ada[None], dl[None], mn[None], vn[None]

    return (loss, dx0[None], *[grads[n] for n in WEIGHTS], *[delta[n] for n in WEIGHTS],
            *[new_m[n] for n in WEIGHTS], *[new_v[n] for n in WEIGHTS])
```

```python
import functools
import math

import jax
import jax.numpy as jnp
from jax import lax
from jax.experimental import pallas as pl
from jax.experimental.pallas import tpu as pltpu

F32 = jnp.float32
BF16 = jnp.bfloat16
MESH = pl.DeviceIdType.MESH

EPS = 1e-6
POOL_WINDOWS = (2, 4, 8, 16)
POOL_HALO = 16
SSM_GROUP = 16
SSM_STATE = 64
SSM_BLOCKS = 4
N_DEV = 8
N_CHIPS = 4
ADAM_LR = 0.001
ADAM_B1 = 0.9
ADAM_B2 = 0.999
ADAM_EPS = 1e-08
ADAM_WD = 0.01
ADAM_STEP = 10
VMEM_LIMIT = 56 * 1024 * 1024
ROWS = 512


ANY_SPEC = pl.BlockSpec(memory_space=pl.ANY)
HBM_SPEC = pl.BlockSpec(memory_space=pltpu.HBM)
SEM_SPEC = pl.BlockSpec(memory_space=pltpu.SEMAPHORE)
EFFECT = pltpu.SideEffectType.DATAFLOW_SIDE_EFFECTING


def _hbm(a):
    return pltpu.with_memory_space_constraint(a, pltpu.HBM)


def _pcall(body, **kw):
    return pl.pallas_call(body, **kw)


def _params(*sem):
    return pltpu.CompilerParams(dimension_semantics=sem, vmem_limit_bytes=VMEM_LIMIT)


def _pick(n, cap, mult=128):
    if n <= cap:
        return n
    best = None
    for d in range(mult, cap + 1, mult):
        if n % d == 0:
            best = d
    assert best is not None, (n, cap, mult)
    return best


def _sigmoid(v):
    return 1.0 / (1.0 + jnp.exp(-v))


def _rowwise(name, fn, ins, params, outs, reds, tm, deps=()):
    t = ins[0][0].shape[0]
    tm = min(tm, t)
    nb = t // tm
    ni, npar, no, nd = len(ins), len(params), len(outs), len(deps)

    def body(*refs):
        iv = [r[...] for r in refs[:ni]]
        pv = [r[...] for r in refs[ni:ni + npar]]
        o_refs = refs[ni + npar + nd:ni + npar + nd + no]
        r_refs = refs[ni + npar + nd + no:]
        ovals, rvals = fn(iv, pv)
        for o_ref, val in zip(o_refs, ovals):
            off = 0
            if isinstance(val, tuple) and val[0] == "at":
                _, off, val = val
            parts = val if isinstance(val, (list, tuple)) else [val]
            for p in parts:
                o_ref[:, off:off + p.shape[1]] = p.astype(o_ref.dtype)
                off += p.shape[1]
        if r_refs:
            @pl.when(pl.program_id(0) == 0)
            def _():
                for r in r_refs:
                    r[...] = jnp.zeros_like(r)
            for r, val in zip(r_refs, rvals):
                r[...] += val

    in_specs = [pl.BlockSpec((tm, w), functools.partial(lambda i, cb: (i, cb), cb=cb)) for (_, w, cb) in ins]
    in_specs += [pl.BlockSpec(p.shape, lambda i: (0, 0)) for p in params]
    in_specs += [ANY_SPEC] * nd
    out_shape = [jax.ShapeDtypeStruct((t, w), dt) for (w, dt) in outs]
    out_shape += [jax.ShapeDtypeStruct((1, w), F32) for w in reds]
    out_specs = [pl.BlockSpec((tm, w), lambda i: (i, 0)) for (w, _) in outs]
    out_specs += [pl.BlockSpec((1, w), lambda i: (0, 0)) for w in reds]
    res = _pcall(body, name=name, grid=(nb,), in_specs=in_specs, out_specs=out_specs, out_shape=out_shape,
                 compiler_params=_params("arbitrary"))(*[a for a, _, _ in ins], *params, *deps)
    return res


def _colsum(v):
    return jnp.sum(v, axis=0, keepdims=True)


def _mm(name, a, b, *, ta=False, tb=False, b_stacked=False, out_stacked=False, tm=ROWS, tn=1024, tk=2816,
        out_dtype=BF16, epi=None, epi_ins=(), epi_outs=None, epi_reds=(), deps=(), j_outer=False, a_halves=False,
        b_halves=False, b_resident=False, wgrad=None, wgrad_rows=False):
    if a_halves:
        _, m, kdim = a.shape
        kdim *= 2
    elif ta:
        kdim, m = a.shape
    else:
        m, kdim = a.shape
    ns = None
    if b_stacked:
        ns = b.shape[2]
        n = b.shape[1] if tb else N_CHIPS * ns
        assert kdim == (N_CHIPS * ns if tb else b.shape[1]), (name, a.shape, b.shape)
    else:
        if b_halves:
            n = 2 * b.shape[2]
            assert kdim == b.shape[1] and not tb, (name, a.shape, b.shape)
        else:
            n = b.shape[0] if tb else b.shape[1]
            assert kdim == (b.shape[1] if tb else b.shape[0]), (name, a.shape, b.shape)
        if out_stacked:
            ns = n // N_CHIPS

    def shards(want):
        return max(g for g in (1, 2, 4) if g * ns <= max(want, ns))

    tm = _pick(m, tm, 128 if ta else 8)
    gn = gk = 1
    if (b_stacked and not tb) or out_stacked:
        gn = shards(min(tn, n // 2) if b_halves else tn)
        tn = gn * ns
    else:
        tn = _pick(n, tn)
    if b_stacked and tb:
        gk = shards(tk)
        tk = gk * ns
    else:
        tk = _pick(kdim, tk, 8 if ta else 128)
    nm, nn, nk = m // tm, n // tn, kdim // tk

    def ij(f):
        return (lambda g0, g1, k: f(g1, g0, k)) if j_outer else f

    if a_halves:
        assert b_stacked and tb and gk == N_CHIPS and nk == 1, name
        a_spec = pl.BlockSpec((2, tm, kdim // 2), ij(lambda i, j, k: (0, i, 0)))
    elif ta:
        a_spec = pl.BlockSpec((tk, tm), ij(lambda i, j, k: (k, i)))
    else:
        a_spec = pl.BlockSpec((tm, tk), ij(lambda i, j, k: (i, k)))
    if b_stacked and not tb:
        b_spec = pl.BlockSpec((gn, tk, ns), ij(lambda i, j, k: (j, k, 0)))
    elif b_stacked and tb:
        b_spec = pl.BlockSpec((gk, tn, ns), ij(lambda i, j, k: (k, j, 0)))
    elif b_halves:
        bph = (n // 2) // tn
        b_spec = pl.BlockSpec((None, tk, tn), ij(lambda i, j, k: (j // bph, k, j % bph)))
    elif tb:
        b_spec = pl.BlockSpec((tn, tk), ij(lambda i, j, k: (j, k)))
    else:
        b_spec = pl.BlockSpec((tk, tn), ij(lambda i, j, k: (k, j)))
    dims = (((0 if ta else 1,), (1 if tb else 0,)), ((), ()))

    if epi_outs is None:
        if out_stacked:
            epi_outs = [((N_CHIPS, m, ns), out_dtype, (gn, tm, ns), lambda i, j: (j, i, 0))]
        else:
            epi_outs = [((m, n), out_dtype, (tm, tn), lambda i, j: (i, j))]
    ne, no, nd, nr = len(epi_ins), len(epi_outs), len(deps), len(epi_reds)
    nx = 0 if wgrad is None else 1
    assert not (nr or nx) or (nn == 1 and not j_outer), name
    assert not nx or (nk == 1 and not ta and not a_halves and kdim % N_CHIPS == 0), name

    def body(a_ref, b_ref, *rest):
        e_refs = rest[:ne]
        x_refs = rest[ne:ne + nx]
        o_refs = rest[ne + nx + nd:ne + nx + nd + no]
        r_refs = rest[ne + nx + nd + no:ne + nx + nd + no + nr]
        wg_refs = rest[ne + nx + nd + no + nr:ne + nx + nd + no + nr + nx]
        scratch = rest[ne + nx + nd + no + nr + nx:]
        av = None if a_halves else a_ref[...].astype(BF16)
        if nx:
            wg_acc = scratch[-1]
            i = pl.program_id(0)
            pw_ = lax.dot_general(x_refs[0][...].astype(BF16), av, (((0,), (0,)), ((), ())), preferred_element_type=F32)

            @pl.when(i == 0)
            def _():
                wg_acc[...] = pw_

            @pl.when(i > 0)
            def _():
                wg_acc[...] += pw_

            @pl.when(i == nm - 1)
            def _():
                if wgrad_rows:
                    wg_refs[0][...] = wg_acc[...].astype(BF16)
                else:
                    ks = kdim // N_CHIPS
                    for s in range(N_CHIPS):
                        wg_refs[0][s] = wg_acc[:, s * ks:(s + 1) * ks].astype(BF16)
        if b_stacked and not tb:
            parts = [lax.dot_general(av, b_ref[s].astype(BF16), dims, preferred_element_type=F32) for s in range(gn)]
        elif b_stacked and tb:
            p = None
            for s in range(gk):
                if a_halves:
                    a_s = a_ref[s // 2, :, (s % 2) * ns:(s % 2 + 1) * ns].astype(BF16)
                else:
                    a_s = av[:, s * ns:(s + 1) * ns]
                q = lax.dot_general(a_s, b_ref[s].astype(BF16), dims, preferred_element_type=F32)
                p = q if p is None else p + q
            parts = [p]
        else:
            parts = [lax.dot_general(av, b_ref[...].astype(BF16), dims, preferred_element_type=F32)]

        def finish(acc_parts):
            if epi is None and out_stacked:
                acc = acc_parts[0]
                for s in range(gn):
                    o_refs[0][s] = acc[:, s * ns:(s + 1) * ns].astype(out_dtype)
            elif epi is None:
                w = acc_parts[0].shape[1]
                for s, part in enumerate(acc_parts):
                    o_refs[0][:, s * w:(s + 1) * w] = part.astype(out_dtype)
            else:
                acc = acc_parts[0] if len(acc_parts) == 1 else jnp.concatenate(acc_parts, axis=1)
                vals = epi(acc, *[r[...] for r in e_refs])
                if nr:
                    vals, reds = vals

                    @pl.when(pl.program_id(0) == 0)
                    def _():
                        for r in r_refs:
                            r[...] = jnp.zeros_like(r)
                    for r, val in zip(r_refs, reds):
                        r[...] += val
                for o_ref, val in zip(o_refs, vals):
                    if isinstance(val, tuple) and val[0] == "at":
                        o_ref[:, val[1]:val[1] + val[2].shape[1]] = val[2].astype(o_ref.dtype)
                    else:
                        o_ref[...] = val.astype(o_ref.dtype)

        if nk == 1:
            finish(parts)
        else:
            acc_ref = scratch[0]
            k = pl.program_id(2)
            w = parts[0].shape[1]

            @pl.when(k == 0)
            def _():
                for s, part in enumerate(parts):
                    acc_ref[:, s * w:(s + 1) * w] = part

            @pl.when(k > 0)
            def _():
                for s, part in enumerate(parts):
                    acc_ref[:, s * w:(s + 1) * w] += part

            @pl.when(k == nk - 1)
            def _():
                finish([acc_ref[...]])

    def _ij(f):
        return ij(lambda i, j, k: f(i, j))

    if b_resident:
        assert nk == 1 and (nn == 1 or j_outer), name
        b_spec = pl.BlockSpec(b_spec.block_shape, b_spec.index_map, pipeline_mode=pl.Buffered(1))
    in_specs = [a_spec, b_spec] + [pl.BlockSpec(blk, _ij(f)) for (_, blk, f) in epi_ins]
    if nx:
        in_specs.append(pl.BlockSpec((tm, wgrad.shape[1]), lambda i, j, k: (i, 0)))
    in_specs += [ANY_SPEC] * nd
    out_specs = [pl.BlockSpec(blk, _ij(f)) for (_, _, blk, f) in epi_outs]
    out_specs += [pl.BlockSpec((1, w), lambda *_: (0, 0)) for w in epi_reds]
    out_shape = [jax.ShapeDtypeStruct(s, dt) for (s, dt, _, _) in epi_outs] + [jax.ShapeDtypeStruct((1, w), F32) for w in epi_reds]
    scratch = [pltpu.VMEM((tm, tn), F32)] if nk > 1 else []
    if nx:
        wg_shape = (wgrad.shape[1], kdim) if wgrad_rows else (N_CHIPS, wgrad.shape[1], kdim // N_CHIPS)
        out_specs.append(pl.BlockSpec(wg_shape, lambda *_: (0,) * len(wg_shape)))
        out_shape.append(jax.ShapeDtypeStruct(wg_shape, BF16))
        scratch.append(pltpu.VMEM((wgrad.shape[1], kdim), F32))
    grid = (nn, nm, nk) if j_outer else (nm, nn, nk)
    sem = ("arbitrary",) * 3 if (nr or nx) else ("parallel", "parallel", "arbitrary")
    res = _pcall(body, name=name, grid=grid, in_specs=in_specs, out_specs=out_specs, out_shape=out_shape, scratch_shapes=scratch,
                 compiler_params=_params(*sem))(a, b, *[x for x, _, _ in epi_ins], *([wgrad] * nx), *deps)
    return res[0] if len(res) == 1 else res


def _norm_fwd(name, x, g, sc, sh, deps=()):
    d = x.shape[1]

    def fn(iv, pv):
        (xv,), (gv, scv, shv) = iv, pv
        r = lax.rsqrt(jnp.mean(xv * xv, axis=-1, keepdims=True) + EPS)
        return [xv * r * gv * (1.0 + scv) + shv], []

    return _rowwise(name, fn, [(x, d, 0)], [g, sc, sh], [(d, BF16)], [], ROWS, deps=deps)[0]


def _adamw(name, w, g, m, v, tm=256):
    c = w.shape[1]

    def fn(iv, pv):
        wv, gv, mv, vv = iv
        mn = ADAM_B1 * mv + (1.0 - ADAM_B1) * gv
        vn = ADAM_B2 * vv + (1.0 - ADAM_B2) * (gv * gv)
        m_hat = mn / (1.0 - ADAM_B1 ** ADAM_STEP)
        v_hat = vn / (1.0 - ADAM_B2 ** ADAM_STEP)
        delta = -ADAM_LR * (m_hat / (jnp.sqrt(v_hat) + ADAM_EPS) + ADAM_WD * wv)
        return [gv, delta, mn, vn], []

    return _rowwise(name, fn, [(w, c, 0), (g, c, 0), (m, c, 0), (v, c, 0)], [], [(c, F32)] * 4, [], _pick(w.shape[0], tm, 8))


def _unpack_plan(shape):
    n = math.prod(shape)
    if len(shape) == 2 and shape[0] == 1 and n % 128 == 0:
        return [((slice(None), slice(128 * r, 128 * (r + 1))), slice(r, r + 1), slice(None)) for r in range(n // 128)]
    if len(shape) == 2 and shape[0] == 1 and n < 128:
        return [((slice(None), slice(None)), slice(0, 1), slice(0, n))]
    if len(shape) == 1 and n % 128 == 0:
        return [((slice(128 * r, 128 * (r + 1)),), r, slice(None)) for r in range(n // 128)]
    if len(shape) == 3 and shape[0] == 1 and shape[2] == 64:
        return [((0, slice(2 * r + h, 2 * r + h + 1), slice(None)), slice(r, r + 1), slice(64 * h, 64 * (h + 1)))
                for r in range(n // 128) for h in range(2)]
    if len(shape) == 4 and shape[0] == 1 and shape[2:] == (128, 128):
        return [((0, k), slice(128 * k, 128 * (k + 1)), slice(None)) for k in range(shape[1])]
    return None


def _adamw_small(name, w, g8, m, v, shapes):
    r = w.shape[0]
    plans, rows0, off = [], [], 0
    for s in shapes:
        plans.append(_unpack_plan(s))
        rows0.append(off // 128)
        off += math.prod(s) + (-math.prod(s)) % 128
    direct = [i for i, p in enumerate(plans) if p is not None]

    def body(w_ref, g_ref, m_ref, v_ref, *out):
        packed = out[4 * len(direct):]
        gv = g_ref[0:r, :]
        for k in range(1, N_DEV):
            gv = gv + g_ref[k * r:(k + 1) * r, :]
        mn = ADAM_B1 * m_ref[...] + (1.0 - ADAM_B1) * gv
        vn = ADAM_B2 * v_ref[...] + (1.0 - ADAM_B2) * (gv * gv)
        m_hat = mn / (1.0 - ADAM_B1 ** ADAM_STEP)
        v_hat = vn / (1.0 - ADAM_B2 ** ADAM_STEP)
        packed[0][...] = gv
        packed[1][...] = -ADAM_LR * (m_hat / (jnp.sqrt(v_hat) + ADAM_EPS) + ADAM_WD * w_ref[...])
        packed[2][...] = mn
        packed[3][...] = vn
        for di, i in enumerate(direct):
            for kind in range(4):
                o_ref, src = out[4 * di + kind], packed[kind]
                for o_idx, row, lanes in plans[i]:
                    row = (rows0[i] + row) if isinstance(row, int) else slice(rows0[i] + row.start, rows0[i] + row.stop)
                    o_ref[o_idx] = src[row, lanes]

    out_shape = [jax.ShapeDtypeStruct(shapes[i], F32) for i in direct for _ in range(4)]
    out_shape += [jax.ShapeDtypeStruct((r, 128), F32)] * 4
    res = _pcall(body, name=name, out_shape=out_shape,
                 compiler_params=pltpu.CompilerParams(vmem_limit_bytes=VMEM_LIMIT))(w, g8, m, v)
    per_param = [None] * len(shapes)
    for di, i in enumerate(direct):
        per_param[i] = res[4 * di:4 * di + 4]
    return per_param, res[4 * len(direct):]


def _pool_fwd(name, uin, pool_w, pool_b, pool_scale, w_up, pw, tm=ROWS):
    t = uin.shape[0]
    tm = min(tm, t)
    ng = len(POOL_WINDOWS)
    gw = pw // ng
    ns = w_up.shape[2]

    def body(u_ref, w_ref, b_ref, s_ref, wu_ref, p_ref, z_ref, y_ref, ext):
        i = pl.program_id(0)

        @pl.when(i == 0)
        def _():
            ext[0:POOL_HALO, :] = jnp.zeros((POOL_HALO, pw), F32)

        u = u_ref[...].astype(F32)
        ext[POOL_HALO:POOL_HALO + tm, :] = u
        pos = i * tm + lax.broadcasted_iota(jnp.int32, (tm, 1), 0)
        for k, win in enumerate(POOL_WINDOWS):
            cols = slice(k * gw, (k + 1) * gw)
            acc = u[:, cols]
            for j in range(1, win):
                acc = acc + ext[POOL_HALO - j:POOL_HALO - j + tm, cols]
            cnt = jnp.minimum(pos + 1, win).astype(F32)
            z = acc / cnt - u[:, cols]
            zp = jnp.dot(z.astype(BF16), w_ref[k].astype(BF16), preferred_element_type=F32) + b_ref[:, cols]
            p_ref[:, cols] = (zp * s_ref[:, cols]).astype(BF16)
            z_ref[:, cols] = z.astype(BF16)
        ext[0:POOL_HALO, :] = u[tm - POOL_HALO:tm, :]
        pv = p_ref[...]
        for s in range(N_CHIPS):
            y_ref[:, s * ns:(s + 1) * ns] = jnp.dot(pv, wu_ref[s], preferred_element_type=F32).astype(BF16)

    full3 = lambda i: (0, 0, 0)
    return _pcall(
        body, name=name, grid=(t // tm,),
        in_specs=[pl.BlockSpec((tm, pw), lambda i: (i, 0)), pl.BlockSpec(pool_w.shape, full3),
                  pl.BlockSpec(pool_b.shape, lambda i: (0, 0)), pl.BlockSpec(pool_scale.shape, lambda i: (0, 0)),
                  pl.BlockSpec(w_up.shape, full3)],
        out_specs=[pl.BlockSpec((tm, pw), lambda i: (i, 0))] * 2 + [pl.BlockSpec((tm, N_CHIPS * ns), lambda i: (i, 0))],
        out_shape=[jax.ShapeDtypeStruct((t, pw), BF16)] * 2 + [jax.ShapeDtypeStruct((t, N_CHIPS * ns), BF16)],
        scratch_shapes=[pltpu.VMEM((POOL_HALO + tm, pw), F32)],
        compiler_params=_params("arbitrary"))(uin, pool_w, pool_b, pool_scale, w_up)


def _pool_bwd(name, dy, z, p, pool_w, pool_b, pool_scale, w_up, duin, tm=ROWS):
    t, pw = z.shape
    tm = min(tm, t)
    nb = t // tm
    ng = len(POOL_WINDOWS)
    gw = pw // ng
    ns = w_up.shape[2]

    def body(dy_ref, z_ref, p_ref, w_ref, b_ref, s_ref, wu_ref, duin_ref, du_ref, dw_ref, db_ref, ds_ref, dwu_ref, ext, dwu_acc):
        i = pl.program_id(0)

        @pl.when(i == 0)
        def _():
            ext[tm:tm + POOL_HALO, :] = jnp.zeros((POOL_HALO, pw), F32)
            dw_ref[...] = jnp.zeros_like(dw_ref)
            db_ref[...] = jnp.zeros_like(db_ref)
            ds_ref[...] = jnp.zeros_like(ds_ref)
            dwu_acc[...] = jnp.zeros_like(dwu_acc)

        dwu_acc[...] += lax.dot_general(p_ref[...], dy_ref[...], (((0,), (0,)), ((), ())), preferred_element_type=F32)

        @pl.when(i == nb - 1)
        def _():
            for s in range(N_CHIPS):
                dwu_ref[s] = dwu_acc[:, s * ns:(s + 1) * ns].astype(BF16)

        pos = (nb - 1 - i) * tm + lax.broadcasted_iota(jnp.int32, (tm, 1), 0)
        dp = None
        for s in range(N_CHIPS):
            q = lax.dot_general(dy_ref[:, s * ns:(s + 1) * ns], wu_ref[s], (((1,), (1,)), ((), ())), preferred_element_type=F32)
            dp = q if dp is None else dp + q
        for k, win in enumerate(POOL_WINDOWS):
            cols = slice(k * gw, (k + 1) * gw)
            zk = z_ref[:, cols]
            dpk = dp[:, cols]
            wk = w_ref[k].astype(BF16)
            zp = jnp.dot(zk, wk, preferred_element_type=F32) + b_ref[:, cols]
            ds_ref[:, cols] += _colsum(dpk * zp)
            dzp = dpk * s_ref[:, cols]
            db_ref[:, cols] += _colsum(dzp)
            dzpb = dzp.astype(BF16)
            dz = lax.dot_general(dzpb, wk, (((1,), (1,)), ((), ())), preferred_element_type=F32)
            dw_ref[k] += lax.dot_general(zk, dzpb, (((0,), (0,)), ((), ())), preferred_element_type=F32)
            cnt = jnp.minimum(pos + 1, win).astype(F32)
            r = dz / cnt
            ext[0:tm, cols] = r
            acc = r - dz
            for j in range(1, win):
                acc = acc + ext[j:j + tm, cols]
            du_ref[:, cols] = acc.astype(BF16)
        ext[tm:tm + POOL_HALO, :] = ext[0:POOL_HALO, :]

    rev = lambda i: (nb - 1 - i, 0)
    full3 = lambda i: (0, 0, 0)
    return _pcall(
        body, name=name, grid=(nb,),
        in_specs=[pl.BlockSpec((tm, N_CHIPS * ns), rev), pl.BlockSpec((tm, pw), rev), pl.BlockSpec((tm, pw), rev),
                  pl.BlockSpec(pool_w.shape, full3), pl.BlockSpec(pool_b.shape, lambda i: (0, 0)),
                  pl.BlockSpec(pool_scale.shape, lambda i: (0, 0)), pl.BlockSpec(w_up.shape, full3), ANY_SPEC],
        out_specs=[pl.BlockSpec((tm, pw), rev), pl.BlockSpec(pool_w.shape, full3),
                   pl.BlockSpec((1, pw), lambda i: (0, 0)), pl.BlockSpec((1, pw), lambda i: (0, 0)), pl.BlockSpec(w_up.shape, full3)],
        out_shape=[jax.ShapeDtypeStruct(duin.shape, BF16), jax.ShapeDtypeStruct(pool_w.shape, F32),
                   jax.ShapeDtypeStruct((1, pw), F32), jax.ShapeDtypeStruct((1, pw), F32), jax.ShapeDtypeStruct(w_up.shape, BF16)],
        scratch_shapes=[pltpu.VMEM((tm + POOL_HALO, pw), F32), pltpu.VMEM((pw, N_CHIPS * ns), F32)], input_output_aliases={7: 0},
        compiler_params=_params("arbitrary"))(dy, z, p, pool_w, pool_b, pool_scale, w_up, duin)


def _ssm_disc(lrl, li, ldt):
    lr = -jnp.exp(lrl)
    dt = jnp.exp(ldt)
    mag = jnp.exp(lr * dt)
    ang = li * dt
    ab_re = mag * jnp.cos(ang)
    ab_im = mag * jnp.sin(ang)
    num_re = ab_re - 1.0
    num_im = ab_im
    den = lr * lr + li * li
    f_re = (num_re * lr + num_im * li) / den
    f_im = (num_im * lr - num_re * li) / den
    return lr, dt, mag, ang, ab_re, ab_im, num_re, num_im, den, f_re, f_im


def _ssm_prep(name, lrl, li, ldt, b_re, b_im):
    gn, h = b_re.shape

    def body(lrl_ref, li_ref, ldt_ref, br_ref, bi_ref, lrdt_ref, ang_ref, bbr_ref, bbi_ref):
        lr, dt, _, ang, _, _, _, _, _, f_re, f_im = _ssm_disc(lrl_ref[...], li_ref[...], ldt_ref[...])
        lrdt_ref[...] = lr * dt
        ang_ref[...] = ang
        br, bi = br_ref[...], bi_ref[...]
        bbr_ref[...] = f_re * br - f_im * bi
        bbi_ref[...] = f_re * bi + f_im * br

    col = jax.ShapeDtypeStruct((gn, 1), F32)
    mat = jax.ShapeDtypeStruct((gn, h), F32)
    return _pcall(body, name=name, out_shape=[col, col, mat, mat])(lrl, li, ldt, b_re, b_im)


def _ssm_param_bwd(name, lrl, li, ldt, b_re, b_im, g_abre, g_abim, g_bbre, g_bbim):
    gn, h = b_re.shape

    def body(lrl_ref, li_ref, ldt_ref, br_ref, bi_ref, gar_ref, gai_ref, gbr_ref, gbi_ref,
             dlrl_ref, dli_ref, dldt_ref, dbr_ref, dbi_ref):
        li_v = li_ref[...]
        lr, dt, mag, ang, ab_re, ab_im, num_re, num_im, den, f_re, f_im = _ssm_disc(lrl_ref[...], li_v, ldt_ref[...])
        br, bi = br_ref[...], bi_ref[...]
        gbr, gbi = gbr_ref[...], gbi_ref[...]
        g_fre = jnp.sum(gbr * br + gbi * bi, axis=1, keepdims=True)
        g_fim = jnp.sum(gbi * br - gbr * bi, axis=1, keepdims=True)
        dbr_ref[...] = gbr * f_re + gbi * f_im
        dbi_ref[...] = gbi * f_re - gbr * f_im
        g_num_re = (g_fre * lr - g_fim * li_v) / den
        g_num_im = (g_fre * li_v + g_fim * lr) / den
        g_den = -(g_fre * f_re + g_fim * f_im) / den
        g_lr = (g_fre * num_re + g_fim * num_im) / den + g_den * 2.0 * lr
        g_li = (g_fre * num_im - g_fim * num_re) / den + g_den * 2.0 * li_v
        g_are = gar_ref[...] + g_num_re
        g_aim = gai_ref[...] + g_num_im
        g_mag = g_are * jnp.cos(ang) + g_aim * jnp.sin(ang)
        g_ang = g_aim * ab_re - g_are * ab_im
        g_lrdt = g_mag * mag
        g_lr = g_lr + g_lrdt * dt
        g_dt = g_lrdt * lr + g_ang * li_v
        g_li = g_li + g_ang * dt
        dlrl_ref[...] = g_lr * lr
        dli_ref[...] = g_li
        dldt_ref[...] = g_dt * dt

    col = jax.ShapeDtypeStruct((gn, 1), F32)
    mat = jax.ShapeDtypeStruct((gn, h), F32)
    return _pcall(body, name=name, out_shape=[col, col, col, mat, mat])(lrl, li, ldt, b_re, b_im, g_abre, g_abim, g_bbre, g_bbim)


def _pow_rows(lrdt, ang, k):
    mag = jnp.exp(k * lrdt)
    return mag * jnp.cos(k * ang), mag * jnp.sin(k * ang)


def _ssm_chunk(t):
    return 256 if t >= 2048 else 128


def _to_segments(dst, srcs, jn):
    for q, src in enumerate(srcs):
        for j in range(jn):
            dst[8 * j:8 * j + 8, 128 * q:128 * (q + 1)] = src[pl.ds(j, 8, stride=jn), :]


def _from_segments(dst, src, q, jn, dtype):
    for s in range(8):
        dst[s * jn:(s + 1) * jn, 128 * q:128 * (q + 1)] = src[q, pl.ds(s, jn, stride=8), :].astype(dtype)


def _fill_rows8(dst_re, dst_im, v_re, v_im):
    for j in range(v_re.shape[0]):
        dst_re[8 * j:8 * j + 8, :] = jnp.broadcast_to(v_re[j:j + 1, :], (8, v_re.shape[1]))
        dst_im[8 * j:8 * j + 8, :] = jnp.broadcast_to(v_im[j:j + 1, :], (8, v_im.shape[1]))


def _cmul(ar, ai, br, bi):
    return ar * br - ai * bi, ar * bi + ai * br


def _cmul_conj(ar, ai, br, bi):
    return ar * br + ai * bi, ar * bi - ai * br


def _ssm_fwd(name, uin, lrdt, ang, bb_re, bb_im, cc_re, cc_im, d_skip, sw):
    t = uin.shape[0]
    gn = lrdt.shape[1]
    lc = _ssm_chunk(t)
    jn = lc // 8
    ub, sb = sw // SSM_BLOCKS, gn // SSM_BLOCKS
    nq = sw // 128
    assert ub == 128 and nq == SSM_BLOCKS

    def body(u_ref, lrdt_ref, ang_ref, bbr_ref, bbi_ref, ccr_ref, cci_ref, d_ref, y_ref, ge_ref, sre_ref, sim_ref,
             p_re, p_im, a_re, a_im, up, yp, cst_re, cst_im, car_re, car_im):
        i = pl.program_id(0)
        lrdt_v, ang_v = lrdt_ref[...], ang_ref[...]

        @pl.when(i == 0)
        def _():
            k = (lax.broadcasted_iota(jnp.int32, (jn, 1), 0) + 1).astype(F32)
            _fill_rows8(p_re, p_im, *_pow_rows(lrdt_v, ang_v, k))
            car_re[...] = jnp.zeros_like(car_re)
            car_im[...] = jnp.zeros_like(car_im)

        for q in range(nq):
            yp[q] = u_ref[:, 128 * q:128 * (q + 1)].astype(F32)
        _to_segments(up, [yp.at[q] for q in range(nq)], jn)
        u = up[...]
        ubf = u.astype(BF16)
        for q in range(SSM_BLOCKS):
            uq = ubf[:, q * ub:(q + 1) * ub]
            a_re[:, q * sb:(q + 1) * sb] = jnp.dot(uq, bbr_ref[q], preferred_element_type=F32)
            a_im[:, q * sb:(q + 1) * sb] = jnp.dot(uq, bbi_ref[q], preferred_element_type=F32)
        a1r, a1i = _pow_rows(lrdt_v, ang_v, 1.0)
        ajr, aji = _pow_rows(lrdt_v, ang_v, float(jn))
        for q in range(0, SSM_BLOCKS, 2):
            cols = slice(q * sb, (q + 2) * sb)
            ar8 = jnp.broadcast_to(a1r[:, cols], (8, 2 * sb))
            ai8 = jnp.broadcast_to(a1i[:, cols], (8, 2 * sb))

            def step(j, carry, cols=cols, ar8=ar8, ai8=ai8):
                sr, si = carry
                rows = pl.ds(pl.multiple_of(j * 8, 8), 8)
                mr, mi = _cmul(ar8, ai8, sr, si)
                nr, ni = mr + a_re[rows, cols], mi + a_im[rows, cols]
                a_re[rows, cols] = nr
                a_im[rows, cols] = ni
                return nr, ni

            lax.fori_loop(1, jn, step, (a_re[0:8, cols], a_im[0:8, cols]), unroll=8)
        er, ei = a_re[lc - 8:lc, :], a_im[lc - 8:lc, :]
        gr, gi = car_re[...], car_im[...]
        for s in range(8):
            cst_re[s:s + 1, :] = gr
            cst_im[s:s + 1, :] = gi
            mr, mi = _cmul(ajr, aji, gr, gi)
            gr, gi = mr + er[s:s + 1, :], mi + ei[s:s + 1, :]
        car_re[...] = gr
        car_im[...] = gi
        for q in range(SSM_BLOCKS):
            cols = slice(q * sb, (q + 1) * sb)
            cr = jnp.tile(cst_re[:, cols], (jn, 1))
            ci = jnp.tile(cst_im[:, cols], (jn, 1))
            mr, mi = _cmul(p_re[:, cols], p_im[:, cols], cr, ci)
            srb, sib = (a_re[:, cols] + mr).astype(BF16), (a_im[:, cols] + mi).astype(BF16)
            sre_ref[:, cols] = srb
            sim_ref[:, cols] = sib
            ycols = slice(q * ub, (q + 1) * ub)
            y = (jnp.dot(srb, ccr_ref[q], preferred_element_type=F32) - jnp.dot(sib, cci_ref[q], preferred_element_type=F32)
                 + d_ref[:, ycols] * u[:, ycols])
            yp[q] = y
            _from_segments(y_ref, yp, q, jn, F32)
            yt = y_ref[:, ycols]
            ge_ref[:, ycols] = (0.5 * yt * (1.0 + lax.erf(yt * (1.0 / math.sqrt(2.0))))).astype(BF16)

    row = lambda i: (0, 0)
    blk3 = lambda i: (0, 0, 0)
    return _pcall(
        body, name=name, grid=(t // lc,),
        in_specs=[pl.BlockSpec((lc, sw), lambda i: (i, 1)), pl.BlockSpec((1, gn), row), pl.BlockSpec((1, gn), row),
                  pl.BlockSpec(bb_re.shape, blk3), pl.BlockSpec(bb_im.shape, blk3),
                  pl.BlockSpec(cc_re.shape, blk3), pl.BlockSpec(cc_im.shape, blk3), pl.BlockSpec((1, sw), row)],
        out_specs=[pl.BlockSpec((lc, sw), lambda i: (i, 0)), pl.BlockSpec((lc, sw), lambda i: (i, 0)),
                   pl.BlockSpec((lc, gn), lambda i: (i, 0)), pl.BlockSpec((lc, gn), lambda i: (i, 0))],
        out_shape=[jax.ShapeDtypeStruct((t, sw), F32), jax.ShapeDtypeStruct((t, sw), BF16),
                   jax.ShapeDtypeStruct((t, gn), BF16), jax.ShapeDtypeStruct((t, gn), BF16)],
        scratch_shapes=[pltpu.VMEM((lc, gn), F32), pltpu.VMEM((lc, gn), F32), pltpu.VMEM((lc, gn), F32), pltpu.VMEM((lc, gn), F32),
                        pltpu.VMEM((lc, sw), F32), pltpu.VMEM((nq, lc, 128), F32),
                        pltpu.VMEM((8, gn), F32), pltpu.VMEM((8, gn), F32), pltpu.VMEM((1, gn), F32), pltpu.VMEM((1, gn), F32)],
        compiler_params=_params("arbitrary"))(uin, lrdt, ang, bb_re, bb_im, cc_re, cc_im, d_skip)


def _ssm_bwd(name, dge, y, uin, s_re, s_im, lrdt, ang, bbt_re, bbt_im, cct_re, cct_im, d_skip, sw, duin):
    t = uin.shape[0]
    gn = lrdt.shape[1]
    lc = _ssm_chunk(t)
    nb = t // lc
    jn = lc // 8
    ub, sb = sw // SSM_BLOCKS, gn // SSM_BLOCKS
    nq = sw // 128
    tail = 16

    def body(dge_ref, y_ref, u_ref, sre_ref, sim_ref, tre_ref, tim_ref, lrdt_ref, ang_ref, btr_ref, bti_ref, ctr_ref, cti_ref,
             d_ref, duin_ref, du_ref, dar_ref, dai_ref, dbr_ref, dbi_ref, dcr_ref, dci_ref, dd_ref,
             q_re, q_im, a_re, a_im, dyp, up, dys, us, dup, cst_re, cst_im, sp_re, sp_im, car_re, car_im):
        i = pl.program_id(0)
        lrdt_v, ang_v = lrdt_ref[...], ang_ref[...]

        @pl.when(i == 0)
        def _():
            k = (jn - lax.broadcasted_iota(jnp.int32, (jn, 1), 0)).astype(F32)
            _fill_rows8(q_re, q_im, *_pow_rows(lrdt_v, ang_v, k))
            car_re[...] = jnp.zeros_like(car_re)
            car_im[...] = jnp.zeros_like(car_im)
            for r in (dar_ref, dai_ref, dbr_ref, dbi_ref, dcr_ref, dci_ref, dd_ref):
                r[...] = jnp.zeros_like(r)

        yv = y_ref[...]
        ut = u_ref[...].astype(F32)
        cdf =0.5 * (1.0 + lax.erf(yv * (1.0 / math.sqrt(2.0))))
        pdf = jnp.exp(-0.5 * yv * yv) * (1.0 / math.sqrt(2.0 * math.pi))
        dyt = dge_ref[...].astype(F32) * (cdf + yv * pdf)
        dd_ref[...] += _colsum(dyt * ut)
        for q in range(nq):
            dys[q] = dyt[:, 128 * q:128 * (q + 1)]
            us[q] = ut[:, 128 * q:128 * (q + 1)]
        _to_segments(dyp, [dys.at[q] for q in range(nq)], jn)
        _to_segments(up, [us.at[q] for q in range(nq)], jn)
        dy = dyp[...]
        u = up[...]
        dyb = dy.astype(BF16)
        ubf = u.astype(BF16)
        for q in range(SSM_BLOCKS):
            dq = dyb[:, q * ub:(q + 1) * ub]
            a_re[:, q * sb:(q + 1) * sb] = jnp.dot(dq, ctr_ref[q], preferred_element_type=F32)
            a_im[:, q * sb:(q + 1) * sb] = -jnp.dot(dq, cti_ref[q], preferred_element_type=F32)
        a1r, a1i = _pow_rows(lrdt_v, ang_v, 1.0)
        ajr, aji = _pow_rows(lrdt_v, ang_v, float(jn))
        for q in range(0, SSM_BLOCKS, 2):
            cols = slice(q * sb, (q + 2) * sb)
            ar8 = jnp.broadcast_to(a1r[:, cols], (8, 2 * sb))
            ai8 = jnp.broadcast_to(a1i[:, cols], (8, 2 * sb))

            def step(jj, carry, cols=cols, ar8=ar8, ai8=ai8):
                sr, si = carry
                rows = pl.ds(pl.multiple_of((jn - 2 - jj) * 8, 8), 8)
                mr, mi = _cmul_conj(ar8, ai8, sr, si)
                nr, ni = mr + a_re[rows, cols], mi + a_im[rows, cols]
                a_re[rows, cols] = nr
                a_im[rows, cols] = ni
                return nr, ni

            lax.fori_loop(0, jn - 1, step, (a_re[lc - 8:lc, cols], a_im[lc - 8:lc, cols]), unroll=8)
        er, ei = a_re[0:8, :], a_im[0:8, :]
        hr, hi = car_re[...], car_im[...]
        for s in range(7, -1, -1):
            cst_re[s:s + 1, :] = hr
            cst_im[s:s + 1, :] = hi
            mr, mi = _cmul_conj(ajr, aji, hr, hi)
            hr, hi = mr + er[s:s + 1, :], mi + ei[s:s + 1, :]
        car_re[...] = hr
        car_im[...] = hi
        first = (i == nb - 1).astype(F32)
        sp_re[0:tail, :] = tre_ref[...].astype(F32) * (1.0 - first)
        sp_im[0:tail, :] = tim_ref[...].astype(F32) * (1.0 - first)
        sp_re[tail:tail + 8, :] = sre_ref[lc - tail:lc, :].astype(F32)[tail - 8:tail]
        sp_im[tail:tail + 8, :] = sim_ref[lc - tail:lc, :].astype(F32)[tail - 8:tail]
        tn_dims = (((0,), (0,)), ((), ()))
        for q in range(SSM_BLOCKS):
            cols = slice(q * sb, (q + 1) * sb)
            ycols = slice(q * ub, (q + 1) * ub)
            cr = jnp.tile(cst_re[:, cols], (jn, 1))
            ci = jnp.tile(cst_im[:, cols], (jn, 1))
            mr, mi = _cmul_conj(q_re[:, cols], q_im[:, cols], cr, ci)
            lam_r, lam_i = a_re[:, cols] + mr, a_im[:, cols] + mi
            s_r, s_i = sre_ref[:, cols], sim_ref[:, cols]
            p0r, p0i = sp_re[tail - 1:tail + 7, cols], sp_im[tail - 1:tail + 7, cols]
            l0r, l0i, l1r, l1i = lam_r[0:8], lam_i[0:8], lam_r[8:lc], lam_i[8:lc]
            pvr, pvi = s_r.astype(F32)[0:lc - 8], s_i.astype(F32)[0:lc - 8]
            dar_ref[:, cols] += _colsum(l1r * pvr + l1i * pvi) + _colsum(l0r * p0r + l0i * p0i)
            dai_ref[:, cols] += _colsum(l1i * pvr - l1r * pvi) + _colsum(l0i * p0r - l0r * p0i)
            lrb, lib = lam_r.astype(BF16), lam_i.astype(BF16)
            dup[q] = (jnp.dot(lrb, btr_ref[q], preferred_element_type=F32)
                      + jnp.dot(lib, bti_ref[q], preferred_element_type=F32) + d_ref[:, ycols] * dy[:, ycols])
            _from_segments(du_ref, dup, q, jn, BF16)
            uq = ubf[:, ycols]
            dbr_ref[q] += lax.dot_general(uq, lrb, tn_dims, preferred_element_type=F32)
            dbi_ref[q] += lax.dot_general(uq, lib, tn_dims, preferred_element_type=F32)
            dq = dyb[:, ycols]
            dcr_ref[q] += lax.dot_general(s_r.astype(BF16), dq, tn_dims, preferred_element_type=F32)
            dci_ref[q] -= lax.dot_general(s_i.astype(BF16), dq, tn_dims, preferred_element_type=F32)

    rev = lambda i: (nb - 1 - i, 0)
    tailmap = lambda i: (jnp.maximum((nb - 1 - i) * (lc // tail) - 1, 0), 0)
    row = lambda i: (0, 0)
    blk3 = lambda i: (0, 0, 0)
    return _pcall(
        body, name=name, grid=(nb,),
        in_specs=[pl.BlockSpec((lc, sw), rev), pl.BlockSpec((lc, sw), rev), pl.BlockSpec((lc, sw), lambda i: (nb - 1 - i, 1)),
                  pl.BlockSpec((lc, gn), rev), pl.BlockSpec((lc, gn), rev),
                  pl.BlockSpec((tail, gn), tailmap), pl.BlockSpec((tail, gn), tailmap),
                  pl.BlockSpec((1, gn), row), pl.BlockSpec((1, gn), row),
                  pl.BlockSpec(bbt_re.shape, blk3), pl.BlockSpec(bbt_im.shape, blk3),
                  pl.BlockSpec(cct_re.shape, blk3), pl.BlockSpec(cct_im.shape, blk3), pl.BlockSpec((1, sw), row), ANY_SPEC],
        out_specs=[pl.BlockSpec((lc, sw), lambda i: (nb - 1 - i, 1)), pl.BlockSpec((1, gn), row), pl.BlockSpec((1, gn), row),
                   pl.BlockSpec((SSM_BLOCKS, ub, sb), blk3), pl.BlockSpec((SSM_BLOCKS, ub, sb), blk3),
                   pl.BlockSpec((SSM_BLOCKS, sb, ub), blk3), pl.BlockSpec((SSM_BLOCKS, sb, ub), blk3),
                   pl.BlockSpec((1, sw), row)],
        out_shape=[jax.ShapeDtypeStruct(duin.shape, BF16), jax.ShapeDtypeStruct((1, gn), F32), jax.ShapeDtypeStruct((1, gn), F32),
                   jax.ShapeDtypeStruct((SSM_BLOCKS, ub, sb), F32), jax.ShapeDtypeStruct((SSM_BLOCKS, ub, sb), F32),
                   jax.ShapeDtypeStruct((SSM_BLOCKS, sb, ub), F32), jax.ShapeDtypeStruct((SSM_BLOCKS, sb, ub), F32),
                   jax.ShapeDtypeStruct((1, sw), F32)],
        scratch_shapes=[pltpu.VMEM((lc, gn), F32), pltpu.VMEM((lc, gn), F32), pltpu.VMEM((lc, gn), F32), pltpu.VMEM((lc, gn), F32),
                        pltpu.VMEM((lc, sw), F32), pltpu.VMEM((lc, sw), F32),
                        pltpu.VMEM((nq, lc, 128), F32), pltpu.VMEM((nq, lc, 128), F32), pltpu.VMEM((nq, lc, 128), F32),
                        pltpu.VMEM((8, gn), F32), pltpu.VMEM((8, gn), F32),
                        pltpu.VMEM((tail + 8, gn), F32), pltpu.VMEM((tail + 8, gn), F32),
                        pltpu.VMEM((1, gn), F32), pltpu.VMEM((1, gn), F32)],
        input_output_aliases={14: 0},
        compiler_params=_params("arbitrary"))(dge, y, uin, s_re, s_im, s_re, s_im, lrdt, ang,
                                               bbt_re, bbt_im, cct_re, cct_im, d_skip, duin)


def _blockdiag_b(bb, sw):
    gpb = (sw // SSM_GROUP) // SSM_BLOCKS
    b4 = bb.reshape(SSM_BLOCKS, gpb, SSM_STATE, SSM_GROUP)
    eye = jnp.eye(gpb, dtype=bb.dtype)
    out = jnp.einsum('qgnh,gk->qghkn', b4, eye)
    return out.reshape(SSM_BLOCKS, gpb * SSM_GROUP, gpb * SSM_STATE)


def _blockdiag_c(cc, sw):
    gpb = (sw // SSM_GROUP) // SSM_BLOCKS
    c4 = cc.reshape(SSM_BLOCKS, gpb, SSM_GROUP, SSM_STATE)
    eye = jnp.eye(gpb, dtype=cc.dtype)
    out = jnp.einsum('qghn,gk->qgnkh', c4, eye)
    return out.reshape(SSM_BLOCKS, gpb * SSM_STATE, gpb * SSM_GROUP)


def _diag_of_b(dbb, sw):
    gpb = (sw // SSM_GROUP) // SSM_BLOCKS
    d5 = dbb.reshape(SSM_BLOCKS, gpb, SSM_GROUP, gpb, SSM_STATE)
    return jnp.einsum('qghgn->qgnh', d5).reshape(SSM_BLOCKS * gpb * SSM_STATE, SSM_GROUP)


def _diag_of_c(dcc, sw):
    gpb = (sw // SSM_GROUP) // SSM_BLOCKS
    d5 = dcc.reshape(SSM_BLOCKS, gpb, SSM_STATE, gpb, SSM_GROUP)
    return jnp.einsum('qgngh->qghn', d5).reshape(SSM_BLOCKS * gpb, SSM_GROUP, SSM_STATE)


def _ada_fwd(name, c_all, w_sh, b_sh):
    nb, d = c_all.shape
    ncol = w_sh.shape[1]
    tn = _pick(ncol, 768)

    def body(c_ref, w_ref, b_ref, o_ref):
        cv = c_ref[...]
        sil = cv * _sigmoid(cv)
        o_ref[...] = jnp.dot(sil, w_ref[...], preferred_element_type=F32, precision=lax.Precision.HIGHEST) + b_ref[...]

    return _pcall(body, name=name, grid=(ncol // tn,),
                  in_specs=[pl.BlockSpec((nb, d), lambda j: (0, 0)), pl.BlockSpec((d, tn), lambda j: (0, j)),
                            pl.BlockSpec((1, tn), lambda j: (0, j))],
                  out_specs=pl.BlockSpec((nb, tn), lambda j: (0, j)),
                  out_shape=jax.ShapeDtypeStruct((nb, ncol), F32), compiler_params=_params("parallel"))(c_all, w_sh, b_sh)


def _ada_bwd_adamw(name, c_all, dmod_sh, w, m, v):
    nb, d = c_all.shape
    ncol = dmod_sh.shape[1]
    tr, tn = _pick(d, 512), _pick(ncol, 768)

    def body(c_ref, g_ref, w_ref, m_ref, v_ref, go_ref, d_ref, mo_ref, vo_ref):
        cv = c_ref[...]
        sil = cv * _sigmoid(cv)
        gv = lax.dot_general(sil, g_ref[...], (((0,), (0,)), ((), ())), preferred_element_type=F32,
                             precision=lax.Precision.HIGHEST)
        mn = ADAM_B1 * m_ref[...] + (1.0 - ADAM_B1) * gv
        vn = ADAM_B2 * v_ref[...] + (1.0 - ADAM_B2) * (gv * gv)
        m_hat = mn / (1.0 - ADAM_B1 ** ADAM_STEP)
        v_hat = vn / (1.0 - ADAM_B2 ** ADAM_STEP)
        go_ref[...] = gv
        d_ref[...] = -ADAM_LR * (m_hat / (jnp.sqrt(v_hat) + ADAM_EPS) + ADAM_WD * w_ref[...])
        mo_ref[...] = mn
        vo_ref[...] = vn

    tile = pl.BlockSpec((tr, tn), lambda i, j: (i, j))
    return _pcall(body, name=name, grid=(d // tr, ncol // tn),
                  in_specs=[pl.BlockSpec((nb, tr), lambda i, j: (0, i)), pl.BlockSpec((nb, tn), lambda i, j: (0, j)), tile, tile, tile],
                  out_specs=[tile] * 4, out_shape=[jax.ShapeDtypeStruct((d, ncol), F32)] * 4,
                  compiler_params=_params("parallel", "parallel"))(c_all, dmod_sh, w, m, v)


def _place():
    return lax.axis_index("x"), lax.axis_index("y"), lax.axis_index("c")


def _allgather_small(name, blk, deps=()):
    m_per, n = blk.shape

    def body(x_ref, *rest):
        out_ref, send_sems, recv_sems, local_sem = rest[len(deps):]
        x, y, c = _place()
        me, sibling = (x, y, c), (x, y, 1 - c)
        chips = [(1 - x, y), (x, 1 - y), (1 - x, 1 - y)]

        def rows(px, py, pc):
            return out_ref.at[pl.ds((4 * px + 2 * py + pc) * m_per, m_per), :]

        def copy(k, block, to, src=None):
            return pltpu.make_async_remote_copy(
                src_ref=rows(*block) if src is None else src, dst_ref=rows(*block),
                send_sem=send_sems.at[k], recv_sem=recv_sems.at[k], device_id=to, device_id_type=MESH)

        mine = pltpu.make_async_copy(x_ref, rows(*me), local_sem)
        mine.start()
        first = [copy(0, me, sibling, src=x_ref)]
        first += [copy(1 + j, me, (*chip, c), src=x_ref) for j, chip in enumerate(chips)]
        for cp in first:
            cp.start()
        passed = [copy(4 + j, (*chip, c), sibling) for j, chip in enumerate(chips)]
        for j, chip in enumerate(chips):
            copy(1 + j, (*chip, c), me).wait_recv()
            passed[j].start()
        copy(0, sibling, me).wait_recv()
        for j, chip in enumerate(chips):
            copy(4 + j, (*chip, 1 - c), me).wait_recv()
        for cp in first + passed:
            cp.wait_send()
        mine.wait()

    return _pcall(body, name=name, out_shape=jax.ShapeDtypeStruct((N_DEV * m_per, n), blk.dtype),
                  in_specs=[pl.BlockSpec(memory_space=pltpu.VMEM)] + [ANY_SPEC] * len(deps),
                  out_specs=pl.BlockSpec(memory_space=pltpu.VMEM),
                  scratch_shapes=[pltpu.SemaphoreType.DMA((7,)), pltpu.SemaphoreType.DMA((7,)), pltpu.SemaphoreType.DMA],
                  compiler_params=pltpu.CompilerParams(vmem_limit_bytes=VMEM_LIMIT))(blk, *deps)


def _other_chips(x, y):
    return [(1 - x, y), (x, 1 - y), (1 - x, 1 - y)]


def _place_shards(shards, s_me):
    return [lax.dynamic_update_slice(lax.empty((N_CHIPS,) + s.shape, s.dtype), s[None], (s_me, 0, 0)) for s in shards]


def _ag_copy(src, land, send, recv, wi, j, chip, x, y, c, tc, both):
    hr = src.shape[0] // 2
    half = pl.ds(pl.multiple_of(c * hr, 16), hr)
    k = 3 * wi + j
    return pltpu.make_async_remote_copy(
        src_ref=src.at[half, :], dst_ref=land.at[2 * x + y, half, :],
        send_sem=send.at[2 * k + tc if both else k], recv_sem=recv.at[2 * k + c if both else k],
        device_id=(chip[0], chip[1], tc), device_id_type=MESH)


def _ag_targets(c, both):
    return (0, 1) if both else (c,)


def _ag_start(name, shards, lands, groups, direct, deps=()):
    nw, ng, nd = len(shards), len(groups), len(deps)

    def body(*refs):
        src, land = refs[:nw], refs[nw:2 * nw]
        sems = refs[2 * nw + nd:2 * nw + nd + 2 * ng]
        token = refs[-1]
        x, y, c = _place()
        for gi, grp in enumerate(groups):
            for wi, w in enumerate(grp):
                for j, chip in enumerate(_other_chips(x, y)):
                    for tc in _ag_targets(c, direct[gi]):
                        _ag_copy(src[w], land[w], sems[2 * gi], sems[2 * gi + 1], wi, j, chip, x, y, c, tc, direct[gi]).start()
        token[...] = jnp.zeros_like(token)

    sem_shapes = []
    for gi, grp in enumerate(groups):
        sem_shapes += [pltpu.SemaphoreType.DMA(((6 if direct[gi] else 3) * len(grp),))] * 2
    out_shape = sem_shapes + [pltpu.HBM(s.shape, s.dtype) for s in shards] + [pltpu.HBM(l.shape, l.dtype) for l in lands]
    out_shape += [jax.ShapeDtypeStruct((8, 128), F32)]
    res = _pcall(body, name=name, out_shape=out_shape, in_specs=[HBM_SPEC] * (2 * nw) + [ANY_SPEC] * nd,
                 out_specs=[SEM_SPEC] * (2 * ng) + [HBM_SPEC] * (2 * nw) + [pl.BlockSpec(memory_space=pltpu.VMEM)],
                 input_output_aliases={i: 2 * ng + i for i in range(2 * nw)},
                 compiler_params=pltpu.CompilerParams(has_side_effects=EFFECT))(
                     *[_hbm(s) for s in shards], *[_hbm(l) for l in lands], *deps)
    sems = [(res[2 * gi], res[2 * gi + 1]) for gi in range(ng)]
    return sems, list(res[2 * ng:2 * ng + nw]), list(res[2 * ng + nw:2 * ng + 2 * nw]), res[-1]


def _ag_wait(name, shards, lands, send, recv, both, after):
    n = len(shards)

    def body(*refs):
        src, land = refs[:n], refs[n:2 * n]
        send_sem, recv_sem = refs[2 * n], refs[2 * n + 1]
        x, y, c = _place()
        for wi in range(n):
            for j, chip in enumerate(_other_chips(x, y)):
                for tc in _ag_targets(c, both):
                    _ag_copy(src[wi], land[wi], send_sem, recv_sem, wi, j, chip, x, y, c, tc, both).wait_send()
                    _ag_copy(src[wi], land[wi], send_sem, recv_sem, wi, j, chip, chip[0], chip[1], tc, c, both).wait_recv()

    res = _pcall(body, name=name, out_shape=[pltpu.HBM(a.shape, a.dtype) for a in list(shards) + list(lands)],
                 in_specs=[HBM_SPEC] * (2 * n) + [SEM_SPEC, SEM_SPEC] + [ANY_SPEC] * len(after), out_specs=[HBM_SPEC] * (2 * n),
                 input_output_aliases={i: i for i in range(2 * n)},
                 compiler_params=pltpu.CompilerParams(has_side_effects=EFFECT))(*shards, *lands, send, recv, *after)
    return list(res[n:])


def _ag_forward(name, lands):
    n = len(lands)

    def body(*refs):
        out = refs[n:2 * n]
        send, recv = refs[2 * n], refs[2 * n + 1]
        x, y, c = _place()
        sib = (x, y, 1 - c)
        cps = []
        for wi in range(n):
            hr = out[wi].shape[1] // 2
            for j, (cx, cy) in enumerate(_other_chips(x, y)):
                got = out[wi].at[2 * cx + cy, pl.ds(pl.multiple_of(c * hr, 16), hr), :]
                cp = pltpu.make_async_remote_copy(src_ref=got, dst_ref=got, send_sem=send.at[3 * wi + j], recv_sem=recv.at[3 * wi + j],
                                                  device_id=sib, device_id_type=MESH)
                cp.start()
                cps.append(cp)
        for wi in range(n):
            hr = out[wi].shape[1] // 2
            for j, (cx, cy) in enumerate(_other_chips(x, y)):
                got = out[wi].at[2 * cx + cy, pl.ds(pl.multiple_of((1 - c) * hr, 16), hr), :]
                pltpu.make_async_remote_copy(src_ref=got, dst_ref=got, send_sem=send.at[3 * wi + j], recv_sem=recv.at[3 * wi + j],
                                             device_id=sib, device_id_type=MESH).wait_recv()
        for cp in cps:
            cp.wait_send()

    res = _pcall(body, name=name, out_shape=[jax.ShapeDtypeStruct(l.shape, l.dtype) for l in lands],
                 in_specs=[ANY_SPEC] * n, out_specs=[ANY_SPEC] * n, input_output_aliases={i: i for i in range(n)},
                 scratch_shapes=[pltpu.SemaphoreType.DMA((3 * n,)), pltpu.SemaphoreType.DMA((3 * n,))])(*lands)
    return list(res)


def _peers(x, y, c):
    offs = [(dx, dy, dc) for dx in (0, 1) for dy in (0, 1) for dc in (0, 1)][1:]
    return [(1 - x if dx else x, 1 - y if dy else y, 1 - c if dc else c) for dx, dy, dc in offs]


def _rs_copy(g_ref, land_ref, send, recv, wi, k, to, sender):
    hr = g_ref.shape[1] // 2
    return pltpu.make_async_remote_copy(
        src_ref=g_ref.at[2 * to[0] + to[1], pl.ds(pl.multiple_of(to[2] * hr, 16), hr), :], dst_ref=land_ref.at[sender],
        send_sem=send.at[7 * wi + k], recv_sem=recv.at[7 * wi + k], device_id=to, device_id_type=MESH)


def _rs_start(name, gs):
    n = len(gs)
    lands = [lax.empty((N_DEV, g.shape[1] // 2, g.shape[2]), BF16) for g in gs]

    def body(*refs):
        g, land = refs[:n], refs[n:2 * n]
        send, recv = refs[2 * n], refs[2 * n + 1]
        token = refs[-1]
        x, y, c = _place()
        me = 4 * x + 2 * y + c
        for wi in range(n):
            for k, to in enumerate(_peers(x, y, c)):
                _rs_copy(g[wi], land[wi], send, recv, wi, k, to, me).start()
        token[...] = jnp.zeros_like(token)

    out_shape = [pltpu.SemaphoreType.DMA((7 * n,))] * 2 + [pltpu.HBM(a.shape, a.dtype) for a in list(gs) + lands]
    out_shape += [jax.ShapeDtypeStruct((8, 128), F32)]
    res = _pcall(body, name=name, out_shape=out_shape, in_specs=[HBM_SPEC] * (2 * n),
                 out_specs=[SEM_SPEC] * 2 + [HBM_SPEC] * (2 * n) + [pl.BlockSpec(memory_space=pltpu.VMEM)],
                 input_output_aliases={i: 2 + i for i in range(2 * n)},
                 compiler_params=pltpu.CompilerParams(has_side_effects=EFFECT))(
                     *[_hbm(a) for a in gs], *[_hbm(a) for a in lands])
    return res[0], res[1], list(res[2:2 + n]), list(res[2 + n:2 + 2 * n]), res[-1]


def _rs_wait(name, gs, lands, send, recv, after):
    n = len(gs)

    def body(*refs):
        g, land = refs[:n], refs[n:2 * n]
        send_sem, recv_sem = refs[2 * n], refs[2 * n + 1]
        x, y, c = _place()
        me = 4 * x + 2 * y + c
        for wi in range(n):
            for k, to in enumerate(_peers(x, y, c)):
                _rs_copy(g[wi], land[wi], send_sem, recv_sem, wi, k, to, me).wait_send()
                _rs_copy(g[wi], land[wi], send_sem, recv_sem, wi, k, (x, y, c), 4 * to[0] + 2 * to[1] + to[2]).wait_recv()

    res = _pcall(body, name=name, out_shape=[pltpu.HBM(a.shape, a.dtype) for a in list(gs) + list(lands)],
                 in_specs=[HBM_SPEC] * (2 * n) + [SEM_SPEC, SEM_SPEC, ANY_SPEC], out_specs=[HBM_SPEC] * (2 * n),
                 input_output_aliases={i: i for i in range(2 * n)},
                 compiler_params=pltpu.CompilerParams(has_side_effects=EFFECT))(*gs, *lands, send, recv, after)
    return list(res[:n]), list(res[n:])


def _bc_copy(blk_ref, land_ref, send, recv, k, to, slot):
    return pltpu.make_async_remote_copy(src_ref=blk_ref, dst_ref=land_ref.at[slot], send_sem=send.at[k], recv_sem=recv.at[k],
                                        device_id=to, device_id_type=MESH)


def _bcast_start(name, blk):
    land = lax.empty((N_DEV,) + blk.shape, blk.dtype)

    def body(blk_ref, land_ref, send, recv, blk_thru, land_thru, token):
        x, y, c = _place()
        for k, to in enumerate(_peers(x, y, c)):
            _bc_copy(blk_ref, land_ref, send, recv, k, to, 4 * x + 2 * y + c).start()
        token[...] = jnp.zeros_like(token)

    return _pcall(body, name=name,
                  out_shape=[pltpu.SemaphoreType.DMA((7,)), pltpu.SemaphoreType.DMA((7,)), pltpu.HBM(blk.shape, blk.dtype),
                             pltpu.HBM(land.shape, land.dtype), jax.ShapeDtypeStruct((8, 128), F32)],
                  in_specs=[HBM_SPEC, HBM_SPEC], out_specs=[SEM_SPEC, SEM_SPEC, HBM_SPEC, HBM_SPEC, pl.BlockSpec(memory_space=pltpu.VMEM)],
                  input_output_aliases={0: 2, 1: 3}, compiler_params=pltpu.CompilerParams(has_side_effects=EFFECT))(_hbm(blk), _hbm(land))


def _bcast_wait(name, blk, land, send, recv, after):
    def body(blk_ref, land_ref, send_sem, recv_sem, after_ref, blk_thru, land_thru):
        x, y, c = _place()
        for k, to in enumerate(_peers(x, y, c)):
            _bc_copy(blk_ref, land_ref, send_sem, recv_sem, k, to, 4 * x + 2 * y + c).wait_send()
            _bc_copy(blk_ref, land_ref, send_sem, recv_sem, k, to, 4 * to[0] + 2 * to[1] + to[2]).wait_recv()

    return _pcall(body, name=name, out_shape=[pltpu.HBM(blk.shape, blk.dtype), pltpu.HBM(land.shape, land.dtype)],
                  in_specs=[HBM_SPEC, HBM_SPEC, SEM_SPEC, SEM_SPEC, ANY_SPEC], out_specs=[HBM_SPEC, HBM_SPEC],
                  input_output_aliases={0: 0, 1: 1}, compiler_params=pltpu.CompilerParams(has_side_effects=EFFECT))(
                      blk, land, send, recv, after)[1]


def _rs_sum(name, gs, lands):
    n = len(gs)

    def body(*refs):
        g_refs, land_refs, out_refs = refs[:n], refs[n:2 * n], refs[2 * n:3 * n]
        recvs = refs[3 * n:4 * n]
        local_sems, sib_send, sib_recv = refs[4 * n:]
        x, y, c = _place()
        me = 4 * x + 2 * y + c
        cps = []
        for wi in range(n):
            hr = g_refs[wi].shape[1] // 2
            own = g_refs[wi].at[2 * x + y, pl.ds(pl.multiple_of(c * hr, 16), hr), :]
            cps.append(pltpu.make_async_copy(own, recvs[wi].at[me], local_sems.at[8 * wi + 7]))
            for k, (tx, ty, tc) in enumerate(_peers(x, y, c)):
                slot = 4 * tx + 2 * ty + tc
                cps.append(pltpu.make_async_copy(land_refs[wi].at[slot], recvs[wi].at[slot], local_sems.at[8 * wi + k]))
        for k, cp in enumerate(cps):
            cp.start(priority=k % 2)
        sibs = []
        for wi in range(n):
            hr = g_refs[wi].shape[1] // 2
            ch = _pick(hr, 64, 16)
            for cp in cps[8 * wi:8 * wi + 8]:
                cp.wait()
            base = pl.multiple_of(c * hr, 16)
            for r0 in range(0, hr, ch):
                acc = recvs[wi][0, r0:r0 + ch, :].astype(F32)
                for k in range(1, N_DEV):
                    acc = acc + recvs[wi][k, r0:r0 + ch, :].astype(F32)
                out_refs[wi][pl.ds(base + r0, ch), :] = acc
            half = out_refs[wi].at[pl.ds(base, hr), :]
            sib = pltpu.make_async_remote_copy(src_ref=half, dst_ref=half, send_sem=sib_send.at[wi], recv_sem=sib_recv.at[wi],
                                               device_id=(x, y, 1 - c), device_id_type=MESH)
            sib.start()
            sibs.append(sib)
        for wi in range(n):
            hr = g_refs[wi].shape[1] // 2
            other = out_refs[wi].at[pl.ds(pl.multiple_of((1 - c) * hr, 16), hr), :]
            pltpu.make_async_remote_copy(src_ref=other, dst_ref=other, send_sem=sib_send.at[wi], recv_sem=sib_recv.at[wi],
                                         device_id=(x, y, 1 - c), device_id_type=MESH).wait_recv()
            sibs[wi].wait_send()

    res = _pcall(
        body, name=name, out_shape=[jax.ShapeDtypeStruct(g.shape[1:], F32) for g in gs],
        in_specs=[ANY_SPEC] * (2 * n), out_specs=[pl.BlockSpec(memory_space=pltpu.VMEM)] * n,
        scratch_shapes=[pltpu.VMEM((N_DEV, g.shape[1] // 2, g.shape[2]), BF16) for g in gs]
        + [pltpu.SemaphoreType.DMA((8 * n,)), pltpu.SemaphoreType.DMA((n,)), pltpu.SemaphoreType.DMA((n,))],
        compiler_params=pltpu.CompilerParams(vmem_limit_bytes=VMEM_LIMIT))(*gs, *lands)
    return list(res)


def _gate_norm_epilogue(factor, x_in, gt, nxt, t, d):
    blk = (ROWS, d)
    full = lambda i, j: (i, j)
    rowv = lambda i, j: (0, j)

    def epi(acc, res, scale, *norm):
        x_new = res + (factor * scale) * acc
        if not norm:
            return x_new, acc
        gv, scv, shv = norm
        r = lax.rsqrt(jnp.mean(x_new * x_new, axis=-1, keepdims=True) + EPS)
        return x_new, acc, x_new * r * gv * (1.0 + scv) + shv

    ins = [(x_in, blk, full), (gt, (1, d), rowv)] + [(v, (1, d), rowv) for v in (nxt or ())]
    outs = [((t, d), F32, blk, full), ((t, d), BF16, blk, full)] + ([((t, d), BF16, blk, full)] if nxt else [])
    return epi, ins, outs


def _mixer_tail(name, ge, w_glu, b_glu, w_up, y_pool, uin, w_out, x_in, gt, nxt, d, pw, tm=ROWS):
    t, sw = ge.shape
    tm = min(tm, t)
    ns = w_glu.shape[2]
    cb = (2 * pw) // d

    def body(ge_ref, wg_ref, b_ref, wu_ref, yp_ref, glp_ref, gls_ref, wo_ref, x_ref, gt_ref, gn_ref, sc_ref, sh_ref,
             gv_ref, sg_ref, ys_ref, mg_ref, y_ref, h_ref, xn_ref):
        gev = ge_ref[...]
        gv = jnp.concatenate([jnp.dot(gev, wg_ref[s], preferred_element_type=F32) for s in range(N_CHIPS)], axis=1)
        gv_ref[...] = gv.astype(BF16)
        g = gv + b_ref[...]
        sg = (g[:, :sw] * _sigmoid(g[:, sw:])).astype(BF16)
        sg_ref[...] = sg
        ys = jnp.concatenate([jnp.dot(sg, wu_ref[s], preferred_element_type=F32) for s in range(N_CHIPS)], axis=1)
        ys_ref[...] = ys.astype(BF16)
        mg = (_sigmoid(glp_ref[...].astype(F32)) * yp_ref[...].astype(F32) + _sigmoid(gls_ref[...].astype(F32)) * ys).astype(BF16)
        mg_ref[...] = mg
        y = jnp.dot(mg, wo_ref[...], preferred_element_type=F32)
        y_ref[...] = y.astype(BF16)
        xn = x_ref[...] + gt_ref[...] * y
        xn_ref[...] = xn
        r = lax.rsqrt(jnp.mean(xn * xn, axis=-1, keepdims=True) + EPS)
        h_ref[...] = (xn * r * gn_ref[...] * (1.0 + sc_ref[...]) + sh_ref[...]).astype(BF16)

    full3 = lambda i: (0, 0, 0)
    full2 = lambda i: (0, 0)
    rows = lambda w: pl.BlockSpec((tm, w), lambda i: (i, 0))
    rowv = pl.BlockSpec((1, d), full2)
    return _pcall(
        body, name=name, grid=(t // tm,),
        in_specs=[rows(sw), pl.BlockSpec(w_glu.shape, full3), pl.BlockSpec(b_glu.shape, full2),
                  pl.BlockSpec(w_up.shape, full3), rows(d), pl.BlockSpec((tm, d), lambda i: (i, cb)),
                  pl.BlockSpec((tm, d), lambda i: (i, cb + 1)), pl.BlockSpec(w_out.shape, full2), rows(d), rowv, rowv, rowv, rowv],
        out_specs=[rows(N_CHIPS * ns), rows(sw), rows(d), rows(d), rows(d), rows(d), rows(d)],
        out_shape=[jax.ShapeDtypeStruct((t, N_CHIPS * ns), BF16), jax.ShapeDtypeStruct((t, sw), BF16)]
        + [jax.ShapeDtypeStruct((t, d), BF16)] * 4 + [jax.ShapeDtypeStruct((t, d), F32)],
        compiler_params=_params("parallel"))(ge, w_glu, b_glu, w_up, y_pool, uin, uin, w_out, x_in, gt, *nxt)


def _glu_up_bwd(name, dy, w_up, gv, b_glu, w_glu, sg, ge, tm=ROWS, deps=()):
    t, sw = ge.shape
    tm = min(tm, t)
    nb = t // tm
    ns = w_up.shape[2]
    nt = (((1,), (1,)), ((), ()))
    tn_ = (((0,), (0,)), ((), ()))

    def body(dy_ref, wu_ref, gv_ref, b_ref, wg_ref, sg_ref, ge_ref, *rest):
        dge_ref, db_ref, dwu_ref, dwg_ref, dwu_acc, dwg_acc = rest[len(deps):]
        i = pl.program_id(0)

        @pl.when(i == 0)
        def _():
            db_ref[...] = jnp.zeros_like(db_ref)
            dwu_acc[...] = jnp.zeros_like(dwu_acc)
            dwg_acc[...] = jnp.zeros_like(dwg_acc)

        dyv = dy_ref[...]
        dsg = None
        for s in range(N_CHIPS):
            q = lax.dot_general(dyv[:, s * ns:(s + 1) * ns], wu_ref[s], nt, preferred_element_type=F32)
            dsg = q if dsg is None else dsg + q
        g = gv_ref[...].astype(F32) + b_ref[...]
        val, sgm = g[:, :sw], _sigmoid(g[:, sw:])
        dval, dgate = dsg * sgm, dsg * val * sgm * (1.0 - sgm)
        db_ref[...] += jnp.concatenate([_colsum(dval), _colsum(dgate)], axis=1)
        dgv = jnp.concatenate([dval, dgate], axis=1).astype(BF16)
        dge = None
        for s in range(N_CHIPS):
            q = lax.dot_general(dgv[:, s * ns:(s + 1) * ns], wg_ref[s], nt, preferred_element_type=F32)
            dge = q if dge is None else dge + q
        dge_ref[...] = dge.astype(BF16)
        dwu_acc[...] += lax.dot_general(sg_ref[...], dyv, tn_, preferred_element_type=F32)
        dwg_acc[...] += lax.dot_general(ge_ref[...], dgv, tn_, preferred_element_type=F32)

        @pl.when(i == nb - 1)
        def _():
            for s in range(N_CHIPS):
                dwu_ref[s] = dwu_acc[:, s * ns:(s + 1) * ns].astype(BF16)
                dwg_ref[s] = dwg_acc[:, s * ns:(s + 1) * ns].astype(BF16)

    full3 = lambda i: (0, 0, 0)
    rows = lambda w: pl.BlockSpec((tm, w), lambda i: (i, 0))
    wide = N_CHIPS * ns
    return _pcall(
        body, name=name, grid=(nb,),
        in_specs=[rows(wide), pl.BlockSpec(w_up.shape, full3), rows(wide), pl.BlockSpec(b_glu.shape, lambda i: (0, 0)),
                  pl.BlockSpec(w_glu.shape, full3), rows(sw), rows(sw)] + [ANY_SPEC] * len(deps),
        out_specs=[rows(sw), pl.BlockSpec((1, wide), lambda i: (0, 0)), pl.BlockSpec(w_up.shape, full3),
                   pl.BlockSpec(w_glu.shape, full3)],
        out_shape=[jax.ShapeDtypeStruct((t, sw), BF16), jax.ShapeDtypeStruct((1, wide), F32),
                   jax.ShapeDtypeStruct(w_up.shape, BF16), jax.ShapeDtypeStruct(w_glu.shape, BF16)],
        scratch_shapes=[pltpu.VMEM((sw, wide), F32), pltpu.VMEM((sw, wide), F32)],
        compiler_params=_params("arbitrary"))(dy, w_up, gv, b_glu, w_glu, sg, ge, *deps)


_FULL = lambda i, j: (i, j)
_ROWV = lambda i, j: (0, j)


def _gates_bwd_epilogue(y_pool, y_ssm, uin, t, d, pw):
    cb = (2 * pw) // d

    def epi(dm, yp, ys, glp, gls):
        sp, ss = _sigmoid(glp.astype(F32)), _sigmoid(gls.astype(F32))
        dgl = jnp.concatenate([dm * yp.astype(F32) * sp * (1.0 - sp), dm * ys.astype(F32) * ss * (1.0 - ss)], axis=1)
        return dm * sp, dm * ss, ("at", 2 * pw, dgl)

    ins = [(y_pool, (ROWS, d), _FULL), (y_ssm, (ROWS, d), _FULL),
           (uin, (ROWS, d), lambda i, j: (i, cb)), (uin, (ROWS, d), lambda i, j: (i, cb + 1))]
    wide = 2 * pw + 2 * d
    return epi, ins, [((t, d), BF16, (ROWS, d), _FULL)] * 2 + [((t, wide), BF16, (ROWS, wide), _FULL)]


def _loss_epilogue(x_in, gt, tgt, g_final, t, d):
    blk = (ROWS, d)
    full = lambda i, j: (i, j)
    rowv = lambda i, j: (0, j)

    def epi(acc, res, scale, tv, gv):
        xv = res + (0.5 * scale) * acc
        r = lax.rsqrt(jnp.mean(xv * xv, axis=-1, keepdims=True) + EPS)
        xr = xv * r
        e = xr * gv - tv
        loss_row = 0.5 * jnp.mean(e * e, axis=-1, keepdims=True)
        dout = e * (1.0 / d)
        gd = gv * dout
        dx = r * (gd - xr * jnp.mean(gd * xr, axis=-1, keepdims=True))
        fdx = 0.5 * dx
        return [dx, scale * fdx], [_colsum(dout * xr), _colsum(loss_row * jnp.ones((1, 128), F32)), _colsum(fdx * acc)]

    ins = [(x_in, blk, full), (gt, (1, d), rowv), (tgt, blk, full), (g_final, (1, d), rowv)]
    return epi, ins, [((t, d), F32, blk, full), ((t, d), BF16, blk, full)], [d, 128, d]


def _norm_bwd_epilogue(x, dres, g, sc, up, t, d):
    blk = (ROWS, d)
    full = lambda i, j: (i, j)
    rowv = lambda i, j: (0, j)

    def epi(dhv, xv, drv, gv, scv, *rest):
        r = lax.rsqrt(jnp.mean(xv * xv, axis=-1, keepdims=True) + EPS)
        xr = xv * r
        dn = dhv * (1.0 + scv)
        gd = gv * dn
        dx = drv + r * (gd - xr * jnp.mean(gd * xr, axis=-1, keepdims=True))
        outs, reds = [dx], [_colsum(dhv), _colsum(dhv * xr * gv), _colsum(dn * xr)]
        if up:
            yv, gtv = rest
            fdx = up[2] * dx
            outs.append(gtv * fdx)
            reds.append(_colsum(fdx * yv.astype(F32)))
        return outs, reds

    ins = [(x, blk, full), (dres, blk, full), (g, (1, d), rowv), (sc, (1, d), rowv)]
    ins += [(up[0], blk, full), (up[1], (1, d), rowv)] if up else []
    outs = [((t, d), F32, blk, full)] + ([((t, d), BF16, blk, full)] if up else [])
    return epi, ins, outs, [d] * (4 if up else 3)


def _ffn_in_act(name, h, w_in, tm=ROWS):
    t, d = h.shape
    ns = w_in.shape[2]
    tm = _pick(t, tm, 8)
    w4 = w_in.reshape(2, 2, d, ns)

    def body(h_ref, w_ref, ab_ref, act_ref):
        hv = h_ref[...]
        a = jnp.dot(hv, w_ref[0], preferred_element_type=F32)
        b = jnp.dot(hv, w_ref[1], preferred_element_type=F32)
        ab_ref[0] = a.astype(BF16)
        ab_ref[1] = b.astype(BF16)
        act_ref[...] = (a * _sigmoid(a) * b).astype(BF16)

    return _pcall(body, name=name, grid=(2, t // tm),
                  in_specs=[pl.BlockSpec((tm, d), lambda c, i: (i, 0)), pl.BlockSpec((2, None, d, ns), lambda c, i: (0, c, 0, 0))],
                  out_specs=[pl.BlockSpec((2, tm, ns), lambda c, i: (0, i, c)), pl.BlockSpec((tm, ns), lambda c, i: (i, c))],
                  out_shape=[jax.ShapeDtypeStruct((2, t, 2 * ns), BF16), jax.ShapeDtypeStruct((t, 2 * ns), BF16)],
                  compiler_params=_params("parallel", "parallel"))(h, w4)


def _dswiglu_epilogue(ab, t, f, tn):
    blk = (2, ROWS, tn)
    idx = lambda i, j: (0, i, j)

    def epi(dact, abv):
        a, b = abv[0].astype(F32), abv[1].astype(F32)
        s = _sigmoid(a)
        return (jnp.stack([dact * b * (s * (1.0 + a * (1.0 - s))), dact * (a * s)]),)

    return epi, [(ab, blk, idx)], [((2, t, f), BF16, blk, idx)]


def _ffn_fwd(tag, x, h, gt, get_w_in, get_w_out, nxt=None, loss=None):
    t, d = x.shape
    ab, act = _ffn_in_act(tag + "_in", h, get_w_in(h))
    if loss:
        epi, epi_ins, epi_outs, epi_reds = _loss_epilogue(x, gt, *loss, t, d)
        res = _mm(tag + "_out", act, get_w_out(act), tm=ROWS, tn=d, epi=epi, epi_ins=epi_ins, epi_outs=epi_outs,
                  epi_reds=epi_reds, b_resident=True)
        return res, (x, h, ab, act, None)
    epi, epi_ins, epi_outs = _gate_norm_epilogue(0.5, x, gt, nxt, t, d)
    res = _mm(tag + "_out", act, get_w_out(act), tm=ROWS, tn=d, epi=epi, epi_ins=epi_ins, epi_outs=epi_outs, b_resident=True)
    return res[0], res[2], (x, h, ab, act, res[1])


def _ffn_bwd(tag, dx_new, dy, saved, g, sc, w_in, w_out, start_rs, up, deps=()):
    x, h, ab, act, _ = saved
    d = x.shape[1]
    f = act.shape[1]
    dw_out = _mm(tag + "_dwout", act, dy, ta=True, tm=1408, tn=d, tk=2048, deps=deps)
    tok = start_rs("out", dw_out.reshape(N_CHIPS, f // N_CHIPS, d))
    epi, epi_ins, epi_outs = _dswiglu_epilogue(ab, x.shape[0], f, w_in.shape[2])
    dab = _mm(tag + "_dact", dy, w_out, tb=True, tm=ROWS, tn=w_in.shape[2], epi=epi, epi_ins=epi_ins, epi_outs=epi_outs,
              deps=(tok,), j_outer=True)
    dw_in = _mm(tag + "_dwin", h, dab, ta=True, out_stacked=True, b_halves=True, tm=ROWS, tn=1408, tk=4096, j_outer=True)
    tok = start_rs("in", dw_in)
    epi, epi_ins, epi_outs, epi_reds = _norm_bwd_epilogue(x, dx_new, g, sc, up, x.shape[0], d)
    res = _mm(tag + "_dh", dab, w_in, tb=True, b_stacked=True, a_halves=True, tn=d, tk=5632, deps=(tok,), epi=epi,
              epi_ins=epi_ins, epi_outs=epi_outs, epi_reds=epi_reds, b_resident=True)
    if up:
        return res
    return res[0], None, res[1], res[2], res[3], None


def _row(v):
    return v.reshape(1, -1)


def _pack(parts):
    cols = []
    for p in parts:
        flat = p.reshape(-1).astype(F32)
        padn = (-flat.shape[0]) % 128
        cols.append(jnp.pad(flat, (0, padn)) if padn else flat)
    flat = jnp.concatenate(cols)
    padn = (-flat.shape[0]) % 1024
    if padn:
        flat = jnp.pad(flat, (0, padn))
    return flat.reshape(-1, 128)


def _unpack(packed, shapes):
    flat = packed.reshape(-1)
    out, off = [], 0
    for s in shapes:
        n = math.prod(s)
        out.append(flat[off:off + n].reshape(s))
        off += n + ((-n) % 128)
    return out


SMALL = ['b_ada', 'g_ffn1', 'g_mix', 'pool_w', 'pool_b', 'pool_scale', 'ssm_lam_re_log', 'ssm_lam_im', 'ssm_log_dt',
         'ssm_b_re', 'ssm_b_im', 'ssm_c_re', 'ssm_c_im', 'ssm_d', 'b_glu', 'g_ffn2', 'g_final']
BIG = ['w_ffn1_in', 'w_ffn1_out', 'w_in', 'w_pool_up', 'w_glu', 'w_ssm_up', 'w_out', 'w_ffn2_in', 'w_ffn2_out']
AG_GROUPS = [[0], [1], [2, 3, 4, 5, 6], [7, 8]]
AG_DIRECT = [False, True, False, True]
SMALL_LATE = ['b_ada', 'g_ffn1']
RS_SUM_GROUPS = [[0, 1, 2, 3], [4, 5]]
WEIGHTS = ['w_ada', 'b_ada', 'g_ffn1', 'w_ffn1_in', 'w_ffn1_out', 'g_mix', 'w_in', 'pool_w', 'pool_b', 'pool_scale', 'w_pool_up',
           'ssm_lam_re_log', 'ssm_lam_im', 'ssm_log_dt', 'ssm_b_re', 'ssm_b_im', 'ssm_c_re', 'ssm_c_im', 'ssm_d', 'w_glu', 'b_glu',
           'w_ssm_up', 'w_out', 'g_ffn2', 'w_ffn2_in', 'w_ffn2_out', 'g_final']


def kernel(x, c, w_ada, b_ada, g_ffn1, w_ffn1_in, w_ffn1_out, g_mix, w_in, pool_w, pool_b, pool_scale, w_pool_up, ssm_lam_re_log, ssm_lam_im, ssm_log_dt, ssm_b_re, ssm_b_im, ssm_c_re, ssm_c_im, ssm_d, w_glu, b_glu, w_ssm_up, w_out, g_ffn2, w_ffn2_in, w_ffn2_out, g_final, loss_target, m_w_ada, m_b_ada, m_g_ffn1, m_w_ffn1_in, m_w_ffn1_out, m_g_mix, m_w_in, m_pool_w, m_pool_b, m_pool_scale, m_w_pool_up, m_ssm_lam_re_log, m_ssm_lam_im, m_ssm_log_dt, m_ssm_b_re, m_ssm_b_im, m_ssm_c_re, m_ssm_c_im, m_ssm_d, m_w_glu, m_b_glu, m_w_ssm_up, m_w_out, m_g_ffn2, m_w_ffn2_in, m_w_ffn2_out, m_g_final, v_w_ada, v_b_ada, v_g_ffn1, v_w_ffn1_in, v_w_ffn1_out, v_g_mix, v_w_in, v_pool_w, v_pool_b, v_pool_scale, v_w_pool_up, v_ssm_lam_re_log, v_ssm_lam_im, v_ssm_log_dt, v_ssm_b_re, v_ssm_b_im, v_ssm_c_re, v_ssm_c_im, v_ssm_d, v_w_glu, v_b_glu, v_w_ssm_up, v_w_out, v_g_ffn2, v_w_ffn2_in, v_w_ffn2_out, v_g_final):
    args = dict(locals())
    wt = {n: args[n] for n in WEIGHTS}
    mom = {n: args["m_" + n] for n in WEIGHTS}
    var = {n: args["v_" + n] for n in WEIGHTS}

    t, d = x.shape[1], x.shape[2]
    pw = pool_b.shape[1]
    sw = ssm_d.shape[1]
    ngrp = sw // SSM_GROUP
    gn = ngrp * SSM_STATE
    xi, yi, ci = _place()
    b_me = 4 * xi + 2 * yi + ci
    s_me = 2 * xi + yi
    x2d = x[0]
    tgt = loss_target[0]

    c_all = _allgather_small("ag_c", c.reshape(8, d // 8)).reshape(N_DEV, d)
    ncol = w_ada.shape[2]
    b_sh = lax.dynamic_slice(b_ada, (0, s_me * ncol), (1, ncol))
    mod_sh = _ada_fwd("ada_fwd", c_all, w_ada[0], b_sh)
    md_send, md_recv, mod_sh, md_land, tok = _bcast_start("mod_start", mod_sh)

    shards = [wt[n][0].astype(BF16) for n in BIG]
    lands = _place_shards(shards, s_me)
    n0 = len(AG_GROUPS[0])
    sems_a, shards_a, lands_a, tok = _ag_start("ag_start_first", shards[:n0], lands[:n0], AG_GROUPS[:1], AG_DIRECT[:1], deps=(tok,))
    rest = [[w - n0 for w in grp] for grp in AG_GROUPS[1:]]
    sems_b, shards_b, lands_b, tok = _ag_start("ag_start_rest", shards[n0:], lands[n0:], rest, AG_DIRECT[1:], deps=(tok,))
    ag_sems, shards_t, lands_t = sems_a + sems_b, shards_a + shards_b, lands_a + lands_b
    md_land = _bcast_wait("mod_wait", mod_sh, md_land, md_send, md_recv, tok)
    mod_all = lax.dynamic_update_slice(md_land, mod_sh[None], (b_me, 0, 0))
    full = {}

    def weights(gi, *after):
        grp = AG_GROUPS[gi]
        if BIG[grp[0]] not in full:
            ls = _ag_wait("ag_wait%d" % gi, [shards_t[w] for w in grp], [lands_t[w] for w in grp], *ag_sems[gi],
                          AG_DIRECT[gi], after)
            for w, l in zip(grp, ls if AG_DIRECT[gi] else _ag_forward("ag_fwd%d" % gi, ls)):
                full[BIG[w]] = l
        return full

    mod_me = jnp.concatenate([lax.dynamic_slice(mod_all, (2 * s, b_me, 0), (1, 1, ncol))[0] for s in range(N_CHIPS)], axis=1)
    mod = mod_me.reshape(9, d)
    sh1, sc1, gt1, sh2, sc2, gt2, sh3, sc3, gt3 = [mod[k:k + 1] for k in range(9)]

    f = w_ffn1_out.shape[1] * N_CHIPS

    col = lambda a: a.reshape(gn, 1)
    lrl_c, li_c = col(ssm_lam_re_log), col(ssm_lam_im)
    ldt_c = col(jnp.broadcast_to(ssm_log_dt.reshape(ngrp, 1), (ngrp, SSM_STATE)))
    b_re2, b_im2 = ssm_b_re.reshape(gn, SSM_GROUP), ssm_b_im.reshape(gn, SSM_GROUP)
    lrdt_c, ang_c, bb_re, bb_im = _ssm_prep("ssm_prep", lrl_c, li_c, ldt_c, b_re2, b_im2)
    lrdt, ang = lrdt_c.reshape(1, gn), ang_c.reshape(1, gn)
    tr = lambda a: jnp.swapaxes(a, 1, 2)
    bbd = [_blockdiag_b(v, sw).astype(BF16) for v in (bb_re, bb_im)]
    ccd = [_blockdiag_c(v[0], sw).astype(BF16) for v in (ssm_c_re, ssm_c_im)]
    bbd_t, ccd_t = [tr(v) for v in bbd], [tr(v) for v in ccd]
    early = [n for n in SMALL if n not in SMALL_LATE]
    packs = {}
    for tag, names, extra in (("early", early, []), ("late", SMALL_LATE, [jnp.zeros((128,), F32)])):
        packs[tag] = [_pack([src[n] for n in names] + extra) for src in (wt, mom, var)]
    shadow = [lrdt, ang, ldt_c, *bbd, *ccd, *bbd_t, *ccd_t, *packs["early"], *packs["late"]]

    h1 = _norm_fwd("ffn1_norm", x2d, g_ffn1, sc1, sh1, deps=(tok,))
    x1, h2, sav1 = _ffn_fwd("ffn1", x2d, h1, gt1, lambda after: weights(0, after, *shadow)['w_ffn1_in'],
                            lambda after: weights(1, after)['w_ffn1_out'].reshape(f, d), (g_mix, sc2, sh2))
    weights(2, h2)
    wo = full['w_out'].reshape(d, d)
    uin = _mm("mix_in", h2, full['w_in'], b_stacked=True, tm=2048, tn=768, j_outer=True)
    p_pool, z_pool, y_pool = _pool_fwd("pool_fwd", uin, pool_w[0], pool_b, pool_scale, full['w_pool_up'], pw, tm=1024)
    y_s, ge, s_re, s_im = _ssm_fwd("ssm_fwd", uin, lrdt, ang, *bbd, *ccd, ssm_d, sw)
    gv, sg, y_ssm, merged, y2, h3, x2 = _mixer_tail("mixer_tail", ge, full['w_glu'], b_glu, full['w_ssm_up'], y_pool, uin, wo,
                                                    x1, gt2, (g_ffn2, sc3, sh3), d, pw)

    (dx3, dy3, dg_final, loss_v, dgt3), sav3 = _ffn_fwd(
        "ffn2", x2, h3, gt3, lambda after: weights(3, after)['w_ffn2_in'],
        lambda after: weights(3, after)['w_ffn2_out'].reshape(f, d), loss=(tgt, _row(g_final)))

    gs = {}
    rs_open = []

    def start_rs(names, gbs):
        send, recv, g_thru, land_thru, token = _rs_start("rs_start_" + names[0], gbs)
        rs_open.append((names, send, recv, g_thru, land_thru))
        return token

    dx2, dy2, dsh3, dsc3, gs['g_ffn2'], dgt2 = _ffn_bwd(
        "ffn2b", dx3, dy3, sav3, g_ffn2, sc3, full['w_ffn2_in'], full['w_ffn2_out'].reshape(f, d),
        lambda key, g: start_rs(['w_ffn2_' + key], [g]), (y2, gt2, 1.0))

    epi, epi_ins, epi_outs = _gates_bwd_epilogue(y_pool, y_ssm, uin, t, d, pw)
    dy_pool, dy_ssm, duin, g_wo = _mm("mix_dmerged", dy2, wo, tb=True, tm=ROWS, tn=d, epi=epi, epi_ins=epi_ins,
                                      epi_outs=epi_outs, b_resident=True, wgrad=merged, wgrad_rows=True)
    g_wo = g_wo.reshape(N_CHIPS, d // N_CHIPS, d)

    duin, gs['pool_w'], gs['pool_b'], gs['pool_scale'], g_wpu = _pool_bwd(
        "pool_bwd", dy_pool, z_pool, p_pool, pool_w[0], pool_b, pool_scale, full['w_pool_up'], duin, tm=1024)

    tok = start_rs(['w_out', 'w_pool_up'], [g_wo, g_wpu])
    dge, gs['b_glu'], g_wsu, g_wglu = _glu_up_bwd("glu_up_bwd", dy_ssm, full['w_ssm_up'], gv, b_glu, full['w_glu'], sg, ge,
                                                  deps=(tok,))
    (duin, g_abre, g_abim, g_bbd_re, g_bbd_im, g_ccd_re, g_ccd_im, gs['ssm_d']) = _ssm_bwd(
        "ssm_bwd", dge, y_s, uin, s_re, s_im, lrdt, ang, *bbd_t, *ccd_t, ssm_d, sw, duin)
    gs['ssm_c_re'], gs['ssm_c_im'] = _diag_of_c(g_ccd_re, sw), _diag_of_c(g_ccd_im, sw)
    d_lrl, d_li, d_ldt, d_bre, d_bim = _ssm_param_bwd(
        "ssm_param_bwd", lrl_c, li_c, ldt_c, b_re2, b_im2, g_abre.reshape(gn, 1), g_abim.reshape(gn, 1),
        _diag_of_b(g_bbd_re, sw), _diag_of_b(g_bbd_im, sw))
    gs['ssm_lam_re_log'], gs['ssm_lam_im'] = d_lrl, d_li
    gs['ssm_log_dt'] = jnp.sum(d_ldt.reshape(ngrp, SSM_STATE), axis=1)
    gs['ssm_b_re'], gs['ssm_b_im'] = d_bre, d_bim

    g_win =_mm("mix_dwin", h2, duin, ta=True, out_stacked=True, tm=d, tn=768, tk=4096)
    tok = start_rs(['w_ssm_up', 'w_glu', 'w_in'], [g_wsu, g_wglu, g_win])
    epi, epi_ins, epi_outs, epi_reds = _norm_bwd_epilogue(x1, dx2, g_mix, sc2, (sav1[4], gt1, 0.5), t, d)
    dx1, dy1, dsh2, dsc2, gs['g_mix'], dgt1 = _mm(
        "mix_dh", duin, full['w_in'], tb=True, b_stacked=True, tn=d, tk=3072, deps=(tok,), epi=epi, epi_ins=epi_ins,
        epi_outs=epi_outs, epi_reds=epi_reds, b_resident=True)

    gs['g_final'] = dg_final
    sg_blk = _pack([gs[n] for n in early])
    sg_send, sg_recv, sg_blk, sg_land, tok = _bcast_start("sg_start", sg_blk)

    dx0, _, dsh1, dsc1, gs['g_ffn1'], _ = _ffn_bwd(
        "ffn1b", dx1, dy1, sav1, g_ffn1, sc1, full['w_ffn1_in'], full['w_ffn1_out'].reshape(f, d),
        lambda key, g: start_rs(['w_ffn1_' + key], [g]), None, deps=(tok,))

    gs['b_ada'] = jnp.concatenate([dsh1, dsc1, dgt1, dsh2, dsc2, dgt2, dsh3, dsc3, dgt3], axis=1)
    lt_blk = _pack([gs[n] for n in SMALL_LATE] + [loss_v])
    lt_send, lt_recv, lt_blk, lt_land, tok = _bcast_start("late_start", lt_blk)

    grads, delta, new_m, new_v = {}, {}, {}, {}

    def small_update(tag, names, blk, land):
        g8 = lax.dynamic_update_slice(land, blk[None], (b_me, 0, 0)).reshape(-1, 128)
        w_pack, m_pack, v_pack = packs[tag]
        shapes = [wt[n].shape for n in names]
        per_param, packed = _adamw_small("adamw_small_" + tag, w_pack, g8, m_pack, v_pack, shapes)
        for k, dst in enumerate((grads, delta, new_m, new_v)):
            rest = _unpack(packed[k], shapes)
            for i, n in enumerate(names):
                dst[n] = per_param[i][k] if per_param[i] is not None else rest[i]
        return g8

    after = tok
    for group in RS_SUM_GROUPS:
        names, g_done, land_done = [], [], []
        for k in group:
            nk, send, recv, g_thru, land_thru = rs_open[k]
            gd, ld = _rs_wait("rs_wait_" + nk[0], g_thru, land_thru, send, recv, after)
            names, g_done, land_done = names + nk, g_done + gd, land_done + ld
        for n, g_sum in zip(names, _rs_sum("rs_sum_" + names[0], g_done, land_done)):
            g_out, dl, mn, vn = _adamw("adamw_" + n, wt[n][0], g_sum, mom[n][0], var[n][0])
            grads[n], delta[n], new_m[n], new_v[n] = g_out[None], dl[None], mn[None], vn[None]
            after = dl
        if group is RS_SUM_GROUPS[-2]:
            small_update("early", early, sg_blk, _bcast_wait("sg_wait", sg_blk, sg_land, sg_send, sg_recv, after))

    lt_land = _bcast_wait("late_wait", lt_blk, lt_land, lt_send, lt_recv, after)
    late8 = small_update("late", SMALL_LATE, lt_blk, lt_land).reshape(N_DEV, -1)
    loss = jnp.sum(late8[:, 10 * d])
    dmod_sh = lax.dynamic_slice(late8, (0, s_me * ncol), (N_DEV, ncol))
    g_w_ada, dl, mn, vn = _ada_bwd_adamw("ada_bwd_adamw", c_all, dmod_sh, w_ada[0], m_w_ada[0], v_w_ada[0])
    grads['w_ada'], delta['w_ada'], new_m['w_ada'], new_v['w_ada'] = g_w_ada[None], dl[None], mn[None], vn[None]

    return (loss, dx0[None], *[grads[n] for n in WEIGHTS], *[delta[n] for n in WEIGHTS],
            *[new_m[n] for n in WEIGHTS], *[new_v[n] for n in WEIGHTS])
```
